```python
import math
import jax, jax.numpy as jnp
from jax import lax
import numpy as np

D_MODEL = 1024
BATCH = 8
SEQ = 4096
DEPTH = 4

N_MIXERS = 3
D_FF = 4 * D_MODEL
D_MIX = D_MODEL
EPS = 1e-6
CONV_WIDTH = 3
S5_GROUP = 16
S5_GROUPS = D_MIX // S5_GROUP
S5_STATE = 64
DT_MIN = 1e-3
DT_MAX = 1e-1
CHUNK = 128
SG_HEADS = 8
SG_HEAD_DIM = D_MIX // SG_HEADS
N_A = (DEPTH + 2) // 3
N_B = (DEPTH + 1) // 3
N_C = DEPTH // 3

kernel_name = "hybrid_conv_s5_sgmlp_trunk"


def rmsnorm(x, g):
    xf = x.astype(jnp.float32)
    y = xf * lax.rsqrt(jnp.mean(xf * xf, axis=-1, keepdims=True) + EPS)
    return (y * g.astype(jnp.float32)).astype(x.dtype)


def short_conv_mixer(h, w_in, conv_w, conv_b, w_out):
    bcx = h @ w_in
    b_gate, c_gate, xh = jnp.split(bcx, 3, axis=-1)
    z = c_gate * xh
    conv = lax.conv_general_dilated(
        z, conv_w[:, None, :].astype(z.dtype), window_strides=(1,),
        padding=[(CONV_WIDTH - 1, 0)], dimension_numbers=("NWC", "WIO", "NWC"),
        feature_group_count=D_MIX) + conv_b
    return (b_gate * conv) @ w_out


def _ssm_combine(e1, e2):
    a1r, a1i, b1r, b1i = e1
    a2r, a2i, b2r, b2i = e2
    ar = a2r * a1r - a2i * a1i
    ai = a2r * a1i + a2i * a1r
    br = a2r * b1r - a2i * b1i + b2r
    bi = a2r * b1i + a2i * b1r + b2i
    return (ar, ai, br, bi)


def s5_mixer(h, w_in, a_re, a_im, log_dt, b_re, b_im, c_re, c_im, d_skip, glu_w, glu_b, w_out):
    bsz, seq_len, _ = h.shape
    f32 = jnp.float32
    u = (h @ w_in).astype(f32).reshape(bsz, seq_len, S5_GROUPS, S5_GROUP)
    a_re = a_re.astype(f32); a_im = a_im.astype(f32)
    dt = jnp.exp(log_dt.astype(f32))[:, None]
    mag = jnp.exp(a_re * dt)
    abar_re = mag * jnp.cos(a_im * dt)
    abar_im = mag * jnp.sin(a_im * dt)
    den = a_re * a_re + a_im * a_im
    nr = abar_re - 1.0
    ni = abar_im
    f_re = ((nr * a_re + ni * a_im) / den)[..., None]
    f_im = ((ni * a_re - nr * a_im) / den)[..., None]
    b_re = b_re.astype(f32); b_im = b_im.astype(f32)
    bbar_re = f_re * b_re - f_im * b_im
    bbar_im = f_re * b_im + f_im * b_re
    bu_re = jnp.einsum("blgh,gph->blgp", u, bbar_re)
    bu_im = jnp.einsum("blgh,gph->blgp", u, bbar_im)
    a_seq_re = jnp.broadcast_to(abar_re, (1, seq_len, S5_GROUPS, S5_STATE))
    a_seq_im = jnp.broadcast_to(abar_im, (1, seq_len, S5_GROUPS, S5_STATE))
    _, _, s_re, s_im = lax.associative_scan(
        _ssm_combine, (a_seq_re, a_seq_im, bu_re, bu_im), axis=1)
    y = (jnp.einsum("blgp,ghp->blgh", s_re, c_re.astype(f32))
         - jnp.einsum("blgp,ghp->blgh", s_im, c_im.astype(f32)))
    y = y + d_skip.astype(f32).reshape(S5_GROUPS, S5_GROUP) * u
    y = jax.nn.gelu(y.reshape(bsz, seq_len, D_MIX))
    y = y * jax.nn.sigmoid(y @ glu_w.astype(f32) + glu_b.astype(f32))
    return y.astype(h.dtype) @ w_out


def spatial_gating_mixer(h, w_in, v_gain, w_s, b_s, w_out):
    bsz, seq_len, _ = h.shape
    u, v = jnp.split(h @ w_in, 2, axis=-1)
    v = rmsnorm(v, v_gain)
    vc = v.reshape(bsz, seq_len // CHUNK, CHUNK, SG_HEADS, SG_HEAD_DIM)
    causal = jnp.tril(jnp.ones((CHUNK, CHUNK), dtype=bool))
    ws = jnp.where(causal[None], w_s, jnp.zeros_like(w_s))
    vm = jnp.einsum("hts,bnshd->bnthd", ws, vc) + b_s.T[:, :, None]
    return (u * vm.reshape(bsz, seq_len, D_MIX)) @ w_out


def squared_relu_mlp(h, w1, w2):
    return jnp.square(jax.nn.relu(h @ w1)) @ w2


def _fwd_setup_inputs(seed: int = 0) -> dict:
    key = jax.random.key(seed)
    ks = iter(jax.random.split(key, 40))
    f32 = jnp.float32

    def nrm(shape, std):
        return std * jax.random.normal(next(ks), shape, f32)

    D = D_MODEL
    G, P = S5_GROUPS, S5_STATE
    x = nrm((BATCH, SEQ, D), 1.0)
    c = nrm((BATCH, D), 1.0)
    ada_w = nrm((DEPTH, D, 6 * D), 0.5 * D ** -0.5)
    ada_b = nrm((DEPTH, 6 * D), 0.02)
    norm1_g = 1.0 + nrm((DEPTH, D), 0.02)
    norm2_g = 1.0 + nrm((DEPTH, D), 0.02)
    ff_w1 = nrm((DEPTH, D, D_FF), D ** -0.5)
    ff_w2 = nrm((DEPTH, D_FF, D), D_FF ** -0.5)
    final_g = 1.0 + nrm((D,), 0.02)
    conv_w_in = nrm((N_A, D, 3 * D_MIX), D ** -0.5)
    conv_w = nrm((N_A, CONV_WIDTH, D_MIX), CONV_WIDTH ** -0.5)
    conv_b = nrm((N_A, D_MIX), 0.02)
    conv_w_out = nrm((N_A, D_MIX, D), D_MIX ** -0.5)
    ssm_w_in = nrm((N_B, D, D_MIX), D ** -0.5)
    ssm_a_re = -0.5 + nrm((N_B, G, P), 0.01)
    ssm_a_im = math.pi * jnp.arange(P, dtype=f32) + nrm((N_B, G, P), 0.01)
    ssm_log_dt = jax.random.uniform(next(ks), (N_B, G), f32,
                                    minval=math.log(DT_MIN), maxval=math.log(DT_MAX))
    ssm_b_re = nrm((N_B, G, P, S5_GROUP), S5_GROUP ** -0.5)
    ssm_b_im = nrm((N_B, G, P, S5_GROUP), S5_GROUP ** -0.5)
    ssm_c_re = nrm((N_B, G, S5_GROUP, P), P ** -0.5)
    ssm_c_im = nrm((N_B, G, S5_GROUP, P), P ** -0.5)
    ssm_d = nrm((N_B, D_MIX), 0.5)
    ssm_glu_w = nrm((N_B, D_MIX, D_MIX), D_MIX ** -0.5)
    ssm_glu_b = nrm((N_B, D_MIX), 0.02)
    ssm_w_out = nrm((N_B, D_MIX, D), D_MIX ** -0.5)
    sg_w_in = nrm((N_C, D, 2 * D_MIX), D ** -0.5)
    sg_v_g = 1.0 + nrm((N_C, D_MIX), 0.02)
    sg_w_s = nrm((N_C, SG_HEADS, CHUNK, CHUNK), CHUNK ** -0.5)
    sg_b_s = 1.0 + nrm((N_C, SG_HEADS, CHUNK), 0.02)
    sg_w_out = nrm((N_C, D_MIX, D), D_MIX ** -0.5)
    return {
        "x": x, "c": c, "ada_w": ada_w, "ada_b": ada_b,
        "norm1_g": norm1_g, "norm2_g": norm2_g, "ff_w1": ff_w1, "ff_w2": ff_w2,
        "final_g": final_g,
        "conv_w_in": conv_w_in, "conv_w": conv_w, "conv_b": conv_b, "conv_w_out": conv_w_out,
        "ssm_w_in": ssm_w_in, "ssm_a_re": ssm_a_re, "ssm_a_im": ssm_a_im,
        "ssm_log_dt": ssm_log_dt, "ssm_b_re": ssm_b_re, "ssm_b_im": ssm_b_im,
        "ssm_c_re": ssm_c_re, "ssm_c_im": ssm_c_im, "ssm_d": ssm_d,
        "ssm_glu_w": ssm_glu_w, "ssm_glu_b": ssm_glu_b, "ssm_w_out": ssm_w_out,
        "sg_w_in": sg_w_in, "sg_v_g": sg_v_g, "sg_w_s": sg_w_s, "sg_b_s": sg_b_s,
        "sg_w_out": sg_w_out,
    }


def _fwd_reference(x, c, ada_w, ada_b, norm1_g, norm2_g, ff_w1, ff_w2, final_g,
              conv_w_in, conv_w, conv_b, conv_w_out,
              ssm_w_in, ssm_a_re, ssm_a_im, ssm_log_dt, ssm_b_re, ssm_b_im,
              ssm_c_re, ssm_c_im, ssm_d, ssm_glu_w, ssm_glu_b, ssm_w_out,
              sg_w_in, sg_v_g, sg_w_s, sg_b_s, sg_w_out):
    c_act = jax.nn.silu(c)
    for i in range(DEPTH):
        kind = i % N_MIXERS
        j = i // N_MIXERS
        mod = (c_act @ ada_w[i] + ada_b[i])[:, None, :]
        sh1, sc1, g1, sh2, sc2, g2 = jnp.split(mod, 6, axis=-1)
        h = rmsnorm(x, norm1_g[i]) * (1.0 + sc1) + sh1
        if kind == 0:
            y = short_conv_mixer(h, conv_w_in[j], conv_w[j], conv_b[j], conv_w_out[j])
        elif kind == 1:
            y = s5_mixer(h, ssm_w_in[j], ssm_a_re[j], ssm_a_im[j], ssm_log_dt[j],
                         ssm_b_re[j], ssm_b_im[j], ssm_c_re[j], ssm_c_im[j], ssm_d[j],
                         ssm_glu_w[j], ssm_glu_b[j], ssm_w_out[j])
        else:
            y = spatial_gating_mixer(h, sg_w_in[j], sg_v_g[j], sg_w_s[j], sg_b_s[j], sg_w_out[j])
        x = x + g1 * y
        h = rmsnorm(x, norm2_g[i]) * (1.0 + sc2) + sh2
        x = x + g2 * squared_relu_mlp(h, ff_w1[i], ff_w2[i])
    return rmsnorm(x, final_g)


import jax as _jax
import jax.numpy as _jnp

TWIN_FORMAT = 'train_step'
FWD_PARAMS = ['x', 'c', 'ada_w', 'ada_b', 'norm1_g', 'norm2_g', 'ff_w1', 'ff_w2', 'final_g', 'conv_w_in', 'conv_w', 'conv_b', 'conv_w_out', 'ssm_w_in', 'ssm_a_re', 'ssm_a_im', 'ssm_log_dt', 'ssm_b_re', 'ssm_b_im', 'ssm_c_re', 'ssm_c_im', 'ssm_d', 'ssm_glu_w', 'ssm_glu_b', 'ssm_w_out', 'sg_w_in', 'sg_v_g', 'sg_w_s', 'sg_b_s', 'sg_w_out']
TWIN_WEIGHTS = ['ada_w', 'ada_b', 'norm1_g', 'norm2_g', 'ff_w1', 'ff_w2', 'final_g', 'conv_w_in', 'conv_w', 'conv_b', 'conv_w_out', 'ssm_w_in', 'ssm_a_re', 'ssm_a_im', 'ssm_log_dt', 'ssm_b_re', 'ssm_b_im', 'ssm_c_re', 'ssm_c_im', 'ssm_d', 'ssm_glu_w', 'ssm_glu_b', 'ssm_w_out', 'sg_w_in', 'sg_v_g', 'sg_w_s', 'sg_b_s', 'sg_w_out']
TWIN_DIFF_INPUT = 'x'
TWIN_INPUTS = ['x', 'c', 'ada_w', 'ada_b', 'norm1_g', 'norm2_g', 'ff_w1', 'ff_w2', 'final_g', 'conv_w_in', 'conv_w', 'conv_b', 'conv_w_out', 'ssm_w_in', 'ssm_a_re', 'ssm_a_im', 'ssm_log_dt', 'ssm_b_re', 'ssm_b_im', 'ssm_c_re', 'ssm_c_im', 'ssm_d', 'ssm_glu_w', 'ssm_glu_b', 'ssm_w_out', 'sg_w_in', 'sg_v_g', 'sg_w_s', 'sg_b_s', 'sg_w_out', 'loss_target', 'm_ada_w', 'm_ada_b', 'm_norm1_g', 'm_norm2_g', 'm_ff_w1', 'm_ff_w2', 'm_final_g', 'm_conv_w_in', 'm_conv_w', 'm_conv_b', 'm_conv_w_out', 'm_ssm_w_in', 'm_ssm_a_re', 'm_ssm_a_im', 'm_ssm_log_dt', 'm_ssm_b_re', 'm_ssm_b_im', 'm_ssm_c_re', 'm_ssm_c_im', 'm_ssm_d', 'm_ssm_glu_w', 'm_ssm_glu_b', 'm_ssm_w_out', 'm_sg_w_in', 'm_sg_v_g', 'm_sg_w_s', 'm_sg_b_s', 'm_sg_w_out', 'v_ada_w', 'v_ada_b', 'v_norm1_g', 'v_norm2_g', 'v_ff_w1', 'v_ff_w2', 'v_final_g', 'v_conv_w_in', 'v_conv_w', 'v_conv_b', 'v_conv_w_out', 'v_ssm_w_in', 'v_ssm_a_re', 'v_ssm_a_im', 'v_ssm_log_dt', 'v_ssm_b_re', 'v_ssm_b_im', 'v_ssm_c_re', 'v_ssm_c_im', 'v_ssm_d', 'v_ssm_glu_w', 'v_ssm_glu_b', 'v_ssm_w_out', 'v_sg_w_in', 'v_sg_v_g', 'v_sg_w_s', 'v_sg_b_s', 'v_sg_w_out']
TWIN_OUTPUTS = ['loss', 'grad_x', 'grad_ada_w', 'grad_ada_b', 'grad_norm1_g', 'grad_norm2_g', 'grad_ff_w1', 'grad_ff_w2', 'grad_final_g', 'grad_conv_w_in', 'grad_conv_w', 'grad_conv_b', 'grad_conv_w_out', 'grad_ssm_w_in', 'grad_ssm_a_re', 'grad_ssm_a_im', 'grad_ssm_log_dt', 'grad_ssm_b_re', 'grad_ssm_b_im', 'grad_ssm_c_re', 'grad_ssm_c_im', 'grad_ssm_d', 'grad_ssm_glu_w', 'grad_ssm_glu_b', 'grad_ssm_w_out', 'grad_sg_w_in', 'grad_sg_v_g', 'grad_sg_w_s', 'grad_sg_b_s', 'grad_sg_w_out', 'delta_ada_w', 'delta_ada_b', 'delta_norm1_g', 'delta_norm2_g', 'delta_ff_w1', 'delta_ff_w2', 'delta_final_g', 'delta_conv_w_in', 'delta_conv_w', 'delta_conv_b', 'delta_conv_w_out', 'delta_ssm_w_in', 'delta_ssm_a_re', 'delta_ssm_a_im', 'delta_ssm_log_dt', 'delta_ssm_b_re', 'delta_ssm_b_im', 'delta_ssm_c_re', 'delta_ssm_c_im', 'delta_ssm_d', 'delta_ssm_glu_w', 'delta_ssm_glu_b', 'delta_ssm_w_out', 'delta_sg_w_in', 'delta_sg_v_g', 'delta_sg_w_s', 'delta_sg_b_s', 'delta_sg_w_out', 'new_m_ada_w', 'new_m_ada_b', 'new_m_norm1_g', 'new_m_norm2_g', 'new_m_ff_w1', 'new_m_ff_w2', 'new_m_final_g', 'new_m_conv_w_in', 'new_m_conv_w', 'new_m_conv_b', 'new_m_conv_w_out', 'new_m_ssm_w_in', 'new_m_ssm_a_re', 'new_m_ssm_a_im', 'new_m_ssm_log_dt', 'new_m_ssm_b_re', 'new_m_ssm_b_im', 'new_m_ssm_c_re', 'new_m_ssm_c_im', 'new_m_ssm_d', 'new_m_ssm_glu_w', 'new_m_ssm_glu_b', 'new_m_ssm_w_out', 'new_m_sg_w_in', 'new_m_sg_v_g', 'new_m_sg_w_s', 'new_m_sg_b_s', 'new_m_sg_w_out', 'new_v_ada_w', 'new_v_ada_b', 'new_v_norm1_g', 'new_v_norm2_g', 'new_v_ff_w1', 'new_v_ff_w2', 'new_v_final_g', 'new_v_conv_w_in', 'new_v_conv_w', 'new_v_conv_b', 'new_v_conv_w_out', 'new_v_ssm_w_in', 'new_v_ssm_a_re', 'new_v_ssm_a_im', 'new_v_ssm_log_dt', 'new_v_ssm_b_re', 'new_v_ssm_b_im', 'new_v_ssm_c_re', 'new_v_ssm_c_im', 'new_v_ssm_d', 'new_v_ssm_glu_w', 'new_v_ssm_glu_b', 'new_v_ssm_w_out', 'new_v_sg_w_in', 'new_v_sg_v_g', 'new_v_sg_w_s', 'new_v_sg_b_s', 'new_v_sg_w_out']
TWIN_LEAF_KINDS = {'loss': 'loss', 'grad_x': 'grad_x', 'grad_ada_w': 'grad_w', 'grad_ada_b': 'grad_w', 'grad_norm1_g': 'grad_w', 'grad_norm2_g': 'grad_w', 'grad_ff_w1': 'grad_w', 'grad_ff_w2': 'grad_w', 'grad_final_g': 'grad_w', 'grad_conv_w_in': 'grad_w', 'grad_conv_w': 'grad_w', 'grad_conv_b': 'grad_w', 'grad_conv_w_out': 'grad_w', 'grad_ssm_w_in': 'grad_w', 'grad_ssm_a_re': 'grad_w', 'grad_ssm_a_im': 'grad_w', 'grad_ssm_log_dt': 'grad_w', 'grad_ssm_b_re': 'grad_w', 'grad_ssm_b_im': 'grad_w', 'grad_ssm_c_re': 'grad_w', 'grad_ssm_c_im': 'grad_w', 'grad_ssm_d': 'grad_w', 'grad_ssm_glu_w': 'grad_w', 'grad_ssm_glu_b': 'grad_w', 'grad_ssm_w_out': 'grad_w', 'grad_sg_w_in': 'grad_w', 'grad_sg_v_g': 'grad_w', 'grad_sg_w_s': 'grad_w', 'grad_sg_b_s': 'grad_w', 'grad_sg_w_out': 'grad_w', 'delta_ada_w': 'delta_w', 'delta_ada_b': 'delta_w', 'delta_norm1_g': 'delta_w', 'delta_norm2_g': 'delta_w', 'delta_ff_w1': 'delta_w', 'delta_ff_w2': 'delta_w', 'delta_final_g': 'delta_w', 'delta_conv_w_in': 'delta_w', 'delta_conv_w': 'delta_w', 'delta_conv_b': 'delta_w', 'delta_conv_w_out': 'delta_w', 'delta_ssm_w_in': 'delta_w', 'delta_ssm_a_re': 'delta_w', 'delta_ssm_a_im': 'delta_w', 'delta_ssm_log_dt': 'delta_w', 'delta_ssm_b_re': 'delta_w', 'delta_ssm_b_im': 'delta_w', 'delta_ssm_c_re': 'delta_w', 'delta_ssm_c_im': 'delta_w', 'delta_ssm_d': 'delta_w', 'delta_ssm_glu_w': 'delta_w', 'delta_ssm_glu_b': 'delta_w', 'delta_ssm_w_out': 'delta_w', 'delta_sg_w_in': 'delta_w', 'delta_sg_v_g': 'delta_w', 'delta_sg_w_s': 'delta_w', 'delta_sg_b_s': 'delta_w', 'delta_sg_w_out': 'delta_w', 'new_m_ada_w': 'new_m', 'new_m_ada_b': 'new_m', 'new_m_norm1_g': 'new_m', 'new_m_norm2_g': 'new_m', 'new_m_ff_w1': 'new_m', 'new_m_ff_w2': 'new_m', 'new_m_final_g': 'new_m', 'new_m_conv_w_in': 'new_m', 'new_m_conv_w': 'new_m', 'new_m_conv_b': 'new_m', 'new_m_conv_w_out': 'new_m', 'new_m_ssm_w_in': 'new_m', 'new_m_ssm_a_re': 'new_m', 'new_m_ssm_a_im': 'new_m', 'new_m_ssm_log_dt': 'new_m', 'new_m_ssm_b_re': 'new_m', 'new_m_ssm_b_im': 'new_m', 'new_m_ssm_c_re': 'new_m', 'new_m_ssm_c_im': 'new_m', 'new_m_ssm_d': 'new_m', 'new_m_ssm_glu_w': 'new_m', 'new_m_ssm_glu_b': 'new_m', 'new_m_ssm_w_out': 'new_m', 'new_m_sg_w_in': 'new_m', 'new_m_sg_v_g': 'new_m', 'new_m_sg_w_s': 'new_m', 'new_m_sg_b_s': 'new_m', 'new_m_sg_w_out': 'new_m', 'new_v_ada_w': 'new_v', 'new_v_ada_b': 'new_v', 'new_v_norm1_g': 'new_v', 'new_v_norm2_g': 'new_v', 'new_v_ff_w1': 'new_v', 'new_v_ff_w2': 'new_v', 'new_v_final_g': 'new_v', 'new_v_conv_w_in': 'new_v', 'new_v_conv_w': 'new_v', 'new_v_conv_b': 'new_v', 'new_v_conv_w_out': 'new_v', 'new_v_ssm_w_in': 'new_v', 'new_v_ssm_a_re': 'new_v', 'new_v_ssm_a_im': 'new_v', 'new_v_ssm_log_dt': 'new_v', 'new_v_ssm_b_re': 'new_v', 'new_v_ssm_b_im': 'new_v', 'new_v_ssm_c_re': 'new_v', 'new_v_ssm_c_im': 'new_v', 'new_v_ssm_d': 'new_v', 'new_v_ssm_glu_w': 'new_v', 'new_v_ssm_glu_b': 'new_v', 'new_v_ssm_w_out': 'new_v', 'new_v_sg_w_in': 'new_v', 'new_v_sg_v_g': 'new_v', 'new_v_sg_w_s': 'new_v', 'new_v_sg_b_s': 'new_v', 'new_v_sg_w_out': 'new_v'}


def _forward(args):
    return _fwd_reference(*[args[k] for k in FWD_PARAMS])


def _output_shape():
    out = _jax.eval_shape(lambda: _forward(_fwd_setup_inputs(0)))
    return out.shape, out.dtype

N_MICROBATCH = 1
ADAM_LR = 0.001
ADAM_B1 = 0.9
ADAM_B2 = 0.999
ADAM_EPS = 1e-08
ADAM_WD = 0.01
ADAM_STEP = 10
PER_EXAMPLE_BATCH_AXIS = {'x': 0, 'c': 0, 'loss_target': 0}
SHARED_INPUTS = []
_WEIGHT_DTYPES = {'ada_w': _jnp.float32, 'ada_b': _jnp.float32, 'norm1_g': _jnp.float32, 'norm2_g': _jnp.float32, 'ff_w1': _jnp.float32, 'ff_w2': _jnp.float32, 'final_g': _jnp.float32, 'conv_w_in': _jnp.float32, 'conv_w': _jnp.float32, 'conv_b': _jnp.float32, 'conv_w_out': _jnp.float32, 'ssm_w_in': _jnp.float32, 'ssm_a_re': _jnp.float32, 'ssm_a_im': _jnp.float32, 'ssm_log_dt': _jnp.float32, 'ssm_b_re': _jnp.float32, 'ssm_b_im': _jnp.float32, 'ssm_c_re': _jnp.float32, 'ssm_c_im': _jnp.float32, 'ssm_d': _jnp.float32, 'ssm_glu_w': _jnp.float32, 'ssm_glu_b': _jnp.float32, 'ssm_w_out': _jnp.float32, 'sg_w_in': _jnp.float32, 'sg_v_g': _jnp.float32, 'sg_w_s': _jnp.float32, 'sg_b_s': _jnp.float32, 'sg_w_out': _jnp.float32}
MOMENT_SCALE = {'ada_w': 8.662009e-02, 'ada_b': 1.425808e-01, 'norm1_g': 8.487000e-02, 'norm2_g': 7.310706e-02, 'ff_w1': 3.827410e-02, 'ff_w2': 6.901223e-02, 'final_g': 3.252286e+01, 'conv_w_in': 6.489666e-02, 'conv_w': 6.490052e-02, 'conv_b': 5.471934e-02, 'conv_w_out': 6.446942e-02, 'ssm_w_in': 1.023359e-02, 'ssm_a_re': 4.403179e-03, 'ssm_a_im': 3.896311e-03, 'ssm_log_dt': 1.447080e+00, 'ssm_b_re': 1.200285e-03, 'ssm_b_im': 1.274780e-03, 'ssm_c_re': 2.564255e-03, 'ssm_c_im': 2.501648e-03, 'ssm_d': 1.991142e-02, 'ssm_glu_w': 1.775394e-03, 'ssm_glu_b': 5.023228e-03, 'ssm_w_out': 1.021187e-02, 'sg_w_in': 5.001907e-02, 'sg_v_g': 3.556822e-02, 'sg_w_s': 3.515199e-02, 'sg_b_s': 5.138062e-02, 'sg_w_out': 6.137036e-02}


def _to_microbatches(a, axis):
    t = _jnp.moveaxis(a, axis, 0)
    t = t.reshape((N_MICROBATCH, t.shape[0] // N_MICROBATCH) + t.shape[1:])
    return _jnp.moveaxis(t, 1, axis + 1)


def setup_inputs(seed: int = 0) -> dict:
    inp = _fwd_setup_inputs(seed)
    key = _jax.random.fold_in(_jax.random.key(seed), 7919)
    shape, _ = _output_shape()
    out = dict(inp)
    out["loss_target"] = _jax.random.normal(_jax.random.fold_in(key, 0), shape, _jnp.float32)
    for i, name in enumerate(TWIN_WEIGHTS):
        w = inp[name].astype(_jnp.float32)
        if MOMENT_SCALE is None:
            s = _jnp.sqrt(_jnp.mean(_jnp.square(w)) + 1e-30)
        else:
            s = MOMENT_SCALE[name]
        km, kv = _jax.random.split(_jax.random.fold_in(key, i + 1))
        out[name] = w
        out["m_" + name] = s * _jax.random.normal(km, w.shape, _jnp.float32)
        out["v_" + name] = (s * s) * _jax.random.uniform(kv, w.shape, _jnp.float32, 0.5, 1.5)
    if N_MICROBATCH > 1:
        for name, axis in PER_EXAMPLE_BATCH_AXIS.items():
            out[name] = _to_microbatches(out[name], axis)
    return {'x': out['x'], 'c': out['c'], 'ada_w': out['ada_w'], 'ada_b': out['ada_b'], 'norm1_g': out['norm1_g'], 'norm2_g': out['norm2_g'], 'ff_w1': out['ff_w1'], 'ff_w2': out['ff_w2'], 'final_g': out['final_g'], 'conv_w_in': out['conv_w_in'], 'conv_w': out['conv_w'], 'conv_b': out['conv_b'], 'conv_w_out': out['conv_w_out'], 'ssm_w_in': out['ssm_w_in'], 'ssm_a_re': out['ssm_a_re'], 'ssm_a_im': out['ssm_a_im'], 'ssm_log_dt': out['ssm_log_dt'], 'ssm_b_re': out['ssm_b_re'], 'ssm_b_im': out['ssm_b_im'], 'ssm_c_re': out['ssm_c_re'], 'ssm_c_im': out['ssm_c_im'], 'ssm_d': out['ssm_d'], 'ssm_glu_w': out['ssm_glu_w'], 'ssm_glu_b': out['ssm_glu_b'], 'ssm_w_out': out['ssm_w_out'], 'sg_w_in': out['sg_w_in'], 'sg_v_g': out['sg_v_g'], 'sg_w_s': out['sg_w_s'], 'sg_b_s': out['sg_b_s'], 'sg_w_out': out['sg_w_out'], 'loss_target': out['loss_target'], 'm_ada_w': out['m_ada_w'], 'm_ada_b': out['m_ada_b'], 'm_norm1_g': out['m_norm1_g'], 'm_norm2_g': out['m_norm2_g'], 'm_ff_w1': out['m_ff_w1'], 'm_ff_w2': out['m_ff_w2'], 'm_final_g': out['m_final_g'], 'm_conv_w_in': out['m_conv_w_in'], 'm_conv_w': out['m_conv_w'], 'm_conv_b': out['m_conv_b'], 'm_conv_w_out': out['m_conv_w_out'], 'm_ssm_w_in': out['m_ssm_w_in'], 'm_ssm_a_re': out['m_ssm_a_re'], 'm_ssm_a_im': out['m_ssm_a_im'], 'm_ssm_log_dt': out['m_ssm_log_dt'], 'm_ssm_b_re': out['m_ssm_b_re'], 'm_ssm_b_im': out['m_ssm_b_im'], 'm_ssm_c_re': out['m_ssm_c_re'], 'm_ssm_c_im': out['m_ssm_c_im'], 'm_ssm_d': out['m_ssm_d'], 'm_ssm_glu_w': out['m_ssm_glu_w'], 'm_ssm_glu_b': out['m_ssm_glu_b'], 'm_ssm_w_out': out['m_ssm_w_out'], 'm_sg_w_in': out['m_sg_w_in'], 'm_sg_v_g': out['m_sg_v_g'], 'm_sg_w_s': out['m_sg_w_s'], 'm_sg_b_s': out['m_sg_b_s'], 'm_sg_w_out': out['m_sg_w_out'], 'v_ada_w': out['v_ada_w'], 'v_ada_b': out['v_ada_b'], 'v_norm1_g': out['v_norm1_g'], 'v_norm2_g': out['v_norm2_g'], 'v_ff_w1': out['v_ff_w1'], 'v_ff_w2': out['v_ff_w2'], 'v_final_g': out['v_final_g'], 'v_conv_w_in': out['v_conv_w_in'], 'v_conv_w': out['v_conv_w'], 'v_conv_b': out['v_conv_b'], 'v_conv_w_out': out['v_conv_w_out'], 'v_ssm_w_in': out['v_ssm_w_in'], 'v_ssm_a_re': out['v_ssm_a_re'], 'v_ssm_a_im': out['v_ssm_a_im'], 'v_ssm_log_dt': out['v_ssm_log_dt'], 'v_ssm_b_re': out['v_ssm_b_re'], 'v_ssm_b_im': out['v_ssm_b_im'], 'v_ssm_c_re': out['v_ssm_c_re'], 'v_ssm_c_im': out['v_ssm_c_im'], 'v_ssm_d': out['v_ssm_d'], 'v_ssm_glu_w': out['v_ssm_glu_w'], 'v_ssm_glu_b': out['v_ssm_glu_b'], 'v_ssm_w_out': out['v_ssm_w_out'], 'v_sg_w_in': out['v_sg_w_in'], 'v_sg_v_g': out['v_sg_v_g'], 'v_sg_w_s': out['v_sg_w_s'], 'v_sg_b_s': out['v_sg_b_s'], 'v_sg_w_out': out['v_sg_w_out']}


def _loss(weights, diff, rest, loss_target):
    with _jax.named_scope("forward"):
        args = {**rest, TWIN_DIFF_INPUT: diff, **{k: w.astype(_WEIGHT_DTYPES[k]) for k, w in weights.items()}}
        y = _forward(args)
    with _jax.named_scope("loss_head"):
        err = _jnp.square(y.astype(_jnp.float32) - loss_target)
        return 0.5 * _jnp.sum(_jnp.mean(err, axis=-1)) if err.ndim else 0.5 * err


def _adamw(w, g, m, v):
    m = ADAM_B1 * m + (1.0 - ADAM_B1) * g
    v = ADAM_B2 * v + (1.0 - ADAM_B2) * _jnp.square(g)
    m_hat = m / (1.0 - ADAM_B1 ** ADAM_STEP)
    v_hat = v / (1.0 - ADAM_B2 ** ADAM_STEP)
    delta = -ADAM_LR * (m_hat / (_jnp.sqrt(v_hat) + ADAM_EPS) + ADAM_WD * w)
    return delta, m, v


def reference(x, c, ada_w, ada_b, norm1_g, norm2_g, ff_w1, ff_w2, final_g, conv_w_in, conv_w, conv_b, conv_w_out, ssm_w_in, ssm_a_re, ssm_a_im, ssm_log_dt, ssm_b_re, ssm_b_im, ssm_c_re, ssm_c_im, ssm_d, ssm_glu_w, ssm_glu_b, ssm_w_out, sg_w_in, sg_v_g, sg_w_s, sg_b_s, sg_w_out, loss_target, m_ada_w, m_ada_b, m_norm1_g, m_norm2_g, m_ff_w1, m_ff_w2, m_final_g, m_conv_w_in, m_conv_w, m_conv_b, m_conv_w_out, m_ssm_w_in, m_ssm_a_re, m_ssm_a_im, m_ssm_log_dt, m_ssm_b_re, m_ssm_b_im, m_ssm_c_re, m_ssm_c_im, m_ssm_d, m_ssm_glu_w, m_ssm_glu_b, m_ssm_w_out, m_sg_w_in, m_sg_v_g, m_sg_w_s, m_sg_b_s, m_sg_w_out, v_ada_w, v_ada_b, v_norm1_g, v_norm2_g, v_ff_w1, v_ff_w2, v_final_g, v_conv_w_in, v_conv_w, v_conv_b, v_conv_w_out, v_ssm_w_in, v_ssm_a_re, v_ssm_a_im, v_ssm_log_dt, v_ssm_b_re, v_ssm_b_im, v_ssm_c_re, v_ssm_c_im, v_ssm_d, v_ssm_glu_w, v_ssm_glu_b, v_ssm_w_out, v_sg_w_in, v_sg_v_g, v_sg_w_s, v_sg_b_s, v_sg_w_out):
    given = dict(x=x, c=c, ada_w=ada_w, ada_b=ada_b, norm1_g=norm1_g, norm2_g=norm2_g, ff_w1=ff_w1, ff_w2=ff_w2, final_g=final_g, conv_w_in=conv_w_in, conv_w=conv_w, conv_b=conv_b, conv_w_out=conv_w_out, ssm_w_in=ssm_w_in, ssm_a_re=ssm_a_re, ssm_a_im=ssm_a_im, ssm_log_dt=ssm_log_dt, ssm_b_re=ssm_b_re, ssm_b_im=ssm_b_im, ssm_c_re=ssm_c_re, ssm_c_im=ssm_c_im, ssm_d=ssm_d, ssm_glu_w=ssm_glu_w, ssm_glu_b=ssm_glu_b, ssm_w_out=ssm_w_out, sg_w_in=sg_w_in, sg_v_g=sg_v_g, sg_w_s=sg_w_s, sg_b_s=sg_b_s, sg_w_out=sg_w_out, loss_target=loss_target, m_ada_w=m_ada_w, m_ada_b=m_ada_b, m_norm1_g=m_norm1_g, m_norm2_g=m_norm2_g, m_ff_w1=m_ff_w1, m_ff_w2=m_ff_w2, m_final_g=m_final_g, m_conv_w_in=m_conv_w_in, m_conv_w=m_conv_w, m_conv_b=m_conv_b, m_conv_w_out=m_conv_w_out, m_ssm_w_in=m_ssm_w_in, m_ssm_a_re=m_ssm_a_re, m_ssm_a_im=m_ssm_a_im, m_ssm_log_dt=m_ssm_log_dt, m_ssm_b_re=m_ssm_b_re, m_ssm_b_im=m_ssm_b_im, m_ssm_c_re=m_ssm_c_re, m_ssm_c_im=m_ssm_c_im, m_ssm_d=m_ssm_d, m_ssm_glu_w=m_ssm_glu_w, m_ssm_glu_b=m_ssm_glu_b, m_ssm_w_out=m_ssm_w_out, m_sg_w_in=m_sg_w_in, m_sg_v_g=m_sg_v_g, m_sg_w_s=m_sg_w_s, m_sg_b_s=m_sg_b_s, m_sg_w_out=m_sg_w_out, v_ada_w=v_ada_w, v_ada_b=v_ada_b, v_norm1_g=v_norm1_g, v_norm2_g=v_norm2_g, v_ff_w1=v_ff_w1, v_ff_w2=v_ff_w2, v_final_g=v_final_g, v_conv_w_in=v_conv_w_in, v_conv_w=v_conv_w, v_conv_b=v_conv_b, v_conv_w_out=v_conv_w_out, v_ssm_w_in=v_ssm_w_in, v_ssm_a_re=v_ssm_a_re, v_ssm_a_im=v_ssm_a_im, v_ssm_log_dt=v_ssm_log_dt, v_ssm_b_re=v_ssm_b_re, v_ssm_b_im=v_ssm_b_im, v_ssm_c_re=v_ssm_c_re, v_ssm_c_im=v_ssm_c_im, v_ssm_d=v_ssm_d, v_ssm_glu_w=v_ssm_glu_w, v_ssm_glu_b=v_ssm_glu_b, v_ssm_w_out=v_ssm_w_out, v_sg_w_in=v_sg_w_in, v_sg_v_g=v_sg_v_g, v_sg_w_s=v_sg_w_s, v_sg_b_s=v_sg_b_s, v_sg_w_out=v_sg_w_out)
    weights = {n: given[n] for n in TWIN_WEIGHTS}
    shared = {n: given[n] for n in SHARED_INPUTS}
    per_example = {n: given[n] for n in ['x', 'c']}
    grad_fn = _jax.value_and_grad(_loss, argnums=(0, 1))

    def one_microbatch(ex, loss_target):
        ex = dict(ex)
        diff = ex.pop(TWIN_DIFF_INPUT)
        return grad_fn(weights, diff, {**shared, **ex}, loss_target)

    if N_MICROBATCH == 1:
        loss, (grad_w, grad_x) = one_microbatch(per_example, given["loss_target"])
    else:
        def body(carry, xs):
            loss_sum, grad_sum = carry
            l_k, (gw_k, gx_k) = one_microbatch(xs[0], xs[1])
            with _jax.named_scope("update"):
                return (loss_sum + l_k, _jax.tree.map(_jnp.add, grad_sum, gw_k)), gx_k

        init = (_jnp.zeros((), _jnp.float32), _jax.tree.map(_jnp.zeros_like, weights))
        (loss, grad_w), grad_x = _jax.lax.scan(body, init, (per_example, given["loss_target"]))
    with _jax.named_scope("update"):
        delta_w, new_m, new_v = {}, {}, {}
        for n in TWIN_WEIGHTS:
            delta_w[n], new_m[n], new_v[n] = _adamw(weights[n], grad_w[n], given["m_" + n], given["v_" + n])
    return (loss, grad_x, *[grad_w[n] for n in TWIN_WEIGHTS], *[delta_w[n] for n in TWIN_WEIGHTS],
            *[new_m[n] for n in TWIN_WEIGHTS], *[new_v[n] for n in TWIN_WEIGHTS])
```

```python
import functools

import jax
import jax.numpy as jnp
from jax import lax
from jax.experimental import pallas as pl
from jax.experimental.pallas import tpu as pltpu

F32 = jnp.float32
BF16 = jnp.bfloat16
D = 1024
EPS = 1e-6
DEPTH = 4
MIXER_OF_LAYER = (0, 1, 2, 0)
S5_G, S5_H, S5_P = 64, 16, 64
S5_NB = 4
S5_BH = S5_H * 16
S5_BP = S5_P * 16
NSTATE = S5_G * S5_P
SG_HEADS, SG_CHUNK = 8, 128
ADAM_LR, ADAM_B1, ADAM_B2, ADAM_EPS, ADAM_WD, ADAM_STEP = 0.001, 0.9, 0.999, 1e-08, 0.01, 10
N_DEV = 8
MESH = pl.DeviceIdType.MESH
LANES = 1024
R_SH1, R_SC1, R_G1, R_SH2, R_SC2, R_G2, R_N1, R_N2 = range(8)


def _dot(a, b):
    return jnp.dot(a, b, preferred_element_type=F32)


def _dot_nt(a, b):
    return lax.dot_general(a, b, (((1,), (1,)), ((), ())), preferred_element_type=F32)


def _dot_tn(a, b):
    return lax.dot_general(a, b, (((0,), (0,)), ((), ())), preferred_element_type=F32)


def _bf(x):
    return x.astype(BF16)


def _sum0(x):
    return jnp.sum(x, axis=0, keepdims=True)


def _params(n_axes, vmem_mb=48):
    return pltpu.CompilerParams(dimension_semantics=("arbitrary",) * n_axes, vmem_limit_bytes=vmem_mb << 20)


def _rows(tm, cols, nt=None):
    if nt is None:
        return pl.BlockSpec((tm, cols), lambda i: (i, 0))
    return pl.BlockSpec((tm, cols), lambda i: (nt - 1 - i, 0))


def _whole(shape):
    nd = len(shape)
    return pl.BlockSpec(shape, lambda *_: (0,) * nd)


def _layer_w(r, c, layer):
    return pl.BlockSpec((None, r, c), lambda *_: (layer, 0, 0), pipeline_mode=pl.Buffered(1))


def _const_w(shape):
    nd = len(shape)
    return pl.BlockSpec(shape, lambda *_: (0,) * nd, pipeline_mode=pl.Buffered(1))


def _norm_mod(x, ng, sc, sh):
    r = lax.rsqrt(jnp.mean(x * x, axis=-1, keepdims=True) + EPS)
    xn = x * r
    return (xn * ng) * (1.0 + sc) + sh, xn, r


def _norm_mod_bwd(dh, xn, r, ng, sc):
    dxn = dh * (ng * (1.0 + sc))
    return r * (dxn - xn * jnp.mean(dxn * xn, axis=-1, keepdims=True))


def _shift_down(z, prev8, k):
    row = lax.broadcasted_iota(jnp.int32, z.shape, 0)
    if k == 1:
        return jnp.where(row >= 1, pltpu.roll(z, 1, 0), prev8[7:8])
    return jnp.where(row >= 2, pltpu.roll(z, 2, 0), jnp.where(row == 0, prev8[6:7], prev8[7:8]))


def _shift_up(z, next8, k):
    n = z.shape[0]
    row = lax.broadcasted_iota(jnp.int32, z.shape, 0)
    if k == 1:
        return jnp.where(row <= n - 2, pltpu.roll(z, n - 1, 0), next8[0:1])
    return jnp.where(row <= n - 3, pltpu.roll(z, n - 2, 0), jnp.where(row == n - 2, next8[0:1], next8[1:2]))


def _place():
    x, y, c = lax.axis_index("x"), lax.axis_index("y"), lax.axis_index("c")
    chips = [(1 - x, y), (x, 1 - y), (1 - x, 1 - y)]
    return x, y, c, chips


def _allgather_small(x_shard, name, with_sum=False):
    m_per, n = x_shard.shape

    def body(x_ref, *rest):
        if with_sum:
            out_ref, sum_ref, send_sems, recv_sems, local_sem = rest
        else:
            out_ref, send_sems, recv_sems, local_sem = rest
        x, y, c, chips = _place()
        me, sibling = (x, y, c), (x, y, 1 - c)

        def rows(px, py, pc):
            return out_ref.at[pl.ds((4 * px + 2 * py + pc) * m_per, m_per), :]

        def copy(k, block, to, src=None):
            return pltpu.make_async_remote_copy(
                src_ref=rows(*block) if src is None else src, dst_ref=rows(*block),
                send_sem=send_sems.at[k], recv_sem=recv_sems.at[k], device_id=to, device_id_type=MESH)

        mine = pltpu.make_async_copy(x_ref, rows(*me), local_sem)
        mine.start()
        first = [copy(0, me, sibling, src=x_ref)]
        first += [copy(1 + j, me, (*chip, c), src=x_ref) for j, chip in enumerate(chips)]
        for cp in first:
            cp.start()
        passed = [copy(4 + j, (*chip, c), sibling) for j, chip in enumerate(chips)]
        for j, chip in enumerate(chips):
            copy(1 + j, (*chip, c), me).wait_recv()
            passed[j].start()
        copy(0, sibling, me).wait_recv()
        for j, chip in enumerate(chips):
            copy(4 + j, (*chip, 1 - c), me).wait_recv()
        for cp in first + passed:
            cp.wait_send()
        mine.wait()
        if with_sum:
            acc = out_ref[0:m_per, :]
            for d in range(1, N_DEV):
                acc = acc + out_ref[d * m_per:(d + 1) * m_per, :]
            sum_ref[...] = acc

    out_shape = [jax.ShapeDtypeStruct((N_DEV * m_per, n), F32)]
    out_specs = [pl.BlockSpec(memory_space=pltpu.VMEM)]
    if with_sum:
        out_shape.append(jax.ShapeDtypeStruct((m_per, n), F32))
        out_specs.append(pl.BlockSpec(memory_space=pltpu.VMEM))
    res = pl.pallas_call(
        body, name=name, out_shape=out_shape,
        in_specs=[pl.BlockSpec(memory_space=pltpu.VMEM)], out_specs=out_specs,
        scratch_shapes=[pltpu.SemaphoreType.DMA((7,)), pltpu.SemaphoreType.DMA((7,)), pltpu.SemaphoreType.DMA],
        compiler_params=pltpu.CompilerParams(vmem_limit_bytes=48 << 20),
    )(x_shard)
    return tuple(res) if with_sum else res[0]


def _shard_region(ref, full_shape, axis, chip_k, half=None):
    _, r, c = full_shape
    if axis == 1:
        rs = r // 4
        if half is None:
            return ref.at[:, pl.ds(pl.multiple_of(chip_k * rs, 128), rs), :]
        return ref.at[:, pl.ds(pl.multiple_of(chip_k * rs + half * (rs // 2), 128), rs // 2), :]
    cs = c // 4
    if half is None:
        return ref.at[:, :, pl.ds(pl.multiple_of(chip_k * cs, 128), cs)]
    return ref.at[:, pl.ds(pl.multiple_of(half * (r // 2), 128), r // 2), pl.ds(pl.multiple_of(chip_k * cs, 128), cs)]


def _allgather_weights(shards, axes, name):
    n_arr = len(shards)
    fulls = []
    for s, ax in zip(shards, axes):
        shp = list(s.shape)
        shp[ax] *= 4
        fulls.append(tuple(shp))

    def body(*refs):
        ins, outs = refs[:n_arr], refs[n_arr:2 * n_arr]
        send_sems, recv_sems, local_sems = refs[2 * n_arr:]
        x, y, c, chips = _place()
        k_me = 2 * x + y
        sibling = (x, y, 1 - c)

        def src_half(a, half):
            r = shards[a].shape[1]
            return ins[a].at[:, pl.ds(pl.multiple_of(half * (r // 2), 64), r // 2), :]

        def remote(a, slot, src, dst, to):
            return pltpu.make_async_remote_copy(
                src_ref=src, dst_ref=dst, send_sem=send_sems.at[a * 6 + slot], recv_sem=recv_sems.at[a * 6 + slot],
                device_id=to, device_id_type=MESH)

        started, locals_ = [], []
        for a in range(n_arr):
            lc = pltpu.make_async_copy(ins[a], _shard_region(outs[a], fulls[a], axes[a], k_me), local_sems.at[a])
            lc.start()
            locals_.append(lc)
            for j, chip in enumerate(chips):
                cp = remote(a, j, src_half(a, c), _shard_region(outs[a], fulls[a], axes[a], k_me, c), (*chip, c))
                cp.start()
                started.append(cp)
        for a in range(n_arr):
            for j, chip in enumerate(chips):
                k_j = 2 * chip[0] + chip[1]
                reg = _shard_region(outs[a], fulls[a], axes[a], k_j, c)
                remote(a, j, reg, reg, (x, y, c)).wait_recv()
                fw = remote(a, 3 + j, reg, reg, sibling)
                fw.start()
                started.append(fw)
        for a in range(n_arr):
            for j, chip in enumerate(chips):
                k_j = 2 * chip[0] + chip[1]
                reg = _shard_region(outs[a], fulls[a], axes[a], k_j, 1 - c)
                remote(a, 3 + j, reg, reg, (x, y, c)).wait_recv()
        for cp in started:
            cp.wait_send()
        for lc in locals_:
            lc.wait()

    any_spec = pl.BlockSpec(memory_space=pl.ANY)
    return pl.pallas_call(
        body, name=name, out_shape=[jax.ShapeDtypeStruct(f, BF16) for f in fulls],
        in_specs=[any_spec] * n_arr, out_specs=[any_spec] * n_arr,
        scratch_shapes=[pltpu.SemaphoreType.DMA((6 * n_arr,)), pltpu.SemaphoreType.DMA((6 * n_arr,)),
                        pltpu.SemaphoreType.DMA((n_arr,))],
    )(*shards)


def _scatter_grads(grads, axes, name):
    n_arr = len(grads)
    outs_shape = []
    for g, ax in zip(grads, axes):
        shp = list(g.shape)
        shp[ax] //= 4
        outs_shape.append((4,) + tuple(shp[1:]))

    def body(*refs):
        ins, outs = refs[:n_arr], refs[n_arr:2 * n_arr]
        send_sems, recv_sems, local_sems = refs[2 * n_arr:]
        x, y, c, chips = _place()
        k_me = 2 * x + y
        copies, locals_ = [], []
        for a in range(n_arr):
            lc = pltpu.make_async_copy(_shard_region(ins[a], grads[a].shape, axes[a], k_me), outs[a].at[pl.ds(3, 1)],
                                       local_sems.at[a])
            lc.start()
            locals_.append(lc)
            for j, chip in enumerate(chips):
                k_j = 2 * chip[0] + chip[1]
                cp = pltpu.make_async_remote_copy(
                    src_ref=_shard_region(ins[a], grads[a].shape, axes[a], k_j), dst_ref=outs[a].at[pl.ds(j, 1)],
                    send_sem=send_sems.at[a * 3 + j], recv_sem=recv_sems.at[a * 3 + j],
                    device_id=(*chip, c), device_id_type=MESH)
                cp.start()
                copies.append(cp)
        for cp in copies:
            cp.wait()
        for lc in locals_:
            lc.wait()

    any_spec = pl.BlockSpec(memory_space=pl.ANY)
    return pl.pallas_call(
        body, name=name, out_shape=[jax.ShapeDtypeStruct(s, BF16) for s in outs_shape],
        in_specs=[any_spec] * n_arr, out_specs=[any_spec] * n_arr,
        scratch_shapes=[pltpu.SemaphoreType.DMA((3 * n_arr,)), pltpu.SemaphoreType.DMA((3 * n_arr,)),
                        pltpu.SemaphoreType.DMA((n_arr,))],
    )(*grads)


def _swap_with_sibling(arrs, name):
    n_arr = len(arrs)

    def body(*refs):
        ins, outs = refs[:n_arr], refs[n_arr:2 * n_arr]
        send_sems, recv_sems = refs[2 * n_arr:]
        x, y, c, _ = _place()
        copies = []
        for a in range(n_arr):
            cp = pltpu.make_async_remote_copy(
                src_ref=ins[a], dst_ref=outs[a], send_sem=send_sems.at[a], recv_sem=recv_sems.at[a],
                device_id=(x, y, 1 - c), device_id_type=MESH)
            cp.start()
            copies.append(cp)
        for cp in copies:
            cp.wait()

    any_spec = pl.BlockSpec(memory_space=pl.ANY)
    return pl.pallas_call(
        body, name=name, out_shape=[jax.ShapeDtypeStruct(a.shape, a.dtype) for a in arrs],
        in_specs=[any_spec] * n_arr, out_specs=[any_spec] * n_arr,
        scratch_shapes=[pltpu.SemaphoreType.DMA((n_arr,)), pltpu.SemaphoreType.DMA((n_arr,))],
    )(*arrs)


def _mm_tn(a, b, name, out_dtype=BF16):
    L, m = a.shape
    n = b.shape[1]
    bm, bn, bk = min(m, 1024), min(n, 1024), min(L, 512)
    nk = L // bk

    def body(a_ref, b_ref, o_ref, acc):
        k = pl.program_id(2)

        @pl.when(k == 0)
        def _():
            acc[...] = jnp.zeros_like(acc)

        acc[...] += _dot_tn(_bf(a_ref[...]), _bf(b_ref[...]))

        @pl.when(k == nk - 1)
        def _():
            o_ref[...] = acc[...].astype(out_dtype)

    return pl.pallas_call(
        body, name=name, grid=(m // bm, n // bn, nk),
        in_specs=[pl.BlockSpec((bk, bm), lambda i, j, k: (k, i)), pl.BlockSpec((bk, bn), lambda i, j, k: (k, j))],
        out_specs=pl.BlockSpec((bm, bn), lambda i, j, k: (i, j)),
        out_shape=jax.ShapeDtypeStruct((m, n), out_dtype),
        scratch_shapes=[pltpu.VMEM((bm, bn), F32)],
        compiler_params=_params(3),
    )(a, b)


def _mm_tn_blocks(a, b, wa, wb, name):
    L = a.shape[0]
    nb = a.shape[1] // wa
    bk = min(L, 512)
    nk = L // bk

    def body(a_ref, b_ref, o_ref):
        @pl.when(pl.program_id(1) == 0)
        def _():
            o_ref[...] = jnp.zeros_like(o_ref)

        o_ref[...] += _dot_tn(_bf(a_ref[...]), _bf(b_ref[...]))

    return pl.pallas_call(
        body, name=name, grid=(nb, nk),
        in_specs=[pl.BlockSpec((bk, wa), lambda j, k: (k, j)), pl.BlockSpec((bk, wb), lambda j, k: (k, j))],
        out_specs=pl.BlockSpec((None, wa, wb), lambda j, k: (j, 0, 0)),
        out_shape=jax.ShapeDtypeStruct((nb, wa, wb), F32),
        compiler_params=_params(2),
    )(a, b)


def _sum4(parts, name):
    _, r, c = parts.shape
    tr = min(r, 256)

    def body(p_ref, o_ref):
        p = p_ref[...].astype(F32)
        o_ref[...] = ((p[0] + p[1]) + p[2]) + p[3]

    return pl.pallas_call(
        body, name=name, grid=(r // tr,),
        in_specs=[pl.BlockSpec((4, tr, c), lambda i: (0, i, 0))],
        out_specs=pl.BlockSpec((tr, c), lambda i: (i, 0)),
        out_shape=jax.ShapeDtypeStruct((r, c), F32),
        compiler_params=_params(1),
    )(parts)


def _adamw(w, g_parts, m, v, name):
    r, c = w.shape
    tr = r
    for cand in (512, 256, 128, 64, 32, 16, 8):
        if r % cand == 0 and cand * c * 4 <= (2 << 20):
            tr = cand
            break
    n_g = len(g_parts)
    c1 = 1.0 / (1.0 - ADAM_B1 ** ADAM_STEP)
    c2 = 1.0 / (1.0 - ADAM_B2 ** ADAM_STEP)

    def body(*refs):
        w_ref, g_refs, m_ref, v_ref = refs[0], refs[1:1 + n_g], refs[1 + n_g], refs[2 + n_g]
        g_out, d_out, m_out, v_out = refs[3 + n_g:]
        g = g_refs[0][...]
        for gr in g_refs[1:]:
            g = g + gr[...]
        m_new = ADAM_B1 * m_ref[...] + (1.0 - ADAM_B1) * g
        v_new = ADAM_B2 * v_ref[...] + (1.0 - ADAM_B2) * (g * g)
        m_hat = m_new * c1
        v_hat = v_new * c2
        g_out[...] = g
        d_out[...] = -ADAM_LR * (m_hat / (jnp.sqrt(v_hat) + ADAM_EPS) + ADAM_WD * w_ref[...])
        m_out[...] = m_new
        v_out[...] = v_new

    spec = pl.BlockSpec((tr, c), lambda i: (i, 0))
    return pl.pallas_call(
        body, name=name, grid=(r // tr,),
        in_specs=[spec] * (3 + n_g), out_specs=[spec] * 4,
        out_shape=[jax.ShapeDtypeStruct((r, c), F32)] * 4,
        compiler_params=_params(1),
    )(w, *g_parts, m, v)


def _ada_fwd(c16, ada_w, ada_b_cols):
    cols = ada_w.shape[2]

    def body(c_ref, w_ref, b_ref, o_ref):
        cv = c_ref[...]
        ca = _bf(cv * jax.nn.sigmoid(cv))
        o_ref[...] = _dot(ca, _bf(w_ref[...])) + b_ref[...]

    return pl.pallas_call(
        body, name="ada_fwd", grid=(DEPTH,),
        in_specs=[_whole((16, D)), pl.BlockSpec((None, D, cols), lambda i: (i, 0, 0)),
                  pl.BlockSpec((None, 1, cols), lambda i: (i, 0, 0))],
        out_specs=pl.BlockSpec((None, 16, cols), lambda i: (i, 0, 0)),
        out_shape=jax.ShapeDtypeStruct((DEPTH, 16, cols), F32),
        compiler_params=_params(1),
    )(c16, ada_w, ada_b_cols)


def _ada_bwd(c16, dmod16):
    cols = dmod16.shape[2]

    def body(c_ref, d_ref, o_ref):
        cv = c_ref[...]
        ca = _bf(cv * jax.nn.sigmoid(cv))
        o_ref[...] = _dot_tn(ca, _bf(d_ref[...]))

    return pl.pallas_call(
        body, name="ada_bwd", grid=(DEPTH,),
        in_specs=[_whole((16, D)), pl.BlockSpec((None, 16, cols), lambda i: (i, 0, 0))],
        out_specs=pl.BlockSpec((None, D, cols), lambda i: (i, 0, 0)),
        out_shape=jax.ShapeDtypeStruct((DEPTH, D, cols), F32),
        compiler_params=_params(1),
    )(c16, dmod16)


def _mod_bwd(vs_mix, vs_ffn, pv):
    def body(m_ref, f_ref, pv_ref, o_ref):
        for i in range(DEPTH):
            vm, vf, p = m_ref[i], f_ref[i], pv_ref[i]
            o_ref[i] = jnp.concatenate([
                vm[2:3], vm[1:2] * p[R_N1:R_N1 + 1], vm[0:1],
                vf[2:3], vf[1:2] * p[R_N2:R_N2 + 1], vf[0:1],
                vm[1:2] * (1.0 + p[R_SC1:R_SC1 + 1]), vf[1:2] * (1.0 + p[R_SC2:R_SC2 + 1])], axis=0)

    return pl.pallas_call(body, name="mod_bwd", out_shape=jax.ShapeDtypeStruct((DEPTH, 8, D), F32))(vs_mix, vs_ffn, pv)


def _ffn_fwd(x1, pv, w1, w2, layer, tm):
    L = x1.shape[0]
    dff = w1.shape[2]

    def body(x1_ref, pv_ref, w1_ref, w2_ref, x2_ref, h2_ref, a_ref, f_ref):
        x1v, p = x1_ref[...], pv_ref[...]
        h2, _, _ = _norm_mod(x1v, p[R_N2:R_N2 + 1], p[R_SC2:R_SC2 + 1], p[R_SH2:R_SH2 + 1])
        hb = _bf(h2)
        h2_ref[...] = hb
        a = _dot(hb, w1_ref[...])
        a_ref[...] = a
        ra = jnp.maximum(a, 0.0)
        f = _dot(_bf(ra * ra), w2_ref[...])
        f_ref[...] = f
        x2_ref[...] = x1v + p[R_G2:R_G2 + 1] * f

    return pl.pallas_call(
        body, name=f"ffn_fwd{layer}", grid=(L // tm,),
        in_specs=[_rows(tm, D), pl.BlockSpec((None, 8, D), lambda i: (layer, 0, 0)), _layer_w(D, dff, layer), _layer_w(dff, D, layer)],
        out_specs=[_rows(tm, D), _rows(tm, D), _rows(tm, dff), _rows(tm, D)],
        out_shape=[jax.ShapeDtypeStruct((L, D), F32), jax.ShapeDtypeStruct((L, D), BF16),
                   jax.ShapeDtypeStruct((L, dff), F32), jax.ShapeDtypeStruct((L, D), F32)],
        compiler_params=_params(1, 56),
    )(x1, pv, w1, w2)


def _ffn_bwd(dx2, x1, a, f, pv, w1, w2, layer, tm):
    L = x1.shape[0]
    dff = w1.shape[2]

    def body(dx2_ref, x1_ref, a_ref, f_ref, pv_ref, w1_ref, w2_ref, dx1_ref, p_ref, da_ref, df_ref, vs_ref):
        @pl.when(pl.program_id(0) == 0)
        def _():
            vs_ref[...] = jnp.zeros_like(vs_ref)

        dx2v, p = dx2_ref[...], pv_ref[...]
        dfb = _bf(dx2v * p[R_G2:R_G2 + 1])
        df_ref[...] = dfb
        vs_ref[0:1, :] += _sum0(dx2v * f_ref[...])
        dp = _dot_nt(dfb, w2_ref[...])
        ra = jnp.maximum(a_ref[...], 0.0)
        p_ref[...] = _bf(ra * ra)
        dab = _bf(dp * (2.0 * ra))
        da_ref[...] = dab
        dh2 = _dot_nt(dab, w1_ref[...])
        _, xn, r = _norm_mod(x1_ref[...], p[R_N2:R_N2 + 1], p[R_SC2:R_SC2 + 1], p[R_SH2:R_SH2 + 1])
        dx1_ref[...] = dx2v + _norm_mod_bwd(dh2, xn, r, p[R_N2:R_N2 + 1], p[R_SC2:R_SC2 + 1])
        vs_ref[1:2, :] += _sum0(dh2 * xn)
        vs_ref[2:3, :] += _sum0(dh2)

    return pl.pallas_call(
        body, name=f"ffn_bwd{layer}", grid=(L // tm,),
        in_specs=[_rows(tm, D), _rows(tm, D), _rows(tm, dff), _rows(tm, D),
                  pl.BlockSpec((None, 8, D), lambda i: (layer, 0, 0)), _layer_w(D, dff, layer), _layer_w(dff, D, layer)],
        out_specs=[_rows(tm, D), _rows(tm, dff), _rows(tm, dff), _rows(tm, D), _whole((8, D))],
        out_shape=[jax.ShapeDtypeStruct((L, D), F32), jax.ShapeDtypeStruct((L, dff), BF16),
                   jax.ShapeDtypeStruct((L, dff), BF16), jax.ShapeDtypeStruct((L, D), BF16),
                   jax.ShapeDtypeStruct((8, D), F32)],
        compiler_params=_params(1, 56),
    )(dx2, x1, a, f, pv, w1, w2)


def _conv_fwd(x, pv, w_in, w_out, cw, layer, j, tm):
    L = x.shape[0]

    def body(x_ref, pv_ref, win_ref, wout_ref, cw_ref, x1_ref, h_ref, bcx_ref, conv_ref, q_ref, y_ref, carry):
        @pl.when(pl.program_id(0) == 0)
        def _():
            carry[...] = jnp.zeros_like(carry)

        xv, p, cwv = x_ref[...], pv_ref[...], cw_ref[...]
        h, _, _ = _norm_mod(xv, p[R_N1:R_N1 + 1], p[R_SC1:R_SC1 + 1], p[R_SH1:R_SH1 + 1])
        hb = _bf(h)
        h_ref[...] = hb
        bcx = _dot(hb, win_ref[...])
        bcx_ref[...] = bcx
        z = bcx[:, D:2 * D] * bcx[:, 2 * D:]
        prev8 = carry[...]
        conv = cwv[0:1] * _shift_down(z, prev8, 2) + cwv[1:2] * _shift_down(z, prev8, 1) + cwv[2:3] * z + cwv[3:4]
        conv_ref[...] = conv
        qb = _bf(bcx[:, :D] * conv)
        q_ref[...] = qb
        y = _dot(qb, wout_ref[...])
        y_ref[...] = y
        x1_ref[...] = xv + p[R_G1:R_G1 + 1] * y
        carry[...] = z[tm - 8:tm]

    return pl.pallas_call(
        body, name=f"conv_fwd{layer}", grid=(L // tm,),
        in_specs=[_rows(tm, D), pl.BlockSpec((None, 8, D), lambda i: (layer, 0, 0)), _layer_w(D, 3 * D, j), _layer_w(D, D, j),
                  pl.BlockSpec((None, 8, D), lambda i: (j, 0, 0))],
        out_specs=[_rows(tm, D), _rows(tm, D), _rows(tm, 3 * D), _rows(tm, D), _rows(tm, D), _rows(tm, D)],
        out_shape=[jax.ShapeDtypeStruct((L, D), F32), jax.ShapeDtypeStruct((L, D), BF16), jax.ShapeDtypeStruct((L, 3 * D), F32),
                   jax.ShapeDtypeStruct((L, D), F32), jax.ShapeDtypeStruct((L, D), BF16), jax.ShapeDtypeStruct((L, D), F32)],
        scratch_shapes=[pltpu.VMEM((8, D), F32)],
        compiler_params=_params(1, 56),
    )(x, pv, w_in, w_out, cw)


def _conv_bwd(dx1, x, y, bcx, conv, pv, w_in, w_out, cw, layer, j, tm):
    L = x.shape[0]
    nt = L // tm

    def body(dx1_ref, x_ref, y_ref, bcx_ref, conv_ref, halo_ref, pv_ref, win_ref, wout_ref, cw_ref,
             dx_ref, dbcx_ref, dy_ref, vs_ref, carry):
        gi = pl.program_id(0)
        tile = nt - 1 - gi

        @pl.when(gi == 0)
        def _():
            vs_ref[...] = jnp.zeros_like(vs_ref)
            carry[...] = jnp.zeros_like(carry)

        dx1v, p, cwv = dx1_ref[...], pv_ref[...], cw_ref[...]
        dyb = _bf(dx1v * p[R_G1:R_G1 + 1])
        dy_ref[...] = dyb
        vs_ref[0:1, :] += _sum0(dx1v * y_ref[...])
        dq = _dot_nt(dyb, wout_ref[...])
        bcx = bcx_ref[...]
        b, cg, xh = bcx[:, :D], bcx[:, D:2 * D], bcx[:, 2 * D:]
        db = dq * conv_ref[...]
        dc = dq * b
        z = cg * xh
        halo = halo_ref[...]
        zprev = jnp.where(tile > 0, halo[:, D:2 * D] * halo[:, 2 * D:], 0.0)
        vs_ref[3:4, :] += _sum0(dc * _shift_down(z, zprev, 2))
        vs_ref[4:5, :] += _sum0(dc * _shift_down(z, zprev, 1))
        vs_ref[5:6, :] += _sum0(dc * z)
        vs_ref[6:7, :] += _sum0(dc)
        next8 = carry[...]
        dz = cwv[2:3] * dc + cwv[1:2] * _shift_up(dc, next8, 1) + cwv[0:1] * _shift_up(dc, next8, 2)
        dbb, dcgb, dxhb = _bf(db), _bf(dz * xh), _bf(dz * cg)
        dbcx_ref[:, 0:D] = dbb
        dbcx_ref[:, D:2 * D] = dcgb
        dbcx_ref[:, 2 * D:3 * D] = dxhb
        dh = (_dot_nt(dbb, win_ref[:, 0:D]) + _dot_nt(dcgb, win_ref[:, D:2 * D])) + _dot_nt(dxhb, win_ref[:, 2 * D:3 * D])
        _, xn, r = _norm_mod(x_ref[...], p[R_N1:R_N1 + 1], p[R_SC1:R_SC1 + 1], p[R_SH1:R_SH1 + 1])
        dx_ref[...] = dx1v + _norm_mod_bwd(dh, xn, r, p[R_N1:R_N1 + 1], p[R_SC1:R_SC1 + 1])
        vs_ref[1:2, :] += _sum0(dh * xn)
        vs_ref[2:3, :] += _sum0(dh)
        carry[...] = dc[0:8]

    halo_spec = pl.BlockSpec((8, 3 * D), lambda i: (jnp.maximum((nt - 1 - i) * (tm // 8) - 1, 0), 0))
    return pl.pallas_call(
        body, name=f"conv_bwd{layer}", grid=(nt,),
        in_specs=[_rows(tm, D, nt), _rows(tm, D, nt), _rows(tm, D, nt), _rows(tm, 3 * D, nt), _rows(tm, D, nt), halo_spec,
                  pl.BlockSpec((None, 8, D), lambda i: (layer, 0, 0)), _layer_w(D, 3 * D, j), _layer_w(D, D, j),
                  pl.BlockSpec((None, 8, D), lambda i: (j, 0, 0))],
        out_specs=[_rows(tm, D, nt), _rows(tm, 3 * D, nt), _rows(tm, D, nt), _whole((8, D))],
        out_shape=[jax.ShapeDtypeStruct((L, D), F32), jax.ShapeDtypeStruct((L, 3 * D), BF16),
                   jax.ShapeDtypeStruct((L, D), BF16), jax.ShapeDtypeStruct((8, D), F32)],
        scratch_shapes=[pltpu.VMEM((8, D), F32)],
        compiler_params=_params(1, 56),
    )(dx1, x, y, bcx, conv, bcx, pv, w_in, w_out, cw)


def _s5_discretize(a_re, a_im, log_dt, bt_re, bt_im):
    dt = jnp.exp(log_dt)
    mag = jnp.exp(a_re * dt)
    abar_re = mag * jnp.cos(a_im * dt)
    abar_im = mag * jnp.sin(a_im * dt)
    den = a_re * a_re + a_im * a_im
    nr = abar_re - 1.0
    ni = abar_im
    f_re = (nr * a_re + ni * a_im) / den
    f_im = (ni * a_re - nr * a_im) / den
    bbar_re = f_re * bt_re - f_im * bt_im
    bbar_im = f_re * bt_im + f_im * bt_re
    return abar_re, abar_im, bbar_re, bbar_im


def _s5_params_fwd(a_re, a_im, log_dt, bt_re, bt_im):
    def body(ar, ai, ld, br, bi, o_ar, o_ai, o_br, o_bi):
        r = _s5_discretize(ar[...], ai[...], ld[...], br[...], bi[...])
        o_ar[...], o_ai[...], o_br[...], o_bi[...] = r

    gp = jax.ShapeDtypeStruct((S5_G, S5_P), F32)
    hgp = jax.ShapeDtypeStruct((S5_H, S5_G, S5_P), F32)
    return pl.pallas_call(body, name="s5_params_fwd", out_shape=[gp, gp, hgp, hgp])(a_re, a_im, log_dt, bt_re, bt_im)


def _s5_params_bwd(a_re, a_im, log_dt, bt_re, bt_im, d_ar, d_ai, d_br, d_bi):
    def body(ar, ai, ld, br, bi, gar, gai, gbr, gbi, o_ar, o_ai, o_ld, o_br, o_bi):
        _, vjp = jax.vjp(_s5_discretize, ar[...], ai[...], ld[...], br[...], bi[...])
        r = vjp((gar[...], gai[...], gbr[...], gbi[...]))
        o_ar[...], o_ai[...], o_ld[...], o_br[...], o_bi[...] = r

    gp = jax.ShapeDtypeStruct((S5_G, S5_P), F32)
    hgp = jax.ShapeDtypeStruct((S5_H, S5_G, S5_P), F32)
    return pl.pallas_call(body, name="s5_params_bwd", out_shape=[gp, gp, jax.ShapeDtypeStruct((S5_G, 1), F32), hgp, hgp])(
        a_re, a_im, log_dt, bt_re, bt_im, d_ar, d_ai, d_br, d_bi)


def _s5_in_fwd(x, pv, w_in, b_re, b_im, layer, tm):
    L = x.shape[0]

    def body(x_ref, pv_ref, win_ref, bre_ref, bim_ref, h_ref, u_ref, ore_ref, oim_ref):
        p = pv_ref[...]
        h, _, _ = _norm_mod(x_ref[...], p[R_N1:R_N1 + 1], p[R_SC1:R_SC1 + 1], p[R_SH1:R_SH1 + 1])
        hb = _bf(h)
        h_ref[...] = hb
        u = _dot(hb, win_ref[...])
        u_ref[...] = u
        ub = _bf(u)
        for k in range(S5_NB):
            uk = ub[:, k * S5_BH:(k + 1) * S5_BH]
            ore_ref[:, k * S5_BP:(k + 1) * S5_BP] = _dot(uk, bre_ref[k])
            oim_ref[:, k * S5_BP:(k + 1) * S5_BP] = _dot(uk, bim_ref[k])

    return pl.pallas_call(
        body, name="s5_in_fwd", grid=(L // tm,),
        in_specs=[_rows(tm, D), pl.BlockSpec((None, 8, D), lambda i: (layer, 0, 0)), _layer_w(D, D, 0),
                  _const_w((S5_NB, S5_BH, S5_BP)), _const_w((S5_NB, S5_BH, S5_BP))],
        out_specs=[_rows(tm, D), _rows(tm, D), _rows(tm, NSTATE), _rows(tm, NSTATE)],
        out_shape=[jax.ShapeDtypeStruct((L, D), BF16), jax.ShapeDtypeStruct((L, D), F32),
                   jax.ShapeDtypeStruct((L, NSTATE), F32), jax.ShapeDtypeStruct((L, NSTATE), F32)],
        compiler_params=_params(1, 56),
    )(x, pv, w_in, b_re, b_im)


def _s5_scan_fwd(bu_re, bu_im, ar, ai, tr):
    L = bu_re.shape[0]
    nl = 1024

    def body(bre_ref, bim_ref, ar_ref, ai_ref, sre_ref, sim_ref, st_re, st_im):
        @pl.when(pl.program_id(1) == 0)
        def _():
            st_re[...] = jnp.zeros_like(st_re)
            st_im[...] = jnp.zeros_like(st_im)

        a_r, a_i = ar_ref[...], ai_ref[...]

        def step(t, carry):
            s_r, s_i = carry
            n_r = a_r * s_r - a_i * s_i + bre_ref[pl.ds(t, 1), :]
            n_i = a_r * s_i + a_i * s_r + bim_ref[pl.ds(t, 1), :]
            sre_ref[pl.ds(t, 1), :] = n_r
            sim_ref[pl.ds(t, 1), :] = n_i
            return n_r, n_i

        s_r, s_i = lax.fori_loop(0, tr, step, (st_re[...], st_im[...]), unroll=8)
        st_re[...] = s_r
        st_im[...] = s_i

    blk = pl.BlockSpec((tr, nl), lambda j, i: (i, j))
    vec = pl.BlockSpec((1, nl), lambda j, i: (0, j))
    return pl.pallas_call(
        body, name="s5_scan_fwd", grid=(NSTATE // nl, L // tr),
        in_specs=[blk, blk, vec, vec], out_specs=[blk, blk],
        out_shape=[jax.ShapeDtypeStruct((L, NSTATE), F32)] * 2,
        scratch_shapes=[pltpu.VMEM((1, nl), F32), pltpu.VMEM((1, nl), F32)],
        compiler_params=_params(2),
    )(bu_re, bu_im, ar, ai)


def _s5_out_fwd(x, u, s_re, s_im, pv, c_re, c_im, dvec, glu_w, glu_b, w_out, layer, tm):
    L = x.shape[0]

    def body(x_ref, u_ref, sre_ref, sim_ref, pv_ref, cre_ref, cim_ref, d_ref, gw_ref, gb_ref, wout_ref,
             x1_ref, y1_ref, zg_ref, y3_ref, y_ref):
        p = pv_ref[...]
        srb, sib = _bf(sre_ref[...]), _bf(sim_ref[...])
        parts = []
        for k in range(S5_NB):
            sl = slice(k * S5_BP, (k + 1) * S5_BP)
            parts.append(_dot(srb[:, sl], cre_ref[k]) - _dot(sib[:, sl], cim_ref[k]))
        y1 = jnp.concatenate(parts, axis=1) + d_ref[...] * u_ref[...]
        y1_ref[...] = y1
        y2 = jax.nn.gelu(y1)
        zg = _dot(_bf(y2), gw_ref[...]) + gb_ref[...]
        zg_ref[...] = zg
        y3b = _bf(y2 * jax.nn.sigmoid(zg))
        y3_ref[...] = y3b
        y = _dot(y3b, wout_ref[...])
        y_ref[...] = y
        x1_ref[...] = x_ref[...] + p[R_G1:R_G1 + 1] * y

    return pl.pallas_call(
        body, name="s5_out_fwd", grid=(L // tm,),
        in_specs=[_rows(tm, D), _rows(tm, D), _rows(tm, NSTATE), _rows(tm, NSTATE),
                  pl.BlockSpec((None, 8, D), lambda i: (layer, 0, 0)),
                  _const_w((S5_NB, S5_BP, S5_BH)), _const_w((S5_NB, S5_BP, S5_BH)), _whole((1, D)),
                  _layer_w(D, D, 0), _whole((1, D)), _layer_w(D, D, 0)],
        out_specs=[_rows(tm, D)] * 5,
        out_shape=[jax.ShapeDtypeStruct((L, D), F32), jax.ShapeDtypeStruct((L, D), F32), jax.ShapeDtypeStruct((L, D), F32),
                   jax.ShapeDtypeStruct((L, D), BF16), jax.ShapeDtypeStruct((L, D), F32)],
        compiler_params=_params(1, 56),
    )(x, u, s_re, s_im, pv, c_re, c_im, dvec, glu_w, glu_b, w_out)


def _s5_out_bwd(dx1, y, y1, zg, u, pv, c_re, c_im, dvec, glu_w, w_out, layer, tm):
    L = dx1.shape[0]

    def body(dx1_ref, y_ref, y1_ref, zg_ref, u_ref, pv_ref, cre_ref, cim_ref, d_ref, gw_ref, wout_ref,
             dy_ref, y2_ref, dzg_ref, dy1_ref, dus_ref, gre_ref, gim_ref, vs_ref):
        @pl.when(pl.program_id(0) == 0)
        def _():
            vs_ref[...] = jnp.zeros_like(vs_ref)

        dx1v, p = dx1_ref[...], pv_ref[...]
        dyb = _bf(dx1v * p[R_G1:R_G1 + 1])
        dy_ref[...] = dyb
        vs_ref[0:1, :] += _sum0(dx1v * y_ref[...])
        dy3 = _dot_nt(dyb, wout_ref[...])
        y2, gelu_vjp = jax.vjp(jax.nn.gelu, y1_ref[...])
        y2_ref[...] = _bf(y2)
        gate = jax.nn.sigmoid(zg_ref[...])
        dzg = dy3 * y2 * gate * (1.0 - gate)
        dzgb = _bf(dzg)
        dzg_ref[...] = dzgb
        vs_ref[1:2, :] += _sum0(dzg)
        dy2 = dy3 * gate + _dot_nt(dzgb, gw_ref[...])
        dy1 = gelu_vjp(dy2)[0]
        vs_ref[2:3, :] += _sum0(dy1 * u_ref[...])
        dus_ref[...] = dy1 * d_ref[...]
        dy1b = _bf(dy1)
        dy1_ref[...] = dy1b
        for k in range(S5_NB):
            dk = dy1b[:, k * S5_BH:(k + 1) * S5_BH]
            gre_ref[:, k * S5_BP:(k + 1) * S5_BP] = _dot_nt(dk, cre_ref[k])
            gim_ref[:, k * S5_BP:(k + 1) * S5_BP] = -_dot_nt(dk, cim_ref[k])

    return pl.pallas_call(
        body, name="s5_out_bwd", grid=(L // tm,),
        in_specs=[_rows(tm, D)] * 5 + [pl.BlockSpec((None, 8, D), lambda i: (layer, 0, 0)),
                  _const_w((S5_NB, S5_BP, S5_BH)), _const_w((S5_NB, S5_BP, S5_BH)), _whole((1, D)),
                  _layer_w(D, D, 0), _layer_w(D, D, 0)],
        out_specs=[_rows(tm, D)] * 5 + [_rows(tm, NSTATE), _rows(tm, NSTATE), _whole((8, D))],
        out_shape=[jax.ShapeDtypeStruct((L, D), BF16)] * 4 + [jax.ShapeDtypeStruct((L, D), F32),
                   jax.ShapeDtypeStruct((L, NSTATE), F32), jax.ShapeDtypeStruct((L, NSTATE), F32),
                   jax.ShapeDtypeStruct((8, D), F32)],
        compiler_params=_params(1, 56),
    )(dx1, y, y1, zg, u, pv, c_re, c_im, dvec, glu_w, w_out)


def _s5_scan_bwd(g_re, g_im, s_re, s_im, ar, ai, tr):
    L = g_re.shape[0]
    nl = 1024
    nt = L // tr

    def body(gre_ref, gim_ref, sre_ref, sim_ref, hre_ref, him_ref, ar_ref, ai_ref, lre_ref, lim_ref, da_ref, st_re, st_im):
        gi = pl.program_id(1)
        tile = nt - 1 - gi

        @pl.when(gi == 0)
        def _():
            st_re[...] = jnp.zeros_like(st_re)
            st_im[...] = jnp.zeros_like(st_im)
            da_ref[...] = jnp.zeros_like(da_ref)

        a_r, a_i = ar_ref[...], ai_ref[...]

        def step(k, carry):
            l_r, l_i = carry
            t = tr - 1 - k
            n_r = gre_ref[pl.ds(t, 1), :] + a_r * l_r + a_i * l_i
            n_i = gim_ref[pl.ds(t, 1), :] - a_i * l_r + a_r * l_i
            lre_ref[pl.ds(t, 1), :] = n_r
            lim_ref[pl.ds(t, 1), :] = n_i
            return n_r, n_i

        l_r, l_i = lax.fori_loop(0, tr, step, (st_re[...], st_im[...]), unroll=8)
        st_re[...] = l_r
        st_im[...] = l_i
        lam_r, lam_i = lre_ref[...], lim_ref[...]
        p_r = jnp.where(tile > 0, hre_ref[...], 0.0)
        p_i = jnp.where(tile > 0, him_ref[...], 0.0)
        sp_r = _shift_down(sre_ref[...], p_r, 1)
        sp_i = _shift_down(sim_ref[...], p_i, 1)
        da_ref[0:1, :] += _sum0(lam_r * sp_r + lam_i * sp_i)
        da_ref[1:2, :] += _sum0(lam_i * sp_r - lam_r * sp_i)

    blk = pl.BlockSpec((tr, nl), lambda j, i: (nt - 1 - i, j))
    halo = pl.BlockSpec((8, nl), lambda j, i: (jnp.maximum((nt - 1 - i) * (tr // 8) - 1, 0), j))
    vec = pl.BlockSpec((1, nl), lambda j, i: (0, j))
    return pl.pallas_call(
        body, name="s5_scan_bwd", grid=(NSTATE // nl, nt),
        in_specs=[blk, blk, blk, blk, halo, halo, vec, vec],
        out_specs=[blk, blk, pl.BlockSpec((8, nl), lambda j, i: (0, j))],
        out_shape=[jax.ShapeDtypeStruct((L, NSTATE), F32)] * 2 + [jax.ShapeDtypeStruct((8, NSTATE), F32)],
        scratch_shapes=[pltpu.VMEM((1, nl), F32), pltpu.VMEM((1, nl), F32)],
        compiler_params=_params(2),
    )(g_re, g_im, s_re, s_im, s_re, s_im, ar, ai)


def _s5_in_bwd(dx1, lam_re, lam_im, du_skip, x, pv, b_re, b_im, w_in, layer, tm):
    L = x.shape[0]

    def body(dx1_ref, lre_ref, lim_ref, dus_ref, x_ref, pv_ref, bre_ref, bim_ref, win_ref, dx_ref, du_ref, vs_ref):
        @pl.when(pl.program_id(0) == 0)
        def _():
            vs_ref[...] = jnp.zeros_like(vs_ref)

        p = pv_ref[...]
        lrb, lib = _bf(lre_ref[...]), _bf(lim_ref[...])
        parts = []
        for k in range(S5_NB):
            sl = slice(k * S5_BP, (k + 1) * S5_BP)
            parts.append(_dot_nt(lrb[:, sl], bre_ref[k]) + _dot_nt(lib[:, sl], bim_ref[k]))
        dub = _bf(jnp.concatenate(parts, axis=1) + dus_ref[...])
        du_ref[...] = dub
        dh = _dot_nt(dub, win_ref[...])
        _, xn, r = _norm_mod(x_ref[...], p[R_N1:R_N1 + 1], p[R_SC1:R_SC1 + 1], p[R_SH1:R_SH1 + 1])
        dx_ref[...] = dx1_ref[...] + _norm_mod_bwd(dh, xn, r, p[R_N1:R_N1 + 1], p[R_SC1:R_SC1 + 1])
        vs_ref[1:2, :] += _sum0(dh * xn)
        vs_ref[2:3, :] += _sum0(dh)

    return pl.pallas_call(
        body, name="s5_in_bwd", grid=(L // tm,),
        in_specs=[_rows(tm, D), _rows(tm, NSTATE), _rows(tm, NSTATE), _rows(tm, D), _rows(tm, D),
                  pl.BlockSpec((None, 8, D), lambda i: (layer, 0, 0)),
                  _const_w((S5_NB, S5_BH, S5_BP)), _const_w((S5_NB, S5_BH, S5_BP)), _layer_w(D, D, 0)],
        out_specs=[_rows(tm, D), _rows(tm, D), _whole((8, D))],
        out_shape=[jax.ShapeDtypeStruct((L, D), F32), jax.ShapeDtypeStruct((L, D), BF16), jax.ShapeDtypeStruct((8, D), F32)],
        compiler_params=_params(1, 56),
    )(dx1, lam_re, lam_im, du_skip, x, pv, b_re, b_im, w_in)


def _blockdiag_b(bt):
    b = bt.reshape(S5_H, S5_NB, 16, S5_P).transpose(1, 2, 0, 3)
    eye = jnp.eye(16, dtype=bt.dtype)
    return (b[:, :, :, None, :] * eye[None, :, None, :, None]).reshape(S5_NB, S5_BH, S5_BP)


def _unblock_b(d):
    d = jnp.einsum("bghgp->bghp", d.reshape(S5_NB, 16, S5_H, 16, S5_P))
    return d.transpose(2, 0, 1, 3).reshape(S5_H, S5_G, S5_P)


def _blockdiag_c(cm):
    c4 = cm.reshape(S5_NB, 16, S5_H, S5_P)
    eye = jnp.eye(16, dtype=cm.dtype)
    out = c4.transpose(0, 1, 3, 2)[:, :, :, None, :] * eye[None, :, None, :, None]
    return out.reshape(S5_NB, S5_BP, S5_BH)


def _unblock_c(d):
    d = jnp.einsum("bgpgh->bghp", d.reshape(S5_NB, 16, S5_P, 16, S5_H))
    return d.reshape(S5_G, S5_H, S5_P)


def _tril_mask():
    return lax.broadcasted_iota(jnp.int32, (SG_CHUNK, SG_CHUNK), 0) >= lax.broadcasted_iota(jnp.int32, (SG_CHUNK, SG_CHUNK), 1)


def _sg_fwd(x, pv, w_in, w_s, b_t, vg, w_out, layer, tm):
    L = x.shape[0]
    nc = tm // SG_CHUNK

    def body(x_ref, pv_ref, win_ref, ws_ref, bt_ref, vg_ref, wout_ref, x1_ref, h_ref, uv_ref, vm_ref, q_ref, y_ref):
        xv, p = x_ref[...], pv_ref[...]
        h, _, _ = _norm_mod(xv, p[R_N1:R_N1 + 1], p[R_SC1:R_SC1 + 1], p[R_SH1:R_SH1 + 1])
        hb = _bf(h)
        h_ref[...] = hb
        uv = _dot(hb, win_ref[...])
        uv_ref[...] = uv
        v = uv[:, D:]
        rv = lax.rsqrt(jnp.mean(v * v, axis=-1, keepdims=True) + EPS)
        vnb = _bf((v * rv) * vg_ref[...])
        mask = _tril_mask()
        bt = bt_ref[...]
        for hd in range(SG_HEADS):
            wm = _bf(jnp.where(mask, ws_ref[hd], 0.0))
            cs = slice(hd * SG_CHUNK, (hd + 1) * SG_CHUNK)
            for ck in range(nc):
                rs = slice(ck * SG_CHUNK, (ck + 1) * SG_CHUNK)
                vm_ref[rs, cs] = _dot(wm, vnb[rs, cs]) + bt[:, hd:hd + 1]
        qb = _bf(uv[:, :D] * vm_ref[...])
        q_ref[...] = qb
        y = _dot(qb, wout_ref[...])
        y_ref[...] = y
        x1_ref[...] = xv + p[R_G1:R_G1 + 1] * y

    return pl.pallas_call(
        body, name="sg_fwd", grid=(L // tm,),
        in_specs=[_rows(tm, D), pl.BlockSpec((None, 8, D), lambda i: (layer, 0, 0)), _layer_w(D, 2 * D, 0),
                  _whole((SG_HEADS, SG_CHUNK, SG_CHUNK)), _whole((SG_CHUNK, SG_HEADS)), _whole((1, D)), _layer_w(D, D, 0)],
        out_specs=[_rows(tm, D), _rows(tm, D), _rows(tm, 2 * D), _rows(tm, D), _rows(tm, D), _rows(tm, D)],
        out_shape=[jax.ShapeDtypeStruct((L, D), F32), jax.ShapeDtypeStruct((L, D), BF16), jax.ShapeDtypeStruct((L, 2 * D), F32),
                   jax.ShapeDtypeStruct((L, D), F32), jax.ShapeDtypeStruct((L, D), BF16), jax.ShapeDtypeStruct((L, D), F32)],
        compiler_params=_params(1, 56),
    )(x, pv, w_in, w_s, b_t, vg, w_out)


def _sg_bwd(dx1, x, y, uv, vm, pv, w_in, w_s, vg, w_out, layer, tm):
    L = x.shape[0]
    nc = tm // SG_CHUNK

    def body(dx1_ref, x_ref, y_ref, uv_ref, vm_ref, pv_ref, win_ref, ws_ref, vg_ref, wout_ref,
             dx_ref, duv_ref, dy_ref, vs_ref, dws_ref, dbt_ref, dvn_scr):
        @pl.when(pl.program_id(0) == 0)
        def _():
            vs_ref[...] = jnp.zeros_like(vs_ref)
            dws_ref[...] = jnp.zeros_like(dws_ref)
            dbt_ref[...] = jnp.zeros_like(dbt_ref)

        dx1v, p = dx1_ref[...], pv_ref[...]
        dyb = _bf(dx1v * p[R_G1:R_G1 + 1])
        dy_ref[...] = dyb
        vs_ref[0:1, :] += _sum0(dx1v * y_ref[...])
        dq = _dot_nt(dyb, wout_ref[...])
        uv = uv_ref[...]
        u, v = uv[:, :D], uv[:, D:]
        dub = _bf(dq * vm_ref[...])
        dvm = dq * u
        dvmb = _bf(dvm)
        rv = lax.rsqrt(jnp.mean(v * v, axis=-1, keepdims=True) + EPS)
        vh = v * rv
        vgv = vg_ref[...]
        vnb = _bf(vh * vgv)
        mask = _tril_mask()
        for hd in range(SG_HEADS):
            wm = _bf(jnp.where(mask, ws_ref[hd], 0.0))
            cs = slice(hd * SG_CHUNK, (hd + 1) * SG_CHUNK)
            dws = jnp.zeros((SG_CHUNK, SG_CHUNK), F32)
            dbs = jnp.zeros((SG_CHUNK, 1), F32)
            for ck in range(nc):
                rs = slice(ck * SG_CHUNK, (ck + 1) * SG_CHUNK)
                dvn_scr[rs, cs] = _dot_tn(wm, dvmb[rs, cs])
                dws = dws + _dot_nt(dvmb[rs, cs], vnb[rs, cs])
                dbs = dbs + jnp.sum(dvm[rs, cs], axis=1, keepdims=True)
            dws_ref[hd] += jnp.where(mask, dws, 0.0)
            dbt_ref[:, hd:hd + 1] += dbs
        dvn = dvn_scr[...]
        vs_ref[3:4, :] += _sum0(dvn * vh)
        dvnn = dvn * vgv
        dvb = _bf(rv * (dvnn - vh * jnp.mean(dvnn * vh, axis=-1, keepdims=True)))
        duv_ref[:, 0:D] = dub
        duv_ref[:, D:2 * D] = dvb
        dh = _dot_nt(dub, win_ref[:, 0:D]) + _dot_nt(dvb, win_ref[:, D:2 * D])
        _, xn, r = _norm_mod(x_ref[...], p[R_N1:R_N1 + 1], p[R_SC1:R_SC1 + 1], p[R_SH1:R_SH1 + 1])
        dx_ref[...] = dx1v + _norm_mod_bwd(dh, xn, r, p[R_N1:R_N1 + 1], p[R_SC1:R_SC1 + 1])
        vs_ref[1:2, :] += _sum0(dh * xn)
        vs_ref[2:3, :] += _sum0(dh)

    return pl.pallas_call(
        body, name="sg_bwd", grid=(L // tm,),
        in_specs=[_rows(tm, D), _rows(tm, D), _rows(tm, D), _rows(tm, 2 * D), _rows(tm, D),
                  pl.BlockSpec((None, 8, D), lambda i: (layer, 0, 0)), _layer_w(D, 2 * D, 0),
                  _whole((SG_HEADS, SG_CHUNK, SG_CHUNK)), _whole((1, D)), _layer_w(D, D, 0)],
        out_specs=[_rows(tm, D), _rows(tm, 2 * D), _rows(tm, D), _whole((8, D)),
                   _whole((SG_HEADS, SG_CHUNK, SG_CHUNK)), _whole((SG_CHUNK, SG_HEADS))],
        out_shape=[jax.ShapeDtypeStruct((L, D), F32), jax.ShapeDtypeStruct((L, 2 * D), BF16), jax.ShapeDtypeStruct((L, D), BF16),
                   jax.ShapeDtypeStruct((8, D), F32), jax.ShapeDtypeStruct((SG_HEADS, SG_CHUNK, SG_CHUNK), F32),
                   jax.ShapeDtypeStruct((SG_CHUNK, SG_HEADS), F32)],
        scratch_shapes=[pltpu.VMEM((tm, D), F32)],
        compiler_params=_params(1, 56),
    )(dx1, x, y, uv, vm, pv, w_in, w_s, vg, w_out)


def _final(x, target, fg, tm):
    L = x.shape[0]

    def body(x_ref, t_ref, g_ref, dx_ref, vs_ref):
        @pl.when(pl.program_id(0) == 0)
        def _():
            vs_ref[...] = jnp.zeros_like(vs_ref)

        xv, g = x_ref[...], g_ref[...]
        r = lax.rsqrt(jnp.mean(xv * xv, axis=-1, keepdims=True) + EPS)
        xn = xv * r
        e = xn * g - t_ref[...]
        vs_ref[0:1, :] += jnp.sum(e * e)
        dout = e * (1.0 / D)
        vs_ref[1:2, :] += _sum0(dout * xn)
        dxn = dout * g
        dx_ref[...] = r * (dxn - xn * jnp.mean(dxn * xn, axis=-1, keepdims=True))

    return pl.pallas_call(
        body, name="final_loss", grid=(L // tm,),
        in_specs=[_rows(tm, D), _rows(tm, D), _whole((1, D))],
        out_specs=[_rows(tm, D), _whole((8, D))],
        out_shape=[jax.ShapeDtypeStruct((L, D), F32), jax.ShapeDtypeStruct((8, D), F32)],
        compiler_params=_params(1),
    )(x, target, fg)


def _pack(arrs):
    flat = jnp.concatenate([a.reshape(-1).astype(F32) for a in arrs])
    rows = -(-flat.shape[0] // LANES)
    rows = -(-rows // 8) * 8
    return jnp.pad(flat, (0, rows * LANES - flat.shape[0])).reshape(rows, LANES)


def _unpack(buf, shapes, lead=()):
    flat = buf.reshape(lead + (-1,))
    out, off = [], 0
    for s in shapes:
        n = 1
        for d in s:
            n *= d
        out.append(flat[..., off:off + n].reshape(lead + tuple(s)))
        off += n
    return out


BIG = ("ff_w1", "ff_w2", "conv_w_in", "conv_w_out", "ssm_w_in", "ssm_glu_w", "ssm_w_out", "sg_w_in", "sg_w_out")
BIG_AXIS = {"ff_w1": 2, "ff_w2": 1, "conv_w_in": 2, "conv_w_out": 1, "ssm_w_in": 1, "ssm_glu_w": 1, "ssm_w_out": 1,
            "sg_w_in": 2, "sg_w_out": 1}
SMALL = ("ada_b", "norm1_g", "norm2_g", "final_g", "conv_w", "conv_b", "ssm_a_re", "ssm_a_im", "ssm_log_dt", "ssm_b_re",
         "ssm_b_im", "ssm_c_re", "ssm_c_im", "ssm_d", "ssm_glu_b", "sg_v_g", "sg_w_s", "sg_b_s")
WEIGHTS = ("ada_w", "ada_b", "norm1_g", "norm2_g", "ff_w1", "ff_w2", "final_g", "conv_w_in", "conv_w", "conv_b", "conv_w_out",
           "ssm_w_in", "ssm_a_re", "ssm_a_im", "ssm_log_dt", "ssm_b_re", "ssm_b_im", "ssm_c_re", "ssm_c_im", "ssm_d",
           "ssm_glu_w", "ssm_glu_b", "ssm_w_out", "sg_w_in", "sg_v_g", "sg_w_s", "sg_b_s", "sg_w_out")


def kernel(x, c, ada_w, ada_b, norm1_g, norm2_g, ff_w1, ff_w2, final_g, conv_w_in, conv_w, conv_b, conv_w_out, ssm_w_in, ssm_a_re, ssm_a_im, ssm_log_dt, ssm_b_re, ssm_b_im, ssm_c_re, ssm_c_im, ssm_d, ssm_glu_w, ssm_glu_b, ssm_w_out, sg_w_in, sg_v_g, sg_w_s, sg_b_s, sg_w_out, loss_target, m_ada_w, m_ada_b, m_norm1_g, m_norm2_g, m_ff_w1, m_ff_w2, m_final_g, m_conv_w_in, m_conv_w, m_conv_b, m_conv_w_out, m_ssm_w_in, m_ssm_a_re, m_ssm_a_im, m_ssm_log_dt, m_ssm_b_re, m_ssm_b_im, m_ssm_c_re, m_ssm_c_im, m_ssm_d, m_ssm_glu_w, m_ssm_glu_b, m_ssm_w_out, m_sg_w_in, m_sg_v_g, m_sg_w_s, m_sg_b_s, m_sg_w_out, v_ada_w, v_ada_b, v_norm1_g, v_norm2_g, v_ff_w1, v_ff_w2, v_final_g, v_conv_w_in, v_conv_w, v_conv_b, v_conv_w_out, v_ssm_w_in, v_ssm_a_re, v_ssm_a_im, v_ssm_log_dt, v_ssm_b_re, v_ssm_b_im, v_ssm_c_re, v_ssm_c_im, v_ssm_d, v_ssm_glu_w, v_ssm_glu_b, v_ssm_w_out, v_sg_w_in, v_sg_v_g, v_sg_w_s, v_sg_b_s, v_sg_w_out):
    args = dict(locals())
    w = {n: args[n] for n in WEIGHTS}
    m = {n: args["m_" + n] for n in WEIGHTS}
    v = {n: args["v_" + n] for n in WEIGHTS}
    L = x.shape[1]
    tm = min(L, 256)
    tr = min(L, 512)
    chip = 2 * lax.axis_index("x") + lax.axis_index("y")
    me = 2 * chip + lax.axis_index("c")
    xin = x[0]
    target = loss_target[0]

    small_in = _pack([c, conv_w, conv_b, sg_v_g])
    got = _allgather_small(small_in, "gather_small_inputs").reshape(N_DEV, -1)
    c_all, cw_sh, cb_sh, vg_sh = _unpack(got, [(D,), conv_w.shape, conv_b.shape, sg_v_g.shape], lead=(N_DEV,))
    conv_w_full = jnp.concatenate([cw_sh[2 * k] for k in range(4)], axis=-1)
    conv_b_full = jnp.concatenate([cb_sh[2 * k] for k in range(4)], axis=-1)
    vg_full = jnp.concatenate([vg_sh[2 * k] for k in range(4)], axis=-1)
    c16 = jnp.pad(c_all, ((0, 16 - N_DEV), (0, 0)))

    cols = ada_w.shape[2]
    ada_b_cols = lax.dynamic_slice_in_dim(ada_b, chip * cols, cols, axis=1)[:, None, :]
    mod_sh = _ada_fwd(c16, ada_w, ada_b_cols)[:, :N_DEV, :]
    mod_all = _allgather_small(_pack([mod_sh]), "gather_mod").reshape(N_DEV, -1)
    mod_all = _unpack(mod_all, [mod_sh.shape], lead=(N_DEV,))[0]
    mod_mine = lax.dynamic_index_in_dim(mod_all[0::2], me, axis=2, keepdims=False)
    mod_mine = mod_mine.transpose(1, 0, 2).reshape(DEPTH, 6, D)
    pv = jnp.concatenate([mod_mine, norm1_g[:, None, :], norm2_g[:, None, :]], axis=1)

    full = dict(zip(BIG, _allgather_weights([w[n].astype(BF16) for n in BIG], [BIG_AXIS[n] for n in BIG], "gather_weights")))

    cw_rows = jnp.concatenate([conv_w_full, conv_b_full[:, None, :], jnp.zeros((conv_w_full.shape[0], 4, D), F32)], axis=1)

    a_re, a_im = ssm_a_re[0], ssm_a_im[0]
    log_dt = ssm_log_dt[0][:, None]
    bt_re, bt_im = ssm_b_re[0].transpose(2, 0, 1), ssm_b_im[0].transpose(2, 0, 1)
    abar_re, abar_im, bbar_re, bbar_im = _s5_params_fwd(a_re, a_im, log_dt, bt_re, bt_im)
    ar_vec, ai_vec = abar_re.reshape(1, NSTATE), abar_im.reshape(1, NSTATE)
    bd_re, bd_im = _bf(_blockdiag_b(bbar_re)), _bf(_blockdiag_b(bbar_im))
    cd_re, cd_im = _bf(_blockdiag_c(ssm_c_re[0])), _bf(_blockdiag_c(ssm_c_im[0]))

    saved = []
    xl = xin
    for i in range(DEPTH):
        kind = MIXER_OF_LAYER[i]
        j = i // 3
        if kind == 0:
            x1, h, bcx, conv, q, y = _conv_fwd(xl, pv, full["conv_w_in"], full["conv_w_out"], cw_rows, i, j, tm)
            mix = dict(h=h, bcx=bcx, conv=conv, q=q, y=y)
        elif kind == 1:
            h, u, bu_re, bu_im = _s5_in_fwd(xl, pv, full["ssm_w_in"], bd_re, bd_im, i, tm)
            s_re, s_im = _s5_scan_fwd(bu_re, bu_im, ar_vec, ai_vec, tr)
            x1, y1, zg, y3, y = _s5_out_fwd(xl, u, s_re, s_im, pv, cd_re, cd_im, ssm_d, full["ssm_glu_w"], ssm_glu_b,
                                            full["ssm_w_out"], i, tm)
            mix = dict(h=h, u=u, s_re=s_re, s_im=s_im, y1=y1, zg=zg, y3=y3, y=y)
        else:
            x1, h, uv, vm, q, y = _sg_fwd(xl, pv, full["sg_w_in"], sg_w_s[0], sg_b_s[0].T, vg_full, full["sg_w_out"], i, tm)
            mix = dict(h=h, uv=uv, vm=vm, q=q, y=y)
        x2, h2, a, f = _ffn_fwd(x1, pv, full["ff_w1"], full["ff_w2"], i, tm)
        saved.append(dict(x=xl, x1=x1, h2=h2, a=a, f=f, **mix))
        xl = x2

    dxl, vs_fin = _final(xl, target, final_g[None, :], tm)

    gfull = {n: [None] * w[n].shape[0] for n in BIG}
    vs_mix, vs_ffn = [None] * DEPTH, [None] * DEPTH
    small_g = {}
    for i in reversed(range(DEPTH)):
        kind = MIXER_OF_LAYER[i]
        j = i // 3
        sv = saved[i]
        dx1, p_b, da_b, df_b, vs_ffn[i] = _ffn_bwd(dxl, sv["x1"], sv["a"], sv["f"], pv, full["ff_w1"], full["ff_w2"], i, tm)
        gfull["ff_w1"][i] = _mm_tn(sv["h2"], da_b, f"wgrad_ff_w1_{i}")
        gfull["ff_w2"][i] = _mm_tn(p_b, df_b, f"wgrad_ff_w2_{i}")
        if kind == 0:
            dxl, dbcx_b, dy_b, vsm = _conv_bwd(dx1, sv["x"], sv["y"], sv["bcx"], sv["conv"], pv, full["conv_w_in"],
                                               full["conv_w_out"], cw_rows, i, j, tm)
            gfull["conv_w_in"][j] = _mm_tn(sv["h"], dbcx_b, f"wgrad_conv_w_in_{j}")
            gfull["conv_w_out"][j] = _mm_tn(sv["q"], dy_b, f"wgrad_conv_w_out_{j}")
            small_g.setdefault("conv_w", [None, None])[j] = vsm[3:6]
            small_g.setdefault("conv_b", [None, None])[j] = vsm[6]
        elif kind == 1:
            dy_b, y2_b, dzg_b, dy1_b, du_skip, g_re, g_im, vsm = _s5_out_bwd(
                dx1, sv["y"], sv["y1"], sv["zg"], sv["u"], pv, cd_re, cd_im, ssm_d, full["ssm_glu_w"], full["ssm_w_out"], i, tm)
            lam_re, lam_im, dabar = _s5_scan_bwd(g_re, g_im, sv["s_re"], sv["s_im"], ar_vec, ai_vec, tr)
            dxl, du_b, vs_in = _s5_in_bwd(dx1, lam_re, lam_im, du_skip, sv["x"], pv, bd_re, bd_im, full["ssm_w_in"], i, tm)
            gfull["ssm_w_out"][0] = _mm_tn(sv["y3"], dy_b, "wgrad_ssm_w_out")
            gfull["ssm_glu_w"][0] = _mm_tn(y2_b, dzg_b, "wgrad_ssm_glu_w")
            gfull["ssm_w_in"][0] = _mm_tn(sv["h"], du_b, "wgrad_ssm_w_in")
            d_cre = _unblock_c(_mm_tn_blocks(sv["s_re"], dy1_b, S5_BP, S5_BH, "wgrad_s5_c_re"))
            d_cim = -_unblock_c(_mm_tn_blocks(sv["s_im"], dy1_b, S5_BP, S5_BH, "wgrad_s5_c_im"))
            d_bbre = _unblock_b(_mm_tn_blocks(sv["u"], lam_re, S5_BH, S5_BP, "wgrad_s5_b_re"))
            d_bbim = _unblock_b(_mm_tn_blocks(sv["u"], lam_im, S5_BH, S5_BP, "wgrad_s5_b_im"))
            d_are, d_aim, d_ldt, d_btre, d_btim = _s5_params_bwd(
                a_re, a_im, log_dt, bt_re, bt_im, dabar[0].reshape(S5_G, S5_P), dabar[1].reshape(S5_G, S5_P), d_bbre, d_bbim)
            small_g.update(ssm_a_re=d_are, ssm_a_im=d_aim, ssm_log_dt=d_ldt, ssm_b_re=d_btre.transpose(1, 2, 0),
                           ssm_b_im=d_btim.transpose(1, 2, 0), ssm_c_re=d_cre, ssm_c_im=d_cim, ssm_d=vsm[2], ssm_glu_b=vsm[1])
            vsm = jnp.concatenate([vsm[0:1], vs_in[1:3], jnp.zeros((5, D), F32)], axis=0)
        else:
            dxl, duv_b, dy_b, vsm, d_ws, d_bt = _sg_bwd(dx1, sv["x"], sv["y"], sv["uv"], sv["vm"], pv, full["sg_w_in"],
                                                        sg_w_s[0], vg_full, full["sg_w_out"], i, tm)
            gfull["sg_w_in"][0] = _mm_tn(sv["h"], duv_b, "wgrad_sg_w_in")
            gfull["sg_w_out"][0] = _mm_tn(sv["q"], dy_b, "wgrad_sg_w_out")
            small_g.update(sg_v_g=vsm[3], sg_w_s=d_ws, sg_b_s=d_bt.T)
        vs_mix[i] = vsm
    grad_x = dxl[None]

    dmod = _mod_bwd(jnp.stack(vs_mix), jnp.stack(vs_ffn), pv)
    small_g.update(ada_b=dmod[:, :6, :], norm1_g=dmod[:, 6, :], norm2_g=dmod[:, 7, :], final_g=vs_fin[1],
                   conv_w=jnp.stack(small_g["conv_w"]), conv_b=jnp.stack(small_g["conv_b"]))

    loss_part = (0.5 / D) * vs_fin[0, 0:1]
    part_shapes = [(1,)] + [tuple(small_g[n].shape) for n in SMALL]
    parts_all, parts_sum = _allgather_small(_pack([loss_part] + [small_g[n] for n in SMALL]), "reduce_small_grads", with_sum=True)
    summed = _unpack(parts_sum, part_shapes)
    loss = summed[0][0]
    gsum = dict(zip(SMALL, summed[1:]))
    dmod_all = _unpack(parts_all.reshape(N_DEV, -1), part_shapes[:2], lead=(N_DEV,))[1]
    dmod_all = dmod_all.reshape(N_DEV, DEPTH, 6 * D)
    dmod_cols = lax.dynamic_slice_in_dim(dmod_all, chip * cols, cols, axis=2).transpose(1, 0, 2)
    g_ada_w = _ada_bwd(c16, jnp.pad(dmod_cols, ((0, 0), (0, 16 - N_DEV), (0, 0))))

    names, layers, garrs, gaxes = [], [], [], []
    for n in BIG:
        for li, g in enumerate(gfull[n]):
            names.append(n)
            layers.append(li)
            garrs.append(g[None])
            gaxes.append(BIG_AXIS[n])
    recv = _scatter_grads(garrs, gaxes, "scatter_grads")
    sums = {n: [None] * w[n].shape[0] for n in BIG}
    for n, li, r4 in zip(names, layers, recv):
        sums[n][li] = _sum4(r4, f"sum4_{n}_{li}")
    q_mine = [jnp.stack(sums[n]) for n in BIG]
    q_sib = _swap_with_sibling(q_mine, "swap_grad_sums")

    res = {}
    for n, qa, qb in zip(BIG, q_mine, q_sib):
        shp = w[n].shape
        two = lambda t, shp=shp: t.reshape(shp[0] * shp[1], shp[2])
        res[n] = [t.reshape(shp) for t in _adamw(two(w[n]), [two(qa), two(qb)], two(m[n]), two(v[n]), f"adamw_{n}")]
    shp = ada_w.shape
    two = lambda t: t.reshape(shp[0] * shp[1], shp[2])
    res["ada_w"] = [t.reshape(shp) for t in _adamw(two(ada_w), [two(g_ada_w)], two(m_ada_w), two(v_ada_w), "adamw_ada_w")]

    def mine(n, g):
        if n in ("conv_w", "conv_b", "sg_v_g"):
            size = w[n].shape[-1]
            return lax.dynamic_slice_in_dim(g, chip * size, size, axis=g.ndim - 1)
        return g

    g_loc = [mine(n, gsum[n]).reshape(w[n].shape) for n in SMALL]
    small_shapes = [tuple(w[n].shape) for n in SMALL]
    packed = _adamw(_pack([w[n] for n in SMALL]), [_pack(g_loc)], _pack([m[n] for n in SMALL]), _pack([v[n] for n in SMALL]),
                    "adamw_small")
    unpacked = [_unpack(t, small_shapes) for t in packed]
    for k, n in enumerate(SMALL):
        res[n] = [unpacked[0][k], unpacked[1][k], unpacked[2][k], unpacked[3][k]]

    outs = [loss, grad_x]
    for part in range(4):
        outs += [res[n][part] for n in WEIGHTS]
    return tuple(outs)
```

```python
import functools

import jax
import jax.numpy as jnp
from jax import lax
from jax.experimental import pallas as pl
from jax.experimental.pallas import tpu as pltpu

F32 = jnp.float32
BF16 = jnp.bfloat16
D = 1024
EPS = 1e-6
DEPTH = 4
MIXER_OF_LAYER = (0, 1, 2, 0)
S5_G, S5_H, S5_P = 64, 16, 64
S5_NB = 4
S5_BH = S5_H * 16
S5_BP = S5_P * 16
NSTATE = S5_G * S5_P
SG_HEADS, SG_CHUNK = 8, 128
ADAM_LR, ADAM_B1, ADAM_B2, ADAM_EPS, ADAM_WD, ADAM_STEP = 0.001, 0.9, 0.999, 1e-08, 0.01, 10
N_DEV = 8
MESH = pl.DeviceIdType.MESH
LANES = 1024
R_SH1, R_SC1, R_G1, R_SH2, R_SC2, R_G2, R_N1, R_N2 = range(8)


def _dot(a, b):
    return jnp.dot(a, b, preferred_element_type=F32)


def _dot_nt(a, b):
    return lax.dot_general(a, b, (((1,), (1,)), ((), ())), preferred_element_type=F32)


def _dot_tn(a, b):
    return lax.dot_general(a, b, (((0,), (0,)), ((), ())), preferred_element_type=F32)


def _bf(x):
    return x.astype(BF16)


def _sum0(x):
    return jnp.sum(x, axis=0, keepdims=True)


def _params(n_axes, vmem_mb=48):
    return pltpu.CompilerParams(dimension_semantics=("arbitrary",) * n_axes, vmem_limit_bytes=vmem_mb << 20)


def _rows(tm, cols, nt=None):
    if nt is None:
        return pl.BlockSpec((tm, cols), lambda i: (i, 0))
    return pl.BlockSpec((tm, cols), lambda i: (nt - 1 - i, 0))


def _whole(shape):
    nd = len(shape)
    return pl.BlockSpec(shape, lambda *_: (0,) * nd)


def _layer_w(r, c, layer):
    return pl.BlockSpec((None, r, c), lambda *_: (layer, 0, 0), pipeline_mode=pl.Buffered(1))


def _const_w(shape):
    nd = len(shape)
    return pl.BlockSpec(shape, lambda *_: (0,) * nd, pipeline_mode=pl.Buffered(1))


def _norm_mod(x, ng, sc, sh):
    r = lax.rsqrt(jnp.mean(x * x, axis=-1, keepdims=True) + EPS)
    xn = x * r
    return (xn * ng) * (1.0 + sc) + sh, xn, r


def _norm_mod_bwd(dh, xn, r, ng, sc):
    dxn = dh * (ng * (1.0 + sc))
    return r * (dxn - xn * jnp.mean(dxn * xn, axis=-1, keepdims=True))


def _shift_down(z, prev8, k):
    row = lax.broadcasted_iota(jnp.int32, z.shape, 0)
    if k == 1:
        return jnp.where(row >= 1, pltpu.roll(z, 1, 0), prev8[7:8])
    return jnp.where(row >= 2, pltpu.roll(z, 2, 0), jnp.where(row == 0, prev8[6:7], prev8[7:8]))


def _shift_up(z, next8, k):
    n = z.shape[0]
    row = lax.broadcasted_iota(jnp.int32, z.shape, 0)
    if k == 1:
        return jnp.where(row <= n - 2, pltpu.roll(z, n - 1, 0), next8[0:1])
    return jnp.where(row <= n - 3, pltpu.roll(z, n - 2, 0), jnp.where(row == n - 2, next8[0:1], next8[1:2]))


def _place():
    x, y, c = lax.axis_index("x"), lax.axis_index("y"), lax.axis_index("c")
    chips = [(1 - x, y), (x, 1 - y), (1 - x, 1 - y)]
    return x, y, c, chips


def _allgather_small(x_shard, name, with_sum=False):
    m_per, n = x_shard.shape

    def body(x_ref, *rest):
        if with_sum:
            out_ref, sum_ref, send_sems, recv_sems, local_sem = rest
        else:
            out_ref, send_sems, recv_sems, local_sem = rest
        x, y, c, chips = _place()
        me, sibling = (x, y, c), (x, y, 1 - c)

        def rows(px, py, pc):
            return out_ref.at[pl.ds((4 * px + 2 * py + pc) * m_per, m_per), :]

        def copy(k, block, to, src=None):
            return pltpu.make_async_remote_copy(
                src_ref=rows(*block) if src is None else src, dst_ref=rows(*block),
                send_sem=send_sems.at[k], recv_sem=recv_sems.at[k], device_id=to, device_id_type=MESH)

        mine = pltpu.make_async_copy(x_ref, rows(*me), local_sem)
        mine.start()
        first = [copy(0, me, sibling, src=x_ref)]
        first += [copy(1 + j, me, (*chip, c), src=x_ref) for j, chip in enumerate(chips)]
        for cp in first:
            cp.start()
        passed = [copy(4 + j, (*chip, c), sibling) for j, chip in enumerate(chips)]
        for j, chip in enumerate(chips):
            copy(1 + j, (*chip, c), me).wait_recv()
            passed[j].start()
        copy(0, sibling, me).wait_recv()
        for j, chip in enumerate(chips):
            copy(4 + j, (*chip, 1 - c), me).wait_recv()
        for cp in first + passed:
            cp.wait_send()
        mine.wait()
        if with_sum:
            acc = out_ref[0:m_per, :]
            for d in range(1, N_DEV):
                acc = acc + out_ref[d * m_per:(d + 1) * m_per, :]
            sum_ref[...] = acc

    out_shape = [jax.ShapeDtypeStruct((N_DEV * m_per, n), F32)]
    out_specs = [pl.BlockSpec(memory_space=pltpu.VMEM)]
    if with_sum:
        out_shape.append(jax.ShapeDtypeStruct((m_per, n), F32))
        out_specs.append(pl.BlockSpec(memory_space=pltpu.VMEM))
    res = pl.pallas_call(
        body, name=name, out_shape=out_shape,
        in_specs=[pl.BlockSpec(memory_space=pltpu.VMEM)], out_specs=out_specs,
        scratch_shapes=[pltpu.SemaphoreType.DMA((7,)), pltpu.SemaphoreType.DMA((7,)), pltpu.SemaphoreType.DMA],
        compiler_params=pltpu.CompilerParams(vmem_limit_bytes=48 << 20),
    )(x_shard)
    return tuple(res) if with_sum else res[0]


def _shard_region(ref, full_shape, axis, chip_k, half=None):
    _, r, c = full_shape
    if axis == 1:
        rs = r // 4
        if half is None:
            return ref.at[:, pl.ds(pl.multiple_of(chip_k * rs, 128), rs), :]
        return ref.at[:, pl.ds(pl.multiple_of(chip_k * rs + half * (rs // 2), 128), rs // 2), :]
    cs = c // 4
    if half is None:
        return ref.at[:, :, pl.ds(pl.multiple_of(chip_k * cs, 128), cs)]
    return ref.at[:, pl.ds(pl.multiple_of(half * (r // 2), 128), r // 2), pl.ds(pl.multiple_of(chip_k * cs, 128), cs)]


HBM_SPEC = pl.BlockSpec(memory_space=pltpu.HBM)
SEM_SPEC = pl.BlockSpec(memory_space=pltpu.SEMAPHORE)
ANY_SPEC = pl.BlockSpec(memory_space=pl.ANY)
SPLIT_COPY_PARAMS = pltpu.CompilerParams(has_side_effects=pltpu.SideEffectType.DATAFLOW_SIDE_EFFECTING)


def _in_hbm(arrs):
    return [pltpu.with_memory_space_constraint(a, pltpu.HBM) for a in arrs]


def _full_shapes(shards, axes):
    fulls = []
    for s, ax in zip(shards, axes):
        shp = list(s.shape)
        shp[ax] *= 4
        fulls.append(tuple(shp))
    return fulls


def _half_of_shard(ref, rows, half):
    return ref.at[:, pl.ds(pl.multiple_of(half * (rows // 2), 64), rows // 2), :]


def _gather_start(shards, axes, name, after):
    n_arr = len(shards)
    fulls = _full_shapes(shards, axes)
    lands = [lax.empty(f, BF16) for f in fulls]

    def body(*refs):
        ins, land = refs[:n_arr], refs[n_arr:2 * n_arr]
        send_sems, recv_sems = refs[2 * n_arr + 1:2 * n_arr + 3]
        token = refs[-1]
        x, y, c, chips = _place()
        k_me = 2 * x + y
        for a in range(n_arr):
            for j, chip in enumerate(chips):
                pltpu.make_async_remote_copy(
                    src_ref=_half_of_shard(ins[a], shards[a].shape[1], c),
                    dst_ref=_shard_region(land[a], fulls[a], axes[a], k_me, c),
                    send_sem=send_sems.at[a * 3 + j], recv_sem=recv_sems.at[a * 3 + j],
                    device_id=(*chip, c), device_id_type=MESH).start()
        token[...] = jnp.zeros_like(token)

    res = pl.pallas_call(
        body, name=name,
        out_shape=(pltpu.SemaphoreType.DMA((3 * n_arr,)), pltpu.SemaphoreType.DMA((3 * n_arr,)),
                   *[pltpu.HBM(s.shape, BF16) for s in shards], *[pltpu.HBM(f, BF16) for f in fulls],
                   jax.ShapeDtypeStruct((8, 128), F32)),
        in_specs=[HBM_SPEC] * (2 * n_arr) + [ANY_SPEC],
        out_specs=(SEM_SPEC, SEM_SPEC, *[HBM_SPEC] * (2 * n_arr), pl.BlockSpec(memory_space=pltpu.VMEM)),
        input_output_aliases={a: 2 + a for a in range(2 * n_arr)},
        compiler_params=SPLIT_COPY_PARAMS,
    )(*_in_hbm(shards), *_in_hbm(lands), after)
    return res[0], res[1], list(res[2:2 + n_arr]), list(res[2 + n_arr:2 + 2 * n_arr]), res[-1]


def _gather_wait(send_sems, recv_sems, shards, lands, axes, name, after):
    n_arr = len(shards)
    fulls = [tuple(l.shape) for l in lands]

    def body(*refs):
        ins, land = refs[:n_arr], refs[n_arr:2 * n_arr]
        s_sems, r_sems = refs[2 * n_arr:2 * n_arr + 2]
        x, y, c, chips = _place()
        for a in range(n_arr):
            for j, chip in enumerate(chips):
                k_j = 2 * chip[0] + chip[1]
                cp = pltpu.make_async_remote_copy(
                    src_ref=_half_of_shard(ins[a], shards[a].shape[1], c),
                    dst_ref=_shard_region(land[a], fulls[a], axes[a], k_j, c),
                    send_sem=s_sems.at[a * 3 + j], recv_sem=r_sems.at[a * 3 + j],
                    device_id=(x, y, c), device_id_type=MESH)
                cp.wait_send()
                cp.wait_recv()

    res = pl.pallas_call(
        body, name=name,
        out_shape=(*[pltpu.HBM(s.shape, BF16) for s in shards], *[pltpu.HBM(f, BF16) for f in fulls]),
        in_specs=[HBM_SPEC] * (2 * n_arr) + [SEM_SPEC, SEM_SPEC, ANY_SPEC],
        out_specs=tuple([HBM_SPEC] * (2 * n_arr)),
        input_output_aliases={a: a for a in range(2 * n_arr)},
        compiler_params=SPLIT_COPY_PARAMS,
    )(*shards, *lands, send_sems, recv_sems, after)
    return list(res[n_arr:])


def _gather_share(shards, lands, axes, name):
    n_arr = len(shards)
    fulls = [tuple(l.shape) for l in lands]

    def body(*refs):
        ins, land_in, land = refs[:n_arr], refs[n_arr:2 * n_arr], refs[2 * n_arr:3 * n_arr]
        send_sems, recv_sems, local_sems = refs[3 * n_arr:]
        x, y, c, chips = _place()
        k_me = 2 * x + y
        copies = []
        for a in range(n_arr):
            lc = pltpu.make_async_copy(ins[a], _shard_region(land[a], fulls[a], axes[a], k_me), local_sems.at[a])
            lc.start()
            copies.append(lc)
            for j, chip in enumerate(chips):
                k_j = 2 * chip[0] + chip[1]
                cp = pltpu.make_async_remote_copy(
                    src_ref=_shard_region(land_in[a], fulls[a], axes[a], k_j, c),
                    dst_ref=_shard_region(land[a], fulls[a], axes[a], k_j, c),
                    send_sem=send_sems.at[a * 3 + j], recv_sem=recv_sems.at[a * 3 + j],
                    device_id=(x, y, 1 - c), device_id_type=MESH)
                cp.start()
                copies.append(cp)
        for cp in copies:
            cp.wait()

    return pl.pallas_call(
        body, name=name, out_shape=[jax.ShapeDtypeStruct(f, BF16) for f in fulls],
        in_specs=[ANY_SPEC] * (2 * n_arr), out_specs=[ANY_SPEC] * n_arr,
        input_output_aliases={n_arr + a: a for a in range(n_arr)},
        scratch_shapes=[pltpu.SemaphoreType.DMA((3 * n_arr,)), pltpu.SemaphoreType.DMA((3 * n_arr,)),
                        pltpu.SemaphoreType.DMA((n_arr,))],
    )(*shards, *lands)


def _scatter_shapes(grads, axes):
    out = []
    for g, ax in zip(grads, axes):
        shp = list(g.shape)
        shp[ax] //= 4
        out.append((3,) + tuple(shp[1:]))
    return out


def _scatter_start(grads, axes, name):
    n_arr = len(grads)
    shapes = _scatter_shapes(grads, axes)
    lands = [lax.empty(s, BF16) for s in shapes]

    def body(*refs):
        ins, land = refs[:n_arr], refs[n_arr:2 * n_arr]
        send_sems, recv_sems = refs[2 * n_arr:2 * n_arr + 2]
        token = refs[-1]
        x, y, c, chips = _place()
        for a in range(n_arr):
            for j, chip in enumerate(chips):
                k_j = 2 * chip[0] + chip[1]
                pltpu.make_async_remote_copy(
                    src_ref=_shard_region(ins[a], grads[a].shape, axes[a], k_j), dst_ref=land[a].at[pl.ds(j, 1)],
                    send_sem=send_sems.at[a * 3 + j], recv_sem=recv_sems.at[a * 3 + j],
                    device_id=(*chip, c), device_id_type=MESH).start()
        token[...] = jnp.zeros_like(token)

    res = pl.pallas_call(
        body, name=name,
        out_shape=(pltpu.SemaphoreType.DMA((3 * n_arr,)), pltpu.SemaphoreType.DMA((3 * n_arr,)),
                   *[pltpu.HBM(g.shape, BF16) for g in grads], *[pltpu.HBM(s, BF16) for s in shapes],
                   jax.ShapeDtypeStruct((8, 128), F32)),
        in_specs=[HBM_SPEC] * (2 * n_arr),
        out_specs=(SEM_SPEC, SEM_SPEC, *[HBM_SPEC] * (2 * n_arr), pl.BlockSpec(memory_space=pltpu.VMEM)),
        input_output_aliases={a: 2 + a for a in range(2 * n_arr)},
        compiler_params=SPLIT_COPY_PARAMS,
    )(*_in_hbm(grads), *_in_hbm(lands))
    return res[0], res[1], list(res[2:2 + n_arr]), list(res[2 + n_arr:2 + 2 * n_arr]), res[-1]


def _scatter_wait(send_sems, recv_sems, grads, lands, axes, name, after):
    n_arr = len(grads)

    def body(*refs):
        ins, land = refs[:n_arr], refs[n_arr:2 * n_arr]
        s_sems, r_sems = refs[2 * n_arr:2 * n_arr + 2]
        x, y, c, chips = _place()
        for a in range(n_arr):
            for j, chip in enumerate(chips):
                k_j = 2 * chip[0] + chip[1]
                cp = pltpu.make_async_remote_copy(
                    src_ref=_shard_region(ins[a], grads[a].shape, axes[a], k_j), dst_ref=land[a].at[pl.ds(j, 1)],
                    send_sem=s_sems.at[a * 3 + j], recv_sem=r_sems.at[a * 3 + j],
                    device_id=(x, y, c), device_id_type=MESH)
                cp.wait_send()
                cp.wait_recv()

    res = pl.pallas_call(
        body, name=name,
        out_shape=(*[pltpu.HBM(g.shape, BF16) for g in grads], *[pltpu.HBM(l.shape, BF16) for l in lands]),
        in_specs=[HBM_SPEC] * (2 * n_arr) + [SEM_SPEC, SEM_SPEC, ANY_SPEC],
        out_specs=tuple([HBM_SPEC] * (2 * n_arr)),
        input_output_aliases={a: a for a in range(2 * n_arr)},
        compiler_params=SPLIT_COPY_PARAMS,
    )(*grads, *lands, send_sems, recv_sems, after)
    return list(res[:n_arr]), list(res[n_arr:])


def _swap_with_sibling(arrs, name):
    n_arr = len(arrs)

    def body(*refs):
        ins, outs = refs[:n_arr], refs[n_arr:2 * n_arr]
        send_sems, recv_sems = refs[2 * n_arr:]
        x, y, c, _ = _place()
        copies = []
        for a in range(n_arr):
            cp = pltpu.make_async_remote_copy(
                src_ref=ins[a], dst_ref=outs[a], send_sem=send_sems.at[a], recv_sem=recv_sems.at[a],
                device_id=(x, y, 1 - c), device_id_type=MESH)
            cp.start()
            copies.append(cp)
        for cp in copies:
            cp.wait()

    any_spec = pl.BlockSpec(memory_space=pl.ANY)
    return pl.pallas_call(
        body, name=name, out_shape=[jax.ShapeDtypeStruct(a.shape, a.dtype) for a in arrs],
        in_specs=[any_spec] * n_arr, out_specs=[any_spec] * n_arr,
        scratch_shapes=[pltpu.SemaphoreType.DMA((n_arr,)), pltpu.SemaphoreType.DMA((n_arr,))],
    )(*arrs)


def _mm_tn(a, b, name, out_dtype=BF16):
    L, m = a.shape
    n = b.shape[1]
    bm, bn, bk = min(m, 1024), min(n, 1024), min(L, 512)
    nk = L // bk

    def body(a_ref, b_ref, o_ref, acc):
        k = pl.program_id(2)

        @pl.when(k == 0)
        def _():
            acc[...] = jnp.zeros_like(acc)

        acc[...] += _dot_tn(_bf(a_ref[...]), _bf(b_ref[...]))

        @pl.when(k == nk - 1)
        def _():
            o_ref[...] = acc[...].astype(out_dtype)

    return pl.pallas_call(
        body, name=name, grid=(m // bm, n // bn, nk),
        in_specs=[pl.BlockSpec((bk, bm), lambda i, j, k: (k, i)), pl.BlockSpec((bk, bn), lambda i, j, k: (k, j))],
        out_specs=pl.BlockSpec((bm, bn), lambda i, j, k: (i, j)),
        out_shape=jax.ShapeDtypeStruct((m, n), out_dtype),
        scratch_shapes=[pltpu.VMEM((bm, bn), F32)],
        compiler_params=_params(3),
    )(a, b)


def _mm_tn_blocks(a, b, wa, wb, name):
    L = a.shape[0]
    nb = a.shape[1] // wa
    bk = min(L, 512)
    nk = L // bk

    def body(a_ref, b_ref, o_ref):
        @pl.when(pl.program_id(1) == 0)
        def _():
            o_ref[...] = jnp.zeros_like(o_ref)

        o_ref[...] += _dot_tn(_bf(a_ref[...]), _bf(b_ref[...]))

    return pl.pallas_call(
        body, name=name, grid=(nb, nk),
        in_specs=[pl.BlockSpec((bk, wa), lambda j, k: (k, j)), pl.BlockSpec((bk, wb), lambda j, k: (k, j))],
        out_specs=pl.BlockSpec((None, wa, wb), lambda j, k: (j, 0, 0)),
        out_shape=jax.ShapeDtypeStruct((nb, wa, wb), F32),
        compiler_params=_params(2),
    )(a, b)


def _sum_parts(parts, own, axis, chip, name):
    _, r, c = parts.shape
    tr = min(r, 256)
    if axis == 1:
        own_spec = pl.BlockSpec((None, tr, c), lambda i, k: (0, k[0] * (r // tr) + i, 0))
    else:
        own_spec = pl.BlockSpec((None, tr, c), lambda i, k: (0, i, k[0]))

    def body(k_ref, p_ref, g_ref, o_ref):
        p = p_ref[...].astype(F32)
        o_ref[...] = ((p[0] + p[1]) + p[2]) + g_ref[...].astype(F32)

    return pl.pallas_call(
        body, name=name,
        grid_spec=pltpu.PrefetchScalarGridSpec(
            num_scalar_prefetch=1, grid=(r // tr,),
            in_specs=[pl.BlockSpec((3, tr, c), lambda i, k: (0, i, 0)), own_spec],
            out_specs=pl.BlockSpec((tr, c), lambda i, k: (i, 0))),
        out_shape=jax.ShapeDtypeStruct((r, c), F32),
        compiler_params=_params(1),
    )(chip, parts, own)


def _adamw(w, g_parts, m, v, name):
    r, c = w.shape
    tr = r
    for cand in (512, 256, 128, 64, 32, 16, 8):
        if r % cand == 0 and cand * c * 4 <= (2 << 20):
            tr = cand
            break
    n_g = len(g_parts)
    c1 = 1.0 / (1.0 - ADAM_B1 ** ADAM_STEP)
    c2 = 1.0 / (1.0 - ADAM_B2 ** ADAM_STEP)

    def body(*refs):
        w_ref, g_refs, m_ref, v_ref = refs[0], refs[1:1 + n_g], refs[1 + n_g], refs[2 + n_g]
        g_out, d_out, m_out, v_out = refs[3 + n_g:]
        g = g_refs[0][...]
        for gr in g_refs[1:]:
            g = g + gr[...]
        m_new = ADAM_B1 * m_ref[...] + (1.0 - ADAM_B1) * g
        v_new = ADAM_B2 * v_ref[...] + (1.0 - ADAM_B2) * (g * g)
        m_hat = m_new * c1
        v_hat = v_new * c2
        g_out[...] = g
        d_out[...] = -ADAM_LR * (m_hat / (jnp.sqrt(v_hat) + ADAM_EPS) + ADAM_WD * w_ref[...])
        m_out[...] = m_new
        v_out[...] = v_new

    spec = pl.BlockSpec((tr, c), lambda i: (i, 0))
    return pl.pallas_call(
        body, name=name, grid=(r // tr,),
        in_specs=[spec] * (3 + n_g), out_specs=[spec] * 4,
        out_shape=[jax.ShapeDtypeStruct((r, c), F32)] * 4,
        compiler_params=_params(1),
    )(w, *g_parts, m, v)


def _ada_fwd(c16, ada_w, ada_b_cols):
    cols = ada_w.shape[2]

    def body(c_ref, w_ref, b_ref, o_ref):
        cv = c_ref[...]
        ca = _bf(cv * jax.nn.sigmoid(cv))
        o_ref[...] = _dot(ca, _bf(w_ref[...])) + b_ref[...]

    return pl.pallas_call(
        body, name="ada_fwd", grid=(DEPTH,),
        in_specs=[_whole((16, D)), pl.BlockSpec((None, D, cols), lambda i: (i, 0, 0)),
                  pl.BlockSpec((None, 1, cols), lambda i: (i, 0, 0))],
        out_specs=pl.BlockSpec((None, 16, cols), lambda i: (i, 0, 0)),
        out_shape=jax.ShapeDtypeStruct((DEPTH, 16, cols), F32),
        compiler_params=_params(1),
    )(c16, ada_w, ada_b_cols)


def _ada_bwd(c16, dmod16):
    cols = dmod16.shape[2]

    def body(c_ref, d_ref, o_ref):
        cv = c_ref[...]
        ca = _bf(cv * jax.nn.sigmoid(cv))
        o_ref[...] = _dot_tn(ca, _bf(d_ref[...]))

    return pl.pallas_call(
        body, name="ada_bwd", grid=(DEPTH,),
        in_specs=[_whole((16, D)), pl.BlockSpec((None, 16, cols), lambda i: (i, 0, 0))],
        out_specs=pl.BlockSpec((None, D, cols), lambda i: (i, 0, 0)),
        out_shape=jax.ShapeDtypeStruct((DEPTH, D, cols), F32),
        compiler_params=_params(1),
    )(c16, dmod16)


def _mod_bwd(vs_mix, vs_ffn, pv):
    def body(m_ref, f_ref, pv_ref, o_ref):
        for i in range(DEPTH):
            vm, vf, p = m_ref[i], f_ref[i], pv_ref[i]
            o_ref[i] = jnp.concatenate([
                vm[2:3], vm[1:2] * p[R_N1:R_N1 + 1], vm[0:1],
                vf[2:3], vf[1:2] * p[R_N2:R_N2 + 1], vf[0:1],
                vm[1:2] * (1.0 + p[R_SC1:R_SC1 + 1]), vf[1:2] * (1.0 + p[R_SC2:R_SC2 + 1])], axis=0)

    return pl.pallas_call(body, name="mod_bwd", out_shape=jax.ShapeDtypeStruct((DEPTH, 8, D), F32))(vs_mix, vs_ffn, pv)


def _ffn_fwd(x1, pv, w1, w2, layer, tm):
    L = x1.shape[0]
    dff = w1.shape[2]

    def body(x1_ref, pv_ref, w1_ref, w2_ref, x2_ref, h2_ref, a_ref, f_ref):
        x1v, p = x1_ref[...], pv_ref[...]
        h2, _, _ = _norm_mod(x1v, p[R_N2:R_N2 + 1], p[R_SC2:R_SC2 + 1], p[R_SH2:R_SH2 + 1])
        hb = _bf(h2)
        h2_ref[...] = hb
        a = _dot(hb, w1_ref[...])
        a_ref[...] = a
        ra = jnp.maximum(a, 0.0)
        f = _dot(_bf(ra * ra), w2_ref[...])
        f_ref[...] = f
        x2_ref[...] = x1v + p[R_G2:R_G2 + 1] * f

    return pl.pallas_call(
        body, name=f"ffn_fwd{layer}", grid=(L // tm,),
        in_specs=[_rows(tm, D), pl.BlockSpec((None, 8, D), lambda i: (layer, 0, 0)), _layer_w(D, dff, 0), _layer_w(dff, D, 0)],
        out_specs=[_rows(tm, D), _rows(tm, D), _rows(tm, dff), _rows(tm, D)],
        out_shape=[jax.ShapeDtypeStruct((L, D), F32), jax.ShapeDtypeStruct((L, D), BF16),
                   jax.ShapeDtypeStruct((L, dff), F32), jax.ShapeDtypeStruct((L, D), F32)],
        compiler_params=_params(1, 56),
    )(x1, pv, w1, w2)


def _ffn_bwd(dx2, x1, a, f, pv, w1, w2, layer, tm, after=None):
    L = x1.shape[0]
    dff = w1.shape[2]
    extra = [] if after is None else [pl.BlockSpec(memory_space=pl.ANY)]
    extra_args = [] if after is None else [after]

    def body(dx2_ref, x1_ref, a_ref, f_ref, pv_ref, w1_ref, w2_ref, *rest):
        dx1_ref, p_ref, da_ref, df_ref, vs_ref = rest[len(extra):]

        @pl.when(pl.program_id(0) == 0)
        def _():
            vs_ref[...] = jnp.zeros_like(vs_ref)

        dx2v, p = dx2_ref[...], pv_ref[...]
        dfb = _bf(dx2v * p[R_G2:R_G2 + 1])
        df_ref[...] = dfb
        vs_ref[0:1, :] += _sum0(dx2v * f_ref[...])
        dp = _dot_nt(dfb, w2_ref[...])
        ra = jnp.maximum(a_ref[...], 0.0)
        p_ref[...] = _bf(ra * ra)
        dab = _bf(dp * (2.0 * ra))
        da_ref[...] = dab
        dh2 = _dot_nt(dab, w1_ref[...])
        _, xn, r = _norm_mod(x1_ref[...], p[R_N2:R_N2 + 1], p[R_SC2:R_SC2 + 1], p[R_SH2:R_SH2 + 1])
        dx1_ref[...] = dx2v + _norm_mod_bwd(dh2, xn, r, p[R_N2:R_N2 + 1], p[R_SC2:R_SC2 + 1])
        vs_ref[1:2, :] += _sum0(dh2 * xn)
        vs_ref[2:3, :] += _sum0(dh2)

    return pl.pallas_call(
        body, name=f"ffn_bwd{layer}", grid=(L // tm,),
        in_specs=[_rows(tm, D), _rows(tm, D), _rows(tm, dff), _rows(tm, D),
                  pl.BlockSpec((None, 8, D), lambda i: (layer, 0, 0)), _layer_w(D, dff, 0), _layer_w(dff, D, 0)] + extra,
        out_specs=[_rows(tm, D), _rows(tm, dff), _rows(tm, dff), _rows(tm, D), _whole((8, D))],
        out_shape=[jax.ShapeDtypeStruct((L, D), F32), jax.ShapeDtypeStruct((L, dff), BF16),
                   jax.ShapeDtypeStruct((L, dff), BF16), jax.ShapeDtypeStruct((L, D), BF16),
                   jax.ShapeDtypeStruct((8, D), F32)],
        compiler_params=_params(1, 56),
    )(dx2, x1, a, f, pv, w1, w2, *extra_args)


def _conv_fwd(x, pv, w_in, w_out, cw, layer, j, tm):
    L = x.shape[0]

    def body(x_ref, pv_ref, win_ref, wout_ref, cw_ref, x1_ref, h_ref, bcx_ref, conv_ref, q_ref, y_ref, carry):
        @pl.when(pl.program_id(0) == 0)
        def _():
            carry[...] = jnp.zeros_like(carry)

        xv, p, cwv = x_ref[...], pv_ref[...], cw_ref[...]
        h, _, _ = _norm_mod(xv, p[R_N1:R_N1 + 1], p[R_SC1:R_SC1 + 1], p[R_SH1:R_SH1 + 1])
        hb = _bf(h)
        h_ref[...] = hb
        bcx = _dot(hb, win_ref[...])
        bcx_ref[...] = bcx
        z = bcx[:, D:2 * D] * bcx[:, 2 * D:]
        prev8 = carry[...]
        conv = cwv[0:1] * _shift_down(z, prev8, 2) + cwv[1:2] * _shift_down(z, prev8, 1) + cwv[2:3] * z + cwv[3:4]
        conv_ref[...] = conv
        qb = _bf(bcx[:, :D] * conv)
        q_ref[...] = qb
        y = _dot(qb, wout_ref[...])
        y_ref[...] = y
        x1_ref[...] = xv + p[R_G1:R_G1 + 1] * y
        carry[...] = z[tm - 8:tm]

    return pl.pallas_call(
        body, name=f"conv_fwd{layer}", grid=(L // tm,),
        in_specs=[_rows(tm, D), pl.BlockSpec((None, 8, D), lambda i: (layer, 0, 0)), _layer_w(D, 3 * D, 0), _layer_w(D, D, 0),
                  pl.BlockSpec((None, 8, D), lambda i: (j, 0, 0))],
        out_specs=[_rows(tm, D), _rows(tm, D), _rows(tm, 3 * D), _rows(tm, D), _rows(tm, D), _rows(tm, D)],
        out_shape=[jax.ShapeDtypeStruct((L, D), F32), jax.ShapeDtypeStruct((L, D), BF16), jax.ShapeDtypeStruct((L, 3 * D), F32),
                   jax.ShapeDtypeStruct((L, D), F32), jax.ShapeDtypeStruct((L, D), BF16), jax.ShapeDtypeStruct((L, D), F32)],
        scratch_shapes=[pltpu.VMEM((8, D), F32)],
        compiler_params=_params(1, 56),
    )(x, pv, w_in, w_out, cw)


def _conv_bwd(dx1, x, y, bcx, conv, pv, w_in, w_out, cw, layer, j, tm):
    L = x.shape[0]
    nt = L // tm

    def body(dx1_ref, x_ref, y_ref, bcx_ref, conv_ref, halo_ref, pv_ref, win_ref, wout_ref, cw_ref,
             dx_ref, dbcx_ref, dy_ref, vs_ref, carry):
        gi = pl.program_id(0)
        tile = nt - 1 - gi

        @pl.when(gi == 0)
        def _():
            vs_ref[...] = jnp.zeros_like(vs_ref)
            carry[...] = jnp.zeros_like(carry)

        dx1v, p, cwv = dx1_ref[...], pv_ref[...], cw_ref[...]
        dyb = _bf(dx1v * p[R_G1:R_G1 + 1])
        dy_ref[...] = dyb
        vs_ref[0:1, :] += _sum0(dx1v * y_ref[...])
        dq = _dot_nt(dyb, wout_ref[...])
        bcx = bcx_ref[...]
        b, cg, xh = bcx[:, :D], bcx[:, D:2 * D], bcx[:, 2 * D:]
        db = dq * conv_ref[...]
        dc = dq * b
        z = cg * xh
        halo = halo_ref[...]
        zprev = jnp.where(tile > 0, halo[:, D:2 * D] * halo[:, 2 * D:], 0.0)
        vs_ref[3:4, :] += _sum0(dc * _shift_down(z, zprev, 2))
        vs_ref[4:5, :] += _sum0(dc * _shift_down(z, zprev, 1))
        vs_ref[5:6, :] += _sum0(dc * z)
        vs_ref[6:7, :] += _sum0(dc)
        next8 = carry[...]
        dz = cwv[2:3] * dc + cwv[1:2] * _shift_up(dc, next8, 1) + cwv[0:1] * _shift_up(dc, next8, 2)
        dbb, dcgb, dxhb = _bf(db), _bf(dz * xh), _bf(dz * cg)
        dbcx_ref[:, 0:D] = dbb
        dbcx_ref[:, D:2 * D] = dcgb
        dbcx_ref[:, 2 * D:3 * D] = dxhb
        dh = (_dot_nt(dbb, win_ref[:, 0:D]) + _dot_nt(dcgb, win_ref[:, D:2 * D])) + _dot_nt(dxhb, win_ref[:, 2 * D:3 * D])
        _, xn, r = _norm_mod(x_ref[...], p[R_N1:R_N1 + 1], p[R_SC1:R_SC1 + 1], p[R_SH1:R_SH1 + 1])
        dx_ref[...] = dx1v + _norm_mod_bwd(dh, xn, r, p[R_N1:R_N1 + 1], p[R_SC1:R_SC1 + 1])
        vs_ref[1:2, :] += _sum0(dh * xn)
        vs_ref[2:3, :] += _sum0(dh)
        carry[...] = dc[0:8]

    halo_spec = pl.BlockSpec((8, 3 * D), lambda i: (jnp.maximum((nt - 1 - i) * (tm // 8) - 1, 0), 0))
    return pl.pallas_call(
        body, name=f"conv_bwd{layer}", grid=(nt,),
        in_specs=[_rows(tm, D, nt), _rows(tm, D, nt), _rows(tm, D, nt), _rows(tm, 3 * D, nt), _rows(tm, D, nt), halo_spec,
                  pl.BlockSpec((None, 8, D), lambda i: (layer, 0, 0)), _layer_w(D, 3 * D, 0), _layer_w(D, D, 0),
                  pl.BlockSpec((None, 8, D), lambda i: (j, 0, 0))],
        out_specs=[_rows(tm, D, nt), _rows(tm, 3 * D, nt), _rows(tm, D, nt), _whole((8, D))],
        out_shape=[jax.ShapeDtypeStruct((L, D), F32), jax.ShapeDtypeStruct((L, 3 * D), BF16),
                   jax.ShapeDtypeStruct((L, D), BF16), jax.ShapeDtypeStruct((8, D), F32)],
        scratch_shapes=[pltpu.VMEM((8, D), F32)],
        compiler_params=_params(1, 56),
    )(dx1, x, y, bcx, conv, bcx, pv, w_in, w_out, cw)


def _s5_discretize(a_re, a_im, log_dt, bt_re, bt_im):
    dt = jnp.exp(log_dt)
    mag = jnp.exp(a_re * dt)
    abar_re = mag * jnp.cos(a_im * dt)
    abar_im = mag * jnp.sin(a_im * dt)
    den = a_re * a_re + a_im * a_im
    nr = abar_re - 1.0
    ni = abar_im
    f_re = (nr * a_re + ni * a_im) / den
    f_im = (ni * a_re - nr * a_im) / den
    bbar_re = f_re * bt_re - f_im * bt_im
    bbar_im = f_re * bt_im + f_im * bt_re
    return abar_re, abar_im, bbar_re, bbar_im


def _s5_params_fwd(a_re, a_im, log_dt, bt_re, bt_im):
    def body(ar, ai, ld, br, bi, o_ar, o_ai, o_br, o_bi):
        r = _s5_discretize(ar[...], ai[...], ld[...], br[...], bi[...])
        o_ar[...], o_ai[...], o_br[...], o_bi[...] = r

    gp = jax.ShapeDtypeStruct((S5_G, S5_P), F32)
    hgp = jax.ShapeDtypeStruct((S5_H, S5_G, S5_P), F32)
    return pl.pallas_call(body, name="s5_params_fwd", out_shape=[gp, gp, hgp, hgp])(a_re, a_im, log_dt, bt_re, bt_im)


def _s5_params_bwd(a_re, a_im, log_dt, bt_re, bt_im, d_ar, d_ai, d_br, d_bi):
    def body(ar, ai, ld, br, bi, gar, gai, gbr, gbi, o_ar, o_ai, o_ld, o_br, o_bi):
        _, vjp = jax.vjp(_s5_discretize, ar[...], ai[...], ld[...], br[...], bi[...])
        r = vjp((gar[...], gai[...], gbr[...], gbi[...]))
        o_ar[...], o_ai[...], o_ld[...], o_br[...], o_bi[...] = r

    gp = jax.ShapeDtypeStruct((S5_G, S5_P), F32)
    hgp = jax.ShapeDtypeStruct((S5_H, S5_G, S5_P), F32)
    return pl.pallas_call(body, name="s5_params_bwd", out_shape=[gp, gp, jax.ShapeDtypeStruct((S5_G, 1), F32), hgp, hgp])(
        a_re, a_im, log_dt, bt_re, bt_im, d_ar, d_ai, d_br, d_bi)


def _s5_in_fwd(x, pv, w_in, b_re, b_im, layer, tm):
    L = x.shape[0]

    def body(x_ref, pv_ref, win_ref, bre_ref, bim_ref, h_ref, u_ref, ore_ref, oim_ref):
        p = pv_ref[...]
        h, _, _ = _norm_mod(x_ref[...], p[R_N1:R_N1 + 1], p[R_SC1:R_SC1 + 1], p[R_SH1:R_SH1 + 1])
        hb = _bf(h)
        h_ref[...] = hb
        u = _dot(hb, win_ref[...])
        u_ref[...] = u
        ub = _bf(u)
        for k in range(S5_NB):
            uk = ub[:, k * S5_BH:(k + 1) * S5_BH]
            ore_ref[:, k * S5_BP:(k + 1) * S5_BP] = _dot(uk, bre_ref[k])
            oim_ref[:, k * S5_BP:(k + 1) * S5_BP] = _dot(uk, bim_ref[k])

    return pl.pallas_call(
        body, name="s5_in_fwd", grid=(L // tm,),
        in_specs=[_rows(tm, D), pl.BlockSpec((None, 8, D), lambda i: (layer, 0, 0)), _layer_w(D, D, 0),
                  _const_w((S5_NB, S5_BH, S5_BP)), _const_w((S5_NB, S5_BH, S5_BP))],
        out_specs=[_rows(tm, D), _rows(tm, D), _rows(tm, NSTATE), _rows(tm, NSTATE)],
        out_shape=[jax.ShapeDtypeStruct((L, D), BF16), jax.ShapeDtypeStruct((L, D), F32),
                   jax.ShapeDtypeStruct((L, NSTATE), F32), jax.ShapeDtypeStruct((L, NSTATE), F32)],
        compiler_params=_params(1, 56),
    )(x, pv, w_in, b_re, b_im)


def _s5_scan_fwd(bu_re, bu_im, ar, ai, tr):
    L = bu_re.shape[0]
    nl = 1024

    def body(bre_ref, bim_ref, ar_ref, ai_ref, sre_ref, sim_ref, st_re, st_im):
        @pl.when(pl.program_id(1) == 0)
        def _():
            st_re[...] = jnp.zeros_like(st_re)
            st_im[...] = jnp.zeros_like(st_im)

        a_r, a_i = ar_ref[...], ai_ref[...]

        def step(t, carry):
            s_r, s_i = carry
            n_r = a_r * s_r - a_i * s_i + bre_ref[pl.ds(t, 1), :]
            n_i = a_r * s_i + a_i * s_r + bim_ref[pl.ds(t, 1), :]
            sre_ref[pl.ds(t, 1), :] = n_r
            sim_ref[pl.ds(t, 1), :] = n_i
            return n_r, n_i

        s_r, s_i = lax.fori_loop(0, tr, step, (st_re[...], st_im[...]), unroll=8)
        st_re[...] = s_r
        st_im[...] = s_i

    blk = pl.BlockSpec((tr, nl), lambda j, i: (i, j))
    vec = pl.BlockSpec((1, nl), lambda j, i: (0, j))
    return pl.pallas_call(
        body, name="s5_scan_fwd", grid=(NSTATE // nl, L // tr),
        in_specs=[blk, blk, vec, vec], out_specs=[blk, blk],
        out_shape=[jax.ShapeDtypeStruct((L, NSTATE), F32)] * 2,
        scratch_shapes=[pltpu.VMEM((1, nl), F32), pltpu.VMEM((1, nl), F32)],
        compiler_params=_params(2),
    )(bu_re, bu_im, ar, ai)


def _s5_out_fwd(x, u, s_re, s_im, pv, c_re, c_im, dvec, glu_w, glu_b, w_out, layer, tm):
    L = x.shape[0]

    def body(x_ref, u_ref, sre_ref, sim_ref, pv_ref, cre_ref, cim_ref, d_ref, gw_ref, gb_ref, wout_ref,
             x1_ref, y1_ref, zg_ref, y3_ref, y_ref):
        p = pv_ref[...]
        srb, sib = _bf(sre_ref[...]), _bf(sim_ref[...])
        parts = []
        for k in range(S5_NB):
            sl = slice(k * S5_BP, (k + 1) * S5_BP)
            parts.append(_dot(srb[:, sl], cre_ref[k]) - _dot(sib[:, sl], cim_ref[k]))
        y1 = jnp.concatenate(parts, axis=1) + d_ref[...] * u_ref[...]
        y1_ref[...] = y1
        y2 = jax.nn.gelu(y1)
        zg = _dot(_bf(y2), gw_ref[...]) + gb_ref[...]
        zg_ref[...] = zg
        y3b = _bf(y2 * jax.nn.sigmoid(zg))
        y3_ref[...] = y3b
        y = _dot(y3b, wout_ref[...])
        y_ref[...] = y
        x1_ref[...] = x_ref[...] + p[R_G1:R_G1 + 1] * y

    return pl.pallas_call(
        body, name="s5_out_fwd", grid=(L // tm,),
        in_specs=[_rows(tm, D), _rows(tm, D), _rows(tm, NSTATE), _rows(tm, NSTATE),
                  pl.BlockSpec((None, 8, D), lambda i: (layer, 0, 0)),
                  _const_w((S5_NB, S5_BP, S5_BH)), _const_w((S5_NB, S5_BP, S5_BH)), _whole((1, D)),
                  _layer_w(D, D, 0), _whole((1, D)), _layer_w(D, D, 0)],
        out_specs=[_rows(tm, D)] * 5,
        out_shape=[jax.ShapeDtypeStruct((L, D), F32), jax.ShapeDtypeStruct((L, D), F32), jax.ShapeDtypeStruct((L, D), F32),
                   jax.ShapeDtypeStruct((L, D), BF16), jax.ShapeDtypeStruct((L, D), F32)],
        compiler_params=_params(1, 56),
    )(x, u, s_re, s_im, pv, c_re, c_im, dvec, glu_w, glu_b, w_out)


def _s5_out_bwd(dx1, y, y1, zg, u, pv, c_re, c_im, dvec, glu_w, w_out, layer, tm):
    L = dx1.shape[0]

    def body(dx1_ref, y_ref, y1_ref, zg_ref, u_ref, pv_ref, cre_ref, cim_ref, d_ref, gw_ref, wout_ref,
             dy_ref, y2_ref, dzg_ref, dy1_ref, dus_ref, gre_ref, gim_ref, vs_ref):
        @pl.when(pl.program_id(0) == 0)
        def _():
            vs_ref[...] = jnp.zeros_like(vs_ref)

        dx1v, p = dx1_ref[...], pv_ref[...]
        dyb = _bf(dx1v * p[R_G1:R_G1 + 1])
        dy_ref[...] = dyb
        vs_ref[0:1, :] += _sum0(dx1v * y_ref[...])
        dy3 = _dot_nt(dyb, wout_ref[...])
        y2, gelu_vjp = jax.vjp(jax.nn.gelu, y1_ref[...])
        y2_ref[...] = _bf(y2)
        gate = jax.nn.sigmoid(zg_ref[...])
        dzg = dy3 * y2 * gate * (1.0 - gate)
        dzgb = _bf(dzg)
        dzg_ref[...] = dzgb
        vs_ref[1:2, :] += _sum0(dzg)
        dy2 = dy3 * gate + _dot_nt(dzgb, gw_ref[...])
        dy1 = gelu_vjp(dy2)[0]
        vs_ref[2:3, :] += _sum0(dy1 * u_ref[...])
        dus_ref[...] = dy1 * d_ref[...]
        dy1b = _bf(dy1)
        dy1_ref[...] = dy1b
        for k in range(S5_NB):
            dk = dy1b[:, k * S5_BH:(k + 1) * S5_BH]
            gre_ref[:, k * S5_BP:(k + 1) * S5_BP] = _dot_nt(dk, cre_ref[k])
            gim_ref[:, k * S5_BP:(k + 1) * S5_BP] = -_dot_nt(dk, cim_ref[k])

    return pl.pallas_call(
        body, name="s5_out_bwd", grid=(L // tm,),
        in_specs=[_rows(tm, D)] * 5 + [pl.BlockSpec((None, 8, D), lambda i: (layer, 0, 0)),
                  _const_w((S5_NB, S5_BP, S5_BH)), _const_w((S5_NB, S5_BP, S5_BH)), _whole((1, D)),
                  _layer_w(D, D, 0), _layer_w(D, D, 0)],
        out_specs=[_rows(tm, D)] * 5 + [_rows(tm, NSTATE), _rows(tm, NSTATE), _whole((8, D))],
        out_shape=[jax.ShapeDtypeStruct((L, D), BF16)] * 4 + [jax.ShapeDtypeStruct((L, D), F32),
                   jax.ShapeDtypeStruct((L, NSTATE), F32), jax.ShapeDtypeStruct((L, NSTATE), F32),
                   jax.ShapeDtypeStruct((8, D), F32)],
        compiler_params=_params(1, 56),
    )(dx1, y, y1, zg, u, pv, c_re, c_im, dvec, glu_w, w_out)


def _s5_scan_bwd(g_re, g_im, s_re, s_im, ar, ai, tr):
    L = g_re.shape[0]
    nl = 1024
    nt = L // tr

    def body(gre_ref, gim_ref, sre_ref, sim_ref, hre_ref, him_ref, ar_ref, ai_ref, lre_ref, lim_ref, da_ref, st_re, st_im):
        gi = pl.program_id(1)
        tile = nt - 1 - gi

        @pl.when(gi == 0)
        def _():
            st_re[...] = jnp.zeros_like(st_re)
            st_im[...] = jnp.zeros_like(st_im)
            da_ref[...] = jnp.zeros_like(da_ref)

        a_r, a_i = ar_ref[...], ai_ref[...]

        def step(k, carry):
            l_r, l_i = carry
            t = tr - 1 - k
            n_r = gre_ref[pl.ds(t, 1), :] + a_r * l_r + a_i * l_i
            n_i = gim_ref[pl.ds(t, 1), :] - a_i * l_r + a_r * l_i
            lre_ref[pl.ds(t, 1), :] = n_r
            lim_ref[pl.ds(t, 1), :] = n_i
            return n_r, n_i

        l_r, l_i = lax.fori_loop(0, tr, step, (st_re[...], st_im[...]), unroll=8)
        st_re[...] = l_r
        st_im[...] = l_i
        lam_r, lam_i = lre_ref[...], lim_ref[...]
        p_r = jnp.where(tile > 0, hre_ref[...], 0.0)
        p_i = jnp.where(tile > 0, him_ref[...], 0.0)
        sp_r = _shift_down(sre_ref[...], p_r, 1)
        sp_i = _shift_down(sim_ref[...], p_i, 1)
        da_ref[0:1, :] += _sum0(lam_r * sp_r + lam_i * sp_i)
        da_ref[1:2, :] += _sum0(lam_i * sp_r - lam_r * sp_i)

    blk = pl.BlockSpec((tr, nl), lambda j, i: (nt - 1 - i, j))
    halo = pl.BlockSpec((8, nl), lambda j, i: (jnp.maximum((nt - 1 - i) * (tr // 8) - 1, 0), j))
    vec = pl.BlockSpec((1, nl), lambda j, i: (0, j))
    return pl.pallas_call(
        body, name="s5_scan_bwd", grid=(NSTATE // nl, nt),
        in_specs=[blk, blk, blk, blk, halo, halo, vec, vec],
        out_specs=[blk, blk, pl.BlockSpec((8, nl), lambda j, i: (0, j))],
        out_shape=[jax.ShapeDtypeStruct((L, NSTATE), F32)] * 2 + [jax.ShapeDtypeStruct((8, NSTATE), F32)],
        scratch_shapes=[pltpu.VMEM((1, nl), F32), pltpu.VMEM((1, nl), F32)],
        compiler_params=_params(2),
    )(g_re, g_im, s_re, s_im, s_re, s_im, ar, ai)


def _s5_in_bwd(dx1, lam_re, lam_im, du_skip, x, pv, b_re, b_im, w_in, layer, tm):
    L = x.shape[0]

    def body(dx1_ref, lre_ref, lim_ref, dus_ref, x_ref, pv_ref, bre_ref, bim_ref, win_ref, dx_ref, du_ref, vs_ref):
        @pl.when(pl.program_id(0) == 0)
        def _():
            vs_ref[...] = jnp.zeros_like(vs_ref)

        p = pv_ref[...]
        lrb, lib = _bf(lre_ref[...]), _bf(lim_ref[...])
        parts = []
        for k in range(S5_NB):
            sl = slice(k * S5_BP, (k + 1) * S5_BP)
            parts.append(_dot_nt(lrb[:, sl], bre_ref[k]) + _dot_nt(lib[:, sl], bim_ref[k]))
        dub = _bf(jnp.concatenate(parts, axis=1) + dus_ref[...])
        du_ref[...] = dub
        dh = _dot_nt(dub, win_ref[...])
        _, xn, r = _norm_mod(x_ref[...], p[R_N1:R_N1 + 1], p[R_SC1:R_SC1 + 1], p[R_SH1:R_SH1 + 1])
        dx_ref[...] = dx1_ref[...] + _norm_mod_bwd(dh, xn, r, p[R_N1:R_N1 + 1], p[R_SC1:R_SC1 + 1])
        vs_ref[1:2, :] += _sum0(dh * xn)
        vs_ref[2:3, :] += _sum0(dh)

    return pl.pallas_call(
        body, name="s5_in_bwd", grid=(L // tm,),
        in_specs=[_rows(tm, D), _rows(tm, NSTATE), _rows(tm, NSTATE), _rows(tm, D), _rows(tm, D),
                  pl.BlockSpec((None, 8, D), lambda i: (layer, 0, 0)),
                  _const_w((S5_NB, S5_BH, S5_BP)), _const_w((S5_NB, S5_BH, S5_BP)), _layer_w(D, D, 0)],
        out_specs=[_rows(tm, D), _rows(tm, D), _whole((8, D))],
        out_shape=[jax.ShapeDtypeStruct((L, D), F32), jax.ShapeDtypeStruct((L, D), BF16), jax.ShapeDtypeStruct((8, D), F32)],
        compiler_params=_params(1, 56),
    )(dx1, lam_re, lam_im, du_skip, x, pv, b_re, b_im, w_in)


def _blockdiag_b(bt):
    b = bt.reshape(S5_H, S5_NB, 16, S5_P).transpose(1, 2, 0, 3)
    eye = jnp.eye(16, dtype=bt.dtype)
    return (b[:, :, :, None, :] * eye[None, :, None, :, None]).reshape(S5_NB, S5_BH, S5_BP)


def _unblock_b(d):
    d = jnp.einsum("bghgp->bghp", d.reshape(S5_NB, 16, S5_H, 16, S5_P))
    return d.transpose(2, 0, 1, 3).reshape(S5_H, S5_G, S5_P)


def _blockdiag_c(cm):
    c4 = cm.reshape(S5_NB, 16, S5_H, S5_P)
    eye = jnp.eye(16, dtype=cm.dtype)
    out = c4.transpose(0, 1, 3, 2)[:, :, :, None, :] * eye[None, :, None, :, None]
    return out.reshape(S5_NB, S5_BP, S5_BH)


def _unblock_c(d):
    d = jnp.einsum("bgpgh->bghp", d.reshape(S5_NB, 16, S5_P, 16, S5_H))
    return d.reshape(S5_G, S5_H, S5_P)


def _tril_mask():
    return lax.broadcasted_iota(jnp.int32, (SG_CHUNK, SG_CHUNK), 0) >= lax.broadcasted_iota(jnp.int32, (SG_CHUNK, SG_CHUNK), 1)


def _sg_fwd(x, pv, w_in, w_s, b_t, vg, w_out, layer, tm):
    L = x.shape[0]
    nc = tm // SG_CHUNK

    def body(x_ref, pv_ref, win_ref, ws_ref, bt_ref, vg_ref, wout_ref, x1_ref, h_ref, uv_ref, vm_ref, q_ref, y_ref):
        xv, p = x_ref[...], pv_ref[...]
        h, _, _ = _norm_mod(xv, p[R_N1:R_N1 + 1], p[R_SC1:R_SC1 + 1], p[R_SH1:R_SH1 + 1])
        hb = _bf(h)
        h_ref[...] = hb
        uv = _dot(hb, win_ref[...])
        uv_ref[...] = uv
        v = uv[:, D:]
        rv = lax.rsqrt(jnp.mean(v * v, axis=-1, keepdims=True) + EPS)
        vnb = _bf((v * rv) * vg_ref[...])
        mask = _tril_mask()
        bt = bt_ref[...]
        for hd in range(SG_HEADS):
            wm = _bf(jnp.where(mask, ws_ref[hd], 0.0))
            cs = slice(hd * SG_CHUNK, (hd + 1) * SG_CHUNK)
            for ck in range(nc):
                rs = slice(ck * SG_CHUNK, (ck + 1) * SG_CHUNK)
                vm_ref[rs, cs] = _dot(wm, vnb[rs, cs]) + bt[:, hd:hd + 1]
        qb = _bf(uv[:, :D] * vm_ref[...])
        q_ref[...] = qb
        y = _dot(qb, wout_ref[...])
        y_ref[...] = y
        x1_ref[...] = xv + p[R_G1:R_G1 + 1] * y

    return pl.pallas_call(
        body, name="sg_fwd", grid=(L // tm,),
        in_specs=[_rows(tm, D), pl.BlockSpec((None, 8, D), lambda i: (layer, 0, 0)), _layer_w(D, 2 * D, 0),
                  _whole((SG_HEADS, SG_CHUNK, SG_CHUNK)), _whole((SG_CHUNK, SG_HEADS)), _whole((1, D)), _layer_w(D, D, 0)],
        out_specs=[_rows(tm, D), _rows(tm, D), _rows(tm, 2 * D), _rows(tm, D), _rows(tm, D), _rows(tm, D)],
        out_shape=[jax.ShapeDtypeStruct((L, D), F32), jax.ShapeDtypeStruct((L, D), BF16), jax.ShapeDtypeStruct((L, 2 * D), F32),
                   jax.ShapeDtypeStruct((L, D), F32), jax.ShapeDtypeStruct((L, D), BF16), jax.ShapeDtypeStruct((L, D), F32)],
        compiler_params=_params(1, 56),
    )(x, pv, w_in, w_s, b_t, vg, w_out)


def _sg_bwd(dx1, x, y, uv, vm, pv, w_in, w_s, vg, w_out, layer, tm):
    L = x.shape[0]
    nc = tm // SG_CHUNK

    def body(dx1_ref, x_ref, y_ref, uv_ref, vm_ref, pv_ref, win_ref, ws_ref, vg_ref, wout_ref,
             dx_ref, duv_ref, dy_ref, vs_ref, dws_ref, dbt_ref, dvn_scr):
        @pl.when(pl.program_id(0) == 0)
        def _():
            vs_ref[...] = jnp.zeros_like(vs_ref)
            dws_ref[...] = jnp.zeros_like(dws_ref)
            dbt_ref[...] = jnp.zeros_like(dbt_ref)

        dx1v, p = dx1_ref[...], pv_ref[...]
        dyb = _bf(dx1v * p[R_G1:R_G1 + 1])
        dy_ref[...] = dyb
        vs_ref[0:1, :] += _sum0(dx1v * y_ref[...])
        dq = _dot_nt(dyb, wout_ref[...])
        uv = uv_ref[...]
        u, v = uv[:, :D], uv[:, D:]
        dub = _bf(dq * vm_ref[...])
        dvm = dq * u
        dvmb = _bf(dvm)
        rv = lax.rsqrt(jnp.mean(v * v, axis=-1, keepdims=True) + EPS)
        vh = v * rv
        vgv = vg_ref[...]
        vnb = _bf(vh * vgv)
        mask = _tril_mask()
        for hd in range(SG_HEADS):
            wm = _bf(jnp.where(mask, ws_ref[hd], 0.0))
            cs = slice(hd * SG_CHUNK, (hd + 1) * SG_CHUNK)
            dws = jnp.zeros((SG_CHUNK, SG_CHUNK), F32)
            dbs = jnp.zeros((SG_CHUNK, 1), F32)
            for ck in range(nc):
                rs = slice(ck * SG_CHUNK, (ck + 1) * SG_CHUNK)
                dvn_scr[rs, cs] = _dot_tn(wm, dvmb[rs, cs])
                dws = dws + _dot_nt(dvmb[rs, cs], vnb[rs, cs])
                dbs = dbs + jnp.sum(dvm[rs, cs], axis=1, keepdims=True)
            dws_ref[hd] += jnp.where(mask, dws, 0.0)
            dbt_ref[:, hd:hd + 1] += dbs
        dvn = dvn_scr[...]
        vs_ref[3:4, :] += _sum0(dvn * vh)
        dvnn = dvn * vgv
        dvb = _bf(rv * (dvnn - vh * jnp.mean(dvnn * vh, axis=-1, keepdims=True)))
        duv_ref[:, 0:D] = dub
        duv_ref[:, D:2 * D] = dvb
        dh = _dot_nt(dub, win_ref[:, 0:D]) + _dot_nt(dvb, win_ref[:, D:2 * D])
        _, xn, r = _norm_mod(x_ref[...], p[R_N1:R_N1 + 1], p[R_SC1:R_SC1 + 1], p[R_SH1:R_SH1 + 1])
        dx_ref[...] = dx1v + _norm_mod_bwd(dh, xn, r, p[R_N1:R_N1 + 1], p[R_SC1:R_SC1 + 1])
        vs_ref[1:2, :] += _sum0(dh * xn)
        vs_ref[2:3, :] += _sum0(dh)

    return pl.pallas_call(
        body, name="sg_bwd", grid=(L // tm,),
        in_specs=[_rows(tm, D), _rows(tm, D), _rows(tm, D), _rows(tm, 2 * D), _rows(tm, D),
                  pl.BlockSpec((None, 8, D), lambda i: (layer, 0, 0)), _layer_w(D, 2 * D, 0),
                  _whole((SG_HEADS, SG_CHUNK, SG_CHUNK)), _whole((1, D)), _layer_w(D, D, 0)],
        out_specs=[_rows(tm, D), _rows(tm, 2 * D), _rows(tm, D), _whole((8, D)),
                   _whole((SG_HEADS, SG_CHUNK, SG_CHUNK)), _whole((SG_CHUNK, SG_HEADS))],
        out_shape=[jax.ShapeDtypeStruct((L, D), F32), jax.ShapeDtypeStruct((L, 2 * D), BF16), jax.ShapeDtypeStruct((L, D), BF16),
                   jax.ShapeDtypeStruct((8, D), F32), jax.ShapeDtypeStruct((SG_HEADS, SG_CHUNK, SG_CHUNK), F32),
                   jax.ShapeDtypeStruct((SG_CHUNK, SG_HEADS), F32)],
        scratch_shapes=[pltpu.VMEM((tm, D), F32)],
        compiler_params=_params(1, 56),
    )(dx1, x, y, uv, vm, pv, w_in, w_s, vg, w_out)


def _final(x, target, fg, tm):
    L = x.shape[0]

    def body(x_ref, t_ref, g_ref, dx_ref, vs_ref):
        @pl.when(pl.program_id(0) == 0)
        def _():
            vs_ref[...] = jnp.zeros_like(vs_ref)

        xv, g = x_ref[...], g_ref[...]
        r = lax.rsqrt(jnp.mean(xv * xv, axis=-1, keepdims=True) + EPS)
        xn = xv * r
        e = xn * g - t_ref[...]
        vs_ref[0:1, :] += jnp.sum(e * e)
        dout = e * (1.0 / D)
        vs_ref[1:2, :] += _sum0(dout * xn)
        dxn = dout * g
        dx_ref[...] = r * (dxn - xn * jnp.mean(dxn * xn, axis=-1, keepdims=True))

    return pl.pallas_call(
        body, name="final_loss", grid=(L // tm,),
        in_specs=[_rows(tm, D), _rows(tm, D), _whole((1, D))],
        out_specs=[_rows(tm, D), _whole((8, D))],
        out_shape=[jax.ShapeDtypeStruct((L, D), F32), jax.ShapeDtypeStruct((8, D), F32)],
        compiler_params=_params(1),
    )(x, target, fg)


def _pack(arrs):
    flat = jnp.concatenate([a.reshape(-1).astype(F32) for a in arrs])
    rows = -(-flat.shape[0] // LANES)
    rows = -(-rows // 8) * 8
    return jnp.pad(flat, (0, rows * LANES - flat.shape[0])).reshape(rows, LANES)


def _unpack(buf, shapes, lead=()):
    flat = buf.reshape(lead + (-1,))
    out, off = [], 0
    for s in shapes:
        n = 1
        for d in s:
            n *= d
        out.append(flat[..., off:off + n].reshape(lead + tuple(s)))
        off += n
    return out


BIG = ("ff_w1", "ff_w2", "conv_w_in", "conv_w_out", "ssm_w_in", "ssm_glu_w", "ssm_w_out", "sg_w_in", "sg_w_out")
BIG_AXIS = {"ff_w1": 2, "ff_w2": 1, "conv_w_in": 2, "conv_w_out": 1, "ssm_w_in": 1, "ssm_glu_w": 1, "ssm_w_out": 1,
            "sg_w_in": 2, "sg_w_out": 1}
LAYER_WEIGHTS = (
    (("conv_w_in", 0), ("conv_w_out", 0), ("ff_w1", 0), ("ff_w2", 0)),
    (("ssm_w_in", 0), ("ssm_glu_w", 0), ("ssm_w_out", 0), ("ff_w1", 1), ("ff_w2", 1)),
    (("sg_w_in", 0), ("sg_w_out", 0), ("ff_w1", 2), ("ff_w2", 2)),
    (("conv_w_in", 1), ("conv_w_out", 1), ("ff_w1", 3), ("ff_w2", 3)),
)
SMALL = ("ada_b", "norm1_g", "norm2_g", "final_g", "conv_w", "conv_b", "ssm_a_re", "ssm_a_im", "ssm_log_dt", "ssm_b_re",
         "ssm_b_im", "ssm_c_re", "ssm_c_im", "ssm_d", "ssm_glu_b", "sg_v_g", "sg_w_s", "sg_b_s")
WEIGHTS = ("ada_w", "ada_b", "norm1_g", "norm2_g", "ff_w1", "ff_w2", "final_g", "conv_w_in", "conv_w", "conv_b", "conv_w_out",
           "ssm_w_in", "ssm_a_re", "ssm_a_im", "ssm_log_dt", "ssm_b_re", "ssm_b_im", "ssm_c_re", "ssm_c_im", "ssm_d",
           "ssm_glu_w", "ssm_glu_b", "ssm_w_out", "sg_w_in", "sg_v_g", "sg_w_s", "sg_b_s", "sg_w_out")


def kernel(x, c, ada_w, ada_b, norm1_g, norm2_g, ff_w1, ff_w2, final_g, conv_w_in, conv_w, conv_b, conv_w_out, ssm_w_in, ssm_a_re, ssm_a_im, ssm_log_dt, ssm_b_re, ssm_b_im, ssm_c_re, ssm_c_im, ssm_d, ssm_glu_w, ssm_glu_b, ssm_w_out, sg_w_in, sg_v_g, sg_w_s, sg_b_s, sg_w_out, loss_target, m_ada_w, m_ada_b, m_norm1_g, m_norm2_g, m_ff_w1, m_ff_w2, m_final_g, m_conv_w_in, m_conv_w, m_conv_b, m_conv_w_out, m_ssm_w_in, m_ssm_a_re, m_ssm_a_im, m_ssm_log_dt, m_ssm_b_re, m_ssm_b_im, m_ssm_c_re, m_ssm_c_im, m_ssm_d, m_ssm_glu_w, m_ssm_glu_b, m_ssm_w_out, m_sg_w_in, m_sg_v_g, m_sg_w_s, m_sg_b_s, m_sg_w_out, v_ada_w, v_ada_b, v_norm1_g, v_norm2_g, v_ff_w1, v_ff_w2, v_final_g, v_conv_w_in, v_conv_w, v_conv_b, v_conv_w_out, v_ssm_w_in, v_ssm_a_re, v_ssm_a_im, v_ssm_log_dt, v_ssm_b_re, v_ssm_b_im, v_ssm_c_re, v_ssm_c_im, v_ssm_d, v_ssm_glu_w, v_ssm_glu_b, v_ssm_w_out, v_sg_w_in, v_sg_v_g, v_sg_w_s, v_sg_b_s, v_sg_w_out):
    args = dict(locals())
    w = {n: args[n] for n in WEIGHTS}
    m = {n: args["m_" + n] for n in WEIGHTS}
    v = {n: args["v_" + n] for n in WEIGHTS}
    L = x.shape[1]
    tm = min(L, 256)
    tr = min(L, 512)
    chip = 2 * lax.axis_index("x") + lax.axis_index("y")
    me = 2 * chip + lax.axis_index("c")
    xin = x[0]
    target = loss_target[0]
    chip1 = chip.reshape(1).astype(jnp.int32)

    gathers = []
    token = jnp.zeros((8, 128), F32)
    for i in range(DEPTH):
        shards = [w[n][li:li + 1].astype(BF16) for n, li in LAYER_WEIGHTS[i]]
        axes = [BIG_AXIS[n] for n, _ in LAYER_WEIGHTS[i]]
        s_sems, r_sems, shards, lands, token = _gather_start(shards, axes, f"gather_start{i}", token)
        gathers.append((s_sems, r_sems, shards, lands, axes))

    def weights_of(i, after):
        s_sems, r_sems, shards, lands, axes = gathers[i]
        lands = _gather_wait(s_sems, r_sems, shards, lands, axes, f"gather_wait{i}", after)
        return dict(zip([n for n, _ in LAYER_WEIGHTS[i]], _gather_share(shards, lands, axes, f"gather_share{i}")))

    small_in = _pack([c, conv_w, conv_b, sg_v_g]) + token[0:1, 0:1]
    got = _allgather_small(small_in, "gather_small_inputs").reshape(N_DEV, -1)
    c_all, cw_sh, cb_sh, vg_sh = _unpack(got, [(D,), conv_w.shape, conv_b.shape, sg_v_g.shape], lead=(N_DEV,))
    conv_w_full = jnp.concatenate([cw_sh[2 * k] for k in range(4)], axis=-1)
    conv_b_full = jnp.concatenate([cb_sh[2 * k] for k in range(4)], axis=-1)
    vg_full = jnp.concatenate([vg_sh[2 * k] for k in range(4)], axis=-1)
    c16 = jnp.pad(c_all, ((0, 16 - N_DEV), (0, 0)))

    cols = ada_w.shape[2]
    ada_b_cols = lax.dynamic_slice_in_dim(ada_b, chip * cols, cols, axis=1)[:, None, :]
    mod_sh = _ada_fwd(c16, ada_w, ada_b_cols)[:, :N_DEV, :]
    mod_all = _allgather_small(_pack([mod_sh]), "gather_mod").reshape(N_DEV, -1)
    mod_all = _unpack(mod_all, [mod_sh.shape], lead=(N_DEV,))[0]
    mod_mine = lax.dynamic_index_in_dim(mod_all[0::2], me, axis=2, keepdims=False)
    mod_mine = mod_mine.transpose(1, 0, 2).reshape(DEPTH, 6, D)
    pv = jnp.concatenate([mod_mine, norm1_g[:, None, :], norm2_g[:, None, :]], axis=1)

    cw_rows = jnp.concatenate([conv_w_full, conv_b_full[:, None, :], jnp.zeros((conv_w_full.shape[0], 4, D), F32)], axis=1)

    a_re, a_im = ssm_a_re[0], ssm_a_im[0]
    log_dt = ssm_log_dt[0][:, None]
    bt_re, bt_im = ssm_b_re[0].transpose(2, 0, 1), ssm_b_im[0].transpose(2, 0, 1)
    abar_re, abar_im, bbar_re, bbar_im = _s5_params_fwd(a_re, a_im, log_dt, bt_re, bt_im)
    ar_vec, ai_vec = abar_re.reshape(1, NSTATE), abar_im.reshape(1, NSTATE)
    bd_re, bd_im = _bf(_blockdiag_b(bbar_re)), _bf(_blockdiag_b(bbar_im))
    cd_re, cd_im = _bf(_blockdiag_c(ssm_c_re[0])), _bf(_blockdiag_c(ssm_c_im[0]))

    saved = []
    fulls = []
    xl = xin
    for i in range(DEPTH):
        kind = MIXER_OF_LAYER[i]
        j = i // 3
        full = weights_of(i, pv if i == 0 else xl)
        fulls.append(full)
        if kind == 0:
            x1, h, bcx, conv, q, y = _conv_fwd(xl, pv, full["conv_w_in"], full["conv_w_out"], cw_rows, i, j, tm)
            mix = dict(h=h, bcx=bcx, conv=conv, q=q, y=y)
        elif kind == 1:
            h, u, bu_re, bu_im = _s5_in_fwd(xl, pv, full["ssm_w_in"], bd_re, bd_im, i, tm)
            s_re, s_im = _s5_scan_fwd(bu_re, bu_im, ar_vec, ai_vec, tr)
            x1, y1, zg, y3, y = _s5_out_fwd(xl, u, s_re, s_im, pv, cd_re, cd_im, ssm_d, full["ssm_glu_w"], ssm_glu_b,
                                            full["ssm_w_out"], i, tm)
            mix = dict(h=h, u=u, s_re=s_re, s_im=s_im, y1=y1, zg=zg, y3=y3, y=y)
        else:
            x1, h, uv, vm, q, y = _sg_fwd(xl, pv, full["sg_w_in"], sg_w_s[0], sg_b_s[0].T, vg_full, full["sg_w_out"], i, tm)
            mix = dict(h=h, uv=uv, vm=vm, q=q, y=y)
        x2, h2, a, f = _ffn_fwd(x1, pv, full["ff_w1"], full["ff_w2"], i, tm)
        saved.append(dict(x=xl, x1=x1, h2=h2, a=a, f=f, **mix))
        xl = x2

    dxl, vs_fin = _final(xl, target, final_g[None, :], tm)

    gfull = {n: [None] * w[n].shape[0] for n in BIG}
    vs_mix, vs_ffn = [None] * DEPTH, [None] * DEPTH
    small_g = {}
    scatters = [None] * DEPTH
    token = None
    for i in reversed(range(DEPTH)):
        kind = MIXER_OF_LAYER[i]
        j = i // 3
        sv = saved[i]
        full = fulls[i]
        dx1, p_b, da_b, df_b, vs_ffn[i] = _ffn_bwd(dxl, sv["x1"], sv["a"], sv["f"], pv, full["ff_w1"], full["ff_w2"], i, tm,
                                                   after=token)
        gfull["ff_w1"][i] = _mm_tn(sv["h2"], da_b, f"wgrad_ff_w1_{i}")
        gfull["ff_w2"][i] = _mm_tn(p_b, df_b, f"wgrad_ff_w2_{i}")
        if kind == 0:
            dxl, dbcx_b, dy_b, vsm = _conv_bwd(dx1, sv["x"], sv["y"], sv["bcx"], sv["conv"], pv, full["conv_w_in"],
                                               full["conv_w_out"], cw_rows, i, j, tm)
            gfull["conv_w_in"][j] = _mm_tn(sv["h"], dbcx_b, f"wgrad_conv_w_in_{j}")
            gfull["conv_w_out"][j] = _mm_tn(sv["q"], dy_b, f"wgrad_conv_w_out_{j}")
            small_g.setdefault("conv_w", [None, None])[j] = vsm[3:6]
            small_g.setdefault("conv_b", [None, None])[j] = vsm[6]
        elif kind == 1:
            dy_b, y2_b, dzg_b, dy1_b, du_skip, g_re, g_im, vsm = _s5_out_bwd(
                dx1, sv["y"], sv["y1"], sv["zg"], sv["u"], pv, cd_re, cd_im, ssm_d, full["ssm_glu_w"], full["ssm_w_out"], i, tm)
            lam_re, lam_im, dabar = _s5_scan_bwd(g_re, g_im, sv["s_re"], sv["s_im"], ar_vec, ai_vec, tr)
            dxl, du_b, vs_in = _s5_in_bwd(dx1, lam_re, lam_im, du_skip, sv["x"], pv, bd_re, bd_im, full["ssm_w_in"], i, tm)
            gfull["ssm_w_out"][0] = _mm_tn(sv["y3"], dy_b, "wgrad_ssm_w_out")
            gfull["ssm_glu_w"][0] = _mm_tn(y2_b, dzg_b, "wgrad_ssm_glu_w")
            gfull["ssm_w_in"][0] = _mm_tn(sv["h"], du_b, "wgrad_ssm_w_in")
            d_cre = _unblock_c(_mm_tn_blocks(sv["s_re"], dy1_b, S5_BP, S5_BH, "wgrad_s5_c_re"))
            d_cim = -_unblock_c(_mm_tn_blocks(sv["s_im"], dy1_b, S5_BP, S5_BH, "wgrad_s5_c_im"))
            d_bbre = _unblock_b(_mm_tn_blocks(sv["u"], lam_re, S5_BH, S5_BP, "wgrad_s5_b_re"))
            d_bbim = _unblock_b(_mm_tn_blocks(sv["u"], lam_im, S5_BH, S5_BP, "wgrad_s5_b_im"))
            d_are, d_aim, d_ldt, d_btre, d_btim = _s5_params_bwd(
                a_re, a_im, log_dt, bt_re, bt_im, dabar[0].reshape(S5_G, S5_P), dabar[1].reshape(S5_G, S5_P), d_bbre, d_bbim)
            small_g.update(ssm_a_re=d_are, ssm_a_im=d_aim, ssm_log_dt=d_ldt, ssm_b_re=d_btre.transpose(1, 2, 0),
                           ssm_b_im=d_btim.transpose(1, 2, 0), ssm_c_re=d_cre, ssm_c_im=d_cim, ssm_d=vsm[2], ssm_glu_b=vsm[1])
            vsm = jnp.concatenate([vsm[0:1], vs_in[1:3], jnp.zeros((5, D), F32)], axis=0)
        else:
            dxl, duv_b, dy_b, vsm, d_ws, d_bt = _sg_bwd(dx1, sv["x"], sv["y"], sv["uv"], sv["vm"], pv, full["sg_w_in"],
                                                        sg_w_s[0], vg_full, full["sg_w_out"], i, tm)
            gfull["sg_w_in"][0] = _mm_tn(sv["h"], duv_b, "wgrad_sg_w_in")
            gfull["sg_w_out"][0] = _mm_tn(sv["q"], dy_b, "wgrad_sg_w_out")
            small_g.update(sg_v_g=vsm[3], sg_w_s=d_ws, sg_b_s=d_bt.T)
        vs_mix[i] = vsm
        garrs = [gfull[n][li][None] for n, li in LAYER_WEIGHTS[i]]
        gaxes = [BIG_AXIS[n] for n, _ in LAYER_WEIGHTS[i]]
        s_sems, r_sems, garrs, lands, token = _scatter_start(garrs, gaxes, f"scatter_start{i}")
        scatters[i] = (s_sems, r_sems, garrs, lands, gaxes)
    grad_x = dxl[None]

    dmod = _mod_bwd(jnp.stack(vs_mix), jnp.stack(vs_ffn), pv)
    small_g.update(ada_b=dmod[:, :6, :], norm1_g=dmod[:, 6, :], norm2_g=dmod[:, 7, :], final_g=vs_fin[1],
                   conv_w=jnp.stack(small_g["conv_w"]), conv_b=jnp.stack(small_g["conv_b"]))

    loss_part = (0.5 / D) * vs_fin[0, 0:1]
    part_shapes = [(1,)] + [tuple(small_g[n].shape) for n in SMALL]
    parts_all, parts_sum = _allgather_small(_pack([loss_part] + [small_g[n] for n in SMALL]), "reduce_small_grads", with_sum=True)
    summed = _unpack(parts_sum, part_shapes)
    loss = summed[0][0]
    gsum = dict(zip(SMALL, summed[1:]))
    dmod_all = _unpack(parts_all.reshape(N_DEV, -1), part_shapes[:2], lead=(N_DEV,))[1]
    dmod_all = dmod_all.reshape(N_DEV, DEPTH, 6 * D)
    dmod_cols = lax.dynamic_slice_in_dim(dmod_all, chip * cols, cols, axis=2).transpose(1, 0, 2)
    g_ada_w = _ada_bwd(c16, jnp.pad(dmod_cols, ((0, 0), (0, 16 - N_DEV), (0, 0))))

    res = {}
    shp = ada_w.shape
    two = lambda t: t.reshape(shp[0] * shp[1], shp[2])
    res["ada_w"] = [t.reshape(shp) for t in _adamw(two(ada_w), [two(g_ada_w)], two(m_ada_w), two(v_ada_w), "adamw_ada_w")]

    def mine(n, g):
        if n in ("conv_w", "conv_b", "sg_v_g"):
            size = w[n].shape[-1]
            return lax.dynamic_slice_in_dim(g, chip * size, size, axis=g.ndim - 1)
        return g

    g_loc = [mine(n, gsum[n]).reshape(w[n].shape) for n in SMALL]
    small_shapes = [tuple(w[n].shape) for n in SMALL]
    packed = _adamw(_pack([w[n] for n in SMALL]), [_pack(g_loc)], _pack([m[n] for n in SMALL]), _pack([v[n] for n in SMALL]),
                    "adamw_small")
    unpacked = [_unpack(t, small_shapes) for t in packed]
    for k, n in enumerate(SMALL):
        res[n] = [unpacked[0][k], unpacked[1][k], unpacked[2][k], unpacked[3][k]]

    sums = {n: [None] * w[n].shape[0] for n in BIG}
    after = packed[0]
    for i in reversed(range(DEPTH)):
        s_sems, r_sems, garrs, lands, gaxes = scatters[i]
        garrs, recv = _scatter_wait(s_sems, r_sems, garrs, lands, gaxes, f"scatter_wait{i}", after)
        for (n, li), g, r3, ax in zip(LAYER_WEIGHTS[i], garrs, recv, gaxes):
            sums[n][li] = _sum_parts(r3, g, ax, chip1, f"sum_{n}_{li}")
            after = sums[n][li]
    q_mine = [jnp.stack(sums[n]) for n in BIG]
    q_sib = _swap_with_sibling(q_mine, "swap_grad_sums")
    for n, qa, qb in zip(BIG, q_mine, q_sib):
        shp = w[n].shape
        two = lambda t, shp=shp: t.reshape(shp[0] * shp[1], shp[2])
        res[n] = [t.reshape(shp) for t in _adamw(two(w[n]), [two(qa), two(qb)], two(m[n]), two(v[n]), f"adamw_{n}")]

    outs = [loss, grad_x]
    for part in range(4):
        outs += [res[n][part] for n in WEIGHTS]
    return tuple(outs)
```

```python
import functools

import jax
import jax.numpy as jnp
from jax import lax
from jax.experimental import pallas as pl
from jax.experimental.pallas import tpu as pltpu

F32 = jnp.float32
BF16 = jnp.bfloat16
D = 1024
EPS = 1e-6
DEPTH = 4
MIXER_OF_LAYER = (0, 1, 2, 0)
S5_G, S5_H, S5_P = 64, 16, 64
S5_NB = 4
S5_BH = S5_H * 16
S5_BP = S5_P * 16
NSTATE = S5_G * S5_P
SG_HEADS, SG_CHUNK = 8, 128
ADAM_LR, ADAM_B1, ADAM_B2, ADAM_EPS, ADAM_WD, ADAM_STEP = 0.001, 0.9, 0.999, 1e-08, 0.01, 10
N_DEV = 8
MESH = pl.DeviceIdType.MESH
LANES = 1024
R_SH1, R_SC1, R_G1, R_SH2, R_SC2, R_G2, R_N1, R_N2 = range(8)


def _dot(a, b):
    return jnp.dot(a, b, preferred_element_type=F32)


def _dot_nt(a, b):
    return lax.dot_general(a, b, (((1,), (1,)), ((), ())), preferred_element_type=F32)


def _dot_tn(a, b):
    return lax.dot_general(a, b, (((0,), (0,)), ((), ())), preferred_element_type=F32)


def _bf(x):
    return x.astype(BF16)


def _sum0(x):
    return jnp.sum(x, axis=0, keepdims=True)


def _params(n_axes, vmem_mb=48):
    return pltpu.CompilerParams(dimension_semantics=("arbitrary",) * n_axes, vmem_limit_bytes=vmem_mb << 20)


def _rows(tm, cols, nt=None):
    if nt is None:
        return pl.BlockSpec((tm, cols), lambda i: (i, 0))
    return pl.BlockSpec((tm, cols), lambda i: (nt - 1 - i, 0))


def _whole(shape):
    nd = len(shape)
    return pl.BlockSpec(shape, lambda *_: (0,) * nd)


def _layer_w(r, c, layer):
    return pl.BlockSpec((None, r, c), lambda *_: (layer, 0, 0), pipeline_mode=pl.Buffered(1))


def _const_w(shape):
    nd = len(shape)
    return pl.BlockSpec(shape, lambda *_: (0,) * nd, pipeline_mode=pl.Buffered(1))


def _call_after(after, body, n_in, *, in_specs, **kw):
    if after is None:
        return pl.pallas_call(body, in_specs=in_specs, **kw), ()

    def body_after(*refs):
        return body(*refs[:n_in], *refs[n_in + 1:])

    return pl.pallas_call(body_after, in_specs=list(in_specs) + [pl.BlockSpec(memory_space=pl.ANY)], **kw), (after,)


def _norm_mod(x, ng, sc, sh):
    r = lax.rsqrt(jnp.mean(x * x, axis=-1, keepdims=True) + EPS)
    xn = x * r
    return (xn * ng) * (1.0 + sc) + sh, xn, r


def _norm_mod_bwd(dh, xn, r, ng, sc):
    dxn = dh * (ng * (1.0 + sc))
    return r * (dxn - xn * jnp.mean(dxn * xn, axis=-1, keepdims=True))


def _shift_down(z, prev8, k):
    row = lax.broadcasted_iota(jnp.int32, z.shape, 0)
    if k == 1:
        return jnp.where(row >= 1, pltpu.roll(z, 1, 0), prev8[7:8])
    return jnp.where(row >= 2, pltpu.roll(z, 2, 0), jnp.where(row == 0, prev8[6:7], prev8[7:8]))


def _shift_up(z, next8, k):
    n = z.shape[0]
    row = lax.broadcasted_iota(jnp.int32, z.shape, 0)
    if k == 1:
        return jnp.where(row <= n - 2, pltpu.roll(z, n - 1, 0), next8[0:1])
    return jnp.where(row <= n - 3, pltpu.roll(z, n - 2, 0), jnp.where(row == n - 2, next8[0:1], next8[1:2]))


def _place():
    x, y, c = lax.axis_index("x"), lax.axis_index("y"), lax.axis_index("c")
    chips = [(1 - x, y), (x, 1 - y), (1 - x, 1 - y)]
    return x, y, c, chips


def _allgather_small(x_shard, name, with_sum=False):
    m_per, n = x_shard.shape

    def body(x_ref, *rest):
        if with_sum:
            out_ref, sum_ref, send_sems, recv_sems, local_sem = rest
        else:
            out_ref, send_sems, recv_sems, local_sem = rest
        x, y, c, chips = _place()
        me, sibling = (x, y, c), (x, y, 1 - c)

        def rows(px, py, pc):
            return out_ref.at[pl.ds((4 * px + 2 * py + pc) * m_per, m_per), :]

        def copy(k, block, to, src=None):
            return pltpu.make_async_remote_copy(
                src_ref=rows(*block) if src is None else src, dst_ref=rows(*block),
                send_sem=send_sems.at[k], recv_sem=recv_sems.at[k], device_id=to, device_id_type=MESH)

        mine = pltpu.make_async_copy(x_ref, rows(*me), local_sem)
        mine.start()
        first = [copy(0, me, sibling, src=x_ref)]
        first += [copy(1 + j, me, (*chip, c), src=x_ref) for j, chip in enumerate(chips)]
        for cp in first:
            cp.start()
        passed = [copy(4 + j, (*chip, c), sibling) for j, chip in enumerate(chips)]
        for j, chip in enumerate(chips):
            copy(1 + j, (*chip, c), me).wait_recv()
            passed[j].start()
        copy(0, sibling, me).wait_recv()
        for j, chip in enumerate(chips):
            copy(4 + j, (*chip, 1 - c), me).wait_recv()
        for cp in first + passed:
            cp.wait_send()
        mine.wait()
        if with_sum:
            acc = out_ref[0:m_per, :]
            for d in range(1, N_DEV):
                acc = acc + out_ref[d * m_per:(d + 1) * m_per, :]
            sum_ref[...] = acc

    out_shape = [jax.ShapeDtypeStruct((N_DEV * m_per, n), F32)]
    out_specs = [pl.BlockSpec(memory_space=pltpu.VMEM)]
    if with_sum:
        out_shape.append(jax.ShapeDtypeStruct((m_per, n), F32))
        out_specs.append(pl.BlockSpec(memory_space=pltpu.VMEM))
    res = pl.pallas_call(
        body, name=name, out_shape=out_shape,
        in_specs=[pl.BlockSpec(memory_space=pltpu.VMEM)], out_specs=out_specs,
        scratch_shapes=[pltpu.SemaphoreType.DMA((7,)), pltpu.SemaphoreType.DMA((7,)), pltpu.SemaphoreType.DMA],
        compiler_params=pltpu.CompilerParams(vmem_limit_bytes=48 << 20),
    )(x_shard)
    return tuple(res) if with_sum else res[0]


def _shard_region(ref, full_shape, axis, chip_k, half=None):
    _, r, c = full_shape
    if axis == 1:
        rs = r // 4
        if half is None:
            return ref.at[:, pl.ds(pl.multiple_of(chip_k * rs, 128), rs), :]
        return ref.at[:, pl.ds(pl.multiple_of(chip_k * rs + half * (rs // 2), 128), rs // 2), :]
    cs = c // 4
    if half is None:
        return ref.at[:, :, pl.ds(pl.multiple_of(chip_k * cs, 128), cs)]
    return ref.at[:, pl.ds(pl.multiple_of(half * (r // 2), 128), r // 2), pl.ds(pl.multiple_of(chip_k * cs, 128), cs)]


HBM_SPEC = pl.BlockSpec(memory_space=pltpu.HBM)
SEM_SPEC = pl.BlockSpec(memory_space=pltpu.SEMAPHORE)
ANY_SPEC = pl.BlockSpec(memory_space=pl.ANY)
SPLIT_COPY_PARAMS = pltpu.CompilerParams(has_side_effects=pltpu.SideEffectType.DATAFLOW_SIDE_EFFECTING)


def _in_hbm(arrs):
    return [pltpu.with_memory_space_constraint(a, pltpu.HBM) for a in arrs]


def _full_shapes(shards, axes):
    fulls = []
    for s, ax in zip(shards, axes):
        shp = list(s.shape)
        shp[ax] *= 4
        fulls.append(tuple(shp))
    return fulls


def _half_of_shard(ref, rows, half):
    return ref.at[:, pl.ds(pl.multiple_of(half * (rows // 2), 64), rows // 2), :]


def _gather_start(shards, axes, name, after):
    n_arr = len(shards)
    fulls = _full_shapes(shards, axes)
    lands = [lax.empty(f, BF16) for f in fulls]

    def body(*refs):
        ins, land = refs[:n_arr], refs[n_arr:2 * n_arr]
        send_sems, recv_sems = refs[2 * n_arr + 1:2 * n_arr + 3]
        token = refs[-1]
        x, y, c, chips = _place()
        k_me = 2 * x + y
        for a in range(n_arr):
            for j, chip in enumerate(chips):
                pltpu.make_async_remote_copy(
                    src_ref=_half_of_shard(ins[a], shards[a].shape[1], c),
                    dst_ref=_shard_region(land[a], fulls[a], axes[a], k_me, c),
                    send_sem=send_sems.at[a * 3 + j], recv_sem=recv_sems.at[a * 3 + j],
                    device_id=(*chip, c), device_id_type=MESH).start()
        token[...] = jnp.zeros_like(token)

    res = pl.pallas_call(
        body, name=name,
        out_shape=(pltpu.SemaphoreType.DMA((3 * n_arr,)), pltpu.SemaphoreType.DMA((3 * n_arr,)),
                   *[pltpu.HBM(s.shape, BF16) for s in shards], *[pltpu.HBM(f, BF16) for f in fulls],
                   jax.ShapeDtypeStruct((8, 128), F32)),
        in_specs=[HBM_SPEC] * (2 * n_arr) + [ANY_SPEC],
        out_specs=(SEM_SPEC, SEM_SPEC, *[HBM_SPEC] * (2 * n_arr), pl.BlockSpec(memory_space=pltpu.VMEM)),
        input_output_aliases={a: 2 + a for a in range(2 * n_arr)},
        compiler_params=SPLIT_COPY_PARAMS,
    )(*_in_hbm(shards), *_in_hbm(lands), after)
    return res[0], res[1], list(res[2:2 + n_arr]), list(res[2 + n_arr:2 + 2 * n_arr]), res[-1]


def _gather_wait(send_sems, recv_sems, shards, lands, axes, name, after):
    n_arr = len(shards)
    fulls = [tuple(l.shape) for l in lands]

    def body(*refs):
        ins, land = refs[:n_arr], refs[n_arr:2 * n_arr]
        s_sems, r_sems = refs[2 * n_arr:2 * n_arr + 2]
        x, y, c, chips = _place()
        for a in range(n_arr):
            for j, chip in enumerate(chips):
                k_j = 2 * chip[0] + chip[1]
                cp = pltpu.make_async_remote_copy(
                    src_ref=_half_of_shard(ins[a], shards[a].shape[1], c),
                    dst_ref=_shard_region(land[a], fulls[a], axes[a], k_j, c),
                    send_sem=s_sems.at[a * 3 + j], recv_sem=r_sems.at[a * 3 + j],
                    device_id=(x, y, c), device_id_type=MESH)
                cp.wait_send()
                cp.wait_recv()

    res = pl.pallas_call(
        body, name=name,
        out_shape=(*[pltpu.HBM(s.shape, BF16) for s in shards], *[pltpu.HBM(f, BF16) for f in fulls]),
        in_specs=[HBM_SPEC] * (2 * n_arr) + [SEM_SPEC, SEM_SPEC, ANY_SPEC],
        out_specs=tuple([HBM_SPEC] * (2 * n_arr)),
        input_output_aliases={a: a for a in range(2 * n_arr)},
        compiler_params=SPLIT_COPY_PARAMS,
    )(*shards, *lands, send_sems, recv_sems, after)
    return list(res[n_arr:])


def _gather_share(shards, lands, axes, name):
    n_arr = len(shards)
    fulls = [tuple(l.shape) for l in lands]

    def body(*refs):
        ins, land_in, land = refs[:n_arr], refs[n_arr:2 * n_arr], refs[2 * n_arr:3 * n_arr]
        send_sems, recv_sems, local_sems = refs[3 * n_arr:]
        x, y, c, chips = _place()
        k_me = 2 * x + y
        copies = []
        for a in range(n_arr):
            lc = pltpu.make_async_copy(ins[a], _shard_region(land[a], fulls[a], axes[a], k_me), local_sems.at[a])
            lc.start()
            copies.append(lc)
            for j, chip in enumerate(chips):
                k_j = 2 * chip[0] + chip[1]
                cp = pltpu.make_async_remote_copy(
                    src_ref=_shard_region(land_in[a], fulls[a], axes[a], k_j, c),
                    dst_ref=_shard_region(land[a], fulls[a], axes[a], k_j, c),
                    send_sem=send_sems.at[a * 3 + j], recv_sem=recv_sems.at[a * 3 + j],
                    device_id=(x, y, 1 - c), device_id_type=MESH)
                cp.start()
                copies.append(cp)
        for cp in copies:
            cp.wait()

    return pl.pallas_call(
        body, name=name, out_shape=[jax.ShapeDtypeStruct(f, BF16) for f in fulls],
        in_specs=[ANY_SPEC] * (2 * n_arr), out_specs=[ANY_SPEC] * n_arr,
        input_output_aliases={n_arr + a: a for a in range(n_arr)},
        scratch_shapes=[pltpu.SemaphoreType.DMA((3 * n_arr,)), pltpu.SemaphoreType.DMA((3 * n_arr,)),
                        pltpu.SemaphoreType.DMA((n_arr,))],
    )(*shards, *lands)


def _scatter_shapes(grads, axes):
    out = []
    for g, ax in zip(grads, axes):
        shp = list(g.shape)
        shp[ax] //= 4
        out.append((3,) + tuple(shp[1:]))
    return out


def _scatter_start(grads, axes, name, after):
    n_arr = len(grads)
    shapes = _scatter_shapes(grads, axes)
    lands = [lax.empty(s, BF16) for s in shapes]

    def body(*refs):
        ins, land = refs[:n_arr], refs[n_arr:2 * n_arr]
        send_sems, recv_sems = refs[2 * n_arr + 1:2 * n_arr + 3]
        token = refs[-1]
        x, y, c, chips = _place()
        for a in range(n_arr):
            for j, chip in enumerate(chips):
                k_j = 2 * chip[0] + chip[1]
                pltpu.make_async_remote_copy(
                    src_ref=_shard_region(ins[a], grads[a].shape, axes[a], k_j), dst_ref=land[a].at[pl.ds(j, 1)],
                    send_sem=send_sems.at[a * 3 + j], recv_sem=recv_sems.at[a * 3 + j],
                    device_id=(*chip, c), device_id_type=MESH).start()
        token[...] = jnp.zeros_like(token)

    res = pl.pallas_call(
        body, name=name,
        out_shape=(pltpu.SemaphoreType.DMA((3 * n_arr,)), pltpu.SemaphoreType.DMA((3 * n_arr,)),
                   *[pltpu.HBM(g.shape, BF16) for g in grads], *[pltpu.HBM(s, BF16) for s in shapes],
                   jax.ShapeDtypeStruct((8, 128), F32)),
        in_specs=[HBM_SPEC] * (2 * n_arr) + [ANY_SPEC],
        out_specs=(SEM_SPEC, SEM_SPEC, *[HBM_SPEC] * (2 * n_arr), pl.BlockSpec(memory_space=pltpu.VMEM)),
        input_output_aliases={a: 2 + a for a in range(2 * n_arr)},
        compiler_params=SPLIT_COPY_PARAMS,
    )(*_in_hbm(grads), *_in_hbm(lands), after)
    return res[0], res[1], list(res[2:2 + n_arr]), list(res[2 + n_arr:2 + 2 * n_arr]), res[-1]


def _scatter_wait(send_sems, recv_sems, grads, lands, axes, name, after):
    n_arr = len(grads)

    def body(*refs):
        ins, land = refs[:n_arr], refs[n_arr:2 * n_arr]
        s_sems, r_sems = refs[2 * n_arr:2 * n_arr + 2]
        x, y, c, chips = _place()
        for a in range(n_arr):
            for j, chip in enumerate(chips):
                k_j = 2 * chip[0] + chip[1]
                cp = pltpu.make_async_remote_copy(
                    src_ref=_shard_region(ins[a], grads[a].shape, axes[a], k_j), dst_ref=land[a].at[pl.ds(j, 1)],
                    send_sem=s_sems.at[a * 3 + j], recv_sem=r_sems.at[a * 3 + j],
                    device_id=(x, y, c), device_id_type=MESH)
                cp.wait_send()
                cp.wait_recv()

    res = pl.pallas_call(
        body, name=name,
        out_shape=(*[pltpu.HBM(g.shape, BF16) for g in grads], *[pltpu.HBM(l.shape, BF16) for l in lands]),
        in_specs=[HBM_SPEC] * (2 * n_arr) + [SEM_SPEC, SEM_SPEC, ANY_SPEC],
        out_specs=tuple([HBM_SPEC] * (2 * n_arr)),
        input_output_aliases={a: a for a in range(2 * n_arr)},
        compiler_params=SPLIT_COPY_PARAMS,
    )(*grads, *lands, send_sems, recv_sems, after)
    return list(res[:n_arr]), list(res[n_arr:])


def _swap_with_sibling(arrs, name):
    n_arr = len(arrs)

    def body(*refs):
        ins, outs = refs[:n_arr], refs[n_arr:2 * n_arr]
        send_sems, recv_sems = refs[2 * n_arr:]
        x, y, c, _ = _place()
        copies = []
        for a in range(n_arr):
            cp = pltpu.make_async_remote_copy(
                src_ref=ins[a], dst_ref=outs[a], send_sem=send_sems.at[a], recv_sem=recv_sems.at[a],
                device_id=(x, y, 1 - c), device_id_type=MESH)
            cp.start()
            copies.append(cp)
        for cp in copies:
            cp.wait()

    any_spec = pl.BlockSpec(memory_space=pl.ANY)
    return pl.pallas_call(
        body, name=name, out_shape=[jax.ShapeDtypeStruct(a.shape, a.dtype) for a in arrs],
        in_specs=[any_spec] * n_arr, out_specs=[any_spec] * n_arr,
        scratch_shapes=[pltpu.SemaphoreType.DMA((n_arr,)), pltpu.SemaphoreType.DMA((n_arr,))],
    )(*arrs)


def _mm_tn(a, b, name, out_dtype=BF16):
    L, m = a.shape
    n = b.shape[1]
    bm, bn, bk = min(m, 1024), min(n, 1024), min(L, 512)
    nk = L // bk

    def body(a_ref, b_ref, o_ref, acc):
        k = pl.program_id(2)

        @pl.when(k == 0)
        def _():
            acc[...] = jnp.zeros_like(acc)

        acc[...] += _dot_tn(_bf(a_ref[...]), _bf(b_ref[...]))

        @pl.when(k == nk - 1)
        def _():
            o_ref[...] = acc[...].astype(out_dtype)

    return pl.pallas_call(
        body, name=name, grid=(m // bm, n // bn, nk),
        in_specs=[pl.BlockSpec((bk, bm), lambda i, j, k: (k, i)), pl.BlockSpec((bk, bn), lambda i, j, k: (k, j))],
        out_specs=pl.BlockSpec((bm, bn), lambda i, j, k: (i, j)),
        out_shape=jax.ShapeDtypeStruct((m, n), out_dtype),
        scratch_shapes=[pltpu.VMEM((bm, bn), F32)],
        compiler_params=_params(3),
    )(a, b)


def _mm_tn_blocks(a, b, wa, wb, name):
    L = a.shape[0]
    nb = a.shape[1] // wa
    bk = min(L, 512)
    nk = L // bk

    def body(a_ref, b_ref, o_ref):
        @pl.when(pl.program_id(1) == 0)
        def _():
            o_ref[...] = jnp.zeros_like(o_ref)

        o_ref[...] += _dot_tn(_bf(a_ref[...]), _bf(b_ref[...]))

    return pl.pallas_call(
        body, name=name, grid=(nb, nk),
        in_specs=[pl.BlockSpec((bk, wa), lambda j, k: (k, j)), pl.BlockSpec((bk, wb), lambda j, k: (k, j))],
        out_specs=pl.BlockSpec((None, wa, wb), lambda j, k: (j, 0, 0)),
        out_shape=jax.ShapeDtypeStruct((nb, wa, wb), F32),
        compiler_params=_params(2),
    )(a, b)


def _sum_parts(parts, own, axis, chip, name):
    _, r, c = parts.shape
    tr = min(r, 256)
    if axis == 1:
        own_spec = pl.BlockSpec((None, tr, c), lambda i, k: (0, k[0] * (r // tr) + i, 0))
    else:
        own_spec = pl.BlockSpec((None, tr, c), lambda i, k: (0, i, k[0]))

    def body(k_ref, p_ref, g_ref, o_ref):
        p = p_ref[...].astype(F32)
        o_ref[...] = ((p[0] + p[1]) + p[2]) + g_ref[...].astype(F32)

    return pl.pallas_call(
        body, name=name,
        grid_spec=pltpu.PrefetchScalarGridSpec(
            num_scalar_prefetch=1, grid=(r // tr,),
            in_specs=[pl.BlockSpec((3, tr, c), lambda i, k: (0, i, 0)), own_spec],
            out_specs=pl.BlockSpec((tr, c), lambda i, k: (i, 0))),
        out_shape=jax.ShapeDtypeStruct((r, c), F32),
        compiler_params=_params(1),
    )(chip, parts, own)


def _adamw(w, g_parts, m, v, name):
    r, c = w.shape
    tr = r
    for cand in (512, 256, 128, 64, 32, 16, 8):
        if r % cand == 0 and cand * c * 4 <= (2 << 20):
            tr = cand
            break
    n_g = len(g_parts)
    c1 = 1.0 / (1.0 - ADAM_B1 ** ADAM_STEP)
    c2 = 1.0 / (1.0 - ADAM_B2 ** ADAM_STEP)

    def body(*refs):
        w_ref, g_refs, m_ref, v_ref = refs[0], refs[1:1 + n_g], refs[1 + n_g], refs[2 + n_g]
        g_out, d_out, m_out, v_out = refs[3 + n_g:]
        g = g_refs[0][...]
        for gr in g_refs[1:]:
            g = g + gr[...]
        m_new = ADAM_B1 * m_ref[...] + (1.0 - ADAM_B1) * g
        v_new = ADAM_B2 * v_ref[...] + (1.0 - ADAM_B2) * (g * g)
        m_hat = m_new * c1
        v_hat = v_new * c2
        g_out[...] = g
        d_out[...] = -ADAM_LR * (m_hat / (jnp.sqrt(v_hat) + ADAM_EPS) + ADAM_WD * w_ref[...])
        m_out[...] = m_new
        v_out[...] = v_new

    spec = pl.BlockSpec((tr, c), lambda i: (i, 0))
    return pl.pallas_call(
        body, name=name, grid=(r // tr,),
        in_specs=[spec] * (3 + n_g), out_specs=[spec] * 4,
        out_shape=[jax.ShapeDtypeStruct((r, c), F32)] * 4,
        compiler_params=_params(1),
    )(w, *g_parts, m, v)


def _ada_fwd(c16, ada_w, ada_b_cols):
    cols = ada_w.shape[2]

    def body(c_ref, w_ref, b_ref, o_ref):
        cv = c_ref[...]
        ca = _bf(cv * jax.nn.sigmoid(cv))
        o_ref[...] = _dot(ca, _bf(w_ref[...])) + b_ref[...]

    return pl.pallas_call(
        body, name="ada_fwd", grid=(DEPTH,),
        in_specs=[_whole((16, D)), pl.BlockSpec((None, D, cols), lambda i: (i, 0, 0)),
                  pl.BlockSpec((None, 1, cols), lambda i: (i, 0, 0))],
        out_specs=pl.BlockSpec((None, 16, cols), lambda i: (i, 0, 0)),
        out_shape=jax.ShapeDtypeStruct((DEPTH, 16, cols), F32),
        compiler_params=_params(1),
    )(c16, ada_w, ada_b_cols)


def _ada_bwd(c16, dmod16):
    cols = dmod16.shape[2]

    def body(c_ref, d_ref, o_ref):
        cv = c_ref[...]
        ca = _bf(cv * jax.nn.sigmoid(cv))
        o_ref[...] = _dot_tn(ca, _bf(d_ref[...]))

    return pl.pallas_call(
        body, name="ada_bwd", grid=(DEPTH,),
        in_specs=[_whole((16, D)), pl.BlockSpec((None, 16, cols), lambda i: (i, 0, 0))],
        out_specs=pl.BlockSpec((None, D, cols), lambda i: (i, 0, 0)),
        out_shape=jax.ShapeDtypeStruct((DEPTH, D, cols), F32),
        compiler_params=_params(1),
    )(c16, dmod16)


def _mod_bwd(vs_mix, vs_ffn, pv):
    def body(m_ref, f_ref, pv_ref, o_ref):
        for i in range(DEPTH):
            vm, vf, p = m_ref[i], f_ref[i], pv_ref[i]
            o_ref[i] = jnp.concatenate([
                vm[2:3], vm[1:2] * p[R_N1:R_N1 + 1], vm[0:1],
                vf[2:3], vf[1:2] * p[R_N2:R_N2 + 1], vf[0:1],
                vm[1:2] * (1.0 + p[R_SC1:R_SC1 + 1]), vf[1:2] * (1.0 + p[R_SC2:R_SC2 + 1])], axis=0)

    return pl.pallas_call(body, name="mod_bwd", out_shape=jax.ShapeDtypeStruct((DEPTH, 8, D), F32))(vs_mix, vs_ffn, pv)


def _ffn_fwd(x1, pv, w1, w2, layer, tm):
    L = x1.shape[0]
    dff = w1.shape[2]

    def body(x1_ref, pv_ref, w1_ref, w2_ref, x2_ref, h2_ref, a_ref, f_ref):
        x1v, p = x1_ref[...], pv_ref[...]
        h2, _, _ = _norm_mod(x1v, p[R_N2:R_N2 + 1], p[R_SC2:R_SC2 + 1], p[R_SH2:R_SH2 + 1])
        hb = _bf(h2)
        h2_ref[...] = hb
        a = _dot(hb, w1_ref[...])
        a_ref[...] = a
        ra = jnp.maximum(a, 0.0)
        f = _dot(_bf(ra * ra), w2_ref[...])
        f_ref[...] = f
        x2_ref[...] = x1v + p[R_G2:R_G2 + 1] * f

    return pl.pallas_call(
        body, name=f"ffn_fwd{layer}", grid=(L // tm,),
        in_specs=[_rows(tm, D), pl.BlockSpec((None, 8, D), lambda i: (layer, 0, 0)), _layer_w(D, dff, 0), _layer_w(dff, D, 0)],
        out_specs=[_rows(tm, D), _rows(tm, D), _rows(tm, dff), _rows(tm, D)],
        out_shape=[jax.ShapeDtypeStruct((L, D), F32), jax.ShapeDtypeStruct((L, D), BF16),
                   jax.ShapeDtypeStruct((L, dff), F32), jax.ShapeDtypeStruct((L, D), F32)],
        compiler_params=_params(1, 56),
    )(x1, pv, w1, w2)


def _ffn_bwd(dx2, x1, a, f, pv, w1, w2, layer, tm, after=None):
    L = x1.shape[0]
    dff = w1.shape[2]
    extra = [] if after is None else [pl.BlockSpec(memory_space=pl.ANY)]
    extra_args = [] if after is None else [after]

    def body(dx2_ref, x1_ref, a_ref, f_ref, pv_ref, w1_ref, w2_ref, *rest):
        dx1_ref, p_ref, da_ref, df_ref, vs_ref = rest[len(extra):]

        @pl.when(pl.program_id(0) == 0)
        def _():
            vs_ref[...] = jnp.zeros_like(vs_ref)

        dx2v, p = dx2_ref[...], pv_ref[...]
        dfb = _bf(dx2v * p[R_G2:R_G2 + 1])
        df_ref[...] = dfb
        vs_ref[0:1, :] += _sum0(dx2v * f_ref[...])
        dp = _dot_nt(dfb, w2_ref[...])
        ra = jnp.maximum(a_ref[...], 0.0)
        p_ref[...] = _bf(ra * ra)
        dab = _bf(dp * (2.0 * ra))
        da_ref[...] = dab
        dh2 = _dot_nt(dab, w1_ref[...])
        _, xn, r = _norm_mod(x1_ref[...], p[R_N2:R_N2 + 1], p[R_SC2:R_SC2 + 1], p[R_SH2:R_SH2 + 1])
        dx1_ref[...] = dx2v + _norm_mod_bwd(dh2, xn, r, p[R_N2:R_N2 + 1], p[R_SC2:R_SC2 + 1])
        vs_ref[1:2, :] += _sum0(dh2 * xn)
        vs_ref[2:3, :] += _sum0(dh2)

    return pl.pallas_call(
        body, name=f"ffn_bwd{layer}", grid=(L // tm,),
        in_specs=[_rows(tm, D), _rows(tm, D), _rows(tm, dff), _rows(tm, D),
                  pl.BlockSpec((None, 8, D), lambda i: (layer, 0, 0)), _layer_w(D, dff, 0), _layer_w(dff, D, 0)] + extra,
        out_specs=[_rows(tm, D), _rows(tm, dff), _rows(tm, dff), _rows(tm, D), _whole((8, D))],
        out_shape=[jax.ShapeDtypeStruct((L, D), F32), jax.ShapeDtypeStruct((L, dff), BF16),
                   jax.ShapeDtypeStruct((L, dff), BF16), jax.ShapeDtypeStruct((L, D), BF16),
                   jax.ShapeDtypeStruct((8, D), F32)],
        compiler_params=_params(1, 56),
    )(dx2, x1, a, f, pv, w1, w2, *extra_args)


def _conv_fwd(x, pv, w_in, w_out, cw, layer, j, tm, after=None):
    L = x.shape[0]

    def body(x_ref, pv_ref, win_ref, wout_ref, cw_ref, x1_ref, h_ref, bcx_ref, conv_ref, q_ref, y_ref, carry):
        @pl.when(pl.program_id(0) == 0)
        def _():
            carry[...] = jnp.zeros_like(carry)

        xv, p, cwv = x_ref[...], pv_ref[...], cw_ref[...]
        h, _, _ = _norm_mod(xv, p[R_N1:R_N1 + 1], p[R_SC1:R_SC1 + 1], p[R_SH1:R_SH1 + 1])
        hb = _bf(h)
        h_ref[...] = hb
        bcx = _dot(hb, win_ref[...])
        bcx_ref[...] = bcx
        z = bcx[:, D:2 * D] * bcx[:, 2 * D:]
        prev8 = carry[...]
        conv = cwv[0:1] * _shift_down(z, prev8, 2) + cwv[1:2] * _shift_down(z, prev8, 1) + cwv[2:3] * z + cwv[3:4]
        conv_ref[...] = conv
        qb = _bf(bcx[:, :D] * conv)
        q_ref[...] = qb
        y = _dot(qb, wout_ref[...])
        y_ref[...] = y
        x1_ref[...] = xv + p[R_G1:R_G1 + 1] * y
        carry[...] = z[tm - 8:tm]

    call, tail = _call_after(
        after, body, 5, name=f"conv_fwd{layer}", grid=(L // tm,),
        in_specs=[_rows(tm, D), pl.BlockSpec((None, 8, D), lambda i: (layer, 0, 0)), _layer_w(D, 3 * D, 0), _layer_w(D, D, 0),
                  pl.BlockSpec((None, 8, D), lambda i: (j, 0, 0))],
        out_specs=[_rows(tm, D), _rows(tm, D), _rows(tm, 3 * D), _rows(tm, D), _rows(tm, D), _rows(tm, D)],
        out_shape=[jax.ShapeDtypeStruct((L, D), F32), jax.ShapeDtypeStruct((L, D), BF16), jax.ShapeDtypeStruct((L, 3 * D), F32),
                   jax.ShapeDtypeStruct((L, D), F32), jax.ShapeDtypeStruct((L, D), BF16), jax.ShapeDtypeStruct((L, D), F32)],
        scratch_shapes=[pltpu.VMEM((8, D), F32)],
        compiler_params=_params(1, 56),
    )
    return call(x, pv, w_in, w_out, cw, *tail)


def _conv_bwd(dx1, x, y, bcx, conv, pv, w_in, w_out, cw, layer, j, tm, after=None):
    L = x.shape[0]
    nt = L // tm

    def body(dx1_ref, x_ref, y_ref, bcx_ref, conv_ref, halo_ref, pv_ref, win_ref, wout_ref, cw_ref,
             dx_ref, dbcx_ref, dy_ref, vs_ref, carry):
        gi = pl.program_id(0)
        tile = nt - 1 - gi

        @pl.when(gi == 0)
        def _():
            vs_ref[...] = jnp.zeros_like(vs_ref)
            carry[...] = jnp.zeros_like(carry)

        dx1v, p, cwv = dx1_ref[...], pv_ref[...], cw_ref[...]
        dyb = _bf(dx1v * p[R_G1:R_G1 + 1])
        dy_ref[...] = dyb
        vs_ref[0:1, :] += _sum0(dx1v * y_ref[...])
        dq = _dot_nt(dyb, wout_ref[...])
        bcx = bcx_ref[...]
        b, cg, xh = bcx[:, :D], bcx[:, D:2 * D], bcx[:, 2 * D:]
        db = dq * conv_ref[...]
        dc = dq * b
        z = cg * xh
        halo = halo_ref[...]
        zprev = jnp.where(tile > 0, halo[:, D:2 * D] * halo[:, 2 * D:], 0.0)
        vs_ref[3:4, :] += _sum0(dc * _shift_down(z, zprev, 2))
        vs_ref[4:5, :] += _sum0(dc * _shift_down(z, zprev, 1))
        vs_ref[5:6, :] += _sum0(dc * z)
        vs_ref[6:7, :] += _sum0(dc)
        next8 = carry[...]
        dz = cwv[2:3] * dc + cwv[1:2] * _shift_up(dc, next8, 1) + cwv[0:1] * _shift_up(dc, next8, 2)
        dbb, dcgb, dxhb = _bf(db), _bf(dz * xh), _bf(dz * cg)
        dbcx_ref[:, 0:D] = dbb
        dbcx_ref[:, D:2 * D] = dcgb
        dbcx_ref[:, 2 * D:3 * D] = dxhb
        dh = (_dot_nt(dbb, win_ref[:, 0:D]) + _dot_nt(dcgb, win_ref[:, D:2 * D])) + _dot_nt(dxhb, win_ref[:, 2 * D:3 * D])
        _, xn, r = _norm_mod(x_ref[...], p[R_N1:R_N1 + 1], p[R_SC1:R_SC1 + 1], p[R_SH1:R_SH1 + 1])
        dx_ref[...] = dx1v + _norm_mod_bwd(dh, xn, r, p[R_N1:R_N1 + 1], p[R_SC1:R_SC1 + 1])
        vs_ref[1:2, :] += _sum0(dh * xn)
        vs_ref[2:3, :] += _sum0(dh)
        carry[...] = dc[0:8]

    halo_spec = pl.BlockSpec((8, 3 * D), lambda i: (jnp.maximum((nt - 1 - i) * (tm // 8) - 1, 0), 0))
    call, tail = _call_after(
        after, body, 10, name=f"conv_bwd{layer}", grid=(nt,),
        in_specs=[_rows(tm, D, nt), _rows(tm, D, nt), _rows(tm, D, nt), _rows(tm, 3 * D, nt), _rows(tm, D, nt), halo_spec,
                  pl.BlockSpec((None, 8, D), lambda i: (layer, 0, 0)), _layer_w(D, 3 * D, 0), _layer_w(D, D, 0),
                  pl.BlockSpec((None, 8, D), lambda i: (j, 0, 0))],
        out_specs=[_rows(tm, D, nt), _rows(tm, 3 * D, nt), _rows(tm, D, nt), _whole((8, D))],
        out_shape=[jax.ShapeDtypeStruct((L, D), F32), jax.ShapeDtypeStruct((L, 3 * D), BF16),
                   jax.ShapeDtypeStruct((L, D), BF16), jax.ShapeDtypeStruct((8, D), F32)],
        scratch_shapes=[pltpu.VMEM((8, D), F32)],
        compiler_params=_params(1, 56),
    )
    return call(dx1, x, y, bcx, conv, bcx, pv, w_in, w_out, cw, *tail)


def _s5_discretize(a_re, a_im, log_dt, bt_re, bt_im):
    dt = jnp.exp(log_dt)
    mag = jnp.exp(a_re * dt)
    abar_re = mag * jnp.cos(a_im * dt)
    abar_im = mag * jnp.sin(a_im * dt)
    den = a_re * a_re + a_im * a_im
    nr = abar_re - 1.0
    ni = abar_im
    f_re = (nr * a_re + ni * a_im) / den
    f_im = (ni * a_re - nr * a_im) / den
    bbar_re = f_re * bt_re - f_im * bt_im
    bbar_im = f_re * bt_im + f_im * bt_re
    return abar_re, abar_im, bbar_re, bbar_im


def _s5_params_fwd(a_re, a_im, log_dt, bt_re, bt_im):
    def body(ar, ai, ld, br, bi, o_ar, o_ai, o_br, o_bi):
        r = _s5_discretize(ar[...], ai[...], ld[...], br[...], bi[...])
        o_ar[...], o_ai[...], o_br[...], o_bi[...] = r

    gp = jax.ShapeDtypeStruct((S5_G, S5_P), F32)
    hgp = jax.ShapeDtypeStruct((S5_H, S5_G, S5_P), F32)
    return pl.pallas_call(body, name="s5_params_fwd", out_shape=[gp, gp, hgp, hgp])(a_re, a_im, log_dt, bt_re, bt_im)


def _s5_params_bwd(a_re, a_im, log_dt, bt_re, bt_im, d_ar, d_ai, d_br, d_bi):
    def body(ar, ai, ld, br, bi, gar, gai, gbr, gbi, o_ar, o_ai, o_ld, o_br, o_bi):
        _, vjp = jax.vjp(_s5_discretize, ar[...], ai[...], ld[...], br[...], bi[...])
        r = vjp((gar[...], gai[...], gbr[...], gbi[...]))
        o_ar[...], o_ai[...], o_ld[...], o_br[...], o_bi[...] = r

    gp = jax.ShapeDtypeStruct((S5_G, S5_P), F32)
    hgp = jax.ShapeDtypeStruct((S5_H, S5_G, S5_P), F32)
    return pl.pallas_call(body, name="s5_params_bwd", out_shape=[gp, gp, jax.ShapeDtypeStruct((S5_G, 1), F32), hgp, hgp])(
        a_re, a_im, log_dt, bt_re, bt_im, d_ar, d_ai, d_br, d_bi)


def _s5_in_fwd(x, pv, w_in, b_re, b_im, layer, tm, after=None):
    L = x.shape[0]

    def body(x_ref, pv_ref, win_ref, bre_ref, bim_ref, h_ref, u_ref, ore_ref, oim_ref):
        p = pv_ref[...]
        h, _, _ = _norm_mod(x_ref[...], p[R_N1:R_N1 + 1], p[R_SC1:R_SC1 + 1], p[R_SH1:R_SH1 + 1])
        hb = _bf(h)
        h_ref[...] = hb
        u = _dot(hb, win_ref[...])
        u_ref[...] = u
        ub = _bf(u)
        for k in range(S5_NB):
            uk = ub[:, k * S5_BH:(k + 1) * S5_BH]
            ore_ref[:, k * S5_BP:(k + 1) * S5_BP] = _dot(uk, bre_ref[k])
            oim_ref[:, k * S5_BP:(k + 1) * S5_BP] = _dot(uk, bim_ref[k])

    call, tail = _call_after(
        after, body, 5, name="s5_in_fwd", grid=(L // tm,),
        in_specs=[_rows(tm, D), pl.BlockSpec((None, 8, D), lambda i: (layer, 0, 0)), _layer_w(D, D, 0),
                  _const_w((S5_NB, S5_BH, S5_BP)), _const_w((S5_NB, S5_BH, S5_BP))],
        out_specs=[_rows(tm, D), _rows(tm, D), _rows(tm, NSTATE), _rows(tm, NSTATE)],
        out_shape=[jax.ShapeDtypeStruct((L, D), BF16), jax.ShapeDtypeStruct((L, D), F32),
                   jax.ShapeDtypeStruct((L, NSTATE), F32), jax.ShapeDtypeStruct((L, NSTATE), F32)],
        compiler_params=_params(1, 56),
    )
    return call(x, pv, w_in, b_re, b_im, *tail)


def _s5_scan_fwd(bu_re, bu_im, ar, ai, tr):
    L = bu_re.shape[0]
    nl = 1024

    def body(bre_ref, bim_ref, ar_ref, ai_ref, sre_ref, sim_ref, st_re, st_im):
        @pl.when(pl.program_id(1) == 0)
        def _():
            st_re[...] = jnp.zeros_like(st_re)
            st_im[...] = jnp.zeros_like(st_im)

        a_r, a_i = ar_ref[...], ai_ref[...]

        def step(t, carry):
            s_r, s_i = carry
            n_r = a_r * s_r - a_i * s_i + bre_ref[pl.ds(t, 1), :]
            n_i = a_r * s_i + a_i * s_r + bim_ref[pl.ds(t, 1), :]
            sre_ref[pl.ds(t, 1), :] = n_r
            sim_ref[pl.ds(t, 1), :] = n_i
            return n_r, n_i

        s_r, s_i = lax.fori_loop(0, tr, step, (st_re[...], st_im[...]), unroll=8)
        st_re[...] = s_r
        st_im[...] = s_i

    blk = pl.BlockSpec((tr, nl), lambda j, i: (i, j))
    vec = pl.BlockSpec((1, nl), lambda j, i: (0, j))
    return pl.pallas_call(
        body, name="s5_scan_fwd", grid=(NSTATE // nl, L // tr),
        in_specs=[blk, blk, vec, vec], out_specs=[blk, blk],
        out_shape=[jax.ShapeDtypeStruct((L, NSTATE), F32)] * 2,
        scratch_shapes=[pltpu.VMEM((1, nl), F32), pltpu.VMEM((1, nl), F32)],
        compiler_params=_params(2),
    )(bu_re, bu_im, ar, ai)


def _s5_out_fwd(x, u, s_re, s_im, pv, c_re, c_im, dvec, glu_w, glu_b, w_out, layer, tm):
    L = x.shape[0]

    def body(x_ref, u_ref, sre_ref, sim_ref, pv_ref, cre_ref, cim_ref, d_ref, gw_ref, gb_ref, wout_ref,
             x1_ref, y1_ref, zg_ref, y3_ref, y_ref):
        p = pv_ref[...]
        srb, sib = _bf(sre_ref[...]), _bf(sim_ref[...])
        parts = []
        for k in range(S5_NB):
            sl = slice(k * S5_BP, (k + 1) * S5_BP)
            parts.append(_dot(srb[:, sl], cre_ref[k]) - _dot(sib[:, sl], cim_ref[k]))
        y1 = jnp.concatenate(parts, axis=1) + d_ref[...] * u_ref[...]
        y1_ref[...] = y1
        y2 = jax.nn.gelu(y1)
        zg = _dot(_bf(y2), gw_ref[...]) + gb_ref[...]
        zg_ref[...] = zg
        y3b = _bf(y2 * jax.nn.sigmoid(zg))
        y3_ref[...] = y3b
        y = _dot(y3b, wout_ref[...])
        y_ref[...] = y
        x1_ref[...] = x_ref[...] + p[R_G1:R_G1 + 1] * y

    return pl.pallas_call(
        body, name="s5_out_fwd", grid=(L // tm,),
        in_specs=[_rows(tm, D), _rows(tm, D), _rows(tm, NSTATE), _rows(tm, NSTATE),
                  pl.BlockSpec((None, 8, D), lambda i: (layer, 0, 0)),
                  _const_w((S5_NB, S5_BP, S5_BH)), _const_w((S5_NB, S5_BP, S5_BH)), _whole((1, D)),
                  _layer_w(D, D, 0), _whole((1, D)), _layer_w(D, D, 0)],
        out_specs=[_rows(tm, D)] * 5,
        out_shape=[jax.ShapeDtypeStruct((L, D), F32), jax.ShapeDtypeStruct((L, D), F32), jax.ShapeDtypeStruct((L, D), F32),
                   jax.ShapeDtypeStruct((L, D), BF16), jax.ShapeDtypeStruct((L, D), F32)],
        compiler_params=_params(1, 56),
    )(x, u, s_re, s_im, pv, c_re, c_im, dvec, glu_w, glu_b, w_out)


def _s5_out_bwd(dx1, y, y1, zg, u, pv, c_re, c_im, dvec, glu_w, w_out, layer, tm, after=None):
    L = dx1.shape[0]

    def body(dx1_ref, y_ref, y1_ref, zg_ref, u_ref, pv_ref, cre_ref, cim_ref, d_ref, gw_ref, wout_ref,
             dy_ref, y2_ref, dzg_ref, dy1_ref, dus_ref, gre_ref, gim_ref, vs_ref):
        @pl.when(pl.program_id(0) == 0)
        def _():
            vs_ref[...] = jnp.zeros_like(vs_ref)

        dx1v, p = dx1_ref[...], pv_ref[...]
        dyb = _bf(dx1v * p[R_G1:R_G1 + 1])
        dy_ref[...] = dyb
        vs_ref[0:1, :] += _sum0(dx1v * y_ref[...])
        dy3 = _dot_nt(dyb, wout_ref[...])
        y2, gelu_vjp = jax.vjp(jax.nn.gelu, y1_ref[...])
        y2_ref[...] = _bf(y2)
        gate = jax.nn.sigmoid(zg_ref[...])
        dzg = dy3 * y2 * gate * (1.0 - gate)
        dzgb = _bf(dzg)
        dzg_ref[...] = dzgb
        vs_ref[1:2, :] += _sum0(dzg)
        dy2 = dy3 * gate + _dot_nt(dzgb, gw_ref[...])
        dy1 = gelu_vjp(dy2)[0]
        vs_ref[2:3, :] += _sum0(dy1 * u_ref[...])
        dus_ref[...] = dy1 * d_ref[...]
        dy1b = _bf(dy1)
        dy1_ref[...] = dy1b
        for k in range(S5_NB):
            dk = dy1b[:, k * S5_BH:(k + 1) * S5_BH]
            gre_ref[:, k * S5_BP:(k + 1) * S5_BP] = _dot_nt(dk, cre_ref[k])
            gim_ref[:, k * S5_BP:(k + 1) * S5_BP] = -_dot_nt(dk, cim_ref[k])

    call, tail = _call_after(
        after, body, 11, name="s5_out_bwd", grid=(L // tm,),
        in_specs=[_rows(tm, D)] * 5 + [pl.BlockSpec((None, 8, D), lambda i: (layer, 0, 0)),
                  _const_w((S5_NB, S5_BP, S5_BH)), _const_w((S5_NB, S5_BP, S5_BH)), _whole((1, D)),
                  _layer_w(D, D, 0), _layer_w(D, D, 0)],
        out_specs=[_rows(tm, D)] * 5 + [_rows(tm, NSTATE), _rows(tm, NSTATE), _whole((8, D))],
        out_shape=[jax.ShapeDtypeStruct((L, D), BF16)] * 4 + [jax.ShapeDtypeStruct((L, D), F32),
                   jax.ShapeDtypeStruct((L, NSTATE), F32), jax.ShapeDtypeStruct((L, NSTATE), F32),
                   jax.ShapeDtypeStruct((8, D), F32)],
        compiler_params=_params(1, 56),
    )
    return call(dx1, y, y1, zg, u, pv, c_re, c_im, dvec, glu_w, w_out, *tail)


def _s5_scan_bwd(g_re, g_im, s_re, s_im, ar, ai, tr):
    L = g_re.shape[0]
    nl = 1024
    nt = L // tr

    def body(gre_ref, gim_ref, sre_ref, sim_ref, hre_ref, him_ref, ar_ref, ai_ref, lre_ref, lim_ref, da_ref, st_re, st_im):
        gi = pl.program_id(1)
        tile = nt - 1 - gi

        @pl.when(gi == 0)
        def _():
            st_re[...] = jnp.zeros_like(st_re)
            st_im[...] = jnp.zeros_like(st_im)
            da_ref[...] = jnp.zeros_like(da_ref)

        a_r, a_i = ar_ref[...], ai_ref[...]

        def step(k, carry):
            l_r, l_i = carry
            t = tr - 1 - k
            n_r = gre_ref[pl.ds(t, 1), :] + a_r * l_r + a_i * l_i
            n_i = gim_ref[pl.ds(t, 1), :] - a_i * l_r + a_r * l_i
            lre_ref[pl.ds(t, 1), :] = n_r
            lim_ref[pl.ds(t, 1), :] = n_i
            return n_r, n_i

        l_r, l_i = lax.fori_loop(0, tr, step, (st_re[...], st_im[...]), unroll=8)
        st_re[...] = l_r
        st_im[...] = l_i
        lam_r, lam_i = lre_ref[...], lim_ref[...]
        p_r = jnp.where(tile > 0, hre_ref[...], 0.0)
        p_i = jnp.where(tile > 0, him_ref[...], 0.0)
        sp_r = _shift_down(sre_ref[...], p_r, 1)
        sp_i = _shift_down(sim_ref[...], p_i, 1)
        da_ref[0:1, :] += _sum0(lam_r * sp_r + lam_i * sp_i)
        da_ref[1:2, :] += _sum0(lam_i * sp_r - lam_r * sp_i)

    blk = pl.BlockSpec((tr, nl), lambda j, i: (nt - 1 - i, j))
    halo = pl.BlockSpec((8, nl), lambda j, i: (jnp.maximum((nt - 1 - i) * (tr // 8) - 1, 0), j))
    vec = pl.BlockSpec((1, nl), lambda j, i: (0, j))
    return pl.pallas_call(
        body, name="s5_scan_bwd", grid=(NSTATE // nl, nt),
        in_specs=[blk, blk, blk, blk, halo, halo, vec, vec],
        out_specs=[blk, blk, pl.BlockSpec((8, nl), lambda j, i: (0, j))],
        out_shape=[jax.ShapeDtypeStruct((L, NSTATE), F32)] * 2 + [jax.ShapeDtypeStruct((8, NSTATE), F32)],
        scratch_shapes=[pltpu.VMEM((1, nl), F32), pltpu.VMEM((1, nl), F32)],
        compiler_params=_params(2),
    )(g_re, g_im, s_re, s_im, s_re, s_im, ar, ai)


def _s5_in_bwd(dx1, lam_re, lam_im, du_skip, x, pv, b_re, b_im, w_in, layer, tm):
    L = x.shape[0]

    def body(dx1_ref, lre_ref, lim_ref, dus_ref, x_ref, pv_ref, bre_ref, bim_ref, win_ref, dx_ref, du_ref, vs_ref):
        @pl.when(pl.program_id(0) == 0)
        def _():
            vs_ref[...] = jnp.zeros_like(vs_ref)

        p = pv_ref[...]
        lrb, lib = _bf(lre_ref[...]), _bf(lim_ref[...])
        parts = []
        for k in range(S5_NB):
            sl = slice(k * S5_BP, (k + 1) * S5_BP)
            parts.append(_dot_nt(lrb[:, sl], bre_ref[k]) + _dot_nt(lib[:, sl], bim_ref[k]))
        dub = _bf(jnp.concatenate(parts, axis=1) + dus_ref[...])
        du_ref[...] = dub
        dh = _dot_nt(dub, win_ref[...])
        _, xn, r = _norm_mod(x_ref[...], p[R_N1:R_N1 + 1], p[R_SC1:R_SC1 + 1], p[R_SH1:R_SH1 + 1])
        dx_ref[...] = dx1_ref[...] + _norm_mod_bwd(dh, xn, r, p[R_N1:R_N1 + 1], p[R_SC1:R_SC1 + 1])
        vs_ref[1:2, :] += _sum0(dh * xn)
        vs_ref[2:3, :] += _sum0(dh)

    return pl.pallas_call(
        body, name="s5_in_bwd", grid=(L // tm,),
        in_specs=[_rows(tm, D), _rows(tm, NSTATE), _rows(tm, NSTATE), _rows(tm, D), _rows(tm, D),
                  pl.BlockSpec((None, 8, D), lambda i: (layer, 0, 0)),
                  _const_w((S5_NB, S5_BH, S5_BP)), _const_w((S5_NB, S5_BH, S5_BP)), _layer_w(D, D, 0)],
        out_specs=[_rows(tm, D), _rows(tm, D), _whole((8, D))],
        out_shape=[jax.ShapeDtypeStruct((L, D), F32), jax.ShapeDtypeStruct((L, D), BF16), jax.ShapeDtypeStruct((8, D), F32)],
        compiler_params=_params(1, 56),
    )(dx1, lam_re, lam_im, du_skip, x, pv, b_re, b_im, w_in)


def _blockdiag_b(bt):
    b = bt.reshape(S5_H, S5_NB, 16, S5_P).transpose(1, 2, 0, 3)
    eye = jnp.eye(16, dtype=bt.dtype)
    return (b[:, :, :, None, :] * eye[None, :, None, :, None]).reshape(S5_NB, S5_BH, S5_BP)


def _unblock_b(d):
    d = jnp.einsum("bghgp->bghp", d.reshape(S5_NB, 16, S5_H, 16, S5_P))
    return d.transpose(2, 0, 1, 3).reshape(S5_H, S5_G, S5_P)


def _blockdiag_c(cm):
    c4 = cm.reshape(S5_NB, 16, S5_H, S5_P)
    eye = jnp.eye(16, dtype=cm.dtype)
    out = c4.transpose(0, 1, 3, 2)[:, :, :, None, :] * eye[None, :, None, :, None]
    return out.reshape(S5_NB, S5_BP, S5_BH)


def _unblock_c(d):
    d = jnp.einsum("bgpgh->bghp", d.reshape(S5_NB, 16, S5_P, 16, S5_H))
    return d.reshape(S5_G, S5_H, S5_P)


def _tril_mask():
    return lax.broadcasted_iota(jnp.int32, (SG_CHUNK, SG_CHUNK), 0) >= lax.broadcasted_iota(jnp.int32, (SG_CHUNK, SG_CHUNK), 1)


def _sg_fwd(x, pv, w_in, w_s, b_t, vg, w_out, layer, tm, after=None):
    L = x.shape[0]
    nc = tm // SG_CHUNK

    def body(x_ref, pv_ref, win_ref, ws_ref, bt_ref, vg_ref, wout_ref, x1_ref, h_ref, uv_ref, vm_ref, q_ref, y_ref):
        xv, p = x_ref[...], pv_ref[...]
        h, _, _ = _norm_mod(xv, p[R_N1:R_N1 + 1], p[R_SC1:R_SC1 + 1], p[R_SH1:R_SH1 + 1])
        hb = _bf(h)
        h_ref[...] = hb
        uv = _dot(hb, win_ref[...])
        uv_ref[...] = uv
        v = uv[:, D:]
        rv = lax.rsqrt(jnp.mean(v * v, axis=-1, keepdims=True) + EPS)
        vnb = _bf((v * rv) * vg_ref[...])
        mask = _tril_mask()
        bt = bt_ref[...]
        for hd in range(SG_HEADS):
            wm = _bf(jnp.where(mask, ws_ref[hd], 0.0))
            cs = slice(hd * SG_CHUNK, (hd + 1) * SG_CHUNK)
            for ck in range(nc):
                rs = slice(ck * SG_CHUNK, (ck + 1) * SG_CHUNK)
                vm_ref[rs, cs] = _dot(wm, vnb[rs, cs]) + bt[:, hd:hd + 1]
        qb = _bf(uv[:, :D] * vm_ref[...])
        q_ref[...] = qb
        y = _dot(qb, wout_ref[...])
        y_ref[...] = y
        x1_ref[...] = xv + p[R_G1:R_G1 + 1] * y

    call, tail = _call_after(
        after, body, 7, name="sg_fwd", grid=(L // tm,),
        in_specs=[_rows(tm, D), pl.BlockSpec((None, 8, D), lambda i: (layer, 0, 0)), _layer_w(D, 2 * D, 0),
                  _whole((SG_HEADS, SG_CHUNK, SG_CHUNK)), _whole((SG_CHUNK, SG_HEADS)), _whole((1, D)), _layer_w(D, D, 0)],
        out_specs=[_rows(tm, D), _rows(tm, D), _rows(tm, 2 * D), _rows(tm, D), _rows(tm, D), _rows(tm, D)],
        out_shape=[jax.ShapeDtypeStruct((L, D), F32), jax.ShapeDtypeStruct((L, D), BF16), jax.ShapeDtypeStruct((L, 2 * D), F32),
                   jax.ShapeDtypeStruct((L, D), F32), jax.ShapeDtypeStruct((L, D), BF16), jax.ShapeDtypeStruct((L, D), F32)],
        compiler_params=_params(1, 56),
    )
    return call(x, pv, w_in, w_s, b_t, vg, w_out, *tail)


def _sg_bwd(dx1, x, y, uv, vm, pv, w_in, w_s, vg, w_out, layer, tm, after=None):
    L = x.shape[0]
    nc = tm // SG_CHUNK

    def body(dx1_ref, x_ref, y_ref, uv_ref, vm_ref, pv_ref, win_ref, ws_ref, vg_ref, wout_ref,
             dx_ref, duv_ref, dy_ref, vs_ref, dws_ref, dbt_ref, dvn_scr):
        @pl.when(pl.program_id(0) == 0)
        def _():
            vs_ref[...] = jnp.zeros_like(vs_ref)
            dws_ref[...] = jnp.zeros_like(dws_ref)
            dbt_ref[...] = jnp.zeros_like(dbt_ref)

        dx1v, p = dx1_ref[...], pv_ref[...]
        dyb = _bf(dx1v * p[R_G1:R_G1 + 1])
        dy_ref[...] = dyb
        vs_ref[0:1, :] += _sum0(dx1v * y_ref[...])
        dq = _dot_nt(dyb, wout_ref[...])
        uv = uv_ref[...]
        u, v = uv[:, :D], uv[:, D:]
        dub = _bf(dq * vm_ref[...])
        dvm = dq * u
        dvmb = _bf(dvm)
        rv = lax.rsqrt(jnp.mean(v * v, axis=-1, keepdims=True) + EPS)
        vh = v * rv
        vgv = vg_ref[...]
        vnb = _bf(vh * vgv)
        mask = _tril_mask()
        for hd in range(SG_HEADS):
            wm = _bf(jnp.where(mask, ws_ref[hd], 0.0))
            cs = slice(hd * SG_CHUNK, (hd + 1) * SG_CHUNK)
            dws = jnp.zeros((SG_CHUNK, SG_CHUNK), F32)
            dbs = jnp.zeros((SG_CHUNK, 1), F32)
            for ck in range(nc):
                rs = slice(ck * SG_CHUNK, (ck + 1) * SG_CHUNK)
                dvn_scr[rs, cs] = _dot_tn(wm, dvmb[rs, cs])
                dws = dws + _dot_nt(dvmb[rs, cs], vnb[rs, cs])
                dbs = dbs + jnp.sum(dvm[rs, cs], axis=1, keepdims=True)
            dws_ref[hd] += jnp.where(mask, dws, 0.0)
            dbt_ref[:, hd:hd + 1] += dbs
        dvn = dvn_scr[...]
        vs_ref[3:4, :] += _sum0(dvn * vh)
        dvnn = dvn * vgv
        dvb = _bf(rv * (dvnn - vh * jnp.mean(dvnn * vh, axis=-1, keepdims=True)))
        duv_ref[:, 0:D] = dub
        duv_ref[:, D:2 * D] = dvb
        dh = _dot_nt(dub, win_ref[:, 0:D]) + _dot_nt(dvb, win_ref[:, D:2 * D])
        _, xn, r = _norm_mod(x_ref[...], p[R_N1:R_N1 + 1], p[R_SC1:R_SC1 + 1], p[R_SH1:R_SH1 + 1])
        dx_ref[...] = dx1v + _norm_mod_bwd(dh, xn, r, p[R_N1:R_N1 + 1], p[R_SC1:R_SC1 + 1])
        vs_ref[1:2, :] += _sum0(dh * xn)
        vs_ref[2:3, :] += _sum0(dh)

    call, tail = _call_after(
        after, body, 10, name="sg_bwd", grid=(L // tm,),
        in_specs=[_rows(tm, D), _rows(tm, D), _rows(tm, D), _rows(tm, 2 * D), _rows(tm, D),
                  pl.BlockSpec((None, 8, D), lambda i: (layer, 0, 0)), _layer_w(D, 2 * D, 0),
                  _whole((SG_HEADS, SG_CHUNK, SG_CHUNK)), _whole((1, D)), _layer_w(D, D, 0)],
        out_specs=[_rows(tm, D), _rows(tm, 2 * D), _rows(tm, D), _whole((8, D)),
                   _whole((SG_HEADS, SG_CHUNK, SG_CHUNK)), _whole((SG_CHUNK, SG_HEADS))],
        out_shape=[jax.ShapeDtypeStruct((L, D), F32), jax.ShapeDtypeStruct((L, 2 * D), BF16), jax.ShapeDtypeStruct((L, D), BF16),
                   jax.ShapeDtypeStruct((8, D), F32), jax.ShapeDtypeStruct((SG_HEADS, SG_CHUNK, SG_CHUNK), F32),
                   jax.ShapeDtypeStruct((SG_CHUNK, SG_HEADS), F32)],
        scratch_shapes=[pltpu.VMEM((tm, D), F32)],
        compiler_params=_params(1, 56),
    )
    return call(dx1, x, y, uv, vm, pv, w_in, w_s, vg, w_out, *tail)


def _final(x, target, fg, tm):
    L = x.shape[0]

    def body(x_ref, t_ref, g_ref, dx_ref, vs_ref):
        @pl.when(pl.program_id(0) == 0)
        def _():
            vs_ref[...] = jnp.zeros_like(vs_ref)

        xv, g = x_ref[...], g_ref[...]
        r = lax.rsqrt(jnp.mean(xv * xv, axis=-1, keepdims=True) + EPS)
        xn = xv * r
        e = xn * g - t_ref[...]
        vs_ref[0:1, :] += jnp.sum(e * e)
        dout = e * (1.0 / D)
        vs_ref[1:2, :] += _sum0(dout * xn)
        dxn = dout * g
        dx_ref[...] = r * (dxn - xn * jnp.mean(dxn * xn, axis=-1, keepdims=True))

    return pl.pallas_call(
        body, name="final_loss", grid=(L // tm,),
        in_specs=[_rows(tm, D), _rows(tm, D), _whole((1, D))],
        out_specs=[_rows(tm, D), _whole((8, D))],
        out_shape=[jax.ShapeDtypeStruct((L, D), F32), jax.ShapeDtypeStruct((8, D), F32)],
        compiler_params=_params(1),
    )(x, target, fg)


def _pack(arrs):
    flat = jnp.concatenate([a.reshape(-1).astype(F32) for a in arrs])
    rows = -(-flat.shape[0] // LANES)
    rows = -(-rows // 8) * 8
    return jnp.pad(flat, (0, rows * LANES - flat.shape[0])).reshape(rows, LANES)


def _unpack(buf, shapes, lead=()):
    flat = buf.reshape(lead + (-1,))
    out, off = [], 0
    for s in shapes:
        n = 1
        for d in s:
            n *= d
        out.append(flat[..., off:off + n].reshape(lead + tuple(s)))
        off += n
    return out


BIG = ("ff_w1", "ff_w2", "conv_w_in", "conv_w_out", "ssm_w_in", "ssm_glu_w", "ssm_w_out", "sg_w_in", "sg_w_out")
BIG_AXIS = {"ff_w1": 2, "ff_w2": 1, "conv_w_in": 2, "conv_w_out": 1, "ssm_w_in": 1, "ssm_glu_w": 1, "ssm_w_out": 1,
            "sg_w_in": 2, "sg_w_out": 1}
LAYER_WEIGHTS = (
    (("conv_w_in", 0), ("conv_w_out", 0), ("ff_w1", 0), ("ff_w2", 0)),
    (("ssm_w_in", 0), ("ssm_glu_w", 0), ("ssm_w_out", 0), ("ff_w1", 1), ("ff_w2", 1)),
    (("sg_w_in", 0), ("sg_w_out", 0), ("ff_w1", 2), ("ff_w2", 2)),
    (("conv_w_in", 1), ("conv_w_out", 1), ("ff_w1", 3), ("ff_w2", 3)),
)
SMALL = ("ada_b", "norm1_g", "norm2_g", "final_g", "conv_w", "conv_b", "ssm_a_re", "ssm_a_im", "ssm_log_dt", "ssm_b_re",
         "ssm_b_im", "ssm_c_re", "ssm_c_im", "ssm_d", "ssm_glu_b", "sg_v_g", "sg_w_s", "sg_b_s")
WEIGHTS = ("ada_w", "ada_b", "norm1_g", "norm2_g", "ff_w1", "ff_w2", "final_g", "conv_w_in", "conv_w", "conv_b", "conv_w_out",
           "ssm_w_in", "ssm_a_re", "ssm_a_im", "ssm_log_dt", "ssm_b_re", "ssm_b_im", "ssm_c_re", "ssm_c_im", "ssm_d",
           "ssm_glu_w", "ssm_glu_b", "ssm_w_out", "sg_w_in", "sg_v_g", "sg_w_s", "sg_b_s", "sg_w_out")


def kernel(x, c, ada_w, ada_b, norm1_g, norm2_g, ff_w1, ff_w2, final_g, conv_w_in, conv_w, conv_b, conv_w_out, ssm_w_in, ssm_a_re, ssm_a_im, ssm_log_dt, ssm_b_re, ssm_b_im, ssm_c_re, ssm_c_im, ssm_d, ssm_glu_w, ssm_glu_b, ssm_w_out, sg_w_in, sg_v_g, sg_w_s, sg_b_s, sg_w_out, loss_target, m_ada_w, m_ada_b, m_norm1_g, m_norm2_g, m_ff_w1, m_ff_w2, m_final_g, m_conv_w_in, m_conv_w, m_conv_b, m_conv_w_out, m_ssm_w_in, m_ssm_a_re, m_ssm_a_im, m_ssm_log_dt, m_ssm_b_re, m_ssm_b_im, m_ssm_c_re, m_ssm_c_im, m_ssm_d, m_ssm_glu_w, m_ssm_glu_b, m_ssm_w_out, m_sg_w_in, m_sg_v_g, m_sg_w_s, m_sg_b_s, m_sg_w_out, v_ada_w, v_ada_b, v_norm1_g, v_norm2_g, v_ff_w1, v_ff_w2, v_final_g, v_conv_w_in, v_conv_w, v_conv_b, v_conv_w_out, v_ssm_w_in, v_ssm_a_re, v_ssm_a_im, v_ssm_log_dt, v_ssm_b_re, v_ssm_b_im, v_ssm_c_re, v_ssm_c_im, v_ssm_d, v_ssm_glu_w, v_ssm_glu_b, v_ssm_w_out, v_sg_w_in, v_sg_v_g, v_sg_w_s, v_sg_b_s, v_sg_w_out):
    args = dict(locals())
    w = {n: args[n] for n in WEIGHTS}
    m = {n: args["m_" + n] for n in WEIGHTS}
    v = {n: args["v_" + n] for n in WEIGHTS}
    L = x.shape[1]
    tm = min(L, 256)
    tr = min(L, 512)
    chip = 2 * lax.axis_index("x") + lax.axis_index("y")
    me = 2 * chip + lax.axis_index("c")
    xin = x[0]
    target = loss_target[0]
    chip1 = chip.reshape(1).astype(jnp.int32)

    gathers = []
    def start_gather(i, after):
        shards = [w[n][li:li + 1].astype(BF16) for n, li in LAYER_WEIGHTS[i]]
        axes = [BIG_AXIS[n] for n, _ in LAYER_WEIGHTS[i]]
        s_sems, r_sems, shards, lands, token = _gather_start(shards, axes, f"gather_start{i}", after)
        gathers.append((s_sems, r_sems, shards, lands, axes))
        return token

    def weights_of(i, after):
        s_sems, r_sems, shards, lands, axes = gathers[i]
        lands = _gather_wait(s_sems, r_sems, shards, lands, axes, f"gather_wait{i}", after)
        return dict(zip([n for n, _ in LAYER_WEIGHTS[i]], _gather_share(shards, lands, axes, f"gather_share{i}")))

    small_in = _pack([c, conv_w, conv_b, sg_v_g])
    got = _allgather_small(small_in, "gather_small_inputs").reshape(N_DEV, -1)
    c_all, cw_sh, cb_sh, vg_sh = _unpack(got, [(D,), conv_w.shape, conv_b.shape, sg_v_g.shape], lead=(N_DEV,))
    conv_w_full = jnp.concatenate([cw_sh[2 * k] for k in range(4)], axis=-1)
    conv_b_full = jnp.concatenate([cb_sh[2 * k] for k in range(4)], axis=-1)
    vg_full = jnp.concatenate([vg_sh[2 * k] for k in range(4)], axis=-1)
    c16 = jnp.pad(c_all, ((0, 16 - N_DEV), (0, 0)))

    cols = ada_w.shape[2]
    ada_b_cols = lax.dynamic_slice_in_dim(ada_b, chip * cols, cols, axis=1)[:, None, :]
    mod_sh = _ada_fwd(c16, ada_w, ada_b_cols)[:, :N_DEV, :]
    mod_all = _allgather_small(_pack([mod_sh]), "gather_mod").reshape(N_DEV, -1)
    mod_all = _unpack(mod_all, [mod_sh.shape], lead=(N_DEV,))[0]
    mod_mine = lax.dynamic_index_in_dim(mod_all[0::2], me, axis=2, keepdims=False)
    mod_mine = mod_mine.transpose(1, 0, 2).reshape(DEPTH, 6, D)
    pv = jnp.concatenate([mod_mine, norm1_g[:, None, :], norm2_g[:, None, :]], axis=1)

    start_gather(0, pv)

    cw_rows = jnp.concatenate([conv_w_full, conv_b_full[:, None, :], jnp.zeros((conv_w_full.shape[0], 4, D), F32)], axis=1)

    a_re, a_im = ssm_a_re[0], ssm_a_im[0]
    log_dt = ssm_log_dt[0][:, None]
    bt_re, bt_im = ssm_b_re[0].transpose(2, 0, 1), ssm_b_im[0].transpose(2, 0, 1)
    abar_re, abar_im, bbar_re, bbar_im = _s5_params_fwd(a_re, a_im, log_dt, bt_re, bt_im)
    ar_vec, ai_vec = abar_re.reshape(1, NSTATE), abar_im.reshape(1, NSTATE)
    bd_re, bd_im = _bf(_blockdiag_b(bbar_re)), _bf(_blockdiag_b(bbar_im))
    cd_re, cd_im = _bf(_blockdiag_c(ssm_c_re[0])), _bf(_blockdiag_c(ssm_c_im[0]))

    saved = []
    fulls = []
    xl = xin
    for i in range(DEPTH):
        kind = MIXER_OF_LAYER[i]
        j = i // 3
        full = weights_of(i, cd_im if i == 0 else xl)
        fulls.append(full)
        tok = start_gather(i + 1, full["ff_w2"]) if i + 1 < DEPTH else None
        if kind == 0:
            x1, h, bcx, conv, q, y = _conv_fwd(xl, pv, full["conv_w_in"], full["conv_w_out"], cw_rows, i, j, tm, after=tok)
            mix = dict(h=h, bcx=bcx, conv=conv, q=q, y=y)
        elif kind == 1:
            h, u, bu_re, bu_im = _s5_in_fwd(xl, pv, full["ssm_w_in"], bd_re, bd_im, i, tm, after=tok)
            s_re, s_im = _s5_scan_fwd(bu_re, bu_im, ar_vec, ai_vec, tr)
            x1, y1, zg, y3, y = _s5_out_fwd(xl, u, s_re, s_im, pv, cd_re, cd_im, ssm_d, full["ssm_glu_w"], ssm_glu_b,
                                            full["ssm_w_out"], i, tm)
            mix = dict(h=h, u=u, s_re=s_re, s_im=s_im, y1=y1, zg=zg, y3=y3, y=y)
        else:
            x1, h, uv, vm, q, y = _sg_fwd(xl, pv, full["sg_w_in"], sg_w_s[0], sg_b_s[0].T, vg_full, full["sg_w_out"], i, tm,
                                          after=tok)
            mix = dict(h=h, uv=uv, vm=vm, q=q, y=y)
        x2, h2, a, f = _ffn_fwd(x1, pv, full["ff_w1"], full["ff_w2"], i, tm)
        saved.append(dict(x=xl, x1=x1, h2=h2, a=a, f=f, **mix))
        xl = x2

    dxl, vs_fin = _final(xl, target, final_g[None, :], tm)

    gfull = {n: [None] * w[n].shape[0] for n in BIG}
    vs_mix, vs_ffn = [None] * DEPTH, [None] * DEPTH
    small_g = {}
    scatters = [None] * DEPTH
    token = None

    def start_scatter(i, after):
        garrs = [gfull[n][li][None] for n, li in LAYER_WEIGHTS[i]]
        gaxes = [BIG_AXIS[n] for n, _ in LAYER_WEIGHTS[i]]
        s_sems, r_sems, garrs, lands, tok = _scatter_start(garrs, gaxes, f"scatter_start{i}", after)
        scatters[i] = (s_sems, r_sems, garrs, lands, gaxes)
        return tok

    for i in reversed(range(DEPTH)):
        kind = MIXER_OF_LAYER[i]
        j = i // 3
        sv = saved[i]
        full = fulls[i]
        dx1, p_b, da_b, df_b, vs_ffn[i] = _ffn_bwd(dxl, sv["x1"], sv["a"], sv["f"], pv, full["ff_w1"], full["ff_w2"], i, tm,
                                                   after=token)
        gfull["ff_w1"][i] = _mm_tn(sv["h2"], da_b, f"wgrad_ff_w1_{i}")
        gfull["ff_w2"][i] = _mm_tn(p_b, df_b, f"wgrad_ff_w2_{i}")
        if kind == 0:
            dxl, dbcx_b, dy_b, vsm = _conv_bwd(dx1, sv["x"], sv["y"], sv["bcx"], sv["conv"], pv, full["conv_w_in"],
                                               full["conv_w_out"], cw_rows, i, j, tm)
            gfull["conv_w_in"][j] = _mm_tn(sv["h"], dbcx_b, f"wgrad_conv_w_in_{j}")
            gfull["conv_w_out"][j] = _mm_tn(sv["q"], dy_b, f"wgrad_conv_w_out_{j}")
            small_g.setdefault("conv_w", [None, None])[j] = vsm[3:6]
            small_g.setdefault("conv_b", [None, None])[j] = vsm[6]
        elif kind == 1:
            dy_b, y2_b, dzg_b, dy1_b, du_skip, g_re, g_im, vsm = _s5_out_bwd(
                dx1, sv["y"], sv["y1"], sv["zg"], sv["u"], pv, cd_re, cd_im, ssm_d, full["ssm_glu_w"], full["ssm_w_out"], i, tm)
            lam_re, lam_im, dabar = _s5_scan_bwd(g_re, g_im, sv["s_re"], sv["s_im"], ar_vec, ai_vec, tr)
            dxl, du_b, vs_in = _s5_in_bwd(dx1, lam_re, lam_im, du_skip, sv["x"], pv, bd_re, bd_im, full["ssm_w_in"], i, tm)
            gfull["ssm_w_out"][0] = _mm_tn(sv["y3"], dy_b, "wgrad_ssm_w_out")
            gfull["ssm_glu_w"][0] = _mm_tn(y2_b, dzg_b, "wgrad_ssm_glu_w")
            gfull["ssm_w_in"][0] = _mm_tn(sv["h"], du_b, "wgrad_ssm_w_in")
            d_cre = _unblock_c(_mm_tn_blocks(sv["s_re"], dy1_b, S5_BP, S5_BH, "wgrad_s5_c_re"))
            d_cim = -_unblock_c(_mm_tn_blocks(sv["s_im"], dy1_b, S5_BP, S5_BH, "wgrad_s5_c_im"))
            d_bbre = _unblock_b(_mm_tn_blocks(sv["u"], lam_re, S5_BH, S5_BP, "wgrad_s5_b_re"))
            d_bbim = _unblock_b(_mm_tn_blocks(sv["u"], lam_im, S5_BH, S5_BP, "wgrad_s5_b_im"))
            d_are, d_aim, d_ldt, d_btre, d_btim = _s5_params_bwd(
                a_re, a_im, log_dt, bt_re, bt_im, dabar[0].reshape(S5_G, S5_P), dabar[1].reshape(S5_G, S5_P), d_bbre, d_bbim)
            small_g.update(ssm_a_re=d_are, ssm_a_im=d_aim, ssm_log_dt=d_ldt, ssm_b_re=d_btre.transpose(1, 2, 0),
                           ssm_b_im=d_btim.transpose(1, 2, 0), ssm_c_re=d_cre, ssm_c_im=d_cim, ssm_d=vsm[2], ssm_glu_b=vsm[1])
            vsm = jnp.concatenate([vsm[0:1], vs_in[1:3], jnp.zeros((5, D), F32)], axis=0)
        else:
            dxl, duv_b, dy_b, vsm, d_ws, d_bt = _sg_bwd(dx1, sv["x"], sv["y"], sv["uv"], sv["vm"], pv, full["sg_w_in"],
                                                        sg_w_s[0], vg_full, full["sg_w_out"], i, tm)
            gfull["sg_w_in"][0] = _mm_tn(sv["h"], duv_b, "wgrad_sg_w_in")
            gfull["sg_w_out"][0] = _mm_tn(sv["q"], dy_b, "wgrad_sg_w_out")
            small_g.update(sg_v_g=vsm[3], sg_w_s=d_ws, sg_b_s=d_bt.T)
        vs_mix[i] = vsm
        if i > 0:
            token = start_scatter(i, dxl)
    grad_x = dxl[None]

    dmod = _mod_bwd(jnp.stack(vs_mix), jnp.stack(vs_ffn), pv)
    small_g.update(ada_b=dmod[:, :6, :], norm1_g=dmod[:, 6, :], norm2_g=dmod[:, 7, :], final_g=vs_fin[1],
                   conv_w=jnp.stack(small_g["conv_w"]), conv_b=jnp.stack(small_g["conv_b"]))

    loss_part = (0.5 / D) * vs_fin[0, 0:1]
    part_shapes = [(1,)] + [tuple(small_g[n].shape) for n in SMALL]
    parts_all, parts_sum = _allgather_small(_pack([loss_part] + [small_g[n] for n in SMALL]), "reduce_small_grads", with_sum=True)
    tok0 = start_scatter(0, parts_sum)[0:1, 0:1]
    summed = _unpack(parts_sum, part_shapes)
    loss = summed[0][0]
    gsum = dict(zip(SMALL, summed[1:]))
    dmod_all = _unpack(parts_all.reshape(N_DEV, -1), part_shapes[:2], lead=(N_DEV,))[1]
    dmod_all = dmod_all.reshape(N_DEV, DEPTH, 6 * D)
    dmod_cols = lax.dynamic_slice_in_dim(dmod_all, chip * cols, cols, axis=2).transpose(1, 0, 2)
    g_ada_w = _ada_bwd(c16, jnp.pad(dmod_cols + tok0[0], ((0, 0), (0, 16 - N_DEV), (0, 0))))

    res = {}
    shp = ada_w.shape
    two = lambda t: t.reshape(shp[0] * shp[1], shp[2])
    res["ada_w"] = [t.reshape(shp) for t in _adamw(two(ada_w), [two(g_ada_w)], two(m_ada_w), two(v_ada_w), "adamw_ada_w")]

    def mine(n, g):
        if n in ("conv_w", "conv_b", "sg_v_g"):
            size = w[n].shape[-1]
            return lax.dynamic_slice_in_dim(g, chip * size, size, axis=g.ndim - 1)
        return g

    g_loc = [mine(n, gsum[n]).reshape(w[n].shape) for n in SMALL]
    small_shapes = [tuple(w[n].shape) for n in SMALL]
    packed = _adamw(_pack([w[n] for n in SMALL]), [_pack(g_loc) + tok0], _pack([m[n] for n in SMALL]), _pack([v[n] for n in SMALL]),
                    "adamw_small")
    unpacked = [_unpack(t, small_shapes) for t in packed]
    for k, n in enumerate(SMALL):
        res[n] = [unpacked[0][k], unpacked[1][k], unpacked[2][k], unpacked[3][k]]

    sums = {n: [None] * w[n].shape[0] for n in BIG}
    after = packed[0]
    for i in reversed(range(DEPTH)):
        s_sems, r_sems, garrs, lands, gaxes = scatters[i]
        garrs, recv = _scatter_wait(s_sems, r_sems, garrs, lands, gaxes, f"scatter_wait{i}", after)
        for (n, li), g, r3, ax in zip(LAYER_WEIGHTS[i], garrs, recv, gaxes):
            sums[n][li] = _sum_parts(r3, g, ax, chip1, f"sum_{n}_{li}")
            after = sums[n][li]
    q_mine = [jnp.stack(sums[n]) for n in BIG]
    q_sib = _swap_with_sibling(q_mine, "swap_grad_sums")
    for n, qa, qb in zip(BIG, q_mine, q_sib):
        shp = w[n].shape
        two = lambda t, shp=shp: t.reshape(shp[0] * shp[1], shp[2])
        res[n] = [t.reshape(shp) for t in _adamw(two(w[n]), [two(qa), two(qb)], two(m[n]), two(v[n]), f"adamw_{n}")]

    outs = [loss, grad_x]
    for part in range(4):
        outs += [res[n][part] for n in WEIGHTS]
    return tuple(outs)
```

```python
import functools

import jax
import jax.numpy as jnp
from jax import lax
from jax.experimental import pallas as pl
from jax.experimental.pallas import tpu as pltpu

F32 = jnp.float32
BF16 = jnp.bfloat16
D = 1024
EPS = 1e-6
DEPTH = 4
MIXER_OF_LAYER = (0, 1, 2, 0)
S5_G, S5_H, S5_P = 64, 16, 64
S5_NB = 4
S5_BH = S5_H * 16
S5_BP = S5_P * 16
NSTATE = S5_G * S5_P
SG_HEADS, SG_CHUNK = 8, 128
ADAM_LR, ADAM_B1, ADAM_B2, ADAM_EPS, ADAM_WD, ADAM_STEP = 0.001, 0.9, 0.999, 1e-08, 0.01, 10
N_DEV = 8
MESH = pl.DeviceIdType.MESH
LANES = 1024
R_SH1, R_SC1, R_G1, R_SH2, R_SC2, R_G2, R_N1, R_N2 = range(8)


def _dot(a, b):
    return jnp.dot(a, b, preferred_element_type=F32)


def _dot_nt(a, b):
    return lax.dot_general(a, b, (((1,), (1,)), ((), ())), preferred_element_type=F32)


def _dot_tn(a, b):
    return lax.dot_general(a, b, (((0,), (0,)), ((), ())), preferred_element_type=F32)


def _bf(x):
    return x.astype(BF16)


def _sum0(x):
    return jnp.sum(x, axis=0, keepdims=True)


def _params(n_axes, vmem_mb=48):
    return pltpu.CompilerParams(dimension_semantics=("arbitrary",) * n_axes, vmem_limit_bytes=vmem_mb << 20)


def _rows(tm, cols, nt=None):
    if nt is None:
        return pl.BlockSpec((tm, cols), lambda i: (i, 0))
    return pl.BlockSpec((tm, cols), lambda i: (nt - 1 - i, 0))


def _whole(shape):
    nd = len(shape)
    return pl.BlockSpec(shape, lambda *_: (0,) * nd)


def _layer_w(r, c, layer):
    return pl.BlockSpec((None, r, c), lambda *_: (layer, 0, 0), pipeline_mode=pl.Buffered(1))


def _const_w(shape):
    nd = len(shape)
    return pl.BlockSpec(shape, lambda *_: (0,) * nd, pipeline_mode=pl.Buffered(1))


def _call_after(after, body, n_in, *, in_specs, **kw):
    if after is None:
        return pl.pallas_call(body, in_specs=in_specs, **kw), ()

    def body_after(*refs):
        return body(*refs[:n_in], *refs[n_in + 1:])

    return pl.pallas_call(body_after, in_specs=list(in_specs) + [pl.BlockSpec(memory_space=pl.ANY)], **kw), (after,)


def _norm_mod(x, ng, sc, sh):
    r = lax.rsqrt(jnp.mean(x * x, axis=-1, keepdims=True) + EPS)
    xn = x * r
    return (xn * ng) * (1.0 + sc) + sh, xn, r


def _norm_mod_bwd(dh, xn, r, ng, sc):
    dxn = dh * (ng * (1.0 + sc))
    return r * (dxn - xn * jnp.mean(dxn * xn, axis=-1, keepdims=True))


def _shift_down(z, prev8, k):
    row = lax.broadcasted_iota(jnp.int32, z.shape, 0)
    if k == 1:
        return jnp.where(row >= 1, pltpu.roll(z, 1, 0), prev8[7:8])
    return jnp.where(row >= 2, pltpu.roll(z, 2, 0), jnp.where(row == 0, prev8[6:7], prev8[7:8]))


def _shift_up(z, next8, k):
    n = z.shape[0]
    row = lax.broadcasted_iota(jnp.int32, z.shape, 0)
    if k == 1:
        return jnp.where(row <= n - 2, pltpu.roll(z, n - 1, 0), next8[0:1])
    return jnp.where(row <= n - 3, pltpu.roll(z, n - 2, 0), jnp.where(row == n - 2, next8[0:1], next8[1:2]))


def _place():
    x, y, c = lax.axis_index("x"), lax.axis_index("y"), lax.axis_index("c")
    chips = [(1 - x, y), (x, 1 - y), (1 - x, 1 - y)]
    return x, y, c, chips


def _allgather_small(x_shard, name, with_sum=False):
    m_per, n = x_shard.shape

    def body(x_ref, *rest):
        if with_sum:
            out_ref, sum_ref, send_sems, recv_sems, local_sem = rest
        else:
            out_ref, send_sems, recv_sems, local_sem = rest
        x, y, c, chips = _place()
        me, sibling = (x, y, c), (x, y, 1 - c)

        def rows(px, py, pc):
            return out_ref.at[pl.ds((4 * px + 2 * py + pc) * m_per, m_per), :]

        def copy(k, block, to, src=None):
            return pltpu.make_async_remote_copy(
                src_ref=rows(*block) if src is None else src, dst_ref=rows(*block),
                send_sem=send_sems.at[k], recv_sem=recv_sems.at[k], device_id=to, device_id_type=MESH)

        mine = pltpu.make_async_copy(x_ref, rows(*me), local_sem)
        mine.start()
        first = [copy(0, me, sibling, src=x_ref)]
        first += [copy(1 + j, me, (*chip, c), src=x_ref) for j, chip in enumerate(chips)]
        for cp in first:
            cp.start()
        passed = [copy(4 + j, (*chip, c), sibling) for j, chip in enumerate(chips)]
        for j, chip in enumerate(chips):
            copy(1 + j, (*chip, c), me).wait_recv()
            passed[j].start()
        copy(0, sibling, me).wait_recv()
        for j, chip in enumerate(chips):
            copy(4 + j, (*chip, 1 - c), me).wait_recv()
        for cp in first + passed:
            cp.wait_send()
        mine.wait()
        if with_sum:
            acc = out_ref[0:m_per, :]
            for d in range(1, N_DEV):
                acc = acc + out_ref[d * m_per:(d + 1) * m_per, :]
            sum_ref[...] = acc

    out_shape = [jax.ShapeDtypeStruct((N_DEV * m_per, n), F32)]
    out_specs = [pl.BlockSpec(memory_space=pltpu.VMEM)]
    if with_sum:
        out_shape.append(jax.ShapeDtypeStruct((m_per, n), F32))
        out_specs.append(pl.BlockSpec(memory_space=pltpu.VMEM))
    res = pl.pallas_call(
        body, name=name, out_shape=out_shape,
        in_specs=[pl.BlockSpec(memory_space=pltpu.VMEM)], out_specs=out_specs,
        scratch_shapes=[pltpu.SemaphoreType.DMA((7,)), pltpu.SemaphoreType.DMA((7,)), pltpu.SemaphoreType.DMA],
        compiler_params=pltpu.CompilerParams(vmem_limit_bytes=48 << 20),
    )(x_shard)
    return tuple(res) if with_sum else res[0]


def _shard_region(ref, full_shape, axis, chip_k, half=None):
    _, r, c = full_shape
    if axis == 1:
        rs = r // 4
        if half is None:
            return ref.at[:, pl.ds(pl.multiple_of(chip_k * rs, 128), rs), :]
        return ref.at[:, pl.ds(pl.multiple_of(chip_k * rs + half * (rs // 2), 128), rs // 2), :]
    cs = c // 4
    if half is None:
        return ref.at[:, :, pl.ds(pl.multiple_of(chip_k * cs, 128), cs)]
    return ref.at[:, pl.ds(pl.multiple_of(half * (r // 2), 128), r // 2), pl.ds(pl.multiple_of(chip_k * cs, 128), cs)]


HBM_SPEC = pl.BlockSpec(memory_space=pltpu.HBM)
SEM_SPEC = pl.BlockSpec(memory_space=pltpu.SEMAPHORE)
ANY_SPEC = pl.BlockSpec(memory_space=pl.ANY)
SPLIT_COPY_PARAMS = pltpu.CompilerParams(has_side_effects=pltpu.SideEffectType.DATAFLOW_SIDE_EFFECTING)


def _in_hbm(arrs):
    return [pltpu.with_memory_space_constraint(a, pltpu.HBM) for a in arrs]


def _cast_place(w_stack, li, axis, chip, name):
    _, r, c = w_stack.shape
    full = (1, 4 * r, c) if axis == 1 else (1, r, 4 * c)
    tr = min(r, 256)
    if axis == 1:
        out_spec = pl.BlockSpec((None, tr, c), lambda i, k: (0, k[0] * (r // tr) + i, 0))
    else:
        out_spec = pl.BlockSpec((None, tr, c), lambda i, k: (0, i, k[0]))

    def body(k_ref, w_ref, o_ref):
        o_ref[...] = _bf(w_ref[...])

    return pl.pallas_call(
        body, name=name,
        grid_spec=pltpu.PrefetchScalarGridSpec(
            num_scalar_prefetch=1, grid=(r // tr,),
            in_specs=[pl.BlockSpec((None, tr, c), lambda i, k: (li, i, 0))], out_specs=out_spec),
        out_shape=jax.ShapeDtypeStruct(full, BF16),
        compiler_params=_params(1),
    )(chip, w_stack)


def _gather_start(lands, axes, name, after):
    n_arr = len(lands)
    fulls = [tuple(l.shape) for l in lands]

    def body(*refs):
        land = refs[:n_arr]
        send_sems, recv_sems = refs[n_arr + 1:n_arr + 3]
        token = refs[-1]
        x, y, c, chips = _place()
        k_me = 2 * x + y
        for a in range(n_arr):
            mine = _shard_region(land[a], fulls[a], axes[a], k_me, c)
            for j, chip in enumerate(chips):
                pltpu.make_async_remote_copy(
                    src_ref=mine, dst_ref=mine, send_sem=send_sems.at[a * 3 + j], recv_sem=recv_sems.at[a * 3 + j],
                    device_id=(*chip, c), device_id_type=MESH).start()
        token[...] = jnp.zeros_like(token)

    res = pl.pallas_call(
        body, name=name,
        out_shape=(pltpu.SemaphoreType.DMA((3 * n_arr,)), pltpu.SemaphoreType.DMA((3 * n_arr,)),
                   *[pltpu.HBM(f, BF16) for f in fulls], jax.ShapeDtypeStruct((8, 128), F32)),
        in_specs=[HBM_SPEC] * n_arr + [ANY_SPEC],
        out_specs=(SEM_SPEC, SEM_SPEC, *[HBM_SPEC] * n_arr, pl.BlockSpec(memory_space=pltpu.VMEM)),
        input_output_aliases={a: 2 + a for a in range(n_arr)},
        compiler_params=SPLIT_COPY_PARAMS,
    )(*_in_hbm(lands), after)
    return res[0], res[1], list(res[2:2 + n_arr]), res[-1]


def _gather_wait(send_sems, recv_sems, lands, axes, name, after):
    n_arr = len(lands)
    fulls = [tuple(l.shape) for l in lands]

    def body(*refs):
        land = refs[:n_arr]
        s_sems, r_sems = refs[n_arr:n_arr + 2]
        x, y, c, chips = _place()
        for a in range(n_arr):
            for j, chip in enumerate(chips):
                k_j = 2 * chip[0] + chip[1]
                got = _shard_region(land[a], fulls[a], axes[a], k_j, c)
                cp = pltpu.make_async_remote_copy(
                    src_ref=got, dst_ref=got, send_sem=s_sems.at[a * 3 + j], recv_sem=r_sems.at[a * 3 + j],
                    device_id=(x, y, c), device_id_type=MESH)
                cp.wait_send()
                cp.wait_recv()

    res = pl.pallas_call(
        body, name=name,
        out_shape=tuple(pltpu.HBM(f, BF16) for f in fulls),
        in_specs=[HBM_SPEC] * n_arr + [SEM_SPEC, SEM_SPEC, ANY_SPEC],
        out_specs=tuple([HBM_SPEC] * n_arr),
        input_output_aliases={a: a for a in range(n_arr)},
        compiler_params=SPLIT_COPY_PARAMS,
    )(*lands, send_sems, recv_sems, after)
    return list(res)


def _gather_share(lands, axes, name):
    n_arr = len(lands)
    fulls = [tuple(l.shape) for l in lands]

    def body(*refs):
        land_in, land = refs[:n_arr], refs[n_arr:2 * n_arr]
        send_sems, recv_sems = refs[2 * n_arr:]
        x, y, c, chips = _place()
        copies = []
        for a in range(n_arr):
            for j, chip in enumerate(chips):
                k_j = 2 * chip[0] + chip[1]
                cp = pltpu.make_async_remote_copy(
                    src_ref=_shard_region(land_in[a], fulls[a], axes[a], k_j, c),
                    dst_ref=_shard_region(land[a], fulls[a], axes[a], k_j, c),
                    send_sem=send_sems.at[a * 3 + j], recv_sem=recv_sems.at[a * 3 + j],
                    device_id=(x, y, 1 - c), device_id_type=MESH)
                cp.start()
                copies.append(cp)
        for cp in copies:
            cp.wait()

    return pl.pallas_call(
        body, name=name, out_shape=[jax.ShapeDtypeStruct(f, BF16) for f in fulls],
        in_specs=[ANY_SPEC] * n_arr, out_specs=[ANY_SPEC] * n_arr,
        input_output_aliases={a: a for a in range(n_arr)},
        scratch_shapes=[pltpu.SemaphoreType.DMA((3 * n_arr,)), pltpu.SemaphoreType.DMA((3 * n_arr,))],
    )(*lands)


def _scatter_shapes(grads, axes):
    out = []
    for g, ax in zip(grads, axes):
        shp = list(g.shape)
        shp[ax] //= 4
        out.append((3,) + tuple(shp[1:]))
    return out


def _scatter_start(grads, axes, name, after):
    n_arr = len(grads)
    shapes = _scatter_shapes(grads, axes)
    lands = [lax.empty(s, BF16) for s in shapes]

    def body(*refs):
        ins, land = refs[:n_arr], refs[n_arr:2 * n_arr]
        send_sems, recv_sems = refs[2 * n_arr + 1:2 * n_arr + 3]
        token = refs[-1]
        x, y, c, chips = _place()
        for a in range(n_arr):
            for j, chip in enumerate(chips):
                k_j = 2 * chip[0] + chip[1]
                pltpu.make_async_remote_copy(
                    src_ref=_shard_region(ins[a], grads[a].shape, axes[a], k_j), dst_ref=land[a].at[pl.ds(j, 1)],
                    send_sem=send_sems.at[a * 3 + j], recv_sem=recv_sems.at[a * 3 + j],
                    device_id=(*chip, c), device_id_type=MESH).start()
        token[...] = jnp.zeros_like(token)

    res = pl.pallas_call(
        body, name=name,
        out_shape=(pltpu.SemaphoreType.DMA((3 * n_arr,)), pltpu.SemaphoreType.DMA((3 * n_arr,)),
                   *[pltpu.HBM(g.shape, BF16) for g in grads], *[pltpu.HBM(s, BF16) for s in shapes],
                   jax.ShapeDtypeStruct((8, 128), F32)),
        in_specs=[HBM_SPEC] * (2 * n_arr) + [ANY_SPEC],
        out_specs=(SEM_SPEC, SEM_SPEC, *[HBM_SPEC] * (2 * n_arr), pl.BlockSpec(memory_space=pltpu.VMEM)),
        input_output_aliases={a: 2 + a for a in range(2 * n_arr)},
        compiler_params=SPLIT_COPY_PARAMS,
    )(*_in_hbm(grads), *_in_hbm(lands), after)
    return res[0], res[1], list(res[2:2 + n_arr]), list(res[2 + n_arr:2 + 2 * n_arr]), res[-1]


def _scatter_wait(send_sems, recv_sems, grads, lands, axes, name, after):
    n_arr = len(grads)

    def body(*refs):
        ins, land = refs[:n_arr], refs[n_arr:2 * n_arr]
        s_sems, r_sems = refs[2 * n_arr:2 * n_arr + 2]
        x, y, c, chips = _place()
        for a in range(n_arr):
            for j, chip in enumerate(chips):
                k_j = 2 * chip[0] + chip[1]
                cp = pltpu.make_async_remote_copy(
                    src_ref=_shard_region(ins[a], grads[a].shape, axes[a], k_j), dst_ref=land[a].at[pl.ds(j, 1)],
                    send_sem=s_sems.at[a * 3 + j], recv_sem=r_sems.at[a * 3 + j],
                    device_id=(x, y, c), device_id_type=MESH)
                cp.wait_send()
                cp.wait_recv()

    res = pl.pallas_call(
        body, name=name,
        out_shape=(*[pltpu.HBM(g.shape, BF16) for g in grads], *[pltpu.HBM(l.shape, BF16) for l in lands]),
        in_specs=[HBM_SPEC] * (2 * n_arr) + [SEM_SPEC, SEM_SPEC, ANY_SPEC],
        out_specs=tuple([HBM_SPEC] * (2 * n_arr)),
        input_output_aliases={a: a for a in range(2 * n_arr)},
        compiler_params=SPLIT_COPY_PARAMS,
    )(*grads, *lands, send_sems, recv_sems, after)
    return list(res[:n_arr]), list(res[n_arr:])


def _swap_with_sibling(arrs, name):
    n_arr = len(arrs)

    def body(*refs):
        ins, outs = refs[:n_arr], refs[n_arr:2 * n_arr]
        send_sems, recv_sems = refs[2 * n_arr:]
        x, y, c, _ = _place()
        copies = []
        for a in range(n_arr):
            cp = pltpu.make_async_remote_copy(
                src_ref=ins[a], dst_ref=outs[a], send_sem=send_sems.at[a], recv_sem=recv_sems.at[a],
                device_id=(x, y, 1 - c), device_id_type=MESH)
            cp.start()
            copies.append(cp)
        for cp in copies:
            cp.wait()

    any_spec = pl.BlockSpec(memory_space=pl.ANY)
    return pl.pallas_call(
        body, name=name, out_shape=[jax.ShapeDtypeStruct(a.shape, a.dtype) for a in arrs],
        in_specs=[any_spec] * n_arr, out_specs=[any_spec] * n_arr,
        scratch_shapes=[pltpu.SemaphoreType.DMA((n_arr,)), pltpu.SemaphoreType.DMA((n_arr,))],
    )(*arrs)


def _mm_tn(a, b, name, out_dtype=BF16):
    L, m = a.shape
    n = b.shape[1]
    bm, bn, bk = min(m, 1024), min(n, 1024), min(L, 512)
    nk = L // bk

    def body(a_ref, b_ref, o_ref, acc):
        k = pl.program_id(2)

        @pl.when(k == 0)
        def _():
            acc[...] = jnp.zeros_like(acc)

        acc[...] += _dot_tn(_bf(a_ref[...]), _bf(b_ref[...]))

        @pl.when(k == nk - 1)
        def _():
            o_ref[...] = acc[...].astype(out_dtype)

    return pl.pallas_call(
        body, name=name, grid=(m // bm, n // bn, nk),
        in_specs=[pl.BlockSpec((bk, bm), lambda i, j, k: (k, i)), pl.BlockSpec((bk, bn), lambda i, j, k: (k, j))],
        out_specs=pl.BlockSpec((bm, bn), lambda i, j, k: (i, j)),
        out_shape=jax.ShapeDtypeStruct((m, n), out_dtype),
        scratch_shapes=[pltpu.VMEM((bm, bn), F32)],
        compiler_params=_params(3),
    )(a, b)


def _mm_tn_blocks(a, b, wa, wb, name):
    L = a.shape[0]
    nb = a.shape[1] // wa
    bk = min(L, 512)
    nk = L // bk

    def body(a_ref, b_ref, o_ref):
        @pl.when(pl.program_id(1) == 0)
        def _():
            o_ref[...] = jnp.zeros_like(o_ref)

        o_ref[...] += _dot_tn(_bf(a_ref[...]), _bf(b_ref[...]))

    return pl.pallas_call(
        body, name=name, grid=(nb, nk),
        in_specs=[pl.BlockSpec((bk, wa), lambda j, k: (k, j)), pl.BlockSpec((bk, wb), lambda j, k: (k, j))],
        out_specs=pl.BlockSpec((None, wa, wb), lambda j, k: (j, 0, 0)),
        out_shape=jax.ShapeDtypeStruct((nb, wa, wb), F32),
        compiler_params=_params(2),
    )(a, b)


def _sum_parts(parts, own, axis, chip, name):
    _, r, c = parts.shape
    tr = min(r, 256)
    if axis == 1:
        own_spec = pl.BlockSpec((None, tr, c), lambda i, k: (0, k[0] * (r // tr) + i, 0))
    else:
        own_spec = pl.BlockSpec((None, tr, c), lambda i, k: (0, i, k[0]))

    def body(k_ref, p_ref, g_ref, o_ref):
        p = p_ref[...].astype(F32)
        o_ref[...] = ((p[0] + p[1]) + p[2]) + g_ref[...].astype(F32)

    return pl.pallas_call(
        body, name=name,
        grid_spec=pltpu.PrefetchScalarGridSpec(
            num_scalar_prefetch=1, grid=(r // tr,),
            in_specs=[pl.BlockSpec((3, tr, c), lambda i, k: (0, i, 0)), own_spec],
            out_specs=pl.BlockSpec((tr, c), lambda i, k: (i, 0))),
        out_shape=jax.ShapeDtypeStruct((r, c), F32),
        compiler_params=_params(1),
    )(chip, parts, own)


def _adamw(w, g_parts, m, v, name):
    r, c = w.shape
    tr = r
    for cand in (512, 256, 128, 64, 32, 16, 8):
        if r % cand == 0 and cand * c * 4 <= (2 << 20):
            tr = cand
            break
    n_g = len(g_parts)
    c1 = 1.0 / (1.0 - ADAM_B1 ** ADAM_STEP)
    c2 = 1.0 / (1.0 - ADAM_B2 ** ADAM_STEP)

    def body(*refs):
        w_ref, g_refs, m_ref, v_ref = refs[0], refs[1:1 + n_g], refs[1 + n_g], refs[2 + n_g]
        g_out, d_out, m_out, v_out = refs[3 + n_g:]
        g = g_refs[0][...]
        for gr in g_refs[1:]:
            g = g + gr[...]
        m_new = ADAM_B1 * m_ref[...] + (1.0 - ADAM_B1) * g
        v_new = ADAM_B2 * v_ref[...] + (1.0 - ADAM_B2) * (g * g)
        m_hat = m_new * c1
        v_hat = v_new * c2
        g_out[...] = g
        d_out[...] = -ADAM_LR * (m_hat / (jnp.sqrt(v_hat) + ADAM_EPS) + ADAM_WD * w_ref[...])
        m_out[...] = m_new
        v_out[...] = v_new

    spec = pl.BlockSpec((tr, c), lambda i: (i, 0))
    return pl.pallas_call(
        body, name=name, grid=(r // tr,),
        in_specs=[spec] * (3 + n_g), out_specs=[spec] * 4,
        out_shape=[jax.ShapeDtypeStruct((r, c), F32)] * 4,
        compiler_params=_params(1),
    )(w, *g_parts, m, v)


def _ada_fwd(c16, ada_w, ada_b_cols):
    cols = ada_w.shape[2]

    def body(c_ref, w_ref, b_ref, o_ref):
        cv = c_ref[...]
        ca = _bf(cv * jax.nn.sigmoid(cv))
        o_ref[...] = _dot(ca, _bf(w_ref[...])) + b_ref[...]

    return pl.pallas_call(
        body, name="ada_fwd", grid=(DEPTH,),
        in_specs=[_whole((16, D)), pl.BlockSpec((None, D, cols), lambda i: (i, 0, 0)),
                  pl.BlockSpec((None, 1, cols), lambda i: (i, 0, 0))],
        out_specs=pl.BlockSpec((None, 16, cols), lambda i: (i, 0, 0)),
        out_shape=jax.ShapeDtypeStruct((DEPTH, 16, cols), F32),
        compiler_params=_params(1),
    )(c16, ada_w, ada_b_cols)


def _ada_bwd(c16, dmod16):
    cols = dmod16.shape[2]

    def body(c_ref, d_ref, o_ref):
        cv = c_ref[...]
        ca = _bf(cv * jax.nn.sigmoid(cv))
        o_ref[...] = _dot_tn(ca, _bf(d_ref[...]))

    return pl.pallas_call(
        body, name="ada_bwd", grid=(DEPTH,),
        in_specs=[_whole((16, D)), pl.BlockSpec((None, 16, cols), lambda i: (i, 0, 0))],
        out_specs=pl.BlockSpec((None, D, cols), lambda i: (i, 0, 0)),
        out_shape=jax.ShapeDtypeStruct((DEPTH, D, cols), F32),
        compiler_params=_params(1),
    )(c16, dmod16)


def _mod_bwd(vs_mix, vs_ffn, pv):
    def body(m_ref, f_ref, pv_ref, o_ref):
        for i in range(DEPTH):
            vm, vf, p = m_ref[i], f_ref[i], pv_ref[i]
            o_ref[i] = jnp.concatenate([
                vm[2:3], vm[1:2] * p[R_N1:R_N1 + 1], vm[0:1],
                vf[2:3], vf[1:2] * p[R_N2:R_N2 + 1], vf[0:1],
                vm[1:2] * (1.0 + p[R_SC1:R_SC1 + 1]), vf[1:2] * (1.0 + p[R_SC2:R_SC2 + 1])], axis=0)

    return pl.pallas_call(body, name="mod_bwd", out_shape=jax.ShapeDtypeStruct((DEPTH, 8, D), F32))(vs_mix, vs_ffn, pv)


def _ffn_fwd(x1, pv, w1, w2, layer, tm):
    L = x1.shape[0]
    dff = w1.shape[2]

    def body(x1_ref, pv_ref, w1_ref, w2_ref, x2_ref, h2_ref, a_ref, f_ref):
        x1v, p = x1_ref[...], pv_ref[...]
        h2, _, _ = _norm_mod(x1v, p[R_N2:R_N2 + 1], p[R_SC2:R_SC2 + 1], p[R_SH2:R_SH2 + 1])
        hb = _bf(h2)
        h2_ref[...] = hb
        a = _dot(hb, w1_ref[...])
        a_ref[...] = a
        ra = jnp.maximum(a, 0.0)
        f = _dot(_bf(ra * ra), w2_ref[...])
        f_ref[...] = f
        x2_ref[...] = x1v + p[R_G2:R_G2 + 1] * f

    return pl.pallas_call(
        body, name=f"ffn_fwd{layer}", grid=(L // tm,),
        in_specs=[_rows(tm, D), pl.BlockSpec((None, 8, D), lambda i: (layer, 0, 0)), _layer_w(D, dff, 0), _layer_w(dff, D, 0)],
        out_specs=[_rows(tm, D), _rows(tm, D), _rows(tm, dff), _rows(tm, D)],
        out_shape=[jax.ShapeDtypeStruct((L, D), F32), jax.ShapeDtypeStruct((L, D), BF16),
                   jax.ShapeDtypeStruct((L, dff), F32), jax.ShapeDtypeStruct((L, D), F32)],
        compiler_params=_params(1, 56),
    )(x1, pv, w1, w2)


def _ffn_bwd(dx2, x1, a, f, pv, w1, w2, layer, tm, after=None):
    L = x1.shape[0]
    dff = w1.shape[2]
    extra = [] if after is None else [pl.BlockSpec(memory_space=pl.ANY)]
    extra_args = [] if after is None else [after]

    def body(dx2_ref, x1_ref, a_ref, f_ref, pv_ref, w1_ref, w2_ref, *rest):
        dx1_ref, p_ref, da_ref, df_ref, vs_ref = rest[len(extra):]

        @pl.when(pl.program_id(0) == 0)
        def _():
            vs_ref[...] = jnp.zeros_like(vs_ref)

        dx2v, p = dx2_ref[...], pv_ref[...]
        dfb = _bf(dx2v * p[R_G2:R_G2 + 1])
        df_ref[...] = dfb
        vs_ref[0:1, :] += _sum0(dx2v * f_ref[...])
        dp = _dot_nt(dfb, w2_ref[...])
        ra = jnp.maximum(a_ref[...], 0.0)
        p_ref[...] = _bf(ra * ra)
        dab = _bf(dp * (2.0 * ra))
        da_ref[...] = dab
        dh2 = _dot_nt(dab, w1_ref[...])
        _, xn, r = _norm_mod(x1_ref[...], p[R_N2:R_N2 + 1], p[R_SC2:R_SC2 + 1], p[R_SH2:R_SH2 + 1])
        dx1_ref[...] = dx2v + _norm_mod_bwd(dh2, xn, r, p[R_N2:R_N2 + 1], p[R_SC2:R_SC2 + 1])
        vs_ref[1:2, :] += _sum0(dh2 * xn)
        vs_ref[2:3, :] += _sum0(dh2)

    return pl.pallas_call(
        body, name=f"ffn_bwd{layer}", grid=(L // tm,),
        in_specs=[_rows(tm, D), _rows(tm, D), _rows(tm, dff), _rows(tm, D),
                  pl.BlockSpec((None, 8, D), lambda i: (layer, 0, 0)), _layer_w(D, dff, 0), _layer_w(dff, D, 0)] + extra,
        out_specs=[_rows(tm, D), _rows(tm, dff), _rows(tm, dff), _rows(tm, D), _whole((8, D))],
        out_shape=[jax.ShapeDtypeStruct((L, D), F32), jax.ShapeDtypeStruct((L, dff), BF16),
                   jax.ShapeDtypeStruct((L, dff), BF16), jax.ShapeDtypeStruct((L, D), BF16),
                   jax.ShapeDtypeStruct((8, D), F32)],
        compiler_params=_params(1, 56),
    )(dx2, x1, a, f, pv, w1, w2, *extra_args)


def _conv_fwd(x, pv, w_in, w_out, cw, layer, j, tm, after=None):
    L = x.shape[0]

    def body(x_ref, pv_ref, win_ref, wout_ref, cw_ref, x1_ref, h_ref, bcx_ref, conv_ref, q_ref, y_ref, carry):
        @pl.when(pl.program_id(0) == 0)
        def _():
            carry[...] = jnp.zeros_like(carry)

        xv, p, cwv = x_ref[...], pv_ref[...], cw_ref[...]
        h, _, _ = _norm_mod(xv, p[R_N1:R_N1 + 1], p[R_SC1:R_SC1 + 1], p[R_SH1:R_SH1 + 1])
        hb = _bf(h)
        h_ref[...] = hb
        bcx = _dot(hb, win_ref[...])
        bcx_ref[...] = bcx
        z = bcx[:, D:2 * D] * bcx[:, 2 * D:]
        prev8 = carry[...]
        conv = cwv[0:1] * _shift_down(z, prev8, 2) + cwv[1:2] * _shift_down(z, prev8, 1) + cwv[2:3] * z + cwv[3:4]
        conv_ref[...] = conv
        qb = _bf(bcx[:, :D] * conv)
        q_ref[...] = qb
        y = _dot(qb, wout_ref[...])
        y_ref[...] = y
        x1_ref[...] = xv + p[R_G1:R_G1 + 1] * y
        carry[...] = z[tm - 8:tm]

    call, tail = _call_after(
        after, body, 5, name=f"conv_fwd{layer}", grid=(L // tm,),
        in_specs=[_rows(tm, D), pl.BlockSpec((None, 8, D), lambda i: (layer, 0, 0)), _layer_w(D, 3 * D, 0), _layer_w(D, D, 0),
                  pl.BlockSpec((None, 8, D), lambda i: (j, 0, 0))],
        out_specs=[_rows(tm, D), _rows(tm, D), _rows(tm, 3 * D), _rows(tm, D), _rows(tm, D), _rows(tm, D)],
        out_shape=[jax.ShapeDtypeStruct((L, D), F32), jax.ShapeDtypeStruct((L, D), BF16), jax.ShapeDtypeStruct((L, 3 * D), F32),
                   jax.ShapeDtypeStruct((L, D), F32), jax.ShapeDtypeStruct((L, D), BF16), jax.ShapeDtypeStruct((L, D), F32)],
        scratch_shapes=[pltpu.VMEM((8, D), F32)],
        compiler_params=_params(1, 56),
    )
    return call(x, pv, w_in, w_out, cw, *tail)


def _conv_bwd(dx1, x, y, bcx, conv, pv, w_in, w_out, cw, layer, j, tm, after=None):
    L = x.shape[0]
    nt = L // tm

    def body(dx1_ref, x_ref, y_ref, bcx_ref, conv_ref, halo_ref, pv_ref, win_ref, wout_ref, cw_ref,
             dx_ref, dbcx_ref, dy_ref, vs_ref, carry):
        gi = pl.program_id(0)
        tile = nt - 1 - gi

        @pl.when(gi == 0)
        def _():
            vs_ref[...] = jnp.zeros_like(vs_ref)
            carry[...] = jnp.zeros_like(carry)

        dx1v, p, cwv = dx1_ref[...], pv_ref[...], cw_ref[...]
        dyb = _bf(dx1v * p[R_G1:R_G1 + 1])
        dy_ref[...] = dyb
        vs_ref[0:1, :] += _sum0(dx1v * y_ref[...])
        dq = _dot_nt(dyb, wout_ref[...])
        bcx = bcx_ref[...]
        b, cg, xh = bcx[:, :D], bcx[:, D:2 * D], bcx[:, 2 * D:]
        db = dq * conv_ref[...]
        dc = dq * b
        z = cg * xh
        halo = halo_ref[...]
        zprev = jnp.where(tile > 0, halo[:, D:2 * D] * halo[:, 2 * D:], 0.0)
        vs_ref[3:4, :] += _sum0(dc * _shift_down(z, zprev, 2))
        vs_ref[4:5, :] += _sum0(dc * _shift_down(z, zprev, 1))
        vs_ref[5:6, :] += _sum0(dc * z)
        vs_ref[6:7, :] += _sum0(dc)
        next8 = carry[...]
        dz = cwv[2:3] * dc + cwv[1:2] * _shift_up(dc, next8, 1) + cwv[0:1] * _shift_up(dc, next8, 2)
        dbb, dcgb, dxhb = _bf(db), _bf(dz * xh), _bf(dz * cg)
        dbcx_ref[:, 0:D] = dbb
        dbcx_ref[:, D:2 * D] = dcgb
        dbcx_ref[:, 2 * D:3 * D] = dxhb
        dh = (_dot_nt(dbb, win_ref[:, 0:D]) + _dot_nt(dcgb, win_ref[:, D:2 * D])) + _dot_nt(dxhb, win_ref[:, 2 * D:3 * D])
        _, xn, r = _norm_mod(x_ref[...], p[R_N1:R_N1 + 1], p[R_SC1:R_SC1 + 1], p[R_SH1:R_SH1 + 1])
        dx_ref[...] = dx1v + _norm_mod_bwd(dh, xn, r, p[R_N1:R_N1 + 1], p[R_SC1:R_SC1 + 1])
        vs_ref[1:2, :] += _sum0(dh * xn)
        vs_ref[2:3, :] += _sum0(dh)
        carry[...] = dc[0:8]

    halo_spec = pl.BlockSpec((8, 3 * D), lambda i: (jnp.maximum((nt - 1 - i) * (tm // 8) - 1, 0), 0))
    call, tail = _call_after(
        after, body, 10, name=f"conv_bwd{layer}", grid=(nt,),
        in_specs=[_rows(tm, D, nt), _rows(tm, D, nt), _rows(tm, D, nt), _rows(tm, 3 * D, nt), _rows(tm, D, nt), halo_spec,
                  pl.BlockSpec((None, 8, D), lambda i: (layer, 0, 0)), _layer_w(D, 3 * D, 0), _layer_w(D, D, 0),
                  pl.BlockSpec((None, 8, D), lambda i: (j, 0, 0))],
        out_specs=[_rows(tm, D, nt), _rows(tm, 3 * D, nt), _rows(tm, D, nt), _whole((8, D))],
        out_shape=[jax.ShapeDtypeStruct((L, D), F32), jax.ShapeDtypeStruct((L, 3 * D), BF16),
                   jax.ShapeDtypeStruct((L, D), BF16), jax.ShapeDtypeStruct((8, D), F32)],
        scratch_shapes=[pltpu.VMEM((8, D), F32)],
        compiler_params=_params(1, 56),
    )
    return call(dx1, x, y, bcx, conv, bcx, pv, w_in, w_out, cw, *tail)


def _s5_discretize(a_re, a_im, log_dt, bt_re, bt_im):
    dt = jnp.exp(log_dt)
    mag = jnp.exp(a_re * dt)
    abar_re = mag * jnp.cos(a_im * dt)
    abar_im = mag * jnp.sin(a_im * dt)
    den = a_re * a_re + a_im * a_im
    nr = abar_re - 1.0
    ni = abar_im
    f_re = (nr * a_re + ni * a_im) / den
    f_im = (ni * a_re - nr * a_im) / den
    bbar_re = f_re * bt_re - f_im * bt_im
    bbar_im = f_re * bt_im + f_im * bt_re
    return abar_re, abar_im, bbar_re, bbar_im


def _s5_params_fwd(a_re, a_im, log_dt, bt_re, bt_im):
    def body(ar, ai, ld, br, bi, o_ar, o_ai, o_br, o_bi):
        r = _s5_discretize(ar[...], ai[...], ld[...], br[...], bi[...])
        o_ar[...], o_ai[...], o_br[...], o_bi[...] = r

    gp = jax.ShapeDtypeStruct((S5_G, S5_P), F32)
    hgp = jax.ShapeDtypeStruct((S5_H, S5_G, S5_P), F32)
    return pl.pallas_call(body, name="s5_params_fwd", out_shape=[gp, gp, hgp, hgp])(a_re, a_im, log_dt, bt_re, bt_im)


def _s5_params_bwd(a_re, a_im, log_dt, bt_re, bt_im, d_ar, d_ai, d_br, d_bi):
    def body(ar, ai, ld, br, bi, gar, gai, gbr, gbi, o_ar, o_ai, o_ld, o_br, o_bi):
        _, vjp = jax.vjp(_s5_discretize, ar[...], ai[...], ld[...], br[...], bi[...])
        r = vjp((gar[...], gai[...], gbr[...], gbi[...]))
        o_ar[...], o_ai[...], o_ld[...], o_br[...], o_bi[...] = r

    gp = jax.ShapeDtypeStruct((S5_G, S5_P), F32)
    hgp = jax.ShapeDtypeStruct((S5_H, S5_G, S5_P), F32)
    return pl.pallas_call(body, name="s5_params_bwd", out_shape=[gp, gp, jax.ShapeDtypeStruct((S5_G, 1), F32), hgp, hgp])(
        a_re, a_im, log_dt, bt_re, bt_im, d_ar, d_ai, d_br, d_bi)


def _s5_in_fwd(x, pv, w_in, b_re, b_im, layer, tm, after=None):
    L = x.shape[0]

    def body(x_ref, pv_ref, win_ref, bre_ref, bim_ref, h_ref, u_ref, ore_ref, oim_ref):
        p = pv_ref[...]
        h, _, _ = _norm_mod(x_ref[...], p[R_N1:R_N1 + 1], p[R_SC1:R_SC1 + 1], p[R_SH1:R_SH1 + 1])
        hb = _bf(h)
        h_ref[...] = hb
        u = _dot(hb, win_ref[...])
        u_ref[...] = u
        ub = _bf(u)
        for k in range(S5_NB):
            uk = ub[:, k * S5_BH:(k + 1) * S5_BH]
            ore_ref[:, k * S5_BP:(k + 1) * S5_BP] = _dot(uk, bre_ref[k])
            oim_ref[:, k * S5_BP:(k + 1) * S5_BP] = _dot(uk, bim_ref[k])

    call, tail = _call_after(
        after, body, 5, name="s5_in_fwd", grid=(L // tm,),
        in_specs=[_rows(tm, D), pl.BlockSpec((None, 8, D), lambda i: (layer, 0, 0)), _layer_w(D, D, 0),
                  _const_w((S5_NB, S5_BH, S5_BP)), _const_w((S5_NB, S5_BH, S5_BP))],
        out_specs=[_rows(tm, D), _rows(tm, D), _rows(tm, NSTATE), _rows(tm, NSTATE)],
        out_shape=[jax.ShapeDtypeStruct((L, D), BF16), jax.ShapeDtypeStruct((L, D), F32),
                   jax.ShapeDtypeStruct((L, NSTATE), F32), jax.ShapeDtypeStruct((L, NSTATE), F32)],
        compiler_params=_params(1, 56),
    )
    return call(x, pv, w_in, b_re, b_im, *tail)


def _s5_scan_fwd(bu_re, bu_im, ar, ai, tr):
    L = bu_re.shape[0]
    nl = 1024

    def body(bre_ref, bim_ref, ar_ref, ai_ref, sre_ref, sim_ref, st_re, st_im):
        @pl.when(pl.program_id(1) == 0)
        def _():
            st_re[...] = jnp.zeros_like(st_re)
            st_im[...] = jnp.zeros_like(st_im)

        a_r, a_i = ar_ref[...], ai_ref[...]

        def step(t, carry):
            s_r, s_i = carry
            n_r = a_r * s_r - a_i * s_i + bre_ref[pl.ds(t, 1), :]
            n_i = a_r * s_i + a_i * s_r + bim_ref[pl.ds(t, 1), :]
            sre_ref[pl.ds(t, 1), :] = n_r
            sim_ref[pl.ds(t, 1), :] = n_i
            return n_r, n_i

        s_r, s_i = lax.fori_loop(0, tr, step, (st_re[...], st_im[...]), unroll=8)
        st_re[...] = s_r
        st_im[...] = s_i

    blk = pl.BlockSpec((tr, nl), lambda j, i: (i, j))
    vec = pl.BlockSpec((1, nl), lambda j, i: (0, j))
    return pl.pallas_call(
        body, name="s5_scan_fwd", grid=(NSTATE // nl, L // tr),
        in_specs=[blk, blk, vec, vec], out_specs=[blk, blk],
        out_shape=[jax.ShapeDtypeStruct((L, NSTATE), F32)] * 2,
        scratch_shapes=[pltpu.VMEM((1, nl), F32), pltpu.VMEM((1, nl), F32)],
        compiler_params=_params(2),
    )(bu_re, bu_im, ar, ai)


def _s5_out_fwd(x, u, s_re, s_im, pv, c_re, c_im, dvec, glu_w, glu_b, w_out, layer, tm):
    L = x.shape[0]

    def body(x_ref, u_ref, sre_ref, sim_ref, pv_ref, cre_ref, cim_ref, d_ref, gw_ref, gb_ref, wout_ref,
             x1_ref, y1_ref, zg_ref, y3_ref, y_ref):
        p = pv_ref[...]
        srb, sib = _bf(sre_ref[...]), _bf(sim_ref[...])
        parts = []
        for k in range(S5_NB):
            sl = slice(k * S5_BP, (k + 1) * S5_BP)
            parts.append(_dot(srb[:, sl], cre_ref[k]) - _dot(sib[:, sl], cim_ref[k]))
        y1 = jnp.concatenate(parts, axis=1) + d_ref[...] * u_ref[...]
        y1_ref[...] = y1
        y2 = jax.nn.gelu(y1)
        zg = _dot(_bf(y2), gw_ref[...]) + gb_ref[...]
        zg_ref[...] = zg
        y3b = _bf(y2 * jax.nn.sigmoid(zg))
        y3_ref[...] = y3b
        y = _dot(y3b, wout_ref[...])
        y_ref[...] = y
        x1_ref[...] = x_ref[...] + p[R_G1:R_G1 + 1] * y

    return pl.pallas_call(
        body, name="s5_out_fwd", grid=(L // tm,),
        in_specs=[_rows(tm, D), _rows(tm, D), _rows(tm, NSTATE), _rows(tm, NSTATE),
                  pl.BlockSpec((None, 8, D), lambda i: (layer, 0, 0)),
                  _const_w((S5_NB, S5_BP, S5_BH)), _const_w((S5_NB, S5_BP, S5_BH)), _whole((1, D)),
                  _layer_w(D, D, 0), _whole((1, D)), _layer_w(D, D, 0)],
        out_specs=[_rows(tm, D)] * 5,
        out_shape=[jax.ShapeDtypeStruct((L, D), F32), jax.ShapeDtypeStruct((L, D), F32), jax.ShapeDtypeStruct((L, D), F32),
                   jax.ShapeDtypeStruct((L, D), BF16), jax.ShapeDtypeStruct((L, D), F32)],
        compiler_params=_params(1, 56),
    )(x, u, s_re, s_im, pv, c_re, c_im, dvec, glu_w, glu_b, w_out)


def _s5_out_bwd(dx1, y, y1, zg, u, pv, c_re, c_im, dvec, glu_w, w_out, layer, tm, after=None):
    L = dx1.shape[0]

    def body(dx1_ref, y_ref, y1_ref, zg_ref, u_ref, pv_ref, cre_ref, cim_ref, d_ref, gw_ref, wout_ref,
             dy_ref, y2_ref, dzg_ref, dy1_ref, dus_ref, gre_ref, gim_ref, vs_ref):
        @pl.when(pl.program_id(0) == 0)
        def _():
            vs_ref[...] = jnp.zeros_like(vs_ref)

        dx1v, p = dx1_ref[...], pv_ref[...]
        dyb = _bf(dx1v * p[R_G1:R_G1 + 1])
        dy_ref[...] = dyb
        vs_ref[0:1, :] += _sum0(dx1v * y_ref[...])
        dy3 = _dot_nt(dyb, wout_ref[...])
        y2, gelu_vjp = jax.vjp(jax.nn.gelu, y1_ref[...])
        y2_ref[...] = _bf(y2)
        gate = jax.nn.sigmoid(zg_ref[...])
        dzg = dy3 * y2 * gate * (1.0 - gate)
        dzgb = _bf(dzg)
        dzg_ref[...] = dzgb
        vs_ref[1:2, :] += _sum0(dzg)
        dy2 = dy3 * gate + _dot_nt(dzgb, gw_ref[...])
        dy1 = gelu_vjp(dy2)[0]
        vs_ref[2:3, :] += _sum0(dy1 * u_ref[...])
        dus_ref[...] = dy1 * d_ref[...]
        dy1b = _bf(dy1)
        dy1_ref[...] = dy1b
        for k in range(S5_NB):
            dk = dy1b[:, k * S5_BH:(k + 1) * S5_BH]
            gre_ref[:, k * S5_BP:(k + 1) * S5_BP] = _dot_nt(dk, cre_ref[k])
            gim_ref[:, k * S5_BP:(k + 1) * S5_BP] = -_dot_nt(dk, cim_ref[k])

    call, tail = _call_after(
        after, body, 11, name="s5_out_bwd", grid=(L // tm,),
        in_specs=[_rows(tm, D)] * 5 + [pl.BlockSpec((None, 8, D), lambda i: (layer, 0, 0)),
                  _const_w((S5_NB, S5_BP, S5_BH)), _const_w((S5_NB, S5_BP, S5_BH)), _whole((1, D)),
                  _layer_w(D, D, 0), _layer_w(D, D, 0)],
        out_specs=[_rows(tm, D)] * 5 + [_rows(tm, NSTATE), _rows(tm, NSTATE), _whole((8, D))],
        out_shape=[jax.ShapeDtypeStruct((L, D), BF16)] * 4 + [jax.ShapeDtypeStruct((L, D), F32),
                   jax.ShapeDtypeStruct((L, NSTATE), F32), jax.ShapeDtypeStruct((L, NSTATE), F32),
                   jax.ShapeDtypeStruct((8, D), F32)],
        compiler_params=_params(1, 56),
    )
    return call(dx1, y, y1, zg, u, pv, c_re, c_im, dvec, glu_w, w_out, *tail)


def _s5_scan_bwd(g_re, g_im, s_re, s_im, ar, ai, tr):
    L = g_re.shape[0]
    nl = 1024
    nt = L // tr

    def body(gre_ref, gim_ref, sre_ref, sim_ref, hre_ref, him_ref, ar_ref, ai_ref, lre_ref, lim_ref, da_ref, st_re, st_im):
        gi = pl.program_id(1)
        tile = nt - 1 - gi

        @pl.when(gi == 0)
        def _():
            st_re[...] = jnp.zeros_like(st_re)
            st_im[...] = jnp.zeros_like(st_im)
            da_ref[...] = jnp.zeros_like(da_ref)

        a_r, a_i = ar_ref[...], ai_ref[...]

        def step(k, carry):
            l_r, l_i = carry
            t = tr - 1 - k
            n_r = gre_ref[pl.ds(t, 1), :] + a_r * l_r + a_i * l_i
            n_i = gim_ref[pl.ds(t, 1), :] - a_i * l_r + a_r * l_i
            lre_ref[pl.ds(t, 1), :] = n_r
            lim_ref[pl.ds(t, 1), :] = n_i
            return n_r, n_i

        l_r, l_i = lax.fori_loop(0, tr, step, (st_re[...], st_im[...]), unroll=8)
        st_re[...] = l_r
        st_im[...] = l_i
        lam_r, lam_i = lre_ref[...], lim_ref[...]
        p_r = jnp.where(tile > 0, hre_ref[...], 0.0)
        p_i = jnp.where(tile > 0, him_ref[...], 0.0)
        sp_r = _shift_down(sre_ref[...], p_r, 1)
        sp_i = _shift_down(sim_ref[...], p_i, 1)
        da_ref[0:1, :] += _sum0(lam_r * sp_r + lam_i * sp_i)
        da_ref[1:2, :] += _sum0(lam_i * sp_r - lam_r * sp_i)

    blk = pl.BlockSpec((tr, nl), lambda j, i: (nt - 1 - i, j))
    halo = pl.BlockSpec((8, nl), lambda j, i: (jnp.maximum((nt - 1 - i) * (tr // 8) - 1, 0), j))
    vec = pl.BlockSpec((1, nl), lambda j, i: (0, j))
    return pl.pallas_call(
        body, name="s5_scan_bwd", grid=(NSTATE // nl, nt),
        in_specs=[blk, blk, blk, blk, halo, halo, vec, vec],
        out_specs=[blk, blk, pl.BlockSpec((8, nl), lambda j, i: (0, j))],
        out_shape=[jax.ShapeDtypeStruct((L, NSTATE), F32)] * 2 + [jax.ShapeDtypeStruct((8, NSTATE), F32)],
        scratch_shapes=[pltpu.VMEM((1, nl), F32), pltpu.VMEM((1, nl), F32)],
        compiler_params=_params(2),
    )(g_re, g_im, s_re, s_im, s_re, s_im, ar, ai)


def _s5_in_bwd(dx1, lam_re, lam_im, du_skip, x, pv, b_re, b_im, w_in, layer, tm):
    L = x.shape[0]

    def body(dx1_ref, lre_ref, lim_ref, dus_ref, x_ref, pv_ref, bre_ref, bim_ref, win_ref, dx_ref, du_ref, vs_ref):
        @pl.when(pl.program_id(0) == 0)
        def _():
            vs_ref[...] = jnp.zeros_like(vs_ref)

        p = pv_ref[...]
        lrb, lib = _bf(lre_ref[...]), _bf(lim_ref[...])
        parts = []
        for k in range(S5_NB):
            sl = slice(k * S5_BP, (k + 1) * S5_BP)
            parts.append(_dot_nt(lrb[:, sl], bre_ref[k]) + _dot_nt(lib[:, sl], bim_ref[k]))
        dub = _bf(jnp.concatenate(parts, axis=1) + dus_ref[...])
        du_ref[...] = dub
        dh = _dot_nt(dub, win_ref[...])
        _, xn, r = _norm_mod(x_ref[...], p[R_N1:R_N1 + 1], p[R_SC1:R_SC1 + 1], p[R_SH1:R_SH1 + 1])
        dx_ref[...] = dx1_ref[...] + _norm_mod_bwd(dh, xn, r, p[R_N1:R_N1 + 1], p[R_SC1:R_SC1 + 1])
        vs_ref[1:2, :] += _sum0(dh * xn)
        vs_ref[2:3, :] += _sum0(dh)

    return pl.pallas_call(
        body, name="s5_in_bwd", grid=(L // tm,),
        in_specs=[_rows(tm, D), _rows(tm, NSTATE), _rows(tm, NSTATE), _rows(tm, D), _rows(tm, D),
                  pl.BlockSpec((None, 8, D), lambda i: (layer, 0, 0)),
                  _const_w((S5_NB, S5_BH, S5_BP)), _const_w((S5_NB, S5_BH, S5_BP)), _layer_w(D, D, 0)],
        out_specs=[_rows(tm, D), _rows(tm, D), _whole((8, D))],
        out_shape=[jax.ShapeDtypeStruct((L, D), F32), jax.ShapeDtypeStruct((L, D), BF16), jax.ShapeDtypeStruct((8, D), F32)],
        compiler_params=_params(1, 56),
    )(dx1, lam_re, lam_im, du_skip, x, pv, b_re, b_im, w_in)


def _blockdiag_b(bt):
    b = bt.reshape(S5_H, S5_NB, 16, S5_P).transpose(1, 2, 0, 3)
    eye = jnp.eye(16, dtype=bt.dtype)
    return (b[:, :, :, None, :] * eye[None, :, None, :, None]).reshape(S5_NB, S5_BH, S5_BP)


def _unblock_b(d):
    d = jnp.einsum("bghgp->bghp", d.reshape(S5_NB, 16, S5_H, 16, S5_P))
    return d.transpose(2, 0, 1, 3).reshape(S5_H, S5_G, S5_P)


def _blockdiag_c(cm):
    c4 = cm.reshape(S5_NB, 16, S5_H, S5_P)
    eye = jnp.eye(16, dtype=cm.dtype)
    out = c4.transpose(0, 1, 3, 2)[:, :, :, None, :] * eye[None, :, None, :, None]
    return out.reshape(S5_NB, S5_BP, S5_BH)


def _unblock_c(d):
    d = jnp.einsum("bgpgh->bghp", d.reshape(S5_NB, 16, S5_P, 16, S5_H))
    return d.reshape(S5_G, S5_H, S5_P)


def _tril_mask():
    return lax.broadcasted_iota(jnp.int32, (SG_CHUNK, SG_CHUNK), 0) >= lax.broadcasted_iota(jnp.int32, (SG_CHUNK, SG_CHUNK), 1)


def _sg_fwd(x, pv, w_in, w_s, b_t, vg, w_out, layer, tm, after=None):
    L = x.shape[0]
    nc = tm // SG_CHUNK

    def body(x_ref, pv_ref, win_ref, ws_ref, bt_ref, vg_ref, wout_ref, x1_ref, h_ref, uv_ref, vm_ref, q_ref, y_ref):
        xv, p = x_ref[...], pv_ref[...]
        h, _, _ = _norm_mod(xv, p[R_N1:R_N1 + 1], p[R_SC1:R_SC1 + 1], p[R_SH1:R_SH1 + 1])
        hb = _bf(h)
        h_ref[...] = hb
        uv = _dot(hb, win_ref[...])
        uv_ref[...] = uv
        v = uv[:, D:]
        rv = lax.rsqrt(jnp.mean(v * v, axis=-1, keepdims=True) + EPS)
        vnb = _bf((v * rv) * vg_ref[...])
        mask = _tril_mask()
        bt = bt_ref[...]
        for hd in range(SG_HEADS):
            wm = _bf(jnp.where(mask, ws_ref[hd], 0.0))
            cs = slice(hd * SG_CHUNK, (hd + 1) * SG_CHUNK)
            for ck in range(nc):
                rs = slice(ck * SG_CHUNK, (ck + 1) * SG_CHUNK)
                vm_ref[rs, cs] = _dot(wm, vnb[rs, cs]) + bt[:, hd:hd + 1]
        qb = _bf(uv[:, :D] * vm_ref[...])
        q_ref[...] = qb
        y = _dot(qb, wout_ref[...])
        y_ref[...] = y
        x1_ref[...] = xv + p[R_G1:R_G1 + 1] * y

    call, tail = _call_after(
        after, body, 7, name="sg_fwd", grid=(L // tm,),
        in_specs=[_rows(tm, D), pl.BlockSpec((None, 8, D), lambda i: (layer, 0, 0)), _layer_w(D, 2 * D, 0),
                  _whole((SG_HEADS, SG_CHUNK, SG_CHUNK)), _whole((SG_CHUNK, SG_HEADS)), _whole((1, D)), _layer_w(D, D, 0)],
        out_specs=[_rows(tm, D), _rows(tm, D), _rows(tm, 2 * D), _rows(tm, D), _rows(tm, D), _rows(tm, D)],
        out_shape=[jax.ShapeDtypeStruct((L, D), F32), jax.ShapeDtypeStruct((L, D), BF16), jax.ShapeDtypeStruct((L, 2 * D), F32),
                   jax.ShapeDtypeStruct((L, D), F32), jax.ShapeDtypeStruct((L, D), BF16), jax.ShapeDtypeStruct((L, D), F32)],
        compiler_params=_params(1, 56),
    )
    return call(x, pv, w_in, w_s, b_t, vg, w_out, *tail)


def _sg_bwd(dx1, x, y, uv, vm, pv, w_in, w_s, vg, w_out, layer, tm, after=None):
    L = x.shape[0]
    nc = tm // SG_CHUNK

    def body(dx1_ref, x_ref, y_ref, uv_ref, vm_ref, pv_ref, win_ref, ws_ref, vg_ref, wout_ref,
             dx_ref, duv_ref, dy_ref, vs_ref, dws_ref, dbt_ref, dvn_scr):
        @pl.when(pl.program_id(0) == 0)
        def _():
            vs_ref[...] = jnp.zeros_like(vs_ref)
            dws_ref[...] = jnp.zeros_like(dws_ref)
            dbt_ref[...] = jnp.zeros_like(dbt_ref)

        dx1v, p = dx1_ref[...], pv_ref[...]
        dyb = _bf(dx1v * p[R_G1:R_G1 + 1])
        dy_ref[...] = dyb
        vs_ref[0:1, :] += _sum0(dx1v * y_ref[...])
        dq = _dot_nt(dyb, wout_ref[...])
        uv = uv_ref[...]
        u, v = uv[:, :D], uv[:, D:]
        dub = _bf(dq * vm_ref[...])
        dvm = dq * u
        dvmb = _bf(dvm)
        rv = lax.rsqrt(jnp.mean(v * v, axis=-1, keepdims=True) + EPS)
        vh = v * rv
        vgv = vg_ref[...]
        vnb = _bf(vh * vgv)
        mask = _tril_mask()
        for hd in range(SG_HEADS):
            wm = _bf(jnp.where(mask, ws_ref[hd], 0.0))
            cs = slice(hd * SG_CHUNK, (hd + 1) * SG_CHUNK)
            dws = jnp.zeros((SG_CHUNK, SG_CHUNK), F32)
            dbs = jnp.zeros((SG_CHUNK, 1), F32)
            for ck in range(nc):
                rs = slice(ck * SG_CHUNK, (ck + 1) * SG_CHUNK)
                dvn_scr[rs, cs] = _dot_tn(wm, dvmb[rs, cs])
                dws = dws + _dot_nt(dvmb[rs, cs], vnb[rs, cs])
                dbs = dbs + jnp.sum(dvm[rs, cs], axis=1, keepdims=True)
            dws_ref[hd] += jnp.where(mask, dws, 0.0)
            dbt_ref[:, hd:hd + 1] += dbs
        dvn = dvn_scr[...]
        vs_ref[3:4, :] += _sum0(dvn * vh)
        dvnn = dvn * vgv
        dvb = _bf(rv * (dvnn - vh * jnp.mean(dvnn * vh, axis=-1, keepdims=True)))
        duv_ref[:, 0:D] = dub
        duv_ref[:, D:2 * D] = dvb
        dh = _dot_nt(dub, win_ref[:, 0:D]) + _dot_nt(dvb, win_ref[:, D:2 * D])
        _, xn, r = _norm_mod(x_ref[...], p[R_N1:R_N1 + 1], p[R_SC1:R_SC1 + 1], p[R_SH1:R_SH1 + 1])
        dx_ref[...] = dx1v + _norm_mod_bwd(dh, xn, r, p[R_N1:R_N1 + 1], p[R_SC1:R_SC1 + 1])
        vs_ref[1:2, :] += _sum0(dh * xn)
        vs_ref[2:3, :] += _sum0(dh)

    call, tail = _call_after(
        after, body, 10, name="sg_bwd", grid=(L // tm,),
        in_specs=[_rows(tm, D), _rows(tm, D), _rows(tm, D), _rows(tm, 2 * D), _rows(tm, D),
                  pl.BlockSpec((None, 8, D), lambda i: (layer, 0, 0)), _layer_w(D, 2 * D, 0),
                  _whole((SG_HEADS, SG_CHUNK, SG_CHUNK)), _whole((1, D)), _layer_w(D, D, 0)],
        out_specs=[_rows(tm, D), _rows(tm, 2 * D), _rows(tm, D), _whole((8, D)),
                   _whole((SG_HEADS, SG_CHUNK, SG_CHUNK)), _whole((SG_CHUNK, SG_HEADS))],
        out_shape=[jax.ShapeDtypeStruct((L, D), F32), jax.ShapeDtypeStruct((L, 2 * D), BF16), jax.ShapeDtypeStruct((L, D), BF16),
                   jax.ShapeDtypeStruct((8, D), F32), jax.ShapeDtypeStruct((SG_HEADS, SG_CHUNK, SG_CHUNK), F32),
                   jax.ShapeDtypeStruct((SG_CHUNK, SG_HEADS), F32)],
        scratch_shapes=[pltpu.VMEM((tm, D), F32)],
        compiler_params=_params(1, 56),
    )
    return call(dx1, x, y, uv, vm, pv, w_in, w_s, vg, w_out, *tail)


def _final(x, target, fg, tm):
    L = x.shape[0]

    def body(x_ref, t_ref, g_ref, dx_ref, vs_ref):
        @pl.when(pl.program_id(0) == 0)
        def _():
            vs_ref[...] = jnp.zeros_like(vs_ref)

        xv, g = x_ref[...], g_ref[...]
        r = lax.rsqrt(jnp.mean(xv * xv, axis=-1, keepdims=True) + EPS)
        xn = xv * r
        e = xn * g - t_ref[...]
        vs_ref[0:1, :] += jnp.sum(e * e)
        dout = e * (1.0 / D)
        vs_ref[1:2, :] += _sum0(dout * xn)
        dxn = dout * g
        dx_ref[...] = r * (dxn - xn * jnp.mean(dxn * xn, axis=-1, keepdims=True))

    return pl.pallas_call(
        body, name="final_loss", grid=(L // tm,),
        in_specs=[_rows(tm, D), _rows(tm, D), _whole((1, D))],
        out_specs=[_rows(tm, D), _whole((8, D))],
        out_shape=[jax.ShapeDtypeStruct((L, D), F32), jax.ShapeDtypeStruct((8, D), F32)],
        compiler_params=_params(1),
    )(x, target, fg)


def _pack(arrs):
    flat = jnp.concatenate([a.reshape(-1).astype(F32) for a in arrs])
    rows = -(-flat.shape[0] // LANES)
    rows = -(-rows // 8) * 8
    return jnp.pad(flat, (0, rows * LANES - flat.shape[0])).reshape(rows, LANES)


def _unpack(buf, shapes, lead=()):
    flat = buf.reshape(lead + (-1,))
    out, off = [], 0
    for s in shapes:
        n = 1
        for d in s:
            n *= d
        out.append(flat[..., off:off + n].reshape(lead + tuple(s)))
        off += n
    return out


BIG = ("ff_w1", "ff_w2", "conv_w_in", "conv_w_out", "ssm_w_in", "ssm_glu_w", "ssm_w_out", "sg_w_in", "sg_w_out")
BIG_AXIS = {"ff_w1": 2, "ff_w2": 1, "conv_w_in": 2, "conv_w_out": 1, "ssm_w_in": 1, "ssm_glu_w": 1, "ssm_w_out": 1,
            "sg_w_in": 2, "sg_w_out": 1}
LAYER_WEIGHTS = (
    (("conv_w_in", 0), ("conv_w_out", 0), ("ff_w1", 0), ("ff_w2", 0)),
    (("ssm_w_in", 0), ("ssm_glu_w", 0), ("ssm_w_out", 0), ("ff_w1", 1), ("ff_w2", 1)),
    (("sg_w_in", 0), ("sg_w_out", 0), ("ff_w1", 2), ("ff_w2", 2)),
    (("conv_w_in", 1), ("conv_w_out", 1), ("ff_w1", 3), ("ff_w2", 3)),
)
SMALL = ("ada_b", "norm1_g", "norm2_g", "final_g", "conv_w", "conv_b", "ssm_a_re", "ssm_a_im", "ssm_log_dt", "ssm_b_re",
         "ssm_b_im", "ssm_c_re", "ssm_c_im", "ssm_d", "ssm_glu_b", "sg_v_g", "sg_w_s", "sg_b_s")
WEIGHTS = ("ada_w", "ada_b", "norm1_g", "norm2_g", "ff_w1", "ff_w2", "final_g", "conv_w_in", "conv_w", "conv_b", "conv_w_out",
           "ssm_w_in", "ssm_a_re", "ssm_a_im", "ssm_log_dt", "ssm_b_re", "ssm_b_im", "ssm_c_re", "ssm_c_im", "ssm_d",
           "ssm_glu_w", "ssm_glu_b", "ssm_w_out", "sg_w_in", "sg_v_g", "sg_w_s", "sg_b_s", "sg_w_out")


def kernel(x, c, ada_w, ada_b, norm1_g, norm2_g, ff_w1, ff_w2, final_g, conv_w_in, conv_w, conv_b, conv_w_out, ssm_w_in, ssm_a_re, ssm_a_im, ssm_log_dt, ssm_b_re, ssm_b_im, ssm_c_re, ssm_c_im, ssm_d, ssm_glu_w, ssm_glu_b, ssm_w_out, sg_w_in, sg_v_g, sg_w_s, sg_b_s, sg_w_out, loss_target, m_ada_w, m_ada_b, m_norm1_g, m_norm2_g, m_ff_w1, m_ff_w2, m_final_g, m_conv_w_in, m_conv_w, m_conv_b, m_conv_w_out, m_ssm_w_in, m_ssm_a_re, m_ssm_a_im, m_ssm_log_dt, m_ssm_b_re, m_ssm_b_im, m_ssm_c_re, m_ssm_c_im, m_ssm_d, m_ssm_glu_w, m_ssm_glu_b, m_ssm_w_out, m_sg_w_in, m_sg_v_g, m_sg_w_s, m_sg_b_s, m_sg_w_out, v_ada_w, v_ada_b, v_norm1_g, v_norm2_g, v_ff_w1, v_ff_w2, v_final_g, v_conv_w_in, v_conv_w, v_conv_b, v_conv_w_out, v_ssm_w_in, v_ssm_a_re, v_ssm_a_im, v_ssm_log_dt, v_ssm_b_re, v_ssm_b_im, v_ssm_c_re, v_ssm_c_im, v_ssm_d, v_ssm_glu_w, v_ssm_glu_b, v_ssm_w_out, v_sg_w_in, v_sg_v_g, v_sg_w_s, v_sg_b_s, v_sg_w_out):
    args = dict(locals())
    w = {n: args[n] for n in WEIGHTS}
    m = {n: args["m_" + n] for n in WEIGHTS}
    v = {n: args["v_" + n] for n in WEIGHTS}
    L = x.shape[1]
    tm = min(L, 256)
    tr = min(L, 512)
    chip = 2 * lax.axis_index("x") + lax.axis_index("y")
    me = 2 * chip + lax.axis_index("c")
    xin = x[0]
    target = loss_target[0]
    chip1 = chip.reshape(1).astype(jnp.int32)

    gathers = []
    def start_gather(i, after):
        axes = [BIG_AXIS[n] for n, _ in LAYER_WEIGHTS[i]]
        lands = [_cast_place(w[n], li, BIG_AXIS[n], chip1, f"cast_{n}_{li}") for n, li in LAYER_WEIGHTS[i]]
        s_sems, r_sems, lands, token = _gather_start(lands, axes, f"gather_start{i}", after)
        gathers.append((s_sems, r_sems, lands, axes))
        return token

    def weights_of(i, after):
        s_sems, r_sems, lands, axes = gathers[i]
        lands = _gather_wait(s_sems, r_sems, lands, axes, f"gather_wait{i}", after)
        return dict(zip([n for n, _ in LAYER_WEIGHTS[i]], _gather_share(lands, axes, f"gather_share{i}")))

    small_in = _pack([c, conv_w, conv_b, sg_v_g])
    got = _allgather_small(small_in, "gather_small_inputs").reshape(N_DEV, -1)
    c_all, cw_sh, cb_sh, vg_sh = _unpack(got, [(D,), conv_w.shape, conv_b.shape, sg_v_g.shape], lead=(N_DEV,))
    conv_w_full = jnp.concatenate([cw_sh[2 * k] for k in range(4)], axis=-1)
    conv_b_full = jnp.concatenate([cb_sh[2 * k] for k in range(4)], axis=-1)
    vg_full = jnp.concatenate([vg_sh[2 * k] for k in range(4)], axis=-1)
    c16 = jnp.pad(c_all, ((0, 16 - N_DEV), (0, 0)))

    cols = ada_w.shape[2]
    ada_b_cols = lax.dynamic_slice_in_dim(ada_b, chip * cols, cols, axis=1)[:, None, :]
    mod_sh = _ada_fwd(c16, ada_w, ada_b_cols)[:, :N_DEV, :]
    mod_all = _allgather_small(_pack([mod_sh]), "gather_mod").reshape(N_DEV, -1)
    mod_all = _unpack(mod_all, [mod_sh.shape], lead=(N_DEV,))[0]
    mod_mine = lax.dynamic_index_in_dim(mod_all[0::2], me, axis=2, keepdims=False)
    mod_mine = mod_mine.transpose(1, 0, 2).reshape(DEPTH, 6, D)
    pv = jnp.concatenate([mod_mine, norm1_g[:, None, :], norm2_g[:, None, :]], axis=1)

    start_gather(0, pv)

    cw_rows = jnp.concatenate([conv_w_full, conv_b_full[:, None, :], jnp.zeros((conv_w_full.shape[0], 4, D), F32)], axis=1)

    a_re, a_im = ssm_a_re[0], ssm_a_im[0]
    log_dt = ssm_log_dt[0][:, None]
    bt_re, bt_im = ssm_b_re[0].transpose(2, 0, 1), ssm_b_im[0].transpose(2, 0, 1)
    abar_re, abar_im, bbar_re, bbar_im = _s5_params_fwd(a_re, a_im, log_dt, bt_re, bt_im)
    ar_vec, ai_vec = abar_re.reshape(1, NSTATE), abar_im.reshape(1, NSTATE)
    bd_re, bd_im = _bf(_blockdiag_b(bbar_re)), _bf(_blockdiag_b(bbar_im))
    cd_re, cd_im = _bf(_blockdiag_c(ssm_c_re[0])), _bf(_blockdiag_c(ssm_c_im[0]))

    saved = []
    fulls = []
    xl = xin
    for i in range(DEPTH):
        kind = MIXER_OF_LAYER[i]
        j = i // 3
        full = weights_of(i, cd_im if i == 0 else xl)
        fulls.append(full)
        tok = start_gather(i + 1, full["ff_w2"]) if i + 1 < DEPTH else None
        if kind == 0:
            x1, h, bcx, conv, q, y = _conv_fwd(xl, pv, full["conv_w_in"], full["conv_w_out"], cw_rows, i, j, tm, after=tok)
            mix = dict(h=h, bcx=bcx, conv=conv, q=q, y=y)
        elif kind == 1:
            h, u, bu_re, bu_im = _s5_in_fwd(xl, pv, full["ssm_w_in"], bd_re, bd_im, i, tm, after=tok)
            s_re, s_im = _s5_scan_fwd(bu_re, bu_im, ar_vec, ai_vec, tr)
            x1, y1, zg, y3, y = _s5_out_fwd(xl, u, s_re, s_im, pv, cd_re, cd_im, ssm_d, full["ssm_glu_w"], ssm_glu_b,
                                            full["ssm_w_out"], i, tm)
            mix = dict(h=h, u=u, s_re=s_re, s_im=s_im, y1=y1, zg=zg, y3=y3, y=y)
        else:
            x1, h, uv, vm, q, y = _sg_fwd(xl, pv, full["sg_w_in"], sg_w_s[0], sg_b_s[0].T, vg_full, full["sg_w_out"], i, tm,
                                          after=tok)
            mix = dict(h=h, uv=uv, vm=vm, q=q, y=y)
        x2, h2, a, f = _ffn_fwd(x1, pv, full["ff_w1"], full["ff_w2"], i, tm)
        saved.append(dict(x=xl, x1=x1, h2=h2, a=a, f=f, **mix))
        xl = x2

    dxl, vs_fin = _final(xl, target, final_g[None, :], tm)

    gfull = {n: [None] * w[n].shape[0] for n in BIG}
    vs_mix, vs_ffn = [None] * DEPTH, [None] * DEPTH
    small_g = {}
    scatters = [None] * DEPTH
    token = None

    def start_scatter(i, after):
        garrs = [gfull[n][li][None] for n, li in LAYER_WEIGHTS[i]]
        gaxes = [BIG_AXIS[n] for n, _ in LAYER_WEIGHTS[i]]
        s_sems, r_sems, garrs, lands, tok = _scatter_start(garrs, gaxes, f"scatter_start{i}", after)
        scatters[i] = (s_sems, r_sems, garrs, lands, gaxes)
        return tok

    for i in reversed(range(DEPTH)):
        kind = MIXER_OF_LAYER[i]
        j = i // 3
        sv = saved[i]
        full = fulls[i]
        dx1, p_b, da_b, df_b, vs_ffn[i] = _ffn_bwd(dxl, sv["x1"], sv["a"], sv["f"], pv, full["ff_w1"], full["ff_w2"], i, tm,
                                                   after=token)
        gfull["ff_w1"][i] = _mm_tn(sv["h2"], da_b, f"wgrad_ff_w1_{i}")
        gfull["ff_w2"][i] = _mm_tn(p_b, df_b, f"wgrad_ff_w2_{i}")
        if kind == 0:
            dxl, dbcx_b, dy_b, vsm = _conv_bwd(dx1, sv["x"], sv["y"], sv["bcx"], sv["conv"], pv, full["conv_w_in"],
                                               full["conv_w_out"], cw_rows, i, j, tm)
            gfull["conv_w_in"][j] = _mm_tn(sv["h"], dbcx_b, f"wgrad_conv_w_in_{j}")
            gfull["conv_w_out"][j] = _mm_tn(sv["q"], dy_b, f"wgrad_conv_w_out_{j}")
            small_g.setdefault("conv_w", [None, None])[j] = vsm[3:6]
            small_g.setdefault("conv_b", [None, None])[j] = vsm[6]
        elif kind == 1:
            dy_b, y2_b, dzg_b, dy1_b, du_skip, g_re, g_im, vsm = _s5_out_bwd(
                dx1, sv["y"], sv["y1"], sv["zg"], sv["u"], pv, cd_re, cd_im, ssm_d, full["ssm_glu_w"], full["ssm_w_out"], i, tm)
            lam_re, lam_im, dabar = _s5_scan_bwd(g_re, g_im, sv["s_re"], sv["s_im"], ar_vec, ai_vec, tr)
            dxl, du_b, vs_in = _s5_in_bwd(dx1, lam_re, lam_im, du_skip, sv["x"], pv, bd_re, bd_im, full["ssm_w_in"], i, tm)
            gfull["ssm_w_out"][0] = _mm_tn(sv["y3"], dy_b, "wgrad_ssm_w_out")
            gfull["ssm_glu_w"][0] = _mm_tn(y2_b, dzg_b, "wgrad_ssm_glu_w")
            gfull["ssm_w_in"][0] = _mm_tn(sv["h"], du_b, "wgrad_ssm_w_in")
            d_cre = _unblock_c(_mm_tn_blocks(sv["s_re"], dy1_b, S5_BP, S5_BH, "wgrad_s5_c_re"))
            d_cim = -_unblock_c(_mm_tn_blocks(sv["s_im"], dy1_b, S5_BP, S5_BH, "wgrad_s5_c_im"))
            d_bbre = _unblock_b(_mm_tn_blocks(sv["u"], lam_re, S5_BH, S5_BP, "wgrad_s5_b_re"))
            d_bbim = _unblock_b(_mm_tn_blocks(sv["u"], lam_im, S5_BH, S5_BP, "wgrad_s5_b_im"))
            d_are, d_aim, d_ldt, d_btre, d_btim = _s5_params_bwd(
                a_re, a_im, log_dt, bt_re, bt_im, dabar[0].reshape(S5_G, S5_P), dabar[1].reshape(S5_G, S5_P), d_bbre, d_bbim)
            small_g.update(ssm_a_re=d_are, ssm_a_im=d_aim, ssm_log_dt=d_ldt, ssm_b_re=d_btre.transpose(1, 2, 0),
                           ssm_b_im=d_btim.transpose(1, 2, 0), ssm_c_re=d_cre, ssm_c_im=d_cim, ssm_d=vsm[2], ssm_glu_b=vsm[1])
            vsm = jnp.concatenate([vsm[0:1], vs_in[1:3], jnp.zeros((5, D), F32)], axis=0)
        else:
            dxl, duv_b, dy_b, vsm, d_ws, d_bt = _sg_bwd(dx1, sv["x"], sv["y"], sv["uv"], sv["vm"], pv, full["sg_w_in"],
                                                        sg_w_s[0], vg_full, full["sg_w_out"], i, tm)
            gfull["sg_w_in"][0] = _mm_tn(sv["h"], duv_b, "wgrad_sg_w_in")
            gfull["sg_w_out"][0] = _mm_tn(sv["q"], dy_b, "wgrad_sg_w_out")
            small_g.update(sg_v_g=vsm[3], sg_w_s=d_ws, sg_b_s=d_bt.T)
        vs_mix[i] = vsm
        if i > 0:
            token = start_scatter(i, dxl)
    grad_x = dxl[None]

    dmod = _mod_bwd(jnp.stack(vs_mix), jnp.stack(vs_ffn), pv)
    small_g.update(ada_b=dmod[:, :6, :], norm1_g=dmod[:, 6, :], norm2_g=dmod[:, 7, :], final_g=vs_fin[1],
                   conv_w=jnp.stack(small_g["conv_w"]), conv_b=jnp.stack(small_g["conv_b"]))

    loss_part = (0.5 / D) * vs_fin[0, 0:1]
    part_shapes = [(1,)] + [tuple(small_g[n].shape) for n in SMALL]
    parts_all, parts_sum = _allgather_small(_pack([loss_part] + [small_g[n] for n in SMALL]), "reduce_small_grads", with_sum=True)
    tok0 = start_scatter(0, parts_sum)[0:1, 0:1]
    summed = _unpack(parts_sum, part_shapes)
    loss = summed[0][0]
    gsum = dict(zip(SMALL, summed[1:]))
    dmod_all = _unpack(parts_all.reshape(N_DEV, -1), part_shapes[:2], lead=(N_DEV,))[1]
    dmod_all = dmod_all.reshape(N_DEV, DEPTH, 6 * D)
    dmod_cols = lax.dynamic_slice_in_dim(dmod_all, chip * cols, cols, axis=2).transpose(1, 0, 2)
    g_ada_w = _ada_bwd(c16, jnp.pad(dmod_cols + tok0[0], ((0, 0), (0, 16 - N_DEV), (0, 0))))

    res = {}
    shp = ada_w.shape
    two = lambda t: t.reshape(shp[0] * shp[1], shp[2])
    res["ada_w"] = [t.reshape(shp) for t in _adamw(two(ada_w), [two(g_ada_w)], two(m_ada_w), two(v_ada_w), "adamw_ada_w")]

    def mine(n, g):
        if n in ("conv_w", "conv_b", "sg_v_g"):
            size = w[n].shape[-1]
            return lax.dynamic_slice_in_dim(g, chip * size, size, axis=g.ndim - 1)
        return g

    g_loc = [mine(n, gsum[n]).reshape(w[n].shape) for n in SMALL]
    small_shapes = [tuple(w[n].shape) for n in SMALL]
    packed = _adamw(_pack([w[n] for n in SMALL]), [_pack(g_loc) + tok0], _pack([m[n] for n in SMALL]), _pack([v[n] for n in SMALL]),
                    "adamw_small")
    unpacked = [_unpack(t, small_shapes) for t in packed]
    for k, n in enumerate(SMALL):
        res[n] = [unpacked[0][k], unpacked[1][k], unpacked[2][k], unpacked[3][k]]

    sums = {n: [None] * w[n].shape[0] for n in BIG}
    after = packed[0]
    for i in reversed(range(DEPTH)):
        s_sems, r_sems, garrs, lands, gaxes = scatters[i]
        garrs, recv = _scatter_wait(s_sems, r_sems, garrs, lands, gaxes, f"scatter_wait{i}", after)
        for (n, li), g, r3, ax in zip(LAYER_WEIGHTS[i], garrs, recv, gaxes):
            sums[n][li] = _sum_parts(r3, g, ax, chip1, f"sum_{n}_{li}")
            after = sums[n][li]
    q_mine = [jnp.stack(sums[n]) for n in BIG]
    q_sib = _swap_with_sibling(q_mine, "swap_grad_sums")
    for n, qa, qb in zip(BIG, q_mine, q_sib):
        shp = w[n].shape
        two = lambda t, shp=shp: t.reshape(shp[0] * shp[1], shp[2])
        res[n] = [t.reshape(shp) for t in _adamw(two(w[n]), [two(qa), two(qb)], two(m[n]), two(v[n]), f"adamw_{n}")]

    outs = [loss, grad_x]
    for part in range(4):
        outs += [res[n][part] for n in WEIGHTS]
    return tuple(outs)
```

```python
import functools

import jax
import jax.numpy as jnp
from jax import lax
from jax.experimental import pallas as pl
from jax.experimental.pallas import tpu as pltpu

F32 = jnp.float32
BF16 = jnp.bfloat16
D = 1024
EPS = 1e-6
DEPTH = 4
MIXER_OF_LAYER = (0, 1, 2, 0)
S5_G, S5_H, S5_P = 64, 16, 64
S5_NB = 4
S5_BH = S5_H * 16
S5_BP = S5_P * 16
NSTATE = S5_G * S5_P
SG_HEADS, SG_CHUNK = 8, 128
ADAM_LR, ADAM_B1, ADAM_B2, ADAM_EPS, ADAM_WD, ADAM_STEP = 0.001, 0.9, 0.999, 1e-08, 0.01, 10
N_DEV = 8
MESH = pl.DeviceIdType.MESH
LANES = 1024
R_SH1, R_SC1, R_G1, R_SH2, R_SC2, R_G2, R_N1, R_N2 = range(8)


def _dot(a, b):
    return jnp.dot(a, b, preferred_element_type=F32)


def _dot_nt(a, b):
    return lax.dot_general(a, b, (((1,), (1,)), ((), ())), preferred_element_type=F32)


def _dot_tn(a, b):
    return lax.dot_general(a, b, (((0,), (0,)), ((), ())), preferred_element_type=F32)


def _bf(x):
    return x.astype(BF16)


def _sum0(x):
    return jnp.sum(x, axis=0, keepdims=True)


def _params(n_axes, vmem_mb=48):
    return pltpu.CompilerParams(dimension_semantics=("arbitrary",) * n_axes, vmem_limit_bytes=vmem_mb << 20)


def _rows(tm, cols, nt=None):
    if nt is None:
        return pl.BlockSpec((tm, cols), lambda i: (i, 0))
    return pl.BlockSpec((tm, cols), lambda i: (nt - 1 - i, 0))


def _whole(shape):
    nd = len(shape)
    return pl.BlockSpec(shape, lambda *_: (0,) * nd)


def _layer_w(r, c, layer):
    return pl.BlockSpec((None, r, c), lambda *_: (layer, 0, 0), pipeline_mode=pl.Buffered(1))


def _const_w(shape):
    nd = len(shape)
    return pl.BlockSpec(shape, lambda *_: (0,) * nd, pipeline_mode=pl.Buffered(1))


def _call_after(after, body, n_in, *, in_specs, **kw):
    if after is None:
        return pl.pallas_call(body, in_specs=in_specs, **kw), ()

    def body_after(*refs):
        return body(*refs[:n_in], *refs[n_in + 1:])

    return pl.pallas_call(body_after, in_specs=list(in_specs) + [pl.BlockSpec(memory_space=pl.ANY)], **kw), (after,)


def _norm_mod(x, ng, sc, sh):
    r = lax.rsqrt(jnp.mean(x * x, axis=-1, keepdims=True) + EPS)
    xn = x * r
    return (xn * ng) * (1.0 + sc) + sh, xn, r


def _norm_mod_bwd(dh, xn, r, ng, sc):
    dxn = dh * (ng * (1.0 + sc))
    return r * (dxn - xn * jnp.mean(dxn * xn, axis=-1, keepdims=True))


def _shift_down(z, prev8, k):
    row = lax.broadcasted_iota(jnp.int32, z.shape, 0)
    if k == 1:
        return jnp.where(row >= 1, pltpu.roll(z, 1, 0), prev8[7:8])
    return jnp.where(row >= 2, pltpu.roll(z, 2, 0), jnp.where(row == 0, prev8[6:7], prev8[7:8]))


def _shift_up(z, next8, k):
    n = z.shape[0]
    row = lax.broadcasted_iota(jnp.int32, z.shape, 0)
    if k == 1:
        return jnp.where(row <= n - 2, pltpu.roll(z, n - 1, 0), next8[0:1])
    return jnp.where(row <= n - 3, pltpu.roll(z, n - 2, 0), jnp.where(row == n - 2, next8[0:1], next8[1:2]))


def _place():
    x, y, c = lax.axis_index("x"), lax.axis_index("y"), lax.axis_index("c")
    chips = [(1 - x, y), (x, 1 - y), (1 - x, 1 - y)]
    return x, y, c, chips


def _allgather_small(x_shard, name, after=None):
    m_per, n = x_shard.shape

    def body(x_ref, out_ref, send_sems, recv_sems, local_sem):
        x, y, c, chips = _place()
        me, sibling = (x, y, c), (x, y, 1 - c)

        def rows(px, py, pc):
            return out_ref.at[pl.ds((4 * px + 2 * py + pc) * m_per, m_per), :]

        def copy(k, block, to, src=None):
            return pltpu.make_async_remote_copy(
                src_ref=rows(*block) if src is None else src, dst_ref=rows(*block),
                send_sem=send_sems.at[k], recv_sem=recv_sems.at[k], device_id=to, device_id_type=MESH)

        mine = pltpu.make_async_copy(x_ref, rows(*me), local_sem)
        mine.start()
        first = [copy(0, me, sibling, src=x_ref)]
        first += [copy(1 + j, me, (*chip, c), src=x_ref) for j, chip in enumerate(chips)]
        for cp in first:
            cp.start()
        passed = [copy(4 + j, (*chip, c), sibling) for j, chip in enumerate(chips)]
        for j, chip in enumerate(chips):
            copy(1 + j, (*chip, c), me).wait_recv()
            passed[j].start()
        copy(0, sibling, me).wait_recv()
        for j, chip in enumerate(chips):
            copy(4 + j, (*chip, 1 - c), me).wait_recv()
        for cp in first + passed:
            cp.wait_send()
        mine.wait()

    call, tail = _call_after(
        after, body, 1, name=name, out_shape=jax.ShapeDtypeStruct((N_DEV * m_per, n), F32),
        in_specs=[pl.BlockSpec(memory_space=pltpu.VMEM)], out_specs=pl.BlockSpec(memory_space=pltpu.VMEM),
        scratch_shapes=[pltpu.SemaphoreType.DMA((7,)), pltpu.SemaphoreType.DMA((7,)), pltpu.SemaphoreType.DMA],
        compiler_params=pltpu.CompilerParams(vmem_limit_bytes=48 << 20),
    )
    return call(x_shard, *tail)


def _allreduce_small(x_part, name, after=None):
    m, n = x_part.shape
    h = m // 2

    def body(x_ref, out_ref, sib_buf, slots, send_sems, recv_sems):
        x, y, c, chips = _place()
        k_me = 2 * x + y
        sibling = (x, y, 1 - c)
        mine = pl.ds(pl.multiple_of(c * h, 8), h)

        def copy(k, src, dst, to):
            return pltpu.make_async_remote_copy(src_ref=src, dst_ref=dst, send_sem=send_sems.at[k], recv_sem=recv_sems.at[k],
                                                device_id=to, device_id_type=MESH)

        swap = copy(0, x_ref, sib_buf, sibling)
        swap.start()
        swap.wait()
        slots[pl.ds(k_me, 1)] = (x_ref[mine, :] + sib_buf[mine, :])[None]
        my_slot = slots.at[pl.ds(k_me, 1)]
        sends = [copy(1 + j, my_slot, my_slot, (*chip, c)) for j, chip in enumerate(chips)]
        for cp in sends:
            cp.start()
        for j, chip in enumerate(chips):
            their_slot = slots.at[pl.ds(2 * chip[0] + chip[1], 1)]
            copy(1 + j, their_slot, their_slot, (x, y, c)).wait_recv()
        for cp in sends:
            cp.wait_send()
        out_ref[mine, :] = ((slots[0] + slots[1]) + slots[2]) + slots[3]
        give = copy(4, out_ref.at[mine, :], out_ref.at[mine, :], sibling)
        give.start()
        give.wait_send()
        theirs = pl.ds(pl.multiple_of((1 - c) * h, 8), h)
        copy(4, out_ref.at[theirs, :], out_ref.at[theirs, :], (x, y, c)).wait_recv()

    call, tail = _call_after(
        after, body, 1, name=name, out_shape=jax.ShapeDtypeStruct((m, n), F32),
        in_specs=[pl.BlockSpec(memory_space=pltpu.VMEM)], out_specs=pl.BlockSpec(memory_space=pltpu.VMEM),
        scratch_shapes=[pltpu.VMEM((m, n), F32), pltpu.VMEM((4, h, n), F32),
                        pltpu.SemaphoreType.DMA((5,)), pltpu.SemaphoreType.DMA((5,))],
        compiler_params=pltpu.CompilerParams(vmem_limit_bytes=48 << 20),
    )
    return call(x_part, *tail)


def _shard_region(ref, full_shape, axis, chip_k, half=None):
    _, r, c = full_shape
    if axis == 1:
        rs = r // 4
        if half is None:
            return ref.at[:, pl.ds(pl.multiple_of(chip_k * rs, 128), rs), :]
        return ref.at[:, pl.ds(pl.multiple_of(chip_k * rs + half * (rs // 2), 128), rs // 2), :]
    cs = c // 4
    if half is None:
        return ref.at[:, :, pl.ds(pl.multiple_of(chip_k * cs, 128), cs)]
    return ref.at[:, pl.ds(pl.multiple_of(half * (r // 2), 128), r // 2), pl.ds(pl.multiple_of(chip_k * cs, 128), cs)]


HBM_SPEC = pl.BlockSpec(memory_space=pltpu.HBM)
SEM_SPEC = pl.BlockSpec(memory_space=pltpu.SEMAPHORE)
ANY_SPEC = pl.BlockSpec(memory_space=pl.ANY)
SPLIT_COPY_PARAMS = pltpu.CompilerParams(has_side_effects=pltpu.SideEffectType.DATAFLOW_SIDE_EFFECTING)


def _in_hbm(arrs):
    return [pltpu.with_memory_space_constraint(a, pltpu.HBM) for a in arrs]


def _cast_place(w_stack, li, axis, chip, name):
    _, r, c = w_stack.shape
    full = (1, 4 * r, c) if axis == 1 else (1, r, 4 * c)
    tr = min(r, 256)
    if axis == 1:
        out_spec = pl.BlockSpec((None, tr, c), lambda i, k: (0, k[0] * (r // tr) + i, 0))
    else:
        out_spec = pl.BlockSpec((None, tr, c), lambda i, k: (0, i, k[0]))

    def body(k_ref, w_ref, o_ref):
        o_ref[...] = _bf(w_ref[...])

    return pl.pallas_call(
        body, name=name,
        grid_spec=pltpu.PrefetchScalarGridSpec(
            num_scalar_prefetch=1, grid=(r // tr,),
            in_specs=[pl.BlockSpec((None, tr, c), lambda i, k: (li, i, 0))], out_specs=out_spec),
        out_shape=jax.ShapeDtypeStruct(full, BF16),
        compiler_params=_params(1),
    )(chip, w_stack)


def _gather_start(lands, axes, name, after):
    n_arr = len(lands)
    fulls = [tuple(l.shape) for l in lands]

    def body(*refs):
        land = refs[:n_arr]
        send_sems, recv_sems = refs[n_arr + 1:n_arr + 3]
        token = refs[-1]
        x, y, c, chips = _place()
        k_me = 2 * x + y
        for a in range(n_arr):
            mine = _shard_region(land[a], fulls[a], axes[a], k_me, c)
            for j, chip in enumerate(chips):
                pltpu.make_async_remote_copy(
                    src_ref=mine, dst_ref=mine, send_sem=send_sems.at[a * 3 + j], recv_sem=recv_sems.at[a * 3 + j],
                    device_id=(*chip, c), device_id_type=MESH).start()
        token[...] = jnp.zeros_like(token)

    res = pl.pallas_call(
        body, name=name,
        out_shape=(pltpu.SemaphoreType.DMA((3 * n_arr,)), pltpu.SemaphoreType.DMA((3 * n_arr,)),
                   *[pltpu.HBM(f, BF16) for f in fulls], jax.ShapeDtypeStruct((8, 128), F32)),
        in_specs=[HBM_SPEC] * n_arr + [ANY_SPEC],
        out_specs=(SEM_SPEC, SEM_SPEC, *[HBM_SPEC] * n_arr, pl.BlockSpec(memory_space=pltpu.VMEM)),
        input_output_aliases={a: 2 + a for a in range(n_arr)},
        compiler_params=SPLIT_COPY_PARAMS,
    )(*_in_hbm(lands), after)
    return res[0], res[1], list(res[2:2 + n_arr]), res[-1]


def _gather_wait(send_sems, recv_sems, lands, axes, name, after):
    n_arr = len(lands)
    fulls = [tuple(l.shape) for l in lands]

    def body(*refs):
        land = refs[:n_arr]
        s_sems, r_sems = refs[n_arr:n_arr + 2]
        x, y, c, chips = _place()
        for a in range(n_arr):
            for j, chip in enumerate(chips):
                k_j = 2 * chip[0] + chip[1]
                got = _shard_region(land[a], fulls[a], axes[a], k_j, c)
                cp = pltpu.make_async_remote_copy(
                    src_ref=got, dst_ref=got, send_sem=s_sems.at[a * 3 + j], recv_sem=r_sems.at[a * 3 + j],
                    device_id=(x, y, c), device_id_type=MESH)
                cp.wait_send()
                cp.wait_recv()

    res = pl.pallas_call(
        body, name=name,
        out_shape=tuple(pltpu.HBM(f, BF16) for f in fulls),
        in_specs=[HBM_SPEC] * n_arr + [SEM_SPEC, SEM_SPEC, ANY_SPEC],
        out_specs=tuple([HBM_SPEC] * n_arr),
        input_output_aliases={a: a for a in range(n_arr)},
        compiler_params=SPLIT_COPY_PARAMS,
    )(*lands, send_sems, recv_sems, after)
    return list(res)


def _gather_share(lands, axes, name):
    n_arr = len(lands)
    fulls = [tuple(l.shape) for l in lands]

    def body(*refs):
        land_in, land = refs[:n_arr], refs[n_arr:2 * n_arr]
        send_sems, recv_sems = refs[2 * n_arr:]
        x, y, c, chips = _place()
        copies = []
        for a in range(n_arr):
            for j, chip in enumerate(chips):
                k_j = 2 * chip[0] + chip[1]
                cp = pltpu.make_async_remote_copy(
                    src_ref=_shard_region(land_in[a], fulls[a], axes[a], k_j, c),
                    dst_ref=_shard_region(land[a], fulls[a], axes[a], k_j, c),
                    send_sem=send_sems.at[a * 3 + j], recv_sem=recv_sems.at[a * 3 + j],
                    device_id=(x, y, 1 - c), device_id_type=MESH)
                cp.start()
                copies.append(cp)
        for cp in copies:
            cp.wait()

    return pl.pallas_call(
        body, name=name, out_shape=[jax.ShapeDtypeStruct(f, BF16) for f in fulls],
        in_specs=[ANY_SPEC] * n_arr, out_specs=[ANY_SPEC] * n_arr,
        input_output_aliases={a: a for a in range(n_arr)},
        scratch_shapes=[pltpu.SemaphoreType.DMA((3 * n_arr,)), pltpu.SemaphoreType.DMA((3 * n_arr,))],
    )(*lands)


def _scatter_shapes(grads, axes):
    out = []
    for g, ax in zip(grads, axes):
        shp = list(g.shape)
        shp[ax] //= 4
        out.append((3,) + tuple(shp[1:]))
    return out


def _scatter_start(grads, axes, name, after):
    n_arr = len(grads)
    shapes = _scatter_shapes(grads, axes)
    lands = [lax.empty(s, BF16) for s in shapes]

    def body(*refs):
        ins, land = refs[:n_arr], refs[n_arr:2 * n_arr]
        send_sems, recv_sems = refs[2 * n_arr + 1:2 * n_arr + 3]
        token = refs[-1]
        x, y, c, chips = _place()
        for a in range(n_arr):
            for j, chip in enumerate(chips):
                k_j = 2 * chip[0] + chip[1]
                pltpu.make_async_remote_copy(
                    src_ref=_shard_region(ins[a], grads[a].shape, axes[a], k_j), dst_ref=land[a].at[pl.ds(j, 1)],
                    send_sem=send_sems.at[a * 3 + j], recv_sem=recv_sems.at[a * 3 + j],
                    device_id=(*chip, c), device_id_type=MESH).start()
        token[...] = jnp.zeros_like(token)

    res = pl.pallas_call(
        body, name=name,
        out_shape=(pltpu.SemaphoreType.DMA((3 * n_arr,)), pltpu.SemaphoreType.DMA((3 * n_arr,)),
                   *[pltpu.HBM(g.shape, BF16) for g in grads], *[pltpu.HBM(s, BF16) for s in shapes],
                   jax.ShapeDtypeStruct((8, 128), F32)),
        in_specs=[HBM_SPEC] * (2 * n_arr) + [ANY_SPEC],
        out_specs=(SEM_SPEC, SEM_SPEC, *[HBM_SPEC] * (2 * n_arr), pl.BlockSpec(memory_space=pltpu.VMEM)),
        input_output_aliases={a: 2 + a for a in range(2 * n_arr)},
        compiler_params=SPLIT_COPY_PARAMS,
    )(*_in_hbm(grads), *_in_hbm(lands), after)
    return res[0], res[1], list(res[2:2 + n_arr]), list(res[2 + n_arr:2 + 2 * n_arr]), res[-1]


def _scatter_wait(send_sems, recv_sems, grads, lands, axes, name, after):
    n_arr = len(grads)

    def body(*refs):
        ins, land = refs[:n_arr], refs[n_arr:2 * n_arr]
        s_sems, r_sems = refs[2 * n_arr:2 * n_arr + 2]
        x, y, c, chips = _place()
        for a in range(n_arr):
            for j, chip in enumerate(chips):
                k_j = 2 * chip[0] + chip[1]
                cp = pltpu.make_async_remote_copy(
                    src_ref=_shard_region(ins[a], grads[a].shape, axes[a], k_j), dst_ref=land[a].at[pl.ds(j, 1)],
                    send_sem=s_sems.at[a * 3 + j], recv_sem=r_sems.at[a * 3 + j],
                    device_id=(x, y, c), device_id_type=MESH)
                cp.wait_send()
                cp.wait_recv()

    res = pl.pallas_call(
        body, name=name,
        out_shape=(*[pltpu.HBM(g.shape, BF16) for g in grads], *[pltpu.HBM(l.shape, BF16) for l in lands]),
        in_specs=[HBM_SPEC] * (2 * n_arr) + [SEM_SPEC, SEM_SPEC, ANY_SPEC],
        out_specs=tuple([HBM_SPEC] * (2 * n_arr)),
        input_output_aliases={a: a for a in range(2 * n_arr)},
        compiler_params=SPLIT_COPY_PARAMS,
    )(*grads, *lands, send_sems, recv_sems, after)
    return list(res[:n_arr]), list(res[n_arr:])


def _swap_with_sibling(arrs, name):
    n_arr = len(arrs)

    def body(*refs):
        ins, outs = refs[:n_arr], refs[n_arr:2 * n_arr]
        send_sems, recv_sems = refs[2 * n_arr:]
        x, y, c, _ = _place()
        copies = []
        for a in range(n_arr):
            cp = pltpu.make_async_remote_copy(
                src_ref=ins[a], dst_ref=outs[a], send_sem=send_sems.at[a], recv_sem=recv_sems.at[a],
                device_id=(x, y, 1 - c), device_id_type=MESH)
            cp.start()
            copies.append(cp)
        for cp in copies:
            cp.wait()

    any_spec = pl.BlockSpec(memory_space=pl.ANY)
    return pl.pallas_call(
        body, name=name, out_shape=[jax.ShapeDtypeStruct(a.shape, a.dtype) for a in arrs],
        in_specs=[any_spec] * n_arr, out_specs=[any_spec] * n_arr,
        scratch_shapes=[pltpu.SemaphoreType.DMA((n_arr,)), pltpu.SemaphoreType.DMA((n_arr,))],
    )(*arrs)


def _mm_tn(a, b, name, out_dtype=BF16):
    L, m = a.shape
    n = b.shape[1]
    bm, bn, bk = min(m, 1024), min(n, 1024), min(L, 512)
    nk = L // bk

    def body(a_ref, b_ref, o_ref, acc):
        k = pl.program_id(2)

        @pl.when(k == 0)
        def _():
            acc[...] = jnp.zeros_like(acc)

        acc[...] += _dot_tn(_bf(a_ref[...]), _bf(b_ref[...]))

        @pl.when(k == nk - 1)
        def _():
            o_ref[...] = acc[...].astype(out_dtype)

    return pl.pallas_call(
        body, name=name, grid=(m // bm, n // bn, nk),
        in_specs=[pl.BlockSpec((bk, bm), lambda i, j, k: (k, i)), pl.BlockSpec((bk, bn), lambda i, j, k: (k, j))],
        out_specs=pl.BlockSpec((bm, bn), lambda i, j, k: (i, j)),
        out_shape=jax.ShapeDtypeStruct((m, n), out_dtype),
        scratch_shapes=[pltpu.VMEM((bm, bn), F32)],
        compiler_params=_params(3),
    )(a, b)


def _mm_tn_blocks(a, b, wa, wb, name):
    L = a.shape[0]
    nb = a.shape[1] // wa
    bk = min(L, 512)
    nk = L // bk

    def body(a_ref, b_ref, o_ref):
        @pl.when(pl.program_id(1) == 0)
        def _():
            o_ref[...] = jnp.zeros_like(o_ref)

        o_ref[...] += _dot_tn(_bf(a_ref[...]), _bf(b_ref[...]))

    return pl.pallas_call(
        body, name=name, grid=(nb, nk),
        in_specs=[pl.BlockSpec((bk, wa), lambda j, k: (k, j)), pl.BlockSpec((bk, wb), lambda j, k: (k, j))],
        out_specs=pl.BlockSpec((None, wa, wb), lambda j, k: (j, 0, 0)),
        out_shape=jax.ShapeDtypeStruct((nb, wa, wb), F32),
        compiler_params=_params(2),
    )(a, b)


def _sum_parts(parts, own, axis, chip, name):
    _, r, c = parts.shape
    tr = min(r, 256)
    if axis == 1:
        own_spec = pl.BlockSpec((None, tr, c), lambda i, k: (0, k[0] * (r // tr) + i, 0))
    else:
        own_spec = pl.BlockSpec((None, tr, c), lambda i, k: (0, i, k[0]))

    def body(k_ref, p_ref, g_ref, o_ref):
        p = p_ref[...].astype(F32)
        o_ref[...] = ((p[0] + p[1]) + p[2]) + g_ref[...].astype(F32)

    return pl.pallas_call(
        body, name=name,
        grid_spec=pltpu.PrefetchScalarGridSpec(
            num_scalar_prefetch=1, grid=(r // tr,),
            in_specs=[pl.BlockSpec((3, tr, c), lambda i, k: (0, i, 0)), own_spec],
            out_specs=pl.BlockSpec((tr, c), lambda i, k: (i, 0))),
        out_shape=jax.ShapeDtypeStruct((r, c), F32),
        compiler_params=_params(1),
    )(chip, parts, own)


def _adamw(w, g_parts, m, v, name):
    r, c = w.shape
    tr = r
    for cand in (512, 256, 128, 64, 32, 16, 8):
        if r % cand == 0 and cand * c * 4 <= (2 << 20):
            tr = cand
            break
    n_g = len(g_parts)
    c1 = 1.0 / (1.0 - ADAM_B1 ** ADAM_STEP)
    c2 = 1.0 / (1.0 - ADAM_B2 ** ADAM_STEP)

    def body(*refs):
        w_ref, g_refs, m_ref, v_ref = refs[0], refs[1:1 + n_g], refs[1 + n_g], refs[2 + n_g]
        g_out, d_out, m_out, v_out = refs[3 + n_g:]
        g = g_refs[0][...]
        for gr in g_refs[1:]:
            g = g + gr[...]
        m_new = ADAM_B1 * m_ref[...] + (1.0 - ADAM_B1) * g
        v_new = ADAM_B2 * v_ref[...] + (1.0 - ADAM_B2) * (g * g)
        m_hat = m_new * c1
        v_hat = v_new * c2
        g_out[...] = g
        d_out[...] = -ADAM_LR * (m_hat / (jnp.sqrt(v_hat) + ADAM_EPS) + ADAM_WD * w_ref[...])
        m_out[...] = m_new
        v_out[...] = v_new

    spec = pl.BlockSpec((tr, c), lambda i: (i, 0))
    return pl.pallas_call(
        body, name=name, grid=(r // tr,),
        in_specs=[spec] * (3 + n_g), out_specs=[spec] * 4,
        out_shape=[jax.ShapeDtypeStruct((r, c), F32)] * 4,
        compiler_params=_params(1),
    )(w, *g_parts, m, v)


def _ada_fwd(c16, ada_w, ada_b_cols):
    cols = ada_w.shape[2]

    def body(c_ref, w_ref, b_ref, o_ref):
        cv = c_ref[...]
        ca = _bf(cv * jax.nn.sigmoid(cv))
        o_ref[...] = _dot(ca, _bf(w_ref[...])) + b_ref[...]

    return pl.pallas_call(
        body, name="ada_fwd", grid=(DEPTH,),
        in_specs=[_whole((16, D)), pl.BlockSpec((None, D, cols), lambda i: (i, 0, 0)),
                  pl.BlockSpec((None, 1, cols), lambda i: (i, 0, 0))],
        out_specs=pl.BlockSpec((None, 16, cols), lambda i: (i, 0, 0)),
        out_shape=jax.ShapeDtypeStruct((DEPTH, 16, cols), F32),
        compiler_params=_params(1),
    )(c16, ada_w, ada_b_cols)


def _ada_bwd(c16, dmod16):
    cols = dmod16.shape[2]

    def body(c_ref, d_ref, o_ref):
        cv = c_ref[...]
        ca = _bf(cv * jax.nn.sigmoid(cv))
        o_ref[...] = _dot_tn(ca, _bf(d_ref[...]))

    return pl.pallas_call(
        body, name="ada_bwd", grid=(DEPTH,),
        in_specs=[_whole((16, D)), pl.BlockSpec((None, 16, cols), lambda i: (i, 0, 0))],
        out_specs=pl.BlockSpec((None, D, cols), lambda i: (i, 0, 0)),
        out_shape=jax.ShapeDtypeStruct((DEPTH, D, cols), F32),
        compiler_params=_params(1),
    )(c16, dmod16)


def _mod_bwd(vs_mix, vs_ffn, pv):
    def body(m_ref, f_ref, pv_ref, o_ref):
        for i in range(DEPTH):
            vm, vf, p = m_ref[i], f_ref[i], pv_ref[i]
            o_ref[i] = jnp.concatenate([
                vm[2:3], vm[1:2] * p[R_N1:R_N1 + 1], vm[0:1],
                vf[2:3], vf[1:2] * p[R_N2:R_N2 + 1], vf[0:1],
                vm[1:2] * (1.0 + p[R_SC1:R_SC1 + 1]), vf[1:2] * (1.0 + p[R_SC2:R_SC2 + 1])], axis=0)

    return pl.pallas_call(body, name="mod_bwd", out_shape=jax.ShapeDtypeStruct((DEPTH, 8, D), F32))(vs_mix, vs_ffn, pv)


def _ffn_fwd(x1, pv, w1, w2, layer, tm):
    L = x1.shape[0]
    dff = w1.shape[2]

    def body(x1_ref, pv_ref, w1_ref, w2_ref, x2_ref, h2_ref, a_ref, f_ref):
        x1v, p = x1_ref[...], pv_ref[...]
        h2, _, _ = _norm_mod(x1v, p[R_N2:R_N2 + 1], p[R_SC2:R_SC2 + 1], p[R_SH2:R_SH2 + 1])
        hb = _bf(h2)
        h2_ref[...] = hb
        a = _dot(hb, w1_ref[...])
        a_ref[...] = a
        ra = jnp.maximum(a, 0.0)
        f = _dot(_bf(ra * ra), w2_ref[...])
        f_ref[...] = f
        x2_ref[...] = x1v + p[R_G2:R_G2 + 1] * f

    return pl.pallas_call(
        body, name=f"ffn_fwd{layer}", grid=(L // tm,),
        in_specs=[_rows(tm, D), pl.BlockSpec((None, 8, D), lambda i: (layer, 0, 0)), _layer_w(D, dff, 0), _layer_w(dff, D, 0)],
        out_specs=[_rows(tm, D), _rows(tm, D), _rows(tm, dff), _rows(tm, D)],
        out_shape=[jax.ShapeDtypeStruct((L, D), F32), jax.ShapeDtypeStruct((L, D), BF16),
                   jax.ShapeDtypeStruct((L, dff), F32), jax.ShapeDtypeStruct((L, D), F32)],
        compiler_params=_params(1, 56),
    )(x1, pv, w1, w2)


def _ffn_bwd(dx2, x1, a, f, pv, w1, w2, layer, tm, after=None):
    L = x1.shape[0]
    dff = w1.shape[2]
    extra = [] if after is None else [pl.BlockSpec(memory_space=pl.ANY)]
    extra_args = [] if after is None else [after]

    def body(dx2_ref, x1_ref, a_ref, f_ref, pv_ref, w1_ref, w2_ref, *rest):
        dx1_ref, p_ref, da_ref, df_ref, vs_ref = rest[len(extra):]

        @pl.when(pl.program_id(0) == 0)
        def _():
            vs_ref[...] = jnp.zeros_like(vs_ref)

        dx2v, p = dx2_ref[...], pv_ref[...]
        dfb = _bf(dx2v * p[R_G2:R_G2 + 1])
        df_ref[...] = dfb
        vs_ref[0:1, :] += _sum0(dx2v * f_ref[...])
        dp = _dot_nt(dfb, w2_ref[...])
        ra = jnp.maximum(a_ref[...], 0.0)
        p_ref[...] = _bf(ra * ra)
        dab = _bf(dp * (2.0 * ra))
        da_ref[...] = dab
        dh2 = _dot_nt(dab, w1_ref[...])
        _, xn, r = _norm_mod(x1_ref[...], p[R_N2:R_N2 + 1], p[R_SC2:R_SC2 + 1], p[R_SH2:R_SH2 + 1])
        dx1_ref[...] = dx2v + _norm_mod_bwd(dh2, xn, r, p[R_N2:R_N2 + 1], p[R_SC2:R_SC2 + 1])
        vs_ref[1:2, :] += _sum0(dh2 * xn)
        vs_ref[2:3, :] += _sum0(dh2)

    return pl.pallas_call(
        body, name=f"ffn_bwd{layer}", grid=(L // tm,),
        in_specs=[_rows(tm, D), _rows(tm, D), _rows(tm, dff), _rows(tm, D),
                  pl.BlockSpec((None, 8, D), lambda i: (layer, 0, 0)), _layer_w(D, dff, 0), _layer_w(dff, D, 0)] + extra,
        out_specs=[_rows(tm, D), _rows(tm, dff), _rows(tm, dff), _rows(tm, D), _whole((8, D))],
        out_shape=[jax.ShapeDtypeStruct((L, D), F32), jax.ShapeDtypeStruct((L, dff), BF16),
                   jax.ShapeDtypeStruct((L, dff), BF16), jax.ShapeDtypeStruct((L, D), BF16),
                   jax.ShapeDtypeStruct((8, D), F32)],
        compiler_params=_params(1, 56),
    )(dx2, x1, a, f, pv, w1, w2, *extra_args)


def _conv_fwd(x, pv, w_in, w_out, cw, layer, j, tm, after=None):
    L = x.shape[0]

    def body(x_ref, pv_ref, win_ref, wout_ref, cw_ref, x1_ref, h_ref, bcx_ref, conv_ref, q_ref, y_ref, carry):
        @pl.when(pl.program_id(0) == 0)
        def _():
            carry[...] = jnp.zeros_like(carry)

        xv, p, cwv = x_ref[...], pv_ref[...], cw_ref[...]
        h, _, _ = _norm_mod(xv, p[R_N1:R_N1 + 1], p[R_SC1:R_SC1 + 1], p[R_SH1:R_SH1 + 1])
        hb = _bf(h)
        h_ref[...] = hb
        bcx = _dot(hb, win_ref[...])
        bcx_ref[...] = bcx
        z = bcx[:, D:2 * D] * bcx[:, 2 * D:]
        prev8 = carry[...]
        conv = cwv[0:1] * _shift_down(z, prev8, 2) + cwv[1:2] * _shift_down(z, prev8, 1) + cwv[2:3] * z + cwv[3:4]
        conv_ref[...] = conv
        qb = _bf(bcx[:, :D] * conv)
        q_ref[...] = qb
        y = _dot(qb, wout_ref[...])
        y_ref[...] = y
        x1_ref[...] = xv + p[R_G1:R_G1 + 1] * y
        carry[...] = z[tm - 8:tm]

    call, tail = _call_after(
        after, body, 5, name=f"conv_fwd{layer}", grid=(L // tm,),
        in_specs=[_rows(tm, D), pl.BlockSpec((None, 8, D), lambda i: (layer, 0, 0)), _layer_w(D, 3 * D, 0), _layer_w(D, D, 0),
                  pl.BlockSpec((None, 8, D), lambda i: (j, 0, 0))],
        out_specs=[_rows(tm, D), _rows(tm, D), _rows(tm, 3 * D), _rows(tm, D), _rows(tm, D), _rows(tm, D)],
        out_shape=[jax.ShapeDtypeStruct((L, D), F32), jax.ShapeDtypeStruct((L, D), BF16), jax.ShapeDtypeStruct((L, 3 * D), F32),
                   jax.ShapeDtypeStruct((L, D), F32), jax.ShapeDtypeStruct((L, D), BF16), jax.ShapeDtypeStruct((L, D), F32)],
        scratch_shapes=[pltpu.VMEM((8, D), F32)],
        compiler_params=_params(1, 56),
    )
    return call(x, pv, w_in, w_out, cw, *tail)


def _conv_bwd(dx1, x, y, bcx, conv, pv, w_in, w_out, cw, layer, j, tm, after=None):
    L = x.shape[0]
    nt = L // tm

    def body(dx1_ref, x_ref, y_ref, bcx_ref, conv_ref, halo_ref, pv_ref, win_ref, wout_ref, cw_ref,
             dx_ref, dbcx_ref, dy_ref, vs_ref, carry):
        gi = pl.program_id(0)
        tile = nt - 1 - gi

        @pl.when(gi == 0)
        def _():
            vs_ref[...] = jnp.zeros_like(vs_ref)
            carry[...] = jnp.zeros_like(carry)

        dx1v, p, cwv = dx1_ref[...], pv_ref[...], cw_ref[...]
        dyb = _bf(dx1v * p[R_G1:R_G1 + 1])
        dy_ref[...] = dyb
        vs_ref[0:1, :] += _sum0(dx1v * y_ref[...])
        dq = _dot_nt(dyb, wout_ref[...])
        bcx = bcx_ref[...]
        b, cg, xh = bcx[:, :D], bcx[:, D:2 * D], bcx[:, 2 * D:]
        db = dq * conv_ref[...]
        dc = dq * b
        z = cg * xh
        halo = halo_ref[...]
        zprev = jnp.where(tile > 0, halo[:, D:2 * D] * halo[:, 2 * D:], 0.0)
        vs_ref[3:4, :] += _sum0(dc * _shift_down(z, zprev, 2))
        vs_ref[4:5, :] += _sum0(dc * _shift_down(z, zprev, 1))
        vs_ref[5:6, :] += _sum0(dc * z)
        vs_ref[6:7, :] += _sum0(dc)
        next8 = carry[...]
        dz = cwv[2:3] * dc + cwv[1:2] * _shift_up(dc, next8, 1) + cwv[0:1] * _shift_up(dc, next8, 2)
        dbb, dcgb, dxhb = _bf(db), _bf(dz * xh), _bf(dz * cg)
        dbcx_ref[:, 0:D] = dbb
        dbcx_ref[:, D:2 * D] = dcgb
        dbcx_ref[:, 2 * D:3 * D] = dxhb
        dh = (_dot_nt(dbb, win_ref[:, 0:D]) + _dot_nt(dcgb, win_ref[:, D:2 * D])) + _dot_nt(dxhb, win_ref[:, 2 * D:3 * D])
        _, xn, r = _norm_mod(x_ref[...], p[R_N1:R_N1 + 1], p[R_SC1:R_SC1 + 1], p[R_SH1:R_SH1 + 1])
        dx_ref[...] = dx1v + _norm_mod_bwd(dh, xn, r, p[R_N1:R_N1 + 1], p[R_SC1:R_SC1 + 1])
        vs_ref[1:2, :] += _sum0(dh * xn)
        vs_ref[2:3, :] += _sum0(dh)
        carry[...] = dc[0:8]

    halo_spec = pl.BlockSpec((8, 3 * D), lambda i: (jnp.maximum((nt - 1 - i) * (tm // 8) - 1, 0), 0))
    call, tail = _call_after(
        after, body, 10, name=f"conv_bwd{layer}", grid=(nt,),
        in_specs=[_rows(tm, D, nt), _rows(tm, D, nt), _rows(tm, D, nt), _rows(tm, 3 * D, nt), _rows(tm, D, nt), halo_spec,
                  pl.BlockSpec((None, 8, D), lambda i: (layer, 0, 0)), _layer_w(D, 3 * D, 0), _layer_w(D, D, 0),
                  pl.BlockSpec((None, 8, D), lambda i: (j, 0, 0))],
        out_specs=[_rows(tm, D, nt), _rows(tm, 3 * D, nt), _rows(tm, D, nt), _whole((8, D))],
        out_shape=[jax.ShapeDtypeStruct((L, D), F32), jax.ShapeDtypeStruct((L, 3 * D), BF16),
                   jax.ShapeDtypeStruct((L, D), BF16), jax.ShapeDtypeStruct((8, D), F32)],
        scratch_shapes=[pltpu.VMEM((8, D), F32)],
        compiler_params=_params(1, 56),
    )
    return call(dx1, x, y, bcx, conv, bcx, pv, w_in, w_out, cw, *tail)


def _s5_discretize(a_re, a_im, log_dt, bt_re, bt_im):
    dt = jnp.exp(log_dt)
    mag = jnp.exp(a_re * dt)
    abar_re = mag * jnp.cos(a_im * dt)
    abar_im = mag * jnp.sin(a_im * dt)
    den = a_re * a_re + a_im * a_im
    nr = abar_re - 1.0
    ni = abar_im
    f_re = (nr * a_re + ni * a_im) / den
    f_im = (ni * a_re - nr * a_im) / den
    bbar_re = f_re * bt_re - f_im * bt_im
    bbar_im = f_re * bt_im + f_im * bt_re
    return abar_re, abar_im, bbar_re, bbar_im


def _s5_params_fwd(a_re, a_im, log_dt, bt_re, bt_im):
    def body(ar, ai, ld, br, bi, o_ar, o_ai, o_br, o_bi):
        r = _s5_discretize(ar[...], ai[...], ld[...], br[...], bi[...])
        o_ar[...], o_ai[...], o_br[...], o_bi[...] = r

    gp = jax.ShapeDtypeStruct((S5_G, S5_P), F32)
    hgp = jax.ShapeDtypeStruct((S5_H, S5_G, S5_P), F32)
    return pl.pallas_call(body, name="s5_params_fwd", out_shape=[gp, gp, hgp, hgp])(a_re, a_im, log_dt, bt_re, bt_im)


def _s5_params_bwd(a_re, a_im, log_dt, bt_re, bt_im, d_ar, d_ai, d_br, d_bi):
    def body(ar, ai, ld, br, bi, gar, gai, gbr, gbi, o_ar, o_ai, o_ld, o_br, o_bi):
        _, vjp = jax.vjp(_s5_discretize, ar[...], ai[...], ld[...], br[...], bi[...])
        r = vjp((gar[...], gai[...], gbr[...], gbi[...]))
        o_ar[...], o_ai[...], o_ld[...], o_br[...], o_bi[...] = r

    gp = jax.ShapeDtypeStruct((S5_G, S5_P), F32)
    hgp = jax.ShapeDtypeStruct((S5_H, S5_G, S5_P), F32)
    return pl.pallas_call(body, name="s5_params_bwd", out_shape=[gp, gp, jax.ShapeDtypeStruct((S5_G, 1), F32), hgp, hgp])(
        a_re, a_im, log_dt, bt_re, bt_im, d_ar, d_ai, d_br, d_bi)


def _s5_in_fwd(x, pv, w_in, b_re, b_im, layer, tm, after=None):
    L = x.shape[0]

    def body(x_ref, pv_ref, win_ref, bre_ref, bim_ref, h_ref, u_ref, ore_ref, oim_ref):
        p = pv_ref[...]
        h, _, _ = _norm_mod(x_ref[...], p[R_N1:R_N1 + 1], p[R_SC1:R_SC1 + 1], p[R_SH1:R_SH1 + 1])
        hb = _bf(h)
        h_ref[...] = hb
        u = _dot(hb, win_ref[...])
        u_ref[...] = u
        ub = _bf(u)
        for k in range(S5_NB):
            uk = ub[:, k * S5_BH:(k + 1) * S5_BH]
            ore_ref[:, k * S5_BP:(k + 1) * S5_BP] = _dot(uk, bre_ref[k])
            oim_ref[:, k * S5_BP:(k + 1) * S5_BP] = _dot(uk, bim_ref[k])

    call, tail = _call_after(
        after, body, 5, name="s5_in_fwd", grid=(L // tm,),
        in_specs=[_rows(tm, D), pl.BlockSpec((None, 8, D), lambda i: (layer, 0, 0)), _layer_w(D, D, 0),
                  _const_w((S5_NB, S5_BH, S5_BP)), _const_w((S5_NB, S5_BH, S5_BP))],
        out_specs=[_rows(tm, D), _rows(tm, D), _rows(tm, NSTATE), _rows(tm, NSTATE)],
        out_shape=[jax.ShapeDtypeStruct((L, D), BF16), jax.ShapeDtypeStruct((L, D), F32),
                   jax.ShapeDtypeStruct((L, NSTATE), F32), jax.ShapeDtypeStruct((L, NSTATE), F32)],
        compiler_params=_params(1, 56),
    )
    return call(x, pv, w_in, b_re, b_im, *tail)


def _s5_scan_fwd(bu_re, bu_im, ar, ai, tr):
    L = bu_re.shape[0]
    nl = 1024

    def body(bre_ref, bim_ref, ar_ref, ai_ref, sre_ref, sim_ref, st_re, st_im):
        @pl.when(pl.program_id(1) == 0)
        def _():
            st_re[...] = jnp.zeros_like(st_re)
            st_im[...] = jnp.zeros_like(st_im)

        a_r, a_i = ar_ref[...], ai_ref[...]

        def step(t, carry):
            s_r, s_i = carry
            n_r = a_r * s_r - a_i * s_i + bre_ref[pl.ds(t, 1), :]
            n_i = a_r * s_i + a_i * s_r + bim_ref[pl.ds(t, 1), :]
            sre_ref[pl.ds(t, 1), :] = n_r
            sim_ref[pl.ds(t, 1), :] = n_i
            return n_r, n_i

        s_r, s_i = lax.fori_loop(0, tr, step, (st_re[...], st_im[...]), unroll=8)
        st_re[...] = s_r
        st_im[...] = s_i

    blk = pl.BlockSpec((tr, nl), lambda j, i: (i, j))
    vec = pl.BlockSpec((1, nl), lambda j, i: (0, j))
    return pl.pallas_call(
        body, name="s5_scan_fwd", grid=(NSTATE // nl, L // tr),
        in_specs=[blk, blk, vec, vec], out_specs=[blk, blk],
        out_shape=[jax.ShapeDtypeStruct((L, NSTATE), F32)] * 2,
        scratch_shapes=[pltpu.VMEM((1, nl), F32), pltpu.VMEM((1, nl), F32)],
        compiler_params=_params(2),
    )(bu_re, bu_im, ar, ai)


def _s5_out_fwd(x, u, s_re, s_im, pv, c_re, c_im, dvec, glu_w, glu_b, w_out, layer, tm):
    L = x.shape[0]

    def body(x_ref, u_ref, sre_ref, sim_ref, pv_ref, cre_ref, cim_ref, d_ref, gw_ref, gb_ref, wout_ref,
             x1_ref, y1_ref, zg_ref, y3_ref, y_ref):
        p = pv_ref[...]
        srb, sib = _bf(sre_ref[...]), _bf(sim_ref[...])
        parts = []
        for k in range(S5_NB):
            sl = slice(k * S5_BP, (k + 1) * S5_BP)
            parts.append(_dot(srb[:, sl], cre_ref[k]) - _dot(sib[:, sl], cim_ref[k]))
        y1 = jnp.concatenate(parts, axis=1) + d_ref[...] * u_ref[...]
        y1_ref[...] = y1
        y2 = jax.nn.gelu(y1)
        zg = _dot(_bf(y2), gw_ref[...]) + gb_ref[...]
        zg_ref[...] = zg
        y3b = _bf(y2 * jax.nn.sigmoid(zg))
        y3_ref[...] = y3b
        y = _dot(y3b, wout_ref[...])
        y_ref[...] = y
        x1_ref[...] = x_ref[...] + p[R_G1:R_G1 + 1] * y

    return pl.pallas_call(
        body, name="s5_out_fwd", grid=(L // tm,),
        in_specs=[_rows(tm, D), _rows(tm, D), _rows(tm, NSTATE), _rows(tm, NSTATE),
                  pl.BlockSpec((None, 8, D), lambda i: (layer, 0, 0)),
                  _const_w((S5_NB, S5_BP, S5_BH)), _const_w((S5_NB, S5_BP, S5_BH)), _whole((1, D)),
                  _layer_w(D, D, 0), _whole((1, D)), _layer_w(D, D, 0)],
        out_specs=[_rows(tm, D)] * 5,
        out_shape=[jax.ShapeDtypeStruct((L, D), F32), jax.ShapeDtypeStruct((L, D), F32), jax.ShapeDtypeStruct((L, D), F32),
                   jax.ShapeDtypeStruct((L, D), BF16), jax.ShapeDtypeStruct((L, D), F32)],
        compiler_params=_params(1, 56),
    )(x, u, s_re, s_im, pv, c_re, c_im, dvec, glu_w, glu_b, w_out)


def _s5_out_bwd(dx1, y, y1, zg, u, pv, c_re, c_im, dvec, glu_w, w_out, layer, tm, after=None):
    L = dx1.shape[0]

    def body(dx1_ref, y_ref, y1_ref, zg_ref, u_ref, pv_ref, cre_ref, cim_ref, d_ref, gw_ref, wout_ref,
             dy_ref, y2_ref, dzg_ref, dy1_ref, dus_ref, gre_ref, gim_ref, vs_ref):
        @pl.when(pl.program_id(0) == 0)
        def _():
            vs_ref[...] = jnp.zeros_like(vs_ref)

        dx1v, p = dx1_ref[...], pv_ref[...]
        dyb = _bf(dx1v * p[R_G1:R_G1 + 1])
        dy_ref[...] = dyb
        vs_ref[0:1, :] += _sum0(dx1v * y_ref[...])
        dy3 = _dot_nt(dyb, wout_ref[...])
        y2, gelu_vjp = jax.vjp(jax.nn.gelu, y1_ref[...])
        y2_ref[...] = _bf(y2)
        gate = jax.nn.sigmoid(zg_ref[...])
        dzg = dy3 * y2 * gate * (1.0 - gate)
        dzgb = _bf(dzg)
        dzg_ref[...] = dzgb
        vs_ref[1:2, :] += _sum0(dzg)
        dy2 = dy3 * gate + _dot_nt(dzgb, gw_ref[...])
        dy1 = gelu_vjp(dy2)[0]
        vs_ref[2:3, :] += _sum0(dy1 * u_ref[...])
        dus_ref[...] = dy1 * d_ref[...]
        dy1b = _bf(dy1)
        dy1_ref[...] = dy1b
        for k in range(S5_NB):
            dk = dy1b[:, k * S5_BH:(k + 1) * S5_BH]
            gre_ref[:, k * S5_BP:(k + 1) * S5_BP] = _dot_nt(dk, cre_ref[k])
            gim_ref[:, k * S5_BP:(k + 1) * S5_BP] = -_dot_nt(dk, cim_ref[k])

    call, tail = _call_after(
        after, body, 11, name="s5_out_bwd", grid=(L // tm,),
        in_specs=[_rows(tm, D)] * 5 + [pl.BlockSpec((None, 8, D), lambda i: (layer, 0, 0)),
                  _const_w((S5_NB, S5_BP, S5_BH)), _const_w((S5_NB, S5_BP, S5_BH)), _whole((1, D)),
                  _layer_w(D, D, 0), _layer_w(D, D, 0)],
        out_specs=[_rows(tm, D)] * 5 + [_rows(tm, NSTATE), _rows(tm, NSTATE), _whole((8, D))],
        out_shape=[jax.ShapeDtypeStruct((L, D), BF16)] * 4 + [jax.ShapeDtypeStruct((L, D), F32),
                   jax.ShapeDtypeStruct((L, NSTATE), F32), jax.ShapeDtypeStruct((L, NSTATE), F32),
                   jax.ShapeDtypeStruct((8, D), F32)],
        compiler_params=_params(1, 56),
    )
    return call(dx1, y, y1, zg, u, pv, c_re, c_im, dvec, glu_w, w_out, *tail)


def _s5_scan_bwd(g_re, g_im, s_re, s_im, ar, ai, tr):
    L = g_re.shape[0]
    nl = 1024
    nt = L // tr

    def body(gre_ref, gim_ref, sre_ref, sim_ref, hre_ref, him_ref, ar_ref, ai_ref, lre_ref, lim_ref, da_ref, st_re, st_im):
        gi = pl.program_id(1)
        tile = nt - 1 - gi

        @pl.when(gi == 0)
        def _():
            st_re[...] = jnp.zeros_like(st_re)
            st_im[...] = jnp.zeros_like(st_im)
            da_ref[...] = jnp.zeros_like(da_ref)

        a_r, a_i = ar_ref[...], ai_ref[...]

        def step(k, carry):
            l_r, l_i = carry
            t = tr - 1 - k
            n_r = gre_ref[pl.ds(t, 1), :] + a_r * l_r + a_i * l_i
            n_i = gim_ref[pl.ds(t, 1), :] - a_i * l_r + a_r * l_i
            lre_ref[pl.ds(t, 1), :] = n_r
            lim_ref[pl.ds(t, 1), :] = n_i
            return n_r, n_i

        l_r, l_i = lax.fori_loop(0, tr, step, (st_re[...], st_im[...]), unroll=8)
        st_re[...] = l_r
        st_im[...] = l_i
        lam_r, lam_i = lre_ref[...], lim_ref[...]
        p_r = jnp.where(tile > 0, hre_ref[...], 0.0)
        p_i = jnp.where(tile > 0, him_ref[...], 0.0)
        sp_r = _shift_down(sre_ref[...], p_r, 1)
        sp_i = _shift_down(sim_ref[...], p_i, 1)
        da_ref[0:1, :] += _sum0(lam_r * sp_r + lam_i * sp_i)
        da_ref[1:2, :] += _sum0(lam_i * sp_r - lam_r * sp_i)

    blk = pl.BlockSpec((tr, nl), lambda j, i: (nt - 1 - i, j))
    halo = pl.BlockSpec((8, nl), lambda j, i: (jnp.maximum((nt - 1 - i) * (tr // 8) - 1, 0), j))
    vec = pl.BlockSpec((1, nl), lambda j, i: (0, j))
    return pl.pallas_call(
        body, name="s5_scan_bwd", grid=(NSTATE // nl, nt),
        in_specs=[blk, blk, blk, blk, halo, halo, vec, vec],
        out_specs=[blk, blk, pl.BlockSpec((8, nl), lambda j, i: (0, j))],
        out_shape=[jax.ShapeDtypeStruct((L, NSTATE), F32)] * 2 + [jax.ShapeDtypeStruct((8, NSTATE), F32)],
        scratch_shapes=[pltpu.VMEM((1, nl), F32), pltpu.VMEM((1, nl), F32)],
        compiler_params=_params(2),
    )(g_re, g_im, s_re, s_im, s_re, s_im, ar, ai)


def _s5_in_bwd(dx1, lam_re, lam_im, du_skip, x, pv, b_re, b_im, w_in, layer, tm):
    L = x.shape[0]

    def body(dx1_ref, lre_ref, lim_ref, dus_ref, x_ref, pv_ref, bre_ref, bim_ref, win_ref, dx_ref, du_ref, vs_ref):
        @pl.when(pl.program_id(0) == 0)
        def _():
            vs_ref[...] = jnp.zeros_like(vs_ref)

        p = pv_ref[...]
        lrb, lib = _bf(lre_ref[...]), _bf(lim_ref[...])
        parts = []
        for k in range(S5_NB):
            sl = slice(k * S5_BP, (k + 1) * S5_BP)
            parts.append(_dot_nt(lrb[:, sl], bre_ref[k]) + _dot_nt(lib[:, sl], bim_ref[k]))
        dub = _bf(jnp.concatenate(parts, axis=1) + dus_ref[...])
        du_ref[...] = dub
        dh = _dot_nt(dub, win_ref[...])
        _, xn, r = _norm_mod(x_ref[...], p[R_N1:R_N1 + 1], p[R_SC1:R_SC1 + 1], p[R_SH1:R_SH1 + 1])
        dx_ref[...] = dx1_ref[...] + _norm_mod_bwd(dh, xn, r, p[R_N1:R_N1 + 1], p[R_SC1:R_SC1 + 1])
        vs_ref[1:2, :] += _sum0(dh * xn)
        vs_ref[2:3, :] += _sum0(dh)

    return pl.pallas_call(
        body, name="s5_in_bwd", grid=(L // tm,),
        in_specs=[_rows(tm, D), _rows(tm, NSTATE), _rows(tm, NSTATE), _rows(tm, D), _rows(tm, D),
                  pl.BlockSpec((None, 8, D), lambda i: (layer, 0, 0)),
                  _const_w((S5_NB, S5_BH, S5_BP)), _const_w((S5_NB, S5_BH, S5_BP)), _layer_w(D, D, 0)],
        out_specs=[_rows(tm, D), _rows(tm, D), _whole((8, D))],
        out_shape=[jax.ShapeDtypeStruct((L, D), F32), jax.ShapeDtypeStruct((L, D), BF16), jax.ShapeDtypeStruct((8, D), F32)],
        compiler_params=_params(1, 56),
    )(dx1, lam_re, lam_im, du_skip, x, pv, b_re, b_im, w_in)


def _blockdiag_b(bt):
    b = bt.reshape(S5_H, S5_NB, 16, S5_P).transpose(1, 2, 0, 3)
    eye = jnp.eye(16, dtype=bt.dtype)
    return (b[:, :, :, None, :] * eye[None, :, None, :, None]).reshape(S5_NB, S5_BH, S5_BP)


def _unblock_b(d):
    d = jnp.einsum("bghgp->bghp", d.reshape(S5_NB, 16, S5_H, 16, S5_P))
    return d.transpose(2, 0, 1, 3).reshape(S5_H, S5_G, S5_P)


def _blockdiag_c(cm):
    c4 = cm.reshape(S5_NB, 16, S5_H, S5_P)
    eye = jnp.eye(16, dtype=cm.dtype)
    out = c4.transpose(0, 1, 3, 2)[:, :, :, None, :] * eye[None, :, None, :, None]
    return out.reshape(S5_NB, S5_BP, S5_BH)


def _unblock_c(d):
    d = jnp.einsum("bgpgh->bghp", d.reshape(S5_NB, 16, S5_P, 16, S5_H))
    return d.reshape(S5_G, S5_H, S5_P)


def _tril_mask():
    return lax.broadcasted_iota(jnp.int32, (SG_CHUNK, SG_CHUNK), 0) >= lax.broadcasted_iota(jnp.int32, (SG_CHUNK, SG_CHUNK), 1)


def _sg_fwd(x, pv, w_in, w_s, b_t, vg, w_out, layer, tm, after=None):
    L = x.shape[0]
    nc = tm // SG_CHUNK

    def body(x_ref, pv_ref, win_ref, ws_ref, bt_ref, vg_ref, wout_ref, x1_ref, h_ref, uv_ref, vm_ref, q_ref, y_ref):
        xv, p = x_ref[...], pv_ref[...]
        h, _, _ = _norm_mod(xv, p[R_N1:R_N1 + 1], p[R_SC1:R_SC1 + 1], p[R_SH1:R_SH1 + 1])
        hb = _bf(h)
        h_ref[...] = hb
        uv = _dot(hb, win_ref[...])
        uv_ref[...] = uv
        v = uv[:, D:]
        rv = lax.rsqrt(jnp.mean(v * v, axis=-1, keepdims=True) + EPS)
        vnb = _bf((v * rv) * vg_ref[...])
        mask = _tril_mask()
        bt = bt_ref[...]
        for hd in range(SG_HEADS):
            wm = _bf(jnp.where(mask, ws_ref[hd], 0.0))
            cs = slice(hd * SG_CHUNK, (hd + 1) * SG_CHUNK)
            for ck in range(nc):
                rs = slice(ck * SG_CHUNK, (ck + 1) * SG_CHUNK)
                vm_ref[rs, cs] = _dot(wm, vnb[rs, cs]) + bt[:, hd:hd + 1]
        qb = _bf(uv[:, :D] * vm_ref[...])
        q_ref[...] = qb
        y = _dot(qb, wout_ref[...])
        y_ref[...] = y
        x1_ref[...] = xv + p[R_G1:R_G1 + 1] * y

    call, tail = _call_after(
        after, body, 7, name="sg_fwd", grid=(L // tm,),
        in_specs=[_rows(tm, D), pl.BlockSpec((None, 8, D), lambda i: (layer, 0, 0)), _layer_w(D, 2 * D, 0),
                  _whole((SG_HEADS, SG_CHUNK, SG_CHUNK)), _whole((SG_CHUNK, SG_HEADS)), _whole((1, D)), _layer_w(D, D, 0)],
        out_specs=[_rows(tm, D), _rows(tm, D), _rows(tm, 2 * D), _rows(tm, D), _rows(tm, D), _rows(tm, D)],
        out_shape=[jax.ShapeDtypeStruct((L, D), F32), jax.ShapeDtypeStruct((L, D), BF16), jax.ShapeDtypeStruct((L, 2 * D), F32),
                   jax.ShapeDtypeStruct((L, D), F32), jax.ShapeDtypeStruct((L, D), BF16), jax.ShapeDtypeStruct((L, D), F32)],
        compiler_params=_params(1, 56),
    )
    return call(x, pv, w_in, w_s, b_t, vg, w_out, *tail)


def _sg_bwd(dx1, x, y, uv, vm, pv, w_in, w_s, vg, w_out, layer, tm, after=None):
    L = x.shape[0]
    nc = tm // SG_CHUNK

    def body(dx1_ref, x_ref, y_ref, uv_ref, vm_ref, pv_ref, win_ref, ws_ref, vg_ref, wout_ref,
             dx_ref, duv_ref, dy_ref, vs_ref, dws_ref, dbt_ref, dvn_scr):
        @pl.when(pl.program_id(0) == 0)
        def _():
            vs_ref[...] = jnp.zeros_like(vs_ref)
            dws_ref[...] = jnp.zeros_like(dws_ref)
            dbt_ref[...] = jnp.zeros_like(dbt_ref)

        dx1v, p = dx1_ref[...], pv_ref[...]
        dyb = _bf(dx1v * p[R_G1:R_G1 + 1])
        dy_ref[...] = dyb
        vs_ref[0:1, :] += _sum0(dx1v * y_ref[...])
        dq = _dot_nt(dyb, wout_ref[...])
        uv = uv_ref[...]
        u, v = uv[:, :D], uv[:, D:]
        dub = _bf(dq * vm_ref[...])
        dvm = dq * u
        dvmb = _bf(dvm)
        rv = lax.rsqrt(jnp.mean(v * v, axis=-1, keepdims=True) + EPS)
        vh = v * rv
        vgv = vg_ref[...]
        vnb = _bf(vh * vgv)
        mask = _tril_mask()
        for hd in range(SG_HEADS):
            wm = _bf(jnp.where(mask, ws_ref[hd], 0.0))
            cs = slice(hd * SG_CHUNK, (hd + 1) * SG_CHUNK)
            dws = jnp.zeros((SG_CHUNK, SG_CHUNK), F32)
            dbs = jnp.zeros((SG_CHUNK, 1), F32)
            for ck in range(nc):
                rs = slice(ck * SG_CHUNK, (ck + 1) * SG_CHUNK)
                dvn_scr[rs, cs] = _dot_tn(wm, dvmb[rs, cs])
                dws = dws + _dot_nt(dvmb[rs, cs], vnb[rs, cs])
                dbs = dbs + jnp.sum(dvm[rs, cs], axis=1, keepdims=True)
            dws_ref[hd] += jnp.where(mask, dws, 0.0)
            dbt_ref[:, hd:hd + 1] += dbs
        dvn = dvn_scr[...]
        vs_ref[3:4, :] += _sum0(dvn * vh)
        dvnn = dvn * vgv
        dvb = _bf(rv * (dvnn - vh * jnp.mean(dvnn * vh, axis=-1, keepdims=True)))
        duv_ref[:, 0:D] = dub
        duv_ref[:, D:2 * D] = dvb
        dh = _dot_nt(dub, win_ref[:, 0:D]) + _dot_nt(dvb, win_ref[:, D:2 * D])
        _, xn, r = _norm_mod(x_ref[...], p[R_N1:R_N1 + 1], p[R_SC1:R_SC1 + 1], p[R_SH1:R_SH1 + 1])
        dx_ref[...] = dx1v + _norm_mod_bwd(dh, xn, r, p[R_N1:R_N1 + 1], p[R_SC1:R_SC1 + 1])
        vs_ref[1:2, :] += _sum0(dh * xn)
        vs_ref[2:3, :] += _sum0(dh)

    call, tail = _call_after(
        after, body, 10, name="sg_bwd", grid=(L // tm,),
        in_specs=[_rows(tm, D), _rows(tm, D), _rows(tm, D), _rows(tm, 2 * D), _rows(tm, D),
                  pl.BlockSpec((None, 8, D), lambda i: (layer, 0, 0)), _layer_w(D, 2 * D, 0),
                  _whole((SG_HEADS, SG_CHUNK, SG_CHUNK)), _whole((1, D)), _layer_w(D, D, 0)],
        out_specs=[_rows(tm, D), _rows(tm, 2 * D), _rows(tm, D), _whole((8, D)),
                   _whole((SG_HEADS, SG_CHUNK, SG_CHUNK)), _whole((SG_CHUNK, SG_HEADS))],
        out_shape=[jax.ShapeDtypeStruct((L, D), F32), jax.ShapeDtypeStruct((L, 2 * D), BF16), jax.ShapeDtypeStruct((L, D), BF16),
                   jax.ShapeDtypeStruct((8, D), F32), jax.ShapeDtypeStruct((SG_HEADS, SG_CHUNK, SG_CHUNK), F32),
                   jax.ShapeDtypeStruct((SG_CHUNK, SG_HEADS), F32)],
        scratch_shapes=[pltpu.VMEM((tm, D), F32)],
        compiler_params=_params(1, 56),
    )
    return call(dx1, x, y, uv, vm, pv, w_in, w_s, vg, w_out, *tail)


def _final(x, target, fg, tm):
    L = x.shape[0]

    def body(x_ref, t_ref, g_ref, dx_ref, vs_ref):
        @pl.when(pl.program_id(0) == 0)
        def _():
            vs_ref[...] = jnp.zeros_like(vs_ref)

        xv, g = x_ref[...], g_ref[...]
        r = lax.rsqrt(jnp.mean(xv * xv, axis=-1, keepdims=True) + EPS)
        xn = xv * r
        e = xn * g - t_ref[...]
        vs_ref[0:1, :] += jnp.sum(e * e)
        dout = e * (1.0 / D)
        vs_ref[1:2, :] += _sum0(dout * xn)
        dxn = dout * g
        dx_ref[...] = r * (dxn - xn * jnp.mean(dxn * xn, axis=-1, keepdims=True))

    return pl.pallas_call(
        body, name="final_loss", grid=(L // tm,),
        in_specs=[_rows(tm, D), _rows(tm, D), _whole((1, D))],
        out_specs=[_rows(tm, D), _whole((8, D))],
        out_shape=[jax.ShapeDtypeStruct((L, D), F32), jax.ShapeDtypeStruct((8, D), F32)],
        compiler_params=_params(1),
    )(x, target, fg)


def _pack(arrs, row_multiple=8):
    flat = jnp.concatenate([a.reshape(-1).astype(F32) for a in arrs])
    rows = -(-flat.shape[0] // LANES)
    rows = -(-rows // row_multiple) * row_multiple
    return jnp.pad(flat, (0, rows * LANES - flat.shape[0])).reshape(rows, LANES)


def _unpack(buf, shapes, lead=()):
    flat = buf.reshape(lead + (-1,))
    out, off = [], 0
    for s in shapes:
        n = 1
        for d in s:
            n *= d
        out.append(flat[..., off:off + n].reshape(lead + tuple(s)))
        off += n
    return out


BIG = ("ff_w1", "ff_w2", "conv_w_in", "conv_w_out", "ssm_w_in", "ssm_glu_w", "ssm_w_out", "sg_w_in", "sg_w_out")
BIG_AXIS = {"ff_w1": 2, "ff_w2": 1, "conv_w_in": 2, "conv_w_out": 1, "ssm_w_in": 1, "ssm_glu_w": 1, "ssm_w_out": 1,
            "sg_w_in": 2, "sg_w_out": 1}
LAYER_WEIGHTS = (
    (("conv_w_in", 0), ("conv_w_out", 0), ("ff_w1", 0), ("ff_w2", 0)),
    (("ssm_w_in", 0), ("ssm_glu_w", 0), ("ssm_w_out", 0), ("ff_w1", 1), ("ff_w2", 1)),
    (("sg_w_in", 0), ("sg_w_out", 0), ("ff_w1", 2), ("ff_w2", 2)),
    (("conv_w_in", 1), ("conv_w_out", 1), ("ff_w1", 3), ("ff_w2", 3)),
)
SMALL = ("ada_b", "norm1_g", "norm2_g", "final_g", "conv_w", "conv_b", "ssm_a_re", "ssm_a_im", "ssm_log_dt", "ssm_b_re",
         "ssm_b_im", "ssm_c_re", "ssm_c_im", "ssm_d", "ssm_glu_b", "sg_v_g", "sg_w_s", "sg_b_s")
WEIGHTS = ("ada_w", "ada_b", "norm1_g", "norm2_g", "ff_w1", "ff_w2", "final_g", "conv_w_in", "conv_w", "conv_b", "conv_w_out",
           "ssm_w_in", "ssm_a_re", "ssm_a_im", "ssm_log_dt", "ssm_b_re", "ssm_b_im", "ssm_c_re", "ssm_c_im", "ssm_d",
           "ssm_glu_w", "ssm_glu_b", "ssm_w_out", "sg_w_in", "sg_v_g", "sg_w_s", "sg_b_s", "sg_w_out")


def kernel(x, c, ada_w, ada_b, norm1_g, norm2_g, ff_w1, ff_w2, final_g, conv_w_in, conv_w, conv_b, conv_w_out, ssm_w_in, ssm_a_re, ssm_a_im, ssm_log_dt, ssm_b_re, ssm_b_im, ssm_c_re, ssm_c_im, ssm_d, ssm_glu_w, ssm_glu_b, ssm_w_out, sg_w_in, sg_v_g, sg_w_s, sg_b_s, sg_w_out, loss_target, m_ada_w, m_ada_b, m_norm1_g, m_norm2_g, m_ff_w1, m_ff_w2, m_final_g, m_conv_w_in, m_conv_w, m_conv_b, m_conv_w_out, m_ssm_w_in, m_ssm_a_re, m_ssm_a_im, m_ssm_log_dt, m_ssm_b_re, m_ssm_b_im, m_ssm_c_re, m_ssm_c_im, m_ssm_d, m_ssm_glu_w, m_ssm_glu_b, m_ssm_w_out, m_sg_w_in, m_sg_v_g, m_sg_w_s, m_sg_b_s, m_sg_w_out, v_ada_w, v_ada_b, v_norm1_g, v_norm2_g, v_ff_w1, v_ff_w2, v_final_g, v_conv_w_in, v_conv_w, v_conv_b, v_conv_w_out, v_ssm_w_in, v_ssm_a_re, v_ssm_a_im, v_ssm_log_dt, v_ssm_b_re, v_ssm_b_im, v_ssm_c_re, v_ssm_c_im, v_ssm_d, v_ssm_glu_w, v_ssm_glu_b, v_ssm_w_out, v_sg_w_in, v_sg_v_g, v_sg_w_s, v_sg_b_s, v_sg_w_out):
    args = dict(locals())
    w = {n: args[n] for n in WEIGHTS}
    m = {n: args["m_" + n] for n in WEIGHTS}
    v = {n: args["v_" + n] for n in WEIGHTS}
    L = x.shape[1]
    tm = min(L, 256)
    tr = min(L, 512)
    chip = 2 * lax.axis_index("x") + lax.axis_index("y")
    me = 2 * chip + lax.axis_index("c")
    xin = x[0]
    target = loss_target[0]
    chip1 = chip.reshape(1).astype(jnp.int32)

    gathers = []
    def start_gather(i, after):
        axes = [BIG_AXIS[n] for n, _ in LAYER_WEIGHTS[i]]
        lands = [_cast_place(w[n], li, BIG_AXIS[n], chip1, f"cast_{n}_{li}") for n, li in LAYER_WEIGHTS[i]]
        s_sems, r_sems, lands, token = _gather_start(lands, axes, f"gather_start{i}", after)
        gathers.append((s_sems, r_sems, lands, axes))
        return token

    def weights_of(i, after):
        s_sems, r_sems, lands, axes = gathers[i]
        lands = _gather_wait(s_sems, r_sems, lands, axes, f"gather_wait{i}", after)
        return dict(zip([n for n, _ in LAYER_WEIGHTS[i]], _gather_share(lands, axes, f"gather_share{i}")))

    small_in = _pack([c, conv_w, conv_b, sg_v_g])
    got = _allgather_small(small_in, "gather_small_inputs").reshape(N_DEV, -1)
    c_all, cw_sh, cb_sh, vg_sh = _unpack(got, [(D,), conv_w.shape, conv_b.shape, sg_v_g.shape], lead=(N_DEV,))
    conv_w_full = jnp.concatenate([cw_sh[2 * k] for k in range(4)], axis=-1)
    conv_b_full = jnp.concatenate([cb_sh[2 * k] for k in range(4)], axis=-1)
    vg_full = jnp.concatenate([vg_sh[2 * k] for k in range(4)], axis=-1)
    c16 = jnp.pad(c_all, ((0, 16 - N_DEV), (0, 0)))

    cols = ada_w.shape[2]
    ada_b_cols = lax.dynamic_slice_in_dim(ada_b, chip * cols, cols, axis=1)[:, None, :]
    mod_sh = _ada_fwd(c16, ada_w, ada_b_cols)[:, :N_DEV, :]
    mod_all = _allgather_small(_pack([mod_sh]), "gather_mod").reshape(N_DEV, -1)
    mod_all = _unpack(mod_all, [mod_sh.shape], lead=(N_DEV,))[0]
    mod_mine = lax.dynamic_index_in_dim(mod_all[0::2], me, axis=2, keepdims=False)
    mod_mine = mod_mine.transpose(1, 0, 2).reshape(DEPTH, 6, D)
    pv = jnp.concatenate([mod_mine, norm1_g[:, None, :], norm2_g[:, None, :]], axis=1)

    start_gather(0, pv)

    cw_rows = jnp.concatenate([conv_w_full, conv_b_full[:, None, :], jnp.zeros((conv_w_full.shape[0], 4, D), F32)], axis=1)

    a_re, a_im = ssm_a_re[0], ssm_a_im[0]
    log_dt = ssm_log_dt[0][:, None]
    bt_re, bt_im = ssm_b_re[0].transpose(2, 0, 1), ssm_b_im[0].transpose(2, 0, 1)
    abar_re, abar_im, bbar_re, bbar_im = _s5_params_fwd(a_re, a_im, log_dt, bt_re, bt_im)
    ar_vec, ai_vec = abar_re.reshape(1, NSTATE), abar_im.reshape(1, NSTATE)
    bd_re, bd_im = _bf(_blockdiag_b(bbar_re)), _bf(_blockdiag_b(bbar_im))
    cd_re, cd_im = _bf(_blockdiag_c(ssm_c_re[0])), _bf(_blockdiag_c(ssm_c_im[0]))

    saved = []
    fulls = []
    xl = xin
    for i in range(DEPTH):
        kind = MIXER_OF_LAYER[i]
        j = i // 3
        full = weights_of(i, cd_im if i == 0 else xl)
        fulls.append(full)
        tok = start_gather(i + 1, full["ff_w2"]) if i + 1 < DEPTH else None
        if kind == 0:
            x1, h, bcx, conv, q, y = _conv_fwd(xl, pv, full["conv_w_in"], full["conv_w_out"], cw_rows, i, j, tm, after=tok)
            mix = dict(h=h, bcx=bcx, conv=conv, q=q, y=y)
        elif kind == 1:
            h, u, bu_re, bu_im = _s5_in_fwd(xl, pv, full["ssm_w_in"], bd_re, bd_im, i, tm, after=tok)
            s_re, s_im = _s5_scan_fwd(bu_re, bu_im, ar_vec, ai_vec, tr)
            x1, y1, zg, y3, y = _s5_out_fwd(xl, u, s_re, s_im, pv, cd_re, cd_im, ssm_d, full["ssm_glu_w"], ssm_glu_b,
                                            full["ssm_w_out"], i, tm)
            mix = dict(h=h, u=u, s_re=s_re, s_im=s_im, y1=y1, zg=zg, y3=y3, y=y)
        else:
            x1, h, uv, vm, q, y = _sg_fwd(xl, pv, full["sg_w_in"], sg_w_s[0], sg_b_s[0].T, vg_full, full["sg_w_out"], i, tm,
                                          after=tok)
            mix = dict(h=h, uv=uv, vm=vm, q=q, y=y)
        x2, h2, a, f = _ffn_fwd(x1, pv, full["ff_w1"], full["ff_w2"], i, tm)
        saved.append(dict(x=xl, x1=x1, h2=h2, a=a, f=f, **mix))
        xl = x2

    dxl, vs_fin = _final(xl, target, final_g[None, :], tm)

    gfull = {n: [None] * w[n].shape[0] for n in BIG}
    vs_mix, vs_ffn = [None] * DEPTH, [None] * DEPTH
    small_g = {}
    scatters = {}
    token = None

    def start_scatter(key, entries, after):
        garrs = [gfull[n][li][None] for n, li in entries]
        gaxes = [BIG_AXIS[n] for n, _ in entries]
        s_sems, r_sems, garrs, lands, tok = _scatter_start(garrs, gaxes, f"scatter_start{key}", after)
        scatters[key] = (s_sems, r_sems, garrs, lands, gaxes, entries)
        return tok

    for i in reversed(range(DEPTH)):
        kind = MIXER_OF_LAYER[i]
        j = i // 3
        sv = saved[i]
        full = fulls[i]
        dx1, p_b, da_b, df_b, vs_ffn[i] = _ffn_bwd(dxl, sv["x1"], sv["a"], sv["f"], pv, full["ff_w1"], full["ff_w2"], i, tm,
                                                   after=token)
        gfull["ff_w1"][i] = _mm_tn(sv["h2"], da_b, f"wgrad_ff_w1_{i}")
        gfull["ff_w2"][i] = _mm_tn(p_b, df_b, f"wgrad_ff_w2_{i}")
        if i == 0:
            token = start_scatter("0f", LAYER_WEIGHTS[0][2:], dx1)
        if kind == 0:
            dxl, dbcx_b, dy_b, vsm = _conv_bwd(dx1, sv["x"], sv["y"], sv["bcx"], sv["conv"], pv, full["conv_w_in"],
                                               full["conv_w_out"], cw_rows, i, j, tm, after=token if i == 0 else None)
            gfull["conv_w_in"][j] = _mm_tn(sv["h"], dbcx_b, f"wgrad_conv_w_in_{j}")
            gfull["conv_w_out"][j] = _mm_tn(sv["q"], dy_b, f"wgrad_conv_w_out_{j}")
            small_g.setdefault("conv_w", [None, None])[j] = vsm[3:6]
            small_g.setdefault("conv_b", [None, None])[j] = vsm[6]
        elif kind == 1:
            dy_b, y2_b, dzg_b, dy1_b, du_skip, g_re, g_im, vsm = _s5_out_bwd(
                dx1, sv["y"], sv["y1"], sv["zg"], sv["u"], pv, cd_re, cd_im, ssm_d, full["ssm_glu_w"], full["ssm_w_out"], i, tm)
            lam_re, lam_im, dabar = _s5_scan_bwd(g_re, g_im, sv["s_re"], sv["s_im"], ar_vec, ai_vec, tr)
            dxl, du_b, vs_in = _s5_in_bwd(dx1, lam_re, lam_im, du_skip, sv["x"], pv, bd_re, bd_im, full["ssm_w_in"], i, tm)
            gfull["ssm_w_out"][0] = _mm_tn(sv["y3"], dy_b, "wgrad_ssm_w_out")
            gfull["ssm_glu_w"][0] = _mm_tn(y2_b, dzg_b, "wgrad_ssm_glu_w")
            gfull["ssm_w_in"][0] = _mm_tn(sv["h"], du_b, "wgrad_ssm_w_in")
            d_cre = _unblock_c(_mm_tn_blocks(sv["s_re"], dy1_b, S5_BP, S5_BH, "wgrad_s5_c_re"))
            d_cim = -_unblock_c(_mm_tn_blocks(sv["s_im"], dy1_b, S5_BP, S5_BH, "wgrad_s5_c_im"))
            d_bbre = _unblock_b(_mm_tn_blocks(sv["u"], lam_re, S5_BH, S5_BP, "wgrad_s5_b_re"))
            d_bbim = _unblock_b(_mm_tn_blocks(sv["u"], lam_im, S5_BH, S5_BP, "wgrad_s5_b_im"))
            d_are, d_aim, d_ldt, d_btre, d_btim = _s5_params_bwd(
                a_re, a_im, log_dt, bt_re, bt_im, dabar[0].reshape(S5_G, S5_P), dabar[1].reshape(S5_G, S5_P), d_bbre, d_bbim)
            small_g.update(ssm_a_re=d_are, ssm_a_im=d_aim, ssm_log_dt=d_ldt, ssm_b_re=d_btre.transpose(1, 2, 0),
                           ssm_b_im=d_btim.transpose(1, 2, 0), ssm_c_re=d_cre, ssm_c_im=d_cim, ssm_d=vsm[2], ssm_glu_b=vsm[1])
            vsm = jnp.concatenate([vsm[0:1], vs_in[1:3], jnp.zeros((5, D), F32)], axis=0)
        else:
            dxl, duv_b, dy_b, vsm, d_ws, d_bt = _sg_bwd(dx1, sv["x"], sv["y"], sv["uv"], sv["vm"], pv, full["sg_w_in"],
                                                        sg_w_s[0], vg_full, full["sg_w_out"], i, tm)
            gfull["sg_w_in"][0] = _mm_tn(sv["h"], duv_b, "wgrad_sg_w_in")
            gfull["sg_w_out"][0] = _mm_tn(sv["q"], dy_b, "wgrad_sg_w_out")
            small_g.update(sg_v_g=vsm[3], sg_w_s=d_ws, sg_b_s=d_bt.T)
        vs_mix[i] = vsm
        token = start_scatter(str(i), LAYER_WEIGHTS[i], dxl) if i > 0 else start_scatter("0c", LAYER_WEIGHTS[0][:2], dxl)
    grad_x = dxl[None]

    sums = {n: [None] * w[n].shape[0] for n in BIG}

    def collect(key, after):
        s_sems, r_sems, garrs, lands, gaxes, entries = scatters[key]
        garrs, recv = _scatter_wait(s_sems, r_sems, garrs, lands, gaxes, f"scatter_wait{key}", after)
        for (n, li), g, r3, ax in zip(entries, garrs, recv, gaxes):
            sums[n][li] = _sum_parts(r3, g, ax, chip1, f"sum_{n}_{li}")
        return sums[entries[-1][0]][entries[-1][1]]

    after = token
    for key in ("3", "2", "1"):
        after = collect(key, after)
    early = [(n, li) for i in (3, 2, 1) for n, li in LAYER_WEIGHTS[i]]
    late = list(LAYER_WEIGHTS[0][2:]) + list(LAYER_WEIGHTS[0][:2])
    sib = dict(zip(early, _swap_with_sibling([sums[n][li] for n, li in early], "swap_grad_sums_early")))
    after = sib[early[-1]]
    for key in ("0f", "0c"):
        after = collect(key, after)
    sib.update(zip(late, _swap_with_sibling([sums[n][li] for n, li in late], "swap_grad_sums_late")))

    dmod = _mod_bwd(jnp.stack(vs_mix), jnp.stack(vs_ffn), pv)
    small_g.update(ada_b=dmod[:, :6, :], norm1_g=dmod[:, 6, :], norm2_g=dmod[:, 7, :], final_g=vs_fin[1],
                   conv_w=jnp.stack(small_g["conv_w"]), conv_b=jnp.stack(small_g["conv_b"]))

    loss_part = (0.5 / D) * vs_fin[0, 0:1]
    part_shapes = [(1,)] + [tuple(small_g[n].shape) for n in SMALL]
    parts_sum = _allreduce_small(_pack([loss_part] + [small_g[n] for n in SMALL], 16), "reduce_small_grads", sib[late[-1]])
    summed = _unpack(parts_sum, part_shapes)
    loss = summed[0][0]
    gsum = dict(zip(SMALL, summed[1:]))
    dmod_all = _allgather_small(_pack([small_g["ada_b"]]), "gather_dmod", parts_sum)
    dmod_all = dmod_all.reshape(N_DEV, DEPTH, 6 * D)
    dmod_cols = lax.dynamic_slice_in_dim(dmod_all, chip * cols, cols, axis=2).transpose(1, 0, 2)
    g_ada_w = _ada_bwd(c16, jnp.pad(dmod_cols, ((0, 0), (0, 16 - N_DEV), (0, 0))))

    res = {}
    shp = ada_w.shape
    two = lambda t: t.reshape(shp[0] * shp[1], shp[2])
    res["ada_w"] = [t.reshape(shp) for t in _adamw(two(ada_w), [two(g_ada_w)], two(m_ada_w), two(v_ada_w), "adamw_ada_w")]

    def mine(n, g):
        if n in ("conv_w", "conv_b", "sg_v_g"):
            size = w[n].shape[-1]
            return lax.dynamic_slice_in_dim(g, chip * size, size, axis=g.ndim - 1)
        return g

    g_loc = [mine(n, gsum[n]).reshape(w[n].shape) for n in SMALL]
    small_shapes = [tuple(w[n].shape) for n in SMALL]
    packed = _adamw(_pack([w[n] for n in SMALL]), [_pack(g_loc)], _pack([m[n] for n in SMALL]), _pack([v[n] for n in SMALL]),
                    "adamw_small")
    unpacked = [_unpack(t, small_shapes) for t in packed]
    for k, n in enumerate(SMALL):
        res[n] = [unpacked[0][k], unpacked[1][k], unpacked[2][k], unpacked[3][k]]

    q_mine = [jnp.stack(sums[n]) for n in BIG]
    q_sib = [jnp.stack([sib[(n, li)] for li in range(w[n].shape[0])]) for n in BIG]
    for n, qa, qb in zip(BIG, q_mine, q_sib):
        shp = w[n].shape
        two = lambda t, shp=shp: t.reshape(shp[0] * shp[1], shp[2])
        res[n] = [t.reshape(shp) for t in _adamw(two(w[n]), [two(qa), two(qb)], two(m[n]), two(v[n]), f"adamw_{n}")]

    outs = [loss, grad_x]
    for part in range(4):
        outs += [res[n][part] for n in WEIGHTS]
    return tuple(outs)
```

```python
import functools

import jax
import jax.numpy as jnp
from jax import lax
from jax.experimental import pallas as pl
from jax.experimental.pallas import tpu as pltpu

F32 = jnp.float32
BF16 = jnp.bfloat16
D = 1024
EPS = 1e-6
DEPTH = 4
MIXER_OF_LAYER = (0, 1, 2, 0)
S5_G, S5_H, S5_P = 64, 16, 64
S5_NB = 4
S5_BH = S5_H * 16
S5_BP = S5_P * 16
NSTATE = S5_G * S5_P
SG_HEADS, SG_CHUNK = 8, 128
ADAM_LR, ADAM_B1, ADAM_B2, ADAM_EPS, ADAM_WD, ADAM_STEP = 0.001, 0.9, 0.999, 1e-08, 0.01, 10
N_DEV = 8
MESH = pl.DeviceIdType.MESH
LANES = 1024
R_SH1, R_SC1, R_G1, R_SH2, R_SC2, R_G2, R_N1, R_N2 = range(8)


def _dot(a, b):
    return jnp.dot(a, b, preferred_element_type=F32)


def _dot_nt(a, b):
    return lax.dot_general(a, b, (((1,), (1,)), ((), ())), preferred_element_type=F32)


def _dot_tn(a, b):
    return lax.dot_general(a, b, (((0,), (0,)), ((), ())), preferred_element_type=F32)


def _bf(x):
    return x.astype(BF16)


def _sum0(x):
    return jnp.sum(x, axis=0, keepdims=True)


def _params(n_axes, vmem_mb=48):
    return pltpu.CompilerParams(dimension_semantics=("arbitrary",) * n_axes, vmem_limit_bytes=vmem_mb << 20)


def _rows(tm, cols, nt=None):
    if nt is None:
        return pl.BlockSpec((tm, cols), lambda i: (i, 0))
    return pl.BlockSpec((tm, cols), lambda i: (nt - 1 - i, 0))


def _whole(shape):
    nd = len(shape)
    return pl.BlockSpec(shape, lambda *_: (0,) * nd)


def _layer_w(r, c, layer):
    return pl.BlockSpec((None, r, c), lambda *_: (layer, 0, 0), pipeline_mode=pl.Buffered(1))


def _const_w(shape):
    nd = len(shape)
    return pl.BlockSpec(shape, lambda *_: (0,) * nd, pipeline_mode=pl.Buffered(1))


def _call_after(after, body, n_in, *, in_specs, **kw):
    if after is None:
        return pl.pallas_call(body, in_specs=in_specs, **kw), ()

    def body_after(*refs):
        return body(*refs[:n_in], *refs[n_in + 1:])

    return pl.pallas_call(body_after, in_specs=list(in_specs) + [pl.BlockSpec(memory_space=pl.ANY)], **kw), (after,)


def _norm_mod(x, ng, sc, sh):
    r = lax.rsqrt(jnp.mean(x * x, axis=-1, keepdims=True) + EPS)
    xn = x * r
    return (xn * ng) * (1.0 + sc) + sh, xn, r


def _norm_mod_bwd(dh, xn, r, ng, sc):
    dxn = dh * (ng * (1.0 + sc))
    return r * (dxn - xn * jnp.mean(dxn * xn, axis=-1, keepdims=True))


def _shift_down(z, prev8, k):
    row = lax.broadcasted_iota(jnp.int32, z.shape, 0)
    if k == 1:
        return jnp.where(row >= 1, pltpu.roll(z, 1, 0), prev8[7:8])
    return jnp.where(row >= 2, pltpu.roll(z, 2, 0), jnp.where(row == 0, prev8[6:7], prev8[7:8]))


def _shift_up(z, next8, k):
    n = z.shape[0]
    row = lax.broadcasted_iota(jnp.int32, z.shape, 0)
    if k == 1:
        return jnp.where(row <= n - 2, pltpu.roll(z, n - 1, 0), next8[0:1])
    return jnp.where(row <= n - 3, pltpu.roll(z, n - 2, 0), jnp.where(row == n - 2, next8[0:1], next8[1:2]))


def _place():
    x, y, c = lax.axis_index("x"), lax.axis_index("y"), lax.axis_index("c")
    chips = [(1 - x, y), (x, 1 - y), (1 - x, 1 - y)]
    return x, y, c, chips


def _allgather_small(x_shard, name, after=None):
    m_per, n = x_shard.shape

    def body(x_ref, out_ref, send_sems, recv_sems, local_sem):
        x, y, c, chips = _place()
        me, sibling = (x, y, c), (x, y, 1 - c)

        def rows(px, py, pc):
            return out_ref.at[pl.ds((4 * px + 2 * py + pc) * m_per, m_per), :]

        def copy(k, block, to, src=None):
            return pltpu.make_async_remote_copy(
                src_ref=rows(*block) if src is None else src, dst_ref=rows(*block),
                send_sem=send_sems.at[k], recv_sem=recv_sems.at[k], device_id=to, device_id_type=MESH)

        mine = pltpu.make_async_copy(x_ref, rows(*me), local_sem)
        mine.start()
        first = [copy(0, me, sibling, src=x_ref)]
        first += [copy(1 + j, me, (*chip, c), src=x_ref) for j, chip in enumerate(chips)]
        for cp in first:
            cp.start()
        passed = [copy(4 + j, (*chip, c), sibling) for j, chip in enumerate(chips)]
        for j, chip in enumerate(chips):
            copy(1 + j, (*chip, c), me).wait_recv()
            passed[j].start()
        copy(0, sibling, me).wait_recv()
        for j, chip in enumerate(chips):
            copy(4 + j, (*chip, 1 - c), me).wait_recv()
        for cp in first + passed:
            cp.wait_send()
        mine.wait()

    call, tail = _call_after(
        after, body, 1, name=name, out_shape=jax.ShapeDtypeStruct((N_DEV * m_per, n), F32),
        in_specs=[pl.BlockSpec(memory_space=pltpu.VMEM)], out_specs=pl.BlockSpec(memory_space=pltpu.VMEM),
        scratch_shapes=[pltpu.SemaphoreType.DMA((7,)), pltpu.SemaphoreType.DMA((7,)), pltpu.SemaphoreType.DMA],
        compiler_params=pltpu.CompilerParams(vmem_limit_bytes=48 << 20),
    )
    return call(x_shard, *tail)


def _allreduce_small(x_part, name, after=None):
    m, n = x_part.shape
    h = m // 2

    def body(x_ref, out_ref, sib_buf, slots, send_sems, recv_sems):
        x, y, c, chips = _place()
        k_me = 2 * x + y
        sibling = (x, y, 1 - c)
        mine = pl.ds(pl.multiple_of(c * h, 8), h)

        def copy(k, src, dst, to):
            return pltpu.make_async_remote_copy(src_ref=src, dst_ref=dst, send_sem=send_sems.at[k], recv_sem=recv_sems.at[k],
                                                device_id=to, device_id_type=MESH)

        swap = copy(0, x_ref, sib_buf, sibling)
        swap.start()
        swap.wait()
        slots[pl.ds(k_me, 1)] = (x_ref[mine, :] + sib_buf[mine, :])[None]
        my_slot = slots.at[pl.ds(k_me, 1)]
        sends = [copy(1 + j, my_slot, my_slot, (*chip, c)) for j, chip in enumerate(chips)]
        for cp in sends:
            cp.start()
        for j, chip in enumerate(chips):
            their_slot = slots.at[pl.ds(2 * chip[0] + chip[1], 1)]
            copy(1 + j, their_slot, their_slot, (x, y, c)).wait_recv()
        for cp in sends:
            cp.wait_send()
        out_ref[mine, :] = ((slots[0] + slots[1]) + slots[2]) + slots[3]
        give = copy(4, out_ref.at[mine, :], out_ref.at[mine, :], sibling)
        give.start()
        give.wait_send()
        theirs = pl.ds(pl.multiple_of((1 - c) * h, 8), h)
        copy(4, out_ref.at[theirs, :], out_ref.at[theirs, :], (x, y, c)).wait_recv()

    call, tail = _call_after(
        after, body, 1, name=name, out_shape=jax.ShapeDtypeStruct((m, n), F32),
        in_specs=[pl.BlockSpec(memory_space=pltpu.VMEM)], out_specs=pl.BlockSpec(memory_space=pltpu.VMEM),
        scratch_shapes=[pltpu.VMEM((m, n), F32), pltpu.VMEM((4, h, n), F32),
                        pltpu.SemaphoreType.DMA((5,)), pltpu.SemaphoreType.DMA((5,))],
        compiler_params=pltpu.CompilerParams(vmem_limit_bytes=48 << 20),
    )
    return call(x_part, *tail)


def _shard_region(ref, full_shape, axis, chip_k, half=None):
    _, r, c = full_shape
    if axis == 1:
        rs = r // 4
        if half is None:
            return ref.at[:, pl.ds(pl.multiple_of(chip_k * rs, 128), rs), :]
        return ref.at[:, pl.ds(pl.multiple_of(chip_k * rs + half * (rs // 2), 128), rs // 2), :]
    cs = c // 4
    if half is None:
        return ref.at[:, :, pl.ds(pl.multiple_of(chip_k * cs, 128), cs)]
    return ref.at[:, pl.ds(pl.multiple_of(half * (r // 2), 128), r // 2), pl.ds(pl.multiple_of(chip_k * cs, 128), cs)]


HBM_SPEC = pl.BlockSpec(memory_space=pltpu.HBM)
SEM_SPEC = pl.BlockSpec(memory_space=pltpu.SEMAPHORE)
ANY_SPEC = pl.BlockSpec(memory_space=pl.ANY)
SPLIT_COPY_PARAMS = pltpu.CompilerParams(has_side_effects=pltpu.SideEffectType.DATAFLOW_SIDE_EFFECTING)


def _in_hbm(arrs):
    return [pltpu.with_memory_space_constraint(a, pltpu.HBM) for a in arrs]


def _cast_place(w_stack, li, axis, chip, name):
    _, r, c = w_stack.shape
    full = (1, 4 * r, c) if axis == 1 else (1, r, 4 * c)
    tr = min(r, 256)
    if axis == 1:
        out_spec = pl.BlockSpec((None, tr, c), lambda i, k: (0, k[0] * (r // tr) + i, 0))
    else:
        out_spec = pl.BlockSpec((None, tr, c), lambda i, k: (0, i, k[0]))

    def body(k_ref, w_ref, o_ref):
        o_ref[...] = _bf(w_ref[...])

    return pl.pallas_call(
        body, name=name,
        grid_spec=pltpu.PrefetchScalarGridSpec(
            num_scalar_prefetch=1, grid=(r // tr,),
            in_specs=[pl.BlockSpec((None, tr, c), lambda i, k: (li, i, 0))], out_specs=out_spec),
        out_shape=jax.ShapeDtypeStruct(full, BF16),
        compiler_params=_params(1),
    )(chip, w_stack)


def _gather_start(lands, axes, name, after):
    n_arr = len(lands)
    fulls = [tuple(l.shape) for l in lands]

    def body(*refs):
        land = refs[:n_arr]
        send_sems, recv_sems = refs[n_arr + 1:n_arr + 3]
        token = refs[-1]
        x, y, c, chips = _place()
        k_me = 2 * x + y
        for a in range(n_arr):
            mine = _shard_region(land[a], fulls[a], axes[a], k_me, c)
            for j, chip in enumerate(chips):
                pltpu.make_async_remote_copy(
                    src_ref=mine, dst_ref=mine, send_sem=send_sems.at[a * 3 + j], recv_sem=recv_sems.at[a * 3 + j],
                    device_id=(*chip, c), device_id_type=MESH).start()
        token[...] = jnp.zeros_like(token)

    res = pl.pallas_call(
        body, name=name,
        out_shape=(pltpu.SemaphoreType.DMA((3 * n_arr,)), pltpu.SemaphoreType.DMA((3 * n_arr,)),
                   *[pltpu.HBM(f, BF16) for f in fulls], jax.ShapeDtypeStruct((8, 128), F32)),
        in_specs=[HBM_SPEC] * n_arr + [ANY_SPEC],
        out_specs=(SEM_SPEC, SEM_SPEC, *[HBM_SPEC] * n_arr, pl.BlockSpec(memory_space=pltpu.VMEM)),
        input_output_aliases={a: 2 + a for a in range(n_arr)},
        compiler_params=SPLIT_COPY_PARAMS,
    )(*_in_hbm(lands), after)
    return res[0], res[1], list(res[2:2 + n_arr]), res[-1]


def _gather_wait(send_sems, recv_sems, lands, axes, name, after):
    n_arr = len(lands)
    fulls = [tuple(l.shape) for l in lands]

    def body(*refs):
        land = refs[:n_arr]
        s_sems, r_sems = refs[n_arr:n_arr + 2]
        x, y, c, chips = _place()
        for a in range(n_arr):
            for j, chip in enumerate(chips):
                k_j = 2 * chip[0] + chip[1]
                got = _shard_region(land[a], fulls[a], axes[a], k_j, c)
                cp = pltpu.make_async_remote_copy(
                    src_ref=got, dst_ref=got, send_sem=s_sems.at[a * 3 + j], recv_sem=r_sems.at[a * 3 + j],
                    device_id=(x, y, c), device_id_type=MESH)
                cp.wait_send()
                cp.wait_recv()

    res = pl.pallas_call(
        body, name=name,
        out_shape=tuple(pltpu.HBM(f, BF16) for f in fulls),
        in_specs=[HBM_SPEC] * n_arr + [SEM_SPEC, SEM_SPEC, ANY_SPEC],
        out_specs=tuple([HBM_SPEC] * n_arr),
        input_output_aliases={a: a for a in range(n_arr)},
        compiler_params=SPLIT_COPY_PARAMS,
    )(*lands, send_sems, recv_sems, after)
    return list(res)


def _gather_share(lands, axes, name):
    n_arr = len(lands)
    fulls = [tuple(l.shape) for l in lands]

    def body(*refs):
        land_in, land = refs[:n_arr], refs[n_arr:2 * n_arr]
        send_sems, recv_sems = refs[2 * n_arr:]
        x, y, c, chips = _place()
        copies = []
        for a in range(n_arr):
            for j, chip in enumerate(chips):
                k_j = 2 * chip[0] + chip[1]
                cp = pltpu.make_async_remote_copy(
                    src_ref=_shard_region(land_in[a], fulls[a], axes[a], k_j, c),
                    dst_ref=_shard_region(land[a], fulls[a], axes[a], k_j, c),
                    send_sem=send_sems.at[a * 3 + j], recv_sem=recv_sems.at[a * 3 + j],
                    device_id=(x, y, 1 - c), device_id_type=MESH)
                cp.start()
                copies.append(cp)
        for cp in copies:
            cp.wait()

    return pl.pallas_call(
        body, name=name, out_shape=[jax.ShapeDtypeStruct(f, BF16) for f in fulls],
        in_specs=[ANY_SPEC] * n_arr, out_specs=[ANY_SPEC] * n_arr,
        input_output_aliases={a: a for a in range(n_arr)},
        scratch_shapes=[pltpu.SemaphoreType.DMA((3 * n_arr,)), pltpu.SemaphoreType.DMA((3 * n_arr,))],
    )(*lands)


def _scatter_shapes(grads, axes):
    out = []
    for g, ax in zip(grads, axes):
        shp = list(g.shape)
        shp[ax] //= 4
        out.append((3,) + tuple(shp[1:]))
    return out


def _scatter_start(grads, axes, name, after):
    n_arr = len(grads)
    shapes = _scatter_shapes(grads, axes)
    lands = [lax.empty(s, BF16) for s in shapes]

    def body(*refs):
        ins, land = refs[:n_arr], refs[n_arr:2 * n_arr]
        send_sems, recv_sems = refs[2 * n_arr + 1:2 * n_arr + 3]
        token = refs[-1]
        x, y, c, chips = _place()
        for a in range(n_arr):
            for j, chip in enumerate(chips):
                k_j = 2 * chip[0] + chip[1]
                pltpu.make_async_remote_copy(
                    src_ref=_shard_region(ins[a], grads[a].shape, axes[a], k_j), dst_ref=land[a].at[pl.ds(j, 1)],
                    send_sem=send_sems.at[a * 3 + j], recv_sem=recv_sems.at[a * 3 + j],
                    device_id=(*chip, c), device_id_type=MESH).start()
        token[...] = jnp.zeros_like(token)

    res = pl.pallas_call(
        body, name=name,
        out_shape=(pltpu.SemaphoreType.DMA((3 * n_arr,)), pltpu.SemaphoreType.DMA((3 * n_arr,)),
                   *[pltpu.HBM(g.shape, BF16) for g in grads], *[pltpu.HBM(s, BF16) for s in shapes],
                   jax.ShapeDtypeStruct((8, 128), F32)),
        in_specs=[HBM_SPEC] * (2 * n_arr) + [ANY_SPEC],
        out_specs=(SEM_SPEC, SEM_SPEC, *[HBM_SPEC] * (2 * n_arr), pl.BlockSpec(memory_space=pltpu.VMEM)),
        input_output_aliases={a: 2 + a for a in range(2 * n_arr)},
        compiler_params=SPLIT_COPY_PARAMS,
    )(*_in_hbm(grads), *_in_hbm(lands), after)
    return res[0], res[1], list(res[2:2 + n_arr]), list(res[2 + n_arr:2 + 2 * n_arr]), res[-1]


def _scatter_wait(send_sems, recv_sems, grads, lands, axes, name, after):
    n_arr = len(grads)

    def body(*refs):
        ins, land = refs[:n_arr], refs[n_arr:2 * n_arr]
        s_sems, r_sems = refs[2 * n_arr:2 * n_arr + 2]
        x, y, c, chips = _place()
        for a in range(n_arr):
            for j, chip in enumerate(chips):
                k_j = 2 * chip[0] + chip[1]
                cp = pltpu.make_async_remote_copy(
                    src_ref=_shard_region(ins[a], grads[a].shape, axes[a], k_j), dst_ref=land[a].at[pl.ds(j, 1)],
                    send_sem=s_sems.at[a * 3 + j], recv_sem=r_sems.at[a * 3 + j],
                    device_id=(x, y, c), device_id_type=MESH)
                cp.wait_send()
                cp.wait_recv()

    res = pl.pallas_call(
        body, name=name,
        out_shape=(*[pltpu.HBM(g.shape, BF16) for g in grads], *[pltpu.HBM(l.shape, BF16) for l in lands]),
        in_specs=[HBM_SPEC] * (2 * n_arr) + [SEM_SPEC, SEM_SPEC, ANY_SPEC],
        out_specs=tuple([HBM_SPEC] * (2 * n_arr)),
        input_output_aliases={a: a for a in range(2 * n_arr)},
        compiler_params=SPLIT_COPY_PARAMS,
    )(*grads, *lands, send_sems, recv_sems, after)
    return list(res[:n_arr]), list(res[n_arr:])


def _swap_with_sibling(arrs, name):
    n_arr = len(arrs)

    def body(*refs):
        ins, outs = refs[:n_arr], refs[n_arr:2 * n_arr]
        send_sems, recv_sems = refs[2 * n_arr:]
        x, y, c, _ = _place()
        copies = []
        for a in range(n_arr):
            cp = pltpu.make_async_remote_copy(
                src_ref=ins[a], dst_ref=outs[a], send_sem=send_sems.at[a], recv_sem=recv_sems.at[a],
                device_id=(x, y, 1 - c), device_id_type=MESH)
            cp.start()
            copies.append(cp)
        for cp in copies:
            cp.wait()

    any_spec = pl.BlockSpec(memory_space=pl.ANY)
    return pl.pallas_call(
        body, name=name, out_shape=[jax.ShapeDtypeStruct(a.shape, a.dtype) for a in arrs],
        in_specs=[any_spec] * n_arr, out_specs=[any_spec] * n_arr,
        scratch_shapes=[pltpu.SemaphoreType.DMA((n_arr,)), pltpu.SemaphoreType.DMA((n_arr,))],
    )(*arrs)


def _mm_tn(a, b, name, out_dtype=BF16):
    L, m = a.shape
    n = b.shape[1]
    bm, bn, bk = min(m, 1024), min(n, 1024), min(L, 2048)
    nk = L // bk

    def body(a_ref, b_ref, o_ref, acc):
        k = pl.program_id(2)

        @pl.when(k == 0)
        def _():
            acc[...] = jnp.zeros_like(acc)

        acc[...] += _dot_tn(_bf(a_ref[...]), _bf(b_ref[...]))

        @pl.when(k == nk - 1)
        def _():
            o_ref[...] = acc[...].astype(out_dtype)

    return pl.pallas_call(
        body, name=name, grid=(m // bm, n // bn, nk),
        in_specs=[pl.BlockSpec((bk, bm), lambda i, j, k: (k, i)), pl.BlockSpec((bk, bn), lambda i, j, k: (k, j))],
        out_specs=pl.BlockSpec((bm, bn), lambda i, j, k: (i, j)),
        out_shape=jax.ShapeDtypeStruct((m, n), out_dtype),
        scratch_shapes=[pltpu.VMEM((bm, bn), F32)],
        compiler_params=_params(3),
    )(a, b)


def _mm_tn_blocks(a, b, wa, wb, name):
    L = a.shape[0]
    nb = a.shape[1] // wa
    bk = min(L, 1024)
    nk = L // bk

    def body(a_ref, b_ref, o_ref):
        @pl.when(pl.program_id(1) == 0)
        def _():
            o_ref[...] = jnp.zeros_like(o_ref)

        o_ref[...] += _dot_tn(_bf(a_ref[...]), _bf(b_ref[...]))

    return pl.pallas_call(
        body, name=name, grid=(nb, nk),
        in_specs=[pl.BlockSpec((bk, wa), lambda j, k: (k, j)), pl.BlockSpec((bk, wb), lambda j, k: (k, j))],
        out_specs=pl.BlockSpec((None, wa, wb), lambda j, k: (j, 0, 0)),
        out_shape=jax.ShapeDtypeStruct((nb, wa, wb), F32),
        compiler_params=_params(2),
    )(a, b)


def _sum_parts(parts, own, axis, chip, name):
    _, r, c = parts.shape
    tr = min(r, 256)
    if axis == 1:
        own_spec = pl.BlockSpec((None, tr, c), lambda i, k: (0, k[0] * (r // tr) + i, 0))
    else:
        own_spec = pl.BlockSpec((None, tr, c), lambda i, k: (0, i, k[0]))

    def body(k_ref, p_ref, g_ref, o_ref):
        p = p_ref[...].astype(F32)
        o_ref[...] = ((p[0] + p[1]) + p[2]) + g_ref[...].astype(F32)

    return pl.pallas_call(
        body, name=name,
        grid_spec=pltpu.PrefetchScalarGridSpec(
            num_scalar_prefetch=1, grid=(r // tr,),
            in_specs=[pl.BlockSpec((3, tr, c), lambda i, k: (0, i, 0)), own_spec],
            out_specs=pl.BlockSpec((tr, c), lambda i, k: (i, 0))),
        out_shape=jax.ShapeDtypeStruct((r, c), F32),
        compiler_params=_params(1),
    )(chip, parts, own)


def _adamw(w, g_parts, m, v, name):
    r, c = w.shape
    tr = r
    for cand in (512, 256, 128, 64, 32, 16, 8):
        if r % cand == 0 and cand * c * 4 <= (2 << 20):
            tr = cand
            break
    n_g = len(g_parts)
    c1 = 1.0 / (1.0 - ADAM_B1 ** ADAM_STEP)
    c2 = 1.0 / (1.0 - ADAM_B2 ** ADAM_STEP)

    def body(*refs):
        w_ref, g_refs, m_ref, v_ref = refs[0], refs[1:1 + n_g], refs[1 + n_g], refs[2 + n_g]
        g_out, d_out, m_out, v_out = refs[3 + n_g:]
        g = g_refs[0][...]
        for gr in g_refs[1:]:
            g = g + gr[...]
        m_new = ADAM_B1 * m_ref[...] + (1.0 - ADAM_B1) * g
        v_new = ADAM_B2 * v_ref[...] + (1.0 - ADAM_B2) * (g * g)
        m_hat = m_new * c1
        v_hat = v_new * c2
        g_out[...] = g
        d_out[...] = -ADAM_LR * (m_hat / (jnp.sqrt(v_hat) + ADAM_EPS) + ADAM_WD * w_ref[...])
        m_out[...] = m_new
        v_out[...] = v_new

    spec = pl.BlockSpec((tr, c), lambda i: (i, 0))
    return pl.pallas_call(
        body, name=name, grid=(r // tr,),
        in_specs=[spec] * (3 + n_g), out_specs=[spec] * 4,
        out_shape=[jax.ShapeDtypeStruct((r, c), F32)] * 4,
        compiler_params=_params(1),
    )(w, *g_parts, m, v)


def _ada_fwd(c16, ada_w, ada_b_cols):
    cols = ada_w.shape[2]

    def body(c_ref, w_ref, b_ref, o_ref):
        cv = c_ref[...]
        ca = _bf(cv * jax.nn.sigmoid(cv))
        o_ref[...] = _dot(ca, _bf(w_ref[...])) + b_ref[...]

    return pl.pallas_call(
        body, name="ada_fwd", grid=(DEPTH,),
        in_specs=[_whole((16, D)), pl.BlockSpec((None, D, cols), lambda i: (i, 0, 0)),
                  pl.BlockSpec((None, 1, cols), lambda i: (i, 0, 0))],
        out_specs=pl.BlockSpec((None, 16, cols), lambda i: (i, 0, 0)),
        out_shape=jax.ShapeDtypeStruct((DEPTH, 16, cols), F32),
        compiler_params=_params(1),
    )(c16, ada_w, ada_b_cols)


def _ada_bwd(c16, dmod16):
    cols = dmod16.shape[2]

    def body(c_ref, d_ref, o_ref):
        cv = c_ref[...]
        ca = _bf(cv * jax.nn.sigmoid(cv))
        o_ref[...] = _dot_tn(ca, _bf(d_ref[...]))

    return pl.pallas_call(
        body, name="ada_bwd", grid=(DEPTH,),
        in_specs=[_whole((16, D)), pl.BlockSpec((None, 16, cols), lambda i: (i, 0, 0))],
        out_specs=pl.BlockSpec((None, D, cols), lambda i: (i, 0, 0)),
        out_shape=jax.ShapeDtypeStruct((DEPTH, D, cols), F32),
        compiler_params=_params(1),
    )(c16, dmod16)


def _mod_bwd(vs_mix, vs_ffn, pv):
    def body(m_ref, f_ref, pv_ref, o_ref):
        for i in range(DEPTH):
            vm, vf, p = m_ref[i], f_ref[i], pv_ref[i]
            o_ref[i] = jnp.concatenate([
                vm[2:3], vm[1:2] * p[R_N1:R_N1 + 1], vm[0:1],
                vf[2:3], vf[1:2] * p[R_N2:R_N2 + 1], vf[0:1],
                vm[1:2] * (1.0 + p[R_SC1:R_SC1 + 1]), vf[1:2] * (1.0 + p[R_SC2:R_SC2 + 1])], axis=0)

    return pl.pallas_call(body, name="mod_bwd", out_shape=jax.ShapeDtypeStruct((DEPTH, 8, D), F32))(vs_mix, vs_ffn, pv)


def _ffn_fwd(x1, pv, w1, w2, layer, tm):
    L = x1.shape[0]
    dff = w1.shape[2]

    def body(x1_ref, pv_ref, w1_ref, w2_ref, x2_ref, h2_ref, a_ref, f_ref):
        x1v, p = x1_ref[...], pv_ref[...]
        h2, _, _ = _norm_mod(x1v, p[R_N2:R_N2 + 1], p[R_SC2:R_SC2 + 1], p[R_SH2:R_SH2 + 1])
        hb = _bf(h2)
        h2_ref[...] = hb
        a = _dot(hb, w1_ref[...])
        a_ref[...] = a
        ra = jnp.maximum(a, 0.0)
        f = _dot(_bf(ra * ra), w2_ref[...])
        f_ref[...] = f
        x2_ref[...] = x1v + p[R_G2:R_G2 + 1] * f

    return pl.pallas_call(
        body, name=f"ffn_fwd{layer}", grid=(L // tm,),
        in_specs=[_rows(tm, D), pl.BlockSpec((None, 8, D), lambda i: (layer, 0, 0)), _layer_w(D, dff, 0), _layer_w(dff, D, 0)],
        out_specs=[_rows(tm, D), _rows(tm, D), _rows(tm, dff), _rows(tm, D)],
        out_shape=[jax.ShapeDtypeStruct((L, D), F32), jax.ShapeDtypeStruct((L, D), BF16),
                   jax.ShapeDtypeStruct((L, dff), F32), jax.ShapeDtypeStruct((L, D), F32)],
        compiler_params=_params(1, 56),
    )(x1, pv, w1, w2)


def _ffn_bwd(dx2, x1, a, f, pv, w1, w2, layer, tm, after=None):
    L = x1.shape[0]
    dff = w1.shape[2]
    extra = [] if after is None else [pl.BlockSpec(memory_space=pl.ANY)]
    extra_args = [] if after is None else [after]

    def body(dx2_ref, x1_ref, a_ref, f_ref, pv_ref, w1_ref, w2_ref, *rest):
        dx1_ref, p_ref, da_ref, df_ref, vs_ref = rest[len(extra):]

        @pl.when(pl.program_id(0) == 0)
        def _():
            vs_ref[...] = jnp.zeros_like(vs_ref)

        dx2v, p = dx2_ref[...], pv_ref[...]
        dfb = _bf(dx2v * p[R_G2:R_G2 + 1])
        df_ref[...] = dfb
        vs_ref[0:1, :] += _sum0(dx2v * f_ref[...])
        dp = _dot_nt(dfb, w2_ref[...])
        ra = jnp.maximum(a_ref[...], 0.0)
        p_ref[...] = _bf(ra * ra)
        dab = _bf(dp * (2.0 * ra))
        da_ref[...] = dab
        dh2 = _dot_nt(dab, w1_ref[...])
        _, xn, r = _norm_mod(x1_ref[...], p[R_N2:R_N2 + 1], p[R_SC2:R_SC2 + 1], p[R_SH2:R_SH2 + 1])
        dx1_ref[...] = dx2v + _norm_mod_bwd(dh2, xn, r, p[R_N2:R_N2 + 1], p[R_SC2:R_SC2 + 1])
        vs_ref[1:2, :] += _sum0(dh2 * xn)
        vs_ref[2:3, :] += _sum0(dh2)

    return pl.pallas_call(
        body, name=f"ffn_bwd{layer}", grid=(L // tm,),
        in_specs=[_rows(tm, D), _rows(tm, D), _rows(tm, dff), _rows(tm, D),
                  pl.BlockSpec((None, 8, D), lambda i: (layer, 0, 0)), _layer_w(D, dff, 0), _layer_w(dff, D, 0)] + extra,
        out_specs=[_rows(tm, D), _rows(tm, dff), _rows(tm, dff), _rows(tm, D), _whole((8, D))],
        out_shape=[jax.ShapeDtypeStruct((L, D), F32), jax.ShapeDtypeStruct((L, dff), BF16),
                   jax.ShapeDtypeStruct((L, dff), BF16), jax.ShapeDtypeStruct((L, D), BF16),
                   jax.ShapeDtypeStruct((8, D), F32)],
        compiler_params=_params(1, 56),
    )(dx2, x1, a, f, pv, w1, w2, *extra_args)


def _conv_fwd(x, pv, w_in, w_out, cw, layer, j, tm, after=None):
    L = x.shape[0]

    def body(x_ref, pv_ref, win_ref, wout_ref, cw_ref, x1_ref, h_ref, bcx_ref, conv_ref, q_ref, y_ref, carry):
        @pl.when(pl.program_id(0) == 0)
        def _():
            carry[...] = jnp.zeros_like(carry)

        xv, p, cwv = x_ref[...], pv_ref[...], cw_ref[...]
        h, _, _ = _norm_mod(xv, p[R_N1:R_N1 + 1], p[R_SC1:R_SC1 + 1], p[R_SH1:R_SH1 + 1])
        hb = _bf(h)
        h_ref[...] = hb
        bcx = _dot(hb, win_ref[...])
        bcx_ref[...] = bcx
        z = bcx[:, D:2 * D] * bcx[:, 2 * D:]
        prev8 = carry[...]
        conv = cwv[0:1] * _shift_down(z, prev8, 2) + cwv[1:2] * _shift_down(z, prev8, 1) + cwv[2:3] * z + cwv[3:4]
        conv_ref[...] = conv
        qb = _bf(bcx[:, :D] * conv)
        q_ref[...] = qb
        y = _dot(qb, wout_ref[...])
        y_ref[...] = y
        x1_ref[...] = xv + p[R_G1:R_G1 + 1] * y
        carry[...] = z[tm - 8:tm]

    call, tail = _call_after(
        after, body, 5, name=f"conv_fwd{layer}", grid=(L // tm,),
        in_specs=[_rows(tm, D), pl.BlockSpec((None, 8, D), lambda i: (layer, 0, 0)), _layer_w(D, 3 * D, 0), _layer_w(D, D, 0),
                  pl.BlockSpec((None, 8, D), lambda i: (j, 0, 0))],
        out_specs=[_rows(tm, D), _rows(tm, D), _rows(tm, 3 * D), _rows(tm, D), _rows(tm, D), _rows(tm, D)],
        out_shape=[jax.ShapeDtypeStruct((L, D), F32), jax.ShapeDtypeStruct((L, D), BF16), jax.ShapeDtypeStruct((L, 3 * D), F32),
                   jax.ShapeDtypeStruct((L, D), F32), jax.ShapeDtypeStruct((L, D), BF16), jax.ShapeDtypeStruct((L, D), F32)],
        scratch_shapes=[pltpu.VMEM((8, D), F32)],
        compiler_params=_params(1, 56),
    )
    return call(x, pv, w_in, w_out, cw, *tail)


def _conv_bwd(dx1, x, y, bcx, conv, pv, w_in, w_out, cw, layer, j, tm, after=None):
    L = x.shape[0]
    nt = L // tm

    def body(dx1_ref, x_ref, y_ref, bcx_ref, conv_ref, halo_ref, pv_ref, win_ref, wout_ref, cw_ref,
             dx_ref, dbcx_ref, dy_ref, vs_ref, carry):
        gi = pl.program_id(0)
        tile = nt - 1 - gi

        @pl.when(gi == 0)
        def _():
            vs_ref[...] = jnp.zeros_like(vs_ref)
            carry[...] = jnp.zeros_like(carry)

        dx1v, p, cwv = dx1_ref[...], pv_ref[...], cw_ref[...]
        dyb = _bf(dx1v * p[R_G1:R_G1 + 1])
        dy_ref[...] = dyb
        vs_ref[0:1, :] += _sum0(dx1v * y_ref[...])
        dq = _dot_nt(dyb, wout_ref[...])
        bcx = bcx_ref[...]
        b, cg, xh = bcx[:, :D], bcx[:, D:2 * D], bcx[:, 2 * D:]
        db = dq * conv_ref[...]
        dc = dq * b
        z = cg * xh
        halo = halo_ref[...]
        zprev = jnp.where(tile > 0, halo[:, D:2 * D] * halo[:, 2 * D:], 0.0)
        vs_ref[3:4, :] += _sum0(dc * _shift_down(z, zprev, 2))
        vs_ref[4:5, :] += _sum0(dc * _shift_down(z, zprev, 1))
        vs_ref[5:6, :] += _sum0(dc * z)
        vs_ref[6:7, :] += _sum0(dc)
        next8 = carry[...]
        dz = cwv[2:3] * dc + cwv[1:2] * _shift_up(dc, next8, 1) + cwv[0:1] * _shift_up(dc, next8, 2)
        dbb, dcgb, dxhb = _bf(db), _bf(dz * xh), _bf(dz * cg)
        dbcx_ref[:, 0:D] = dbb
        dbcx_ref[:, D:2 * D] = dcgb
        dbcx_ref[:, 2 * D:3 * D] = dxhb
        dh = (_dot_nt(dbb, win_ref[:, 0:D]) + _dot_nt(dcgb, win_ref[:, D:2 * D])) + _dot_nt(dxhb, win_ref[:, 2 * D:3 * D])
        _, xn, r = _norm_mod(x_ref[...], p[R_N1:R_N1 + 1], p[R_SC1:R_SC1 + 1], p[R_SH1:R_SH1 + 1])
        dx_ref[...] = dx1v + _norm_mod_bwd(dh, xn, r, p[R_N1:R_N1 + 1], p[R_SC1:R_SC1 + 1])
        vs_ref[1:2, :] += _sum0(dh * xn)
        vs_ref[2:3, :] += _sum0(dh)
        carry[...] = dc[0:8]

    halo_spec = pl.BlockSpec((8, 3 * D), lambda i: (jnp.maximum((nt - 1 - i) * (tm // 8) - 1, 0), 0))
    call, tail = _call_after(
        after, body, 10, name=f"conv_bwd{layer}", grid=(nt,),
        in_specs=[_rows(tm, D, nt), _rows(tm, D, nt), _rows(tm, D, nt), _rows(tm, 3 * D, nt), _rows(tm, D, nt), halo_spec,
                  pl.BlockSpec((None, 8, D), lambda i: (layer, 0, 0)), _layer_w(D, 3 * D, 0), _layer_w(D, D, 0),
                  pl.BlockSpec((None, 8, D), lambda i: (j, 0, 0))],
        out_specs=[_rows(tm, D, nt), _rows(tm, 3 * D, nt), _rows(tm, D, nt), _whole((8, D))],
        out_shape=[jax.ShapeDtypeStruct((L, D), F32), jax.ShapeDtypeStruct((L, 3 * D), BF16),
                   jax.ShapeDtypeStruct((L, D), BF16), jax.ShapeDtypeStruct((8, D), F32)],
        scratch_shapes=[pltpu.VMEM((8, D), F32)],
        compiler_params=_params(1, 56),
    )
    return call(dx1, x, y, bcx, conv, bcx, pv, w_in, w_out, cw, *tail)


def _s5_discretize(a_re, a_im, log_dt, bt_re, bt_im):
    dt = jnp.exp(log_dt)
    mag = jnp.exp(a_re * dt)
    abar_re = mag * jnp.cos(a_im * dt)
    abar_im = mag * jnp.sin(a_im * dt)
    den = a_re * a_re + a_im * a_im
    nr = abar_re - 1.0
    ni = abar_im
    f_re = (nr * a_re + ni * a_im) / den
    f_im = (ni * a_re - nr * a_im) / den
    bbar_re = f_re * bt_re - f_im * bt_im
    bbar_im = f_re * bt_im + f_im * bt_re
    return abar_re, abar_im, bbar_re, bbar_im


def _s5_params_fwd(a_re, a_im, log_dt, bt_re, bt_im):
    def body(ar, ai, ld, br, bi, o_ar, o_ai, o_br, o_bi):
        r = _s5_discretize(ar[...], ai[...], ld[...], br[...], bi[...])
        o_ar[...], o_ai[...], o_br[...], o_bi[...] = r

    gp = jax.ShapeDtypeStruct((S5_G, S5_P), F32)
    hgp = jax.ShapeDtypeStruct((S5_H, S5_G, S5_P), F32)
    return pl.pallas_call(body, name="s5_params_fwd", out_shape=[gp, gp, hgp, hgp])(a_re, a_im, log_dt, bt_re, bt_im)


def _s5_params_bwd(a_re, a_im, log_dt, bt_re, bt_im, d_ar, d_ai, d_br, d_bi):
    def body(ar, ai, ld, br, bi, gar, gai, gbr, gbi, o_ar, o_ai, o_ld, o_br, o_bi):
        _, vjp = jax.vjp(_s5_discretize, ar[...], ai[...], ld[...], br[...], bi[...])
        r = vjp((gar[...], gai[...], gbr[...], gbi[...]))
        o_ar[...], o_ai[...], o_ld[...], o_br[...], o_bi[...] = r

    gp = jax.ShapeDtypeStruct((S5_G, S5_P), F32)
    hgp = jax.ShapeDtypeStruct((S5_H, S5_G, S5_P), F32)
    return pl.pallas_call(body, name="s5_params_bwd", out_shape=[gp, gp, jax.ShapeDtypeStruct((S5_G, 1), F32), hgp, hgp])(
        a_re, a_im, log_dt, bt_re, bt_im, d_ar, d_ai, d_br, d_bi)


def _s5_in_fwd(x, pv, w_in, b_re, b_im, layer, tm, after=None):
    L = x.shape[0]

    def body(x_ref, pv_ref, win_ref, bre_ref, bim_ref, h_ref, u_ref, ore_ref, oim_ref):
        p = pv_ref[...]
        h, _, _ = _norm_mod(x_ref[...], p[R_N1:R_N1 + 1], p[R_SC1:R_SC1 + 1], p[R_SH1:R_SH1 + 1])
        hb = _bf(h)
        h_ref[...] = hb
        u = _dot(hb, win_ref[...])
        u_ref[...] = u
        ub = _bf(u)
        for k in range(S5_NB):
            uk = ub[:, k * S5_BH:(k + 1) * S5_BH]
            ore_ref[:, k * S5_BP:(k + 1) * S5_BP] = _dot(uk, bre_ref[k])
            oim_ref[:, k * S5_BP:(k + 1) * S5_BP] = _dot(uk, bim_ref[k])

    call, tail = _call_after(
        after, body, 5, name="s5_in_fwd", grid=(L // tm,),
        in_specs=[_rows(tm, D), pl.BlockSpec((None, 8, D), lambda i: (layer, 0, 0)), _layer_w(D, D, 0),
                  _const_w((S5_NB, S5_BH, S5_BP)), _const_w((S5_NB, S5_BH, S5_BP))],
        out_specs=[_rows(tm, D), _rows(tm, D), _rows(tm, NSTATE), _rows(tm, NSTATE)],
        out_shape=[jax.ShapeDtypeStruct((L, D), BF16), jax.ShapeDtypeStruct((L, D), F32),
                   jax.ShapeDtypeStruct((L, NSTATE), F32), jax.ShapeDtypeStruct((L, NSTATE), F32)],
        compiler_params=_params(1, 56),
    )
    return call(x, pv, w_in, b_re, b_im, *tail)


def _s5_scan_fwd(bu_re, bu_im, ar, ai, tr):
    L = bu_re.shape[0]
    nl = 1024

    def body(bre_ref, bim_ref, ar_ref, ai_ref, sre_ref, sim_ref, st_re, st_im):
        @pl.when(pl.program_id(1) == 0)
        def _():
            st_re[...] = jnp.zeros_like(st_re)
            st_im[...] = jnp.zeros_like(st_im)

        a_r, a_i = ar_ref[...], ai_ref[...]

        def step(t, carry):
            s_r, s_i = carry
            n_r = a_r * s_r - a_i * s_i + bre_ref[pl.ds(t, 1), :]
            n_i = a_r * s_i + a_i * s_r + bim_ref[pl.ds(t, 1), :]
            sre_ref[pl.ds(t, 1), :] = n_r
            sim_ref[pl.ds(t, 1), :] = n_i
            return n_r, n_i

        s_r, s_i = lax.fori_loop(0, tr, step, (st_re[...], st_im[...]), unroll=8)
        st_re[...] = s_r
        st_im[...] = s_i

    blk = pl.BlockSpec((tr, nl), lambda j, i: (i, j))
    vec = pl.BlockSpec((1, nl), lambda j, i: (0, j))
    return pl.pallas_call(
        body, name="s5_scan_fwd", grid=(NSTATE // nl, L // tr),
        in_specs=[blk, blk, vec, vec], out_specs=[blk, blk],
        out_shape=[jax.ShapeDtypeStruct((L, NSTATE), F32)] * 2,
        scratch_shapes=[pltpu.VMEM((1, nl), F32), pltpu.VMEM((1, nl), F32)],
        compiler_params=_params(2),
    )(bu_re, bu_im, ar, ai)


def _s5_out_fwd(x, u, s_re, s_im, pv, c_re, c_im, dvec, glu_w, glu_b, w_out, layer, tm):
    L = x.shape[0]

    def body(x_ref, u_ref, sre_ref, sim_ref, pv_ref, cre_ref, cim_ref, d_ref, gw_ref, gb_ref, wout_ref,
             x1_ref, y1_ref, zg_ref, y3_ref, y_ref):
        p = pv_ref[...]
        srb, sib = _bf(sre_ref[...]), _bf(sim_ref[...])
        parts = []
        for k in range(S5_NB):
            sl = slice(k * S5_BP, (k + 1) * S5_BP)
            parts.append(_dot(srb[:, sl], cre_ref[k]) - _dot(sib[:, sl], cim_ref[k]))
        y1 = jnp.concatenate(parts, axis=1) + d_ref[...] * u_ref[...]
        y1_ref[...] = y1
        y2 = jax.nn.gelu(y1)
        zg = _dot(_bf(y2), gw_ref[...]) + gb_ref[...]
        zg_ref[...] = zg
        y3b = _bf(y2 * jax.nn.sigmoid(zg))
        y3_ref[...] = y3b
        y = _dot(y3b, wout_ref[...])
        y_ref[...] = y
        x1_ref[...] = x_ref[...] + p[R_G1:R_G1 + 1] * y

    return pl.pallas_call(
        body, name="s5_out_fwd", grid=(L // tm,),
        in_specs=[_rows(tm, D), _rows(tm, D), _rows(tm, NSTATE), _rows(tm, NSTATE),
                  pl.BlockSpec((None, 8, D), lambda i: (layer, 0, 0)),
                  _const_w((S5_NB, S5_BP, S5_BH)), _const_w((S5_NB, S5_BP, S5_BH)), _whole((1, D)),
                  _layer_w(D, D, 0), _whole((1, D)), _layer_w(D, D, 0)],
        out_specs=[_rows(tm, D)] * 5,
        out_shape=[jax.ShapeDtypeStruct((L, D), F32), jax.ShapeDtypeStruct((L, D), F32), jax.ShapeDtypeStruct((L, D), F32),
                   jax.ShapeDtypeStruct((L, D), BF16), jax.ShapeDtypeStruct((L, D), F32)],
        compiler_params=_params(1, 56),
    )(x, u, s_re, s_im, pv, c_re, c_im, dvec, glu_w, glu_b, w_out)


def _s5_out_bwd(dx1, y, y1, zg, u, pv, c_re, c_im, dvec, glu_w, w_out, layer, tm, after=None):
    L = dx1.shape[0]

    def body(dx1_ref, y_ref, y1_ref, zg_ref, u_ref, pv_ref, cre_ref, cim_ref, d_ref, gw_ref, wout_ref,
             dy_ref, y2_ref, dzg_ref, dy1_ref, dus_ref, gre_ref, gim_ref, vs_ref):
        @pl.when(pl.program_id(0) == 0)
        def _():
            vs_ref[...] = jnp.zeros_like(vs_ref)

        dx1v, p = dx1_ref[...], pv_ref[...]
        dyb = _bf(dx1v * p[R_G1:R_G1 + 1])
        dy_ref[...] = dyb
        vs_ref[0:1, :] += _sum0(dx1v * y_ref[...])
        dy3 = _dot_nt(dyb, wout_ref[...])
        y2, gelu_vjp = jax.vjp(jax.nn.gelu, y1_ref[...])
        y2_ref[...] = _bf(y2)
        gate = jax.nn.sigmoid(zg_ref[...])
        dzg = dy3 * y2 * gate * (1.0 - gate)
        dzgb = _bf(dzg)
        dzg_ref[...] = dzgb
        vs_ref[1:2, :] += _sum0(dzg)
        dy2 = dy3 * gate + _dot_nt(dzgb, gw_ref[...])
        dy1 = gelu_vjp(dy2)[0]
        vs_ref[2:3, :] += _sum0(dy1 * u_ref[...])
        dus_ref[...] = dy1 * d_ref[...]
        dy1b = _bf(dy1)
        dy1_ref[...] = dy1b
        for k in range(S5_NB):
            dk = dy1b[:, k * S5_BH:(k + 1) * S5_BH]
            gre_ref[:, k * S5_BP:(k + 1) * S5_BP] = _dot_nt(dk, cre_ref[k])
            gim_ref[:, k * S5_BP:(k + 1) * S5_BP] = -_dot_nt(dk, cim_ref[k])

    call, tail = _call_after(
        after, body, 11, name="s5_out_bwd", grid=(L // tm,),
        in_specs=[_rows(tm, D)] * 5 + [pl.BlockSpec((None, 8, D), lambda i: (layer, 0, 0)),
                  _const_w((S5_NB, S5_BP, S5_BH)), _const_w((S5_NB, S5_BP, S5_BH)), _whole((1, D)),
                  _layer_w(D, D, 0), _layer_w(D, D, 0)],
        out_specs=[_rows(tm, D)] * 5 + [_rows(tm, NSTATE), _rows(tm, NSTATE), _whole((8, D))],
        out_shape=[jax.ShapeDtypeStruct((L, D), BF16)] * 4 + [jax.ShapeDtypeStruct((L, D), F32),
                   jax.ShapeDtypeStruct((L, NSTATE), F32), jax.ShapeDtypeStruct((L, NSTATE), F32),
                   jax.ShapeDtypeStruct((8, D), F32)],
        compiler_params=_params(1, 56),
    )
    return call(dx1, y, y1, zg, u, pv, c_re, c_im, dvec, glu_w, w_out, *tail)


def _s5_scan_bwd(g_re, g_im, s_re, s_im, ar, ai, tr):
    L = g_re.shape[0]
    nl = 1024
    nt = L // tr

    def body(gre_ref, gim_ref, sre_ref, sim_ref, hre_ref, him_ref, ar_ref, ai_ref, lre_ref, lim_ref, da_ref, st_re, st_im):
        gi = pl.program_id(1)
        tile = nt - 1 - gi

        @pl.when(gi == 0)
        def _():
            st_re[...] = jnp.zeros_like(st_re)
            st_im[...] = jnp.zeros_like(st_im)
            da_ref[...] = jnp.zeros_like(da_ref)

        a_r, a_i = ar_ref[...], ai_ref[...]

        def step(k, carry):
            l_r, l_i = carry
            t = tr - 1 - k
            n_r = gre_ref[pl.ds(t, 1), :] + a_r * l_r + a_i * l_i
            n_i = gim_ref[pl.ds(t, 1), :] - a_i * l_r + a_r * l_i
            lre_ref[pl.ds(t, 1), :] = n_r
            lim_ref[pl.ds(t, 1), :] = n_i
            return n_r, n_i

        l_r, l_i = lax.fori_loop(0, tr, step, (st_re[...], st_im[...]), unroll=8)
        st_re[...] = l_r
        st_im[...] = l_i
        lam_r, lam_i = lre_ref[...], lim_ref[...]
        p_r = jnp.where(tile > 0, hre_ref[...], 0.0)
        p_i = jnp.where(tile > 0, him_ref[...], 0.0)
        sp_r = _shift_down(sre_ref[...], p_r, 1)
        sp_i = _shift_down(sim_ref[...], p_i, 1)
        da_ref[0:1, :] += _sum0(lam_r * sp_r + lam_i * sp_i)
        da_ref[1:2, :] += _sum0(lam_i * sp_r - lam_r * sp_i)

    blk = pl.BlockSpec((tr, nl), lambda j, i: (nt - 1 - i, j))
    halo = pl.BlockSpec((8, nl), lambda j, i: (jnp.maximum((nt - 1 - i) * (tr // 8) - 1, 0), j))
    vec = pl.BlockSpec((1, nl), lambda j, i: (0, j))
    return pl.pallas_call(
        body, name="s5_scan_bwd", grid=(NSTATE // nl, nt),
        in_specs=[blk, blk, blk, blk, halo, halo, vec, vec],
        out_specs=[blk, blk, pl.BlockSpec((8, nl), lambda j, i: (0, j))],
        out_shape=[jax.ShapeDtypeStruct((L, NSTATE), F32)] * 2 + [jax.ShapeDtypeStruct((8, NSTATE), F32)],
        scratch_shapes=[pltpu.VMEM((1, nl), F32), pltpu.VMEM((1, nl), F32)],
        compiler_params=_params(2),
    )(g_re, g_im, s_re, s_im, s_re, s_im, ar, ai)


def _s5_in_bwd(dx1, lam_re, lam_im, du_skip, x, pv, b_re, b_im, w_in, layer, tm):
    L = x.shape[0]

    def body(dx1_ref, lre_ref, lim_ref, dus_ref, x_ref, pv_ref, bre_ref, bim_ref, win_ref, dx_ref, du_ref, vs_ref):
        @pl.when(pl.program_id(0) == 0)
        def _():
            vs_ref[...] = jnp.zeros_like(vs_ref)

        p = pv_ref[...]
        lrb, lib = _bf(lre_ref[...]), _bf(lim_ref[...])
        parts = []
        for k in range(S5_NB):
            sl = slice(k * S5_BP, (k + 1) * S5_BP)
            parts.append(_dot_nt(lrb[:, sl], bre_ref[k]) + _dot_nt(lib[:, sl], bim_ref[k]))
        dub = _bf(jnp.concatenate(parts, axis=1) + dus_ref[...])
        du_ref[...] = dub
        dh = _dot_nt(dub, win_ref[...])
        _, xn, r = _norm_mod(x_ref[...], p[R_N1:R_N1 + 1], p[R_SC1:R_SC1 + 1], p[R_SH1:R_SH1 + 1])
        dx_ref[...] = dx1_ref[...] + _norm_mod_bwd(dh, xn, r, p[R_N1:R_N1 + 1], p[R_SC1:R_SC1 + 1])
        vs_ref[1:2, :] += _sum0(dh * xn)
        vs_ref[2:3, :] += _sum0(dh)

    return pl.pallas_call(
        body, name="s5_in_bwd", grid=(L // tm,),
        in_specs=[_rows(tm, D), _rows(tm, NSTATE), _rows(tm, NSTATE), _rows(tm, D), _rows(tm, D),
                  pl.BlockSpec((None, 8, D), lambda i: (layer, 0, 0)),
                  _const_w((S5_NB, S5_BH, S5_BP)), _const_w((S5_NB, S5_BH, S5_BP)), _layer_w(D, D, 0)],
        out_specs=[_rows(tm, D), _rows(tm, D), _whole((8, D))],
        out_shape=[jax.ShapeDtypeStruct((L, D), F32), jax.ShapeDtypeStruct((L, D), BF16), jax.ShapeDtypeStruct((8, D), F32)],
        compiler_params=_params(1, 56),
    )(dx1, lam_re, lam_im, du_skip, x, pv, b_re, b_im, w_in)


NSEG = 8
SCAN_LANES = 1024


def _to_segments(x):
    n, c = x.shape
    return x.reshape(NSEG, n // NSEG, c).transpose(1, 0, 2).reshape(n, c)


def _from_segments(x):
    n, c = x.shape
    return x.reshape(n // NSEG, NSEG, c).transpose(1, 0, 2).reshape(n, c)


def _segment_scan(re_ref, im_ref, st_re, st_im, a_re, a_im, n_slabs, adjoint, write):
    for q in range(NSTATE // SCAN_LANES):
        ls = slice(q * SCAN_LANES, (q + 1) * SCAN_LANES)
        ar = jnp.broadcast_to(a_re[:, ls], (8, SCAN_LANES))
        ai = jnp.broadcast_to(a_im[:, ls], (8, SCAN_LANES))

        def step(k, carry, ls=ls, ar=ar, ai=ai):
            s_r, s_i = carry
            slab = (n_slabs - 1 - k) if adjoint else k
            rows = pl.ds(pl.multiple_of(slab * 8, 8), 8)
            b_r, b_i = re_ref[rows, ls], im_ref[rows, ls]
            if adjoint:
                n_r = b_r + ar * s_r + ai * s_i
                n_i = b_i - ai * s_r + ar * s_i
            else:
                n_r = ar * s_r - ai * s_i + b_r
                n_i = ar * s_i + ai * s_r + b_i
            if write:
                re_ref[rows, ls] = n_r
                im_ref[rows, ls] = n_i
            return n_r, n_i

        s_r, s_i = lax.fori_loop(0, n_slabs, step, (st_re[:, ls], st_im[:, ls]), unroll=4)
        st_re[:, ls] = s_r
        st_im[:, ls] = s_i


def _s5_segment_states(e_re, e_im, ar, ai, seg_len, adjoint):
    def body(ere_ref, eim_ref, ar_ref, ai_ref, ore_ref, oim_ref):
        p_r, p_i = ar_ref[...], ai_ref[...]
        if adjoint:
            p_i = -p_i
        acc_r, acc_i = jnp.ones_like(p_r), jnp.zeros_like(p_r)
        n = seg_len
        while n:
            if n & 1:
                acc_r, acc_i = acc_r * p_r - acc_i * p_i, acc_r * p_i + acc_i * p_r
            n >>= 1
            if n:
                p_r, p_i = p_r * p_r - p_i * p_i, 2.0 * p_r * p_i
        e_r, e_i = ere_ref[...], eim_ref[...]
        s_r, s_i = jnp.zeros_like(acc_r), jnp.zeros_like(acc_r)
        rows_r, rows_i = [None] * NSEG, [None] * NSEG
        order = range(NSEG - 1, -1, -1) if adjoint else range(NSEG)
        for j in order:
            rows_r[j], rows_i[j] = s_r, s_i
            s_r, s_i = (acc_r * s_r - acc_i * s_i + e_r[j:j + 1], acc_r * s_i + acc_i * s_r + e_i[j:j + 1])
        ore_ref[...] = jnp.concatenate(rows_r, axis=0)
        oim_ref[...] = jnp.concatenate(rows_i, axis=0)

    st = jax.ShapeDtypeStruct((NSEG, NSTATE), F32)
    return pl.pallas_call(body, name="s5_segment_states_bwd" if adjoint else "s5_segment_states_fwd", out_shape=[st, st])(
        e_re, e_im, ar, ai)


def _s5_fwd_ends(x, pv, w_in, b_re, b_im, ar, ai, layer, tm, after=None):
    L = x.shape[0]

    def body(x_ref, pv_ref, win_ref, bre_ref, bim_ref, ar_ref, ai_ref, h_ref, u_ref, ere_ref, eim_ref, bu_re, bu_im):
        @pl.when(pl.program_id(0) == 0)
        def _():
            ere_ref[...] = jnp.zeros_like(ere_ref)
            eim_ref[...] = jnp.zeros_like(eim_ref)

        p = pv_ref[...]
        h, _, _ = _norm_mod(x_ref[...], p[R_N1:R_N1 + 1], p[R_SC1:R_SC1 + 1], p[R_SH1:R_SH1 + 1])
        hb = _bf(h)
        h_ref[...] = hb
        u = _dot(hb, win_ref[...])
        u_ref[...] = u
        ub = _bf(u)
        for k in range(S5_NB):
            uk = ub[:, k * S5_BH:(k + 1) * S5_BH]
            bu_re[:, k * S5_BP:(k + 1) * S5_BP] = _dot(uk, bre_ref[k])
            bu_im[:, k * S5_BP:(k + 1) * S5_BP] = _dot(uk, bim_ref[k])
        _segment_scan(bu_re, bu_im, ere_ref, eim_ref, ar_ref[...], ai_ref[...], tm // 8, adjoint=False, write=False)

    call, tail = _call_after(
        after, body, 7, name="s5_fwd_ends", grid=(L // tm,),
        in_specs=[_rows(tm, D), pl.BlockSpec((None, 8, D), lambda i: (layer, 0, 0)), _layer_w(D, D, 0),
                  _const_w((S5_NB, S5_BH, S5_BP)), _const_w((S5_NB, S5_BH, S5_BP)), _whole((1, NSTATE)), _whole((1, NSTATE))],
        out_specs=[_rows(tm, D), _rows(tm, D), _whole((NSEG, NSTATE)), _whole((NSEG, NSTATE))],
        out_shape=[jax.ShapeDtypeStruct((L, D), BF16), jax.ShapeDtypeStruct((L, D), F32),
                   jax.ShapeDtypeStruct((NSEG, NSTATE), F32), jax.ShapeDtypeStruct((NSEG, NSTATE), F32)],
        scratch_shapes=[pltpu.VMEM((tm, NSTATE), F32), pltpu.VMEM((tm, NSTATE), F32)],
        compiler_params=_params(1, 56),
    )
    return call(x, pv, w_in, b_re, b_im, ar, ai, *tail)


def _s5_fwd_out(x, u, pv, b_re, b_im, s0_re, s0_im, ar, ai, c_re, c_im, dvec, glu_w, glu_b, w_out, layer, tm):
    L = x.shape[0]

    def body(x_ref, u_ref, pv_ref, bre_ref, bim_ref, s0re_ref, s0im_ref, ar_ref, ai_ref, cre_ref, cim_ref, d_ref, gw_ref,
             gb_ref, wout_ref, x1_ref, sre_ref, sim_ref, y1_ref, zg_ref, y3_ref, y_ref, st_re, st_im):
        @pl.when(pl.program_id(0) == 0)
        def _():
            st_re[...] = s0re_ref[...]
            st_im[...] = s0im_ref[...]

        p = pv_ref[...]
        uv = u_ref[...]
        ub = _bf(uv)
        for k in range(S5_NB):
            uk = ub[:, k * S5_BH:(k + 1) * S5_BH]
            sre_ref[:, k * S5_BP:(k + 1) * S5_BP] = _dot(uk, bre_ref[k])
            sim_ref[:, k * S5_BP:(k + 1) * S5_BP] = _dot(uk, bim_ref[k])
        _segment_scan(sre_ref, sim_ref, st_re, st_im, ar_ref[...], ai_ref[...], tm // 8, adjoint=False, write=True)
        parts = []
        for k in range(S5_NB):
            sl = slice(k * S5_BP, (k + 1) * S5_BP)
            parts.append(_dot(_bf(sre_ref[:, sl]), cre_ref[k]) - _dot(_bf(sim_ref[:, sl]), cim_ref[k]))
        y1 = jnp.concatenate(parts, axis=1) + d_ref[...] * uv
        y1_ref[...] = y1
        y2 = jax.nn.gelu(y1)
        zg = _dot(_bf(y2), gw_ref[...]) + gb_ref[...]
        zg_ref[...] = zg
        y3b = _bf(y2 * jax.nn.sigmoid(zg))
        y3_ref[...] = y3b
        y = _dot(y3b, wout_ref[...])
        y_ref[...] = y
        x1_ref[...] = x_ref[...] + p[R_G1:R_G1 + 1] * y

    return pl.pallas_call(
        body, name="s5_fwd_out", grid=(L // tm,),
        in_specs=[_rows(tm, D), _rows(tm, D), pl.BlockSpec((None, 8, D), lambda i: (layer, 0, 0)),
                  _const_w((S5_NB, S5_BH, S5_BP)), _const_w((S5_NB, S5_BH, S5_BP)),
                  _whole((NSEG, NSTATE)), _whole((NSEG, NSTATE)), _whole((1, NSTATE)), _whole((1, NSTATE)),
                  _const_w((S5_NB, S5_BP, S5_BH)), _const_w((S5_NB, S5_BP, S5_BH)), _whole((1, D)),
                  _layer_w(D, D, 0), _whole((1, D)), _layer_w(D, D, 0)],
        out_specs=[_rows(tm, D), _rows(tm, NSTATE), _rows(tm, NSTATE), _rows(tm, D), _rows(tm, D), _rows(tm, D), _rows(tm, D)],
        out_shape=[jax.ShapeDtypeStruct((L, D), F32), jax.ShapeDtypeStruct((L, NSTATE), F32), jax.ShapeDtypeStruct((L, NSTATE), F32),
                   jax.ShapeDtypeStruct((L, D), F32), jax.ShapeDtypeStruct((L, D), F32),
                   jax.ShapeDtypeStruct((L, D), BF16), jax.ShapeDtypeStruct((L, D), F32)],
        scratch_shapes=[pltpu.VMEM((NSEG, NSTATE), F32), pltpu.VMEM((NSEG, NSTATE), F32)],
        compiler_params=_params(1, 56),
    )(x, u, pv, b_re, b_im, s0_re, s0_im, ar, ai, c_re, c_im, dvec, glu_w, glu_b, w_out)


def _s5_bwd_ends(dx1, y, y1, zg, u, pv, c_re, c_im, ar, ai, dvec, glu_w, w_out, layer, tm, after=None):
    L = dx1.shape[0]
    nt = L // tm

    def body(dx1_ref, y_ref, y1_ref, zg_ref, u_ref, pv_ref, cre_ref, cim_ref, ar_ref, ai_ref, d_ref, gw_ref, wout_ref,
             dy_ref, y2_ref, dzg_ref, dy1_ref, dus_ref, ere_ref, eim_ref, vs_ref, g_re, g_im):
        @pl.when(pl.program_id(0) == 0)
        def _():
            vs_ref[...] = jnp.zeros_like(vs_ref)
            ere_ref[...] = jnp.zeros_like(ere_ref)
            eim_ref[...] = jnp.zeros_like(eim_ref)

        dx1v, p = dx1_ref[...], pv_ref[...]
        dyb = _bf(dx1v * p[R_G1:R_G1 + 1])
        dy_ref[...] = dyb
        vs_ref[0:1, :] += _sum0(dx1v * y_ref[...])
        dy3 = _dot_nt(dyb, wout_ref[...])
        y2, gelu_vjp = jax.vjp(jax.nn.gelu, y1_ref[...])
        y2_ref[...] = _bf(y2)
        gate = jax.nn.sigmoid(zg_ref[...])
        dzg = dy3 * y2 * gate * (1.0 - gate)
        dzgb = _bf(dzg)
        dzg_ref[...] = dzgb
        vs_ref[1:2, :] += _sum0(dzg)
        dy2 = dy3 * gate + _dot_nt(dzgb, gw_ref[...])
        dy1 = gelu_vjp(dy2)[0]
        vs_ref[2:3, :] += _sum0(dy1 * u_ref[...])
        dus_ref[...] = dy1 * d_ref[...]
        dy1b = _bf(dy1)
        dy1_ref[...] = dy1b
        for k in range(S5_NB):
            dk = dy1b[:, k * S5_BH:(k + 1) * S5_BH]
            g_re[:, k * S5_BP:(k + 1) * S5_BP] = _dot_nt(dk, cre_ref[k])
            g_im[:, k * S5_BP:(k + 1) * S5_BP] = -_dot_nt(dk, cim_ref[k])
        _segment_scan(g_re, g_im, ere_ref, eim_ref, ar_ref[...], ai_ref[...], tm // 8, adjoint=True, write=False)

    call, tail = _call_after(
        after, body, 13, name="s5_bwd_ends", grid=(nt,),
        in_specs=[_rows(tm, D, nt)] * 5 + [pl.BlockSpec((None, 8, D), lambda i: (layer, 0, 0)),
                  _const_w((S5_NB, S5_BP, S5_BH)), _const_w((S5_NB, S5_BP, S5_BH)), _whole((1, NSTATE)), _whole((1, NSTATE)),
                  _whole((1, D)), _layer_w(D, D, 0), _layer_w(D, D, 0)],
        out_specs=[_rows(tm, D, nt)] * 5 + [_whole((NSEG, NSTATE)), _whole((NSEG, NSTATE)), _whole((8, D))],
        out_shape=[jax.ShapeDtypeStruct((L, D), BF16)] * 4 + [jax.ShapeDtypeStruct((L, D), F32),
                   jax.ShapeDtypeStruct((NSEG, NSTATE), F32), jax.ShapeDtypeStruct((NSEG, NSTATE), F32),
                   jax.ShapeDtypeStruct((8, D), F32)],
        scratch_shapes=[pltpu.VMEM((tm, NSTATE), F32), pltpu.VMEM((tm, NSTATE), F32)],
        compiler_params=_params(1, 56),
    )
    return call(dx1, y, y1, zg, u, pv, c_re, c_im, ar, ai, dvec, glu_w, w_out, *tail)


def _s5_bwd_in(dx1, dy1_b, du_skip, x, s_re, s_im, pv, b_re, b_im, c_re, c_im, l0_re, l0_im, ar, ai, w_in, layer, tm):
    L = x.shape[0]
    nt = L // tm

    def body(dx1_ref, dy1_ref, dus_ref, x_ref, sre_ref, sim_ref, hre_ref, him_ref, lre_ref, lim_ref, pv_ref, bre_ref, bim_ref,
             cre_ref, cim_ref, l0re_ref, l0im_ref, ar_ref, ai_ref, win_ref,
             dx_ref, du_ref, lamre_ref, lamim_ref, da_ref, vs_ref, g_re, g_im, st_re, st_im):
        gi = pl.program_id(0)
        tile = nt - 1 - gi

        @pl.when(gi == 0)
        def _():
            vs_ref[...] = jnp.zeros_like(vs_ref)
            da_ref[...] = jnp.zeros_like(da_ref)
            st_re[...] = l0re_ref[...]
            st_im[...] = l0im_ref[...]

        p = pv_ref[...]
        dy1b = dy1_ref[...]
        for k in range(S5_NB):
            dk = dy1b[:, k * S5_BH:(k + 1) * S5_BH]
            g_re[:, k * S5_BP:(k + 1) * S5_BP] = _dot_nt(dk, cre_ref[k])
            g_im[:, k * S5_BP:(k + 1) * S5_BP] = -_dot_nt(dk, cim_ref[k])
        _segment_scan(g_re, g_im, st_re, st_im, ar_ref[...], ai_ref[...], tm // 8, adjoint=True, write=True)
        lam_r, lam_i = g_re[...], g_im[...]
        lrb, lib = _bf(lam_r), _bf(lam_i)
        lamre_ref[...] = lrb
        lamim_ref[...] = lib

        def wrapped(last_ref):
            z = last_ref[...]
            row = lax.broadcasted_iota(jnp.int32, z.shape, 0)
            return jnp.where(row >= 1, pltpu.roll(z, 1, 0), 0.0)

        first_r = jnp.where(tile > 0, hre_ref[...], wrapped(lre_ref))
        first_i = jnp.where(tile > 0, him_ref[...], wrapped(lim_ref))
        sp_r = jnp.concatenate([first_r, sre_ref[0:tm - 8, :]], axis=0)
        sp_i = jnp.concatenate([first_i, sim_ref[0:tm - 8, :]], axis=0)
        da_ref[0:1, :] += _sum0(lam_r * sp_r + lam_i * sp_i)
        da_ref[1:2, :] += _sum0(lam_i * sp_r - lam_r * sp_i)

        parts = []
        for k in range(S5_NB):
            sl = slice(k * S5_BP, (k + 1) * S5_BP)
            parts.append(_dot_nt(lrb[:, sl], bre_ref[k]) + _dot_nt(lib[:, sl], bim_ref[k]))
        dub = _bf(jnp.concatenate(parts, axis=1) + dus_ref[...])
        du_ref[...] = dub
        dh = _dot_nt(dub, win_ref[...])
        _, xn, r = _norm_mod(x_ref[...], p[R_N1:R_N1 + 1], p[R_SC1:R_SC1 + 1], p[R_SH1:R_SH1 + 1])
        dx_ref[...] = dx1_ref[...] + _norm_mod_bwd(dh, xn, r, p[R_N1:R_N1 + 1], p[R_SC1:R_SC1 + 1])
        vs_ref[1:2, :] += _sum0(dh * xn)
        vs_ref[2:3, :] += _sum0(dh)

    halo = pl.BlockSpec((8, NSTATE), lambda i: (jnp.maximum((nt - 1 - i) * (tm // 8) - 1, 0), 0))
    last = pl.BlockSpec((8, NSTATE), lambda i: (L // 8 - 1, 0))
    return pl.pallas_call(
        body, name="s5_bwd_in", grid=(nt,),
        in_specs=[_rows(tm, D, nt), _rows(tm, D, nt), _rows(tm, D, nt), _rows(tm, D, nt), _rows(tm, NSTATE, nt), _rows(tm, NSTATE, nt),
                  halo, halo, last, last, pl.BlockSpec((None, 8, D), lambda i: (layer, 0, 0)),
                  _const_w((S5_NB, S5_BH, S5_BP)), _const_w((S5_NB, S5_BH, S5_BP)),
                  _const_w((S5_NB, S5_BP, S5_BH)), _const_w((S5_NB, S5_BP, S5_BH)),
                  _whole((NSEG, NSTATE)), _whole((NSEG, NSTATE)), _whole((1, NSTATE)), _whole((1, NSTATE)), _layer_w(D, D, 0)],
        out_specs=[_rows(tm, D, nt), _rows(tm, D, nt), _rows(tm, NSTATE, nt), _rows(tm, NSTATE, nt), _whole((8, NSTATE)), _whole((8, D))],
        out_shape=[jax.ShapeDtypeStruct((L, D), F32), jax.ShapeDtypeStruct((L, D), BF16),
                   jax.ShapeDtypeStruct((L, NSTATE), BF16), jax.ShapeDtypeStruct((L, NSTATE), BF16),
                   jax.ShapeDtypeStruct((8, NSTATE), F32), jax.ShapeDtypeStruct((8, D), F32)],
        scratch_shapes=[pltpu.VMEM((tm, NSTATE), F32), pltpu.VMEM((tm, NSTATE), F32),
                        pltpu.VMEM((NSEG, NSTATE), F32), pltpu.VMEM((NSEG, NSTATE), F32)],
        compiler_params=_params(1, 60),
    )(dx1, dy1_b, du_skip, x, s_re, s_im, s_re, s_im, s_re, s_im, pv, b_re, b_im, c_re, c_im, l0_re, l0_im, ar, ai, w_in)


def _blockdiag_b(bt):
    b = bt.reshape(S5_H, S5_NB, 16, S5_P).transpose(1, 2, 0, 3)
    eye = jnp.eye(16, dtype=bt.dtype)
    return (b[:, :, :, None, :] * eye[None, :, None, :, None]).reshape(S5_NB, S5_BH, S5_BP)


def _unblock_b(d):
    d = jnp.einsum("bghgp->bghp", d.reshape(S5_NB, 16, S5_H, 16, S5_P))
    return d.transpose(2, 0, 1, 3).reshape(S5_H, S5_G, S5_P)


def _blockdiag_c(cm):
    c4 = cm.reshape(S5_NB, 16, S5_H, S5_P)
    eye = jnp.eye(16, dtype=cm.dtype)
    out = c4.transpose(0, 1, 3, 2)[:, :, :, None, :] * eye[None, :, None, :, None]
    return out.reshape(S5_NB, S5_BP, S5_BH)


def _unblock_c(d):
    d = jnp.einsum("bgpgh->bghp", d.reshape(S5_NB, 16, S5_P, 16, S5_H))
    return d.reshape(S5_G, S5_H, S5_P)


def _tril_mask():
    return lax.broadcasted_iota(jnp.int32, (SG_CHUNK, SG_CHUNK), 0) >= lax.broadcasted_iota(jnp.int32, (SG_CHUNK, SG_CHUNK), 1)


def _sg_fwd(x, pv, w_in, w_s, b_t, vg, w_out, layer, tm, after=None):
    L = x.shape[0]
    nc = tm // SG_CHUNK

    def body(x_ref, pv_ref, win_ref, ws_ref, bt_ref, vg_ref, wout_ref, x1_ref, h_ref, uv_ref, vm_ref, q_ref, y_ref):
        xv, p = x_ref[...], pv_ref[...]
        h, _, _ = _norm_mod(xv, p[R_N1:R_N1 + 1], p[R_SC1:R_SC1 + 1], p[R_SH1:R_SH1 + 1])
        hb = _bf(h)
        h_ref[...] = hb
        uv = _dot(hb, win_ref[...])
        uv_ref[...] = uv
        v = uv[:, D:]
        rv = lax.rsqrt(jnp.mean(v * v, axis=-1, keepdims=True) + EPS)
        vnb = _bf((v * rv) * vg_ref[...])
        mask = _tril_mask()
        bt = bt_ref[...]
        for hd in range(SG_HEADS):
            wm = _bf(jnp.where(mask, ws_ref[hd], 0.0))
            cs = slice(hd * SG_CHUNK, (hd + 1) * SG_CHUNK)
            for ck in range(nc):
                rs = slice(ck * SG_CHUNK, (ck + 1) * SG_CHUNK)
                vm_ref[rs, cs] = _dot(wm, vnb[rs, cs]) + bt[:, hd:hd + 1]
        qb = _bf(uv[:, :D] * vm_ref[...])
        q_ref[...] = qb
        y = _dot(qb, wout_ref[...])
        y_ref[...] = y
        x1_ref[...] = xv + p[R_G1:R_G1 + 1] * y

    call, tail = _call_after(
        after, body, 7, name="sg_fwd", grid=(L // tm,),
        in_specs=[_rows(tm, D), pl.BlockSpec((None, 8, D), lambda i: (layer, 0, 0)), _layer_w(D, 2 * D, 0),
                  _whole((SG_HEADS, SG_CHUNK, SG_CHUNK)), _whole((SG_CHUNK, SG_HEADS)), _whole((1, D)), _layer_w(D, D, 0)],
        out_specs=[_rows(tm, D), _rows(tm, D), _rows(tm, 2 * D), _rows(tm, D), _rows(tm, D), _rows(tm, D)],
        out_shape=[jax.ShapeDtypeStruct((L, D), F32), jax.ShapeDtypeStruct((L, D), BF16), jax.ShapeDtypeStruct((L, 2 * D), F32),
                   jax.ShapeDtypeStruct((L, D), F32), jax.ShapeDtypeStruct((L, D), BF16), jax.ShapeDtypeStruct((L, D), F32)],
        compiler_params=_params(1, 56),
    )
    return call(x, pv, w_in, w_s, b_t, vg, w_out, *tail)


def _sg_bwd(dx1, x, y, uv, vm, pv, w_in, w_s, vg, w_out, layer, tm, after=None):
    L = x.shape[0]
    nc = tm // SG_CHUNK

    def body(dx1_ref, x_ref, y_ref, uv_ref, vm_ref, pv_ref, win_ref, ws_ref, vg_ref, wout_ref,
             dx_ref, duv_ref, dy_ref, vs_ref, dws_ref, dbt_ref, dvn_scr):
        @pl.when(pl.program_id(0) == 0)
        def _():
            vs_ref[...] = jnp.zeros_like(vs_ref)
            dws_ref[...] = jnp.zeros_like(dws_ref)
            dbt_ref[...] = jnp.zeros_like(dbt_ref)

        dx1v, p = dx1_ref[...], pv_ref[...]
        dyb = _bf(dx1v * p[R_G1:R_G1 + 1])
        dy_ref[...] = dyb
        vs_ref[0:1, :] += _sum0(dx1v * y_ref[...])
        dq = _dot_nt(dyb, wout_ref[...])
        uv = uv_ref[...]
        u, v = uv[:, :D], uv[:, D:]
        dub = _bf(dq * vm_ref[...])
        dvm = dq * u
        dvmb = _bf(dvm)
        rv = lax.rsqrt(jnp.mean(v * v, axis=-1, keepdims=True) + EPS)
        vh = v * rv
        vgv = vg_ref[...]
        vnb = _bf(vh * vgv)
        mask = _tril_mask()
        for hd in range(SG_HEADS):
            wm = _bf(jnp.where(mask, ws_ref[hd], 0.0))
            cs = slice(hd * SG_CHUNK, (hd + 1) * SG_CHUNK)
            dws = jnp.zeros((SG_CHUNK, SG_CHUNK), F32)
            dbs = jnp.zeros((SG_CHUNK, 1), F32)
            for ck in range(nc):
                rs = slice(ck * SG_CHUNK, (ck + 1) * SG_CHUNK)
                dvn_scr[rs, cs] = _dot_tn(wm, dvmb[rs, cs])
                dws = dws + _dot_nt(dvmb[rs, cs], vnb[rs, cs])
                dbs = dbs + jnp.sum(dvm[rs, cs], axis=1, keepdims=True)
            dws_ref[hd] += jnp.where(mask, dws, 0.0)
            dbt_ref[:, hd:hd + 1] += dbs
        dvn = dvn_scr[...]
        vs_ref[3:4, :] += _sum0(dvn * vh)
        dvnn = dvn * vgv
        dvb = _bf(rv * (dvnn - vh * jnp.mean(dvnn * vh, axis=-1, keepdims=True)))
        duv_ref[:, 0:D] = dub
        duv_ref[:, D:2 * D] = dvb
        dh = _dot_nt(dub, win_ref[:, 0:D]) + _dot_nt(dvb, win_ref[:, D:2 * D])
        _, xn, r = _norm_mod(x_ref[...], p[R_N1:R_N1 + 1], p[R_SC1:R_SC1 + 1], p[R_SH1:R_SH1 + 1])
        dx_ref[...] = dx1v + _norm_mod_bwd(dh, xn, r, p[R_N1:R_N1 + 1], p[R_SC1:R_SC1 + 1])
        vs_ref[1:2, :] += _sum0(dh * xn)
        vs_ref[2:3, :] += _sum0(dh)

    call, tail = _call_after(
        after, body, 10, name="sg_bwd", grid=(L // tm,),
        in_specs=[_rows(tm, D), _rows(tm, D), _rows(tm, D), _rows(tm, 2 * D), _rows(tm, D),
                  pl.BlockSpec((None, 8, D), lambda i: (layer, 0, 0)), _layer_w(D, 2 * D, 0),
                  _whole((SG_HEADS, SG_CHUNK, SG_CHUNK)), _whole((1, D)), _layer_w(D, D, 0)],
        out_specs=[_rows(tm, D), _rows(tm, 2 * D), _rows(tm, D), _whole((8, D)),
                   _whole((SG_HEADS, SG_CHUNK, SG_CHUNK)), _whole((SG_CHUNK, SG_HEADS))],
        out_shape=[jax.ShapeDtypeStruct((L, D), F32), jax.ShapeDtypeStruct((L, 2 * D), BF16), jax.ShapeDtypeStruct((L, D), BF16),
                   jax.ShapeDtypeStruct((8, D), F32), jax.ShapeDtypeStruct((SG_HEADS, SG_CHUNK, SG_CHUNK), F32),
                   jax.ShapeDtypeStruct((SG_CHUNK, SG_HEADS), F32)],
        scratch_shapes=[pltpu.VMEM((tm, D), F32)],
        compiler_params=_params(1, 56),
    )
    return call(dx1, x, y, uv, vm, pv, w_in, w_s, vg, w_out, *tail)


def _final(x, target, fg, tm):
    L = x.shape[0]

    def body(x_ref, t_ref, g_ref, dx_ref, vs_ref):
        @pl.when(pl.program_id(0) == 0)
        def _():
            vs_ref[...] = jnp.zeros_like(vs_ref)

        xv, g = x_ref[...], g_ref[...]
        r = lax.rsqrt(jnp.mean(xv * xv, axis=-1, keepdims=True) + EPS)
        xn = xv * r
        e = xn * g - t_ref[...]
        vs_ref[0:1, :] += jnp.sum(e * e)
        dout = e * (1.0 / D)
        vs_ref[1:2, :] += _sum0(dout * xn)
        dxn = dout * g
        dx_ref[...] = r * (dxn - xn * jnp.mean(dxn * xn, axis=-1, keepdims=True))

    return pl.pallas_call(
        body, name="final_loss", grid=(L // tm,),
        in_specs=[_rows(tm, D), _rows(tm, D), _whole((1, D))],
        out_specs=[_rows(tm, D), _whole((8, D))],
        out_shape=[jax.ShapeDtypeStruct((L, D), F32), jax.ShapeDtypeStruct((8, D), F32)],
        compiler_params=_params(1),
    )(x, target, fg)


def _pack(arrs, row_multiple=8):
    flat = jnp.concatenate([a.reshape(-1).astype(F32) for a in arrs])
    rows = -(-flat.shape[0] // LANES)
    rows = -(-rows // row_multiple) * row_multiple
    return jnp.pad(flat, (0, rows * LANES - flat.shape[0])).reshape(rows, LANES)


def _unpack(buf, shapes, lead=()):
    flat = buf.reshape(lead + (-1,))
    out, off = [], 0
    for s in shapes:
        n = 1
        for d in s:
            n *= d
        out.append(flat[..., off:off + n].reshape(lead + tuple(s)))
        off += n
    return out


BIG = ("ff_w1", "ff_w2", "conv_w_in", "conv_w_out", "ssm_w_in", "ssm_glu_w", "ssm_w_out", "sg_w_in", "sg_w_out")
BIG_AXIS = {"ff_w1": 2, "ff_w2": 1, "conv_w_in": 2, "conv_w_out": 1, "ssm_w_in": 1, "ssm_glu_w": 1, "ssm_w_out": 1,
            "sg_w_in": 2, "sg_w_out": 1}
LAYER_WEIGHTS = (
    (("conv_w_in", 0), ("conv_w_out", 0), ("ff_w1", 0), ("ff_w2", 0)),
    (("ssm_w_in", 0), ("ssm_glu_w", 0), ("ssm_w_out", 0), ("ff_w1", 1), ("ff_w2", 1)),
    (("sg_w_in", 0), ("sg_w_out", 0), ("ff_w1", 2), ("ff_w2", 2)),
    (("conv_w_in", 1), ("conv_w_out", 1), ("ff_w1", 3), ("ff_w2", 3)),
)
SMALL = ("ada_b", "norm1_g", "norm2_g", "final_g", "conv_w", "conv_b", "ssm_a_re", "ssm_a_im", "ssm_log_dt", "ssm_b_re",
         "ssm_b_im", "ssm_c_re", "ssm_c_im", "ssm_d", "ssm_glu_b", "sg_v_g", "sg_w_s", "sg_b_s")
WEIGHTS = ("ada_w", "ada_b", "norm1_g", "norm2_g", "ff_w1", "ff_w2", "final_g", "conv_w_in", "conv_w", "conv_b", "conv_w_out",
           "ssm_w_in", "ssm_a_re", "ssm_a_im", "ssm_log_dt", "ssm_b_re", "ssm_b_im", "ssm_c_re", "ssm_c_im", "ssm_d",
           "ssm_glu_w", "ssm_glu_b", "ssm_w_out", "sg_w_in", "sg_v_g", "sg_w_s", "sg_b_s", "sg_w_out")


def kernel(x, c, ada_w, ada_b, norm1_g, norm2_g, ff_w1, ff_w2, final_g, conv_w_in, conv_w, conv_b, conv_w_out, ssm_w_in, ssm_a_re, ssm_a_im, ssm_log_dt, ssm_b_re, ssm_b_im, ssm_c_re, ssm_c_im, ssm_d, ssm_glu_w, ssm_glu_b, ssm_w_out, sg_w_in, sg_v_g, sg_w_s, sg_b_s, sg_w_out, loss_target, m_ada_w, m_ada_b, m_norm1_g, m_norm2_g, m_ff_w1, m_ff_w2, m_final_g, m_conv_w_in, m_conv_w, m_conv_b, m_conv_w_out, m_ssm_w_in, m_ssm_a_re, m_ssm_a_im, m_ssm_log_dt, m_ssm_b_re, m_ssm_b_im, m_ssm_c_re, m_ssm_c_im, m_ssm_d, m_ssm_glu_w, m_ssm_glu_b, m_ssm_w_out, m_sg_w_in, m_sg_v_g, m_sg_w_s, m_sg_b_s, m_sg_w_out, v_ada_w, v_ada_b, v_norm1_g, v_norm2_g, v_ff_w1, v_ff_w2, v_final_g, v_conv_w_in, v_conv_w, v_conv_b, v_conv_w_out, v_ssm_w_in, v_ssm_a_re, v_ssm_a_im, v_ssm_log_dt, v_ssm_b_re, v_ssm_b_im, v_ssm_c_re, v_ssm_c_im, v_ssm_d, v_ssm_glu_w, v_ssm_glu_b, v_ssm_w_out, v_sg_w_in, v_sg_v_g, v_sg_w_s, v_sg_b_s, v_sg_w_out):
    args = dict(locals())
    w = {n: args[n] for n in WEIGHTS}
    m = {n: args["m_" + n] for n in WEIGHTS}
    v = {n: args["v_" + n] for n in WEIGHTS}
    L = x.shape[1]
    tm = min(L, 256)
    tr = min(L, 512)
    chip = 2 * lax.axis_index("x") + lax.axis_index("y")
    me = 2 * chip + lax.axis_index("c")
    xin = x[0]
    target = loss_target[0]
    chip1 = chip.reshape(1).astype(jnp.int32)

    gathers = []
    def start_gather(i, after):
        axes = [BIG_AXIS[n] for n, _ in LAYER_WEIGHTS[i]]
        lands = [_cast_place(w[n], li, BIG_AXIS[n], chip1, f"cast_{n}_{li}") for n, li in LAYER_WEIGHTS[i]]
        s_sems, r_sems, lands, token = _gather_start(lands, axes, f"gather_start{i}", after)
        gathers.append((s_sems, r_sems, lands, axes))
        return token

    def weights_of(i, after):
        s_sems, r_sems, lands, axes = gathers[i]
        lands = _gather_wait(s_sems, r_sems, lands, axes, f"gather_wait{i}", after)
        return dict(zip([n for n, _ in LAYER_WEIGHTS[i]], _gather_share(lands, axes, f"gather_share{i}")))

    small_in = _pack([c, conv_w, conv_b, sg_v_g])
    got = _allgather_small(small_in, "gather_small_inputs").reshape(N_DEV, -1)
    c_all, cw_sh, cb_sh, vg_sh = _unpack(got, [(D,), conv_w.shape, conv_b.shape, sg_v_g.shape], lead=(N_DEV,))
    conv_w_full = jnp.concatenate([cw_sh[2 * k] for k in range(4)], axis=-1)
    conv_b_full = jnp.concatenate([cb_sh[2 * k] for k in range(4)], axis=-1)
    vg_full = jnp.concatenate([vg_sh[2 * k] for k in range(4)], axis=-1)
    c16 = jnp.pad(c_all, ((0, 16 - N_DEV), (0, 0)))

    cols = ada_w.shape[2]
    ada_b_cols = lax.dynamic_slice_in_dim(ada_b, chip * cols, cols, axis=1)[:, None, :]
    mod_sh = _ada_fwd(c16, ada_w, ada_b_cols)[:, :N_DEV, :]
    mod_all = _allgather_small(_pack([mod_sh]), "gather_mod").reshape(N_DEV, -1)
    mod_all = _unpack(mod_all, [mod_sh.shape], lead=(N_DEV,))[0]
    mod_mine = lax.dynamic_index_in_dim(mod_all[0::2], me, axis=2, keepdims=False)
    mod_mine = mod_mine.transpose(1, 0, 2).reshape(DEPTH, 6, D)
    pv = jnp.concatenate([mod_mine, norm1_g[:, None, :], norm2_g[:, None, :]], axis=1)

    start_gather(0, pv)

    cw_rows = jnp.concatenate([conv_w_full, conv_b_full[:, None, :], jnp.zeros((conv_w_full.shape[0], 4, D), F32)], axis=1)

    a_re, a_im = ssm_a_re[0], ssm_a_im[0]
    log_dt = ssm_log_dt[0][:, None]
    bt_re, bt_im = ssm_b_re[0].transpose(2, 0, 1), ssm_b_im[0].transpose(2, 0, 1)
    abar_re, abar_im, bbar_re, bbar_im = _s5_params_fwd(a_re, a_im, log_dt, bt_re, bt_im)
    ar_vec, ai_vec = abar_re.reshape(1, NSTATE), abar_im.reshape(1, NSTATE)
    bd_re, bd_im = _bf(_blockdiag_b(bbar_re)), _bf(_blockdiag_b(bbar_im))
    cd_re, cd_im = _bf(_blockdiag_c(ssm_c_re[0])), _bf(_blockdiag_c(ssm_c_im[0]))

    saved = []
    fulls = []
    xl = xin
    for i in range(DEPTH):
        kind = MIXER_OF_LAYER[i]
        j = i // 3
        full = weights_of(i, cd_im if i == 0 else xl)
        fulls.append(full)
        tok = start_gather(i + 1, full["ff_w2"]) if i + 1 < DEPTH else None
        if kind == 0:
            x1, h, bcx, conv, q, y = _conv_fwd(xl, pv, full["conv_w_in"], full["conv_w_out"], cw_rows, i, j, tm, after=tok)
            mix = dict(h=h, bcx=bcx, conv=conv, q=q, y=y)
        elif kind == 1:
            xp = _to_segments(xl)
            h, u, e_re, e_im = _s5_fwd_ends(xp, pv, full["ssm_w_in"], bd_re, bd_im, ar_vec, ai_vec, i, tm, after=tok)
            s0_re, s0_im = _s5_segment_states(e_re, e_im, ar_vec, ai_vec, L // NSEG, adjoint=False)
            x1p, s_re, s_im, y1, zg, y3, y = _s5_fwd_out(xp, u, pv, bd_re, bd_im, s0_re, s0_im, ar_vec, ai_vec, cd_re, cd_im,
                                                         ssm_d, full["ssm_glu_w"], ssm_glu_b, full["ssm_w_out"], i, tm)
            x1 = _from_segments(x1p)
            mix = dict(xp=xp, h=h, u=u, s_re=s_re, s_im=s_im, y1=y1, zg=zg, y3=y3, y=y)
        else:
            x1, h, uv, vm, q, y = _sg_fwd(xl, pv, full["sg_w_in"], sg_w_s[0], sg_b_s[0].T, vg_full, full["sg_w_out"], i, tm,
                                          after=tok)
            mix = dict(h=h, uv=uv, vm=vm, q=q, y=y)
        x2, h2, a, f = _ffn_fwd(x1, pv, full["ff_w1"], full["ff_w2"], i, tm)
        saved.append(dict(x=xl, x1=x1, h2=h2, a=a, f=f, **mix))
        xl = x2

    dxl, vs_fin = _final(xl, target, final_g[None, :], tm)

    gfull = {n: [None] * w[n].shape[0] for n in BIG}
    vs_mix, vs_ffn = [None] * DEPTH, [None] * DEPTH
    small_g = {}
    scatters = {}
    token = None

    def start_scatter(key, entries, after):
        garrs = [gfull[n][li][None] for n, li in entries]
        gaxes = [BIG_AXIS[n] for n, _ in entries]
        s_sems, r_sems, garrs, lands, tok = _scatter_start(garrs, gaxes, f"scatter_start{key}", after)
        scatters[key] = (s_sems, r_sems, garrs, lands, gaxes, entries)
        return tok

    for i in reversed(range(DEPTH)):
        kind = MIXER_OF_LAYER[i]
        j = i // 3
        sv = saved[i]
        full = fulls[i]
        dx1, p_b, da_b, df_b, vs_ffn[i] = _ffn_bwd(dxl, sv["x1"], sv["a"], sv["f"], pv, full["ff_w1"], full["ff_w2"], i, tm,
                                                   after=token)
        gfull["ff_w1"][i] = _mm_tn(sv["h2"], da_b, f"wgrad_ff_w1_{i}")
        gfull["ff_w2"][i] = _mm_tn(p_b, df_b, f"wgrad_ff_w2_{i}")
        if i == 0:
            token = start_scatter("0f", LAYER_WEIGHTS[0][2:], dx1)
        if kind == 0:
            dxl, dbcx_b, dy_b, vsm = _conv_bwd(dx1, sv["x"], sv["y"], sv["bcx"], sv["conv"], pv, full["conv_w_in"],
                                               full["conv_w_out"], cw_rows, i, j, tm, after=token if i == 0 else None)
            gfull["conv_w_in"][j] = _mm_tn(sv["h"], dbcx_b, f"wgrad_conv_w_in_{j}")
            gfull["conv_w_out"][j] = _mm_tn(sv["q"], dy_b, f"wgrad_conv_w_out_{j}")
            small_g.setdefault("conv_w", [None, None])[j] = vsm[3:6]
            small_g.setdefault("conv_b", [None, None])[j] = vsm[6]
        elif kind == 1:
            dx1p = _to_segments(dx1)
            dy_b, y2_b, dzg_b, dy1_b, du_skip, eb_re, eb_im, vsm = _s5_bwd_ends(
                dx1p, sv["y"], sv["y1"], sv["zg"], sv["u"], pv, cd_re, cd_im, ar_vec, ai_vec, ssm_d, full["ssm_glu_w"],
                full["ssm_w_out"], i, tm)
            l0_re, l0_im = _s5_segment_states(eb_re, eb_im, ar_vec, ai_vec, L // NSEG, adjoint=True)
            dxp, du_b, lam_re, lam_im, dabar, vs_in = _s5_bwd_in(
                dx1p, dy1_b, du_skip, sv["xp"], sv["s_re"], sv["s_im"], pv, bd_re, bd_im, cd_re, cd_im, l0_re, l0_im,
                ar_vec, ai_vec, full["ssm_w_in"], i, tm)
            dxl = _from_segments(dxp)
            gfull["ssm_w_out"][0] = _mm_tn(sv["y3"], dy_b, "wgrad_ssm_w_out")
            gfull["ssm_glu_w"][0] = _mm_tn(y2_b, dzg_b, "wgrad_ssm_glu_w")
            gfull["ssm_w_in"][0] = _mm_tn(sv["h"], du_b, "wgrad_ssm_w_in")
            d_cre = _unblock_c(_mm_tn_blocks(sv["s_re"], dy1_b, S5_BP, S5_BH, "wgrad_s5_c_re"))
            d_cim = -_unblock_c(_mm_tn_blocks(sv["s_im"], dy1_b, S5_BP, S5_BH, "wgrad_s5_c_im"))
            d_bbre = _unblock_b(_mm_tn_blocks(sv["u"], lam_re, S5_BH, S5_BP, "wgrad_s5_b_re"))
            d_bbim = _unblock_b(_mm_tn_blocks(sv["u"], lam_im, S5_BH, S5_BP, "wgrad_s5_b_im"))
            d_are, d_aim, d_ldt, d_btre, d_btim = _s5_params_bwd(
                a_re, a_im, log_dt, bt_re, bt_im, dabar[0].reshape(S5_G, S5_P), dabar[1].reshape(S5_G, S5_P), d_bbre, d_bbim)
            small_g.update(ssm_a_re=d_are, ssm_a_im=d_aim, ssm_log_dt=d_ldt, ssm_b_re=d_btre.transpose(1, 2, 0),
                           ssm_b_im=d_btim.transpose(1, 2, 0), ssm_c_re=d_cre, ssm_c_im=d_cim, ssm_d=vsm[2], ssm_glu_b=vsm[1])
            vsm = jnp.concatenate([vsm[0:1], vs_in[1:3], jnp.zeros((5, D), F32)], axis=0)
        else:
            dxl, duv_b, dy_b, vsm, d_ws, d_bt = _sg_bwd(dx1, sv["x"], sv["y"], sv["uv"], sv["vm"], pv, full["sg_w_in"],
                                                        sg_w_s[0], vg_full, full["sg_w_out"], i, tm)
            gfull["sg_w_in"][0] = _mm_tn(sv["h"], duv_b, "wgrad_sg_w_in")
            gfull["sg_w_out"][0] = _mm_tn(sv["q"], dy_b, "wgrad_sg_w_out")
            small_g.update(sg_v_g=vsm[3], sg_w_s=d_ws, sg_b_s=d_bt.T)
        vs_mix[i] = vsm
        token = start_scatter(str(i), LAYER_WEIGHTS[i], dxl) if i > 0 else start_scatter("0c", LAYER_WEIGHTS[0][:2], dxl)
    grad_x = dxl[None]

    sums = {n: [None] * w[n].shape[0] for n in BIG}

    def collect(key, after):
        s_sems, r_sems, garrs, lands, gaxes, entries = scatters[key]
        garrs, recv = _scatter_wait(s_sems, r_sems, garrs, lands, gaxes, f"scatter_wait{key}", after)
        for (n, li), g, r3, ax in zip(entries, garrs, recv, gaxes):
            sums[n][li] = _sum_parts(r3, g, ax, chip1, f"sum_{n}_{li}")
        return sums[entries[-1][0]][entries[-1][1]]

    after = token
    for key in ("3", "2", "1"):
        after = collect(key, after)
    early = [(n, li) for i in (3, 2, 1) for n, li in LAYER_WEIGHTS[i]]
    late = list(LAYER_WEIGHTS[0][2:]) + list(LAYER_WEIGHTS[0][:2])
    sib = dict(zip(early, _swap_with_sibling([sums[n][li] for n, li in early], "swap_grad_sums_early")))
    after = sib[early[-1]]
    for key in ("0f", "0c"):
        after = collect(key, after)
    sib.update(zip(late, _swap_with_sibling([sums[n][li] for n, li in late], "swap_grad_sums_late")))

    dmod = _mod_bwd(jnp.stack(vs_mix), jnp.stack(vs_ffn), pv)
    small_g.update(ada_b=dmod[:, :6, :], norm1_g=dmod[:, 6, :], norm2_g=dmod[:, 7, :], final_g=vs_fin[1],
                   conv_w=jnp.stack(small_g["conv_w"]), conv_b=jnp.stack(small_g["conv_b"]))

    loss_part = (0.5 / D) * vs_fin[0, 0:1]
    part_shapes = [(1,)] + [tuple(small_g[n].shape) for n in SMALL]
    parts_sum = _allreduce_small(_pack([loss_part] + [small_g[n] for n in SMALL], 16), "reduce_small_grads", sib[late[-1]])
    summed = _unpack(parts_sum, part_shapes)
    loss = summed[0][0]
    gsum = dict(zip(SMALL, summed[1:]))
    dmod_all = _allgather_small(_pack([small_g["ada_b"]]), "gather_dmod", parts_sum)
    dmod_all = dmod_all.reshape(N_DEV, DEPTH, 6 * D)
    dmod_cols = lax.dynamic_slice_in_dim(dmod_all, chip * cols, cols, axis=2).transpose(1, 0, 2)
    g_ada_w = _ada_bwd(c16, jnp.pad(dmod_cols, ((0, 0), (0, 16 - N_DEV), (0, 0))))

    res = {}
    shp = ada_w.shape
    two = lambda t: t.reshape(shp[0] * shp[1], shp[2])
    res["ada_w"] = [t.reshape(shp) for t in _adamw(two(ada_w), [two(g_ada_w)], two(m_ada_w), two(v_ada_w), "adamw_ada_w")]

    def mine(n, g):
        if n in ("conv_w", "conv_b", "sg_v_g"):
            size = w[n].shape[-1]
            return lax.dynamic_slice_in_dim(g, chip * size, size, axis=g.ndim - 1)
        return g

    g_loc = [mine(n, gsum[n]).reshape(w[n].shape) for n in SMALL]
    small_shapes = [tuple(w[n].shape) for n in SMALL]
    packed = _adamw(_pack([w[n] for n in SMALL]), [_pack(g_loc)], _pack([m[n] for n in SMALL]), _pack([v[n] for n in SMALL]),
                    "adamw_small")
    unpacked = [_unpack(t, small_shapes) for t in packed]
    for k, n in enumerate(SMALL):
        res[n] = [unpacked[0][k], unpacked[1][k], unpacked[2][k], unpacked[3][k]]

    q_mine = [jnp.stack(sums[n]) for n in BIG]
    q_sib = [jnp.stack([sib[(n, li)] for li in range(w[n].shape[0])]) for n in BIG]
    for n, qa, qb in zip(BIG, q_mine, q_sib):
        shp = w[n].shape
        two = lambda t, shp=shp: t.reshape(shp[0] * shp[1], shp[2])
        res[n] = [t.reshape(shp) for t in _adamw(two(w[n]), [two(qa), two(qb)], two(m[n]), two(v[n]), f"adamw_{n}")]

    outs = [loss, grad_x]
    for part in range(4):
        outs += [res[n][part] for n in WEIGHTS]
    return tuple(outs)
```

```python
import functools

import jax
import jax.numpy as jnp
from jax import lax
from jax.experimental import pallas as pl
from jax.experimental.pallas import tpu as pltpu

F32 = jnp.float32
BF16 = jnp.bfloat16
D = 1024
EPS = 1e-6
DEPTH = 4
MIXER_OF_LAYER = (0, 1, 2, 0)
S5_G, S5_H, S5_P = 64, 16, 64
S5_NB = 4
S5_BH = S5_H * 16
S5_BP = S5_P * 16
NSTATE = S5_G * S5_P
SG_HEADS, SG_CHUNK = 8, 128
ADAM_LR, ADAM_B1, ADAM_B2, ADAM_EPS, ADAM_WD, ADAM_STEP = 0.001, 0.9, 0.999, 1e-08, 0.01, 10
N_DEV = 8
MESH = pl.DeviceIdType.MESH
LANES = 1024
R_SH1, R_SC1, R_G1, R_SH2, R_SC2, R_G2, R_N1, R_N2 = range(8)


def _dot(a, b):
    return jnp.dot(a, b, preferred_element_type=F32)


def _dot_nt(a, b):
    return lax.dot_general(a, b, (((1,), (1,)), ((), ())), preferred_element_type=F32)


def _dot_tn(a, b):
    return lax.dot_general(a, b, (((0,), (0,)), ((), ())), preferred_element_type=F32)


def _bf(x):
    return x.astype(BF16)


def _sum0(x):
    return jnp.sum(x, axis=0, keepdims=True)


def _params(n_axes, vmem_mb=48):
    return pltpu.CompilerParams(dimension_semantics=("arbitrary",) * n_axes, vmem_limit_bytes=vmem_mb << 20)


def _rows(tm, cols, nt=None):
    if nt is None:
        return pl.BlockSpec((tm, cols), lambda i: (i, 0))
    return pl.BlockSpec((tm, cols), lambda i: (nt - 1 - i, 0))


def _whole(shape):
    nd = len(shape)
    return pl.BlockSpec(shape, lambda *_: (0,) * nd)


def _layer_w(r, c, layer):
    return pl.BlockSpec((None, r, c), lambda *_: (layer, 0, 0), pipeline_mode=pl.Buffered(1))


def _const_w(shape):
    nd = len(shape)
    return pl.BlockSpec(shape, lambda *_: (0,) * nd, pipeline_mode=pl.Buffered(1))


def _call_after(after, body, n_in, *, in_specs, **kw):
    if after is None:
        return pl.pallas_call(body, in_specs=in_specs, **kw), ()

    def body_after(*refs):
        return body(*refs[:n_in], *refs[n_in + 1:])

    return pl.pallas_call(body_after, in_specs=list(in_specs) + [pl.BlockSpec(memory_space=pl.ANY)], **kw), (after,)


def _norm_mod(x, ng, sc, sh):
    r = lax.rsqrt(jnp.mean(x * x, axis=-1, keepdims=True) + EPS)
    xn = x * r
    return (xn * ng) * (1.0 + sc) + sh, xn, r


def _norm_mod_bwd(dh, xn, r, ng, sc):
    dxn = dh * (ng * (1.0 + sc))
    return r * (dxn - xn * jnp.mean(dxn * xn, axis=-1, keepdims=True))


def _shift_down(z, prev8, k):
    row = lax.broadcasted_iota(jnp.int32, z.shape, 0)
    if k == 1:
        return jnp.where(row >= 1, pltpu.roll(z, 1, 0), prev8[7:8])
    return jnp.where(row >= 2, pltpu.roll(z, 2, 0), jnp.where(row == 0, prev8[6:7], prev8[7:8]))


def _shift_up(z, next8, k):
    n = z.shape[0]
    row = lax.broadcasted_iota(jnp.int32, z.shape, 0)
    if k == 1:
        return jnp.where(row <= n - 2, pltpu.roll(z, n - 1, 0), next8[0:1])
    return jnp.where(row <= n - 3, pltpu.roll(z, n - 2, 0), jnp.where(row == n - 2, next8[0:1], next8[1:2]))


def _place():
    x, y, c = lax.axis_index("x"), lax.axis_index("y"), lax.axis_index("c")
    chips = [(1 - x, y), (x, 1 - y), (1 - x, 1 - y)]
    return x, y, c, chips


def _allgather_small(x_shard, name, after=None):
    m_per, n = x_shard.shape

    def body(x_ref, out_ref, send_sems, recv_sems, local_sem):
        x, y, c, chips = _place()
        me, sibling = (x, y, c), (x, y, 1 - c)

        def rows(px, py, pc):
            return out_ref.at[pl.ds((4 * px + 2 * py + pc) * m_per, m_per), :]

        def copy(k, block, to, src=None):
            return pltpu.make_async_remote_copy(
                src_ref=rows(*block) if src is None else src, dst_ref=rows(*block),
                send_sem=send_sems.at[k], recv_sem=recv_sems.at[k], device_id=to, device_id_type=MESH)

        mine = pltpu.make_async_copy(x_ref, rows(*me), local_sem)
        mine.start()
        first = [copy(0, me, sibling, src=x_ref)]
        first += [copy(1 + j, me, (*chip, c), src=x_ref) for j, chip in enumerate(chips)]
        for cp in first:
            cp.start()
        passed = [copy(4 + j, (*chip, c), sibling) for j, chip in enumerate(chips)]
        for j, chip in enumerate(chips):
            copy(1 + j, (*chip, c), me).wait_recv()
            passed[j].start()
        copy(0, sibling, me).wait_recv()
        for j, chip in enumerate(chips):
            copy(4 + j, (*chip, 1 - c), me).wait_recv()
        for cp in first + passed:
            cp.wait_send()
        mine.wait()

    call, tail = _call_after(
        after, body, 1, name=name, out_shape=jax.ShapeDtypeStruct((N_DEV * m_per, n), F32),
        in_specs=[pl.BlockSpec(memory_space=pltpu.VMEM)], out_specs=pl.BlockSpec(memory_space=pltpu.VMEM),
        scratch_shapes=[pltpu.SemaphoreType.DMA((7,)), pltpu.SemaphoreType.DMA((7,)), pltpu.SemaphoreType.DMA],
        compiler_params=pltpu.CompilerParams(vmem_limit_bytes=48 << 20),
    )
    return call(x_shard, *tail)


def _allreduce_small(x_part, name, after=None):
    m, n = x_part.shape
    h = m // 2

    def body(x_ref, out_ref, sib_buf, slots, send_sems, recv_sems):
        x, y, c, chips = _place()
        k_me = 2 * x + y
        sibling = (x, y, 1 - c)
        mine = pl.ds(pl.multiple_of(c * h, 8), h)

        def copy(k, src, dst, to):
            return pltpu.make_async_remote_copy(src_ref=src, dst_ref=dst, send_sem=send_sems.at[k], recv_sem=recv_sems.at[k],
                                                device_id=to, device_id_type=MESH)

        swap = copy(0, x_ref, sib_buf, sibling)
        swap.start()
        swap.wait()
        slots[pl.ds(k_me, 1)] = (x_ref[mine, :] + sib_buf[mine, :])[None]
        my_slot = slots.at[pl.ds(k_me, 1)]
        sends = [copy(1 + j, my_slot, my_slot, (*chip, c)) for j, chip in enumerate(chips)]
        for cp in sends:
            cp.start()
        for j, chip in enumerate(chips):
            their_slot = slots.at[pl.ds(2 * chip[0] + chip[1], 1)]
            copy(1 + j, their_slot, their_slot, (x, y, c)).wait_recv()
        for cp in sends:
            cp.wait_send()
        out_ref[mine, :] = ((slots[0] + slots[1]) + slots[2]) + slots[3]
        give = copy(4, out_ref.at[mine, :], out_ref.at[mine, :], sibling)
        give.start()
        give.wait_send()
        theirs = pl.ds(pl.multiple_of((1 - c) * h, 8), h)
        copy(4, out_ref.at[theirs, :], out_ref.at[theirs, :], (x, y, c)).wait_recv()

    call, tail = _call_after(
        after, body, 1, name=name, out_shape=jax.ShapeDtypeStruct((m, n), F32),
        in_specs=[pl.BlockSpec(memory_space=pltpu.VMEM)], out_specs=pl.BlockSpec(memory_space=pltpu.VMEM),
        scratch_shapes=[pltpu.VMEM((m, n), F32), pltpu.VMEM((4, h, n), F32),
                        pltpu.SemaphoreType.DMA((5,)), pltpu.SemaphoreType.DMA((5,))],
        compiler_params=pltpu.CompilerParams(vmem_limit_bytes=48 << 20),
    )
    return call(x_part, *tail)


def _shard_region(ref, full_shape, axis, chip_k, half=None):
    _, r, c = full_shape
    if axis == 1:
        rs = r // 4
        if half is None:
            return ref.at[:, pl.ds(pl.multiple_of(chip_k * rs, 128), rs), :]
        return ref.at[:, pl.ds(pl.multiple_of(chip_k * rs + half * (rs // 2), 128), rs // 2), :]
    cs = c // 4
    if half is None:
        return ref.at[:, :, pl.ds(pl.multiple_of(chip_k * cs, 128), cs)]
    return ref.at[:, pl.ds(pl.multiple_of(half * (r // 2), 128), r // 2), pl.ds(pl.multiple_of(chip_k * cs, 128), cs)]


HBM_SPEC = pl.BlockSpec(memory_space=pltpu.HBM)
SEM_SPEC = pl.BlockSpec(memory_space=pltpu.SEMAPHORE)
ANY_SPEC = pl.BlockSpec(memory_space=pl.ANY)
SPLIT_COPY_PARAMS = pltpu.CompilerParams(has_side_effects=pltpu.SideEffectType.DATAFLOW_SIDE_EFFECTING)


def _in_hbm(arrs):
    return [pltpu.with_memory_space_constraint(a, pltpu.HBM) for a in arrs]


def _cast_place(w_stack, li, axis, chip, name):
    _, r, c = w_stack.shape
    full = (1, 4 * r, c) if axis == 1 else (1, r, 4 * c)
    tr = min(r, 256)
    if axis == 1:
        out_spec = pl.BlockSpec((None, tr, c), lambda i, k: (0, k[0] * (r // tr) + i, 0))
    else:
        out_spec = pl.BlockSpec((None, tr, c), lambda i, k: (0, i, k[0]))

    def body(k_ref, w_ref, o_ref):
        o_ref[...] = _bf(w_ref[...])

    return pl.pallas_call(
        body, name=name,
        grid_spec=pltpu.PrefetchScalarGridSpec(
            num_scalar_prefetch=1, grid=(r // tr,),
            in_specs=[pl.BlockSpec((None, tr, c), lambda i, k: (li, i, 0))], out_specs=out_spec),
        out_shape=jax.ShapeDtypeStruct(full, BF16),
        compiler_params=_params(1),
    )(chip, w_stack)


def _gather_start(lands, axes, name, after):
    n_arr = len(lands)
    fulls = [tuple(l.shape) for l in lands]

    def body(*refs):
        land = refs[:n_arr]
        send_sems, recv_sems = refs[n_arr + 1:n_arr + 3]
        token = refs[-1]
        x, y, c, chips = _place()
        k_me = 2 * x + y
        for a in range(n_arr):
            mine = _shard_region(land[a], fulls[a], axes[a], k_me, c)
            for j, chip in enumerate(chips):
                pltpu.make_async_remote_copy(
                    src_ref=mine, dst_ref=mine, send_sem=send_sems.at[a * 3 + j], recv_sem=recv_sems.at[a * 3 + j],
                    device_id=(*chip, c), device_id_type=MESH).start()
        token[...] = jnp.zeros_like(token)

    res = pl.pallas_call(
        body, name=name,
        out_shape=(pltpu.SemaphoreType.DMA((3 * n_arr,)), pltpu.SemaphoreType.DMA((3 * n_arr,)),
                   *[pltpu.HBM(f, BF16) for f in fulls], jax.ShapeDtypeStruct((8, 128), F32)),
        in_specs=[HBM_SPEC] * n_arr + [ANY_SPEC],
        out_specs=(SEM_SPEC, SEM_SPEC, *[HBM_SPEC] * n_arr, pl.BlockSpec(memory_space=pltpu.VMEM)),
        input_output_aliases={a: 2 + a for a in range(n_arr)},
        compiler_params=SPLIT_COPY_PARAMS,
    )(*_in_hbm(lands), after)
    return res[0], res[1], list(res[2:2 + n_arr]), res[-1]


def _gather_wait(send_sems, recv_sems, lands, axes, name, after):
    n_arr = len(lands)
    fulls = [tuple(l.shape) for l in lands]

    def body(*refs):
        land = refs[:n_arr]
        s_sems, r_sems = refs[n_arr:n_arr + 2]
        x, y, c, chips = _place()
        for a in range(n_arr):
            for j, chip in enumerate(chips):
                k_j = 2 * chip[0] + chip[1]
                got = _shard_region(land[a], fulls[a], axes[a], k_j, c)
                cp = pltpu.make_async_remote_copy(
                    src_ref=got, dst_ref=got, send_sem=s_sems.at[a * 3 + j], recv_sem=r_sems.at[a * 3 + j],
                    device_id=(x, y, c), device_id_type=MESH)
                cp.wait_send()
                cp.wait_recv()

    res = pl.pallas_call(
        body, name=name,
        out_shape=tuple(pltpu.HBM(f, BF16) for f in fulls),
        in_specs=[HBM_SPEC] * n_arr + [SEM_SPEC, SEM_SPEC, ANY_SPEC],
        out_specs=tuple([HBM_SPEC] * n_arr),
        input_output_aliases={a: a for a in range(n_arr)},
        compiler_params=SPLIT_COPY_PARAMS,
    )(*lands, send_sems, recv_sems, after)
    return list(res)


def _gather_share(lands, axes, name):
    n_arr = len(lands)
    fulls = [tuple(l.shape) for l in lands]

    def body(*refs):
        land_in, land = refs[:n_arr], refs[n_arr:2 * n_arr]
        send_sems, recv_sems = refs[2 * n_arr:]
        x, y, c, chips = _place()
        copies = []
        for a in range(n_arr):
            for j, chip in enumerate(chips):
                k_j = 2 * chip[0] + chip[1]
                cp = pltpu.make_async_remote_copy(
                    src_ref=_shard_region(land_in[a], fulls[a], axes[a], k_j, c),
                    dst_ref=_shard_region(land[a], fulls[a], axes[a], k_j, c),
                    send_sem=send_sems.at[a * 3 + j], recv_sem=recv_sems.at[a * 3 + j],
                    device_id=(x, y, 1 - c), device_id_type=MESH)
                cp.start()
                copies.append(cp)
        for cp in copies:
            cp.wait()

    return pl.pallas_call(
        body, name=name, out_shape=[jax.ShapeDtypeStruct(f, BF16) for f in fulls],
        in_specs=[ANY_SPEC] * n_arr, out_specs=[ANY_SPEC] * n_arr,
        input_output_aliases={a: a for a in range(n_arr)},
        scratch_shapes=[pltpu.SemaphoreType.DMA((3 * n_arr,)), pltpu.SemaphoreType.DMA((3 * n_arr,))],
    )(*lands)


def _scatter_shapes(grads, axes):
    out = []
    for g, ax in zip(grads, axes):
        shp = list(g.shape)
        shp[ax] //= 4
        out.append((3,) + tuple(shp[1:]))
    return out


def _scatter_start(grads, axes, name, after):
    n_arr = len(grads)
    shapes = _scatter_shapes(grads, axes)
    lands = [lax.empty(s, BF16) for s in shapes]

    def body(*refs):
        ins, land = refs[:n_arr], refs[n_arr:2 * n_arr]
        send_sems, recv_sems = refs[2 * n_arr + 1:2 * n_arr + 3]
        token = refs[-1]
        x, y, c, chips = _place()
        for a in range(n_arr):
            for j, chip in enumerate(chips):
                k_j = 2 * chip[0] + chip[1]
                pltpu.make_async_remote_copy(
                    src_ref=_shard_region(ins[a], grads[a].shape, axes[a], k_j), dst_ref=land[a].at[pl.ds(j, 1)],
                    send_sem=send_sems.at[a * 3 + j], recv_sem=recv_sems.at[a * 3 + j],
                    device_id=(*chip, c), device_id_type=MESH).start()
        token[...] = jnp.zeros_like(token)

    res = pl.pallas_call(
        body, name=name,
        out_shape=(pltpu.SemaphoreType.DMA((3 * n_arr,)), pltpu.SemaphoreType.DMA((3 * n_arr,)),
                   *[pltpu.HBM(g.shape, BF16) for g in grads], *[pltpu.HBM(s, BF16) for s in shapes],
                   jax.ShapeDtypeStruct((8, 128), F32)),
        in_specs=[HBM_SPEC] * (2 * n_arr) + [ANY_SPEC],
        out_specs=(SEM_SPEC, SEM_SPEC, *[HBM_SPEC] * (2 * n_arr), pl.BlockSpec(memory_space=pltpu.VMEM)),
        input_output_aliases={a: 2 + a for a in range(2 * n_arr)},
        compiler_params=SPLIT_COPY_PARAMS,
    )(*_in_hbm(grads), *_in_hbm(lands), after)
    return res[0], res[1], list(res[2:2 + n_arr]), list(res[2 + n_arr:2 + 2 * n_arr]), res[-1]


def _scatter_wait(send_sems, recv_sems, grads, lands, axes, name, after):
    n_arr = len(grads)

    def body(*refs):
        ins, land = refs[:n_arr], refs[n_arr:2 * n_arr]
        s_sems, r_sems = refs[2 * n_arr:2 * n_arr + 2]
        x, y, c, chips = _place()
        for a in range(n_arr):
            for j, chip in enumerate(chips):
                k_j = 2 * chip[0] + chip[1]
                cp = pltpu.make_async_remote_copy(
                    src_ref=_shard_region(ins[a], grads[a].shape, axes[a], k_j), dst_ref=land[a].at[pl.ds(j, 1)],
                    send_sem=s_sems.at[a * 3 + j], recv_sem=r_sems.at[a * 3 + j],
                    device_id=(x, y, c), device_id_type=MESH)
                cp.wait_send()
                cp.wait_recv()

    res = pl.pallas_call(
        body, name=name,
        out_shape=(*[pltpu.HBM(g.shape, BF16) for g in grads], *[pltpu.HBM(l.shape, BF16) for l in lands]),
        in_specs=[HBM_SPEC] * (2 * n_arr) + [SEM_SPEC, SEM_SPEC, ANY_SPEC],
        out_specs=tuple([HBM_SPEC] * (2 * n_arr)),
        input_output_aliases={a: a for a in range(2 * n_arr)},
        compiler_params=SPLIT_COPY_PARAMS,
    )(*grads, *lands, send_sems, recv_sems, after)
    return list(res[:n_arr]), list(res[n_arr:])


def _swap_with_sibling(arrs, name):
    n_arr = len(arrs)

    def body(*refs):
        ins, outs = refs[:n_arr], refs[n_arr:2 * n_arr]
        send_sems, recv_sems = refs[2 * n_arr:]
        x, y, c, _ = _place()
        copies = []
        for a in range(n_arr):
            cp = pltpu.make_async_remote_copy(
                src_ref=ins[a], dst_ref=outs[a], send_sem=send_sems.at[a], recv_sem=recv_sems.at[a],
                device_id=(x, y, 1 - c), device_id_type=MESH)
            cp.start()
            copies.append(cp)
        for cp in copies:
            cp.wait()

    any_spec = pl.BlockSpec(memory_space=pl.ANY)
    return pl.pallas_call(
        body, name=name, out_shape=[jax.ShapeDtypeStruct(a.shape, a.dtype) for a in arrs],
        in_specs=[any_spec] * n_arr, out_specs=[any_spec] * n_arr,
        scratch_shapes=[pltpu.SemaphoreType.DMA((n_arr,)), pltpu.SemaphoreType.DMA((n_arr,))],
    )(*arrs)


def _mm_tn(a, b, name, out_dtype=BF16):
    L, m = a.shape
    n = b.shape[1]
    bm, bn, bk = min(m, 1024), min(n, 1024), min(L, 2048)
    nk = L // bk

    def body(a_ref, b_ref, o_ref, acc):
        k = pl.program_id(2)

        @pl.when(k == 0)
        def _():
            acc[...] = jnp.zeros_like(acc)

        acc[...] += _dot_tn(_bf(a_ref[...]), _bf(b_ref[...]))

        @pl.when(k == nk - 1)
        def _():
            o_ref[...] = acc[...].astype(out_dtype)

    return pl.pallas_call(
        body, name=name, grid=(m // bm, n // bn, nk),
        in_specs=[pl.BlockSpec((bk, bm), lambda i, j, k: (k, i)), pl.BlockSpec((bk, bn), lambda i, j, k: (k, j))],
        out_specs=pl.BlockSpec((bm, bn), lambda i, j, k: (i, j)),
        out_shape=jax.ShapeDtypeStruct((m, n), out_dtype),
        scratch_shapes=[pltpu.VMEM((bm, bn), F32)],
        compiler_params=_params(3),
    )(a, b)


def _mm_tn_blocks(a, b, wa, wb, name):
    L = a.shape[0]
    nb = a.shape[1] // wa
    bk = min(L, 1024)
    nk = L // bk

    def body(a_ref, b_ref, o_ref):
        @pl.when(pl.program_id(1) == 0)
        def _():
            o_ref[...] = jnp.zeros_like(o_ref)

        o_ref[...] += _dot_tn(_bf(a_ref[...]), _bf(b_ref[...]))

    return pl.pallas_call(
        body, name=name, grid=(nb, nk),
        in_specs=[pl.BlockSpec((bk, wa), lambda j, k: (k, j)), pl.BlockSpec((bk, wb), lambda j, k: (k, j))],
        out_specs=pl.BlockSpec((None, wa, wb), lambda j, k: (j, 0, 0)),
        out_shape=jax.ShapeDtypeStruct((nb, wa, wb), F32),
        compiler_params=_params(2),
    )(a, b)


def _sum_parts(parts, own, axis, chip, name):
    _, r, c = parts.shape
    tr = min(r, 256)
    if axis == 1:
        own_spec = pl.BlockSpec((None, tr, c), lambda i, k: (0, k[0] * (r // tr) + i, 0))
    else:
        own_spec = pl.BlockSpec((None, tr, c), lambda i, k: (0, i, k[0]))

    def body(k_ref, p_ref, g_ref, o_ref):
        p = p_ref[...].astype(F32)
        o_ref[...] = ((p[0] + p[1]) + p[2]) + g_ref[...].astype(F32)

    return pl.pallas_call(
        body, name=name,
        grid_spec=pltpu.PrefetchScalarGridSpec(
            num_scalar_prefetch=1, grid=(r // tr,),
            in_specs=[pl.BlockSpec((3, tr, c), lambda i, k: (0, i, 0)), own_spec],
            out_specs=pl.BlockSpec((tr, c), lambda i, k: (i, 0))),
        out_shape=jax.ShapeDtypeStruct((r, c), F32),
        compiler_params=_params(1),
    )(chip, parts, own)


def _adamw(w, g_parts, m, v, name):
    n_g = len(g_parts)
    if w.ndim == 2:
        r, c = w.shape
        tr = r
        for cand in (512, 256, 128, 64, 32, 16, 8):
            if r % cand == 0 and cand * c * 4 <= (2 << 20):
                tr = cand
                break
        spec = pl.BlockSpec((tr, c), lambda i: (i, 0))
        tiling = dict(grid=(r // tr,), in_specs=[spec] * (3 + n_g), out_specs=[spec] * 4, compiler_params=_params(1))
    else:
        tiling = dict(compiler_params=pltpu.CompilerParams(vmem_limit_bytes=48 << 20))
    c1 = 1.0 / (1.0 - ADAM_B1 ** ADAM_STEP)
    c2 = 1.0 / (1.0 - ADAM_B2 ** ADAM_STEP)

    def body(*refs):
        w_ref, g_refs, m_ref, v_ref = refs[0], refs[1:1 + n_g], refs[1 + n_g], refs[2 + n_g]
        g_out, d_out, m_out, v_out = refs[3 + n_g:]
        g = g_refs[0][...]
        for gr in g_refs[1:]:
            g = g + gr[...]
        m_new = ADAM_B1 * m_ref[...] + (1.0 - ADAM_B1) * g
        v_new = ADAM_B2 * v_ref[...] + (1.0 - ADAM_B2) * (g * g)
        m_hat = m_new * c1
        v_hat = v_new * c2
        g_out[...] = g
        d_out[...] = -ADAM_LR * (m_hat / (jnp.sqrt(v_hat) + ADAM_EPS) + ADAM_WD * w_ref[...])
        m_out[...] = m_new
        v_out[...] = v_new

    return pl.pallas_call(body, name=name, out_shape=[jax.ShapeDtypeStruct(w.shape, F32)] * 4, **tiling)(w, *g_parts, m, v)


def _ada_fwd(c16, ada_w, ada_b_cols):
    cols = ada_w.shape[2]

    def body(c_ref, w_ref, b_ref, o_ref):
        cv = c_ref[...]
        ca = _bf(cv * jax.nn.sigmoid(cv))
        o_ref[...] = _dot(ca, _bf(w_ref[...])) + b_ref[...]

    return pl.pallas_call(
        body, name="ada_fwd", grid=(DEPTH,),
        in_specs=[_whole((16, D)), pl.BlockSpec((None, D, cols), lambda i: (i, 0, 0)),
                  pl.BlockSpec((None, 1, cols), lambda i: (i, 0, 0))],
        out_specs=pl.BlockSpec((None, 16, cols), lambda i: (i, 0, 0)),
        out_shape=jax.ShapeDtypeStruct((DEPTH, 16, cols), F32),
        compiler_params=_params(1),
    )(c16, ada_w, ada_b_cols)


def _ada_bwd(c16, dmod16):
    cols = dmod16.shape[2]

    def body(c_ref, d_ref, o_ref):
        cv = c_ref[...]
        ca = _bf(cv * jax.nn.sigmoid(cv))
        o_ref[...] = _dot_tn(ca, _bf(d_ref[...]))

    return pl.pallas_call(
        body, name="ada_bwd", grid=(DEPTH,),
        in_specs=[_whole((16, D)), pl.BlockSpec((None, 16, cols), lambda i: (i, 0, 0))],
        out_specs=pl.BlockSpec((None, D, cols), lambda i: (i, 0, 0)),
        out_shape=jax.ShapeDtypeStruct((DEPTH, D, cols), F32),
        compiler_params=_params(1),
    )(c16, dmod16)


def _mod_bwd(vs_mix, vs_ffn, pv):
    def body(m_ref, f_ref, pv_ref, o_ref):
        for i in range(DEPTH):
            vm, vf, p = m_ref[i], f_ref[i], pv_ref[i]
            o_ref[i] = jnp.concatenate([
                vm[2:3], vm[1:2] * p[R_N1:R_N1 + 1], vm[0:1],
                vf[2:3], vf[1:2] * p[R_N2:R_N2 + 1], vf[0:1],
                vm[1:2] * (1.0 + p[R_SC1:R_SC1 + 1]), vf[1:2] * (1.0 + p[R_SC2:R_SC2 + 1])], axis=0)

    return pl.pallas_call(body, name="mod_bwd", out_shape=jax.ShapeDtypeStruct((DEPTH, 8, D), F32))(vs_mix, vs_ffn, pv)


def _ffn_fwd(x1, pv, w1, w2, layer, tm):
    L = x1.shape[0]
    dff = w1.shape[2]

    def body(x1_ref, pv_ref, w1_ref, w2_ref, x2_ref, h2_ref, a_ref, f_ref):
        x1v, p = x1_ref[...], pv_ref[...]
        h2, _, _ = _norm_mod(x1v, p[R_N2:R_N2 + 1], p[R_SC2:R_SC2 + 1], p[R_SH2:R_SH2 + 1])
        hb = _bf(h2)
        h2_ref[...] = hb
        a = _dot(hb, w1_ref[...])
        a_ref[...] = a
        ra = jnp.maximum(a, 0.0)
        f = _dot(_bf(ra * ra), w2_ref[...])
        f_ref[...] = f
        x2_ref[...] = x1v + p[R_G2:R_G2 + 1] * f

    return pl.pallas_call(
        body, name=f"ffn_fwd{layer}", grid=(L // tm,),
        in_specs=[_rows(tm, D), pl.BlockSpec((None, 8, D), lambda i: (layer, 0, 0)), _layer_w(D, dff, 0), _layer_w(dff, D, 0)],
        out_specs=[_rows(tm, D), _rows(tm, D), _rows(tm, dff), _rows(tm, D)],
        out_shape=[jax.ShapeDtypeStruct((L, D), F32), jax.ShapeDtypeStruct((L, D), BF16),
                   jax.ShapeDtypeStruct((L, dff), F32), jax.ShapeDtypeStruct((L, D), F32)],
        compiler_params=_params(1, 56),
    )(x1, pv, w1, w2)


def _ffn_bwd(dx2, x1, a, f, pv, w1, w2, layer, tm, after=None):
    L = x1.shape[0]
    dff = w1.shape[2]
    extra = [] if after is None else [pl.BlockSpec(memory_space=pl.ANY)]
    extra_args = [] if after is None else [after]

    def body(dx2_ref, x1_ref, a_ref, f_ref, pv_ref, w1_ref, w2_ref, *rest):
        dx1_ref, p_ref, da_ref, df_ref, vs_ref = rest[len(extra):]

        @pl.when(pl.program_id(0) == 0)
        def _():
            vs_ref[...] = jnp.zeros_like(vs_ref)

        dx2v, p = dx2_ref[...], pv_ref[...]
        dfb = _bf(dx2v * p[R_G2:R_G2 + 1])
        df_ref[...] = dfb
        vs_ref[0:1, :] += _sum0(dx2v * f_ref[...])
        dp = _dot_nt(dfb, w2_ref[...])
        ra = jnp.maximum(a_ref[...], 0.0)
        p_ref[...] = _bf(ra * ra)
        dab = _bf(dp * (2.0 * ra))
        da_ref[...] = dab
        dh2 = _dot_nt(dab, w1_ref[...])
        _, xn, r = _norm_mod(x1_ref[...], p[R_N2:R_N2 + 1], p[R_SC2:R_SC2 + 1], p[R_SH2:R_SH2 + 1])
        dx1_ref[...] = dx2v + _norm_mod_bwd(dh2, xn, r, p[R_N2:R_N2 + 1], p[R_SC2:R_SC2 + 1])
        vs_ref[1:2, :] += _sum0(dh2 * xn)
        vs_ref[2:3, :] += _sum0(dh2)

    return pl.pallas_call(
        body, name=f"ffn_bwd{layer}", grid=(L // tm,),
        in_specs=[_rows(tm, D), _rows(tm, D), _rows(tm, dff), _rows(tm, D),
                  pl.BlockSpec((None, 8, D), lambda i: (layer, 0, 0)), _layer_w(D, dff, 0), _layer_w(dff, D, 0)] + extra,
        out_specs=[_rows(tm, D), _rows(tm, dff), _rows(tm, dff), _rows(tm, D), _whole((8, D))],
        out_shape=[jax.ShapeDtypeStruct((L, D), F32), jax.ShapeDtypeStruct((L, dff), BF16),
                   jax.ShapeDtypeStruct((L, dff), BF16), jax.ShapeDtypeStruct((L, D), BF16),
                   jax.ShapeDtypeStruct((8, D), F32)],
        compiler_params=_params(1, 56),
    )(dx2, x1, a, f, pv, w1, w2, *extra_args)


def _conv_fwd(x, pv, w_in, w_out, cw, layer, j, tm, after=None):
    L = x.shape[0]

    def body(x_ref, pv_ref, win_ref, wout_ref, cw_ref, x1_ref, h_ref, bcx_ref, conv_ref, q_ref, y_ref, carry):
        @pl.when(pl.program_id(0) == 0)
        def _():
            carry[...] = jnp.zeros_like(carry)

        xv, p, cwv = x_ref[...], pv_ref[...], cw_ref[...]
        h, _, _ = _norm_mod(xv, p[R_N1:R_N1 + 1], p[R_SC1:R_SC1 + 1], p[R_SH1:R_SH1 + 1])
        hb = _bf(h)
        h_ref[...] = hb
        bcx = _dot(hb, win_ref[...])
        bcx_ref[...] = bcx
        z = bcx[:, D:2 * D] * bcx[:, 2 * D:]
        prev8 = carry[...]
        conv = cwv[0:1] * _shift_down(z, prev8, 2) + cwv[1:2] * _shift_down(z, prev8, 1) + cwv[2:3] * z + cwv[3:4]
        conv_ref[...] = conv
        qb = _bf(bcx[:, :D] * conv)
        q_ref[...] = qb
        y = _dot(qb, wout_ref[...])
        y_ref[...] = y
        x1_ref[...] = xv + p[R_G1:R_G1 + 1] * y
        carry[...] = z[tm - 8:tm]

    call, tail = _call_after(
        after, body, 5, name=f"conv_fwd{layer}", grid=(L // tm,),
        in_specs=[_rows(tm, D), pl.BlockSpec((None, 8, D), lambda i: (layer, 0, 0)), _layer_w(D, 3 * D, 0), _layer_w(D, D, 0),
                  pl.BlockSpec((None, 8, D), lambda i: (j, 0, 0))],
        out_specs=[_rows(tm, D), _rows(tm, D), _rows(tm, 3 * D), _rows(tm, D), _rows(tm, D), _rows(tm, D)],
        out_shape=[jax.ShapeDtypeStruct((L, D), F32), jax.ShapeDtypeStruct((L, D), BF16), jax.ShapeDtypeStruct((L, 3 * D), F32),
                   jax.ShapeDtypeStruct((L, D), F32), jax.ShapeDtypeStruct((L, D), BF16), jax.ShapeDtypeStruct((L, D), F32)],
        scratch_shapes=[pltpu.VMEM((8, D), F32)],
        compiler_params=_params(1, 56),
    )
    return call(x, pv, w_in, w_out, cw, *tail)


def _conv_bwd(dx1, x, y, bcx, conv, pv, w_in, w_out, cw, layer, j, tm, after=None):
    L = x.shape[0]
    nt = L // tm

    def body(dx1_ref, x_ref, y_ref, bcx_ref, conv_ref, halo_ref, pv_ref, win_ref, wout_ref, cw_ref,
             dx_ref, dbcx_ref, dy_ref, vs_ref, carry):
        gi = pl.program_id(0)
        tile = nt - 1 - gi

        @pl.when(gi == 0)
        def _():
            vs_ref[...] = jnp.zeros_like(vs_ref)
            carry[...] = jnp.zeros_like(carry)

        dx1v, p, cwv = dx1_ref[...], pv_ref[...], cw_ref[...]
        dyb = _bf(dx1v * p[R_G1:R_G1 + 1])
        dy_ref[...] = dyb
        vs_ref[0:1, :] += _sum0(dx1v * y_ref[...])
        dq = _dot_nt(dyb, wout_ref[...])
        bcx = bcx_ref[...]
        b, cg, xh = bcx[:, :D], bcx[:, D:2 * D], bcx[:, 2 * D:]
        db = dq * conv_ref[...]
        dc = dq * b
        z = cg * xh
        halo = halo_ref[...]
        zprev = jnp.where(tile > 0, halo[:, D:2 * D] * halo[:, 2 * D:], 0.0)
        vs_ref[3:4, :] += _sum0(dc * _shift_down(z, zprev, 2))
        vs_ref[4:5, :] += _sum0(dc * _shift_down(z, zprev, 1))
        vs_ref[5:6, :] += _sum0(dc * z)
        vs_ref[6:7, :] += _sum0(dc)
        next8 = carry[...]
        dz = cwv[2:3] * dc + cwv[1:2] * _shift_up(dc, next8, 1) + cwv[0:1] * _shift_up(dc, next8, 2)
        dbb, dcgb, dxhb = _bf(db), _bf(dz * xh), _bf(dz * cg)
        dbcx_ref[:, 0:D] = dbb
        dbcx_ref[:, D:2 * D] = dcgb
        dbcx_ref[:, 2 * D:3 * D] = dxhb
        dh = (_dot_nt(dbb, win_ref[:, 0:D]) + _dot_nt(dcgb, win_ref[:, D:2 * D])) + _dot_nt(dxhb, win_ref[:, 2 * D:3 * D])
        _, xn, r = _norm_mod(x_ref[...], p[R_N1:R_N1 + 1], p[R_SC1:R_SC1 + 1], p[R_SH1:R_SH1 + 1])
        dx_ref[...] = dx1v + _norm_mod_bwd(dh, xn, r, p[R_N1:R_N1 + 1], p[R_SC1:R_SC1 + 1])
        vs_ref[1:2, :] += _sum0(dh * xn)
        vs_ref[2:3, :] += _sum0(dh)
        carry[...] = dc[0:8]

    halo_spec = pl.BlockSpec((8, 3 * D), lambda i: (jnp.maximum((nt - 1 - i) * (tm // 8) - 1, 0), 0))
    call, tail = _call_after(
        after, body, 10, name=f"conv_bwd{layer}", grid=(nt,),
        in_specs=[_rows(tm, D, nt), _rows(tm, D, nt), _rows(tm, D, nt), _rows(tm, 3 * D, nt), _rows(tm, D, nt), halo_spec,
                  pl.BlockSpec((None, 8, D), lambda i: (layer, 0, 0)), _layer_w(D, 3 * D, 0), _layer_w(D, D, 0),
                  pl.BlockSpec((None, 8, D), lambda i: (j, 0, 0))],
        out_specs=[_rows(tm, D, nt), _rows(tm, 3 * D, nt), _rows(tm, D, nt), _whole((8, D))],
        out_shape=[jax.ShapeDtypeStruct((L, D), F32), jax.ShapeDtypeStruct((L, 3 * D), BF16),
                   jax.ShapeDtypeStruct((L, D), BF16), jax.ShapeDtypeStruct((8, D), F32)],
        scratch_shapes=[pltpu.VMEM((8, D), F32)],
        compiler_params=_params(1, 56),
    )
    return call(dx1, x, y, bcx, conv, bcx, pv, w_in, w_out, cw, *tail)


def _s5_discretize(a_re, a_im, log_dt, bt_re, bt_im):
    dt = jnp.exp(log_dt)
    mag = jnp.exp(a_re * dt)
    abar_re = mag * jnp.cos(a_im * dt)
    abar_im = mag * jnp.sin(a_im * dt)
    den = a_re * a_re + a_im * a_im
    nr = abar_re - 1.0
    ni = abar_im
    f_re = (nr * a_re + ni * a_im) / den
    f_im = (ni * a_re - nr * a_im) / den
    bbar_re = f_re * bt_re - f_im * bt_im
    bbar_im = f_re * bt_im + f_im * bt_re
    return abar_re, abar_im, bbar_re, bbar_im


def _s5_params_fwd(a_re, a_im, log_dt, bt_re, bt_im):
    def body(ar, ai, ld, br, bi, o_ar, o_ai, o_br, o_bi):
        r = _s5_discretize(ar[...], ai[...], ld[...], br[...], bi[...])
        o_ar[...], o_ai[...], o_br[...], o_bi[...] = r

    gp = jax.ShapeDtypeStruct((S5_G, S5_P), F32)
    hgp = jax.ShapeDtypeStruct((S5_H, S5_G, S5_P), F32)
    return pl.pallas_call(body, name="s5_params_fwd", out_shape=[gp, gp, hgp, hgp])(a_re, a_im, log_dt, bt_re, bt_im)


def _s5_params_bwd(a_re, a_im, log_dt, bt_re, bt_im, d_ar, d_ai, d_br, d_bi):
    def body(ar, ai, ld, br, bi, gar, gai, gbr, gbi, o_ar, o_ai, o_ld, o_br, o_bi):
        _, vjp = jax.vjp(_s5_discretize, ar[...], ai[...], ld[...], br[...], bi[...])
        r = vjp((gar[...], gai[...], gbr[...], gbi[...]))
        o_ar[...], o_ai[...], o_ld[...], o_br[...], o_bi[...] = r

    gp = jax.ShapeDtypeStruct((S5_G, S5_P), F32)
    hgp = jax.ShapeDtypeStruct((S5_H, S5_G, S5_P), F32)
    return pl.pallas_call(body, name="s5_params_bwd", out_shape=[gp, gp, jax.ShapeDtypeStruct((S5_G, 1), F32), hgp, hgp])(
        a_re, a_im, log_dt, bt_re, bt_im, d_ar, d_ai, d_br, d_bi)


def _s5_in_fwd(x, pv, w_in, b_re, b_im, layer, tm, after=None):
    L = x.shape[0]

    def body(x_ref, pv_ref, win_ref, bre_ref, bim_ref, h_ref, u_ref, ore_ref, oim_ref):
        p = pv_ref[...]
        h, _, _ = _norm_mod(x_ref[...], p[R_N1:R_N1 + 1], p[R_SC1:R_SC1 + 1], p[R_SH1:R_SH1 + 1])
        hb = _bf(h)
        h_ref[...] = hb
        u = _dot(hb, win_ref[...])
        u_ref[...] = u
        ub = _bf(u)
        for k in range(S5_NB):
            uk = ub[:, k * S5_BH:(k + 1) * S5_BH]
            ore_ref[:, k * S5_BP:(k + 1) * S5_BP] = _dot(uk, bre_ref[k])
            oim_ref[:, k * S5_BP:(k + 1) * S5_BP] = _dot(uk, bim_ref[k])

    call, tail = _call_after(
        after, body, 5, name="s5_in_fwd", grid=(L // tm,),
        in_specs=[_rows(tm, D), pl.BlockSpec((None, 8, D), lambda i: (layer, 0, 0)), _layer_w(D, D, 0),
                  _const_w((S5_NB, S5_BH, S5_BP)), _const_w((S5_NB, S5_BH, S5_BP))],
        out_specs=[_rows(tm, D), _rows(tm, D), _rows(tm, NSTATE), _rows(tm, NSTATE)],
        out_shape=[jax.ShapeDtypeStruct((L, D), BF16), jax.ShapeDtypeStruct((L, D), F32),
                   jax.ShapeDtypeStruct((L, NSTATE), F32), jax.ShapeDtypeStruct((L, NSTATE), F32)],
        compiler_params=_params(1, 56),
    )
    return call(x, pv, w_in, b_re, b_im, *tail)


def _s5_scan_fwd(bu_re, bu_im, ar, ai, tr):
    L = bu_re.shape[0]
    nl = 1024

    def body(bre_ref, bim_ref, ar_ref, ai_ref, sre_ref, sim_ref, st_re, st_im):
        @pl.when(pl.program_id(1) == 0)
        def _():
            st_re[...] = jnp.zeros_like(st_re)
            st_im[...] = jnp.zeros_like(st_im)

        a_r, a_i = ar_ref[...], ai_ref[...]

        def step(t, carry):
            s_r, s_i = carry
            n_r = a_r * s_r - a_i * s_i + bre_ref[pl.ds(t, 1), :]
            n_i = a_r * s_i + a_i * s_r + bim_ref[pl.ds(t, 1), :]
            sre_ref[pl.ds(t, 1), :] = n_r
            sim_ref[pl.ds(t, 1), :] = n_i
            return n_r, n_i

        s_r, s_i = lax.fori_loop(0, tr, step, (st_re[...], st_im[...]), unroll=8)
        st_re[...] = s_r
        st_im[...] = s_i

    blk = pl.BlockSpec((tr, nl), lambda j, i: (i, j))
    vec = pl.BlockSpec((1, nl), lambda j, i: (0, j))
    return pl.pallas_call(
        body, name="s5_scan_fwd", grid=(NSTATE // nl, L // tr),
        in_specs=[blk, blk, vec, vec], out_specs=[blk, blk],
        out_shape=[jax.ShapeDtypeStruct((L, NSTATE), F32)] * 2,
        scratch_shapes=[pltpu.VMEM((1, nl), F32), pltpu.VMEM((1, nl), F32)],
        compiler_params=_params(2),
    )(bu_re, bu_im, ar, ai)


def _s5_out_fwd(x, u, s_re, s_im, pv, c_re, c_im, dvec, glu_w, glu_b, w_out, layer, tm):
    L = x.shape[0]

    def body(x_ref, u_ref, sre_ref, sim_ref, pv_ref, cre_ref, cim_ref, d_ref, gw_ref, gb_ref, wout_ref,
             x1_ref, y1_ref, zg_ref, y3_ref, y_ref):
        p = pv_ref[...]
        srb, sib = _bf(sre_ref[...]), _bf(sim_ref[...])
        parts = []
        for k in range(S5_NB):
            sl = slice(k * S5_BP, (k + 1) * S5_BP)
            parts.append(_dot(srb[:, sl], cre_ref[k]) - _dot(sib[:, sl], cim_ref[k]))
        y1 = jnp.concatenate(parts, axis=1) + d_ref[...] * u_ref[...]
        y1_ref[...] = y1
        y2 = jax.nn.gelu(y1)
        zg = _dot(_bf(y2), gw_ref[...]) + gb_ref[...]
        zg_ref[...] = zg
        y3b = _bf(y2 * jax.nn.sigmoid(zg))
        y3_ref[...] = y3b
        y = _dot(y3b, wout_ref[...])
        y_ref[...] = y
        x1_ref[...] = x_ref[...] + p[R_G1:R_G1 + 1] * y

    return pl.pallas_call(
        body, name="s5_out_fwd", grid=(L // tm,),
        in_specs=[_rows(tm, D), _rows(tm, D), _rows(tm, NSTATE), _rows(tm, NSTATE),
                  pl.BlockSpec((None, 8, D), lambda i: (layer, 0, 0)),
                  _const_w((S5_NB, S5_BP, S5_BH)), _const_w((S5_NB, S5_BP, S5_BH)), _whole((1, D)),
                  _layer_w(D, D, 0), _whole((1, D)), _layer_w(D, D, 0)],
        out_specs=[_rows(tm, D)] * 5,
        out_shape=[jax.ShapeDtypeStruct((L, D), F32), jax.ShapeDtypeStruct((L, D), F32), jax.ShapeDtypeStruct((L, D), F32),
                   jax.ShapeDtypeStruct((L, D), BF16), jax.ShapeDtypeStruct((L, D), F32)],
        compiler_params=_params(1, 56),
    )(x, u, s_re, s_im, pv, c_re, c_im, dvec, glu_w, glu_b, w_out)


def _s5_out_bwd(dx1, y, y1, zg, u, pv, c_re, c_im, dvec, glu_w, w_out, layer, tm, after=None):
    L = dx1.shape[0]

    def body(dx1_ref, y_ref, y1_ref, zg_ref, u_ref, pv_ref, cre_ref, cim_ref, d_ref, gw_ref, wout_ref,
             dy_ref, y2_ref, dzg_ref, dy1_ref, dus_ref, gre_ref, gim_ref, vs_ref):
        @pl.when(pl.program_id(0) == 0)
        def _():
            vs_ref[...] = jnp.zeros_like(vs_ref)

        dx1v, p = dx1_ref[...], pv_ref[...]
        dyb = _bf(dx1v * p[R_G1:R_G1 + 1])
        dy_ref[...] = dyb
        vs_ref[0:1, :] += _sum0(dx1v * y_ref[...])
        dy3 = _dot_nt(dyb, wout_ref[...])
        y2, gelu_vjp = jax.vjp(jax.nn.gelu, y1_ref[...])
        y2_ref[...] = _bf(y2)
        gate = jax.nn.sigmoid(zg_ref[...])
        dzg = dy3 * y2 * gate * (1.0 - gate)
        dzgb = _bf(dzg)
        dzg_ref[...] = dzgb
        vs_ref[1:2, :] += _sum0(dzg)
        dy2 = dy3 * gate + _dot_nt(dzgb, gw_ref[...])
        dy1 = gelu_vjp(dy2)[0]
        vs_ref[2:3, :] += _sum0(dy1 * u_ref[...])
        dus_ref[...] = dy1 * d_ref[...]
        dy1b = _bf(dy1)
        dy1_ref[...] = dy1b
        for k in range(S5_NB):
            dk = dy1b[:, k * S5_BH:(k + 1) * S5_BH]
            gre_ref[:, k * S5_BP:(k + 1) * S5_BP] = _dot_nt(dk, cre_ref[k])
            gim_ref[:, k * S5_BP:(k + 1) * S5_BP] = -_dot_nt(dk, cim_ref[k])

    call, tail = _call_after(
        after, body, 11, name="s5_out_bwd", grid=(L // tm,),
        in_specs=[_rows(tm, D)] * 5 + [pl.BlockSpec((None, 8, D), lambda i: (layer, 0, 0)),
                  _const_w((S5_NB, S5_BP, S5_BH)), _const_w((S5_NB, S5_BP, S5_BH)), _whole((1, D)),
                  _layer_w(D, D, 0), _layer_w(D, D, 0)],
        out_specs=[_rows(tm, D)] * 5 + [_rows(tm, NSTATE), _rows(tm, NSTATE), _whole((8, D))],
        out_shape=[jax.ShapeDtypeStruct((L, D), BF16)] * 4 + [jax.ShapeDtypeStruct((L, D), F32),
                   jax.ShapeDtypeStruct((L, NSTATE), F32), jax.ShapeDtypeStruct((L, NSTATE), F32),
                   jax.ShapeDtypeStruct((8, D), F32)],
        compiler_params=_params(1, 56),
    )
    return call(dx1, y, y1, zg, u, pv, c_re, c_im, dvec, glu_w, w_out, *tail)


def _s5_scan_bwd(g_re, g_im, s_re, s_im, ar, ai, tr):
    L = g_re.shape[0]
    nl = 1024
    nt = L // tr

    def body(gre_ref, gim_ref, sre_ref, sim_ref, hre_ref, him_ref, ar_ref, ai_ref, lre_ref, lim_ref, da_ref, st_re, st_im):
        gi = pl.program_id(1)
        tile = nt - 1 - gi

        @pl.when(gi == 0)
        def _():
            st_re[...] = jnp.zeros_like(st_re)
            st_im[...] = jnp.zeros_like(st_im)
            da_ref[...] = jnp.zeros_like(da_ref)

        a_r, a_i = ar_ref[...], ai_ref[...]

        def step(k, carry):
            l_r, l_i = carry
            t = tr - 1 - k
            n_r = gre_ref[pl.ds(t, 1), :] + a_r * l_r + a_i * l_i
            n_i = gim_ref[pl.ds(t, 1), :] - a_i * l_r + a_r * l_i
            lre_ref[pl.ds(t, 1), :] = n_r
            lim_ref[pl.ds(t, 1), :] = n_i
            return n_r, n_i

        l_r, l_i = lax.fori_loop(0, tr, step, (st_re[...], st_im[...]), unroll=8)
        st_re[...] = l_r
        st_im[...] = l_i
        lam_r, lam_i = lre_ref[...], lim_ref[...]
        p_r = jnp.where(tile > 0, hre_ref[...], 0.0)
        p_i = jnp.where(tile > 0, him_ref[...], 0.0)
        sp_r = _shift_down(sre_ref[...], p_r, 1)
        sp_i = _shift_down(sim_ref[...], p_i, 1)
        da_ref[0:1, :] += _sum0(lam_r * sp_r + lam_i * sp_i)
        da_ref[1:2, :] += _sum0(lam_i * sp_r - lam_r * sp_i)

    blk = pl.BlockSpec((tr, nl), lambda j, i: (nt - 1 - i, j))
    halo = pl.BlockSpec((8, nl), lambda j, i: (jnp.maximum((nt - 1 - i) * (tr // 8) - 1, 0), j))
    vec = pl.BlockSpec((1, nl), lambda j, i: (0, j))
    return pl.pallas_call(
        body, name="s5_scan_bwd", grid=(NSTATE // nl, nt),
        in_specs=[blk, blk, blk, blk, halo, halo, vec, vec],
        out_specs=[blk, blk, pl.BlockSpec((8, nl), lambda j, i: (0, j))],
        out_shape=[jax.ShapeDtypeStruct((L, NSTATE), F32)] * 2 + [jax.ShapeDtypeStruct((8, NSTATE), F32)],
        scratch_shapes=[pltpu.VMEM((1, nl), F32), pltpu.VMEM((1, nl), F32)],
        compiler_params=_params(2),
    )(g_re, g_im, s_re, s_im, s_re, s_im, ar, ai)


def _s5_in_bwd(dx1, lam_re, lam_im, du_skip, x, pv, b_re, b_im, w_in, layer, tm):
    L = x.shape[0]

    def body(dx1_ref, lre_ref, lim_ref, dus_ref, x_ref, pv_ref, bre_ref, bim_ref, win_ref, dx_ref, du_ref, vs_ref):
        @pl.when(pl.program_id(0) == 0)
        def _():
            vs_ref[...] = jnp.zeros_like(vs_ref)

        p = pv_ref[...]
        lrb, lib = _bf(lre_ref[...]), _bf(lim_ref[...])
        parts = []
        for k in range(S5_NB):
            sl = slice(k * S5_BP, (k + 1) * S5_BP)
            parts.append(_dot_nt(lrb[:, sl], bre_ref[k]) + _dot_nt(lib[:, sl], bim_ref[k]))
        dub = _bf(jnp.concatenate(parts, axis=1) + dus_ref[...])
        du_ref[...] = dub
        dh = _dot_nt(dub, win_ref[...])
        _, xn, r = _norm_mod(x_ref[...], p[R_N1:R_N1 + 1], p[R_SC1:R_SC1 + 1], p[R_SH1:R_SH1 + 1])
        dx_ref[...] = dx1_ref[...] + _norm_mod_bwd(dh, xn, r, p[R_N1:R_N1 + 1], p[R_SC1:R_SC1 + 1])
        vs_ref[1:2, :] += _sum0(dh * xn)
        vs_ref[2:3, :] += _sum0(dh)

    return pl.pallas_call(
        body, name="s5_in_bwd", grid=(L // tm,),
        in_specs=[_rows(tm, D), _rows(tm, NSTATE), _rows(tm, NSTATE), _rows(tm, D), _rows(tm, D),
                  pl.BlockSpec((None, 8, D), lambda i: (layer, 0, 0)),
                  _const_w((S5_NB, S5_BH, S5_BP)), _const_w((S5_NB, S5_BH, S5_BP)), _layer_w(D, D, 0)],
        out_specs=[_rows(tm, D), _rows(tm, D), _whole((8, D))],
        out_shape=[jax.ShapeDtypeStruct((L, D), F32), jax.ShapeDtypeStruct((L, D), BF16), jax.ShapeDtypeStruct((8, D), F32)],
        compiler_params=_params(1, 56),
    )(dx1, lam_re, lam_im, du_skip, x, pv, b_re, b_im, w_in)


NSEG = 8
SCAN_LANES = 1024


def _to_segments(x):
    n, c = x.shape
    return x.reshape(NSEG, n // NSEG, c).transpose(1, 0, 2).reshape(n, c)


def _from_segments(x):
    n, c = x.shape
    return x.reshape(n // NSEG, NSEG, c).transpose(1, 0, 2).reshape(n, c)


def _segment_scan(re_ref, im_ref, st_re, st_im, a_re, a_im, n_slabs, adjoint, write):
    for q in range(NSTATE // SCAN_LANES):
        ls = slice(q * SCAN_LANES, (q + 1) * SCAN_LANES)
        ar = jnp.broadcast_to(a_re[:, ls], (8, SCAN_LANES))
        ai = jnp.broadcast_to(a_im[:, ls], (8, SCAN_LANES))

        def step(k, carry, ls=ls, ar=ar, ai=ai):
            s_r, s_i = carry
            slab = (n_slabs - 1 - k) if adjoint else k
            rows = pl.ds(pl.multiple_of(slab * 8, 8), 8)
            b_r, b_i = re_ref[rows, ls], im_ref[rows, ls]
            if adjoint:
                n_r = b_r + ar * s_r + ai * s_i
                n_i = b_i - ai * s_r + ar * s_i
            else:
                n_r = ar * s_r - ai * s_i + b_r
                n_i = ar * s_i + ai * s_r + b_i
            if write:
                re_ref[rows, ls] = n_r
                im_ref[rows, ls] = n_i
            return n_r, n_i

        s_r, s_i = lax.fori_loop(0, n_slabs, step, (st_re[:, ls], st_im[:, ls]), unroll=4)
        st_re[:, ls] = s_r
        st_im[:, ls] = s_i


def _s5_segment_states(e_re, e_im, ar, ai, seg_len, adjoint):
    def body(ere_ref, eim_ref, ar_ref, ai_ref, ore_ref, oim_ref):
        p_r, p_i = ar_ref[...], ai_ref[...]
        if adjoint:
            p_i = -p_i
        acc_r, acc_i = jnp.ones_like(p_r), jnp.zeros_like(p_r)
        n = seg_len
        while n:
            if n & 1:
                acc_r, acc_i = acc_r * p_r - acc_i * p_i, acc_r * p_i + acc_i * p_r
            n >>= 1
            if n:
                p_r, p_i = p_r * p_r - p_i * p_i, 2.0 * p_r * p_i
        e_r, e_i = ere_ref[...], eim_ref[...]
        s_r, s_i = jnp.zeros_like(acc_r), jnp.zeros_like(acc_r)
        rows_r, rows_i = [None] * NSEG, [None] * NSEG
        order = range(NSEG - 1, -1, -1) if adjoint else range(NSEG)
        for j in order:
            rows_r[j], rows_i[j] = s_r, s_i
            s_r, s_i = (acc_r * s_r - acc_i * s_i + e_r[j:j + 1], acc_r * s_i + acc_i * s_r + e_i[j:j + 1])
        ore_ref[...] = jnp.concatenate(rows_r, axis=0)
        oim_ref[...] = jnp.concatenate(rows_i, axis=0)

    st = jax.ShapeDtypeStruct((NSEG, NSTATE), F32)
    return pl.pallas_call(body, name="s5_segment_states_bwd" if adjoint else "s5_segment_states_fwd", out_shape=[st, st])(
        e_re, e_im, ar, ai)


def _s5_fwd_ends(x, pv, w_in, b_re, b_im, ar, ai, layer, tm, after=None):
    L = x.shape[0]

    def body(x_ref, pv_ref, win_ref, bre_ref, bim_ref, ar_ref, ai_ref, h_ref, u_ref, ere_ref, eim_ref, bu_re, bu_im):
        @pl.when(pl.program_id(0) == 0)
        def _():
            ere_ref[...] = jnp.zeros_like(ere_ref)
            eim_ref[...] = jnp.zeros_like(eim_ref)

        p = pv_ref[...]
        h, _, _ = _norm_mod(x_ref[...], p[R_N1:R_N1 + 1], p[R_SC1:R_SC1 + 1], p[R_SH1:R_SH1 + 1])
        hb = _bf(h)
        h_ref[...] = hb
        u = _dot(hb, win_ref[...])
        u_ref[...] = u
        ub = _bf(u)
        for k in range(S5_NB):
            uk = ub[:, k * S5_BH:(k + 1) * S5_BH]
            bu_re[:, k * S5_BP:(k + 1) * S5_BP] = _dot(uk, bre_ref[k])
            bu_im[:, k * S5_BP:(k + 1) * S5_BP] = _dot(uk, bim_ref[k])
        _segment_scan(bu_re, bu_im, ere_ref, eim_ref, ar_ref[...], ai_ref[...], tm // 8, adjoint=False, write=False)

    call, tail = _call_after(
        after, body, 7, name="s5_fwd_ends", grid=(L // tm,),
        in_specs=[_rows(tm, D), pl.BlockSpec((None, 8, D), lambda i: (layer, 0, 0)), _layer_w(D, D, 0),
                  _const_w((S5_NB, S5_BH, S5_BP)), _const_w((S5_NB, S5_BH, S5_BP)), _whole((1, NSTATE)), _whole((1, NSTATE))],
        out_specs=[_rows(tm, D), _rows(tm, D), _whole((NSEG, NSTATE)), _whole((NSEG, NSTATE))],
        out_shape=[jax.ShapeDtypeStruct((L, D), BF16), jax.ShapeDtypeStruct((L, D), F32),
                   jax.ShapeDtypeStruct((NSEG, NSTATE), F32), jax.ShapeDtypeStruct((NSEG, NSTATE), F32)],
        scratch_shapes=[pltpu.VMEM((tm, NSTATE), F32), pltpu.VMEM((tm, NSTATE), F32)],
        compiler_params=_params(1, 56),
    )
    return call(x, pv, w_in, b_re, b_im, ar, ai, *tail)


def _s5_fwd_out(x, u, pv, b_re, b_im, s0_re, s0_im, ar, ai, c_re, c_im, dvec, glu_w, glu_b, w_out, layer, tm):
    L = x.shape[0]

    def body(x_ref, u_ref, pv_ref, bre_ref, bim_ref, s0re_ref, s0im_ref, ar_ref, ai_ref, cre_ref, cim_ref, d_ref, gw_ref,
             gb_ref, wout_ref, x1_ref, sre_ref, sim_ref, y1_ref, zg_ref, y3_ref, y_ref, st_re, st_im):
        @pl.when(pl.program_id(0) == 0)
        def _():
            st_re[...] = s0re_ref[...]
            st_im[...] = s0im_ref[...]

        p = pv_ref[...]
        uv = u_ref[...]
        ub = _bf(uv)
        for k in range(S5_NB):
            uk = ub[:, k * S5_BH:(k + 1) * S5_BH]
            sre_ref[:, k * S5_BP:(k + 1) * S5_BP] = _dot(uk, bre_ref[k])
            sim_ref[:, k * S5_BP:(k + 1) * S5_BP] = _dot(uk, bim_ref[k])
        _segment_scan(sre_ref, sim_ref, st_re, st_im, ar_ref[...], ai_ref[...], tm // 8, adjoint=False, write=True)
        parts = []
        for k in range(S5_NB):
            sl = slice(k * S5_BP, (k + 1) * S5_BP)
            parts.append(_dot(_bf(sre_ref[:, sl]), cre_ref[k]) - _dot(_bf(sim_ref[:, sl]), cim_ref[k]))
        y1 = jnp.concatenate(parts, axis=1) + d_ref[...] * uv
        y1_ref[...] = y1
        y2 = jax.nn.gelu(y1)
        zg = _dot(_bf(y2), gw_ref[...]) + gb_ref[...]
        zg_ref[...] = zg
        y3b = _bf(y2 * jax.nn.sigmoid(zg))
        y3_ref[...] = y3b
        y = _dot(y3b, wout_ref[...])
        y_ref[...] = y
        x1_ref[...] = x_ref[...] + p[R_G1:R_G1 + 1] * y

    return pl.pallas_call(
        body, name="s5_fwd_out", grid=(L // tm,),
        in_specs=[_rows(tm, D), _rows(tm, D), pl.BlockSpec((None, 8, D), lambda i: (layer, 0, 0)),
                  _const_w((S5_NB, S5_BH, S5_BP)), _const_w((S5_NB, S5_BH, S5_BP)),
                  _whole((NSEG, NSTATE)), _whole((NSEG, NSTATE)), _whole((1, NSTATE)), _whole((1, NSTATE)),
                  _const_w((S5_NB, S5_BP, S5_BH)), _const_w((S5_NB, S5_BP, S5_BH)), _whole((1, D)),
                  _layer_w(D, D, 0), _whole((1, D)), _layer_w(D, D, 0)],
        out_specs=[_rows(tm, D), _rows(tm, NSTATE), _rows(tm, NSTATE), _rows(tm, D), _rows(tm, D), _rows(tm, D), _rows(tm, D)],
        out_shape=[jax.ShapeDtypeStruct((L, D), F32), jax.ShapeDtypeStruct((L, NSTATE), F32), jax.ShapeDtypeStruct((L, NSTATE), F32),
                   jax.ShapeDtypeStruct((L, D), F32), jax.ShapeDtypeStruct((L, D), F32),
                   jax.ShapeDtypeStruct((L, D), BF16), jax.ShapeDtypeStruct((L, D), F32)],
        scratch_shapes=[pltpu.VMEM((NSEG, NSTATE), F32), pltpu.VMEM((NSEG, NSTATE), F32)],
        compiler_params=_params(1, 56),
    )(x, u, pv, b_re, b_im, s0_re, s0_im, ar, ai, c_re, c_im, dvec, glu_w, glu_b, w_out)


def _s5_bwd_ends(dx1, y, y1, zg, u, pv, c_re, c_im, ar, ai, dvec, glu_w, w_out, layer, tm, after=None):
    L = dx1.shape[0]
    nt = L // tm

    def body(dx1_ref, y_ref, y1_ref, zg_ref, u_ref, pv_ref, cre_ref, cim_ref, ar_ref, ai_ref, d_ref, gw_ref, wout_ref,
             dy_ref, y2_ref, dzg_ref, dy1_ref, dus_ref, ere_ref, eim_ref, vs_ref, g_re, g_im):
        @pl.when(pl.program_id(0) == 0)
        def _():
            vs_ref[...] = jnp.zeros_like(vs_ref)
            ere_ref[...] = jnp.zeros_like(ere_ref)
            eim_ref[...] = jnp.zeros_like(eim_ref)

        dx1v, p = dx1_ref[...], pv_ref[...]
        dyb = _bf(dx1v * p[R_G1:R_G1 + 1])
        dy_ref[...] = dyb
        vs_ref[0:1, :] += _sum0(dx1v * y_ref[...])
        dy3 = _dot_nt(dyb, wout_ref[...])
        y2, gelu_vjp = jax.vjp(jax.nn.gelu, y1_ref[...])
        y2_ref[...] = _bf(y2)
        gate = jax.nn.sigmoid(zg_ref[...])
        dzg = dy3 * y2 * gate * (1.0 - gate)
        dzgb = _bf(dzg)
        dzg_ref[...] = dzgb
        vs_ref[1:2, :] += _sum0(dzg)
        dy2 = dy3 * gate + _dot_nt(dzgb, gw_ref[...])
        dy1 = gelu_vjp(dy2)[0]
        vs_ref[2:3, :] += _sum0(dy1 * u_ref[...])
        dus_ref[...] = dy1 * d_ref[...]
        dy1b = _bf(dy1)
        dy1_ref[...] = dy1b
        for k in range(S5_NB):
            dk = dy1b[:, k * S5_BH:(k + 1) * S5_BH]
            g_re[:, k * S5_BP:(k + 1) * S5_BP] = _dot_nt(dk, cre_ref[k])
            g_im[:, k * S5_BP:(k + 1) * S5_BP] = -_dot_nt(dk, cim_ref[k])
        _segment_scan(g_re, g_im, ere_ref, eim_ref, ar_ref[...], ai_ref[...], tm // 8, adjoint=True, write=False)

    call, tail = _call_after(
        after, body, 13, name="s5_bwd_ends", grid=(nt,),
        in_specs=[_rows(tm, D, nt)] * 5 + [pl.BlockSpec((None, 8, D), lambda i: (layer, 0, 0)),
                  _const_w((S5_NB, S5_BP, S5_BH)), _const_w((S5_NB, S5_BP, S5_BH)), _whole((1, NSTATE)), _whole((1, NSTATE)),
                  _whole((1, D)), _layer_w(D, D, 0), _layer_w(D, D, 0)],
        out_specs=[_rows(tm, D, nt)] * 5 + [_whole((NSEG, NSTATE)), _whole((NSEG, NSTATE)), _whole((8, D))],
        out_shape=[jax.ShapeDtypeStruct((L, D), BF16)] * 4 + [jax.ShapeDtypeStruct((L, D), F32),
                   jax.ShapeDtypeStruct((NSEG, NSTATE), F32), jax.ShapeDtypeStruct((NSEG, NSTATE), F32),
                   jax.ShapeDtypeStruct((8, D), F32)],
        scratch_shapes=[pltpu.VMEM((tm, NSTATE), F32), pltpu.VMEM((tm, NSTATE), F32)],
        compiler_params=_params(1, 56),
    )
    return call(dx1, y, y1, zg, u, pv, c_re, c_im, ar, ai, dvec, glu_w, w_out, *tail)


def _s5_bwd_in(dx1, dy1_b, du_skip, x, s_re, s_im, pv, b_re, b_im, c_re, c_im, l0_re, l0_im, ar, ai, w_in, layer, tm):
    L = x.shape[0]
    nt = L // tm

    def body(dx1_ref, dy1_ref, dus_ref, x_ref, sre_ref, sim_ref, hre_ref, him_ref, lre_ref, lim_ref, pv_ref, bre_ref, bim_ref,
             cre_ref, cim_ref, l0re_ref, l0im_ref, ar_ref, ai_ref, win_ref,
             dx_ref, du_ref, lamre_ref, lamim_ref, da_ref, vs_ref, g_re, g_im, st_re, st_im):
        gi = pl.program_id(0)
        tile = nt - 1 - gi

        @pl.when(gi == 0)
        def _():
            vs_ref[...] = jnp.zeros_like(vs_ref)
            da_ref[...] = jnp.zeros_like(da_ref)
            st_re[...] = l0re_ref[...]
            st_im[...] = l0im_ref[...]

        p = pv_ref[...]
        dy1b = dy1_ref[...]
        for k in range(S5_NB):
            dk = dy1b[:, k * S5_BH:(k + 1) * S5_BH]
            g_re[:, k * S5_BP:(k + 1) * S5_BP] = _dot_nt(dk, cre_ref[k])
            g_im[:, k * S5_BP:(k + 1) * S5_BP] = -_dot_nt(dk, cim_ref[k])
        _segment_scan(g_re, g_im, st_re, st_im, ar_ref[...], ai_ref[...], tm // 8, adjoint=True, write=True)
        lam_r, lam_i = g_re[...], g_im[...]
        lrb, lib = _bf(lam_r), _bf(lam_i)
        lamre_ref[...] = lrb
        lamim_ref[...] = lib

        def wrapped(last_ref):
            z = last_ref[...]
            row = lax.broadcasted_iota(jnp.int32, z.shape, 0)
            return jnp.where(row >= 1, pltpu.roll(z, 1, 0), 0.0)

        first_r = jnp.where(tile > 0, hre_ref[...], wrapped(lre_ref))
        first_i = jnp.where(tile > 0, him_ref[...], wrapped(lim_ref))
        sp_r = jnp.concatenate([first_r, sre_ref[0:tm - 8, :]], axis=0)
        sp_i = jnp.concatenate([first_i, sim_ref[0:tm - 8, :]], axis=0)
        da_ref[0:1, :] += _sum0(lam_r * sp_r + lam_i * sp_i)
        da_ref[1:2, :] += _sum0(lam_i * sp_r - lam_r * sp_i)

        parts = []
        for k in range(S5_NB):
            sl = slice(k * S5_BP, (k + 1) * S5_BP)
            parts.append(_dot_nt(lrb[:, sl], bre_ref[k]) + _dot_nt(lib[:, sl], bim_ref[k]))
        dub = _bf(jnp.concatenate(parts, axis=1) + dus_ref[...])
        du_ref[...] = dub
        dh = _dot_nt(dub, win_ref[...])
        _, xn, r = _norm_mod(x_ref[...], p[R_N1:R_N1 + 1], p[R_SC1:R_SC1 + 1], p[R_SH1:R_SH1 + 1])
        dx_ref[...] = dx1_ref[...] + _norm_mod_bwd(dh, xn, r, p[R_N1:R_N1 + 1], p[R_SC1:R_SC1 + 1])
        vs_ref[1:2, :] += _sum0(dh * xn)
        vs_ref[2:3, :] += _sum0(dh)

    halo = pl.BlockSpec((8, NSTATE), lambda i: (jnp.maximum((nt - 1 - i) * (tm // 8) - 1, 0), 0))
    last = pl.BlockSpec((8, NSTATE), lambda i: (L // 8 - 1, 0))
    return pl.pallas_call(
        body, name="s5_bwd_in", grid=(nt,),
        in_specs=[_rows(tm, D, nt), _rows(tm, D, nt), _rows(tm, D, nt), _rows(tm, D, nt), _rows(tm, NSTATE, nt), _rows(tm, NSTATE, nt),
                  halo, halo, last, last, pl.BlockSpec((None, 8, D), lambda i: (layer, 0, 0)),
                  _const_w((S5_NB, S5_BH, S5_BP)), _const_w((S5_NB, S5_BH, S5_BP)),
                  _const_w((S5_NB, S5_BP, S5_BH)), _const_w((S5_NB, S5_BP, S5_BH)),
                  _whole((NSEG, NSTATE)), _whole((NSEG, NSTATE)), _whole((1, NSTATE)), _whole((1, NSTATE)), _layer_w(D, D, 0)],
        out_specs=[_rows(tm, D, nt), _rows(tm, D, nt), _rows(tm, NSTATE, nt), _rows(tm, NSTATE, nt), _whole((8, NSTATE)), _whole((8, D))],
        out_shape=[jax.ShapeDtypeStruct((L, D), F32), jax.ShapeDtypeStruct((L, D), BF16),
                   jax.ShapeDtypeStruct((L, NSTATE), BF16), jax.ShapeDtypeStruct((L, NSTATE), BF16),
                   jax.ShapeDtypeStruct((8, NSTATE), F32), jax.ShapeDtypeStruct((8, D), F32)],
        scratch_shapes=[pltpu.VMEM((tm, NSTATE), F32), pltpu.VMEM((tm, NSTATE), F32),
                        pltpu.VMEM((NSEG, NSTATE), F32), pltpu.VMEM((NSEG, NSTATE), F32)],
        compiler_params=_params(1, 60),
    )(dx1, dy1_b, du_skip, x, s_re, s_im, s_re, s_im, s_re, s_im, pv, b_re, b_im, c_re, c_im, l0_re, l0_im, ar, ai, w_in)


def _blockdiag_b(bt):
    b = bt.reshape(S5_H, S5_NB, 16, S5_P).transpose(1, 2, 0, 3)
    eye = jnp.eye(16, dtype=bt.dtype)
    return (b[:, :, :, None, :] * eye[None, :, None, :, None]).reshape(S5_NB, S5_BH, S5_BP)


def _unblock_b(d):
    d = jnp.einsum("bghgp->bghp", d.reshape(S5_NB, 16, S5_H, 16, S5_P))
    return d.transpose(2, 0, 1, 3).reshape(S5_H, S5_G, S5_P)


def _blockdiag_c(cm):
    c4 = cm.reshape(S5_NB, 16, S5_H, S5_P)
    eye = jnp.eye(16, dtype=cm.dtype)
    out = c4.transpose(0, 1, 3, 2)[:, :, :, None, :] * eye[None, :, None, :, None]
    return out.reshape(S5_NB, S5_BP, S5_BH)


def _unblock_c(d):
    d = jnp.einsum("bgpgh->bghp", d.reshape(S5_NB, 16, S5_P, 16, S5_H))
    return d.reshape(S5_G, S5_H, S5_P)


def _tril_mask():
    return lax.broadcasted_iota(jnp.int32, (SG_CHUNK, SG_CHUNK), 0) >= lax.broadcasted_iota(jnp.int32, (SG_CHUNK, SG_CHUNK), 1)


def _sg_fwd(x, pv, w_in, w_s, b_t, vg, w_out, layer, tm, after=None):
    L = x.shape[0]
    nc = tm // SG_CHUNK

    def body(x_ref, pv_ref, win_ref, ws_ref, bt_ref, vg_ref, wout_ref, x1_ref, h_ref, uv_ref, vm_ref, q_ref, y_ref):
        xv, p = x_ref[...], pv_ref[...]
        h, _, _ = _norm_mod(xv, p[R_N1:R_N1 + 1], p[R_SC1:R_SC1 + 1], p[R_SH1:R_SH1 + 1])
        hb = _bf(h)
        h_ref[...] = hb
        uv = _dot(hb, win_ref[...])
        uv_ref[...] = uv
        v = uv[:, D:]
        rv = lax.rsqrt(jnp.mean(v * v, axis=-1, keepdims=True) + EPS)
        vnb = _bf((v * rv) * vg_ref[...])
        mask = _tril_mask()
        bt = bt_ref[...]
        for hd in range(SG_HEADS):
            wm = _bf(jnp.where(mask, ws_ref[hd], 0.0))
            cs = slice(hd * SG_CHUNK, (hd + 1) * SG_CHUNK)
            for ck in range(nc):
                rs = slice(ck * SG_CHUNK, (ck + 1) * SG_CHUNK)
                vm_ref[rs, cs] = _dot(wm, vnb[rs, cs]) + bt[:, hd:hd + 1]
        qb = _bf(uv[:, :D] * vm_ref[...])
        q_ref[...] = qb
        y = _dot(qb, wout_ref[...])
        y_ref[...] = y
        x1_ref[...] = xv + p[R_G1:R_G1 + 1] * y

    call, tail = _call_after(
        after, body, 7, name="sg_fwd", grid=(L // tm,),
        in_specs=[_rows(tm, D), pl.BlockSpec((None, 8, D), lambda i: (layer, 0, 0)), _layer_w(D, 2 * D, 0),
                  _whole((SG_HEADS, SG_CHUNK, SG_CHUNK)), _whole((SG_CHUNK, SG_HEADS)), _whole((1, D)), _layer_w(D, D, 0)],
        out_specs=[_rows(tm, D), _rows(tm, D), _rows(tm, 2 * D), _rows(tm, D), _rows(tm, D), _rows(tm, D)],
        out_shape=[jax.ShapeDtypeStruct((L, D), F32), jax.ShapeDtypeStruct((L, D), BF16), jax.ShapeDtypeStruct((L, 2 * D), F32),
                   jax.ShapeDtypeStruct((L, D), F32), jax.ShapeDtypeStruct((L, D), BF16), jax.ShapeDtypeStruct((L, D), F32)],
        compiler_params=_params(1, 56),
    )
    return call(x, pv, w_in, w_s, b_t, vg, w_out, *tail)


def _sg_bwd(dx1, x, y, uv, vm, pv, w_in, w_s, vg, w_out, layer, tm, after=None):
    L = x.shape[0]
    nc = tm // SG_CHUNK

    def body(dx1_ref, x_ref, y_ref, uv_ref, vm_ref, pv_ref, win_ref, ws_ref, vg_ref, wout_ref,
             dx_ref, duv_ref, dy_ref, vs_ref, dws_ref, dbt_ref, dvn_scr):
        @pl.when(pl.program_id(0) == 0)
        def _():
            vs_ref[...] = jnp.zeros_like(vs_ref)
            dws_ref[...] = jnp.zeros_like(dws_ref)
            dbt_ref[...] = jnp.zeros_like(dbt_ref)

        dx1v, p = dx1_ref[...], pv_ref[...]
        dyb = _bf(dx1v * p[R_G1:R_G1 + 1])
        dy_ref[...] = dyb
        vs_ref[0:1, :] += _sum0(dx1v * y_ref[...])
        dq = _dot_nt(dyb, wout_ref[...])
        uv = uv_ref[...]
        u, v = uv[:, :D], uv[:, D:]
        dub = _bf(dq * vm_ref[...])
        dvm = dq * u
        dvmb = _bf(dvm)
        rv = lax.rsqrt(jnp.mean(v * v, axis=-1, keepdims=True) + EPS)
        vh = v * rv
        vgv = vg_ref[...]
        vnb = _bf(vh * vgv)
        mask = _tril_mask()
        for hd in range(SG_HEADS):
            wm = _bf(jnp.where(mask, ws_ref[hd], 0.0))
            cs = slice(hd * SG_CHUNK, (hd + 1) * SG_CHUNK)
            dws = jnp.zeros((SG_CHUNK, SG_CHUNK), F32)
            dbs = jnp.zeros((SG_CHUNK, 1), F32)
            for ck in range(nc):
                rs = slice(ck * SG_CHUNK, (ck + 1) * SG_CHUNK)
                dvn_scr[rs, cs] = _dot_tn(wm, dvmb[rs, cs])
                dws = dws + _dot_nt(dvmb[rs, cs], vnb[rs, cs])
                dbs = dbs + jnp.sum(dvm[rs, cs], axis=1, keepdims=True)
            dws_ref[hd] += jnp.where(mask, dws, 0.0)
            dbt_ref[:, hd:hd + 1] += dbs
        dvn = dvn_scr[...]
        vs_ref[3:4, :] += _sum0(dvn * vh)
        dvnn = dvn * vgv
        dvb = _bf(rv * (dvnn - vh * jnp.mean(dvnn * vh, axis=-1, keepdims=True)))
        duv_ref[:, 0:D] = dub
        duv_ref[:, D:2 * D] = dvb
        dh = _dot_nt(dub, win_ref[:, 0:D]) + _dot_nt(dvb, win_ref[:, D:2 * D])
        _, xn, r = _norm_mod(x_ref[...], p[R_N1:R_N1 + 1], p[R_SC1:R_SC1 + 1], p[R_SH1:R_SH1 + 1])
        dx_ref[...] = dx1v + _norm_mod_bwd(dh, xn, r, p[R_N1:R_N1 + 1], p[R_SC1:R_SC1 + 1])
        vs_ref[1:2, :] += _sum0(dh * xn)
        vs_ref[2:3, :] += _sum0(dh)

    call, tail = _call_after(
        after, body, 10, name="sg_bwd", grid=(L // tm,),
        in_specs=[_rows(tm, D), _rows(tm, D), _rows(tm, D), _rows(tm, 2 * D), _rows(tm, D),
                  pl.BlockSpec((None, 8, D), lambda i: (layer, 0, 0)), _layer_w(D, 2 * D, 0),
                  _whole((SG_HEADS, SG_CHUNK, SG_CHUNK)), _whole((1, D)), _layer_w(D, D, 0)],
        out_specs=[_rows(tm, D), _rows(tm, 2 * D), _rows(tm, D), _whole((8, D)),
                   _whole((SG_HEADS, SG_CHUNK, SG_CHUNK)), _whole((SG_CHUNK, SG_HEADS))],
        out_shape=[jax.ShapeDtypeStruct((L, D), F32), jax.ShapeDtypeStruct((L, 2 * D), BF16), jax.ShapeDtypeStruct((L, D), BF16),
                   jax.ShapeDtypeStruct((8, D), F32), jax.ShapeDtypeStruct((SG_HEADS, SG_CHUNK, SG_CHUNK), F32),
                   jax.ShapeDtypeStruct((SG_CHUNK, SG_HEADS), F32)],
        scratch_shapes=[pltpu.VMEM((tm, D), F32)],
        compiler_params=_params(1, 56),
    )
    return call(dx1, x, y, uv, vm, pv, w_in, w_s, vg, w_out, *tail)


def _final(x, target, fg, tm):
    L = x.shape[0]

    def body(x_ref, t_ref, g_ref, dx_ref, vs_ref):
        @pl.when(pl.program_id(0) == 0)
        def _():
            vs_ref[...] = jnp.zeros_like(vs_ref)

        xv, g = x_ref[...], g_ref[...]
        r = lax.rsqrt(jnp.mean(xv * xv, axis=-1, keepdims=True) + EPS)
        xn = xv * r
        e = xn * g - t_ref[...]
        vs_ref[0:1, :] += jnp.sum(e * e)
        dout = e * (1.0 / D)
        vs_ref[1:2, :] += _sum0(dout * xn)
        dxn = dout * g
        dx_ref[...] = r * (dxn - xn * jnp.mean(dxn * xn, axis=-1, keepdims=True))

    return pl.pallas_call(
        body, name="final_loss", grid=(L // tm,),
        in_specs=[_rows(tm, D), _rows(tm, D), _whole((1, D))],
        out_specs=[_rows(tm, D), _whole((8, D))],
        out_shape=[jax.ShapeDtypeStruct((L, D), F32), jax.ShapeDtypeStruct((8, D), F32)],
        compiler_params=_params(1),
    )(x, target, fg)


def _pack_flat(arrs, multiple=LANES):
    n = sum(a.size for a in arrs)
    pieces = [a.reshape(-1).astype(F32) for a in arrs]
    if -n % multiple:
        pieces.append(jnp.zeros((-n % multiple,), F32))
    return jnp.concatenate(pieces)


def _pack(arrs, row_multiple=8):
    return _pack_flat(arrs, row_multiple * LANES).reshape(-1, LANES)


def _unpack(buf, shapes, lead=()):
    flat = buf.reshape(lead + (-1,))
    out, off = [], 0
    for s in shapes:
        n = 1
        for d in s:
            n *= d
        out.append(flat[..., off:off + n].reshape(lead + tuple(s)))
        off += n
    return out


BIG = ("ff_w1", "ff_w2", "conv_w_in", "conv_w_out", "ssm_w_in", "ssm_glu_w", "ssm_w_out", "sg_w_in", "sg_w_out")
BIG_AXIS = {"ff_w1": 2, "ff_w2": 1, "conv_w_in": 2, "conv_w_out": 1, "ssm_w_in": 1, "ssm_glu_w": 1, "ssm_w_out": 1,
            "sg_w_in": 2, "sg_w_out": 1}
LAYER_WEIGHTS = (
    (("conv_w_in", 0), ("conv_w_out", 0), ("ff_w1", 0), ("ff_w2", 0)),
    (("ssm_w_in", 0), ("ssm_glu_w", 0), ("ssm_w_out", 0), ("ff_w1", 1), ("ff_w2", 1)),
    (("sg_w_in", 0), ("sg_w_out", 0), ("ff_w1", 2), ("ff_w2", 2)),
    (("conv_w_in", 1), ("conv_w_out", 1), ("ff_w1", 3), ("ff_w2", 3)),
)
GATHER_GROUPS = tuple(grp for lw in LAYER_WEIGHTS for grp in (lw[:-2], lw[-2:]))
SMALL = ("ada_b", "norm1_g", "norm2_g", "final_g", "conv_w", "conv_b", "ssm_a_re", "ssm_a_im", "ssm_log_dt", "ssm_b_re",
         "ssm_b_im", "ssm_c_re", "ssm_c_im", "ssm_d", "ssm_glu_b", "sg_v_g", "sg_w_s", "sg_b_s")
WEIGHTS = ("ada_w", "ada_b", "norm1_g", "norm2_g", "ff_w1", "ff_w2", "final_g", "conv_w_in", "conv_w", "conv_b", "conv_w_out",
           "ssm_w_in", "ssm_a_re", "ssm_a_im", "ssm_log_dt", "ssm_b_re", "ssm_b_im", "ssm_c_re", "ssm_c_im", "ssm_d",
           "ssm_glu_w", "ssm_glu_b", "ssm_w_out", "sg_w_in", "sg_v_g", "sg_w_s", "sg_b_s", "sg_w_out")


def kernel(x, c, ada_w, ada_b, norm1_g, norm2_g, ff_w1, ff_w2, final_g, conv_w_in, conv_w, conv_b, conv_w_out, ssm_w_in, ssm_a_re, ssm_a_im, ssm_log_dt, ssm_b_re, ssm_b_im, ssm_c_re, ssm_c_im, ssm_d, ssm_glu_w, ssm_glu_b, ssm_w_out, sg_w_in, sg_v_g, sg_w_s, sg_b_s, sg_w_out, loss_target, m_ada_w, m_ada_b, m_norm1_g, m_norm2_g, m_ff_w1, m_ff_w2, m_final_g, m_conv_w_in, m_conv_w, m_conv_b, m_conv_w_out, m_ssm_w_in, m_ssm_a_re, m_ssm_a_im, m_ssm_log_dt, m_ssm_b_re, m_ssm_b_im, m_ssm_c_re, m_ssm_c_im, m_ssm_d, m_ssm_glu_w, m_ssm_glu_b, m_ssm_w_out, m_sg_w_in, m_sg_v_g, m_sg_w_s, m_sg_b_s, m_sg_w_out, v_ada_w, v_ada_b, v_norm1_g, v_norm2_g, v_ff_w1, v_ff_w2, v_final_g, v_conv_w_in, v_conv_w, v_conv_b, v_conv_w_out, v_ssm_w_in, v_ssm_a_re, v_ssm_a_im, v_ssm_log_dt, v_ssm_b_re, v_ssm_b_im, v_ssm_c_re, v_ssm_c_im, v_ssm_d, v_ssm_glu_w, v_ssm_glu_b, v_ssm_w_out, v_sg_w_in, v_sg_v_g, v_sg_w_s, v_sg_b_s, v_sg_w_out):
    args = dict(locals())
    w = {n: args[n] for n in WEIGHTS}
    m = {n: args["m_" + n] for n in WEIGHTS}
    v = {n: args["v_" + n] for n in WEIGHTS}
    L = x.shape[1]
    tm = min(L, 256)
    tr = min(L, 512)
    chip = 2 * lax.axis_index("x") + lax.axis_index("y")
    me = 2 * chip + lax.axis_index("c")
    xin = x[0]
    target = loss_target[0]
    chip1 = chip.reshape(1).astype(jnp.int32)

    gathers = []

    def start_gather(g, after):
        entries = GATHER_GROUPS[g]
        axes = [BIG_AXIS[n] for n, _ in entries]
        lands = [_cast_place(w[n], li, BIG_AXIS[n], chip1, f"cast_{n}_{li}") for n, li in entries]
        s_sems, r_sems, lands, token = _gather_start(lands, axes, f"gather_start{g}", after)
        gathers.append((s_sems, r_sems, lands, axes))
        return token

    def weights_of(g, after):
        s_sems, r_sems, lands, axes = gathers[g]
        lands = _gather_wait(s_sems, r_sems, lands, axes, f"gather_wait{g}", after)
        return dict(zip([n for n, _ in GATHER_GROUPS[g]], _gather_share(lands, axes, f"gather_share{g}")))

    small_in = _pack([c, conv_w, conv_b, sg_v_g])
    got = _allgather_small(small_in, "gather_small_inputs").reshape(N_DEV, -1)
    c_all, cw_sh, cb_sh, vg_sh = _unpack(got, [(D,), conv_w.shape, conv_b.shape, sg_v_g.shape], lead=(N_DEV,))
    conv_w_full = jnp.concatenate([cw_sh[2 * k] for k in range(4)], axis=-1)
    conv_b_full = jnp.concatenate([cb_sh[2 * k] for k in range(4)], axis=-1)
    vg_full = jnp.concatenate([vg_sh[2 * k] for k in range(4)], axis=-1)
    c16 = jnp.pad(c_all, ((0, 16 - N_DEV), (0, 0)))

    cols = ada_w.shape[2]
    ada_b_cols = lax.dynamic_slice_in_dim(ada_b, chip * cols, cols, axis=1)[:, None, :]
    mod_sh = _ada_fwd(c16, ada_w, ada_b_cols)[:, :N_DEV, :]
    mod_all = _allgather_small(_pack([mod_sh]), "gather_mod").reshape(N_DEV, -1)
    mod_all = _unpack(mod_all, [mod_sh.shape], lead=(N_DEV,))[0]
    mod_mine = lax.dynamic_index_in_dim(mod_all[0::2], me, axis=2, keepdims=False)
    mod_mine = mod_mine.transpose(1, 0, 2).reshape(DEPTH, 6, D)
    pv = jnp.concatenate([mod_mine, norm1_g[:, None, :], norm2_g[:, None, :]], axis=1)

    token = pv
    for g in range(len(GATHER_GROUPS)):
        token = start_gather(g, token)

    cw_rows = jnp.concatenate([conv_w_full, conv_b_full[:, None, :], jnp.zeros((conv_w_full.shape[0], 4, D), F32)], axis=1)

    a_re, a_im = ssm_a_re[0], ssm_a_im[0]
    log_dt = ssm_log_dt[0][:, None]
    bt_re, bt_im = ssm_b_re[0].transpose(2, 0, 1), ssm_b_im[0].transpose(2, 0, 1)
    abar_re, abar_im, bbar_re, bbar_im = _s5_params_fwd(a_re, a_im, log_dt, bt_re, bt_im)
    ar_vec, ai_vec = abar_re.reshape(1, NSTATE), abar_im.reshape(1, NSTATE)
    bd_re, bd_im = _bf(_blockdiag_b(bbar_re)), _bf(_blockdiag_b(bbar_im))
    cd_re, cd_im = _bf(_blockdiag_c(ssm_c_re[0])), _bf(_blockdiag_c(ssm_c_im[0]))

    saved = []
    fulls = []
    xl = xin
    for i in range(DEPTH):
        kind = MIXER_OF_LAYER[i]
        j = i // 3
        full = weights_of(2 * i, cd_im if i == 0 else xl)
        fulls.append(full)
        tok = None
        if kind == 0:
            x1, h, bcx, conv, q, y = _conv_fwd(xl, pv, full["conv_w_in"], full["conv_w_out"], cw_rows, i, j, tm, after=tok)
            mix = dict(h=h, bcx=bcx, conv=conv, q=q, y=y)
        elif kind == 1:
            xp = _to_segments(xl)
            h, u, e_re, e_im = _s5_fwd_ends(xp, pv, full["ssm_w_in"], bd_re, bd_im, ar_vec, ai_vec, i, tm, after=tok)
            s0_re, s0_im = _s5_segment_states(e_re, e_im, ar_vec, ai_vec, L // NSEG, adjoint=False)
            x1p, s_re, s_im, y1, zg, y3, y = _s5_fwd_out(xp, u, pv, bd_re, bd_im, s0_re, s0_im, ar_vec, ai_vec, cd_re, cd_im,
                                                         ssm_d, full["ssm_glu_w"], ssm_glu_b, full["ssm_w_out"], i, tm)
            x1 = _from_segments(x1p)
            mix = dict(xp=xp, h=h, u=u, s_re=s_re, s_im=s_im, y1=y1, zg=zg, y3=y3, y=y)
        else:
            x1, h, uv, vm, q, y = _sg_fwd(xl, pv, full["sg_w_in"], sg_w_s[0], sg_b_s[0].T, vg_full, full["sg_w_out"], i, tm,
                                          after=tok)
            mix = dict(h=h, uv=uv, vm=vm, q=q, y=y)
        full.update(weights_of(2 * i + 1, x1))
        x2, h2, a, f = _ffn_fwd(x1, pv, full["ff_w1"], full["ff_w2"], i, tm)
        saved.append(dict(x=xl, x1=x1, h2=h2, a=a, f=f, **mix))
        xl = x2

    dxl, vs_fin = _final(xl, target, final_g[None, :], tm)

    gfull = {n: [None] * w[n].shape[0] for n in BIG}
    vs_mix, vs_ffn = [None] * DEPTH, [None] * DEPTH
    small_g = {}
    scatters = {}
    token = None

    def start_scatter(key, entries, after):
        garrs = [gfull[n][li][None] for n, li in entries]
        gaxes = [BIG_AXIS[n] for n, _ in entries]
        s_sems, r_sems, garrs, lands, tok = _scatter_start(garrs, gaxes, f"scatter_start{key}", after)
        scatters[key] = (s_sems, r_sems, garrs, lands, gaxes, entries)
        return tok

    for i in reversed(range(DEPTH)):
        kind = MIXER_OF_LAYER[i]
        j = i // 3
        sv = saved[i]
        full = fulls[i]
        dx1, p_b, da_b, df_b, vs_ffn[i] = _ffn_bwd(dxl, sv["x1"], sv["a"], sv["f"], pv, full["ff_w1"], full["ff_w2"], i, tm,
                                                   after=token)
        gfull["ff_w1"][i] = _mm_tn(sv["h2"], da_b, f"wgrad_ff_w1_{i}")
        gfull["ff_w2"][i] = _mm_tn(p_b, df_b, f"wgrad_ff_w2_{i}")
        if i == 0:
            token = start_scatter("0f", LAYER_WEIGHTS[0][2:], dx1)
        if kind == 0:
            dxl, dbcx_b, dy_b, vsm = _conv_bwd(dx1, sv["x"], sv["y"], sv["bcx"], sv["conv"], pv, full["conv_w_in"],
                                               full["conv_w_out"], cw_rows, i, j, tm, after=token if i == 0 else None)
            gfull["conv_w_in"][j] = _mm_tn(sv["h"], dbcx_b, f"wgrad_conv_w_in_{j}")
            gfull["conv_w_out"][j] = _mm_tn(sv["q"], dy_b, f"wgrad_conv_w_out_{j}")
            small_g.setdefault("conv_w", [None, None])[j] = vsm[3:6]
            small_g.setdefault("conv_b", [None, None])[j] = vsm[6]
        elif kind == 1:
            dx1p = _to_segments(dx1)
            dy_b, y2_b, dzg_b, dy1_b, du_skip, eb_re, eb_im, vsm = _s5_bwd_ends(
                dx1p, sv["y"], sv["y1"], sv["zg"], sv["u"], pv, cd_re, cd_im, ar_vec, ai_vec, ssm_d, full["ssm_glu_w"],
                full["ssm_w_out"], i, tm)
            l0_re, l0_im = _s5_segment_states(eb_re, eb_im, ar_vec, ai_vec, L // NSEG, adjoint=True)
            dxp, du_b, lam_re, lam_im, dabar, vs_in = _s5_bwd_in(
                dx1p, dy1_b, du_skip, sv["xp"], sv["s_re"], sv["s_im"], pv, bd_re, bd_im, cd_re, cd_im, l0_re, l0_im,
                ar_vec, ai_vec, full["ssm_w_in"], i, tm)
            dxl = _from_segments(dxp)
            gfull["ssm_w_out"][0] = _mm_tn(sv["y3"], dy_b, "wgrad_ssm_w_out")
            gfull["ssm_glu_w"][0] = _mm_tn(y2_b, dzg_b, "wgrad_ssm_glu_w")
            gfull["ssm_w_in"][0] = _mm_tn(sv["h"], du_b, "wgrad_ssm_w_in")
            d_cre = _unblock_c(_mm_tn_blocks(sv["s_re"], dy1_b, S5_BP, S5_BH, "wgrad_s5_c_re"))
            d_cim = -_unblock_c(_mm_tn_blocks(sv["s_im"], dy1_b, S5_BP, S5_BH, "wgrad_s5_c_im"))
            d_bbre = _unblock_b(_mm_tn_blocks(sv["u"], lam_re, S5_BH, S5_BP, "wgrad_s5_b_re"))
            d_bbim = _unblock_b(_mm_tn_blocks(sv["u"], lam_im, S5_BH, S5_BP, "wgrad_s5_b_im"))
            d_are, d_aim, d_ldt, d_btre, d_btim = _s5_params_bwd(
                a_re, a_im, log_dt, bt_re, bt_im, dabar[0].reshape(S5_G, S5_P), dabar[1].reshape(S5_G, S5_P), d_bbre, d_bbim)
            small_g.update(ssm_a_re=d_are, ssm_a_im=d_aim, ssm_log_dt=d_ldt, ssm_b_re=d_btre.transpose(1, 2, 0),
                           ssm_b_im=d_btim.transpose(1, 2, 0), ssm_c_re=d_cre, ssm_c_im=d_cim, ssm_d=vsm[2], ssm_glu_b=vsm[1])
            vsm = jnp.concatenate([vsm[0:1], vs_in[1:3], jnp.zeros((5, D), F32)], axis=0)
        else:
            dxl, duv_b, dy_b, vsm, d_ws, d_bt = _sg_bwd(dx1, sv["x"], sv["y"], sv["uv"], sv["vm"], pv, full["sg_w_in"],
                                                        sg_w_s[0], vg_full, full["sg_w_out"], i, tm)
            gfull["sg_w_in"][0] = _mm_tn(sv["h"], duv_b, "wgrad_sg_w_in")
            gfull["sg_w_out"][0] = _mm_tn(sv["q"], dy_b, "wgrad_sg_w_out")
            small_g.update(sg_v_g=vsm[3], sg_w_s=d_ws, sg_b_s=d_bt.T)
        vs_mix[i] = vsm
        token = start_scatter(str(i), LAYER_WEIGHTS[i], dxl) if i > 0 else start_scatter("0c", LAYER_WEIGHTS[0][:2], dxl)
    grad_x = dxl[None]

    sums = {n: [None] * w[n].shape[0] for n in BIG}

    def collect(key, after):
        s_sems, r_sems, garrs, lands, gaxes, entries = scatters[key]
        garrs, recv = _scatter_wait(s_sems, r_sems, garrs, lands, gaxes, f"scatter_wait{key}", after)
        for (n, li), g, r3, ax in zip(entries, garrs, recv, gaxes):
            sums[n][li] = _sum_parts(r3, g, ax, chip1, f"sum_{n}_{li}")
        return sums[entries[-1][0]][entries[-1][1]]

    after = token
    for key in ("3", "2", "1"):
        after = collect(key, after)
    early = [(n, li) for i in (3, 2, 1) for n, li in LAYER_WEIGHTS[i]]
    late = list(LAYER_WEIGHTS[0][2:]) + list(LAYER_WEIGHTS[0][:2])
    sib = dict(zip(early, _swap_with_sibling([sums[n][li] for n, li in early], "swap_grad_sums_early")))
    after = sib[early[-1]]
    for key in ("0f", "0c"):
        after = collect(key, after)
    sib.update(zip(late, _swap_with_sibling([sums[n][li] for n, li in late], "swap_grad_sums_late")))

    dmod = _mod_bwd(jnp.stack(vs_mix), jnp.stack(vs_ffn), pv)
    small_g.update(ada_b=dmod[:, :6, :], norm1_g=dmod[:, 6, :], norm2_g=dmod[:, 7, :], final_g=vs_fin[1],
                   conv_w=jnp.stack(small_g["conv_w"]), conv_b=jnp.stack(small_g["conv_b"]))

    loss_part = (0.5 / D) * vs_fin[0, 0:1]
    part_shapes = [(1,)] + [tuple(small_g[n].shape) for n in SMALL]
    parts_sum = _allreduce_small(_pack([loss_part] + [small_g[n] for n in SMALL], 16), "reduce_small_grads", sib[late[-1]])
    summed = _unpack(parts_sum, part_shapes)
    loss = summed[0][0]
    gsum = dict(zip(SMALL, summed[1:]))
    dmod_all = _allgather_small(_pack([small_g["ada_b"]]), "gather_dmod", parts_sum)
    dmod_all = dmod_all.reshape(N_DEV, DEPTH, 6 * D)
    dmod_cols = lax.dynamic_slice_in_dim(dmod_all, chip * cols, cols, axis=2).transpose(1, 0, 2)
    g_ada_w = _ada_bwd(c16, jnp.pad(dmod_cols, ((0, 0), (0, 16 - N_DEV), (0, 0))))

    res = {}
    shp = ada_w.shape
    two = lambda t: t.reshape(shp[0] * shp[1], shp[2])
    res["ada_w"] = [t.reshape(shp) for t in _adamw(two(ada_w), [two(g_ada_w)], two(m_ada_w), two(v_ada_w), "adamw_ada_w")]

    def mine(n, g):
        if n in ("conv_w", "conv_b", "sg_v_g"):
            size = w[n].shape[-1]
            return lax.dynamic_slice_in_dim(g, chip * size, size, axis=g.ndim - 1)
        return g

    g_loc = [mine(n, gsum[n]).reshape(w[n].shape) for n in SMALL]
    small_shapes = [tuple(w[n].shape) for n in SMALL]
    packed = _adamw(_pack_flat([w[n] for n in SMALL]), [_pack_flat(g_loc)], _pack_flat([m[n] for n in SMALL]), _pack_flat([v[n] for n in SMALL]),
                    "adamw_small")
    unpacked = [_unpack(t, small_shapes) for t in packed]
    for k, n in enumerate(SMALL):
        res[n] = [unpacked[0][k], unpacked[1][k], unpacked[2][k], unpacked[3][k]]

    q_mine = [jnp.stack(sums[n]) for n in BIG]
    q_sib = [jnp.stack([sib[(n, li)] for li in range(w[n].shape[0])]) for n in BIG]
    for n, qa, qb in zip(BIG, q_mine, q_sib):
        shp = w[n].shape
        two = lambda t, shp=shp: t.reshape(shp[0] * shp[1], shp[2])
        res[n] = [t.reshape(shp) for t in _adamw(two(w[n]), [two(qa), two(qb)], two(m[n]), two(v[n]), f"adamw_{n}")]

    outs = [loss, grad_x]
    for part in range(4):
        outs += [res[n][part] for n in WEIGHTS]
    return tuple(outs)
```

```python
import functools

import jax
import jax.numpy as jnp
from jax import lax
from jax.experimental import pallas as pl
from jax.experimental.pallas import tpu as pltpu

F32 = jnp.float32
BF16 = jnp.bfloat16
D = 1024
EPS = 1e-6
DEPTH = 4
MIXER_OF_LAYER = (0, 1, 2, 0)
S5_G, S5_H, S5_P = 64, 16, 64
S5_NB = 4
S5_BH = S5_H * 16
S5_BP = S5_P * 16
NSTATE = S5_G * S5_P
SG_HEADS, SG_CHUNK = 8, 128
ADAM_LR, ADAM_B1, ADAM_B2, ADAM_EPS, ADAM_WD, ADAM_STEP = 0.001, 0.9, 0.999, 1e-08, 0.01, 10
N_DEV = 8
MESH = pl.DeviceIdType.MESH
LANES = 1024
R_SH1, R_SC1, R_G1, R_SH2, R_SC2, R_G2, R_N1, R_N2 = range(8)


def _dot(a, b):
    return jnp.dot(a, b, preferred_element_type=F32)


def _dot_nt(a, b):
    return lax.dot_general(a, b, (((1,), (1,)), ((), ())), preferred_element_type=F32)


def _dot_tn(a, b):
    return lax.dot_general(a, b, (((0,), (0,)), ((), ())), preferred_element_type=F32)


def _bf(x):
    return x.astype(BF16)


def _sum0(x):
    return jnp.sum(x, axis=0, keepdims=True)


def _params(n_axes, vmem_mb=48):
    return pltpu.CompilerParams(dimension_semantics=("arbitrary",) * n_axes, vmem_limit_bytes=vmem_mb << 20)


def _rows(tm, cols, nt=None):
    if nt is None:
        return pl.BlockSpec((tm, cols), lambda i: (i, 0))
    return pl.BlockSpec((tm, cols), lambda i: (nt - 1 - i, 0))


def _whole(shape):
    nd = len(shape)
    return pl.BlockSpec(shape, lambda *_: (0,) * nd)


def _layer_w(r, c, layer):
    return pl.BlockSpec((None, r, c), lambda *_: (layer, 0, 0), pipeline_mode=pl.Buffered(1))


def _const_w(shape):
    nd = len(shape)
    return pl.BlockSpec(shape, lambda *_: (0,) * nd, pipeline_mode=pl.Buffered(1))


def _call_after(after, body, n_in, *, in_specs, **kw):
    if after is None:
        return pl.pallas_call(body, in_specs=in_specs, **kw), ()

    def body_after(*refs):
        return body(*refs[:n_in], *refs[n_in + 1:])

    return pl.pallas_call(body_after, in_specs=list(in_specs) + [pl.BlockSpec(memory_space=pl.ANY)], **kw), (after,)


def _norm_mod(x, ng, sc, sh):
    r = lax.rsqrt(jnp.mean(x * x, axis=-1, keepdims=True) + EPS)
    xn = x * r
    return (xn * ng) * (1.0 + sc) + sh, xn, r


def _norm_mod_bwd(dh, xn, r, ng, sc):
    dxn = dh * (ng * (1.0 + sc))
    return r * (dxn - xn * jnp.mean(dxn * xn, axis=-1, keepdims=True))


def _shift_down(z, prev8, k):
    row = lax.broadcasted_iota(jnp.int32, z.shape, 0)
    if k == 1:
        return jnp.where(row >= 1, pltpu.roll(z, 1, 0), prev8[7:8])
    return jnp.where(row >= 2, pltpu.roll(z, 2, 0), jnp.where(row == 0, prev8[6:7], prev8[7:8]))


def _shift_up(z, next8, k):
    n = z.shape[0]
    row = lax.broadcasted_iota(jnp.int32, z.shape, 0)
    if k == 1:
        return jnp.where(row <= n - 2, pltpu.roll(z, n - 1, 0), next8[0:1])
    return jnp.where(row <= n - 3, pltpu.roll(z, n - 2, 0), jnp.where(row == n - 2, next8[0:1], next8[1:2]))


def _place():
    x, y, c = lax.axis_index("x"), lax.axis_index("y"), lax.axis_index("c")
    chips = [(1 - x, y), (x, 1 - y), (1 - x, 1 - y)]
    return x, y, c, chips


def _allgather_small(x_shard, name, after=None):
    m_per, n = x_shard.shape

    def body(x_ref, out_ref, send_sems, recv_sems, local_sem):
        x, y, c, chips = _place()
        me, sibling = (x, y, c), (x, y, 1 - c)

        def rows(px, py, pc):
            return out_ref.at[pl.ds((4 * px + 2 * py + pc) * m_per, m_per), :]

        def copy(k, block, to, src=None):
            return pltpu.make_async_remote_copy(
                src_ref=rows(*block) if src is None else src, dst_ref=rows(*block),
                send_sem=send_sems.at[k], recv_sem=recv_sems.at[k], device_id=to, device_id_type=MESH)

        mine = pltpu.make_async_copy(x_ref, rows(*me), local_sem)
        mine.start()
        first = [copy(0, me, sibling, src=x_ref)]
        first += [copy(1 + j, me, (*chip, c), src=x_ref) for j, chip in enumerate(chips)]
        for cp in first:
            cp.start()
        passed = [copy(4 + j, (*chip, c), sibling) for j, chip in enumerate(chips)]
        for j, chip in enumerate(chips):
            copy(1 + j, (*chip, c), me).wait_recv()
            passed[j].start()
        copy(0, sibling, me).wait_recv()
        for j, chip in enumerate(chips):
            copy(4 + j, (*chip, 1 - c), me).wait_recv()
        for cp in first + passed:
            cp.wait_send()
        mine.wait()

    call, tail = _call_after(
        after, body, 1, name=name, out_shape=jax.ShapeDtypeStruct((N_DEV * m_per, n), F32),
        in_specs=[pl.BlockSpec(memory_space=pltpu.VMEM)], out_specs=pl.BlockSpec(memory_space=pltpu.VMEM),
        scratch_shapes=[pltpu.SemaphoreType.DMA((7,)), pltpu.SemaphoreType.DMA((7,)), pltpu.SemaphoreType.DMA],
        compiler_params=pltpu.CompilerParams(vmem_limit_bytes=48 << 20),
    )
    return call(x_shard, *tail)


def _allreduce_small(x_part, name, after=None):
    m, n = x_part.shape
    h = m // 2

    def body(x_ref, out_ref, sib_buf, slots, send_sems, recv_sems):
        x, y, c, chips = _place()
        k_me = 2 * x + y
        sibling = (x, y, 1 - c)
        mine = pl.ds(pl.multiple_of(c * h, 8), h)

        def copy(k, src, dst, to):
            return pltpu.make_async_remote_copy(src_ref=src, dst_ref=dst, send_sem=send_sems.at[k], recv_sem=recv_sems.at[k],
                                                device_id=to, device_id_type=MESH)

        swap = copy(0, x_ref, sib_buf, sibling)
        swap.start()
        swap.wait()
        slots[pl.ds(k_me, 1)] = (x_ref[mine, :] + sib_buf[mine, :])[None]
        my_slot = slots.at[pl.ds(k_me, 1)]
        sends = [copy(1 + j, my_slot, my_slot, (*chip, c)) for j, chip in enumerate(chips)]
        for cp in sends:
            cp.start()
        for j, chip in enumerate(chips):
            their_slot = slots.at[pl.ds(2 * chip[0] + chip[1], 1)]
            copy(1 + j, their_slot, their_slot, (x, y, c)).wait_recv()
        for cp in sends:
            cp.wait_send()
        out_ref[mine, :] = ((slots[0] + slots[1]) + slots[2]) + slots[3]
        give = copy(4, out_ref.at[mine, :], out_ref.at[mine, :], sibling)
        give.start()
        give.wait_send()
        theirs = pl.ds(pl.multiple_of((1 - c) * h, 8), h)
        copy(4, out_ref.at[theirs, :], out_ref.at[theirs, :], (x, y, c)).wait_recv()

    call, tail = _call_after(
        after, body, 1, name=name, out_shape=jax.ShapeDtypeStruct((m, n), F32),
        in_specs=[pl.BlockSpec(memory_space=pltpu.VMEM)], out_specs=pl.BlockSpec(memory_space=pltpu.VMEM),
        scratch_shapes=[pltpu.VMEM((m, n), F32), pltpu.VMEM((4, h, n), F32),
                        pltpu.SemaphoreType.DMA((5,)), pltpu.SemaphoreType.DMA((5,))],
        compiler_params=pltpu.CompilerParams(vmem_limit_bytes=48 << 20),
    )
    return call(x_part, *tail)


def _shard_region(ref, full_shape, axis, chip_k, half=None):
    _, r, c = full_shape
    if axis == 1:
        rs = r // 4
        if half is None:
            return ref.at[:, pl.ds(pl.multiple_of(chip_k * rs, 128), rs), :]
        return ref.at[:, pl.ds(pl.multiple_of(chip_k * rs + half * (rs // 2), 128), rs // 2), :]
    cs = c // 4
    if half is None:
        return ref.at[:, :, pl.ds(pl.multiple_of(chip_k * cs, 128), cs)]
    return ref.at[:, pl.ds(pl.multiple_of(half * (r // 2), 128), r // 2), pl.ds(pl.multiple_of(chip_k * cs, 128), cs)]


HBM_SPEC = pl.BlockSpec(memory_space=pltpu.HBM)
SEM_SPEC = pl.BlockSpec(memory_space=pltpu.SEMAPHORE)
ANY_SPEC = pl.BlockSpec(memory_space=pl.ANY)
SPLIT_COPY_PARAMS = pltpu.CompilerParams(has_side_effects=pltpu.SideEffectType.DATAFLOW_SIDE_EFFECTING)


def _in_hbm(arrs):
    return [pltpu.with_memory_space_constraint(a, pltpu.HBM) for a in arrs]


def _cast_place(w_stack, li, axis, chip, name):
    _, r, c = w_stack.shape
    full = (1, 4 * r, c) if axis == 1 else (1, r, 4 * c)
    tr = min(r, 256)
    if axis == 1:
        out_spec = pl.BlockSpec((None, tr, c), lambda i, k: (0, k[0] * (r // tr) + i, 0))
    else:
        out_spec = pl.BlockSpec((None, tr, c), lambda i, k: (0, i, k[0]))

    def body(k_ref, w_ref, o_ref):
        o_ref[...] = _bf(w_ref[...])

    return pl.pallas_call(
        body, name=name,
        grid_spec=pltpu.PrefetchScalarGridSpec(
            num_scalar_prefetch=1, grid=(r // tr,),
            in_specs=[pl.BlockSpec((None, tr, c), lambda i, k: (li, i, 0))], out_specs=out_spec),
        out_shape=jax.ShapeDtypeStruct(full, BF16),
        compiler_params=_params(1),
    )(chip, w_stack)


def _gather_start(lands, axes, name, after):
    n_arr = len(lands)
    fulls = [tuple(l.shape) for l in lands]

    def body(*refs):
        land = refs[:n_arr]
        send_sems, recv_sems = refs[n_arr + 1:n_arr + 3]
        token = refs[-1]
        x, y, c, chips = _place()
        k_me = 2 * x + y
        for a in range(n_arr):
            mine = _shard_region(land[a], fulls[a], axes[a], k_me, c)
            for j, chip in enumerate(chips):
                pltpu.make_async_remote_copy(
                    src_ref=mine, dst_ref=mine, send_sem=send_sems.at[a * 3 + j], recv_sem=recv_sems.at[a * 3 + j],
                    device_id=(*chip, c), device_id_type=MESH).start()
        token[...] = jnp.zeros_like(token)

    res = pl.pallas_call(
        body, name=name,
        out_shape=(pltpu.SemaphoreType.DMA((3 * n_arr,)), pltpu.SemaphoreType.DMA((3 * n_arr,)),
                   *[pltpu.HBM(f, BF16) for f in fulls], jax.ShapeDtypeStruct((8, 128), F32)),
        in_specs=[HBM_SPEC] * n_arr + [ANY_SPEC],
        out_specs=(SEM_SPEC, SEM_SPEC, *[HBM_SPEC] * n_arr, pl.BlockSpec(memory_space=pltpu.VMEM)),
        input_output_aliases={a: 2 + a for a in range(n_arr)},
        compiler_params=SPLIT_COPY_PARAMS,
    )(*_in_hbm(lands), after)
    return res[0], res[1], list(res[2:2 + n_arr]), res[-1]


def _gather_wait(send_sems, recv_sems, lands, axes, name, after):
    n_arr = len(lands)
    fulls = [tuple(l.shape) for l in lands]

    def body(*refs):
        land = refs[:n_arr]
        s_sems, r_sems = refs[n_arr:n_arr + 2]
        x, y, c, chips = _place()
        for a in range(n_arr):
            for j, chip in enumerate(chips):
                k_j = 2 * chip[0] + chip[1]
                got = _shard_region(land[a], fulls[a], axes[a], k_j, c)
                cp = pltpu.make_async_remote_copy(
                    src_ref=got, dst_ref=got, send_sem=s_sems.at[a * 3 + j], recv_sem=r_sems.at[a * 3 + j],
                    device_id=(x, y, c), device_id_type=MESH)
                cp.wait_send()
                cp.wait_recv()

    res = pl.pallas_call(
        body, name=name,
        out_shape=tuple(pltpu.HBM(f, BF16) for f in fulls),
        in_specs=[HBM_SPEC] * n_arr + [SEM_SPEC, SEM_SPEC, ANY_SPEC],
        out_specs=tuple([HBM_SPEC] * n_arr),
        input_output_aliases={a: a for a in range(n_arr)},
        compiler_params=SPLIT_COPY_PARAMS,
    )(*lands, send_sems, recv_sems, after)
    return list(res)


def _gather_share(lands, axes, name):
    n_arr = len(lands)
    fulls = [tuple(l.shape) for l in lands]

    def body(*refs):
        land_in, land = refs[:n_arr], refs[n_arr:2 * n_arr]
        send_sems, recv_sems = refs[2 * n_arr:]
        x, y, c, chips = _place()
        copies = []
        for a in range(n_arr):
            for j, chip in enumerate(chips):
                k_j = 2 * chip[0] + chip[1]
                cp = pltpu.make_async_remote_copy(
                    src_ref=_shard_region(land_in[a], fulls[a], axes[a], k_j, c),
                    dst_ref=_shard_region(land[a], fulls[a], axes[a], k_j, c),
                    send_sem=send_sems.at[a * 3 + j], recv_sem=recv_sems.at[a * 3 + j],
                    device_id=(x, y, 1 - c), device_id_type=MESH)
                cp.start()
                copies.append(cp)
        for cp in copies:
            cp.wait()

    return pl.pallas_call(
        body, name=name, out_shape=[jax.ShapeDtypeStruct(f, BF16) for f in fulls],
        in_specs=[ANY_SPEC] * n_arr, out_specs=[ANY_SPEC] * n_arr,
        input_output_aliases={a: a for a in range(n_arr)},
        scratch_shapes=[pltpu.SemaphoreType.DMA((3 * n_arr,)), pltpu.SemaphoreType.DMA((3 * n_arr,))],
    )(*lands)


def _scatter_shapes(grads, axes):
    out = []
    for g, ax in zip(grads, axes):
        shp = list(g.shape)
        shp[ax] //= 4
        out.append((3,) + tuple(shp[1:]))
    return out


def _scatter_start(grads, axes, name, after):
    n_arr = len(grads)
    shapes = _scatter_shapes(grads, axes)
    lands = [lax.empty(s, BF16) for s in shapes]

    def body(*refs):
        ins, land = refs[:n_arr], refs[n_arr:2 * n_arr]
        send_sems, recv_sems = refs[2 * n_arr + 1:2 * n_arr + 3]
        token = refs[-1]
        x, y, c, chips = _place()
        for a in range(n_arr):
            for j, chip in enumerate(chips):
                k_j = 2 * chip[0] + chip[1]
                pltpu.make_async_remote_copy(
                    src_ref=_shard_region(ins[a], grads[a].shape, axes[a], k_j), dst_ref=land[a].at[pl.ds(j, 1)],
                    send_sem=send_sems.at[a * 3 + j], recv_sem=recv_sems.at[a * 3 + j],
                    device_id=(*chip, c), device_id_type=MESH).start()
        token[...] = jnp.zeros_like(token)

    res = pl.pallas_call(
        body, name=name,
        out_shape=(pltpu.SemaphoreType.DMA((3 * n_arr,)), pltpu.SemaphoreType.DMA((3 * n_arr,)),
                   *[pltpu.HBM(g.shape, BF16) for g in grads], *[pltpu.HBM(s, BF16) for s in shapes],
                   jax.ShapeDtypeStruct((8, 128), F32)),
        in_specs=[HBM_SPEC] * (2 * n_arr) + [ANY_SPEC],
        out_specs=(SEM_SPEC, SEM_SPEC, *[HBM_SPEC] * (2 * n_arr), pl.BlockSpec(memory_space=pltpu.VMEM)),
        input_output_aliases={a: 2 + a for a in range(2 * n_arr)},
        compiler_params=SPLIT_COPY_PARAMS,
    )(*_in_hbm(grads), *_in_hbm(lands), after)
    return res[0], res[1], list(res[2:2 + n_arr]), list(res[2 + n_arr:2 + 2 * n_arr]), res[-1]


def _scatter_wait(send_sems, recv_sems, grads, lands, axes, name, after):
    n_arr = len(grads)

    def body(*refs):
        ins, land = refs[:n_arr], refs[n_arr:2 * n_arr]
        s_sems, r_sems = refs[2 * n_arr:2 * n_arr + 2]
        x, y, c, chips = _place()
        for a in range(n_arr):
            for j, chip in enumerate(chips):
                k_j = 2 * chip[0] + chip[1]
                cp = pltpu.make_async_remote_copy(
                    src_ref=_shard_region(ins[a], grads[a].shape, axes[a], k_j), dst_ref=land[a].at[pl.ds(j, 1)],
                    send_sem=s_sems.at[a * 3 + j], recv_sem=r_sems.at[a * 3 + j],
                    device_id=(x, y, c), device_id_type=MESH)
                cp.wait_send()
                cp.wait_recv()

    res = pl.pallas_call(
        body, name=name,
        out_shape=(*[pltpu.HBM(g.shape, BF16) for g in grads], *[pltpu.HBM(l.shape, BF16) for l in lands]),
        in_specs=[HBM_SPEC] * (2 * n_arr) + [SEM_SPEC, SEM_SPEC, ANY_SPEC],
        out_specs=tuple([HBM_SPEC] * (2 * n_arr)),
        input_output_aliases={a: a for a in range(2 * n_arr)},
        compiler_params=SPLIT_COPY_PARAMS,
    )(*grads, *lands, send_sems, recv_sems, after)
    return list(res[:n_arr]), list(res[n_arr:])


def _swap_with_sibling(arrs, name):
    n_arr = len(arrs)

    def body(*refs):
        ins, outs = refs[:n_arr], refs[n_arr:2 * n_arr]
        send_sems, recv_sems = refs[2 * n_arr:]
        x, y, c, _ = _place()
        copies = []
        for a in range(n_arr):
            cp = pltpu.make_async_remote_copy(
                src_ref=ins[a], dst_ref=outs[a], send_sem=send_sems.at[a], recv_sem=recv_sems.at[a],
                device_id=(x, y, 1 - c), device_id_type=MESH)
            cp.start()
            copies.append(cp)
        for cp in copies:
            cp.wait()

    any_spec = pl.BlockSpec(memory_space=pl.ANY)
    return pl.pallas_call(
        body, name=name, out_shape=[jax.ShapeDtypeStruct(a.shape, a.dtype) for a in arrs],
        in_specs=[any_spec] * n_arr, out_specs=[any_spec] * n_arr,
        scratch_shapes=[pltpu.SemaphoreType.DMA((n_arr,)), pltpu.SemaphoreType.DMA((n_arr,))],
    )(*arrs)


def _mm_tn(a, b, name, out_dtype=BF16):
    L, m = a.shape
    n = b.shape[1]
    bm, bn, bk = min(m, 1024), min(n, 1024), min(L, 2048)
    nk = L // bk

    def body(a_ref, b_ref, o_ref, acc):
        k = pl.program_id(2)

        @pl.when(k == 0)
        def _():
            acc[...] = jnp.zeros_like(acc)

        acc[...] += _dot_tn(_bf(a_ref[...]), _bf(b_ref[...]))

        @pl.when(k == nk - 1)
        def _():
            o_ref[...] = acc[...].astype(out_dtype)

    return pl.pallas_call(
        body, name=name, grid=(m // bm, n // bn, nk),
        in_specs=[pl.BlockSpec((bk, bm), lambda i, j, k: (k, i)), pl.BlockSpec((bk, bn), lambda i, j, k: (k, j))],
        out_specs=pl.BlockSpec((bm, bn), lambda i, j, k: (i, j)),
        out_shape=jax.ShapeDtypeStruct((m, n), out_dtype),
        scratch_shapes=[pltpu.VMEM((bm, bn), F32)],
        compiler_params=_params(3),
    )(a, b)


def _mm_tn_blocks(a, b, wa, wb, name):
    L = a.shape[0]
    nb = a.shape[1] // wa
    bk = min(L, 1024)
    nk = L // bk

    def body(a_ref, b_ref, o_ref):
        @pl.when(pl.program_id(1) == 0)
        def _():
            o_ref[...] = jnp.zeros_like(o_ref)

        o_ref[...] += _dot_tn(_bf(a_ref[...]), _bf(b_ref[...]))

    return pl.pallas_call(
        body, name=name, grid=(nb, nk),
        in_specs=[pl.BlockSpec((bk, wa), lambda j, k: (k, j)), pl.BlockSpec((bk, wb), lambda j, k: (k, j))],
        out_specs=pl.BlockSpec((None, wa, wb), lambda j, k: (j, 0, 0)),
        out_shape=jax.ShapeDtypeStruct((nb, wa, wb), F32),
        compiler_params=_params(2),
    )(a, b)


def _sum_parts(parts, own, axis, chip, name):
    _, r, c = parts.shape
    tr = min(r, 256)
    if axis == 1:
        own_spec = pl.BlockSpec((None, tr, c), lambda i, k: (0, k[0] * (r // tr) + i, 0))
    else:
        own_spec = pl.BlockSpec((None, tr, c), lambda i, k: (0, i, k[0]))

    def body(k_ref, p_ref, g_ref, o_ref):
        p = p_ref[...].astype(F32)
        o_ref[...] = ((p[0] + p[1]) + p[2]) + g_ref[...].astype(F32)

    return pl.pallas_call(
        body, name=name,
        grid_spec=pltpu.PrefetchScalarGridSpec(
            num_scalar_prefetch=1, grid=(r // tr,),
            in_specs=[pl.BlockSpec((3, tr, c), lambda i, k: (0, i, 0)), own_spec],
            out_specs=pl.BlockSpec((tr, c), lambda i, k: (i, 0))),
        out_shape=jax.ShapeDtypeStruct((r, c), F32),
        compiler_params=_params(1),
    )(chip, parts, own)


def _adamw(w, g_parts, m, v, name):
    n_g = len(g_parts)
    if w.ndim == 2:
        r, c = w.shape
        tr = r
        for cand in (512, 256, 128, 64, 32, 16, 8):
            if r % cand == 0 and cand * c * 4 <= (2 << 20):
                tr = cand
                break
        spec = pl.BlockSpec((tr, c), lambda i: (i, 0))
        tiling = dict(grid=(r // tr,), in_specs=[spec] * (3 + n_g), out_specs=[spec] * 4, compiler_params=_params(1))
    else:
        tiling = dict(compiler_params=pltpu.CompilerParams(vmem_limit_bytes=48 << 20))
    c1 = 1.0 / (1.0 - ADAM_B1 ** ADAM_STEP)
    c2 = 1.0 / (1.0 - ADAM_B2 ** ADAM_STEP)

    def body(*refs):
        w_ref, g_refs, m_ref, v_ref = refs[0], refs[1:1 + n_g], refs[1 + n_g], refs[2 + n_g]
        g_out, d_out, m_out, v_out = refs[3 + n_g:]
        g = g_refs[0][...]
        for gr in g_refs[1:]:
            g = g + gr[...]
        m_new = ADAM_B1 * m_ref[...] + (1.0 - ADAM_B1) * g
        v_new = ADAM_B2 * v_ref[...] + (1.0 - ADAM_B2) * (g * g)
        m_hat = m_new * c1
        v_hat = v_new * c2
        g_out[...] = g
        d_out[...] = -ADAM_LR * (m_hat / (jnp.sqrt(v_hat) + ADAM_EPS) + ADAM_WD * w_ref[...])
        m_out[...] = m_new
        v_out[...] = v_new

    return pl.pallas_call(body, name=name, out_shape=[jax.ShapeDtypeStruct(w.shape, F32)] * 4, **tiling)(w, *g_parts, m, v)


def _ada_fwd(c16, ada_w, ada_b_cols):
    cols = ada_w.shape[2]

    def body(c_ref, w_ref, b_ref, o_ref):
        cv = c_ref[...]
        ca = _bf(cv * jax.nn.sigmoid(cv))
        o_ref[...] = _dot(ca, _bf(w_ref[...])) + b_ref[...]

    return pl.pallas_call(
        body, name="ada_fwd", grid=(DEPTH,),
        in_specs=[_whole((16, D)), pl.BlockSpec((None, D, cols), lambda i: (i, 0, 0)),
                  pl.BlockSpec((None, 1, cols), lambda i: (i, 0, 0))],
        out_specs=pl.BlockSpec((None, 16, cols), lambda i: (i, 0, 0)),
        out_shape=jax.ShapeDtypeStruct((DEPTH, 16, cols), F32),
        compiler_params=_params(1),
    )(c16, ada_w, ada_b_cols)


def _ada_bwd(c16, dmod16):
    cols = dmod16.shape[2]

    def body(c_ref, d_ref, o_ref):
        cv = c_ref[...]
        ca = _bf(cv * jax.nn.sigmoid(cv))
        o_ref[...] = _dot_tn(ca, _bf(d_ref[...]))

    return pl.pallas_call(
        body, name="ada_bwd", grid=(DEPTH,),
        in_specs=[_whole((16, D)), pl.BlockSpec((None, 16, cols), lambda i: (i, 0, 0))],
        out_specs=pl.BlockSpec((None, D, cols), lambda i: (i, 0, 0)),
        out_shape=jax.ShapeDtypeStruct((DEPTH, D, cols), F32),
        compiler_params=_params(1),
    )(c16, dmod16)


def _mod_bwd(vs_mix, vs_ffn, pv):
    def body(m_ref, f_ref, pv_ref, o_ref):
        for i in range(DEPTH):
            vm, vf, p = m_ref[i], f_ref[i], pv_ref[i]
            o_ref[i] = jnp.concatenate([
                vm[2:3], vm[1:2] * p[R_N1:R_N1 + 1], vm[0:1],
                vf[2:3], vf[1:2] * p[R_N2:R_N2 + 1], vf[0:1],
                vm[1:2] * (1.0 + p[R_SC1:R_SC1 + 1]), vf[1:2] * (1.0 + p[R_SC2:R_SC2 + 1])], axis=0)

    return pl.pallas_call(body, name="mod_bwd", out_shape=jax.ShapeDtypeStruct((DEPTH, 8, D), F32))(vs_mix, vs_ffn, pv)


def _ffn_fwd(x1, pv, w1, w2, layer, tm, after=None):
    L = x1.shape[0]
    dff = w1.shape[2]

    def body(x1_ref, pv_ref, w1_ref, w2_ref, x2_ref, h2_ref, a_ref, f_ref):
        x1v, p = x1_ref[...], pv_ref[...]
        h2, _, _ = _norm_mod(x1v, p[R_N2:R_N2 + 1], p[R_SC2:R_SC2 + 1], p[R_SH2:R_SH2 + 1])
        hb = _bf(h2)
        h2_ref[...] = hb
        a = _dot(hb, w1_ref[...])
        a_ref[...] = a
        ra = jnp.maximum(a, 0.0)
        f = _dot(_bf(ra * ra), w2_ref[...])
        f_ref[...] = f
        x2_ref[...] = x1v + p[R_G2:R_G2 + 1] * f

    call, tail = _call_after(
        after, body, 4, name=f"ffn_fwd{layer}", grid=(L // tm,),
        in_specs=[_rows(tm, D), pl.BlockSpec((None, 8, D), lambda i: (layer, 0, 0)), _layer_w(D, dff, 0), _layer_w(dff, D, 0)],
        out_specs=[_rows(tm, D), _rows(tm, D), _rows(tm, dff), _rows(tm, D)],
        out_shape=[jax.ShapeDtypeStruct((L, D), F32), jax.ShapeDtypeStruct((L, D), BF16),
                   jax.ShapeDtypeStruct((L, dff), F32), jax.ShapeDtypeStruct((L, D), F32)],
        compiler_params=_params(1, 56),
    )
    return call(x1, pv, w1, w2, *tail)


def _ffn_bwd(dx2, x1, a, f, pv, w1, w2, layer, tm, after=None):
    L = x1.shape[0]
    dff = w1.shape[2]
    extra = [] if after is None else [pl.BlockSpec(memory_space=pl.ANY)]
    extra_args = [] if after is None else [after]

    def body(dx2_ref, x1_ref, a_ref, f_ref, pv_ref, w1_ref, w2_ref, *rest):
        dx1_ref, p_ref, da_ref, df_ref, vs_ref = rest[len(extra):]

        @pl.when(pl.program_id(0) == 0)
        def _():
            vs_ref[...] = jnp.zeros_like(vs_ref)

        dx2v, p = dx2_ref[...], pv_ref[...]
        dfb = _bf(dx2v * p[R_G2:R_G2 + 1])
        df_ref[...] = dfb
        vs_ref[0:1, :] += _sum0(dx2v * f_ref[...])
        dp = _dot_nt(dfb, w2_ref[...])
        ra = jnp.maximum(a_ref[...], 0.0)
        p_ref[...] = _bf(ra * ra)
        dab = _bf(dp * (2.0 * ra))
        da_ref[...] = dab
        dh2 = _dot_nt(dab, w1_ref[...])
        _, xn, r = _norm_mod(x1_ref[...], p[R_N2:R_N2 + 1], p[R_SC2:R_SC2 + 1], p[R_SH2:R_SH2 + 1])
        dx1_ref[...] = dx2v + _norm_mod_bwd(dh2, xn, r, p[R_N2:R_N2 + 1], p[R_SC2:R_SC2 + 1])
        vs_ref[1:2, :] += _sum0(dh2 * xn)
        vs_ref[2:3, :] += _sum0(dh2)

    return pl.pallas_call(
        body, name=f"ffn_bwd{layer}", grid=(L // tm,),
        in_specs=[_rows(tm, D), _rows(tm, D), _rows(tm, dff), _rows(tm, D),
                  pl.BlockSpec((None, 8, D), lambda i: (layer, 0, 0)), _layer_w(D, dff, 0), _layer_w(dff, D, 0)] + extra,
        out_specs=[_rows(tm, D), _rows(tm, dff), _rows(tm, dff), _rows(tm, D), _whole((8, D))],
        out_shape=[jax.ShapeDtypeStruct((L, D), F32), jax.ShapeDtypeStruct((L, dff), BF16),
                   jax.ShapeDtypeStruct((L, dff), BF16), jax.ShapeDtypeStruct((L, D), BF16),
                   jax.ShapeDtypeStruct((8, D), F32)],
        compiler_params=_params(1, 56),
    )(dx2, x1, a, f, pv, w1, w2, *extra_args)


def _conv_fwd(x, pv, w_in, w_out, cw, layer, j, tm, after=None):
    L = x.shape[0]

    def body(x_ref, pv_ref, win_ref, wout_ref, cw_ref, x1_ref, h_ref, bcx_ref, conv_ref, q_ref, y_ref, carry):
        @pl.when(pl.program_id(0) == 0)
        def _():
            carry[...] = jnp.zeros_like(carry)

        xv, p, cwv = x_ref[...], pv_ref[...], cw_ref[...]
        h, _, _ = _norm_mod(xv, p[R_N1:R_N1 + 1], p[R_SC1:R_SC1 + 1], p[R_SH1:R_SH1 + 1])
        hb = _bf(h)
        h_ref[...] = hb
        bcx = _dot(hb, win_ref[...])
        bcx_ref[...] = bcx
        z = bcx[:, D:2 * D] * bcx[:, 2 * D:]
        prev8 = carry[...]
        conv = cwv[0:1] * _shift_down(z, prev8, 2) + cwv[1:2] * _shift_down(z, prev8, 1) + cwv[2:3] * z + cwv[3:4]
        conv_ref[...] = conv
        qb = _bf(bcx[:, :D] * conv)
        q_ref[...] = qb
        y = _dot(qb, wout_ref[...])
        y_ref[...] = y
        x1_ref[...] = xv + p[R_G1:R_G1 + 1] * y
        carry[...] = z[tm - 8:tm]

    call, tail = _call_after(
        after, body, 5, name=f"conv_fwd{layer}", grid=(L // tm,),
        in_specs=[_rows(tm, D), pl.BlockSpec((None, 8, D), lambda i: (layer, 0, 0)), _layer_w(D, 3 * D, 0), _layer_w(D, D, 0),
                  pl.BlockSpec((None, 8, D), lambda i: (j, 0, 0))],
        out_specs=[_rows(tm, D), _rows(tm, D), _rows(tm, 3 * D), _rows(tm, D), _rows(tm, D), _rows(tm, D)],
        out_shape=[jax.ShapeDtypeStruct((L, D), F32), jax.ShapeDtypeStruct((L, D), BF16), jax.ShapeDtypeStruct((L, 3 * D), F32),
                   jax.ShapeDtypeStruct((L, D), F32), jax.ShapeDtypeStruct((L, D), BF16), jax.ShapeDtypeStruct((L, D), F32)],
        scratch_shapes=[pltpu.VMEM((8, D), F32)],
        compiler_params=_params(1, 56),
    )
    return call(x, pv, w_in, w_out, cw, *tail)


def _conv_bwd(dx1, x, y, bcx, conv, pv, w_in, w_out, cw, layer, j, tm, after=None):
    L = x.shape[0]
    nt = L // tm

    def body(dx1_ref, x_ref, y_ref, bcx_ref, conv_ref, halo_ref, pv_ref, win_ref, wout_ref, cw_ref,
             dx_ref, dbcx_ref, dy_ref, vs_ref, carry):
        gi = pl.program_id(0)
        tile = nt - 1 - gi

        @pl.when(gi == 0)
        def _():
            vs_ref[...] = jnp.zeros_like(vs_ref)
            carry[...] = jnp.zeros_like(carry)

        dx1v, p, cwv = dx1_ref[...], pv_ref[...], cw_ref[...]
        dyb = _bf(dx1v * p[R_G1:R_G1 + 1])
        dy_ref[...] = dyb
        vs_ref[0:1, :] += _sum0(dx1v * y_ref[...])
        dq = _dot_nt(dyb, wout_ref[...])
        bcx = bcx_ref[...]
        b, cg, xh = bcx[:, :D], bcx[:, D:2 * D], bcx[:, 2 * D:]
        db = dq * conv_ref[...]
        dc = dq * b
        z = cg * xh
        halo = halo_ref[...]
        zprev = jnp.where(tile > 0, halo[:, D:2 * D] * halo[:, 2 * D:], 0.0)
        vs_ref[3:4, :] += _sum0(dc * _shift_down(z, zprev, 2))
        vs_ref[4:5, :] += _sum0(dc * _shift_down(z, zprev, 1))
        vs_ref[5:6, :] += _sum0(dc * z)
        vs_ref[6:7, :] += _sum0(dc)
        next8 = carry[...]
        dz = cwv[2:3] * dc + cwv[1:2] * _shift_up(dc, next8, 1) + cwv[0:1] * _shift_up(dc, next8, 2)
        dbb, dcgb, dxhb = _bf(db), _bf(dz * xh), _bf(dz * cg)
        dbcx_ref[:, 0:D] = dbb
        dbcx_ref[:, D:2 * D] = dcgb
        dbcx_ref[:, 2 * D:3 * D] = dxhb
        dh = (_dot_nt(dbb, win_ref[:, 0:D]) + _dot_nt(dcgb, win_ref[:, D:2 * D])) + _dot_nt(dxhb, win_ref[:, 2 * D:3 * D])
        _, xn, r = _norm_mod(x_ref[...], p[R_N1:R_N1 + 1], p[R_SC1:R_SC1 + 1], p[R_SH1:R_SH1 + 1])
        dx_ref[...] = dx1v + _norm_mod_bwd(dh, xn, r, p[R_N1:R_N1 + 1], p[R_SC1:R_SC1 + 1])
        vs_ref[1:2, :] += _sum0(dh * xn)
        vs_ref[2:3, :] += _sum0(dh)
        carry[...] = dc[0:8]

    halo_spec = pl.BlockSpec((8, 3 * D), lambda i: (jnp.maximum((nt - 1 - i) * (tm // 8) - 1, 0), 0))
    call, tail = _call_after(
        after, body, 10, name=f"conv_bwd{layer}", grid=(nt,),
        in_specs=[_rows(tm, D, nt), _rows(tm, D, nt), _rows(tm, D, nt), _rows(tm, 3 * D, nt), _rows(tm, D, nt), halo_spec,
                  pl.BlockSpec((None, 8, D), lambda i: (layer, 0, 0)), _layer_w(D, 3 * D, 0), _layer_w(D, D, 0),
                  pl.BlockSpec((None, 8, D), lambda i: (j, 0, 0))],
        out_specs=[_rows(tm, D, nt), _rows(tm, 3 * D, nt), _rows(tm, D, nt), _whole((8, D))],
        out_shape=[jax.ShapeDtypeStruct((L, D), F32), jax.ShapeDtypeStruct((L, 3 * D), BF16),
                   jax.ShapeDtypeStruct((L, D), BF16), jax.ShapeDtypeStruct((8, D), F32)],
        scratch_shapes=[pltpu.VMEM((8, D), F32)],
        compiler_params=_params(1, 56),
    )
    return call(dx1, x, y, bcx, conv, bcx, pv, w_in, w_out, cw, *tail)


def _s5_discretize(a_re, a_im, log_dt, bt_re, bt_im):
    dt = jnp.exp(log_dt)
    mag = jnp.exp(a_re * dt)
    abar_re = mag * jnp.cos(a_im * dt)
    abar_im = mag * jnp.sin(a_im * dt)
    den = a_re * a_re + a_im * a_im
    nr = abar_re - 1.0
    ni = abar_im
    f_re = (nr * a_re + ni * a_im) / den
    f_im = (ni * a_re - nr * a_im) / den
    bbar_re = f_re * bt_re - f_im * bt_im
    bbar_im = f_re * bt_im + f_im * bt_re
    return abar_re, abar_im, bbar_re, bbar_im


def _s5_params_fwd(a_re, a_im, log_dt, bt_re, bt_im):
    def body(ar, ai, ld, br, bi, o_ar, o_ai, o_br, o_bi):
        r = _s5_discretize(ar[...], ai[...], ld[...], br[...], bi[...])
        o_ar[...], o_ai[...], o_br[...], o_bi[...] = r

    gp = jax.ShapeDtypeStruct((S5_G, S5_P), F32)
    hgp = jax.ShapeDtypeStruct((S5_H, S5_G, S5_P), F32)
    return pl.pallas_call(body, name="s5_params_fwd", out_shape=[gp, gp, hgp, hgp])(a_re, a_im, log_dt, bt_re, bt_im)


def _s5_params_bwd(a_re, a_im, log_dt, bt_re, bt_im, d_ar, d_ai, d_br, d_bi):
    def body(ar, ai, ld, br, bi, gar, gai, gbr, gbi, o_ar, o_ai, o_ld, o_br, o_bi):
        _, vjp = jax.vjp(_s5_discretize, ar[...], ai[...], ld[...], br[...], bi[...])
        r = vjp((gar[...], gai[...], gbr[...], gbi[...]))
        o_ar[...], o_ai[...], o_ld[...], o_br[...], o_bi[...] = r

    gp = jax.ShapeDtypeStruct((S5_G, S5_P), F32)
    hgp = jax.ShapeDtypeStruct((S5_H, S5_G, S5_P), F32)
    return pl.pallas_call(body, name="s5_params_bwd", out_shape=[gp, gp, jax.ShapeDtypeStruct((S5_G, 1), F32), hgp, hgp])(
        a_re, a_im, log_dt, bt_re, bt_im, d_ar, d_ai, d_br, d_bi)


def _s5_in_fwd(x, pv, w_in, b_re, b_im, layer, tm, after=None):
    L = x.shape[0]

    def body(x_ref, pv_ref, win_ref, bre_ref, bim_ref, h_ref, u_ref, ore_ref, oim_ref):
        p = pv_ref[...]
        h, _, _ = _norm_mod(x_ref[...], p[R_N1:R_N1 + 1], p[R_SC1:R_SC1 + 1], p[R_SH1:R_SH1 + 1])
        hb = _bf(h)
        h_ref[...] = hb
        u = _dot(hb, win_ref[...])
        u_ref[...] = u
        ub = _bf(u)
        for k in range(S5_NB):
            uk = ub[:, k * S5_BH:(k + 1) * S5_BH]
            ore_ref[:, k * S5_BP:(k + 1) * S5_BP] = _dot(uk, bre_ref[k])
            oim_ref[:, k * S5_BP:(k + 1) * S5_BP] = _dot(uk, bim_ref[k])

    call, tail = _call_after(
        after, body, 5, name="s5_in_fwd", grid=(L // tm,),
        in_specs=[_rows(tm, D), pl.BlockSpec((None, 8, D), lambda i: (layer, 0, 0)), _layer_w(D, D, 0),
                  _const_w((S5_NB, S5_BH, S5_BP)), _const_w((S5_NB, S5_BH, S5_BP))],
        out_specs=[_rows(tm, D), _rows(tm, D), _rows(tm, NSTATE), _rows(tm, NSTATE)],
        out_shape=[jax.ShapeDtypeStruct((L, D), BF16), jax.ShapeDtypeStruct((L, D), F32),
                   jax.ShapeDtypeStruct((L, NSTATE), F32), jax.ShapeDtypeStruct((L, NSTATE), F32)],
        compiler_params=_params(1, 56),
    )
    return call(x, pv, w_in, b_re, b_im, *tail)


def _s5_scan_fwd(bu_re, bu_im, ar, ai, tr):
    L = bu_re.shape[0]
    nl = 1024

    def body(bre_ref, bim_ref, ar_ref, ai_ref, sre_ref, sim_ref, st_re, st_im):
        @pl.when(pl.program_id(1) == 0)
        def _():
            st_re[...] = jnp.zeros_like(st_re)
            st_im[...] = jnp.zeros_like(st_im)

        a_r, a_i = ar_ref[...], ai_ref[...]

        def step(t, carry):
            s_r, s_i = carry
            n_r = a_r * s_r - a_i * s_i + bre_ref[pl.ds(t, 1), :]
            n_i = a_r * s_i + a_i * s_r + bim_ref[pl.ds(t, 1), :]
            sre_ref[pl.ds(t, 1), :] = n_r
            sim_ref[pl.ds(t, 1), :] = n_i
            return n_r, n_i

        s_r, s_i = lax.fori_loop(0, tr, step, (st_re[...], st_im[...]), unroll=8)
        st_re[...] = s_r
        st_im[...] = s_i

    blk = pl.BlockSpec((tr, nl), lambda j, i: (i, j))
    vec = pl.BlockSpec((1, nl), lambda j, i: (0, j))
    return pl.pallas_call(
        body, name="s5_scan_fwd", grid=(NSTATE // nl, L // tr),
        in_specs=[blk, blk, vec, vec], out_specs=[blk, blk],
        out_shape=[jax.ShapeDtypeStruct((L, NSTATE), F32)] * 2,
        scratch_shapes=[pltpu.VMEM((1, nl), F32), pltpu.VMEM((1, nl), F32)],
        compiler_params=_params(2),
    )(bu_re, bu_im, ar, ai)


def _s5_out_fwd(x, u, s_re, s_im, pv, c_re, c_im, dvec, glu_w, glu_b, w_out, layer, tm):
    L = x.shape[0]

    def body(x_ref, u_ref, sre_ref, sim_ref, pv_ref, cre_ref, cim_ref, d_ref, gw_ref, gb_ref, wout_ref,
             x1_ref, y1_ref, zg_ref, y3_ref, y_ref):
        p = pv_ref[...]
        srb, sib = _bf(sre_ref[...]), _bf(sim_ref[...])
        parts = []
        for k in range(S5_NB):
            sl = slice(k * S5_BP, (k + 1) * S5_BP)
            parts.append(_dot(srb[:, sl], cre_ref[k]) - _dot(sib[:, sl], cim_ref[k]))
        y1 = jnp.concatenate(parts, axis=1) + d_ref[...] * u_ref[...]
        y1_ref[...] = y1
        y2 = jax.nn.gelu(y1)
        zg = _dot(_bf(y2), gw_ref[...]) + gb_ref[...]
        zg_ref[...] = zg
        y3b = _bf(y2 * jax.nn.sigmoid(zg))
        y3_ref[...] = y3b
        y = _dot(y3b, wout_ref[...])
        y_ref[...] = y
        x1_ref[...] = x_ref[...] + p[R_G1:R_G1 + 1] * y

    return pl.pallas_call(
        body, name="s5_out_fwd", grid=(L // tm,),
        in_specs=[_rows(tm, D), _rows(tm, D), _rows(tm, NSTATE), _rows(tm, NSTATE),
                  pl.BlockSpec((None, 8, D), lambda i: (layer, 0, 0)),
                  _const_w((S5_NB, S5_BP, S5_BH)), _const_w((S5_NB, S5_BP, S5_BH)), _whole((1, D)),
                  _layer_w(D, D, 0), _whole((1, D)), _layer_w(D, D, 0)],
        out_specs=[_rows(tm, D)] * 5,
        out_shape=[jax.ShapeDtypeStruct((L, D), F32), jax.ShapeDtypeStruct((L, D), F32), jax.ShapeDtypeStruct((L, D), F32),
                   jax.ShapeDtypeStruct((L, D), BF16), jax.ShapeDtypeStruct((L, D), F32)],
        compiler_params=_params(1, 56),
    )(x, u, s_re, s_im, pv, c_re, c_im, dvec, glu_w, glu_b, w_out)


def _s5_out_bwd(dx1, y, y1, zg, u, pv, c_re, c_im, dvec, glu_w, w_out, layer, tm, after=None):
    L = dx1.shape[0]

    def body(dx1_ref, y_ref, y1_ref, zg_ref, u_ref, pv_ref, cre_ref, cim_ref, d_ref, gw_ref, wout_ref,
             dy_ref, y2_ref, dzg_ref, dy1_ref, dus_ref, gre_ref, gim_ref, vs_ref):
        @pl.when(pl.program_id(0) == 0)
        def _():
            vs_ref[...] = jnp.zeros_like(vs_ref)

        dx1v, p = dx1_ref[...], pv_ref[...]
        dyb = _bf(dx1v * p[R_G1:R_G1 + 1])
        dy_ref[...] = dyb
        vs_ref[0:1, :] += _sum0(dx1v * y_ref[...])
        dy3 = _dot_nt(dyb, wout_ref[...])
        y2, gelu_vjp = jax.vjp(jax.nn.gelu, y1_ref[...])
        y2_ref[...] = _bf(y2)
        gate = jax.nn.sigmoid(zg_ref[...])
        dzg = dy3 * y2 * gate * (1.0 - gate)
        dzgb = _bf(dzg)
        dzg_ref[...] = dzgb
        vs_ref[1:2, :] += _sum0(dzg)
        dy2 = dy3 * gate + _dot_nt(dzgb, gw_ref[...])
        dy1 = gelu_vjp(dy2)[0]
        vs_ref[2:3, :] += _sum0(dy1 * u_ref[...])
        dus_ref[...] = dy1 * d_ref[...]
        dy1b = _bf(dy1)
        dy1_ref[...] = dy1b
        for k in range(S5_NB):
            dk = dy1b[:, k * S5_BH:(k + 1) * S5_BH]
            gre_ref[:, k * S5_BP:(k + 1) * S5_BP] = _dot_nt(dk, cre_ref[k])
            gim_ref[:, k * S5_BP:(k + 1) * S5_BP] = -_dot_nt(dk, cim_ref[k])

    call, tail = _call_after(
        after, body, 11, name="s5_out_bwd", grid=(L // tm,),
        in_specs=[_rows(tm, D)] * 5 + [pl.BlockSpec((None, 8, D), lambda i: (layer, 0, 0)),
                  _const_w((S5_NB, S5_BP, S5_BH)), _const_w((S5_NB, S5_BP, S5_BH)), _whole((1, D)),
                  _layer_w(D, D, 0), _layer_w(D, D, 0)],
        out_specs=[_rows(tm, D)] * 5 + [_rows(tm, NSTATE), _rows(tm, NSTATE), _whole((8, D))],
        out_shape=[jax.ShapeDtypeStruct((L, D), BF16)] * 4 + [jax.ShapeDtypeStruct((L, D), F32),
                   jax.ShapeDtypeStruct((L, NSTATE), F32), jax.ShapeDtypeStruct((L, NSTATE), F32),
                   jax.ShapeDtypeStruct((8, D), F32)],
        compiler_params=_params(1, 56),
    )
    return call(dx1, y, y1, zg, u, pv, c_re, c_im, dvec, glu_w, w_out, *tail)


def _s5_scan_bwd(g_re, g_im, s_re, s_im, ar, ai, tr):
    L = g_re.shape[0]
    nl = 1024
    nt = L // tr

    def body(gre_ref, gim_ref, sre_ref, sim_ref, hre_ref, him_ref, ar_ref, ai_ref, lre_ref, lim_ref, da_ref, st_re, st_im):
        gi = pl.program_id(1)
        tile = nt - 1 - gi

        @pl.when(gi == 0)
        def _():
            st_re[...] = jnp.zeros_like(st_re)
            st_im[...] = jnp.zeros_like(st_im)
            da_ref[...] = jnp.zeros_like(da_ref)

        a_r, a_i = ar_ref[...], ai_ref[...]

        def step(k, carry):
            l_r, l_i = carry
            t = tr - 1 - k
            n_r = gre_ref[pl.ds(t, 1), :] + a_r * l_r + a_i * l_i
            n_i = gim_ref[pl.ds(t, 1), :] - a_i * l_r + a_r * l_i
            lre_ref[pl.ds(t, 1), :] = n_r
            lim_ref[pl.ds(t, 1), :] = n_i
            return n_r, n_i

        l_r, l_i = lax.fori_loop(0, tr, step, (st_re[...], st_im[...]), unroll=8)
        st_re[...] = l_r
        st_im[...] = l_i
        lam_r, lam_i = lre_ref[...], lim_ref[...]
        p_r = jnp.where(tile > 0, hre_ref[...], 0.0)
        p_i = jnp.where(tile > 0, him_ref[...], 0.0)
        sp_r = _shift_down(sre_ref[...], p_r, 1)
        sp_i = _shift_down(sim_ref[...], p_i, 1)
        da_ref[0:1, :] += _sum0(lam_r * sp_r + lam_i * sp_i)
        da_ref[1:2, :] += _sum0(lam_i * sp_r - lam_r * sp_i)

    blk = pl.BlockSpec((tr, nl), lambda j, i: (nt - 1 - i, j))
    halo = pl.BlockSpec((8, nl), lambda j, i: (jnp.maximum((nt - 1 - i) * (tr // 8) - 1, 0), j))
    vec = pl.BlockSpec((1, nl), lambda j, i: (0, j))
    return pl.pallas_call(
        body, name="s5_scan_bwd", grid=(NSTATE // nl, nt),
        in_specs=[blk, blk, blk, blk, halo, halo, vec, vec],
        out_specs=[blk, blk, pl.BlockSpec((8, nl), lambda j, i: (0, j))],
        out_shape=[jax.ShapeDtypeStruct((L, NSTATE), F32)] * 2 + [jax.ShapeDtypeStruct((8, NSTATE), F32)],
        scratch_shapes=[pltpu.VMEM((1, nl), F32), pltpu.VMEM((1, nl), F32)],
        compiler_params=_params(2),
    )(g_re, g_im, s_re, s_im, s_re, s_im, ar, ai)


def _s5_in_bwd(dx1, lam_re, lam_im, du_skip, x, pv, b_re, b_im, w_in, layer, tm):
    L = x.shape[0]

    def body(dx1_ref, lre_ref, lim_ref, dus_ref, x_ref, pv_ref, bre_ref, bim_ref, win_ref, dx_ref, du_ref, vs_ref):
        @pl.when(pl.program_id(0) == 0)
        def _():
            vs_ref[...] = jnp.zeros_like(vs_ref)

        p = pv_ref[...]
        lrb, lib = _bf(lre_ref[...]), _bf(lim_ref[...])
        parts = []
        for k in range(S5_NB):
            sl = slice(k * S5_BP, (k + 1) * S5_BP)
            parts.append(_dot_nt(lrb[:, sl], bre_ref[k]) + _dot_nt(lib[:, sl], bim_ref[k]))
        dub = _bf(jnp.concatenate(parts, axis=1) + dus_ref[...])
        du_ref[...] = dub
        dh = _dot_nt(dub, win_ref[...])
        _, xn, r = _norm_mod(x_ref[...], p[R_N1:R_N1 + 1], p[R_SC1:R_SC1 + 1], p[R_SH1:R_SH1 + 1])
        dx_ref[...] = dx1_ref[...] + _norm_mod_bwd(dh, xn, r, p[R_N1:R_N1 + 1], p[R_SC1:R_SC1 + 1])
        vs_ref[1:2, :] += _sum0(dh * xn)
        vs_ref[2:3, :] += _sum0(dh)

    return pl.pallas_call(
        body, name="s5_in_bwd", grid=(L // tm,),
        in_specs=[_rows(tm, D), _rows(tm, NSTATE), _rows(tm, NSTATE), _rows(tm, D), _rows(tm, D),
                  pl.BlockSpec((None, 8, D), lambda i: (layer, 0, 0)),
                  _const_w((S5_NB, S5_BH, S5_BP)), _const_w((S5_NB, S5_BH, S5_BP)), _layer_w(D, D, 0)],
        out_specs=[_rows(tm, D), _rows(tm, D), _whole((8, D))],
        out_shape=[jax.ShapeDtypeStruct((L, D), F32), jax.ShapeDtypeStruct((L, D), BF16), jax.ShapeDtypeStruct((8, D), F32)],
        compiler_params=_params(1, 56),
    )(dx1, lam_re, lam_im, du_skip, x, pv, b_re, b_im, w_in)


NSEG = 8
SCAN_LANES = 1024


def _to_segments(x):
    n, c = x.shape
    return x.reshape(NSEG, n // NSEG, c).transpose(1, 0, 2).reshape(n, c)


def _from_segments(x):
    n, c = x.shape
    return x.reshape(n // NSEG, NSEG, c).transpose(1, 0, 2).reshape(n, c)


def _segment_scan(re_ref, im_ref, st_re, st_im, a_re, a_im, n_slabs, adjoint, write):
    for q in range(NSTATE // SCAN_LANES):
        ls = slice(q * SCAN_LANES, (q + 1) * SCAN_LANES)
        ar = jnp.broadcast_to(a_re[:, ls], (8, SCAN_LANES))
        ai = jnp.broadcast_to(a_im[:, ls], (8, SCAN_LANES))

        def step(k, carry, ls=ls, ar=ar, ai=ai):
            s_r, s_i = carry
            slab = (n_slabs - 1 - k) if adjoint else k
            rows = pl.ds(pl.multiple_of(slab * 8, 8), 8)
            b_r, b_i = re_ref[rows, ls], im_ref[rows, ls]
            if adjoint:
                n_r = b_r + ar * s_r + ai * s_i
                n_i = b_i - ai * s_r + ar * s_i
            else:
                n_r = ar * s_r - ai * s_i + b_r
                n_i = ar * s_i + ai * s_r + b_i
            if write:
                re_ref[rows, ls] = n_r
                im_ref[rows, ls] = n_i
            return n_r, n_i

        s_r, s_i = lax.fori_loop(0, n_slabs, step, (st_re[:, ls], st_im[:, ls]), unroll=4)
        st_re[:, ls] = s_r
        st_im[:, ls] = s_i


def _s5_segment_states(e_re, e_im, ar, ai, seg_len, adjoint):
    def body(ere_ref, eim_ref, ar_ref, ai_ref, ore_ref, oim_ref):
        p_r, p_i = ar_ref[...], ai_ref[...]
        if adjoint:
            p_i = -p_i
        acc_r, acc_i = jnp.ones_like(p_r), jnp.zeros_like(p_r)
        n = seg_len
        while n:
            if n & 1:
                acc_r, acc_i = acc_r * p_r - acc_i * p_i, acc_r * p_i + acc_i * p_r
            n >>= 1
            if n:
                p_r, p_i = p_r * p_r - p_i * p_i, 2.0 * p_r * p_i
        e_r, e_i = ere_ref[...], eim_ref[...]
        s_r, s_i = jnp.zeros_like(acc_r), jnp.zeros_like(acc_r)
        rows_r, rows_i = [None] * NSEG, [None] * NSEG
        order = range(NSEG - 1, -1, -1) if adjoint else range(NSEG)
        for j in order:
            rows_r[j], rows_i[j] = s_r, s_i
            s_r, s_i = (acc_r * s_r - acc_i * s_i + e_r[j:j + 1], acc_r * s_i + acc_i * s_r + e_i[j:j + 1])
        ore_ref[...] = jnp.concatenate(rows_r, axis=0)
        oim_ref[...] = jnp.concatenate(rows_i, axis=0)

    st = jax.ShapeDtypeStruct((NSEG, NSTATE), F32)
    return pl.pallas_call(body, name="s5_segment_states_bwd" if adjoint else "s5_segment_states_fwd", out_shape=[st, st])(
        e_re, e_im, ar, ai)


def _s5_fwd_ends(x, pv, w_in, b_re, b_im, ar, ai, layer, tm, after=None):
    L = x.shape[0]

    def body(x_ref, pv_ref, win_ref, bre_ref, bim_ref, ar_ref, ai_ref, h_ref, u_ref, ere_ref, eim_ref, bu_re, bu_im):
        @pl.when(pl.program_id(0) == 0)
        def _():
            ere_ref[...] = jnp.zeros_like(ere_ref)
            eim_ref[...] = jnp.zeros_like(eim_ref)

        p = pv_ref[...]
        h, _, _ = _norm_mod(x_ref[...], p[R_N1:R_N1 + 1], p[R_SC1:R_SC1 + 1], p[R_SH1:R_SH1 + 1])
        hb = _bf(h)
        h_ref[...] = hb
        u = _dot(hb, win_ref[...])
        u_ref[...] = u
        ub = _bf(u)
        for k in range(S5_NB):
            uk = ub[:, k * S5_BH:(k + 1) * S5_BH]
            bu_re[:, k * S5_BP:(k + 1) * S5_BP] = _dot(uk, bre_ref[k])
            bu_im[:, k * S5_BP:(k + 1) * S5_BP] = _dot(uk, bim_ref[k])
        _segment_scan(bu_re, bu_im, ere_ref, eim_ref, ar_ref[...], ai_ref[...], tm // 8, adjoint=False, write=False)

    call, tail = _call_after(
        after, body, 7, name="s5_fwd_ends", grid=(L // tm,),
        in_specs=[_rows(tm, D), pl.BlockSpec((None, 8, D), lambda i: (layer, 0, 0)), _layer_w(D, D, 0),
                  _const_w((S5_NB, S5_BH, S5_BP)), _const_w((S5_NB, S5_BH, S5_BP)), _whole((1, NSTATE)), _whole((1, NSTATE))],
        out_specs=[_rows(tm, D), _rows(tm, D), _whole((NSEG, NSTATE)), _whole((NSEG, NSTATE))],
        out_shape=[jax.ShapeDtypeStruct((L, D), BF16), jax.ShapeDtypeStruct((L, D), F32),
                   jax.ShapeDtypeStruct((NSEG, NSTATE), F32), jax.ShapeDtypeStruct((NSEG, NSTATE), F32)],
        scratch_shapes=[pltpu.VMEM((tm, NSTATE), F32), pltpu.VMEM((tm, NSTATE), F32)],
        compiler_params=_params(1, 56),
    )
    return call(x, pv, w_in, b_re, b_im, ar, ai, *tail)


def _s5_fwd_out(x, u, pv, b_re, b_im, s0_re, s0_im, ar, ai, c_re, c_im, dvec, glu_w, glu_b, w_out, layer, tm):
    L = x.shape[0]

    def body(x_ref, u_ref, pv_ref, bre_ref, bim_ref, s0re_ref, s0im_ref, ar_ref, ai_ref, cre_ref, cim_ref, d_ref, gw_ref,
             gb_ref, wout_ref, x1_ref, sre_ref, sim_ref, y1_ref, zg_ref, y3_ref, y_ref, st_re, st_im):
        @pl.when(pl.program_id(0) == 0)
        def _():
            st_re[...] = s0re_ref[...]
            st_im[...] = s0im_ref[...]

        p = pv_ref[...]
        uv = u_ref[...]
        ub = _bf(uv)
        for k in range(S5_NB):
            uk = ub[:, k * S5_BH:(k + 1) * S5_BH]
            sre_ref[:, k * S5_BP:(k + 1) * S5_BP] = _dot(uk, bre_ref[k])
            sim_ref[:, k * S5_BP:(k + 1) * S5_BP] = _dot(uk, bim_ref[k])
        _segment_scan(sre_ref, sim_ref, st_re, st_im, ar_ref[...], ai_ref[...], tm // 8, adjoint=False, write=True)
        parts = []
        for k in range(S5_NB):
            sl = slice(k * S5_BP, (k + 1) * S5_BP)
            parts.append(_dot(_bf(sre_ref[:, sl]), cre_ref[k]) - _dot(_bf(sim_ref[:, sl]), cim_ref[k]))
        y1 = jnp.concatenate(parts, axis=1) + d_ref[...] * uv
        y1_ref[...] = y1
        y2 = jax.nn.gelu(y1)
        zg = _dot(_bf(y2), gw_ref[...]) + gb_ref[...]
        zg_ref[...] = zg
        y3b = _bf(y2 * jax.nn.sigmoid(zg))
        y3_ref[...] = y3b
        y = _dot(y3b, wout_ref[...])
        y_ref[...] = y
        x1_ref[...] = x_ref[...] + p[R_G1:R_G1 + 1] * y

    return pl.pallas_call(
        body, name="s5_fwd_out", grid=(L // tm,),
        in_specs=[_rows(tm, D), _rows(tm, D), pl.BlockSpec((None, 8, D), lambda i: (layer, 0, 0)),
                  _const_w((S5_NB, S5_BH, S5_BP)), _const_w((S5_NB, S5_BH, S5_BP)),
                  _whole((NSEG, NSTATE)), _whole((NSEG, NSTATE)), _whole((1, NSTATE)), _whole((1, NSTATE)),
                  _const_w((S5_NB, S5_BP, S5_BH)), _const_w((S5_NB, S5_BP, S5_BH)), _whole((1, D)),
                  _layer_w(D, D, 0), _whole((1, D)), _layer_w(D, D, 0)],
        out_specs=[_rows(tm, D), _rows(tm, NSTATE), _rows(tm, NSTATE), _rows(tm, D), _rows(tm, D), _rows(tm, D), _rows(tm, D)],
        out_shape=[jax.ShapeDtypeStruct((L, D), F32), jax.ShapeDtypeStruct((L, NSTATE), F32), jax.ShapeDtypeStruct((L, NSTATE), F32),
                   jax.ShapeDtypeStruct((L, D), F32), jax.ShapeDtypeStruct((L, D), F32),
                   jax.ShapeDtypeStruct((L, D), BF16), jax.ShapeDtypeStruct((L, D), F32)],
        scratch_shapes=[pltpu.VMEM((NSEG, NSTATE), F32), pltpu.VMEM((NSEG, NSTATE), F32)],
        compiler_params=_params(1, 56),
    )(x, u, pv, b_re, b_im, s0_re, s0_im, ar, ai, c_re, c_im, dvec, glu_w, glu_b, w_out)


def _s5_bwd_ends(dx1, y, y1, zg, u, pv, c_re, c_im, ar, ai, dvec, glu_w, w_out, layer, tm, after=None):
    L = dx1.shape[0]
    nt = L // tm

    def body(dx1_ref, y_ref, y1_ref, zg_ref, u_ref, pv_ref, cre_ref, cim_ref, ar_ref, ai_ref, d_ref, gw_ref, wout_ref,
             dy_ref, y2_ref, dzg_ref, dy1_ref, dus_ref, ere_ref, eim_ref, vs_ref, g_re, g_im):
        @pl.when(pl.program_id(0) == 0)
        def _():
            vs_ref[...] = jnp.zeros_like(vs_ref)
            ere_ref[...] = jnp.zeros_like(ere_ref)
            eim_ref[...] = jnp.zeros_like(eim_ref)

        dx1v, p = dx1_ref[...], pv_ref[...]
        dyb = _bf(dx1v * p[R_G1:R_G1 + 1])
        dy_ref[...] = dyb
        vs_ref[0:1, :] += _sum0(dx1v * y_ref[...])
        dy3 = _dot_nt(dyb, wout_ref[...])
        y2, gelu_vjp = jax.vjp(jax.nn.gelu, y1_ref[...])
        y2_ref[...] = _bf(y2)
        gate = jax.nn.sigmoid(zg_ref[...])
        dzg = dy3 * y2 * gate * (1.0 - gate)
        dzgb = _bf(dzg)
        dzg_ref[...] = dzgb
        vs_ref[1:2, :] += _sum0(dzg)
        dy2 = dy3 * gate + _dot_nt(dzgb, gw_ref[...])
        dy1 = gelu_vjp(dy2)[0]
        vs_ref[2:3, :] += _sum0(dy1 * u_ref[...])
        dus_ref[...] = dy1 * d_ref[...]
        dy1b = _bf(dy1)
        dy1_ref[...] = dy1b
        for k in range(S5_NB):
            dk = dy1b[:, k * S5_BH:(k + 1) * S5_BH]
            g_re[:, k * S5_BP:(k + 1) * S5_BP] = _dot_nt(dk, cre_ref[k])
            g_im[:, k * S5_BP:(k + 1) * S5_BP] = -_dot_nt(dk, cim_ref[k])
        _segment_scan(g_re, g_im, ere_ref, eim_ref, ar_ref[...], ai_ref[...], tm // 8, adjoint=True, write=False)

    call, tail = _call_after(
        after, body, 13, name="s5_bwd_ends", grid=(nt,),
        in_specs=[_rows(tm, D, nt)] * 5 + [pl.BlockSpec((None, 8, D), lambda i: (layer, 0, 0)),
                  _const_w((S5_NB, S5_BP, S5_BH)), _const_w((S5_NB, S5_BP, S5_BH)), _whole((1, NSTATE)), _whole((1, NSTATE)),
                  _whole((1, D)), _layer_w(D, D, 0), _layer_w(D, D, 0)],
        out_specs=[_rows(tm, D, nt)] * 5 + [_whole((NSEG, NSTATE)), _whole((NSEG, NSTATE)), _whole((8, D))],
        out_shape=[jax.ShapeDtypeStruct((L, D), BF16)] * 4 + [jax.ShapeDtypeStruct((L, D), F32),
                   jax.ShapeDtypeStruct((NSEG, NSTATE), F32), jax.ShapeDtypeStruct((NSEG, NSTATE), F32),
                   jax.ShapeDtypeStruct((8, D), F32)],
        scratch_shapes=[pltpu.VMEM((tm, NSTATE), F32), pltpu.VMEM((tm, NSTATE), F32)],
        compiler_params=_params(1, 56),
    )
    return call(dx1, y, y1, zg, u, pv, c_re, c_im, ar, ai, dvec, glu_w, w_out, *tail)


def _s5_bwd_in(dx1, dy1_b, du_skip, x, s_re, s_im, pv, b_re, b_im, c_re, c_im, l0_re, l0_im, ar, ai, w_in, layer, tm):
    L = x.shape[0]
    nt = L // tm

    def body(dx1_ref, dy1_ref, dus_ref, x_ref, sre_ref, sim_ref, hre_ref, him_ref, lre_ref, lim_ref, pv_ref, bre_ref, bim_ref,
             cre_ref, cim_ref, l0re_ref, l0im_ref, ar_ref, ai_ref, win_ref,
             dx_ref, du_ref, lamre_ref, lamim_ref, da_ref, vs_ref, g_re, g_im, st_re, st_im):
        gi = pl.program_id(0)
        tile = nt - 1 - gi

        @pl.when(gi == 0)
        def _():
            vs_ref[...] = jnp.zeros_like(vs_ref)
            da_ref[...] = jnp.zeros_like(da_ref)
            st_re[...] = l0re_ref[...]
            st_im[...] = l0im_ref[...]

        p = pv_ref[...]
        dy1b = dy1_ref[...]
        for k in range(S5_NB):
            dk = dy1b[:, k * S5_BH:(k + 1) * S5_BH]
            g_re[:, k * S5_BP:(k + 1) * S5_BP] = _dot_nt(dk, cre_ref[k])
            g_im[:, k * S5_BP:(k + 1) * S5_BP] = -_dot_nt(dk, cim_ref[k])
        _segment_scan(g_re, g_im, st_re, st_im, ar_ref[...], ai_ref[...], tm // 8, adjoint=True, write=True)
        lam_r, lam_i = g_re[...], g_im[...]
        lrb, lib = _bf(lam_r), _bf(lam_i)
        lamre_ref[...] = lrb
        lamim_ref[...] = lib

        def wrapped(last_ref):
            z = last_ref[...]
            row = lax.broadcasted_iota(jnp.int32, z.shape, 0)
            return jnp.where(row >= 1, pltpu.roll(z, 1, 0), 0.0)

        first_r = jnp.where(tile > 0, hre_ref[...], wrapped(lre_ref))
        first_i = jnp.where(tile > 0, him_ref[...], wrapped(lim_ref))
        sp_r = jnp.concatenate([first_r, sre_ref[0:tm - 8, :]], axis=0)
        sp_i = jnp.concatenate([first_i, sim_ref[0:tm - 8, :]], axis=0)
        da_ref[0:1, :] += _sum0(lam_r * sp_r + lam_i * sp_i)
        da_ref[1:2, :] += _sum0(lam_i * sp_r - lam_r * sp_i)

        parts = []
        for k in range(S5_NB):
            sl = slice(k * S5_BP, (k + 1) * S5_BP)
            parts.append(_dot_nt(lrb[:, sl], bre_ref[k]) + _dot_nt(lib[:, sl], bim_ref[k]))
        dub = _bf(jnp.concatenate(parts, axis=1) + dus_ref[...])
        du_ref[...] = dub
        dh = _dot_nt(dub, win_ref[...])
        _, xn, r = _norm_mod(x_ref[...], p[R_N1:R_N1 + 1], p[R_SC1:R_SC1 + 1], p[R_SH1:R_SH1 + 1])
        dx_ref[...] = dx1_ref[...] + _norm_mod_bwd(dh, xn, r, p[R_N1:R_N1 + 1], p[R_SC1:R_SC1 + 1])
        vs_ref[1:2, :] += _sum0(dh * xn)
        vs_ref[2:3, :] += _sum0(dh)

    halo = pl.BlockSpec((8, NSTATE), lambda i: (jnp.maximum((nt - 1 - i) * (tm // 8) - 1, 0), 0))
    last = pl.BlockSpec((8, NSTATE), lambda i: (L // 8 - 1, 0))
    return pl.pallas_call(
        body, name="s5_bwd_in", grid=(nt,),
        in_specs=[_rows(tm, D, nt), _rows(tm, D, nt), _rows(tm, D, nt), _rows(tm, D, nt), _rows(tm, NSTATE, nt), _rows(tm, NSTATE, nt),
                  halo, halo, last, last, pl.BlockSpec((None, 8, D), lambda i: (layer, 0, 0)),
                  _const_w((S5_NB, S5_BH, S5_BP)), _const_w((S5_NB, S5_BH, S5_BP)),
                  _const_w((S5_NB, S5_BP, S5_BH)), _const_w((S5_NB, S5_BP, S5_BH)),
                  _whole((NSEG, NSTATE)), _whole((NSEG, NSTATE)), _whole((1, NSTATE)), _whole((1, NSTATE)), _layer_w(D, D, 0)],
        out_specs=[_rows(tm, D, nt), _rows(tm, D, nt), _rows(tm, NSTATE, nt), _rows(tm, NSTATE, nt), _whole((8, NSTATE)), _whole((8, D))],
        out_shape=[jax.ShapeDtypeStruct((L, D), F32), jax.ShapeDtypeStruct((L, D), BF16),
                   jax.ShapeDtypeStruct((L, NSTATE), BF16), jax.ShapeDtypeStruct((L, NSTATE), BF16),
                   jax.ShapeDtypeStruct((8, NSTATE), F32), jax.ShapeDtypeStruct((8, D), F32)],
        scratch_shapes=[pltpu.VMEM((tm, NSTATE), F32), pltpu.VMEM((tm, NSTATE), F32),
                        pltpu.VMEM((NSEG, NSTATE), F32), pltpu.VMEM((NSEG, NSTATE), F32)],
        compiler_params=_params(1, 60),
    )(dx1, dy1_b, du_skip, x, s_re, s_im, s_re, s_im, s_re, s_im, pv, b_re, b_im, c_re, c_im, l0_re, l0_im, ar, ai, w_in)


def _blockdiag_b(bt):
    b = bt.reshape(S5_H, S5_NB, 16, S5_P).transpose(1, 2, 0, 3)
    eye = jnp.eye(16, dtype=bt.dtype)
    return (b[:, :, :, None, :] * eye[None, :, None, :, None]).reshape(S5_NB, S5_BH, S5_BP)


def _unblock_b(d):
    d = jnp.einsum("bghgp->bghp", d.reshape(S5_NB, 16, S5_H, 16, S5_P))
    return d.transpose(2, 0, 1, 3).reshape(S5_H, S5_G, S5_P)


def _blockdiag_c(cm):
    c4 = cm.reshape(S5_NB, 16, S5_H, S5_P)
    eye = jnp.eye(16, dtype=cm.dtype)
    out = c4.transpose(0, 1, 3, 2)[:, :, :, None, :] * eye[None, :, None, :, None]
    return out.reshape(S5_NB, S5_BP, S5_BH)


def _unblock_c(d):
    d = jnp.einsum("bgpgh->bghp", d.reshape(S5_NB, 16, S5_P, 16, S5_H))
    return d.reshape(S5_G, S5_H, S5_P)


def _tril_mask():
    return lax.broadcasted_iota(jnp.int32, (SG_CHUNK, SG_CHUNK), 0) >= lax.broadcasted_iota(jnp.int32, (SG_CHUNK, SG_CHUNK), 1)


def _sg_fwd(x, pv, w_in, w_s, b_t, vg, w_out, layer, tm, after=None):
    L = x.shape[0]
    nc = tm // SG_CHUNK

    def body(x_ref, pv_ref, win_ref, ws_ref, bt_ref, vg_ref, wout_ref, x1_ref, h_ref, uv_ref, vm_ref, q_ref, y_ref):
        xv, p = x_ref[...], pv_ref[...]
        h, _, _ = _norm_mod(xv, p[R_N1:R_N1 + 1], p[R_SC1:R_SC1 + 1], p[R_SH1:R_SH1 + 1])
        hb = _bf(h)
        h_ref[...] = hb
        uv = _dot(hb, win_ref[...])
        uv_ref[...] = uv
        v = uv[:, D:]
        rv = lax.rsqrt(jnp.mean(v * v, axis=-1, keepdims=True) + EPS)
        vnb = _bf((v * rv) * vg_ref[...])
        mask = _tril_mask()
        bt = bt_ref[...]
        for hd in range(SG_HEADS):
            wm = _bf(jnp.where(mask, ws_ref[hd], 0.0))
            cs = slice(hd * SG_CHUNK, (hd + 1) * SG_CHUNK)
            for ck in range(nc):
                rs = slice(ck * SG_CHUNK, (ck + 1) * SG_CHUNK)
                vm_ref[rs, cs] = _dot(wm, vnb[rs, cs]) + bt[:, hd:hd + 1]
        qb = _bf(uv[:, :D] * vm_ref[...])
        q_ref[...] = qb
        y = _dot(qb, wout_ref[...])
        y_ref[...] = y
        x1_ref[...] = xv + p[R_G1:R_G1 + 1] * y

    call, tail = _call_after(
        after, body, 7, name="sg_fwd", grid=(L // tm,),
        in_specs=[_rows(tm, D), pl.BlockSpec((None, 8, D), lambda i: (layer, 0, 0)), _layer_w(D, 2 * D, 0),
                  _whole((SG_HEADS, SG_CHUNK, SG_CHUNK)), _whole((SG_CHUNK, SG_HEADS)), _whole((1, D)), _layer_w(D, D, 0)],
        out_specs=[_rows(tm, D), _rows(tm, D), _rows(tm, 2 * D), _rows(tm, D), _rows(tm, D), _rows(tm, D)],
        out_shape=[jax.ShapeDtypeStruct((L, D), F32), jax.ShapeDtypeStruct((L, D), BF16), jax.ShapeDtypeStruct((L, 2 * D), F32),
                   jax.ShapeDtypeStruct((L, D), F32), jax.ShapeDtypeStruct((L, D), BF16), jax.ShapeDtypeStruct((L, D), F32)],
        compiler_params=_params(1, 56),
    )
    return call(x, pv, w_in, w_s, b_t, vg, w_out, *tail)


def _sg_bwd(dx1, x, y, uv, vm, pv, w_in, w_s, vg, w_out, layer, tm, after=None):
    L = x.shape[0]
    nc = tm // SG_CHUNK

    def body(dx1_ref, x_ref, y_ref, uv_ref, vm_ref, pv_ref, win_ref, ws_ref, vg_ref, wout_ref,
             dx_ref, duv_ref, dy_ref, vs_ref, dws_ref, dbt_ref, dvn_scr):
        @pl.when(pl.program_id(0) == 0)
        def _():
            vs_ref[...] = jnp.zeros_like(vs_ref)
            dws_ref[...] = jnp.zeros_like(dws_ref)
            dbt_ref[...] = jnp.zeros_like(dbt_ref)

        dx1v, p = dx1_ref[...], pv_ref[...]
        dyb = _bf(dx1v * p[R_G1:R_G1 + 1])
        dy_ref[...] = dyb
        vs_ref[0:1, :] += _sum0(dx1v * y_ref[...])
        dq = _dot_nt(dyb, wout_ref[...])
        uv = uv_ref[...]
        u, v = uv[:, :D], uv[:, D:]
        dub = _bf(dq * vm_ref[...])
        dvm = dq * u
        dvmb = _bf(dvm)
        rv = lax.rsqrt(jnp.mean(v * v, axis=-1, keepdims=True) + EPS)
        vh = v * rv
        vgv = vg_ref[...]
        vnb = _bf(vh * vgv)
        mask = _tril_mask()
        for hd in range(SG_HEADS):
            wm = _bf(jnp.where(mask, ws_ref[hd], 0.0))
            cs = slice(hd * SG_CHUNK, (hd + 1) * SG_CHUNK)
            dws = jnp.zeros((SG_CHUNK, SG_CHUNK), F32)
            dbs = jnp.zeros((SG_CHUNK, 1), F32)
            for ck in range(nc):
                rs = slice(ck * SG_CHUNK, (ck + 1) * SG_CHUNK)
                dvn_scr[rs, cs] = _dot_tn(wm, dvmb[rs, cs])
                dws = dws + _dot_nt(dvmb[rs, cs], vnb[rs, cs])
                dbs = dbs + jnp.sum(dvm[rs, cs], axis=1, keepdims=True)
            dws_ref[hd] += jnp.where(mask, dws, 0.0)
            dbt_ref[:, hd:hd + 1] += dbs
        dvn = dvn_scr[...]
        vs_ref[3:4, :] += _sum0(dvn * vh)
        dvnn = dvn * vgv
        dvb = _bf(rv * (dvnn - vh * jnp.mean(dvnn * vh, axis=-1, keepdims=True)))
        duv_ref[:, 0:D] = dub
        duv_ref[:, D:2 * D] = dvb
        dh = _dot_nt(dub, win_ref[:, 0:D]) + _dot_nt(dvb, win_ref[:, D:2 * D])
        _, xn, r = _norm_mod(x_ref[...], p[R_N1:R_N1 + 1], p[R_SC1:R_SC1 + 1], p[R_SH1:R_SH1 + 1])
        dx_ref[...] = dx1v + _norm_mod_bwd(dh, xn, r, p[R_N1:R_N1 + 1], p[R_SC1:R_SC1 + 1])
        vs_ref[1:2, :] += _sum0(dh * xn)
        vs_ref[2:3, :] += _sum0(dh)

    call, tail = _call_after(
        after, body, 10, name="sg_bwd", grid=(L // tm,),
        in_specs=[_rows(tm, D), _rows(tm, D), _rows(tm, D), _rows(tm, 2 * D), _rows(tm, D),
                  pl.BlockSpec((None, 8, D), lambda i: (layer, 0, 0)), _layer_w(D, 2 * D, 0),
                  _whole((SG_HEADS, SG_CHUNK, SG_CHUNK)), _whole((1, D)), _layer_w(D, D, 0)],
        out_specs=[_rows(tm, D), _rows(tm, 2 * D), _rows(tm, D), _whole((8, D)),
                   _whole((SG_HEADS, SG_CHUNK, SG_CHUNK)), _whole((SG_CHUNK, SG_HEADS))],
        out_shape=[jax.ShapeDtypeStruct((L, D), F32), jax.ShapeDtypeStruct((L, 2 * D), BF16), jax.ShapeDtypeStruct((L, D), BF16),
                   jax.ShapeDtypeStruct((8, D), F32), jax.ShapeDtypeStruct((SG_HEADS, SG_CHUNK, SG_CHUNK), F32),
                   jax.ShapeDtypeStruct((SG_CHUNK, SG_HEADS), F32)],
        scratch_shapes=[pltpu.VMEM((tm, D), F32)],
        compiler_params=_params(1, 56),
    )
    return call(dx1, x, y, uv, vm, pv, w_in, w_s, vg, w_out, *tail)


def _final(x, target, fg, tm):
    L = x.shape[0]

    def body(x_ref, t_ref, g_ref, dx_ref, vs_ref):
        @pl.when(pl.program_id(0) == 0)
        def _():
            vs_ref[...] = jnp.zeros_like(vs_ref)

        xv, g = x_ref[...], g_ref[...]
        r = lax.rsqrt(jnp.mean(xv * xv, axis=-1, keepdims=True) + EPS)
        xn = xv * r
        e = xn * g - t_ref[...]
        vs_ref[0:1, :] += jnp.sum(e * e)
        dout = e * (1.0 / D)
        vs_ref[1:2, :] += _sum0(dout * xn)
        dxn = dout * g
        dx_ref[...] = r * (dxn - xn * jnp.mean(dxn * xn, axis=-1, keepdims=True))

    return pl.pallas_call(
        body, name="final_loss", grid=(L // tm,),
        in_specs=[_rows(tm, D), _rows(tm, D), _whole((1, D))],
        out_specs=[_rows(tm, D), _whole((8, D))],
        out_shape=[jax.ShapeDtypeStruct((L, D), F32), jax.ShapeDtypeStruct((8, D), F32)],
        compiler_params=_params(1),
    )(x, target, fg)


def _pack_flat(arrs, multiple=LANES):
    flat = jnp.concatenate([a.reshape(-1).astype(F32) for a in arrs])
    return jnp.pad(flat, (0, -flat.shape[0] % multiple))


def _pack(arrs, row_multiple=8):
    return _pack_flat(arrs, row_multiple * LANES).reshape(-1, LANES)


def _unpack(buf, shapes, lead=()):
    flat = buf.reshape(lead + (-1,))
    out, off = [], 0
    for s in shapes:
        n = 1
        for d in s:
            n *= d
        out.append(flat[..., off:off + n].reshape(lead + tuple(s)))
        off += n
    return out


BIG = ("ff_w1", "ff_w2", "conv_w_in", "conv_w_out", "ssm_w_in", "ssm_glu_w", "ssm_w_out", "sg_w_in", "sg_w_out")
BIG_AXIS = {"ff_w1": 2, "ff_w2": 1, "conv_w_in": 2, "conv_w_out": 1, "ssm_w_in": 1, "ssm_glu_w": 1, "ssm_w_out": 1,
            "sg_w_in": 2, "sg_w_out": 1}
LAYER_WEIGHTS = (
    (("conv_w_in", 0), ("conv_w_out", 0), ("ff_w1", 0), ("ff_w2", 0)),
    (("ssm_w_in", 0), ("ssm_glu_w", 0), ("ssm_w_out", 0), ("ff_w1", 1), ("ff_w2", 1)),
    (("sg_w_in", 0), ("sg_w_out", 0), ("ff_w1", 2), ("ff_w2", 2)),
    (("conv_w_in", 1), ("conv_w_out", 1), ("ff_w1", 3), ("ff_w2", 3)),
)
GATHER_GROUPS = tuple(grp for lw in LAYER_WEIGHTS for grp in (lw[:-2], lw[-2:]))
SMALL = ("ada_b", "norm1_g", "norm2_g", "final_g", "conv_w", "conv_b", "ssm_a_re", "ssm_a_im", "ssm_log_dt", "ssm_b_re",
         "ssm_b_im", "ssm_c_re", "ssm_c_im", "ssm_d", "ssm_glu_b", "sg_v_g", "sg_w_s", "sg_b_s")
WEIGHTS = ("ada_w", "ada_b", "norm1_g", "norm2_g", "ff_w1", "ff_w2", "final_g", "conv_w_in", "conv_w", "conv_b", "conv_w_out",
           "ssm_w_in", "ssm_a_re", "ssm_a_im", "ssm_log_dt", "ssm_b_re", "ssm_b_im", "ssm_c_re", "ssm_c_im", "ssm_d",
           "ssm_glu_w", "ssm_glu_b", "ssm_w_out", "sg_w_in", "sg_v_g", "sg_w_s", "sg_b_s", "sg_w_out")


def kernel(x, c, ada_w, ada_b, norm1_g, norm2_g, ff_w1, ff_w2, final_g, conv_w_in, conv_w, conv_b, conv_w_out, ssm_w_in, ssm_a_re, ssm_a_im, ssm_log_dt, ssm_b_re, ssm_b_im, ssm_c_re, ssm_c_im, ssm_d, ssm_glu_w, ssm_glu_b, ssm_w_out, sg_w_in, sg_v_g, sg_w_s, sg_b_s, sg_w_out, loss_target, m_ada_w, m_ada_b, m_norm1_g, m_norm2_g, m_ff_w1, m_ff_w2, m_final_g, m_conv_w_in, m_conv_w, m_conv_b, m_conv_w_out, m_ssm_w_in, m_ssm_a_re, m_ssm_a_im, m_ssm_log_dt, m_ssm_b_re, m_ssm_b_im, m_ssm_c_re, m_ssm_c_im, m_ssm_d, m_ssm_glu_w, m_ssm_glu_b, m_ssm_w_out, m_sg_w_in, m_sg_v_g, m_sg_w_s, m_sg_b_s, m_sg_w_out, v_ada_w, v_ada_b, v_norm1_g, v_norm2_g, v_ff_w1, v_ff_w2, v_final_g, v_conv_w_in, v_conv_w, v_conv_b, v_conv_w_out, v_ssm_w_in, v_ssm_a_re, v_ssm_a_im, v_ssm_log_dt, v_ssm_b_re, v_ssm_b_im, v_ssm_c_re, v_ssm_c_im, v_ssm_d, v_ssm_glu_w, v_ssm_glu_b, v_ssm_w_out, v_sg_w_in, v_sg_v_g, v_sg_w_s, v_sg_b_s, v_sg_w_out):
    args = dict(locals())
    w = {n: args[n] for n in WEIGHTS}
    m = {n: args["m_" + n] for n in WEIGHTS}
    v = {n: args["v_" + n] for n in WEIGHTS}
    L = x.shape[1]
    tm = min(L, 256)
    tr = min(L, 512)
    chip = 2 * lax.axis_index("x") + lax.axis_index("y")
    me = 2 * chip + lax.axis_index("c")
    xin = x[0]
    target = loss_target[0]
    chip1 = chip.reshape(1).astype(jnp.int32)

    gathers = []

    def start_gather(g, after):
        entries = GATHER_GROUPS[g]
        axes = [BIG_AXIS[n] for n, _ in entries]
        lands = [_cast_place(w[n], li, BIG_AXIS[n], chip1, f"cast_{n}_{li}") for n, li in entries]
        s_sems, r_sems, lands, token = _gather_start(lands, axes, f"gather_start{g}", after)
        gathers.append((s_sems, r_sems, lands, axes))
        return token

    def weights_of(g, after):
        s_sems, r_sems, lands, axes = gathers[g]
        lands = _gather_wait(s_sems, r_sems, lands, axes, f"gather_wait{g}", after)
        return dict(zip([n for n, _ in GATHER_GROUPS[g]], _gather_share(lands, axes, f"gather_share{g}")))

    small_in = _pack([c, conv_w, conv_b, sg_v_g])
    got = _allgather_small(small_in, "gather_small_inputs").reshape(N_DEV, -1)
    c_all, cw_sh, cb_sh, vg_sh = _unpack(got, [(D,), conv_w.shape, conv_b.shape, sg_v_g.shape], lead=(N_DEV,))
    conv_w_full = jnp.concatenate([cw_sh[2 * k] for k in range(4)], axis=-1)
    conv_b_full = jnp.concatenate([cb_sh[2 * k] for k in range(4)], axis=-1)
    vg_full = jnp.concatenate([vg_sh[2 * k] for k in range(4)], axis=-1)
    c16 = jnp.pad(c_all, ((0, 16 - N_DEV), (0, 0)))

    cols = ada_w.shape[2]
    ada_b_cols = lax.dynamic_slice_in_dim(ada_b, chip * cols, cols, axis=1)[:, None, :]
    mod_sh = _ada_fwd(c16, ada_w, ada_b_cols)[:, :N_DEV, :]
    mod_all = _allgather_small(_pack([mod_sh]), "gather_mod").reshape(N_DEV, -1)
    mod_all = _unpack(mod_all, [mod_sh.shape], lead=(N_DEV,))[0]
    mod_mine = lax.dynamic_index_in_dim(mod_all[0::2], me, axis=2, keepdims=False)
    mod_mine = mod_mine.transpose(1, 0, 2).reshape(DEPTH, 6, D)
    pv = jnp.concatenate([mod_mine, norm1_g[:, None, :], norm2_g[:, None, :]], axis=1)

    start_gather(1, start_gather(0, pv))

    cw_rows = jnp.concatenate([conv_w_full, conv_b_full[:, None, :], jnp.zeros((conv_w_full.shape[0], 4, D), F32)], axis=1)

    a_re, a_im = ssm_a_re[0], ssm_a_im[0]
    log_dt = ssm_log_dt[0][:, None]
    bt_re, bt_im = ssm_b_re[0].transpose(2, 0, 1), ssm_b_im[0].transpose(2, 0, 1)
    abar_re, abar_im, bbar_re, bbar_im = _s5_params_fwd(a_re, a_im, log_dt, bt_re, bt_im)
    ar_vec, ai_vec = abar_re.reshape(1, NSTATE), abar_im.reshape(1, NSTATE)
    bd_re, bd_im = _bf(_blockdiag_b(bbar_re)), _bf(_blockdiag_b(bbar_im))
    cd_re, cd_im = _bf(_blockdiag_c(ssm_c_re[0])), _bf(_blockdiag_c(ssm_c_im[0]))

    saved = []
    fulls = []
    xl = xin
    for i in range(DEPTH):
        kind = MIXER_OF_LAYER[i]
        j = i // 3
        full = weights_of(2 * i, cd_im if i == 0 else xl)
        fulls.append(full)
        tok = start_gather(2 * i + 2, next(iter(full.values()))) if i + 1 < DEPTH else None
        if kind == 0:
            x1, h, bcx, conv, q, y = _conv_fwd(xl, pv, full["conv_w_in"], full["conv_w_out"], cw_rows, i, j, tm, after=tok)
            mix = dict(h=h, bcx=bcx, conv=conv, q=q, y=y)
        elif kind == 1:
            xp = _to_segments(xl)
            h, u, e_re, e_im = _s5_fwd_ends(xp, pv, full["ssm_w_in"], bd_re, bd_im, ar_vec, ai_vec, i, tm, after=tok)
            s0_re, s0_im = _s5_segment_states(e_re, e_im, ar_vec, ai_vec, L // NSEG, adjoint=False)
            x1p, s_re, s_im, y1, zg, y3, y = _s5_fwd_out(xp, u, pv, bd_re, bd_im, s0_re, s0_im, ar_vec, ai_vec, cd_re, cd_im,
                                                         ssm_d, full["ssm_glu_w"], ssm_glu_b, full["ssm_w_out"], i, tm)
            x1 = _from_segments(x1p)
            mix = dict(xp=xp, h=h, u=u, s_re=s_re, s_im=s_im, y1=y1, zg=zg, y3=y3, y=y)
        else:
            x1, h, uv, vm, q, y = _sg_fwd(xl, pv, full["sg_w_in"], sg_w_s[0], sg_b_s[0].T, vg_full, full["sg_w_out"], i, tm,
                                          after=tok)
            mix = dict(h=h, uv=uv, vm=vm, q=q, y=y)
        full.update(weights_of(2 * i + 1, x1))
        tok = start_gather(2 * i + 3, full["ff_w2"]) if i + 1 < DEPTH else None
        x2, h2, a, f = _ffn_fwd(x1, pv, full["ff_w1"], full["ff_w2"], i, tm, after=tok)
        saved.append(dict(x=xl, x1=x1, h2=h2, a=a, f=f, **mix))
        xl = x2

    dxl, vs_fin = _final(xl, target, final_g[None, :], tm)

    gfull = {n: [None] * w[n].shape[0] for n in BIG}
    vs_mix, vs_ffn = [None] * DEPTH, [None] * DEPTH
    small_g = {}
    scatters = {}
    token = None

    def start_scatter(key, entries, after):
        garrs = [gfull[n][li][None] for n, li in entries]
        gaxes = [BIG_AXIS[n] for n, _ in entries]
        s_sems, r_sems, garrs, lands, tok = _scatter_start(garrs, gaxes, f"scatter_start{key}", after)
        scatters[key] = (s_sems, r_sems, garrs, lands, gaxes, entries)
        return tok

    for i in reversed(range(DEPTH)):
        kind = MIXER_OF_LAYER[i]
        j = i // 3
        sv = saved[i]
        full = fulls[i]
        dx1, p_b, da_b, df_b, vs_ffn[i] = _ffn_bwd(dxl, sv["x1"], sv["a"], sv["f"], pv, full["ff_w1"], full["ff_w2"], i, tm,
                                                   after=token)
        gfull["ff_w1"][i] = _mm_tn(sv["h2"], da_b, f"wgrad_ff_w1_{i}")
        gfull["ff_w2"][i] = _mm_tn(p_b, df_b, f"wgrad_ff_w2_{i}")
        if i == 0:
            token = start_scatter("0f", LAYER_WEIGHTS[0][2:], dx1)
        if kind == 0:
            dxl, dbcx_b, dy_b, vsm = _conv_bwd(dx1, sv["x"], sv["y"], sv["bcx"], sv["conv"], pv, full["conv_w_in"],
                                               full["conv_w_out"], cw_rows, i, j, tm, after=token if i == 0 else None)
            gfull["conv_w_in"][j] = _mm_tn(sv["h"], dbcx_b, f"wgrad_conv_w_in_{j}")
            gfull["conv_w_out"][j] = _mm_tn(sv["q"], dy_b, f"wgrad_conv_w_out_{j}")
            small_g.setdefault("conv_w", [None, None])[j] = vsm[3:6]
            small_g.setdefault("conv_b", [None, None])[j] = vsm[6]
        elif kind == 1:
            dx1p = _to_segments(dx1)
            dy_b, y2_b, dzg_b, dy1_b, du_skip, eb_re, eb_im, vsm = _s5_bwd_ends(
                dx1p, sv["y"], sv["y1"], sv["zg"], sv["u"], pv, cd_re, cd_im, ar_vec, ai_vec, ssm_d, full["ssm_glu_w"],
                full["ssm_w_out"], i, tm)
            l0_re, l0_im = _s5_segment_states(eb_re, eb_im, ar_vec, ai_vec, L // NSEG, adjoint=True)
            dxp, du_b, lam_re, lam_im, dabar, vs_in = _s5_bwd_in(
                dx1p, dy1_b, du_skip, sv["xp"], sv["s_re"], sv["s_im"], pv, bd_re, bd_im, cd_re, cd_im, l0_re, l0_im,
                ar_vec, ai_vec, full["ssm_w_in"], i, tm)
            dxl = _from_segments(dxp)
            gfull["ssm_w_out"][0] = _mm_tn(sv["y3"], dy_b, "wgrad_ssm_w_out")
            gfull["ssm_glu_w"][0] = _mm_tn(y2_b, dzg_b, "wgrad_ssm_glu_w")
            gfull["ssm_w_in"][0] = _mm_tn(sv["h"], du_b, "wgrad_ssm_w_in")
            d_cre = _unblock_c(_mm_tn_blocks(sv["s_re"], dy1_b, S5_BP, S5_BH, "wgrad_s5_c_re"))
            d_cim = -_unblock_c(_mm_tn_blocks(sv["s_im"], dy1_b, S5_BP, S5_BH, "wgrad_s5_c_im"))
            d_bbre = _unblock_b(_mm_tn_blocks(sv["u"], lam_re, S5_BH, S5_BP, "wgrad_s5_b_re"))
            d_bbim = _unblock_b(_mm_tn_blocks(sv["u"], lam_im, S5_BH, S5_BP, "wgrad_s5_b_im"))
            d_are, d_aim, d_ldt, d_btre, d_btim = _s5_params_bwd(
                a_re, a_im, log_dt, bt_re, bt_im, dabar[0].reshape(S5_G, S5_P), dabar[1].reshape(S5_G, S5_P), d_bbre, d_bbim)
            small_g.update(ssm_a_re=d_are, ssm_a_im=d_aim, ssm_log_dt=d_ldt, ssm_b_re=d_btre.transpose(1, 2, 0),
                           ssm_b_im=d_btim.transpose(1, 2, 0), ssm_c_re=d_cre, ssm_c_im=d_cim, ssm_d=vsm[2], ssm_glu_b=vsm[1])
            vsm = jnp.concatenate([vsm[0:1], vs_in[1:3], jnp.zeros((5, D), F32)], axis=0)
        else:
            dxl, duv_b, dy_b, vsm, d_ws, d_bt = _sg_bwd(dx1, sv["x"], sv["y"], sv["uv"], sv["vm"], pv, full["sg_w_in"],
                                                        sg_w_s[0], vg_full, full["sg_w_out"], i, tm)
            gfull["sg_w_in"][0] = _mm_tn(sv["h"], duv_b, "wgrad_sg_w_in")
            gfull["sg_w_out"][0] = _mm_tn(sv["q"], dy_b, "wgrad_sg_w_out")
            small_g.update(sg_v_g=vsm[3], sg_w_s=d_ws, sg_b_s=d_bt.T)
        vs_mix[i] = vsm
        token = start_scatter(str(i), LAYER_WEIGHTS[i], dxl) if i > 0 else start_scatter("0c", LAYER_WEIGHTS[0][:2], dxl)
    grad_x = dxl[None]

    sums = {n: [None] * w[n].shape[0] for n in BIG}

    def collect(key, after):
        s_sems, r_sems, garrs, lands, gaxes, entries = scatters[key]
        garrs, recv = _scatter_wait(s_sems, r_sems, garrs, lands, gaxes, f"scatter_wait{key}", after)
        for (n, li), g, r3, ax in zip(entries, garrs, recv, gaxes):
            sums[n][li] = _sum_parts(r3, g, ax, chip1, f"sum_{n}_{li}")
        return sums[entries[-1][0]][entries[-1][1]]

    after = token
    for key in ("3", "2", "1"):
        after = collect(key, after)
    early = [(n, li) for i in (3, 2, 1) for n, li in LAYER_WEIGHTS[i]]
    late = list(LAYER_WEIGHTS[0][2:]) + list(LAYER_WEIGHTS[0][:2])
    sib = dict(zip(early, _swap_with_sibling([sums[n][li] for n, li in early], "swap_grad_sums_early")))
    after = sib[early[-1]]
    for key in ("0f", "0c"):
        after = collect(key, after)
    sib.update(zip(late, _swap_with_sibling([sums[n][li] for n, li in late], "swap_grad_sums_late")))

    dmod = _mod_bwd(jnp.stack(vs_mix), jnp.stack(vs_ffn), pv)
    small_g.update(ada_b=dmod[:, :6, :], norm1_g=dmod[:, 6, :], norm2_g=dmod[:, 7, :], final_g=vs_fin[1],
                   conv_w=jnp.stack(small_g["conv_w"]), conv_b=jnp.stack(small_g["conv_b"]))

    loss_part = (0.5 / D) * vs_fin[0, 0:1]
    part_shapes = [(1,)] + [tuple(small_g[n].shape) for n in SMALL]
    parts_sum = _allreduce_small(_pack([loss_part] + [small_g[n] for n in SMALL], 16), "reduce_small_grads", sib[late[-1]])
    summed = _unpack(parts_sum, part_shapes)
    loss = summed[0][0]
    gsum = dict(zip(SMALL, summed[1:]))
    dmod_all = _allgather_small(_pack([small_g["ada_b"]]), "gather_dmod", parts_sum)
    dmod_all = dmod_all.reshape(N_DEV, DEPTH, 6 * D)
    dmod_cols = lax.dynamic_slice_in_dim(dmod_all, chip * cols, cols, axis=2).transpose(1, 0, 2)
    g_ada_w = _ada_bwd(c16, jnp.pad(dmod_cols, ((0, 0), (0, 16 - N_DEV), (0, 0))))

    res = {}
    shp = ada_w.shape
    two = lambda t: t.reshape(shp[0] * shp[1], shp[2])
    res["ada_w"] = [t.reshape(shp) for t in _adamw(two(ada_w), [two(g_ada_w)], two(m_ada_w), two(v_ada_w), "adamw_ada_w")]

    def mine(n, g):
        if n in ("conv_w", "conv_b", "sg_v_g"):
            size = w[n].shape[-1]
            return lax.dynamic_slice_in_dim(g, chip * size, size, axis=g.ndim - 1)
        return g

    g_loc = [mine(n, gsum[n]).reshape(w[n].shape) for n in SMALL]
    small_shapes = [tuple(w[n].shape) for n in SMALL]
    packed = _adamw(_pack_flat([w[n] for n in SMALL]), [_pack_flat(g_loc)], _pack_flat([m[n] for n in SMALL]), _pack_flat([v[n] for n in SMALL]),
                    "adamw_small")
    unpacked = [_unpack(t, small_shapes) for t in packed]
    for k, n in enumerate(SMALL):
        res[n] = [unpacked[0][k], unpacked[1][k], unpacked[2][k], unpacked[3][k]]

    q_mine = [jnp.stack(sums[n]) for n in BIG]
    q_sib = [jnp.stack([sib[(n, li)] for li in range(w[n].shape[0])]) for n in BIG]
    for n, qa, qb in zip(BIG, q_mine, q_sib):
        shp = w[n].shape
        two = lambda t, shp=shp: t.reshape(shp[0] * shp[1], shp[2])
        res[n] = [t.reshape(shp) for t in _adamw(two(w[n]), [two(qa), two(qb)], two(m[n]), two(v[n]), f"adamw_{n}")]

    outs = [loss, grad_x]
    for part in range(4):
        outs += [res[n][part] for n in WEIGHTS]
    return tuple(outs)
```

```python
import functools

import jax
import jax.numpy as jnp
from jax import lax
from jax.experimental import pallas as pl
from jax.experimental.pallas import tpu as pltpu

F32 = jnp.float32
BF16 = jnp.bfloat16
D = 1024
EPS = 1e-6
DEPTH = 4
MIXER_OF_LAYER = (0, 1, 2, 0)
S5_G, S5_H, S5_P = 64, 16, 64
S5_NB = 4
S5_BH = S5_H * 16
S5_BP = S5_P * 16
NSTATE = S5_G * S5_P
SG_HEADS, SG_CHUNK = 8, 128
ADAM_LR, ADAM_B1, ADAM_B2, ADAM_EPS, ADAM_WD, ADAM_STEP = 0.001, 0.9, 0.999, 1e-08, 0.01, 10
N_DEV = 8
MESH = pl.DeviceIdType.MESH
LANES = 1024
R_SH1, R_SC1, R_G1, R_SH2, R_SC2, R_G2, R_N1, R_N2 = range(8)


def _dot(a, b):
    return jnp.dot(a, b, preferred_element_type=F32)


def _dot_nt(a, b):
    return lax.dot_general(a, b, (((1,), (1,)), ((), ())), preferred_element_type=F32)


def _dot_tn(a, b):
    return lax.dot_general(a, b, (((0,), (0,)), ((), ())), preferred_element_type=F32)


def _bf(x):
    return x.astype(BF16)


def _sum0(x):
    return jnp.sum(x, axis=0, keepdims=True)


def _params(n_axes, vmem_mb=48):
    return pltpu.CompilerParams(dimension_semantics=("arbitrary",) * n_axes, vmem_limit_bytes=vmem_mb << 20)


def _rows(tm, cols, nt=None):
    if nt is None:
        return pl.BlockSpec((tm, cols), lambda i: (i, 0))
    return pl.BlockSpec((tm, cols), lambda i: (nt - 1 - i, 0))


def _whole(shape):
    nd = len(shape)
    return pl.BlockSpec(shape, lambda *_: (0,) * nd)


def _layer_w(r, c, layer):
    return pl.BlockSpec((None, r, c), lambda *_: (layer, 0, 0), pipeline_mode=pl.Buffered(1))


def _const_w(shape):
    nd = len(shape)
    return pl.BlockSpec(shape, lambda *_: (0,) * nd, pipeline_mode=pl.Buffered(1))


def _call_after(after, body, n_in, *, in_specs, **kw):
    if after is None:
        return pl.pallas_call(body, in_specs=in_specs, **kw), ()

    def body_after(*refs):
        return body(*refs[:n_in], *refs[n_in + 1:])

    return pl.pallas_call(body_after, in_specs=list(in_specs) + [pl.BlockSpec(memory_space=pl.ANY)], **kw), (after,)


def _norm_mod(x, ng, sc, sh):
    r = lax.rsqrt(jnp.mean(x * x, axis=-1, keepdims=True) + EPS)
    xn = x * r
    return (xn * ng) * (1.0 + sc) + sh, xn, r


def _norm_mod_bwd(dh, xn, r, ng, sc):
    dxn = dh * (ng * (1.0 + sc))
    return r * (dxn - xn * jnp.mean(dxn * xn, axis=-1, keepdims=True))


def _shift_down(z, prev8, k):
    row = lax.broadcasted_iota(jnp.int32, z.shape, 0)
    if k == 1:
        return jnp.where(row >= 1, pltpu.roll(z, 1, 0), prev8[7:8])
    return jnp.where(row >= 2, pltpu.roll(z, 2, 0), jnp.where(row == 0, prev8[6:7], prev8[7:8]))


def _shift_up(z, next8, k):
    n = z.shape[0]
    row = lax.broadcasted_iota(jnp.int32, z.shape, 0)
    if k == 1:
        return jnp.where(row <= n - 2, pltpu.roll(z, n - 1, 0), next8[0:1])
    return jnp.where(row <= n - 3, pltpu.roll(z, n - 2, 0), jnp.where(row == n - 2, next8[0:1], next8[1:2]))


def _place():
    x, y, c = lax.axis_index("x"), lax.axis_index("y"), lax.axis_index("c")
    chips = [(1 - x, y), (x, 1 - y), (1 - x, 1 - y)]
    return x, y, c, chips


def _allgather_small(x_shard, name, after=None):
    m_per, n = x_shard.shape

    def body(x_ref, out_ref, send_sems, recv_sems, local_sem):
        x, y, c, chips = _place()
        me, sibling = (x, y, c), (x, y, 1 - c)

        def rows(px, py, pc):
            return out_ref.at[pl.ds((4 * px + 2 * py + pc) * m_per, m_per), :]

        def copy(k, block, to, src=None):
            return pltpu.make_async_remote_copy(
                src_ref=rows(*block) if src is None else src, dst_ref=rows(*block),
                send_sem=send_sems.at[k], recv_sem=recv_sems.at[k], device_id=to, device_id_type=MESH)

        mine = pltpu.make_async_copy(x_ref, rows(*me), local_sem)
        mine.start()
        first = [copy(0, me, sibling, src=x_ref)]
        first += [copy(1 + j, me, (*chip, c), src=x_ref) for j, chip in enumerate(chips)]
        for cp in first:
            cp.start()
        passed = [copy(4 + j, (*chip, c), sibling) for j, chip in enumerate(chips)]
        for j, chip in enumerate(chips):
            copy(1 + j, (*chip, c), me).wait_recv()
            passed[j].start()
        copy(0, sibling, me).wait_recv()
        for j, chip in enumerate(chips):
            copy(4 + j, (*chip, 1 - c), me).wait_recv()
        for cp in first + passed:
            cp.wait_send()
        mine.wait()

    call, tail = _call_after(
        after, body, 1, name=name, out_shape=jax.ShapeDtypeStruct((N_DEV * m_per, n), F32),
        in_specs=[pl.BlockSpec(memory_space=pltpu.VMEM)], out_specs=pl.BlockSpec(memory_space=pltpu.VMEM),
        scratch_shapes=[pltpu.SemaphoreType.DMA((7,)), pltpu.SemaphoreType.DMA((7,)), pltpu.SemaphoreType.DMA],
        compiler_params=pltpu.CompilerParams(vmem_limit_bytes=48 << 20),
    )
    return call(x_shard, *tail)


def _allreduce_small(x_part, name, after=None):
    m, n = x_part.shape
    h = m // 2

    def body(x_ref, out_ref, sib_buf, slots, send_sems, recv_sems):
        x, y, c, chips = _place()
        k_me = 2 * x + y
        sibling = (x, y, 1 - c)
        mine = pl.ds(pl.multiple_of(c * h, 8), h)

        def copy(k, src, dst, to):
            return pltpu.make_async_remote_copy(src_ref=src, dst_ref=dst, send_sem=send_sems.at[k], recv_sem=recv_sems.at[k],
                                                device_id=to, device_id_type=MESH)

        swap = copy(0, x_ref, sib_buf, sibling)
        swap.start()
        swap.wait()
        slots[pl.ds(k_me, 1)] = (x_ref[mine, :] + sib_buf[mine, :])[None]
        my_slot = slots.at[pl.ds(k_me, 1)]
        sends = [copy(1 + j, my_slot, my_slot, (*chip, c)) for j, chip in enumerate(chips)]
        for cp in sends:
            cp.start()
        for j, chip in enumerate(chips):
            their_slot = slots.at[pl.ds(2 * chip[0] + chip[1], 1)]
            copy(1 + j, their_slot, their_slot, (x, y, c)).wait_recv()
        for cp in sends:
            cp.wait_send()
        out_ref[mine, :] = ((slots[0] + slots[1]) + slots[2]) + slots[3]
        give = copy(4, out_ref.at[mine, :], out_ref.at[mine, :], sibling)
        give.start()
        give.wait_send()
        theirs = pl.ds(pl.multiple_of((1 - c) * h, 8), h)
        copy(4, out_ref.at[theirs, :], out_ref.at[theirs, :], (x, y, c)).wait_recv()

    call, tail = _call_after(
        after, body, 1, name=name, out_shape=jax.ShapeDtypeStruct((m, n), F32),
        in_specs=[pl.BlockSpec(memory_space=pltpu.VMEM)], out_specs=pl.BlockSpec(memory_space=pltpu.VMEM),
        scratch_shapes=[pltpu.VMEM((m, n), F32), pltpu.VMEM((4, h, n), F32),
                        pltpu.SemaphoreType.DMA((5,)), pltpu.SemaphoreType.DMA((5,))],
        compiler_params=pltpu.CompilerParams(vmem_limit_bytes=48 << 20),
    )
    return call(x_part, *tail)


def _shard_region(ref, full_shape, axis, chip_k, half=None):
    _, r, c = full_shape
    if axis == 1:
        rs = r // 4
        if half is None:
            return ref.at[:, pl.ds(pl.multiple_of(chip_k * rs, 128), rs), :]
        return ref.at[:, pl.ds(pl.multiple_of(chip_k * rs + half * (rs // 2), 128), rs // 2), :]
    cs = c // 4
    if half is None:
        return ref.at[:, :, pl.ds(pl.multiple_of(chip_k * cs, 128), cs)]
    return ref.at[:, pl.ds(pl.multiple_of(half * (r // 2), 128), r // 2), pl.ds(pl.multiple_of(chip_k * cs, 128), cs)]


HBM_SPEC = pl.BlockSpec(memory_space=pltpu.HBM)
SEM_SPEC = pl.BlockSpec(memory_space=pltpu.SEMAPHORE)
ANY_SPEC = pl.BlockSpec(memory_space=pl.ANY)
SPLIT_COPY_PARAMS = pltpu.CompilerParams(has_side_effects=pltpu.SideEffectType.DATAFLOW_SIDE_EFFECTING)


def _in_hbm(arrs):
    return [pltpu.with_memory_space_constraint(a, pltpu.HBM) for a in arrs]


def _cast_place(w_stack, li, axis, chip, name):
    _, r, c = w_stack.shape
    full = (1, 4 * r, c) if axis == 1 else (1, r, 4 * c)
    tr = min(r, 256)
    if axis == 1:
        out_spec = pl.BlockSpec((None, tr, c), lambda i, k: (0, k[0] * (r // tr) + i, 0))
    else:
        out_spec = pl.BlockSpec((None, tr, c), lambda i, k: (0, i, k[0]))

    def body(k_ref, w_ref, o_ref):
        o_ref[...] = _bf(w_ref[...])

    return pl.pallas_call(
        body, name=name,
        grid_spec=pltpu.PrefetchScalarGridSpec(
            num_scalar_prefetch=1, grid=(r // tr,),
            in_specs=[pl.BlockSpec((None, tr, c), lambda i, k: (li, i, 0))], out_specs=out_spec),
        out_shape=jax.ShapeDtypeStruct(full, BF16),
        compiler_params=_params(1),
    )(chip, w_stack)


def _gather_start(lands, axes, name, after):
    n_arr = len(lands)
    fulls = [tuple(l.shape) for l in lands]

    def body(*refs):
        land = refs[:n_arr]
        send_sems, recv_sems = refs[n_arr + 1:n_arr + 3]
        token = refs[-1]
        x, y, c, chips = _place()
        k_me = 2 * x + y
        for a in range(n_arr):
            mine = _shard_region(land[a], fulls[a], axes[a], k_me, c)
            for j, chip in enumerate(chips):
                pltpu.make_async_remote_copy(
                    src_ref=mine, dst_ref=mine, send_sem=send_sems.at[a * 3 + j], recv_sem=recv_sems.at[a * 3 + j],
                    device_id=(*chip, c), device_id_type=MESH).start()
        token[...] = jnp.zeros_like(token)

    res = pl.pallas_call(
        body, name=name,
        out_shape=(pltpu.SemaphoreType.DMA((3 * n_arr,)), pltpu.SemaphoreType.DMA((3 * n_arr,)),
                   *[pltpu.HBM(f, BF16) for f in fulls], jax.ShapeDtypeStruct((8, 128), F32)),
        in_specs=[HBM_SPEC] * n_arr + [ANY_SPEC],
        out_specs=(SEM_SPEC, SEM_SPEC, *[HBM_SPEC] * n_arr, pl.BlockSpec(memory_space=pltpu.VMEM)),
        input_output_aliases={a: 2 + a for a in range(n_arr)},
        compiler_params=SPLIT_COPY_PARAMS,
    )(*_in_hbm(lands), after)
    return res[0], res[1], list(res[2:2 + n_arr]), res[-1]


def _gather_wait(send_sems, recv_sems, lands, axes, name, after):
    n_arr = len(lands)
    fulls = [tuple(l.shape) for l in lands]

    def body(*refs):
        land = refs[:n_arr]
        s_sems, r_sems = refs[n_arr:n_arr + 2]
        x, y, c, chips = _place()
        for a in range(n_arr):
            for j, chip in enumerate(chips):
                k_j = 2 * chip[0] + chip[1]
                got = _shard_region(land[a], fulls[a], axes[a], k_j, c)
                cp = pltpu.make_async_remote_copy(
                    src_ref=got, dst_ref=got, send_sem=s_sems.at[a * 3 + j], recv_sem=r_sems.at[a * 3 + j],
                    device_id=(x, y, c), device_id_type=MESH)
                cp.wait_send()
                cp.wait_recv()

    res = pl.pallas_call(
        body, name=name,
        out_shape=tuple(pltpu.HBM(f, BF16) for f in fulls),
        in_specs=[HBM_SPEC] * n_arr + [SEM_SPEC, SEM_SPEC, ANY_SPEC],
        out_specs=tuple([HBM_SPEC] * n_arr),
        input_output_aliases={a: a for a in range(n_arr)},
        compiler_params=SPLIT_COPY_PARAMS,
    )(*lands, send_sems, recv_sems, after)
    return list(res)


def _gather_share(lands, axes, name):
    n_arr = len(lands)
    fulls = [tuple(l.shape) for l in lands]

    def body(*refs):
        land_in, land = refs[:n_arr], refs[n_arr:2 * n_arr]
        send_sems, recv_sems = refs[2 * n_arr:]
        x, y, c, chips = _place()
        copies = []
        for a in range(n_arr):
            for j, chip in enumerate(chips):
                k_j = 2 * chip[0] + chip[1]
                cp = pltpu.make_async_remote_copy(
                    src_ref=_shard_region(land_in[a], fulls[a], axes[a], k_j, c),
                    dst_ref=_shard_region(land[a], fulls[a], axes[a], k_j, c),
                    send_sem=send_sems.at[a * 3 + j], recv_sem=recv_sems.at[a * 3 + j],
                    device_id=(x, y, 1 - c), device_id_type=MESH)
                cp.start()
                copies.append(cp)
        for cp in copies:
            cp.wait()

    return pl.pallas_call(
        body, name=name, out_shape=[jax.ShapeDtypeStruct(f, BF16) for f in fulls],
        in_specs=[ANY_SPEC] * n_arr, out_specs=[ANY_SPEC] * n_arr,
        input_output_aliases={a: a for a in range(n_arr)},
        scratch_shapes=[pltpu.SemaphoreType.DMA((3 * n_arr,)), pltpu.SemaphoreType.DMA((3 * n_arr,))],
    )(*lands)


def _scatter_shapes(grads, axes):
    out = []
    for g, ax in zip(grads, axes):
        shp = list(g.shape)
        shp[ax] //= 4
        out.append((3,) + tuple(shp[1:]))
    return out


def _scatter_start(grads, axes, name, after):
    n_arr = len(grads)
    shapes = _scatter_shapes(grads, axes)
    lands = [lax.empty(s, BF16) for s in shapes]

    def body(*refs):
        ins, land = refs[:n_arr], refs[n_arr:2 * n_arr]
        send_sems, recv_sems = refs[2 * n_arr + 1:2 * n_arr + 3]
        token = refs[-1]
        x, y, c, chips = _place()
        for a in range(n_arr):
            for j, chip in enumerate(chips):
                k_j = 2 * chip[0] + chip[1]
                pltpu.make_async_remote_copy(
                    src_ref=_shard_region(ins[a], grads[a].shape, axes[a], k_j), dst_ref=land[a].at[pl.ds(j, 1)],
                    send_sem=send_sems.at[a * 3 + j], recv_sem=recv_sems.at[a * 3 + j],
                    device_id=(*chip, c), device_id_type=MESH).start()
        token[...] = jnp.zeros_like(token)

    res = pl.pallas_call(
        body, name=name,
        out_shape=(pltpu.SemaphoreType.DMA((3 * n_arr,)), pltpu.SemaphoreType.DMA((3 * n_arr,)),
                   *[pltpu.HBM(g.shape, BF16) for g in grads], *[pltpu.HBM(s, BF16) for s in shapes],
                   jax.ShapeDtypeStruct((8, 128), F32)),
        in_specs=[HBM_SPEC] * (2 * n_arr) + [ANY_SPEC],
        out_specs=(SEM_SPEC, SEM_SPEC, *[HBM_SPEC] * (2 * n_arr), pl.BlockSpec(memory_space=pltpu.VMEM)),
        input_output_aliases={a: 2 + a for a in range(2 * n_arr)},
        compiler_params=SPLIT_COPY_PARAMS,
    )(*_in_hbm(grads), *_in_hbm(lands), after)
    return res[0], res[1], list(res[2:2 + n_arr]), list(res[2 + n_arr:2 + 2 * n_arr]), res[-1]


def _scatter_wait(send_sems, recv_sems, grads, lands, axes, name, after):
    n_arr = len(grads)

    def body(*refs):
        ins, land = refs[:n_arr], refs[n_arr:2 * n_arr]
        s_sems, r_sems = refs[2 * n_arr:2 * n_arr + 2]
        x, y, c, chips = _place()
        for a in range(n_arr):
            for j, chip in enumerate(chips):
                k_j = 2 * chip[0] + chip[1]
                cp = pltpu.make_async_remote_copy(
                    src_ref=_shard_region(ins[a], grads[a].shape, axes[a], k_j), dst_ref=land[a].at[pl.ds(j, 1)],
                    send_sem=s_sems.at[a * 3 + j], recv_sem=r_sems.at[a * 3 + j],
                    device_id=(x, y, c), device_id_type=MESH)
                cp.wait_send()
                cp.wait_recv()

    res = pl.pallas_call(
        body, name=name,
        out_shape=(*[pltpu.HBM(g.shape, BF16) for g in grads], *[pltpu.HBM(l.shape, BF16) for l in lands]),
        in_specs=[HBM_SPEC] * (2 * n_arr) + [SEM_SPEC, SEM_SPEC, ANY_SPEC],
        out_specs=tuple([HBM_SPEC] * (2 * n_arr)),
        input_output_aliases={a: a for a in range(2 * n_arr)},
        compiler_params=SPLIT_COPY_PARAMS,
    )(*grads, *lands, send_sems, recv_sems, after)
    return list(res[:n_arr]), list(res[n_arr:])


def _swap_with_sibling(arrs, name):
    n_arr = len(arrs)

    def body(*refs):
        ins, outs = refs[:n_arr], refs[n_arr:2 * n_arr]
        send_sems, recv_sems = refs[2 * n_arr:]
        x, y, c, _ = _place()
        copies = []
        for a in range(n_arr):
            cp = pltpu.make_async_remote_copy(
                src_ref=ins[a], dst_ref=outs[a], send_sem=send_sems.at[a], recv_sem=recv_sems.at[a],
                device_id=(x, y, 1 - c), device_id_type=MESH)
            cp.start()
            copies.append(cp)
        for cp in copies:
            cp.wait()

    any_spec = pl.BlockSpec(memory_space=pl.ANY)
    return pl.pallas_call(
        body, name=name, out_shape=[jax.ShapeDtypeStruct(a.shape, a.dtype) for a in arrs],
        in_specs=[any_spec] * n_arr, out_specs=[any_spec] * n_arr,
        scratch_shapes=[pltpu.SemaphoreType.DMA((n_arr,)), pltpu.SemaphoreType.DMA((n_arr,))],
    )(*arrs)


def _mm_tn(a, b, name, out_dtype=BF16):
    L, m = a.shape
    n = b.shape[1]
    bm, bn, bk = min(m, 1024), min(n, 1024), min(L, 2048)
    nk = L // bk

    def body(a_ref, b_ref, o_ref, acc):
        k = pl.program_id(2)

        @pl.when(k == 0)
        def _():
            acc[...] = jnp.zeros_like(acc)

        acc[...] += _dot_tn(_bf(a_ref[...]), _bf(b_ref[...]))

        @pl.when(k == nk - 1)
        def _():
            o_ref[...] = acc[...].astype(out_dtype)

    return pl.pallas_call(
        body, name=name, grid=(m // bm, n // bn, nk),
        in_specs=[pl.BlockSpec((bk, bm), lambda i, j, k: (k, i)), pl.BlockSpec((bk, bn), lambda i, j, k: (k, j))],
        out_specs=pl.BlockSpec((bm, bn), lambda i, j, k: (i, j)),
        out_shape=jax.ShapeDtypeStruct((m, n), out_dtype),
        scratch_shapes=[pltpu.VMEM((bm, bn), F32)],
        compiler_params=_params(3),
    )(a, b)


def _mm_tn_blocks(a, b, wa, wb, name):
    L = a.shape[0]
    nb = a.shape[1] // wa
    bk = min(L, 1024)
    nk = L // bk

    def body(a_ref, b_ref, o_ref):
        @pl.when(pl.program_id(1) == 0)
        def _():
            o_ref[...] = jnp.zeros_like(o_ref)

        o_ref[...] += _dot_tn(_bf(a_ref[...]), _bf(b_ref[...]))

    return pl.pallas_call(
        body, name=name, grid=(nb, nk),
        in_specs=[pl.BlockSpec((bk, wa), lambda j, k: (k, j)), pl.BlockSpec((bk, wb), lambda j, k: (k, j))],
        out_specs=pl.BlockSpec((None, wa, wb), lambda j, k: (j, 0, 0)),
        out_shape=jax.ShapeDtypeStruct((nb, wa, wb), F32),
        compiler_params=_params(2),
    )(a, b)


def _sum_parts(parts, own, axis, chip, name):
    _, r, c = parts.shape
    tr = min(r, 256)
    if axis == 1:
        own_spec = pl.BlockSpec((None, tr, c), lambda i, k: (0, k[0] * (r // tr) + i, 0))
    else:
        own_spec = pl.BlockSpec((None, tr, c), lambda i, k: (0, i, k[0]))

    def body(k_ref, p_ref, g_ref, o_ref):
        p = p_ref[...].astype(F32)
        o_ref[...] = ((p[0] + p[1]) + p[2]) + g_ref[...].astype(F32)

    return pl.pallas_call(
        body, name=name,
        grid_spec=pltpu.PrefetchScalarGridSpec(
            num_scalar_prefetch=1, grid=(r // tr,),
            in_specs=[pl.BlockSpec((3, tr, c), lambda i, k: (0, i, 0)), own_spec],
            out_specs=pl.BlockSpec((tr, c), lambda i, k: (i, 0))),
        out_shape=jax.ShapeDtypeStruct((r, c), F32),
        compiler_params=_params(1),
    )(chip, parts, own)


def _adamw(w, g_parts, m, v, name):
    n_g = len(g_parts)
    if w.ndim == 2:
        r, c = w.shape
        tr = r
        for cand in (512, 256, 128, 64, 32, 16, 8):
            if r % cand == 0 and cand * c * 4 <= (2 << 20):
                tr = cand
                break
        spec = pl.BlockSpec((tr, c), lambda i: (i, 0))
        tiling = dict(grid=(r // tr,), in_specs=[spec] * (3 + n_g), out_specs=[spec] * 4, compiler_params=_params(1))
    else:
        tiling = dict(compiler_params=pltpu.CompilerParams(vmem_limit_bytes=48 << 20))

    def body(*refs):
        w_ref, g_refs, m_ref, v_ref = refs[0], refs[1:1 + n_g], refs[1 + n_g], refs[2 + n_g]
        g = g_refs[0][...]
        for gr in g_refs[1:]:
            g = g + gr[...]
        _adamw_update(g, w_ref, m_ref, v_ref, *refs[3 + n_g:])

    return pl.pallas_call(body, name=name, out_shape=[jax.ShapeDtypeStruct(w.shape, F32)] * 4, **tiling)(w, *g_parts, m, v)


def _adamw_update(g, w_ref, m_ref, v_ref, g_out, d_out, m_out, v_out):
    m_new = ADAM_B1 * m_ref[...] + (1.0 - ADAM_B1) * g
    v_new = ADAM_B2 * v_ref[...] + (1.0 - ADAM_B2) * (g * g)
    m_hat = m_new * (1.0 / (1.0 - ADAM_B1 ** ADAM_STEP))
    v_hat = v_new * (1.0 / (1.0 - ADAM_B2 ** ADAM_STEP))
    g_out[...] = g
    d_out[...] = -ADAM_LR * (m_hat / (jnp.sqrt(v_hat) + ADAM_EPS) + ADAM_WD * w_ref[...])
    m_out[...] = m_new
    v_out[...] = v_new


def _adamw_layers(w, q_mine, q_sib, m, v, name):
    n, r, c = w.shape
    tr = r
    for cand in (512, 256, 128, 64, 32, 16, 8):
        if r % cand == 0 and cand * c * 4 <= (1 << 20):
            tr = cand
            break

    def body(*refs):
        w_ref, qm, qs, m_ref, v_ref = refs[0], refs[1:1 + n], refs[1 + n:1 + 2 * n], refs[1 + 2 * n], refs[2 + 2 * n]
        layer = pl.program_id(0)
        g = qm[0][...] + qs[0][...]
        for k in range(1, n):
            g = jnp.where(layer == k, qm[k][...] + qs[k][...], g)
        _adamw_update(g, w_ref, m_ref, v_ref, *refs[3 + 2 * n:])

    stacked = pl.BlockSpec((None, tr, c), lambda l, i: (l, i, 0))
    per_layer = [pl.BlockSpec((tr, c), lambda l, i, k=k: (jnp.where(l == k, i, 0), 0)) for k in range(n)]
    return pl.pallas_call(
        body, name=name, grid=(n, r // tr),
        in_specs=[stacked] + per_layer + per_layer + [stacked, stacked], out_specs=[stacked] * 4,
        out_shape=[jax.ShapeDtypeStruct(w.shape, F32)] * 4,
        compiler_params=_params(2),
    )(w, *q_mine, *q_sib, m, v)


def _ada_fwd(c16, ada_w, ada_b_cols):
    cols = ada_w.shape[2]

    def body(c_ref, w_ref, b_ref, o_ref):
        cv = c_ref[...]
        ca = _bf(cv * jax.nn.sigmoid(cv))
        o_ref[...] = _dot(ca, _bf(w_ref[...])) + b_ref[...]

    return pl.pallas_call(
        body, name="ada_fwd", grid=(DEPTH,),
        in_specs=[_whole((16, D)), pl.BlockSpec((None, D, cols), lambda i: (i, 0, 0)),
                  pl.BlockSpec((None, 1, cols), lambda i: (i, 0, 0))],
        out_specs=pl.BlockSpec((None, 16, cols), lambda i: (i, 0, 0)),
        out_shape=jax.ShapeDtypeStruct((DEPTH, 16, cols), F32),
        compiler_params=_params(1),
    )(c16, ada_w, ada_b_cols)


def _ada_bwd(c16, dmod16):
    cols = dmod16.shape[2]

    def body(c_ref, d_ref, o_ref):
        cv = c_ref[...]
        ca = _bf(cv * jax.nn.sigmoid(cv))
        o_ref[...] = _dot_tn(ca, _bf(d_ref[...]))

    return pl.pallas_call(
        body, name="ada_bwd", grid=(DEPTH,),
        in_specs=[_whole((16, D)), pl.BlockSpec((None, 16, cols), lambda i: (i, 0, 0))],
        out_specs=pl.BlockSpec((None, D, cols), lambda i: (i, 0, 0)),
        out_shape=jax.ShapeDtypeStruct((DEPTH, D, cols), F32),
        compiler_params=_params(1),
    )(c16, dmod16)


def _mod_bwd(vs_mix, vs_ffn, pv):
    def body(m_ref, f_ref, pv_ref, o_ref):
        for i in range(DEPTH):
            vm, vf, p = m_ref[i], f_ref[i], pv_ref[i]
            o_ref[i] = jnp.concatenate([
                vm[2:3], vm[1:2] * p[R_N1:R_N1 + 1], vm[0:1],
                vf[2:3], vf[1:2] * p[R_N2:R_N2 + 1], vf[0:1],
                vm[1:2] * (1.0 + p[R_SC1:R_SC1 + 1]), vf[1:2] * (1.0 + p[R_SC2:R_SC2 + 1])], axis=0)

    return pl.pallas_call(body, name="mod_bwd", out_shape=jax.ShapeDtypeStruct((DEPTH, 8, D), F32))(vs_mix, vs_ffn, pv)


def _ffn_fwd(x1, pv, w1, w2, layer, tm, after=None):
    L = x1.shape[0]
    dff = w1.shape[2]

    def body(x1_ref, pv_ref, w1_ref, w2_ref, x2_ref, h2_ref, a_ref, f_ref):
        x1v, p = x1_ref[...], pv_ref[...]
        h2, _, _ = _norm_mod(x1v, p[R_N2:R_N2 + 1], p[R_SC2:R_SC2 + 1], p[R_SH2:R_SH2 + 1])
        hb = _bf(h2)
        h2_ref[...] = hb
        a = _dot(hb, w1_ref[...])
        a_ref[...] = a
        ra = jnp.maximum(a, 0.0)
        f = _dot(_bf(ra * ra), w2_ref[...])
        f_ref[...] = f
        x2_ref[...] = x1v + p[R_G2:R_G2 + 1] * f

    call, tail = _call_after(
        after, body, 4, name=f"ffn_fwd{layer}", grid=(L // tm,),
        in_specs=[_rows(tm, D), pl.BlockSpec((None, 8, D), lambda i: (layer, 0, 0)), _layer_w(D, dff, 0), _layer_w(dff, D, 0)],
        out_specs=[_rows(tm, D), _rows(tm, D), _rows(tm, dff), _rows(tm, D)],
        out_shape=[jax.ShapeDtypeStruct((L, D), F32), jax.ShapeDtypeStruct((L, D), BF16),
                   jax.ShapeDtypeStruct((L, dff), F32), jax.ShapeDtypeStruct((L, D), F32)],
        compiler_params=_params(1, 56),
    )
    return call(x1, pv, w1, w2, *tail)


def _ffn_bwd(dx2, x1, a, f, pv, w1, w2, layer, tm, after=None):
    L = x1.shape[0]
    dff = w1.shape[2]
    extra = [] if after is None else [pl.BlockSpec(memory_space=pl.ANY)]
    extra_args = [] if after is None else [after]

    def body(dx2_ref, x1_ref, a_ref, f_ref, pv_ref, w1_ref, w2_ref, *rest):
        dx1_ref, p_ref, da_ref, df_ref, vs_ref = rest[len(extra):]

        @pl.when(pl.program_id(0) == 0)
        def _():
            vs_ref[...] = jnp.zeros_like(vs_ref)

        dx2v, p = dx2_ref[...], pv_ref[...]
        dfb = _bf(dx2v * p[R_G2:R_G2 + 1])
        df_ref[...] = dfb
        vs_ref[0:1, :] += _sum0(dx2v * f_ref[...])
        dp = _dot_nt(dfb, w2_ref[...])
        ra = jnp.maximum(a_ref[...], 0.0)
        p_ref[...] = _bf(ra * ra)
        dab = _bf(dp * (2.0 * ra))
        da_ref[...] = dab
        dh2 = _dot_nt(dab, w1_ref[...])
        _, xn, r = _norm_mod(x1_ref[...], p[R_N2:R_N2 + 1], p[R_SC2:R_SC2 + 1], p[R_SH2:R_SH2 + 1])
        dx1_ref[...] = dx2v + _norm_mod_bwd(dh2, xn, r, p[R_N2:R_N2 + 1], p[R_SC2:R_SC2 + 1])
        vs_ref[1:2, :] += _sum0(dh2 * xn)
        vs_ref[2:3, :] += _sum0(dh2)

    return pl.pallas_call(
        body, name=f"ffn_bwd{layer}", grid=(L // tm,),
        in_specs=[_rows(tm, D), _rows(tm, D), _rows(tm, dff), _rows(tm, D),
                  pl.BlockSpec((None, 8, D), lambda i: (layer, 0, 0)), _layer_w(D, dff, 0), _layer_w(dff, D, 0)] + extra,
        out_specs=[_rows(tm, D), _rows(tm, dff), _rows(tm, dff), _rows(tm, D), _whole((8, D))],
        out_shape=[jax.ShapeDtypeStruct((L, D), F32), jax.ShapeDtypeStruct((L, dff), BF16),
                   jax.ShapeDtypeStruct((L, dff), BF16), jax.ShapeDtypeStruct((L, D), BF16),
                   jax.ShapeDtypeStruct((8, D), F32)],
        compiler_params=_params(1, 56),
    )(dx2, x1, a, f, pv, w1, w2, *extra_args)


def _conv_fwd(x, pv, w_in, w_out, cw, layer, j, tm, after=None):
    L = x.shape[0]

    def body(x_ref, pv_ref, win_ref, wout_ref, cw_ref, x1_ref, h_ref, bcx_ref, conv_ref, q_ref, y_ref, carry):
        @pl.when(pl.program_id(0) == 0)
        def _():
            carry[...] = jnp.zeros_like(carry)

        xv, p, cwv = x_ref[...], pv_ref[...], cw_ref[...]
        h, _, _ = _norm_mod(xv, p[R_N1:R_N1 + 1], p[R_SC1:R_SC1 + 1], p[R_SH1:R_SH1 + 1])
        hb = _bf(h)
        h_ref[...] = hb
        bcx = _dot(hb, win_ref[...])
        bcx_ref[...] = bcx
        z = bcx[:, D:2 * D] * bcx[:, 2 * D:]
        prev8 = carry[...]
        conv = cwv[0:1] * _shift_down(z, prev8, 2) + cwv[1:2] * _shift_down(z, prev8, 1) + cwv[2:3] * z + cwv[3:4]
        conv_ref[...] = conv
        qb = _bf(bcx[:, :D] * conv)
        q_ref[...] = qb
        y = _dot(qb, wout_ref[...])
        y_ref[...] = y
        x1_ref[...] = xv + p[R_G1:R_G1 + 1] * y
        carry[...] = z[tm - 8:tm]

    call, tail = _call_after(
        after, body, 5, name=f"conv_fwd{layer}", grid=(L // tm,),
        in_specs=[_rows(tm, D), pl.BlockSpec((None, 8, D), lambda i: (layer, 0, 0)), _layer_w(D, 3 * D, 0), _layer_w(D, D, 0),
                  pl.BlockSpec((None, 8, D), lambda i: (j, 0, 0))],
        out_specs=[_rows(tm, D), _rows(tm, D), _rows(tm, 3 * D), _rows(tm, D), _rows(tm, D), _rows(tm, D)],
        out_shape=[jax.ShapeDtypeStruct((L, D), F32), jax.ShapeDtypeStruct((L, D), BF16), jax.ShapeDtypeStruct((L, 3 * D), F32),
                   jax.ShapeDtypeStruct((L, D), F32), jax.ShapeDtypeStruct((L, D), BF16), jax.ShapeDtypeStruct((L, D), F32)],
        scratch_shapes=[pltpu.VMEM((8, D), F32)],
        compiler_params=_params(1, 56),
    )
    return call(x, pv, w_in, w_out, cw, *tail)


def _conv_bwd(dx1, x, y, bcx, conv, pv, w_in, w_out, cw, layer, j, tm, after=None):
    L = x.shape[0]
    nt = L // tm

    def body(dx1_ref, x_ref, y_ref, bcx_ref, conv_ref, halo_ref, pv_ref, win_ref, wout_ref, cw_ref,
             dx_ref, dbcx_ref, dy_ref, vs_ref, carry):
        gi = pl.program_id(0)
        tile = nt - 1 - gi

        @pl.when(gi == 0)
        def _():
            vs_ref[...] = jnp.zeros_like(vs_ref)
            carry[...] = jnp.zeros_like(carry)

        dx1v, p, cwv = dx1_ref[...], pv_ref[...], cw_ref[...]
        dyb = _bf(dx1v * p[R_G1:R_G1 + 1])
        dy_ref[...] = dyb
        vs_ref[0:1, :] += _sum0(dx1v * y_ref[...])
        dq = _dot_nt(dyb, wout_ref[...])
        bcx = bcx_ref[...]
        b, cg, xh = bcx[:, :D], bcx[:, D:2 * D], bcx[:, 2 * D:]
        db = dq * conv_ref[...]
        dc = dq * b
        z = cg * xh
        halo = halo_ref[...]
        zprev = jnp.where(tile > 0, halo[:, D:2 * D] * halo[:, 2 * D:], 0.0)
        vs_ref[3:4, :] += _sum0(dc * _shift_down(z, zprev, 2))
        vs_ref[4:5, :] += _sum0(dc * _shift_down(z, zprev, 1))
        vs_ref[5:6, :] += _sum0(dc * z)
        vs_ref[6:7, :] += _sum0(dc)
        next8 = carry[...]
        dz = cwv[2:3] * dc + cwv[1:2] * _shift_up(dc, next8, 1) + cwv[0:1] * _shift_up(dc, next8, 2)
        dbb, dcgb, dxhb = _bf(db), _bf(dz * xh), _bf(dz * cg)
        dbcx_ref[:, 0:D] = dbb
        dbcx_ref[:, D:2 * D] = dcgb
        dbcx_ref[:, 2 * D:3 * D] = dxhb
        dh = (_dot_nt(dbb, win_ref[:, 0:D]) + _dot_nt(dcgb, win_ref[:, D:2 * D])) + _dot_nt(dxhb, win_ref[:, 2 * D:3 * D])
        _, xn, r = _norm_mod(x_ref[...], p[R_N1:R_N1 + 1], p[R_SC1:R_SC1 + 1], p[R_SH1:R_SH1 + 1])
        dx_ref[...] = dx1v + _norm_mod_bwd(dh, xn, r, p[R_N1:R_N1 + 1], p[R_SC1:R_SC1 + 1])
        vs_ref[1:2, :] += _sum0(dh * xn)
        vs_ref[2:3, :] += _sum0(dh)
        carry[...] = dc[0:8]

    halo_spec = pl.BlockSpec((8, 3 * D), lambda i: (jnp.maximum((nt - 1 - i) * (tm // 8) - 1, 0), 0))
    call, tail = _call_after(
        after, body, 10, name=f"conv_bwd{layer}", grid=(nt,),
        in_specs=[_rows(tm, D, nt), _rows(tm, D, nt), _rows(tm, D, nt), _rows(tm, 3 * D, nt), _rows(tm, D, nt), halo_spec,
                  pl.BlockSpec((None, 8, D), lambda i: (layer, 0, 0)), _layer_w(D, 3 * D, 0), _layer_w(D, D, 0),
                  pl.BlockSpec((None, 8, D), lambda i: (j, 0, 0))],
        out_specs=[_rows(tm, D, nt), _rows(tm, 3 * D, nt), _rows(tm, D, nt), _whole((8, D))],
        out_shape=[jax.ShapeDtypeStruct((L, D), F32), jax.ShapeDtypeStruct((L, 3 * D), BF16),
                   jax.ShapeDtypeStruct((L, D), BF16), jax.ShapeDtypeStruct((8, D), F32)],
        scratch_shapes=[pltpu.VMEM((8, D), F32)],
        compiler_params=_params(1, 56),
    )
    return call(dx1, x, y, bcx, conv, bcx, pv, w_in, w_out, cw, *tail)


def _s5_discretize(a_re, a_im, log_dt, bt_re, bt_im):
    dt = jnp.exp(log_dt)
    mag = jnp.exp(a_re * dt)
    abar_re = mag * jnp.cos(a_im * dt)
    abar_im = mag * jnp.sin(a_im * dt)
    den = a_re * a_re + a_im * a_im
    nr = abar_re - 1.0
    ni = abar_im
    f_re = (nr * a_re + ni * a_im) / den
    f_im = (ni * a_re - nr * a_im) / den
    bbar_re = f_re * bt_re - f_im * bt_im
    bbar_im = f_re * bt_im + f_im * bt_re
    return abar_re, abar_im, bbar_re, bbar_im


def _s5_params_fwd(a_re, a_im, log_dt, bt_re, bt_im):
    def body(ar, ai, ld, br, bi, o_ar, o_ai, o_br, o_bi):
        r = _s5_discretize(ar[...], ai[...], ld[...], br[...], bi[...])
        o_ar[...], o_ai[...], o_br[...], o_bi[...] = r

    gp = jax.ShapeDtypeStruct((S5_G, S5_P), F32)
    hgp = jax.ShapeDtypeStruct((S5_H, S5_G, S5_P), F32)
    return pl.pallas_call(body, name="s5_params_fwd", out_shape=[gp, gp, hgp, hgp])(a_re, a_im, log_dt, bt_re, bt_im)


def _s5_params_bwd(a_re, a_im, log_dt, bt_re, bt_im, d_ar, d_ai, d_br, d_bi):
    def body(ar, ai, ld, br, bi, gar, gai, gbr, gbi, o_ar, o_ai, o_ld, o_br, o_bi):
        _, vjp = jax.vjp(_s5_discretize, ar[...], ai[...], ld[...], br[...], bi[...])
        r = vjp((gar[...], gai[...], gbr[...], gbi[...]))
        o_ar[...], o_ai[...], o_ld[...], o_br[...], o_bi[...] = r

    gp = jax.ShapeDtypeStruct((S5_G, S5_P), F32)
    hgp = jax.ShapeDtypeStruct((S5_H, S5_G, S5_P), F32)
    return pl.pallas_call(body, name="s5_params_bwd", out_shape=[gp, gp, jax.ShapeDtypeStruct((S5_G, 1), F32), hgp, hgp])(
        a_re, a_im, log_dt, bt_re, bt_im, d_ar, d_ai, d_br, d_bi)


def _s5_in_fwd(x, pv, w_in, b_re, b_im, layer, tm, after=None):
    L = x.shape[0]

    def body(x_ref, pv_ref, win_ref, bre_ref, bim_ref, h_ref, u_ref, ore_ref, oim_ref):
        p = pv_ref[...]
        h, _, _ = _norm_mod(x_ref[...], p[R_N1:R_N1 + 1], p[R_SC1:R_SC1 + 1], p[R_SH1:R_SH1 + 1])
        hb = _bf(h)
        h_ref[...] = hb
        u = _dot(hb, win_ref[...])
        u_ref[...] = u
        ub = _bf(u)
        for k in range(S5_NB):
            uk = ub[:, k * S5_BH:(k + 1) * S5_BH]
            ore_ref[:, k * S5_BP:(k + 1) * S5_BP] = _dot(uk, bre_ref[k])
            oim_ref[:, k * S5_BP:(k + 1) * S5_BP] = _dot(uk, bim_ref[k])

    call, tail = _call_after(
        after, body, 5, name="s5_in_fwd", grid=(L // tm,),
        in_specs=[_rows(tm, D), pl.BlockSpec((None, 8, D), lambda i: (layer, 0, 0)), _layer_w(D, D, 0),
                  _const_w((S5_NB, S5_BH, S5_BP)), _const_w((S5_NB, S5_BH, S5_BP))],
        out_specs=[_rows(tm, D), _rows(tm, D), _rows(tm, NSTATE), _rows(tm, NSTATE)],
        out_shape=[jax.ShapeDtypeStruct((L, D), BF16), jax.ShapeDtypeStruct((L, D), F32),
                   jax.ShapeDtypeStruct((L, NSTATE), F32), jax.ShapeDtypeStruct((L, NSTATE), F32)],
        compiler_params=_params(1, 56),
    )
    return call(x, pv, w_in, b_re, b_im, *tail)


def _s5_scan_fwd(bu_re, bu_im, ar, ai, tr):
    L = bu_re.shape[0]
    nl = 1024

    def body(bre_ref, bim_ref, ar_ref, ai_ref, sre_ref, sim_ref, st_re, st_im):
        @pl.when(pl.program_id(1) == 0)
        def _():
            st_re[...] = jnp.zeros_like(st_re)
            st_im[...] = jnp.zeros_like(st_im)

        a_r, a_i = ar_ref[...], ai_ref[...]

        def step(t, carry):
            s_r, s_i = carry
            n_r = a_r * s_r - a_i * s_i + bre_ref[pl.ds(t, 1), :]
            n_i = a_r * s_i + a_i * s_r + bim_ref[pl.ds(t, 1), :]
            sre_ref[pl.ds(t, 1), :] = n_r
            sim_ref[pl.ds(t, 1), :] = n_i
            return n_r, n_i

        s_r, s_i = lax.fori_loop(0, tr, step, (st_re[...], st_im[...]), unroll=8)
        st_re[...] = s_r
        st_im[...] = s_i

    blk = pl.BlockSpec((tr, nl), lambda j, i: (i, j))
    vec = pl.BlockSpec((1, nl), lambda j, i: (0, j))
    return pl.pallas_call(
        body, name="s5_scan_fwd", grid=(NSTATE // nl, L // tr),
        in_specs=[blk, blk, vec, vec], out_specs=[blk, blk],
        out_shape=[jax.ShapeDtypeStruct((L, NSTATE), F32)] * 2,
        scratch_shapes=[pltpu.VMEM((1, nl), F32), pltpu.VMEM((1, nl), F32)],
        compiler_params=_params(2),
    )(bu_re, bu_im, ar, ai)


def _s5_out_fwd(x, u, s_re, s_im, pv, c_re, c_im, dvec, glu_w, glu_b, w_out, layer, tm):
    L = x.shape[0]

    def body(x_ref, u_ref, sre_ref, sim_ref, pv_ref, cre_ref, cim_ref, d_ref, gw_ref, gb_ref, wout_ref,
             x1_ref, y1_ref, zg_ref, y3_ref, y_ref):
        p = pv_ref[...]
        srb, sib = _bf(sre_ref[...]), _bf(sim_ref[...])
        parts = []
        for k in range(S5_NB):
            sl = slice(k * S5_BP, (k + 1) * S5_BP)
            parts.append(_dot(srb[:, sl], cre_ref[k]) - _dot(sib[:, sl], cim_ref[k]))
        y1 = jnp.concatenate(parts, axis=1) + d_ref[...] * u_ref[...]
        y1_ref[...] = y1
        y2 = jax.nn.gelu(y1)
        zg = _dot(_bf(y2), gw_ref[...]) + gb_ref[...]
        zg_ref[...] = zg
        y3b = _bf(y2 * jax.nn.sigmoid(zg))
        y3_ref[...] = y3b
        y = _dot(y3b, wout_ref[...])
        y_ref[...] = y
        x1_ref[...] = x_ref[...] + p[R_G1:R_G1 + 1] * y

    return pl.pallas_call(
        body, name="s5_out_fwd", grid=(L // tm,),
        in_specs=[_rows(tm, D), _rows(tm, D), _rows(tm, NSTATE), _rows(tm, NSTATE),
                  pl.BlockSpec((None, 8, D), lambda i: (layer, 0, 0)),
                  _const_w((S5_NB, S5_BP, S5_BH)), _const_w((S5_NB, S5_BP, S5_BH)), _whole((1, D)),
                  _layer_w(D, D, 0), _whole((1, D)), _layer_w(D, D, 0)],
        out_specs=[_rows(tm, D)] * 5,
        out_shape=[jax.ShapeDtypeStruct((L, D), F32), jax.ShapeDtypeStruct((L, D), F32), jax.ShapeDtypeStruct((L, D), F32),
                   jax.ShapeDtypeStruct((L, D), BF16), jax.ShapeDtypeStruct((L, D), F32)],
        compiler_params=_params(1, 56),
    )(x, u, s_re, s_im, pv, c_re, c_im, dvec, glu_w, glu_b, w_out)


def _s5_out_bwd(dx1, y, y1, zg, u, pv, c_re, c_im, dvec, glu_w, w_out, layer, tm, after=None):
    L = dx1.shape[0]

    def body(dx1_ref, y_ref, y1_ref, zg_ref, u_ref, pv_ref, cre_ref, cim_ref, d_ref, gw_ref, wout_ref,
             dy_ref, y2_ref, dzg_ref, dy1_ref, dus_ref, gre_ref, gim_ref, vs_ref):
        @pl.when(pl.program_id(0) == 0)
        def _():
            vs_ref[...] = jnp.zeros_like(vs_ref)

        dx1v, p = dx1_ref[...], pv_ref[...]
        dyb = _bf(dx1v * p[R_G1:R_G1 + 1])
        dy_ref[...] = dyb
        vs_ref[0:1, :] += _sum0(dx1v * y_ref[...])
        dy3 = _dot_nt(dyb, wout_ref[...])
        y2, gelu_vjp = jax.vjp(jax.nn.gelu, y1_ref[...])
        y2_ref[...] = _bf(y2)
        gate = jax.nn.sigmoid(zg_ref[...])
        dzg = dy3 * y2 * gate * (1.0 - gate)
        dzgb = _bf(dzg)
        dzg_ref[...] = dzgb
        vs_ref[1:2, :] += _sum0(dzg)
        dy2 = dy3 * gate + _dot_nt(dzgb, gw_ref[...])
        dy1 = gelu_vjp(dy2)[0]
        vs_ref[2:3, :] += _sum0(dy1 * u_ref[...])
        dus_ref[...] = dy1 * d_ref[...]
        dy1b = _bf(dy1)
        dy1_ref[...] = dy1b
        for k in range(S5_NB):
            dk = dy1b[:, k * S5_BH:(k + 1) * S5_BH]
            gre_ref[:, k * S5_BP:(k + 1) * S5_BP] = _dot_nt(dk, cre_ref[k])
            gim_ref[:, k * S5_BP:(k + 1) * S5_BP] = -_dot_nt(dk, cim_ref[k])

    call, tail = _call_after(
        after, body, 11, name="s5_out_bwd", grid=(L // tm,),
        in_specs=[_rows(tm, D)] * 5 + [pl.BlockSpec((None, 8, D), lambda i: (layer, 0, 0)),
                  _const_w((S5_NB, S5_BP, S5_BH)), _const_w((S5_NB, S5_BP, S5_BH)), _whole((1, D)),
                  _layer_w(D, D, 0), _layer_w(D, D, 0)],
        out_specs=[_rows(tm, D)] * 5 + [_rows(tm, NSTATE), _rows(tm, NSTATE), _whole((8, D))],
        out_shape=[jax.ShapeDtypeStruct((L, D), BF16)] * 4 + [jax.ShapeDtypeStruct((L, D), F32),
                   jax.ShapeDtypeStruct((L, NSTATE), F32), jax.ShapeDtypeStruct((L, NSTATE), F32),
                   jax.ShapeDtypeStruct((8, D), F32)],
        compiler_params=_params(1, 56),
    )
    return call(dx1, y, y1, zg, u, pv, c_re, c_im, dvec, glu_w, w_out, *tail)


def _s5_scan_bwd(g_re, g_im, s_re, s_im, ar, ai, tr):
    L = g_re.shape[0]
    nl = 1024
    nt = L // tr

    def body(gre_ref, gim_ref, sre_ref, sim_ref, hre_ref, him_ref, ar_ref, ai_ref, lre_ref, lim_ref, da_ref, st_re, st_im):
        gi = pl.program_id(1)
        tile = nt - 1 - gi

        @pl.when(gi == 0)
        def _():
            st_re[...] = jnp.zeros_like(st_re)
            st_im[...] = jnp.zeros_like(st_im)
            da_ref[...] = jnp.zeros_like(da_ref)

        a_r, a_i = ar_ref[...], ai_ref[...]

        def step(k, carry):
            l_r, l_i = carry
            t = tr - 1 - k
            n_r = gre_ref[pl.ds(t, 1), :] + a_r * l_r + a_i * l_i
            n_i = gim_ref[pl.ds(t, 1), :] - a_i * l_r + a_r * l_i
            lre_ref[pl.ds(t, 1), :] = n_r
            lim_ref[pl.ds(t, 1), :] = n_i
            return n_r, n_i

        l_r, l_i = lax.fori_loop(0, tr, step, (st_re[...], st_im[...]), unroll=8)
        st_re[...] = l_r
        st_im[...] = l_i
        lam_r, lam_i = lre_ref[...], lim_ref[...]
        p_r = jnp.where(tile > 0, hre_ref[...], 0.0)
        p_i = jnp.where(tile > 0, him_ref[...], 0.0)
        sp_r = _shift_down(sre_ref[...], p_r, 1)
        sp_i = _shift_down(sim_ref[...], p_i, 1)
        da_ref[0:1, :] += _sum0(lam_r * sp_r + lam_i * sp_i)
        da_ref[1:2, :] += _sum0(lam_i * sp_r - lam_r * sp_i)

    blk = pl.BlockSpec((tr, nl), lambda j, i: (nt - 1 - i, j))
    halo = pl.BlockSpec((8, nl), lambda j, i: (jnp.maximum((nt - 1 - i) * (tr // 8) - 1, 0), j))
    vec = pl.BlockSpec((1, nl), lambda j, i: (0, j))
    return pl.pallas_call(
        body, name="s5_scan_bwd", grid=(NSTATE // nl, nt),
        in_specs=[blk, blk, blk, blk, halo, halo, vec, vec],
        out_specs=[blk, blk, pl.BlockSpec((8, nl), lambda j, i: (0, j))],
        out_shape=[jax.ShapeDtypeStruct((L, NSTATE), F32)] * 2 + [jax.ShapeDtypeStruct((8, NSTATE), F32)],
        scratch_shapes=[pltpu.VMEM((1, nl), F32), pltpu.VMEM((1, nl), F32)],
        compiler_params=_params(2),
    )(g_re, g_im, s_re, s_im, s_re, s_im, ar, ai)


def _s5_in_bwd(dx1, lam_re, lam_im, du_skip, x, pv, b_re, b_im, w_in, layer, tm):
    L = x.shape[0]

    def body(dx1_ref, lre_ref, lim_ref, dus_ref, x_ref, pv_ref, bre_ref, bim_ref, win_ref, dx_ref, du_ref, vs_ref):
        @pl.when(pl.program_id(0) == 0)
        def _():
            vs_ref[...] = jnp.zeros_like(vs_ref)

        p = pv_ref[...]
        lrb, lib = _bf(lre_ref[...]), _bf(lim_ref[...])
        parts = []
        for k in range(S5_NB):
            sl = slice(k * S5_BP, (k + 1) * S5_BP)
            parts.append(_dot_nt(lrb[:, sl], bre_ref[k]) + _dot_nt(lib[:, sl], bim_ref[k]))
        dub = _bf(jnp.concatenate(parts, axis=1) + dus_ref[...])
        du_ref[...] = dub
        dh = _dot_nt(dub, win_ref[...])
        _, xn, r = _norm_mod(x_ref[...], p[R_N1:R_N1 + 1], p[R_SC1:R_SC1 + 1], p[R_SH1:R_SH1 + 1])
        dx_ref[...] = dx1_ref[...] + _norm_mod_bwd(dh, xn, r, p[R_N1:R_N1 + 1], p[R_SC1:R_SC1 + 1])
        vs_ref[1:2, :] += _sum0(dh * xn)
        vs_ref[2:3, :] += _sum0(dh)

    return pl.pallas_call(
        body, name="s5_in_bwd", grid=(L // tm,),
        in_specs=[_rows(tm, D), _rows(tm, NSTATE), _rows(tm, NSTATE), _rows(tm, D), _rows(tm, D),
                  pl.BlockSpec((None, 8, D), lambda i: (layer, 0, 0)),
                  _const_w((S5_NB, S5_BH, S5_BP)), _const_w((S5_NB, S5_BH, S5_BP)), _layer_w(D, D, 0)],
        out_specs=[_rows(tm, D), _rows(tm, D), _whole((8, D))],
        out_shape=[jax.ShapeDtypeStruct((L, D), F32), jax.ShapeDtypeStruct((L, D), BF16), jax.ShapeDtypeStruct((8, D), F32)],
        compiler_params=_params(1, 56),
    )(dx1, lam_re, lam_im, du_skip, x, pv, b_re, b_im, w_in)


NSEG = 8
SCAN_LANES = 1024


def _to_segments(x):
    n, c = x.shape
    return x.reshape(NSEG, n // NSEG, c).transpose(1, 0, 2).reshape(n, c)


def _from_segments(x):
    n, c = x.shape
    return x.reshape(n // NSEG, NSEG, c).transpose(1, 0, 2).reshape(n, c)


def _segment_scan(re_ref, im_ref, st_re, st_im, a_re, a_im, n_slabs, adjoint, write):
    for q in range(NSTATE // SCAN_LANES):
        ls = slice(q * SCAN_LANES, (q + 1) * SCAN_LANES)
        ar = jnp.broadcast_to(a_re[:, ls], (8, SCAN_LANES))
        ai = jnp.broadcast_to(a_im[:, ls], (8, SCAN_LANES))

        def step(k, carry, ls=ls, ar=ar, ai=ai):
            s_r, s_i = carry
            slab = (n_slabs - 1 - k) if adjoint else k
            rows = pl.ds(pl.multiple_of(slab * 8, 8), 8)
            b_r, b_i = re_ref[rows, ls], im_ref[rows, ls]
            if adjoint:
                n_r = b_r + ar * s_r + ai * s_i
                n_i = b_i - ai * s_r + ar * s_i
            else:
                n_r = ar * s_r - ai * s_i + b_r
                n_i = ar * s_i + ai * s_r + b_i
            if write:
                re_ref[rows, ls] = n_r
                im_ref[rows, ls] = n_i
            return n_r, n_i

        s_r, s_i = lax.fori_loop(0, n_slabs, step, (st_re[:, ls], st_im[:, ls]), unroll=4)
        st_re[:, ls] = s_r
        st_im[:, ls] = s_i


def _s5_segment_states(e_re, e_im, ar, ai, seg_len, adjoint):
    def body(ere_ref, eim_ref, ar_ref, ai_ref, ore_ref, oim_ref):
        p_r, p_i = ar_ref[...], ai_ref[...]
        if adjoint:
            p_i = -p_i
        acc_r, acc_i = jnp.ones_like(p_r), jnp.zeros_like(p_r)
        n = seg_len
        while n:
            if n & 1:
                acc_r, acc_i = acc_r * p_r - acc_i * p_i, acc_r * p_i + acc_i * p_r
            n >>= 1
            if n:
                p_r, p_i = p_r * p_r - p_i * p_i, 2.0 * p_r * p_i
        e_r, e_i = ere_ref[...], eim_ref[...]
        s_r, s_i = jnp.zeros_like(acc_r), jnp.zeros_like(acc_r)
        rows_r, rows_i = [None] * NSEG, [None] * NSEG
        order = range(NSEG - 1, -1, -1) if adjoint else range(NSEG)
        for j in order:
            rows_r[j], rows_i[j] = s_r, s_i
            s_r, s_i = (acc_r * s_r - acc_i * s_i + e_r[j:j + 1], acc_r * s_i + acc_i * s_r + e_i[j:j + 1])
        ore_ref[...] = jnp.concatenate(rows_r, axis=0)
        oim_ref[...] = jnp.concatenate(rows_i, axis=0)

    st = jax.ShapeDtypeStruct((NSEG, NSTATE), F32)
    return pl.pallas_call(body, name="s5_segment_states_bwd" if adjoint else "s5_segment_states_fwd", out_shape=[st, st])(
        e_re, e_im, ar, ai)


def _s5_fwd_ends(x, pv, w_in, b_re, b_im, ar, ai, layer, tm, after=None):
    L = x.shape[0]

    def body(x_ref, pv_ref, win_ref, bre_ref, bim_ref, ar_ref, ai_ref, h_ref, u_ref, ere_ref, eim_ref, bu_re, bu_im):
        @pl.when(pl.program_id(0) == 0)
        def _():
            ere_ref[...] = jnp.zeros_like(ere_ref)
            eim_ref[...] = jnp.zeros_like(eim_ref)

        p = pv_ref[...]
        h, _, _ = _norm_mod(x_ref[...], p[R_N1:R_N1 + 1], p[R_SC1:R_SC1 + 1], p[R_SH1:R_SH1 + 1])
        hb = _bf(h)
        h_ref[...] = hb
        u = _dot(hb, win_ref[...])
        u_ref[...] = u
        ub = _bf(u)
        for k in range(S5_NB):
            uk = ub[:, k * S5_BH:(k + 1) * S5_BH]
            bu_re[:, k * S5_BP:(k + 1) * S5_BP] = _dot(uk, bre_ref[k])
            bu_im[:, k * S5_BP:(k + 1) * S5_BP] = _dot(uk, bim_ref[k])
        _segment_scan(bu_re, bu_im, ere_ref, eim_ref, ar_ref[...], ai_ref[...], tm // 8, adjoint=False, write=False)

    call, tail = _call_after(
        after, body, 7, name="s5_fwd_ends", grid=(L // tm,),
        in_specs=[_rows(tm, D), pl.BlockSpec((None, 8, D), lambda i: (layer, 0, 0)), _layer_w(D, D, 0),
                  _const_w((S5_NB, S5_BH, S5_BP)), _const_w((S5_NB, S5_BH, S5_BP)), _whole((1, NSTATE)), _whole((1, NSTATE))],
        out_specs=[_rows(tm, D), _rows(tm, D), _whole((NSEG, NSTATE)), _whole((NSEG, NSTATE))],
        out_shape=[jax.ShapeDtypeStruct((L, D), BF16), jax.ShapeDtypeStruct((L, D), F32),
                   jax.ShapeDtypeStruct((NSEG, NSTATE), F32), jax.ShapeDtypeStruct((NSEG, NSTATE), F32)],
        scratch_shapes=[pltpu.VMEM((tm, NSTATE), F32), pltpu.VMEM((tm, NSTATE), F32)],
        compiler_params=_params(1, 56),
    )
    return call(x, pv, w_in, b_re, b_im, ar, ai, *tail)


def _s5_fwd_out(x, u, pv, b_re, b_im, s0_re, s0_im, ar, ai, c_re, c_im, dvec, glu_w, glu_b, w_out, layer, tm):
    L = x.shape[0]

    def body(x_ref, u_ref, pv_ref, bre_ref, bim_ref, s0re_ref, s0im_ref, ar_ref, ai_ref, cre_ref, cim_ref, d_ref, gw_ref,
             gb_ref, wout_ref, x1_ref, sre_ref, sim_ref, y1_ref, zg_ref, y3_ref, y_ref, st_re, st_im):
        @pl.when(pl.program_id(0) == 0)
        def _():
            st_re[...] = s0re_ref[...]
            st_im[...] = s0im_ref[...]

        p = pv_ref[...]
        uv = u_ref[...]
        ub = _bf(uv)
        for k in range(S5_NB):
            uk = ub[:, k * S5_BH:(k + 1) * S5_BH]
            sre_ref[:, k * S5_BP:(k + 1) * S5_BP] = _dot(uk, bre_ref[k])
            sim_ref[:, k * S5_BP:(k + 1) * S5_BP] = _dot(uk, bim_ref[k])
        _segment_scan(sre_ref, sim_ref, st_re, st_im, ar_ref[...], ai_ref[...], tm // 8, adjoint=False, write=True)
        parts = []
        for k in range(S5_NB):
            sl = slice(k * S5_BP, (k + 1) * S5_BP)
            parts.append(_dot(_bf(sre_ref[:, sl]), cre_ref[k]) - _dot(_bf(sim_ref[:, sl]), cim_ref[k]))
        y1 = jnp.concatenate(parts, axis=1) + d_ref[...] * uv
        y1_ref[...] = y1
        y2 = jax.nn.gelu(y1)
        zg = _dot(_bf(y2), gw_ref[...]) + gb_ref[...]
        zg_ref[...] = zg
        y3b = _bf(y2 * jax.nn.sigmoid(zg))
        y3_ref[...] = y3b
        y = _dot(y3b, wout_ref[...])
        y_ref[...] = y
        x1_ref[...] = x_ref[...] + p[R_G1:R_G1 + 1] * y

    return pl.pallas_call(
        body, name="s5_fwd_out", grid=(L // tm,),
        in_specs=[_rows(tm, D), _rows(tm, D), pl.BlockSpec((None, 8, D), lambda i: (layer, 0, 0)),
                  _const_w((S5_NB, S5_BH, S5_BP)), _const_w((S5_NB, S5_BH, S5_BP)),
                  _whole((NSEG, NSTATE)), _whole((NSEG, NSTATE)), _whole((1, NSTATE)), _whole((1, NSTATE)),
                  _const_w((S5_NB, S5_BP, S5_BH)), _const_w((S5_NB, S5_BP, S5_BH)), _whole((1, D)),
                  _layer_w(D, D, 0), _whole((1, D)), _layer_w(D, D, 0)],
        out_specs=[_rows(tm, D), _rows(tm, NSTATE), _rows(tm, NSTATE), _rows(tm, D), _rows(tm, D), _rows(tm, D), _rows(tm, D)],
        out_shape=[jax.ShapeDtypeStruct((L, D), F32), jax.ShapeDtypeStruct((L, NSTATE), F32), jax.ShapeDtypeStruct((L, NSTATE), F32),
                   jax.ShapeDtypeStruct((L, D), F32), jax.ShapeDtypeStruct((L, D), F32),
                   jax.ShapeDtypeStruct((L, D), BF16), jax.ShapeDtypeStruct((L, D), F32)],
        scratch_shapes=[pltpu.VMEM((NSEG, NSTATE), F32), pltpu.VMEM((NSEG, NSTATE), F32)],
        compiler_params=_params(1, 56),
    )(x, u, pv, b_re, b_im, s0_re, s0_im, ar, ai, c_re, c_im, dvec, glu_w, glu_b, w_out)


def _s5_bwd_ends(dx1, y, y1, zg, u, pv, c_re, c_im, ar, ai, dvec, glu_w, w_out, layer, tm, after=None):
    L = dx1.shape[0]
    nt = L // tm

    def body(dx1_ref, y_ref, y1_ref, zg_ref, u_ref, pv_ref, cre_ref, cim_ref, ar_ref, ai_ref, d_ref, gw_ref, wout_ref,
             dy_ref, y2_ref, dzg_ref, dy1_ref, dus_ref, ere_ref, eim_ref, vs_ref, g_re, g_im):
        @pl.when(pl.program_id(0) == 0)
        def _():
            vs_ref[...] = jnp.zeros_like(vs_ref)
            ere_ref[...] = jnp.zeros_like(ere_ref)
            eim_ref[...] = jnp.zeros_like(eim_ref)

        dx1v, p = dx1_ref[...], pv_ref[...]
        dyb = _bf(dx1v * p[R_G1:R_G1 + 1])
        dy_ref[...] = dyb
        vs_ref[0:1, :] += _sum0(dx1v * y_ref[...])
        dy3 = _dot_nt(dyb, wout_ref[...])
        y2, gelu_vjp = jax.vjp(jax.nn.gelu, y1_ref[...])
        y2_ref[...] = _bf(y2)
        gate = jax.nn.sigmoid(zg_ref[...])
        dzg = dy3 * y2 * gate * (1.0 - gate)
        dzgb = _bf(dzg)
        dzg_ref[...] = dzgb
        vs_ref[1:2, :] += _sum0(dzg)
        dy2 = dy3 * gate + _dot_nt(dzgb, gw_ref[...])
        dy1 = gelu_vjp(dy2)[0]
        vs_ref[2:3, :] += _sum0(dy1 * u_ref[...])
        dus_ref[...] = dy1 * d_ref[...]
        dy1b = _bf(dy1)
        dy1_ref[...] = dy1b
        for k in range(S5_NB):
            dk = dy1b[:, k * S5_BH:(k + 1) * S5_BH]
            g_re[:, k * S5_BP:(k + 1) * S5_BP] = _dot_nt(dk, cre_ref[k])
            g_im[:, k * S5_BP:(k + 1) * S5_BP] = -_dot_nt(dk, cim_ref[k])
        _segment_scan(g_re, g_im, ere_ref, eim_ref, ar_ref[...], ai_ref[...], tm // 8, adjoint=True, write=False)

    call, tail = _call_after(
        after, body, 13, name="s5_bwd_ends", grid=(nt,),
        in_specs=[_rows(tm, D, nt)] * 5 + [pl.BlockSpec((None, 8, D), lambda i: (layer, 0, 0)),
                  _const_w((S5_NB, S5_BP, S5_BH)), _const_w((S5_NB, S5_BP, S5_BH)), _whole((1, NSTATE)), _whole((1, NSTATE)),
                  _whole((1, D)), _layer_w(D, D, 0), _layer_w(D, D, 0)],
        out_specs=[_rows(tm, D, nt)] * 5 + [_whole((NSEG, NSTATE)), _whole((NSEG, NSTATE)), _whole((8, D))],
        out_shape=[jax.ShapeDtypeStruct((L, D), BF16)] * 4 + [jax.ShapeDtypeStruct((L, D), F32),
                   jax.ShapeDtypeStruct((NSEG, NSTATE), F32), jax.ShapeDtypeStruct((NSEG, NSTATE), F32),
                   jax.ShapeDtypeStruct((8, D), F32)],
        scratch_shapes=[pltpu.VMEM((tm, NSTATE), F32), pltpu.VMEM((tm, NSTATE), F32)],
        compiler_params=_params(1, 56),
    )
    return call(dx1, y, y1, zg, u, pv, c_re, c_im, ar, ai, dvec, glu_w, w_out, *tail)


def _s5_bwd_in(dx1, dy1_b, du_skip, x, s_re, s_im, pv, b_re, b_im, c_re, c_im, l0_re, l0_im, ar, ai, w_in, layer, tm):
    L = x.shape[0]
    nt = L // tm

    def body(dx1_ref, dy1_ref, dus_ref, x_ref, sre_ref, sim_ref, hre_ref, him_ref, lre_ref, lim_ref, pv_ref, bre_ref, bim_ref,
             cre_ref, cim_ref, l0re_ref, l0im_ref, ar_ref, ai_ref, win_ref,
             dx_ref, du_ref, lamre_ref, lamim_ref, da_ref, vs_ref, g_re, g_im, st_re, st_im):
        gi = pl.program_id(0)
        tile = nt - 1 - gi

        @pl.when(gi == 0)
        def _():
            vs_ref[...] = jnp.zeros_like(vs_ref)
            da_ref[...] = jnp.zeros_like(da_ref)
            st_re[...] = l0re_ref[...]
            st_im[...] = l0im_ref[...]

        p = pv_ref[...]
        dy1b = dy1_ref[...]
        for k in range(S5_NB):
            dk = dy1b[:, k * S5_BH:(k + 1) * S5_BH]
            g_re[:, k * S5_BP:(k + 1) * S5_BP] = _dot_nt(dk, cre_ref[k])
            g_im[:, k * S5_BP:(k + 1) * S5_BP] = -_dot_nt(dk, cim_ref[k])
        _segment_scan(g_re, g_im, st_re, st_im, ar_ref[...], ai_ref[...], tm // 8, adjoint=True, write=True)
        lam_r, lam_i = g_re[...], g_im[...]
        lrb, lib = _bf(lam_r), _bf(lam_i)
        lamre_ref[...] = lrb
        lamim_ref[...] = lib

        def wrapped(last_ref):
            z = last_ref[...]
            row = lax.broadcasted_iota(jnp.int32, z.shape, 0)
            return jnp.where(row >= 1, pltpu.roll(z, 1, 0), 0.0)

        first_r = jnp.where(tile > 0, hre_ref[...], wrapped(lre_ref))
        first_i = jnp.where(tile > 0, him_ref[...], wrapped(lim_ref))
        sp_r = jnp.concatenate([first_r, sre_ref[0:tm - 8, :]], axis=0)
        sp_i = jnp.concatenate([first_i, sim_ref[0:tm - 8, :]], axis=0)
        da_ref[0:1, :] += _sum0(lam_r * sp_r + lam_i * sp_i)
        da_ref[1:2, :] += _sum0(lam_i * sp_r - lam_r * sp_i)

        parts = []
        for k in range(S5_NB):
            sl = slice(k * S5_BP, (k + 1) * S5_BP)
            parts.append(_dot_nt(lrb[:, sl], bre_ref[k]) + _dot_nt(lib[:, sl], bim_ref[k]))
        dub = _bf(jnp.concatenate(parts, axis=1) + dus_ref[...])
        du_ref[...] = dub
        dh = _dot_nt(dub, win_ref[...])
        _, xn, r = _norm_mod(x_ref[...], p[R_N1:R_N1 + 1], p[R_SC1:R_SC1 + 1], p[R_SH1:R_SH1 + 1])
        dx_ref[...] = dx1_ref[...] + _norm_mod_bwd(dh, xn, r, p[R_N1:R_N1 + 1], p[R_SC1:R_SC1 + 1])
        vs_ref[1:2, :] += _sum0(dh * xn)
        vs_ref[2:3, :] += _sum0(dh)

    halo = pl.BlockSpec((8, NSTATE), lambda i: (jnp.maximum((nt - 1 - i) * (tm // 8) - 1, 0), 0))
    last = pl.BlockSpec((8, NSTATE), lambda i: (L // 8 - 1, 0))
    return pl.pallas_call(
        body, name="s5_bwd_in", grid=(nt,),
        in_specs=[_rows(tm, D, nt), _rows(tm, D, nt), _rows(tm, D, nt), _rows(tm, D, nt), _rows(tm, NSTATE, nt), _rows(tm, NSTATE, nt),
                  halo, halo, last, last, pl.BlockSpec((None, 8, D), lambda i: (layer, 0, 0)),
                  _const_w((S5_NB, S5_BH, S5_BP)), _const_w((S5_NB, S5_BH, S5_BP)),
                  _const_w((S5_NB, S5_BP, S5_BH)), _const_w((S5_NB, S5_BP, S5_BH)),
                  _whole((NSEG, NSTATE)), _whole((NSEG, NSTATE)), _whole((1, NSTATE)), _whole((1, NSTATE)), _layer_w(D, D, 0)],
        out_specs=[_rows(tm, D, nt), _rows(tm, D, nt), _rows(tm, NSTATE, nt), _rows(tm, NSTATE, nt), _whole((8, NSTATE)), _whole((8, D))],
        out_shape=[jax.ShapeDtypeStruct((L, D), F32), jax.ShapeDtypeStruct((L, D), BF16),
                   jax.ShapeDtypeStruct((L, NSTATE), BF16), jax.ShapeDtypeStruct((L, NSTATE), BF16),
                   jax.ShapeDtypeStruct((8, NSTATE), F32), jax.ShapeDtypeStruct((8, D), F32)],
        scratch_shapes=[pltpu.VMEM((tm, NSTATE), F32), pltpu.VMEM((tm, NSTATE), F32),
                        pltpu.VMEM((NSEG, NSTATE), F32), pltpu.VMEM((NSEG, NSTATE), F32)],
        compiler_params=_params(1, 60),
    )(dx1, dy1_b, du_skip, x, s_re, s_im, s_re, s_im, s_re, s_im, pv, b_re, b_im, c_re, c_im, l0_re, l0_im, ar, ai, w_in)


def _blockdiag_b(bt):
    b = bt.reshape(S5_H, S5_NB, 16, S5_P).transpose(1, 2, 0, 3)
    eye = jnp.eye(16, dtype=bt.dtype)
    return (b[:, :, :, None, :] * eye[None, :, None, :, None]).reshape(S5_NB, S5_BH, S5_BP)


def _unblock_b(d):
    d = jnp.einsum("bghgp->bghp", d.reshape(S5_NB, 16, S5_H, 16, S5_P))
    return d.transpose(2, 0, 1, 3).reshape(S5_H, S5_G, S5_P)


def _blockdiag_c(cm):
    c4 = cm.reshape(S5_NB, 16, S5_H, S5_P)
    eye = jnp.eye(16, dtype=cm.dtype)
    out = c4.transpose(0, 1, 3, 2)[:, :, :, None, :] * eye[None, :, None, :, None]
    return out.reshape(S5_NB, S5_BP, S5_BH)


def _unblock_c(d):
    d = jnp.einsum("bgpgh->bghp", d.reshape(S5_NB, 16, S5_P, 16, S5_H))
    return d.reshape(S5_G, S5_H, S5_P)


def _tril_mask():
    return lax.broadcasted_iota(jnp.int32, (SG_CHUNK, SG_CHUNK), 0) >= lax.broadcasted_iota(jnp.int32, (SG_CHUNK, SG_CHUNK), 1)


def _sg_fwd(x, pv, w_in, w_s, b_t, vg, w_out, layer, tm, after=None):
    L = x.shape[0]
    nc = tm // SG_CHUNK

    def body(x_ref, pv_ref, win_ref, ws_ref, bt_ref, vg_ref, wout_ref, x1_ref, h_ref, uv_ref, vm_ref, q_ref, y_ref):
        xv, p = x_ref[...], pv_ref[...]
        h, _, _ = _norm_mod(xv, p[R_N1:R_N1 + 1], p[R_SC1:R_SC1 + 1], p[R_SH1:R_SH1 + 1])
        hb = _bf(h)
        h_ref[...] = hb
        uv = _dot(hb, win_ref[...])
        uv_ref[...] = uv
        v = uv[:, D:]
        rv = lax.rsqrt(jnp.mean(v * v, axis=-1, keepdims=True) + EPS)
        vnb = _bf((v * rv) * vg_ref[...])
        mask = _tril_mask()
        bt = bt_ref[...]
        for hd in range(SG_HEADS):
            wm = _bf(jnp.where(mask, ws_ref[hd], 0.0))
            cs = slice(hd * SG_CHUNK, (hd + 1) * SG_CHUNK)
            for ck in range(nc):
                rs = slice(ck * SG_CHUNK, (ck + 1) * SG_CHUNK)
                vm_ref[rs, cs] = _dot(wm, vnb[rs, cs]) + bt[:, hd:hd + 1]
        qb = _bf(uv[:, :D] * vm_ref[...])
        q_ref[...] = qb
        y = _dot(qb, wout_ref[...])
        y_ref[...] = y
        x1_ref[...] = xv + p[R_G1:R_G1 + 1] * y

    call, tail = _call_after(
        after, body, 7, name="sg_fwd", grid=(L // tm,),
        in_specs=[_rows(tm, D), pl.BlockSpec((None, 8, D), lambda i: (layer, 0, 0)), _layer_w(D, 2 * D, 0),
                  _whole((SG_HEADS, SG_CHUNK, SG_CHUNK)), _whole((SG_CHUNK, SG_HEADS)), _whole((1, D)), _layer_w(D, D, 0)],
        out_specs=[_rows(tm, D), _rows(tm, D), _rows(tm, 2 * D), _rows(tm, D), _rows(tm, D), _rows(tm, D)],
        out_shape=[jax.ShapeDtypeStruct((L, D), F32), jax.ShapeDtypeStruct((L, D), BF16), jax.ShapeDtypeStruct((L, 2 * D), F32),
                   jax.ShapeDtypeStruct((L, D), F32), jax.ShapeDtypeStruct((L, D), BF16), jax.ShapeDtypeStruct((L, D), F32)],
        compiler_params=_params(1, 56),
    )
    return call(x, pv, w_in, w_s, b_t, vg, w_out, *tail)


def _sg_bwd(dx1, x, y, uv, vm, pv, w_in, w_s, vg, w_out, layer, tm, after=None):
    L = x.shape[0]
    nc = tm // SG_CHUNK

    def body(dx1_ref, x_ref, y_ref, uv_ref, vm_ref, pv_ref, win_ref, ws_ref, vg_ref, wout_ref,
             dx_ref, duv_ref, dy_ref, vs_ref, dws_ref, dbt_ref, dvn_scr):
        @pl.when(pl.program_id(0) == 0)
        def _():
            vs_ref[...] = jnp.zeros_like(vs_ref)
            dws_ref[...] = jnp.zeros_like(dws_ref)
            dbt_ref[...] = jnp.zeros_like(dbt_ref)

        dx1v, p = dx1_ref[...], pv_ref[...]
        dyb = _bf(dx1v * p[R_G1:R_G1 + 1])
        dy_ref[...] = dyb
        vs_ref[0:1, :] += _sum0(dx1v * y_ref[...])
        dq = _dot_nt(dyb, wout_ref[...])
        uv = uv_ref[...]
        u, v = uv[:, :D], uv[:, D:]
        dub = _bf(dq * vm_ref[...])
        dvm = dq * u
        dvmb = _bf(dvm)
        rv = lax.rsqrt(jnp.mean(v * v, axis=-1, keepdims=True) + EPS)
        vh = v * rv
        vgv = vg_ref[...]
        vnb = _bf(vh * vgv)
        mask = _tril_mask()
        for hd in range(SG_HEADS):
            wm = _bf(jnp.where(mask, ws_ref[hd], 0.0))
            cs = slice(hd * SG_CHUNK, (hd + 1) * SG_CHUNK)
            dws = jnp.zeros((SG_CHUNK, SG_CHUNK), F32)
            dbs = jnp.zeros((SG_CHUNK, 1), F32)
            for ck in range(nc):
                rs = slice(ck * SG_CHUNK, (ck + 1) * SG_CHUNK)
                dvn_scr[rs, cs] = _dot_tn(wm, dvmb[rs, cs])
                dws = dws + _dot_nt(dvmb[rs, cs], vnb[rs, cs])
                dbs = dbs + jnp.sum(dvm[rs, cs], axis=1, keepdims=True)
            dws_ref[hd] += jnp.where(mask, dws, 0.0)
            dbt_ref[:, hd:hd + 1] += dbs
        dvn = dvn_scr[...]
        vs_ref[3:4, :] += _sum0(dvn * vh)
        dvnn = dvn * vgv
        dvb = _bf(rv * (dvnn - vh * jnp.mean(dvnn * vh, axis=-1, keepdims=True)))
        duv_ref[:, 0:D] = dub
        duv_ref[:, D:2 * D] = dvb
        dh = _dot_nt(dub, win_ref[:, 0:D]) + _dot_nt(dvb, win_ref[:, D:2 * D])
        _, xn, r = _norm_mod(x_ref[...], p[R_N1:R_N1 + 1], p[R_SC1:R_SC1 + 1], p[R_SH1:R_SH1 + 1])
        dx_ref[...] = dx1v + _norm_mod_bwd(dh, xn, r, p[R_N1:R_N1 + 1], p[R_SC1:R_SC1 + 1])
        vs_ref[1:2, :] += _sum0(dh * xn)
        vs_ref[2:3, :] += _sum0(dh)

    call, tail = _call_after(
        after, body, 10, name="sg_bwd", grid=(L // tm,),
        in_specs=[_rows(tm, D), _rows(tm, D), _rows(tm, D), _rows(tm, 2 * D), _rows(tm, D),
                  pl.BlockSpec((None, 8, D), lambda i: (layer, 0, 0)), _layer_w(D, 2 * D, 0),
                  _whole((SG_HEADS, SG_CHUNK, SG_CHUNK)), _whole((1, D)), _layer_w(D, D, 0)],
        out_specs=[_rows(tm, D), _rows(tm, 2 * D), _rows(tm, D), _whole((8, D)),
                   _whole((SG_HEADS, SG_CHUNK, SG_CHUNK)), _whole((SG_CHUNK, SG_HEADS))],
        out_shape=[jax.ShapeDtypeStruct((L, D), F32), jax.ShapeDtypeStruct((L, 2 * D), BF16), jax.ShapeDtypeStruct((L, D), BF16),
                   jax.ShapeDtypeStruct((8, D), F32), jax.ShapeDtypeStruct((SG_HEADS, SG_CHUNK, SG_CHUNK), F32),
                   jax.ShapeDtypeStruct((SG_CHUNK, SG_HEADS), F32)],
        scratch_shapes=[pltpu.VMEM((tm, D), F32)],
        compiler_params=_params(1, 56),
    )
    return call(dx1, x, y, uv, vm, pv, w_in, w_s, vg, w_out, *tail)


def _final(x, target, fg, tm):
    L = x.shape[0]

    def body(x_ref, t_ref, g_ref, dx_ref, vs_ref):
        @pl.when(pl.program_id(0) == 0)
        def _():
            vs_ref[...] = jnp.zeros_like(vs_ref)

        xv, g = x_ref[...], g_ref[...]
        r = lax.rsqrt(jnp.mean(xv * xv, axis=-1, keepdims=True) + EPS)
        xn = xv * r
        e = xn * g - t_ref[...]
        vs_ref[0:1, :] += jnp.sum(e * e)
        dout = e * (1.0 / D)
        vs_ref[1:2, :] += _sum0(dout * xn)
        dxn = dout * g
        dx_ref[...] = r * (dxn - xn * jnp.mean(dxn * xn, axis=-1, keepdims=True))

    return pl.pallas_call(
        body, name="final_loss", grid=(L // tm,),
        in_specs=[_rows(tm, D), _rows(tm, D), _whole((1, D))],
        out_specs=[_rows(tm, D), _whole((8, D))],
        out_shape=[jax.ShapeDtypeStruct((L, D), F32), jax.ShapeDtypeStruct((8, D), F32)],
        compiler_params=_params(1),
    )(x, target, fg)


def _pack_flat(arrs, multiple=LANES):
    flat = jnp.concatenate([a.reshape(-1).astype(F32) for a in arrs])
    return jnp.pad(flat, (0, -flat.shape[0] % multiple))


def _pack(arrs, row_multiple=8):
    return _pack_flat(arrs, row_multiple * LANES).reshape(-1, LANES)


def _unpack(buf, shapes, lead=()):
    flat = buf.reshape(lead + (-1,))
    out, off = [], 0
    for s in shapes:
        n = 1
        for d in s:
            n *= d
        out.append(flat[..., off:off + n].reshape(lead + tuple(s)))
        off += n
    return out


BIG = ("ff_w1", "ff_w2", "conv_w_in", "conv_w_out", "ssm_w_in", "ssm_glu_w", "ssm_w_out", "sg_w_in", "sg_w_out")
BIG_AXIS = {"ff_w1": 2, "ff_w2": 1, "conv_w_in": 2, "conv_w_out": 1, "ssm_w_in": 1, "ssm_glu_w": 1, "ssm_w_out": 1,
            "sg_w_in": 2, "sg_w_out": 1}
LAYER_WEIGHTS = (
    (("conv_w_in", 0), ("conv_w_out", 0), ("ff_w1", 0), ("ff_w2", 0)),
    (("ssm_w_in", 0), ("ssm_glu_w", 0), ("ssm_w_out", 0), ("ff_w1", 1), ("ff_w2", 1)),
    (("sg_w_in", 0), ("sg_w_out", 0), ("ff_w1", 2), ("ff_w2", 2)),
    (("conv_w_in", 1), ("conv_w_out", 1), ("ff_w1", 3), ("ff_w2", 3)),
)
GATHER_GROUPS = tuple(grp for lw in LAYER_WEIGHTS for grp in (lw[:-2], lw[-2:]))
SMALL_SHARDED = ("conv_w", "conv_b", "sg_v_g")
SMALL = ("ada_b", "norm1_g", "norm2_g", "final_g", "ssm_a_re", "ssm_a_im", "ssm_log_dt", "ssm_b_re", "ssm_b_im", "ssm_c_re",
         "ssm_c_im", "ssm_d", "ssm_glu_b", "sg_w_s", "sg_b_s") + SMALL_SHARDED
WEIGHTS = ("ada_w", "ada_b", "norm1_g", "norm2_g", "ff_w1", "ff_w2", "final_g", "conv_w_in", "conv_w", "conv_b", "conv_w_out",
           "ssm_w_in", "ssm_a_re", "ssm_a_im", "ssm_log_dt", "ssm_b_re", "ssm_b_im", "ssm_c_re", "ssm_c_im", "ssm_d",
           "ssm_glu_w", "ssm_glu_b", "ssm_w_out", "sg_w_in", "sg_v_g", "sg_w_s", "sg_b_s", "sg_w_out")


def kernel(x, c, ada_w, ada_b, norm1_g, norm2_g, ff_w1, ff_w2, final_g, conv_w_in, conv_w, conv_b, conv_w_out, ssm_w_in, ssm_a_re, ssm_a_im, ssm_log_dt, ssm_b_re, ssm_b_im, ssm_c_re, ssm_c_im, ssm_d, ssm_glu_w, ssm_glu_b, ssm_w_out, sg_w_in, sg_v_g, sg_w_s, sg_b_s, sg_w_out, loss_target, m_ada_w, m_ada_b, m_norm1_g, m_norm2_g, m_ff_w1, m_ff_w2, m_final_g, m_conv_w_in, m_conv_w, m_conv_b, m_conv_w_out, m_ssm_w_in, m_ssm_a_re, m_ssm_a_im, m_ssm_log_dt, m_ssm_b_re, m_ssm_b_im, m_ssm_c_re, m_ssm_c_im, m_ssm_d, m_ssm_glu_w, m_ssm_glu_b, m_ssm_w_out, m_sg_w_in, m_sg_v_g, m_sg_w_s, m_sg_b_s, m_sg_w_out, v_ada_w, v_ada_b, v_norm1_g, v_norm2_g, v_ff_w1, v_ff_w2, v_final_g, v_conv_w_in, v_conv_w, v_conv_b, v_conv_w_out, v_ssm_w_in, v_ssm_a_re, v_ssm_a_im, v_ssm_log_dt, v_ssm_b_re, v_ssm_b_im, v_ssm_c_re, v_ssm_c_im, v_ssm_d, v_ssm_glu_w, v_ssm_glu_b, v_ssm_w_out, v_sg_w_in, v_sg_v_g, v_sg_w_s, v_sg_b_s, v_sg_w_out):
    args = dict(locals())
    w = {n: args[n] for n in WEIGHTS}
    m = {n: args["m_" + n] for n in WEIGHTS}
    v = {n: args["v_" + n] for n in WEIGHTS}
    L = x.shape[1]
    tm = min(L, 256)
    tr = min(L, 512)
    chip = 2 * lax.axis_index("x") + lax.axis_index("y")
    me = 2 * chip + lax.axis_index("c")
    xin = x[0]
    target = loss_target[0]
    chip1 = chip.reshape(1).astype(jnp.int32)

    gathers = []

    def start_gather(g, after):
        entries = GATHER_GROUPS[g]
        axes = [BIG_AXIS[n] for n, _ in entries]
        lands = [_cast_place(w[n], li, BIG_AXIS[n], chip1, f"cast_{n}_{li}") for n, li in entries]
        s_sems, r_sems, lands, token = _gather_start(lands, axes, f"gather_start{g}", after)
        gathers.append((s_sems, r_sems, lands, axes))
        return token

    def weights_of(g, after):
        s_sems, r_sems, lands, axes = gathers[g]
        lands = _gather_wait(s_sems, r_sems, lands, axes, f"gather_wait{g}", after)
        return dict(zip([n for n, _ in GATHER_GROUPS[g]], _gather_share(lands, axes, f"gather_share{g}")))

    small_in = _pack([c, conv_w, conv_b, sg_v_g])
    got = _allgather_small(small_in, "gather_small_inputs").reshape(N_DEV, -1)
    c_all, cw_sh, cb_sh, vg_sh = _unpack(got, [(D,), conv_w.shape, conv_b.shape, sg_v_g.shape], lead=(N_DEV,))
    conv_w_full = jnp.concatenate([cw_sh[2 * k] for k in range(4)], axis=-1)
    conv_b_full = jnp.concatenate([cb_sh[2 * k] for k in range(4)], axis=-1)
    vg_full = jnp.concatenate([vg_sh[2 * k] for k in range(4)], axis=-1)
    c16 = jnp.pad(c_all, ((0, 16 - N_DEV), (0, 0)))

    cols = ada_w.shape[2]
    ada_b_cols = lax.dynamic_slice_in_dim(ada_b, chip * cols, cols, axis=1)[:, None, :]
    mod_sh = _ada_fwd(c16, ada_w, ada_b_cols)[:, :N_DEV, :]
    mod_all = _allgather_small(_pack([mod_sh]), "gather_mod").reshape(N_DEV, -1)
    mod_all = _unpack(mod_all, [mod_sh.shape], lead=(N_DEV,))[0]
    mod_mine = lax.dynamic_index_in_dim(mod_all[0::2], me, axis=2, keepdims=False)
    mod_mine = mod_mine.transpose(1, 0, 2).reshape(DEPTH, 6, D)
    pv = jnp.concatenate([mod_mine, norm1_g[:, None, :], norm2_g[:, None, :]], axis=1)

    start_gather(1, start_gather(0, pv))

    cw_rows = jnp.concatenate([conv_w_full, conv_b_full[:, None, :], jnp.zeros((conv_w_full.shape[0], 4, D), F32)], axis=1)

    a_re, a_im = ssm_a_re[0], ssm_a_im[0]
    log_dt = ssm_log_dt[0][:, None]
    bt_re, bt_im = ssm_b_re[0].transpose(2, 0, 1), ssm_b_im[0].transpose(2, 0, 1)
    abar_re, abar_im, bbar_re, bbar_im = _s5_params_fwd(a_re, a_im, log_dt, bt_re, bt_im)
    ar_vec, ai_vec = abar_re.reshape(1, NSTATE), abar_im.reshape(1, NSTATE)
    bd_re, bd_im = _bf(_blockdiag_b(bbar_re)), _bf(_blockdiag_b(bbar_im))
    cd_re, cd_im = _bf(_blockdiag_c(ssm_c_re[0])), _bf(_blockdiag_c(ssm_c_im[0]))

    saved = []
    fulls = []
    xl = xin
    for i in range(DEPTH):
        kind = MIXER_OF_LAYER[i]
        j = i // 3
        full = weights_of(2 * i, cd_im if i == 0 else xl)
        fulls.append(full)
        tok = start_gather(2 * i + 2, next(iter(full.values()))) if i + 1 < DEPTH else None
        if kind == 0:
            x1, h, bcx, conv, q, y = _conv_fwd(xl, pv, full["conv_w_in"], full["conv_w_out"], cw_rows, i, j, tm, after=tok)
            mix = dict(h=h, bcx=bcx, conv=conv, q=q, y=y)
        elif kind == 1:
            xp = _to_segments(xl)
            h, u, e_re, e_im = _s5_fwd_ends(xp, pv, full["ssm_w_in"], bd_re, bd_im, ar_vec, ai_vec, i, tm, after=tok)
            s0_re, s0_im = _s5_segment_states(e_re, e_im, ar_vec, ai_vec, L // NSEG, adjoint=False)
            x1p, s_re, s_im, y1, zg, y3, y = _s5_fwd_out(xp, u, pv, bd_re, bd_im, s0_re, s0_im, ar_vec, ai_vec, cd_re, cd_im,
                                                         ssm_d, full["ssm_glu_w"], ssm_glu_b, full["ssm_w_out"], i, tm)
            x1 = _from_segments(x1p)
            mix = dict(xp=xp, h=h, u=u, s_re=s_re, s_im=s_im, y1=y1, zg=zg, y3=y3, y=y)
        else:
            x1, h, uv, vm, q, y = _sg_fwd(xl, pv, full["sg_w_in"], sg_w_s[0], sg_b_s[0].T, vg_full, full["sg_w_out"], i, tm,
                                          after=tok)
            mix = dict(h=h, uv=uv, vm=vm, q=q, y=y)
        full.update(weights_of(2 * i + 1, x1))
        tok = start_gather(2 * i + 3, full["ff_w2"]) if i + 1 < DEPTH else None
        x2, h2, a, f = _ffn_fwd(x1, pv, full["ff_w1"], full["ff_w2"], i, tm, after=tok)
        saved.append(dict(x=xl, x1=x1, h2=h2, a=a, f=f, **mix))
        xl = x2

    dxl, vs_fin = _final(xl, target, final_g[None, :], tm)

    gfull = {n: [None] * w[n].shape[0] for n in BIG}
    vs_mix, vs_ffn = [None] * DEPTH, [None] * DEPTH
    small_g = {}
    scatters = {}
    token = None

    def start_scatter(key, entries, after):
        garrs = [gfull[n][li][None] for n, li in entries]
        gaxes = [BIG_AXIS[n] for n, _ in entries]
        s_sems, r_sems, garrs, lands, tok = _scatter_start(garrs, gaxes, f"scatter_start{key}", after)
        scatters[key] = (s_sems, r_sems, garrs, lands, gaxes, entries)
        return tok

    for i in reversed(range(DEPTH)):
        kind = MIXER_OF_LAYER[i]
        j = i // 3
        sv = saved[i]
        full = fulls[i]
        dx1, p_b, da_b, df_b, vs_ffn[i] = _ffn_bwd(dxl, sv["x1"], sv["a"], sv["f"], pv, full["ff_w1"], full["ff_w2"], i, tm,
                                                   after=token)
        gfull["ff_w1"][i] = _mm_tn(sv["h2"], da_b, f"wgrad_ff_w1_{i}")
        gfull["ff_w2"][i] = _mm_tn(p_b, df_b, f"wgrad_ff_w2_{i}")
        if i == 0:
            token = start_scatter("0f", LAYER_WEIGHTS[0][2:], dx1)
        if kind == 0:
            dxl, dbcx_b, dy_b, vsm = _conv_bwd(dx1, sv["x"], sv["y"], sv["bcx"], sv["conv"], pv, full["conv_w_in"],
                                               full["conv_w_out"], cw_rows, i, j, tm, after=token if i == 0 else None)
            gfull["conv_w_in"][j] = _mm_tn(sv["h"], dbcx_b, f"wgrad_conv_w_in_{j}")
            gfull["conv_w_out"][j] = _mm_tn(sv["q"], dy_b, f"wgrad_conv_w_out_{j}")
            small_g.setdefault("conv_w", [None, None])[j] = vsm[3:6]
            small_g.setdefault("conv_b", [None, None])[j] = vsm[6]
        elif kind == 1:
            dx1p = _to_segments(dx1)
            dy_b, y2_b, dzg_b, dy1_b, du_skip, eb_re, eb_im, vsm = _s5_bwd_ends(
                dx1p, sv["y"], sv["y1"], sv["zg"], sv["u"], pv, cd_re, cd_im, ar_vec, ai_vec, ssm_d, full["ssm_glu_w"],
                full["ssm_w_out"], i, tm)
            l0_re, l0_im = _s5_segment_states(eb_re, eb_im, ar_vec, ai_vec, L // NSEG, adjoint=True)
            dxp, du_b, lam_re, lam_im, dabar, vs_in = _s5_bwd_in(
                dx1p, dy1_b, du_skip, sv["xp"], sv["s_re"], sv["s_im"], pv, bd_re, bd_im, cd_re, cd_im, l0_re, l0_im,
                ar_vec, ai_vec, full["ssm_w_in"], i, tm)
            dxl = _from_segments(dxp)
            gfull["ssm_w_out"][0] = _mm_tn(sv["y3"], dy_b, "wgrad_ssm_w_out")
            gfull["ssm_glu_w"][0] = _mm_tn(y2_b, dzg_b, "wgrad_ssm_glu_w")
            gfull["ssm_w_in"][0] = _mm_tn(sv["h"], du_b, "wgrad_ssm_w_in")
            d_cre = _unblock_c(_mm_tn_blocks(sv["s_re"], dy1_b, S5_BP, S5_BH, "wgrad_s5_c_re"))
            d_cim = -_unblock_c(_mm_tn_blocks(sv["s_im"], dy1_b, S5_BP, S5_BH, "wgrad_s5_c_im"))
            d_bbre = _unblock_b(_mm_tn_blocks(sv["u"], lam_re, S5_BH, S5_BP, "wgrad_s5_b_re"))
            d_bbim = _unblock_b(_mm_tn_blocks(sv["u"], lam_im, S5_BH, S5_BP, "wgrad_s5_b_im"))
            d_are, d_aim, d_ldt, d_btre, d_btim = _s5_params_bwd(
                a_re, a_im, log_dt, bt_re, bt_im, dabar[0].reshape(S5_G, S5_P), dabar[1].reshape(S5_G, S5_P), d_bbre, d_bbim)
            small_g.update(ssm_a_re=d_are, ssm_a_im=d_aim, ssm_log_dt=d_ldt, ssm_b_re=d_btre.transpose(1, 2, 0),
                           ssm_b_im=d_btim.transpose(1, 2, 0), ssm_c_re=d_cre, ssm_c_im=d_cim, ssm_d=vsm[2], ssm_glu_b=vsm[1])
            vsm = jnp.concatenate([vsm[0:1], vs_in[1:3], jnp.zeros((5, D), F32)], axis=0)
        else:
            dxl, duv_b, dy_b, vsm, d_ws, d_bt = _sg_bwd(dx1, sv["x"], sv["y"], sv["uv"], sv["vm"], pv, full["sg_w_in"],
                                                        sg_w_s[0], vg_full, full["sg_w_out"], i, tm)
            gfull["sg_w_in"][0] = _mm_tn(sv["h"], duv_b, "wgrad_sg_w_in")
            gfull["sg_w_out"][0] = _mm_tn(sv["q"], dy_b, "wgrad_sg_w_out")
            small_g.update(sg_v_g=vsm[3], sg_w_s=d_ws, sg_b_s=d_bt.T)
        vs_mix[i] = vsm
        token = start_scatter(str(i), LAYER_WEIGHTS[i], dxl) if i > 0 else start_scatter("0c", LAYER_WEIGHTS[0][:2], dxl)
    grad_x = dxl[None]

    sums = {n: [None] * w[n].shape[0] for n in BIG}

    def collect(key, after):
        s_sems, r_sems, garrs, lands, gaxes, entries = scatters[key]
        garrs, recv = _scatter_wait(s_sems, r_sems, garrs, lands, gaxes, f"scatter_wait{key}", after)
        for (n, li), g, r3, ax in zip(entries, garrs, recv, gaxes):
            sums[n][li] = _sum_parts(r3, g, ax, chip1, f"sum_{n}_{li}")
        return sums[entries[-1][0]][entries[-1][1]]

    after = token
    for key in ("3", "2", "1"):
        after = collect(key, after)
    early = [(n, li) for i in (3, 2, 1) for n, li in LAYER_WEIGHTS[i]]
    late = list(LAYER_WEIGHTS[0][2:]) + list(LAYER_WEIGHTS[0][:2])
    sib = dict(zip(early, _swap_with_sibling([sums[n][li] for n, li in early], "swap_grad_sums_early")))
    after = sib[early[-1]]
    for key in ("0f", "0c"):
        after = collect(key, after)
    sib.update(zip(late, _swap_with_sibling([sums[n][li] for n, li in late], "swap_grad_sums_late")))

    dmod = _mod_bwd(jnp.stack(vs_mix), jnp.stack(vs_ffn), pv)
    small_g.update(ada_b=dmod[:, :6, :], norm1_g=dmod[:, 6, :], norm2_g=dmod[:, 7, :], final_g=vs_fin[1],
                   conv_w=jnp.stack(small_g["conv_w"]), conv_b=jnp.stack(small_g["conv_b"]))

    loss_part = (0.5 / D) * vs_fin[0, 0:1]
    part_shapes = [(1,)] + [tuple(small_g[n].shape) for n in SMALL]
    parts_sum = _allreduce_small(_pack([loss_part] + [small_g[n] for n in SMALL], 16), "reduce_small_grads", sib[late[-1]])
    summed = _unpack(parts_sum, part_shapes)
    loss = summed[0][0]
    gsum = dict(zip(SMALL, summed[1:]))
    dmod_all = _allgather_small(_pack([small_g["ada_b"]]), "gather_dmod", parts_sum)
    dmod_all = dmod_all.reshape(N_DEV, DEPTH, 6 * D)
    dmod_cols = lax.dynamic_slice_in_dim(dmod_all, chip * cols, cols, axis=2).transpose(1, 0, 2)
    g_ada_w = _ada_bwd(c16, jnp.pad(dmod_cols, ((0, 0), (0, 16 - N_DEV), (0, 0))))

    res = {}
    shp = ada_w.shape
    two = lambda t: t.reshape(shp[0] * shp[1], shp[2])
    res["ada_w"] = [t.reshape(shp) for t in _adamw(two(ada_w), [two(g_ada_w)], two(m_ada_w), two(v_ada_w), "adamw_ada_w")]

    n_shared = sum(w[n].size for n in SMALL if n not in SMALL_SHARDED)
    g_shards = [lax.dynamic_slice_in_dim(gsum[n], chip * w[n].shape[-1], w[n].shape[-1], axis=gsum[n].ndim - 1)
                for n in SMALL_SHARDED]
    g_flat = _pack_flat([parts_sum.reshape(-1)[1:1 + n_shared]] + g_shards)
    small_shapes = [tuple(w[n].shape) for n in SMALL]
    packed = _adamw(_pack_flat([w[n] for n in SMALL]), [g_flat], _pack_flat([m[n] for n in SMALL]), _pack_flat([v[n] for n in SMALL]),
                    "adamw_small")
    unpacked = [_unpack(t, small_shapes) for t in packed]
    for k, n in enumerate(SMALL):
        res[n] = [unpacked[0][k], unpacked[1][k], unpacked[2][k], unpacked[3][k]]

    for n in BIG:
        res[n] = _adamw_layers(w[n], sums[n], [sib[(n, li)] for li in range(w[n].shape[0])], m[n], v[n], f"adamw_{n}")

    outs = [loss, grad_x]
    for part in range(4):
        outs += [res[n][part] for n in WEIGHTS]
    return tuple(outs)
```

```python
import functools

import jax
import jax.numpy as jnp
from jax import lax
from jax.experimental import pallas as pl
from jax.experimental.pallas import tpu as pltpu

F32 = jnp.float32
BF16 = jnp.bfloat16
D = 1024
EPS = 1e-6
DEPTH = 4
MIXER_OF_LAYER = (0, 1, 2, 0)
S5_G, S5_H, S5_P = 64, 16, 64
S5_NB = 4
S5_BH = S5_H * 16
S5_BP = S5_P * 16
NSTATE = S5_G * S5_P
SG_HEADS, SG_CHUNK = 8, 128
ADAM_LR, ADAM_B1, ADAM_B2, ADAM_EPS, ADAM_WD, ADAM_STEP = 0.001, 0.9, 0.999, 1e-08, 0.01, 10
N_DEV = 8
MESH = pl.DeviceIdType.MESH
LANES = 1024
R_SH1, R_SC1, R_G1, R_SH2, R_SC2, R_G2, R_N1, R_N2 = range(8)


def _dot(a, b):
    return jnp.dot(a, b, preferred_element_type=F32)


def _dot_nt(a, b):
    return lax.dot_general(a, b, (((1,), (1,)), ((), ())), preferred_element_type=F32)


def _dot_tn(a, b):
    return lax.dot_general(a, b, (((0,), (0,)), ((), ())), preferred_element_type=F32)


def _bf(x):
    return x.astype(BF16)


def _sum0(x):
    return jnp.sum(x, axis=0, keepdims=True)


def _params(n_axes, vmem_mb=48):
    return pltpu.CompilerParams(dimension_semantics=("arbitrary",) * n_axes, vmem_limit_bytes=vmem_mb << 20)


def _rows(tm, cols, nt=None):
    if nt is None:
        return pl.BlockSpec((tm, cols), lambda i: (i, 0))
    return pl.BlockSpec((tm, cols), lambda i: (nt - 1 - i, 0))


def _whole(shape):
    nd = len(shape)
    return pl.BlockSpec(shape, lambda *_: (0,) * nd)


def _layer_w(r, c, layer):
    return pl.BlockSpec((None, r, c), lambda *_: (layer, 0, 0), pipeline_mode=pl.Buffered(1))


def _const_w(shape):
    nd = len(shape)
    return pl.BlockSpec(shape, lambda *_: (0,) * nd, pipeline_mode=pl.Buffered(1))


def _call_after(after, body, n_in, *, in_specs, **kw):
    if after is None:
        return pl.pallas_call(body, in_specs=in_specs, **kw), ()

    def body_after(*refs):
        return body(*refs[:n_in], *refs[n_in + 1:])

    return pl.pallas_call(body_after, in_specs=list(in_specs) + [pl.BlockSpec(memory_space=pl.ANY)], **kw), (after,)


def _norm_mod(x, ng, sc, sh):
    r = lax.rsqrt(jnp.mean(x * x, axis=-1, keepdims=True) + EPS)
    xn = x * r
    return (xn * ng) * (1.0 + sc) + sh, xn, r


def _norm_mod_bwd(dh, xn, r, ng, sc):
    dxn = dh * (ng * (1.0 + sc))
    return r * (dxn - xn * jnp.mean(dxn * xn, axis=-1, keepdims=True))


def _shift_down(z, prev8, k):
    row = lax.broadcasted_iota(jnp.int32, z.shape, 0)
    if k == 1:
        return jnp.where(row >= 1, pltpu.roll(z, 1, 0), prev8[7:8])
    return jnp.where(row >= 2, pltpu.roll(z, 2, 0), jnp.where(row == 0, prev8[6:7], prev8[7:8]))


def _shift_up(z, next8, k):
    n = z.shape[0]
    row = lax.broadcasted_iota(jnp.int32, z.shape, 0)
    if k == 1:
        return jnp.where(row <= n - 2, pltpu.roll(z, n - 1, 0), next8[0:1])
    return jnp.where(row <= n - 3, pltpu.roll(z, n - 2, 0), jnp.where(row == n - 2, next8[0:1], next8[1:2]))


def _place():
    x, y, c = lax.axis_index("x"), lax.axis_index("y"), lax.axis_index("c")
    chips = [(1 - x, y), (x, 1 - y), (1 - x, 1 - y)]
    return x, y, c, chips


def _allgather_small(x_shard, name, after=None):
    m_per, n = x_shard.shape

    def body(x_ref, out_ref, send_sems, recv_sems, local_sem):
        x, y, c, chips = _place()
        me, sibling = (x, y, c), (x, y, 1 - c)

        def rows(px, py, pc):
            return out_ref.at[pl.ds((4 * px + 2 * py + pc) * m_per, m_per), :]

        def copy(k, block, to, src=None):
            return pltpu.make_async_remote_copy(
                src_ref=rows(*block) if src is None else src, dst_ref=rows(*block),
                send_sem=send_sems.at[k], recv_sem=recv_sems.at[k], device_id=to, device_id_type=MESH)

        mine = pltpu.make_async_copy(x_ref, rows(*me), local_sem)
        mine.start()
        first = [copy(0, me, sibling, src=x_ref)]
        first += [copy(1 + j, me, (*chip, c), src=x_ref) for j, chip in enumerate(chips)]
        for cp in first:
            cp.start()
        passed = [copy(4 + j, (*chip, c), sibling) for j, chip in enumerate(chips)]
        for j, chip in enumerate(chips):
            copy(1 + j, (*chip, c), me).wait_recv()
            passed[j].start()
        copy(0, sibling, me).wait_recv()
        for j, chip in enumerate(chips):
            copy(4 + j, (*chip, 1 - c), me).wait_recv()
        for cp in first + passed:
            cp.wait_send()
        mine.wait()

    call, tail = _call_after(
        after, body, 1, name=name, out_shape=jax.ShapeDtypeStruct((N_DEV * m_per, n), F32),
        in_specs=[pl.BlockSpec(memory_space=pltpu.VMEM)], out_specs=pl.BlockSpec(memory_space=pltpu.VMEM),
        scratch_shapes=[pltpu.SemaphoreType.DMA((7,)), pltpu.SemaphoreType.DMA((7,)), pltpu.SemaphoreType.DMA],
        compiler_params=pltpu.CompilerParams(vmem_limit_bytes=48 << 20),
    )
    return call(x_shard, *tail)


def _allreduce_small(x_part, name, after=None):
    m, n = x_part.shape
    h = m // 2

    def body(x_ref, out_ref, sib_buf, slots, send_sems, recv_sems):
        x, y, c, chips = _place()
        k_me = 2 * x + y
        sibling = (x, y, 1 - c)
        mine = pl.ds(pl.multiple_of(c * h, 8), h)

        def copy(k, src, dst, to):
            return pltpu.make_async_remote_copy(src_ref=src, dst_ref=dst, send_sem=send_sems.at[k], recv_sem=recv_sems.at[k],
                                                device_id=to, device_id_type=MESH)

        swap = copy(0, x_ref, sib_buf, sibling)
        swap.start()
        swap.wait()
        slots[pl.ds(k_me, 1)] = (x_ref[mine, :] + sib_buf[mine, :])[None]
        my_slot = slots.at[pl.ds(k_me, 1)]
        sends = [copy(1 + j, my_slot, my_slot, (*chip, c)) for j, chip in enumerate(chips)]
        for cp in sends:
            cp.start()
        for j, chip in enumerate(chips):
            their_slot = slots.at[pl.ds(2 * chip[0] + chip[1], 1)]
            copy(1 + j, their_slot, their_slot, (x, y, c)).wait_recv()
        for cp in sends:
            cp.wait_send()
        out_ref[mine, :] = ((slots[0] + slots[1]) + slots[2]) + slots[3]
        give = copy(4, out_ref.at[mine, :], out_ref.at[mine, :], sibling)
        give.start()
        give.wait_send()
        theirs = pl.ds(pl.multiple_of((1 - c) * h, 8), h)
        copy(4, out_ref.at[theirs, :], out_ref.at[theirs, :], (x, y, c)).wait_recv()

    call, tail = _call_after(
        after, body, 1, name=name, out_shape=jax.ShapeDtypeStruct((m, n), F32),
        in_specs=[pl.BlockSpec(memory_space=pltpu.VMEM)], out_specs=pl.BlockSpec(memory_space=pltpu.VMEM),
        scratch_shapes=[pltpu.VMEM((m, n), F32), pltpu.VMEM((4, h, n), F32),
                        pltpu.SemaphoreType.DMA((5,)), pltpu.SemaphoreType.DMA((5,))],
        compiler_params=pltpu.CompilerParams(vmem_limit_bytes=48 << 20),
    )
    return call(x_part, *tail)


def _shard_region(ref, full_shape, axis, chip_k, half=None):
    _, r, c = full_shape
    if axis == 1:
        rs = r // 4
        if half is None:
            return ref.at[:, pl.ds(pl.multiple_of(chip_k * rs, 128), rs), :]
        return ref.at[:, pl.ds(pl.multiple_of(chip_k * rs + half * (rs // 2), 128), rs // 2), :]
    cs = c // 4
    if half is None:
        return ref.at[:, :, pl.ds(pl.multiple_of(chip_k * cs, 128), cs)]
    return ref.at[:, pl.ds(pl.multiple_of(half * (r // 2), 128), r // 2), pl.ds(pl.multiple_of(chip_k * cs, 128), cs)]


HBM_SPEC = pl.BlockSpec(memory_space=pltpu.HBM)
SEM_SPEC = pl.BlockSpec(memory_space=pltpu.SEMAPHORE)
ANY_SPEC = pl.BlockSpec(memory_space=pl.ANY)
SPLIT_COPY_PARAMS = pltpu.CompilerParams(has_side_effects=pltpu.SideEffectType.DATAFLOW_SIDE_EFFECTING)


def _in_hbm(arrs):
    return [pltpu.with_memory_space_constraint(a, pltpu.HBM) for a in arrs]


def _cast_place(w_stack, li, axis, chip, name):
    _, r, c = w_stack.shape
    full = (1, 4 * r, c) if axis == 1 else (1, r, 4 * c)
    tr = min(r, 256)
    if axis == 1:
        out_spec = pl.BlockSpec((None, tr, c), lambda i, k: (0, k[0] * (r // tr) + i, 0))
    else:
        out_spec = pl.BlockSpec((None, tr, c), lambda i, k: (0, i, k[0]))

    def body(k_ref, w_ref, o_ref):
        o_ref[...] = _bf(w_ref[...])

    return pl.pallas_call(
        body, name=name,
        grid_spec=pltpu.PrefetchScalarGridSpec(
            num_scalar_prefetch=1, grid=(r // tr,),
            in_specs=[pl.BlockSpec((None, tr, c), lambda i, k: (li, i, 0))], out_specs=out_spec),
        out_shape=jax.ShapeDtypeStruct(full, BF16),
        compiler_params=_params(1),
    )(chip, w_stack)


def _gather_start(lands, axes, name, after):
    n_arr = len(lands)
    fulls = [tuple(l.shape) for l in lands]

    def body(*refs):
        land = refs[:n_arr]
        send_sems, recv_sems = refs[n_arr + 1:n_arr + 3]
        token = refs[-1]
        x, y, c, chips = _place()
        k_me = 2 * x + y
        for a in range(n_arr):
            mine = _shard_region(land[a], fulls[a], axes[a], k_me, c)
            for j, chip in enumerate(chips):
                pltpu.make_async_remote_copy(
                    src_ref=mine, dst_ref=mine, send_sem=send_sems.at[a * 3 + j], recv_sem=recv_sems.at[a * 3 + j],
                    device_id=(*chip, c), device_id_type=MESH).start()
        token[...] = jnp.zeros_like(token)

    res = pl.pallas_call(
        body, name=name,
        out_shape=(pltpu.SemaphoreType.DMA((3 * n_arr,)), pltpu.SemaphoreType.DMA((3 * n_arr,)),
                   *[pltpu.HBM(f, BF16) for f in fulls], jax.ShapeDtypeStruct((8, 128), F32)),
        in_specs=[HBM_SPEC] * n_arr + [ANY_SPEC],
        out_specs=(SEM_SPEC, SEM_SPEC, *[HBM_SPEC] * n_arr, pl.BlockSpec(memory_space=pltpu.VMEM)),
        input_output_aliases={a: 2 + a for a in range(n_arr)},
        compiler_params=SPLIT_COPY_PARAMS,
    )(*_in_hbm(lands), after)
    return res[0], res[1], list(res[2:2 + n_arr]), res[-1]


def _gather_wait(send_sems, recv_sems, lands, axes, name, after):
    n_arr = len(lands)
    fulls = [tuple(l.shape) for l in lands]

    def body(*refs):
        land = refs[:n_arr]
        s_sems, r_sems = refs[n_arr:n_arr + 2]
        x, y, c, chips = _place()
        for a in range(n_arr):
            for j, chip in enumerate(chips):
                k_j = 2 * chip[0] + chip[1]
                got = _shard_region(land[a], fulls[a], axes[a], k_j, c)
                cp = pltpu.make_async_remote_copy(
                    src_ref=got, dst_ref=got, send_sem=s_sems.at[a * 3 + j], recv_sem=r_sems.at[a * 3 + j],
                    device_id=(x, y, c), device_id_type=MESH)
                cp.wait_send()
                cp.wait_recv()

    res = pl.pallas_call(
        body, name=name,
        out_shape=tuple(pltpu.HBM(f, BF16) for f in fulls),
        in_specs=[HBM_SPEC] * n_arr + [SEM_SPEC, SEM_SPEC, ANY_SPEC],
        out_specs=tuple([HBM_SPEC] * n_arr),
        input_output_aliases={a: a for a in range(n_arr)},
        compiler_params=SPLIT_COPY_PARAMS,
    )(*lands, send_sems, recv_sems, after)
    return list(res)


def _gather_share(lands, axes, name):
    n_arr = len(lands)
    fulls = [tuple(l.shape) for l in lands]

    def body(*refs):
        land_in, land = refs[:n_arr], refs[n_arr:2 * n_arr]
        send_sems, recv_sems = refs[2 * n_arr:]
        x, y, c, chips = _place()
        copies = []
        for a in range(n_arr):
            for j, chip in enumerate(chips):
                k_j = 2 * chip[0] + chip[1]
                cp = pltpu.make_async_remote_copy(
                    src_ref=_shard_region(land_in[a], fulls[a], axes[a], k_j, c),
                    dst_ref=_shard_region(land[a], fulls[a], axes[a], k_j, c),
                    send_sem=send_sems.at[a * 3 + j], recv_sem=recv_sems.at[a * 3 + j],
                    device_id=(x, y, 1 - c), device_id_type=MESH)
                cp.start()
                copies.append(cp)
        for cp in copies:
            cp.wait()

    return pl.pallas_call(
        body, name=name, out_shape=[jax.ShapeDtypeStruct(f, BF16) for f in fulls],
        in_specs=[ANY_SPEC] * n_arr, out_specs=[ANY_SPEC] * n_arr,
        input_output_aliases={a: a for a in range(n_arr)},
        scratch_shapes=[pltpu.SemaphoreType.DMA((3 * n_arr,)), pltpu.SemaphoreType.DMA((3 * n_arr,))],
    )(*lands)


def _scatter_shapes(grads, axes):
    out = []
    for g, ax in zip(grads, axes):
        shp = list(g.shape)
        shp[ax] //= 4
        out.append((3,) + tuple(shp[1:]))
    return out


def _scatter_start(grads, axes, name, after):
    n_arr = len(grads)
    shapes = _scatter_shapes(grads, axes)
    lands = [lax.empty(s, BF16) for s in shapes]

    def body(*refs):
        ins, land = refs[:n_arr], refs[n_arr:2 * n_arr]
        send_sems, recv_sems = refs[2 * n_arr + 1:2 * n_arr + 3]
        token = refs[-1]
        x, y, c, chips = _place()
        for a in range(n_arr):
            for j, chip in enumerate(chips):
                k_j = 2 * chip[0] + chip[1]
                pltpu.make_async_remote_copy(
                    src_ref=_shard_region(ins[a], grads[a].shape, axes[a], k_j), dst_ref=land[a].at[pl.ds(j, 1)],
                    send_sem=send_sems.at[a * 3 + j], recv_sem=recv_sems.at[a * 3 + j],
                    device_id=(*chip, c), device_id_type=MESH).start()
        token[...] = jnp.zeros_like(token)

    res = pl.pallas_call(
        body, name=name,
        out_shape=(pltpu.SemaphoreType.DMA((3 * n_arr,)), pltpu.SemaphoreType.DMA((3 * n_arr,)),
                   *[pltpu.HBM(g.shape, BF16) for g in grads], *[pltpu.HBM(s, BF16) for s in shapes],
                   jax.ShapeDtypeStruct((8, 128), F32)),
        in_specs=[HBM_SPEC] * (2 * n_arr) + [ANY_SPEC],
        out_specs=(SEM_SPEC, SEM_SPEC, *[HBM_SPEC] * (2 * n_arr), pl.BlockSpec(memory_space=pltpu.VMEM)),
        input_output_aliases={a: 2 + a for a in range(2 * n_arr)},
        compiler_params=SPLIT_COPY_PARAMS,
    )(*_in_hbm(grads), *_in_hbm(lands), after)
    return res[0], res[1], list(res[2:2 + n_arr]), list(res[2 + n_arr:2 + 2 * n_arr]), res[-1]


def _scatter_wait(send_sems, recv_sems, grads, lands, axes, name, after):
    n_arr = len(grads)

    def body(*refs):
        ins, land = refs[:n_arr], refs[n_arr:2 * n_arr]
        s_sems, r_sems = refs[2 * n_arr:2 * n_arr + 2]
        x, y, c, chips = _place()
        for a in range(n_arr):
            for j, chip in enumerate(chips):
                k_j = 2 * chip[0] + chip[1]
                cp = pltpu.make_async_remote_copy(
                    src_ref=_shard_region(ins[a], grads[a].shape, axes[a], k_j), dst_ref=land[a].at[pl.ds(j, 1)],
                    send_sem=s_sems.at[a * 3 + j], recv_sem=r_sems.at[a * 3 + j],
                    device_id=(x, y, c), device_id_type=MESH)
                cp.wait_send()
                cp.wait_recv()

    res = pl.pallas_call(
        body, name=name,
        out_shape=(*[pltpu.HBM(g.shape, BF16) for g in grads], *[pltpu.HBM(l.shape, BF16) for l in lands]),
        in_specs=[HBM_SPEC] * (2 * n_arr) + [SEM_SPEC, SEM_SPEC, ANY_SPEC],
        out_specs=tuple([HBM_SPEC] * (2 * n_arr)),
        input_output_aliases={a: a for a in range(2 * n_arr)},
        compiler_params=SPLIT_COPY_PARAMS,
    )(*grads, *lands, send_sems, recv_sems, after)
    return list(res[:n_arr]), list(res[n_arr:])


def _swap_start(arrs, name, after):
    n_arr = len(arrs)
    lands = [lax.empty(a.shape, a.dtype) for a in arrs]

    def body(*refs):
        ins, land = refs[:n_arr], refs[n_arr:2 * n_arr]
        send_sems, recv_sems = refs[2 * n_arr + 1:2 * n_arr + 3]
        token = refs[-1]
        x, y, c, _ = _place()
        for a in range(n_arr):
            pltpu.make_async_remote_copy(
                src_ref=ins[a], dst_ref=land[a], send_sem=send_sems.at[a], recv_sem=recv_sems.at[a],
                device_id=(x, y, 1 - c), device_id_type=MESH).start()
        token[...] = jnp.zeros_like(token)

    res = pl.pallas_call(
        body, name=name,
        out_shape=(pltpu.SemaphoreType.DMA((n_arr,)), pltpu.SemaphoreType.DMA((n_arr,)),
                   *[pltpu.HBM(a.shape, a.dtype) for a in arrs], *[pltpu.HBM(a.shape, a.dtype) for a in arrs],
                   jax.ShapeDtypeStruct((8, 128), F32)),
        in_specs=[HBM_SPEC] * (2 * n_arr) + [ANY_SPEC],
        out_specs=(SEM_SPEC, SEM_SPEC, *[HBM_SPEC] * (2 * n_arr), pl.BlockSpec(memory_space=pltpu.VMEM)),
        input_output_aliases={a: 2 + a for a in range(2 * n_arr)},
        compiler_params=SPLIT_COPY_PARAMS,
    )(*_in_hbm(arrs), *_in_hbm(lands), after)
    return res[0], res[1], list(res[2:2 + n_arr]), list(res[2 + n_arr:2 + 2 * n_arr]), res[-1]


def _swap_wait(send_sems, recv_sems, arrs, lands, name, after):
    n_arr = len(arrs)

    def body(*refs):
        ins, land = refs[:n_arr], refs[n_arr:2 * n_arr]
        s_sems, r_sems = refs[2 * n_arr:2 * n_arr + 2]
        x, y, c, _ = _place()
        for a in range(n_arr):
            cp = pltpu.make_async_remote_copy(
                src_ref=ins[a], dst_ref=land[a], send_sem=s_sems.at[a], recv_sem=r_sems.at[a],
                device_id=(x, y, c), device_id_type=MESH)
            cp.wait_send()
            cp.wait_recv()

    res = pl.pallas_call(
        body, name=name,
        out_shape=(*[pltpu.HBM(a.shape, a.dtype) for a in arrs], *[pltpu.HBM(a.shape, a.dtype) for a in arrs]),
        in_specs=[HBM_SPEC] * (2 * n_arr) + [SEM_SPEC, SEM_SPEC, ANY_SPEC],
        out_specs=tuple([HBM_SPEC] * (2 * n_arr)),
        input_output_aliases={a: a for a in range(2 * n_arr)},
        compiler_params=SPLIT_COPY_PARAMS,
    )(*arrs, *lands, send_sems, recv_sems, after)
    return list(res[:n_arr]), list(res[n_arr:])


def _swap_with_sibling(arrs, name):
    n_arr = len(arrs)

    def body(*refs):
        ins, outs = refs[:n_arr], refs[n_arr:2 * n_arr]
        send_sems, recv_sems = refs[2 * n_arr:]
        x, y, c, _ = _place()
        copies = []
        for a in range(n_arr):
            cp = pltpu.make_async_remote_copy(
                src_ref=ins[a], dst_ref=outs[a], send_sem=send_sems.at[a], recv_sem=recv_sems.at[a],
                device_id=(x, y, 1 - c), device_id_type=MESH)
            cp.start()
            copies.append(cp)
        for cp in copies:
            cp.wait()

    any_spec = pl.BlockSpec(memory_space=pl.ANY)
    return pl.pallas_call(
        body, name=name, out_shape=[jax.ShapeDtypeStruct(a.shape, a.dtype) for a in arrs],
        in_specs=[any_spec] * n_arr, out_specs=[any_spec] * n_arr,
        scratch_shapes=[pltpu.SemaphoreType.DMA((n_arr,)), pltpu.SemaphoreType.DMA((n_arr,))],
    )(*arrs)


def _mm_tn(a, b, name, out_dtype=BF16):
    L, m = a.shape
    n = b.shape[1]
    bm, bn, bk = min(m, 1024), min(n, 1024), min(L, 2048)
    nk = L // bk

    def body(a_ref, b_ref, o_ref, acc):
        k = pl.program_id(2)

        @pl.when(k == 0)
        def _():
            acc[...] = jnp.zeros_like(acc)

        acc[...] += _dot_tn(_bf(a_ref[...]), _bf(b_ref[...]))

        @pl.when(k == nk - 1)
        def _():
            o_ref[...] = acc[...].astype(out_dtype)

    return pl.pallas_call(
        body, name=name, grid=(m // bm, n // bn, nk),
        in_specs=[pl.BlockSpec((bk, bm), lambda i, j, k: (k, i)), pl.BlockSpec((bk, bn), lambda i, j, k: (k, j))],
        out_specs=pl.BlockSpec((bm, bn), lambda i, j, k: (i, j)),
        out_shape=jax.ShapeDtypeStruct((m, n), out_dtype),
        scratch_shapes=[pltpu.VMEM((bm, bn), F32)],
        compiler_params=_params(3),
    )(a, b)


def _mm_tn_blocks(a, b, wa, wb, name, after=None):
    L = a.shape[0]
    nb = a.shape[1] // wa
    bk = min(L, 1024)
    nk = L // bk

    def body(a_ref, b_ref, o_ref):
        @pl.when(pl.program_id(1) == 0)
        def _():
            o_ref[...] = jnp.zeros_like(o_ref)

        o_ref[...] += _dot_tn(_bf(a_ref[...]), _bf(b_ref[...]))

    call, tail = _call_after(
        after, body, 2, name=name, grid=(nb, nk),
        in_specs=[pl.BlockSpec((bk, wa), lambda j, k: (k, j)), pl.BlockSpec((bk, wb), lambda j, k: (k, j))],
        out_specs=pl.BlockSpec((None, wa, wb), lambda j, k: (j, 0, 0)),
        out_shape=jax.ShapeDtypeStruct((nb, wa, wb), F32),
        compiler_params=_params(2),
    )
    return call(a, b, *tail)


def _sum_parts(parts, own, axis, chip, name):
    _, r, c = parts.shape
    tr = min(r, 256)
    if axis == 1:
        own_spec = pl.BlockSpec((None, tr, c), lambda i, k: (0, k[0] * (r // tr) + i, 0))
    else:
        own_spec = pl.BlockSpec((None, tr, c), lambda i, k: (0, i, k[0]))

    def body(k_ref, p_ref, g_ref, o_ref):
        p = p_ref[...].astype(F32)
        o_ref[...] = ((p[0] + p[1]) + p[2]) + g_ref[...].astype(F32)

    return pl.pallas_call(
        body, name=name,
        grid_spec=pltpu.PrefetchScalarGridSpec(
            num_scalar_prefetch=1, grid=(r // tr,),
            in_specs=[pl.BlockSpec((3, tr, c), lambda i, k: (0, i, 0)), own_spec],
            out_specs=pl.BlockSpec((tr, c), lambda i, k: (i, 0))),
        out_shape=jax.ShapeDtypeStruct((r, c), F32),
        compiler_params=_params(1),
    )(chip, parts, own)


def _adamw(w, g_parts, m, v, name):
    n_g = len(g_parts)
    if w.ndim == 2:
        r, c = w.shape
        tr = r
        for cand in (512, 256, 128, 64, 32, 16, 8):
            if r % cand == 0 and cand * c * 4 <= (2 << 20):
                tr = cand
                break
        spec = pl.BlockSpec((tr, c), lambda i: (i, 0))
        tiling = dict(grid=(r // tr,), in_specs=[spec] * (3 + n_g), out_specs=[spec] * 4, compiler_params=_params(1))
    else:
        tiling = dict(compiler_params=pltpu.CompilerParams(vmem_limit_bytes=48 << 20))

    def body(*refs):
        w_ref, g_refs, m_ref, v_ref = refs[0], refs[1:1 + n_g], refs[1 + n_g], refs[2 + n_g]
        g = g_refs[0][...]
        for gr in g_refs[1:]:
            g = g + gr[...]
        _adamw_update(g, w_ref, m_ref, v_ref, *refs[3 + n_g:])

    return pl.pallas_call(body, name=name, out_shape=[jax.ShapeDtypeStruct(w.shape, F32)] * 4, **tiling)(w, *g_parts, m, v)


def _adamw_update(g, w_ref, m_ref, v_ref, g_out, d_out, m_out, v_out):
    m_new = ADAM_B1 * m_ref[...] + (1.0 - ADAM_B1) * g
    v_new = ADAM_B2 * v_ref[...] + (1.0 - ADAM_B2) * (g * g)
    m_hat = m_new * (1.0 / (1.0 - ADAM_B1 ** ADAM_STEP))
    v_hat = v_new * (1.0 / (1.0 - ADAM_B2 ** ADAM_STEP))
    g_out[...] = g
    d_out[...] = -ADAM_LR * (m_hat / (jnp.sqrt(v_hat) + ADAM_EPS) + ADAM_WD * w_ref[...])
    m_out[...] = m_new
    v_out[...] = v_new


def _adamw_layers(w, q_mine, q_sib, m, v, name):
    n, r, c = w.shape
    tr = r
    for cand in (512, 256, 128, 64, 32, 16, 8):
        if r % cand == 0 and cand * c * 4 <= (1 << 20):
            tr = cand
            break

    def body(*refs):
        w_ref, qm, qs, m_ref, v_ref = refs[0], refs[1:1 + n], refs[1 + n:1 + 2 * n], refs[1 + 2 * n], refs[2 + 2 * n]
        layer = pl.program_id(0)
        g = qm[0][...] + qs[0][...]
        for k in range(1, n):
            g = jnp.where(layer == k, qm[k][...] + qs[k][...], g)
        _adamw_update(g, w_ref, m_ref, v_ref, *refs[3 + 2 * n:])

    stacked = pl.BlockSpec((None, tr, c), lambda l, i: (l, i, 0))
    per_layer = [pl.BlockSpec((tr, c), lambda l, i, k=k: (jnp.where(l == k, i, 0), 0)) for k in range(n)]
    return pl.pallas_call(
        body, name=name, grid=(n, r // tr),
        in_specs=[stacked] + per_layer + per_layer + [stacked, stacked], out_specs=[stacked] * 4,
        out_shape=[jax.ShapeDtypeStruct(w.shape, F32)] * 4,
        compiler_params=_params(2),
    )(w, *q_mine, *q_sib, m, v)


def _ada_fwd(c16, ada_w, ada_b_cols):
    cols = ada_w.shape[2]

    def body(c_ref, w_ref, b_ref, o_ref):
        cv = c_ref[...]
        ca = _bf(cv * jax.nn.sigmoid(cv))
        o_ref[...] = _dot(ca, _bf(w_ref[...])) + b_ref[...]

    return pl.pallas_call(
        body, name="ada_fwd", grid=(DEPTH,),
        in_specs=[_whole((16, D)), pl.BlockSpec((None, D, cols), lambda i: (i, 0, 0)),
                  pl.BlockSpec((None, 1, cols), lambda i: (i, 0, 0))],
        out_specs=pl.BlockSpec((None, 16, cols), lambda i: (i, 0, 0)),
        out_shape=jax.ShapeDtypeStruct((DEPTH, 16, cols), F32),
        compiler_params=_params(1),
    )(c16, ada_w, ada_b_cols)


def _ada_bwd(c16, dmod16):
    cols = dmod16.shape[2]

    def body(c_ref, d_ref, o_ref):
        cv = c_ref[...]
        ca = _bf(cv * jax.nn.sigmoid(cv))
        o_ref[...] = _dot_tn(ca, _bf(d_ref[...]))

    return pl.pallas_call(
        body, name="ada_bwd", grid=(DEPTH,),
        in_specs=[_whole((16, D)), pl.BlockSpec((None, 16, cols), lambda i: (i, 0, 0))],
        out_specs=pl.BlockSpec((None, D, cols), lambda i: (i, 0, 0)),
        out_shape=jax.ShapeDtypeStruct((DEPTH, D, cols), F32),
        compiler_params=_params(1),
    )(c16, dmod16)


def _mod_bwd(vs_mix, vs_ffn, pv):
    def body(m_ref, f_ref, pv_ref, o_ref):
        for i in range(DEPTH):
            vm, vf, p = m_ref[i], f_ref[i], pv_ref[i]
            o_ref[i] = jnp.concatenate([
                vm[2:3], vm[1:2] * p[R_N1:R_N1 + 1], vm[0:1],
                vf[2:3], vf[1:2] * p[R_N2:R_N2 + 1], vf[0:1],
                vm[1:2] * (1.0 + p[R_SC1:R_SC1 + 1]), vf[1:2] * (1.0 + p[R_SC2:R_SC2 + 1])], axis=0)

    return pl.pallas_call(body, name="mod_bwd", out_shape=jax.ShapeDtypeStruct((DEPTH, 8, D), F32))(vs_mix, vs_ffn, pv)


def _ffn_fwd(x1, pv, w1, w2, layer, tm, after=None):
    L = x1.shape[0]
    dff = w1.shape[2]

    def body(x1_ref, pv_ref, w1_ref, w2_ref, x2_ref, h2_ref, a_ref, f_ref):
        x1v, p = x1_ref[...], pv_ref[...]
        h2, _, _ = _norm_mod(x1v, p[R_N2:R_N2 + 1], p[R_SC2:R_SC2 + 1], p[R_SH2:R_SH2 + 1])
        hb = _bf(h2)
        h2_ref[...] = hb
        a = _dot(hb, w1_ref[...])
        a_ref[...] = a
        ra = jnp.maximum(a, 0.0)
        f = _dot(_bf(ra * ra), w2_ref[...])
        f_ref[...] = f
        x2_ref[...] = x1v + p[R_G2:R_G2 + 1] * f

    call, tail = _call_after(
        after, body, 4, name=f"ffn_fwd{layer}", grid=(L // tm,),
        in_specs=[_rows(tm, D), pl.BlockSpec((None, 8, D), lambda i: (layer, 0, 0)), _layer_w(D, dff, 0), _layer_w(dff, D, 0)],
        out_specs=[_rows(tm, D), _rows(tm, D), _rows(tm, dff), _rows(tm, D)],
        out_shape=[jax.ShapeDtypeStruct((L, D), F32), jax.ShapeDtypeStruct((L, D), BF16),
                   jax.ShapeDtypeStruct((L, dff), F32), jax.ShapeDtypeStruct((L, D), F32)],
        compiler_params=_params(1, 56),
    )
    return call(x1, pv, w1, w2, *tail)


def _ffn_bwd(dx2, x1, a, f, pv, w1, w2, layer, tm, after=None):
    L = x1.shape[0]
    dff = w1.shape[2]
    extra = [] if after is None else [pl.BlockSpec(memory_space=pl.ANY)]
    extra_args = [] if after is None else [after]

    def body(dx2_ref, x1_ref, a_ref, f_ref, pv_ref, w1_ref, w2_ref, *rest):
        dx1_ref, p_ref, da_ref, df_ref, vs_ref = rest[len(extra):]

        @pl.when(pl.program_id(0) == 0)
        def _():
            vs_ref[...] = jnp.zeros_like(vs_ref)

        dx2v, p = dx2_ref[...], pv_ref[...]
        dfb = _bf(dx2v * p[R_G2:R_G2 + 1])
        df_ref[...] = dfb
        vs_ref[0:1, :] += _sum0(dx2v * f_ref[...])
        dp = _dot_nt(dfb, w2_ref[...])
        ra = jnp.maximum(a_ref[...], 0.0)
        p_ref[...] = _bf(ra * ra)
        dab = _bf(dp * (2.0 * ra))
        da_ref[...] = dab
        dh2 = _dot_nt(dab, w1_ref[...])
        _, xn, r = _norm_mod(x1_ref[...], p[R_N2:R_N2 + 1], p[R_SC2:R_SC2 + 1], p[R_SH2:R_SH2 + 1])
        dx1_ref[...] = dx2v + _norm_mod_bwd(dh2, xn, r, p[R_N2:R_N2 + 1], p[R_SC2:R_SC2 + 1])
        vs_ref[1:2, :] += _sum0(dh2 * xn)
        vs_ref[2:3, :] += _sum0(dh2)

    return pl.pallas_call(
        body, name=f"ffn_bwd{layer}", grid=(L // tm,),
        in_specs=[_rows(tm, D), _rows(tm, D), _rows(tm, dff), _rows(tm, D),
                  pl.BlockSpec((None, 8, D), lambda i: (layer, 0, 0)), _layer_w(D, dff, 0), _layer_w(dff, D, 0)] + extra,
        out_specs=[_rows(tm, D), _rows(tm, dff), _rows(tm, dff), _rows(tm, D), _whole((8, D))],
        out_shape=[jax.ShapeDtypeStruct((L, D), F32), jax.ShapeDtypeStruct((L, dff), BF16),
                   jax.ShapeDtypeStruct((L, dff), BF16), jax.ShapeDtypeStruct((L, D), BF16),
                   jax.ShapeDtypeStruct((8, D), F32)],
        compiler_params=_params(1, 56),
    )(dx2, x1, a, f, pv, w1, w2, *extra_args)


def _conv_fwd(x, pv, w_in, w_out, cw, layer, j, tm, after=None):
    L = x.shape[0]

    def body(x_ref, pv_ref, win_ref, wout_ref, cw_ref, x1_ref, h_ref, bcx_ref, conv_ref, q_ref, y_ref, carry):
        @pl.when(pl.program_id(0) == 0)
        def _():
            carry[...] = jnp.zeros_like(carry)

        xv, p, cwv = x_ref[...], pv_ref[...], cw_ref[...]
        h, _, _ = _norm_mod(xv, p[R_N1:R_N1 + 1], p[R_SC1:R_SC1 + 1], p[R_SH1:R_SH1 + 1])
        hb = _bf(h)
        h_ref[...] = hb
        bcx = _dot(hb, win_ref[...])
        bcx_ref[...] = bcx
        z = bcx[:, D:2 * D] * bcx[:, 2 * D:]
        prev8 = carry[...]
        conv = cwv[0:1] * _shift_down(z, prev8, 2) + cwv[1:2] * _shift_down(z, prev8, 1) + cwv[2:3] * z + cwv[3:4]
        conv_ref[...] = conv
        qb = _bf(bcx[:, :D] * conv)
        q_ref[...] = qb
        y = _dot(qb, wout_ref[...])
        y_ref[...] = y
        x1_ref[...] = xv + p[R_G1:R_G1 + 1] * y
        carry[...] = z[tm - 8:tm]

    call, tail = _call_after(
        after, body, 5, name=f"conv_fwd{layer}", grid=(L // tm,),
        in_specs=[_rows(tm, D), pl.BlockSpec((None, 8, D), lambda i: (layer, 0, 0)), _layer_w(D, 3 * D, 0), _layer_w(D, D, 0),
                  pl.BlockSpec((None, 8, D), lambda i: (j, 0, 0))],
        out_specs=[_rows(tm, D), _rows(tm, D), _rows(tm, 3 * D), _rows(tm, D), _rows(tm, D), _rows(tm, D)],
        out_shape=[jax.ShapeDtypeStruct((L, D), F32), jax.ShapeDtypeStruct((L, D), BF16), jax.ShapeDtypeStruct((L, 3 * D), F32),
                   jax.ShapeDtypeStruct((L, D), F32), jax.ShapeDtypeStruct((L, D), BF16), jax.ShapeDtypeStruct((L, D), F32)],
        scratch_shapes=[pltpu.VMEM((8, D), F32)],
        compiler_params=_params(1, 56),
    )
    return call(x, pv, w_in, w_out, cw, *tail)


def _conv_bwd(dx1, x, y, bcx, conv, pv, w_in, w_out, cw, layer, j, tm, after=None):
    L = x.shape[0]
    nt = L // tm

    def body(dx1_ref, x_ref, y_ref, bcx_ref, conv_ref, halo_ref, pv_ref, win_ref, wout_ref, cw_ref,
             dx_ref, dbcx_ref, dy_ref, vs_ref, carry):
        gi = pl.program_id(0)
        tile = nt - 1 - gi

        @pl.when(gi == 0)
        def _():
            vs_ref[...] = jnp.zeros_like(vs_ref)
            carry[...] = jnp.zeros_like(carry)

        dx1v, p, cwv = dx1_ref[...], pv_ref[...], cw_ref[...]
        dyb = _bf(dx1v * p[R_G1:R_G1 + 1])
        dy_ref[...] = dyb
        vs_ref[0:1, :] += _sum0(dx1v * y_ref[...])
        dq = _dot_nt(dyb, wout_ref[...])
        bcx = bcx_ref[...]
        b, cg, xh = bcx[:, :D], bcx[:, D:2 * D], bcx[:, 2 * D:]
        db = dq * conv_ref[...]
        dc = dq * b
        z = cg * xh
        halo = halo_ref[...]
        zprev = jnp.where(tile > 0, halo[:, D:2 * D] * halo[:, 2 * D:], 0.0)
        vs_ref[3:4, :] += _sum0(dc * _shift_down(z, zprev, 2))
        vs_ref[4:5, :] += _sum0(dc * _shift_down(z, zprev, 1))
        vs_ref[5:6, :] += _sum0(dc * z)
        vs_ref[6:7, :] += _sum0(dc)
        next8 = carry[...]
        dz = cwv[2:3] * dc + cwv[1:2] * _shift_up(dc, next8, 1) + cwv[0:1] * _shift_up(dc, next8, 2)
        dbb, dcgb, dxhb = _bf(db), _bf(dz * xh), _bf(dz * cg)
        dbcx_ref[:, 0:D] = dbb
        dbcx_ref[:, D:2 * D] = dcgb
        dbcx_ref[:, 2 * D:3 * D] = dxhb
        dh = (_dot_nt(dbb, win_ref[:, 0:D]) + _dot_nt(dcgb, win_ref[:, D:2 * D])) + _dot_nt(dxhb, win_ref[:, 2 * D:3 * D])
        _, xn, r = _norm_mod(x_ref[...], p[R_N1:R_N1 + 1], p[R_SC1:R_SC1 + 1], p[R_SH1:R_SH1 + 1])
        dx_ref[...] = dx1v + _norm_mod_bwd(dh, xn, r, p[R_N1:R_N1 + 1], p[R_SC1:R_SC1 + 1])
        vs_ref[1:2, :] += _sum0(dh * xn)
        vs_ref[2:3, :] += _sum0(dh)
        carry[...] = dc[0:8]

    halo_spec = pl.BlockSpec((8, 3 * D), lambda i: (jnp.maximum((nt - 1 - i) * (tm // 8) - 1, 0), 0))
    call, tail = _call_after(
        after, body, 10, name=f"conv_bwd{layer}", grid=(nt,),
        in_specs=[_rows(tm, D, nt), _rows(tm, D, nt), _rows(tm, D, nt), _rows(tm, 3 * D, nt), _rows(tm, D, nt), halo_spec,
                  pl.BlockSpec((None, 8, D), lambda i: (layer, 0, 0)), _layer_w(D, 3 * D, 0), _layer_w(D, D, 0),
                  pl.BlockSpec((None, 8, D), lambda i: (j, 0, 0))],
        out_specs=[_rows(tm, D, nt), _rows(tm, 3 * D, nt), _rows(tm, D, nt), _whole((8, D))],
        out_shape=[jax.ShapeDtypeStruct((L, D), F32), jax.ShapeDtypeStruct((L, 3 * D), BF16),
                   jax.ShapeDtypeStruct((L, D), BF16), jax.ShapeDtypeStruct((8, D), F32)],
        scratch_shapes=[pltpu.VMEM((8, D), F32)],
        compiler_params=_params(1, 56),
    )
    return call(dx1, x, y, bcx, conv, bcx, pv, w_in, w_out, cw, *tail)


def _s5_discretize(a_re, a_im, log_dt, bt_re, bt_im):
    dt = jnp.exp(log_dt)
    mag = jnp.exp(a_re * dt)
    abar_re = mag * jnp.cos(a_im * dt)
    abar_im = mag * jnp.sin(a_im * dt)
    den = a_re * a_re + a_im * a_im
    nr = abar_re - 1.0
    ni = abar_im
    f_re = (nr * a_re + ni * a_im) / den
    f_im = (ni * a_re - nr * a_im) / den
    bbar_re = f_re * bt_re - f_im * bt_im
    bbar_im = f_re * bt_im + f_im * bt_re
    return abar_re, abar_im, bbar_re, bbar_im


def _s5_params_fwd(a_re, a_im, log_dt, bt_re, bt_im):
    def body(ar, ai, ld, br, bi, o_ar, o_ai, o_br, o_bi):
        r = _s5_discretize(ar[...], ai[...], ld[...], br[...], bi[...])
        o_ar[...], o_ai[...], o_br[...], o_bi[...] = r

    gp = jax.ShapeDtypeStruct((S5_G, S5_P), F32)
    hgp = jax.ShapeDtypeStruct((S5_H, S5_G, S5_P), F32)
    return pl.pallas_call(body, name="s5_params_fwd", out_shape=[gp, gp, hgp, hgp])(a_re, a_im, log_dt, bt_re, bt_im)


def _s5_params_bwd(a_re, a_im, log_dt, bt_re, bt_im, d_ar, d_ai, d_br, d_bi):
    def body(ar, ai, ld, br, bi, gar, gai, gbr, gbi, o_ar, o_ai, o_ld, o_br, o_bi):
        _, vjp = jax.vjp(_s5_discretize, ar[...], ai[...], ld[...], br[...], bi[...])
        r = vjp((gar[...], gai[...], gbr[...], gbi[...]))
        o_ar[...], o_ai[...], o_ld[...], o_br[...], o_bi[...] = r

    gp = jax.ShapeDtypeStruct((S5_G, S5_P), F32)
    hgp = jax.ShapeDtypeStruct((S5_H, S5_G, S5_P), F32)
    return pl.pallas_call(body, name="s5_params_bwd", out_shape=[gp, gp, jax.ShapeDtypeStruct((S5_G, 1), F32), hgp, hgp])(
        a_re, a_im, log_dt, bt_re, bt_im, d_ar, d_ai, d_br, d_bi)


def _s5_in_fwd(x, pv, w_in, b_re, b_im, layer, tm, after=None):
    L = x.shape[0]

    def body(x_ref, pv_ref, win_ref, bre_ref, bim_ref, h_ref, u_ref, ore_ref, oim_ref):
        p = pv_ref[...]
        h, _, _ = _norm_mod(x_ref[...], p[R_N1:R_N1 + 1], p[R_SC1:R_SC1 + 1], p[R_SH1:R_SH1 + 1])
        hb = _bf(h)
        h_ref[...] = hb
        u = _dot(hb, win_ref[...])
        u_ref[...] = u
        ub = _bf(u)
        for k in range(S5_NB):
            uk = ub[:, k * S5_BH:(k + 1) * S5_BH]
            ore_ref[:, k * S5_BP:(k + 1) * S5_BP] = _dot(uk, bre_ref[k])
            oim_ref[:, k * S5_BP:(k + 1) * S5_BP] = _dot(uk, bim_ref[k])

    call, tail = _call_after(
        after, body, 5, name="s5_in_fwd", grid=(L // tm,),
        in_specs=[_rows(tm, D), pl.BlockSpec((None, 8, D), lambda i: (layer, 0, 0)), _layer_w(D, D, 0),
                  _const_w((S5_NB, S5_BH, S5_BP)), _const_w((S5_NB, S5_BH, S5_BP))],
        out_specs=[_rows(tm, D), _rows(tm, D), _rows(tm, NSTATE), _rows(tm, NSTATE)],
        out_shape=[jax.ShapeDtypeStruct((L, D), BF16), jax.ShapeDtypeStruct((L, D), F32),
                   jax.ShapeDtypeStruct((L, NSTATE), F32), jax.ShapeDtypeStruct((L, NSTATE), F32)],
        compiler_params=_params(1, 56),
    )
    return call(x, pv, w_in, b_re, b_im, *tail)


def _s5_scan_fwd(bu_re, bu_im, ar, ai, tr):
    L = bu_re.shape[0]
    nl = 1024

    def body(bre_ref, bim_ref, ar_ref, ai_ref, sre_ref, sim_ref, st_re, st_im):
        @pl.when(pl.program_id(1) == 0)
        def _():
            st_re[...] = jnp.zeros_like(st_re)
            st_im[...] = jnp.zeros_like(st_im)

        a_r, a_i = ar_ref[...], ai_ref[...]

        def step(t, carry):
            s_r, s_i = carry
            n_r = a_r * s_r - a_i * s_i + bre_ref[pl.ds(t, 1), :]
            n_i = a_r * s_i + a_i * s_r + bim_ref[pl.ds(t, 1), :]
            sre_ref[pl.ds(t, 1), :] = n_r
            sim_ref[pl.ds(t, 1), :] = n_i
            return n_r, n_i

        s_r, s_i = lax.fori_loop(0, tr, step, (st_re[...], st_im[...]), unroll=8)
        st_re[...] = s_r
        st_im[...] = s_i

    blk = pl.BlockSpec((tr, nl), lambda j, i: (i, j))
    vec = pl.BlockSpec((1, nl), lambda j, i: (0, j))
    return pl.pallas_call(
        body, name="s5_scan_fwd", grid=(NSTATE // nl, L // tr),
        in_specs=[blk, blk, vec, vec], out_specs=[blk, blk],
        out_shape=[jax.ShapeDtypeStruct((L, NSTATE), F32)] * 2,
        scratch_shapes=[pltpu.VMEM((1, nl), F32), pltpu.VMEM((1, nl), F32)],
        compiler_params=_params(2),
    )(bu_re, bu_im, ar, ai)


def _s5_out_fwd(x, u, s_re, s_im, pv, c_re, c_im, dvec, glu_w, glu_b, w_out, layer, tm):
    L = x.shape[0]

    def body(x_ref, u_ref, sre_ref, sim_ref, pv_ref, cre_ref, cim_ref, d_ref, gw_ref, gb_ref, wout_ref,
             x1_ref, y1_ref, zg_ref, y3_ref, y_ref):
        p = pv_ref[...]
        srb, sib = _bf(sre_ref[...]), _bf(sim_ref[...])
        parts = []
        for k in range(S5_NB):
            sl = slice(k * S5_BP, (k + 1) * S5_BP)
            parts.append(_dot(srb[:, sl], cre_ref[k]) - _dot(sib[:, sl], cim_ref[k]))
        y1 = jnp.concatenate(parts, axis=1) + d_ref[...] * u_ref[...]
        y1_ref[...] = y1
        y2 = jax.nn.gelu(y1)
        zg = _dot(_bf(y2), gw_ref[...]) + gb_ref[...]
        zg_ref[...] = zg
        y3b = _bf(y2 * jax.nn.sigmoid(zg))
        y3_ref[...] = y3b
        y = _dot(y3b, wout_ref[...])
        y_ref[...] = y
        x1_ref[...] = x_ref[...] + p[R_G1:R_G1 + 1] * y

    return pl.pallas_call(
        body, name="s5_out_fwd", grid=(L // tm,),
        in_specs=[_rows(tm, D), _rows(tm, D), _rows(tm, NSTATE), _rows(tm, NSTATE),
                  pl.BlockSpec((None, 8, D), lambda i: (layer, 0, 0)),
                  _const_w((S5_NB, S5_BP, S5_BH)), _const_w((S5_NB, S5_BP, S5_BH)), _whole((1, D)),
                  _layer_w(D, D, 0), _whole((1, D)), _layer_w(D, D, 0)],
        out_specs=[_rows(tm, D)] * 5,
        out_shape=[jax.ShapeDtypeStruct((L, D), F32), jax.ShapeDtypeStruct((L, D), F32), jax.ShapeDtypeStruct((L, D), F32),
                   jax.ShapeDtypeStruct((L, D), BF16), jax.ShapeDtypeStruct((L, D), F32)],
        compiler_params=_params(1, 56),
    )(x, u, s_re, s_im, pv, c_re, c_im, dvec, glu_w, glu_b, w_out)


def _s5_out_bwd(dx1, y, y1, zg, u, pv, c_re, c_im, dvec, glu_w, w_out, layer, tm, after=None):
    L = dx1.shape[0]

    def body(dx1_ref, y_ref, y1_ref, zg_ref, u_ref, pv_ref, cre_ref, cim_ref, d_ref, gw_ref, wout_ref,
             dy_ref, y2_ref, dzg_ref, dy1_ref, dus_ref, gre_ref, gim_ref, vs_ref):
        @pl.when(pl.program_id(0) == 0)
        def _():
            vs_ref[...] = jnp.zeros_like(vs_ref)

        dx1v, p = dx1_ref[...], pv_ref[...]
        dyb = _bf(dx1v * p[R_G1:R_G1 + 1])
        dy_ref[...] = dyb
        vs_ref[0:1, :] += _sum0(dx1v * y_ref[...])
        dy3 = _dot_nt(dyb, wout_ref[...])
        y2, gelu_vjp = jax.vjp(jax.nn.gelu, y1_ref[...])
        y2_ref[...] = _bf(y2)
        gate = jax.nn.sigmoid(zg_ref[...])
        dzg = dy3 * y2 * gate * (1.0 - gate)
        dzgb = _bf(dzg)
        dzg_ref[...] = dzgb
        vs_ref[1:2, :] += _sum0(dzg)
        dy2 = dy3 * gate + _dot_nt(dzgb, gw_ref[...])
        dy1 = gelu_vjp(dy2)[0]
        vs_ref[2:3, :] += _sum0(dy1 * u_ref[...])
        dus_ref[...] = dy1 * d_ref[...]
        dy1b = _bf(dy1)
        dy1_ref[...] = dy1b
        for k in range(S5_NB):
            dk = dy1b[:, k * S5_BH:(k + 1) * S5_BH]
            gre_ref[:, k * S5_BP:(k + 1) * S5_BP] = _dot_nt(dk, cre_ref[k])
            gim_ref[:, k * S5_BP:(k + 1) * S5_BP] = -_dot_nt(dk, cim_ref[k])

    call, tail = _call_after(
        after, body, 11, name="s5_out_bwd", grid=(L // tm,),
        in_specs=[_rows(tm, D)] * 5 + [pl.BlockSpec((None, 8, D), lambda i: (layer, 0, 0)),
                  _const_w((S5_NB, S5_BP, S5_BH)), _const_w((S5_NB, S5_BP, S5_BH)), _whole((1, D)),
                  _layer_w(D, D, 0), _layer_w(D, D, 0)],
        out_specs=[_rows(tm, D)] * 5 + [_rows(tm, NSTATE), _rows(tm, NSTATE), _whole((8, D))],
        out_shape=[jax.ShapeDtypeStruct((L, D), BF16)] * 4 + [jax.ShapeDtypeStruct((L, D), F32),
                   jax.ShapeDtypeStruct((L, NSTATE), F32), jax.ShapeDtypeStruct((L, NSTATE), F32),
                   jax.ShapeDtypeStruct((8, D), F32)],
        compiler_params=_params(1, 56),
    )
    return call(dx1, y, y1, zg, u, pv, c_re, c_im, dvec, glu_w, w_out, *tail)


def _s5_scan_bwd(g_re, g_im, s_re, s_im, ar, ai, tr):
    L = g_re.shape[0]
    nl = 1024
    nt = L // tr

    def body(gre_ref, gim_ref, sre_ref, sim_ref, hre_ref, him_ref, ar_ref, ai_ref, lre_ref, lim_ref, da_ref, st_re, st_im):
        gi = pl.program_id(1)
        tile = nt - 1 - gi

        @pl.when(gi == 0)
        def _():
            st_re[...] = jnp.zeros_like(st_re)
            st_im[...] = jnp.zeros_like(st_im)
            da_ref[...] = jnp.zeros_like(da_ref)

        a_r, a_i = ar_ref[...], ai_ref[...]

        def step(k, carry):
            l_r, l_i = carry
            t = tr - 1 - k
            n_r = gre_ref[pl.ds(t, 1), :] + a_r * l_r + a_i * l_i
            n_i = gim_ref[pl.ds(t, 1), :] - a_i * l_r + a_r * l_i
            lre_ref[pl.ds(t, 1), :] = n_r
            lim_ref[pl.ds(t, 1), :] = n_i
            return n_r, n_i

        l_r, l_i = lax.fori_loop(0, tr, step, (st_re[...], st_im[...]), unroll=8)
        st_re[...] = l_r
        st_im[...] = l_i
        lam_r, lam_i = lre_ref[...], lim_ref[...]
        p_r = jnp.where(tile > 0, hre_ref[...], 0.0)
        p_i = jnp.where(tile > 0, him_ref[...], 0.0)
        sp_r = _shift_down(sre_ref[...], p_r, 1)
        sp_i = _shift_down(sim_ref[...], p_i, 1)
        da_ref[0:1, :] += _sum0(lam_r * sp_r + lam_i * sp_i)
        da_ref[1:2, :] += _sum0(lam_i * sp_r - lam_r * sp_i)

    blk = pl.BlockSpec((tr, nl), lambda j, i: (nt - 1 - i, j))
    halo = pl.BlockSpec((8, nl), lambda j, i: (jnp.maximum((nt - 1 - i) * (tr // 8) - 1, 0), j))
    vec = pl.BlockSpec((1, nl), lambda j, i: (0, j))
    return pl.pallas_call(
        body, name="s5_scan_bwd", grid=(NSTATE // nl, nt),
        in_specs=[blk, blk, blk, blk, halo, halo, vec, vec],
        out_specs=[blk, blk, pl.BlockSpec((8, nl), lambda j, i: (0, j))],
        out_shape=[jax.ShapeDtypeStruct((L, NSTATE), F32)] * 2 + [jax.ShapeDtypeStruct((8, NSTATE), F32)],
        scratch_shapes=[pltpu.VMEM((1, nl), F32), pltpu.VMEM((1, nl), F32)],
        compiler_params=_params(2),
    )(g_re, g_im, s_re, s_im, s_re, s_im, ar, ai)


def _s5_in_bwd(dx1, lam_re, lam_im, du_skip, x, pv, b_re, b_im, w_in, layer, tm):
    L = x.shape[0]

    def body(dx1_ref, lre_ref, lim_ref, dus_ref, x_ref, pv_ref, bre_ref, bim_ref, win_ref, dx_ref, du_ref, vs_ref):
        @pl.when(pl.program_id(0) == 0)
        def _():
            vs_ref[...] = jnp.zeros_like(vs_ref)

        p = pv_ref[...]
        lrb, lib = _bf(lre_ref[...]), _bf(lim_ref[...])
        parts = []
        for k in range(S5_NB):
            sl = slice(k * S5_BP, (k + 1) * S5_BP)
            parts.append(_dot_nt(lrb[:, sl], bre_ref[k]) + _dot_nt(lib[:, sl], bim_ref[k]))
        dub = _bf(jnp.concatenate(parts, axis=1) + dus_ref[...])
        du_ref[...] = dub
        dh = _dot_nt(dub, win_ref[...])
        _, xn, r = _norm_mod(x_ref[...], p[R_N1:R_N1 + 1], p[R_SC1:R_SC1 + 1], p[R_SH1:R_SH1 + 1])
        dx_ref[...] = dx1_ref[...] + _norm_mod_bwd(dh, xn, r, p[R_N1:R_N1 + 1], p[R_SC1:R_SC1 + 1])
        vs_ref[1:2, :] += _sum0(dh * xn)
        vs_ref[2:3, :] += _sum0(dh)

    return pl.pallas_call(
        body, name="s5_in_bwd", grid=(L // tm,),
        in_specs=[_rows(tm, D), _rows(tm, NSTATE), _rows(tm, NSTATE), _rows(tm, D), _rows(tm, D),
                  pl.BlockSpec((None, 8, D), lambda i: (layer, 0, 0)),
                  _const_w((S5_NB, S5_BH, S5_BP)), _const_w((S5_NB, S5_BH, S5_BP)), _layer_w(D, D, 0)],
        out_specs=[_rows(tm, D), _rows(tm, D), _whole((8, D))],
        out_shape=[jax.ShapeDtypeStruct((L, D), F32), jax.ShapeDtypeStruct((L, D), BF16), jax.ShapeDtypeStruct((8, D), F32)],
        compiler_params=_params(1, 56),
    )(dx1, lam_re, lam_im, du_skip, x, pv, b_re, b_im, w_in)


NSEG = 8
SCAN_LANES = 1024


def _to_segments(x):
    n, c = x.shape
    return x.reshape(NSEG, n // NSEG, c).transpose(1, 0, 2).reshape(n, c)


def _from_segments(x):
    n, c = x.shape
    return x.reshape(n // NSEG, NSEG, c).transpose(1, 0, 2).reshape(n, c)


def _segment_scan(re_ref, im_ref, st_re, st_im, a_re, a_im, n_slabs, adjoint, write):
    for q in range(NSTATE // SCAN_LANES):
        ls = slice(q * SCAN_LANES, (q + 1) * SCAN_LANES)
        ar = jnp.broadcast_to(a_re[:, ls], (8, SCAN_LANES))
        ai = jnp.broadcast_to(a_im[:, ls], (8, SCAN_LANES))

        def step(k, carry, ls=ls, ar=ar, ai=ai):
            s_r, s_i = carry
            slab = (n_slabs - 1 - k) if adjoint else k
            rows = pl.ds(pl.multiple_of(slab * 8, 8), 8)
            b_r, b_i = re_ref[rows, ls], im_ref[rows, ls]
            if adjoint:
                n_r = b_r + ar * s_r + ai * s_i
                n_i = b_i - ai * s_r + ar * s_i
            else:
                n_r = ar * s_r - ai * s_i + b_r
                n_i = ar * s_i + ai * s_r + b_i
            if write:
                re_ref[rows, ls] = n_r
                im_ref[rows, ls] = n_i
            return n_r, n_i

        s_r, s_i = lax.fori_loop(0, n_slabs, step, (st_re[:, ls], st_im[:, ls]), unroll=4)
        st_re[:, ls] = s_r
        st_im[:, ls] = s_i


def _s5_segment_states(e_re, e_im, ar, ai, seg_len, adjoint):
    def body(ere_ref, eim_ref, ar_ref, ai_ref, ore_ref, oim_ref):
        p_r, p_i = ar_ref[...], ai_ref[...]
        if adjoint:
            p_i = -p_i
        acc_r, acc_i = jnp.ones_like(p_r), jnp.zeros_like(p_r)
        n = seg_len
        while n:
            if n & 1:
                acc_r, acc_i = acc_r * p_r - acc_i * p_i, acc_r * p_i + acc_i * p_r
            n >>= 1
            if n:
                p_r, p_i = p_r * p_r - p_i * p_i, 2.0 * p_r * p_i
        e_r, e_i = ere_ref[...], eim_ref[...]
        s_r, s_i = jnp.zeros_like(acc_r), jnp.zeros_like(acc_r)
        rows_r, rows_i = [None] * NSEG, [None] * NSEG
        order = range(NSEG - 1, -1, -1) if adjoint else range(NSEG)
        for j in order:
            rows_r[j], rows_i[j] = s_r, s_i
            s_r, s_i = (acc_r * s_r - acc_i * s_i + e_r[j:j + 1], acc_r * s_i + acc_i * s_r + e_i[j:j + 1])
        ore_ref[...] = jnp.concatenate(rows_r, axis=0)
        oim_ref[...] = jnp.concatenate(rows_i, axis=0)

    st = jax.ShapeDtypeStruct((NSEG, NSTATE), F32)
    return pl.pallas_call(body, name="s5_segment_states_bwd" if adjoint else "s5_segment_states_fwd", out_shape=[st, st])(
        e_re, e_im, ar, ai)


def _s5_fwd_ends(x, pv, w_in, b_re, b_im, ar, ai, layer, tm, after=None):
    L = x.shape[0]

    def body(x_ref, pv_ref, win_ref, bre_ref, bim_ref, ar_ref, ai_ref, h_ref, u_ref, ere_ref, eim_ref, bu_re, bu_im):
        @pl.when(pl.program_id(0) == 0)
        def _():
            ere_ref[...] = jnp.zeros_like(ere_ref)
            eim_ref[...] = jnp.zeros_like(eim_ref)

        p = pv_ref[...]
        h, _, _ = _norm_mod(x_ref[...], p[R_N1:R_N1 + 1], p[R_SC1:R_SC1 + 1], p[R_SH1:R_SH1 + 1])
        hb = _bf(h)
        h_ref[...] = hb
        u = _dot(hb, win_ref[...])
        u_ref[...] = u
        ub = _bf(u)
        for k in range(S5_NB):
            uk = ub[:, k * S5_BH:(k + 1) * S5_BH]
            bu_re[:, k * S5_BP:(k + 1) * S5_BP] = _dot(uk, bre_ref[k])
            bu_im[:, k * S5_BP:(k + 1) * S5_BP] = _dot(uk, bim_ref[k])
        _segment_scan(bu_re, bu_im, ere_ref, eim_ref, ar_ref[...], ai_ref[...], tm // 8, adjoint=False, write=False)

    call, tail = _call_after(
        after, body, 7, name="s5_fwd_ends", grid=(L // tm,),
        in_specs=[_rows(tm, D), pl.BlockSpec((None, 8, D), lambda i: (layer, 0, 0)), _layer_w(D, D, 0),
                  _const_w((S5_NB, S5_BH, S5_BP)), _const_w((S5_NB, S5_BH, S5_BP)), _whole((1, NSTATE)), _whole((1, NSTATE))],
        out_specs=[_rows(tm, D), _rows(tm, D), _whole((NSEG, NSTATE)), _whole((NSEG, NSTATE))],
        out_shape=[jax.ShapeDtypeStruct((L, D), BF16), jax.ShapeDtypeStruct((L, D), F32),
                   jax.ShapeDtypeStruct((NSEG, NSTATE), F32), jax.ShapeDtypeStruct((NSEG, NSTATE), F32)],
        scratch_shapes=[pltpu.VMEM((tm, NSTATE), F32), pltpu.VMEM((tm, NSTATE), F32)],
        compiler_params=_params(1, 56),
    )
    return call(x, pv, w_in, b_re, b_im, ar, ai, *tail)


def _s5_fwd_out(x, u, pv, b_re, b_im, s0_re, s0_im, ar, ai, c_re, c_im, dvec, glu_w, glu_b, w_out, layer, tm):
    L = x.shape[0]

    def body(x_ref, u_ref, pv_ref, bre_ref, bim_ref, s0re_ref, s0im_ref, ar_ref, ai_ref, cre_ref, cim_ref, d_ref, gw_ref,
             gb_ref, wout_ref, x1_ref, sre_ref, sim_ref, y1_ref, zg_ref, y3_ref, y_ref, st_re, st_im):
        @pl.when(pl.program_id(0) == 0)
        def _():
            st_re[...] = s0re_ref[...]
            st_im[...] = s0im_ref[...]

        p = pv_ref[...]
        uv = u_ref[...]
        ub = _bf(uv)
        for k in range(S5_NB):
            uk = ub[:, k * S5_BH:(k + 1) * S5_BH]
            sre_ref[:, k * S5_BP:(k + 1) * S5_BP] = _dot(uk, bre_ref[k])
            sim_ref[:, k * S5_BP:(k + 1) * S5_BP] = _dot(uk, bim_ref[k])
        _segment_scan(sre_ref, sim_ref, st_re, st_im, ar_ref[...], ai_ref[...], tm // 8, adjoint=False, write=True)
        parts = []
        for k in range(S5_NB):
            sl = slice(k * S5_BP, (k + 1) * S5_BP)
            parts.append(_dot(_bf(sre_ref[:, sl]), cre_ref[k]) - _dot(_bf(sim_ref[:, sl]), cim_ref[k]))
        y1 = jnp.concatenate(parts, axis=1) + d_ref[...] * uv
        y1_ref[...] = y1
        y2 = jax.nn.gelu(y1)
        zg = _dot(_bf(y2), gw_ref[...]) + gb_ref[...]
        zg_ref[...] = zg
        y3b = _bf(y2 * jax.nn.sigmoid(zg))
        y3_ref[...] = y3b
        y = _dot(y3b, wout_ref[...])
        y_ref[...] = y
        x1_ref[...] = x_ref[...] + p[R_G1:R_G1 + 1] * y

    return pl.pallas_call(
        body, name="s5_fwd_out", grid=(L // tm,),
        in_specs=[_rows(tm, D), _rows(tm, D), pl.BlockSpec((None, 8, D), lambda i: (layer, 0, 0)),
                  _const_w((S5_NB, S5_BH, S5_BP)), _const_w((S5_NB, S5_BH, S5_BP)),
                  _whole((NSEG, NSTATE)), _whole((NSEG, NSTATE)), _whole((1, NSTATE)), _whole((1, NSTATE)),
                  _const_w((S5_NB, S5_BP, S5_BH)), _const_w((S5_NB, S5_BP, S5_BH)), _whole((1, D)),
                  _layer_w(D, D, 0), _whole((1, D)), _layer_w(D, D, 0)],
        out_specs=[_rows(tm, D), _rows(tm, NSTATE), _rows(tm, NSTATE), _rows(tm, D), _rows(tm, D), _rows(tm, D), _rows(tm, D)],
        out_shape=[jax.ShapeDtypeStruct((L, D), F32), jax.ShapeDtypeStruct((L, NSTATE), F32), jax.ShapeDtypeStruct((L, NSTATE), F32),
                   jax.ShapeDtypeStruct((L, D), F32), jax.ShapeDtypeStruct((L, D), F32),
                   jax.ShapeDtypeStruct((L, D), BF16), jax.ShapeDtypeStruct((L, D), F32)],
        scratch_shapes=[pltpu.VMEM((NSEG, NSTATE), F32), pltpu.VMEM((NSEG, NSTATE), F32)],
        compiler_params=_params(1, 56),
    )(x, u, pv, b_re, b_im, s0_re, s0_im, ar, ai, c_re, c_im, dvec, glu_w, glu_b, w_out)


def _s5_bwd_ends(dx1, y, y1, zg, u, pv, c_re, c_im, ar, ai, dvec, glu_w, w_out, layer, tm, after=None):
    L = dx1.shape[0]
    nt = L // tm

    def body(dx1_ref, y_ref, y1_ref, zg_ref, u_ref, pv_ref, cre_ref, cim_ref, ar_ref, ai_ref, d_ref, gw_ref, wout_ref,
             dy_ref, y2_ref, dzg_ref, dy1_ref, dus_ref, ere_ref, eim_ref, vs_ref, g_re, g_im):
        @pl.when(pl.program_id(0) == 0)
        def _():
            vs_ref[...] = jnp.zeros_like(vs_ref)
            ere_ref[...] = jnp.zeros_like(ere_ref)
            eim_ref[...] = jnp.zeros_like(eim_ref)

        dx1v, p = dx1_ref[...], pv_ref[...]
        dyb = _bf(dx1v * p[R_G1:R_G1 + 1])
        dy_ref[...] = dyb
        vs_ref[0:1, :] += _sum0(dx1v * y_ref[...])
        dy3 = _dot_nt(dyb, wout_ref[...])
        y2, gelu_vjp = jax.vjp(jax.nn.gelu, y1_ref[...])
        y2_ref[...] = _bf(y2)
        gate = jax.nn.sigmoid(zg_ref[...])
        dzg = dy3 * y2 * gate * (1.0 - gate)
        dzgb = _bf(dzg)
        dzg_ref[...] = dzgb
        vs_ref[1:2, :] += _sum0(dzg)
        dy2 = dy3 * gate + _dot_nt(dzgb, gw_ref[...])
        dy1 = gelu_vjp(dy2)[0]
        vs_ref[2:3, :] += _sum0(dy1 * u_ref[...])
        dus_ref[...] = dy1 * d_ref[...]
        dy1b = _bf(dy1)
        dy1_ref[...] = dy1b
        for k in range(S5_NB):
            dk = dy1b[:, k * S5_BH:(k + 1) * S5_BH]
            g_re[:, k * S5_BP:(k + 1) * S5_BP] = _dot_nt(dk, cre_ref[k])
            g_im[:, k * S5_BP:(k + 1) * S5_BP] = -_dot_nt(dk, cim_ref[k])
        _segment_scan(g_re, g_im, ere_ref, eim_ref, ar_ref[...], ai_ref[...], tm // 8, adjoint=True, write=False)

    call, tail = _call_after(
        after, body, 13, name="s5_bwd_ends", grid=(nt,),
        in_specs=[_rows(tm, D, nt)] * 5 + [pl.BlockSpec((None, 8, D), lambda i: (layer, 0, 0)),
                  _const_w((S5_NB, S5_BP, S5_BH)), _const_w((S5_NB, S5_BP, S5_BH)), _whole((1, NSTATE)), _whole((1, NSTATE)),
                  _whole((1, D)), _layer_w(D, D, 0), _layer_w(D, D, 0)],
        out_specs=[_rows(tm, D, nt)] * 5 + [_whole((NSEG, NSTATE)), _whole((NSEG, NSTATE)), _whole((8, D))],
        out_shape=[jax.ShapeDtypeStruct((L, D), BF16)] * 4 + [jax.ShapeDtypeStruct((L, D), F32),
                   jax.ShapeDtypeStruct((NSEG, NSTATE), F32), jax.ShapeDtypeStruct((NSEG, NSTATE), F32),
                   jax.ShapeDtypeStruct((8, D), F32)],
        scratch_shapes=[pltpu.VMEM((tm, NSTATE), F32), pltpu.VMEM((tm, NSTATE), F32)],
        compiler_params=_params(1, 56),
    )
    return call(dx1, y, y1, zg, u, pv, c_re, c_im, ar, ai, dvec, glu_w, w_out, *tail)


def _s5_bwd_in(dx1, dy1_b, du_skip, x, s_re, s_im, pv, b_re, b_im, c_re, c_im, l0_re, l0_im, ar, ai, w_in, layer, tm):
    L = x.shape[0]
    nt = L // tm

    def body(dx1_ref, dy1_ref, dus_ref, x_ref, sre_ref, sim_ref, hre_ref, him_ref, lre_ref, lim_ref, pv_ref, bre_ref, bim_ref,
             cre_ref, cim_ref, l0re_ref, l0im_ref, ar_ref, ai_ref, win_ref,
             dx_ref, du_ref, lamre_ref, lamim_ref, da_ref, vs_ref, g_re, g_im, st_re, st_im):
        gi = pl.program_id(0)
        tile = nt - 1 - gi

        @pl.when(gi == 0)
        def _():
            vs_ref[...] = jnp.zeros_like(vs_ref)
            da_ref[...] = jnp.zeros_like(da_ref)
            st_re[...] = l0re_ref[...]
            st_im[...] = l0im_ref[...]

        p = pv_ref[...]
        dy1b = dy1_ref[...]
        for k in range(S5_NB):
            dk = dy1b[:, k * S5_BH:(k + 1) * S5_BH]
            g_re[:, k * S5_BP:(k + 1) * S5_BP] = _dot_nt(dk, cre_ref[k])
            g_im[:, k * S5_BP:(k + 1) * S5_BP] = -_dot_nt(dk, cim_ref[k])
        _segment_scan(g_re, g_im, st_re, st_im, ar_ref[...], ai_ref[...], tm // 8, adjoint=True, write=True)
        lam_r, lam_i = g_re[...], g_im[...]
        lrb, lib = _bf(lam_r), _bf(lam_i)
        lamre_ref[...] = lrb
        lamim_ref[...] = lib

        def wrapped(last_ref):
            z = last_ref[...]
            row = lax.broadcasted_iota(jnp.int32, z.shape, 0)
            return jnp.where(row >= 1, pltpu.roll(z, 1, 0), 0.0)

        first_r = jnp.where(tile > 0, hre_ref[...], wrapped(lre_ref))
        first_i = jnp.where(tile > 0, him_ref[...], wrapped(lim_ref))
        sp_r = jnp.concatenate([first_r, sre_ref[0:tm - 8, :]], axis=0)
        sp_i = jnp.concatenate([first_i, sim_ref[0:tm - 8, :]], axis=0)
        da_ref[0:1, :] += _sum0(lam_r * sp_r + lam_i * sp_i)
        da_ref[1:2, :] += _sum0(lam_i * sp_r - lam_r * sp_i)

        parts = []
        for k in range(S5_NB):
            sl = slice(k * S5_BP, (k + 1) * S5_BP)
            parts.append(_dot_nt(lrb[:, sl], bre_ref[k]) + _dot_nt(lib[:, sl], bim_ref[k]))
        dub = _bf(jnp.concatenate(parts, axis=1) + dus_ref[...])
        du_ref[...] = dub
        dh = _dot_nt(dub, win_ref[...])
        _, xn, r = _norm_mod(x_ref[...], p[R_N1:R_N1 + 1], p[R_SC1:R_SC1 + 1], p[R_SH1:R_SH1 + 1])
        dx_ref[...] = dx1_ref[...] + _norm_mod_bwd(dh, xn, r, p[R_N1:R_N1 + 1], p[R_SC1:R_SC1 + 1])
        vs_ref[1:2, :] += _sum0(dh * xn)
        vs_ref[2:3, :] += _sum0(dh)

    halo = pl.BlockSpec((8, NSTATE), lambda i: (jnp.maximum((nt - 1 - i) * (tm // 8) - 1, 0), 0))
    last = pl.BlockSpec((8, NSTATE), lambda i: (L // 8 - 1, 0))
    return pl.pallas_call(
        body, name="s5_bwd_in", grid=(nt,),
        in_specs=[_rows(tm, D, nt), _rows(tm, D, nt), _rows(tm, D, nt), _rows(tm, D, nt), _rows(tm, NSTATE, nt), _rows(tm, NSTATE, nt),
                  halo, halo, last, last, pl.BlockSpec((None, 8, D), lambda i: (layer, 0, 0)),
                  _const_w((S5_NB, S5_BH, S5_BP)), _const_w((S5_NB, S5_BH, S5_BP)),
                  _const_w((S5_NB, S5_BP, S5_BH)), _const_w((S5_NB, S5_BP, S5_BH)),
                  _whole((NSEG, NSTATE)), _whole((NSEG, NSTATE)), _whole((1, NSTATE)), _whole((1, NSTATE)), _layer_w(D, D, 0)],
        out_specs=[_rows(tm, D, nt), _rows(tm, D, nt), _rows(tm, NSTATE, nt), _rows(tm, NSTATE, nt), _whole((8, NSTATE)), _whole((8, D))],
        out_shape=[jax.ShapeDtypeStruct((L, D), F32), jax.ShapeDtypeStruct((L, D), BF16),
                   jax.ShapeDtypeStruct((L, NSTATE), BF16), jax.ShapeDtypeStruct((L, NSTATE), BF16),
                   jax.ShapeDtypeStruct((8, NSTATE), F32), jax.ShapeDtypeStruct((8, D), F32)],
        scratch_shapes=[pltpu.VMEM((tm, NSTATE), F32), pltpu.VMEM((tm, NSTATE), F32),
                        pltpu.VMEM((NSEG, NSTATE), F32), pltpu.VMEM((NSEG, NSTATE), F32)],
        compiler_params=_params(1, 60),
    )(dx1, dy1_b, du_skip, x, s_re, s_im, s_re, s_im, s_re, s_im, pv, b_re, b_im, c_re, c_im, l0_re, l0_im, ar, ai, w_in)


def _blockdiag_b(bt):
    b = bt.reshape(S5_H, S5_NB, 16, S5_P).transpose(1, 2, 0, 3)
    eye = jnp.eye(16, dtype=bt.dtype)
    return (b[:, :, :, None, :] * eye[None, :, None, :, None]).reshape(S5_NB, S5_BH, S5_BP)


def _unblock_b(d):
    d = jnp.einsum("bghgp->bghp", d.reshape(S5_NB, 16, S5_H, 16, S5_P))
    return d.transpose(2, 0, 1, 3).reshape(S5_H, S5_G, S5_P)


def _blockdiag_c(cm):
    c4 = cm.reshape(S5_NB, 16, S5_H, S5_P)
    eye = jnp.eye(16, dtype=cm.dtype)
    out = c4.transpose(0, 1, 3, 2)[:, :, :, None, :] * eye[None, :, None, :, None]
    return out.reshape(S5_NB, S5_BP, S5_BH)


def _unblock_c(d):
    d = jnp.einsum("bgpgh->bghp", d.reshape(S5_NB, 16, S5_P, 16, S5_H))
    return d.reshape(S5_G, S5_H, S5_P)


def _tril_mask():
    return lax.broadcasted_iota(jnp.int32, (SG_CHUNK, SG_CHUNK), 0) >= lax.broadcasted_iota(jnp.int32, (SG_CHUNK, SG_CHUNK), 1)


def _sg_fwd(x, pv, w_in, w_s, b_t, vg, w_out, layer, tm, after=None):
    L = x.shape[0]
    nc = tm // SG_CHUNK

    def body(x_ref, pv_ref, win_ref, ws_ref, bt_ref, vg_ref, wout_ref, x1_ref, h_ref, uv_ref, vm_ref, q_ref, y_ref):
        xv, p = x_ref[...], pv_ref[...]
        h, _, _ = _norm_mod(xv, p[R_N1:R_N1 + 1], p[R_SC1:R_SC1 + 1], p[R_SH1:R_SH1 + 1])
        hb = _bf(h)
        h_ref[...] = hb
        uv = _dot(hb, win_ref[...])
        uv_ref[...] = uv
        v = uv[:, D:]
        rv = lax.rsqrt(jnp.mean(v * v, axis=-1, keepdims=True) + EPS)
        vnb = _bf((v * rv) * vg_ref[...])
        mask = _tril_mask()
        bt = bt_ref[...]
        for hd in range(SG_HEADS):
            wm = _bf(jnp.where(mask, ws_ref[hd], 0.0))
            cs = slice(hd * SG_CHUNK, (hd + 1) * SG_CHUNK)
            for ck in range(nc):
                rs = slice(ck * SG_CHUNK, (ck + 1) * SG_CHUNK)
                vm_ref[rs, cs] = _dot(wm, vnb[rs, cs]) + bt[:, hd:hd + 1]
        qb = _bf(uv[:, :D] * vm_ref[...])
        q_ref[...] = qb
        y = _dot(qb, wout_ref[...])
        y_ref[...] = y
        x1_ref[...] = xv + p[R_G1:R_G1 + 1] * y

    call, tail = _call_after(
        after, body, 7, name="sg_fwd", grid=(L // tm,),
        in_specs=[_rows(tm, D), pl.BlockSpec((None, 8, D), lambda i: (layer, 0, 0)), _layer_w(D, 2 * D, 0),
                  _whole((SG_HEADS, SG_CHUNK, SG_CHUNK)), _whole((SG_CHUNK, SG_HEADS)), _whole((1, D)), _layer_w(D, D, 0)],
        out_specs=[_rows(tm, D), _rows(tm, D), _rows(tm, 2 * D), _rows(tm, D), _rows(tm, D), _rows(tm, D)],
        out_shape=[jax.ShapeDtypeStruct((L, D), F32), jax.ShapeDtypeStruct((L, D), BF16), jax.ShapeDtypeStruct((L, 2 * D), F32),
                   jax.ShapeDtypeStruct((L, D), F32), jax.ShapeDtypeStruct((L, D), BF16), jax.ShapeDtypeStruct((L, D), F32)],
        compiler_params=_params(1, 56),
    )
    return call(x, pv, w_in, w_s, b_t, vg, w_out, *tail)


def _sg_bwd(dx1, x, y, uv, vm, pv, w_in, w_s, vg, w_out, layer, tm, after=None):
    L = x.shape[0]
    nc = tm // SG_CHUNK

    def body(dx1_ref, x_ref, y_ref, uv_ref, vm_ref, pv_ref, win_ref, ws_ref, vg_ref, wout_ref,
             dx_ref, duv_ref, dy_ref, vs_ref, dws_ref, dbt_ref, dvn_scr):
        @pl.when(pl.program_id(0) == 0)
        def _():
            vs_ref[...] = jnp.zeros_like(vs_ref)
            dws_ref[...] = jnp.zeros_like(dws_ref)
            dbt_ref[...] = jnp.zeros_like(dbt_ref)

        dx1v, p = dx1_ref[...], pv_ref[...]
        dyb = _bf(dx1v * p[R_G1:R_G1 + 1])
        dy_ref[...] = dyb
        vs_ref[0:1, :] += _sum0(dx1v * y_ref[...])
        dq = _dot_nt(dyb, wout_ref[...])
        uv = uv_ref[...]
        u, v = uv[:, :D], uv[:, D:]
        dub = _bf(dq * vm_ref[...])
        dvm = dq * u
        dvmb = _bf(dvm)
        rv = lax.rsqrt(jnp.mean(v * v, axis=-1, keepdims=True) + EPS)
        vh = v * rv
        vgv = vg_ref[...]
        vnb = _bf(vh * vgv)
        mask = _tril_mask()
        for hd in range(SG_HEADS):
            wm = _bf(jnp.where(mask, ws_ref[hd], 0.0))
            cs = slice(hd * SG_CHUNK, (hd + 1) * SG_CHUNK)
            dws = jnp.zeros((SG_CHUNK, SG_CHUNK), F32)
            dbs = jnp.zeros((SG_CHUNK, 1), F32)
            for ck in range(nc):
                rs = slice(ck * SG_CHUNK, (ck + 1) * SG_CHUNK)
                dvn_scr[rs, cs] = _dot_tn(wm, dvmb[rs, cs])
                dws = dws + _dot_nt(dvmb[rs, cs], vnb[rs, cs])
                dbs = dbs + jnp.sum(dvm[rs, cs], axis=1, keepdims=True)
            dws_ref[hd] += jnp.where(mask, dws, 0.0)
            dbt_ref[:, hd:hd + 1] += dbs
        dvn = dvn_scr[...]
        vs_ref[3:4, :] += _sum0(dvn * vh)
        dvnn = dvn * vgv
        dvb = _bf(rv * (dvnn - vh * jnp.mean(dvnn * vh, axis=-1, keepdims=True)))
        duv_ref[:, 0:D] = dub
        duv_ref[:, D:2 * D] = dvb
        dh = _dot_nt(dub, win_ref[:, 0:D]) + _dot_nt(dvb, win_ref[:, D:2 * D])
        _, xn, r = _norm_mod(x_ref[...], p[R_N1:R_N1 + 1], p[R_SC1:R_SC1 + 1], p[R_SH1:R_SH1 + 1])
        dx_ref[...] = dx1v + _norm_mod_bwd(dh, xn, r, p[R_N1:R_N1 + 1], p[R_SC1:R_SC1 + 1])
        vs_ref[1:2, :] += _sum0(dh * xn)
        vs_ref[2:3, :] += _sum0(dh)

    call, tail = _call_after(
        after, body, 10, name="sg_bwd", grid=(L // tm,),
        in_specs=[_rows(tm, D), _rows(tm, D), _rows(tm, D), _rows(tm, 2 * D), _rows(tm, D),
                  pl.BlockSpec((None, 8, D), lambda i: (layer, 0, 0)), _layer_w(D, 2 * D, 0),
                  _whole((SG_HEADS, SG_CHUNK, SG_CHUNK)), _whole((1, D)), _layer_w(D, D, 0)],
        out_specs=[_rows(tm, D), _rows(tm, 2 * D), _rows(tm, D), _whole((8, D)),
                   _whole((SG_HEADS, SG_CHUNK, SG_CHUNK)), _whole((SG_CHUNK, SG_HEADS))],
        out_shape=[jax.ShapeDtypeStruct((L, D), F32), jax.ShapeDtypeStruct((L, 2 * D), BF16), jax.ShapeDtypeStruct((L, D), BF16),
                   jax.ShapeDtypeStruct((8, D), F32), jax.ShapeDtypeStruct((SG_HEADS, SG_CHUNK, SG_CHUNK), F32),
                   jax.ShapeDtypeStruct((SG_CHUNK, SG_HEADS), F32)],
        scratch_shapes=[pltpu.VMEM((tm, D), F32)],
        compiler_params=_params(1, 56),
    )
    return call(dx1, x, y, uv, vm, pv, w_in, w_s, vg, w_out, *tail)


def _final(x, target, fg, tm):
    L = x.shape[0]

    def body(x_ref, t_ref, g_ref, dx_ref, vs_ref):
        @pl.when(pl.program_id(0) == 0)
        def _():
            vs_ref[...] = jnp.zeros_like(vs_ref)

        xv, g = x_ref[...], g_ref[...]
        r = lax.rsqrt(jnp.mean(xv * xv, axis=-1, keepdims=True) + EPS)
        xn = xv * r
        e = xn * g - t_ref[...]
        vs_ref[0:1, :] += jnp.sum(e * e)
        dout = e * (1.0 / D)
        vs_ref[1:2, :] += _sum0(dout * xn)
        dxn = dout * g
        dx_ref[...] = r * (dxn - xn * jnp.mean(dxn * xn, axis=-1, keepdims=True))

    return pl.pallas_call(
        body, name="final_loss", grid=(L // tm,),
        in_specs=[_rows(tm, D), _rows(tm, D), _whole((1, D))],
        out_specs=[_rows(tm, D), _whole((8, D))],
        out_shape=[jax.ShapeDtypeStruct((L, D), F32), jax.ShapeDtypeStruct((8, D), F32)],
        compiler_params=_params(1),
    )(x, target, fg)


def _pack_flat(arrs, multiple=LANES):
    flat = jnp.concatenate([a.reshape(-1).astype(F32) for a in arrs])
    return jnp.pad(flat, (0, -flat.shape[0] % multiple))


def _pack(arrs, row_multiple=8):
    return _pack_flat(arrs, row_multiple * LANES).reshape(-1, LANES)


def _unpack(buf, shapes, lead=()):
    flat = buf.reshape(lead + (-1,))
    out, off = [], 0
    for s in shapes:
        n = 1
        for d in s:
            n *= d
        out.append(flat[..., off:off + n].reshape(lead + tuple(s)))
        off += n
    return out


BIG = ("ff_w1", "ff_w2", "conv_w_in", "conv_w_out", "ssm_w_in", "ssm_glu_w", "ssm_w_out", "sg_w_in", "sg_w_out")
BIG_AXIS = {"ff_w1": 2, "ff_w2": 1, "conv_w_in": 2, "conv_w_out": 1, "ssm_w_in": 1, "ssm_glu_w": 1, "ssm_w_out": 1,
            "sg_w_in": 2, "sg_w_out": 1}
LAYER_WEIGHTS = (
    (("conv_w_in", 0), ("conv_w_out", 0), ("ff_w1", 0), ("ff_w2", 0)),
    (("ssm_w_in", 0), ("ssm_glu_w", 0), ("ssm_w_out", 0), ("ff_w1", 1), ("ff_w2", 1)),
    (("sg_w_in", 0), ("sg_w_out", 0), ("ff_w1", 2), ("ff_w2", 2)),
    (("conv_w_in", 1), ("conv_w_out", 1), ("ff_w1", 3), ("ff_w2", 3)),
)
GATHER_GROUPS = tuple(grp for lw in LAYER_WEIGHTS for grp in (lw[:-2], lw[-2:]))
SMALL_SHARDED = ("conv_w", "conv_b", "sg_v_g")
SMALL = ("ada_b", "norm1_g", "norm2_g", "final_g", "ssm_a_re", "ssm_a_im", "ssm_log_dt", "ssm_b_re", "ssm_b_im", "ssm_c_re",
         "ssm_c_im", "ssm_d", "ssm_glu_b", "sg_w_s", "sg_b_s") + SMALL_SHARDED
WEIGHTS = ("ada_w", "ada_b", "norm1_g", "norm2_g", "ff_w1", "ff_w2", "final_g", "conv_w_in", "conv_w", "conv_b", "conv_w_out",
           "ssm_w_in", "ssm_a_re", "ssm_a_im", "ssm_log_dt", "ssm_b_re", "ssm_b_im", "ssm_c_re", "ssm_c_im", "ssm_d",
           "ssm_glu_w", "ssm_glu_b", "ssm_w_out", "sg_w_in", "sg_v_g", "sg_w_s", "sg_b_s", "sg_w_out")


def kernel(x, c, ada_w, ada_b, norm1_g, norm2_g, ff_w1, ff_w2, final_g, conv_w_in, conv_w, conv_b, conv_w_out, ssm_w_in, ssm_a_re, ssm_a_im, ssm_log_dt, ssm_b_re, ssm_b_im, ssm_c_re, ssm_c_im, ssm_d, ssm_glu_w, ssm_glu_b, ssm_w_out, sg_w_in, sg_v_g, sg_w_s, sg_b_s, sg_w_out, loss_target, m_ada_w, m_ada_b, m_norm1_g, m_norm2_g, m_ff_w1, m_ff_w2, m_final_g, m_conv_w_in, m_conv_w, m_conv_b, m_conv_w_out, m_ssm_w_in, m_ssm_a_re, m_ssm_a_im, m_ssm_log_dt, m_ssm_b_re, m_ssm_b_im, m_ssm_c_re, m_ssm_c_im, m_ssm_d, m_ssm_glu_w, m_ssm_glu_b, m_ssm_w_out, m_sg_w_in, m_sg_v_g, m_sg_w_s, m_sg_b_s, m_sg_w_out, v_ada_w, v_ada_b, v_norm1_g, v_norm2_g, v_ff_w1, v_ff_w2, v_final_g, v_conv_w_in, v_conv_w, v_conv_b, v_conv_w_out, v_ssm_w_in, v_ssm_a_re, v_ssm_a_im, v_ssm_log_dt, v_ssm_b_re, v_ssm_b_im, v_ssm_c_re, v_ssm_c_im, v_ssm_d, v_ssm_glu_w, v_ssm_glu_b, v_ssm_w_out, v_sg_w_in, v_sg_v_g, v_sg_w_s, v_sg_b_s, v_sg_w_out):
    args = dict(locals())
    w = {n: args[n] for n in WEIGHTS}
    m = {n: args["m_" + n] for n in WEIGHTS}
    v = {n: args["v_" + n] for n in WEIGHTS}
    L = x.shape[1]
    tm = min(L, 256)
    tr = min(L, 512)
    chip = 2 * lax.axis_index("x") + lax.axis_index("y")
    me = 2 * chip + lax.axis_index("c")
    xin = x[0]
    target = loss_target[0]
    chip1 = chip.reshape(1).astype(jnp.int32)

    gathers = []

    def start_gather(g, after):
        entries = GATHER_GROUPS[g]
        axes = [BIG_AXIS[n] for n, _ in entries]
        lands = [_cast_place(w[n], li, BIG_AXIS[n], chip1, f"cast_{n}_{li}") for n, li in entries]
        s_sems, r_sems, lands, token = _gather_start(lands, axes, f"gather_start{g}", after)
        gathers.append((s_sems, r_sems, lands, axes))
        return token

    def weights_of(g, after):
        s_sems, r_sems, lands, axes = gathers[g]
        lands = _gather_wait(s_sems, r_sems, lands, axes, f"gather_wait{g}", after)
        return dict(zip([n for n, _ in GATHER_GROUPS[g]], _gather_share(lands, axes, f"gather_share{g}")))

    small_in = _pack([c, conv_w, conv_b, sg_v_g])
    got = _allgather_small(small_in, "gather_small_inputs").reshape(N_DEV, -1)
    c_all, cw_sh, cb_sh, vg_sh = _unpack(got, [(D,), conv_w.shape, conv_b.shape, sg_v_g.shape], lead=(N_DEV,))
    conv_w_full = jnp.concatenate([cw_sh[2 * k] for k in range(4)], axis=-1)
    conv_b_full = jnp.concatenate([cb_sh[2 * k] for k in range(4)], axis=-1)
    vg_full = jnp.concatenate([vg_sh[2 * k] for k in range(4)], axis=-1)
    c16 = jnp.pad(c_all, ((0, 16 - N_DEV), (0, 0)))

    cols = ada_w.shape[2]
    ada_b_cols = lax.dynamic_slice_in_dim(ada_b, chip * cols, cols, axis=1)[:, None, :]
    mod_sh = _ada_fwd(c16, ada_w, ada_b_cols)[:, :N_DEV, :]
    mod_all = _allgather_small(_pack([mod_sh]), "gather_mod").reshape(N_DEV, -1)
    mod_all = _unpack(mod_all, [mod_sh.shape], lead=(N_DEV,))[0]
    mod_mine = lax.dynamic_index_in_dim(mod_all[0::2], me, axis=2, keepdims=False)
    mod_mine = mod_mine.transpose(1, 0, 2).reshape(DEPTH, 6, D)
    pv = jnp.concatenate([mod_mine, norm1_g[:, None, :], norm2_g[:, None, :]], axis=1)

    start_gather(1, start_gather(0, pv))

    cw_rows = jnp.concatenate([conv_w_full, conv_b_full[:, None, :], jnp.zeros((conv_w_full.shape[0], 4, D), F32)], axis=1)

    a_re, a_im = ssm_a_re[0], ssm_a_im[0]
    log_dt = ssm_log_dt[0][:, None]
    bt_re, bt_im = ssm_b_re[0].transpose(2, 0, 1), ssm_b_im[0].transpose(2, 0, 1)
    abar_re, abar_im, bbar_re, bbar_im = _s5_params_fwd(a_re, a_im, log_dt, bt_re, bt_im)
    ar_vec, ai_vec = abar_re.reshape(1, NSTATE), abar_im.reshape(1, NSTATE)
    bd_re, bd_im = _bf(_blockdiag_b(bbar_re)), _bf(_blockdiag_b(bbar_im))
    cd_re, cd_im = _bf(_blockdiag_c(ssm_c_re[0])), _bf(_blockdiag_c(ssm_c_im[0]))

    saved = []
    fulls = []
    xl = xin
    for i in range(DEPTH):
        kind = MIXER_OF_LAYER[i]
        j = i // 3
        full = weights_of(2 * i, cd_im if i == 0 else xl)
        fulls.append(full)
        tok = start_gather(2 * i + 2, next(iter(full.values()))) if i + 1 < DEPTH else None
        if kind == 0:
            x1, h, bcx, conv, q, y = _conv_fwd(xl, pv, full["conv_w_in"], full["conv_w_out"], cw_rows, i, j, tm, after=tok)
            mix = dict(h=h, bcx=bcx, conv=conv, q=q, y=y)
        elif kind == 1:
            xp = _to_segments(xl)
            h, u, e_re, e_im = _s5_fwd_ends(xp, pv, full["ssm_w_in"], bd_re, bd_im, ar_vec, ai_vec, i, tm, after=tok)
            s0_re, s0_im = _s5_segment_states(e_re, e_im, ar_vec, ai_vec, L // NSEG, adjoint=False)
            x1p, s_re, s_im, y1, zg, y3, y = _s5_fwd_out(xp, u, pv, bd_re, bd_im, s0_re, s0_im, ar_vec, ai_vec, cd_re, cd_im,
                                                         ssm_d, full["ssm_glu_w"], ssm_glu_b, full["ssm_w_out"], i, tm)
            x1 = _from_segments(x1p)
            mix = dict(xp=xp, h=h, u=u, s_re=s_re, s_im=s_im, y1=y1, zg=zg, y3=y3, y=y)
        else:
            x1, h, uv, vm, q, y = _sg_fwd(xl, pv, full["sg_w_in"], sg_w_s[0], sg_b_s[0].T, vg_full, full["sg_w_out"], i, tm,
                                          after=tok)
            mix = dict(h=h, uv=uv, vm=vm, q=q, y=y)
        full.update(weights_of(2 * i + 1, x1))
        tok = start_gather(2 * i + 3, full["ff_w2"]) if i + 1 < DEPTH else None
        x2, h2, a, f = _ffn_fwd(x1, pv, full["ff_w1"], full["ff_w2"], i, tm, after=tok)
        saved.append(dict(x=xl, x1=x1, h2=h2, a=a, f=f, **mix))
        xl = x2

    dxl, vs_fin = _final(xl, target, final_g[None, :], tm)

    gfull = {n: [None] * w[n].shape[0] for n in BIG}
    vs_mix, vs_ffn = [None] * DEPTH, [None] * DEPTH
    small_g = {}
    scatters = {}
    token = None

    def start_scatter(key, entries, after):
        garrs = [gfull[n][li][None] for n, li in entries]
        gaxes = [BIG_AXIS[n] for n, _ in entries]
        s_sems, r_sems, garrs, lands, tok = _scatter_start(garrs, gaxes, f"scatter_start{key}", after)
        scatters[key] = (s_sems, r_sems, garrs, lands, gaxes, entries)
        return tok

    for i in reversed(range(DEPTH)):
        kind = MIXER_OF_LAYER[i]
        j = i // 3
        sv = saved[i]
        full = fulls[i]
        dx1, p_b, da_b, df_b, vs_ffn[i] = _ffn_bwd(dxl, sv["x1"], sv["a"], sv["f"], pv, full["ff_w1"], full["ff_w2"], i, tm,
                                                   after=token)
        gfull["ff_w1"][i] = _mm_tn(sv["h2"], da_b, f"wgrad_ff_w1_{i}")
        gfull["ff_w2"][i] = _mm_tn(p_b, df_b, f"wgrad_ff_w2_{i}")
        if i == 0:
            token = start_scatter("0f", LAYER_WEIGHTS[0][2:], dx1)
        if kind == 0:
            dxl, dbcx_b, dy_b, vsm = _conv_bwd(dx1, sv["x"], sv["y"], sv["bcx"], sv["conv"], pv, full["conv_w_in"],
                                               full["conv_w_out"], cw_rows, i, j, tm, after=token if i == 0 else None)
            gfull["conv_w_in"][j] = _mm_tn(sv["h"], dbcx_b, f"wgrad_conv_w_in_{j}")
            gfull["conv_w_out"][j] = _mm_tn(sv["q"], dy_b, f"wgrad_conv_w_out_{j}")
            small_g.setdefault("conv_w", [None, None])[j] = vsm[3:6]
            small_g.setdefault("conv_b", [None, None])[j] = vsm[6]
        elif kind == 1:
            dx1p = _to_segments(dx1)
            dy_b, y2_b, dzg_b, dy1_b, du_skip, eb_re, eb_im, vsm = _s5_bwd_ends(
                dx1p, sv["y"], sv["y1"], sv["zg"], sv["u"], pv, cd_re, cd_im, ar_vec, ai_vec, ssm_d, full["ssm_glu_w"],
                full["ssm_w_out"], i, tm)
            l0_re, l0_im = _s5_segment_states(eb_re, eb_im, ar_vec, ai_vec, L // NSEG, adjoint=True)
            dxp, du_b, lam_re, lam_im, dabar, vs_in = _s5_bwd_in(
                dx1p, dy1_b, du_skip, sv["xp"], sv["s_re"], sv["s_im"], pv, bd_re, bd_im, cd_re, cd_im, l0_re, l0_im,
                ar_vec, ai_vec, full["ssm_w_in"], i, tm)
            dxl = _from_segments(dxp)
            gfull["ssm_w_out"][0] = _mm_tn(sv["y3"], dy_b, "wgrad_ssm_w_out")
            gfull["ssm_glu_w"][0] = _mm_tn(y2_b, dzg_b, "wgrad_ssm_glu_w")
            gfull["ssm_w_in"][0] = _mm_tn(sv["h"], du_b, "wgrad_ssm_w_in")
            s5_late = dict(s_re=sv["s_re"], s_im=sv["s_im"], u=sv["u"], dy1_b=dy1_b, lam_re=lam_re, lam_im=lam_im, dabar=dabar)
            small_g.update(ssm_d=vsm[2], ssm_glu_b=vsm[1])
            vsm = jnp.concatenate([vsm[0:1], vs_in[1:3], jnp.zeros((5, D), F32)], axis=0)
        else:
            dxl, duv_b, dy_b, vsm, d_ws, d_bt = _sg_bwd(dx1, sv["x"], sv["y"], sv["uv"], sv["vm"], pv, full["sg_w_in"],
                                                        sg_w_s[0], vg_full, full["sg_w_out"], i, tm)
            gfull["sg_w_in"][0] = _mm_tn(sv["h"], duv_b, "wgrad_sg_w_in")
            gfull["sg_w_out"][0] = _mm_tn(sv["q"], dy_b, "wgrad_sg_w_out")
            small_g.update(sg_v_g=vsm[3], sg_w_s=d_ws, sg_b_s=d_bt.T)
        vs_mix[i] = vsm
        token = start_scatter(str(i), LAYER_WEIGHTS[i], dxl) if i > 0 else start_scatter("0c", LAYER_WEIGHTS[0][:2], dxl)
    grad_x = dxl[None]

    sums = {n: [None] * w[n].shape[0] for n in BIG}

    def collect(key, after):
        s_sems, r_sems, garrs, lands, gaxes, entries = scatters[key]
        garrs, recv = _scatter_wait(s_sems, r_sems, garrs, lands, gaxes, f"scatter_wait{key}", after)
        for (n, li), g, r3, ax in zip(entries, garrs, recv, gaxes):
            sums[n][li] = _sum_parts(r3, g, ax, chip1, f"sum_{n}_{li}")
        return sums[entries[-1][0]][entries[-1][1]]

    after = token
    for key in ("3", "2", "1"):
        after = collect(key, after)
    early = [(n, li) for i in (3, 2, 1) for n, li in LAYER_WEIGHTS[i]]
    late = list(LAYER_WEIGHTS[0][2:]) + list(LAYER_WEIGHTS[0][:2])
    s_sems, r_sems, mine_thru, lands, tok = _swap_start([sums[n][li] for n, li in early], "swap_start_early", after)

    blocks = dict(
        c_re=_mm_tn_blocks(s5_late["s_re"], s5_late["dy1_b"], S5_BP, S5_BH, "wgrad_s5_c_re", after=tok),
        c_im=_mm_tn_blocks(s5_late["s_im"], s5_late["dy1_b"], S5_BP, S5_BH, "wgrad_s5_c_im", after=tok),
        b_re=_mm_tn_blocks(s5_late["u"], s5_late["lam_re"], S5_BH, S5_BP, "wgrad_s5_b_re", after=tok),
        b_im=_mm_tn_blocks(s5_late["u"], s5_late["lam_im"], S5_BH, S5_BP, "wgrad_s5_b_im", after=tok))
    d_are, d_aim, d_ldt, d_btre, d_btim = _s5_params_bwd(
        a_re, a_im, log_dt, bt_re, bt_im, s5_late["dabar"][0].reshape(S5_G, S5_P), s5_late["dabar"][1].reshape(S5_G, S5_P),
        _unblock_b(blocks["b_re"]), _unblock_b(blocks["b_im"]))
    small_g.update(ssm_a_re=d_are, ssm_a_im=d_aim, ssm_log_dt=d_ldt, ssm_b_re=d_btre.transpose(1, 2, 0),
                   ssm_b_im=d_btim.transpose(1, 2, 0), ssm_c_re=_unblock_c(blocks["c_re"]), ssm_c_im=-_unblock_c(blocks["c_im"]))

    mine_thru, got = _swap_wait(s_sems, r_sems, mine_thru, lands, "swap_wait_early", blocks["b_im"])
    sib = dict(zip(early, got))
    for (n, li), t in zip(early, mine_thru):
        sums[n][li] = t
    after = got[-1]
    for key in ("0f", "0c"):
        after = collect(key, after)
    sib.update(zip(late, _swap_with_sibling([sums[n][li] for n, li in late], "swap_grad_sums_late")))

    dmod = _mod_bwd(jnp.stack(vs_mix), jnp.stack(vs_ffn), pv)
    small_g.update(ada_b=dmod[:, :6, :], norm1_g=dmod[:, 6, :], norm2_g=dmod[:, 7, :], final_g=vs_fin[1],
                   conv_w=jnp.stack(small_g["conv_w"]), conv_b=jnp.stack(small_g["conv_b"]))

    loss_part = (0.5 / D) * vs_fin[0, 0:1]
    part_shapes = [(1,)] + [tuple(small_g[n].shape) for n in SMALL]
    parts_sum = _allreduce_small(_pack([loss_part] + [small_g[n] for n in SMALL], 16), "reduce_small_grads", sib[late[-1]])
    summed = _unpack(parts_sum, part_shapes)
    loss = summed[0][0]
    gsum = dict(zip(SMALL, summed[1:]))
    dmod_all = _allgather_small(_pack([small_g["ada_b"]]), "gather_dmod", parts_sum)
    dmod_all = dmod_all.reshape(N_DEV, DEPTH, 6 * D)
    dmod_cols = lax.dynamic_slice_in_dim(dmod_all, chip * cols, cols, axis=2).transpose(1, 0, 2)
    g_ada_w = _ada_bwd(c16, jnp.pad(dmod_cols, ((0, 0), (0, 16 - N_DEV), (0, 0))))

    res = {}
    shp = ada_w.shape
    two = lambda t: t.reshape(shp[0] * shp[1], shp[2])
    res["ada_w"] = [t.reshape(shp) for t in _adamw(two(ada_w), [two(g_ada_w)], two(m_ada_w), two(v_ada_w), "adamw_ada_w")]

    n_shared = sum(w[n].size for n in SMALL if n not in SMALL_SHARDED)
    g_shards = [lax.dynamic_slice_in_dim(gsum[n], chip * w[n].shape[-1], w[n].shape[-1], axis=gsum[n].ndim - 1)
                for n in SMALL_SHARDED]
    g_flat = _pack_flat([parts_sum.reshape(-1)[1:1 + n_shared]] + g_shards)
    small_shapes = [tuple(w[n].shape) for n in SMALL]
    packed = _adamw(_pack_flat([w[n] for n in SMALL]), [g_flat], _pack_flat([m[n] for n in SMALL]), _pack_flat([v[n] for n in SMALL]),
                    "adamw_small")
    unpacked = [_unpack(t, small_shapes) for t in packed]
    for k, n in enumerate(SMALL):
        res[n] = [unpacked[0][k], unpacked[1][k], unpacked[2][k], unpacked[3][k]]

    for n in BIG:
        res[n] = _adamw_layers(w[n], sums[n], [sib[(n, li)] for li in range(w[n].shape[0])], m[n], v[n], f"adamw_{n}")

    outs = [loss, grad_x]
    for part in range(4):
        outs += [res[n][part] for n in WEIGHTS]
    return tuple(outs)
```

```python
import functools

import jax
import jax.numpy as jnp
from jax import lax
from jax.experimental import pallas as pl
from jax.experimental.pallas import tpu as pltpu

F32 = jnp.float32
BF16 = jnp.bfloat16
D = 1024
EPS = 1e-6
DEPTH = 4
MIXER_OF_LAYER = (0, 1, 2, 0)
S5_G, S5_H, S5_P = 64, 16, 64
S5_NB = 4
S5_BH = S5_H * 16
S5_BP = S5_P * 16
NSTATE = S5_G * S5_P
SG_HEADS, SG_CHUNK = 8, 128
ADAM_LR, ADAM_B1, ADAM_B2, ADAM_EPS, ADAM_WD, ADAM_STEP = 0.001, 0.9, 0.999, 1e-08, 0.01, 10
N_DEV = 8
MESH = pl.DeviceIdType.MESH
LANES = 1024
R_SH1, R_SC1, R_G1, R_SH2, R_SC2, R_G2, R_N1, R_N2 = range(8)


def _dot(a, b):
    return jnp.dot(a, b, preferred_element_type=F32)


def _dot_nt(a, b):
    return lax.dot_general(a, b, (((1,), (1,)), ((), ())), preferred_element_type=F32)


def _dot_tn(a, b):
    return lax.dot_general(a, b, (((0,), (0,)), ((), ())), preferred_element_type=F32)


def _bf(x):
    return x.astype(BF16)


def _sum0(x):
    return jnp.sum(x, axis=0, keepdims=True)


def _params(n_axes, vmem_mb=48):
    return pltpu.CompilerParams(dimension_semantics=("arbitrary",) * n_axes, vmem_limit_bytes=vmem_mb << 20)


def _rows(tm, cols, nt=None):
    if nt is None:
        return pl.BlockSpec((tm, cols), lambda i: (i, 0))
    return pl.BlockSpec((tm, cols), lambda i: (nt - 1 - i, 0))


def _whole(shape):
    nd = len(shape)
    return pl.BlockSpec(shape, lambda *_: (0,) * nd)


def _layer_w(r, c, layer):
    return pl.BlockSpec((None, r, c), lambda *_: (layer, 0, 0), pipeline_mode=pl.Buffered(1))


def _const_w(shape):
    nd = len(shape)
    return pl.BlockSpec(shape, lambda *_: (0,) * nd, pipeline_mode=pl.Buffered(1))


def _call_after(after, body, n_in, *, in_specs, **kw):
    if after is None:
        return pl.pallas_call(body, in_specs=in_specs, **kw), ()

    def body_after(*refs):
        return body(*refs[:n_in], *refs[n_in + 1:])

    return pl.pallas_call(body_after, in_specs=list(in_specs) + [pl.BlockSpec(memory_space=pl.ANY)], **kw), (after,)


def _norm_mod(x, ng, sc, sh):
    r = lax.rsqrt(jnp.mean(x * x, axis=-1, keepdims=True) + EPS)
    xn = x * r
    return (xn * ng) * (1.0 + sc) + sh, xn, r


def _norm_mod_bwd(dh, xn, r, ng, sc):
    dxn = dh * (ng * (1.0 + sc))
    return r * (dxn - xn * jnp.mean(dxn * xn, axis=-1, keepdims=True))


def _shift_down(z, prev8, k):
    row = lax.broadcasted_iota(jnp.int32, z.shape, 0)
    if k == 1:
        return jnp.where(row >= 1, pltpu.roll(z, 1, 0), prev8[7:8])
    return jnp.where(row >= 2, pltpu.roll(z, 2, 0), jnp.where(row == 0, prev8[6:7], prev8[7:8]))


def _shift_up(z, next8, k):
    n = z.shape[0]
    row = lax.broadcasted_iota(jnp.int32, z.shape, 0)
    if k == 1:
        return jnp.where(row <= n - 2, pltpu.roll(z, n - 1, 0), next8[0:1])
    return jnp.where(row <= n - 3, pltpu.roll(z, n - 2, 0), jnp.where(row == n - 2, next8[0:1], next8[1:2]))


def _place():
    x, y, c = lax.axis_index("x"), lax.axis_index("y"), lax.axis_index("c")
    chips = [(1 - x, y), (x, 1 - y), (1 - x, 1 - y)]
    return x, y, c, chips


def _allgather_small(x_shard, name, after=None):
    m_per, n = x_shard.shape

    def body(x_ref, out_ref, send_sems, recv_sems, local_sem):
        x, y, c, chips = _place()
        me, sibling = (x, y, c), (x, y, 1 - c)

        def rows(px, py, pc):
            return out_ref.at[pl.ds((4 * px + 2 * py + pc) * m_per, m_per), :]

        def copy(k, block, to, src=None):
            return pltpu.make_async_remote_copy(
                src_ref=rows(*block) if src is None else src, dst_ref=rows(*block),
                send_sem=send_sems.at[k], recv_sem=recv_sems.at[k], device_id=to, device_id_type=MESH)

        mine = pltpu.make_async_copy(x_ref, rows(*me), local_sem)
        mine.start()
        first = [copy(0, me, sibling, src=x_ref)]
        first += [copy(1 + j, me, (*chip, c), src=x_ref) for j, chip in enumerate(chips)]
        for cp in first:
            cp.start()
        passed = [copy(4 + j, (*chip, c), sibling) for j, chip in enumerate(chips)]
        for j, chip in enumerate(chips):
            copy(1 + j, (*chip, c), me).wait_recv()
            passed[j].start()
        copy(0, sibling, me).wait_recv()
        for j, chip in enumerate(chips):
            copy(4 + j, (*chip, 1 - c), me).wait_recv()
        for cp in first + passed:
            cp.wait_send()
        mine.wait()

    call, tail = _call_after(
        after, body, 1, name=name, out_shape=jax.ShapeDtypeStruct((N_DEV * m_per, n), F32),
        in_specs=[pl.BlockSpec(memory_space=pltpu.VMEM)], out_specs=pl.BlockSpec(memory_space=pltpu.VMEM),
        scratch_shapes=[pltpu.SemaphoreType.DMA((7,)), pltpu.SemaphoreType.DMA((7,)), pltpu.SemaphoreType.DMA],
        compiler_params=pltpu.CompilerParams(vmem_limit_bytes=48 << 20),
    )
    return call(x_shard, *tail)


def _allreduce_small(x_part, name, after=None):
    m, n = x_part.shape
    h = m // 2

    def body(x_ref, out_ref, sib_buf, slots, send_sems, recv_sems):
        x, y, c, chips = _place()
        k_me = 2 * x + y
        sibling = (x, y, 1 - c)
        mine = pl.ds(pl.multiple_of(c * h, 8), h)

        def copy(k, src, dst, to):
            return pltpu.make_async_remote_copy(src_ref=src, dst_ref=dst, send_sem=send_sems.at[k], recv_sem=recv_sems.at[k],
                                                device_id=to, device_id_type=MESH)

        swap = copy(0, x_ref, sib_buf, sibling)
        swap.start()
        swap.wait()
        slots[pl.ds(k_me, 1)] = (x_ref[mine, :] + sib_buf[mine, :])[None]
        my_slot = slots.at[pl.ds(k_me, 1)]
        sends = [copy(1 + j, my_slot, my_slot, (*chip, c)) for j, chip in enumerate(chips)]
        for cp in sends:
            cp.start()
        for j, chip in enumerate(chips):
            their_slot = slots.at[pl.ds(2 * chip[0] + chip[1], 1)]
            copy(1 + j, their_slot, their_slot, (x, y, c)).wait_recv()
        for cp in sends:
            cp.wait_send()
        out_ref[mine, :] = ((slots[0] + slots[1]) + slots[2]) + slots[3]
        give = copy(4, out_ref.at[mine, :], out_ref.at[mine, :], sibling)
        give.start()
        give.wait_send()
        theirs = pl.ds(pl.multiple_of((1 - c) * h, 8), h)
        copy(4, out_ref.at[theirs, :], out_ref.at[theirs, :], (x, y, c)).wait_recv()

    call, tail = _call_after(
        after, body, 1, name=name, out_shape=jax.ShapeDtypeStruct((m, n), F32),
        in_specs=[pl.BlockSpec(memory_space=pltpu.VMEM)], out_specs=pl.BlockSpec(memory_space=pltpu.VMEM),
        scratch_shapes=[pltpu.VMEM((m, n), F32), pltpu.VMEM((4, h, n), F32),
                        pltpu.SemaphoreType.DMA((5,)), pltpu.SemaphoreType.DMA((5,))],
        compiler_params=pltpu.CompilerParams(vmem_limit_bytes=48 << 20),
    )
    return call(x_part, *tail)


def _shard_region(ref, full_shape, axis, chip_k, half=None):
    _, r, c = full_shape
    if axis == 1:
        rs = r // 4
        if half is None:
            return ref.at[:, pl.ds(pl.multiple_of(chip_k * rs, 128), rs), :]
        return ref.at[:, pl.ds(pl.multiple_of(chip_k * rs + half * (rs // 2), 128), rs // 2), :]
    cs = c // 4
    if half is None:
        return ref.at[:, :, pl.ds(pl.multiple_of(chip_k * cs, 128), cs)]
    return ref.at[:, pl.ds(pl.multiple_of(half * (r // 2), 128), r // 2), pl.ds(pl.multiple_of(chip_k * cs, 128), cs)]


HBM_SPEC = pl.BlockSpec(memory_space=pltpu.HBM)
SEM_SPEC = pl.BlockSpec(memory_space=pltpu.SEMAPHORE)
ANY_SPEC = pl.BlockSpec(memory_space=pl.ANY)
SPLIT_COPY_PARAMS = pltpu.CompilerParams(has_side_effects=pltpu.SideEffectType.DATAFLOW_SIDE_EFFECTING)


def _in_hbm(arrs):
    return [pltpu.with_memory_space_constraint(a, pltpu.HBM) for a in arrs]


def _cast_place(w_stack, li, axis, chip, name):
    _, r, c = w_stack.shape
    full = (1, 4 * r, c) if axis == 1 else (1, r, 4 * c)
    tr = min(r, 256)
    if axis == 1:
        out_spec = pl.BlockSpec((None, tr, c), lambda i, k: (0, k[0] * (r // tr) + i, 0))
    else:
        out_spec = pl.BlockSpec((None, tr, c), lambda i, k: (0, i, k[0]))

    def body(k_ref, w_ref, o_ref):
        o_ref[...] = _bf(w_ref[...])

    return pl.pallas_call(
        body, name=name,
        grid_spec=pltpu.PrefetchScalarGridSpec(
            num_scalar_prefetch=1, grid=(r // tr,),
            in_specs=[pl.BlockSpec((None, tr, c), lambda i, k: (li, i, 0))], out_specs=out_spec),
        out_shape=jax.ShapeDtypeStruct(full, BF16),
        compiler_params=_params(1),
    )(chip, w_stack)


def _gather_start(lands, axes, name, after):
    n_arr = len(lands)
    fulls = [tuple(l.shape) for l in lands]

    def body(*refs):
        land = refs[:n_arr]
        send_sems, recv_sems = refs[n_arr + 1:n_arr + 3]
        token = refs[-1]
        x, y, c, chips = _place()
        k_me = 2 * x + y
        for a in range(n_arr):
            mine = _shard_region(land[a], fulls[a], axes[a], k_me, c)
            for j, chip in enumerate(chips):
                pltpu.make_async_remote_copy(
                    src_ref=mine, dst_ref=mine, send_sem=send_sems.at[a * 3 + j], recv_sem=recv_sems.at[a * 3 + j],
                    device_id=(*chip, c), device_id_type=MESH).start()
        token[...] = jnp.zeros_like(token)

    res = pl.pallas_call(
        body, name=name,
        out_shape=(pltpu.SemaphoreType.DMA((3 * n_arr,)), pltpu.SemaphoreType.DMA((3 * n_arr,)),
                   *[pltpu.HBM(f, BF16) for f in fulls], jax.ShapeDtypeStruct((8, 128), F32)),
        in_specs=[HBM_SPEC] * n_arr + [ANY_SPEC],
        out_specs=(SEM_SPEC, SEM_SPEC, *[HBM_SPEC] * n_arr, pl.BlockSpec(memory_space=pltpu.VMEM)),
        input_output_aliases={a: 2 + a for a in range(n_arr)},
        compiler_params=SPLIT_COPY_PARAMS,
    )(*_in_hbm(lands), after)
    return res[0], res[1], list(res[2:2 + n_arr]), res[-1]


def _gather_wait(send_sems, recv_sems, lands, axes, name, after):
    n_arr = len(lands)
    fulls = [tuple(l.shape) for l in lands]

    def body(*refs):
        land = refs[:n_arr]
        s_sems, r_sems = refs[n_arr:n_arr + 2]
        x, y, c, chips = _place()
        for a in range(n_arr):
            for j, chip in enumerate(chips):
                k_j = 2 * chip[0] + chip[1]
                got = _shard_region(land[a], fulls[a], axes[a], k_j, c)
                cp = pltpu.make_async_remote_copy(
                    src_ref=got, dst_ref=got, send_sem=s_sems.at[a * 3 + j], recv_sem=r_sems.at[a * 3 + j],
                    device_id=(x, y, c), device_id_type=MESH)
                cp.wait_send()
                cp.wait_recv()

    res = pl.pallas_call(
        body, name=name,
        out_shape=tuple(pltpu.HBM(f, BF16) for f in fulls),
        in_specs=[HBM_SPEC] * n_arr + [SEM_SPEC, SEM_SPEC, ANY_SPEC],
        out_specs=tuple([HBM_SPEC] * n_arr),
        input_output_aliases={a: a for a in range(n_arr)},
        compiler_params=SPLIT_COPY_PARAMS,
    )(*lands, send_sems, recv_sems, after)
    return list(res)


def _gather_share(lands, axes, name):
    n_arr = len(lands)
    fulls = [tuple(l.shape) for l in lands]

    def body(*refs):
        land_in, land = refs[:n_arr], refs[n_arr:2 * n_arr]
        send_sems, recv_sems = refs[2 * n_arr:]
        x, y, c, chips = _place()
        copies = []
        for a in range(n_arr):
            for j, chip in enumerate(chips):
                k_j = 2 * chip[0] + chip[1]
                cp = pltpu.make_async_remote_copy(
                    src_ref=_shard_region(land_in[a], fulls[a], axes[a], k_j, c),
                    dst_ref=_shard_region(land[a], fulls[a], axes[a], k_j, c),
                    send_sem=send_sems.at[a * 3 + j], recv_sem=recv_sems.at[a * 3 + j],
                    device_id=(x, y, 1 - c), device_id_type=MESH)
                cp.start()
                copies.append(cp)
        for cp in copies:
            cp.wait()

    return pl.pallas_call(
        body, name=name, out_shape=[jax.ShapeDtypeStruct(f, BF16) for f in fulls],
        in_specs=[ANY_SPEC] * n_arr, out_specs=[ANY_SPEC] * n_arr,
        input_output_aliases={a: a for a in range(n_arr)},
        scratch_shapes=[pltpu.SemaphoreType.DMA((3 * n_arr,)), pltpu.SemaphoreType.DMA((3 * n_arr,))],
    )(*lands)


def _scatter_shapes(grads, axes):
    out = []
    for g, ax in zip(grads, axes):
        shp = list(g.shape)
        shp[ax] //= 4
        out.append((3,) + tuple(shp[1:]))
    return out


def _scatter_start(grads, axes, name, after):
    n_arr = len(grads)
    shapes = _scatter_shapes(grads, axes)
    lands = [lax.empty(s, BF16) for s in shapes]

    def body(*refs):
        ins, land = refs[:n_arr], refs[n_arr:2 * n_arr]
        send_sems, recv_sems = refs[2 * n_arr + 1:2 * n_arr + 3]
        token = refs[-1]
        x, y, c, chips = _place()
        for a in range(n_arr):
            for j, chip in enumerate(chips):
                k_j = 2 * chip[0] + chip[1]
                pltpu.make_async_remote_copy(
                    src_ref=_shard_region(ins[a], grads[a].shape, axes[a], k_j), dst_ref=land[a].at[pl.ds(j, 1)],
                    send_sem=send_sems.at[a * 3 + j], recv_sem=recv_sems.at[a * 3 + j],
                    device_id=(*chip, c), device_id_type=MESH).start()
        token[...] = jnp.zeros_like(token)

    res = pl.pallas_call(
        body, name=name,
        out_shape=(pltpu.SemaphoreType.DMA((3 * n_arr,)), pltpu.SemaphoreType.DMA((3 * n_arr,)),
                   *[pltpu.HBM(g.shape, BF16) for g in grads], *[pltpu.HBM(s, BF16) for s in shapes],
                   jax.ShapeDtypeStruct((8, 128), F32)),
        in_specs=[HBM_SPEC] * (2 * n_arr) + [ANY_SPEC],
        out_specs=(SEM_SPEC, SEM_SPEC, *[HBM_SPEC] * (2 * n_arr), pl.BlockSpec(memory_space=pltpu.VMEM)),
        input_output_aliases={a: 2 + a for a in range(2 * n_arr)},
        compiler_params=SPLIT_COPY_PARAMS,
    )(*_in_hbm(grads), *_in_hbm(lands), after)
    return res[0], res[1], list(res[2:2 + n_arr]), list(res[2 + n_arr:2 + 2 * n_arr]), res[-1]


def _scatter_wait(send_sems, recv_sems, grads, lands, axes, name, after):
    n_arr = len(grads)

    def body(*refs):
        ins, land = refs[:n_arr], refs[n_arr:2 * n_arr]
        s_sems, r_sems = refs[2 * n_arr:2 * n_arr + 2]
        x, y, c, chips = _place()
        for a in range(n_arr):
            for j, chip in enumerate(chips):
                k_j = 2 * chip[0] + chip[1]
                cp = pltpu.make_async_remote_copy(
                    src_ref=_shard_region(ins[a], grads[a].shape, axes[a], k_j), dst_ref=land[a].at[pl.ds(j, 1)],
                    send_sem=s_sems.at[a * 3 + j], recv_sem=r_sems.at[a * 3 + j],
                    device_id=(x, y, c), device_id_type=MESH)
                cp.wait_send()
                cp.wait_recv()

    res = pl.pallas_call(
        body, name=name,
        out_shape=(*[pltpu.HBM(g.shape, BF16) for g in grads], *[pltpu.HBM(l.shape, BF16) for l in lands]),
        in_specs=[HBM_SPEC] * (2 * n_arr) + [SEM_SPEC, SEM_SPEC, ANY_SPEC],
        out_specs=tuple([HBM_SPEC] * (2 * n_arr)),
        input_output_aliases={a: a for a in range(2 * n_arr)},
        compiler_params=SPLIT_COPY_PARAMS,
    )(*grads, *lands, send_sems, recv_sems, after)
    return list(res[:n_arr]), list(res[n_arr:])


def _swap_start(arrs, name, after):
    n_arr = len(arrs)
    lands = [lax.empty(a.shape, a.dtype) for a in arrs]

    def body(*refs):
        ins, land = refs[:n_arr], refs[n_arr:2 * n_arr]
        send_sems, recv_sems = refs[2 * n_arr + 1:2 * n_arr + 3]
        token = refs[-1]
        x, y, c, _ = _place()
        for a in range(n_arr):
            pltpu.make_async_remote_copy(
                src_ref=ins[a], dst_ref=land[a], send_sem=send_sems.at[a], recv_sem=recv_sems.at[a],
                device_id=(x, y, 1 - c), device_id_type=MESH).start()
        token[...] = jnp.zeros_like(token)

    res = pl.pallas_call(
        body, name=name,
        out_shape=(pltpu.SemaphoreType.DMA((n_arr,)), pltpu.SemaphoreType.DMA((n_arr,)),
                   *[pltpu.HBM(a.shape, a.dtype) for a in arrs], *[pltpu.HBM(a.shape, a.dtype) for a in arrs],
                   jax.ShapeDtypeStruct((8, 128), F32)),
        in_specs=[HBM_SPEC] * (2 * n_arr) + [ANY_SPEC],
        out_specs=(SEM_SPEC, SEM_SPEC, *[HBM_SPEC] * (2 * n_arr), pl.BlockSpec(memory_space=pltpu.VMEM)),
        input_output_aliases={a: 2 + a for a in range(2 * n_arr)},
        compiler_params=SPLIT_COPY_PARAMS,
    )(*_in_hbm(arrs), *_in_hbm(lands), after)
    return res[0], res[1], list(res[2:2 + n_arr]), list(res[2 + n_arr:2 + 2 * n_arr]), res[-1]


def _swap_wait(send_sems, recv_sems, arrs, lands, name, after):
    n_arr = len(arrs)

    def body(*refs):
        ins, land = refs[:n_arr], refs[n_arr:2 * n_arr]
        s_sems, r_sems = refs[2 * n_arr:2 * n_arr + 2]
        x, y, c, _ = _place()
        for a in range(n_arr):
            cp = pltpu.make_async_remote_copy(
                src_ref=ins[a], dst_ref=land[a], send_sem=s_sems.at[a], recv_sem=r_sems.at[a],
                device_id=(x, y, c), device_id_type=MESH)
            cp.wait_send()
            cp.wait_recv()

    res = pl.pallas_call(
        body, name=name,
        out_shape=(*[pltpu.HBM(a.shape, a.dtype) for a in arrs], *[pltpu.HBM(a.shape, a.dtype) for a in arrs]),
        in_specs=[HBM_SPEC] * (2 * n_arr) + [SEM_SPEC, SEM_SPEC, ANY_SPEC],
        out_specs=tuple([HBM_SPEC] * (2 * n_arr)),
        input_output_aliases={a: a for a in range(2 * n_arr)},
        compiler_params=SPLIT_COPY_PARAMS,
    )(*arrs, *lands, send_sems, recv_sems, after)
    return list(res[:n_arr]), list(res[n_arr:])


def _swap_with_sibling(arrs, name):
    n_arr = len(arrs)

    def body(*refs):
        ins, outs = refs[:n_arr], refs[n_arr:2 * n_arr]
        send_sems, recv_sems = refs[2 * n_arr:]
        x, y, c, _ = _place()
        copies = []
        for a in range(n_arr):
            cp = pltpu.make_async_remote_copy(
                src_ref=ins[a], dst_ref=outs[a], send_sem=send_sems.at[a], recv_sem=recv_sems.at[a],
                device_id=(x, y, 1 - c), device_id_type=MESH)
            cp.start()
            copies.append(cp)
        for cp in copies:
            cp.wait()

    any_spec = pl.BlockSpec(memory_space=pl.ANY)
    return pl.pallas_call(
        body, name=name, out_shape=[jax.ShapeDtypeStruct(a.shape, a.dtype) for a in arrs],
        in_specs=[any_spec] * n_arr, out_specs=[any_spec] * n_arr,
        scratch_shapes=[pltpu.SemaphoreType.DMA((n_arr,)), pltpu.SemaphoreType.DMA((n_arr,))],
    )(*arrs)


def _mm_tn(a, b, name, out_dtype=BF16):
    L, m = a.shape
    n = b.shape[1]
    bm, bn, bk = min(m, 1024), min(n, 1024), min(L, 2048)
    nk = L // bk

    def body(a_ref, b_ref, o_ref, acc):
        k = pl.program_id(2)

        @pl.when(k == 0)
        def _():
            acc[...] = jnp.zeros_like(acc)

        acc[...] += _dot_tn(_bf(a_ref[...]), _bf(b_ref[...]))

        @pl.when(k == nk - 1)
        def _():
            o_ref[...] = acc[...].astype(out_dtype)

    return pl.pallas_call(
        body, name=name, grid=(m // bm, n // bn, nk),
        in_specs=[pl.BlockSpec((bk, bm), lambda i, j, k: (k, i)), pl.BlockSpec((bk, bn), lambda i, j, k: (k, j))],
        out_specs=pl.BlockSpec((bm, bn), lambda i, j, k: (i, j)),
        out_shape=jax.ShapeDtypeStruct((m, n), out_dtype),
        scratch_shapes=[pltpu.VMEM((bm, bn), F32)],
        compiler_params=_params(3),
    )(a, b)


def _mm_tn_blocks(a, b, wa, wb, name, after=None):
    L = a.shape[0]
    nb = a.shape[1] // wa
    bk = min(L, 1024)
    nk = L // bk

    def body(a_ref, b_ref, o_ref):
        @pl.when(pl.program_id(1) == 0)
        def _():
            o_ref[...] = jnp.zeros_like(o_ref)

        o_ref[...] += _dot_tn(_bf(a_ref[...]), _bf(b_ref[...]))

    call, tail = _call_after(
        after, body, 2, name=name, grid=(nb, nk),
        in_specs=[pl.BlockSpec((bk, wa), lambda j, k: (k, j)), pl.BlockSpec((bk, wb), lambda j, k: (k, j))],
        out_specs=pl.BlockSpec((None, wa, wb), lambda j, k: (j, 0, 0)),
        out_shape=jax.ShapeDtypeStruct((nb, wa, wb), F32),
        compiler_params=_params(2),
    )
    return call(a, b, *tail)


def _sum_parts(parts, own, axis, chip, name):
    _, r, c = parts.shape
    tr = min(r, 256)
    if axis == 1:
        own_spec = pl.BlockSpec((None, tr, c), lambda i, k: (0, k[0] * (r // tr) + i, 0))
    else:
        own_spec = pl.BlockSpec((None, tr, c), lambda i, k: (0, i, k[0]))

    def body(k_ref, p_ref, g_ref, o_ref):
        p = p_ref[...].astype(F32)
        o_ref[...] = ((p[0] + p[1]) + p[2]) + g_ref[...].astype(F32)

    return pl.pallas_call(
        body, name=name,
        grid_spec=pltpu.PrefetchScalarGridSpec(
            num_scalar_prefetch=1, grid=(r // tr,),
            in_specs=[pl.BlockSpec((3, tr, c), lambda i, k: (0, i, 0)), own_spec],
            out_specs=pl.BlockSpec((tr, c), lambda i, k: (i, 0))),
        out_shape=jax.ShapeDtypeStruct((r, c), F32),
        compiler_params=_params(1),
    )(chip, parts, own)


def _adamw(w, g_parts, m, v, name):
    n_g = len(g_parts)
    if w.ndim == 2:
        r, c = w.shape
        tr = r
        for cand in (512, 256, 128, 64, 32, 16, 8):
            if r % cand == 0 and cand * c * 4 <= (2 << 20):
                tr = cand
                break
        spec = pl.BlockSpec((tr, c), lambda i: (i, 0))
        tiling = dict(grid=(r // tr,), in_specs=[spec] * (3 + n_g), out_specs=[spec] * 4, compiler_params=_params(1))
    else:
        tiling = dict(compiler_params=pltpu.CompilerParams(vmem_limit_bytes=48 << 20))

    def body(*refs):
        w_ref, g_refs, m_ref, v_ref = refs[0], refs[1:1 + n_g], refs[1 + n_g], refs[2 + n_g]
        g = g_refs[0][...]
        for gr in g_refs[1:]:
            g = g + gr[...]
        _adamw_update(g, w_ref, m_ref, v_ref, *refs[3 + n_g:])

    return pl.pallas_call(body, name=name, out_shape=[jax.ShapeDtypeStruct(w.shape, F32)] * 4, **tiling)(w, *g_parts, m, v)


def _adamw_update(g, w_ref, m_ref, v_ref, g_out, d_out, m_out, v_out):
    m_new = ADAM_B1 * m_ref[...] + (1.0 - ADAM_B1) * g
    v_new = ADAM_B2 * v_ref[...] + (1.0 - ADAM_B2) * (g * g)
    m_hat = m_new * (1.0 / (1.0 - ADAM_B1 ** ADAM_STEP))
    v_hat = v_new * (1.0 / (1.0 - ADAM_B2 ** ADAM_STEP))
    g_out[...] = g
    d_out[...] = -ADAM_LR * (m_hat / (jnp.sqrt(v_hat) + ADAM_EPS) + ADAM_WD * w_ref[...])
    m_out[...] = m_new
    v_out[...] = v_new


def _adamw_many(ws, gs, ms, vs, name):
    n = len(ws)

    def body(*refs):
        for k in range(n):
            _adamw_update(refs[n + k][...], refs[k], refs[2 * n + k], refs[3 * n + k],
                          refs[4 * n + k], refs[5 * n + k], refs[6 * n + k], refs[7 * n + k])

    outs = pl.pallas_call(body, name=name, out_shape=[jax.ShapeDtypeStruct(t.shape, F32) for t in ws] * 4,
                          compiler_params=pltpu.CompilerParams(vmem_limit_bytes=56 << 20))(*ws, *gs, *ms, *vs)
    return [outs[part * n:(part + 1) * n] for part in range(4)]


def _adamw_layers(w, q_mine, q_sib, m, v, name):
    n, r, c = w.shape
    tr = r
    for cand in (512, 256, 128, 64, 32, 16, 8):
        if r % cand == 0 and cand * c * 4 <= (1 << 20):
            tr = cand
            break

    def body(*refs):
        w_ref, qm, qs, m_ref, v_ref = refs[0], refs[1:1 + n], refs[1 + n:1 + 2 * n], refs[1 + 2 * n], refs[2 + 2 * n]
        layer = pl.program_id(0)
        g = qm[0][...] + qs[0][...]
        for k in range(1, n):
            g = jnp.where(layer == k, qm[k][...] + qs[k][...], g)
        _adamw_update(g, w_ref, m_ref, v_ref, *refs[3 + 2 * n:])

    stacked = pl.BlockSpec((None, tr, c), lambda l, i: (l, i, 0))
    per_layer = [pl.BlockSpec((tr, c), lambda l, i, k=k: (jnp.where(l == k, i, 0), 0)) for k in range(n)]
    return pl.pallas_call(
        body, name=name, grid=(n, r // tr),
        in_specs=[stacked] + per_layer + per_layer + [stacked, stacked], out_specs=[stacked] * 4,
        out_shape=[jax.ShapeDtypeStruct(w.shape, F32)] * 4,
        compiler_params=_params(2),
    )(w, *q_mine, *q_sib, m, v)


def _ada_fwd(c16, ada_w, ada_b_cols):
    cols = ada_w.shape[2]

    def body(c_ref, w_ref, b_ref, o_ref):
        cv = c_ref[...]
        ca = _bf(cv * jax.nn.sigmoid(cv))
        o_ref[...] = _dot(ca, _bf(w_ref[...])) + b_ref[...]

    return pl.pallas_call(
        body, name="ada_fwd", grid=(DEPTH,),
        in_specs=[_whole((16, D)), pl.BlockSpec((None, D, cols), lambda i: (i, 0, 0)),
                  pl.BlockSpec((None, 1, cols), lambda i: (i, 0, 0))],
        out_specs=pl.BlockSpec((None, 16, cols), lambda i: (i, 0, 0)),
        out_shape=jax.ShapeDtypeStruct((DEPTH, 16, cols), F32),
        compiler_params=_params(1),
    )(c16, ada_w, ada_b_cols)


def _ada_bwd(c16, dmod16):
    cols = dmod16.shape[2]

    def body(c_ref, d_ref, o_ref):
        cv = c_ref[...]
        ca = _bf(cv * jax.nn.sigmoid(cv))
        o_ref[...] = _dot_tn(ca, _bf(d_ref[...]))

    return pl.pallas_call(
        body, name="ada_bwd", grid=(DEPTH,),
        in_specs=[_whole((16, D)), pl.BlockSpec((None, 16, cols), lambda i: (i, 0, 0))],
        out_specs=pl.BlockSpec((None, D, cols), lambda i: (i, 0, 0)),
        out_shape=jax.ShapeDtypeStruct((DEPTH, D, cols), F32),
        compiler_params=_params(1),
    )(c16, dmod16)


def _mod_bwd(vs_mix, vs_ffn, pv):
    def body(m_ref, f_ref, pv_ref, o_ref):
        for i in range(DEPTH):
            vm, vf, p = m_ref[i], f_ref[i], pv_ref[i]
            o_ref[i] = jnp.concatenate([
                vm[2:3], vm[1:2] * p[R_N1:R_N1 + 1], vm[0:1],
                vf[2:3], vf[1:2] * p[R_N2:R_N2 + 1], vf[0:1],
                vm[1:2] * (1.0 + p[R_SC1:R_SC1 + 1]), vf[1:2] * (1.0 + p[R_SC2:R_SC2 + 1])], axis=0)

    return pl.pallas_call(body, name="mod_bwd", out_shape=jax.ShapeDtypeStruct((DEPTH, 8, D), F32))(vs_mix, vs_ffn, pv)


def _ffn_fwd(x1, pv, w1, w2, layer, tm, after=None):
    L = x1.shape[0]
    dff = w1.shape[2]

    def body(x1_ref, pv_ref, w1_ref, w2_ref, x2_ref, h2_ref, a_ref, f_ref):
        x1v, p = x1_ref[...], pv_ref[...]
        h2, _, _ = _norm_mod(x1v, p[R_N2:R_N2 + 1], p[R_SC2:R_SC2 + 1], p[R_SH2:R_SH2 + 1])
        hb = _bf(h2)
        h2_ref[...] = hb
        a = _dot(hb, w1_ref[...])
        a_ref[...] = a
        ra = jnp.maximum(a, 0.0)
        f = _dot(_bf(ra * ra), w2_ref[...])
        f_ref[...] = f
        x2_ref[...] = x1v + p[R_G2:R_G2 + 1] * f

    call, tail = _call_after(
        after, body, 4, name=f"ffn_fwd{layer}", grid=(L // tm,),
        in_specs=[_rows(tm, D), pl.BlockSpec((None, 8, D), lambda i: (layer, 0, 0)), _layer_w(D, dff, 0), _layer_w(dff, D, 0)],
        out_specs=[_rows(tm, D), _rows(tm, D), _rows(tm, dff), _rows(tm, D)],
        out_shape=[jax.ShapeDtypeStruct((L, D), F32), jax.ShapeDtypeStruct((L, D), BF16),
                   jax.ShapeDtypeStruct((L, dff), F32), jax.ShapeDtypeStruct((L, D), F32)],
        compiler_params=_params(1, 56),
    )
    return call(x1, pv, w1, w2, *tail)


def _ffn_bwd(dx2, x1, a, f, pv, w1, w2, layer, tm, after=None):
    L = x1.shape[0]
    dff = w1.shape[2]
    extra = [] if after is None else [pl.BlockSpec(memory_space=pl.ANY)]
    extra_args = [] if after is None else [after]

    def body(dx2_ref, x1_ref, a_ref, f_ref, pv_ref, w1_ref, w2_ref, *rest):
        dx1_ref, p_ref, da_ref, df_ref, vs_ref = rest[len(extra):]

        @pl.when(pl.program_id(0) == 0)
        def _():
            vs_ref[...] = jnp.zeros_like(vs_ref)

        dx2v, p = dx2_ref[...], pv_ref[...]
        dfb = _bf(dx2v * p[R_G2:R_G2 + 1])
        df_ref[...] = dfb
        vs_ref[0:1, :] += _sum0(dx2v * f_ref[...])
        dp = _dot_nt(dfb, w2_ref[...])
        ra = jnp.maximum(a_ref[...], 0.0)
        p_ref[...] = _bf(ra * ra)
        dab = _bf(dp * (2.0 * ra))
        da_ref[...] = dab
        dh2 = _dot_nt(dab, w1_ref[...])
        _, xn, r = _norm_mod(x1_ref[...], p[R_N2:R_N2 + 1], p[R_SC2:R_SC2 + 1], p[R_SH2:R_SH2 + 1])
        dx1_ref[...] = dx2v + _norm_mod_bwd(dh2, xn, r, p[R_N2:R_N2 + 1], p[R_SC2:R_SC2 + 1])
        vs_ref[1:2, :] += _sum0(dh2 * xn)
        vs_ref[2:3, :] += _sum0(dh2)

    return pl.pallas_call(
        body, name=f"ffn_bwd{layer}", grid=(L // tm,),
        in_specs=[_rows(tm, D), _rows(tm, D), _rows(tm, dff), _rows(tm, D),
                  pl.BlockSpec((None, 8, D), lambda i: (layer, 0, 0)), _layer_w(D, dff, 0), _layer_w(dff, D, 0)] + extra,
        out_specs=[_rows(tm, D), _rows(tm, dff), _rows(tm, dff), _rows(tm, D), _whole((8, D))],
        out_shape=[jax.ShapeDtypeStruct((L, D), F32), jax.ShapeDtypeStruct((L, dff), BF16),
                   jax.ShapeDtypeStruct((L, dff), BF16), jax.ShapeDtypeStruct((L, D), BF16),
                   jax.ShapeDtypeStruct((8, D), F32)],
        compiler_params=_params(1, 56),
    )(dx2, x1, a, f, pv, w1, w2, *extra_args)


def _conv_fwd(x, pv, w_in, w_out, cw, layer, j, tm, after=None):
    L = x.shape[0]

    def body(x_ref, pv_ref, win_ref, wout_ref, cw_ref, x1_ref, h_ref, bcx_ref, conv_ref, q_ref, y_ref, carry):
        @pl.when(pl.program_id(0) == 0)
        def _():
            carry[...] = jnp.zeros_like(carry)

        xv, p, cwv = x_ref[...], pv_ref[...], cw_ref[...]
        h, _, _ = _norm_mod(xv, p[R_N1:R_N1 + 1], p[R_SC1:R_SC1 + 1], p[R_SH1:R_SH1 + 1])
        hb = _bf(h)
        h_ref[...] = hb
        bcx = _dot(hb, win_ref[...])
        bcx_ref[...] = bcx
        z = bcx[:, D:2 * D] * bcx[:, 2 * D:]
        prev8 = carry[...]
        conv = cwv[0:1] * _shift_down(z, prev8, 2) + cwv[1:2] * _shift_down(z, prev8, 1) + cwv[2:3] * z + cwv[3:4]
        conv_ref[...] = conv
        qb = _bf(bcx[:, :D] * conv)
        q_ref[...] = qb
        y = _dot(qb, wout_ref[...])
        y_ref[...] = y
        x1_ref[...] = xv + p[R_G1:R_G1 + 1] * y
        carry[...] = z[tm - 8:tm]

    call, tail = _call_after(
        after, body, 5, name=f"conv_fwd{layer}", grid=(L // tm,),
        in_specs=[_rows(tm, D), pl.BlockSpec((None, 8, D), lambda i: (layer, 0, 0)), _layer_w(D, 3 * D, 0), _layer_w(D, D, 0),
                  pl.BlockSpec((None, 8, D), lambda i: (j, 0, 0))],
        out_specs=[_rows(tm, D), _rows(tm, D), _rows(tm, 3 * D), _rows(tm, D), _rows(tm, D), _rows(tm, D)],
        out_shape=[jax.ShapeDtypeStruct((L, D), F32), jax.ShapeDtypeStruct((L, D), BF16), jax.ShapeDtypeStruct((L, 3 * D), F32),
                   jax.ShapeDtypeStruct((L, D), F32), jax.ShapeDtypeStruct((L, D), BF16), jax.ShapeDtypeStruct((L, D), F32)],
        scratch_shapes=[pltpu.VMEM((8, D), F32)],
        compiler_params=_params(1, 56),
    )
    return call(x, pv, w_in, w_out, cw, *tail)


def _conv_bwd(dx1, x, y, bcx, conv, pv, w_in, w_out, cw, layer, j, tm, after=None):
    L = x.shape[0]
    nt = L // tm

    def body(dx1_ref, x_ref, y_ref, bcx_ref, conv_ref, halo_ref, pv_ref, win_ref, wout_ref, cw_ref,
             dx_ref, dbcx_ref, dy_ref, vs_ref, carry):
        gi = pl.program_id(0)
        tile = nt - 1 - gi

        @pl.when(gi == 0)
        def _():
            vs_ref[...] = jnp.zeros_like(vs_ref)
            carry[...] = jnp.zeros_like(carry)

        dx1v, p, cwv = dx1_ref[...], pv_ref[...], cw_ref[...]
        dyb = _bf(dx1v * p[R_G1:R_G1 + 1])
        dy_ref[...] = dyb
        vs_ref[0:1, :] += _sum0(dx1v * y_ref[...])
        dq = _dot_nt(dyb, wout_ref[...])
        bcx = bcx_ref[...]
        b, cg, xh = bcx[:, :D], bcx[:, D:2 * D], bcx[:, 2 * D:]
        db = dq * conv_ref[...]
        dc = dq * b
        z = cg * xh
        halo = halo_ref[...]
        zprev = jnp.where(tile > 0, halo[:, D:2 * D] * halo[:, 2 * D:], 0.0)
        vs_ref[3:4, :] += _sum0(dc * _shift_down(z, zprev, 2))
        vs_ref[4:5, :] += _sum0(dc * _shift_down(z, zprev, 1))
        vs_ref[5:6, :] += _sum0(dc * z)
        vs_ref[6:7, :] += _sum0(dc)
        next8 = carry[...]
        dz = cwv[2:3] * dc + cwv[1:2] * _shift_up(dc, next8, 1) + cwv[0:1] * _shift_up(dc, next8, 2)
        dbb, dcgb, dxhb = _bf(db), _bf(dz * xh), _bf(dz * cg)
        dbcx_ref[:, 0:D] = dbb
        dbcx_ref[:, D:2 * D] = dcgb
        dbcx_ref[:, 2 * D:3 * D] = dxhb
        dh = (_dot_nt(dbb, win_ref[:, 0:D]) + _dot_nt(dcgb, win_ref[:, D:2 * D])) + _dot_nt(dxhb, win_ref[:, 2 * D:3 * D])
        _, xn, r = _norm_mod(x_ref[...], p[R_N1:R_N1 + 1], p[R_SC1:R_SC1 + 1], p[R_SH1:R_SH1 + 1])
        dx_ref[...] = dx1v + _norm_mod_bwd(dh, xn, r, p[R_N1:R_N1 + 1], p[R_SC1:R_SC1 + 1])
        vs_ref[1:2, :] += _sum0(dh * xn)
        vs_ref[2:3, :] += _sum0(dh)
        carry[...] = dc[0:8]

    halo_spec = pl.BlockSpec((8, 3 * D), lambda i: (jnp.maximum((nt - 1 - i) * (tm // 8) - 1, 0), 0))
    call, tail = _call_after(
        after, body, 10, name=f"conv_bwd{layer}", grid=(nt,),
        in_specs=[_rows(tm, D, nt), _rows(tm, D, nt), _rows(tm, D, nt), _rows(tm, 3 * D, nt), _rows(tm, D, nt), halo_spec,
                  pl.BlockSpec((None, 8, D), lambda i: (layer, 0, 0)), _layer_w(D, 3 * D, 0), _layer_w(D, D, 0),
                  pl.BlockSpec((None, 8, D), lambda i: (j, 0, 0))],
        out_specs=[_rows(tm, D, nt), _rows(tm, 3 * D, nt), _rows(tm, D, nt), _whole((8, D))],
        out_shape=[jax.ShapeDtypeStruct((L, D), F32), jax.ShapeDtypeStruct((L, 3 * D), BF16),
                   jax.ShapeDtypeStruct((L, D), BF16), jax.ShapeDtypeStruct((8, D), F32)],
        scratch_shapes=[pltpu.VMEM((8, D), F32)],
        compiler_params=_params(1, 56),
    )
    return call(dx1, x, y, bcx, conv, bcx, pv, w_in, w_out, cw, *tail)


def _s5_discretize(a_re, a_im, log_dt, bt_re, bt_im):
    dt = jnp.exp(log_dt)
    mag = jnp.exp(a_re * dt)
    abar_re = mag * jnp.cos(a_im * dt)
    abar_im = mag * jnp.sin(a_im * dt)
    den = a_re * a_re + a_im * a_im
    nr = abar_re - 1.0
    ni = abar_im
    f_re = (nr * a_re + ni * a_im) / den
    f_im = (ni * a_re - nr * a_im) / den
    bbar_re = f_re * bt_re - f_im * bt_im
    bbar_im = f_re * bt_im + f_im * bt_re
    return abar_re, abar_im, bbar_re, bbar_im


def _s5_params_fwd(a_re, a_im, log_dt, bt_re, bt_im):
    def body(ar, ai, ld, br, bi, o_ar, o_ai, o_br, o_bi):
        r = _s5_discretize(ar[...], ai[...], ld[...], br[...], bi[...])
        o_ar[...], o_ai[...], o_br[...], o_bi[...] = r

    gp = jax.ShapeDtypeStruct((S5_G, S5_P), F32)
    hgp = jax.ShapeDtypeStruct((S5_H, S5_G, S5_P), F32)
    return pl.pallas_call(body, name="s5_params_fwd", out_shape=[gp, gp, hgp, hgp])(a_re, a_im, log_dt, bt_re, bt_im)


def _s5_params_bwd(a_re, a_im, log_dt, bt_re, bt_im, d_ar, d_ai, d_br, d_bi):
    def body(ar, ai, ld, br, bi, gar, gai, gbr, gbi, o_ar, o_ai, o_ld, o_br, o_bi):
        _, vjp = jax.vjp(_s5_discretize, ar[...], ai[...], ld[...], br[...], bi[...])
        r = vjp((gar[...], gai[...], gbr[...], gbi[...]))
        o_ar[...], o_ai[...], o_ld[...], o_br[...], o_bi[...] = r

    gp = jax.ShapeDtypeStruct((S5_G, S5_P), F32)
    hgp = jax.ShapeDtypeStruct((S5_H, S5_G, S5_P), F32)
    return pl.pallas_call(body, name="s5_params_bwd", out_shape=[gp, gp, jax.ShapeDtypeStruct((S5_G, 1), F32), hgp, hgp])(
        a_re, a_im, log_dt, bt_re, bt_im, d_ar, d_ai, d_br, d_bi)


def _s5_in_fwd(x, pv, w_in, b_re, b_im, layer, tm, after=None):
    L = x.shape[0]

    def body(x_ref, pv_ref, win_ref, bre_ref, bim_ref, h_ref, u_ref, ore_ref, oim_ref):
        p = pv_ref[...]
        h, _, _ = _norm_mod(x_ref[...], p[R_N1:R_N1 + 1], p[R_SC1:R_SC1 + 1], p[R_SH1:R_SH1 + 1])
        hb = _bf(h)
        h_ref[...] = hb
        u = _dot(hb, win_ref[...])
        u_ref[...] = u
        ub = _bf(u)
        for k in range(S5_NB):
            uk = ub[:, k * S5_BH:(k + 1) * S5_BH]
            ore_ref[:, k * S5_BP:(k + 1) * S5_BP] = _dot(uk, bre_ref[k])
            oim_ref[:, k * S5_BP:(k + 1) * S5_BP] = _dot(uk, bim_ref[k])

    call, tail = _call_after(
        after, body, 5, name="s5_in_fwd", grid=(L // tm,),
        in_specs=[_rows(tm, D), pl.BlockSpec((None, 8, D), lambda i: (layer, 0, 0)), _layer_w(D, D, 0),
                  _const_w((S5_NB, S5_BH, S5_BP)), _const_w((S5_NB, S5_BH, S5_BP))],
        out_specs=[_rows(tm, D), _rows(tm, D), _rows(tm, NSTATE), _rows(tm, NSTATE)],
        out_shape=[jax.ShapeDtypeStruct((L, D), BF16), jax.ShapeDtypeStruct((L, D), F32),
                   jax.ShapeDtypeStruct((L, NSTATE), F32), jax.ShapeDtypeStruct((L, NSTATE), F32)],
        compiler_params=_params(1, 56),
    )
    return call(x, pv, w_in, b_re, b_im, *tail)


def _s5_scan_fwd(bu_re, bu_im, ar, ai, tr):
    L = bu_re.shape[0]
    nl = 1024

    def body(bre_ref, bim_ref, ar_ref, ai_ref, sre_ref, sim_ref, st_re, st_im):
        @pl.when(pl.program_id(1) == 0)
        def _():
            st_re[...] = jnp.zeros_like(st_re)
            st_im[...] = jnp.zeros_like(st_im)

        a_r, a_i = ar_ref[...], ai_ref[...]

        def step(t, carry):
            s_r, s_i = carry
            n_r = a_r * s_r - a_i * s_i + bre_ref[pl.ds(t, 1), :]
            n_i = a_r * s_i + a_i * s_r + bim_ref[pl.ds(t, 1), :]
            sre_ref[pl.ds(t, 1), :] = n_r
            sim_ref[pl.ds(t, 1), :] = n_i
            return n_r, n_i

        s_r, s_i = lax.fori_loop(0, tr, step, (st_re[...], st_im[...]), unroll=8)
        st_re[...] = s_r
        st_im[...] = s_i

    blk = pl.BlockSpec((tr, nl), lambda j, i: (i, j))
    vec = pl.BlockSpec((1, nl), lambda j, i: (0, j))
    return pl.pallas_call(
        body, name="s5_scan_fwd", grid=(NSTATE // nl, L // tr),
        in_specs=[blk, blk, vec, vec], out_specs=[blk, blk],
        out_shape=[jax.ShapeDtypeStruct((L, NSTATE), F32)] * 2,
        scratch_shapes=[pltpu.VMEM((1, nl), F32), pltpu.VMEM((1, nl), F32)],
        compiler_params=_params(2),
    )(bu_re, bu_im, ar, ai)


def _s5_out_fwd(x, u, s_re, s_im, pv, c_re, c_im, dvec, glu_w, glu_b, w_out, layer, tm):
    L = x.shape[0]

    def body(x_ref, u_ref, sre_ref, sim_ref, pv_ref, cre_ref, cim_ref, d_ref, gw_ref, gb_ref, wout_ref,
             x1_ref, y1_ref, zg_ref, y3_ref, y_ref):
        p = pv_ref[...]
        srb, sib = _bf(sre_ref[...]), _bf(sim_ref[...])
        parts = []
        for k in range(S5_NB):
            sl = slice(k * S5_BP, (k + 1) * S5_BP)
            parts.append(_dot(srb[:, sl], cre_ref[k]) - _dot(sib[:, sl], cim_ref[k]))
        y1 = jnp.concatenate(parts, axis=1) + d_ref[...] * u_ref[...]
        y1_ref[...] = y1
        y2 = jax.nn.gelu(y1)
        zg = _dot(_bf(y2), gw_ref[...]) + gb_ref[...]
        zg_ref[...] = zg
        y3b = _bf(y2 * jax.nn.sigmoid(zg))
        y3_ref[...] = y3b
        y = _dot(y3b, wout_ref[...])
        y_ref[...] = y
        x1_ref[...] = x_ref[...] + p[R_G1:R_G1 + 1] * y

    return pl.pallas_call(
        body, name="s5_out_fwd", grid=(L // tm,),
        in_specs=[_rows(tm, D), _rows(tm, D), _rows(tm, NSTATE), _rows(tm, NSTATE),
                  pl.BlockSpec((None, 8, D), lambda i: (layer, 0, 0)),
                  _const_w((S5_NB, S5_BP, S5_BH)), _const_w((S5_NB, S5_BP, S5_BH)), _whole((1, D)),
                  _layer_w(D, D, 0), _whole((1, D)), _layer_w(D, D, 0)],
        out_specs=[_rows(tm, D)] * 5,
        out_shape=[jax.ShapeDtypeStruct((L, D), F32), jax.ShapeDtypeStruct((L, D), F32), jax.ShapeDtypeStruct((L, D), F32),
                   jax.ShapeDtypeStruct((L, D), BF16), jax.ShapeDtypeStruct((L, D), F32)],
        compiler_params=_params(1, 56),
    )(x, u, s_re, s_im, pv, c_re, c_im, dvec, glu_w, glu_b, w_out)


def _s5_out_bwd(dx1, y, y1, zg, u, pv, c_re, c_im, dvec, glu_w, w_out, layer, tm, after=None):
    L = dx1.shape[0]

    def body(dx1_ref, y_ref, y1_ref, zg_ref, u_ref, pv_ref, cre_ref, cim_ref, d_ref, gw_ref, wout_ref,
             dy_ref, y2_ref, dzg_ref, dy1_ref, dus_ref, gre_ref, gim_ref, vs_ref):
        @pl.when(pl.program_id(0) == 0)
        def _():
            vs_ref[...] = jnp.zeros_like(vs_ref)

        dx1v, p = dx1_ref[...], pv_ref[...]
        dyb = _bf(dx1v * p[R_G1:R_G1 + 1])
        dy_ref[...] = dyb
        vs_ref[0:1, :] += _sum0(dx1v * y_ref[...])
        dy3 = _dot_nt(dyb, wout_ref[...])
        y2, gelu_vjp = jax.vjp(jax.nn.gelu, y1_ref[...])
        y2_ref[...] = _bf(y2)
        gate = jax.nn.sigmoid(zg_ref[...])
        dzg = dy3 * y2 * gate * (1.0 - gate)
        dzgb = _bf(dzg)
        dzg_ref[...] = dzgb
        vs_ref[1:2, :] += _sum0(dzg)
        dy2 = dy3 * gate + _dot_nt(dzgb, gw_ref[...])
        dy1 = gelu_vjp(dy2)[0]
        vs_ref[2:3, :] += _sum0(dy1 * u_ref[...])
        dus_ref[...] = dy1 * d_ref[...]
        dy1b = _bf(dy1)
        dy1_ref[...] = dy1b
        for k in range(S5_NB):
            dk = dy1b[:, k * S5_BH:(k + 1) * S5_BH]
            gre_ref[:, k * S5_BP:(k + 1) * S5_BP] = _dot_nt(dk, cre_ref[k])
            gim_ref[:, k * S5_BP:(k + 1) * S5_BP] = -_dot_nt(dk, cim_ref[k])

    call, tail = _call_after(
        after, body, 11, name="s5_out_bwd", grid=(L // tm,),
        in_specs=[_rows(tm, D)] * 5 + [pl.BlockSpec((None, 8, D), lambda i: (layer, 0, 0)),
                  _const_w((S5_NB, S5_BP, S5_BH)), _const_w((S5_NB, S5_BP, S5_BH)), _whole((1, D)),
                  _layer_w(D, D, 0), _layer_w(D, D, 0)],
        out_specs=[_rows(tm, D)] * 5 + [_rows(tm, NSTATE), _rows(tm, NSTATE), _whole((8, D))],
        out_shape=[jax.ShapeDtypeStruct((L, D), BF16)] * 4 + [jax.ShapeDtypeStruct((L, D), F32),
                   jax.ShapeDtypeStruct((L, NSTATE), F32), jax.ShapeDtypeStruct((L, NSTATE), F32),
                   jax.ShapeDtypeStruct((8, D), F32)],
        compiler_params=_params(1, 56),
    )
    return call(dx1, y, y1, zg, u, pv, c_re, c_im, dvec, glu_w, w_out, *tail)


def _s5_scan_bwd(g_re, g_im, s_re, s_im, ar, ai, tr):
    L = g_re.shape[0]
    nl = 1024
    nt = L // tr

    def body(gre_ref, gim_ref, sre_ref, sim_ref, hre_ref, him_ref, ar_ref, ai_ref, lre_ref, lim_ref, da_ref, st_re, st_im):
        gi = pl.program_id(1)
        tile = nt - 1 - gi

        @pl.when(gi == 0)
        def _():
            st_re[...] = jnp.zeros_like(st_re)
            st_im[...] = jnp.zeros_like(st_im)
            da_ref[...] = jnp.zeros_like(da_ref)

        a_r, a_i = ar_ref[...], ai_ref[...]

        def step(k, carry):
            l_r, l_i = carry
            t = tr - 1 - k
            n_r = gre_ref[pl.ds(t, 1), :] + a_r * l_r + a_i * l_i
            n_i = gim_ref[pl.ds(t, 1), :] - a_i * l_r + a_r * l_i
            lre_ref[pl.ds(t, 1), :] = n_r
            lim_ref[pl.ds(t, 1), :] = n_i
            return n_r, n_i

        l_r, l_i = lax.fori_loop(0, tr, step, (st_re[...], st_im[...]), unroll=8)
        st_re[...] = l_r
        st_im[...] = l_i
        lam_r, lam_i = lre_ref[...], lim_ref[...]
        p_r = jnp.where(tile > 0, hre_ref[...], 0.0)
        p_i = jnp.where(tile > 0, him_ref[...], 0.0)
        sp_r = _shift_down(sre_ref[...], p_r, 1)
        sp_i = _shift_down(sim_ref[...], p_i, 1)
        da_ref[0:1, :] += _sum0(lam_r * sp_r + lam_i * sp_i)
        da_ref[1:2, :] += _sum0(lam_i * sp_r - lam_r * sp_i)

    blk = pl.BlockSpec((tr, nl), lambda j, i: (nt - 1 - i, j))
    halo = pl.BlockSpec((8, nl), lambda j, i: (jnp.maximum((nt - 1 - i) * (tr // 8) - 1, 0), j))
    vec = pl.BlockSpec((1, nl), lambda j, i: (0, j))
    return pl.pallas_call(
        body, name="s5_scan_bwd", grid=(NSTATE // nl, nt),
        in_specs=[blk, blk, blk, blk, halo, halo, vec, vec],
        out_specs=[blk, blk, pl.BlockSpec((8, nl), lambda j, i: (0, j))],
        out_shape=[jax.ShapeDtypeStruct((L, NSTATE), F32)] * 2 + [jax.ShapeDtypeStruct((8, NSTATE), F32)],
        scratch_shapes=[pltpu.VMEM((1, nl), F32), pltpu.VMEM((1, nl), F32)],
        compiler_params=_params(2),
    )(g_re, g_im, s_re, s_im, s_re, s_im, ar, ai)


def _s5_in_bwd(dx1, lam_re, lam_im, du_skip, x, pv, b_re, b_im, w_in, layer, tm):
    L = x.shape[0]

    def body(dx1_ref, lre_ref, lim_ref, dus_ref, x_ref, pv_ref, bre_ref, bim_ref, win_ref, dx_ref, du_ref, vs_ref):
        @pl.when(pl.program_id(0) == 0)
        def _():
            vs_ref[...] = jnp.zeros_like(vs_ref)

        p = pv_ref[...]
        lrb, lib = _bf(lre_ref[...]), _bf(lim_ref[...])
        parts = []
        for k in range(S5_NB):
            sl = slice(k * S5_BP, (k + 1) * S5_BP)
            parts.append(_dot_nt(lrb[:, sl], bre_ref[k]) + _dot_nt(lib[:, sl], bim_ref[k]))
        dub = _bf(jnp.concatenate(parts, axis=1) + dus_ref[...])
        du_ref[...] = dub
        dh = _dot_nt(dub, win_ref[...])
        _, xn, r = _norm_mod(x_ref[...], p[R_N1:R_N1 + 1], p[R_SC1:R_SC1 + 1], p[R_SH1:R_SH1 + 1])
        dx_ref[...] = dx1_ref[...] + _norm_mod_bwd(dh, xn, r, p[R_N1:R_N1 + 1], p[R_SC1:R_SC1 + 1])
        vs_ref[1:2, :] += _sum0(dh * xn)
        vs_ref[2:3, :] += _sum0(dh)

    return pl.pallas_call(
        body, name="s5_in_bwd", grid=(L // tm,),
        in_specs=[_rows(tm, D), _rows(tm, NSTATE), _rows(tm, NSTATE), _rows(tm, D), _rows(tm, D),
                  pl.BlockSpec((None, 8, D), lambda i: (layer, 0, 0)),
                  _const_w((S5_NB, S5_BH, S5_BP)), _const_w((S5_NB, S5_BH, S5_BP)), _layer_w(D, D, 0)],
        out_specs=[_rows(tm, D), _rows(tm, D), _whole((8, D))],
        out_shape=[jax.ShapeDtypeStruct((L, D), F32), jax.ShapeDtypeStruct((L, D), BF16), jax.ShapeDtypeStruct((8, D), F32)],
        compiler_params=_params(1, 56),
    )(dx1, lam_re, lam_im, du_skip, x, pv, b_re, b_im, w_in)


NSEG = 8
SCAN_LANES = 1024


def _to_segments(x):
    n, c = x.shape
    return x.reshape(NSEG, n // NSEG, c).transpose(1, 0, 2).reshape(n, c)


def _from_segments(x):
    n, c = x.shape
    return x.reshape(n // NSEG, NSEG, c).transpose(1, 0, 2).reshape(n, c)


def _segment_scan(re_ref, im_ref, st_re, st_im, a_re, a_im, n_slabs, adjoint, write):
    for q in range(NSTATE // SCAN_LANES):
        ls = slice(q * SCAN_LANES, (q + 1) * SCAN_LANES)
        ar = jnp.broadcast_to(a_re[:, ls], (8, SCAN_LANES))
        ai = jnp.broadcast_to(a_im[:, ls], (8, SCAN_LANES))

        def step(k, carry, ls=ls, ar=ar, ai=ai):
            s_r, s_i = carry
            slab = (n_slabs - 1 - k) if adjoint else k
            rows = pl.ds(pl.multiple_of(slab * 8, 8), 8)
            b_r, b_i = re_ref[rows, ls], im_ref[rows, ls]
            if adjoint:
                n_r = b_r + ar * s_r + ai * s_i
                n_i = b_i - ai * s_r + ar * s_i
            else:
                n_r = ar * s_r - ai * s_i + b_r
                n_i = ar * s_i + ai * s_r + b_i
            if write:
                re_ref[rows, ls] = n_r
                im_ref[rows, ls] = n_i
            return n_r, n_i

        s_r, s_i = lax.fori_loop(0, n_slabs, step, (st_re[:, ls], st_im[:, ls]), unroll=4)
        st_re[:, ls] = s_r
        st_im[:, ls] = s_i


def _s5_segment_states(e_re, e_im, ar, ai, seg_len, adjoint):
    def body(ere_ref, eim_ref, ar_ref, ai_ref, ore_ref, oim_ref):
        p_r, p_i = ar_ref[...], ai_ref[...]
        if adjoint:
            p_i = -p_i
        acc_r, acc_i = jnp.ones_like(p_r), jnp.zeros_like(p_r)
        n = seg_len
        while n:
            if n & 1:
                acc_r, acc_i = acc_r * p_r - acc_i * p_i, acc_r * p_i + acc_i * p_r
            n >>= 1
            if n:
                p_r, p_i = p_r * p_r - p_i * p_i, 2.0 * p_r * p_i
        e_r, e_i = ere_ref[...], eim_ref[...]
        s_r, s_i = jnp.zeros_like(acc_r), jnp.zeros_like(acc_r)
        rows_r, rows_i = [None] * NSEG, [None] * NSEG
        order = range(NSEG - 1, -1, -1) if adjoint else range(NSEG)
        for j in order:
            rows_r[j], rows_i[j] = s_r, s_i
            s_r, s_i = (acc_r * s_r - acc_i * s_i + e_r[j:j + 1], acc_r * s_i + acc_i * s_r + e_i[j:j + 1])
        ore_ref[...] = jnp.concatenate(rows_r, axis=0)
        oim_ref[...] = jnp.concatenate(rows_i, axis=0)

    st = jax.ShapeDtypeStruct((NSEG, NSTATE), F32)
    return pl.pallas_call(body, name="s5_segment_states_bwd" if adjoint else "s5_segment_states_fwd", out_shape=[st, st])(
        e_re, e_im, ar, ai)


def _s5_fwd_ends(x, pv, w_in, b_re, b_im, ar, ai, layer, tm, after=None):
    L = x.shape[0]

    def body(x_ref, pv_ref, win_ref, bre_ref, bim_ref, ar_ref, ai_ref, h_ref, u_ref, ere_ref, eim_ref, bu_re, bu_im):
        @pl.when(pl.program_id(0) == 0)
        def _():
            ere_ref[...] = jnp.zeros_like(ere_ref)
            eim_ref[...] = jnp.zeros_like(eim_ref)

        p = pv_ref[...]
        h, _, _ = _norm_mod(x_ref[...], p[R_N1:R_N1 + 1], p[R_SC1:R_SC1 + 1], p[R_SH1:R_SH1 + 1])
        hb = _bf(h)
        h_ref[...] = hb
        u = _dot(hb, win_ref[...])
        u_ref[...] = u
        ub = _bf(u)
        for k in range(S5_NB):
            uk = ub[:, k * S5_BH:(k + 1) * S5_BH]
            bu_re[:, k * S5_BP:(k + 1) * S5_BP] = _dot(uk, bre_ref[k])
            bu_im[:, k * S5_BP:(k + 1) * S5_BP] = _dot(uk, bim_ref[k])
        _segment_scan(bu_re, bu_im, ere_ref, eim_ref, ar_ref[...], ai_ref[...], tm // 8, adjoint=False, write=False)

    call, tail = _call_after(
        after, body, 7, name="s5_fwd_ends", grid=(L // tm,),
        in_specs=[_rows(tm, D), pl.BlockSpec((None, 8, D), lambda i: (layer, 0, 0)), _layer_w(D, D, 0),
                  _const_w((S5_NB, S5_BH, S5_BP)), _const_w((S5_NB, S5_BH, S5_BP)), _whole((1, NSTATE)), _whole((1, NSTATE))],
        out_specs=[_rows(tm, D), _rows(tm, D), _whole((NSEG, NSTATE)), _whole((NSEG, NSTATE))],
        out_shape=[jax.ShapeDtypeStruct((L, D), BF16), jax.ShapeDtypeStruct((L, D), F32),
                   jax.ShapeDtypeStruct((NSEG, NSTATE), F32), jax.ShapeDtypeStruct((NSEG, NSTATE), F32)],
        scratch_shapes=[pltpu.VMEM((tm, NSTATE), F32), pltpu.VMEM((tm, NSTATE), F32)],
        compiler_params=_params(1, 56),
    )
    return call(x, pv, w_in, b_re, b_im, ar, ai, *tail)


def _s5_fwd_out(x, u, pv, b_re, b_im, s0_re, s0_im, ar, ai, c_re, c_im, dvec, glu_w, glu_b, w_out, layer, tm):
    L = x.shape[0]

    def body(x_ref, u_ref, pv_ref, bre_ref, bim_ref, s0re_ref, s0im_ref, ar_ref, ai_ref, cre_ref, cim_ref, d_ref, gw_ref,
             gb_ref, wout_ref, x1_ref, sre_ref, sim_ref, y1_ref, zg_ref, y3_ref, y_ref, st_re, st_im):
        @pl.when(pl.program_id(0) == 0)
        def _():
            st_re[...] = s0re_ref[...]
            st_im[...] = s0im_ref[...]

        p = pv_ref[...]
        uv = u_ref[...]
        ub = _bf(uv)
        for k in range(S5_NB):
            uk = ub[:, k * S5_BH:(k + 1) * S5_BH]
            sre_ref[:, k * S5_BP:(k + 1) * S5_BP] = _dot(uk, bre_ref[k])
            sim_ref[:, k * S5_BP:(k + 1) * S5_BP] = _dot(uk, bim_ref[k])
        _segment_scan(sre_ref, sim_ref, st_re, st_im, ar_ref[...], ai_ref[...], tm // 8, adjoint=False, write=True)
        parts = []
        for k in range(S5_NB):
            sl = slice(k * S5_BP, (k + 1) * S5_BP)
            parts.append(_dot(_bf(sre_ref[:, sl]), cre_ref[k]) - _dot(_bf(sim_ref[:, sl]), cim_ref[k]))
        y1 = jnp.concatenate(parts, axis=1) + d_ref[...] * uv
        y1_ref[...] = y1
        y2 = jax.nn.gelu(y1)
        zg = _dot(_bf(y2), gw_ref[...]) + gb_ref[...]
        zg_ref[...] = zg
        y3b = _bf(y2 * jax.nn.sigmoid(zg))
        y3_ref[...] = y3b
        y = _dot(y3b, wout_ref[...])
        y_ref[...] = y
        x1_ref[...] = x_ref[...] + p[R_G1:R_G1 + 1] * y

    return pl.pallas_call(
        body, name="s5_fwd_out", grid=(L // tm,),
        in_specs=[_rows(tm, D), _rows(tm, D), pl.BlockSpec((None, 8, D), lambda i: (layer, 0, 0)),
                  _const_w((S5_NB, S5_BH, S5_BP)), _const_w((S5_NB, S5_BH, S5_BP)),
                  _whole((NSEG, NSTATE)), _whole((NSEG, NSTATE)), _whole((1, NSTATE)), _whole((1, NSTATE)),
                  _const_w((S5_NB, S5_BP, S5_BH)), _const_w((S5_NB, S5_BP, S5_BH)), _whole((1, D)),
                  _layer_w(D, D, 0), _whole((1, D)), _layer_w(D, D, 0)],
        out_specs=[_rows(tm, D), _rows(tm, NSTATE), _rows(tm, NSTATE), _rows(tm, D), _rows(tm, D), _rows(tm, D), _rows(tm, D)],
        out_shape=[jax.ShapeDtypeStruct((L, D), F32), jax.ShapeDtypeStruct((L, NSTATE), F32), jax.ShapeDtypeStruct((L, NSTATE), F32),
                   jax.ShapeDtypeStruct((L, D), F32), jax.ShapeDtypeStruct((L, D), F32),
                   jax.ShapeDtypeStruct((L, D), BF16), jax.ShapeDtypeStruct((L, D), F32)],
        scratch_shapes=[pltpu.VMEM((NSEG, NSTATE), F32), pltpu.VMEM((NSEG, NSTATE), F32)],
        compiler_params=_params(1, 56),
    )(x, u, pv, b_re, b_im, s0_re, s0_im, ar, ai, c_re, c_im, dvec, glu_w, glu_b, w_out)


def _s5_bwd_ends(dx1, y, y1, zg, u, pv, c_re, c_im, ar, ai, dvec, glu_w, w_out, layer, tm, after=None):
    L = dx1.shape[0]
    nt = L // tm

    def body(dx1_ref, y_ref, y1_ref, zg_ref, u_ref, pv_ref, cre_ref, cim_ref, ar_ref, ai_ref, d_ref, gw_ref, wout_ref,
             dy_ref, y2_ref, dzg_ref, dy1_ref, dus_ref, ere_ref, eim_ref, vs_ref, g_re, g_im):
        @pl.when(pl.program_id(0) == 0)
        def _():
            vs_ref[...] = jnp.zeros_like(vs_ref)
            ere_ref[...] = jnp.zeros_like(ere_ref)
            eim_ref[...] = jnp.zeros_like(eim_ref)

        dx1v, p = dx1_ref[...], pv_ref[...]
        dyb = _bf(dx1v * p[R_G1:R_G1 + 1])
        dy_ref[...] = dyb
        vs_ref[0:1, :] += _sum0(dx1v * y_ref[...])
        dy3 = _dot_nt(dyb, wout_ref[...])
        y2, gelu_vjp = jax.vjp(jax.nn.gelu, y1_ref[...])
        y2_ref[...] = _bf(y2)
        gate = jax.nn.sigmoid(zg_ref[...])
        dzg = dy3 * y2 * gate * (1.0 - gate)
        dzgb = _bf(dzg)
        dzg_ref[...] = dzgb
        vs_ref[1:2, :] += _sum0(dzg)
        dy2 = dy3 * gate + _dot_nt(dzgb, gw_ref[...])
        dy1 = gelu_vjp(dy2)[0]
        vs_ref[2:3, :] += _sum0(dy1 * u_ref[...])
        dus_ref[...] = dy1 * d_ref[...]
        dy1b = _bf(dy1)
        dy1_ref[...] = dy1b
        for k in range(S5_NB):
            dk = dy1b[:, k * S5_BH:(k + 1) * S5_BH]
            g_re[:, k * S5_BP:(k + 1) * S5_BP] = _dot_nt(dk, cre_ref[k])
            g_im[:, k * S5_BP:(k + 1) * S5_BP] = -_dot_nt(dk, cim_ref[k])
        _segment_scan(g_re, g_im, ere_ref, eim_ref, ar_ref[...], ai_ref[...], tm // 8, adjoint=True, write=False)

    call, tail = _call_after(
        after, body, 13, name="s5_bwd_ends", grid=(nt,),
        in_specs=[_rows(tm, D, nt)] * 5 + [pl.BlockSpec((None, 8, D), lambda i: (layer, 0, 0)),
                  _const_w((S5_NB, S5_BP, S5_BH)), _const_w((S5_NB, S5_BP, S5_BH)), _whole((1, NSTATE)), _whole((1, NSTATE)),
                  _whole((1, D)), _layer_w(D, D, 0), _layer_w(D, D, 0)],
        out_specs=[_rows(tm, D, nt)] * 5 + [_whole((NSEG, NSTATE)), _whole((NSEG, NSTATE)), _whole((8, D))],
        out_shape=[jax.ShapeDtypeStruct((L, D), BF16)] * 4 + [jax.ShapeDtypeStruct((L, D), F32),
                   jax.ShapeDtypeStruct((NSEG, NSTATE), F32), jax.ShapeDtypeStruct((NSEG, NSTATE), F32),
                   jax.ShapeDtypeStruct((8, D), F32)],
        scratch_shapes=[pltpu.VMEM((tm, NSTATE), F32), pltpu.VMEM((tm, NSTATE), F32)],
        compiler_params=_params(1, 56),
    )
    return call(dx1, y, y1, zg, u, pv, c_re, c_im, ar, ai, dvec, glu_w, w_out, *tail)


def _s5_bwd_in(dx1, dy1_b, du_skip, x, s_re, s_im, pv, b_re, b_im, c_re, c_im, l0_re, l0_im, ar, ai, w_in, layer, tm):
    L = x.shape[0]
    nt = L // tm

    def body(dx1_ref, dy1_ref, dus_ref, x_ref, sre_ref, sim_ref, hre_ref, him_ref, lre_ref, lim_ref, pv_ref, bre_ref, bim_ref,
             cre_ref, cim_ref, l0re_ref, l0im_ref, ar_ref, ai_ref, win_ref,
             dx_ref, du_ref, lamre_ref, lamim_ref, da_ref, vs_ref, g_re, g_im, st_re, st_im):
        gi = pl.program_id(0)
        tile = nt - 1 - gi

        @pl.when(gi == 0)
        def _():
            vs_ref[...] = jnp.zeros_like(vs_ref)
            da_ref[...] = jnp.zeros_like(da_ref)
            st_re[...] = l0re_ref[...]
            st_im[...] = l0im_ref[...]

        p = pv_ref[...]
        dy1b = dy1_ref[...]
        for k in range(S5_NB):
            dk = dy1b[:, k * S5_BH:(k + 1) * S5_BH]
            g_re[:, k * S5_BP:(k + 1) * S5_BP] = _dot_nt(dk, cre_ref[k])
            g_im[:, k * S5_BP:(k + 1) * S5_BP] = -_dot_nt(dk, cim_ref[k])
        _segment_scan(g_re, g_im, st_re, st_im, ar_ref[...], ai_ref[...], tm // 8, adjoint=True, write=True)
        lam_r, lam_i = g_re[...], g_im[...]
        lrb, lib = _bf(lam_r), _bf(lam_i)
        lamre_ref[...] = lrb
        lamim_ref[...] = lib

        def wrapped(last_ref):
            z = last_ref[...]
            row = lax.broadcasted_iota(jnp.int32, z.shape, 0)
            return jnp.where(row >= 1, pltpu.roll(z, 1, 0), 0.0)

        first_r = jnp.where(tile > 0, hre_ref[...], wrapped(lre_ref))
        first_i = jnp.where(tile > 0, him_ref[...], wrapped(lim_ref))
        sp_r = jnp.concatenate([first_r, sre_ref[0:tm - 8, :]], axis=0)
        sp_i = jnp.concatenate([first_i, sim_ref[0:tm - 8, :]], axis=0)
        da_ref[0:1, :] += _sum0(lam_r * sp_r + lam_i * sp_i)
        da_ref[1:2, :] += _sum0(lam_i * sp_r - lam_r * sp_i)

        parts = []
        for k in range(S5_NB):
            sl = slice(k * S5_BP, (k + 1) * S5_BP)
            parts.append(_dot_nt(lrb[:, sl], bre_ref[k]) + _dot_nt(lib[:, sl], bim_ref[k]))
        dub = _bf(jnp.concatenate(parts, axis=1) + dus_ref[...])
        du_ref[...] = dub
        dh = _dot_nt(dub, win_ref[...])
        _, xn, r = _norm_mod(x_ref[...], p[R_N1:R_N1 + 1], p[R_SC1:R_SC1 + 1], p[R_SH1:R_SH1 + 1])
        dx_ref[...] = dx1_ref[...] + _norm_mod_bwd(dh, xn, r, p[R_N1:R_N1 + 1], p[R_SC1:R_SC1 + 1])
        vs_ref[1:2, :] += _sum0(dh * xn)
        vs_ref[2:3, :] += _sum0(dh)

    halo = pl.BlockSpec((8, NSTATE), lambda i: (jnp.maximum((nt - 1 - i) * (tm // 8) - 1, 0), 0))
    last = pl.BlockSpec((8, NSTATE), lambda i: (L // 8 - 1, 0))
    return pl.pallas_call(
        body, name="s5_bwd_in", grid=(nt,),
        in_specs=[_rows(tm, D, nt), _rows(tm, D, nt), _rows(tm, D, nt), _rows(tm, D, nt), _rows(tm, NSTATE, nt), _rows(tm, NSTATE, nt),
                  halo, halo, last, last, pl.BlockSpec((None, 8, D), lambda i: (layer, 0, 0)),
                  _const_w((S5_NB, S5_BH, S5_BP)), _const_w((S5_NB, S5_BH, S5_BP)),
                  _const_w((S5_NB, S5_BP, S5_BH)), _const_w((S5_NB, S5_BP, S5_BH)),
                  _whole((NSEG, NSTATE)), _whole((NSEG, NSTATE)), _whole((1, NSTATE)), _whole((1, NSTATE)), _layer_w(D, D, 0)],
        out_specs=[_rows(tm, D, nt), _rows(tm, D, nt), _rows(tm, NSTATE, nt), _rows(tm, NSTATE, nt), _whole((8, NSTATE)), _whole((8, D))],
        out_shape=[jax.ShapeDtypeStruct((L, D), F32), jax.ShapeDtypeStruct((L, D), BF16),
                   jax.ShapeDtypeStruct((L, NSTATE), BF16), jax.ShapeDtypeStruct((L, NSTATE), BF16),
                   jax.ShapeDtypeStruct((8, NSTATE), F32), jax.ShapeDtypeStruct((8, D), F32)],
        scratch_shapes=[pltpu.VMEM((tm, NSTATE), F32), pltpu.VMEM((tm, NSTATE), F32),
                        pltpu.VMEM((NSEG, NSTATE), F32), pltpu.VMEM((NSEG, NSTATE), F32)],
        compiler_params=_params(1, 60),
    )(dx1, dy1_b, du_skip, x, s_re, s_im, s_re, s_im, s_re, s_im, pv, b_re, b_im, c_re, c_im, l0_re, l0_im, ar, ai, w_in)


def _blockdiag_b(bt):
    b = bt.reshape(S5_H, S5_NB, 16, S5_P).transpose(1, 2, 0, 3)
    eye = jnp.eye(16, dtype=bt.dtype)
    return (b[:, :, :, None, :] * eye[None, :, None, :, None]).reshape(S5_NB, S5_BH, S5_BP)


def _unblock_b(d):
    d = jnp.einsum("bghgp->bghp", d.reshape(S5_NB, 16, S5_H, 16, S5_P))
    return d.transpose(2, 0, 1, 3).reshape(S5_H, S5_G, S5_P)


def _blockdiag_c(cm):
    c4 = cm.reshape(S5_NB, 16, S5_H, S5_P)
    eye = jnp.eye(16, dtype=cm.dtype)
    out = c4.transpose(0, 1, 3, 2)[:, :, :, None, :] * eye[None, :, None, :, None]
    return out.reshape(S5_NB, S5_BP, S5_BH)


def _unblock_c(d):
    d = jnp.einsum("bgpgh->bghp", d.reshape(S5_NB, 16, S5_P, 16, S5_H))
    return d.reshape(S5_G, S5_H, S5_P)


def _tril_mask():
    return lax.broadcasted_iota(jnp.int32, (SG_CHUNK, SG_CHUNK), 0) >= lax.broadcasted_iota(jnp.int32, (SG_CHUNK, SG_CHUNK), 1)


def _sg_fwd(x, pv, w_in, w_s, b_t, vg, w_out, layer, tm, after=None):
    L = x.shape[0]
    nc = tm // SG_CHUNK

    def body(x_ref, pv_ref, win_ref, ws_ref, bt_ref, vg_ref, wout_ref, x1_ref, h_ref, uv_ref, vm_ref, q_ref, y_ref):
        xv, p = x_ref[...], pv_ref[...]
        h, _, _ = _norm_mod(xv, p[R_N1:R_N1 + 1], p[R_SC1:R_SC1 + 1], p[R_SH1:R_SH1 + 1])
        hb = _bf(h)
        h_ref[...] = hb
        uv = _dot(hb, win_ref[...])
        uv_ref[...] = uv
        v = uv[:, D:]
        rv = lax.rsqrt(jnp.mean(v * v, axis=-1, keepdims=True) + EPS)
        vnb = _bf((v * rv) * vg_ref[...])
        mask = _tril_mask()
        bt = bt_ref[...]
        for hd in range(SG_HEADS):
            wm = _bf(jnp.where(mask, ws_ref[hd], 0.0))
            cs = slice(hd * SG_CHUNK, (hd + 1) * SG_CHUNK)
            for ck in range(nc):
                rs = slice(ck * SG_CHUNK, (ck + 1) * SG_CHUNK)
                vm_ref[rs, cs] = _dot(wm, vnb[rs, cs]) + bt[:, hd:hd + 1]
        qb = _bf(uv[:, :D] * vm_ref[...])
        q_ref[...] = qb
        y = _dot(qb, wout_ref[...])
        y_ref[...] = y
        x1_ref[...] = xv + p[R_G1:R_G1 + 1] * y

    call, tail = _call_after(
        after, body, 7, name="sg_fwd", grid=(L // tm,),
        in_specs=[_rows(tm, D), pl.BlockSpec((None, 8, D), lambda i: (layer, 0, 0)), _layer_w(D, 2 * D, 0),
                  _whole((SG_HEADS, SG_CHUNK, SG_CHUNK)), _whole((SG_CHUNK, SG_HEADS)), _whole((1, D)), _layer_w(D, D, 0)],
        out_specs=[_rows(tm, D), _rows(tm, D), _rows(tm, 2 * D), _rows(tm, D), _rows(tm, D), _rows(tm, D)],
        out_shape=[jax.ShapeDtypeStruct((L, D), F32), jax.ShapeDtypeStruct((L, D), BF16), jax.ShapeDtypeStruct((L, 2 * D), F32),
                   jax.ShapeDtypeStruct((L, D), F32), jax.ShapeDtypeStruct((L, D), BF16), jax.ShapeDtypeStruct((L, D), F32)],
        compiler_params=_params(1, 56),
    )
    return call(x, pv, w_in, w_s, b_t, vg, w_out, *tail)


def _sg_bwd(dx1, x, y, uv, vm, pv, w_in, w_s, vg, w_out, layer, tm, after=None):
    L = x.shape[0]
    nc = tm // SG_CHUNK

    def body(dx1_ref, x_ref, y_ref, uv_ref, vm_ref, pv_ref, win_ref, ws_ref, vg_ref, wout_ref,
             dx_ref, duv_ref, dy_ref, vs_ref, dws_ref, dbt_ref, dvn_scr):
        @pl.when(pl.program_id(0) == 0)
        def _():
            vs_ref[...] = jnp.zeros_like(vs_ref)
            dws_ref[...] = jnp.zeros_like(dws_ref)
            dbt_ref[...] = jnp.zeros_like(dbt_ref)

        dx1v, p = dx1_ref[...], pv_ref[...]
        dyb = _bf(dx1v * p[R_G1:R_G1 + 1])
        dy_ref[...] = dyb
        vs_ref[0:1, :] += _sum0(dx1v * y_ref[...])
        dq = _dot_nt(dyb, wout_ref[...])
        uv = uv_ref[...]
        u, v = uv[:, :D], uv[:, D:]
        dub = _bf(dq * vm_ref[...])
        dvm = dq * u
        dvmb = _bf(dvm)
        rv = lax.rsqrt(jnp.mean(v * v, axis=-1, keepdims=True) + EPS)
        vh = v * rv
        vgv = vg_ref[...]
        vnb = _bf(vh * vgv)
        mask = _tril_mask()
        for hd in range(SG_HEADS):
            wm = _bf(jnp.where(mask, ws_ref[hd], 0.0))
            cs = slice(hd * SG_CHUNK, (hd + 1) * SG_CHUNK)
            dws = jnp.zeros((SG_CHUNK, SG_CHUNK), F32)
            dbs = jnp.zeros((SG_CHUNK, 1), F32)
            for ck in range(nc):
                rs = slice(ck * SG_CHUNK, (ck + 1) * SG_CHUNK)
                dvn_scr[rs, cs] = _dot_tn(wm, dvmb[rs, cs])
                dws = dws + _dot_nt(dvmb[rs, cs], vnb[rs, cs])
                dbs = dbs + jnp.sum(dvm[rs, cs], axis=1, keepdims=True)
            dws_ref[hd] += jnp.where(mask, dws, 0.0)
            dbt_ref[:, hd:hd + 1] += dbs
        dvn = dvn_scr[...]
        vs_ref[3:4, :] += _sum0(dvn * vh)
        dvnn = dvn * vgv
        dvb = _bf(rv * (dvnn - vh * jnp.mean(dvnn * vh, axis=-1, keepdims=True)))
        duv_ref[:, 0:D] = dub
        duv_ref[:, D:2 * D] = dvb
        dh = _dot_nt(dub, win_ref[:, 0:D]) + _dot_nt(dvb, win_ref[:, D:2 * D])
        _, xn, r = _norm_mod(x_ref[...], p[R_N1:R_N1 + 1], p[R_SC1:R_SC1 + 1], p[R_SH1:R_SH1 + 1])
        dx_ref[...] = dx1v + _norm_mod_bwd(dh, xn, r, p[R_N1:R_N1 + 1], p[R_SC1:R_SC1 + 1])
        vs_ref[1:2, :] += _sum0(dh * xn)
        vs_ref[2:3, :] += _sum0(dh)

    call, tail = _call_after(
        after, body, 10, name="sg_bwd", grid=(L // tm,),
        in_specs=[_rows(tm, D), _rows(tm, D), _rows(tm, D), _rows(tm, 2 * D), _rows(tm, D),
                  pl.BlockSpec((None, 8, D), lambda i: (layer, 0, 0)), _layer_w(D, 2 * D, 0),
                  _whole((SG_HEADS, SG_CHUNK, SG_CHUNK)), _whole((1, D)), _layer_w(D, D, 0)],
        out_specs=[_rows(tm, D), _rows(tm, 2 * D), _rows(tm, D), _whole((8, D)),
                   _whole((SG_HEADS, SG_CHUNK, SG_CHUNK)), _whole((SG_CHUNK, SG_HEADS))],
        out_shape=[jax.ShapeDtypeStruct((L, D), F32), jax.ShapeDtypeStruct((L, 2 * D), BF16), jax.ShapeDtypeStruct((L, D), BF16),
                   jax.ShapeDtypeStruct((8, D), F32), jax.ShapeDtypeStruct((SG_HEADS, SG_CHUNK, SG_CHUNK), F32),
                   jax.ShapeDtypeStruct((SG_CHUNK, SG_HEADS), F32)],
        scratch_shapes=[pltpu.VMEM((tm, D), F32)],
        compiler_params=_params(1, 56),
    )
    return call(dx1, x, y, uv, vm, pv, w_in, w_s, vg, w_out, *tail)


def _final(x, target, fg, tm):
    L = x.shape[0]

    def body(x_ref, t_ref, g_ref, dx_ref, vs_ref):
        @pl.when(pl.program_id(0) == 0)
        def _():
            vs_ref[...] = jnp.zeros_like(vs_ref)

        xv, g = x_ref[...], g_ref[...]
        r = lax.rsqrt(jnp.mean(xv * xv, axis=-1, keepdims=True) + EPS)
        xn = xv * r
        e = xn * g - t_ref[...]
        vs_ref[0:1, :] += jnp.sum(e * e)
        dout = e * (1.0 / D)
        vs_ref[1:2, :] += _sum0(dout * xn)
        dxn = dout * g
        dx_ref[...] = r * (dxn - xn * jnp.mean(dxn * xn, axis=-1, keepdims=True))

    return pl.pallas_call(
        body, name="final_loss", grid=(L // tm,),
        in_specs=[_rows(tm, D), _rows(tm, D), _whole((1, D))],
        out_specs=[_rows(tm, D), _whole((8, D))],
        out_shape=[jax.ShapeDtypeStruct((L, D), F32), jax.ShapeDtypeStruct((8, D), F32)],
        compiler_params=_params(1),
    )(x, target, fg)


def _pack_flat(arrs, multiple=LANES):
    flat = jnp.concatenate([a.reshape(-1).astype(F32) for a in arrs])
    return jnp.pad(flat, (0, -flat.shape[0] % multiple))


def _pack(arrs, row_multiple=8):
    return _pack_flat(arrs, row_multiple * LANES).reshape(-1, LANES)


def _unpack(buf, shapes, lead=()):
    flat = buf.reshape(lead + (-1,))
    out, off = [], 0
    for s in shapes:
        n = 1
        for d in s:
            n *= d
        out.append(flat[..., off:off + n].reshape(lead + tuple(s)))
        off += n
    return out


BIG = ("ff_w1", "ff_w2", "conv_w_in", "conv_w_out", "ssm_w_in", "ssm_glu_w", "ssm_w_out", "sg_w_in", "sg_w_out")
BIG_AXIS = {"ff_w1": 2, "ff_w2": 1, "conv_w_in": 2, "conv_w_out": 1, "ssm_w_in": 1, "ssm_glu_w": 1, "ssm_w_out": 1,
            "sg_w_in": 2, "sg_w_out": 1}
LAYER_WEIGHTS = (
    (("conv_w_in", 0), ("conv_w_out", 0), ("ff_w1", 0), ("ff_w2", 0)),
    (("ssm_w_in", 0), ("ssm_glu_w", 0), ("ssm_w_out", 0), ("ff_w1", 1), ("ff_w2", 1)),
    (("sg_w_in", 0), ("sg_w_out", 0), ("ff_w1", 2), ("ff_w2", 2)),
    (("conv_w_in", 1), ("conv_w_out", 1), ("ff_w1", 3), ("ff_w2", 3)),
)
GATHER_GROUPS = tuple(grp for lw in LAYER_WEIGHTS for grp in (lw[:-2], lw[-2:]))
SMALL_SHARDED = ("conv_w", "conv_b", "sg_v_g")
SMALL_WIDE_PADDED = ("ssm_b_re", "ssm_b_im")
SMALL = ("ada_b", "norm1_g", "norm2_g", "final_g", "ssm_a_re", "ssm_a_im", "ssm_log_dt", "ssm_b_re", "ssm_b_im", "ssm_c_re",
         "ssm_c_im", "ssm_d", "ssm_glu_b", "sg_w_s", "sg_b_s") + SMALL_SHARDED
WEIGHTS = ("ada_w", "ada_b", "norm1_g", "norm2_g", "ff_w1", "ff_w2", "final_g", "conv_w_in", "conv_w", "conv_b", "conv_w_out",
           "ssm_w_in", "ssm_a_re", "ssm_a_im", "ssm_log_dt", "ssm_b_re", "ssm_b_im", "ssm_c_re", "ssm_c_im", "ssm_d",
           "ssm_glu_w", "ssm_glu_b", "ssm_w_out", "sg_w_in", "sg_v_g", "sg_w_s", "sg_b_s", "sg_w_out")


def kernel(x, c, ada_w, ada_b, norm1_g, norm2_g, ff_w1, ff_w2, final_g, conv_w_in, conv_w, conv_b, conv_w_out, ssm_w_in, ssm_a_re, ssm_a_im, ssm_log_dt, ssm_b_re, ssm_b_im, ssm_c_re, ssm_c_im, ssm_d, ssm_glu_w, ssm_glu_b, ssm_w_out, sg_w_in, sg_v_g, sg_w_s, sg_b_s, sg_w_out, loss_target, m_ada_w, m_ada_b, m_norm1_g, m_norm2_g, m_ff_w1, m_ff_w2, m_final_g, m_conv_w_in, m_conv_w, m_conv_b, m_conv_w_out, m_ssm_w_in, m_ssm_a_re, m_ssm_a_im, m_ssm_log_dt, m_ssm_b_re, m_ssm_b_im, m_ssm_c_re, m_ssm_c_im, m_ssm_d, m_ssm_glu_w, m_ssm_glu_b, m_ssm_w_out, m_sg_w_in, m_sg_v_g, m_sg_w_s, m_sg_b_s, m_sg_w_out, v_ada_w, v_ada_b, v_norm1_g, v_norm2_g, v_ff_w1, v_ff_w2, v_final_g, v_conv_w_in, v_conv_w, v_conv_b, v_conv_w_out, v_ssm_w_in, v_ssm_a_re, v_ssm_a_im, v_ssm_log_dt, v_ssm_b_re, v_ssm_b_im, v_ssm_c_re, v_ssm_c_im, v_ssm_d, v_ssm_glu_w, v_ssm_glu_b, v_ssm_w_out, v_sg_w_in, v_sg_v_g, v_sg_w_s, v_sg_b_s, v_sg_w_out):
    args = dict(locals())
    w = {n: args[n] for n in WEIGHTS}
    m = {n: args["m_" + n] for n in WEIGHTS}
    v = {n: args["v_" + n] for n in WEIGHTS}
    L = x.shape[1]
    tm = min(L, 256)
    tr = min(L, 512)
    chip = 2 * lax.axis_index("x") + lax.axis_index("y")
    me = 2 * chip + lax.axis_index("c")
    xin = x[0]
    target = loss_target[0]
    chip1 = chip.reshape(1).astype(jnp.int32)

    gathers = []

    def start_gather(g, after):
        entries = GATHER_GROUPS[g]
        axes = [BIG_AXIS[n] for n, _ in entries]
        lands = [_cast_place(w[n], li, BIG_AXIS[n], chip1, f"cast_{n}_{li}") for n, li in entries]
        s_sems, r_sems, lands, token = _gather_start(lands, axes, f"gather_start{g}", after)
        gathers.append((s_sems, r_sems, lands, axes))
        return token

    def weights_of(g, after):
        s_sems, r_sems, lands, axes = gathers[g]
        lands = _gather_wait(s_sems, r_sems, lands, axes, f"gather_wait{g}", after)
        return dict(zip([n for n, _ in GATHER_GROUPS[g]], _gather_share(lands, axes, f"gather_share{g}")))

    small_in = _pack([c, conv_w, conv_b, sg_v_g])
    got = _allgather_small(small_in, "gather_small_inputs").reshape(N_DEV, -1)
    c_all, cw_sh, cb_sh, vg_sh = _unpack(got, [(D,), conv_w.shape, conv_b.shape, sg_v_g.shape], lead=(N_DEV,))
    conv_w_full = jnp.concatenate([cw_sh[2 * k] for k in range(4)], axis=-1)
    conv_b_full = jnp.concatenate([cb_sh[2 * k] for k in range(4)], axis=-1)
    vg_full = jnp.concatenate([vg_sh[2 * k] for k in range(4)], axis=-1)
    c16 = jnp.pad(c_all, ((0, 16 - N_DEV), (0, 0)))

    cols = ada_w.shape[2]
    ada_b_cols = lax.dynamic_slice_in_dim(ada_b, chip * cols, cols, axis=1)[:, None, :]
    mod_sh = _ada_fwd(c16, ada_w, ada_b_cols)[:, :N_DEV, :]
    mod_all = _allgather_small(_pack([mod_sh]), "gather_mod").reshape(N_DEV, -1)
    mod_all = _unpack(mod_all, [mod_sh.shape], lead=(N_DEV,))[0]
    mod_mine = lax.dynamic_index_in_dim(mod_all[0::2], me, axis=2, keepdims=False)
    mod_mine = mod_mine.transpose(1, 0, 2).reshape(DEPTH, 6, D)
    pv = jnp.concatenate([mod_mine, norm1_g[:, None, :], norm2_g[:, None, :]], axis=1)

    start_gather(1, start_gather(0, pv))

    cw_rows = jnp.concatenate([conv_w_full, conv_b_full[:, None, :], jnp.zeros((conv_w_full.shape[0], 4, D), F32)], axis=1)

    a_re, a_im = ssm_a_re[0], ssm_a_im[0]
    log_dt = ssm_log_dt[0][:, None]
    bt_re, bt_im = ssm_b_re[0].transpose(2, 0, 1), ssm_b_im[0].transpose(2, 0, 1)
    abar_re, abar_im, bbar_re, bbar_im = _s5_params_fwd(a_re, a_im, log_dt, bt_re, bt_im)
    ar_vec, ai_vec = abar_re.reshape(1, NSTATE), abar_im.reshape(1, NSTATE)
    bd_re, bd_im = _bf(_blockdiag_b(bbar_re)), _bf(_blockdiag_b(bbar_im))
    cd_re, cd_im = _bf(_blockdiag_c(ssm_c_re[0])), _bf(_blockdiag_c(ssm_c_im[0]))

    saved = []
    fulls = []
    xl = xin
    for i in range(DEPTH):
        kind = MIXER_OF_LAYER[i]
        j = i // 3
        full = weights_of(2 * i, cd_im if i == 0 else xl)
        fulls.append(full)
        tok = start_gather(2 * i + 2, next(iter(full.values()))) if i + 1 < DEPTH else None
        if kind == 0:
            x1, h, bcx, conv, q, y = _conv_fwd(xl, pv, full["conv_w_in"], full["conv_w_out"], cw_rows, i, j, tm, after=tok)
            mix = dict(h=h, bcx=bcx, conv=conv, q=q, y=y)
        elif kind == 1:
            xp = _to_segments(xl)
            h, u, e_re, e_im = _s5_fwd_ends(xp, pv, full["ssm_w_in"], bd_re, bd_im, ar_vec, ai_vec, i, tm, after=tok)
            s0_re, s0_im = _s5_segment_states(e_re, e_im, ar_vec, ai_vec, L // NSEG, adjoint=False)
            x1p, s_re, s_im, y1, zg, y3, y = _s5_fwd_out(xp, u, pv, bd_re, bd_im, s0_re, s0_im, ar_vec, ai_vec, cd_re, cd_im,
                                                         ssm_d, full["ssm_glu_w"], ssm_glu_b, full["ssm_w_out"], i, tm)
            x1 = _from_segments(x1p)
            mix = dict(xp=xp, h=h, u=u, s_re=s_re, s_im=s_im, y1=y1, zg=zg, y3=y3, y=y)
        else:
            x1, h, uv, vm, q, y = _sg_fwd(xl, pv, full["sg_w_in"], sg_w_s[0], sg_b_s[0].T, vg_full, full["sg_w_out"], i, tm,
                                          after=tok)
            mix = dict(h=h, uv=uv, vm=vm, q=q, y=y)
        full.update(weights_of(2 * i + 1, x1))
        tok = start_gather(2 * i + 3, full["ff_w2"]) if i + 1 < DEPTH else None
        x2, h2, a, f = _ffn_fwd(x1, pv, full["ff_w1"], full["ff_w2"], i, tm, after=tok)
        saved.append(dict(x=xl, x1=x1, h2=h2, a=a, f=f, **mix))
        xl = x2

    dxl, vs_fin = _final(xl, target, final_g[None, :], tm)

    gfull = {n: [None] * w[n].shape[0] for n in BIG}
    vs_mix, vs_ffn = [None] * DEPTH, [None] * DEPTH
    small_g = {}
    scatters = {}
    token = None

    def start_scatter(key, entries, after):
        garrs = [gfull[n][li][None] for n, li in entries]
        gaxes = [BIG_AXIS[n] for n, _ in entries]
        s_sems, r_sems, garrs, lands, tok = _scatter_start(garrs, gaxes, f"scatter_start{key}", after)
        scatters[key] = (s_sems, r_sems, garrs, lands, gaxes, entries)
        return tok

    for i in reversed(range(DEPTH)):
        kind = MIXER_OF_LAYER[i]
        j = i // 3
        sv = saved[i]
        full = fulls[i]
        dx1, p_b, da_b, df_b, vs_ffn[i] = _ffn_bwd(dxl, sv["x1"], sv["a"], sv["f"], pv, full["ff_w1"], full["ff_w2"], i, tm,
                                                   after=token)
        gfull["ff_w1"][i] = _mm_tn(sv["h2"], da_b, f"wgrad_ff_w1_{i}")
        gfull["ff_w2"][i] = _mm_tn(p_b, df_b, f"wgrad_ff_w2_{i}")
        if i == 0:
            token = start_scatter("0f", LAYER_WEIGHTS[0][2:], dx1)
        if kind == 0:
            dxl, dbcx_b, dy_b, vsm = _conv_bwd(dx1, sv["x"], sv["y"], sv["bcx"], sv["conv"], pv, full["conv_w_in"],
                                               full["conv_w_out"], cw_rows, i, j, tm, after=token if i == 0 else None)
            gfull["conv_w_in"][j] = _mm_tn(sv["h"], dbcx_b, f"wgrad_conv_w_in_{j}")
            gfull["conv_w_out"][j] = _mm_tn(sv["q"], dy_b, f"wgrad_conv_w_out_{j}")
            small_g.setdefault("conv_w", [None, None])[j] = vsm[3:6]
            small_g.setdefault("conv_b", [None, None])[j] = vsm[6]
        elif kind == 1:
            dx1p = _to_segments(dx1)
            dy_b, y2_b, dzg_b, dy1_b, du_skip, eb_re, eb_im, vsm = _s5_bwd_ends(
                dx1p, sv["y"], sv["y1"], sv["zg"], sv["u"], pv, cd_re, cd_im, ar_vec, ai_vec, ssm_d, full["ssm_glu_w"],
                full["ssm_w_out"], i, tm)
            l0_re, l0_im = _s5_segment_states(eb_re, eb_im, ar_vec, ai_vec, L // NSEG, adjoint=True)
            dxp, du_b, lam_re, lam_im, dabar, vs_in = _s5_bwd_in(
                dx1p, dy1_b, du_skip, sv["xp"], sv["s_re"], sv["s_im"], pv, bd_re, bd_im, cd_re, cd_im, l0_re, l0_im,
                ar_vec, ai_vec, full["ssm_w_in"], i, tm)
            dxl = _from_segments(dxp)
            gfull["ssm_w_out"][0] = _mm_tn(sv["y3"], dy_b, "wgrad_ssm_w_out")
            gfull["ssm_glu_w"][0] = _mm_tn(y2_b, dzg_b, "wgrad_ssm_glu_w")
            gfull["ssm_w_in"][0] = _mm_tn(sv["h"], du_b, "wgrad_ssm_w_in")
            s5_late = dict(s_re=sv["s_re"], s_im=sv["s_im"], u=sv["u"], dy1_b=dy1_b, lam_re=lam_re, lam_im=lam_im, dabar=dabar)
            small_g.update(ssm_d=vsm[2], ssm_glu_b=vsm[1])
            vsm = jnp.concatenate([vsm[0:1], vs_in[1:3], jnp.zeros((5, D), F32)], axis=0)
        else:
            dxl, duv_b, dy_b, vsm, d_ws, d_bt = _sg_bwd(dx1, sv["x"], sv["y"], sv["uv"], sv["vm"], pv, full["sg_w_in"],
                                                        sg_w_s[0], vg_full, full["sg_w_out"], i, tm)
            gfull["sg_w_in"][0] = _mm_tn(sv["h"], duv_b, "wgrad_sg_w_in")
            gfull["sg_w_out"][0] = _mm_tn(sv["q"], dy_b, "wgrad_sg_w_out")
            small_g.update(sg_v_g=vsm[3], sg_w_s=d_ws, sg_b_s=d_bt.T)
        vs_mix[i] = vsm
        token = start_scatter(str(i), LAYER_WEIGHTS[i], dxl) if i > 0 else start_scatter("0c", LAYER_WEIGHTS[0][:2], dxl)
    grad_x = dxl[None]

    sums = {n: [None] * w[n].shape[0] for n in BIG}

    def collect(key, after):
        s_sems, r_sems, garrs, lands, gaxes, entries = scatters[key]
        garrs, recv = _scatter_wait(s_sems, r_sems, garrs, lands, gaxes, f"scatter_wait{key}", after)
        for (n, li), g, r3, ax in zip(entries, garrs, recv, gaxes):
            sums[n][li] = _sum_parts(r3, g, ax, chip1, f"sum_{n}_{li}")
        return sums[entries[-1][0]][entries[-1][1]]

    after = token
    for key in ("3", "2", "1"):
        after = collect(key, after)
    early = [(n, li) for i in (3, 2, 1) for n, li in LAYER_WEIGHTS[i]]
    late = list(LAYER_WEIGHTS[0][2:]) + list(LAYER_WEIGHTS[0][:2])
    s_sems, r_sems, mine_thru, lands, tok = _swap_start([sums[n][li] for n, li in early], "swap_start_early", after)

    blocks = dict(
        c_re=_mm_tn_blocks(s5_late["s_re"], s5_late["dy1_b"], S5_BP, S5_BH, "wgrad_s5_c_re", after=tok),
        c_im=_mm_tn_blocks(s5_late["s_im"], s5_late["dy1_b"], S5_BP, S5_BH, "wgrad_s5_c_im", after=tok),
        b_re=_mm_tn_blocks(s5_late["u"], s5_late["lam_re"], S5_BH, S5_BP, "wgrad_s5_b_re", after=tok),
        b_im=_mm_tn_blocks(s5_late["u"], s5_late["lam_im"], S5_BH, S5_BP, "wgrad_s5_b_im", after=tok))
    d_are, d_aim, d_ldt, d_btre, d_btim = _s5_params_bwd(
        a_re, a_im, log_dt, bt_re, bt_im, s5_late["dabar"][0].reshape(S5_G, S5_P), s5_late["dabar"][1].reshape(S5_G, S5_P),
        _unblock_b(blocks["b_re"]), _unblock_b(blocks["b_im"]))
    small_g.update(ssm_a_re=d_are, ssm_a_im=d_aim, ssm_log_dt=d_ldt, ssm_b_re=d_btre.transpose(1, 2, 0),
                   ssm_b_im=d_btim.transpose(1, 2, 0), ssm_c_re=_unblock_c(blocks["c_re"]), ssm_c_im=-_unblock_c(blocks["c_im"]))

    mine_thru, got = _swap_wait(s_sems, r_sems, mine_thru, lands, "swap_wait_early", blocks["b_im"])
    sib = dict(zip(early, got))
    for (n, li), t in zip(early, mine_thru):
        sums[n][li] = t
    after = got[-1]
    for key in ("0f", "0c"):
        after = collect(key, after)
    sib.update(zip(late, _swap_with_sibling([sums[n][li] for n, li in late], "swap_grad_sums_late")))

    dmod = _mod_bwd(jnp.stack(vs_mix), jnp.stack(vs_ffn), pv)
    small_g.update(ada_b=dmod[:, :6, :], norm1_g=dmod[:, 6, :], norm2_g=dmod[:, 7, :], final_g=vs_fin[1],
                   conv_w=jnp.stack(small_g["conv_w"]), conv_b=jnp.stack(small_g["conv_b"]))

    loss_part = (0.5 / D) * vs_fin[0, 0:1]
    part_shapes = [(1,)] + [tuple(small_g[n].shape) for n in SMALL]
    parts_sum = _allreduce_small(_pack([loss_part] + [small_g[n] for n in SMALL], 16), "reduce_small_grads", sib[late[-1]])
    summed = _unpack(parts_sum, part_shapes)
    loss = summed[0][0]
    gsum = dict(zip(SMALL, summed[1:]))
    dmod_all = _allgather_small(_pack([small_g["ada_b"]]), "gather_dmod", parts_sum)
    dmod_all = dmod_all.reshape(N_DEV, DEPTH, 6 * D)
    dmod_cols = lax.dynamic_slice_in_dim(dmod_all, chip * cols, cols, axis=2).transpose(1, 0, 2)
    g_ada_w = _ada_bwd(c16, jnp.pad(dmod_cols, ((0, 0), (0, 16 - N_DEV), (0, 0))))

    res = {}
    shp = ada_w.shape
    two = lambda t: t.reshape(shp[0] * shp[1], shp[2])
    res["ada_w"] = [t.reshape(shp) for t in _adamw(two(ada_w), [two(g_ada_w)], two(m_ada_w), two(v_ada_w), "adamw_ada_w")]

    def mine(n):
        g = gsum[n]
        if n in SMALL_SHARDED:
            g = lax.dynamic_slice_in_dim(g, chip * w[n].shape[-1], w[n].shape[-1], axis=g.ndim - 1)
        return g.reshape(w[n].shape)

    for k, names in enumerate(([n for n in SMALL if n not in SMALL_WIDE_PADDED], list(SMALL_WIDE_PADDED))):
        outs = _adamw_many([w[n] for n in names], [mine(n) for n in names], [m[n] for n in names], [v[n] for n in names],
                           f"adamw_small{k}")
        for idx, n in enumerate(names):
            res[n] = [outs[part][idx] for part in range(4)]

    for n in BIG:
        res[n] = _adamw_layers(w[n], sums[n], [sib[(n, li)] for li in range(w[n].shape[0])], m[n], v[n], f"adamw_{n}")

    outs = [loss, grad_x]
    for part in range(4):
        outs += [res[n][part] for n in WEIGHTS]
    return tuple(outs)
```

```python
import functools

import jax
import jax.numpy as jnp
from jax import lax
from jax.experimental import pallas as pl
from jax.experimental.pallas import tpu as pltpu

F32 = jnp.float32
BF16 = jnp.bfloat16
D = 1024
EPS = 1e-6
DEPTH = 4
MIXER_OF_LAYER = (0, 1, 2, 0)
S5_G, S5_H, S5_P = 64, 16, 64
S5_NB = 4
S5_BH = S5_H * 16
S5_BP = S5_P * 16
NSTATE = S5_G * S5_P
SG_HEADS, SG_CHUNK = 8, 128
ADAM_LR, ADAM_B1, ADAM_B2, ADAM_EPS, ADAM_WD, ADAM_STEP = 0.001, 0.9, 0.999, 1e-08, 0.01, 10
N_DEV = 8
MESH = pl.DeviceIdType.MESH
LANES = 1024
R_SH1, R_SC1, R_G1, R_SH2, R_SC2, R_G2, R_N1, R_N2 = range(8)


def _dot(a, b):
    return jnp.dot(a, b, preferred_element_type=F32)


def _dot_nt(a, b):
    return lax.dot_general(a, b, (((1,), (1,)), ((), ())), preferred_element_type=F32)


def _dot_tn(a, b):
    return lax.dot_general(a, b, (((0,), (0,)), ((), ())), preferred_element_type=F32)


def _bf(x):
    return x.astype(BF16)


def _sum0(x):
    return jnp.sum(x, axis=0, keepdims=True)


def _params(n_axes, vmem_mb=48):
    return pltpu.CompilerParams(dimension_semantics=("arbitrary",) * n_axes, vmem_limit_bytes=vmem_mb << 20)


def _rows(tm, cols, nt=None):
    if nt is None:
        return pl.BlockSpec((tm, cols), lambda i: (i, 0))
    return pl.BlockSpec((tm, cols), lambda i: (nt - 1 - i, 0))


def _whole(shape):
    nd = len(shape)
    return pl.BlockSpec(shape, lambda *_: (0,) * nd)


def _layer_w(r, c, layer):
    return pl.BlockSpec((None, r, c), lambda *_: (layer, 0, 0), pipeline_mode=pl.Buffered(1))


def _const_w(shape):
    nd = len(shape)
    return pl.BlockSpec(shape, lambda *_: (0,) * nd, pipeline_mode=pl.Buffered(1))


def _call_after(after, body, n_in, *, in_specs, **kw):
    if after is None:
        return pl.pallas_call(body, in_specs=in_specs, **kw), ()

    def body_after(*refs):
        return body(*refs[:n_in], *refs[n_in + 1:])

    return pl.pallas_call(body_after, in_specs=list(in_specs) + [pl.BlockSpec(memory_space=pl.ANY)], **kw), (after,)


def _norm_mod(x, ng, sc, sh):
    r = lax.rsqrt(jnp.mean(x * x, axis=-1, keepdims=True) + EPS)
    xn = x * r
    return (xn * ng) * (1.0 + sc) + sh, xn, r


def _norm_mod_bwd(dh, xn, r, ng, sc):
    dxn = dh * (ng * (1.0 + sc))
    return r * (dxn - xn * jnp.mean(dxn * xn, axis=-1, keepdims=True))


def _shift_down(z, prev8, k):
    row = lax.broadcasted_iota(jnp.int32, z.shape, 0)
    if k == 1:
        return jnp.where(row >= 1, pltpu.roll(z, 1, 0), prev8[7:8])
    return jnp.where(row >= 2, pltpu.roll(z, 2, 0), jnp.where(row == 0, prev8[6:7], prev8[7:8]))


def _shift_up(z, next8, k):
    n = z.shape[0]
    row = lax.broadcasted_iota(jnp.int32, z.shape, 0)
    if k == 1:
        return jnp.where(row <= n - 2, pltpu.roll(z, n - 1, 0), next8[0:1])
    return jnp.where(row <= n - 3, pltpu.roll(z, n - 2, 0), jnp.where(row == n - 2, next8[0:1], next8[1:2]))


def _place():
    x, y, c = lax.axis_index("x"), lax.axis_index("y"), lax.axis_index("c")
    chips = [(1 - x, y), (x, 1 - y), (1 - x, 1 - y)]
    return x, y, c, chips


def _allgather_small(x_shard, name, after=None):
    m_per, n = x_shard.shape

    def body(x_ref, out_ref, send_sems, recv_sems, local_sem):
        x, y, c, chips = _place()
        me, sibling = (x, y, c), (x, y, 1 - c)

        def rows(px, py, pc):
            return out_ref.at[pl.ds((4 * px + 2 * py + pc) * m_per, m_per), :]

        def copy(k, block, to, src=None):
            return pltpu.make_async_remote_copy(
                src_ref=rows(*block) if src is None else src, dst_ref=rows(*block),
                send_sem=send_sems.at[k], recv_sem=recv_sems.at[k], device_id=to, device_id_type=MESH)

        mine = pltpu.make_async_copy(x_ref, rows(*me), local_sem)
        mine.start()
        first = [copy(0, me, sibling, src=x_ref)]
        first += [copy(1 + j, me, (*chip, c), src=x_ref) for j, chip in enumerate(chips)]
        for cp in first:
            cp.start()
        passed = [copy(4 + j, (*chip, c), sibling) for j, chip in enumerate(chips)]
        for j, chip in enumerate(chips):
            copy(1 + j, (*chip, c), me).wait_recv()
            passed[j].start()
        copy(0, sibling, me).wait_recv()
        for j, chip in enumerate(chips):
            copy(4 + j, (*chip, 1 - c), me).wait_recv()
        for cp in first + passed:
            cp.wait_send()
        mine.wait()

    call, tail = _call_after(
        after, body, 1, name=name, out_shape=jax.ShapeDtypeStruct((N_DEV * m_per, n), F32),
        in_specs=[pl.BlockSpec(memory_space=pltpu.VMEM)], out_specs=pl.BlockSpec(memory_space=pltpu.VMEM),
        scratch_shapes=[pltpu.SemaphoreType.DMA((7,)), pltpu.SemaphoreType.DMA((7,)), pltpu.SemaphoreType.DMA],
        compiler_params=pltpu.CompilerParams(vmem_limit_bytes=48 << 20),
    )
    return call(x_shard, *tail)


def _allreduce_small(x_part, name, after=None):
    m, n = x_part.shape
    h = m // 2

    def body(x_ref, out_ref, sib_buf, slots, send_sems, recv_sems):
        x, y, c, chips = _place()
        k_me = 2 * x + y
        sibling = (x, y, 1 - c)
        mine = pl.ds(pl.multiple_of(c * h, 8), h)

        def copy(k, src, dst, to):
            return pltpu.make_async_remote_copy(src_ref=src, dst_ref=dst, send_sem=send_sems.at[k], recv_sem=recv_sems.at[k],
                                                device_id=to, device_id_type=MESH)

        swap = copy(0, x_ref, sib_buf, sibling)
        swap.start()
        swap.wait()
        slots[pl.ds(k_me, 1)] = (x_ref[mine, :] + sib_buf[mine, :])[None]
        my_slot = slots.at[pl.ds(k_me, 1)]
        sends = [copy(1 + j, my_slot, my_slot, (*chip, c)) for j, chip in enumerate(chips)]
        for cp in sends:
            cp.start()
        for j, chip in enumerate(chips):
            their_slot = slots.at[pl.ds(2 * chip[0] + chip[1], 1)]
            copy(1 + j, their_slot, their_slot, (x, y, c)).wait_recv()
        for cp in sends:
            cp.wait_send()
        out_ref[mine, :] = ((slots[0] + slots[1]) + slots[2]) + slots[3]
        give = copy(4, out_ref.at[mine, :], out_ref.at[mine, :], sibling)
        give.start()
        give.wait_send()
        theirs = pl.ds(pl.multiple_of((1 - c) * h, 8), h)
        copy(4, out_ref.at[theirs, :], out_ref.at[theirs, :], (x, y, c)).wait_recv()

    call, tail = _call_after(
        after, body, 1, name=name, out_shape=jax.ShapeDtypeStruct((m, n), F32),
        in_specs=[pl.BlockSpec(memory_space=pltpu.VMEM)], out_specs=pl.BlockSpec(memory_space=pltpu.VMEM),
        scratch_shapes=[pltpu.VMEM((m, n), F32), pltpu.VMEM((4, h, n), F32),
                        pltpu.SemaphoreType.DMA((5,)), pltpu.SemaphoreType.DMA((5,))],
        compiler_params=pltpu.CompilerParams(vmem_limit_bytes=48 << 20),
    )
    return call(x_part, *tail)


def _shard_region(ref, full_shape, axis, chip_k, half=None):
    _, r, c = full_shape
    if axis == 1:
        rs = r // 4
        if half is None:
            return ref.at[:, pl.ds(pl.multiple_of(chip_k * rs, 128), rs), :]
        return ref.at[:, pl.ds(pl.multiple_of(chip_k * rs + half * (rs // 2), 128), rs // 2), :]
    cs = c // 4
    if half is None:
        return ref.at[:, :, pl.ds(pl.multiple_of(chip_k * cs, 128), cs)]
    return ref.at[:, pl.ds(pl.multiple_of(half * (r // 2), 128), r // 2), pl.ds(pl.multiple_of(chip_k * cs, 128), cs)]


HBM_SPEC = pl.BlockSpec(memory_space=pltpu.HBM)
SEM_SPEC = pl.BlockSpec(memory_space=pltpu.SEMAPHORE)
ANY_SPEC = pl.BlockSpec(memory_space=pl.ANY)
SPLIT_COPY_PARAMS = pltpu.CompilerParams(has_side_effects=pltpu.SideEffectType.DATAFLOW_SIDE_EFFECTING)


def _in_hbm(arrs):
    return [pltpu.with_memory_space_constraint(a, pltpu.HBM) for a in arrs]


def _cast_place(w_stack, li, axis, chip, name):
    _, r, c = w_stack.shape
    full = (1, 4 * r, c) if axis == 1 else (1, r, 4 * c)
    tr = min(r, 256)
    if axis == 1:
        out_spec = pl.BlockSpec((None, tr, c), lambda i, k: (0, k[0] * (r // tr) + i, 0))
    else:
        out_spec = pl.BlockSpec((None, tr, c), lambda i, k: (0, i, k[0]))

    def body(k_ref, w_ref, o_ref):
        o_ref[...] = _bf(w_ref[...])

    return pl.pallas_call(
        body, name=name,
        grid_spec=pltpu.PrefetchScalarGridSpec(
            num_scalar_prefetch=1, grid=(r // tr,),
            in_specs=[pl.BlockSpec((None, tr, c), lambda i, k: (li, i, 0))], out_specs=out_spec),
        out_shape=jax.ShapeDtypeStruct(full, BF16),
        compiler_params=_params(1),
    )(chip, w_stack)


def _gather_start(lands, axes, name, after):
    n_arr = len(lands)
    fulls = [tuple(l.shape) for l in lands]

    def body(*refs):
        land = refs[:n_arr]
        send_sems, recv_sems = refs[n_arr + 1:n_arr + 3]
        token = refs[-1]
        x, y, c, chips = _place()
        k_me = 2 * x + y
        for a in range(n_arr):
            mine = _shard_region(land[a], fulls[a], axes[a], k_me, c)
            for j, chip in enumerate(chips):
                pltpu.make_async_remote_copy(
                    src_ref=mine, dst_ref=mine, send_sem=send_sems.at[a * 3 + j], recv_sem=recv_sems.at[a * 3 + j],
                    device_id=(*chip, c), device_id_type=MESH).start()
        token[...] = jnp.zeros_like(token)

    res = pl.pallas_call(
        body, name=name,
        out_shape=(pltpu.SemaphoreType.DMA((3 * n_arr,)), pltpu.SemaphoreType.DMA((3 * n_arr,)),
                   *[pltpu.HBM(f, BF16) for f in fulls], jax.ShapeDtypeStruct((8, 128), F32)),
        in_specs=[HBM_SPEC] * n_arr + [ANY_SPEC],
        out_specs=(SEM_SPEC, SEM_SPEC, *[HBM_SPEC] * n_arr, pl.BlockSpec(memory_space=pltpu.VMEM)),
        input_output_aliases={a: 2 + a for a in range(n_arr)},
        compiler_params=SPLIT_COPY_PARAMS,
    )(*_in_hbm(lands), after)
    return res[0], res[1], list(res[2:2 + n_arr]), res[-1]


def _gather_wait(send_sems, recv_sems, lands, axes, name, after):
    n_arr = len(lands)
    fulls = [tuple(l.shape) for l in lands]

    def body(*refs):
        land = refs[:n_arr]
        s_sems, r_sems = refs[n_arr:n_arr + 2]
        x, y, c, chips = _place()
        for a in range(n_arr):
            for j, chip in enumerate(chips):
                k_j = 2 * chip[0] + chip[1]
                got = _shard_region(land[a], fulls[a], axes[a], k_j, c)
                cp = pltpu.make_async_remote_copy(
                    src_ref=got, dst_ref=got, send_sem=s_sems.at[a * 3 + j], recv_sem=r_sems.at[a * 3 + j],
                    device_id=(x, y, c), device_id_type=MESH)
                cp.wait_send()
                cp.wait_recv()

    res = pl.pallas_call(
        body, name=name,
        out_shape=tuple(pltpu.HBM(f, BF16) for f in fulls),
        in_specs=[HBM_SPEC] * n_arr + [SEM_SPEC, SEM_SPEC, ANY_SPEC],
        out_specs=tuple([HBM_SPEC] * n_arr),
        input_output_aliases={a: a for a in range(n_arr)},
        compiler_params=SPLIT_COPY_PARAMS,
    )(*lands, send_sems, recv_sems, after)
    return list(res)


def _gather_share(lands, axes, name):
    n_arr = len(lands)
    fulls = [tuple(l.shape) for l in lands]

    def body(*refs):
        land_in, land = refs[:n_arr], refs[n_arr:2 * n_arr]
        send_sems, recv_sems = refs[2 * n_arr:]
        x, y, c, chips = _place()
        copies = []
        for a in range(n_arr):
            for j, chip in enumerate(chips):
                k_j = 2 * chip[0] + chip[1]
                cp = pltpu.make_async_remote_copy(
                    src_ref=_shard_region(land_in[a], fulls[a], axes[a], k_j, c),
                    dst_ref=_shard_region(land[a], fulls[a], axes[a], k_j, c),
                    send_sem=send_sems.at[a * 3 + j], recv_sem=recv_sems.at[a * 3 + j],
                    device_id=(x, y, 1 - c), device_id_type=MESH)
                cp.start()
                copies.append(cp)
        for cp in copies:
            cp.wait()

    return pl.pallas_call(
        body, name=name, out_shape=[jax.ShapeDtypeStruct(f, BF16) for f in fulls],
        in_specs=[ANY_SPEC] * n_arr, out_specs=[ANY_SPEC] * n_arr,
        input_output_aliases={a: a for a in range(n_arr)},
        scratch_shapes=[pltpu.SemaphoreType.DMA((3 * n_arr,)), pltpu.SemaphoreType.DMA((3 * n_arr,))],
    )(*lands)


def _scatter_shapes(grads, axes):
    out = []
    for g, ax in zip(grads, axes):
        shp = list(g.shape)
        shp[ax] //= 4
        out.append((3,) + tuple(shp[1:]))
    return out


def _scatter_start(grads, axes, name, after):
    n_arr = len(grads)
    shapes = _scatter_shapes(grads, axes)
    lands = [lax.empty(s, BF16) for s in shapes]

    def body(*refs):
        ins, land = refs[:n_arr], refs[n_arr:2 * n_arr]
        send_sems, recv_sems = refs[2 * n_arr + 1:2 * n_arr + 3]
        token = refs[-1]
        x, y, c, chips = _place()
        for a in range(n_arr):
            for j, chip in enumerate(chips):
                k_j = 2 * chip[0] + chip[1]
                pltpu.make_async_remote_copy(
                    src_ref=_shard_region(ins[a], grads[a].shape, axes[a], k_j), dst_ref=land[a].at[pl.ds(j, 1)],
                    send_sem=send_sems.at[a * 3 + j], recv_sem=recv_sems.at[a * 3 + j],
                    device_id=(*chip, c), device_id_type=MESH).start()
        token[...] = jnp.zeros_like(token)

    res = pl.pallas_call(
        body, name=name,
        out_shape=(pltpu.SemaphoreType.DMA((3 * n_arr,)), pltpu.SemaphoreType.DMA((3 * n_arr,)),
                   *[pltpu.HBM(g.shape, BF16) for g in grads], *[pltpu.HBM(s, BF16) for s in shapes],
                   jax.ShapeDtypeStruct((8, 128), F32)),
        in_specs=[HBM_SPEC] * (2 * n_arr) + [ANY_SPEC],
        out_specs=(SEM_SPEC, SEM_SPEC, *[HBM_SPEC] * (2 * n_arr), pl.BlockSpec(memory_space=pltpu.VMEM)),
        input_output_aliases={a: 2 + a for a in range(2 * n_arr)},
        compiler_params=SPLIT_COPY_PARAMS,
    )(*_in_hbm(grads), *_in_hbm(lands), after)
    return res[0], res[1], list(res[2:2 + n_arr]), list(res[2 + n_arr:2 + 2 * n_arr]), res[-1]


def _scatter_wait(send_sems, recv_sems, grads, lands, axes, name, after):
    n_arr = len(grads)

    def body(*refs):
        ins, land = refs[:n_arr], refs[n_arr:2 * n_arr]
        s_sems, r_sems = refs[2 * n_arr:2 * n_arr + 2]
        x, y, c, chips = _place()
        for a in range(n_arr):
            for j, chip in enumerate(chips):
                k_j = 2 * chip[0] + chip[1]
                cp = pltpu.make_async_remote_copy(
                    src_ref=_shard_region(ins[a], grads[a].shape, axes[a], k_j), dst_ref=land[a].at[pl.ds(j, 1)],
                    send_sem=s_sems.at[a * 3 + j], recv_sem=r_sems.at[a * 3 + j],
                    device_id=(x, y, c), device_id_type=MESH)
                cp.wait_send()
                cp.wait_recv()

    res = pl.pallas_call(
        body, name=name,
        out_shape=(*[pltpu.HBM(g.shape, BF16) for g in grads], *[pltpu.HBM(l.shape, BF16) for l in lands]),
        in_specs=[HBM_SPEC] * (2 * n_arr) + [SEM_SPEC, SEM_SPEC, ANY_SPEC],
        out_specs=tuple([HBM_SPEC] * (2 * n_arr)),
        input_output_aliases={a: a for a in range(2 * n_arr)},
        compiler_params=SPLIT_COPY_PARAMS,
    )(*grads, *lands, send_sems, recv_sems, after)
    return list(res[:n_arr]), list(res[n_arr:])


def _swap_start(arrs, name, after):
    n_arr = len(arrs)
    lands = [lax.empty(a.shape, a.dtype) for a in arrs]

    def body(*refs):
        ins, land = refs[:n_arr], refs[n_arr:2 * n_arr]
        send_sems, recv_sems = refs[2 * n_arr + 1:2 * n_arr + 3]
        token = refs[-1]
        x, y, c, _ = _place()
        for a in range(n_arr):
            pltpu.make_async_remote_copy(
                src_ref=ins[a], dst_ref=land[a], send_sem=send_sems.at[a], recv_sem=recv_sems.at[a],
                device_id=(x, y, 1 - c), device_id_type=MESH).start()
        token[...] = jnp.zeros_like(token)

    res = pl.pallas_call(
        body, name=name,
        out_shape=(pltpu.SemaphoreType.DMA((n_arr,)), pltpu.SemaphoreType.DMA((n_arr,)),
                   *[pltpu.HBM(a.shape, a.dtype) for a in arrs], *[pltpu.HBM(a.shape, a.dtype) for a in arrs],
                   jax.ShapeDtypeStruct((8, 128), F32)),
        in_specs=[HBM_SPEC] * (2 * n_arr) + [ANY_SPEC],
        out_specs=(SEM_SPEC, SEM_SPEC, *[HBM_SPEC] * (2 * n_arr), pl.BlockSpec(memory_space=pltpu.VMEM)),
        input_output_aliases={a: 2 + a for a in range(2 * n_arr)},
        compiler_params=SPLIT_COPY_PARAMS,
    )(*_in_hbm(arrs), *_in_hbm(lands), after)
    return res[0], res[1], list(res[2:2 + n_arr]), list(res[2 + n_arr:2 + 2 * n_arr]), res[-1]


def _swap_wait(send_sems, recv_sems, arrs, lands, name, after):
    n_arr = len(arrs)

    def body(*refs):
        ins, land = refs[:n_arr], refs[n_arr:2 * n_arr]
        s_sems, r_sems = refs[2 * n_arr:2 * n_arr + 2]
        x, y, c, _ = _place()
        for a in range(n_arr):
            cp = pltpu.make_async_remote_copy(
                src_ref=ins[a], dst_ref=land[a], send_sem=s_sems.at[a], recv_sem=r_sems.at[a],
                device_id=(x, y, c), device_id_type=MESH)
            cp.wait_send()
            cp.wait_recv()

    res = pl.pallas_call(
        body, name=name,
        out_shape=(*[pltpu.HBM(a.shape, a.dtype) for a in arrs], *[pltpu.HBM(a.shape, a.dtype) for a in arrs]),
        in_specs=[HBM_SPEC] * (2 * n_arr) + [SEM_SPEC, SEM_SPEC, ANY_SPEC],
        out_specs=tuple([HBM_SPEC] * (2 * n_arr)),
        input_output_aliases={a: a for a in range(2 * n_arr)},
        compiler_params=SPLIT_COPY_PARAMS,
    )(*arrs, *lands, send_sems, recv_sems, after)
    return list(res[:n_arr]), list(res[n_arr:])


def _swap_with_sibling(arrs, name):
    n_arr = len(arrs)

    def body(*refs):
        ins, outs = refs[:n_arr], refs[n_arr:2 * n_arr]
        send_sems, recv_sems = refs[2 * n_arr:]
        x, y, c, _ = _place()
        copies = []
        for a in range(n_arr):
            cp = pltpu.make_async_remote_copy(
                src_ref=ins[a], dst_ref=outs[a], send_sem=send_sems.at[a], recv_sem=recv_sems.at[a],
                device_id=(x, y, 1 - c), device_id_type=MESH)
            cp.start()
            copies.append(cp)
        for cp in copies:
            cp.wait()

    any_spec = pl.BlockSpec(memory_space=pl.ANY)
    return pl.pallas_call(
        body, name=name, out_shape=[jax.ShapeDtypeStruct(a.shape, a.dtype) for a in arrs],
        in_specs=[any_spec] * n_arr, out_specs=[any_spec] * n_arr,
        scratch_shapes=[pltpu.SemaphoreType.DMA((n_arr,)), pltpu.SemaphoreType.DMA((n_arr,))],
    )(*arrs)


def _mm_tn(a, b, name, out_dtype=BF16):
    L, m = a.shape
    n = b.shape[1]
    bm, bn, bk = min(m, 1024), min(n, 1024), min(L, 2048)
    nk = L // bk

    def body(a_ref, b_ref, o_ref, acc):
        k = pl.program_id(2)

        @pl.when(k == 0)
        def _():
            acc[...] = jnp.zeros_like(acc)

        acc[...] += _dot_tn(_bf(a_ref[...]), _bf(b_ref[...]))

        @pl.when(k == nk - 1)
        def _():
            o_ref[...] = acc[...].astype(out_dtype)

    return pl.pallas_call(
        body, name=name, grid=(m // bm, n // bn, nk),
        in_specs=[pl.BlockSpec((bk, bm), lambda i, j, k: (k, i)), pl.BlockSpec((bk, bn), lambda i, j, k: (k, j))],
        out_specs=pl.BlockSpec((bm, bn), lambda i, j, k: (i, j)),
        out_shape=jax.ShapeDtypeStruct((m, n), out_dtype),
        scratch_shapes=[pltpu.VMEM((bm, bn), F32)],
        compiler_params=_params(3),
    )(a, b)


def _mm_tn_blocks(a, b, wa, wb, name, after=None):
    L = a.shape[0]
    nb = a.shape[1] // wa
    bk = min(L, 1024)
    nk = L // bk

    def body(a_ref, b_ref, o_ref):
        @pl.when(pl.program_id(1) == 0)
        def _():
            o_ref[...] = jnp.zeros_like(o_ref)

        o_ref[...] += _dot_tn(_bf(a_ref[...]), _bf(b_ref[...]))

    call, tail = _call_after(
        after, body, 2, name=name, grid=(nb, nk),
        in_specs=[pl.BlockSpec((bk, wa), lambda j, k: (k, j)), pl.BlockSpec((bk, wb), lambda j, k: (k, j))],
        out_specs=pl.BlockSpec((None, wa, wb), lambda j, k: (j, 0, 0)),
        out_shape=jax.ShapeDtypeStruct((nb, wa, wb), F32),
        compiler_params=_params(2),
    )
    return call(a, b, *tail)


def _sum_parts(parts, own, axis, chip, name):
    _, r, c = parts.shape
    tr = min(r, 256)
    if axis == 1:
        own_spec = pl.BlockSpec((None, tr, c), lambda i, k: (0, k[0] * (r // tr) + i, 0))
    else:
        own_spec = pl.BlockSpec((None, tr, c), lambda i, k: (0, i, k[0]))

    def body(k_ref, p_ref, g_ref, o_ref):
        p = p_ref[...].astype(F32)
        o_ref[...] = ((p[0] + p[1]) + p[2]) + g_ref[...].astype(F32)

    return pl.pallas_call(
        body, name=name,
        grid_spec=pltpu.PrefetchScalarGridSpec(
            num_scalar_prefetch=1, grid=(r // tr,),
            in_specs=[pl.BlockSpec((3, tr, c), lambda i, k: (0, i, 0)), own_spec],
            out_specs=pl.BlockSpec((tr, c), lambda i, k: (i, 0))),
        out_shape=jax.ShapeDtypeStruct((r, c), F32),
        compiler_params=_params(1),
    )(chip, parts, own)


def _adamw(w, g_parts, m, v, name):
    n_g = len(g_parts)
    if w.ndim == 2:
        r, c = w.shape
        tr = r
        for cand in (512, 256, 128, 64, 32, 16, 8):
            if r % cand == 0 and cand * c * 4 <= (2 << 20):
                tr = cand
                break
        spec = pl.BlockSpec((tr, c), lambda i: (i, 0))
        tiling = dict(grid=(r // tr,), in_specs=[spec] * (3 + n_g), out_specs=[spec] * 4, compiler_params=_params(1))
    else:
        tiling = dict(compiler_params=pltpu.CompilerParams(vmem_limit_bytes=48 << 20))

    def body(*refs):
        w_ref, g_refs, m_ref, v_ref = refs[0], refs[1:1 + n_g], refs[1 + n_g], refs[2 + n_g]
        g = g_refs[0][...]
        for gr in g_refs[1:]:
            g = g + gr[...]
        _adamw_update(g, w_ref, m_ref, v_ref, *refs[3 + n_g:])

    return pl.pallas_call(body, name=name, out_shape=[jax.ShapeDtypeStruct(w.shape, F32)] * 4, **tiling)(w, *g_parts, m, v)


def _adamw_update(g, w_ref, m_ref, v_ref, g_out, d_out, m_out, v_out):
    m_new = ADAM_B1 * m_ref[...] + (1.0 - ADAM_B1) * g
    v_new = ADAM_B2 * v_ref[...] + (1.0 - ADAM_B2) * (g * g)
    m_hat = m_new * (1.0 / (1.0 - ADAM_B1 ** ADAM_STEP))
    v_hat = v_new * (1.0 / (1.0 - ADAM_B2 ** ADAM_STEP))
    g_out[...] = g
    d_out[...] = -ADAM_LR * (m_hat / (jnp.sqrt(v_hat) + ADAM_EPS) + ADAM_WD * w_ref[...])
    m_out[...] = m_new
    v_out[...] = v_new


def _adamw_many(ws, gs, ms, vs, name):
    n = len(ws)

    def body(*refs):
        for k in range(n):
            _adamw_update(refs[n + k][...], refs[k], refs[2 * n + k], refs[3 * n + k],
                          refs[4 * n + k], refs[5 * n + k], refs[6 * n + k], refs[7 * n + k])

    outs = pl.pallas_call(body, name=name, out_shape=[jax.ShapeDtypeStruct(t.shape, F32) for t in ws] * 4,
                          compiler_params=pltpu.CompilerParams(vmem_limit_bytes=56 << 20))(*ws, *gs, *ms, *vs)
    return [outs[part * n:(part + 1) * n] for part in range(4)]


def _adamw_layers(w, q_mine, q_sib, m, v, name):
    n, r, c = w.shape
    tr = r
    for cand in (512, 256, 128, 64, 32, 16, 8):
        if r % cand == 0 and cand * c * 4 <= (1 << 20):
            tr = cand
            break

    def body(*refs):
        w_ref, qm, qs, m_ref, v_ref = refs[0], refs[1:1 + n], refs[1 + n:1 + 2 * n], refs[1 + 2 * n], refs[2 + 2 * n]
        layer = pl.program_id(0)
        g = qm[0][...] + qs[0][...]
        for k in range(1, n):
            g = jnp.where(layer == k, qm[k][...] + qs[k][...], g)
        _adamw_update(g, w_ref, m_ref, v_ref, *refs[3 + 2 * n:])

    stacked = pl.BlockSpec((None, tr, c), lambda l, i: (l, i, 0))
    per_layer = [pl.BlockSpec((tr, c), lambda l, i, k=k: (jnp.where(l == k, i, 0), 0)) for k in range(n)]
    return pl.pallas_call(
        body, name=name, grid=(n, r // tr),
        in_specs=[stacked] + per_layer + per_layer + [stacked, stacked], out_specs=[stacked] * 4,
        out_shape=[jax.ShapeDtypeStruct(w.shape, F32)] * 4,
        compiler_params=_params(2),
    )(w, *q_mine, *q_sib, m, v)


def _ada_fwd(c16, ada_w, ada_b_cols):
    cols = ada_w.shape[2]

    def body(c_ref, w_ref, b_ref, o_ref):
        cv = c_ref[...]
        ca = _bf(cv * jax.nn.sigmoid(cv))
        o_ref[...] = _dot(ca, _bf(w_ref[...])) + b_ref[...]

    return pl.pallas_call(
        body, name="ada_fwd", grid=(DEPTH,),
        in_specs=[_whole((16, D)), pl.BlockSpec((None, D, cols), lambda i: (i, 0, 0)),
                  pl.BlockSpec((None, 1, cols), lambda i: (i, 0, 0))],
        out_specs=pl.BlockSpec((None, 16, cols), lambda i: (i, 0, 0)),
        out_shape=jax.ShapeDtypeStruct((DEPTH, 16, cols), F32),
        compiler_params=_params(1),
    )(c16, ada_w, ada_b_cols)


def _ada_bwd(c16, dmod16):
    cols = dmod16.shape[2]

    def body(c_ref, d_ref, o_ref):
        cv = c_ref[...]
        ca = _bf(cv * jax.nn.sigmoid(cv))
        o_ref[...] = _dot_tn(ca, _bf(d_ref[...]))

    return pl.pallas_call(
        body, name="ada_bwd", grid=(DEPTH,),
        in_specs=[_whole((16, D)), pl.BlockSpec((None, 16, cols), lambda i: (i, 0, 0))],
        out_specs=pl.BlockSpec((None, D, cols), lambda i: (i, 0, 0)),
        out_shape=jax.ShapeDtypeStruct((DEPTH, D, cols), F32),
        compiler_params=_params(1),
    )(c16, dmod16)


def _mod_bwd(vs_mix, vs_ffn, pv):
    def body(m_ref, f_ref, pv_ref, o_ref):
        for i in range(DEPTH):
            vm, vf, p = m_ref[i], f_ref[i], pv_ref[i]
            o_ref[i] = jnp.concatenate([
                vm[2:3], vm[1:2] * p[R_N1:R_N1 + 1], vm[0:1],
                vf[2:3], vf[1:2] * p[R_N2:R_N2 + 1], vf[0:1],
                vm[1:2] * (1.0 + p[R_SC1:R_SC1 + 1]), vf[1:2] * (1.0 + p[R_SC2:R_SC2 + 1])], axis=0)

    return pl.pallas_call(body, name="mod_bwd", out_shape=jax.ShapeDtypeStruct((DEPTH, 8, D), F32))(vs_mix, vs_ffn, pv)


def _ffn_fwd(x1, pv, w1, w2, layer, tm, after=None):
    L = x1.shape[0]
    dff = w1.shape[2]

    def body(x1_ref, pv_ref, w1_ref, w2_ref, x2_ref, h2_ref, a_ref, f_ref):
        x1v, p = x1_ref[...], pv_ref[...]
        h2, _, _ = _norm_mod(x1v, p[R_N2:R_N2 + 1], p[R_SC2:R_SC2 + 1], p[R_SH2:R_SH2 + 1])
        hb = _bf(h2)
        h2_ref[...] = hb
        a = _dot(hb, w1_ref[...])
        a_ref[...] = a
        ra = jnp.maximum(a, 0.0)
        f = _dot(_bf(ra * ra), w2_ref[...])
        f_ref[...] = f
        x2_ref[...] = x1v + p[R_G2:R_G2 + 1] * f

    call, tail = _call_after(
        after, body, 4, name=f"ffn_fwd{layer}", grid=(L // tm,),
        in_specs=[_rows(tm, D), pl.BlockSpec((None, 8, D), lambda i: (layer, 0, 0)), _layer_w(D, dff, 0), _layer_w(dff, D, 0)],
        out_specs=[_rows(tm, D), _rows(tm, D), _rows(tm, dff), _rows(tm, D)],
        out_shape=[jax.ShapeDtypeStruct((L, D), F32), jax.ShapeDtypeStruct((L, D), BF16),
                   jax.ShapeDtypeStruct((L, dff), F32), jax.ShapeDtypeStruct((L, D), F32)],
        compiler_params=_params(1, 56),
    )
    return call(x1, pv, w1, w2, *tail)


def _ffn_bwd(dx2, x1, a, f, pv, w1, w2, layer, tm, after=None):
    L = x1.shape[0]
    dff = w1.shape[2]
    extra = [] if after is None else [pl.BlockSpec(memory_space=pl.ANY)]
    extra_args = [] if after is None else [after]

    def body(dx2_ref, x1_ref, a_ref, f_ref, pv_ref, w1_ref, w2_ref, *rest):
        dx1_ref, p_ref, da_ref, df_ref, vs_ref = rest[len(extra):]

        @pl.when(pl.program_id(0) == 0)
        def _():
            vs_ref[...] = jnp.zeros_like(vs_ref)

        dx2v, p = dx2_ref[...], pv_ref[...]
        dfb = _bf(dx2v * p[R_G2:R_G2 + 1])
        df_ref[...] = dfb
        vs_ref[0:1, :] += _sum0(dx2v * f_ref[...])
        dp = _dot_nt(dfb, w2_ref[...])
        ra = jnp.maximum(a_ref[...], 0.0)
        p_ref[...] = _bf(ra * ra)
        dab = _bf(dp * (2.0 * ra))
        da_ref[...] = dab
        dh2 = _dot_nt(dab, w1_ref[...])
        _, xn, r = _norm_mod(x1_ref[...], p[R_N2:R_N2 + 1], p[R_SC2:R_SC2 + 1], p[R_SH2:R_SH2 + 1])
        dx1_ref[...] = dx2v + _norm_mod_bwd(dh2, xn, r, p[R_N2:R_N2 + 1], p[R_SC2:R_SC2 + 1])
        vs_ref[1:2, :] += _sum0(dh2 * xn)
        vs_ref[2:3, :] += _sum0(dh2)

    return pl.pallas_call(
        body, name=f"ffn_bwd{layer}", grid=(L // tm,),
        in_specs=[_rows(tm, D), _rows(tm, D), _rows(tm, dff), _rows(tm, D),
                  pl.BlockSpec((None, 8, D), lambda i: (layer, 0, 0)), _layer_w(D, dff, 0), _layer_w(dff, D, 0)] + extra,
        out_specs=[_rows(tm, D), _rows(tm, dff), _rows(tm, dff), _rows(tm, D), _whole((8, D))],
        out_shape=[jax.ShapeDtypeStruct((L, D), F32), jax.ShapeDtypeStruct((L, dff), BF16),
                   jax.ShapeDtypeStruct((L, dff), BF16), jax.ShapeDtypeStruct((L, D), BF16),
                   jax.ShapeDtypeStruct((8, D), F32)],
        compiler_params=_params(1, 56),
    )(dx2, x1, a, f, pv, w1, w2, *extra_args)


def _conv_fwd(x, pv, w_in, w_out, cw, layer, j, tm, after=None):
    L = x.shape[0]

    def body(x_ref, pv_ref, win_ref, wout_ref, cw_ref, x1_ref, h_ref, bcx_ref, conv_ref, q_ref, y_ref, carry):
        @pl.when(pl.program_id(0) == 0)
        def _():
            carry[...] = jnp.zeros_like(carry)

        xv, p, cwv = x_ref[...], pv_ref[...], cw_ref[...]
        h, _, _ = _norm_mod(xv, p[R_N1:R_N1 + 1], p[R_SC1:R_SC1 + 1], p[R_SH1:R_SH1 + 1])
        hb = _bf(h)
        h_ref[...] = hb
        bcx = _dot(hb, win_ref[...])
        bcx_ref[...] = bcx
        z = bcx[:, D:2 * D] * bcx[:, 2 * D:]
        prev8 = carry[...]
        conv = cwv[0:1] * _shift_down(z, prev8, 2) + cwv[1:2] * _shift_down(z, prev8, 1) + cwv[2:3] * z + cwv[3:4]
        conv_ref[...] = conv
        qb = _bf(bcx[:, :D] * conv)
        q_ref[...] = qb
        y = _dot(qb, wout_ref[...])
        y_ref[...] = y
        x1_ref[...] = xv + p[R_G1:R_G1 + 1] * y
        carry[...] = z[tm - 8:tm]

    call, tail = _call_after(
        after, body, 5, name=f"conv_fwd{layer}", grid=(L // tm,),
        in_specs=[_rows(tm, D), pl.BlockSpec((None, 8, D), lambda i: (layer, 0, 0)), _layer_w(D, 3 * D, 0), _layer_w(D, D, 0),
                  pl.BlockSpec((None, 8, D), lambda i: (j, 0, 0))],
        out_specs=[_rows(tm, D), _rows(tm, D), _rows(tm, 3 * D), _rows(tm, D), _rows(tm, D), _rows(tm, D)],
        out_shape=[jax.ShapeDtypeStruct((L, D), F32), jax.ShapeDtypeStruct((L, D), BF16), jax.ShapeDtypeStruct((L, 3 * D), F32),
                   jax.ShapeDtypeStruct((L, D), F32), jax.ShapeDtypeStruct((L, D), BF16), jax.ShapeDtypeStruct((L, D), F32)],
        scratch_shapes=[pltpu.VMEM((8, D), F32)],
        compiler_params=_params(1, 56),
    )
    return call(x, pv, w_in, w_out, cw, *tail)


def _conv_bwd(dx1, x, y, bcx, conv, pv, w_in, w_out, cw, layer, j, tm, after=None):
    L = x.shape[0]
    nt = L // tm

    def body(dx1_ref, x_ref, y_ref, bcx_ref, conv_ref, halo_ref, pv_ref, win_ref, wout_ref, cw_ref,
             dx_ref, dbcx_ref, dy_ref, vs_ref, carry):
        gi = pl.program_id(0)
        tile = nt - 1 - gi

        @pl.when(gi == 0)
        def _():
            vs_ref[...] = jnp.zeros_like(vs_ref)
            carry[...] = jnp.zeros_like(carry)

        dx1v, p, cwv = dx1_ref[...], pv_ref[...], cw_ref[...]
        dyb = _bf(dx1v * p[R_G1:R_G1 + 1])
        dy_ref[...] = dyb
        vs_ref[0:1, :] += _sum0(dx1v * y_ref[...])
        dq = _dot_nt(dyb, wout_ref[...])
        bcx = bcx_ref[...]
        b, cg, xh = bcx[:, :D], bcx[:, D:2 * D], bcx[:, 2 * D:]
        db = dq * conv_ref[...]
        dc = dq * b
        z = cg * xh
        halo = halo_ref[...]
        zprev = jnp.where(tile > 0, halo[:, D:2 * D] * halo[:, 2 * D:], 0.0)
        vs_ref[3:4, :] += _sum0(dc * _shift_down(z, zprev, 2))
        vs_ref[4:5, :] += _sum0(dc * _shift_down(z, zprev, 1))
        vs_ref[5:6, :] += _sum0(dc * z)
        vs_ref[6:7, :] += _sum0(dc)
        next8 = carry[...]
        dz = cwv[2:3] * dc + cwv[1:2] * _shift_up(dc, next8, 1) + cwv[0:1] * _shift_up(dc, next8, 2)
        dbb, dcgb, dxhb = _bf(db), _bf(dz * xh), _bf(dz * cg)
        dbcx_ref[:, 0:D] = dbb
        dbcx_ref[:, D:2 * D] = dcgb
        dbcx_ref[:, 2 * D:3 * D] = dxhb
        dh = (_dot_nt(dbb, win_ref[:, 0:D]) + _dot_nt(dcgb, win_ref[:, D:2 * D])) + _dot_nt(dxhb, win_ref[:, 2 * D:3 * D])
        _, xn, r = _norm_mod(x_ref[...], p[R_N1:R_N1 + 1], p[R_SC1:R_SC1 + 1], p[R_SH1:R_SH1 + 1])
        dx_ref[...] = dx1v + _norm_mod_bwd(dh, xn, r, p[R_N1:R_N1 + 1], p[R_SC1:R_SC1 + 1])
        vs_ref[1:2, :] += _sum0(dh * xn)
        vs_ref[2:3, :] += _sum0(dh)
        carry[...] = dc[0:8]

    halo_spec = pl.BlockSpec((8, 3 * D), lambda i: (jnp.maximum((nt - 1 - i) * (tm // 8) - 1, 0), 0))
    call, tail = _call_after(
        after, body, 10, name=f"conv_bwd{layer}", grid=(nt,),
        in_specs=[_rows(tm, D, nt), _rows(tm, D, nt), _rows(tm, D, nt), _rows(tm, 3 * D, nt), _rows(tm, D, nt), halo_spec,
                  pl.BlockSpec((None, 8, D), lambda i: (layer, 0, 0)), _layer_w(D, 3 * D, 0), _layer_w(D, D, 0),
                  pl.BlockSpec((None, 8, D), lambda i: (j, 0, 0))],
        out_specs=[_rows(tm, D, nt), _rows(tm, 3 * D, nt), _rows(tm, D, nt), _whole((8, D))],
        out_shape=[jax.ShapeDtypeStruct((L, D), F32), jax.ShapeDtypeStruct((L, 3 * D), BF16),
                   jax.ShapeDtypeStruct((L, D), BF16), jax.ShapeDtypeStruct((8, D), F32)],
        scratch_shapes=[pltpu.VMEM((8, D), F32)],
        compiler_params=_params(1, 56),
    )
    return call(dx1, x, y, bcx, conv, bcx, pv, w_in, w_out, cw, *tail)


def _s5_discretize(a_re, a_im, log_dt, bt_re, bt_im):
    dt = jnp.exp(log_dt)
    mag = jnp.exp(a_re * dt)
    abar_re = mag * jnp.cos(a_im * dt)
    abar_im = mag * jnp.sin(a_im * dt)
    den = a_re * a_re + a_im * a_im
    nr = abar_re - 1.0
    ni = abar_im
    f_re = (nr * a_re + ni * a_im) / den
    f_im = (ni * a_re - nr * a_im) / den
    bbar_re = f_re * bt_re - f_im * bt_im
    bbar_im = f_re * bt_im + f_im * bt_re
    return abar_re, abar_im, bbar_re, bbar_im


def _s5_params_fwd(a_re, a_im, log_dt, bt_re, bt_im):
    def body(ar, ai, ld, br, bi, o_ar, o_ai, o_br, o_bi):
        r = _s5_discretize(ar[...], ai[...], ld[...], br[...], bi[...])
        o_ar[...], o_ai[...], o_br[...], o_bi[...] = r

    gp = jax.ShapeDtypeStruct((S5_G, S5_P), F32)
    hgp = jax.ShapeDtypeStruct((S5_H, S5_G, S5_P), F32)
    return pl.pallas_call(body, name="s5_params_fwd", out_shape=[gp, gp, hgp, hgp])(a_re, a_im, log_dt, bt_re, bt_im)


def _s5_params_bwd(a_re, a_im, log_dt, bt_re, bt_im, d_ar, d_ai, d_br, d_bi):
    def body(ar, ai, ld, br, bi, gar, gai, gbr, gbi, o_ar, o_ai, o_ld, o_br, o_bi):
        _, vjp = jax.vjp(_s5_discretize, ar[...], ai[...], ld[...], br[...], bi[...])
        r = vjp((gar[...], gai[...], gbr[...], gbi[...]))
        o_ar[...], o_ai[...], o_ld[...], o_br[...], o_bi[...] = r

    gp = jax.ShapeDtypeStruct((S5_G, S5_P), F32)
    hgp = jax.ShapeDtypeStruct((S5_H, S5_G, S5_P), F32)
    return pl.pallas_call(body, name="s5_params_bwd", out_shape=[gp, gp, jax.ShapeDtypeStruct((S5_G, 1), F32), hgp, hgp])(
        a_re, a_im, log_dt, bt_re, bt_im, d_ar, d_ai, d_br, d_bi)


NSEG = 8
SCAN_LANES = 1024


def _to_segments(x):
    n, c = x.shape
    return x.reshape(NSEG, n // NSEG, c).transpose(1, 0, 2).reshape(n, c)


def _from_segments(x):
    n, c = x.shape
    return x.reshape(n // NSEG, NSEG, c).transpose(1, 0, 2).reshape(n, c)


def _segment_scan(re_ref, im_ref, st_re, st_im, a_re, a_im, n_slabs, adjoint, write):
    for q in range(NSTATE // SCAN_LANES):
        ls = slice(q * SCAN_LANES, (q + 1) * SCAN_LANES)
        ar = jnp.broadcast_to(a_re[:, ls], (8, SCAN_LANES))
        ai = jnp.broadcast_to(a_im[:, ls], (8, SCAN_LANES))

        def step(k, carry, ls=ls, ar=ar, ai=ai):
            s_r, s_i = carry
            slab = (n_slabs - 1 - k) if adjoint else k
            rows = pl.ds(pl.multiple_of(slab * 8, 8), 8)
            b_r, b_i = re_ref[rows, ls], im_ref[rows, ls]
            if adjoint:
                n_r = b_r + ar * s_r + ai * s_i
                n_i = b_i - ai * s_r + ar * s_i
            else:
                n_r = ar * s_r - ai * s_i + b_r
                n_i = ar * s_i + ai * s_r + b_i
            if write:
                re_ref[rows, ls] = n_r
                im_ref[rows, ls] = n_i
            return n_r, n_i

        s_r, s_i = lax.fori_loop(0, n_slabs, step, (st_re[:, ls], st_im[:, ls]), unroll=4)
        st_re[:, ls] = s_r
        st_im[:, ls] = s_i


def _s5_segment_states(e_re, e_im, ar, ai, seg_len, adjoint):
    def body(ere_ref, eim_ref, ar_ref, ai_ref, ore_ref, oim_ref):
        p_r, p_i = ar_ref[...], ai_ref[...]
        if adjoint:
            p_i = -p_i
        acc_r, acc_i = jnp.ones_like(p_r), jnp.zeros_like(p_r)
        n = seg_len
        while n:
            if n & 1:
                acc_r, acc_i = acc_r * p_r - acc_i * p_i, acc_r * p_i + acc_i * p_r
            n >>= 1
            if n:
                p_r, p_i = p_r * p_r - p_i * p_i, 2.0 * p_r * p_i
        e_r, e_i = ere_ref[...], eim_ref[...]
        s_r, s_i = jnp.zeros_like(acc_r), jnp.zeros_like(acc_r)
        rows_r, rows_i = [None] * NSEG, [None] * NSEG
        order = range(NSEG - 1, -1, -1) if adjoint else range(NSEG)
        for j in order:
            rows_r[j], rows_i[j] = s_r, s_i
            s_r, s_i = (acc_r * s_r - acc_i * s_i + e_r[j:j + 1], acc_r * s_i + acc_i * s_r + e_i[j:j + 1])
        ore_ref[...] = jnp.concatenate(rows_r, axis=0)
        oim_ref[...] = jnp.concatenate(rows_i, axis=0)

    st = jax.ShapeDtypeStruct((NSEG, NSTATE), F32)
    return pl.pallas_call(body, name="s5_segment_states_bwd" if adjoint else "s5_segment_states_fwd", out_shape=[st, st])(
        e_re, e_im, ar, ai)


def _s5_fwd_ends(x, pv, w_in, b_re, b_im, ar, ai, layer, tm, after=None):
    L = x.shape[0]

    def body(x_ref, pv_ref, win_ref, bre_ref, bim_ref, ar_ref, ai_ref, h_ref, u_ref, ere_ref, eim_ref, bu_re, bu_im):
        @pl.when(pl.program_id(0) == 0)
        def _():
            ere_ref[...] = jnp.zeros_like(ere_ref)
            eim_ref[...] = jnp.zeros_like(eim_ref)

        p = pv_ref[...]
        h, _, _ = _norm_mod(x_ref[...], p[R_N1:R_N1 + 1], p[R_SC1:R_SC1 + 1], p[R_SH1:R_SH1 + 1])
        hb = _bf(h)
        h_ref[...] = hb
        u = _dot(hb, win_ref[...])
        u_ref[...] = u
        ub = _bf(u)
        for k in range(S5_NB):
            uk = ub[:, k * S5_BH:(k + 1) * S5_BH]
            bu_re[:, k * S5_BP:(k + 1) * S5_BP] = _dot(uk, bre_ref[k])
            bu_im[:, k * S5_BP:(k + 1) * S5_BP] = _dot(uk, bim_ref[k])
        _segment_scan(bu_re, bu_im, ere_ref, eim_ref, ar_ref[...], ai_ref[...], tm // 8, adjoint=False, write=False)

    call, tail = _call_after(
        after, body, 7, name="s5_fwd_ends", grid=(L // tm,),
        in_specs=[_rows(tm, D), pl.BlockSpec((None, 8, D), lambda i: (layer, 0, 0)), _layer_w(D, D, 0),
                  _const_w((S5_NB, S5_BH, S5_BP)), _const_w((S5_NB, S5_BH, S5_BP)), _whole((1, NSTATE)), _whole((1, NSTATE))],
        out_specs=[_rows(tm, D), _rows(tm, D), _whole((NSEG, NSTATE)), _whole((NSEG, NSTATE))],
        out_shape=[jax.ShapeDtypeStruct((L, D), BF16), jax.ShapeDtypeStruct((L, D), F32),
                   jax.ShapeDtypeStruct((NSEG, NSTATE), F32), jax.ShapeDtypeStruct((NSEG, NSTATE), F32)],
        scratch_shapes=[pltpu.VMEM((tm, NSTATE), F32), pltpu.VMEM((tm, NSTATE), F32)],
        compiler_params=_params(1, 56),
    )
    return call(x, pv, w_in, b_re, b_im, ar, ai, *tail)


def _s5_fwd_out(x, u, pv, b_re, b_im, s0_re, s0_im, ar, ai, c_re, c_im, dvec, glu_w, glu_b, w_out, layer, tm):
    L = x.shape[0]

    def body(x_ref, u_ref, pv_ref, bre_ref, bim_ref, s0re_ref, s0im_ref, ar_ref, ai_ref, cre_ref, cim_ref, d_ref, gw_ref,
             gb_ref, wout_ref, x1_ref, sre_ref, sim_ref, y1_ref, zg_ref, y3_ref, y_ref, st_re, st_im):
        @pl.when(pl.program_id(0) == 0)
        def _():
            st_re[...] = s0re_ref[...]
            st_im[...] = s0im_ref[...]

        p = pv_ref[...]
        uv = u_ref[...]
        ub = _bf(uv)
        for k in range(S5_NB):
            uk = ub[:, k * S5_BH:(k + 1) * S5_BH]
            sre_ref[:, k * S5_BP:(k + 1) * S5_BP] = _dot(uk, bre_ref[k])
            sim_ref[:, k * S5_BP:(k + 1) * S5_BP] = _dot(uk, bim_ref[k])
        _segment_scan(sre_ref, sim_ref, st_re, st_im, ar_ref[...], ai_ref[...], tm // 8, adjoint=False, write=True)
        parts = []
        for k in range(S5_NB):
            sl = slice(k * S5_BP, (k + 1) * S5_BP)
            parts.append(_dot(_bf(sre_ref[:, sl]), cre_ref[k]) - _dot(_bf(sim_ref[:, sl]), cim_ref[k]))
        y1 = jnp.concatenate(parts, axis=1) + d_ref[...] * uv
        y1_ref[...] = y1
        y2 = jax.nn.gelu(y1)
        zg = _dot(_bf(y2), gw_ref[...]) + gb_ref[...]
        zg_ref[...] = zg
        y3b = _bf(y2 * jax.nn.sigmoid(zg))
        y3_ref[...] = y3b
        y = _dot(y3b, wout_ref[...])
        y_ref[...] = y
        x1_ref[...] = x_ref[...] + p[R_G1:R_G1 + 1] * y

    return pl.pallas_call(
        body, name="s5_fwd_out", grid=(L // tm,),
        in_specs=[_rows(tm, D), _rows(tm, D), pl.BlockSpec((None, 8, D), lambda i: (layer, 0, 0)),
                  _const_w((S5_NB, S5_BH, S5_BP)), _const_w((S5_NB, S5_BH, S5_BP)),
                  _whole((NSEG, NSTATE)), _whole((NSEG, NSTATE)), _whole((1, NSTATE)), _whole((1, NSTATE)),
                  _const_w((S5_NB, S5_BP, S5_BH)), _const_w((S5_NB, S5_BP, S5_BH)), _whole((1, D)),
                  _layer_w(D, D, 0), _whole((1, D)), _layer_w(D, D, 0)],
        out_specs=[_rows(tm, D), _rows(tm, NSTATE), _rows(tm, NSTATE), _rows(tm, D), _rows(tm, D), _rows(tm, D), _rows(tm, D)],
        out_shape=[jax.ShapeDtypeStruct((L, D), F32), jax.ShapeDtypeStruct((L, NSTATE), F32), jax.ShapeDtypeStruct((L, NSTATE), F32),
                   jax.ShapeDtypeStruct((L, D), F32), jax.ShapeDtypeStruct((L, D), F32),
                   jax.ShapeDtypeStruct((L, D), BF16), jax.ShapeDtypeStruct((L, D), F32)],
        scratch_shapes=[pltpu.VMEM((NSEG, NSTATE), F32), pltpu.VMEM((NSEG, NSTATE), F32)],
        compiler_params=_params(1, 56),
    )(x, u, pv, b_re, b_im, s0_re, s0_im, ar, ai, c_re, c_im, dvec, glu_w, glu_b, w_out)


def _s5_bwd_ends(dx1, y, y1, zg, u, pv, c_re, c_im, ar, ai, dvec, glu_w, w_out, layer, tm, after=None):
    L = dx1.shape[0]
    nt = L // tm

    def body(dx1_ref, y_ref, y1_ref, zg_ref, u_ref, pv_ref, cre_ref, cim_ref, ar_ref, ai_ref, d_ref, gw_ref, wout_ref,
             dy_ref, y2_ref, dzg_ref, dy1_ref, dus_ref, ere_ref, eim_ref, vs_ref, g_re, g_im):
        @pl.when(pl.program_id(0) == 0)
        def _():
            vs_ref[...] = jnp.zeros_like(vs_ref)
            ere_ref[...] = jnp.zeros_like(ere_ref)
            eim_ref[...] = jnp.zeros_like(eim_ref)

        dx1v, p = dx1_ref[...], pv_ref[...]
        dyb = _bf(dx1v * p[R_G1:R_G1 + 1])
        dy_ref[...] = dyb
        vs_ref[0:1, :] += _sum0(dx1v * y_ref[...])
        dy3 = _dot_nt(dyb, wout_ref[...])
        y2, gelu_vjp = jax.vjp(jax.nn.gelu, y1_ref[...])
        y2_ref[...] = _bf(y2)
        gate = jax.nn.sigmoid(zg_ref[...])
        dzg = dy3 * y2 * gate * (1.0 - gate)
        dzgb = _bf(dzg)
        dzg_ref[...] = dzgb
        vs_ref[1:2, :] += _sum0(dzg)
        dy2 = dy3 * gate + _dot_nt(dzgb, gw_ref[...])
        dy1 = gelu_vjp(dy2)[0]
        vs_ref[2:3, :] += _sum0(dy1 * u_ref[...])
        dus_ref[...] = dy1 * d_ref[...]
        dy1b = _bf(dy1)
        dy1_ref[...] = dy1b
        for k in range(S5_NB):
            dk = dy1b[:, k * S5_BH:(k + 1) * S5_BH]
            g_re[:, k * S5_BP:(k + 1) * S5_BP] = _dot_nt(dk, cre_ref[k])
            g_im[:, k * S5_BP:(k + 1) * S5_BP] = -_dot_nt(dk, cim_ref[k])
        _segment_scan(g_re, g_im, ere_ref, eim_ref, ar_ref[...], ai_ref[...], tm // 8, adjoint=True, write=False)

    call, tail = _call_after(
        after, body, 13, name="s5_bwd_ends", grid=(nt,),
        in_specs=[_rows(tm, D, nt)] * 5 + [pl.BlockSpec((None, 8, D), lambda i: (layer, 0, 0)),
                  _const_w((S5_NB, S5_BP, S5_BH)), _const_w((S5_NB, S5_BP, S5_BH)), _whole((1, NSTATE)), _whole((1, NSTATE)),
                  _whole((1, D)), _layer_w(D, D, 0), _layer_w(D, D, 0)],
        out_specs=[_rows(tm, D, nt)] * 5 + [_whole((NSEG, NSTATE)), _whole((NSEG, NSTATE)), _whole((8, D))],
        out_shape=[jax.ShapeDtypeStruct((L, D), BF16)] * 4 + [jax.ShapeDtypeStruct((L, D), F32),
                   jax.ShapeDtypeStruct((NSEG, NSTATE), F32), jax.ShapeDtypeStruct((NSEG, NSTATE), F32),
                   jax.ShapeDtypeStruct((8, D), F32)],
        scratch_shapes=[pltpu.VMEM((tm, NSTATE), F32), pltpu.VMEM((tm, NSTATE), F32)],
        compiler_params=_params(1, 56),
    )
    return call(dx1, y, y1, zg, u, pv, c_re, c_im, ar, ai, dvec, glu_w, w_out, *tail)


def _s5_bwd_in(dx1, dy1_b, du_skip, x, s_re, s_im, pv, b_re, b_im, c_re, c_im, l0_re, l0_im, ar, ai, w_in, layer, tm):
    L = x.shape[0]
    nt = L // tm

    def body(dx1_ref, dy1_ref, dus_ref, x_ref, sre_ref, sim_ref, hre_ref, him_ref, lre_ref, lim_ref, pv_ref, bre_ref, bim_ref,
             cre_ref, cim_ref, l0re_ref, l0im_ref, ar_ref, ai_ref, win_ref,
             dx_ref, du_ref, lamre_ref, lamim_ref, da_ref, vs_ref, g_re, g_im, st_re, st_im):
        gi = pl.program_id(0)
        tile = nt - 1 - gi

        @pl.when(gi == 0)
        def _():
            vs_ref[...] = jnp.zeros_like(vs_ref)
            da_ref[...] = jnp.zeros_like(da_ref)
            st_re[...] = l0re_ref[...]
            st_im[...] = l0im_ref[...]

        p = pv_ref[...]
        dy1b = dy1_ref[...]
        for k in range(S5_NB):
            dk = dy1b[:, k * S5_BH:(k + 1) * S5_BH]
            g_re[:, k * S5_BP:(k + 1) * S5_BP] = _dot_nt(dk, cre_ref[k])
            g_im[:, k * S5_BP:(k + 1) * S5_BP] = -_dot_nt(dk, cim_ref[k])
        _segment_scan(g_re, g_im, st_re, st_im, ar_ref[...], ai_ref[...], tm // 8, adjoint=True, write=True)
        lam_r, lam_i = g_re[...], g_im[...]
        lrb, lib = _bf(lam_r), _bf(lam_i)
        lamre_ref[...] = lrb
        lamim_ref[...] = lib

        def wrapped(last_ref):
            z = last_ref[...]
            row = lax.broadcasted_iota(jnp.int32, z.shape, 0)
            return jnp.where(row >= 1, pltpu.roll(z, 1, 0), 0.0)

        first_r = jnp.where(tile > 0, hre_ref[...], wrapped(lre_ref))
        first_i = jnp.where(tile > 0, him_ref[...], wrapped(lim_ref))
        sp_r = jnp.concatenate([first_r, sre_ref[0:tm - 8, :]], axis=0)
        sp_i = jnp.concatenate([first_i, sim_ref[0:tm - 8, :]], axis=0)
        da_ref[0:1, :] += _sum0(lam_r * sp_r + lam_i * sp_i)
        da_ref[1:2, :] += _sum0(lam_i * sp_r - lam_r * sp_i)

        parts = []
        for k in range(S5_NB):
            sl = slice(k * S5_BP, (k + 1) * S5_BP)
            parts.append(_dot_nt(lrb[:, sl], bre_ref[k]) + _dot_nt(lib[:, sl], bim_ref[k]))
        dub = _bf(jnp.concatenate(parts, axis=1) + dus_ref[...])
        du_ref[...] = dub
        dh = _dot_nt(dub, win_ref[...])
        _, xn, r = _norm_mod(x_ref[...], p[R_N1:R_N1 + 1], p[R_SC1:R_SC1 + 1], p[R_SH1:R_SH1 + 1])
        dx_ref[...] = dx1_ref[...] + _norm_mod_bwd(dh, xn, r, p[R_N1:R_N1 + 1], p[R_SC1:R_SC1 + 1])
        vs_ref[1:2, :] += _sum0(dh * xn)
        vs_ref[2:3, :] += _sum0(dh)

    halo = pl.BlockSpec((8, NSTATE), lambda i: (jnp.maximum((nt - 1 - i) * (tm // 8) - 1, 0), 0))
    last = pl.BlockSpec((8, NSTATE), lambda i: (L // 8 - 1, 0))
    return pl.pallas_call(
        body, name="s5_bwd_in", grid=(nt,),
        in_specs=[_rows(tm, D, nt), _rows(tm, D, nt), _rows(tm, D, nt), _rows(tm, D, nt), _rows(tm, NSTATE, nt), _rows(tm, NSTATE, nt),
                  halo, halo, last, last, pl.BlockSpec((None, 8, D), lambda i: (layer, 0, 0)),
                  _const_w((S5_NB, S5_BH, S5_BP)), _const_w((S5_NB, S5_BH, S5_BP)),
                  _const_w((S5_NB, S5_BP, S5_BH)), _const_w((S5_NB, S5_BP, S5_BH)),
                  _whole((NSEG, NSTATE)), _whole((NSEG, NSTATE)), _whole((1, NSTATE)), _whole((1, NSTATE)), _layer_w(D, D, 0)],
        out_specs=[_rows(tm, D, nt), _rows(tm, D, nt), _rows(tm, NSTATE, nt), _rows(tm, NSTATE, nt), _whole((8, NSTATE)), _whole((8, D))],
        out_shape=[jax.ShapeDtypeStruct((L, D), F32), jax.ShapeDtypeStruct((L, D), BF16),
                   jax.ShapeDtypeStruct((L, NSTATE), BF16), jax.ShapeDtypeStruct((L, NSTATE), BF16),
                   jax.ShapeDtypeStruct((8, NSTATE), F32), jax.ShapeDtypeStruct((8, D), F32)],
        scratch_shapes=[pltpu.VMEM((tm, NSTATE), F32), pltpu.VMEM((tm, NSTATE), F32),
                        pltpu.VMEM((NSEG, NSTATE), F32), pltpu.VMEM((NSEG, NSTATE), F32)],
        compiler_params=_params(1, 60),
    )(dx1, dy1_b, du_skip, x, s_re, s_im, s_re, s_im, s_re, s_im, pv, b_re, b_im, c_re, c_im, l0_re, l0_im, ar, ai, w_in)


def _blockdiag_b(bt):
    b = bt.reshape(S5_H, S5_NB, 16, S5_P).transpose(1, 2, 0, 3)
    eye = jnp.eye(16, dtype=bt.dtype)
    return (b[:, :, :, None, :] * eye[None, :, None, :, None]).reshape(S5_NB, S5_BH, S5_BP)


def _unblock_b(d):
    d = jnp.einsum("bghgp->bghp", d.reshape(S5_NB, 16, S5_H, 16, S5_P))
    return d.transpose(2, 0, 1, 3).reshape(S5_H, S5_G, S5_P)


def _blockdiag_c(cm):
    c4 = cm.reshape(S5_NB, 16, S5_H, S5_P)
    eye = jnp.eye(16, dtype=cm.dtype)
    out = c4.transpose(0, 1, 3, 2)[:, :, :, None, :] * eye[None, :, None, :, None]
    return out.reshape(S5_NB, S5_BP, S5_BH)


def _unblock_c(d):
    d = jnp.einsum("bgpgh->bghp", d.reshape(S5_NB, 16, S5_P, 16, S5_H))
    return d.reshape(S5_G, S5_H, S5_P)


def _tril_mask():
    return lax.broadcasted_iota(jnp.int32, (SG_CHUNK, SG_CHUNK), 0) >= lax.broadcasted_iota(jnp.int32, (SG_CHUNK, SG_CHUNK), 1)


def _sg_fwd(x, pv, w_in, w_s, b_t, vg, w_out, layer, tm, after=None):
    L = x.shape[0]
    nc = tm // SG_CHUNK

    def body(x_ref, pv_ref, win_ref, ws_ref, bt_ref, vg_ref, wout_ref, x1_ref, h_ref, uv_ref, vm_ref, q_ref, y_ref):
        xv, p = x_ref[...], pv_ref[...]
        h, _, _ = _norm_mod(xv, p[R_N1:R_N1 + 1], p[R_SC1:R_SC1 + 1], p[R_SH1:R_SH1 + 1])
        hb = _bf(h)
        h_ref[...] = hb
        uv = _dot(hb, win_ref[...])
        uv_ref[...] = uv
        v = uv[:, D:]
        rv = lax.rsqrt(jnp.mean(v * v, axis=-1, keepdims=True) + EPS)
        vnb = _bf((v * rv) * vg_ref[...])
        mask = _tril_mask()
        bt = bt_ref[...]
        for hd in range(SG_HEADS):
            wm = _bf(jnp.where(mask, ws_ref[hd], 0.0))
            cs = slice(hd * SG_CHUNK, (hd + 1) * SG_CHUNK)
            for ck in range(nc):
                rs = slice(ck * SG_CHUNK, (ck + 1) * SG_CHUNK)
                vm_ref[rs, cs] = _dot(wm, vnb[rs, cs]) + bt[:, hd:hd + 1]
        qb = _bf(uv[:, :D] * vm_ref[...])
        q_ref[...] = qb
        y = _dot(qb, wout_ref[...])
        y_ref[...] = y
        x1_ref[...] = xv + p[R_G1:R_G1 + 1] * y

    call, tail = _call_after(
        after, body, 7, name="sg_fwd", grid=(L // tm,),
        in_specs=[_rows(tm, D), pl.BlockSpec((None, 8, D), lambda i: (layer, 0, 0)), _layer_w(D, 2 * D, 0),
                  _whole((SG_HEADS, SG_CHUNK, SG_CHUNK)), _whole((SG_CHUNK, SG_HEADS)), _whole((1, D)), _layer_w(D, D, 0)],
        out_specs=[_rows(tm, D), _rows(tm, D), _rows(tm, 2 * D), _rows(tm, D), _rows(tm, D), _rows(tm, D)],
        out_shape=[jax.ShapeDtypeStruct((L, D), F32), jax.ShapeDtypeStruct((L, D), BF16), jax.ShapeDtypeStruct((L, 2 * D), F32),
                   jax.ShapeDtypeStruct((L, D), F32), jax.ShapeDtypeStruct((L, D), BF16), jax.ShapeDtypeStruct((L, D), F32)],
        compiler_params=_params(1, 56),
    )
    return call(x, pv, w_in, w_s, b_t, vg, w_out, *tail)


def _sg_bwd(dx1, x, y, uv, vm, pv, w_in, w_s, vg, w_out, layer, tm, after=None):
    L = x.shape[0]
    nc = tm // SG_CHUNK

    def body(dx1_ref, x_ref, y_ref, uv_ref, vm_ref, pv_ref, win_ref, ws_ref, vg_ref, wout_ref,
             dx_ref, duv_ref, dy_ref, vs_ref, dws_ref, dbt_ref, dvn_scr):
        @pl.when(pl.program_id(0) == 0)
        def _():
            vs_ref[...] = jnp.zeros_like(vs_ref)
            dws_ref[...] = jnp.zeros_like(dws_ref)
            dbt_ref[...] = jnp.zeros_like(dbt_ref)

        dx1v, p = dx1_ref[...], pv_ref[...]
        dyb = _bf(dx1v * p[R_G1:R_G1 + 1])
        dy_ref[...] = dyb
        vs_ref[0:1, :] += _sum0(dx1v * y_ref[...])
        dq = _dot_nt(dyb, wout_ref[...])
        uv = uv_ref[...]
        u, v = uv[:, :D], uv[:, D:]
        dub = _bf(dq * vm_ref[...])
        dvm = dq * u
        dvmb = _bf(dvm)
        rv = lax.rsqrt(jnp.mean(v * v, axis=-1, keepdims=True) + EPS)
        vh = v * rv
        vgv = vg_ref[...]
        vnb = _bf(vh * vgv)
        mask = _tril_mask()
        for hd in range(SG_HEADS):
            wm = _bf(jnp.where(mask, ws_ref[hd], 0.0))
            cs = slice(hd * SG_CHUNK, (hd + 1) * SG_CHUNK)
            dws = jnp.zeros((SG_CHUNK, SG_CHUNK), F32)
            dbs = jnp.zeros((SG_CHUNK, 1), F32)
            for ck in range(nc):
                rs = slice(ck * SG_CHUNK, (ck + 1) * SG_CHUNK)
                dvn_scr[rs, cs] = _dot_tn(wm, dvmb[rs, cs])
                dws = dws + _dot_nt(dvmb[rs, cs], vnb[rs, cs])
                dbs = dbs + jnp.sum(dvm[rs, cs], axis=1, keepdims=True)
            dws_ref[hd] += jnp.where(mask, dws, 0.0)
            dbt_ref[:, hd:hd + 1] += dbs
        dvn = dvn_scr[...]
        vs_ref[3:4, :] += _sum0(dvn * vh)
        dvnn = dvn * vgv
        dvb = _bf(rv * (dvnn - vh * jnp.mean(dvnn * vh, axis=-1, keepdims=True)))
        duv_ref[:, 0:D] = dub
        duv_ref[:, D:2 * D] = dvb
        dh = _dot_nt(dub, win_ref[:, 0:D]) + _dot_nt(dvb, win_ref[:, D:2 * D])
        _, xn, r = _norm_mod(x_ref[...], p[R_N1:R_N1 + 1], p[R_SC1:R_SC1 + 1], p[R_SH1:R_SH1 + 1])
        dx_ref[...] = dx1v + _norm_mod_bwd(dh, xn, r, p[R_N1:R_N1 + 1], p[R_SC1:R_SC1 + 1])
        vs_ref[1:2, :] += _sum0(dh * xn)
        vs_ref[2:3, :] += _sum0(dh)

    call, tail = _call_after(
        after, body, 10, name="sg_bwd", grid=(L // tm,),
        in_specs=[_rows(tm, D), _rows(tm, D), _rows(tm, D), _rows(tm, 2 * D), _rows(tm, D),
                  pl.BlockSpec((None, 8, D), lambda i: (layer, 0, 0)), _layer_w(D, 2 * D, 0),
                  _whole((SG_HEADS, SG_CHUNK, SG_CHUNK)), _whole((1, D)), _layer_w(D, D, 0)],
        out_specs=[_rows(tm, D), _rows(tm, 2 * D), _rows(tm, D), _whole((8, D)),
                   _whole((SG_HEADS, SG_CHUNK, SG_CHUNK)), _whole((SG_CHUNK, SG_HEADS))],
        out_shape=[jax.ShapeDtypeStruct((L, D), F32), jax.ShapeDtypeStruct((L, 2 * D), BF16), jax.ShapeDtypeStruct((L, D), BF16),
                   jax.ShapeDtypeStruct((8, D), F32), jax.ShapeDtypeStruct((SG_HEADS, SG_CHUNK, SG_CHUNK), F32),
                   jax.ShapeDtypeStruct((SG_CHUNK, SG_HEADS), F32)],
        scratch_shapes=[pltpu.VMEM((tm, D), F32)],
        compiler_params=_params(1, 56),
    )
    return call(dx1, x, y, uv, vm, pv, w_in, w_s, vg, w_out, *tail)


def _final(x, target, fg, tm):
    L = x.shape[0]

    def body(x_ref, t_ref, g_ref, dx_ref, vs_ref):
        @pl.when(pl.program_id(0) == 0)
        def _():
            vs_ref[...] = jnp.zeros_like(vs_ref)

        xv, g = x_ref[...], g_ref[...]
        r = lax.rsqrt(jnp.mean(xv * xv, axis=-1, keepdims=True) + EPS)
        xn = xv * r
        e = xn * g - t_ref[...]
        vs_ref[0:1, :] += jnp.sum(e * e)
        dout = e * (1.0 / D)
        vs_ref[1:2, :] += _sum0(dout * xn)
        dxn = dout * g
        dx_ref[...] = r * (dxn - xn * jnp.mean(dxn * xn, axis=-1, keepdims=True))

    return pl.pallas_call(
        body, name="final_loss", grid=(L // tm,),
        in_specs=[_rows(tm, D), _rows(tm, D), _whole((1, D))],
        out_specs=[_rows(tm, D), _whole((8, D))],
        out_shape=[jax.ShapeDtypeStruct((L, D), F32), jax.ShapeDtypeStruct((8, D), F32)],
        compiler_params=_params(1),
    )(x, target, fg)


def _pack_flat(arrs, multiple=LANES):
    flat = jnp.concatenate([a.reshape(-1).astype(F32) for a in arrs])
    return jnp.pad(flat, (0, -flat.shape[0] % multiple))


def _pack(arrs, row_multiple=8):
    return _pack_flat(arrs, row_multiple * LANES).reshape(-1, LANES)


def _unpack(buf, shapes, lead=()):
    flat = buf.reshape(lead + (-1,))
    out, off = [], 0
    for s in shapes:
        n = 1
        for d in s:
            n *= d
        out.append(flat[..., off:off + n].reshape(lead + tuple(s)))
        off += n
    return out


BIG = ("ff_w1", "ff_w2", "conv_w_in", "conv_w_out", "ssm_w_in", "ssm_glu_w", "ssm_w_out", "sg_w_in", "sg_w_out")
BIG_AXIS = {"ff_w1": 2, "ff_w2": 1, "conv_w_in": 2, "conv_w_out": 1, "ssm_w_in": 1, "ssm_glu_w": 1, "ssm_w_out": 1,
            "sg_w_in": 2, "sg_w_out": 1}
LAYER_WEIGHTS = (
    (("conv_w_in", 0), ("conv_w_out", 0), ("ff_w1", 0), ("ff_w2", 0)),
    (("ssm_w_in", 0), ("ssm_glu_w", 0), ("ssm_w_out", 0), ("ff_w1", 1), ("ff_w2", 1)),
    (("sg_w_in", 0), ("sg_w_out", 0), ("ff_w1", 2), ("ff_w2", 2)),
    (("conv_w_in", 1), ("conv_w_out", 1), ("ff_w1", 3), ("ff_w2", 3)),
)
GATHER_GROUPS = tuple(grp for lw in LAYER_WEIGHTS for grp in (lw[:-2], lw[-2:]))
SMALL_SHARDED = ("conv_w", "conv_b", "sg_v_g")
SMALL_WIDE_PADDED = ("ssm_b_re", "ssm_b_im")
SMALL = ("ada_b", "norm1_g", "norm2_g", "final_g", "ssm_a_re", "ssm_a_im", "ssm_log_dt", "ssm_b_re", "ssm_b_im", "ssm_c_re",
         "ssm_c_im", "ssm_d", "ssm_glu_b", "sg_w_s", "sg_b_s") + SMALL_SHARDED
WEIGHTS = ("ada_w", "ada_b", "norm1_g", "norm2_g", "ff_w1", "ff_w2", "final_g", "conv_w_in", "conv_w", "conv_b", "conv_w_out",
           "ssm_w_in", "ssm_a_re", "ssm_a_im", "ssm_log_dt", "ssm_b_re", "ssm_b_im", "ssm_c_re", "ssm_c_im", "ssm_d",
           "ssm_glu_w", "ssm_glu_b", "ssm_w_out", "sg_w_in", "sg_v_g", "sg_w_s", "sg_b_s", "sg_w_out")


def kernel(x, c, ada_w, ada_b, norm1_g, norm2_g, ff_w1, ff_w2, final_g, conv_w_in, conv_w, conv_b, conv_w_out, ssm_w_in, ssm_a_re, ssm_a_im, ssm_log_dt, ssm_b_re, ssm_b_im, ssm_c_re, ssm_c_im, ssm_d, ssm_glu_w, ssm_glu_b, ssm_w_out, sg_w_in, sg_v_g, sg_w_s, sg_b_s, sg_w_out, loss_target, m_ada_w, m_ada_b, m_norm1_g, m_norm2_g, m_ff_w1, m_ff_w2, m_final_g, m_conv_w_in, m_conv_w, m_conv_b, m_conv_w_out, m_ssm_w_in, m_ssm_a_re, m_ssm_a_im, m_ssm_log_dt, m_ssm_b_re, m_ssm_b_im, m_ssm_c_re, m_ssm_c_im, m_ssm_d, m_ssm_glu_w, m_ssm_glu_b, m_ssm_w_out, m_sg_w_in, m_sg_v_g, m_sg_w_s, m_sg_b_s, m_sg_w_out, v_ada_w, v_ada_b, v_norm1_g, v_norm2_g, v_ff_w1, v_ff_w2, v_final_g, v_conv_w_in, v_conv_w, v_conv_b, v_conv_w_out, v_ssm_w_in, v_ssm_a_re, v_ssm_a_im, v_ssm_log_dt, v_ssm_b_re, v_ssm_b_im, v_ssm_c_re, v_ssm_c_im, v_ssm_d, v_ssm_glu_w, v_ssm_glu_b, v_ssm_w_out, v_sg_w_in, v_sg_v_g, v_sg_w_s, v_sg_b_s, v_sg_w_out):
    args = dict(locals())
    w = {n: args[n] for n in WEIGHTS}
    m = {n: args["m_" + n] for n in WEIGHTS}
    v = {n: args["v_" + n] for n in WEIGHTS}
    L = x.shape[1]
    tm = min(L, 256)
    chip = 2 * lax.axis_index("x") + lax.axis_index("y")
    me = 2 * chip + lax.axis_index("c")
    xin = x[0]
    target = loss_target[0]
    chip1 = chip.reshape(1).astype(jnp.int32)

    gathers = []

    def start_gather(g, after):
        entries = GATHER_GROUPS[g]
        axes = [BIG_AXIS[n] for n, _ in entries]
        lands = [_cast_place(w[n], li, BIG_AXIS[n], chip1, f"cast_{n}_{li}") for n, li in entries]
        s_sems, r_sems, lands, token = _gather_start(lands, axes, f"gather_start{g}", after)
        gathers.append((s_sems, r_sems, lands, axes))
        return token

    def weights_of(g, after):
        s_sems, r_sems, lands, axes = gathers[g]
        lands = _gather_wait(s_sems, r_sems, lands, axes, f"gather_wait{g}", after)
        token = start_gather(g + 1, lands[0]) if g + 1 < len(GATHER_GROUPS) else None
        return dict(zip([n for n, _ in GATHER_GROUPS[g]], _gather_share(lands, axes, f"gather_share{g}"))), token

    small_in = _pack([c, conv_w, conv_b, sg_v_g])
    got = _allgather_small(small_in, "gather_small_inputs").reshape(N_DEV, -1)
    c_all, cw_sh, cb_sh, vg_sh = _unpack(got, [(D,), conv_w.shape, conv_b.shape, sg_v_g.shape], lead=(N_DEV,))
    conv_w_full = jnp.concatenate([cw_sh[2 * k] for k in range(4)], axis=-1)
    conv_b_full = jnp.concatenate([cb_sh[2 * k] for k in range(4)], axis=-1)
    vg_full = jnp.concatenate([vg_sh[2 * k] for k in range(4)], axis=-1)
    c16 = jnp.pad(c_all, ((0, 16 - N_DEV), (0, 0)))

    cols = ada_w.shape[2]
    ada_b_cols = lax.dynamic_slice_in_dim(ada_b, chip * cols, cols, axis=1)[:, None, :]
    mod_sh = _ada_fwd(c16, ada_w, ada_b_cols)[:, :N_DEV, :]
    mod_all = _allgather_small(_pack([mod_sh]), "gather_mod").reshape(N_DEV, -1)
    mod_all = _unpack(mod_all, [mod_sh.shape], lead=(N_DEV,))[0]
    mod_mine = lax.dynamic_index_in_dim(mod_all[0::2], me, axis=2, keepdims=False)
    mod_mine = mod_mine.transpose(1, 0, 2).reshape(DEPTH, 6, D)
    pv = jnp.concatenate([mod_mine, norm1_g[:, None, :], norm2_g[:, None, :]], axis=1)

    start_gather(0, pv)

    cw_rows = jnp.concatenate([conv_w_full, conv_b_full[:, None, :], jnp.zeros((conv_w_full.shape[0], 4, D), F32)], axis=1)

    a_re, a_im = ssm_a_re[0], ssm_a_im[0]
    log_dt = ssm_log_dt[0][:, None]
    bt_re, bt_im = ssm_b_re[0].transpose(2, 0, 1), ssm_b_im[0].transpose(2, 0, 1)
    abar_re, abar_im, bbar_re, bbar_im = _s5_params_fwd(a_re, a_im, log_dt, bt_re, bt_im)
    ar_vec, ai_vec = abar_re.reshape(1, NSTATE), abar_im.reshape(1, NSTATE)
    bd_re, bd_im = _bf(_blockdiag_b(bbar_re)), _bf(_blockdiag_b(bbar_im))
    cd_re, cd_im = _bf(_blockdiag_c(ssm_c_re[0])), _bf(_blockdiag_c(ssm_c_im[0]))

    saved = []
    fulls = []
    xl = xin
    for i in range(DEPTH):
        kind = MIXER_OF_LAYER[i]
        j = i // 3
        full, tok = weights_of(2 * i, cd_im if i == 0 else xl)
        fulls.append(full)
        if kind == 0:
            x1, h, bcx, conv, q, y = _conv_fwd(xl, pv, full["conv_w_in"], full["conv_w_out"], cw_rows, i, j, tm, after=tok)
            mix = dict(h=h, bcx=bcx, conv=conv, q=q, y=y)
        elif kind == 1:
            xp = _to_segments(xl)
            h, u, e_re, e_im = _s5_fwd_ends(xp, pv, full["ssm_w_in"], bd_re, bd_im, ar_vec, ai_vec, i, tm, after=tok)
            s0_re, s0_im = _s5_segment_states(e_re, e_im, ar_vec, ai_vec, L // NSEG, adjoint=False)
            x1p, s_re, s_im, y1, zg, y3, y = _s5_fwd_out(xp, u, pv, bd_re, bd_im, s0_re, s0_im, ar_vec, ai_vec, cd_re, cd_im,
                                                         ssm_d, full["ssm_glu_w"], ssm_glu_b, full["ssm_w_out"], i, tm)
            x1 = _from_segments(x1p)
            mix = dict(xp=xp, h=h, u=u, s_re=s_re, s_im=s_im, y1=y1, zg=zg, y3=y3, y=y)
        else:
            x1, h, uv, vm, q, y = _sg_fwd(xl, pv, full["sg_w_in"], sg_w_s[0], sg_b_s[0].T, vg_full, full["sg_w_out"], i, tm,
                                          after=tok)
            mix = dict(h=h, uv=uv, vm=vm, q=q, y=y)
        ffn_weights, tok = weights_of(2 * i + 1, x1)
        full.update(ffn_weights)
        x2, h2, a, f = _ffn_fwd(x1, pv, full["ff_w1"], full["ff_w2"], i, tm, after=tok)
        saved.append(dict(x=xl, x1=x1, h2=h2, a=a, f=f, **mix))
        xl = x2

    dxl, vs_fin = _final(xl, target, final_g[None, :], tm)

    gfull = {n: [None] * w[n].shape[0] for n in BIG}
    vs_mix, vs_ffn = [None] * DEPTH, [None] * DEPTH
    small_g = {}
    scatters = {}
    token = None

    def start_scatter(key, entries, after):
        garrs = [gfull[n][li][None] for n, li in entries]
        gaxes = [BIG_AXIS[n] for n, _ in entries]
        s_sems, r_sems, garrs, lands, tok = _scatter_start(garrs, gaxes, f"scatter_start{key}", after)
        scatters[key] = (s_sems, r_sems, garrs, lands, gaxes, entries)
        return tok

    for i in reversed(range(DEPTH)):
        kind = MIXER_OF_LAYER[i]
        j = i // 3
        sv = saved[i]
        full = fulls[i]
        dx1, p_b, da_b, df_b, vs_ffn[i] = _ffn_bwd(dxl, sv["x1"], sv["a"], sv["f"], pv, full["ff_w1"], full["ff_w2"], i, tm,
                                                   after=token)
        gfull["ff_w1"][i] = _mm_tn(sv["h2"], da_b, f"wgrad_ff_w1_{i}")
        gfull["ff_w2"][i] = _mm_tn(p_b, df_b, f"wgrad_ff_w2_{i}")
        if i == 0:
            token = start_scatter("0f", LAYER_WEIGHTS[0][2:], dx1)
        if kind == 0:
            dxl, dbcx_b, dy_b, vsm = _conv_bwd(dx1, sv["x"], sv["y"], sv["bcx"], sv["conv"], pv, full["conv_w_in"],
                                               full["conv_w_out"], cw_rows, i, j, tm, after=token if i == 0 else None)
            gfull["conv_w_in"][j] = _mm_tn(sv["h"], dbcx_b, f"wgrad_conv_w_in_{j}")
            gfull["conv_w_out"][j] = _mm_tn(sv["q"], dy_b, f"wgrad_conv_w_out_{j}")
            small_g.setdefault("conv_w", [None, None])[j] = vsm[3:6]
            small_g.setdefault("conv_b", [None, None])[j] = vsm[6]
        elif kind == 1:
            dx1p = _to_segments(dx1)
            dy_b, y2_b, dzg_b, dy1_b, du_skip, eb_re, eb_im, vsm = _s5_bwd_ends(
                dx1p, sv["y"], sv["y1"], sv["zg"], sv["u"], pv, cd_re, cd_im, ar_vec, ai_vec, ssm_d, full["ssm_glu_w"],
                full["ssm_w_out"], i, tm)
            l0_re, l0_im = _s5_segment_states(eb_re, eb_im, ar_vec, ai_vec, L // NSEG, adjoint=True)
            dxp, du_b, lam_re, lam_im, dabar, vs_in = _s5_bwd_in(
                dx1p, dy1_b, du_skip, sv["xp"], sv["s_re"], sv["s_im"], pv, bd_re, bd_im, cd_re, cd_im, l0_re, l0_im,
                ar_vec, ai_vec, full["ssm_w_in"], i, tm)
            dxl = _from_segments(dxp)
            gfull["ssm_w_out"][0] = _mm_tn(sv["y3"], dy_b, "wgrad_ssm_w_out")
            gfull["ssm_glu_w"][0] = _mm_tn(y2_b, dzg_b, "wgrad_ssm_glu_w")
            gfull["ssm_w_in"][0] = _mm_tn(sv["h"], du_b, "wgrad_ssm_w_in")
            s5_late = dict(s_re=sv["s_re"], s_im=sv["s_im"], u=sv["u"], dy1_b=dy1_b, lam_re=lam_re, lam_im=lam_im, dabar=dabar)
            small_g.update(ssm_d=vsm[2], ssm_glu_b=vsm[1])
            vsm = jnp.concatenate([vsm[0:1], vs_in[1:3], jnp.zeros((5, D), F32)], axis=0)
        else:
            dxl, duv_b, dy_b, vsm, d_ws, d_bt = _sg_bwd(dx1, sv["x"], sv["y"], sv["uv"], sv["vm"], pv, full["sg_w_in"],
                                                        sg_w_s[0], vg_full, full["sg_w_out"], i, tm)
            gfull["sg_w_in"][0] = _mm_tn(sv["h"], duv_b, "wgrad_sg_w_in")
            gfull["sg_w_out"][0] = _mm_tn(sv["q"], dy_b, "wgrad_sg_w_out")
            small_g.update(sg_v_g=vsm[3], sg_w_s=d_ws, sg_b_s=d_bt.T)
        vs_mix[i] = vsm
        token = start_scatter(str(i), LAYER_WEIGHTS[i], dxl) if i > 0 else start_scatter("0c", LAYER_WEIGHTS[0][:2], dxl)
    grad_x = dxl[None]

    sums = {n: [None] * w[n].shape[0] for n in BIG}

    def collect(key, after):
        s_sems, r_sems, garrs, lands, gaxes, entries = scatters[key]
        garrs, recv = _scatter_wait(s_sems, r_sems, garrs, lands, gaxes, f"scatter_wait{key}", after)
        for (n, li), g, r3, ax in zip(entries, garrs, recv, gaxes):
            sums[n][li] = _sum_parts(r3, g, ax, chip1, f"sum_{n}_{li}")
        return sums[entries[-1][0]][entries[-1][1]]

    after = token
    for key in ("3", "2", "1"):
        after = collect(key, after)
    early = [(n, li) for i in (3, 2, 1) for n, li in LAYER_WEIGHTS[i]]
    late = list(LAYER_WEIGHTS[0][2:]) + list(LAYER_WEIGHTS[0][:2])
    s_sems, r_sems, mine_thru, lands, tok = _swap_start([sums[n][li] for n, li in early], "swap_start_early", after)

    blocks = dict(
        c_re=_mm_tn_blocks(s5_late["s_re"], s5_late["dy1_b"], S5_BP, S5_BH, "wgrad_s5_c_re", after=tok),
        c_im=_mm_tn_blocks(s5_late["s_im"], s5_late["dy1_b"], S5_BP, S5_BH, "wgrad_s5_c_im", after=tok),
        b_re=_mm_tn_blocks(s5_late["u"], s5_late["lam_re"], S5_BH, S5_BP, "wgrad_s5_b_re", after=tok),
        b_im=_mm_tn_blocks(s5_late["u"], s5_late["lam_im"], S5_BH, S5_BP, "wgrad_s5_b_im", after=tok))
    d_are, d_aim, d_ldt, d_btre, d_btim = _s5_params_bwd(
        a_re, a_im, log_dt, bt_re, bt_im, s5_late["dabar"][0].reshape(S5_G, S5_P), s5_late["dabar"][1].reshape(S5_G, S5_P),
        _unblock_b(blocks["b_re"]), _unblock_b(blocks["b_im"]))
    small_g.update(ssm_a_re=d_are, ssm_a_im=d_aim, ssm_log_dt=d_ldt, ssm_b_re=d_btre.transpose(1, 2, 0),
                   ssm_b_im=d_btim.transpose(1, 2, 0), ssm_c_re=_unblock_c(blocks["c_re"]), ssm_c_im=-_unblock_c(blocks["c_im"]))

    mine_thru, got = _swap_wait(s_sems, r_sems, mine_thru, lands, "swap_wait_early", blocks["b_im"])
    sib = dict(zip(early, got))
    for (n, li), t in zip(early, mine_thru):
        sums[n][li] = t
    after = got[-1]
    for key in ("0f", "0c"):
        after = collect(key, after)
    sib.update(zip(late, _swap_with_sibling([sums[n][li] for n, li in late], "swap_grad_sums_late")))

    dmod = _mod_bwd(jnp.stack(vs_mix), jnp.stack(vs_ffn), pv)
    small_g.update(ada_b=dmod[:, :6, :], norm1_g=dmod[:, 6, :], norm2_g=dmod[:, 7, :], final_g=vs_fin[1],
                   conv_w=jnp.stack(small_g["conv_w"]), conv_b=jnp.stack(small_g["conv_b"]))

    loss_part = (0.5 / D) * vs_fin[0, 0:1]
    part_shapes = [(1,)] + [tuple(small_g[n].shape) for n in SMALL]
    parts_sum = _allreduce_small(_pack([loss_part] + [small_g[n] for n in SMALL], 16), "reduce_small_grads", sib[late[-1]])
    summed = _unpack(parts_sum, part_shapes)
    loss = summed[0][0]
    gsum = dict(zip(SMALL, summed[1:]))
    dmod_all = _allgather_small(_pack([small_g["ada_b"]]), "gather_dmod", parts_sum)
    dmod_all = dmod_all.reshape(N_DEV, DEPTH, 6 * D)
    dmod_cols = lax.dynamic_slice_in_dim(dmod_all, chip * cols, cols, axis=2).transpose(1, 0, 2)
    g_ada_w = _ada_bwd(c16, jnp.pad(dmod_cols, ((0, 0), (0, 16 - N_DEV), (0, 0))))

    res = {}
    shp = ada_w.shape
    two = lambda t: t.reshape(shp[0] * shp[1], shp[2])
    res["ada_w"] = [t.reshape(shp) for t in _adamw(two(ada_w), [two(g_ada_w)], two(m_ada_w), two(v_ada_w), "adamw_ada_w")]

    def mine(n):
        g = gsum[n]
        if n in SMALL_SHARDED:
            g = lax.dynamic_slice_in_dim(g, chip * w[n].shape[-1], w[n].shape[-1], axis=g.ndim - 1)
        return g.reshape(w[n].shape)

    for k, names in enumerate(([n for n in SMALL if n not in SMALL_WIDE_PADDED], list(SMALL_WIDE_PADDED))):
        outs = _adamw_many([w[n] for n in names], [mine(n) for n in names], [m[n] for n in names], [v[n] for n in names],
                           f"adamw_small{k}")
        for idx, n in enumerate(names):
            res[n] = [outs[part][idx] for part in range(4)]

    for n in BIG:
        res[n] = _adamw_layers(w[n], sums[n], [sib[(n, li)] for li in range(w[n].shape[0])], m[n], v[n], f"adamw_{n}")

    outs = [loss, grad_x]
    for part in range(4):
        outs += [res[n][part] for n in WEIGHTS]
    return tuple(outs)
```

```python
import functools

import jax
import jax.numpy as jnp
from jax import lax
from jax.experimental import pallas as pl
from jax.experimental.pallas import tpu as pltpu

F32 = jnp.float32
BF16 = jnp.bfloat16
D = 1024
EPS = 1e-6
DEPTH = 4
MIXER_OF_LAYER = (0, 1, 2, 0)
S5_G, S5_H, S5_P = 64, 16, 64
S5_NB = 4
S5_BH = S5_H * 16
S5_BP = S5_P * 16
NSTATE = S5_G * S5_P
SG_HEADS, SG_CHUNK = 8, 128
ADAM_LR, ADAM_B1, ADAM_B2, ADAM_EPS, ADAM_WD, ADAM_STEP = 0.001, 0.9, 0.999, 1e-08, 0.01, 10
N_DEV = 8
MESH = pl.DeviceIdType.MESH
LANES = 1024
R_SH1, R_SC1, R_G1, R_SH2, R_SC2, R_G2, R_N1, R_N2 = range(8)


def _dot(a, b):
    return jnp.dot(a, b, preferred_element_type=F32)


def _dot_nt(a, b):
    return lax.dot_general(a, b, (((1,), (1,)), ((), ())), preferred_element_type=F32)


def _dot_tn(a, b):
    return lax.dot_general(a, b, (((0,), (0,)), ((), ())), preferred_element_type=F32)


def _bf(x):
    return x.astype(BF16)


def _sum0(x):
    return jnp.sum(x, axis=0, keepdims=True)


def _params(n_axes, vmem_mb=48):
    return pltpu.CompilerParams(dimension_semantics=("arbitrary",) * n_axes, vmem_limit_bytes=vmem_mb << 20)


def _rows(tm, cols, nt=None):
    if nt is None:
        return pl.BlockSpec((tm, cols), lambda i: (i, 0))
    return pl.BlockSpec((tm, cols), lambda i: (nt - 1 - i, 0))


def _whole(shape):
    nd = len(shape)
    return pl.BlockSpec(shape, lambda *_: (0,) * nd)


def _layer_w(r, c, layer):
    return pl.BlockSpec((None, r, c), lambda *_: (layer, 0, 0), pipeline_mode=pl.Buffered(1))


def _const_w(shape):
    nd = len(shape)
    return pl.BlockSpec(shape, lambda *_: (0,) * nd, pipeline_mode=pl.Buffered(1))


def _call_after(after, body, n_in, *, in_specs, **kw):
    if after is None:
        return pl.pallas_call(body, in_specs=in_specs, **kw), ()

    def body_after(*refs):
        return body(*refs[:n_in], *refs[n_in + 1:])

    return pl.pallas_call(body_after, in_specs=list(in_specs) + [pl.BlockSpec(memory_space=pl.ANY)], **kw), (after,)


def _norm_mod(x, ng, sc, sh):
    r = lax.rsqrt(jnp.mean(x * x, axis=-1, keepdims=True) + EPS)
    xn = x * r
    return (xn * ng) * (1.0 + sc) + sh, xn, r


def _norm_mod_bwd(dh, xn, r, ng, sc):
    dxn = dh * (ng * (1.0 + sc))
    return r * (dxn - xn * jnp.mean(dxn * xn, axis=-1, keepdims=True))


def _shift_down(z, prev8, k):
    row = lax.broadcasted_iota(jnp.int32, z.shape, 0)
    if k == 1:
        return jnp.where(row >= 1, pltpu.roll(z, 1, 0), prev8[7:8])
    return jnp.where(row >= 2, pltpu.roll(z, 2, 0), jnp.where(row == 0, prev8[6:7], prev8[7:8]))


def _shift_up(z, next8, k):
    n = z.shape[0]
    row = lax.broadcasted_iota(jnp.int32, z.shape, 0)
    if k == 1:
        return jnp.where(row <= n - 2, pltpu.roll(z, n - 1, 0), next8[0:1])
    return jnp.where(row <= n - 3, pltpu.roll(z, n - 2, 0), jnp.where(row == n - 2, next8[0:1], next8[1:2]))


def _place():
    x, y, c = lax.axis_index("x"), lax.axis_index("y"), lax.axis_index("c")
    chips = [(1 - x, y), (x, 1 - y), (1 - x, 1 - y)]
    return x, y, c, chips


def _allgather_small(x_shard, name, after=None):
    m_per, n = x_shard.shape

    def body(x_ref, out_ref, send_sems, recv_sems, local_sem):
        x, y, c, chips = _place()
        me, sibling = (x, y, c), (x, y, 1 - c)

        def rows(px, py, pc):
            return out_ref.at[pl.ds((4 * px + 2 * py + pc) * m_per, m_per), :]

        def copy(k, block, to, src=None):
            return pltpu.make_async_remote_copy(
                src_ref=rows(*block) if src is None else src, dst_ref=rows(*block),
                send_sem=send_sems.at[k], recv_sem=recv_sems.at[k], device_id=to, device_id_type=MESH)

        mine = pltpu.make_async_copy(x_ref, rows(*me), local_sem)
        mine.start()
        first = [copy(0, me, sibling, src=x_ref)]
        first += [copy(1 + j, me, (*chip, c), src=x_ref) for j, chip in enumerate(chips)]
        for cp in first:
            cp.start()
        passed = [copy(4 + j, (*chip, c), sibling) for j, chip in enumerate(chips)]
        for j, chip in enumerate(chips):
            copy(1 + j, (*chip, c), me).wait_recv()
            passed[j].start()
        copy(0, sibling, me).wait_recv()
        for j, chip in enumerate(chips):
            copy(4 + j, (*chip, 1 - c), me).wait_recv()
        for cp in first + passed:
            cp.wait_send()
        mine.wait()

    call, tail = _call_after(
        after, body, 1, name=name, out_shape=jax.ShapeDtypeStruct((N_DEV * m_per, n), F32),
        in_specs=[pl.BlockSpec(memory_space=pltpu.VMEM)], out_specs=pl.BlockSpec(memory_space=pltpu.VMEM),
        scratch_shapes=[pltpu.SemaphoreType.DMA((7,)), pltpu.SemaphoreType.DMA((7,)), pltpu.SemaphoreType.DMA],
        compiler_params=pltpu.CompilerParams(vmem_limit_bytes=48 << 20),
    )
    return call(x_shard, *tail)


def _reduce_pair(x_part, name, after=None):
    m, n = x_part.shape
    h = m // 2

    def body(x_ref, slots_ref, sib_buf, send_sem, recv_sem):
        x, y, c, _ = _place()
        swap = pltpu.make_async_remote_copy(src_ref=x_ref, dst_ref=sib_buf, send_sem=send_sem, recv_sem=recv_sem,
                                            device_id=(x, y, 1 - c), device_id_type=MESH)
        swap.start()
        swap.wait()
        mine = pl.ds(pl.multiple_of(c * h, 8), h)
        slots_ref[pl.ds(2 * x + y, 1)] = (x_ref[mine, :] + sib_buf[mine, :])[None]

    call, tail = _call_after(
        after, body, 1, name=name, out_shape=jax.ShapeDtypeStruct((4, h, n), F32),
        in_specs=[pl.BlockSpec(memory_space=pltpu.VMEM)], out_specs=pl.BlockSpec(memory_space=pltpu.VMEM),
        scratch_shapes=[pltpu.VMEM((m, n), F32), pltpu.SemaphoreType.DMA, pltpu.SemaphoreType.DMA],
        compiler_params=pltpu.CompilerParams(vmem_limit_bytes=48 << 20),
    )
    return call(x_part, *tail)


def _reduce_cross_start(slots, name):
    def body(slots_ref, send_sems, recv_sems, thru, token):
        x, y, c, chips = _place()
        mine = slots_ref.at[pl.ds(2 * x + y, 1)]
        for j, chip in enumerate(chips):
            pltpu.make_async_remote_copy(src_ref=mine, dst_ref=mine, send_sem=send_sems.at[j], recv_sem=recv_sems.at[j],
                                         device_id=(*chip, c), device_id_type=MESH).start()
        token[...] = jnp.zeros_like(token)

    res = pl.pallas_call(
        body, name=name,
        out_shape=(pltpu.SemaphoreType.DMA((3,)), pltpu.SemaphoreType.DMA((3,)), pltpu.HBM(slots.shape, F32),
                   jax.ShapeDtypeStruct((8, 128), F32)),
        in_specs=[HBM_SPEC], out_specs=(SEM_SPEC, SEM_SPEC, HBM_SPEC, pl.BlockSpec(memory_space=pltpu.VMEM)),
        input_output_aliases={0: 2}, compiler_params=SPLIT_COPY_PARAMS,
    )(*_in_hbm([slots]))
    return res


def _reduce_cross_wait(send_sems, recv_sems, slots, name, after):
    def body(slots_ref, s_sems, r_sems, after_ref, thru):
        x, y, c, chips = _place()
        for j, chip in enumerate(chips):
            theirs = slots_ref.at[pl.ds(2 * chip[0] + chip[1], 1)]
            cp = pltpu.make_async_remote_copy(src_ref=theirs, dst_ref=theirs, send_sem=s_sems.at[j], recv_sem=r_sems.at[j],
                                              device_id=(x, y, c), device_id_type=MESH)
            cp.wait_send()
            cp.wait_recv()

    return pl.pallas_call(
        body, name=name, out_shape=pltpu.HBM(slots.shape, F32),
        in_specs=[HBM_SPEC, SEM_SPEC, SEM_SPEC, ANY_SPEC], out_specs=HBM_SPEC,
        input_output_aliases={0: 0}, compiler_params=SPLIT_COPY_PARAMS,
    )(slots, send_sems, recv_sems, after)


def _reduce_finish(slots, name):
    _, h, n = slots.shape

    def body(slots_ref, out_ref, send_sem, recv_sem):
        x, y, c, _ = _place()
        mine = pl.ds(pl.multiple_of(c * h, 8), h)
        theirs = pl.ds(pl.multiple_of((1 - c) * h, 8), h)
        out_ref[mine, :] = ((slots_ref[0] + slots_ref[1]) + slots_ref[2]) + slots_ref[3]
        give = pltpu.make_async_remote_copy(src_ref=out_ref.at[mine, :], dst_ref=out_ref.at[mine, :], send_sem=send_sem,
                                            recv_sem=recv_sem, device_id=(x, y, 1 - c), device_id_type=MESH)
        give.start()
        give.wait_send()
        pltpu.make_async_remote_copy(src_ref=out_ref.at[theirs, :], dst_ref=out_ref.at[theirs, :], send_sem=send_sem,
                                     recv_sem=recv_sem, device_id=(x, y, c), device_id_type=MESH).wait_recv()

    return pl.pallas_call(
        body, name=name, out_shape=jax.ShapeDtypeStruct((2 * h, n), F32),
        in_specs=[pl.BlockSpec(memory_space=pltpu.VMEM)], out_specs=pl.BlockSpec(memory_space=pltpu.VMEM),
        scratch_shapes=[pltpu.SemaphoreType.DMA, pltpu.SemaphoreType.DMA],
        compiler_params=pltpu.CompilerParams(vmem_limit_bytes=48 << 20),
    )(slots)


def _shard_region(ref, full_shape, axis, chip_k, half=None):
    _, r, c = full_shape
    if axis == 1:
        rs = r // 4
        if half is None:
            return ref.at[:, pl.ds(pl.multiple_of(chip_k * rs, 128), rs), :]
        return ref.at[:, pl.ds(pl.multiple_of(chip_k * rs + half * (rs // 2), 128), rs // 2), :]
    cs = c // 4
    if half is None:
        return ref.at[:, :, pl.ds(pl.multiple_of(chip_k * cs, 128), cs)]
    return ref.at[:, pl.ds(pl.multiple_of(half * (r // 2), 128), r // 2), pl.ds(pl.multiple_of(chip_k * cs, 128), cs)]


HBM_SPEC = pl.BlockSpec(memory_space=pltpu.HBM)
SEM_SPEC = pl.BlockSpec(memory_space=pltpu.SEMAPHORE)
ANY_SPEC = pl.BlockSpec(memory_space=pl.ANY)
SPLIT_COPY_PARAMS = pltpu.CompilerParams(has_side_effects=pltpu.SideEffectType.DATAFLOW_SIDE_EFFECTING)


def _in_hbm(arrs):
    return [pltpu.with_memory_space_constraint(a, pltpu.HBM) for a in arrs]


def _cast_place(w_stack, li, axis, chip, name):
    _, r, c = w_stack.shape
    full = (1, 4 * r, c) if axis == 1 else (1, r, 4 * c)
    tr = min(r, 256)
    if axis == 1:
        out_spec = pl.BlockSpec((None, tr, c), lambda i, k: (0, k[0] * (r // tr) + i, 0))
    else:
        out_spec = pl.BlockSpec((None, tr, c), lambda i, k: (0, i, k[0]))

    def body(k_ref, w_ref, o_ref):
        o_ref[...] = _bf(w_ref[...])

    return pl.pallas_call(
        body, name=name,
        grid_spec=pltpu.PrefetchScalarGridSpec(
            num_scalar_prefetch=1, grid=(r // tr,),
            in_specs=[pl.BlockSpec((None, tr, c), lambda i, k: (li, i, 0))], out_specs=out_spec),
        out_shape=jax.ShapeDtypeStruct(full, BF16),
        compiler_params=_params(1),
    )(chip, w_stack)


def _gather_start(lands, axes, name, after):
    n_arr = len(lands)
    fulls = [tuple(l.shape) for l in lands]

    def body(*refs):
        land = refs[:n_arr]
        send_sems, recv_sems = refs[n_arr + 1:n_arr + 3]
        token = refs[-1]
        x, y, c, chips = _place()
        k_me = 2 * x + y
        for a in range(n_arr):
            mine = _shard_region(land[a], fulls[a], axes[a], k_me, c)
            for j, chip in enumerate(chips):
                pltpu.make_async_remote_copy(
                    src_ref=mine, dst_ref=mine, send_sem=send_sems.at[a * 3 + j], recv_sem=recv_sems.at[a * 3 + j],
                    device_id=(*chip, c), device_id_type=MESH).start()
        token[...] = jnp.zeros_like(token)

    res = pl.pallas_call(
        body, name=name,
        out_shape=(pltpu.SemaphoreType.DMA((3 * n_arr,)), pltpu.SemaphoreType.DMA((3 * n_arr,)),
                   *[pltpu.HBM(f, BF16) for f in fulls], jax.ShapeDtypeStruct((8, 128), F32)),
        in_specs=[HBM_SPEC] * n_arr + [ANY_SPEC],
        out_specs=(SEM_SPEC, SEM_SPEC, *[HBM_SPEC] * n_arr, pl.BlockSpec(memory_space=pltpu.VMEM)),
        input_output_aliases={a: 2 + a for a in range(n_arr)},
        compiler_params=SPLIT_COPY_PARAMS,
    )(*_in_hbm(lands), after)
    return res[0], res[1], list(res[2:2 + n_arr]), res[-1]


def _gather_wait(send_sems, recv_sems, lands, axes, name, after):
    n_arr = len(lands)
    fulls = [tuple(l.shape) for l in lands]

    def body(*refs):
        land = refs[:n_arr]
        s_sems, r_sems = refs[n_arr:n_arr + 2]
        x, y, c, chips = _place()
        for a in range(n_arr):
            for j, chip in enumerate(chips):
                k_j = 2 * chip[0] + chip[1]
                got = _shard_region(land[a], fulls[a], axes[a], k_j, c)
                cp = pltpu.make_async_remote_copy(
                    src_ref=got, dst_ref=got, send_sem=s_sems.at[a * 3 + j], recv_sem=r_sems.at[a * 3 + j],
                    device_id=(x, y, c), device_id_type=MESH)
                cp.wait_send()
                cp.wait_recv()

    res = pl.pallas_call(
        body, name=name,
        out_shape=tuple(pltpu.HBM(f, BF16) for f in fulls),
        in_specs=[HBM_SPEC] * n_arr + [SEM_SPEC, SEM_SPEC, ANY_SPEC],
        out_specs=tuple([HBM_SPEC] * n_arr),
        input_output_aliases={a: a for a in range(n_arr)},
        compiler_params=SPLIT_COPY_PARAMS,
    )(*lands, send_sems, recv_sems, after)
    return list(res)


def _gather_share(lands, axes, name):
    n_arr = len(lands)
    fulls = [tuple(l.shape) for l in lands]

    def body(*refs):
        land_in, land = refs[:n_arr], refs[n_arr:2 * n_arr]
        send_sems, recv_sems = refs[2 * n_arr:]
        x, y, c, chips = _place()
        copies = []
        for a in range(n_arr):
            for j, chip in enumerate(chips):
                k_j = 2 * chip[0] + chip[1]
                cp = pltpu.make_async_remote_copy(
                    src_ref=_shard_region(land_in[a], fulls[a], axes[a], k_j, c),
                    dst_ref=_shard_region(land[a], fulls[a], axes[a], k_j, c),
                    send_sem=send_sems.at[a * 3 + j], recv_sem=recv_sems.at[a * 3 + j],
                    device_id=(x, y, 1 - c), device_id_type=MESH)
                cp.start()
                copies.append(cp)
        for cp in copies:
            cp.wait()

    return pl.pallas_call(
        body, name=name, out_shape=[jax.ShapeDtypeStruct(f, BF16) for f in fulls],
        in_specs=[ANY_SPEC] * n_arr, out_specs=[ANY_SPEC] * n_arr,
        input_output_aliases={a: a for a in range(n_arr)},
        scratch_shapes=[pltpu.SemaphoreType.DMA((3 * n_arr,)), pltpu.SemaphoreType.DMA((3 * n_arr,))],
    )(*lands)


def _scatter_shapes(grads, axes):
    out = []
    for g, ax in zip(grads, axes):
        shp = list(g.shape)
        shp[ax] //= 4
        out.append((3,) + tuple(shp[1:]))
    return out


def _scatter_start(grads, axes, name, after):
    n_arr = len(grads)
    shapes = _scatter_shapes(grads, axes)
    lands = [lax.empty(s, BF16) for s in shapes]

    def body(*refs):
        ins, land = refs[:n_arr], refs[n_arr:2 * n_arr]
        send_sems, recv_sems = refs[2 * n_arr + 1:2 * n_arr + 3]
        token = refs[-1]
        x, y, c, chips = _place()
        for a in range(n_arr):
            for j, chip in enumerate(chips):
                k_j = 2 * chip[0] + chip[1]
                pltpu.make_async_remote_copy(
                    src_ref=_shard_region(ins[a], grads[a].shape, axes[a], k_j), dst_ref=land[a].at[pl.ds(j, 1)],
                    send_sem=send_sems.at[a * 3 + j], recv_sem=recv_sems.at[a * 3 + j],
                    device_id=(*chip, c), device_id_type=MESH).start()
        token[...] = jnp.zeros_like(token)

    res = pl.pallas_call(
        body, name=name,
        out_shape=(pltpu.SemaphoreType.DMA((3 * n_arr,)), pltpu.SemaphoreType.DMA((3 * n_arr,)),
                   *[pltpu.HBM(g.shape, BF16) for g in grads], *[pltpu.HBM(s, BF16) for s in shapes],
                   jax.ShapeDtypeStruct((8, 128), F32)),
        in_specs=[HBM_SPEC] * (2 * n_arr) + [ANY_SPEC],
        out_specs=(SEM_SPEC, SEM_SPEC, *[HBM_SPEC] * (2 * n_arr), pl.BlockSpec(memory_space=pltpu.VMEM)),
        input_output_aliases={a: 2 + a for a in range(2 * n_arr)},
        compiler_params=SPLIT_COPY_PARAMS,
    )(*_in_hbm(grads), *_in_hbm(lands), after)
    return res[0], res[1], list(res[2:2 + n_arr]), list(res[2 + n_arr:2 + 2 * n_arr]), res[-1]


def _scatter_wait(send_sems, recv_sems, grads, lands, axes, name, after):
    n_arr = len(grads)

    def body(*refs):
        ins, land = refs[:n_arr], refs[n_arr:2 * n_arr]
        s_sems, r_sems = refs[2 * n_arr:2 * n_arr + 2]
        x, y, c, chips = _place()
        for a in range(n_arr):
            for j, chip in enumerate(chips):
                k_j = 2 * chip[0] + chip[1]
                cp = pltpu.make_async_remote_copy(
                    src_ref=_shard_region(ins[a], grads[a].shape, axes[a], k_j), dst_ref=land[a].at[pl.ds(j, 1)],
                    send_sem=s_sems.at[a * 3 + j], recv_sem=r_sems.at[a * 3 + j],
                    device_id=(x, y, c), device_id_type=MESH)
                cp.wait_send()
                cp.wait_recv()

    res = pl.pallas_call(
        body, name=name,
        out_shape=(*[pltpu.HBM(g.shape, BF16) for g in grads], *[pltpu.HBM(l.shape, BF16) for l in lands]),
        in_specs=[HBM_SPEC] * (2 * n_arr) + [SEM_SPEC, SEM_SPEC, ANY_SPEC],
        out_specs=tuple([HBM_SPEC] * (2 * n_arr)),
        input_output_aliases={a: a for a in range(2 * n_arr)},
        compiler_params=SPLIT_COPY_PARAMS,
    )(*grads, *lands, send_sems, recv_sems, after)
    return list(res[:n_arr]), list(res[n_arr:])


def _swap_start(arrs, name, after):
    n_arr = len(arrs)
    lands = [lax.empty(a.shape, a.dtype) for a in arrs]

    def body(*refs):
        ins, land = refs[:n_arr], refs[n_arr:2 * n_arr]
        send_sems, recv_sems = refs[2 * n_arr + 1:2 * n_arr + 3]
        token = refs[-1]
        x, y, c, _ = _place()
        for a in range(n_arr):
            pltpu.make_async_remote_copy(
                src_ref=ins[a], dst_ref=land[a], send_sem=send_sems.at[a], recv_sem=recv_sems.at[a],
                device_id=(x, y, 1 - c), device_id_type=MESH).start()
        token[...] = jnp.zeros_like(token)

    res = pl.pallas_call(
        body, name=name,
        out_shape=(pltpu.SemaphoreType.DMA((n_arr,)), pltpu.SemaphoreType.DMA((n_arr,)),
                   *[pltpu.HBM(a.shape, a.dtype) for a in arrs], *[pltpu.HBM(a.shape, a.dtype) for a in arrs],
                   jax.ShapeDtypeStruct((8, 128), F32)),
        in_specs=[HBM_SPEC] * (2 * n_arr) + [ANY_SPEC],
        out_specs=(SEM_SPEC, SEM_SPEC, *[HBM_SPEC] * (2 * n_arr), pl.BlockSpec(memory_space=pltpu.VMEM)),
        input_output_aliases={a: 2 + a for a in range(2 * n_arr)},
        compiler_params=SPLIT_COPY_PARAMS,
    )(*_in_hbm(arrs), *_in_hbm(lands), after)
    return res[0], res[1], list(res[2:2 + n_arr]), list(res[2 + n_arr:2 + 2 * n_arr]), res[-1]


def _swap_wait(send_sems, recv_sems, arrs, lands, name, after):
    n_arr = len(arrs)

    def body(*refs):
        ins, land = refs[:n_arr], refs[n_arr:2 * n_arr]
        s_sems, r_sems = refs[2 * n_arr:2 * n_arr + 2]
        x, y, c, _ = _place()
        for a in range(n_arr):
            cp = pltpu.make_async_remote_copy(
                src_ref=ins[a], dst_ref=land[a], send_sem=s_sems.at[a], recv_sem=r_sems.at[a],
                device_id=(x, y, c), device_id_type=MESH)
            cp.wait_send()
            cp.wait_recv()

    res = pl.pallas_call(
        body, name=name,
        out_shape=(*[pltpu.HBM(a.shape, a.dtype) for a in arrs], *[pltpu.HBM(a.shape, a.dtype) for a in arrs]),
        in_specs=[HBM_SPEC] * (2 * n_arr) + [SEM_SPEC, SEM_SPEC, ANY_SPEC],
        out_specs=tuple([HBM_SPEC] * (2 * n_arr)),
        input_output_aliases={a: a for a in range(2 * n_arr)},
        compiler_params=SPLIT_COPY_PARAMS,
    )(*arrs, *lands, send_sems, recv_sems, after)
    return list(res[:n_arr]), list(res[n_arr:])


def _swap_with_sibling(arrs, name):
    n_arr = len(arrs)

    def body(*refs):
        ins, outs = refs[:n_arr], refs[n_arr:2 * n_arr]
        send_sems, recv_sems = refs[2 * n_arr:]
        x, y, c, _ = _place()
        copies = []
        for a in range(n_arr):
            cp = pltpu.make_async_remote_copy(
                src_ref=ins[a], dst_ref=outs[a], send_sem=send_sems.at[a], recv_sem=recv_sems.at[a],
                device_id=(x, y, 1 - c), device_id_type=MESH)
            cp.start()
            copies.append(cp)
        for cp in copies:
            cp.wait()

    any_spec = pl.BlockSpec(memory_space=pl.ANY)
    return pl.pallas_call(
        body, name=name, out_shape=[jax.ShapeDtypeStruct(a.shape, a.dtype) for a in arrs],
        in_specs=[any_spec] * n_arr, out_specs=[any_spec] * n_arr,
        scratch_shapes=[pltpu.SemaphoreType.DMA((n_arr,)), pltpu.SemaphoreType.DMA((n_arr,))],
    )(*arrs)


def _mm_tn(a, b, name, out_dtype=BF16):
    L, m = a.shape
    n = b.shape[1]
    bm, bn, bk = min(m, 1024), min(n, 1024), min(L, 2048)
    nk = L // bk

    def body(a_ref, b_ref, o_ref, acc):
        k = pl.program_id(2)

        @pl.when(k == 0)
        def _():
            acc[...] = jnp.zeros_like(acc)

        acc[...] += _dot_tn(_bf(a_ref[...]), _bf(b_ref[...]))

        @pl.when(k == nk - 1)
        def _():
            o_ref[...] = acc[...].astype(out_dtype)

    return pl.pallas_call(
        body, name=name, grid=(m // bm, n // bn, nk),
        in_specs=[pl.BlockSpec((bk, bm), lambda i, j, k: (k, i)), pl.BlockSpec((bk, bn), lambda i, j, k: (k, j))],
        out_specs=pl.BlockSpec((bm, bn), lambda i, j, k: (i, j)),
        out_shape=jax.ShapeDtypeStruct((m, n), out_dtype),
        scratch_shapes=[pltpu.VMEM((bm, bn), F32)],
        compiler_params=_params(3),
    )(a, b)


def _mm_tn_blocks(a, b, wa, wb, name, after=None):
    L = a.shape[0]
    nb = a.shape[1] // wa
    bk = min(L, 1024)
    nk = L // bk

    def body(a_ref, b_ref, o_ref):
        @pl.when(pl.program_id(1) == 0)
        def _():
            o_ref[...] = jnp.zeros_like(o_ref)

        o_ref[...] += _dot_tn(_bf(a_ref[...]), _bf(b_ref[...]))

    call, tail = _call_after(
        after, body, 2, name=name, grid=(nb, nk),
        in_specs=[pl.BlockSpec((bk, wa), lambda j, k: (k, j)), pl.BlockSpec((bk, wb), lambda j, k: (k, j))],
        out_specs=pl.BlockSpec((None, wa, wb), lambda j, k: (j, 0, 0)),
        out_shape=jax.ShapeDtypeStruct((nb, wa, wb), F32),
        compiler_params=_params(2),
    )
    return call(a, b, *tail)


def _sum_parts(parts, own, axis, chip, name):
    _, r, c = parts.shape
    tr = min(r, 256)
    if axis == 1:
        own_spec = pl.BlockSpec((None, tr, c), lambda i, k: (0, k[0] * (r // tr) + i, 0))
    else:
        own_spec = pl.BlockSpec((None, tr, c), lambda i, k: (0, i, k[0]))

    def body(k_ref, p_ref, g_ref, o_ref):
        p = p_ref[...].astype(F32)
        o_ref[...] = ((p[0] + p[1]) + p[2]) + g_ref[...].astype(F32)

    return pl.pallas_call(
        body, name=name,
        grid_spec=pltpu.PrefetchScalarGridSpec(
            num_scalar_prefetch=1, grid=(r // tr,),
            in_specs=[pl.BlockSpec((3, tr, c), lambda i, k: (0, i, 0)), own_spec],
            out_specs=pl.BlockSpec((tr, c), lambda i, k: (i, 0))),
        out_shape=jax.ShapeDtypeStruct((r, c), F32),
        compiler_params=_params(1),
    )(chip, parts, own)


def _adamw(w, g_parts, m, v, name):
    n_g = len(g_parts)
    if w.ndim == 2:
        r, c = w.shape
        tr = r
        for cand in (512, 256, 128, 64, 32, 16, 8):
            if r % cand == 0 and cand * c * 4 <= (2 << 20):
                tr = cand
                break
        spec = pl.BlockSpec((tr, c), lambda i: (i, 0))
        tiling = dict(grid=(r // tr,), in_specs=[spec] * (3 + n_g), out_specs=[spec] * 4, compiler_params=_params(1))
    else:
        tiling = dict(compiler_params=pltpu.CompilerParams(vmem_limit_bytes=48 << 20))

    def body(*refs):
        w_ref, g_refs, m_ref, v_ref = refs[0], refs[1:1 + n_g], refs[1 + n_g], refs[2 + n_g]
        g = g_refs[0][...]
        for gr in g_refs[1:]:
            g = g + gr[...]
        _adamw_update(g, w_ref, m_ref, v_ref, *refs[3 + n_g:])

    return pl.pallas_call(body, name=name, out_shape=[jax.ShapeDtypeStruct(w.shape, F32)] * 4, **tiling)(w, *g_parts, m, v)


def _adamw_update(g, w_ref, m_ref, v_ref, g_out, d_out, m_out, v_out):
    m_new = ADAM_B1 * m_ref[...] + (1.0 - ADAM_B1) * g
    v_new = ADAM_B2 * v_ref[...] + (1.0 - ADAM_B2) * (g * g)
    m_hat = m_new * (1.0 / (1.0 - ADAM_B1 ** ADAM_STEP))
    v_hat = v_new * (1.0 / (1.0 - ADAM_B2 ** ADAM_STEP))
    g_out[...] = g
    d_out[...] = -ADAM_LR * (m_hat / (jnp.sqrt(v_hat) + ADAM_EPS) + ADAM_WD * w_ref[...])
    m_out[...] = m_new
    v_out[...] = v_new


def _adamw_many(ws, gs, ms, vs, name):
    n = len(ws)

    def body(*refs):
        for k in range(n):
            _adamw_update(refs[n + k][...], refs[k], refs[2 * n + k], refs[3 * n + k],
                          refs[4 * n + k], refs[5 * n + k], refs[6 * n + k], refs[7 * n + k])

    outs = pl.pallas_call(body, name=name, out_shape=[jax.ShapeDtypeStruct(t.shape, F32) for t in ws] * 4,
                          compiler_params=pltpu.CompilerParams(vmem_limit_bytes=56 << 20))(*ws, *gs, *ms, *vs)
    return [outs[part * n:(part + 1) * n] for part in range(4)]


def _adamw_layers(w, q_mine, q_sib, m, v, name, after=None):
    n, r, c = w.shape
    tr = r
    for cand in (512, 256, 128, 64, 32, 16, 8):
        if r % cand == 0 and cand * c * 4 <= (1 << 20):
            tr = cand
            break

    def body(*refs):
        w_ref, qm, qs, m_ref, v_ref = refs[0], refs[1:1 + n], refs[1 + n:1 + 2 * n], refs[1 + 2 * n], refs[2 + 2 * n]
        layer = pl.program_id(0)
        g = qm[0][...] + qs[0][...]
        for k in range(1, n):
            g = jnp.where(layer == k, qm[k][...] + qs[k][...], g)
        _adamw_update(g, w_ref, m_ref, v_ref, *refs[3 + 2 * n:])

    stacked = pl.BlockSpec((None, tr, c), lambda l, i: (l, i, 0))
    per_layer = [pl.BlockSpec((tr, c), lambda l, i, k=k: (jnp.where(l == k, i, 0), 0)) for k in range(n)]
    call, tail = _call_after(
        after, body, 3 + 2 * n, name=name, grid=(n, r // tr),
        in_specs=[stacked] + per_layer + per_layer + [stacked, stacked], out_specs=[stacked] * 4,
        out_shape=[jax.ShapeDtypeStruct(w.shape, F32)] * 4,
        compiler_params=_params(2),
    )
    return call(w, *q_mine, *q_sib, m, v, *tail)


def _ada_fwd(c16, ada_w, ada_b_cols):
    cols = ada_w.shape[2]

    def body(c_ref, w_ref, b_ref, o_ref):
        cv = c_ref[...]
        ca = _bf(cv * jax.nn.sigmoid(cv))
        o_ref[...] = _dot(ca, _bf(w_ref[...])) + b_ref[...]

    return pl.pallas_call(
        body, name="ada_fwd", grid=(DEPTH,),
        in_specs=[_whole((16, D)), pl.BlockSpec((None, D, cols), lambda i: (i, 0, 0)),
                  pl.BlockSpec((None, 1, cols), lambda i: (i, 0, 0))],
        out_specs=pl.BlockSpec((None, 16, cols), lambda i: (i, 0, 0)),
        out_shape=jax.ShapeDtypeStruct((DEPTH, 16, cols), F32),
        compiler_params=_params(1),
    )(c16, ada_w, ada_b_cols)


def _ada_bwd(c16, dmod16):
    cols = dmod16.shape[2]

    def body(c_ref, d_ref, o_ref):
        cv = c_ref[...]
        ca = _bf(cv * jax.nn.sigmoid(cv))
        o_ref[...] = _dot_tn(ca, _bf(d_ref[...]))

    return pl.pallas_call(
        body, name="ada_bwd", grid=(DEPTH,),
        in_specs=[_whole((16, D)), pl.BlockSpec((None, 16, cols), lambda i: (i, 0, 0))],
        out_specs=pl.BlockSpec((None, D, cols), lambda i: (i, 0, 0)),
        out_shape=jax.ShapeDtypeStruct((DEPTH, D, cols), F32),
        compiler_params=_params(1),
    )(c16, dmod16)


def _mod_bwd(vs_mix, vs_ffn, pv):
    def body(m_ref, f_ref, pv_ref, o_ref):
        for i in range(DEPTH):
            vm, vf, p = m_ref[i], f_ref[i], pv_ref[i]
            o_ref[i] = jnp.concatenate([
                vm[2:3], vm[1:2] * p[R_N1:R_N1 + 1], vm[0:1],
                vf[2:3], vf[1:2] * p[R_N2:R_N2 + 1], vf[0:1],
                vm[1:2] * (1.0 + p[R_SC1:R_SC1 + 1]), vf[1:2] * (1.0 + p[R_SC2:R_SC2 + 1])], axis=0)

    return pl.pallas_call(body, name="mod_bwd", out_shape=jax.ShapeDtypeStruct((DEPTH, 8, D), F32))(vs_mix, vs_ffn, pv)


def _ffn_fwd(x1, pv, w1, w2, layer, tm, after=None):
    L = x1.shape[0]
    dff = w1.shape[2]

    def body(x1_ref, pv_ref, w1_ref, w2_ref, x2_ref, h2_ref, a_ref, f_ref):
        x1v, p = x1_ref[...], pv_ref[...]
        h2, _, _ = _norm_mod(x1v, p[R_N2:R_N2 + 1], p[R_SC2:R_SC2 + 1], p[R_SH2:R_SH2 + 1])
        hb = _bf(h2)
        h2_ref[...] = hb
        a = _dot(hb, w1_ref[...])
        a_ref[...] = a
        ra = jnp.maximum(a, 0.0)
        f = _dot(_bf(ra * ra), w2_ref[...])
        f_ref[...] = f
        x2_ref[...] = x1v + p[R_G2:R_G2 + 1] * f

    call, tail = _call_after(
        after, body, 4, name=f"ffn_fwd{layer}", grid=(L // tm,),
        in_specs=[_rows(tm, D), pl.BlockSpec((None, 8, D), lambda i: (layer, 0, 0)), _layer_w(D, dff, 0), _layer_w(dff, D, 0)],
        out_specs=[_rows(tm, D), _rows(tm, D), _rows(tm, dff), _rows(tm, D)],
        out_shape=[jax.ShapeDtypeStruct((L, D), F32), jax.ShapeDtypeStruct((L, D), BF16),
                   jax.ShapeDtypeStruct((L, dff), F32), jax.ShapeDtypeStruct((L, D), F32)],
        compiler_params=_params(1, 56),
    )
    return call(x1, pv, w1, w2, *tail)


def _ffn_bwd(dx2, x1, a, f, pv, w1, w2, layer, tm, after=None):
    L = x1.shape[0]
    dff = w1.shape[2]
    extra = [] if after is None else [pl.BlockSpec(memory_space=pl.ANY)]
    extra_args = [] if after is None else [after]

    def body(dx2_ref, x1_ref, a_ref, f_ref, pv_ref, w1_ref, w2_ref, *rest):
        dx1_ref, p_ref, da_ref, df_ref, vs_ref = rest[len(extra):]

        @pl.when(pl.program_id(0) == 0)
        def _():
            vs_ref[...] = jnp.zeros_like(vs_ref)

        dx2v, p = dx2_ref[...], pv_ref[...]
        dfb = _bf(dx2v * p[R_G2:R_G2 + 1])
        df_ref[...] = dfb
        vs_ref[0:1, :] += _sum0(dx2v * f_ref[...])
        dp = _dot_nt(dfb, w2_ref[...])
        ra = jnp.maximum(a_ref[...], 0.0)
        p_ref[...] = _bf(ra * ra)
        dab = _bf(dp * (2.0 * ra))
        da_ref[...] = dab
        dh2 = _dot_nt(dab, w1_ref[...])
        _, xn, r = _norm_mod(x1_ref[...], p[R_N2:R_N2 + 1], p[R_SC2:R_SC2 + 1], p[R_SH2:R_SH2 + 1])
        dx1_ref[...] = dx2v + _norm_mod_bwd(dh2, xn, r, p[R_N2:R_N2 + 1], p[R_SC2:R_SC2 + 1])
        vs_ref[1:2, :] += _sum0(dh2 * xn)
        vs_ref[2:3, :] += _sum0(dh2)

    return pl.pallas_call(
        body, name=f"ffn_bwd{layer}", grid=(L // tm,),
        in_specs=[_rows(tm, D), _rows(tm, D), _rows(tm, dff), _rows(tm, D),
                  pl.BlockSpec((None, 8, D), lambda i: (layer, 0, 0)), _layer_w(D, dff, 0), _layer_w(dff, D, 0)] + extra,
        out_specs=[_rows(tm, D), _rows(tm, dff), _rows(tm, dff), _rows(tm, D), _whole((8, D))],
        out_shape=[jax.ShapeDtypeStruct((L, D), F32), jax.ShapeDtypeStruct((L, dff), BF16),
                   jax.ShapeDtypeStruct((L, dff), BF16), jax.ShapeDtypeStruct((L, D), BF16),
                   jax.ShapeDtypeStruct((8, D), F32)],
        compiler_params=_params(1, 56),
    )(dx2, x1, a, f, pv, w1, w2, *extra_args)


def _conv_fwd(x, pv, w_in, w_out, cw, layer, j, tm, after=None):
    L = x.shape[0]

    def body(x_ref, pv_ref, win_ref, wout_ref, cw_ref, x1_ref, h_ref, bcx_ref, conv_ref, q_ref, y_ref, carry):
        @pl.when(pl.program_id(0) == 0)
        def _():
            carry[...] = jnp.zeros_like(carry)

        xv, p, cwv = x_ref[...], pv_ref[...], cw_ref[...]
        h, _, _ = _norm_mod(xv, p[R_N1:R_N1 + 1], p[R_SC1:R_SC1 + 1], p[R_SH1:R_SH1 + 1])
        hb = _bf(h)
        h_ref[...] = hb
        bcx = _dot(hb, win_ref[...])
        bcx_ref[...] = bcx
        z = bcx[:, D:2 * D] * bcx[:, 2 * D:]
        prev8 = carry[...]
        conv = cwv[0:1] * _shift_down(z, prev8, 2) + cwv[1:2] * _shift_down(z, prev8, 1) + cwv[2:3] * z + cwv[3:4]
        conv_ref[...] = conv
        qb = _bf(bcx[:, :D] * conv)
        q_ref[...] = qb
        y = _dot(qb, wout_ref[...])
        y_ref[...] = y
        x1_ref[...] = xv + p[R_G1:R_G1 + 1] * y
        carry[...] = z[tm - 8:tm]

    call, tail = _call_after(
        after, body, 5, name=f"conv_fwd{layer}", grid=(L // tm,),
        in_specs=[_rows(tm, D), pl.BlockSpec((None, 8, D), lambda i: (layer, 0, 0)), _layer_w(D, 3 * D, 0), _layer_w(D, D, 0),
                  pl.BlockSpec((None, 8, D), lambda i: (j, 0, 0))],
        out_specs=[_rows(tm, D), _rows(tm, D), _rows(tm, 3 * D), _rows(tm, D), _rows(tm, D), _rows(tm, D)],
        out_shape=[jax.ShapeDtypeStruct((L, D), F32), jax.ShapeDtypeStruct((L, D), BF16), jax.ShapeDtypeStruct((L, 3 * D), F32),
                   jax.ShapeDtypeStruct((L, D), F32), jax.ShapeDtypeStruct((L, D), BF16), jax.ShapeDtypeStruct((L, D), F32)],
        scratch_shapes=[pltpu.VMEM((8, D), F32)],
        compiler_params=_params(1, 56),
    )
    return call(x, pv, w_in, w_out, cw, *tail)


def _conv_bwd(dx1, x, y, bcx, conv, pv, w_in, w_out, cw, layer, j, tm, after=None):
    L = x.shape[0]
    nt = L // tm

    def body(dx1_ref, x_ref, y_ref, bcx_ref, conv_ref, halo_ref, pv_ref, win_ref, wout_ref, cw_ref,
             dx_ref, dbcx_ref, dy_ref, vs_ref, carry):
        gi = pl.program_id(0)
        tile = nt - 1 - gi

        @pl.when(gi == 0)
        def _():
            vs_ref[...] = jnp.zeros_like(vs_ref)
            carry[...] = jnp.zeros_like(carry)

        dx1v, p, cwv = dx1_ref[...], pv_ref[...], cw_ref[...]
        dyb = _bf(dx1v * p[R_G1:R_G1 + 1])
        dy_ref[...] = dyb
        vs_ref[0:1, :] += _sum0(dx1v * y_ref[...])
        dq = _dot_nt(dyb, wout_ref[...])
        bcx = bcx_ref[...]
        b, cg, xh = bcx[:, :D], bcx[:, D:2 * D], bcx[:, 2 * D:]
        db = dq * conv_ref[...]
        dc = dq * b
        z = cg * xh
        halo = halo_ref[...]
        zprev = jnp.where(tile > 0, halo[:, D:2 * D] * halo[:, 2 * D:], 0.0)
        vs_ref[3:4, :] += _sum0(dc * _shift_down(z, zprev, 2))
        vs_ref[4:5, :] += _sum0(dc * _shift_down(z, zprev, 1))
        vs_ref[5:6, :] += _sum0(dc * z)
        vs_ref[6:7, :] += _sum0(dc)
        next8 = carry[...]
        dz = cwv[2:3] * dc + cwv[1:2] * _shift_up(dc, next8, 1) + cwv[0:1] * _shift_up(dc, next8, 2)
        dbb, dcgb, dxhb = _bf(db), _bf(dz * xh), _bf(dz * cg)
        dbcx_ref[:, 0:D] = dbb
        dbcx_ref[:, D:2 * D] = dcgb
        dbcx_ref[:, 2 * D:3 * D] = dxhb
        dh = (_dot_nt(dbb, win_ref[:, 0:D]) + _dot_nt(dcgb, win_ref[:, D:2 * D])) + _dot_nt(dxhb, win_ref[:, 2 * D:3 * D])
        _, xn, r = _norm_mod(x_ref[...], p[R_N1:R_N1 + 1], p[R_SC1:R_SC1 + 1], p[R_SH1:R_SH1 + 1])
        dx_ref[...] = dx1v + _norm_mod_bwd(dh, xn, r, p[R_N1:R_N1 + 1], p[R_SC1:R_SC1 + 1])
        vs_ref[1:2, :] += _sum0(dh * xn)
        vs_ref[2:3, :] += _sum0(dh)
        carry[...] = dc[0:8]

    halo_spec = pl.BlockSpec((8, 3 * D), lambda i: (jnp.maximum((nt - 1 - i) * (tm // 8) - 1, 0), 0))
    call, tail = _call_after(
        after, body, 10, name=f"conv_bwd{layer}", grid=(nt,),
        in_specs=[_rows(tm, D, nt), _rows(tm, D, nt), _rows(tm, D, nt), _rows(tm, 3 * D, nt), _rows(tm, D, nt), halo_spec,
                  pl.BlockSpec((None, 8, D), lambda i: (layer, 0, 0)), _layer_w(D, 3 * D, 0), _layer_w(D, D, 0),
                  pl.BlockSpec((None, 8, D), lambda i: (j, 0, 0))],
        out_specs=[_rows(tm, D, nt), _rows(tm, 3 * D, nt), _rows(tm, D, nt), _whole((8, D))],
        out_shape=[jax.ShapeDtypeStruct((L, D), F32), jax.ShapeDtypeStruct((L, 3 * D), BF16),
                   jax.ShapeDtypeStruct((L, D), BF16), jax.ShapeDtypeStruct((8, D), F32)],
        scratch_shapes=[pltpu.VMEM((8, D), F32)],
        compiler_params=_params(1, 56),
    )
    return call(dx1, x, y, bcx, conv, bcx, pv, w_in, w_out, cw, *tail)


def _s5_discretize(a_re, a_im, log_dt, bt_re, bt_im):
    dt = jnp.exp(log_dt)
    mag = jnp.exp(a_re * dt)
    abar_re = mag * jnp.cos(a_im * dt)
    abar_im = mag * jnp.sin(a_im * dt)
    den = a_re * a_re + a_im * a_im
    nr = abar_re - 1.0
    ni = abar_im
    f_re = (nr * a_re + ni * a_im) / den
    f_im = (ni * a_re - nr * a_im) / den
    bbar_re = f_re * bt_re - f_im * bt_im
    bbar_im = f_re * bt_im + f_im * bt_re
    return abar_re, abar_im, bbar_re, bbar_im


def _s5_params_fwd(a_re, a_im, log_dt, bt_re, bt_im):
    def body(ar, ai, ld, br, bi, o_ar, o_ai, o_br, o_bi):
        r = _s5_discretize(ar[...], ai[...], ld[...], br[...], bi[...])
        o_ar[...], o_ai[...], o_br[...], o_bi[...] = r

    gp = jax.ShapeDtypeStruct((S5_G, S5_P), F32)
    hgp = jax.ShapeDtypeStruct((S5_H, S5_G, S5_P), F32)
    return pl.pallas_call(body, name="s5_params_fwd", out_shape=[gp, gp, hgp, hgp])(a_re, a_im, log_dt, bt_re, bt_im)


def _s5_params_bwd(a_re, a_im, log_dt, bt_re, bt_im, d_ar, d_ai, d_br, d_bi):
    def body(ar, ai, ld, br, bi, gar, gai, gbr, gbi, o_ar, o_ai, o_ld, o_br, o_bi):
        _, vjp = jax.vjp(_s5_discretize, ar[...], ai[...], ld[...], br[...], bi[...])
        r = vjp((gar[...], gai[...], gbr[...], gbi[...]))
        o_ar[...], o_ai[...], o_ld[...], o_br[...], o_bi[...] = r

    gp = jax.ShapeDtypeStruct((S5_G, S5_P), F32)
    hgp = jax.ShapeDtypeStruct((S5_H, S5_G, S5_P), F32)
    return pl.pallas_call(body, name="s5_params_bwd", out_shape=[gp, gp, jax.ShapeDtypeStruct((S5_G, 1), F32), hgp, hgp])(
        a_re, a_im, log_dt, bt_re, bt_im, d_ar, d_ai, d_br, d_bi)


NSEG = 8
SCAN_LANES = 1024


def _to_segments(x):
    n, c = x.shape
    return x.reshape(NSEG, n // NSEG, c).transpose(1, 0, 2).reshape(n, c)


def _from_segments(x):
    n, c = x.shape
    return x.reshape(n // NSEG, NSEG, c).transpose(1, 0, 2).reshape(n, c)


def _segment_scan(re_ref, im_ref, st_re, st_im, a_re, a_im, n_slabs, adjoint, write):
    for q in range(NSTATE // SCAN_LANES):
        ls = slice(q * SCAN_LANES, (q + 1) * SCAN_LANES)
        ar = jnp.broadcast_to(a_re[:, ls], (8, SCAN_LANES))
        ai = jnp.broadcast_to(a_im[:, ls], (8, SCAN_LANES))

        def step(k, carry, ls=ls, ar=ar, ai=ai):
            s_r, s_i = carry
            slab = (n_slabs - 1 - k) if adjoint else k
            rows = pl.ds(pl.multiple_of(slab * 8, 8), 8)
            b_r, b_i = re_ref[rows, ls], im_ref[rows, ls]
            if adjoint:
                n_r = b_r + ar * s_r + ai * s_i
                n_i = b_i - ai * s_r + ar * s_i
            else:
                n_r = ar * s_r - ai * s_i + b_r
                n_i = ar * s_i + ai * s_r + b_i
            if write:
                re_ref[rows, ls] = n_r
                im_ref[rows, ls] = n_i
            return n_r, n_i

        s_r, s_i = lax.fori_loop(0, n_slabs, step, (st_re[:, ls], st_im[:, ls]), unroll=4)
        st_re[:, ls] = s_r
        st_im[:, ls] = s_i


def _s5_segment_states(e_re, e_im, ar, ai, seg_len, adjoint):
    def body(ere_ref, eim_ref, ar_ref, ai_ref, ore_ref, oim_ref):
        p_r, p_i = ar_ref[...], ai_ref[...]
        if adjoint:
            p_i = -p_i
        acc_r, acc_i = jnp.ones_like(p_r), jnp.zeros_like(p_r)
        n = seg_len
        while n:
            if n & 1:
                acc_r, acc_i = acc_r * p_r - acc_i * p_i, acc_r * p_i + acc_i * p_r
            n >>= 1
            if n:
                p_r, p_i = p_r * p_r - p_i * p_i, 2.0 * p_r * p_i
        e_r, e_i = ere_ref[...], eim_ref[...]
        s_r, s_i = jnp.zeros_like(acc_r), jnp.zeros_like(acc_r)
        rows_r, rows_i = [None] * NSEG, [None] * NSEG
        order = range(NSEG - 1, -1, -1) if adjoint else range(NSEG)
        for j in order:
            rows_r[j], rows_i[j] = s_r, s_i
            s_r, s_i = (acc_r * s_r - acc_i * s_i + e_r[j:j + 1], acc_r * s_i + acc_i * s_r + e_i[j:j + 1])
        ore_ref[...] = jnp.concatenate(rows_r, axis=0)
        oim_ref[...] = jnp.concatenate(rows_i, axis=0)

    st = jax.ShapeDtypeStruct((NSEG, NSTATE), F32)
    return pl.pallas_call(body, name="s5_segment_states_bwd" if adjoint else "s5_segment_states_fwd", out_shape=[st, st])(
        e_re, e_im, ar, ai)


def _s5_fwd_ends(x, pv, w_in, b_re, b_im, ar, ai, layer, tm, after=None):
    L = x.shape[0]

    def body(x_ref, pv_ref, win_ref, bre_ref, bim_ref, ar_ref, ai_ref, h_ref, u_ref, ere_ref, eim_ref, bu_re, bu_im):
        @pl.when(pl.program_id(0) == 0)
        def _():
            ere_ref[...] = jnp.zeros_like(ere_ref)
            eim_ref[...] = jnp.zeros_like(eim_ref)

        p = pv_ref[...]
        h, _, _ = _norm_mod(x_ref[...], p[R_N1:R_N1 + 1], p[R_SC1:R_SC1 + 1], p[R_SH1:R_SH1 + 1])
        hb = _bf(h)
        h_ref[...] = hb
        u = _dot(hb, win_ref[...])
        u_ref[...] = u
        ub = _bf(u)
        for k in range(S5_NB):
            uk = ub[:, k * S5_BH:(k + 1) * S5_BH]
            bu_re[:, k * S5_BP:(k + 1) * S5_BP] = _dot(uk, bre_ref[k])
            bu_im[:, k * S5_BP:(k + 1) * S5_BP] = _dot(uk, bim_ref[k])
        _segment_scan(bu_re, bu_im, ere_ref, eim_ref, ar_ref[...], ai_ref[...], tm // 8, adjoint=False, write=False)

    call, tail = _call_after(
        after, body, 7, name="s5_fwd_ends", grid=(L // tm,),
        in_specs=[_rows(tm, D), pl.BlockSpec((None, 8, D), lambda i: (layer, 0, 0)), _layer_w(D, D, 0),
                  _const_w((S5_NB, S5_BH, S5_BP)), _const_w((S5_NB, S5_BH, S5_BP)), _whole((1, NSTATE)), _whole((1, NSTATE))],
        out_specs=[_rows(tm, D), _rows(tm, D), _whole((NSEG, NSTATE)), _whole((NSEG, NSTATE))],
        out_shape=[jax.ShapeDtypeStruct((L, D), BF16), jax.ShapeDtypeStruct((L, D), F32),
                   jax.ShapeDtypeStruct((NSEG, NSTATE), F32), jax.ShapeDtypeStruct((NSEG, NSTATE), F32)],
        scratch_shapes=[pltpu.VMEM((tm, NSTATE), F32), pltpu.VMEM((tm, NSTATE), F32)],
        compiler_params=_params(1, 56),
    )
    return call(x, pv, w_in, b_re, b_im, ar, ai, *tail)


def _s5_fwd_out(x, u, pv, b_re, b_im, s0_re, s0_im, ar, ai, c_re, c_im, dvec, glu_w, glu_b, w_out, layer, tm):
    L = x.shape[0]

    def body(x_ref, u_ref, pv_ref, bre_ref, bim_ref, s0re_ref, s0im_ref, ar_ref, ai_ref, cre_ref, cim_ref, d_ref, gw_ref,
             gb_ref, wout_ref, x1_ref, sre_ref, sim_ref, y1_ref, zg_ref, y3_ref, y_ref, st_re, st_im):
        @pl.when(pl.program_id(0) == 0)
        def _():
            st_re[...] = s0re_ref[...]
            st_im[...] = s0im_ref[...]

        p = pv_ref[...]
        uv = u_ref[...]
        ub = _bf(uv)
        for k in range(S5_NB):
            uk = ub[:, k * S5_BH:(k + 1) * S5_BH]
            sre_ref[:, k * S5_BP:(k + 1) * S5_BP] = _dot(uk, bre_ref[k])
            sim_ref[:, k * S5_BP:(k + 1) * S5_BP] = _dot(uk, bim_ref[k])
        _segment_scan(sre_ref, sim_ref, st_re, st_im, ar_ref[...], ai_ref[...], tm // 8, adjoint=False, write=True)
        parts = []
        for k in range(S5_NB):
            sl = slice(k * S5_BP, (k + 1) * S5_BP)
            parts.append(_dot(_bf(sre_ref[:, sl]), cre_ref[k]) - _dot(_bf(sim_ref[:, sl]), cim_ref[k]))
        y1 = jnp.concatenate(parts, axis=1) + d_ref[...] * uv
        y1_ref[...] = y1
        y2 = jax.nn.gelu(y1)
        zg = _dot(_bf(y2), gw_ref[...]) + gb_ref[...]
        zg_ref[...] = zg
        y3b = _bf(y2 * jax.nn.sigmoid(zg))
        y3_ref[...] = y3b
        y = _dot(y3b, wout_ref[...])
        y_ref[...] = y
        x1_ref[...] = x_ref[...] + p[R_G1:R_G1 + 1] * y

    return pl.pallas_call(
        body, name="s5_fwd_out", grid=(L // tm,),
        in_specs=[_rows(tm, D), _rows(tm, D), pl.BlockSpec((None, 8, D), lambda i: (layer, 0, 0)),
                  _const_w((S5_NB, S5_BH, S5_BP)), _const_w((S5_NB, S5_BH, S5_BP)),
                  _whole((NSEG, NSTATE)), _whole((NSEG, NSTATE)), _whole((1, NSTATE)), _whole((1, NSTATE)),
                  _const_w((S5_NB, S5_BP, S5_BH)), _const_w((S5_NB, S5_BP, S5_BH)), _whole((1, D)),
                  _layer_w(D, D, 0), _whole((1, D)), _layer_w(D, D, 0)],
        out_specs=[_rows(tm, D), _rows(tm, NSTATE), _rows(tm, NSTATE), _rows(tm, D), _rows(tm, D), _rows(tm, D), _rows(tm, D)],
        out_shape=[jax.ShapeDtypeStruct((L, D), F32), jax.ShapeDtypeStruct((L, NSTATE), F32), jax.ShapeDtypeStruct((L, NSTATE), F32),
                   jax.ShapeDtypeStruct((L, D), F32), jax.ShapeDtypeStruct((L, D), F32),
                   jax.ShapeDtypeStruct((L, D), BF16), jax.ShapeDtypeStruct((L, D), F32)],
        scratch_shapes=[pltpu.VMEM((NSEG, NSTATE), F32), pltpu.VMEM((NSEG, NSTATE), F32)],
        compiler_params=_params(1, 56),
    )(x, u, pv, b_re, b_im, s0_re, s0_im, ar, ai, c_re, c_im, dvec, glu_w, glu_b, w_out)


def _s5_bwd_ends(dx1, y, y1, zg, u, pv, c_re, c_im, ar, ai, dvec, glu_w, w_out, layer, tm, after=None):
    L = dx1.shape[0]
    nt = L // tm

    def body(dx1_ref, y_ref, y1_ref, zg_ref, u_ref, pv_ref, cre_ref, cim_ref, ar_ref, ai_ref, d_ref, gw_ref, wout_ref,
             dy_ref, y2_ref, dzg_ref, dy1_ref, dus_ref, ere_ref, eim_ref, vs_ref, g_re, g_im):
        @pl.when(pl.program_id(0) == 0)
        def _():
            vs_ref[...] = jnp.zeros_like(vs_ref)
            ere_ref[...] = jnp.zeros_like(ere_ref)
            eim_ref[...] = jnp.zeros_like(eim_ref)

        dx1v, p = dx1_ref[...], pv_ref[...]
        dyb = _bf(dx1v * p[R_G1:R_G1 + 1])
        dy_ref[...] = dyb
        vs_ref[0:1, :] += _sum0(dx1v * y_ref[...])
        dy3 = _dot_nt(dyb, wout_ref[...])
        y2, gelu_vjp = jax.vjp(jax.nn.gelu, y1_ref[...])
        y2_ref[...] = _bf(y2)
        gate = jax.nn.sigmoid(zg_ref[...])
        dzg = dy3 * y2 * gate * (1.0 - gate)
        dzgb = _bf(dzg)
        dzg_ref[...] = dzgb
        vs_ref[1:2, :] += _sum0(dzg)
        dy2 = dy3 * gate + _dot_nt(dzgb, gw_ref[...])
        dy1 = gelu_vjp(dy2)[0]
        vs_ref[2:3, :] += _sum0(dy1 * u_ref[...])
        dus_ref[...] = dy1 * d_ref[...]
        dy1b = _bf(dy1)
        dy1_ref[...] = dy1b
        for k in range(S5_NB):
            dk = dy1b[:, k * S5_BH:(k + 1) * S5_BH]
            g_re[:, k * S5_BP:(k + 1) * S5_BP] = _dot_nt(dk, cre_ref[k])
            g_im[:, k * S5_BP:(k + 1) * S5_BP] = -_dot_nt(dk, cim_ref[k])
        _segment_scan(g_re, g_im, ere_ref, eim_ref, ar_ref[...], ai_ref[...], tm // 8, adjoint=True, write=False)

    call, tail = _call_after(
        after, body, 13, name="s5_bwd_ends", grid=(nt,),
        in_specs=[_rows(tm, D, nt)] * 5 + [pl.BlockSpec((None, 8, D), lambda i: (layer, 0, 0)),
                  _const_w((S5_NB, S5_BP, S5_BH)), _const_w((S5_NB, S5_BP, S5_BH)), _whole((1, NSTATE)), _whole((1, NSTATE)),
                  _whole((1, D)), _layer_w(D, D, 0), _layer_w(D, D, 0)],
        out_specs=[_rows(tm, D, nt)] * 5 + [_whole((NSEG, NSTATE)), _whole((NSEG, NSTATE)), _whole((8, D))],
        out_shape=[jax.ShapeDtypeStruct((L, D), BF16)] * 4 + [jax.ShapeDtypeStruct((L, D), F32),
                   jax.ShapeDtypeStruct((NSEG, NSTATE), F32), jax.ShapeDtypeStruct((NSEG, NSTATE), F32),
                   jax.ShapeDtypeStruct((8, D), F32)],
        scratch_shapes=[pltpu.VMEM((tm, NSTATE), F32), pltpu.VMEM((tm, NSTATE), F32)],
        compiler_params=_params(1, 56),
    )
    return call(dx1, y, y1, zg, u, pv, c_re, c_im, ar, ai, dvec, glu_w, w_out, *tail)


def _s5_bwd_in(dx1, dy1_b, du_skip, x, s_re, s_im, pv, b_re, b_im, c_re, c_im, l0_re, l0_im, ar, ai, w_in, layer, tm):
    L = x.shape[0]
    nt = L // tm

    def body(dx1_ref, dy1_ref, dus_ref, x_ref, sre_ref, sim_ref, hre_ref, him_ref, lre_ref, lim_ref, pv_ref, bre_ref, bim_ref,
             cre_ref, cim_ref, l0re_ref, l0im_ref, ar_ref, ai_ref, win_ref,
             dx_ref, du_ref, lamre_ref, lamim_ref, da_ref, vs_ref, g_re, g_im, st_re, st_im):
        gi = pl.program_id(0)
        tile = nt - 1 - gi

        @pl.when(gi == 0)
        def _():
            vs_ref[...] = jnp.zeros_like(vs_ref)
            da_ref[...] = jnp.zeros_like(da_ref)
            st_re[...] = l0re_ref[...]
            st_im[...] = l0im_ref[...]

        p = pv_ref[...]
        dy1b = dy1_ref[...]
        for k in range(S5_NB):
            dk = dy1b[:, k * S5_BH:(k + 1) * S5_BH]
            g_re[:, k * S5_BP:(k + 1) * S5_BP] = _dot_nt(dk, cre_ref[k])
            g_im[:, k * S5_BP:(k + 1) * S5_BP] = -_dot_nt(dk, cim_ref[k])
        _segment_scan(g_re, g_im, st_re, st_im, ar_ref[...], ai_ref[...], tm // 8, adjoint=True, write=True)
        lam_r, lam_i = g_re[...], g_im[...]
        lrb, lib = _bf(lam_r), _bf(lam_i)
        lamre_ref[...] = lrb
        lamim_ref[...] = lib

        def wrapped(last_ref):
            z = last_ref[...]
            row = lax.broadcasted_iota(jnp.int32, z.shape, 0)
            return jnp.where(row >= 1, pltpu.roll(z, 1, 0), 0.0)

        first_r = jnp.where(tile > 0, hre_ref[...], wrapped(lre_ref))
        first_i = jnp.where(tile > 0, him_ref[...], wrapped(lim_ref))
        sp_r = jnp.concatenate([first_r, sre_ref[0:tm - 8, :]], axis=0)
        sp_i = jnp.concatenate([first_i, sim_ref[0:tm - 8, :]], axis=0)
        da_ref[0:1, :] += _sum0(lam_r * sp_r + lam_i * sp_i)
        da_ref[1:2, :] += _sum0(lam_i * sp_r - lam_r * sp_i)

        parts = []
        for k in range(S5_NB):
            sl = slice(k * S5_BP, (k + 1) * S5_BP)
            parts.append(_dot_nt(lrb[:, sl], bre_ref[k]) + _dot_nt(lib[:, sl], bim_ref[k]))
        dub = _bf(jnp.concatenate(parts, axis=1) + dus_ref[...])
        du_ref[...] = dub
        dh = _dot_nt(dub, win_ref[...])
        _, xn, r = _norm_mod(x_ref[...], p[R_N1:R_N1 + 1], p[R_SC1:R_SC1 + 1], p[R_SH1:R_SH1 + 1])
        dx_ref[...] = dx1_ref[...] + _norm_mod_bwd(dh, xn, r, p[R_N1:R_N1 + 1], p[R_SC1:R_SC1 + 1])
        vs_ref[1:2, :] += _sum0(dh * xn)
        vs_ref[2:3, :] += _sum0(dh)

    halo = pl.BlockSpec((8, NSTATE), lambda i: (jnp.maximum((nt - 1 - i) * (tm // 8) - 1, 0), 0))
    last = pl.BlockSpec((8, NSTATE), lambda i: (L // 8 - 1, 0))
    return pl.pallas_call(
        body, name="s5_bwd_in", grid=(nt,),
        in_specs=[_rows(tm, D, nt), _rows(tm, D, nt), _rows(tm, D, nt), _rows(tm, D, nt), _rows(tm, NSTATE, nt), _rows(tm, NSTATE, nt),
                  halo, halo, last, last, pl.BlockSpec((None, 8, D), lambda i: (layer, 0, 0)),
                  _const_w((S5_NB, S5_BH, S5_BP)), _const_w((S5_NB, S5_BH, S5_BP)),
                  _const_w((S5_NB, S5_BP, S5_BH)), _const_w((S5_NB, S5_BP, S5_BH)),
                  _whole((NSEG, NSTATE)), _whole((NSEG, NSTATE)), _whole((1, NSTATE)), _whole((1, NSTATE)), _layer_w(D, D, 0)],
        out_specs=[_rows(tm, D, nt), _rows(tm, D, nt), _rows(tm, NSTATE, nt), _rows(tm, NSTATE, nt), _whole((8, NSTATE)), _whole((8, D))],
        out_shape=[jax.ShapeDtypeStruct((L, D), F32), jax.ShapeDtypeStruct((L, D), BF16),
                   jax.ShapeDtypeStruct((L, NSTATE), BF16), jax.ShapeDtypeStruct((L, NSTATE), BF16),
                   jax.ShapeDtypeStruct((8, NSTATE), F32), jax.ShapeDtypeStruct((8, D), F32)],
        scratch_shapes=[pltpu.VMEM((tm, NSTATE), F32), pltpu.VMEM((tm, NSTATE), F32),
                        pltpu.VMEM((NSEG, NSTATE), F32), pltpu.VMEM((NSEG, NSTATE), F32)],
        compiler_params=_params(1, 60),
    )(dx1, dy1_b, du_skip, x, s_re, s_im, s_re, s_im, s_re, s_im, pv, b_re, b_im, c_re, c_im, l0_re, l0_im, ar, ai, w_in)


def _blockdiag_b(bt):
    b = bt.reshape(S5_H, S5_NB, 16, S5_P).transpose(1, 2, 0, 3)
    eye = jnp.eye(16, dtype=bt.dtype)
    return (b[:, :, :, None, :] * eye[None, :, None, :, None]).reshape(S5_NB, S5_BH, S5_BP)


def _unblock_b(d):
    d = jnp.einsum("bghgp->bghp", d.reshape(S5_NB, 16, S5_H, 16, S5_P))
    return d.transpose(2, 0, 1, 3).reshape(S5_H, S5_G, S5_P)


def _blockdiag_c(cm):
    c4 = cm.reshape(S5_NB, 16, S5_H, S5_P)
    eye = jnp.eye(16, dtype=cm.dtype)
    out = c4.transpose(0, 1, 3, 2)[:, :, :, None, :] * eye[None, :, None, :, None]
    return out.reshape(S5_NB, S5_BP, S5_BH)


def _unblock_c(d):
    d = jnp.einsum("bgpgh->bghp", d.reshape(S5_NB, 16, S5_P, 16, S5_H))
    return d.reshape(S5_G, S5_H, S5_P)


def _tril_mask():
    return lax.broadcasted_iota(jnp.int32, (SG_CHUNK, SG_CHUNK), 0) >= lax.broadcasted_iota(jnp.int32, (SG_CHUNK, SG_CHUNK), 1)


def _sg_fwd(x, pv, w_in, w_s, b_t, vg, w_out, layer, tm, after=None):
    L = x.shape[0]
    nc = tm // SG_CHUNK

    def body(x_ref, pv_ref, win_ref, ws_ref, bt_ref, vg_ref, wout_ref, x1_ref, h_ref, uv_ref, vm_ref, q_ref, y_ref):
        xv, p = x_ref[...], pv_ref[...]
        h, _, _ = _norm_mod(xv, p[R_N1:R_N1 + 1], p[R_SC1:R_SC1 + 1], p[R_SH1:R_SH1 + 1])
        hb = _bf(h)
        h_ref[...] = hb
        uv = _dot(hb, win_ref[...])
        uv_ref[...] = uv
        v = uv[:, D:]
        rv = lax.rsqrt(jnp.mean(v * v, axis=-1, keepdims=True) + EPS)
        vnb = _bf((v * rv) * vg_ref[...])
        mask = _tril_mask()
        bt = bt_ref[...]
        for hd in range(SG_HEADS):
            wm = _bf(jnp.where(mask, ws_ref[hd], 0.0))
            cs = slice(hd * SG_CHUNK, (hd + 1) * SG_CHUNK)
            for ck in range(nc):
                rs = slice(ck * SG_CHUNK, (ck + 1) * SG_CHUNK)
                vm_ref[rs, cs] = _dot(wm, vnb[rs, cs]) + bt[:, hd:hd + 1]
        qb = _bf(uv[:, :D] * vm_ref[...])
        q_ref[...] = qb
        y = _dot(qb, wout_ref[...])
        y_ref[...] = y
        x1_ref[...] = xv + p[R_G1:R_G1 + 1] * y

    call, tail = _call_after(
        after, body, 7, name="sg_fwd", grid=(L // tm,),
        in_specs=[_rows(tm, D), pl.BlockSpec((None, 8, D), lambda i: (layer, 0, 0)), _layer_w(D, 2 * D, 0),
                  _whole((SG_HEADS, SG_CHUNK, SG_CHUNK)), _whole((SG_CHUNK, SG_HEADS)), _whole((1, D)), _layer_w(D, D, 0)],
        out_specs=[_rows(tm, D), _rows(tm, D), _rows(tm, 2 * D), _rows(tm, D), _rows(tm, D), _rows(tm, D)],
        out_shape=[jax.ShapeDtypeStruct((L, D), F32), jax.ShapeDtypeStruct((L, D), BF16), jax.ShapeDtypeStruct((L, 2 * D), F32),
                   jax.ShapeDtypeStruct((L, D), F32), jax.ShapeDtypeStruct((L, D), BF16), jax.ShapeDtypeStruct((L, D), F32)],
        compiler_params=_params(1, 56),
    )
    return call(x, pv, w_in, w_s, b_t, vg, w_out, *tail)


def _sg_bwd(dx1, x, y, uv, vm, pv, w_in, w_s, vg, w_out, layer, tm, after=None):
    L = x.shape[0]
    nc = tm // SG_CHUNK

    def body(dx1_ref, x_ref, y_ref, uv_ref, vm_ref, pv_ref, win_ref, ws_ref, vg_ref, wout_ref,
             dx_ref, duv_ref, dy_ref, vs_ref, dws_ref, dbt_ref, dvn_scr):
        @pl.when(pl.program_id(0) == 0)
        def _():
            vs_ref[...] = jnp.zeros_like(vs_ref)
            dws_ref[...] = jnp.zeros_like(dws_ref)
            dbt_ref[...] = jnp.zeros_like(dbt_ref)

        dx1v, p = dx1_ref[...], pv_ref[...]
        dyb = _bf(dx1v * p[R_G1:R_G1 + 1])
        dy_ref[...] = dyb
        vs_ref[0:1, :] += _sum0(dx1v * y_ref[...])
        dq = _dot_nt(dyb, wout_ref[...])
        uv = uv_ref[...]
        u, v = uv[:, :D], uv[:, D:]
        dub = _bf(dq * vm_ref[...])
        dvm = dq * u
        dvmb = _bf(dvm)
        rv = lax.rsqrt(jnp.mean(v * v, axis=-1, keepdims=True) + EPS)
        vh = v * rv
        vgv = vg_ref[...]
        vnb = _bf(vh * vgv)
        mask = _tril_mask()
        for hd in range(SG_HEADS):
            wm = _bf(jnp.where(mask, ws_ref[hd], 0.0))
            cs = slice(hd * SG_CHUNK, (hd + 1) * SG_CHUNK)
            dws = jnp.zeros((SG_CHUNK, SG_CHUNK), F32)
            dbs = jnp.zeros((SG_CHUNK, 1), F32)
            for ck in range(nc):
                rs = slice(ck * SG_CHUNK, (ck + 1) * SG_CHUNK)
                dvn_scr[rs, cs] = _dot_tn(wm, dvmb[rs, cs])
                dws = dws + _dot_nt(dvmb[rs, cs], vnb[rs, cs])
                dbs = dbs + jnp.sum(dvm[rs, cs], axis=1, keepdims=True)
            dws_ref[hd] += jnp.where(mask, dws, 0.0)
            dbt_ref[:, hd:hd + 1] += dbs
        dvn = dvn_scr[...]
        vs_ref[3:4, :] += _sum0(dvn * vh)
        dvnn = dvn * vgv
        dvb = _bf(rv * (dvnn - vh * jnp.mean(dvnn * vh, axis=-1, keepdims=True)))
        duv_ref[:, 0:D] = dub
        duv_ref[:, D:2 * D] = dvb
        dh = _dot_nt(dub, win_ref[:, 0:D]) + _dot_nt(dvb, win_ref[:, D:2 * D])
        _, xn, r = _norm_mod(x_ref[...], p[R_N1:R_N1 + 1], p[R_SC1:R_SC1 + 1], p[R_SH1:R_SH1 + 1])
        dx_ref[...] = dx1v + _norm_mod_bwd(dh, xn, r, p[R_N1:R_N1 + 1], p[R_SC1:R_SC1 + 1])
        vs_ref[1:2, :] += _sum0(dh * xn)
        vs_ref[2:3, :] += _sum0(dh)

    call, tail = _call_after(
        after, body, 10, name="sg_bwd", grid=(L // tm,),
        in_specs=[_rows(tm, D), _rows(tm, D), _rows(tm, D), _rows(tm, 2 * D), _rows(tm, D),
                  pl.BlockSpec((None, 8, D), lambda i: (layer, 0, 0)), _layer_w(D, 2 * D, 0),
                  _whole((SG_HEADS, SG_CHUNK, SG_CHUNK)), _whole((1, D)), _layer_w(D, D, 0)],
        out_specs=[_rows(tm, D), _rows(tm, 2 * D), _rows(tm, D), _whole((8, D)),
                   _whole((SG_HEADS, SG_CHUNK, SG_CHUNK)), _whole((SG_CHUNK, SG_HEADS))],
        out_shape=[jax.ShapeDtypeStruct((L, D), F32), jax.ShapeDtypeStruct((L, 2 * D), BF16), jax.ShapeDtypeStruct((L, D), BF16),
                   jax.ShapeDtypeStruct((8, D), F32), jax.ShapeDtypeStruct((SG_HEADS, SG_CHUNK, SG_CHUNK), F32),
                   jax.ShapeDtypeStruct((SG_CHUNK, SG_HEADS), F32)],
        scratch_shapes=[pltpu.VMEM((tm, D), F32)],
        compiler_params=_params(1, 56),
    )
    return call(dx1, x, y, uv, vm, pv, w_in, w_s, vg, w_out, *tail)


def _final(x, target, fg, tm):
    L = x.shape[0]

    def body(x_ref, t_ref, g_ref, dx_ref, vs_ref):
        @pl.when(pl.program_id(0) == 0)
        def _():
            vs_ref[...] = jnp.zeros_like(vs_ref)

        xv, g = x_ref[...], g_ref[...]
        r = lax.rsqrt(jnp.mean(xv * xv, axis=-1, keepdims=True) + EPS)
        xn = xv * r
        e = xn * g - t_ref[...]
        vs_ref[0:1, :] += jnp.sum(e * e)
        dout = e * (1.0 / D)
        vs_ref[1:2, :] += _sum0(dout * xn)
        dxn = dout * g
        dx_ref[...] = r * (dxn - xn * jnp.mean(dxn * xn, axis=-1, keepdims=True))

    return pl.pallas_call(
        body, name="final_loss", grid=(L // tm,),
        in_specs=[_rows(tm, D), _rows(tm, D), _whole((1, D))],
        out_specs=[_rows(tm, D), _whole((8, D))],
        out_shape=[jax.ShapeDtypeStruct((L, D), F32), jax.ShapeDtypeStruct((8, D), F32)],
        compiler_params=_params(1),
    )(x, target, fg)


def _pack_flat(arrs, multiple=LANES):
    flat = jnp.concatenate([a.reshape(-1).astype(F32) for a in arrs])
    return jnp.pad(flat, (0, -flat.shape[0] % multiple))


def _pack(arrs, row_multiple=8):
    return _pack_flat(arrs, row_multiple * LANES).reshape(-1, LANES)


def _unpack(buf, shapes, lead=()):
    flat = buf.reshape(lead + (-1,))
    out, off = [], 0
    for s in shapes:
        n = 1
        for d in s:
            n *= d
        out.append(flat[..., off:off + n].reshape(lead + tuple(s)))
        off += n
    return out


BIG = ("ff_w1", "ff_w2", "conv_w_in", "conv_w_out", "ssm_w_in", "ssm_glu_w", "ssm_w_out", "sg_w_in", "sg_w_out")
BIG_AXIS = {"ff_w1": 2, "ff_w2": 1, "conv_w_in": 2, "conv_w_out": 1, "ssm_w_in": 1, "ssm_glu_w": 1, "ssm_w_out": 1,
            "sg_w_in": 2, "sg_w_out": 1}
LAYER_WEIGHTS = (
    (("conv_w_in", 0), ("conv_w_out", 0), ("ff_w1", 0), ("ff_w2", 0)),
    (("ssm_w_in", 0), ("ssm_glu_w", 0), ("ssm_w_out", 0), ("ff_w1", 1), ("ff_w2", 1)),
    (("sg_w_in", 0), ("sg_w_out", 0), ("ff_w1", 2), ("ff_w2", 2)),
    (("conv_w_in", 1), ("conv_w_out", 1), ("ff_w1", 3), ("ff_w2", 3)),
)
GATHER_GROUPS = tuple(grp for lw in LAYER_WEIGHTS for grp in (lw[:-2], lw[-2:]))
SMALL_SHARDED = ("conv_w", "conv_b", "sg_v_g")
SMALL_WIDE_PADDED = ("ssm_b_re", "ssm_b_im")
SMALL = ("ada_b", "norm1_g", "norm2_g", "final_g", "ssm_a_re", "ssm_a_im", "ssm_log_dt", "ssm_b_re", "ssm_b_im", "ssm_c_re",
         "ssm_c_im", "ssm_d", "ssm_glu_b", "sg_w_s", "sg_b_s") + SMALL_SHARDED
WEIGHTS = ("ada_w", "ada_b", "norm1_g", "norm2_g", "ff_w1", "ff_w2", "final_g", "conv_w_in", "conv_w", "conv_b", "conv_w_out",
           "ssm_w_in", "ssm_a_re", "ssm_a_im", "ssm_log_dt", "ssm_b_re", "ssm_b_im", "ssm_c_re", "ssm_c_im", "ssm_d",
           "ssm_glu_w", "ssm_glu_b", "ssm_w_out", "sg_w_in", "sg_v_g", "sg_w_s", "sg_b_s", "sg_w_out")


def kernel(x, c, ada_w, ada_b, norm1_g, norm2_g, ff_w1, ff_w2, final_g, conv_w_in, conv_w, conv_b, conv_w_out, ssm_w_in, ssm_a_re, ssm_a_im, ssm_log_dt, ssm_b_re, ssm_b_im, ssm_c_re, ssm_c_im, ssm_d, ssm_glu_w, ssm_glu_b, ssm_w_out, sg_w_in, sg_v_g, sg_w_s, sg_b_s, sg_w_out, loss_target, m_ada_w, m_ada_b, m_norm1_g, m_norm2_g, m_ff_w1, m_ff_w2, m_final_g, m_conv_w_in, m_conv_w, m_conv_b, m_conv_w_out, m_ssm_w_in, m_ssm_a_re, m_ssm_a_im, m_ssm_log_dt, m_ssm_b_re, m_ssm_b_im, m_ssm_c_re, m_ssm_c_im, m_ssm_d, m_ssm_glu_w, m_ssm_glu_b, m_ssm_w_out, m_sg_w_in, m_sg_v_g, m_sg_w_s, m_sg_b_s, m_sg_w_out, v_ada_w, v_ada_b, v_norm1_g, v_norm2_g, v_ff_w1, v_ff_w2, v_final_g, v_conv_w_in, v_conv_w, v_conv_b, v_conv_w_out, v_ssm_w_in, v_ssm_a_re, v_ssm_a_im, v_ssm_log_dt, v_ssm_b_re, v_ssm_b_im, v_ssm_c_re, v_ssm_c_im, v_ssm_d, v_ssm_glu_w, v_ssm_glu_b, v_ssm_w_out, v_sg_w_in, v_sg_v_g, v_sg_w_s, v_sg_b_s, v_sg_w_out):
    args = dict(locals())
    w = {n: args[n] for n in WEIGHTS}
    m = {n: args["m_" + n] for n in WEIGHTS}
    v = {n: args["v_" + n] for n in WEIGHTS}
    L = x.shape[1]
    tm = min(L, 256)
    chip = 2 * lax.axis_index("x") + lax.axis_index("y")
    me = 2 * chip + lax.axis_index("c")
    xin = x[0]
    target = loss_target[0]
    chip1 = chip.reshape(1).astype(jnp.int32)

    gathers = []

    def start_gather(g, after):
        entries = GATHER_GROUPS[g]
        axes = [BIG_AXIS[n] for n, _ in entries]
        lands = [_cast_place(w[n], li, BIG_AXIS[n], chip1, f"cast_{n}_{li}") for n, li in entries]
        s_sems, r_sems, lands, token = _gather_start(lands, axes, f"gather_start{g}", after)
        gathers.append((s_sems, r_sems, lands, axes))
        return token

    def weights_of(g, after):
        s_sems, r_sems, lands, axes = gathers[g]
        lands = _gather_wait(s_sems, r_sems, lands, axes, f"gather_wait{g}", after)
        lands = _gather_share(lands, axes, f"gather_share{g}")
        token = start_gather(g + 2, lands[0]) if g + 2 < len(GATHER_GROUPS) else None
        return dict(zip([n for n, _ in GATHER_GROUPS[g]], lands)), token

    small_in = _pack([c, conv_w, conv_b, sg_v_g])
    got = _allgather_small(small_in, "gather_small_inputs").reshape(N_DEV, -1)
    c_all, cw_sh, cb_sh, vg_sh = _unpack(got, [(D,), conv_w.shape, conv_b.shape, sg_v_g.shape], lead=(N_DEV,))
    conv_w_full = jnp.concatenate([cw_sh[2 * k] for k in range(4)], axis=-1)
    conv_b_full = jnp.concatenate([cb_sh[2 * k] for k in range(4)], axis=-1)
    vg_full = jnp.concatenate([vg_sh[2 * k] for k in range(4)], axis=-1)
    c16 = jnp.pad(c_all, ((0, 16 - N_DEV), (0, 0)))

    cols = ada_w.shape[2]
    ada_b_cols = lax.dynamic_slice_in_dim(ada_b, chip * cols, cols, axis=1)[:, None, :]
    mod_sh = _ada_fwd(c16, ada_w, ada_b_cols)[:, :N_DEV, :]
    mod_all = _allgather_small(_pack([mod_sh]), "gather_mod").reshape(N_DEV, -1)
    mod_all = _unpack(mod_all, [mod_sh.shape], lead=(N_DEV,))[0]
    mod_mine = lax.dynamic_index_in_dim(mod_all[0::2], me, axis=2, keepdims=False)
    mod_mine = mod_mine.transpose(1, 0, 2).reshape(DEPTH, 6, D)
    pv = jnp.concatenate([mod_mine, norm1_g[:, None, :], norm2_g[:, None, :]], axis=1)

    start_gather(1, start_gather(0, pv))

    cw_rows = jnp.concatenate([conv_w_full, conv_b_full[:, None, :], jnp.zeros((conv_w_full.shape[0], 4, D), F32)], axis=1)

    a_re, a_im = ssm_a_re[0], ssm_a_im[0]
    log_dt = ssm_log_dt[0][:, None]
    bt_re, bt_im = ssm_b_re[0].transpose(2, 0, 1), ssm_b_im[0].transpose(2, 0, 1)
    abar_re, abar_im, bbar_re, bbar_im = _s5_params_fwd(a_re, a_im, log_dt, bt_re, bt_im)
    ar_vec, ai_vec = abar_re.reshape(1, NSTATE), abar_im.reshape(1, NSTATE)
    bd_re, bd_im = _bf(_blockdiag_b(bbar_re)), _bf(_blockdiag_b(bbar_im))
    cd_re, cd_im = _bf(_blockdiag_c(ssm_c_re[0])), _bf(_blockdiag_c(ssm_c_im[0]))

    saved = []
    fulls = []
    xl = xin
    for i in range(DEPTH):
        kind = MIXER_OF_LAYER[i]
        j = i // 3
        full, tok = weights_of(2 * i, cd_im if i == 0 else xl)
        fulls.append(full)
        if kind == 0:
            x1, h, bcx, conv, q, y = _conv_fwd(xl, pv, full["conv_w_in"], full["conv_w_out"], cw_rows, i, j, tm, after=tok)
            mix = dict(h=h, bcx=bcx, conv=conv, q=q, y=y)
        elif kind == 1:
            xp = _to_segments(xl)
            h, u, e_re, e_im = _s5_fwd_ends(xp, pv, full["ssm_w_in"], bd_re, bd_im, ar_vec, ai_vec, i, tm, after=tok)
            s0_re, s0_im = _s5_segment_states(e_re, e_im, ar_vec, ai_vec, L // NSEG, adjoint=False)
            x1p, s_re, s_im, y1, zg, y3, y = _s5_fwd_out(xp, u, pv, bd_re, bd_im, s0_re, s0_im, ar_vec, ai_vec, cd_re, cd_im,
                                                         ssm_d, full["ssm_glu_w"], ssm_glu_b, full["ssm_w_out"], i, tm)
            x1 = _from_segments(x1p)
            mix = dict(xp=xp, h=h, u=u, s_re=s_re, s_im=s_im, y1=y1, zg=zg, y3=y3, y=y)
        else:
            x1, h, uv, vm, q, y = _sg_fwd(xl, pv, full["sg_w_in"], sg_w_s[0], sg_b_s[0].T, vg_full, full["sg_w_out"], i, tm,
                                          after=tok)
            mix = dict(h=h, uv=uv, vm=vm, q=q, y=y)
        ffn_weights, tok = weights_of(2 * i + 1, x1)
        full.update(ffn_weights)
        x2, h2, a, f = _ffn_fwd(x1, pv, full["ff_w1"], full["ff_w2"], i, tm, after=tok)
        saved.append(dict(x=xl, x1=x1, h2=h2, a=a, f=f, **mix))
        xl = x2

    dxl, vs_fin = _final(xl, target, final_g[None, :], tm)

    gfull = {n: [None] * w[n].shape[0] for n in BIG}
    vs_mix, vs_ffn = [None] * DEPTH, [None] * DEPTH
    small_g = {}
    scatters = {}
    token = None

    def start_scatter(key, entries, after):
        garrs = [gfull[n][li][None] for n, li in entries]
        gaxes = [BIG_AXIS[n] for n, _ in entries]
        s_sems, r_sems, garrs, lands, tok = _scatter_start(garrs, gaxes, f"scatter_start{key}", after)
        scatters[key] = (s_sems, r_sems, garrs, lands, gaxes, entries)
        return tok

    for i in reversed(range(DEPTH)):
        kind = MIXER_OF_LAYER[i]
        j = i // 3
        sv = saved[i]
        full = fulls[i]
        dx1, p_b, da_b, df_b, vs_ffn[i] = _ffn_bwd(dxl, sv["x1"], sv["a"], sv["f"], pv, full["ff_w1"], full["ff_w2"], i, tm,
                                                   after=token)
        gfull["ff_w1"][i] = _mm_tn(sv["h2"], da_b, f"wgrad_ff_w1_{i}")
        gfull["ff_w2"][i] = _mm_tn(p_b, df_b, f"wgrad_ff_w2_{i}")
        if i == 0:
            token = start_scatter("0f", LAYER_WEIGHTS[0][2:], dx1)
        if kind == 0:
            dxl, dbcx_b, dy_b, vsm = _conv_bwd(dx1, sv["x"], sv["y"], sv["bcx"], sv["conv"], pv, full["conv_w_in"],
                                               full["conv_w_out"], cw_rows, i, j, tm, after=token if i == 0 else None)
            gfull["conv_w_in"][j] = _mm_tn(sv["h"], dbcx_b, f"wgrad_conv_w_in_{j}")
            gfull["conv_w_out"][j] = _mm_tn(sv["q"], dy_b, f"wgrad_conv_w_out_{j}")
            small_g.setdefault("conv_w", [None, None])[j] = vsm[3:6]
            small_g.setdefault("conv_b", [None, None])[j] = vsm[6]
        elif kind == 1:
            dx1p = _to_segments(dx1)
            dy_b, y2_b, dzg_b, dy1_b, du_skip, eb_re, eb_im, vsm = _s5_bwd_ends(
                dx1p, sv["y"], sv["y1"], sv["zg"], sv["u"], pv, cd_re, cd_im, ar_vec, ai_vec, ssm_d, full["ssm_glu_w"],
                full["ssm_w_out"], i, tm)
            l0_re, l0_im = _s5_segment_states(eb_re, eb_im, ar_vec, ai_vec, L // NSEG, adjoint=True)
            dxp, du_b, lam_re, lam_im, dabar, vs_in = _s5_bwd_in(
                dx1p, dy1_b, du_skip, sv["xp"], sv["s_re"], sv["s_im"], pv, bd_re, bd_im, cd_re, cd_im, l0_re, l0_im,
                ar_vec, ai_vec, full["ssm_w_in"], i, tm)
            dxl = _from_segments(dxp)
            gfull["ssm_w_out"][0] = _mm_tn(sv["y3"], dy_b, "wgrad_ssm_w_out")
            gfull["ssm_glu_w"][0] = _mm_tn(y2_b, dzg_b, "wgrad_ssm_glu_w")
            gfull["ssm_w_in"][0] = _mm_tn(sv["h"], du_b, "wgrad_ssm_w_in")
            s5_late = dict(s_re=sv["s_re"], s_im=sv["s_im"], u=sv["u"], dy1_b=dy1_b, lam_re=lam_re, lam_im=lam_im, dabar=dabar)
            small_g.update(ssm_d=vsm[2], ssm_glu_b=vsm[1])
            vsm = jnp.concatenate([vsm[0:1], vs_in[1:3], jnp.zeros((5, D), F32)], axis=0)
        else:
            dxl, duv_b, dy_b, vsm, d_ws, d_bt = _sg_bwd(dx1, sv["x"], sv["y"], sv["uv"], sv["vm"], pv, full["sg_w_in"],
                                                        sg_w_s[0], vg_full, full["sg_w_out"], i, tm)
            gfull["sg_w_in"][0] = _mm_tn(sv["h"], duv_b, "wgrad_sg_w_in")
            gfull["sg_w_out"][0] = _mm_tn(sv["q"], dy_b, "wgrad_sg_w_out")
            small_g.update(sg_v_g=vsm[3], sg_w_s=d_ws, sg_b_s=d_bt.T)
        vs_mix[i] = vsm
        token = start_scatter(str(i), LAYER_WEIGHTS[i], dxl) if i > 0 else start_scatter("0c", LAYER_WEIGHTS[0][:2], dxl)
    grad_x = dxl[None]

    sums = {n: [None] * w[n].shape[0] for n in BIG}

    def collect(key, after):
        s_sems, r_sems, garrs, lands, gaxes, entries = scatters[key]
        garrs, recv = _scatter_wait(s_sems, r_sems, garrs, lands, gaxes, f"scatter_wait{key}", after)
        for (n, li), g, r3, ax in zip(entries, garrs, recv, gaxes):
            sums[n][li] = _sum_parts(r3, g, ax, chip1, f"sum_{n}_{li}")
        return sums[entries[-1][0]][entries[-1][1]]

    after = token
    for key in ("3", "2", "1"):
        after = collect(key, after)
    early = [(n, li) for i in (3, 2, 1) for n, li in LAYER_WEIGHTS[i]]
    late = list(LAYER_WEIGHTS[0][2:]) + list(LAYER_WEIGHTS[0][:2])
    s_sems, r_sems, mine_thru, lands, tok = _swap_start([sums[n][li] for n, li in early], "swap_start_early", after)

    blocks = dict(
        c_re=_mm_tn_blocks(s5_late["s_re"], s5_late["dy1_b"], S5_BP, S5_BH, "wgrad_s5_c_re", after=tok),
        c_im=_mm_tn_blocks(s5_late["s_im"], s5_late["dy1_b"], S5_BP, S5_BH, "wgrad_s5_c_im", after=tok),
        b_re=_mm_tn_blocks(s5_late["u"], s5_late["lam_re"], S5_BH, S5_BP, "wgrad_s5_b_re", after=tok),
        b_im=_mm_tn_blocks(s5_late["u"], s5_late["lam_im"], S5_BH, S5_BP, "wgrad_s5_b_im", after=tok))
    d_are, d_aim, d_ldt, d_btre, d_btim = _s5_params_bwd(
        a_re, a_im, log_dt, bt_re, bt_im, s5_late["dabar"][0].reshape(S5_G, S5_P), s5_late["dabar"][1].reshape(S5_G, S5_P),
        _unblock_b(blocks["b_re"]), _unblock_b(blocks["b_im"]))
    small_g.update(ssm_a_re=d_are, ssm_a_im=d_aim, ssm_log_dt=d_ldt, ssm_b_re=d_btre.transpose(1, 2, 0),
                   ssm_b_im=d_btim.transpose(1, 2, 0), ssm_c_re=_unblock_c(blocks["c_re"]), ssm_c_im=-_unblock_c(blocks["c_im"]))

    mine_thru, got = _swap_wait(s_sems, r_sems, mine_thru, lands, "swap_wait_early", blocks["b_im"])
    sib = dict(zip(early, got))
    for (n, li), t in zip(early, mine_thru):
        sums[n][li] = t
    after = got[-1]
    for key in ("0f", "0c"):
        after = collect(key, after)
    sib.update(zip(late, _swap_with_sibling([sums[n][li] for n, li in late], "swap_grad_sums_late")))

    dmod = _mod_bwd(jnp.stack(vs_mix), jnp.stack(vs_ffn), pv)
    small_g.update(ada_b=dmod[:, :6, :], norm1_g=dmod[:, 6, :], norm2_g=dmod[:, 7, :], final_g=vs_fin[1],
                   conv_w=jnp.stack(small_g["conv_w"]), conv_b=jnp.stack(small_g["conv_b"]))

    loss_part = (0.5 / D) * vs_fin[0, 0:1]
    part_shapes = [(1,)] + [tuple(small_g[n].shape) for n in SMALL]
    slots = _reduce_pair(_pack([loss_part] + [small_g[n] for n in SMALL], 16), "reduce_small_pair", sib[late[-1]])
    s_sems, r_sems, slots, tok = _reduce_cross_start(slots, "reduce_small_cross_start")

    res = {}
    for n in BIG:
        res[n] = _adamw_layers(w[n], sums[n], [sib[(n, li)] for li in range(w[n].shape[0])], m[n], v[n], f"adamw_{n}", after=tok)

    slots = _reduce_cross_wait(s_sems, r_sems, slots, "reduce_small_cross_wait", res[BIG[-1]][0])
    parts_sum = _reduce_finish(slots, "reduce_small_finish")
    summed = _unpack(parts_sum, part_shapes)
    loss = summed[0][0]
    gsum = dict(zip(SMALL, summed[1:]))
    dmod_all = _allgather_small(_pack([small_g["ada_b"]]), "gather_dmod", parts_sum)
    dmod_all = dmod_all.reshape(N_DEV, DEPTH, 6 * D)
    dmod_cols = lax.dynamic_slice_in_dim(dmod_all, chip * cols, cols, axis=2).transpose(1, 0, 2)
    g_ada_w = _ada_bwd(c16, jnp.pad(dmod_cols, ((0, 0), (0, 16 - N_DEV), (0, 0))))

    shp = ada_w.shape
    two = lambda t: t.reshape(shp[0] * shp[1], shp[2])
    res["ada_w"] = [t.reshape(shp) for t in _adamw(two(ada_w), [two(g_ada_w)], two(m_ada_w), two(v_ada_w), "adamw_ada_w")]

    def mine(n):
        g = gsum[n]
        if n in SMALL_SHARDED:
            g = lax.dynamic_slice_in_dim(g, chip * w[n].shape[-1], w[n].shape[-1], axis=g.ndim - 1)
        return g.reshape(w[n].shape)

    for k, names in enumerate(([n for n in SMALL if n not in SMALL_WIDE_PADDED], list(SMALL_WIDE_PADDED))):
        outs = _adamw_many([w[n] for n in names], [mine(n) for n in names], [m[n] for n in names], [v[n] for n in names],
                           f"adamw_small{k}")
        for idx, n in enumerate(names):
            res[n] = [outs[part][idx] for part in range(4)]

    outs = [loss, grad_x]
    for part in range(4):
        outs += [res[n][part] for n in WEIGHTS]
    return tuple(outs)
```

```python
import functools

import jax
import jax.numpy as jnp
from jax import lax
from jax.experimental import pallas as pl
from jax.experimental.pallas import tpu as pltpu

F32 = jnp.float32
BF16 = jnp.bfloat16
D = 1024
EPS = 1e-6
DEPTH = 4
MIXER_OF_LAYER = (0, 1, 2, 0)
S5_G, S5_H, S5_P = 64, 16, 64
S5_NB = 4
S5_BH = S5_H * 16
S5_BP = S5_P * 16
NSTATE = S5_G * S5_P
SG_HEADS, SG_CHUNK = 8, 128
ADAM_LR, ADAM_B1, ADAM_B2, ADAM_EPS, ADAM_WD, ADAM_STEP = 0.001, 0.9, 0.999, 1e-08, 0.01, 10
N_DEV = 8
MESH = pl.DeviceIdType.MESH
LANES = 1024
R_SH1, R_SC1, R_G1, R_SH2, R_SC2, R_G2, R_N1, R_N2 = range(8)


def _dot(a, b):
    return jnp.dot(a, b, preferred_element_type=F32)


def _dot_nt(a, b):
    return lax.dot_general(a, b, (((1,), (1,)), ((), ())), preferred_element_type=F32)


def _dot_tn(a, b):
    return lax.dot_general(a, b, (((0,), (0,)), ((), ())), preferred_element_type=F32)


def _bf(x):
    return x.astype(BF16)


def _sum0(x):
    return jnp.sum(x, axis=0, keepdims=True)


def _params(n_axes, vmem_mb=48):
    return pltpu.CompilerParams(dimension_semantics=("arbitrary",) * n_axes, vmem_limit_bytes=vmem_mb << 20)


def _rows(tm, cols, nt=None):
    if nt is None:
        return pl.BlockSpec((tm, cols), lambda i: (i, 0))
    return pl.BlockSpec((tm, cols), lambda i: (nt - 1 - i, 0))


def _whole(shape):
    nd = len(shape)
    return pl.BlockSpec(shape, lambda *_: (0,) * nd)


def _layer_w(r, c, layer):
    return pl.BlockSpec((None, r, c), lambda *_: (layer, 0, 0), pipeline_mode=pl.Buffered(1))


def _const_w(shape):
    nd = len(shape)
    return pl.BlockSpec(shape, lambda *_: (0,) * nd, pipeline_mode=pl.Buffered(1))


def _call_after(after, body, n_in, *, in_specs, **kw):
    if after is None:
        return pl.pallas_call(body, in_specs=in_specs, **kw), ()

    def body_after(*refs):
        return body(*refs[:n_in], *refs[n_in + 1:])

    return pl.pallas_call(body_after, in_specs=list(in_specs) + [pl.BlockSpec(memory_space=pl.ANY)], **kw), (after,)


def _norm_mod(x, ng, sc, sh):
    r = lax.rsqrt(jnp.mean(x * x, axis=-1, keepdims=True) + EPS)
    xn = x * r
    return (xn * ng) * (1.0 + sc) + sh, xn, r


def _norm_mod_bwd(dh, xn, r, ng, sc):
    dxn = dh * (ng * (1.0 + sc))
    return r * (dxn - xn * jnp.mean(dxn * xn, axis=-1, keepdims=True))


def _shift_down(z, prev8, k):
    row = lax.broadcasted_iota(jnp.int32, z.shape, 0)
    if k == 1:
        return jnp.where(row >= 1, pltpu.roll(z, 1, 0), prev8[7:8])
    return jnp.where(row >= 2, pltpu.roll(z, 2, 0), jnp.where(row == 0, prev8[6:7], prev8[7:8]))


def _shift_up(z, next8, k):
    n = z.shape[0]
    row = lax.broadcasted_iota(jnp.int32, z.shape, 0)
    if k == 1:
        return jnp.where(row <= n - 2, pltpu.roll(z, n - 1, 0), next8[0:1])
    return jnp.where(row <= n - 3, pltpu.roll(z, n - 2, 0), jnp.where(row == n - 2, next8[0:1], next8[1:2]))


def _place():
    x, y, c = lax.axis_index("x"), lax.axis_index("y"), lax.axis_index("c")
    chips = [(1 - x, y), (x, 1 - y), (1 - x, 1 - y)]
    return x, y, c, chips


def _allgather_small(x_shard, name, after=None):
    m_per, n = x_shard.shape

    def body(x_ref, out_ref, send_sems, recv_sems, local_sem):
        x, y, c, chips = _place()
        me, sibling = (x, y, c), (x, y, 1 - c)

        def rows(px, py, pc):
            return out_ref.at[pl.ds((4 * px + 2 * py + pc) * m_per, m_per), :]

        def copy(k, block, to, src=None):
            return pltpu.make_async_remote_copy(
                src_ref=rows(*block) if src is None else src, dst_ref=rows(*block),
                send_sem=send_sems.at[k], recv_sem=recv_sems.at[k], device_id=to, device_id_type=MESH)

        mine = pltpu.make_async_copy(x_ref, rows(*me), local_sem)
        mine.start()
        first = [copy(0, me, sibling, src=x_ref)]
        first += [copy(1 + j, me, (*chip, c), src=x_ref) for j, chip in enumerate(chips)]
        for cp in first:
            cp.start()
        passed = [copy(4 + j, (*chip, c), sibling) for j, chip in enumerate(chips)]
        for j, chip in enumerate(chips):
            copy(1 + j, (*chip, c), me).wait_recv()
            passed[j].start()
        copy(0, sibling, me).wait_recv()
        for j, chip in enumerate(chips):
            copy(4 + j, (*chip, 1 - c), me).wait_recv()
        for cp in first + passed:
            cp.wait_send()
        mine.wait()

    call, tail = _call_after(
        after, body, 1, name=name, out_shape=jax.ShapeDtypeStruct((N_DEV * m_per, n), F32),
        in_specs=[pl.BlockSpec(memory_space=pltpu.VMEM)], out_specs=pl.BlockSpec(memory_space=pltpu.VMEM),
        scratch_shapes=[pltpu.SemaphoreType.DMA((7,)), pltpu.SemaphoreType.DMA((7,)), pltpu.SemaphoreType.DMA],
        compiler_params=pltpu.CompilerParams(vmem_limit_bytes=48 << 20),
    )
    return call(x_shard, *tail)


def _reduce_pair(x_part, name, after=None):
    m, n = x_part.shape
    h = m // 2

    def body(x_ref, slots_ref, sib_buf, send_sem, recv_sem):
        x, y, c, _ = _place()
        swap = pltpu.make_async_remote_copy(src_ref=x_ref, dst_ref=sib_buf, send_sem=send_sem, recv_sem=recv_sem,
                                            device_id=(x, y, 1 - c), device_id_type=MESH)
        swap.start()
        swap.wait()
        mine = pl.ds(pl.multiple_of(c * h, 8), h)
        slots_ref[pl.ds(2 * x + y, 1)] = (x_ref[mine, :] + sib_buf[mine, :])[None]

    call, tail = _call_after(
        after, body, 1, name=name, out_shape=jax.ShapeDtypeStruct((4, h, n), F32),
        in_specs=[pl.BlockSpec(memory_space=pltpu.VMEM)], out_specs=pl.BlockSpec(memory_space=pltpu.VMEM),
        scratch_shapes=[pltpu.VMEM((m, n), F32), pltpu.SemaphoreType.DMA, pltpu.SemaphoreType.DMA],
        compiler_params=pltpu.CompilerParams(vmem_limit_bytes=48 << 20),
    )
    return call(x_part, *tail)


def _reduce_cross_start(slots, name):
    def body(slots_ref, send_sems, recv_sems, thru, token):
        x, y, c, chips = _place()
        mine = slots_ref.at[pl.ds(2 * x + y, 1)]
        for j, chip in enumerate(chips):
            pltpu.make_async_remote_copy(src_ref=mine, dst_ref=mine, send_sem=send_sems.at[j], recv_sem=recv_sems.at[j],
                                         device_id=(*chip, c), device_id_type=MESH).start()
        token[...] = jnp.zeros_like(token)

    res = pl.pallas_call(
        body, name=name,
        out_shape=(pltpu.SemaphoreType.DMA((3,)), pltpu.SemaphoreType.DMA((3,)), pltpu.HBM(slots.shape, F32),
                   jax.ShapeDtypeStruct((8, 128), F32)),
        in_specs=[HBM_SPEC], out_specs=(SEM_SPEC, SEM_SPEC, HBM_SPEC, pl.BlockSpec(memory_space=pltpu.VMEM)),
        input_output_aliases={0: 2}, compiler_params=SPLIT_COPY_PARAMS,
    )(*_in_hbm([slots]))
    return res


def _reduce_cross_wait(send_sems, recv_sems, slots, name, after):
    def body(slots_ref, s_sems, r_sems, after_ref, thru):
        x, y, c, chips = _place()
        for j, chip in enumerate(chips):
            theirs = slots_ref.at[pl.ds(2 * chip[0] + chip[1], 1)]
            cp = pltpu.make_async_remote_copy(src_ref=theirs, dst_ref=theirs, send_sem=s_sems.at[j], recv_sem=r_sems.at[j],
                                              device_id=(x, y, c), device_id_type=MESH)
            cp.wait_send()
            cp.wait_recv()

    return pl.pallas_call(
        body, name=name, out_shape=pltpu.HBM(slots.shape, F32),
        in_specs=[HBM_SPEC, SEM_SPEC, SEM_SPEC, ANY_SPEC], out_specs=HBM_SPEC,
        input_output_aliases={0: 0}, compiler_params=SPLIT_COPY_PARAMS,
    )(slots, send_sems, recv_sems, after)


def _reduce_finish(slots, name):
    _, h, n = slots.shape

    def body(slots_ref, out_ref, send_sem, recv_sem):
        x, y, c, _ = _place()
        mine = pl.ds(pl.multiple_of(c * h, 8), h)
        theirs = pl.ds(pl.multiple_of((1 - c) * h, 8), h)
        out_ref[mine, :] = ((slots_ref[0] + slots_ref[1]) + slots_ref[2]) + slots_ref[3]
        give = pltpu.make_async_remote_copy(src_ref=out_ref.at[mine, :], dst_ref=out_ref.at[mine, :], send_sem=send_sem,
                                            recv_sem=recv_sem, device_id=(x, y, 1 - c), device_id_type=MESH)
        give.start()
        give.wait_send()
        pltpu.make_async_remote_copy(src_ref=out_ref.at[theirs, :], dst_ref=out_ref.at[theirs, :], send_sem=send_sem,
                                     recv_sem=recv_sem, device_id=(x, y, c), device_id_type=MESH).wait_recv()

    return pl.pallas_call(
        body, name=name, out_shape=jax.ShapeDtypeStruct((2 * h, n), F32),
        in_specs=[pl.BlockSpec(memory_space=pltpu.VMEM)], out_specs=pl.BlockSpec(memory_space=pltpu.VMEM),
        scratch_shapes=[pltpu.SemaphoreType.DMA, pltpu.SemaphoreType.DMA],
        compiler_params=pltpu.CompilerParams(vmem_limit_bytes=48 << 20),
    )(slots)


def _shard_region(ref, full_shape, axis, chip_k, half=None):
    _, r, c = full_shape
    if axis == 1:
        rs = r // 4
        if half is None:
            return ref.at[:, pl.ds(pl.multiple_of(chip_k * rs, 128), rs), :]
        return ref.at[:, pl.ds(pl.multiple_of(chip_k * rs + half * (rs // 2), 128), rs // 2), :]
    cs = c // 4
    if half is None:
        return ref.at[:, :, pl.ds(pl.multiple_of(chip_k * cs, 128), cs)]
    return ref.at[:, pl.ds(pl.multiple_of(half * (r // 2), 128), r // 2), pl.ds(pl.multiple_of(chip_k * cs, 128), cs)]


HBM_SPEC = pl.BlockSpec(memory_space=pltpu.HBM)
SEM_SPEC = pl.BlockSpec(memory_space=pltpu.SEMAPHORE)
ANY_SPEC = pl.BlockSpec(memory_space=pl.ANY)
SPLIT_COPY_PARAMS = pltpu.CompilerParams(has_side_effects=pltpu.SideEffectType.DATAFLOW_SIDE_EFFECTING)


def _in_hbm(arrs):
    return [pltpu.with_memory_space_constraint(a, pltpu.HBM) for a in arrs]


def _cast_place(w_stack, li, axis, chip, name):
    _, r, c = w_stack.shape
    full = (1, 4 * r, c) if axis == 1 else (1, r, 4 * c)
    tr = min(r, 256)
    if axis == 1:
        out_spec = pl.BlockSpec((None, tr, c), lambda i, k: (0, k[0] * (r // tr) + i, 0))
    else:
        out_spec = pl.BlockSpec((None, tr, c), lambda i, k: (0, i, k[0]))

    def body(k_ref, w_ref, o_ref):
        o_ref[...] = _bf(w_ref[...])

    return pl.pallas_call(
        body, name=name,
        grid_spec=pltpu.PrefetchScalarGridSpec(
            num_scalar_prefetch=1, grid=(r // tr,),
            in_specs=[pl.BlockSpec((None, tr, c), lambda i, k: (li, i, 0))], out_specs=out_spec),
        out_shape=jax.ShapeDtypeStruct(full, BF16),
        compiler_params=_params(1),
    )(chip, w_stack)


def _gather_start(lands, axes, name, after):
    n_arr = len(lands)
    fulls = [tuple(l.shape) for l in lands]

    def body(*refs):
        land = refs[:n_arr]
        send_sems, recv_sems = refs[n_arr + 1:n_arr + 3]
        token = refs[-1]
        x, y, c, chips = _place()
        k_me = 2 * x + y
        for a in range(n_arr):
            mine = _shard_region(land[a], fulls[a], axes[a], k_me, c)
            for j, chip in enumerate(chips):
                pltpu.make_async_remote_copy(
                    src_ref=mine, dst_ref=mine, send_sem=send_sems.at[a * 3 + j], recv_sem=recv_sems.at[a * 3 + j],
                    device_id=(*chip, c), device_id_type=MESH).start()
        token[...] = jnp.zeros_like(token)

    res = pl.pallas_call(
        body, name=name,
        out_shape=(pltpu.SemaphoreType.DMA((3 * n_arr,)), pltpu.SemaphoreType.DMA((3 * n_arr,)),
                   *[pltpu.HBM(f, BF16) for f in fulls], jax.ShapeDtypeStruct((8, 128), F32)),
        in_specs=[HBM_SPEC] * n_arr + [ANY_SPEC],
        out_specs=(SEM_SPEC, SEM_SPEC, *[HBM_SPEC] * n_arr, pl.BlockSpec(memory_space=pltpu.VMEM)),
        input_output_aliases={a: 2 + a for a in range(n_arr)},
        compiler_params=SPLIT_COPY_PARAMS,
    )(*_in_hbm(lands), after)
    return res[0], res[1], list(res[2:2 + n_arr]), res[-1]


def _gather_wait(send_sems, recv_sems, lands, axes, name, after):
    n_arr = len(lands)
    fulls = [tuple(l.shape) for l in lands]

    def body(*refs):
        land = refs[:n_arr]
        s_sems, r_sems = refs[n_arr:n_arr + 2]
        x, y, c, chips = _place()
        for a in range(n_arr):
            for j, chip in enumerate(chips):
                k_j = 2 * chip[0] + chip[1]
                got = _shard_region(land[a], fulls[a], axes[a], k_j, c)
                cp = pltpu.make_async_remote_copy(
                    src_ref=got, dst_ref=got, send_sem=s_sems.at[a * 3 + j], recv_sem=r_sems.at[a * 3 + j],
                    device_id=(x, y, c), device_id_type=MESH)
                cp.wait_send()
                cp.wait_recv()

    res = pl.pallas_call(
        body, name=name,
        out_shape=tuple(pltpu.HBM(f, BF16) for f in fulls),
        in_specs=[HBM_SPEC] * n_arr + [SEM_SPEC, SEM_SPEC, ANY_SPEC],
        out_specs=tuple([HBM_SPEC] * n_arr),
        input_output_aliases={a: a for a in range(n_arr)},
        compiler_params=SPLIT_COPY_PARAMS,
    )(*lands, send_sems, recv_sems, after)
    return list(res)


def _gather_share(lands, axes, name):
    n_arr = len(lands)
    fulls = [tuple(l.shape) for l in lands]

    def body(*refs):
        land_in, land = refs[:n_arr], refs[n_arr:2 * n_arr]
        send_sems, recv_sems = refs[2 * n_arr:]
        x, y, c, chips = _place()
        copies = []
        for a in range(n_arr):
            for j, chip in enumerate(chips):
                k_j = 2 * chip[0] + chip[1]
                cp = pltpu.make_async_remote_copy(
                    src_ref=_shard_region(land_in[a], fulls[a], axes[a], k_j, c),
                    dst_ref=_shard_region(land[a], fulls[a], axes[a], k_j, c),
                    send_sem=send_sems.at[a * 3 + j], recv_sem=recv_sems.at[a * 3 + j],
                    device_id=(x, y, 1 - c), device_id_type=MESH)
                cp.start()
                copies.append(cp)
        for cp in copies:
            cp.wait()

    return pl.pallas_call(
        body, name=name, out_shape=[jax.ShapeDtypeStruct(f, BF16) for f in fulls],
        in_specs=[ANY_SPEC] * n_arr, out_specs=[ANY_SPEC] * n_arr,
        input_output_aliases={a: a for a in range(n_arr)},
        scratch_shapes=[pltpu.SemaphoreType.DMA((3 * n_arr,)), pltpu.SemaphoreType.DMA((3 * n_arr,))],
    )(*lands)


def _scatter_shapes(grads, axes):
    out = []
    for g, ax in zip(grads, axes):
        shp = list(g.shape)
        shp[ax] //= 4
        out.append((3,) + tuple(shp[1:]))
    return out


def _scatter_start(grads, axes, name, after):
    n_arr = len(grads)
    shapes = _scatter_shapes(grads, axes)
    lands = [lax.empty(s, BF16) for s in shapes]

    def body(*refs):
        ins, land = refs[:n_arr], refs[n_arr:2 * n_arr]
        send_sems, recv_sems = refs[2 * n_arr + 1:2 * n_arr + 3]
        token = refs[-1]
        x, y, c, chips = _place()
        for a in range(n_arr):
            for j, chip in enumerate(chips):
                k_j = 2 * chip[0] + chip[1]
                pltpu.make_async_remote_copy(
                    src_ref=_shard_region(ins[a], grads[a].shape, axes[a], k_j), dst_ref=land[a].at[pl.ds(j, 1)],
                    send_sem=send_sems.at[a * 3 + j], recv_sem=recv_sems.at[a * 3 + j],
                    device_id=(*chip, c), device_id_type=MESH).start()
        token[...] = jnp.zeros_like(token)

    res = pl.pallas_call(
        body, name=name,
        out_shape=(pltpu.SemaphoreType.DMA((3 * n_arr,)), pltpu.SemaphoreType.DMA((3 * n_arr,)),
                   *[pltpu.HBM(g.shape, BF16) for g in grads], *[pltpu.HBM(s, BF16) for s in shapes],
                   jax.ShapeDtypeStruct((8, 128), F32)),
        in_specs=[HBM_SPEC] * (2 * n_arr) + [ANY_SPEC],
        out_specs=(SEM_SPEC, SEM_SPEC, *[HBM_SPEC] * (2 * n_arr), pl.BlockSpec(memory_space=pltpu.VMEM)),
        input_output_aliases={a: 2 + a for a in range(2 * n_arr)},
        compiler_params=SPLIT_COPY_PARAMS,
    )(*_in_hbm(grads), *_in_hbm(lands), after)
    return res[0], res[1], list(res[2:2 + n_arr]), list(res[2 + n_arr:2 + 2 * n_arr]), res[-1]


def _scatter_wait(send_sems, recv_sems, grads, lands, axes, name, after):
    n_arr = len(grads)

    def body(*refs):
        ins, land = refs[:n_arr], refs[n_arr:2 * n_arr]
        s_sems, r_sems = refs[2 * n_arr:2 * n_arr + 2]
        x, y, c, chips = _place()
        for a in range(n_arr):
            for j, chip in enumerate(chips):
                k_j = 2 * chip[0] + chip[1]
                cp = pltpu.make_async_remote_copy(
                    src_ref=_shard_region(ins[a], grads[a].shape, axes[a], k_j), dst_ref=land[a].at[pl.ds(j, 1)],
                    send_sem=s_sems.at[a * 3 + j], recv_sem=r_sems.at[a * 3 + j],
                    device_id=(x, y, c), device_id_type=MESH)
                cp.wait_send()
                cp.wait_recv()

    res = pl.pallas_call(
        body, name=name,
        out_shape=(*[pltpu.HBM(g.shape, BF16) for g in grads], *[pltpu.HBM(l.shape, BF16) for l in lands]),
        in_specs=[HBM_SPEC] * (2 * n_arr) + [SEM_SPEC, SEM_SPEC, ANY_SPEC],
        out_specs=tuple([HBM_SPEC] * (2 * n_arr)),
        input_output_aliases={a: a for a in range(2 * n_arr)},
        compiler_params=SPLIT_COPY_PARAMS,
    )(*grads, *lands, send_sems, recv_sems, after)
    return list(res[:n_arr]), list(res[n_arr:])


def _swap_start(arrs, name, after):
    n_arr = len(arrs)
    lands = [lax.empty(a.shape, a.dtype) for a in arrs]

    def body(*refs):
        ins, land = refs[:n_arr], refs[n_arr:2 * n_arr]
        send_sems, recv_sems = refs[2 * n_arr + 1:2 * n_arr + 3]
        token = refs[-1]
        x, y, c, _ = _place()
        for a in range(n_arr):
            pltpu.make_async_remote_copy(
                src_ref=ins[a], dst_ref=land[a], send_sem=send_sems.at[a], recv_sem=recv_sems.at[a],
                device_id=(x, y, 1 - c), device_id_type=MESH).start()
        token[...] = jnp.zeros_like(token)

    res = pl.pallas_call(
        body, name=name,
        out_shape=(pltpu.SemaphoreType.DMA((n_arr,)), pltpu.SemaphoreType.DMA((n_arr,)),
                   *[pltpu.HBM(a.shape, a.dtype) for a in arrs], *[pltpu.HBM(a.shape, a.dtype) for a in arrs],
                   jax.ShapeDtypeStruct((8, 128), F32)),
        in_specs=[HBM_SPEC] * (2 * n_arr) + [ANY_SPEC],
        out_specs=(SEM_SPEC, SEM_SPEC, *[HBM_SPEC] * (2 * n_arr), pl.BlockSpec(memory_space=pltpu.VMEM)),
        input_output_aliases={a: 2 + a for a in range(2 * n_arr)},
        compiler_params=SPLIT_COPY_PARAMS,
    )(*_in_hbm(arrs), *_in_hbm(lands), after)
    return res[0], res[1], list(res[2:2 + n_arr]), list(res[2 + n_arr:2 + 2 * n_arr]), res[-1]


def _swap_wait(send_sems, recv_sems, arrs, lands, name, after):
    n_arr = len(arrs)

    def body(*refs):
        ins, land = refs[:n_arr], refs[n_arr:2 * n_arr]
        s_sems, r_sems = refs[2 * n_arr:2 * n_arr + 2]
        x, y, c, _ = _place()
        for a in range(n_arr):
            cp = pltpu.make_async_remote_copy(
                src_ref=ins[a], dst_ref=land[a], send_sem=s_sems.at[a], recv_sem=r_sems.at[a],
                device_id=(x, y, c), device_id_type=MESH)
            cp.wait_send()
            cp.wait_recv()

    res = pl.pallas_call(
        body, name=name,
        out_shape=(*[pltpu.HBM(a.shape, a.dtype) for a in arrs], *[pltpu.HBM(a.shape, a.dtype) for a in arrs]),
        in_specs=[HBM_SPEC] * (2 * n_arr) + [SEM_SPEC, SEM_SPEC, ANY_SPEC],
        out_specs=tuple([HBM_SPEC] * (2 * n_arr)),
        input_output_aliases={a: a for a in range(2 * n_arr)},
        compiler_params=SPLIT_COPY_PARAMS,
    )(*arrs, *lands, send_sems, recv_sems, after)
    return list(res[:n_arr]), list(res[n_arr:])


def _swap_with_sibling(arrs, name):
    n_arr = len(arrs)

    def body(*refs):
        ins, outs = refs[:n_arr], refs[n_arr:2 * n_arr]
        send_sems, recv_sems = refs[2 * n_arr:]
        x, y, c, _ = _place()
        copies = []
        for a in range(n_arr):
            cp = pltpu.make_async_remote_copy(
                src_ref=ins[a], dst_ref=outs[a], send_sem=send_sems.at[a], recv_sem=recv_sems.at[a],
                device_id=(x, y, 1 - c), device_id_type=MESH)
            cp.start()
            copies.append(cp)
        for cp in copies:
            cp.wait()

    any_spec = pl.BlockSpec(memory_space=pl.ANY)
    return pl.pallas_call(
        body, name=name, out_shape=[jax.ShapeDtypeStruct(a.shape, a.dtype) for a in arrs],
        in_specs=[any_spec] * n_arr, out_specs=[any_spec] * n_arr,
        scratch_shapes=[pltpu.SemaphoreType.DMA((n_arr,)), pltpu.SemaphoreType.DMA((n_arr,))],
    )(*arrs)


def _mm_tn(a, b, name, out_dtype=BF16):
    L, m = a.shape
    n = b.shape[1]
    bm, bn, bk = min(m, 1024), min(n, 1024), min(L, 2048)
    nk = L // bk

    def body(a_ref, b_ref, o_ref, acc):
        k = pl.program_id(2)

        @pl.when(k == 0)
        def _():
            acc[...] = jnp.zeros_like(acc)

        acc[...] += _dot_tn(_bf(a_ref[...]), _bf(b_ref[...]))

        @pl.when(k == nk - 1)
        def _():
            o_ref[...] = acc[...].astype(out_dtype)

    return pl.pallas_call(
        body, name=name, grid=(m // bm, n // bn, nk),
        in_specs=[pl.BlockSpec((bk, bm), lambda i, j, k: (k, i)), pl.BlockSpec((bk, bn), lambda i, j, k: (k, j))],
        out_specs=pl.BlockSpec((bm, bn), lambda i, j, k: (i, j)),
        out_shape=jax.ShapeDtypeStruct((m, n), out_dtype),
        scratch_shapes=[pltpu.VMEM((bm, bn), F32)],
        compiler_params=_params(3),
    )(a, b)


def _mm_tn_blocks(a, b, wa, wb, name, after=None):
    L = a.shape[0]
    nb = a.shape[1] // wa
    bk = min(L, 1024)
    nk = L // bk

    def body(a_ref, b_ref, o_ref):
        @pl.when(pl.program_id(1) == 0)
        def _():
            o_ref[...] = jnp.zeros_like(o_ref)

        o_ref[...] += _dot_tn(_bf(a_ref[...]), _bf(b_ref[...]))

    call, tail = _call_after(
        after, body, 2, name=name, grid=(nb, nk),
        in_specs=[pl.BlockSpec((bk, wa), lambda j, k: (k, j)), pl.BlockSpec((bk, wb), lambda j, k: (k, j))],
        out_specs=pl.BlockSpec((None, wa, wb), lambda j, k: (j, 0, 0)),
        out_shape=jax.ShapeDtypeStruct((nb, wa, wb), F32),
        compiler_params=_params(2),
    )
    return call(a, b, *tail)


def _sum_parts(parts, own, axis, chip, name):
    _, r, c = parts.shape
    tr = min(r, 256)
    if axis == 1:
        own_spec = pl.BlockSpec((None, tr, c), lambda i, k: (0, k[0] * (r // tr) + i, 0))
    else:
        own_spec = pl.BlockSpec((None, tr, c), lambda i, k: (0, i, k[0]))

    def body(k_ref, p_ref, g_ref, o_ref):
        p = p_ref[...].astype(F32)
        o_ref[...] = ((p[0] + p[1]) + p[2]) + g_ref[...].astype(F32)

    return pl.pallas_call(
        body, name=name,
        grid_spec=pltpu.PrefetchScalarGridSpec(
            num_scalar_prefetch=1, grid=(r // tr,),
            in_specs=[pl.BlockSpec((3, tr, c), lambda i, k: (0, i, 0)), own_spec],
            out_specs=pl.BlockSpec((tr, c), lambda i, k: (i, 0))),
        out_shape=jax.ShapeDtypeStruct((r, c), F32),
        compiler_params=_params(1),
    )(chip, parts, own)


def _adamw(w, g_parts, m, v, name):
    n_g = len(g_parts)
    if w.ndim == 2:
        r, c = w.shape
        tr = r
        for cand in (512, 256, 128, 64, 32, 16, 8):
            if r % cand == 0 and cand * c * 4 <= (2 << 20):
                tr = cand
                break
        spec = pl.BlockSpec((tr, c), lambda i: (i, 0))
        tiling = dict(grid=(r // tr,), in_specs=[spec] * (3 + n_g), out_specs=[spec] * 4, compiler_params=_params(1))
    else:
        tiling = dict(compiler_params=pltpu.CompilerParams(vmem_limit_bytes=48 << 20))

    def body(*refs):
        w_ref, g_refs, m_ref, v_ref = refs[0], refs[1:1 + n_g], refs[1 + n_g], refs[2 + n_g]
        g = g_refs[0][...]
        for gr in g_refs[1:]:
            g = g + gr[...]
        _adamw_update(g, w_ref, m_ref, v_ref, *refs[3 + n_g:])

    return pl.pallas_call(body, name=name, out_shape=[jax.ShapeDtypeStruct(w.shape, F32)] * 4, **tiling)(w, *g_parts, m, v)


def _adamw_update(g, w_ref, m_ref, v_ref, g_out, d_out, m_out, v_out):
    m_new = ADAM_B1 * m_ref[...] + (1.0 - ADAM_B1) * g
    v_new = ADAM_B2 * v_ref[...] + (1.0 - ADAM_B2) * (g * g)
    m_hat = m_new * (1.0 / (1.0 - ADAM_B1 ** ADAM_STEP))
    v_hat = v_new * (1.0 / (1.0 - ADAM_B2 ** ADAM_STEP))
    g_out[...] = g
    d_out[...] = -ADAM_LR * (m_hat / (jnp.sqrt(v_hat) + ADAM_EPS) + ADAM_WD * w_ref[...])
    m_out[...] = m_new
    v_out[...] = v_new


def _adamw_many(ws, gs, ms, vs, name):
    n = len(ws)

    def body(*refs):
        for k in range(n):
            _adamw_update(refs[n + k][...], refs[k], refs[2 * n + k], refs[3 * n + k],
                          refs[4 * n + k], refs[5 * n + k], refs[6 * n + k], refs[7 * n + k])

    outs = pl.pallas_call(body, name=name, out_shape=[jax.ShapeDtypeStruct(t.shape, F32) for t in ws] * 4,
                          compiler_params=pltpu.CompilerParams(vmem_limit_bytes=56 << 20))(*ws, *gs, *ms, *vs)
    return [outs[part * n:(part + 1) * n] for part in range(4)]


def _adamw_layers(w, q_mine, q_sib, m, v, name, after=None):
    n, r, c = w.shape
    tr = r
    for cand in (512, 256, 128, 64, 32, 16, 8):
        if r % cand == 0 and cand * c * 4 <= (1 << 20):
            tr = cand
            break

    def body(*refs):
        w_ref, qm, qs, m_ref, v_ref = refs[0], refs[1:1 + n], refs[1 + n:1 + 2 * n], refs[1 + 2 * n], refs[2 + 2 * n]
        layer = pl.program_id(0)
        g = qm[0][...] + qs[0][...]
        for k in range(1, n):
            g = jnp.where(layer == k, qm[k][...] + qs[k][...], g)
        _adamw_update(g, w_ref, m_ref, v_ref, *refs[3 + 2 * n:])

    stacked = pl.BlockSpec((None, tr, c), lambda l, i: (l, i, 0))
    per_layer = [pl.BlockSpec((tr, c), lambda l, i, k=k: (jnp.where(l == k, i, 0), 0)) for k in range(n)]
    call, tail = _call_after(
        after, body, 3 + 2 * n, name=name, grid=(n, r // tr),
        in_specs=[stacked] + per_layer + per_layer + [stacked, stacked], out_specs=[stacked] * 4,
        out_shape=[jax.ShapeDtypeStruct(w.shape, F32)] * 4,
        compiler_params=_params(2),
    )
    return call(w, *q_mine, *q_sib, m, v, *tail)


def _ada_fwd(c16, ada_w, ada_b_cols):
    cols = ada_w.shape[2]

    def body(c_ref, w_ref, b_ref, o_ref):
        cv = c_ref[...]
        ca = _bf(cv * jax.nn.sigmoid(cv))
        o_ref[...] = _dot(ca, _bf(w_ref[...])) + b_ref[...]

    return pl.pallas_call(
        body, name="ada_fwd", grid=(DEPTH,),
        in_specs=[_whole((16, D)), pl.BlockSpec((None, D, cols), lambda i: (i, 0, 0)),
                  pl.BlockSpec((None, 1, cols), lambda i: (i, 0, 0))],
        out_specs=pl.BlockSpec((None, 16, cols), lambda i: (i, 0, 0)),
        out_shape=jax.ShapeDtypeStruct((DEPTH, 16, cols), F32),
        compiler_params=_params(1),
    )(c16, ada_w, ada_b_cols)


def _ada_bwd(c16, dmod16):
    cols = dmod16.shape[2]

    def body(c_ref, d_ref, o_ref):
        cv = c_ref[...]
        ca = _bf(cv * jax.nn.sigmoid(cv))
        o_ref[...] = _dot_tn(ca, _bf(d_ref[...]))

    return pl.pallas_call(
        body, name="ada_bwd", grid=(DEPTH,),
        in_specs=[_whole((16, D)), pl.BlockSpec((None, 16, cols), lambda i: (i, 0, 0))],
        out_specs=pl.BlockSpec((None, D, cols), lambda i: (i, 0, 0)),
        out_shape=jax.ShapeDtypeStruct((DEPTH, D, cols), F32),
        compiler_params=_params(1),
    )(c16, dmod16)


def _mod_bwd(vs_mix, vs_ffn, pv):
    def body(m_ref, f_ref, pv_ref, o_ref):
        for i in range(DEPTH):
            vm, vf, p = m_ref[i], f_ref[i], pv_ref[i]
            o_ref[i] = jnp.concatenate([
                vm[2:3], vm[1:2] * p[R_N1:R_N1 + 1], vm[0:1],
                vf[2:3], vf[1:2] * p[R_N2:R_N2 + 1], vf[0:1],
                vm[1:2] * (1.0 + p[R_SC1:R_SC1 + 1]), vf[1:2] * (1.0 + p[R_SC2:R_SC2 + 1])], axis=0)

    return pl.pallas_call(body, name="mod_bwd", out_shape=jax.ShapeDtypeStruct((DEPTH, 8, D), F32))(vs_mix, vs_ffn, pv)


def _ffn_fwd(x1, pv, w1, w2, layer, tm, after=None):
    L = x1.shape[0]
    dff = w1.shape[2]

    def body(x1_ref, pv_ref, w1_ref, w2_ref, x2_ref, h2_ref, a_ref, f_ref):
        x1v, p = x1_ref[...], pv_ref[...]
        h2, _, _ = _norm_mod(x1v, p[R_N2:R_N2 + 1], p[R_SC2:R_SC2 + 1], p[R_SH2:R_SH2 + 1])
        hb = _bf(h2)
        h2_ref[...] = hb
        a = _dot(hb, w1_ref[...])
        a_ref[...] = a
        ra = jnp.maximum(a, 0.0)
        f = _dot(_bf(ra * ra), w2_ref[...])
        f_ref[...] = f
        x2_ref[...] = x1v + p[R_G2:R_G2 + 1] * f

    call, tail = _call_after(
        after, body, 4, name=f"ffn_fwd{layer}", grid=(L // tm,),
        in_specs=[_rows(tm, D), pl.BlockSpec((None, 8, D), lambda i: (layer, 0, 0)), _layer_w(D, dff, 0), _layer_w(dff, D, 0)],
        out_specs=[_rows(tm, D), _rows(tm, D), _rows(tm, dff), _rows(tm, D)],
        out_shape=[jax.ShapeDtypeStruct((L, D), F32), jax.ShapeDtypeStruct((L, D), BF16),
                   jax.ShapeDtypeStruct((L, dff), F32), jax.ShapeDtypeStruct((L, D), F32)],
        compiler_params=_params(1, 56),
    )
    return call(x1, pv, w1, w2, *tail)


def _ffn_bwd(dx2, x1, a, f, pv, w1, w2, layer, tm, after=None):
    L = x1.shape[0]
    dff = w1.shape[2]
    extra = [] if after is None else [pl.BlockSpec(memory_space=pl.ANY)]
    extra_args = [] if after is None else [after]

    def body(dx2_ref, x1_ref, a_ref, f_ref, pv_ref, w1_ref, w2_ref, *rest):
        dx1_ref, p_ref, da_ref, df_ref, vs_ref = rest[len(extra):]

        @pl.when(pl.program_id(0) == 0)
        def _():
            vs_ref[...] = jnp.zeros_like(vs_ref)

        dx2v, p = dx2_ref[...], pv_ref[...]
        dfb = _bf(dx2v * p[R_G2:R_G2 + 1])
        df_ref[...] = dfb
        vs_ref[0:1, :] += _sum0(dx2v * f_ref[...])
        dp = _dot_nt(dfb, w2_ref[...])
        ra = jnp.maximum(a_ref[...], 0.0)
        p_ref[...] = _bf(ra * ra)
        dab = _bf(dp * (2.0 * ra))
        da_ref[...] = dab
        dh2 = _dot_nt(dab, w1_ref[...])
        _, xn, r = _norm_mod(x1_ref[...], p[R_N2:R_N2 + 1], p[R_SC2:R_SC2 + 1], p[R_SH2:R_SH2 + 1])
        dx1_ref[...] = dx2v + _norm_mod_bwd(dh2, xn, r, p[R_N2:R_N2 + 1], p[R_SC2:R_SC2 + 1])
        vs_ref[1:2, :] += _sum0(dh2 * xn)
        vs_ref[2:3, :] += _sum0(dh2)

    return pl.pallas_call(
        body, name=f"ffn_bwd{layer}", grid=(L // tm,),
        in_specs=[_rows(tm, D), _rows(tm, D), _rows(tm, dff), _rows(tm, D),
                  pl.BlockSpec((None, 8, D), lambda i: (layer, 0, 0)), _layer_w(D, dff, 0), _layer_w(dff, D, 0)] + extra,
        out_specs=[_rows(tm, D), _rows(tm, dff), _rows(tm, dff), _rows(tm, D), _whole((8, D))],
        out_shape=[jax.ShapeDtypeStruct((L, D), F32), jax.ShapeDtypeStruct((L, dff), BF16),
                   jax.ShapeDtypeStruct((L, dff), BF16), jax.ShapeDtypeStruct((L, D), BF16),
                   jax.ShapeDtypeStruct((8, D), F32)],
        compiler_params=_params(1, 56),
    )(dx2, x1, a, f, pv, w1, w2, *extra_args)


def _conv_fwd(x, pv, w_in, w_out, cw, layer, j, tm, after=None):
    L = x.shape[0]

    def body(x_ref, pv_ref, win_ref, wout_ref, cw_ref, x1_ref, h_ref, bcx_ref, conv_ref, q_ref, y_ref, carry):
        @pl.when(pl.program_id(0) == 0)
        def _():
            carry[...] = jnp.zeros_like(carry)

        xv, p, cwv = x_ref[...], pv_ref[...], cw_ref[...]
        h, _, _ = _norm_mod(xv, p[R_N1:R_N1 + 1], p[R_SC1:R_SC1 + 1], p[R_SH1:R_SH1 + 1])
        hb = _bf(h)
        h_ref[...] = hb
        bcx = _dot(hb, win_ref[...])
        bcx_ref[...] = bcx
        z = bcx[:, D:2 * D] * bcx[:, 2 * D:]
        prev8 = carry[...]
        conv = cwv[0:1] * _shift_down(z, prev8, 2) + cwv[1:2] * _shift_down(z, prev8, 1) + cwv[2:3] * z + cwv[3:4]
        conv_ref[...] = conv
        qb = _bf(bcx[:, :D] * conv)
        q_ref[...] = qb
        y = _dot(qb, wout_ref[...])
        y_ref[...] = y
        x1_ref[...] = xv + p[R_G1:R_G1 + 1] * y
        carry[...] = z[tm - 8:tm]

    call, tail = _call_after(
        after, body, 5, name=f"conv_fwd{layer}", grid=(L // tm,),
        in_specs=[_rows(tm, D), pl.BlockSpec((None, 8, D), lambda i: (layer, 0, 0)), _layer_w(D, 3 * D, 0), _layer_w(D, D, 0),
                  pl.BlockSpec((None, 8, D), lambda i: (j, 0, 0))],
        out_specs=[_rows(tm, D), _rows(tm, D), _rows(tm, 3 * D), _rows(tm, D), _rows(tm, D), _rows(tm, D)],
        out_shape=[jax.ShapeDtypeStruct((L, D), F32), jax.ShapeDtypeStruct((L, D), BF16), jax.ShapeDtypeStruct((L, 3 * D), F32),
                   jax.ShapeDtypeStruct((L, D), F32), jax.ShapeDtypeStruct((L, D), BF16), jax.ShapeDtypeStruct((L, D), F32)],
        scratch_shapes=[pltpu.VMEM((8, D), F32)],
        compiler_params=_params(1, 56),
    )
    return call(x, pv, w_in, w_out, cw, *tail)


def _conv_bwd(dx1, x, y, bcx, conv, pv, w_in, w_out, cw, layer, j, tm, after=None):
    L = x.shape[0]
    nt = L // tm

    def body(dx1_ref, x_ref, y_ref, bcx_ref, conv_ref, halo_ref, pv_ref, win_ref, wout_ref, cw_ref,
             dx_ref, dbcx_ref, dy_ref, vs_ref, carry):
        gi = pl.program_id(0)
        tile = nt - 1 - gi

        @pl.when(gi == 0)
        def _():
            vs_ref[...] = jnp.zeros_like(vs_ref)
            carry[...] = jnp.zeros_like(carry)

        dx1v, p, cwv = dx1_ref[...], pv_ref[...], cw_ref[...]
        dyb = _bf(dx1v * p[R_G1:R_G1 + 1])
        dy_ref[...] = dyb
        vs_ref[0:1, :] += _sum0(dx1v * y_ref[...])
        dq = _dot_nt(dyb, wout_ref[...])
        bcx = bcx_ref[...]
        b, cg, xh = bcx[:, :D], bcx[:, D:2 * D], bcx[:, 2 * D:]
        db = dq * conv_ref[...]
        dc = dq * b
        z = cg * xh
        halo = halo_ref[...]
        zprev = jnp.where(tile > 0, halo[:, D:2 * D] * halo[:, 2 * D:], 0.0)
        vs_ref[3:4, :] += _sum0(dc * _shift_down(z, zprev, 2))
        vs_ref[4:5, :] += _sum0(dc * _shift_down(z, zprev, 1))
        vs_ref[5:6, :] += _sum0(dc * z)
        vs_ref[6:7, :] += _sum0(dc)
        next8 = carry[...]
        dz = cwv[2:3] * dc + cwv[1:2] * _shift_up(dc, next8, 1) + cwv[0:1] * _shift_up(dc, next8, 2)
        dbb, dcgb, dxhb = _bf(db), _bf(dz * xh), _bf(dz * cg)
        dbcx_ref[:, 0:D] = dbb
        dbcx_ref[:, D:2 * D] = dcgb
        dbcx_ref[:, 2 * D:3 * D] = dxhb
        dh = (_dot_nt(dbb, win_ref[:, 0:D]) + _dot_nt(dcgb, win_ref[:, D:2 * D])) + _dot_nt(dxhb, win_ref[:, 2 * D:3 * D])
        _, xn, r = _norm_mod(x_ref[...], p[R_N1:R_N1 + 1], p[R_SC1:R_SC1 + 1], p[R_SH1:R_SH1 + 1])
        dx_ref[...] = dx1v + _norm_mod_bwd(dh, xn, r, p[R_N1:R_N1 + 1], p[R_SC1:R_SC1 + 1])
        vs_ref[1:2, :] += _sum0(dh * xn)
        vs_ref[2:3, :] += _sum0(dh)
        carry[...] = dc[0:8]

    halo_spec = pl.BlockSpec((8, 3 * D), lambda i: (jnp.maximum((nt - 1 - i) * (tm // 8) - 1, 0), 0))
    call, tail = _call_after(
        after, body, 10, name=f"conv_bwd{layer}", grid=(nt,),
        in_specs=[_rows(tm, D, nt), _rows(tm, D, nt), _rows(tm, D, nt), _rows(tm, 3 * D, nt), _rows(tm, D, nt), halo_spec,
                  pl.BlockSpec((None, 8, D), lambda i: (layer, 0, 0)), _layer_w(D, 3 * D, 0), _layer_w(D, D, 0),
                  pl.BlockSpec((None, 8, D), lambda i: (j, 0, 0))],
        out_specs=[_rows(tm, D, nt), _rows(tm, 3 * D, nt), _rows(tm, D, nt), _whole((8, D))],
        out_shape=[jax.ShapeDtypeStruct((L, D), F32), jax.ShapeDtypeStruct((L, 3 * D), BF16),
                   jax.ShapeDtypeStruct((L, D), BF16), jax.ShapeDtypeStruct((8, D), F32)],
        scratch_shapes=[pltpu.VMEM((8, D), F32)],
        compiler_params=_params(1, 56),
    )
    return call(dx1, x, y, bcx, conv, bcx, pv, w_in, w_out, cw, *tail)


def _s5_discretize(a_re, a_im, log_dt, bt_re, bt_im):
    dt = jnp.exp(log_dt)
    mag = jnp.exp(a_re * dt)
    abar_re = mag * jnp.cos(a_im * dt)
    abar_im = mag * jnp.sin(a_im * dt)
    den = a_re * a_re + a_im * a_im
    nr = abar_re - 1.0
    ni = abar_im
    f_re = (nr * a_re + ni * a_im) / den
    f_im = (ni * a_re - nr * a_im) / den
    bbar_re = f_re * bt_re - f_im * bt_im
    bbar_im = f_re * bt_im + f_im * bt_re
    return abar_re, abar_im, bbar_re, bbar_im


def _s5_params_fwd(a_re, a_im, log_dt, bt_re, bt_im):
    def body(ar, ai, ld, br, bi, o_ar, o_ai, o_br, o_bi):
        r = _s5_discretize(ar[...], ai[...], ld[...], br[...], bi[...])
        o_ar[...], o_ai[...], o_br[...], o_bi[...] = r

    gp = jax.ShapeDtypeStruct((S5_G, S5_P), F32)
    hgp = jax.ShapeDtypeStruct((S5_H, S5_G, S5_P), F32)
    return pl.pallas_call(body, name="s5_params_fwd", out_shape=[gp, gp, hgp, hgp])(a_re, a_im, log_dt, bt_re, bt_im)


def _s5_params_bwd(a_re, a_im, log_dt, bt_re, bt_im, d_ar, d_ai, d_br, d_bi):
    def body(ar, ai, ld, br, bi, gar, gai, gbr, gbi, o_ar, o_ai, o_ld, o_br, o_bi):
        _, vjp = jax.vjp(_s5_discretize, ar[...], ai[...], ld[...], br[...], bi[...])
        r = vjp((gar[...], gai[...], gbr[...], gbi[...]))
        o_ar[...], o_ai[...], o_ld[...], o_br[...], o_bi[...] = r

    gp = jax.ShapeDtypeStruct((S5_G, S5_P), F32)
    hgp = jax.ShapeDtypeStruct((S5_H, S5_G, S5_P), F32)
    return pl.pallas_call(body, name="s5_params_bwd", out_shape=[gp, gp, jax.ShapeDtypeStruct((S5_G, 1), F32), hgp, hgp])(
        a_re, a_im, log_dt, bt_re, bt_im, d_ar, d_ai, d_br, d_bi)


NSEG = 8
SCAN_LANES = 1024


def _to_segments(x):
    n, c = x.shape
    return x.reshape(NSEG, n // NSEG, c).transpose(1, 0, 2).reshape(n, c)


def _from_segments(x):
    n, c = x.shape
    return x.reshape(n // NSEG, NSEG, c).transpose(1, 0, 2).reshape(n, c)


def _segment_scan(re_ref, im_ref, st_re, st_im, a_re, a_im, n_slabs, adjoint, write):
    for q in range(NSTATE // SCAN_LANES):
        ls = slice(q * SCAN_LANES, (q + 1) * SCAN_LANES)
        ar = jnp.broadcast_to(a_re[:, ls], (8, SCAN_LANES))
        ai = jnp.broadcast_to(a_im[:, ls], (8, SCAN_LANES))

        def step(k, carry, ls=ls, ar=ar, ai=ai):
            s_r, s_i = carry
            slab = (n_slabs - 1 - k) if adjoint else k
            rows = pl.ds(pl.multiple_of(slab * 8, 8), 8)
            b_r, b_i = re_ref[rows, ls], im_ref[rows, ls]
            if adjoint:
                n_r = b_r + ar * s_r + ai * s_i
                n_i = b_i - ai * s_r + ar * s_i
            else:
                n_r = ar * s_r - ai * s_i + b_r
                n_i = ar * s_i + ai * s_r + b_i
            if write:
                re_ref[rows, ls] = n_r
                im_ref[rows, ls] = n_i
            return n_r, n_i

        s_r, s_i = lax.fori_loop(0, n_slabs, step, (st_re[:, ls], st_im[:, ls]), unroll=4)
        st_re[:, ls] = s_r
        st_im[:, ls] = s_i


def _s5_segment_states(e_re, e_im, ar, ai, seg_len, adjoint):
    def body(ere_ref, eim_ref, ar_ref, ai_ref, ore_ref, oim_ref):
        p_r, p_i = ar_ref[...], ai_ref[...]
        if adjoint:
            p_i = -p_i
        acc_r, acc_i = jnp.ones_like(p_r), jnp.zeros_like(p_r)
        n = seg_len
        while n:
            if n & 1:
                acc_r, acc_i = acc_r * p_r - acc_i * p_i, acc_r * p_i + acc_i * p_r
            n >>= 1
            if n:
                p_r, p_i = p_r * p_r - p_i * p_i, 2.0 * p_r * p_i
        e_r, e_i = ere_ref[...], eim_ref[...]
        s_r, s_i = jnp.zeros_like(acc_r), jnp.zeros_like(acc_r)
        rows_r, rows_i = [None] * NSEG, [None] * NSEG
        order = range(NSEG - 1, -1, -1) if adjoint else range(NSEG)
        for j in order:
            rows_r[j], rows_i[j] = s_r, s_i
            s_r, s_i = (acc_r * s_r - acc_i * s_i + e_r[j:j + 1], acc_r * s_i + acc_i * s_r + e_i[j:j + 1])
        ore_ref[...] = jnp.concatenate(rows_r, axis=0)
        oim_ref[...] = jnp.concatenate(rows_i, axis=0)

    st = jax.ShapeDtypeStruct((NSEG, NSTATE), F32)
    return pl.pallas_call(body, name="s5_segment_states_bwd" if adjoint else "s5_segment_states_fwd", out_shape=[st, st])(
        e_re, e_im, ar, ai)


def _s5_fwd_ends(x, pv, w_in, b_re, b_im, ar, ai, layer, tm, after=None):
    L = x.shape[0]

    def body(x_ref, pv_ref, win_ref, bre_ref, bim_ref, ar_ref, ai_ref, h_ref, u_ref, ere_ref, eim_ref, bu_re, bu_im):
        @pl.when(pl.program_id(0) == 0)
        def _():
            ere_ref[...] = jnp.zeros_like(ere_ref)
            eim_ref[...] = jnp.zeros_like(eim_ref)

        p = pv_ref[...]
        h, _, _ = _norm_mod(x_ref[...], p[R_N1:R_N1 + 1], p[R_SC1:R_SC1 + 1], p[R_SH1:R_SH1 + 1])
        hb = _bf(h)
        h_ref[...] = hb
        u = _dot(hb, win_ref[...])
        u_ref[...] = u
        ub = _bf(u)
        for k in range(S5_NB):
            uk = ub[:, k * S5_BH:(k + 1) * S5_BH]
            bu_re[:, k * S5_BP:(k + 1) * S5_BP] = _dot(uk, bre_ref[k])
            bu_im[:, k * S5_BP:(k + 1) * S5_BP] = _dot(uk, bim_ref[k])
        _segment_scan(bu_re, bu_im, ere_ref, eim_ref, ar_ref[...], ai_ref[...], tm // 8, adjoint=False, write=False)

    call, tail = _call_after(
        after, body, 7, name="s5_fwd_ends", grid=(L // tm,),
        in_specs=[_rows(tm, D), pl.BlockSpec((None, 8, D), lambda i: (layer, 0, 0)), _layer_w(D, D, 0),
                  _const_w((S5_NB, S5_BH, S5_BP)), _const_w((S5_NB, S5_BH, S5_BP)), _whole((1, NSTATE)), _whole((1, NSTATE))],
        out_specs=[_rows(tm, D), _rows(tm, D), _whole((NSEG, NSTATE)), _whole((NSEG, NSTATE))],
        out_shape=[jax.ShapeDtypeStruct((L, D), BF16), jax.ShapeDtypeStruct((L, D), F32),
                   jax.ShapeDtypeStruct((NSEG, NSTATE), F32), jax.ShapeDtypeStruct((NSEG, NSTATE), F32)],
        scratch_shapes=[pltpu.VMEM((tm, NSTATE), F32), pltpu.VMEM((tm, NSTATE), F32)],
        compiler_params=_params(1, 56),
    )
    return call(x, pv, w_in, b_re, b_im, ar, ai, *tail)


def _s5_fwd_out(x, u, pv, b_re, b_im, s0_re, s0_im, ar, ai, c_re, c_im, dvec, glu_w, glu_b, w_out, layer, tm):
    L = x.shape[0]

    def body(x_ref, u_ref, pv_ref, bre_ref, bim_ref, s0re_ref, s0im_ref, ar_ref, ai_ref, cre_ref, cim_ref, d_ref, gw_ref,
             gb_ref, wout_ref, x1_ref, sre_ref, sim_ref, y1_ref, zg_ref, y3_ref, y_ref, st_re, st_im):
        @pl.when(pl.program_id(0) == 0)
        def _():
            st_re[...] = s0re_ref[...]
            st_im[...] = s0im_ref[...]

        p = pv_ref[...]
        uv = u_ref[...]
        ub = _bf(uv)
        for k in range(S5_NB):
            uk = ub[:, k * S5_BH:(k + 1) * S5_BH]
            sre_ref[:, k * S5_BP:(k + 1) * S5_BP] = _dot(uk, bre_ref[k])
            sim_ref[:, k * S5_BP:(k + 1) * S5_BP] = _dot(uk, bim_ref[k])
        _segment_scan(sre_ref, sim_ref, st_re, st_im, ar_ref[...], ai_ref[...], tm // 8, adjoint=False, write=True)
        parts = []
        for k in range(S5_NB):
            sl = slice(k * S5_BP, (k + 1) * S5_BP)
            parts.append(_dot(_bf(sre_ref[:, sl]), cre_ref[k]) - _dot(_bf(sim_ref[:, sl]), cim_ref[k]))
        y1 = jnp.concatenate(parts, axis=1) + d_ref[...] * uv
        y1_ref[...] = y1
        y2 = jax.nn.gelu(y1)
        zg = _dot(_bf(y2), gw_ref[...]) + gb_ref[...]
        zg_ref[...] = zg
        y3b = _bf(y2 * jax.nn.sigmoid(zg))
        y3_ref[...] = y3b
        y = _dot(y3b, wout_ref[...])
        y_ref[...] = y
        x1_ref[...] = x_ref[...] + p[R_G1:R_G1 + 1] * y

    return pl.pallas_call(
        body, name="s5_fwd_out", grid=(L // tm,),
        in_specs=[_rows(tm, D), _rows(tm, D), pl.BlockSpec((None, 8, D), lambda i: (layer, 0, 0)),
                  _const_w((S5_NB, S5_BH, S5_BP)), _const_w((S5_NB, S5_BH, S5_BP)),
                  _whole((NSEG, NSTATE)), _whole((NSEG, NSTATE)), _whole((1, NSTATE)), _whole((1, NSTATE)),
                  _const_w((S5_NB, S5_BP, S5_BH)), _const_w((S5_NB, S5_BP, S5_BH)), _whole((1, D)),
                  _layer_w(D, D, 0), _whole((1, D)), _layer_w(D, D, 0)],
        out_specs=[_rows(tm, D), _rows(tm, NSTATE), _rows(tm, NSTATE), _rows(tm, D), _rows(tm, D), _rows(tm, D), _rows(tm, D)],
        out_shape=[jax.ShapeDtypeStruct((L, D), F32), jax.ShapeDtypeStruct((L, NSTATE), F32), jax.ShapeDtypeStruct((L, NSTATE), F32),
                   jax.ShapeDtypeStruct((L, D), F32), jax.ShapeDtypeStruct((L, D), F32),
                   jax.ShapeDtypeStruct((L, D), BF16), jax.ShapeDtypeStruct((L, D), F32)],
        scratch_shapes=[pltpu.VMEM((NSEG, NSTATE), F32), pltpu.VMEM((NSEG, NSTATE), F32)],
        compiler_params=_params(1, 56),
    )(x, u, pv, b_re, b_im, s0_re, s0_im, ar, ai, c_re, c_im, dvec, glu_w, glu_b, w_out)


def _s5_bwd_ends(dx1, y, y1, zg, u, pv, c_re, c_im, ar, ai, dvec, glu_w, w_out, layer, tm, after=None):
    L = dx1.shape[0]
    nt = L // tm

    def body(dx1_ref, y_ref, y1_ref, zg_ref, u_ref, pv_ref, cre_ref, cim_ref, ar_ref, ai_ref, d_ref, gw_ref, wout_ref,
             dy_ref, y2_ref, dzg_ref, dy1_ref, dus_ref, ere_ref, eim_ref, vs_ref, g_re, g_im):
        @pl.when(pl.program_id(0) == 0)
        def _():
            vs_ref[...] = jnp.zeros_like(vs_ref)
            ere_ref[...] = jnp.zeros_like(ere_ref)
            eim_ref[...] = jnp.zeros_like(eim_ref)

        dx1v, p = dx1_ref[...], pv_ref[...]
        dyb = _bf(dx1v * p[R_G1:R_G1 + 1])
        dy_ref[...] = dyb
        vs_ref[0:1, :] += _sum0(dx1v * y_ref[...])
        dy3 = _dot_nt(dyb, wout_ref[...])
        y2, gelu_vjp = jax.vjp(jax.nn.gelu, y1_ref[...])
        y2_ref[...] = _bf(y2)
        gate = jax.nn.sigmoid(zg_ref[...])
        dzg = dy3 * y2 * gate * (1.0 - gate)
        dzgb = _bf(dzg)
        dzg_ref[...] = dzgb
        vs_ref[1:2, :] += _sum0(dzg)
        dy2 = dy3 * gate + _dot_nt(dzgb, gw_ref[...])
        dy1 = gelu_vjp(dy2)[0]
        vs_ref[2:3, :] += _sum0(dy1 * u_ref[...])
        dus_ref[...] = dy1 * d_ref[...]
        dy1b = _bf(dy1)
        dy1_ref[...] = dy1b
        for k in range(S5_NB):
            dk = dy1b[:, k * S5_BH:(k + 1) * S5_BH]
            g_re[:, k * S5_BP:(k + 1) * S5_BP] = _dot_nt(dk, cre_ref[k])
            g_im[:, k * S5_BP:(k + 1) * S5_BP] = -_dot_nt(dk, cim_ref[k])
        _segment_scan(g_re, g_im, ere_ref, eim_ref, ar_ref[...], ai_ref[...], tm // 8, adjoint=True, write=False)

    call, tail = _call_after(
        after, body, 13, name="s5_bwd_ends", grid=(nt,),
        in_specs=[_rows(tm, D, nt)] * 5 + [pl.BlockSpec((None, 8, D), lambda i: (layer, 0, 0)),
                  _const_w((S5_NB, S5_BP, S5_BH)), _const_w((S5_NB, S5_BP, S5_BH)), _whole((1, NSTATE)), _whole((1, NSTATE)),
                  _whole((1, D)), _layer_w(D, D, 0), _layer_w(D, D, 0)],
        out_specs=[_rows(tm, D, nt)] * 5 + [_whole((NSEG, NSTATE)), _whole((NSEG, NSTATE)), _whole((8, D))],
        out_shape=[jax.ShapeDtypeStruct((L, D), BF16)] * 4 + [jax.ShapeDtypeStruct((L, D), F32),
                   jax.ShapeDtypeStruct((NSEG, NSTATE), F32), jax.ShapeDtypeStruct((NSEG, NSTATE), F32),
                   jax.ShapeDtypeStruct((8, D), F32)],
        scratch_shapes=[pltpu.VMEM((tm, NSTATE), F32), pltpu.VMEM((tm, NSTATE), F32)],
        compiler_params=_params(1, 56),
    )
    return call(dx1, y, y1, zg, u, pv, c_re, c_im, ar, ai, dvec, glu_w, w_out, *tail)


def _s5_bwd_in(dx1, dy1_b, du_skip, x, s_re, s_im, pv, b_re, b_im, c_re, c_im, l0_re, l0_im, ar, ai, w_in, layer, tm):
    L = x.shape[0]
    nt = L // tm

    def body(dx1_ref, dy1_ref, dus_ref, x_ref, sre_ref, sim_ref, hre_ref, him_ref, lre_ref, lim_ref, pv_ref, bre_ref, bim_ref,
             cre_ref, cim_ref, l0re_ref, l0im_ref, ar_ref, ai_ref, win_ref,
             dx_ref, du_ref, lamre_ref, lamim_ref, da_ref, vs_ref, g_re, g_im, st_re, st_im):
        gi = pl.program_id(0)
        tile = nt - 1 - gi

        @pl.when(gi == 0)
        def _():
            vs_ref[...] = jnp.zeros_like(vs_ref)
            da_ref[...] = jnp.zeros_like(da_ref)
            st_re[...] = l0re_ref[...]
            st_im[...] = l0im_ref[...]

        p = pv_ref[...]
        dy1b = dy1_ref[...]
        for k in range(S5_NB):
            dk = dy1b[:, k * S5_BH:(k + 1) * S5_BH]
            g_re[:, k * S5_BP:(k + 1) * S5_BP] = _dot_nt(dk, cre_ref[k])
            g_im[:, k * S5_BP:(k + 1) * S5_BP] = -_dot_nt(dk, cim_ref[k])
        _segment_scan(g_re, g_im, st_re, st_im, ar_ref[...], ai_ref[...], tm // 8, adjoint=True, write=True)
        lam_r, lam_i = g_re[...], g_im[...]
        lrb, lib = _bf(lam_r), _bf(lam_i)
        lamre_ref[...] = lrb
        lamim_ref[...] = lib

        def wrapped(last_ref):
            z = last_ref[...]
            row = lax.broadcasted_iota(jnp.int32, z.shape, 0)
            return jnp.where(row >= 1, pltpu.roll(z, 1, 0), 0.0)

        first_r = jnp.where(tile > 0, hre_ref[...], wrapped(lre_ref))
        first_i = jnp.where(tile > 0, him_ref[...], wrapped(lim_ref))
        sp_r = jnp.concatenate([first_r, sre_ref[0:tm - 8, :]], axis=0)
        sp_i = jnp.concatenate([first_i, sim_ref[0:tm - 8, :]], axis=0)
        da_ref[0:1, :] += _sum0(lam_r * sp_r + lam_i * sp_i)
        da_ref[1:2, :] += _sum0(lam_i * sp_r - lam_r * sp_i)

        parts = []
        for k in range(S5_NB):
            sl = slice(k * S5_BP, (k + 1) * S5_BP)
            parts.append(_dot_nt(lrb[:, sl], bre_ref[k]) + _dot_nt(lib[:, sl], bim_ref[k]))
        dub = _bf(jnp.concatenate(parts, axis=1) + dus_ref[...])
        du_ref[...] = dub
        dh = _dot_nt(dub, win_ref[...])
        _, xn, r = _norm_mod(x_ref[...], p[R_N1:R_N1 + 1], p[R_SC1:R_SC1 + 1], p[R_SH1:R_SH1 + 1])
        dx_ref[...] = dx1_ref[...] + _norm_mod_bwd(dh, xn, r, p[R_N1:R_N1 + 1], p[R_SC1:R_SC1 + 1])
        vs_ref[1:2, :] += _sum0(dh * xn)
        vs_ref[2:3, :] += _sum0(dh)

    halo = pl.BlockSpec((8, NSTATE), lambda i: (jnp.maximum((nt - 1 - i) * (tm // 8) - 1, 0), 0))
    last = pl.BlockSpec((8, NSTATE), lambda i: (L // 8 - 1, 0))
    return pl.pallas_call(
        body, name="s5_bwd_in", grid=(nt,),
        in_specs=[_rows(tm, D, nt), _rows(tm, D, nt), _rows(tm, D, nt), _rows(tm, D, nt), _rows(tm, NSTATE, nt), _rows(tm, NSTATE, nt),
                  halo, halo, last, last, pl.BlockSpec((None, 8, D), lambda i: (layer, 0, 0)),
                  _const_w((S5_NB, S5_BH, S5_BP)), _const_w((S5_NB, S5_BH, S5_BP)),
                  _const_w((S5_NB, S5_BP, S5_BH)), _const_w((S5_NB, S5_BP, S5_BH)),
                  _whole((NSEG, NSTATE)), _whole((NSEG, NSTATE)), _whole((1, NSTATE)), _whole((1, NSTATE)), _layer_w(D, D, 0)],
        out_specs=[_rows(tm, D, nt), _rows(tm, D, nt), _rows(tm, NSTATE, nt), _rows(tm, NSTATE, nt), _whole((8, NSTATE)), _whole((8, D))],
        out_shape=[jax.ShapeDtypeStruct((L, D), F32), jax.ShapeDtypeStruct((L, D), BF16),
                   jax.ShapeDtypeStruct((L, NSTATE), BF16), jax.ShapeDtypeStruct((L, NSTATE), BF16),
                   jax.ShapeDtypeStruct((8, NSTATE), F32), jax.ShapeDtypeStruct((8, D), F32)],
        scratch_shapes=[pltpu.VMEM((tm, NSTATE), F32), pltpu.VMEM((tm, NSTATE), F32),
                        pltpu.VMEM((NSEG, NSTATE), F32), pltpu.VMEM((NSEG, NSTATE), F32)],
        compiler_params=_params(1, 60),
    )(dx1, dy1_b, du_skip, x, s_re, s_im, s_re, s_im, s_re, s_im, pv, b_re, b_im, c_re, c_im, l0_re, l0_im, ar, ai, w_in)


def _blockdiag_b(bt):
    b = bt.reshape(S5_H, S5_NB, 16, S5_P).transpose(1, 2, 0, 3)
    eye = jnp.eye(16, dtype=bt.dtype)
    return (b[:, :, :, None, :] * eye[None, :, None, :, None]).reshape(S5_NB, S5_BH, S5_BP)


def _unblock_b(d):
    d = jnp.einsum("bghgp->bghp", d.reshape(S5_NB, 16, S5_H, 16, S5_P))
    return d.transpose(2, 0, 1, 3).reshape(S5_H, S5_G, S5_P)


def _blockdiag_c(cm):
    c4 = cm.reshape(S5_NB, 16, S5_H, S5_P)
    eye = jnp.eye(16, dtype=cm.dtype)
    out = c4.transpose(0, 1, 3, 2)[:, :, :, None, :] * eye[None, :, None, :, None]
    return out.reshape(S5_NB, S5_BP, S5_BH)


def _unblock_c(d):
    d = jnp.einsum("bgpgh->bghp", d.reshape(S5_NB, 16, S5_P, 16, S5_H))
    return d.reshape(S5_G, S5_H, S5_P)


def _tril_mask():
    return lax.broadcasted_iota(jnp.int32, (SG_CHUNK, SG_CHUNK), 0) >= lax.broadcasted_iota(jnp.int32, (SG_CHUNK, SG_CHUNK), 1)


def _sg_fwd(x, pv, w_in, w_s, b_t, vg, w_out, layer, tm, after=None):
    L = x.shape[0]
    nc = tm // SG_CHUNK

    def body(x_ref, pv_ref, win_ref, ws_ref, bt_ref, vg_ref, wout_ref, x1_ref, h_ref, uv_ref, vm_ref, q_ref, y_ref):
        xv, p = x_ref[...], pv_ref[...]
        h, _, _ = _norm_mod(xv, p[R_N1:R_N1 + 1], p[R_SC1:R_SC1 + 1], p[R_SH1:R_SH1 + 1])
        hb = _bf(h)
        h_ref[...] = hb
        uv = _dot(hb, win_ref[...])
        uv_ref[...] = uv
        v = uv[:, D:]
        rv = lax.rsqrt(jnp.mean(v * v, axis=-1, keepdims=True) + EPS)
        vnb = _bf((v * rv) * vg_ref[...])
        mask = _tril_mask()
        bt = bt_ref[...]
        for hd in range(SG_HEADS):
            wm = _bf(jnp.where(mask, ws_ref[hd], 0.0))
            cs = slice(hd * SG_CHUNK, (hd + 1) * SG_CHUNK)
            for ck in range(nc):
                rs = slice(ck * SG_CHUNK, (ck + 1) * SG_CHUNK)
                vm_ref[rs, cs] = _dot(wm, vnb[rs, cs]) + bt[:, hd:hd + 1]
        qb = _bf(uv[:, :D] * vm_ref[...])
        q_ref[...] = qb
        y = _dot(qb, wout_ref[...])
        y_ref[...] = y
        x1_ref[...] = xv + p[R_G1:R_G1 + 1] * y

    call, tail = _call_after(
        after, body, 7, name="sg_fwd", grid=(L // tm,),
        in_specs=[_rows(tm, D), pl.BlockSpec((None, 8, D), lambda i: (layer, 0, 0)), _layer_w(D, 2 * D, 0),
                  _whole((SG_HEADS, SG_CHUNK, SG_CHUNK)), _whole((SG_CHUNK, SG_HEADS)), _whole((1, D)), _layer_w(D, D, 0)],
        out_specs=[_rows(tm, D), _rows(tm, D), _rows(tm, 2 * D), _rows(tm, D), _rows(tm, D), _rows(tm, D)],
        out_shape=[jax.ShapeDtypeStruct((L, D), F32), jax.ShapeDtypeStruct((L, D), BF16), jax.ShapeDtypeStruct((L, 2 * D), F32),
                   jax.ShapeDtypeStruct((L, D), F32), jax.ShapeDtypeStruct((L, D), BF16), jax.ShapeDtypeStruct((L, D), F32)],
        compiler_params=_params(1, 56),
    )
    return call(x, pv, w_in, w_s, b_t, vg, w_out, *tail)


def _sg_bwd(dx1, x, y, uv, vm, pv, w_in, w_s, vg, w_out, layer, tm, after=None):
    L = x.shape[0]
    nc = tm // SG_CHUNK

    def body(dx1_ref, x_ref, y_ref, uv_ref, vm_ref, pv_ref, win_ref, ws_ref, vg_ref, wout_ref,
             dx_ref, duv_ref, dy_ref, vs_ref, dws_ref, dbt_ref, dvn_scr):
        @pl.when(pl.program_id(0) == 0)
        def _():
            vs_ref[...] = jnp.zeros_like(vs_ref)
            dws_ref[...] = jnp.zeros_like(dws_ref)
            dbt_ref[...] = jnp.zeros_like(dbt_ref)

        dx1v, p = dx1_ref[...], pv_ref[...]
        dyb = _bf(dx1v * p[R_G1:R_G1 + 1])
        dy_ref[...] = dyb
        vs_ref[0:1, :] += _sum0(dx1v * y_ref[...])
        dq = _dot_nt(dyb, wout_ref[...])
        uv = uv_ref[...]
        u, v = uv[:, :D], uv[:, D:]
        dub = _bf(dq * vm_ref[...])
        dvm = dq * u
        dvmb = _bf(dvm)
        rv = lax.rsqrt(jnp.mean(v * v, axis=-1, keepdims=True) + EPS)
        vh = v * rv
        vgv = vg_ref[...]
        vnb = _bf(vh * vgv)
        mask = _tril_mask()
        for hd in range(SG_HEADS):
            wm = _bf(jnp.where(mask, ws_ref[hd], 0.0))
            cs = slice(hd * SG_CHUNK, (hd + 1) * SG_CHUNK)
            dws = jnp.zeros((SG_CHUNK, SG_CHUNK), F32)
            dbs = jnp.zeros((SG_CHUNK, 1), F32)
            for ck in range(nc):
                rs = slice(ck * SG_CHUNK, (ck + 1) * SG_CHUNK)
                dvn_scr[rs, cs] = _dot_tn(wm, dvmb[rs, cs])
                dws = dws + _dot_nt(dvmb[rs, cs], vnb[rs, cs])
                dbs = dbs + jnp.sum(dvm[rs, cs], axis=1, keepdims=True)
            dws_ref[hd] += jnp.where(mask, dws, 0.0)
            dbt_ref[:, hd:hd + 1] += dbs
        dvn = dvn_scr[...]
        vs_ref[3:4, :] += _sum0(dvn * vh)
        dvnn = dvn * vgv
        dvb = _bf(rv * (dvnn - vh * jnp.mean(dvnn * vh, axis=-1, keepdims=True)))
        duv_ref[:, 0:D] = dub
        duv_ref[:, D:2 * D] = dvb
        dh = _dot_nt(dub, win_ref[:, 0:D]) + _dot_nt(dvb, win_ref[:, D:2 * D])
        _, xn, r = _norm_mod(x_ref[...], p[R_N1:R_N1 + 1], p[R_SC1:R_SC1 + 1], p[R_SH1:R_SH1 + 1])
        dx_ref[...] = dx1v + _norm_mod_bwd(dh, xn, r, p[R_N1:R_N1 + 1], p[R_SC1:R_SC1 + 1])
        vs_ref[1:2, :] += _sum0(dh * xn)
        vs_ref[2:3, :] += _sum0(dh)

    call, tail = _call_after(
        after, body, 10, name="sg_bwd", grid=(L // tm,),
        in_specs=[_rows(tm, D), _rows(tm, D), _rows(tm, D), _rows(tm, 2 * D), _rows(tm, D),
                  pl.BlockSpec((None, 8, D), lambda i: (layer, 0, 0)), _layer_w(D, 2 * D, 0),
                  _whole((SG_HEADS, SG_CHUNK, SG_CHUNK)), _whole((1, D)), _layer_w(D, D, 0)],
        out_specs=[_rows(tm, D), _rows(tm, 2 * D), _rows(tm, D), _whole((8, D)),
                   _whole((SG_HEADS, SG_CHUNK, SG_CHUNK)), _whole((SG_CHUNK, SG_HEADS))],
        out_shape=[jax.ShapeDtypeStruct((L, D), F32), jax.ShapeDtypeStruct((L, 2 * D), BF16), jax.ShapeDtypeStruct((L, D), BF16),
                   jax.ShapeDtypeStruct((8, D), F32), jax.ShapeDtypeStruct((SG_HEADS, SG_CHUNK, SG_CHUNK), F32),
                   jax.ShapeDtypeStruct((SG_CHUNK, SG_HEADS), F32)],
        scratch_shapes=[pltpu.VMEM((tm, D), F32)],
        compiler_params=_params(1, 56),
    )
    return call(dx1, x, y, uv, vm, pv, w_in, w_s, vg, w_out, *tail)


def _final(x, target, fg, tm):
    L = x.shape[0]

    def body(x_ref, t_ref, g_ref, dx_ref, vs_ref):
        @pl.when(pl.program_id(0) == 0)
        def _():
            vs_ref[...] = jnp.zeros_like(vs_ref)

        xv, g = x_ref[...], g_ref[...]
        r = lax.rsqrt(jnp.mean(xv * xv, axis=-1, keepdims=True) + EPS)
        xn = xv * r
        e = xn * g - t_ref[...]
        vs_ref[0:1, :] += jnp.sum(e * e)
        dout = e * (1.0 / D)
        vs_ref[1:2, :] += _sum0(dout * xn)
        dxn = dout * g
        dx_ref[...] = r * (dxn - xn * jnp.mean(dxn * xn, axis=-1, keepdims=True))

    return pl.pallas_call(
        body, name="final_loss", grid=(L // tm,),
        in_specs=[_rows(tm, D), _rows(tm, D), _whole((1, D))],
        out_specs=[_rows(tm, D), _whole((8, D))],
        out_shape=[jax.ShapeDtypeStruct((L, D), F32), jax.ShapeDtypeStruct((8, D), F32)],
        compiler_params=_params(1),
    )(x, target, fg)


def _pack_flat(arrs, multiple=LANES):
    flat = jnp.concatenate([a.reshape(-1).astype(F32) for a in arrs])
    return jnp.pad(flat, (0, -flat.shape[0] % multiple))


def _pack(arrs, row_multiple=8):
    return _pack_flat(arrs, row_multiple * LANES).reshape(-1, LANES)


def _unpack(buf, shapes, lead=()):
    flat = buf.reshape(lead + (-1,))
    out, off = [], 0
    for s in shapes:
        n = 1
        for d in s:
            n *= d
        out.append(flat[..., off:off + n].reshape(lead + tuple(s)))
        off += n
    return out


BIG = ("ff_w1", "ff_w2", "conv_w_in", "conv_w_out", "ssm_w_in", "ssm_glu_w", "ssm_w_out", "sg_w_in", "sg_w_out")
BIG_AXIS = {"ff_w1": 2, "ff_w2": 1, "conv_w_in": 2, "conv_w_out": 1, "ssm_w_in": 1, "ssm_glu_w": 1, "ssm_w_out": 1,
            "sg_w_in": 2, "sg_w_out": 1}
LAYER_WEIGHTS = (
    (("conv_w_in", 0), ("conv_w_out", 0), ("ff_w1", 0), ("ff_w2", 0)),
    (("ssm_w_in", 0), ("ssm_glu_w", 0), ("ssm_w_out", 0), ("ff_w1", 1), ("ff_w2", 1)),
    (("sg_w_in", 0), ("sg_w_out", 0), ("ff_w1", 2), ("ff_w2", 2)),
    (("conv_w_in", 1), ("conv_w_out", 1), ("ff_w1", 3), ("ff_w2", 3)),
)
GATHER_GROUPS = tuple(grp for lw in LAYER_WEIGHTS for grp in (lw[:-2], lw[-2:]))
SMALL_SHARDED = ("conv_w", "conv_b", "sg_v_g")
SMALL_WIDE_PADDED = ("ssm_b_re", "ssm_b_im")
SMALL = ("ada_b", "norm1_g", "norm2_g", "final_g", "ssm_a_re", "ssm_a_im", "ssm_log_dt", "ssm_b_re", "ssm_b_im", "ssm_c_re",
         "ssm_c_im", "ssm_d", "ssm_glu_b", "sg_w_s", "sg_b_s") + SMALL_SHARDED
WEIGHTS = ("ada_w", "ada_b", "norm1_g", "norm2_g", "ff_w1", "ff_w2", "final_g", "conv_w_in", "conv_w", "conv_b", "conv_w_out",
           "ssm_w_in", "ssm_a_re", "ssm_a_im", "ssm_log_dt", "ssm_b_re", "ssm_b_im", "ssm_c_re", "ssm_c_im", "ssm_d",
           "ssm_glu_w", "ssm_glu_b", "ssm_w_out", "sg_w_in", "sg_v_g", "sg_w_s", "sg_b_s", "sg_w_out")


def kernel(x, c, ada_w, ada_b, norm1_g, norm2_g, ff_w1, ff_w2, final_g, conv_w_in, conv_w, conv_b, conv_w_out, ssm_w_in, ssm_a_re, ssm_a_im, ssm_log_dt, ssm_b_re, ssm_b_im, ssm_c_re, ssm_c_im, ssm_d, ssm_glu_w, ssm_glu_b, ssm_w_out, sg_w_in, sg_v_g, sg_w_s, sg_b_s, sg_w_out, loss_target, m_ada_w, m_ada_b, m_norm1_g, m_norm2_g, m_ff_w1, m_ff_w2, m_final_g, m_conv_w_in, m_conv_w, m_conv_b, m_conv_w_out, m_ssm_w_in, m_ssm_a_re, m_ssm_a_im, m_ssm_log_dt, m_ssm_b_re, m_ssm_b_im, m_ssm_c_re, m_ssm_c_im, m_ssm_d, m_ssm_glu_w, m_ssm_glu_b, m_ssm_w_out, m_sg_w_in, m_sg_v_g, m_sg_w_s, m_sg_b_s, m_sg_w_out, v_ada_w, v_ada_b, v_norm1_g, v_norm2_g, v_ff_w1, v_ff_w2, v_final_g, v_conv_w_in, v_conv_w, v_conv_b, v_conv_w_out, v_ssm_w_in, v_ssm_a_re, v_ssm_a_im, v_ssm_log_dt, v_ssm_b_re, v_ssm_b_im, v_ssm_c_re, v_ssm_c_im, v_ssm_d, v_ssm_glu_w, v_ssm_glu_b, v_ssm_w_out, v_sg_w_in, v_sg_v_g, v_sg_w_s, v_sg_b_s, v_sg_w_out):
    args = dict(locals())
    w = {n: args[n] for n in WEIGHTS}
    m = {n: args["m_" + n] for n in WEIGHTS}
    v = {n: args["v_" + n] for n in WEIGHTS}
    L = x.shape[1]
    tm = min(L, 256)
    chip = 2 * lax.axis_index("x") + lax.axis_index("y")
    me = 2 * chip + lax.axis_index("c")
    xin = x[0]
    target = loss_target[0]
    chip1 = chip.reshape(1).astype(jnp.int32)

    gathers = []

    def start_gather(g, after):
        entries = GATHER_GROUPS[g]
        axes = [BIG_AXIS[n] for n, _ in entries]
        lands = [_cast_place(w[n], li, BIG_AXIS[n], chip1, f"cast_{n}_{li}") for n, li in entries]
        s_sems, r_sems, lands, token = _gather_start(lands, axes, f"gather_start{g}", after)
        gathers.append((s_sems, r_sems, lands, axes))
        return token

    def weights_of(g, after):
        s_sems, r_sems, lands, axes = gathers[g]
        lands = _gather_wait(s_sems, r_sems, lands, axes, f"gather_wait{g}", after)
        lands = _gather_share(lands, axes, f"gather_share{g}")
        token = start_gather(g + 2, lands[0]) if g + 2 < len(GATHER_GROUPS) else None
        return dict(zip([n for n, _ in GATHER_GROUPS[g]], lands)), token

    small_in = _pack([c, conv_w, conv_b, sg_v_g])
    got = _allgather_small(small_in, "gather_small_inputs").reshape(N_DEV, -1)
    c_all, cw_sh, cb_sh, vg_sh = _unpack(got, [(D,), conv_w.shape, conv_b.shape, sg_v_g.shape], lead=(N_DEV,))
    conv_w_full = jnp.concatenate([cw_sh[2 * k] for k in range(4)], axis=-1)
    conv_b_full = jnp.concatenate([cb_sh[2 * k] for k in range(4)], axis=-1)
    vg_full = jnp.concatenate([vg_sh[2 * k] for k in range(4)], axis=-1)
    c16 = jnp.pad(c_all, ((0, 16 - N_DEV), (0, 0)))

    cols = ada_w.shape[2]
    ada_b_cols = lax.dynamic_slice_in_dim(ada_b, chip * cols, cols, axis=1)[:, None, :]
    mod_sh = _ada_fwd(c16, ada_w, ada_b_cols)[:, :N_DEV, :]
    mod_all = _allgather_small(_pack([mod_sh]), "gather_mod").reshape(N_DEV, -1)
    mod_all = _unpack(mod_all, [mod_sh.shape], lead=(N_DEV,))[0]
    mod_mine = lax.dynamic_index_in_dim(mod_all[0::2], me, axis=2, keepdims=False)
    mod_mine = mod_mine.transpose(1, 0, 2).reshape(DEPTH, 6, D)
    pv = jnp.concatenate([mod_mine, norm1_g[:, None, :], norm2_g[:, None, :]], axis=1)

    start_gather(1, start_gather(0, pv))

    cw_rows = jnp.concatenate([conv_w_full, conv_b_full[:, None, :], jnp.zeros((conv_w_full.shape[0], 4, D), F32)], axis=1)

    a_re, a_im = ssm_a_re[0], ssm_a_im[0]
    log_dt = ssm_log_dt[0][:, None]
    bt_re, bt_im = ssm_b_re[0].transpose(2, 0, 1), ssm_b_im[0].transpose(2, 0, 1)
    abar_re, abar_im, bbar_re, bbar_im = _s5_params_fwd(a_re, a_im, log_dt, bt_re, bt_im)
    ar_vec, ai_vec = abar_re.reshape(1, NSTATE), abar_im.reshape(1, NSTATE)
    bd_re, bd_im = _bf(_blockdiag_b(bbar_re)), _bf(_blockdiag_b(bbar_im))
    cd_re, cd_im = _bf(_blockdiag_c(ssm_c_re[0])), _bf(_blockdiag_c(ssm_c_im[0]))

    saved = []
    fulls = []
    xl = xin
    for i in range(DEPTH):
        kind = MIXER_OF_LAYER[i]
        j = i // 3
        full, tok = weights_of(2 * i, cd_im if i == 0 else xl)
        fulls.append(full)
        if kind == 0:
            x1, h, bcx, conv, q, y = _conv_fwd(xl, pv, full["conv_w_in"], full["conv_w_out"], cw_rows, i, j, tm, after=tok)
            mix = dict(h=h, bcx=bcx, conv=conv, q=q, y=y)
        elif kind == 1:
            xp = _to_segments(xl)
            h, u, e_re, e_im = _s5_fwd_ends(xp, pv, full["ssm_w_in"], bd_re, bd_im, ar_vec, ai_vec, i, tm, after=tok)
            s0_re, s0_im = _s5_segment_states(e_re, e_im, ar_vec, ai_vec, L // NSEG, adjoint=False)
            x1p, s_re, s_im, y1, zg, y3, y = _s5_fwd_out(xp, u, pv, bd_re, bd_im, s0_re, s0_im, ar_vec, ai_vec, cd_re, cd_im,
                                                         ssm_d, full["ssm_glu_w"], ssm_glu_b, full["ssm_w_out"], i, tm)
            x1 = _from_segments(x1p)
            mix = dict(xp=xp, h=h, u=u, s_re=s_re, s_im=s_im, y1=y1, zg=zg, y3=y3, y=y)
        else:
            x1, h, uv, vm, q, y = _sg_fwd(xl, pv, full["sg_w_in"], sg_w_s[0], sg_b_s[0].T, vg_full, full["sg_w_out"], i, tm,
                                          after=tok)
            mix = dict(h=h, uv=uv, vm=vm, q=q, y=y)
        ffn_weights, tok = weights_of(2 * i + 1, x1)
        full.update(ffn_weights)
        x2, h2, a, f = _ffn_fwd(x1, pv, full["ff_w1"], full["ff_w2"], i, tm, after=tok)
        saved.append(dict(x=xl, x1=x1, h2=h2, a=a, f=f, **mix))
        xl = x2

    dxl, vs_fin = _final(xl, target, final_g[None, :], tm)

    gfull = {n: [None] * w[n].shape[0] for n in BIG}
    vs_mix, vs_ffn = [None] * DEPTH, [None] * DEPTH
    small_g = {}
    scatters = {}
    token = None

    def start_scatter(key, entries, after):
        garrs = [gfull[n][li][None] for n, li in entries]
        gaxes = [BIG_AXIS[n] for n, _ in entries]
        s_sems, r_sems, garrs, lands, tok = _scatter_start(garrs, gaxes, f"scatter_start{key}", after)
        scatters[key] = (s_sems, r_sems, garrs, lands, gaxes, entries)
        return tok

    for i in reversed(range(DEPTH)):
        kind = MIXER_OF_LAYER[i]
        j = i // 3
        sv = saved[i]
        full = fulls[i]
        dx1, p_b, da_b, df_b, vs_ffn[i] = _ffn_bwd(dxl, sv["x1"], sv["a"], sv["f"], pv, full["ff_w1"], full["ff_w2"], i, tm,
                                                   after=token)
        gfull["ff_w1"][i] = _mm_tn(sv["h2"], da_b, f"wgrad_ff_w1_{i}")
        gfull["ff_w2"][i] = _mm_tn(p_b, df_b, f"wgrad_ff_w2_{i}")
        if i == 0:
            token = start_scatter("0f", LAYER_WEIGHTS[0][2:], dx1)
        if kind == 0:
            dxl, dbcx_b, dy_b, vsm = _conv_bwd(dx1, sv["x"], sv["y"], sv["bcx"], sv["conv"], pv, full["conv_w_in"],
                                               full["conv_w_out"], cw_rows, i, j, tm, after=token if i == 0 else None)
            gfull["conv_w_in"][j] = _mm_tn(sv["h"], dbcx_b, f"wgrad_conv_w_in_{j}")
            gfull["conv_w_out"][j] = _mm_tn(sv["q"], dy_b, f"wgrad_conv_w_out_{j}")
            small_g.setdefault("conv_w", [None, None])[j] = vsm[3:6]
            small_g.setdefault("conv_b", [None, None])[j] = vsm[6]
        elif kind == 1:
            dx1p = _to_segments(dx1)
            dy_b, y2_b, dzg_b, dy1_b, du_skip, eb_re, eb_im, vsm = _s5_bwd_ends(
                dx1p, sv["y"], sv["y1"], sv["zg"], sv["u"], pv, cd_re, cd_im, ar_vec, ai_vec, ssm_d, full["ssm_glu_w"],
                full["ssm_w_out"], i, tm)
            l0_re, l0_im = _s5_segment_states(eb_re, eb_im, ar_vec, ai_vec, L // NSEG, adjoint=True)
            dxp, du_b, lam_re, lam_im, dabar, vs_in = _s5_bwd_in(
                dx1p, dy1_b, du_skip, sv["xp"], sv["s_re"], sv["s_im"], pv, bd_re, bd_im, cd_re, cd_im, l0_re, l0_im,
                ar_vec, ai_vec, full["ssm_w_in"], i, tm)
            dxl = _from_segments(dxp)
            gfull["ssm_w_out"][0] = _mm_tn(sv["y3"], dy_b, "wgrad_ssm_w_out")
            gfull["ssm_glu_w"][0] = _mm_tn(y2_b, dzg_b, "wgrad_ssm_glu_w")
            gfull["ssm_w_in"][0] = _mm_tn(sv["h"], du_b, "wgrad_ssm_w_in")
            s5_late = dict(s_re=sv["s_re"], s_im=sv["s_im"], u=sv["u"], dy1_b=dy1_b, lam_re=lam_re, lam_im=lam_im, dabar=dabar)
            small_g.update(ssm_d=vsm[2], ssm_glu_b=vsm[1])
            vsm = jnp.concatenate([vsm[0:1], vs_in[1:3], jnp.zeros((5, D), F32)], axis=0)
        else:
            dxl, duv_b, dy_b, vsm, d_ws, d_bt = _sg_bwd(dx1, sv["x"], sv["y"], sv["uv"], sv["vm"], pv, full["sg_w_in"],
                                                        sg_w_s[0], vg_full, full["sg_w_out"], i, tm)
            gfull["sg_w_in"][0] = _mm_tn(sv["h"], duv_b, "wgrad_sg_w_in")
            gfull["sg_w_out"][0] = _mm_tn(sv["q"], dy_b, "wgrad_sg_w_out")
            small_g.update(sg_v_g=vsm[3], sg_w_s=d_ws, sg_b_s=d_bt.T)
        vs_mix[i] = vsm
        token = start_scatter(str(i), LAYER_WEIGHTS[i], dxl) if i > 0 else start_scatter("0c", LAYER_WEIGHTS[0][:2], dxl)
    grad_x = dxl[None]

    sums = {n: [None] * w[n].shape[0] for n in BIG}

    def collect(key, after):
        s_sems, r_sems, garrs, lands, gaxes, entries = scatters[key]
        garrs, recv = _scatter_wait(s_sems, r_sems, garrs, lands, gaxes, f"scatter_wait{key}", after)
        for (n, li), g, r3, ax in zip(entries, garrs, recv, gaxes):
            sums[n][li] = _sum_parts(r3, g, ax, chip1, f"sum_{n}_{li}")
        return sums[entries[-1][0]][entries[-1][1]]

    after = token
    for key in ("3", "2", "1"):
        after = collect(key, after)
    early = [(n, li) for i in (3, 2, 1) for n, li in LAYER_WEIGHTS[i]]
    late = list(LAYER_WEIGHTS[0][2:]) + list(LAYER_WEIGHTS[0][:2])
    s_sems, r_sems, mine_thru, lands, tok = _swap_start([sums[n][li] for n, li in early], "swap_start_early", after)

    blocks = dict(
        c_re=_mm_tn_blocks(s5_late["s_re"], s5_late["dy1_b"], S5_BP, S5_BH, "wgrad_s5_c_re", after=tok),
        c_im=_mm_tn_blocks(s5_late["s_im"], s5_late["dy1_b"], S5_BP, S5_BH, "wgrad_s5_c_im", after=tok),
        b_re=_mm_tn_blocks(s5_late["u"], s5_late["lam_re"], S5_BH, S5_BP, "wgrad_s5_b_re", after=tok),
        b_im=_mm_tn_blocks(s5_late["u"], s5_late["lam_im"], S5_BH, S5_BP, "wgrad_s5_b_im", after=tok))
    d_are, d_aim, d_ldt, d_btre, d_btim = _s5_params_bwd(
        a_re, a_im, log_dt, bt_re, bt_im, s5_late["dabar"][0].reshape(S5_G, S5_P), s5_late["dabar"][1].reshape(S5_G, S5_P),
        _unblock_b(blocks["b_re"]), _unblock_b(blocks["b_im"]))
    small_g.update(ssm_a_re=d_are, ssm_a_im=d_aim, ssm_log_dt=d_ldt, ssm_b_re=d_btre.transpose(1, 2, 0),
                   ssm_b_im=d_btim.transpose(1, 2, 0), ssm_c_re=_unblock_c(blocks["c_re"]), ssm_c_im=-_unblock_c(blocks["c_im"]))

    mine_thru, got = _swap_wait(s_sems, r_sems, mine_thru, lands, "swap_wait_early", blocks["b_im"])
    sib = dict(zip(early, got))
    for (n, li), t in zip(early, mine_thru):
        sums[n][li] = t
    after = got[-1]
    for key in ("0f", "0c"):
        after = collect(key, after)
    sib.update(zip(late, _swap_with_sibling([sums[n][li] for n, li in late], "swap_grad_sums_late")))

    dmod = _mod_bwd(jnp.stack(vs_mix), jnp.stack(vs_ffn), pv)
    small_g.update(ada_b=dmod[:, :6, :], norm1_g=dmod[:, 6, :], norm2_g=dmod[:, 7, :], final_g=vs_fin[1],
                   conv_w=jnp.stack(small_g["conv_w"]), conv_b=jnp.stack(small_g["conv_b"]))

    loss_part = (0.5 / D) * vs_fin[0, 0:1]
    part_shapes = [(1,)] + [tuple(small_g[n].shape) for n in SMALL]
    slots = _reduce_pair(_pack([loss_part] + [small_g[n] for n in SMALL], 16), "reduce_small_pair", sib[late[-1]])
    s_sems, r_sems, slots, tok = _reduce_cross_start(slots, "reduce_small_cross_start")

    res = {}
    for n in BIG:
        res[n] = _adamw_layers(w[n], sums[n], [sib[(n, li)] for li in range(w[n].shape[0])], m[n], v[n], f"adamw_{n}", after=tok)
        tok = res[n][0]

    slots = _reduce_cross_wait(s_sems, r_sems, slots, "reduce_small_cross_wait", tok)
    parts_sum = _reduce_finish(slots, "reduce_small_finish")
    summed = _unpack(parts_sum, part_shapes)
    loss = summed[0][0]
    gsum = dict(zip(SMALL, summed[1:]))
    dmod_all = _allgather_small(_pack([small_g["ada_b"]]), "gather_dmod", parts_sum)
    dmod_all = dmod_all.reshape(N_DEV, DEPTH, 6 * D)
    dmod_cols = lax.dynamic_slice_in_dim(dmod_all, chip * cols, cols, axis=2).transpose(1, 0, 2)
    g_ada_w = _ada_bwd(c16, jnp.pad(dmod_cols, ((0, 0), (0, 16 - N_DEV), (0, 0))))

    shp = ada_w.shape
    two = lambda t: t.reshape(shp[0] * shp[1], shp[2])
    res["ada_w"] = [t.reshape(shp) for t in _adamw(two(ada_w), [two(g_ada_w)], two(m_ada_w), two(v_ada_w), "adamw_ada_w")]

    def mine(n):
        g = gsum[n]
        if n in SMALL_SHARDED:
            g = lax.dynamic_slice_in_dim(g, chip * w[n].shape[-1], w[n].shape[-1], axis=g.ndim - 1)
        return g.reshape(w[n].shape)

    for k, names in enumerate(([n for n in SMALL if n not in SMALL_WIDE_PADDED], list(SMALL_WIDE_PADDED))):
        outs = _adamw_many([w[n] for n in names], [mine(n) for n in names], [m[n] for n in names], [v[n] for n in names],
                           f"adamw_small{k}")
        for idx, n in enumerate(names):
            res[n] = [outs[part][idx] for part in range(4)]

    outs = [loss, grad_x]
    for part in range(4):
        outs += [res[n][part] for n in WEIGHTS]
    return tuple(outs)
```

```python
import functools

import jax
import jax.numpy as jnp
from jax import lax
from jax.experimental import pallas as pl
from jax.experimental.pallas import tpu as pltpu

F32 = jnp.float32
BF16 = jnp.bfloat16
D = 1024
EPS = 1e-6
DEPTH = 4
MIXER_OF_LAYER = (0, 1, 2, 0)
S5_G, S5_H, S5_P = 64, 16, 64
S5_NB = 4
S5_BH = S5_H * 16
S5_BP = S5_P * 16
NSTATE = S5_G * S5_P
SG_HEADS, SG_CHUNK = 8, 128
ADAM_LR, ADAM_B1, ADAM_B2, ADAM_EPS, ADAM_WD, ADAM_STEP = 0.001, 0.9, 0.999, 1e-08, 0.01, 10
N_DEV = 8
MESH = pl.DeviceIdType.MESH
LANES = 1024
R_SH1, R_SC1, R_G1, R_SH2, R_SC2, R_G2, R_N1, R_N2 = range(8)


def _dot(a, b):
    return jnp.dot(a, b, preferred_element_type=F32)


def _dot_nt(a, b):
    return lax.dot_general(a, b, (((1,), (1,)), ((), ())), preferred_element_type=F32)


def _dot_tn(a, b):
    return lax.dot_general(a, b, (((0,), (0,)), ((), ())), preferred_element_type=F32)


def _bf(x):
    return x.astype(BF16)


def _sum0(x):
    return jnp.sum(x, axis=0, keepdims=True)


def _params(n_axes, vmem_mb=48):
    return pltpu.CompilerParams(dimension_semantics=("arbitrary",) * n_axes, vmem_limit_bytes=vmem_mb << 20)


def _rows(tm, cols, nt=None):
    if nt is None:
        return pl.BlockSpec((tm, cols), lambda i: (i, 0))
    return pl.BlockSpec((tm, cols), lambda i: (nt - 1 - i, 0))


def _whole(shape):
    nd = len(shape)
    return pl.BlockSpec(shape, lambda *_: (0,) * nd)


def _layer_w(r, c, layer):
    return pl.BlockSpec((None, r, c), lambda *_: (layer, 0, 0), pipeline_mode=pl.Buffered(1))


def _const_w(shape):
    nd = len(shape)
    return pl.BlockSpec(shape, lambda *_: (0,) * nd, pipeline_mode=pl.Buffered(1))


def _call_after(after, body, n_in, *, in_specs, **kw):
    if after is None:
        return pl.pallas_call(body, in_specs=in_specs, **kw), ()

    def body_after(*refs):
        return body(*refs[:n_in], *refs[n_in + 1:])

    return pl.pallas_call(body_after, in_specs=list(in_specs) + [pl.BlockSpec(memory_space=pl.ANY)], **kw), (after,)


def _norm_mod(x, ng, sc, sh):
    r = lax.rsqrt(jnp.mean(x * x, axis=-1, keepdims=True) + EPS)
    xn = x * r
    return (xn * ng) * (1.0 + sc) + sh, xn, r


def _norm_mod_bwd(dh, xn, r, ng, sc):
    dxn = dh * (ng * (1.0 + sc))
    return r * (dxn - xn * jnp.mean(dxn * xn, axis=-1, keepdims=True))


def _shift_down(z, prev8, k):
    row = lax.broadcasted_iota(jnp.int32, z.shape, 0)
    if k == 1:
        return jnp.where(row >= 1, pltpu.roll(z, 1, 0), prev8[7:8])
    return jnp.where(row >= 2, pltpu.roll(z, 2, 0), jnp.where(row == 0, prev8[6:7], prev8[7:8]))


def _shift_up(z, next8, k):
    n = z.shape[0]
    row = lax.broadcasted_iota(jnp.int32, z.shape, 0)
    if k == 1:
        return jnp.where(row <= n - 2, pltpu.roll(z, n - 1, 0), next8[0:1])
    return jnp.where(row <= n - 3, pltpu.roll(z, n - 2, 0), jnp.where(row == n - 2, next8[0:1], next8[1:2]))


def _place():
    x, y, c = lax.axis_index("x"), lax.axis_index("y"), lax.axis_index("c")
    chips = [(1 - x, y), (x, 1 - y), (1 - x, 1 - y)]
    return x, y, c, chips


def _allgather_small(x_shard, name, after=None):
    m_per, n = x_shard.shape

    def body(x_ref, out_ref, send_sems, recv_sems, local_sem):
        x, y, c, chips = _place()
        me, sibling = (x, y, c), (x, y, 1 - c)

        def rows(px, py, pc):
            return out_ref.at[pl.ds((4 * px + 2 * py + pc) * m_per, m_per), :]

        def copy(k, block, to, src=None):
            return pltpu.make_async_remote_copy(
                src_ref=rows(*block) if src is None else src, dst_ref=rows(*block),
                send_sem=send_sems.at[k], recv_sem=recv_sems.at[k], device_id=to, device_id_type=MESH)

        mine = pltpu.make_async_copy(x_ref, rows(*me), local_sem)
        mine.start()
        first = [copy(0, me, sibling, src=x_ref)]
        first += [copy(1 + j, me, (*chip, c), src=x_ref) for j, chip in enumerate(chips)]
        for cp in first:
            cp.start()
        passed = [copy(4 + j, (*chip, c), sibling) for j, chip in enumerate(chips)]
        for j, chip in enumerate(chips):
            copy(1 + j, (*chip, c), me).wait_recv()
            passed[j].start()
        copy(0, sibling, me).wait_recv()
        for j, chip in enumerate(chips):
            copy(4 + j, (*chip, 1 - c), me).wait_recv()
        for cp in first + passed:
            cp.wait_send()
        mine.wait()

    call, tail = _call_after(
        after, body, 1, name=name, out_shape=jax.ShapeDtypeStruct((N_DEV * m_per, n), F32),
        in_specs=[pl.BlockSpec(memory_space=pltpu.VMEM)], out_specs=pl.BlockSpec(memory_space=pltpu.VMEM),
        scratch_shapes=[pltpu.SemaphoreType.DMA((7,)), pltpu.SemaphoreType.DMA((7,)), pltpu.SemaphoreType.DMA],
        compiler_params=pltpu.CompilerParams(vmem_limit_bytes=48 << 20),
    )
    return call(x_shard, *tail)


def _reduce_pair(x_part, name, after=None):
    m, n = x_part.shape
    h = m // 2

    def body(x_ref, slots_ref, sib_buf, send_sem, recv_sem):
        x, y, c, _ = _place()
        swap = pltpu.make_async_remote_copy(src_ref=x_ref, dst_ref=sib_buf, send_sem=send_sem, recv_sem=recv_sem,
                                            device_id=(x, y, 1 - c), device_id_type=MESH)
        swap.start()
        swap.wait()
        mine = pl.ds(pl.multiple_of(c * h, 8), h)
        slots_ref[pl.ds(2 * x + y, 1)] = (x_ref[mine, :] + sib_buf[mine, :])[None]

    call, tail = _call_after(
        after, body, 1, name=name, out_shape=jax.ShapeDtypeStruct((4, h, n), F32),
        in_specs=[pl.BlockSpec(memory_space=pltpu.VMEM)], out_specs=pl.BlockSpec(memory_space=pltpu.VMEM),
        scratch_shapes=[pltpu.VMEM((m, n), F32), pltpu.SemaphoreType.DMA, pltpu.SemaphoreType.DMA],
        compiler_params=pltpu.CompilerParams(vmem_limit_bytes=48 << 20),
    )
    return call(x_part, *tail)


def _reduce_cross_start(slots, name):
    def body(slots_ref, send_sems, recv_sems, thru, token):
        x, y, c, chips = _place()
        mine = slots_ref.at[pl.ds(2 * x + y, 1)]
        for j, chip in enumerate(chips):
            pltpu.make_async_remote_copy(src_ref=mine, dst_ref=mine, send_sem=send_sems.at[j], recv_sem=recv_sems.at[j],
                                         device_id=(*chip, c), device_id_type=MESH).start()
        token[...] = jnp.zeros_like(token)

    res = pl.pallas_call(
        body, name=name,
        out_shape=(pltpu.SemaphoreType.DMA((3,)), pltpu.SemaphoreType.DMA((3,)), pltpu.HBM(slots.shape, F32),
                   jax.ShapeDtypeStruct((8, 128), F32)),
        in_specs=[HBM_SPEC], out_specs=(SEM_SPEC, SEM_SPEC, HBM_SPEC, pl.BlockSpec(memory_space=pltpu.VMEM)),
        input_output_aliases={0: 2}, compiler_params=SPLIT_COPY_PARAMS,
    )(*_in_hbm([slots]))
    return res


def _reduce_cross_wait(send_sems, recv_sems, slots, name, after):
    def body(slots_ref, s_sems, r_sems, after_ref, thru):
        x, y, c, chips = _place()
        for j, chip in enumerate(chips):
            theirs = slots_ref.at[pl.ds(2 * chip[0] + chip[1], 1)]
            cp = pltpu.make_async_remote_copy(src_ref=theirs, dst_ref=theirs, send_sem=s_sems.at[j], recv_sem=r_sems.at[j],
                                              device_id=(x, y, c), device_id_type=MESH)
            cp.wait_send()
            cp.wait_recv()

    return pl.pallas_call(
        body, name=name, out_shape=pltpu.HBM(slots.shape, F32),
        in_specs=[HBM_SPEC, SEM_SPEC, SEM_SPEC, ANY_SPEC], out_specs=HBM_SPEC,
        input_output_aliases={0: 0}, compiler_params=SPLIT_COPY_PARAMS,
    )(slots, send_sems, recv_sems, after)


def _reduce_finish(slots, name):
    _, h, n = slots.shape

    def body(slots_ref, out_ref, send_sem, recv_sem):
        x, y, c, _ = _place()
        mine = pl.ds(pl.multiple_of(c * h, 8), h)
        theirs = pl.ds(pl.multiple_of((1 - c) * h, 8), h)
        out_ref[mine, :] = ((slots_ref[0] + slots_ref[1]) + slots_ref[2]) + slots_ref[3]
        give = pltpu.make_async_remote_copy(src_ref=out_ref.at[mine, :], dst_ref=out_ref.at[mine, :], send_sem=send_sem,
                                            recv_sem=recv_sem, device_id=(x, y, 1 - c), device_id_type=MESH)
        give.start()
        give.wait_send()
        pltpu.make_async_remote_copy(src_ref=out_ref.at[theirs, :], dst_ref=out_ref.at[theirs, :], send_sem=send_sem,
                                     recv_sem=recv_sem, device_id=(x, y, c), device_id_type=MESH).wait_recv()

    return pl.pallas_call(
        body, name=name, out_shape=jax.ShapeDtypeStruct((2 * h, n), F32),
        in_specs=[pl.BlockSpec(memory_space=pltpu.VMEM)], out_specs=pl.BlockSpec(memory_space=pltpu.VMEM),
        scratch_shapes=[pltpu.SemaphoreType.DMA, pltpu.SemaphoreType.DMA],
        compiler_params=pltpu.CompilerParams(vmem_limit_bytes=48 << 20),
    )(slots)


def _shard_region(ref, full_shape, axis, chip_k, half=None):
    _, r, c = full_shape
    if axis == 1:
        rs = r // 4
        if half is None:
            return ref.at[:, pl.ds(pl.multiple_of(chip_k * rs, 128), rs), :]
        return ref.at[:, pl.ds(pl.multiple_of(chip_k * rs + half * (rs // 2), 128), rs // 2), :]
    cs = c // 4
    if half is None:
        return ref.at[:, :, pl.ds(pl.multiple_of(chip_k * cs, 128), cs)]
    return ref.at[:, pl.ds(pl.multiple_of(half * (r // 2), 128), r // 2), pl.ds(pl.multiple_of(chip_k * cs, 128), cs)]


HBM_SPEC = pl.BlockSpec(memory_space=pltpu.HBM)
SEM_SPEC = pl.BlockSpec(memory_space=pltpu.SEMAPHORE)
ANY_SPEC = pl.BlockSpec(memory_space=pl.ANY)
SPLIT_COPY_PARAMS = pltpu.CompilerParams(has_side_effects=pltpu.SideEffectType.DATAFLOW_SIDE_EFFECTING)


def _in_hbm(arrs):
    return [pltpu.with_memory_space_constraint(a, pltpu.HBM) for a in arrs]


def _cast_place(w_stack, li, axis, chip, name):
    _, r, c = w_stack.shape
    full = (1, 4 * r, c) if axis == 1 else (1, r, 4 * c)
    tr = min(r, 256)
    if axis == 1:
        out_spec = pl.BlockSpec((None, tr, c), lambda i, k: (0, k[0] * (r // tr) + i, 0))
    else:
        out_spec = pl.BlockSpec((None, tr, c), lambda i, k: (0, i, k[0]))

    def body(k_ref, w_ref, o_ref):
        o_ref[...] = _bf(w_ref[...])

    return pl.pallas_call(
        body, name=name,
        grid_spec=pltpu.PrefetchScalarGridSpec(
            num_scalar_prefetch=1, grid=(r // tr,),
            in_specs=[pl.BlockSpec((None, tr, c), lambda i, k: (li, i, 0))], out_specs=out_spec),
        out_shape=jax.ShapeDtypeStruct(full, BF16),
        compiler_params=_params(1),
    )(chip, w_stack)


def _gather_start(lands, axes, name, after):
    n_arr = len(lands)
    fulls = [tuple(l.shape) for l in lands]

    def body(*refs):
        land = refs[:n_arr]
        send_sems, recv_sems = refs[n_arr + 1:n_arr + 3]
        token = refs[-1]
        x, y, c, chips = _place()
        k_me = 2 * x + y
        for a in range(n_arr):
            mine = _shard_region(land[a], fulls[a], axes[a], k_me, c)
            for j, chip in enumerate(chips):
                pltpu.make_async_remote_copy(
                    src_ref=mine, dst_ref=mine, send_sem=send_sems.at[a * 3 + j], recv_sem=recv_sems.at[a * 3 + j],
                    device_id=(*chip, c), device_id_type=MESH).start()
        token[...] = jnp.zeros_like(token)

    res = pl.pallas_call(
        body, name=name,
        out_shape=(pltpu.SemaphoreType.DMA((3 * n_arr,)), pltpu.SemaphoreType.DMA((3 * n_arr,)),
                   *[pltpu.HBM(f, BF16) for f in fulls], jax.ShapeDtypeStruct((8, 128), F32)),
        in_specs=[HBM_SPEC] * n_arr + [ANY_SPEC],
        out_specs=(SEM_SPEC, SEM_SPEC, *[HBM_SPEC] * n_arr, pl.BlockSpec(memory_space=pltpu.VMEM)),
        input_output_aliases={a: 2 + a for a in range(n_arr)},
        compiler_params=SPLIT_COPY_PARAMS,
    )(*_in_hbm(lands), after)
    return res[0], res[1], list(res[2:2 + n_arr]), res[-1]


def _gather_wait(send_sems, recv_sems, lands, axes, name, after):
    n_arr = len(lands)
    fulls = [tuple(l.shape) for l in lands]

    def body(*refs):
        land = refs[:n_arr]
        s_sems, r_sems = refs[n_arr:n_arr + 2]
        x, y, c, chips = _place()
        for a in range(n_arr):
            for j, chip in enumerate(chips):
                k_j = 2 * chip[0] + chip[1]
                got = _shard_region(land[a], fulls[a], axes[a], k_j, c)
                cp = pltpu.make_async_remote_copy(
                    src_ref=got, dst_ref=got, send_sem=s_sems.at[a * 3 + j], recv_sem=r_sems.at[a * 3 + j],
                    device_id=(x, y, c), device_id_type=MESH)
                cp.wait_send()
                cp.wait_recv()

    res = pl.pallas_call(
        body, name=name,
        out_shape=tuple(pltpu.HBM(f, BF16) for f in fulls),
        in_specs=[HBM_SPEC] * n_arr + [SEM_SPEC, SEM_SPEC, ANY_SPEC],
        out_specs=tuple([HBM_SPEC] * n_arr),
        input_output_aliases={a: a for a in range(n_arr)},
        compiler_params=SPLIT_COPY_PARAMS,
    )(*lands, send_sems, recv_sems, after)
    return list(res)


def _gather_share(lands, axes, name):
    n_arr = len(lands)
    fulls = [tuple(l.shape) for l in lands]

    def body(*refs):
        land_in, land = refs[:n_arr], refs[n_arr:2 * n_arr]
        send_sems, recv_sems = refs[2 * n_arr:]
        x, y, c, chips = _place()
        copies = []
        for a in range(n_arr):
            for j, chip in enumerate(chips):
                k_j = 2 * chip[0] + chip[1]
                cp = pltpu.make_async_remote_copy(
                    src_ref=_shard_region(land_in[a], fulls[a], axes[a], k_j, c),
                    dst_ref=_shard_region(land[a], fulls[a], axes[a], k_j, c),
                    send_sem=send_sems.at[a * 3 + j], recv_sem=recv_sems.at[a * 3 + j],
                    device_id=(x, y, 1 - c), device_id_type=MESH)
                cp.start()
                copies.append(cp)
        for cp in copies:
            cp.wait()

    return pl.pallas_call(
        body, name=name, out_shape=[jax.ShapeDtypeStruct(f, BF16) for f in fulls],
        in_specs=[ANY_SPEC] * n_arr, out_specs=[ANY_SPEC] * n_arr,
        input_output_aliases={a: a for a in range(n_arr)},
        scratch_shapes=[pltpu.SemaphoreType.DMA((3 * n_arr,)), pltpu.SemaphoreType.DMA((3 * n_arr,))],
    )(*lands)


def _scatter_shapes(grads, axes):
    out = []
    for g, ax in zip(grads, axes):
        shp = list(g.shape)
        shp[ax] //= 4
        out.append((3,) + tuple(shp[1:]))
    return out


def _scatter_start(grads, axes, name, after):
    n_arr = len(grads)
    shapes = _scatter_shapes(grads, axes)
    lands = [lax.empty(s, BF16) for s in shapes]

    def body(*refs):
        ins, land = refs[:n_arr], refs[n_arr:2 * n_arr]
        send_sems, recv_sems = refs[2 * n_arr + 1:2 * n_arr + 3]
        token = refs[-1]
        x, y, c, chips = _place()
        for a in range(n_arr):
            for j, chip in enumerate(chips):
                k_j = 2 * chip[0] + chip[1]
                pltpu.make_async_remote_copy(
                    src_ref=_shard_region(ins[a], grads[a].shape, axes[a], k_j), dst_ref=land[a].at[pl.ds(j, 1)],
                    send_sem=send_sems.at[a * 3 + j], recv_sem=recv_sems.at[a * 3 + j],
                    device_id=(*chip, c), device_id_type=MESH).start()
        token[...] = jnp.zeros_like(token)

    res = pl.pallas_call(
        body, name=name,
        out_shape=(pltpu.SemaphoreType.DMA((3 * n_arr,)), pltpu.SemaphoreType.DMA((3 * n_arr,)),
                   *[pltpu.HBM(g.shape, BF16) for g in grads], *[pltpu.HBM(s, BF16) for s in shapes],
                   jax.ShapeDtypeStruct((8, 128), F32)),
        in_specs=[HBM_SPEC] * (2 * n_arr) + [ANY_SPEC],
        out_specs=(SEM_SPEC, SEM_SPEC, *[HBM_SPEC] * (2 * n_arr), pl.BlockSpec(memory_space=pltpu.VMEM)),
        input_output_aliases={a: 2 + a for a in range(2 * n_arr)},
        compiler_params=SPLIT_COPY_PARAMS,
    )(*_in_hbm(grads), *_in_hbm(lands), after)
    return res[0], res[1], list(res[2:2 + n_arr]), list(res[2 + n_arr:2 + 2 * n_arr]), res[-1]


def _scatter_wait(send_sems, recv_sems, grads, lands, axes, name, after):
    n_arr = len(grads)

    def body(*refs):
        ins, land = refs[:n_arr], refs[n_arr:2 * n_arr]
        s_sems, r_sems = refs[2 * n_arr:2 * n_arr + 2]
        x, y, c, chips = _place()
        for a in range(n_arr):
            for j, chip in enumerate(chips):
                k_j = 2 * chip[0] + chip[1]
                cp = pltpu.make_async_remote_copy(
                    src_ref=_shard_region(ins[a], grads[a].shape, axes[a], k_j), dst_ref=land[a].at[pl.ds(j, 1)],
                    send_sem=s_sems.at[a * 3 + j], recv_sem=r_sems.at[a * 3 + j],
                    device_id=(x, y, c), device_id_type=MESH)
                cp.wait_send()
                cp.wait_recv()

    res = pl.pallas_call(
        body, name=name,
        out_shape=(*[pltpu.HBM(g.shape, BF16) for g in grads], *[pltpu.HBM(l.shape, BF16) for l in lands]),
        in_specs=[HBM_SPEC] * (2 * n_arr) + [SEM_SPEC, SEM_SPEC, ANY_SPEC],
        out_specs=tuple([HBM_SPEC] * (2 * n_arr)),
        input_output_aliases={a: a for a in range(2 * n_arr)},
        compiler_params=SPLIT_COPY_PARAMS,
    )(*grads, *lands, send_sems, recv_sems, after)
    return list(res[:n_arr]), list(res[n_arr:])


def _swap_start(arrs, name, after):
    n_arr = len(arrs)
    lands = [lax.empty(a.shape, a.dtype) for a in arrs]

    def body(*refs):
        ins, land = refs[:n_arr], refs[n_arr:2 * n_arr]
        send_sems, recv_sems = refs[2 * n_arr + 1:2 * n_arr + 3]
        token = refs[-1]
        x, y, c, _ = _place()
        for a in range(n_arr):
            pltpu.make_async_remote_copy(
                src_ref=ins[a], dst_ref=land[a], send_sem=send_sems.at[a], recv_sem=recv_sems.at[a],
                device_id=(x, y, 1 - c), device_id_type=MESH).start()
        token[...] = jnp.zeros_like(token)

    res = pl.pallas_call(
        body, name=name,
        out_shape=(pltpu.SemaphoreType.DMA((n_arr,)), pltpu.SemaphoreType.DMA((n_arr,)),
                   *[pltpu.HBM(a.shape, a.dtype) for a in arrs], *[pltpu.HBM(a.shape, a.dtype) for a in arrs],
                   jax.ShapeDtypeStruct((8, 128), F32)),
        in_specs=[HBM_SPEC] * (2 * n_arr) + [ANY_SPEC],
        out_specs=(SEM_SPEC, SEM_SPEC, *[HBM_SPEC] * (2 * n_arr), pl.BlockSpec(memory_space=pltpu.VMEM)),
        input_output_aliases={a: 2 + a for a in range(2 * n_arr)},
        compiler_params=SPLIT_COPY_PARAMS,
    )(*_in_hbm(arrs), *_in_hbm(lands), after)
    return res[0], res[1], list(res[2:2 + n_arr]), list(res[2 + n_arr:2 + 2 * n_arr]), res[-1]


def _swap_wait(send_sems, recv_sems, arrs, lands, name, after):
    n_arr = len(arrs)

    def body(*refs):
        ins, land = refs[:n_arr], refs[n_arr:2 * n_arr]
        s_sems, r_sems = refs[2 * n_arr:2 * n_arr + 2]
        x, y, c, _ = _place()
        for a in range(n_arr):
            cp = pltpu.make_async_remote_copy(
                src_ref=ins[a], dst_ref=land[a], send_sem=s_sems.at[a], recv_sem=r_sems.at[a],
                device_id=(x, y, c), device_id_type=MESH)
            cp.wait_send()
            cp.wait_recv()

    res = pl.pallas_call(
        body, name=name,
        out_shape=(*[pltpu.HBM(a.shape, a.dtype) for a in arrs], *[pltpu.HBM(a.shape, a.dtype) for a in arrs]),
        in_specs=[HBM_SPEC] * (2 * n_arr) + [SEM_SPEC, SEM_SPEC, ANY_SPEC],
        out_specs=tuple([HBM_SPEC] * (2 * n_arr)),
        input_output_aliases={a: a for a in range(2 * n_arr)},
        compiler_params=SPLIT_COPY_PARAMS,
    )(*arrs, *lands, send_sems, recv_sems, after)
    return list(res[:n_arr]), list(res[n_arr:])


def _swap_with_sibling(arrs, name):
    n_arr = len(arrs)

    def body(*refs):
        ins, outs = refs[:n_arr], refs[n_arr:2 * n_arr]
        send_sems, recv_sems = refs[2 * n_arr:]
        x, y, c, _ = _place()
        copies = []
        for a in range(n_arr):
            cp = pltpu.make_async_remote_copy(
                src_ref=ins[a], dst_ref=outs[a], send_sem=send_sems.at[a], recv_sem=recv_sems.at[a],
                device_id=(x, y, 1 - c), device_id_type=MESH)
            cp.start()
            copies.append(cp)
        for cp in copies:
            cp.wait()

    any_spec = pl.BlockSpec(memory_space=pl.ANY)
    return pl.pallas_call(
        body, name=name, out_shape=[jax.ShapeDtypeStruct(a.shape, a.dtype) for a in arrs],
        in_specs=[any_spec] * n_arr, out_specs=[any_spec] * n_arr,
        scratch_shapes=[pltpu.SemaphoreType.DMA((n_arr,)), pltpu.SemaphoreType.DMA((n_arr,))],
    )(*arrs)


def _mm_tn(a, b, name, out_dtype=BF16):
    L, m = a.shape
    n = b.shape[1]
    bm, bn, bk = min(m, 1024), min(n, 1024), min(L, 2048)
    nk = L // bk

    def body(a_ref, b_ref, o_ref, acc):
        k = pl.program_id(2)

        @pl.when(k == 0)
        def _():
            acc[...] = jnp.zeros_like(acc)

        acc[...] += _dot_tn(_bf(a_ref[...]), _bf(b_ref[...]))

        @pl.when(k == nk - 1)
        def _():
            o_ref[...] = acc[...].astype(out_dtype)

    return pl.pallas_call(
        body, name=name, grid=(m // bm, n // bn, nk),
        in_specs=[pl.BlockSpec((bk, bm), lambda i, j, k: (k, i)), pl.BlockSpec((bk, bn), lambda i, j, k: (k, j))],
        out_specs=pl.BlockSpec((bm, bn), lambda i, j, k: (i, j)),
        out_shape=jax.ShapeDtypeStruct((m, n), out_dtype),
        scratch_shapes=[pltpu.VMEM((bm, bn), F32)],
        compiler_params=_params(3),
    )(a, b)


def _mm_tn_blocks(a, b, wa, wb, name, after=None):
    L = a.shape[0]
    nb = a.shape[1] // wa
    bk = min(L, 1024)
    nk = L // bk

    def body(a_ref, b_ref, o_ref):
        @pl.when(pl.program_id(1) == 0)
        def _():
            o_ref[...] = jnp.zeros_like(o_ref)

        o_ref[...] += _dot_tn(_bf(a_ref[...]), _bf(b_ref[...]))

    call, tail = _call_after(
        after, body, 2, name=name, grid=(nb, nk),
        in_specs=[pl.BlockSpec((bk, wa), lambda j, k: (k, j)), pl.BlockSpec((bk, wb), lambda j, k: (k, j))],
        out_specs=pl.BlockSpec((None, wa, wb), lambda j, k: (j, 0, 0)),
        out_shape=jax.ShapeDtypeStruct((nb, wa, wb), F32),
        compiler_params=_params(2),
    )
    return call(a, b, *tail)


def _sum_parts(parts, own, axis, chip, name):
    _, r, c = parts.shape
    tr = min(r, 256)
    if axis == 1:
        own_spec = pl.BlockSpec((None, tr, c), lambda i, k: (0, k[0] * (r // tr) + i, 0))
    else:
        own_spec = pl.BlockSpec((None, tr, c), lambda i, k: (0, i, k[0]))

    def body(k_ref, p_ref, g_ref, o_ref):
        p = p_ref[...].astype(F32)
        o_ref[...] = ((p[0] + p[1]) + p[2]) + g_ref[...].astype(F32)

    return pl.pallas_call(
        body, name=name,
        grid_spec=pltpu.PrefetchScalarGridSpec(
            num_scalar_prefetch=1, grid=(r // tr,),
            in_specs=[pl.BlockSpec((3, tr, c), lambda i, k: (0, i, 0)), own_spec],
            out_specs=pl.BlockSpec((tr, c), lambda i, k: (i, 0))),
        out_shape=jax.ShapeDtypeStruct((r, c), F32),
        compiler_params=_params(1),
    )(chip, parts, own)


def _adamw(w, g_parts, m, v, name):
    n_g = len(g_parts)
    if w.ndim == 2:
        r, c = w.shape
        tr = r
        for cand in (512, 256, 128, 64, 32, 16, 8):
            if r % cand == 0 and cand * c * 4 <= (2 << 20):
                tr = cand
                break
        spec = pl.BlockSpec((tr, c), lambda i: (i, 0))
        tiling = dict(grid=(r // tr,), in_specs=[spec] * (3 + n_g), out_specs=[spec] * 4, compiler_params=_params(1))
    else:
        tiling = dict(compiler_params=pltpu.CompilerParams(vmem_limit_bytes=48 << 20))

    def body(*refs):
        w_ref, g_refs, m_ref, v_ref = refs[0], refs[1:1 + n_g], refs[1 + n_g], refs[2 + n_g]
        g = g_refs[0][...]
        for gr in g_refs[1:]:
            g = g + gr[...]
        _adamw_update(g, w_ref, m_ref, v_ref, *refs[3 + n_g:])

    return pl.pallas_call(body, name=name, out_shape=[jax.ShapeDtypeStruct(w.shape, F32)] * 4, **tiling)(w, *g_parts, m, v)


def _adamw_update(g, w_ref, m_ref, v_ref, g_out, d_out, m_out, v_out):
    m_new = ADAM_B1 * m_ref[...] + (1.0 - ADAM_B1) * g
    v_new = ADAM_B2 * v_ref[...] + (1.0 - ADAM_B2) * (g * g)
    m_hat = m_new * (1.0 / (1.0 - ADAM_B1 ** ADAM_STEP))
    v_hat = v_new * (1.0 / (1.0 - ADAM_B2 ** ADAM_STEP))
    g_out[...] = g
    d_out[...] = -ADAM_LR * (m_hat / (jnp.sqrt(v_hat) + ADAM_EPS) + ADAM_WD * w_ref[...])
    m_out[...] = m_new
    v_out[...] = v_new


def _adamw_many(ws, gs, ms, vs, name):
    n = len(ws)

    def body(*refs):
        for k in range(n):
            _adamw_update(refs[n + k][...], refs[k], refs[2 * n + k], refs[3 * n + k],
                          refs[4 * n + k], refs[5 * n + k], refs[6 * n + k], refs[7 * n + k])

    outs = pl.pallas_call(body, name=name, out_shape=[jax.ShapeDtypeStruct(t.shape, F32) for t in ws] * 4,
                          compiler_params=pltpu.CompilerParams(vmem_limit_bytes=56 << 20))(*ws, *gs, *ms, *vs)
    return [outs[part * n:(part + 1) * n] for part in range(4)]


def _adamw_layers(w, q_mine, q_sib, m, v, name, after=None):
    n, r, c = w.shape
    tr = r
    for cand in (512, 256, 128, 64, 32, 16, 8):
        if r % cand == 0 and cand * c * 4 <= (1 << 20):
            tr = cand
            break

    def body(*refs):
        w_ref, qm, qs, m_ref, v_ref = refs[0], refs[1:1 + n], refs[1 + n:1 + 2 * n], refs[1 + 2 * n], refs[2 + 2 * n]
        layer = pl.program_id(0)
        g = qm[0][...] + qs[0][...]
        for k in range(1, n):
            g = jnp.where(layer == k, qm[k][...] + qs[k][...], g)
        _adamw_update(g, w_ref, m_ref, v_ref, *refs[3 + 2 * n:])

    stacked = pl.BlockSpec((None, tr, c), lambda l, i: (l, i, 0))
    per_layer = [pl.BlockSpec((tr, c), lambda l, i, k=k: (jnp.where(l == k, i, 0), 0)) for k in range(n)]
    call, tail = _call_after(
        after, body, 3 + 2 * n, name=name, grid=(n, r // tr),
        in_specs=[stacked] + per_layer + per_layer + [stacked, stacked], out_specs=[stacked] * 4,
        out_shape=[jax.ShapeDtypeStruct(w.shape, F32)] * 4,
        compiler_params=_params(2),
    )
    return call(w, *q_mine, *q_sib, m, v, *tail)


def _ada_fwd(c16, ada_w, ada_b_cols):
    cols = ada_w.shape[2]

    def body(c_ref, w_ref, b_ref, o_ref):
        cv = c_ref[...]
        ca = _bf(cv * jax.nn.sigmoid(cv))
        o_ref[...] = _dot(ca, _bf(w_ref[...])) + b_ref[...]

    return pl.pallas_call(
        body, name="ada_fwd", grid=(DEPTH,),
        in_specs=[_whole((16, D)), pl.BlockSpec((None, D, cols), lambda i: (i, 0, 0)),
                  pl.BlockSpec((None, 1, cols), lambda i: (i, 0, 0))],
        out_specs=pl.BlockSpec((None, 16, cols), lambda i: (i, 0, 0)),
        out_shape=jax.ShapeDtypeStruct((DEPTH, 16, cols), F32),
        compiler_params=_params(1),
    )(c16, ada_w, ada_b_cols)


def _ada_bwd(c16, dmod16):
    cols = dmod16.shape[2]

    def body(c_ref, d_ref, o_ref):
        cv = c_ref[...]
        ca = _bf(cv * jax.nn.sigmoid(cv))
        o_ref[...] = _dot_tn(ca, _bf(d_ref[...]))

    return pl.pallas_call(
        body, name="ada_bwd", grid=(DEPTH,),
        in_specs=[_whole((16, D)), pl.BlockSpec((None, 16, cols), lambda i: (i, 0, 0))],
        out_specs=pl.BlockSpec((None, D, cols), lambda i: (i, 0, 0)),
        out_shape=jax.ShapeDtypeStruct((DEPTH, D, cols), F32),
        compiler_params=_params(1),
    )(c16, dmod16)


def _mod_bwd(vs_mix, vs_ffn, pv):
    def body(m_ref, f_ref, pv_ref, o_ref):
        for i in range(DEPTH):
            vm, vf, p = m_ref[i], f_ref[i], pv_ref[i]
            o_ref[i] = jnp.concatenate([
                vm[2:3], vm[1:2] * p[R_N1:R_N1 + 1], vm[0:1],
                vf[2:3], vf[1:2] * p[R_N2:R_N2 + 1], vf[0:1],
                vm[1:2] * (1.0 + p[R_SC1:R_SC1 + 1]), vf[1:2] * (1.0 + p[R_SC2:R_SC2 + 1])], axis=0)

    return pl.pallas_call(body, name="mod_bwd", out_shape=jax.ShapeDtypeStruct((DEPTH, 8, D), F32))(vs_mix, vs_ffn, pv)


def _ffn_fwd(x1, pv, w1, w2, layer, tm, after=None):
    L = x1.shape[0]
    dff = w1.shape[2]

    def body(x1_ref, pv_ref, w1_ref, w2_ref, x2_ref, h2_ref, a_ref, f_ref):
        x1v, p = x1_ref[...], pv_ref[...]
        h2, _, _ = _norm_mod(x1v, p[R_N2:R_N2 + 1], p[R_SC2:R_SC2 + 1], p[R_SH2:R_SH2 + 1])
        hb = _bf(h2)
        h2_ref[...] = hb
        a = _dot(hb, w1_ref[...])
        a_ref[...] = a
        ra = jnp.maximum(a, 0.0)
        f = _dot(_bf(ra * ra), w2_ref[...])
        f_ref[...] = f
        x2_ref[...] = x1v + p[R_G2:R_G2 + 1] * f

    call, tail = _call_after(
        after, body, 4, name=f"ffn_fwd{layer}", grid=(L // tm,),
        in_specs=[_rows(tm, D), pl.BlockSpec((None, 8, D), lambda i: (layer, 0, 0)), _layer_w(D, dff, 0), _layer_w(dff, D, 0)],
        out_specs=[_rows(tm, D), _rows(tm, D), _rows(tm, dff), _rows(tm, D)],
        out_shape=[jax.ShapeDtypeStruct((L, D), F32), jax.ShapeDtypeStruct((L, D), BF16),
                   jax.ShapeDtypeStruct((L, dff), F32), jax.ShapeDtypeStruct((L, D), F32)],
        compiler_params=_params(1, 56),
    )
    return call(x1, pv, w1, w2, *tail)


def _ffn_bwd(dx2, x1, a, f, pv, w1, w2, layer, tm, after=None):
    L = x1.shape[0]
    dff = w1.shape[2]
    extra = [] if after is None else [pl.BlockSpec(memory_space=pl.ANY)]
    extra_args = [] if after is None else [after]

    def body(dx2_ref, x1_ref, a_ref, f_ref, pv_ref, w1_ref, w2_ref, *rest):
        dx1_ref, p_ref, da_ref, df_ref, vs_ref = rest[len(extra):]

        @pl.when(pl.program_id(0) == 0)
        def _():
            vs_ref[...] = jnp.zeros_like(vs_ref)

        dx2v, p = dx2_ref[...], pv_ref[...]
        dfb = _bf(dx2v * p[R_G2:R_G2 + 1])
        df_ref[...] = dfb
        vs_ref[0:1, :] += _sum0(dx2v * f_ref[...])
        dp = _dot_nt(dfb, w2_ref[...])
        ra = jnp.maximum(a_ref[...], 0.0)
        p_ref[...] = _bf(ra * ra)
        dab = _bf(dp * (2.0 * ra))
        da_ref[...] = dab
        dh2 = _dot_nt(dab, w1_ref[...])
        _, xn, r = _norm_mod(x1_ref[...], p[R_N2:R_N2 + 1], p[R_SC2:R_SC2 + 1], p[R_SH2:R_SH2 + 1])
        dx1_ref[...] = dx2v + _norm_mod_bwd(dh2, xn, r, p[R_N2:R_N2 + 1], p[R_SC2:R_SC2 + 1])
        vs_ref[1:2, :] += _sum0(dh2 * xn)
        vs_ref[2:3, :] += _sum0(dh2)

    return pl.pallas_call(
        body, name=f"ffn_bwd{layer}", grid=(L // tm,),
        in_specs=[_rows(tm, D), _rows(tm, D), _rows(tm, dff), _rows(tm, D),
                  pl.BlockSpec((None, 8, D), lambda i: (layer, 0, 0)), _layer_w(D, dff, 0), _layer_w(dff, D, 0)] + extra,
        out_specs=[_rows(tm, D), _rows(tm, dff), _rows(tm, dff), _rows(tm, D), _whole((8, D))],
        out_shape=[jax.ShapeDtypeStruct((L, D), F32), jax.ShapeDtypeStruct((L, dff), BF16),
                   jax.ShapeDtypeStruct((L, dff), BF16), jax.ShapeDtypeStruct((L, D), BF16),
                   jax.ShapeDtypeStruct((8, D), F32)],
        compiler_params=_params(1, 56),
    )(dx2, x1, a, f, pv, w1, w2, *extra_args)


def _conv_fwd(x, pv, w_in, w_out, cw, layer, j, tm, after=None):
    L = x.shape[0]

    def body(x_ref, pv_ref, win_ref, wout_ref, cw_ref, x1_ref, h_ref, bcx_ref, conv_ref, q_ref, y_ref, carry):
        @pl.when(pl.program_id(0) == 0)
        def _():
            carry[...] = jnp.zeros_like(carry)

        xv, p, cwv = x_ref[...], pv_ref[...], cw_ref[...]
        h, _, _ = _norm_mod(xv, p[R_N1:R_N1 + 1], p[R_SC1:R_SC1 + 1], p[R_SH1:R_SH1 + 1])
        hb = _bf(h)
        h_ref[...] = hb
        bcx = _dot(hb, win_ref[...])
        bcx_ref[...] = bcx
        z = bcx[:, D:2 * D] * bcx[:, 2 * D:]
        prev8 = carry[...]
        conv = cwv[0:1] * _shift_down(z, prev8, 2) + cwv[1:2] * _shift_down(z, prev8, 1) + cwv[2:3] * z + cwv[3:4]
        conv_ref[...] = conv
        qb = _bf(bcx[:, :D] * conv)
        q_ref[...] = qb
        y = _dot(qb, wout_ref[...])
        y_ref[...] = y
        x1_ref[...] = xv + p[R_G1:R_G1 + 1] * y
        carry[...] = z[tm - 8:tm]

    call, tail = _call_after(
        after, body, 5, name=f"conv_fwd{layer}", grid=(L // tm,),
        in_specs=[_rows(tm, D), pl.BlockSpec((None, 8, D), lambda i: (layer, 0, 0)), _layer_w(D, 3 * D, 0), _layer_w(D, D, 0),
                  pl.BlockSpec((None, 8, D), lambda i: (j, 0, 0))],
        out_specs=[_rows(tm, D), _rows(tm, D), _rows(tm, 3 * D), _rows(tm, D), _rows(tm, D), _rows(tm, D)],
        out_shape=[jax.ShapeDtypeStruct((L, D), F32), jax.ShapeDtypeStruct((L, D), BF16), jax.ShapeDtypeStruct((L, 3 * D), F32),
                   jax.ShapeDtypeStruct((L, D), F32), jax.ShapeDtypeStruct((L, D), BF16), jax.ShapeDtypeStruct((L, D), F32)],
        scratch_shapes=[pltpu.VMEM((8, D), F32)],
        compiler_params=_params(1, 56),
    )
    return call(x, pv, w_in, w_out, cw, *tail)


def _conv_bwd(dx1, x, y, bcx, conv, pv, w_in, w_out, cw, layer, j, tm, after=None):
    L = x.shape[0]
    nt = L // tm

    def body(dx1_ref, x_ref, y_ref, bcx_ref, conv_ref, halo_ref, pv_ref, win_ref, wout_ref, cw_ref,
             dx_ref, dbcx_ref, dy_ref, vs_ref, carry):
        gi = pl.program_id(0)
        tile = nt - 1 - gi

        @pl.when(gi == 0)
        def _():
            vs_ref[...] = jnp.zeros_like(vs_ref)
            carry[...] = jnp.zeros_like(carry)

        dx1v, p, cwv = dx1_ref[...], pv_ref[...], cw_ref[...]
        dyb = _bf(dx1v * p[R_G1:R_G1 + 1])
        dy_ref[...] = dyb
        vs_ref[0:1, :] += _sum0(dx1v * y_ref[...])
        dq = _dot_nt(dyb, wout_ref[...])
        bcx = bcx_ref[...]
        b, cg, xh = bcx[:, :D], bcx[:, D:2 * D], bcx[:, 2 * D:]
        db = dq * conv_ref[...]
        dc = dq * b
        z = cg * xh
        halo = halo_ref[...]
        zprev = jnp.where(tile > 0, halo[:, D:2 * D] * halo[:, 2 * D:], 0.0)
        vs_ref[3:4, :] += _sum0(dc * _shift_down(z, zprev, 2))
        vs_ref[4:5, :] += _sum0(dc * _shift_down(z, zprev, 1))
        vs_ref[5:6, :] += _sum0(dc * z)
        vs_ref[6:7, :] += _sum0(dc)
        next8 = carry[...]
        dz = cwv[2:3] * dc + cwv[1:2] * _shift_up(dc, next8, 1) + cwv[0:1] * _shift_up(dc, next8, 2)
        dbb, dcgb, dxhb = _bf(db), _bf(dz * xh), _bf(dz * cg)
        dbcx_ref[:, 0:D] = dbb
        dbcx_ref[:, D:2 * D] = dcgb
        dbcx_ref[:, 2 * D:3 * D] = dxhb
        dh = (_dot_nt(dbb, win_ref[:, 0:D]) + _dot_nt(dcgb, win_ref[:, D:2 * D])) + _dot_nt(dxhb, win_ref[:, 2 * D:3 * D])
        _, xn, r = _norm_mod(x_ref[...], p[R_N1:R_N1 + 1], p[R_SC1:R_SC1 + 1], p[R_SH1:R_SH1 + 1])
        dx_ref[...] = dx1v + _norm_mod_bwd(dh, xn, r, p[R_N1:R_N1 + 1], p[R_SC1:R_SC1 + 1])
        vs_ref[1:2, :] += _sum0(dh * xn)
        vs_ref[2:3, :] += _sum0(dh)
        carry[...] = dc[0:8]

    halo_spec = pl.BlockSpec((8, 3 * D), lambda i: (jnp.maximum((nt - 1 - i) * (tm // 8) - 1, 0), 0))
    call, tail = _call_after(
        after, body, 10, name=f"conv_bwd{layer}", grid=(nt,),
        in_specs=[_rows(tm, D, nt), _rows(tm, D, nt), _rows(tm, D, nt), _rows(tm, 3 * D, nt), _rows(tm, D, nt), halo_spec,
                  pl.BlockSpec((None, 8, D), lambda i: (layer, 0, 0)), _layer_w(D, 3 * D, 0), _layer_w(D, D, 0),
                  pl.BlockSpec((None, 8, D), lambda i: (j, 0, 0))],
        out_specs=[_rows(tm, D, nt), _rows(tm, 3 * D, nt), _rows(tm, D, nt), _whole((8, D))],
        out_shape=[jax.ShapeDtypeStruct((L, D), F32), jax.ShapeDtypeStruct((L, 3 * D), BF16),
                   jax.ShapeDtypeStruct((L, D), BF16), jax.ShapeDtypeStruct((8, D), F32)],
        scratch_shapes=[pltpu.VMEM((8, D), F32)],
        compiler_params=_params(1, 56),
    )
    return call(dx1, x, y, bcx, conv, bcx, pv, w_in, w_out, cw, *tail)


def _s5_discretize(a_re, a_im, log_dt, bt_re, bt_im):
    dt = jnp.exp(log_dt)
    mag = jnp.exp(a_re * dt)
    abar_re = mag * jnp.cos(a_im * dt)
    abar_im = mag * jnp.sin(a_im * dt)
    den = a_re * a_re + a_im * a_im
    nr = abar_re - 1.0
    ni = abar_im
    f_re = (nr * a_re + ni * a_im) / den
    f_im = (ni * a_re - nr * a_im) / den
    bbar_re = f_re * bt_re - f_im * bt_im
    bbar_im = f_re * bt_im + f_im * bt_re
    return abar_re, abar_im, bbar_re, bbar_im


def _s5_params_fwd(a_re, a_im, log_dt, bt_re, bt_im):
    def body(ar, ai, ld, br, bi, o_ar, o_ai, o_br, o_bi):
        r = _s5_discretize(ar[...], ai[...], ld[...], br[...], bi[...])
        o_ar[...], o_ai[...], o_br[...], o_bi[...] = r

    gp = jax.ShapeDtypeStruct((S5_G, S5_P), F32)
    hgp = jax.ShapeDtypeStruct((S5_H, S5_G, S5_P), F32)
    return pl.pallas_call(body, name="s5_params_fwd", out_shape=[gp, gp, hgp, hgp])(a_re, a_im, log_dt, bt_re, bt_im)


def _s5_params_bwd(a_re, a_im, log_dt, bt_re, bt_im, d_ar, d_ai, d_br, d_bi):
    def body(ar, ai, ld, br, bi, gar, gai, gbr, gbi, o_ar, o_ai, o_ld, o_br, o_bi):
        _, vjp = jax.vjp(_s5_discretize, ar[...], ai[...], ld[...], br[...], bi[...])
        r = vjp((gar[...], gai[...], gbr[...], gbi[...]))
        o_ar[...], o_ai[...], o_ld[...], o_br[...], o_bi[...] = r

    gp = jax.ShapeDtypeStruct((S5_G, S5_P), F32)
    hgp = jax.ShapeDtypeStruct((S5_H, S5_G, S5_P), F32)
    return pl.pallas_call(body, name="s5_params_bwd", out_shape=[gp, gp, jax.ShapeDtypeStruct((S5_G, 1), F32), hgp, hgp])(
        a_re, a_im, log_dt, bt_re, bt_im, d_ar, d_ai, d_br, d_bi)


NSEG = 8
SCAN_LANES = 1024


def _to_segments(x):
    n, c = x.shape
    return x.reshape(NSEG, n // NSEG, c).transpose(1, 0, 2).reshape(n, c)


def _from_segments(x):
    n, c = x.shape
    return x.reshape(n // NSEG, NSEG, c).transpose(1, 0, 2).reshape(n, c)


def _segment_scan(re_ref, im_ref, st_re, st_im, a_re, a_im, n_slabs, adjoint, write):
    for q in range(NSTATE // SCAN_LANES):
        ls = slice(q * SCAN_LANES, (q + 1) * SCAN_LANES)
        ar = jnp.broadcast_to(a_re[:, ls], (8, SCAN_LANES))
        ai = jnp.broadcast_to(a_im[:, ls], (8, SCAN_LANES))

        def step(k, carry, ls=ls, ar=ar, ai=ai):
            s_r, s_i = carry
            slab = (n_slabs - 1 - k) if adjoint else k
            rows = pl.ds(pl.multiple_of(slab * 8, 8), 8)
            b_r, b_i = re_ref[rows, ls], im_ref[rows, ls]
            if adjoint:
                n_r = b_r + ar * s_r + ai * s_i
                n_i = b_i - ai * s_r + ar * s_i
            else:
                n_r = ar * s_r - ai * s_i + b_r
                n_i = ar * s_i + ai * s_r + b_i
            if write:
                re_ref[rows, ls] = n_r
                im_ref[rows, ls] = n_i
            return n_r, n_i

        s_r, s_i = lax.fori_loop(0, n_slabs, step, (st_re[:, ls], st_im[:, ls]), unroll=4)
        st_re[:, ls] = s_r
        st_im[:, ls] = s_i


def _s5_segment_states(e_re, e_im, ar, ai, seg_len, adjoint):
    def body(ere_ref, eim_ref, ar_ref, ai_ref, ore_ref, oim_ref):
        p_r, p_i = ar_ref[...], ai_ref[...]
        if adjoint:
            p_i = -p_i
        acc_r, acc_i = jnp.ones_like(p_r), jnp.zeros_like(p_r)
        n = seg_len
        while n:
            if n & 1:
                acc_r, acc_i = acc_r * p_r - acc_i * p_i, acc_r * p_i + acc_i * p_r
            n >>= 1
            if n:
                p_r, p_i = p_r * p_r - p_i * p_i, 2.0 * p_r * p_i
        e_r, e_i = ere_ref[...], eim_ref[...]
        s_r, s_i = jnp.zeros_like(acc_r), jnp.zeros_like(acc_r)
        rows_r, rows_i = [None] * NSEG, [None] * NSEG
        order = range(NSEG - 1, -1, -1) if adjoint else range(NSEG)
        for j in order:
            rows_r[j], rows_i[j] = s_r, s_i
            s_r, s_i = (acc_r * s_r - acc_i * s_i + e_r[j:j + 1], acc_r * s_i + acc_i * s_r + e_i[j:j + 1])
        ore_ref[...] = jnp.concatenate(rows_r, axis=0)
        oim_ref[...] = jnp.concatenate(rows_i, axis=0)

    st = jax.ShapeDtypeStruct((NSEG, NSTATE), F32)
    return pl.pallas_call(body, name="s5_segment_states_bwd" if adjoint else "s5_segment_states_fwd", out_shape=[st, st])(
        e_re, e_im, ar, ai)


def _s5_fwd_ends(x, pv, w_in, b_re, b_im, ar, ai, layer, tm, after=None):
    L = x.shape[0]

    def body(x_ref, pv_ref, win_ref, bre_ref, bim_ref, ar_ref, ai_ref, h_ref, u_ref, ere_ref, eim_ref, bu_re, bu_im):
        @pl.when(pl.program_id(0) == 0)
        def _():
            ere_ref[...] = jnp.zeros_like(ere_ref)
            eim_ref[...] = jnp.zeros_like(eim_ref)

        p = pv_ref[...]
        h, _, _ = _norm_mod(x_ref[...], p[R_N1:R_N1 + 1], p[R_SC1:R_SC1 + 1], p[R_SH1:R_SH1 + 1])
        hb = _bf(h)
        h_ref[...] = hb
        u = _dot(hb, win_ref[...])
        u_ref[...] = u
        ub = _bf(u)
        for k in range(S5_NB):
            uk = ub[:, k * S5_BH:(k + 1) * S5_BH]
            bu_re[:, k * S5_BP:(k + 1) * S5_BP] = _dot(uk, bre_ref[k])
            bu_im[:, k * S5_BP:(k + 1) * S5_BP] = _dot(uk, bim_ref[k])
        _segment_scan(bu_re, bu_im, ere_ref, eim_ref, ar_ref[...], ai_ref[...], tm // 8, adjoint=False, write=False)

    call, tail = _call_after(
        after, body, 7, name="s5_fwd_ends", grid=(L // tm,),
        in_specs=[_rows(tm, D), pl.BlockSpec((None, 8, D), lambda i: (layer, 0, 0)), _layer_w(D, D, 0),
                  _const_w((S5_NB, S5_BH, S5_BP)), _const_w((S5_NB, S5_BH, S5_BP)), _whole((1, NSTATE)), _whole((1, NSTATE))],
        out_specs=[_rows(tm, D), _rows(tm, D), _whole((NSEG, NSTATE)), _whole((NSEG, NSTATE))],
        out_shape=[jax.ShapeDtypeStruct((L, D), BF16), jax.ShapeDtypeStruct((L, D), F32),
                   jax.ShapeDtypeStruct((NSEG, NSTATE), F32), jax.ShapeDtypeStruct((NSEG, NSTATE), F32)],
        scratch_shapes=[pltpu.VMEM((tm, NSTATE), F32), pltpu.VMEM((tm, NSTATE), F32)],
        compiler_params=_params(1, 56),
    )
    return call(x, pv, w_in, b_re, b_im, ar, ai, *tail)


def _s5_fwd_out(x, u, pv, b_re, b_im, s0_re, s0_im, ar, ai, c_re, c_im, dvec, glu_w, glu_b, w_out, layer, tm):
    L = x.shape[0]

    def body(x_ref, u_ref, pv_ref, bre_ref, bim_ref, s0re_ref, s0im_ref, ar_ref, ai_ref, cre_ref, cim_ref, d_ref, gw_ref,
             gb_ref, wout_ref, x1_ref, sre_ref, sim_ref, y1_ref, zg_ref, y3_ref, y_ref, st_re, st_im):
        @pl.when(pl.program_id(0) == 0)
        def _():
            st_re[...] = s0re_ref[...]
            st_im[...] = s0im_ref[...]

        p = pv_ref[...]
        uv = u_ref[...]
        ub = _bf(uv)
        for k in range(S5_NB):
            uk = ub[:, k * S5_BH:(k + 1) * S5_BH]
            sre_ref[:, k * S5_BP:(k + 1) * S5_BP] = _dot(uk, bre_ref[k])
            sim_ref[:, k * S5_BP:(k + 1) * S5_BP] = _dot(uk, bim_ref[k])
        _segment_scan(sre_ref, sim_ref, st_re, st_im, ar_ref[...], ai_ref[...], tm // 8, adjoint=False, write=True)
        parts = []
        for k in range(S5_NB):
            sl = slice(k * S5_BP, (k + 1) * S5_BP)
            parts.append(_dot(_bf(sre_ref[:, sl]), cre_ref[k]) - _dot(_bf(sim_ref[:, sl]), cim_ref[k]))
        y1 = jnp.concatenate(parts, axis=1) + d_ref[...] * uv
        y1_ref[...] = y1
        y2 = jax.nn.gelu(y1)
        zg = _dot(_bf(y2), gw_ref[...]) + gb_ref[...]
        zg_ref[...] = zg
        y3b = _bf(y2 * jax.nn.sigmoid(zg))
        y3_ref[...] = y3b
        y = _dot(y3b, wout_ref[...])
        y_ref[...] = y
        x1_ref[...] = x_ref[...] + p[R_G1:R_G1 + 1] * y

    return pl.pallas_call(
        body, name="s5_fwd_out", grid=(L // tm,),
        in_specs=[_rows(tm, D), _rows(tm, D), pl.BlockSpec((None, 8, D), lambda i: (layer, 0, 0)),
                  _const_w((S5_NB, S5_BH, S5_BP)), _const_w((S5_NB, S5_BH, S5_BP)),
                  _whole((NSEG, NSTATE)), _whole((NSEG, NSTATE)), _whole((1, NSTATE)), _whole((1, NSTATE)),
                  _const_w((S5_NB, S5_BP, S5_BH)), _const_w((S5_NB, S5_BP, S5_BH)), _whole((1, D)),
                  _layer_w(D, D, 0), _whole((1, D)), _layer_w(D, D, 0)],
        out_specs=[_rows(tm, D), _rows(tm, NSTATE), _rows(tm, NSTATE), _rows(tm, D), _rows(tm, D), _rows(tm, D), _rows(tm, D)],
        out_shape=[jax.ShapeDtypeStruct((L, D), F32), jax.ShapeDtypeStruct((L, NSTATE), F32), jax.ShapeDtypeStruct((L, NSTATE), F32),
                   jax.ShapeDtypeStruct((L, D), F32), jax.ShapeDtypeStruct((L, D), F32),
                   jax.ShapeDtypeStruct((L, D), BF16), jax.ShapeDtypeStruct((L, D), F32)],
        scratch_shapes=[pltpu.VMEM((NSEG, NSTATE), F32), pltpu.VMEM((NSEG, NSTATE), F32)],
        compiler_params=_params(1, 56),
    )(x, u, pv, b_re, b_im, s0_re, s0_im, ar, ai, c_re, c_im, dvec, glu_w, glu_b, w_out)


def _s5_bwd_ends(dx1, y, y1, zg, u, pv, c_re, c_im, ar, ai, dvec, glu_w, w_out, layer, tm, after=None):
    L = dx1.shape[0]
    nt = L // tm

    def body(dx1_ref, y_ref, y1_ref, zg_ref, u_ref, pv_ref, cre_ref, cim_ref, ar_ref, ai_ref, d_ref, gw_ref, wout_ref,
             dy_ref, y2_ref, dzg_ref, dy1_ref, dus_ref, ere_ref, eim_ref, vs_ref, g_re, g_im):
        @pl.when(pl.program_id(0) == 0)
        def _():
            vs_ref[...] = jnp.zeros_like(vs_ref)
            ere_ref[...] = jnp.zeros_like(ere_ref)
            eim_ref[...] = jnp.zeros_like(eim_ref)

        dx1v, p = dx1_ref[...], pv_ref[...]
        dyb = _bf(dx1v * p[R_G1:R_G1 + 1])
        dy_ref[...] = dyb
        vs_ref[0:1, :] += _sum0(dx1v * y_ref[...])
        dy3 = _dot_nt(dyb, wout_ref[...])
        y2, gelu_vjp = jax.vjp(jax.nn.gelu, y1_ref[...])
        y2_ref[...] = _bf(y2)
        gate = jax.nn.sigmoid(zg_ref[...])
        dzg = dy3 * y2 * gate * (1.0 - gate)
        dzgb = _bf(dzg)
        dzg_ref[...] = dzgb
        vs_ref[1:2, :] += _sum0(dzg)
        dy2 = dy3 * gate + _dot_nt(dzgb, gw_ref[...])
        dy1 = gelu_vjp(dy2)[0]
        vs_ref[2:3, :] += _sum0(dy1 * u_ref[...])
        dus_ref[...] = dy1 * d_ref[...]
        dy1b = _bf(dy1)
        dy1_ref[...] = dy1b
        for k in range(S5_NB):
            dk = dy1b[:, k * S5_BH:(k + 1) * S5_BH]
            g_re[:, k * S5_BP:(k + 1) * S5_BP] = _dot_nt(dk, cre_ref[k])
            g_im[:, k * S5_BP:(k + 1) * S5_BP] = -_dot_nt(dk, cim_ref[k])
        _segment_scan(g_re, g_im, ere_ref, eim_ref, ar_ref[...], ai_ref[...], tm // 8, adjoint=True, write=False)

    call, tail = _call_after(
        after, body, 13, name="s5_bwd_ends", grid=(nt,),
        in_specs=[_rows(tm, D, nt)] * 5 + [pl.BlockSpec((None, 8, D), lambda i: (layer, 0, 0)),
                  _const_w((S5_NB, S5_BP, S5_BH)), _const_w((S5_NB, S5_BP, S5_BH)), _whole((1, NSTATE)), _whole((1, NSTATE)),
                  _whole((1, D)), _layer_w(D, D, 0), _layer_w(D, D, 0)],
        out_specs=[_rows(tm, D, nt)] * 5 + [_whole((NSEG, NSTATE)), _whole((NSEG, NSTATE)), _whole((8, D))],
        out_shape=[jax.ShapeDtypeStruct((L, D), BF16)] * 4 + [jax.ShapeDtypeStruct((L, D), F32),
                   jax.ShapeDtypeStruct((NSEG, NSTATE), F32), jax.ShapeDtypeStruct((NSEG, NSTATE), F32),
                   jax.ShapeDtypeStruct((8, D), F32)],
        scratch_shapes=[pltpu.VMEM((tm, NSTATE), F32), pltpu.VMEM((tm, NSTATE), F32)],
        compiler_params=_params(1, 56),
    )
    return call(dx1, y, y1, zg, u, pv, c_re, c_im, ar, ai, dvec, glu_w, w_out, *tail)


def _s5_bwd_in(dx1, dy1_b, du_skip, x, s_re, s_im, pv, b_re, b_im, c_re, c_im, l0_re, l0_im, ar, ai, w_in, layer, tm):
    L = x.shape[0]
    nt = L // tm

    def body(dx1_ref, dy1_ref, dus_ref, x_ref, sre_ref, sim_ref, hre_ref, him_ref, lre_ref, lim_ref, pv_ref, bre_ref, bim_ref,
             cre_ref, cim_ref, l0re_ref, l0im_ref, ar_ref, ai_ref, win_ref,
             dx_ref, du_ref, lamre_ref, lamim_ref, da_ref, vs_ref, g_re, g_im, st_re, st_im):
        gi = pl.program_id(0)
        tile = nt - 1 - gi

        @pl.when(gi == 0)
        def _():
            vs_ref[...] = jnp.zeros_like(vs_ref)
            da_ref[...] = jnp.zeros_like(da_ref)
            st_re[...] = l0re_ref[...]
            st_im[...] = l0im_ref[...]

        p = pv_ref[...]
        dy1b = dy1_ref[...]
        for k in range(S5_NB):
            dk = dy1b[:, k * S5_BH:(k + 1) * S5_BH]
            g_re[:, k * S5_BP:(k + 1) * S5_BP] = _dot_nt(dk, cre_ref[k])
            g_im[:, k * S5_BP:(k + 1) * S5_BP] = -_dot_nt(dk, cim_ref[k])
        _segment_scan(g_re, g_im, st_re, st_im, ar_ref[...], ai_ref[...], tm // 8, adjoint=True, write=True)
        lam_r, lam_i = g_re[...], g_im[...]
        lrb, lib = _bf(lam_r), _bf(lam_i)
        lamre_ref[...] = lrb
        lamim_ref[...] = lib

        def wrapped(last_ref):
            z = last_ref[...]
            row = lax.broadcasted_iota(jnp.int32, z.shape, 0)
            return jnp.where(row >= 1, pltpu.roll(z, 1, 0), 0.0)

        first_r = jnp.where(tile > 0, hre_ref[...], wrapped(lre_ref))
        first_i = jnp.where(tile > 0, him_ref[...], wrapped(lim_ref))
        sp_r = jnp.concatenate([first_r, sre_ref[0:tm - 8, :]], axis=0)
        sp_i = jnp.concatenate([first_i, sim_ref[0:tm - 8, :]], axis=0)
        da_ref[0:1, :] += _sum0(lam_r * sp_r + lam_i * sp_i)
        da_ref[1:2, :] += _sum0(lam_i * sp_r - lam_r * sp_i)

        parts = []
        for k in range(S5_NB):
            sl = slice(k * S5_BP, (k + 1) * S5_BP)
            parts.append(_dot_nt(lrb[:, sl], bre_ref[k]) + _dot_nt(lib[:, sl], bim_ref[k]))
        dub = _bf(jnp.concatenate(parts, axis=1) + dus_ref[...])
        du_ref[...] = dub
        dh = _dot_nt(dub, win_ref[...])
        _, xn, r = _norm_mod(x_ref[...], p[R_N1:R_N1 + 1], p[R_SC1:R_SC1 + 1], p[R_SH1:R_SH1 + 1])
        dx_ref[...] = dx1_ref[...] + _norm_mod_bwd(dh, xn, r, p[R_N1:R_N1 + 1], p[R_SC1:R_SC1 + 1])
        vs_ref[1:2, :] += _sum0(dh * xn)
        vs_ref[2:3, :] += _sum0(dh)

    halo = pl.BlockSpec((8, NSTATE), lambda i: (jnp.maximum((nt - 1 - i) * (tm // 8) - 1, 0), 0))
    last = pl.BlockSpec((8, NSTATE), lambda i: (L // 8 - 1, 0))
    return pl.pallas_call(
        body, name="s5_bwd_in", grid=(nt,),
        in_specs=[_rows(tm, D, nt), _rows(tm, D, nt), _rows(tm, D, nt), _rows(tm, D, nt), _rows(tm, NSTATE, nt), _rows(tm, NSTATE, nt),
                  halo, halo, last, last, pl.BlockSpec((None, 8, D), lambda i: (layer, 0, 0)),
                  _const_w((S5_NB, S5_BH, S5_BP)), _const_w((S5_NB, S5_BH, S5_BP)),
                  _const_w((S5_NB, S5_BP, S5_BH)), _const_w((S5_NB, S5_BP, S5_BH)),
                  _whole((NSEG, NSTATE)), _whole((NSEG, NSTATE)), _whole((1, NSTATE)), _whole((1, NSTATE)), _layer_w(D, D, 0)],
        out_specs=[_rows(tm, D, nt), _rows(tm, D, nt), _rows(tm, NSTATE, nt), _rows(tm, NSTATE, nt), _whole((8, NSTATE)), _whole((8, D))],
        out_shape=[jax.ShapeDtypeStruct((L, D), F32), jax.ShapeDtypeStruct((L, D), BF16),
                   jax.ShapeDtypeStruct((L, NSTATE), BF16), jax.ShapeDtypeStruct((L, NSTATE), BF16),
                   jax.ShapeDtypeStruct((8, NSTATE), F32), jax.ShapeDtypeStruct((8, D), F32)],
        scratch_shapes=[pltpu.VMEM((tm, NSTATE), F32), pltpu.VMEM((tm, NSTATE), F32),
                        pltpu.VMEM((NSEG, NSTATE), F32), pltpu.VMEM((NSEG, NSTATE), F32)],
        compiler_params=_params(1, 60),
    )(dx1, dy1_b, du_skip, x, s_re, s_im, s_re, s_im, s_re, s_im, pv, b_re, b_im, c_re, c_im, l0_re, l0_im, ar, ai, w_in)


def _blockdiag_b(bt):
    b = bt.reshape(S5_H, S5_NB, 16, S5_P).transpose(1, 2, 0, 3)
    eye = jnp.eye(16, dtype=bt.dtype)
    return (b[:, :, :, None, :] * eye[None, :, None, :, None]).reshape(S5_NB, S5_BH, S5_BP)


def _unblock_b(d):
    d = jnp.einsum("bghgp->bghp", d.reshape(S5_NB, 16, S5_H, 16, S5_P))
    return d.transpose(2, 0, 1, 3).reshape(S5_H, S5_G, S5_P)


def _blockdiag_c(cm):
    c4 = cm.reshape(S5_NB, 16, S5_H, S5_P)
    eye = jnp.eye(16, dtype=cm.dtype)
    out = c4.transpose(0, 1, 3, 2)[:, :, :, None, :] * eye[None, :, None, :, None]
    return out.reshape(S5_NB, S5_BP, S5_BH)


def _unblock_c(d):
    d = jnp.einsum("bgpgh->bghp", d.reshape(S5_NB, 16, S5_P, 16, S5_H))
    return d.reshape(S5_G, S5_H, S5_P)


def _tril_mask():
    return lax.broadcasted_iota(jnp.int32, (SG_CHUNK, SG_CHUNK), 0) >= lax.broadcasted_iota(jnp.int32, (SG_CHUNK, SG_CHUNK), 1)


def _sg_fwd(x, pv, w_in, w_s, b_t, vg, w_out, layer, tm, after=None):
    L = x.shape[0]
    nc = tm // SG_CHUNK

    def body(x_ref, pv_ref, win_ref, ws_ref, bt_ref, vg_ref, wout_ref, x1_ref, h_ref, uv_ref, vm_ref, q_ref, y_ref):
        xv, p = x_ref[...], pv_ref[...]
        h, _, _ = _norm_mod(xv, p[R_N1:R_N1 + 1], p[R_SC1:R_SC1 + 1], p[R_SH1:R_SH1 + 1])
        hb = _bf(h)
        h_ref[...] = hb
        uv = _dot(hb, win_ref[...])
        uv_ref[...] = uv
        v = uv[:, D:]
        rv = lax.rsqrt(jnp.mean(v * v, axis=-1, keepdims=True) + EPS)
        vnb = _bf((v * rv) * vg_ref[...])
        mask = _tril_mask()
        bt = bt_ref[...]
        for hd in range(SG_HEADS):
            wm = _bf(jnp.where(mask, ws_ref[hd], 0.0))
            cs = slice(hd * SG_CHUNK, (hd + 1) * SG_CHUNK)
            for ck in range(nc):
                rs = slice(ck * SG_CHUNK, (ck + 1) * SG_CHUNK)
                vm_ref[rs, cs] = _dot(wm, vnb[rs, cs]) + bt[:, hd:hd + 1]
        qb = _bf(uv[:, :D] * vm_ref[...])
        q_ref[...] = qb
        y = _dot(qb, wout_ref[...])
        y_ref[...] = y
        x1_ref[...] = xv + p[R_G1:R_G1 + 1] * y

    call, tail = _call_after(
        after, body, 7, name="sg_fwd", grid=(L // tm,),
        in_specs=[_rows(tm, D), pl.BlockSpec((None, 8, D), lambda i: (layer, 0, 0)), _layer_w(D, 2 * D, 0),
                  _whole((SG_HEADS, SG_CHUNK, SG_CHUNK)), _whole((SG_CHUNK, SG_HEADS)), _whole((1, D)), _layer_w(D, D, 0)],
        out_specs=[_rows(tm, D), _rows(tm, D), _rows(tm, 2 * D), _rows(tm, D), _rows(tm, D), _rows(tm, D)],
        out_shape=[jax.ShapeDtypeStruct((L, D), F32), jax.ShapeDtypeStruct((L, D), BF16), jax.ShapeDtypeStruct((L, 2 * D), F32),
                   jax.ShapeDtypeStruct((L, D), F32), jax.ShapeDtypeStruct((L, D), BF16), jax.ShapeDtypeStruct((L, D), F32)],
        compiler_params=_params(1, 56),
    )
    return call(x, pv, w_in, w_s, b_t, vg, w_out, *tail)


def _sg_bwd(dx1, x, y, uv, vm, pv, w_in, w_s, vg, w_out, layer, tm, after=None):
    L = x.shape[0]
    nc = tm // SG_CHUNK

    def body(dx1_ref, x_ref, y_ref, uv_ref, vm_ref, pv_ref, win_ref, ws_ref, vg_ref, wout_ref,
             dx_ref, duv_ref, dy_ref, vs_ref, dws_ref, dbt_ref, dvn_scr):
        @pl.when(pl.program_id(0) == 0)
        def _():
            vs_ref[...] = jnp.zeros_like(vs_ref)
            dws_ref[...] = jnp.zeros_like(dws_ref)
            dbt_ref[...] = jnp.zeros_like(dbt_ref)

        dx1v, p = dx1_ref[...], pv_ref[...]
        dyb = _bf(dx1v * p[R_G1:R_G1 + 1])
        dy_ref[...] = dyb
        vs_ref[0:1, :] += _sum0(dx1v * y_ref[...])
        dq = _dot_nt(dyb, wout_ref[...])
        uv = uv_ref[...]
        u, v = uv[:, :D], uv[:, D:]
        dub = _bf(dq * vm_ref[...])
        dvm = dq * u
        dvmb = _bf(dvm)
        rv = lax.rsqrt(jnp.mean(v * v, axis=-1, keepdims=True) + EPS)
        vh = v * rv
        vgv = vg_ref[...]
        vnb = _bf(vh * vgv)
        mask = _tril_mask()
        for hd in range(SG_HEADS):
            wm = _bf(jnp.where(mask, ws_ref[hd], 0.0))
            cs = slice(hd * SG_CHUNK, (hd + 1) * SG_CHUNK)
            dws = jnp.zeros((SG_CHUNK, SG_CHUNK), F32)
            dbs = jnp.zeros((SG_CHUNK, 1), F32)
            for ck in range(nc):
                rs = slice(ck * SG_CHUNK, (ck + 1) * SG_CHUNK)
                dvn_scr[rs, cs] = _dot_tn(wm, dvmb[rs, cs])
                dws = dws + _dot_nt(dvmb[rs, cs], vnb[rs, cs])
                dbs = dbs + jnp.sum(dvm[rs, cs], axis=1, keepdims=True)
            dws_ref[hd] += jnp.where(mask, dws, 0.0)
            dbt_ref[:, hd:hd + 1] += dbs
        dvn = dvn_scr[...]
        vs_ref[3:4, :] += _sum0(dvn * vh)
        dvnn = dvn * vgv
        dvb = _bf(rv * (dvnn - vh * jnp.mean(dvnn * vh, axis=-1, keepdims=True)))
        duv_ref[:, 0:D] = dub
        duv_ref[:, D:2 * D] = dvb
        dh = _dot_nt(dub, win_ref[:, 0:D]) + _dot_nt(dvb, win_ref[:, D:2 * D])
        _, xn, r = _norm_mod(x_ref[...], p[R_N1:R_N1 + 1], p[R_SC1:R_SC1 + 1], p[R_SH1:R_SH1 + 1])
        dx_ref[...] = dx1v + _norm_mod_bwd(dh, xn, r, p[R_N1:R_N1 + 1], p[R_SC1:R_SC1 + 1])
        vs_ref[1:2, :] += _sum0(dh * xn)
        vs_ref[2:3, :] += _sum0(dh)

    call, tail = _call_after(
        after, body, 10, name="sg_bwd", grid=(L // tm,),
        in_specs=[_rows(tm, D), _rows(tm, D), _rows(tm, D), _rows(tm, 2 * D), _rows(tm, D),
                  pl.BlockSpec((None, 8, D), lambda i: (layer, 0, 0)), _layer_w(D, 2 * D, 0),
                  _whole((SG_HEADS, SG_CHUNK, SG_CHUNK)), _whole((1, D)), _layer_w(D, D, 0)],
        out_specs=[_rows(tm, D), _rows(tm, 2 * D), _rows(tm, D), _whole((8, D)),
                   _whole((SG_HEADS, SG_CHUNK, SG_CHUNK)), _whole((SG_CHUNK, SG_HEADS))],
        out_shape=[jax.ShapeDtypeStruct((L, D), F32), jax.ShapeDtypeStruct((L, 2 * D), BF16), jax.ShapeDtypeStruct((L, D), BF16),
                   jax.ShapeDtypeStruct((8, D), F32), jax.ShapeDtypeStruct((SG_HEADS, SG_CHUNK, SG_CHUNK), F32),
                   jax.ShapeDtypeStruct((SG_CHUNK, SG_HEADS), F32)],
        scratch_shapes=[pltpu.VMEM((tm, D), F32)],
        compiler_params=_params(1, 56),
    )
    return call(dx1, x, y, uv, vm, pv, w_in, w_s, vg, w_out, *tail)


def _final(x, target, fg, tm):
    L = x.shape[0]

    def body(x_ref, t_ref, g_ref, dx_ref, vs_ref):
        @pl.when(pl.program_id(0) == 0)
        def _():
            vs_ref[...] = jnp.zeros_like(vs_ref)

        xv, g = x_ref[...], g_ref[...]
        r = lax.rsqrt(jnp.mean(xv * xv, axis=-1, keepdims=True) + EPS)
        xn = xv * r
        e = xn * g - t_ref[...]
        vs_ref[0:1, :] += jnp.sum(e * e)
        dout = e * (1.0 / D)
        vs_ref[1:2, :] += _sum0(dout * xn)
        dxn = dout * g
        dx_ref[...] = r * (dxn - xn * jnp.mean(dxn * xn, axis=-1, keepdims=True))

    return pl.pallas_call(
        body, name="final_loss", grid=(L // tm,),
        in_specs=[_rows(tm, D), _rows(tm, D), _whole((1, D))],
        out_specs=[_rows(tm, D), _whole((8, D))],
        out_shape=[jax.ShapeDtypeStruct((L, D), F32), jax.ShapeDtypeStruct((8, D), F32)],
        compiler_params=_params(1),
    )(x, target, fg)


def _pack_flat(arrs, multiple=LANES):
    flat = jnp.concatenate([a.reshape(-1).astype(F32) for a in arrs])
    return jnp.pad(flat, (0, -flat.shape[0] % multiple))


def _pack(arrs, row_multiple=8):
    return _pack_flat(arrs, row_multiple * LANES).reshape(-1, LANES)


def _unpack(buf, shapes, lead=()):
    flat = buf.reshape(lead + (-1,))
    out, off = [], 0
    for s in shapes:
        n = 1
        for d in s:
            n *= d
        out.append(flat[..., off:off + n].reshape(lead + tuple(s)))
        off += n
    return out


BIG = ("ff_w1", "ff_w2", "conv_w_in", "conv_w_out", "ssm_w_in", "ssm_glu_w", "ssm_w_out", "sg_w_in", "sg_w_out")
BIG_AXIS = {"ff_w1": 2, "ff_w2": 1, "conv_w_in": 2, "conv_w_out": 1, "ssm_w_in": 1, "ssm_glu_w": 1, "ssm_w_out": 1,
            "sg_w_in": 2, "sg_w_out": 1}
LAYER_WEIGHTS = (
    (("conv_w_in", 0), ("conv_w_out", 0), ("ff_w1", 0), ("ff_w2", 0)),
    (("ssm_w_in", 0), ("ssm_glu_w", 0), ("ssm_w_out", 0), ("ff_w1", 1), ("ff_w2", 1)),
    (("sg_w_in", 0), ("sg_w_out", 0), ("ff_w1", 2), ("ff_w2", 2)),
    (("conv_w_in", 1), ("conv_w_out", 1), ("ff_w1", 3), ("ff_w2", 3)),
)
GATHER_GROUPS = tuple(grp for lw in LAYER_WEIGHTS for grp in (lw[:-2], lw[-2:]))
SMALL_SHARDED = ("conv_w", "conv_b", "sg_v_g")
SMALL_WIDE_PADDED = ("ssm_b_re", "ssm_b_im")
SMALL = ("ada_b", "norm1_g", "norm2_g", "final_g", "ssm_a_re", "ssm_a_im", "ssm_log_dt", "ssm_b_re", "ssm_b_im", "ssm_c_re",
         "ssm_c_im", "ssm_d", "ssm_glu_b", "sg_w_s", "sg_b_s") + SMALL_SHARDED
WEIGHTS = ("ada_w", "ada_b", "norm1_g", "norm2_g", "ff_w1", "ff_w2", "final_g", "conv_w_in", "conv_w", "conv_b", "conv_w_out",
           "ssm_w_in", "ssm_a_re", "ssm_a_im", "ssm_log_dt", "ssm_b_re", "ssm_b_im", "ssm_c_re", "ssm_c_im", "ssm_d",
           "ssm_glu_w", "ssm_glu_b", "ssm_w_out", "sg_w_in", "sg_v_g", "sg_w_s", "sg_b_s", "sg_w_out")


def kernel(x, c, ada_w, ada_b, norm1_g, norm2_g, ff_w1, ff_w2, final_g, conv_w_in, conv_w, conv_b, conv_w_out, ssm_w_in, ssm_a_re, ssm_a_im, ssm_log_dt, ssm_b_re, ssm_b_im, ssm_c_re, ssm_c_im, ssm_d, ssm_glu_w, ssm_glu_b, ssm_w_out, sg_w_in, sg_v_g, sg_w_s, sg_b_s, sg_w_out, loss_target, m_ada_w, m_ada_b, m_norm1_g, m_norm2_g, m_ff_w1, m_ff_w2, m_final_g, m_conv_w_in, m_conv_w, m_conv_b, m_conv_w_out, m_ssm_w_in, m_ssm_a_re, m_ssm_a_im, m_ssm_log_dt, m_ssm_b_re, m_ssm_b_im, m_ssm_c_re, m_ssm_c_im, m_ssm_d, m_ssm_glu_w, m_ssm_glu_b, m_ssm_w_out, m_sg_w_in, m_sg_v_g, m_sg_w_s, m_sg_b_s, m_sg_w_out, v_ada_w, v_ada_b, v_norm1_g, v_norm2_g, v_ff_w1, v_ff_w2, v_final_g, v_conv_w_in, v_conv_w, v_conv_b, v_conv_w_out, v_ssm_w_in, v_ssm_a_re, v_ssm_a_im, v_ssm_log_dt, v_ssm_b_re, v_ssm_b_im, v_ssm_c_re, v_ssm_c_im, v_ssm_d, v_ssm_glu_w, v_ssm_glu_b, v_ssm_w_out, v_sg_w_in, v_sg_v_g, v_sg_w_s, v_sg_b_s, v_sg_w_out):
    args = dict(locals())
    w = {n: args[n] for n in WEIGHTS}
    m = {n: args["m_" + n] for n in WEIGHTS}
    v = {n: args["v_" + n] for n in WEIGHTS}
    L = x.shape[1]
    tm = min(L, 256)
    tm2 = min(L, 512)
    chip = 2 * lax.axis_index("x") + lax.axis_index("y")
    me = 2 * chip + lax.axis_index("c")
    xin = x[0]
    target = loss_target[0]
    chip1 = chip.reshape(1).astype(jnp.int32)

    gathers = []

    def start_gather(g, after):
        entries = GATHER_GROUPS[g]
        axes = [BIG_AXIS[n] for n, _ in entries]
        lands = [_cast_place(w[n], li, BIG_AXIS[n], chip1, f"cast_{n}_{li}") for n, li in entries]
        s_sems, r_sems, lands, token = _gather_start(lands, axes, f"gather_start{g}", after)
        gathers.append((s_sems, r_sems, lands, axes))
        return token

    def weights_of(g, after):
        s_sems, r_sems, lands, axes = gathers[g]
        lands = _gather_wait(s_sems, r_sems, lands, axes, f"gather_wait{g}", after)
        lands = _gather_share(lands, axes, f"gather_share{g}")
        token = start_gather(g + 2, lands[0]) if g + 2 < len(GATHER_GROUPS) else None
        return dict(zip([n for n, _ in GATHER_GROUPS[g]], lands)), token

    small_in = _pack([c, conv_w, conv_b, sg_v_g])
    got = _allgather_small(small_in, "gather_small_inputs").reshape(N_DEV, -1)
    c_all, cw_sh, cb_sh, vg_sh = _unpack(got, [(D,), conv_w.shape, conv_b.shape, sg_v_g.shape], lead=(N_DEV,))
    conv_w_full = jnp.concatenate([cw_sh[2 * k] for k in range(4)], axis=-1)
    conv_b_full = jnp.concatenate([cb_sh[2 * k] for k in range(4)], axis=-1)
    vg_full = jnp.concatenate([vg_sh[2 * k] for k in range(4)], axis=-1)
    c16 = jnp.pad(c_all, ((0, 16 - N_DEV), (0, 0)))

    cols = ada_w.shape[2]
    ada_b_cols = lax.dynamic_slice_in_dim(ada_b, chip * cols, cols, axis=1)[:, None, :]
    mod_sh = _ada_fwd(c16, ada_w, ada_b_cols)[:, :N_DEV, :]
    mod_all = _allgather_small(_pack([mod_sh]), "gather_mod").reshape(N_DEV, -1)
    mod_all = _unpack(mod_all, [mod_sh.shape], lead=(N_DEV,))[0]
    mod_mine = lax.dynamic_index_in_dim(mod_all[0::2], me, axis=2, keepdims=False)
    mod_mine = mod_mine.transpose(1, 0, 2).reshape(DEPTH, 6, D)
    pv = jnp.concatenate([mod_mine, norm1_g[:, None, :], norm2_g[:, None, :]], axis=1)

    start_gather(1, start_gather(0, pv))

    cw_rows = jnp.concatenate([conv_w_full, conv_b_full[:, None, :], jnp.zeros((conv_w_full.shape[0], 4, D), F32)], axis=1)

    a_re, a_im = ssm_a_re[0], ssm_a_im[0]
    log_dt = ssm_log_dt[0][:, None]
    bt_re, bt_im = ssm_b_re[0].transpose(2, 0, 1), ssm_b_im[0].transpose(2, 0, 1)
    abar_re, abar_im, bbar_re, bbar_im = _s5_params_fwd(a_re, a_im, log_dt, bt_re, bt_im)
    ar_vec, ai_vec = abar_re.reshape(1, NSTATE), abar_im.reshape(1, NSTATE)
    bd_re, bd_im = _bf(_blockdiag_b(bbar_re)), _bf(_blockdiag_b(bbar_im))
    cd_re, cd_im = _bf(_blockdiag_c(ssm_c_re[0])), _bf(_blockdiag_c(ssm_c_im[0]))

    saved = []
    fulls = []
    xl = xin
    for i in range(DEPTH):
        kind = MIXER_OF_LAYER[i]
        j = i // 3
        full, tok = weights_of(2 * i, cd_im if i == 0 else xl)
        fulls.append(full)
        if kind == 0:
            x1, h, bcx, conv, q, y = _conv_fwd(xl, pv, full["conv_w_in"], full["conv_w_out"], cw_rows, i, j, tm2, after=tok)
            mix = dict(h=h, bcx=bcx, conv=conv, q=q, y=y)
        elif kind == 1:
            xp = _to_segments(xl)
            h, u, e_re, e_im = _s5_fwd_ends(xp, pv, full["ssm_w_in"], bd_re, bd_im, ar_vec, ai_vec, i, tm, after=tok)
            s0_re, s0_im = _s5_segment_states(e_re, e_im, ar_vec, ai_vec, L // NSEG, adjoint=False)
            x1p, s_re, s_im, y1, zg, y3, y = _s5_fwd_out(xp, u, pv, bd_re, bd_im, s0_re, s0_im, ar_vec, ai_vec, cd_re, cd_im,
                                                         ssm_d, full["ssm_glu_w"], ssm_glu_b, full["ssm_w_out"], i, tm)
            x1 = _from_segments(x1p)
            mix = dict(xp=xp, h=h, u=u, s_re=s_re, s_im=s_im, y1=y1, zg=zg, y3=y3, y=y)
        else:
            x1, h, uv, vm, q, y = _sg_fwd(xl, pv, full["sg_w_in"], sg_w_s[0], sg_b_s[0].T, vg_full, full["sg_w_out"], i, tm2,
                                          after=tok)
            mix = dict(h=h, uv=uv, vm=vm, q=q, y=y)
        ffn_weights, tok = weights_of(2 * i + 1, x1)
        full.update(ffn_weights)
        x2, h2, a, f = _ffn_fwd(x1, pv, full["ff_w1"], full["ff_w2"], i, tm2, after=tok)
        saved.append(dict(x=xl, x1=x1, h2=h2, a=a, f=f, **mix))
        xl = x2

    dxl, vs_fin = _final(xl, target, final_g[None, :], tm)

    gfull = {n: [None] * w[n].shape[0] for n in BIG}
    vs_mix, vs_ffn = [None] * DEPTH, [None] * DEPTH
    small_g = {}
    scatters = {}
    token = None

    def start_scatter(key, entries, after):
        garrs = [gfull[n][li][None] for n, li in entries]
        gaxes = [BIG_AXIS[n] for n, _ in entries]
        s_sems, r_sems, garrs, lands, tok = _scatter_start(garrs, gaxes, f"scatter_start{key}", after)
        scatters[key] = (s_sems, r_sems, garrs, lands, gaxes, entries)
        return tok

    for i in reversed(range(DEPTH)):
        kind = MIXER_OF_LAYER[i]
        j = i // 3
        sv = saved[i]
        full = fulls[i]
        dx1, p_b, da_b, df_b, vs_ffn[i] = _ffn_bwd(dxl, sv["x1"], sv["a"], sv["f"], pv, full["ff_w1"], full["ff_w2"], i, tm,
                                                   after=token)
        gfull["ff_w1"][i] = _mm_tn(sv["h2"], da_b, f"wgrad_ff_w1_{i}")
        gfull["ff_w2"][i] = _mm_tn(p_b, df_b, f"wgrad_ff_w2_{i}")
        if i == 0:
            token = start_scatter("0f", LAYER_WEIGHTS[0][2:], dx1)
        if kind == 0:
            dxl, dbcx_b, dy_b, vsm = _conv_bwd(dx1, sv["x"], sv["y"], sv["bcx"], sv["conv"], pv, full["conv_w_in"],
                                               full["conv_w_out"], cw_rows, i, j, tm2, after=token if i == 0 else None)
            gfull["conv_w_in"][j] = _mm_tn(sv["h"], dbcx_b, f"wgrad_conv_w_in_{j}")
            gfull["conv_w_out"][j] = _mm_tn(sv["q"], dy_b, f"wgrad_conv_w_out_{j}")
            small_g.setdefault("conv_w", [None, None])[j] = vsm[3:6]
            small_g.setdefault("conv_b", [None, None])[j] = vsm[6]
        elif kind == 1:
            dx1p = _to_segments(dx1)
            dy_b, y2_b, dzg_b, dy1_b, du_skip, eb_re, eb_im, vsm = _s5_bwd_ends(
                dx1p, sv["y"], sv["y1"], sv["zg"], sv["u"], pv, cd_re, cd_im, ar_vec, ai_vec, ssm_d, full["ssm_glu_w"],
                full["ssm_w_out"], i, tm)
            l0_re, l0_im = _s5_segment_states(eb_re, eb_im, ar_vec, ai_vec, L // NSEG, adjoint=True)
            dxp, du_b, lam_re, lam_im, dabar, vs_in = _s5_bwd_in(
                dx1p, dy1_b, du_skip, sv["xp"], sv["s_re"], sv["s_im"], pv, bd_re, bd_im, cd_re, cd_im, l0_re, l0_im,
                ar_vec, ai_vec, full["ssm_w_in"], i, tm)
            dxl = _from_segments(dxp)
            gfull["ssm_w_out"][0] = _mm_tn(sv["y3"], dy_b, "wgrad_ssm_w_out")
            gfull["ssm_glu_w"][0] = _mm_tn(y2_b, dzg_b, "wgrad_ssm_glu_w")
            gfull["ssm_w_in"][0] = _mm_tn(sv["h"], du_b, "wgrad_ssm_w_in")
            s5_late = dict(s_re=sv["s_re"], s_im=sv["s_im"], u=sv["u"], dy1_b=dy1_b, lam_re=lam_re, lam_im=lam_im, dabar=dabar)
            small_g.update(ssm_d=vsm[2], ssm_glu_b=vsm[1])
            vsm = jnp.concatenate([vsm[0:1], vs_in[1:3], jnp.zeros((5, D), F32)], axis=0)
        else:
            dxl, duv_b, dy_b, vsm, d_ws, d_bt = _sg_bwd(dx1, sv["x"], sv["y"], sv["uv"], sv["vm"], pv, full["sg_w_in"],
                                                        sg_w_s[0], vg_full, full["sg_w_out"], i, tm2)
            gfull["sg_w_in"][0] = _mm_tn(sv["h"], duv_b, "wgrad_sg_w_in")
            gfull["sg_w_out"][0] = _mm_tn(sv["q"], dy_b, "wgrad_sg_w_out")
            small_g.update(sg_v_g=vsm[3], sg_w_s=d_ws, sg_b_s=d_bt.T)
        vs_mix[i] = vsm
        token = start_scatter(str(i), LAYER_WEIGHTS[i], dxl) if i > 0 else start_scatter("0c", LAYER_WEIGHTS[0][:2], dxl)
    grad_x = dxl[None]

    sums = {n: [None] * w[n].shape[0] for n in BIG}

    def collect(key, after):
        s_sems, r_sems, garrs, lands, gaxes, entries = scatters[key]
        garrs, recv = _scatter_wait(s_sems, r_sems, garrs, lands, gaxes, f"scatter_wait{key}", after)
        for (n, li), g, r3, ax in zip(entries, garrs, recv, gaxes):
            sums[n][li] = _sum_parts(r3, g, ax, chip1, f"sum_{n}_{li}")
        return sums[entries[-1][0]][entries[-1][1]]

    after = token
    for key in ("3", "2", "1"):
        after = collect(key, after)
    early = [(n, li) for i in (3, 2, 1) for n, li in LAYER_WEIGHTS[i]]
    late = list(LAYER_WEIGHTS[0][2:]) + list(LAYER_WEIGHTS[0][:2])
    s_sems, r_sems, mine_thru, lands, tok = _swap_start([sums[n][li] for n, li in early], "swap_start_early", after)

    blocks = dict(
        c_re=_mm_tn_blocks(s5_late["s_re"], s5_late["dy1_b"], S5_BP, S5_BH, "wgrad_s5_c_re", after=tok),
        c_im=_mm_tn_blocks(s5_late["s_im"], s5_late["dy1_b"], S5_BP, S5_BH, "wgrad_s5_c_im", after=tok),
        b_re=_mm_tn_blocks(s5_late["u"], s5_late["lam_re"], S5_BH, S5_BP, "wgrad_s5_b_re", after=tok),
        b_im=_mm_tn_blocks(s5_late["u"], s5_late["lam_im"], S5_BH, S5_BP, "wgrad_s5_b_im", after=tok))
    d_are, d_aim, d_ldt, d_btre, d_btim = _s5_params_bwd(
        a_re, a_im, log_dt, bt_re, bt_im, s5_late["dabar"][0].reshape(S5_G, S5_P), s5_late["dabar"][1].reshape(S5_G, S5_P),
        _unblock_b(blocks["b_re"]), _unblock_b(blocks["b_im"]))
    small_g.update(ssm_a_re=d_are, ssm_a_im=d_aim, ssm_log_dt=d_ldt, ssm_b_re=d_btre.transpose(1, 2, 0),
                   ssm_b_im=d_btim.transpose(1, 2, 0), ssm_c_re=_unblock_c(blocks["c_re"]), ssm_c_im=-_unblock_c(blocks["c_im"]))

    mine_thru, got = _swap_wait(s_sems, r_sems, mine_thru, lands, "swap_wait_early", blocks["b_im"])
    sib = dict(zip(early, got))
    for (n, li), t in zip(early, mine_thru):
        sums[n][li] = t
    after = got[-1]
    for key in ("0f", "0c"):
        after = collect(key, after)
    sib.update(zip(late, _swap_with_sibling([sums[n][li] for n, li in late], "swap_grad_sums_late")))

    dmod = _mod_bwd(jnp.stack(vs_mix), jnp.stack(vs_ffn), pv)
    small_g.update(ada_b=dmod[:, :6, :], norm1_g=dmod[:, 6, :], norm2_g=dmod[:, 7, :], final_g=vs_fin[1],
                   conv_w=jnp.stack(small_g["conv_w"]), conv_b=jnp.stack(small_g["conv_b"]))

    loss_part = (0.5 / D) * vs_fin[0, 0:1]
    part_shapes = [(1,)] + [tuple(small_g[n].shape) for n in SMALL]
    slots = _reduce_pair(_pack([loss_part] + [small_g[n] for n in SMALL], 16), "reduce_small_pair", sib[late[-1]])
    s_sems, r_sems, slots, tok = _reduce_cross_start(slots, "reduce_small_cross_start")

    res = {}
    for n in BIG:
        res[n] = _adamw_layers(w[n], sums[n], [sib[(n, li)] for li in range(w[n].shape[0])], m[n], v[n], f"adamw_{n}", after=tok)
        tok = res[n][0]

    slots = _reduce_cross_wait(s_sems, r_sems, slots, "reduce_small_cross_wait", tok)
    parts_sum = _reduce_finish(slots, "reduce_small_finish")
    summed = _unpack(parts_sum, part_shapes)
    loss = summed[0][0]
    gsum = dict(zip(SMALL, summed[1:]))
    dmod_all = _allgather_small(_pack([small_g["ada_b"]]), "gather_dmod", parts_sum)
    dmod_all = dmod_all.reshape(N_DEV, DEPTH, 6 * D)
    dmod_cols = lax.dynamic_slice_in_dim(dmod_all, chip * cols, cols, axis=2).transpose(1, 0, 2)
    g_ada_w = _ada_bwd(c16, jnp.pad(dmod_cols, ((0, 0), (0, 16 - N_DEV), (0, 0))))

    shp = ada_w.shape
    two = lambda t: t.reshape(shp[0] * shp[1], shp[2])
    res["ada_w"] = [t.reshape(shp) for t in _adamw(two(ada_w), [two(g_ada_w)], two(m_ada_w), two(v_ada_w), "adamw_ada_w")]

    def mine(n):
        g = gsum[n]
        if n in SMALL_SHARDED:
            g = lax.dynamic_slice_in_dim(g, chip * w[n].shape[-1], w[n].shape[-1], axis=g.ndim - 1)
        return g.reshape(w[n].shape)

    for k, names in enumerate(([n for n in SMALL if n not in SMALL_WIDE_PADDED], list(SMALL_WIDE_PADDED))):
        outs = _adamw_many([w[n] for n in names], [mine(n) for n in names], [m[n] for n in names], [v[n] for n in names],
                           f"adamw_small{k}")
        for idx, n in enumerate(names):
            res[n] = [outs[part][idx] for part in range(4)]

    outs = [loss, grad_x]
    for part in range(4):
        outs += [res[n][part] for n in WEIGHTS]
    return tuple(outs)
```

```python
import functools

import jax
import jax.numpy as jnp
from jax import lax
from jax.experimental import pallas as pl
from jax.experimental.pallas import tpu as pltpu

F32 = jnp.float32
BF16 = jnp.bfloat16
D = 1024
EPS = 1e-6
DEPTH = 4
MIXER_OF_LAYER = (0, 1, 2, 0)
S5_G, S5_H, S5_P = 64, 16, 64
S5_NB = 4
S5_BH = S5_H * 16
S5_BP = S5_P * 16
NSTATE = S5_G * S5_P
SG_HEADS, SG_CHUNK = 8, 128
ADAM_LR, ADAM_B1, ADAM_B2, ADAM_EPS, ADAM_WD, ADAM_STEP = 0.001, 0.9, 0.999, 1e-08, 0.01, 10
N_DEV = 8
MESH = pl.DeviceIdType.MESH
LANES = 1024
R_SH1, R_SC1, R_G1, R_SH2, R_SC2, R_G2, R_N1, R_N2 = range(8)


def _dot(a, b):
    return jnp.dot(a, b, preferred_element_type=F32)


def _dot_nt(a, b):
    return lax.dot_general(a, b, (((1,), (1,)), ((), ())), preferred_element_type=F32)


def _dot_tn(a, b):
    return lax.dot_general(a, b, (((0,), (0,)), ((), ())), preferred_element_type=F32)


def _bf(x):
    return x.astype(BF16)


def _sum0(x):
    return jnp.sum(x, axis=0, keepdims=True)


def _params(n_axes, vmem_mb=48):
    return pltpu.CompilerParams(dimension_semantics=("arbitrary",) * n_axes, vmem_limit_bytes=vmem_mb << 20)


def _rows(tm, cols, nt=None):
    if nt is None:
        return pl.BlockSpec((tm, cols), lambda i: (i, 0))
    return pl.BlockSpec((tm, cols), lambda i: (nt - 1 - i, 0))


def _whole(shape):
    nd = len(shape)
    return pl.BlockSpec(shape, lambda *_: (0,) * nd)


def _layer_w(r, c, layer):
    return pl.BlockSpec((None, r, c), lambda *_: (layer, 0, 0), pipeline_mode=pl.Buffered(1))


def _const_w(shape):
    nd = len(shape)
    return pl.BlockSpec(shape, lambda *_: (0,) * nd, pipeline_mode=pl.Buffered(1))


def _call_after(after, body, n_in, *, in_specs, **kw):
    if after is None:
        return pl.pallas_call(body, in_specs=in_specs, **kw), ()

    def body_after(*refs):
        return body(*refs[:n_in], *refs[n_in + 1:])

    return pl.pallas_call(body_after, in_specs=list(in_specs) + [pl.BlockSpec(memory_space=pl.ANY)], **kw), (after,)


def _norm_mod(x, ng, sc, sh):
    r = lax.rsqrt(jnp.mean(x * x, axis=-1, keepdims=True) + EPS)
    xn = x * r
    return (xn * ng) * (1.0 + sc) + sh, xn, r


def _norm_mod_bwd(dh, xn, r, ng, sc):
    dxn = dh * (ng * (1.0 + sc))
    return r * (dxn - xn * jnp.mean(dxn * xn, axis=-1, keepdims=True))


def _shift_down(z, prev8, k):
    row = lax.broadcasted_iota(jnp.int32, z.shape, 0)
    if k == 1:
        return jnp.where(row >= 1, pltpu.roll(z, 1, 0), prev8[7:8])
    return jnp.where(row >= 2, pltpu.roll(z, 2, 0), jnp.where(row == 0, prev8[6:7], prev8[7:8]))


def _shift_up(z, next8, k):
    n = z.shape[0]
    row = lax.broadcasted_iota(jnp.int32, z.shape, 0)
    if k == 1:
        return jnp.where(row <= n - 2, pltpu.roll(z, n - 1, 0), next8[0:1])
    return jnp.where(row <= n - 3, pltpu.roll(z, n - 2, 0), jnp.where(row == n - 2, next8[0:1], next8[1:2]))


def _place():
    x, y, c = lax.axis_index("x"), lax.axis_index("y"), lax.axis_index("c")
    chips = [(1 - x, y), (x, 1 - y), (1 - x, 1 - y)]
    return x, y, c, chips


def _allgather_small(x_shard, name, after=None):
    m_per, n = x_shard.shape

    def body(x_ref, out_ref, send_sems, recv_sems, local_sem):
        x, y, c, chips = _place()
        me, sibling = (x, y, c), (x, y, 1 - c)

        def rows(px, py, pc):
            return out_ref.at[pl.ds((4 * px + 2 * py + pc) * m_per, m_per), :]

        def copy(k, block, to, src=None):
            return pltpu.make_async_remote_copy(
                src_ref=rows(*block) if src is None else src, dst_ref=rows(*block),
                send_sem=send_sems.at[k], recv_sem=recv_sems.at[k], device_id=to, device_id_type=MESH)

        mine = pltpu.make_async_copy(x_ref, rows(*me), local_sem)
        mine.start()
        first = [copy(0, me, sibling, src=x_ref)]
        first += [copy(1 + j, me, (*chip, c), src=x_ref) for j, chip in enumerate(chips)]
        for cp in first:
            cp.start()
        passed = [copy(4 + j, (*chip, c), sibling) for j, chip in enumerate(chips)]
        for j, chip in enumerate(chips):
            copy(1 + j, (*chip, c), me).wait_recv()
            passed[j].start()
        copy(0, sibling, me).wait_recv()
        for j, chip in enumerate(chips):
            copy(4 + j, (*chip, 1 - c), me).wait_recv()
        for cp in first + passed:
            cp.wait_send()
        mine.wait()

    call, tail = _call_after(
        after, body, 1, name=name, out_shape=jax.ShapeDtypeStruct((N_DEV * m_per, n), F32),
        in_specs=[pl.BlockSpec(memory_space=pltpu.VMEM)], out_specs=pl.BlockSpec(memory_space=pltpu.VMEM),
        scratch_shapes=[pltpu.SemaphoreType.DMA((7,)), pltpu.SemaphoreType.DMA((7,)), pltpu.SemaphoreType.DMA],
        compiler_params=pltpu.CompilerParams(vmem_limit_bytes=48 << 20),
    )
    return call(x_shard, *tail)


def _reduce_pair(x_part, name, after=None):
    m, n = x_part.shape
    h = m // 2

    def body(x_ref, slots_ref, sib_buf, send_sem, recv_sem):
        x, y, c, _ = _place()
        swap = pltpu.make_async_remote_copy(src_ref=x_ref, dst_ref=sib_buf, send_sem=send_sem, recv_sem=recv_sem,
                                            device_id=(x, y, 1 - c), device_id_type=MESH)
        swap.start()
        swap.wait()
        mine = pl.ds(pl.multiple_of(c * h, 8), h)
        slots_ref[pl.ds(2 * x + y, 1)] = (x_ref[mine, :] + sib_buf[mine, :])[None]

    call, tail = _call_after(
        after, body, 1, name=name, out_shape=jax.ShapeDtypeStruct((4, h, n), F32),
        in_specs=[pl.BlockSpec(memory_space=pltpu.VMEM)], out_specs=pl.BlockSpec(memory_space=pltpu.VMEM),
        scratch_shapes=[pltpu.VMEM((m, n), F32), pltpu.SemaphoreType.DMA, pltpu.SemaphoreType.DMA],
        compiler_params=pltpu.CompilerParams(vmem_limit_bytes=48 << 20),
    )
    return call(x_part, *tail)


def _reduce_cross_start(slots, name):
    def body(slots_ref, send_sems, recv_sems, thru, token):
        x, y, c, chips = _place()
        mine = slots_ref.at[pl.ds(2 * x + y, 1)]
        for j, chip in enumerate(chips):
            pltpu.make_async_remote_copy(src_ref=mine, dst_ref=mine, send_sem=send_sems.at[j], recv_sem=recv_sems.at[j],
                                         device_id=(*chip, c), device_id_type=MESH).start()
        token[...] = jnp.zeros_like(token)

    res = pl.pallas_call(
        body, name=name,
        out_shape=(pltpu.SemaphoreType.DMA((3,)), pltpu.SemaphoreType.DMA((3,)), pltpu.HBM(slots.shape, F32),
                   jax.ShapeDtypeStruct((8, 128), F32)),
        in_specs=[HBM_SPEC], out_specs=(SEM_SPEC, SEM_SPEC, HBM_SPEC, pl.BlockSpec(memory_space=pltpu.VMEM)),
        input_output_aliases={0: 2}, compiler_params=SPLIT_COPY_PARAMS,
    )(*_in_hbm([slots]))
    return res


def _reduce_cross_wait(send_sems, recv_sems, slots, name, after):
    def body(slots_ref, s_sems, r_sems, after_ref, thru):
        x, y, c, chips = _place()
        for j, chip in enumerate(chips):
            theirs = slots_ref.at[pl.ds(2 * chip[0] + chip[1], 1)]
            cp = pltpu.make_async_remote_copy(src_ref=theirs, dst_ref=theirs, send_sem=s_sems.at[j], recv_sem=r_sems.at[j],
                                              device_id=(x, y, c), device_id_type=MESH)
            cp.wait_send()
            cp.wait_recv()

    return pl.pallas_call(
        body, name=name, out_shape=pltpu.HBM(slots.shape, F32),
        in_specs=[HBM_SPEC, SEM_SPEC, SEM_SPEC, ANY_SPEC], out_specs=HBM_SPEC,
        input_output_aliases={0: 0}, compiler_params=SPLIT_COPY_PARAMS,
    )(slots, send_sems, recv_sems, after)


def _reduce_finish(slots, name):
    _, h, n = slots.shape

    def body(slots_ref, out_ref, send_sem, recv_sem):
        x, y, c, _ = _place()
        mine = pl.ds(pl.multiple_of(c * h, 8), h)
        theirs = pl.ds(pl.multiple_of((1 - c) * h, 8), h)
        out_ref[mine, :] = ((slots_ref[0] + slots_ref[1]) + slots_ref[2]) + slots_ref[3]
        give = pltpu.make_async_remote_copy(src_ref=out_ref.at[mine, :], dst_ref=out_ref.at[mine, :], send_sem=send_sem,
                                            recv_sem=recv_sem, device_id=(x, y, 1 - c), device_id_type=MESH)
        give.start()
        give.wait_send()
        pltpu.make_async_remote_copy(src_ref=out_ref.at[theirs, :], dst_ref=out_ref.at[theirs, :], send_sem=send_sem,
                                     recv_sem=recv_sem, device_id=(x, y, c), device_id_type=MESH).wait_recv()

    return pl.pallas_call(
        body, name=name, out_shape=jax.ShapeDtypeStruct((2 * h, n), F32),
        in_specs=[pl.BlockSpec(memory_space=pltpu.VMEM)], out_specs=pl.BlockSpec(memory_space=pltpu.VMEM),
        scratch_shapes=[pltpu.SemaphoreType.DMA, pltpu.SemaphoreType.DMA],
        compiler_params=pltpu.CompilerParams(vmem_limit_bytes=48 << 20),
    )(slots)


def _shard_region(ref, full_shape, axis, chip_k, half=None):
    _, r, c = full_shape
    if axis == 1:
        rs = r // 4
        if half is None:
            return ref.at[:, pl.ds(pl.multiple_of(chip_k * rs, 128), rs), :]
        return ref.at[:, pl.ds(pl.multiple_of(chip_k * rs + half * (rs // 2), 128), rs // 2), :]
    cs = c // 4
    if half is None:
        return ref.at[:, :, pl.ds(pl.multiple_of(chip_k * cs, 128), cs)]
    return ref.at[:, pl.ds(pl.multiple_of(half * (r // 2), 128), r // 2), pl.ds(pl.multiple_of(chip_k * cs, 128), cs)]


HBM_SPEC = pl.BlockSpec(memory_space=pltpu.HBM)
SEM_SPEC = pl.BlockSpec(memory_space=pltpu.SEMAPHORE)
ANY_SPEC = pl.BlockSpec(memory_space=pl.ANY)
SPLIT_COPY_PARAMS = pltpu.CompilerParams(has_side_effects=pltpu.SideEffectType.DATAFLOW_SIDE_EFFECTING)


def _in_hbm(arrs):
    return [pltpu.with_memory_space_constraint(a, pltpu.HBM) for a in arrs]


def _cast_place(ws, layers, axes, place, name):
    n_arr = len(ws)
    in_specs, out_specs, fulls = [], [], []
    for w_stack, li, axis in zip(ws, layers, axes):
        _, r, c = w_stack.shape
        tr = r // 4
        fulls.append((1, 4 * r, c) if axis == 1 else (1, r, 4 * c))
        in_specs.append(pl.BlockSpec((None, tr, c), lambda i, p, li=li: (li, 2 * p[1] + i, 0)))
        if axis == 1:
            out_specs.append(pl.BlockSpec((None, tr, c), lambda i, p: (0, 4 * p[0] + 2 * p[1] + i, 0)))
        else:
            out_specs.append(pl.BlockSpec((None, tr, c), lambda i, p: (0, 2 * p[1] + i, p[0])))

    def body(p_ref, *refs):
        for a in range(n_arr):
            refs[n_arr + a][...] = _bf(refs[a][...])

    return pl.pallas_call(
        body, name=name,
        grid_spec=pltpu.PrefetchScalarGridSpec(num_scalar_prefetch=1, grid=(2,), in_specs=in_specs, out_specs=out_specs),
        out_shape=[jax.ShapeDtypeStruct(f, BF16) for f in fulls],
        compiler_params=_params(1),
    )(place, *ws)


def _gather_start(lands, axes, name, after):
    n_arr = len(lands)
    fulls = [tuple(l.shape) for l in lands]

    def body(*refs):
        land = refs[:n_arr]
        send_sems, recv_sems = refs[n_arr + 1:n_arr + 3]
        token = refs[-1]
        x, y, c, chips = _place()
        k_me = 2 * x + y
        for a in range(n_arr):
            mine = _shard_region(land[a], fulls[a], axes[a], k_me, c)
            for j, chip in enumerate(chips):
                pltpu.make_async_remote_copy(
                    src_ref=mine, dst_ref=mine, send_sem=send_sems.at[a * 3 + j], recv_sem=recv_sems.at[a * 3 + j],
                    device_id=(*chip, c), device_id_type=MESH).start()
        token[...] = jnp.zeros_like(token)

    res = pl.pallas_call(
        body, name=name,
        out_shape=(pltpu.SemaphoreType.DMA((3 * n_arr,)), pltpu.SemaphoreType.DMA((3 * n_arr,)),
                   *[pltpu.HBM(f, BF16) for f in fulls], jax.ShapeDtypeStruct((8, 128), F32)),
        in_specs=[HBM_SPEC] * n_arr + [ANY_SPEC],
        out_specs=(SEM_SPEC, SEM_SPEC, *[HBM_SPEC] * n_arr, pl.BlockSpec(memory_space=pltpu.VMEM)),
        input_output_aliases={a: 2 + a for a in range(n_arr)},
        compiler_params=SPLIT_COPY_PARAMS,
    )(*_in_hbm(lands), after)
    return res[0], res[1], list(res[2:2 + n_arr]), res[-1]


def _gather_wait(send_sems, recv_sems, lands, axes, name, after):
    n_arr = len(lands)
    fulls = [tuple(l.shape) for l in lands]

    def body(*refs):
        land = refs[:n_arr]
        s_sems, r_sems = refs[n_arr:n_arr + 2]
        x, y, c, chips = _place()
        for a in range(n_arr):
            for j, chip in enumerate(chips):
                k_j = 2 * chip[0] + chip[1]
                got = _shard_region(land[a], fulls[a], axes[a], k_j, c)
                cp = pltpu.make_async_remote_copy(
                    src_ref=got, dst_ref=got, send_sem=s_sems.at[a * 3 + j], recv_sem=r_sems.at[a * 3 + j],
                    device_id=(x, y, c), device_id_type=MESH)
                cp.wait_send()
                cp.wait_recv()

    res = pl.pallas_call(
        body, name=name,
        out_shape=tuple(pltpu.HBM(f, BF16) for f in fulls),
        in_specs=[HBM_SPEC] * n_arr + [SEM_SPEC, SEM_SPEC, ANY_SPEC],
        out_specs=tuple([HBM_SPEC] * n_arr),
        input_output_aliases={a: a for a in range(n_arr)},
        compiler_params=SPLIT_COPY_PARAMS,
    )(*lands, send_sems, recv_sems, after)
    return list(res)


def _gather_share(lands, axes, name):
    n_arr = len(lands)
    fulls = [tuple(l.shape) for l in lands]

    def body(*refs):
        land_in, land = refs[:n_arr], refs[n_arr:2 * n_arr]
        send_sems, recv_sems = refs[2 * n_arr:]
        x, y, c, chips = _place()
        copies = []
        for a in range(n_arr):
            for j, k_j in enumerate([2 * chip[0] + chip[1] for chip in chips] + [2 * x + y]):
                cp = pltpu.make_async_remote_copy(
                    src_ref=_shard_region(land_in[a], fulls[a], axes[a], k_j, c),
                    dst_ref=_shard_region(land[a], fulls[a], axes[a], k_j, c),
                    send_sem=send_sems.at[a * 4 + j], recv_sem=recv_sems.at[a * 4 + j],
                    device_id=(x, y, 1 - c), device_id_type=MESH)
                cp.start()
                copies.append(cp)
        for cp in copies:
            cp.wait()

    return pl.pallas_call(
        body, name=name, out_shape=[jax.ShapeDtypeStruct(f, BF16) for f in fulls],
        in_specs=[ANY_SPEC] * n_arr, out_specs=[ANY_SPEC] * n_arr,
        input_output_aliases={a: a for a in range(n_arr)},
        scratch_shapes=[pltpu.SemaphoreType.DMA((4 * n_arr,)), pltpu.SemaphoreType.DMA((4 * n_arr,))],
    )(*lands)


def _scatter_shapes(grads, axes):
    out = []
    for g, ax in zip(grads, axes):
        shp = list(g.shape)
        shp[ax] //= 4
        out.append((3,) + tuple(shp[1:]))
    return out


def _scatter_start(grads, axes, name, after):
    n_arr = len(grads)
    shapes = _scatter_shapes(grads, axes)
    lands = [lax.empty(s, BF16) for s in shapes]

    def body(*refs):
        ins, land = refs[:n_arr], refs[n_arr:2 * n_arr]
        send_sems, recv_sems = refs[2 * n_arr + 1:2 * n_arr + 3]
        token = refs[-1]
        x, y, c, chips = _place()
        for a in range(n_arr):
            for j, chip in enumerate(chips):
                k_j = 2 * chip[0] + chip[1]
                pltpu.make_async_remote_copy(
                    src_ref=_shard_region(ins[a], grads[a].shape, axes[a], k_j), dst_ref=land[a].at[pl.ds(j, 1)],
                    send_sem=send_sems.at[a * 3 + j], recv_sem=recv_sems.at[a * 3 + j],
                    device_id=(*chip, c), device_id_type=MESH).start()
        token[...] = jnp.zeros_like(token)

    res = pl.pallas_call(
        body, name=name,
        out_shape=(pltpu.SemaphoreType.DMA((3 * n_arr,)), pltpu.SemaphoreType.DMA((3 * n_arr,)),
                   *[pltpu.HBM(g.shape, BF16) for g in grads], *[pltpu.HBM(s, BF16) for s in shapes],
                   jax.ShapeDtypeStruct((8, 128), F32)),
        in_specs=[HBM_SPEC] * (2 * n_arr) + [ANY_SPEC],
        out_specs=(SEM_SPEC, SEM_SPEC, *[HBM_SPEC] * (2 * n_arr), pl.BlockSpec(memory_space=pltpu.VMEM)),
        input_output_aliases={a: 2 + a for a in range(2 * n_arr)},
        compiler_params=SPLIT_COPY_PARAMS,
    )(*_in_hbm(grads), *_in_hbm(lands), after)
    return res[0], res[1], list(res[2:2 + n_arr]), list(res[2 + n_arr:2 + 2 * n_arr]), res[-1]


def _scatter_wait(send_sems, recv_sems, grads, lands, axes, name, after):
    n_arr = len(grads)

    def body(*refs):
        ins, land = refs[:n_arr], refs[n_arr:2 * n_arr]
        s_sems, r_sems = refs[2 * n_arr:2 * n_arr + 2]
        x, y, c, chips = _place()
        for a in range(n_arr):
            for j, chip in enumerate(chips):
                k_j = 2 * chip[0] + chip[1]
                cp = pltpu.make_async_remote_copy(
                    src_ref=_shard_region(ins[a], grads[a].shape, axes[a], k_j), dst_ref=land[a].at[pl.ds(j, 1)],
                    send_sem=s_sems.at[a * 3 + j], recv_sem=r_sems.at[a * 3 + j],
                    device_id=(x, y, c), device_id_type=MESH)
                cp.wait_send()
                cp.wait_recv()

    res = pl.pallas_call(
        body, name=name,
        out_shape=(*[pltpu.HBM(g.shape, BF16) for g in grads], *[pltpu.HBM(l.shape, BF16) for l in lands]),
        in_specs=[HBM_SPEC] * (2 * n_arr) + [SEM_SPEC, SEM_SPEC, ANY_SPEC],
        out_specs=tuple([HBM_SPEC] * (2 * n_arr)),
        input_output_aliases={a: a for a in range(2 * n_arr)},
        compiler_params=SPLIT_COPY_PARAMS,
    )(*grads, *lands, send_sems, recv_sems, after)
    return list(res[:n_arr]), list(res[n_arr:])


def _swap_start(arrs, name, after):
    n_arr = len(arrs)
    lands = [lax.empty(a.shape, a.dtype) for a in arrs]

    def body(*refs):
        ins, land = refs[:n_arr], refs[n_arr:2 * n_arr]
        send_sems, recv_sems = refs[2 * n_arr + 1:2 * n_arr + 3]
        token = refs[-1]
        x, y, c, _ = _place()
        for a in range(n_arr):
            pltpu.make_async_remote_copy(
                src_ref=ins[a], dst_ref=land[a], send_sem=send_sems.at[a], recv_sem=recv_sems.at[a],
                device_id=(x, y, 1 - c), device_id_type=MESH).start()
        token[...] = jnp.zeros_like(token)

    res = pl.pallas_call(
        body, name=name,
        out_shape=(pltpu.SemaphoreType.DMA((n_arr,)), pltpu.SemaphoreType.DMA((n_arr,)),
                   *[pltpu.HBM(a.shape, a.dtype) for a in arrs], *[pltpu.HBM(a.shape, a.dtype) for a in arrs],
                   jax.ShapeDtypeStruct((8, 128), F32)),
        in_specs=[HBM_SPEC] * (2 * n_arr) + [ANY_SPEC],
        out_specs=(SEM_SPEC, SEM_SPEC, *[HBM_SPEC] * (2 * n_arr), pl.BlockSpec(memory_space=pltpu.VMEM)),
        input_output_aliases={a: 2 + a for a in range(2 * n_arr)},
        compiler_params=SPLIT_COPY_PARAMS,
    )(*_in_hbm(arrs), *_in_hbm(lands), after)
    return res[0], res[1], list(res[2:2 + n_arr]), list(res[2 + n_arr:2 + 2 * n_arr]), res[-1]


def _swap_wait(send_sems, recv_sems, arrs, lands, name, after):
    n_arr = len(arrs)

    def body(*refs):
        ins, land = refs[:n_arr], refs[n_arr:2 * n_arr]
        s_sems, r_sems = refs[2 * n_arr:2 * n_arr + 2]
        x, y, c, _ = _place()
        for a in range(n_arr):
            cp = pltpu.make_async_remote_copy(
                src_ref=ins[a], dst_ref=land[a], send_sem=s_sems.at[a], recv_sem=r_sems.at[a],
                device_id=(x, y, c), device_id_type=MESH)
            cp.wait_send()
            cp.wait_recv()

    res = pl.pallas_call(
        body, name=name,
        out_shape=(*[pltpu.HBM(a.shape, a.dtype) for a in arrs], *[pltpu.HBM(a.shape, a.dtype) for a in arrs]),
        in_specs=[HBM_SPEC] * (2 * n_arr) + [SEM_SPEC, SEM_SPEC, ANY_SPEC],
        out_specs=tuple([HBM_SPEC] * (2 * n_arr)),
        input_output_aliases={a: a for a in range(2 * n_arr)},
        compiler_params=SPLIT_COPY_PARAMS,
    )(*arrs, *lands, send_sems, recv_sems, after)
    return list(res[:n_arr]), list(res[n_arr:])


def _swap_with_sibling(arrs, name):
    n_arr = len(arrs)

    def body(*refs):
        ins, outs = refs[:n_arr], refs[n_arr:2 * n_arr]
        send_sems, recv_sems = refs[2 * n_arr:]
        x, y, c, _ = _place()
        copies = []
        for a in range(n_arr):
            cp = pltpu.make_async_remote_copy(
                src_ref=ins[a], dst_ref=outs[a], send_sem=send_sems.at[a], recv_sem=recv_sems.at[a],
                device_id=(x, y, 1 - c), device_id_type=MESH)
            cp.start()
            copies.append(cp)
        for cp in copies:
            cp.wait()

    any_spec = pl.BlockSpec(memory_space=pl.ANY)
    return pl.pallas_call(
        body, name=name, out_shape=[jax.ShapeDtypeStruct(a.shape, a.dtype) for a in arrs],
        in_specs=[any_spec] * n_arr, out_specs=[any_spec] * n_arr,
        scratch_shapes=[pltpu.SemaphoreType.DMA((n_arr,)), pltpu.SemaphoreType.DMA((n_arr,))],
    )(*arrs)


def _mm_tn(a, b, name, out_dtype=BF16):
    L, m = a.shape
    n = b.shape[1]
    bm, bn, bk = min(m, 1024), min(n, 1024), min(L, 2048)
    nk = L // bk

    def body(a_ref, b_ref, o_ref, acc):
        k = pl.program_id(2)

        @pl.when(k == 0)
        def _():
            acc[...] = jnp.zeros_like(acc)

        acc[...] += _dot_tn(_bf(a_ref[...]), _bf(b_ref[...]))

        @pl.when(k == nk - 1)
        def _():
            o_ref[...] = acc[...].astype(out_dtype)

    return pl.pallas_call(
        body, name=name, grid=(m // bm, n // bn, nk),
        in_specs=[pl.BlockSpec((bk, bm), lambda i, j, k: (k, i)), pl.BlockSpec((bk, bn), lambda i, j, k: (k, j))],
        out_specs=pl.BlockSpec((bm, bn), lambda i, j, k: (i, j)),
        out_shape=jax.ShapeDtypeStruct((m, n), out_dtype),
        scratch_shapes=[pltpu.VMEM((bm, bn), F32)],
        compiler_params=_params(3),
    )(a, b)


def _mm_tn_blocks(a, b, wa, wb, name, after=None):
    L = a.shape[0]
    nb = a.shape[1] // wa
    bk = min(L, 1024)
    nk = L // bk

    def body(a_ref, b_ref, o_ref):
        @pl.when(pl.program_id(1) == 0)
        def _():
            o_ref[...] = jnp.zeros_like(o_ref)

        o_ref[...] += _dot_tn(_bf(a_ref[...]), _bf(b_ref[...]))

    call, tail = _call_after(
        after, body, 2, name=name, grid=(nb, nk),
        in_specs=[pl.BlockSpec((bk, wa), lambda j, k: (k, j)), pl.BlockSpec((bk, wb), lambda j, k: (k, j))],
        out_specs=pl.BlockSpec((None, wa, wb), lambda j, k: (j, 0, 0)),
        out_shape=jax.ShapeDtypeStruct((nb, wa, wb), F32),
        compiler_params=_params(2),
    )
    return call(a, b, *tail)


def _sum_parts(parts, owns, axes, chip, name):
    n_arr = len(parts)
    steps = 4
    in_specs, out_specs, shapes = [], [], []
    for part, axis in zip(parts, axes):
        _, r, c = part.shape
        tr = r // steps
        shapes.append((r, c))
        in_specs.append(pl.BlockSpec((3, tr, c), lambda i, k: (0, i, 0)))
        out_specs.append(pl.BlockSpec((tr, c), lambda i, k: (i, 0)))
    for part, axis in zip(parts, axes):
        _, r, c = part.shape
        tr = r // steps
        if axis == 1:
            in_specs.append(pl.BlockSpec((None, tr, c), lambda i, k: (0, steps * k[0] + i, 0)))
        else:
            in_specs.append(pl.BlockSpec((None, tr, c), lambda i, k: (0, i, k[0])))

    def body(k_ref, *refs):
        for a in range(n_arr):
            p = refs[a][...].astype(F32)
            refs[2 * n_arr + a][...] = ((p[0] + p[1]) + p[2]) + refs[n_arr + a][...].astype(F32)

    return pl.pallas_call(
        body, name=name,
        grid_spec=pltpu.PrefetchScalarGridSpec(num_scalar_prefetch=1, grid=(steps,), in_specs=in_specs, out_specs=out_specs),
        out_shape=[jax.ShapeDtypeStruct(sh, F32) for sh in shapes],
        compiler_params=_params(1),
    )(chip, *parts, *owns)


def _adamw(w, g_parts, m, v, name):
    n_g = len(g_parts)
    if w.ndim == 2:
        r, c = w.shape
        tr = r
        for cand in (512, 256, 128, 64, 32, 16, 8):
            if r % cand == 0 and cand * c * 4 <= (2 << 20):
                tr = cand
                break
        spec = pl.BlockSpec((tr, c), lambda i: (i, 0))
        tiling = dict(grid=(r // tr,), in_specs=[spec] * (3 + n_g), out_specs=[spec] * 4, compiler_params=_params(1))
    else:
        tiling = dict(compiler_params=pltpu.CompilerParams(vmem_limit_bytes=48 << 20))

    def body(*refs):
        w_ref, g_refs, m_ref, v_ref = refs[0], refs[1:1 + n_g], refs[1 + n_g], refs[2 + n_g]
        g = g_refs[0][...]
        for gr in g_refs[1:]:
            g = g + gr[...]
        _adamw_update(g, w_ref, m_ref, v_ref, *refs[3 + n_g:])

    return pl.pallas_call(body, name=name, out_shape=[jax.ShapeDtypeStruct(w.shape, F32)] * 4, **tiling)(w, *g_parts, m, v)


def _adamw_update(g, w_ref, m_ref, v_ref, g_out, d_out, m_out, v_out):
    m_new = ADAM_B1 * m_ref[...] + (1.0 - ADAM_B1) * g
    v_new = ADAM_B2 * v_ref[...] + (1.0 - ADAM_B2) * (g * g)
    m_hat = m_new * (1.0 / (1.0 - ADAM_B1 ** ADAM_STEP))
    v_hat = v_new * (1.0 / (1.0 - ADAM_B2 ** ADAM_STEP))
    g_out[...] = g
    d_out[...] = -ADAM_LR * (m_hat / (jnp.sqrt(v_hat) + ADAM_EPS) + ADAM_WD * w_ref[...])
    m_out[...] = m_new
    v_out[...] = v_new


def _adamw_many(ws, gs, ms, vs, name):
    n = len(ws)

    def body(*refs):
        for k in range(n):
            _adamw_update(refs[n + k][...], refs[k], refs[2 * n + k], refs[3 * n + k],
                          refs[4 * n + k], refs[5 * n + k], refs[6 * n + k], refs[7 * n + k])

    outs = pl.pallas_call(body, name=name, out_shape=[jax.ShapeDtypeStruct(t.shape, F32) for t in ws] * 4,
                          compiler_params=pltpu.CompilerParams(vmem_limit_bytes=56 << 20))(*ws, *gs, *ms, *vs)
    return [outs[part * n:(part + 1) * n] for part in range(4)]


def _adamw_layers(w, q_mine, q_sib, m, v, name, after=None):
    n, r, c = w.shape
    tr = r
    for cand in (512, 256, 128, 64, 32, 16, 8):
        if r % cand == 0 and cand * c * 4 <= (1 << 20):
            tr = cand
            break

    def body(*refs):
        w_ref, qm, qs, m_ref, v_ref = refs[0], refs[1:1 + n], refs[1 + n:1 + 2 * n], refs[1 + 2 * n], refs[2 + 2 * n]
        layer = pl.program_id(0)
        g = qm[0][...] + qs[0][...]
        for k in range(1, n):
            g = jnp.where(layer == k, qm[k][...] + qs[k][...], g)
        _adamw_update(g, w_ref, m_ref, v_ref, *refs[3 + 2 * n:])

    stacked = pl.BlockSpec((None, tr, c), lambda l, i: (l, i, 0))
    per_layer = [pl.BlockSpec((tr, c), lambda l, i, k=k: (jnp.where(l == k, i, 0), 0)) for k in range(n)]
    call, tail = _call_after(
        after, body, 3 + 2 * n, name=name, grid=(n, r // tr),
        in_specs=[stacked] + per_layer + per_layer + [stacked, stacked], out_specs=[stacked] * 4,
        out_shape=[jax.ShapeDtypeStruct(w.shape, F32)] * 4,
        compiler_params=_params(2),
    )
    return call(w, *q_mine, *q_sib, m, v, *tail)


def _ada_fwd(c16, ada_w, ada_b_cols):
    cols = ada_w.shape[2]

    def body(c_ref, w_ref, b_ref, o_ref):
        cv = c_ref[...]
        ca = _bf(cv * jax.nn.sigmoid(cv))
        o_ref[...] = _dot(ca, _bf(w_ref[...])) + b_ref[...]

    return pl.pallas_call(
        body, name="ada_fwd", grid=(DEPTH,),
        in_specs=[_whole((16, D)), pl.BlockSpec((None, D, cols), lambda i: (i, 0, 0)),
                  pl.BlockSpec((None, 1, cols), lambda i: (i, 0, 0))],
        out_specs=pl.BlockSpec((None, 16, cols), lambda i: (i, 0, 0)),
        out_shape=jax.ShapeDtypeStruct((DEPTH, 16, cols), F32),
        compiler_params=_params(1),
    )(c16, ada_w, ada_b_cols)


def _ada_bwd(c16, dmod16):
    cols = dmod16.shape[2]

    def body(c_ref, d_ref, o_ref):
        cv = c_ref[...]
        ca = _bf(cv * jax.nn.sigmoid(cv))
        o_ref[...] = _dot_tn(ca, _bf(d_ref[...]))

    return pl.pallas_call(
        body, name="ada_bwd", grid=(DEPTH,),
        in_specs=[_whole((16, D)), pl.BlockSpec((None, 16, cols), lambda i: (i, 0, 0))],
        out_specs=pl.BlockSpec((None, D, cols), lambda i: (i, 0, 0)),
        out_shape=jax.ShapeDtypeStruct((DEPTH, D, cols), F32),
        compiler_params=_params(1),
    )(c16, dmod16)


def _mod_bwd(vs_mix, vs_ffn, pv):
    def body(m_ref, f_ref, pv_ref, o_ref):
        for i in range(DEPTH):
            vm, vf, p = m_ref[i], f_ref[i], pv_ref[i]
            o_ref[i] = jnp.concatenate([
                vm[2:3], vm[1:2] * p[R_N1:R_N1 + 1], vm[0:1],
                vf[2:3], vf[1:2] * p[R_N2:R_N2 + 1], vf[0:1],
                vm[1:2] * (1.0 + p[R_SC1:R_SC1 + 1]), vf[1:2] * (1.0 + p[R_SC2:R_SC2 + 1])], axis=0)

    return pl.pallas_call(body, name="mod_bwd", out_shape=jax.ShapeDtypeStruct((DEPTH, 8, D), F32))(vs_mix, vs_ffn, pv)


def _ffn_fwd(x1, pv, w1, w2, layer, tm, after=None):
    L = x1.shape[0]
    dff = w1.shape[2]

    def body(x1_ref, pv_ref, w1_ref, w2_ref, x2_ref, h2_ref, a_ref, f_ref):
        x1v, p = x1_ref[...], pv_ref[...]
        h2, _, _ = _norm_mod(x1v, p[R_N2:R_N2 + 1], p[R_SC2:R_SC2 + 1], p[R_SH2:R_SH2 + 1])
        hb = _bf(h2)
        h2_ref[...] = hb
        a = _dot(hb, w1_ref[...])
        a_ref[...] = a
        ra = jnp.maximum(a, 0.0)
        f = _dot(_bf(ra * ra), w2_ref[...])
        f_ref[...] = f
        x2_ref[...] = x1v + p[R_G2:R_G2 + 1] * f

    call, tail = _call_after(
        after, body, 4, name=f"ffn_fwd{layer}", grid=(L // tm,),
        in_specs=[_rows(tm, D), pl.BlockSpec((None, 8, D), lambda i: (layer, 0, 0)), _layer_w(D, dff, 0), _layer_w(dff, D, 0)],
        out_specs=[_rows(tm, D), _rows(tm, D), _rows(tm, dff), _rows(tm, D)],
        out_shape=[jax.ShapeDtypeStruct((L, D), F32), jax.ShapeDtypeStruct((L, D), BF16),
                   jax.ShapeDtypeStruct((L, dff), F32), jax.ShapeDtypeStruct((L, D), F32)],
        compiler_params=_params(1, 56),
    )
    return call(x1, pv, w1, w2, *tail)


def _ffn_bwd(dx2, x1, a, f, pv, w1, w2, layer, tm, after=None):
    L = x1.shape[0]
    dff = w1.shape[2]
    extra = [] if after is None else [pl.BlockSpec(memory_space=pl.ANY)]
    extra_args = [] if after is None else [after]

    def body(dx2_ref, x1_ref, a_ref, f_ref, pv_ref, w1_ref, w2_ref, *rest):
        dx1_ref, p_ref, da_ref, df_ref, vs_ref = rest[len(extra):]

        @pl.when(pl.program_id(0) == 0)
        def _():
            vs_ref[...] = jnp.zeros_like(vs_ref)

        dx2v, p = dx2_ref[...], pv_ref[...]
        dfb = _bf(dx2v * p[R_G2:R_G2 + 1])
        df_ref[...] = dfb
        vs_ref[0:1, :] += _sum0(dx2v * f_ref[...])
        dp = _dot_nt(dfb, w2_ref[...])
        ra = jnp.maximum(a_ref[...], 0.0)
        p_ref[...] = _bf(ra * ra)
        dab = _bf(dp * (2.0 * ra))
        da_ref[...] = dab
        dh2 = _dot_nt(dab, w1_ref[...])
        _, xn, r = _norm_mod(x1_ref[...], p[R_N2:R_N2 + 1], p[R_SC2:R_SC2 + 1], p[R_SH2:R_SH2 + 1])
        dx1_ref[...] = dx2v + _norm_mod_bwd(dh2, xn, r, p[R_N2:R_N2 + 1], p[R_SC2:R_SC2 + 1])
        vs_ref[1:2, :] += _sum0(dh2 * xn)
        vs_ref[2:3, :] += _sum0(dh2)

    return pl.pallas_call(
        body, name=f"ffn_bwd{layer}", grid=(L // tm,),
        in_specs=[_rows(tm, D), _rows(tm, D), _rows(tm, dff), _rows(tm, D),
                  pl.BlockSpec((None, 8, D), lambda i: (layer, 0, 0)), _layer_w(D, dff, 0), _layer_w(dff, D, 0)] + extra,
        out_specs=[_rows(tm, D), _rows(tm, dff), _rows(tm, dff), _rows(tm, D), _whole((8, D))],
        out_shape=[jax.ShapeDtypeStruct((L, D), F32), jax.ShapeDtypeStruct((L, dff), BF16),
                   jax.ShapeDtypeStruct((L, dff), BF16), jax.ShapeDtypeStruct((L, D), BF16),
                   jax.ShapeDtypeStruct((8, D), F32)],
        compiler_params=_params(1, 56),
    )(dx2, x1, a, f, pv, w1, w2, *extra_args)


def _conv_fwd(x, pv, w_in, w_out, cw, layer, j, tm, after=None):
    L = x.shape[0]

    def body(x_ref, pv_ref, win_ref, wout_ref, cw_ref, x1_ref, h_ref, bcx_ref, conv_ref, q_ref, y_ref, carry):
        @pl.when(pl.program_id(0) == 0)
        def _():
            carry[...] = jnp.zeros_like(carry)

        xv, p, cwv = x_ref[...], pv_ref[...], cw_ref[...]
        h, _, _ = _norm_mod(xv, p[R_N1:R_N1 + 1], p[R_SC1:R_SC1 + 1], p[R_SH1:R_SH1 + 1])
        hb = _bf(h)
        h_ref[...] = hb
        bcx = _dot(hb, win_ref[...])
        bcx_ref[...] = bcx
        z = bcx[:, D:2 * D] * bcx[:, 2 * D:]
        prev8 = carry[...]
        conv = cwv[0:1] * _shift_down(z, prev8, 2) + cwv[1:2] * _shift_down(z, prev8, 1) + cwv[2:3] * z + cwv[3:4]
        conv_ref[...] = conv
        qb = _bf(bcx[:, :D] * conv)
        q_ref[...] = qb
        y = _dot(qb, wout_ref[...])
        y_ref[...] = y
        x1_ref[...] = xv + p[R_G1:R_G1 + 1] * y
        carry[...] = z[tm - 8:tm]

    call, tail = _call_after(
        after, body, 5, name=f"conv_fwd{layer}", grid=(L // tm,),
        in_specs=[_rows(tm, D), pl.BlockSpec((None, 8, D), lambda i: (layer, 0, 0)), _layer_w(D, 3 * D, 0), _layer_w(D, D, 0),
                  pl.BlockSpec((None, 8, D), lambda i: (j, 0, 0))],
        out_specs=[_rows(tm, D), _rows(tm, D), _rows(tm, 3 * D), _rows(tm, D), _rows(tm, D), _rows(tm, D)],
        out_shape=[jax.ShapeDtypeStruct((L, D), F32), jax.ShapeDtypeStruct((L, D), BF16), jax.ShapeDtypeStruct((L, 3 * D), F32),
                   jax.ShapeDtypeStruct((L, D), F32), jax.ShapeDtypeStruct((L, D), BF16), jax.ShapeDtypeStruct((L, D), F32)],
        scratch_shapes=[pltpu.VMEM((8, D), F32)],
        compiler_params=_params(1, 56),
    )
    return call(x, pv, w_in, w_out, cw, *tail)


def _conv_bwd(dx1, x, y, bcx, conv, pv, w_in, w_out, cw, layer, j, tm, after=None):
    L = x.shape[0]
    nt = L // tm

    def body(dx1_ref, x_ref, y_ref, bcx_ref, conv_ref, halo_ref, pv_ref, win_ref, wout_ref, cw_ref,
             dx_ref, dbcx_ref, dy_ref, vs_ref, carry):
        gi = pl.program_id(0)
        tile = nt - 1 - gi

        @pl.when(gi == 0)
        def _():
            vs_ref[...] = jnp.zeros_like(vs_ref)
            carry[...] = jnp.zeros_like(carry)

        dx1v, p, cwv = dx1_ref[...], pv_ref[...], cw_ref[...]
        dyb = _bf(dx1v * p[R_G1:R_G1 + 1])
        dy_ref[...] = dyb
        vs_ref[0:1, :] += _sum0(dx1v * y_ref[...])
        dq = _dot_nt(dyb, wout_ref[...])
        bcx = bcx_ref[...]
        b, cg, xh = bcx[:, :D], bcx[:, D:2 * D], bcx[:, 2 * D:]
        db = dq * conv_ref[...]
        dc = dq * b
        z = cg * xh
        halo = halo_ref[...]
        zprev = jnp.where(tile > 0, halo[:, D:2 * D] * halo[:, 2 * D:], 0.0)
        vs_ref[3:4, :] += _sum0(dc * _shift_down(z, zprev, 2))
        vs_ref[4:5, :] += _sum0(dc * _shift_down(z, zprev, 1))
        vs_ref[5:6, :] += _sum0(dc * z)
        vs_ref[6:7, :] += _sum0(dc)
        next8 = carry[...]
        dz = cwv[2:3] * dc + cwv[1:2] * _shift_up(dc, next8, 1) + cwv[0:1] * _shift_up(dc, next8, 2)
        dbb, dcgb, dxhb = _bf(db), _bf(dz * xh), _bf(dz * cg)
        dbcx_ref[:, 0:D] = dbb
        dbcx_ref[:, D:2 * D] = dcgb
        dbcx_ref[:, 2 * D:3 * D] = dxhb
        dh = (_dot_nt(dbb, win_ref[:, 0:D]) + _dot_nt(dcgb, win_ref[:, D:2 * D])) + _dot_nt(dxhb, win_ref[:, 2 * D:3 * D])
        _, xn, r = _norm_mod(x_ref[...], p[R_N1:R_N1 + 1], p[R_SC1:R_SC1 + 1], p[R_SH1:R_SH1 + 1])
        dx_ref[...] = dx1v + _norm_mod_bwd(dh, xn, r, p[R_N1:R_N1 + 1], p[R_SC1:R_SC1 + 1])
        vs_ref[1:2, :] += _sum0(dh * xn)
        vs_ref[2:3, :] += _sum0(dh)
        carry[...] = dc[0:8]

    halo_spec = pl.BlockSpec((8, 3 * D), lambda i: (jnp.maximum((nt - 1 - i) * (tm // 8) - 1, 0), 0))
    call, tail = _call_after(
        after, body, 10, name=f"conv_bwd{layer}", grid=(nt,),
        in_specs=[_rows(tm, D, nt), _rows(tm, D, nt), _rows(tm, D, nt), _rows(tm, 3 * D, nt), _rows(tm, D, nt), halo_spec,
                  pl.BlockSpec((None, 8, D), lambda i: (layer, 0, 0)), _layer_w(D, 3 * D, 0), _layer_w(D, D, 0),
                  pl.BlockSpec((None, 8, D), lambda i: (j, 0, 0))],
        out_specs=[_rows(tm, D, nt), _rows(tm, 3 * D, nt), _rows(tm, D, nt), _whole((8, D))],
        out_shape=[jax.ShapeDtypeStruct((L, D), F32), jax.ShapeDtypeStruct((L, 3 * D), BF16),
                   jax.ShapeDtypeStruct((L, D), BF16), jax.ShapeDtypeStruct((8, D), F32)],
        scratch_shapes=[pltpu.VMEM((8, D), F32)],
        compiler_params=_params(1, 56),
    )
    return call(dx1, x, y, bcx, conv, bcx, pv, w_in, w_out, cw, *tail)


def _s5_discretize(a_re, a_im, log_dt, bt_re, bt_im):
    dt = jnp.exp(log_dt)
    mag = jnp.exp(a_re * dt)
    abar_re = mag * jnp.cos(a_im * dt)
    abar_im = mag * jnp.sin(a_im * dt)
    den = a_re * a_re + a_im * a_im
    nr = abar_re - 1.0
    ni = abar_im
    f_re = (nr * a_re + ni * a_im) / den
    f_im = (ni * a_re - nr * a_im) / den
    bbar_re = f_re * bt_re - f_im * bt_im
    bbar_im = f_re * bt_im + f_im * bt_re
    return abar_re, abar_im, bbar_re, bbar_im


def _s5_params_fwd(a_re, a_im, log_dt, bt_re, bt_im):
    def body(ar, ai, ld, br, bi, o_ar, o_ai, o_br, o_bi):
        r = _s5_discretize(ar[...], ai[...], ld[...], br[...], bi[...])
        o_ar[...], o_ai[...], o_br[...], o_bi[...] = r

    gp = jax.ShapeDtypeStruct((S5_G, S5_P), F32)
    hgp = jax.ShapeDtypeStruct((S5_H, S5_G, S5_P), F32)
    return pl.pallas_call(body, name="s5_params_fwd", out_shape=[gp, gp, hgp, hgp])(a_re, a_im, log_dt, bt_re, bt_im)


def _s5_params_bwd(a_re, a_im, log_dt, bt_re, bt_im, d_ar, d_ai, d_br, d_bi):
    def body(ar, ai, ld, br, bi, gar, gai, gbr, gbi, o_ar, o_ai, o_ld, o_br, o_bi):
        _, vjp = jax.vjp(_s5_discretize, ar[...], ai[...], ld[...], br[...], bi[...])
        r = vjp((gar[...], gai[...], gbr[...], gbi[...]))
        o_ar[...], o_ai[...], o_ld[...], o_br[...], o_bi[...] = r

    gp = jax.ShapeDtypeStruct((S5_G, S5_P), F32)
    hgp = jax.ShapeDtypeStruct((S5_H, S5_G, S5_P), F32)
    return pl.pallas_call(body, name="s5_params_bwd", out_shape=[gp, gp, jax.ShapeDtypeStruct((S5_G, 1), F32), hgp, hgp])(
        a_re, a_im, log_dt, bt_re, bt_im, d_ar, d_ai, d_br, d_bi)


NSEG = 8
SCAN_LANES = 1024


def _to_segments(x):
    n, c = x.shape
    return x.reshape(NSEG, n // NSEG, c).transpose(1, 0, 2).reshape(n, c)


def _from_segments(x):
    n, c = x.shape
    return x.reshape(n // NSEG, NSEG, c).transpose(1, 0, 2).reshape(n, c)


def _segment_scan(re_ref, im_ref, st_re, st_im, a_re, a_im, n_slabs, adjoint, write):
    for q in range(NSTATE // SCAN_LANES):
        ls = slice(q * SCAN_LANES, (q + 1) * SCAN_LANES)
        ar = jnp.broadcast_to(a_re[:, ls], (8, SCAN_LANES))
        ai = jnp.broadcast_to(a_im[:, ls], (8, SCAN_LANES))

        def step(k, carry, ls=ls, ar=ar, ai=ai):
            s_r, s_i = carry
            slab = (n_slabs - 1 - k) if adjoint else k
            rows = pl.ds(pl.multiple_of(slab * 8, 8), 8)
            b_r, b_i = re_ref[rows, ls], im_ref[rows, ls]
            if adjoint:
                n_r = b_r + ar * s_r + ai * s_i
                n_i = b_i - ai * s_r + ar * s_i
            else:
                n_r = ar * s_r - ai * s_i + b_r
                n_i = ar * s_i + ai * s_r + b_i
            if write:
                re_ref[rows, ls] = n_r
                im_ref[rows, ls] = n_i
            return n_r, n_i

        s_r, s_i = lax.fori_loop(0, n_slabs, step, (st_re[:, ls], st_im[:, ls]), unroll=4)
        st_re[:, ls] = s_r
        st_im[:, ls] = s_i


def _s5_segment_states(e_re, e_im, ar, ai, seg_len, adjoint):
    def body(ere_ref, eim_ref, ar_ref, ai_ref, ore_ref, oim_ref):
        p_r, p_i = ar_ref[...], ai_ref[...]
        if adjoint:
            p_i = -p_i
        acc_r, acc_i = jnp.ones_like(p_r), jnp.zeros_like(p_r)
        n = seg_len
        while n:
            if n & 1:
                acc_r, acc_i = acc_r * p_r - acc_i * p_i, acc_r * p_i + acc_i * p_r
            n >>= 1
            if n:
                p_r, p_i = p_r * p_r - p_i * p_i, 2.0 * p_r * p_i
        e_r, e_i = ere_ref[...], eim_ref[...]
        s_r, s_i = jnp.zeros_like(acc_r), jnp.zeros_like(acc_r)
        rows_r, rows_i = [None] * NSEG, [None] * NSEG
        order = range(NSEG - 1, -1, -1) if adjoint else range(NSEG)
        for j in order:
            rows_r[j], rows_i[j] = s_r, s_i
            s_r, s_i = (acc_r * s_r - acc_i * s_i + e_r[j:j + 1], acc_r * s_i + acc_i * s_r + e_i[j:j + 1])
        ore_ref[...] = jnp.concatenate(rows_r, axis=0)
        oim_ref[...] = jnp.concatenate(rows_i, axis=0)

    st = jax.ShapeDtypeStruct((NSEG, NSTATE), F32)
    return pl.pallas_call(body, name="s5_segment_states_bwd" if adjoint else "s5_segment_states_fwd", out_shape=[st, st])(
        e_re, e_im, ar, ai)


def _s5_fwd_ends(x, pv, w_in, b_re, b_im, ar, ai, layer, tm, after=None):
    L = x.shape[0]

    def body(x_ref, pv_ref, win_ref, bre_ref, bim_ref, ar_ref, ai_ref, h_ref, u_ref, ere_ref, eim_ref, bu_re, bu_im):
        @pl.when(pl.program_id(0) == 0)
        def _():
            ere_ref[...] = jnp.zeros_like(ere_ref)
            eim_ref[...] = jnp.zeros_like(eim_ref)

        p = pv_ref[...]
        h, _, _ = _norm_mod(x_ref[...], p[R_N1:R_N1 + 1], p[R_SC1:R_SC1 + 1], p[R_SH1:R_SH1 + 1])
        hb = _bf(h)
        h_ref[...] = hb
        u = _dot(hb, win_ref[...])
        u_ref[...] = u
        ub = _bf(u)
        for k in range(S5_NB):
            uk = ub[:, k * S5_BH:(k + 1) * S5_BH]
            bu_re[:, k * S5_BP:(k + 1) * S5_BP] = _dot(uk, bre_ref[k])
            bu_im[:, k * S5_BP:(k + 1) * S5_BP] = _dot(uk, bim_ref[k])
        _segment_scan(bu_re, bu_im, ere_ref, eim_ref, ar_ref[...], ai_ref[...], tm // 8, adjoint=False, write=False)

    call, tail = _call_after(
        after, body, 7, name="s5_fwd_ends", grid=(L // tm,),
        in_specs=[_rows(tm, D), pl.BlockSpec((None, 8, D), lambda i: (layer, 0, 0)), _layer_w(D, D, 0),
                  _const_w((S5_NB, S5_BH, S5_BP)), _const_w((S5_NB, S5_BH, S5_BP)), _whole((1, NSTATE)), _whole((1, NSTATE))],
        out_specs=[_rows(tm, D), _rows(tm, D), _whole((NSEG, NSTATE)), _whole((NSEG, NSTATE))],
        out_shape=[jax.ShapeDtypeStruct((L, D), BF16), jax.ShapeDtypeStruct((L, D), F32),
                   jax.ShapeDtypeStruct((NSEG, NSTATE), F32), jax.ShapeDtypeStruct((NSEG, NSTATE), F32)],
        scratch_shapes=[pltpu.VMEM((tm, NSTATE), F32), pltpu.VMEM((tm, NSTATE), F32)],
        compiler_params=_params(1, 56),
    )
    return call(x, pv, w_in, b_re, b_im, ar, ai, *tail)


def _s5_fwd_out(x, u, pv, b_re, b_im, s0_re, s0_im, ar, ai, c_re, c_im, dvec, glu_w, glu_b, w_out, layer, tm):
    L = x.shape[0]

    def body(x_ref, u_ref, pv_ref, bre_ref, bim_ref, s0re_ref, s0im_ref, ar_ref, ai_ref, cre_ref, cim_ref, d_ref, gw_ref,
             gb_ref, wout_ref, x1_ref, sre_ref, sim_ref, y1_ref, zg_ref, y3_ref, y_ref, st_re, st_im):
        @pl.when(pl.program_id(0) == 0)
        def _():
            st_re[...] = s0re_ref[...]
            st_im[...] = s0im_ref[...]

        p = pv_ref[...]
        uv = u_ref[...]
        ub = _bf(uv)
        for k in range(S5_NB):
            uk = ub[:, k * S5_BH:(k + 1) * S5_BH]
            sre_ref[:, k * S5_BP:(k + 1) * S5_BP] = _dot(uk, bre_ref[k])
            sim_ref[:, k * S5_BP:(k + 1) * S5_BP] = _dot(uk, bim_ref[k])
        _segment_scan(sre_ref, sim_ref, st_re, st_im, ar_ref[...], ai_ref[...], tm // 8, adjoint=False, write=True)
        parts = []
        for k in range(S5_NB):
            sl = slice(k * S5_BP, (k + 1) * S5_BP)
            parts.append(_dot(_bf(sre_ref[:, sl]), cre_ref[k]) - _dot(_bf(sim_ref[:, sl]), cim_ref[k]))
        y1 = jnp.concatenate(parts, axis=1) + d_ref[...] * uv
        y1_ref[...] = y1
        y2 = jax.nn.gelu(y1)
        zg = _dot(_bf(y2), gw_ref[...]) + gb_ref[...]
        zg_ref[...] = zg
        y3b = _bf(y2 * jax.nn.sigmoid(zg))
        y3_ref[...] = y3b
        y = _dot(y3b, wout_ref[...])
        y_ref[...] = y
        x1_ref[...] = x_ref[...] + p[R_G1:R_G1 + 1] * y

    return pl.pallas_call(
        body, name="s5_fwd_out", grid=(L // tm,),
        in_specs=[_rows(tm, D), _rows(tm, D), pl.BlockSpec((None, 8, D), lambda i: (layer, 0, 0)),
                  _const_w((S5_NB, S5_BH, S5_BP)), _const_w((S5_NB, S5_BH, S5_BP)),
                  _whole((NSEG, NSTATE)), _whole((NSEG, NSTATE)), _whole((1, NSTATE)), _whole((1, NSTATE)),
                  _const_w((S5_NB, S5_BP, S5_BH)), _const_w((S5_NB, S5_BP, S5_BH)), _whole((1, D)),
                  _layer_w(D, D, 0), _whole((1, D)), _layer_w(D, D, 0)],
        out_specs=[_rows(tm, D), _rows(tm, NSTATE), _rows(tm, NSTATE), _rows(tm, D), _rows(tm, D), _rows(tm, D), _rows(tm, D)],
        out_shape=[jax.ShapeDtypeStruct((L, D), F32), jax.ShapeDtypeStruct((L, NSTATE), F32), jax.ShapeDtypeStruct((L, NSTATE), F32),
                   jax.ShapeDtypeStruct((L, D), F32), jax.ShapeDtypeStruct((L, D), F32),
                   jax.ShapeDtypeStruct((L, D), BF16), jax.ShapeDtypeStruct((L, D), F32)],
        scratch_shapes=[pltpu.VMEM((NSEG, NSTATE), F32), pltpu.VMEM((NSEG, NSTATE), F32)],
        compiler_params=_params(1, 56),
    )(x, u, pv, b_re, b_im, s0_re, s0_im, ar, ai, c_re, c_im, dvec, glu_w, glu_b, w_out)


def _s5_bwd_ends(dx1, y, y1, zg, u, pv, c_re, c_im, ar, ai, dvec, glu_w, w_out, layer, tm, after=None):
    L = dx1.shape[0]
    nt = L // tm

    def body(dx1_ref, y_ref, y1_ref, zg_ref, u_ref, pv_ref, cre_ref, cim_ref, ar_ref, ai_ref, d_ref, gw_ref, wout_ref,
             dy_ref, y2_ref, dzg_ref, dy1_ref, dus_ref, ere_ref, eim_ref, vs_ref, g_re, g_im):
        @pl.when(pl.program_id(0) == 0)
        def _():
            vs_ref[...] = jnp.zeros_like(vs_ref)
            ere_ref[...] = jnp.zeros_like(ere_ref)
            eim_ref[...] = jnp.zeros_like(eim_ref)

        dx1v, p = dx1_ref[...], pv_ref[...]
        dyb = _bf(dx1v * p[R_G1:R_G1 + 1])
        dy_ref[...] = dyb
        vs_ref[0:1, :] += _sum0(dx1v * y_ref[...])
        dy3 = _dot_nt(dyb, wout_ref[...])
        y2, gelu_vjp = jax.vjp(jax.nn.gelu, y1_ref[...])
        y2_ref[...] = _bf(y2)
        gate = jax.nn.sigmoid(zg_ref[...])
        dzg = dy3 * y2 * gate * (1.0 - gate)
        dzgb = _bf(dzg)
        dzg_ref[...] = dzgb
        vs_ref[1:2, :] += _sum0(dzg)
        dy2 = dy3 * gate + _dot_nt(dzgb, gw_ref[...])
        dy1 = gelu_vjp(dy2)[0]
        vs_ref[2:3, :] += _sum0(dy1 * u_ref[...])
        dus_ref[...] = dy1 * d_ref[...]
        dy1b = _bf(dy1)
        dy1_ref[...] = dy1b
        for k in range(S5_NB):
            dk = dy1b[:, k * S5_BH:(k + 1) * S5_BH]
            g_re[:, k * S5_BP:(k + 1) * S5_BP] = _dot_nt(dk, cre_ref[k])
            g_im[:, k * S5_BP:(k + 1) * S5_BP] = -_dot_nt(dk, cim_ref[k])
        _segment_scan(g_re, g_im, ere_ref, eim_ref, ar_ref[...], ai_ref[...], tm // 8, adjoint=True, write=False)

    call, tail = _call_after(
        after, body, 13, name="s5_bwd_ends", grid=(nt,),
        in_specs=[_rows(tm, D, nt)] * 5 + [pl.BlockSpec((None, 8, D), lambda i: (layer, 0, 0)),
                  _const_w((S5_NB, S5_BP, S5_BH)), _const_w((S5_NB, S5_BP, S5_BH)), _whole((1, NSTATE)), _whole((1, NSTATE)),
                  _whole((1, D)), _layer_w(D, D, 0), _layer_w(D, D, 0)],
        out_specs=[_rows(tm, D, nt)] * 5 + [_whole((NSEG, NSTATE)), _whole((NSEG, NSTATE)), _whole((8, D))],
        out_shape=[jax.ShapeDtypeStruct((L, D), BF16)] * 4 + [jax.ShapeDtypeStruct((L, D), F32),
                   jax.ShapeDtypeStruct((NSEG, NSTATE), F32), jax.ShapeDtypeStruct((NSEG, NSTATE), F32),
                   jax.ShapeDtypeStruct((8, D), F32)],
        scratch_shapes=[pltpu.VMEM((tm, NSTATE), F32), pltpu.VMEM((tm, NSTATE), F32)],
        compiler_params=_params(1, 56),
    )
    return call(dx1, y, y1, zg, u, pv, c_re, c_im, ar, ai, dvec, glu_w, w_out, *tail)


def _s5_bwd_in(dx1, dy1_b, du_skip, x, s_re, s_im, pv, b_re, b_im, c_re, c_im, l0_re, l0_im, ar, ai, w_in, layer, tm):
    L = x.shape[0]
    nt = L // tm

    def body(dx1_ref, dy1_ref, dus_ref, x_ref, sre_ref, sim_ref, hre_ref, him_ref, lre_ref, lim_ref, pv_ref, bre_ref, bim_ref,
             cre_ref, cim_ref, l0re_ref, l0im_ref, ar_ref, ai_ref, win_ref,
             dx_ref, du_ref, lamre_ref, lamim_ref, da_ref, vs_ref, g_re, g_im, st_re, st_im):
        gi = pl.program_id(0)
        tile = nt - 1 - gi

        @pl.when(gi == 0)
        def _():
            vs_ref[...] = jnp.zeros_like(vs_ref)
            da_ref[...] = jnp.zeros_like(da_ref)
            st_re[...] = l0re_ref[...]
            st_im[...] = l0im_ref[...]

        p = pv_ref[...]
        dy1b = dy1_ref[...]
        for k in range(S5_NB):
            dk = dy1b[:, k * S5_BH:(k + 1) * S5_BH]
            g_re[:, k * S5_BP:(k + 1) * S5_BP] = _dot_nt(dk, cre_ref[k])
            g_im[:, k * S5_BP:(k + 1) * S5_BP] = -_dot_nt(dk, cim_ref[k])
        _segment_scan(g_re, g_im, st_re, st_im, ar_ref[...], ai_ref[...], tm // 8, adjoint=True, write=True)
        lam_r, lam_i = g_re[...], g_im[...]
        lrb, lib = _bf(lam_r), _bf(lam_i)
        lamre_ref[...] = lrb
        lamim_ref[...] = lib

        def wrapped(last_ref):
            z = last_ref[...]
            row = lax.broadcasted_iota(jnp.int32, z.shape, 0)
            return jnp.where(row >= 1, pltpu.roll(z, 1, 0), 0.0)

        first_r = jnp.where(tile > 0, hre_ref[...], wrapped(lre_ref))
        first_i = jnp.where(tile > 0, him_ref[...], wrapped(lim_ref))
        sp_r = jnp.concatenate([first_r, sre_ref[0:tm - 8, :]], axis=0)
        sp_i = jnp.concatenate([first_i, sim_ref[0:tm - 8, :]], axis=0)
        da_ref[0:1, :] += _sum0(lam_r * sp_r + lam_i * sp_i)
        da_ref[1:2, :] += _sum0(lam_i * sp_r - lam_r * sp_i)

        parts = []
        for k in range(S5_NB):
            sl = slice(k * S5_BP, (k + 1) * S5_BP)
            parts.append(_dot_nt(lrb[:, sl], bre_ref[k]) + _dot_nt(lib[:, sl], bim_ref[k]))
        dub = _bf(jnp.concatenate(parts, axis=1) + dus_ref[...])
        du_ref[...] = dub
        dh = _dot_nt(dub, win_ref[...])
        _, xn, r = _norm_mod(x_ref[...], p[R_N1:R_N1 + 1], p[R_SC1:R_SC1 + 1], p[R_SH1:R_SH1 + 1])
        dx_ref[...] = dx1_ref[...] + _norm_mod_bwd(dh, xn, r, p[R_N1:R_N1 + 1], p[R_SC1:R_SC1 + 1])
        vs_ref[1:2, :] += _sum0(dh * xn)
        vs_ref[2:3, :] += _sum0(dh)

    halo = pl.BlockSpec((8, NSTATE), lambda i: (jnp.maximum((nt - 1 - i) * (tm // 8) - 1, 0), 0))
    last = pl.BlockSpec((8, NSTATE), lambda i: (L // 8 - 1, 0))
    return pl.pallas_call(
        body, name="s5_bwd_in", grid=(nt,),
        in_specs=[_rows(tm, D, nt), _rows(tm, D, nt), _rows(tm, D, nt), _rows(tm, D, nt), _rows(tm, NSTATE, nt), _rows(tm, NSTATE, nt),
                  halo, halo, last, last, pl.BlockSpec((None, 8, D), lambda i: (layer, 0, 0)),
                  _const_w((S5_NB, S5_BH, S5_BP)), _const_w((S5_NB, S5_BH, S5_BP)),
                  _const_w((S5_NB, S5_BP, S5_BH)), _const_w((S5_NB, S5_BP, S5_BH)),
                  _whole((NSEG, NSTATE)), _whole((NSEG, NSTATE)), _whole((1, NSTATE)), _whole((1, NSTATE)), _layer_w(D, D, 0)],
        out_specs=[_rows(tm, D, nt), _rows(tm, D, nt), _rows(tm, NSTATE, nt), _rows(tm, NSTATE, nt), _whole((8, NSTATE)), _whole((8, D))],
        out_shape=[jax.ShapeDtypeStruct((L, D), F32), jax.ShapeDtypeStruct((L, D), BF16),
                   jax.ShapeDtypeStruct((L, NSTATE), BF16), jax.ShapeDtypeStruct((L, NSTATE), BF16),
                   jax.ShapeDtypeStruct((8, NSTATE), F32), jax.ShapeDtypeStruct((8, D), F32)],
        scratch_shapes=[pltpu.VMEM((tm, NSTATE), F32), pltpu.VMEM((tm, NSTATE), F32),
                        pltpu.VMEM((NSEG, NSTATE), F32), pltpu.VMEM((NSEG, NSTATE), F32)],
        compiler_params=_params(1, 60),
    )(dx1, dy1_b, du_skip, x, s_re, s_im, s_re, s_im, s_re, s_im, pv, b_re, b_im, c_re, c_im, l0_re, l0_im, ar, ai, w_in)


def _blockdiag_b(bt):
    b = bt.reshape(S5_H, S5_NB, 16, S5_P).transpose(1, 2, 0, 3)
    eye = jnp.eye(16, dtype=bt.dtype)
    return (b[:, :, :, None, :] * eye[None, :, None, :, None]).reshape(S5_NB, S5_BH, S5_BP)


def _unblock_b(d):
    d = jnp.einsum("bghgp->bghp", d.reshape(S5_NB, 16, S5_H, 16, S5_P))
    return d.transpose(2, 0, 1, 3).reshape(S5_H, S5_G, S5_P)


def _blockdiag_c(cm):
    c4 = cm.reshape(S5_NB, 16, S5_H, S5_P)
    eye = jnp.eye(16, dtype=cm.dtype)
    out = c4.transpose(0, 1, 3, 2)[:, :, :, None, :] * eye[None, :, None, :, None]
    return out.reshape(S5_NB, S5_BP, S5_BH)


def _unblock_c(d):
    d = jnp.einsum("bgpgh->bghp", d.reshape(S5_NB, 16, S5_P, 16, S5_H))
    return d.reshape(S5_G, S5_H, S5_P)


def _tril_mask():
    return lax.broadcasted_iota(jnp.int32, (SG_CHUNK, SG_CHUNK), 0) >= lax.broadcasted_iota(jnp.int32, (SG_CHUNK, SG_CHUNK), 1)


def _sg_fwd(x, pv, w_in, w_s, b_t, vg, w_out, layer, tm, after=None):
    L = x.shape[0]
    nc = tm // SG_CHUNK

    def body(x_ref, pv_ref, win_ref, ws_ref, bt_ref, vg_ref, wout_ref, x1_ref, h_ref, uv_ref, vm_ref, q_ref, y_ref):
        xv, p = x_ref[...], pv_ref[...]
        h, _, _ = _norm_mod(xv, p[R_N1:R_N1 + 1], p[R_SC1:R_SC1 + 1], p[R_SH1:R_SH1 + 1])
        hb = _bf(h)
        h_ref[...] = hb
        uv = _dot(hb, win_ref[...])
        uv_ref[...] = uv
        v = uv[:, D:]
        rv = lax.rsqrt(jnp.mean(v * v, axis=-1, keepdims=True) + EPS)
        vnb = _bf((v * rv) * vg_ref[...])
        mask = _tril_mask()
        bt = bt_ref[...]
        for hd in range(SG_HEADS):
            wm = _bf(jnp.where(mask, ws_ref[hd], 0.0))
            cs = slice(hd * SG_CHUNK, (hd + 1) * SG_CHUNK)
            for ck in range(nc):
                rs = slice(ck * SG_CHUNK, (ck + 1) * SG_CHUNK)
                vm_ref[rs, cs] = _dot(wm, vnb[rs, cs]) + bt[:, hd:hd + 1]
        qb = _bf(uv[:, :D] * vm_ref[...])
        q_ref[...] = qb
        y = _dot(qb, wout_ref[...])
        y_ref[...] = y
        x1_ref[...] = xv + p[R_G1:R_G1 + 1] * y

    call, tail = _call_after(
        after, body, 7, name="sg_fwd", grid=(L // tm,),
        in_specs=[_rows(tm, D), pl.BlockSpec((None, 8, D), lambda i: (layer, 0, 0)), _layer_w(D, 2 * D, 0),
                  _whole((SG_HEADS, SG_CHUNK, SG_CHUNK)), _whole((SG_CHUNK, SG_HEADS)), _whole((1, D)), _layer_w(D, D, 0)],
        out_specs=[_rows(tm, D), _rows(tm, D), _rows(tm, 2 * D), _rows(tm, D), _rows(tm, D), _rows(tm, D)],
        out_shape=[jax.ShapeDtypeStruct((L, D), F32), jax.ShapeDtypeStruct((L, D), BF16), jax.ShapeDtypeStruct((L, 2 * D), F32),
                   jax.ShapeDtypeStruct((L, D), F32), jax.ShapeDtypeStruct((L, D), BF16), jax.ShapeDtypeStruct((L, D), F32)],
        compiler_params=_params(1, 56),
    )
    return call(x, pv, w_in, w_s, b_t, vg, w_out, *tail)


def _sg_bwd(dx1, x, y, uv, vm, pv, w_in, w_s, vg, w_out, layer, tm, after=None):
    L = x.shape[0]
    nc = tm // SG_CHUNK

    def body(dx1_ref, x_ref, y_ref, uv_ref, vm_ref, pv_ref, win_ref, ws_ref, vg_ref, wout_ref,
             dx_ref, duv_ref, dy_ref, vs_ref, dws_ref, dbt_ref, dvn_scr):
        @pl.when(pl.program_id(0) == 0)
        def _():
            vs_ref[...] = jnp.zeros_like(vs_ref)
            dws_ref[...] = jnp.zeros_like(dws_ref)
            dbt_ref[...] = jnp.zeros_like(dbt_ref)

        dx1v, p = dx1_ref[...], pv_ref[...]
        dyb = _bf(dx1v * p[R_G1:R_G1 + 1])
        dy_ref[...] = dyb
        vs_ref[0:1, :] += _sum0(dx1v * y_ref[...])
        dq = _dot_nt(dyb, wout_ref[...])
        uv = uv_ref[...]
        u, v = uv[:, :D], uv[:, D:]
        dub = _bf(dq * vm_ref[...])
        dvm = dq * u
        dvmb = _bf(dvm)
        rv = lax.rsqrt(jnp.mean(v * v, axis=-1, keepdims=True) + EPS)
        vh = v * rv
        vgv = vg_ref[...]
        vnb = _bf(vh * vgv)
        mask = _tril_mask()
        for hd in range(SG_HEADS):
            wm = _bf(jnp.where(mask, ws_ref[hd], 0.0))
            cs = slice(hd * SG_CHUNK, (hd + 1) * SG_CHUNK)
            dws = jnp.zeros((SG_CHUNK, SG_CHUNK), F32)
            dbs = jnp.zeros((SG_CHUNK, 1), F32)
            for ck in range(nc):
                rs = slice(ck * SG_CHUNK, (ck + 1) * SG_CHUNK)
                dvn_scr[rs, cs] = _dot_tn(wm, dvmb[rs, cs])
                dws = dws + _dot_nt(dvmb[rs, cs], vnb[rs, cs])
                dbs = dbs + jnp.sum(dvm[rs, cs], axis=1, keepdims=True)
            dws_ref[hd] += jnp.where(mask, dws, 0.0)
            dbt_ref[:, hd:hd + 1] += dbs
        dvn = dvn_scr[...]
        vs_ref[3:4, :] += _sum0(dvn * vh)
        dvnn = dvn * vgv
        dvb = _bf(rv * (dvnn - vh * jnp.mean(dvnn * vh, axis=-1, keepdims=True)))
        duv_ref[:, 0:D] = dub
        duv_ref[:, D:2 * D] = dvb
        dh = _dot_nt(dub, win_ref[:, 0:D]) + _dot_nt(dvb, win_ref[:, D:2 * D])
        _, xn, r = _norm_mod(x_ref[...], p[R_N1:R_N1 + 1], p[R_SC1:R_SC1 + 1], p[R_SH1:R_SH1 + 1])
        dx_ref[...] = dx1v + _norm_mod_bwd(dh, xn, r, p[R_N1:R_N1 + 1], p[R_SC1:R_SC1 + 1])
        vs_ref[1:2, :] += _sum0(dh * xn)
        vs_ref[2:3, :] += _sum0(dh)

    call, tail = _call_after(
        after, body, 10, name="sg_bwd", grid=(L // tm,),
        in_specs=[_rows(tm, D), _rows(tm, D), _rows(tm, D), _rows(tm, 2 * D), _rows(tm, D),
                  pl.BlockSpec((None, 8, D), lambda i: (layer, 0, 0)), _layer_w(D, 2 * D, 0),
                  _whole((SG_HEADS, SG_CHUNK, SG_CHUNK)), _whole((1, D)), _layer_w(D, D, 0)],
        out_specs=[_rows(tm, D), _rows(tm, 2 * D), _rows(tm, D), _whole((8, D)),
                   _whole((SG_HEADS, SG_CHUNK, SG_CHUNK)), _whole((SG_CHUNK, SG_HEADS))],
        out_shape=[jax.ShapeDtypeStruct((L, D), F32), jax.ShapeDtypeStruct((L, 2 * D), BF16), jax.ShapeDtypeStruct((L, D), BF16),
                   jax.ShapeDtypeStruct((8, D), F32), jax.ShapeDtypeStruct((SG_HEADS, SG_CHUNK, SG_CHUNK), F32),
                   jax.ShapeDtypeStruct((SG_CHUNK, SG_HEADS), F32)],
        scratch_shapes=[pltpu.VMEM((tm, D), F32)],
        compiler_params=_params(1, 56),
    )
    return call(dx1, x, y, uv, vm, pv, w_in, w_s, vg, w_out, *tail)


def _final(x, target, fg, tm):
    L = x.shape[0]

    def body(x_ref, t_ref, g_ref, dx_ref, vs_ref):
        @pl.when(pl.program_id(0) == 0)
        def _():
            vs_ref[...] = jnp.zeros_like(vs_ref)

        xv, g = x_ref[...], g_ref[...]
        r = lax.rsqrt(jnp.mean(xv * xv, axis=-1, keepdims=True) + EPS)
        xn = xv * r
        e = xn * g - t_ref[...]
        vs_ref[0:1, :] += jnp.sum(e * e)
        dout = e * (1.0 / D)
        vs_ref[1:2, :] += _sum0(dout * xn)
        dxn = dout * g
        dx_ref[...] = r * (dxn - xn * jnp.mean(dxn * xn, axis=-1, keepdims=True))

    return pl.pallas_call(
        body, name="final_loss", grid=(L // tm,),
        in_specs=[_rows(tm, D), _rows(tm, D), _whole((1, D))],
        out_specs=[_rows(tm, D), _whole((8, D))],
        out_shape=[jax.ShapeDtypeStruct((L, D), F32), jax.ShapeDtypeStruct((8, D), F32)],
        compiler_params=_params(1),
    )(x, target, fg)


def _pack_flat(arrs, multiple=LANES):
    flat = jnp.concatenate([a.reshape(-1).astype(F32) for a in arrs])
    return jnp.pad(flat, (0, -flat.shape[0] % multiple))


def _pack(arrs, row_multiple=8):
    return _pack_flat(arrs, row_multiple * LANES).reshape(-1, LANES)


def _unpack(buf, shapes, lead=()):
    flat = buf.reshape(lead + (-1,))
    out, off = [], 0
    for s in shapes:
        n = 1
        for d in s:
            n *= d
        out.append(flat[..., off:off + n].reshape(lead + tuple(s)))
        off += n
    return out


BIG = ("ff_w1", "ff_w2", "conv_w_in", "conv_w_out", "ssm_w_in", "ssm_glu_w", "ssm_w_out", "sg_w_in", "sg_w_out")
BIG_AXIS = {"ff_w1": 2, "ff_w2": 1, "conv_w_in": 2, "conv_w_out": 1, "ssm_w_in": 1, "ssm_glu_w": 1, "ssm_w_out": 1,
            "sg_w_in": 2, "sg_w_out": 1}
LAYER_WEIGHTS = (
    (("conv_w_in", 0), ("conv_w_out", 0), ("ff_w1", 0), ("ff_w2", 0)),
    (("ssm_w_in", 0), ("ssm_glu_w", 0), ("ssm_w_out", 0), ("ff_w1", 1), ("ff_w2", 1)),
    (("sg_w_in", 0), ("sg_w_out", 0), ("ff_w1", 2), ("ff_w2", 2)),
    (("conv_w_in", 1), ("conv_w_out", 1), ("ff_w1", 3), ("ff_w2", 3)),
)
GATHER_GROUPS = tuple(grp for lw in LAYER_WEIGHTS for grp in (lw[:-2], lw[-2:]))
SMALL_SHARDED = ("conv_w", "conv_b", "sg_v_g")
SMALL_WIDE_PADDED = ("ssm_b_re", "ssm_b_im")
SMALL = ("ada_b", "norm1_g", "norm2_g", "final_g", "ssm_a_re", "ssm_a_im", "ssm_log_dt", "ssm_b_re", "ssm_b_im", "ssm_c_re",
         "ssm_c_im", "ssm_d", "ssm_glu_b", "sg_w_s", "sg_b_s") + SMALL_SHARDED
WEIGHTS = ("ada_w", "ada_b", "norm1_g", "norm2_g", "ff_w1", "ff_w2", "final_g", "conv_w_in", "conv_w", "conv_b", "conv_w_out",
           "ssm_w_in", "ssm_a_re", "ssm_a_im", "ssm_log_dt", "ssm_b_re", "ssm_b_im", "ssm_c_re", "ssm_c_im", "ssm_d",
           "ssm_glu_w", "ssm_glu_b", "ssm_w_out", "sg_w_in", "sg_v_g", "sg_w_s", "sg_b_s", "sg_w_out")


def kernel(x, c, ada_w, ada_b, norm1_g, norm2_g, ff_w1, ff_w2, final_g, conv_w_in, conv_w, conv_b, conv_w_out, ssm_w_in, ssm_a_re, ssm_a_im, ssm_log_dt, ssm_b_re, ssm_b_im, ssm_c_re, ssm_c_im, ssm_d, ssm_glu_w, ssm_glu_b, ssm_w_out, sg_w_in, sg_v_g, sg_w_s, sg_b_s, sg_w_out, loss_target, m_ada_w, m_ada_b, m_norm1_g, m_norm2_g, m_ff_w1, m_ff_w2, m_final_g, m_conv_w_in, m_conv_w, m_conv_b, m_conv_w_out, m_ssm_w_in, m_ssm_a_re, m_ssm_a_im, m_ssm_log_dt, m_ssm_b_re, m_ssm_b_im, m_ssm_c_re, m_ssm_c_im, m_ssm_d, m_ssm_glu_w, m_ssm_glu_b, m_ssm_w_out, m_sg_w_in, m_sg_v_g, m_sg_w_s, m_sg_b_s, m_sg_w_out, v_ada_w, v_ada_b, v_norm1_g, v_norm2_g, v_ff_w1, v_ff_w2, v_final_g, v_conv_w_in, v_conv_w, v_conv_b, v_conv_w_out, v_ssm_w_in, v_ssm_a_re, v_ssm_a_im, v_ssm_log_dt, v_ssm_b_re, v_ssm_b_im, v_ssm_c_re, v_ssm_c_im, v_ssm_d, v_ssm_glu_w, v_ssm_glu_b, v_ssm_w_out, v_sg_w_in, v_sg_v_g, v_sg_w_s, v_sg_b_s, v_sg_w_out):
    args = dict(locals())
    w = {n: args[n] for n in WEIGHTS}
    m = {n: args["m_" + n] for n in WEIGHTS}
    v = {n: args["v_" + n] for n in WEIGHTS}
    L = x.shape[1]
    tm = min(L, 256)
    tm2 = min(L, 512)
    chip = 2 * lax.axis_index("x") + lax.axis_index("y")
    me = 2 * chip + lax.axis_index("c")
    xin = x[0]
    target = loss_target[0]
    chip1 = chip.reshape(1).astype(jnp.int32)
    place = jnp.stack([chip, lax.axis_index("c")]).astype(jnp.int32)

    gathers = []

    def start_gather(g, after):
        entries = GATHER_GROUPS[g]
        axes = [BIG_AXIS[n] for n, _ in entries]
        lands = _cast_place([w[n] for n, _ in entries], [li for _, li in entries], axes, place, f"cast_group{g}")
        s_sems, r_sems, lands, token = _gather_start(lands, axes, f"gather_start{g}", after)
        gathers.append((s_sems, r_sems, lands, axes))
        return token

    def weights_of(g, after):
        s_sems, r_sems, lands, axes = gathers[g]
        lands = _gather_wait(s_sems, r_sems, lands, axes, f"gather_wait{g}", after)
        lands = _gather_share(lands, axes, f"gather_share{g}")
        token = start_gather(g + 2, lands[0]) if g + 2 < len(GATHER_GROUPS) else None
        return dict(zip([n for n, _ in GATHER_GROUPS[g]], lands)), token

    small_in = _pack([c, conv_w, conv_b, sg_v_g])
    got = _allgather_small(small_in, "gather_small_inputs").reshape(N_DEV, -1)
    c_all, cw_sh, cb_sh, vg_sh = _unpack(got, [(D,), conv_w.shape, conv_b.shape, sg_v_g.shape], lead=(N_DEV,))
    conv_w_full = jnp.concatenate([cw_sh[2 * k] for k in range(4)], axis=-1)
    conv_b_full = jnp.concatenate([cb_sh[2 * k] for k in range(4)], axis=-1)
    vg_full = jnp.concatenate([vg_sh[2 * k] for k in range(4)], axis=-1)
    c16 = jnp.pad(c_all, ((0, 16 - N_DEV), (0, 0)))

    cols = ada_w.shape[2]
    ada_b_cols = lax.dynamic_slice_in_dim(ada_b, chip * cols, cols, axis=1)[:, None, :]
    mod_sh = _ada_fwd(c16, ada_w, ada_b_cols)[:, :N_DEV, :]
    mod_all = _allgather_small(_pack([mod_sh]), "gather_mod").reshape(N_DEV, -1)
    mod_all = _unpack(mod_all, [mod_sh.shape], lead=(N_DEV,))[0]
    mod_mine = lax.dynamic_index_in_dim(mod_all[0::2], me, axis=2, keepdims=False)
    mod_mine = mod_mine.transpose(1, 0, 2).reshape(DEPTH, 6, D)
    pv = jnp.concatenate([mod_mine, norm1_g[:, None, :], norm2_g[:, None, :]], axis=1)

    start_gather(1, start_gather(0, pv))

    cw_rows = jnp.concatenate([conv_w_full, conv_b_full[:, None, :], jnp.zeros((conv_w_full.shape[0], 4, D), F32)], axis=1)

    a_re, a_im = ssm_a_re[0], ssm_a_im[0]
    log_dt = ssm_log_dt[0][:, None]
    bt_re, bt_im = ssm_b_re[0].transpose(2, 0, 1), ssm_b_im[0].transpose(2, 0, 1)
    abar_re, abar_im, bbar_re, bbar_im = _s5_params_fwd(a_re, a_im, log_dt, bt_re, bt_im)
    ar_vec, ai_vec = abar_re.reshape(1, NSTATE), abar_im.reshape(1, NSTATE)
    bd_re, bd_im = _bf(_blockdiag_b(bbar_re)), _bf(_blockdiag_b(bbar_im))
    cd_re, cd_im = _bf(_blockdiag_c(ssm_c_re[0])), _bf(_blockdiag_c(ssm_c_im[0]))

    saved = []
    fulls = []
    xl = xin
    for i in range(DEPTH):
        kind = MIXER_OF_LAYER[i]
        j = i // 3
        full, tok = weights_of(2 * i, cd_im if i == 0 else xl)
        fulls.append(full)
        if kind == 0:
            x1, h, bcx, conv, q, y = _conv_fwd(xl, pv, full["conv_w_in"], full["conv_w_out"], cw_rows, i, j, tm2, after=tok)
            mix = dict(h=h, bcx=bcx, conv=conv, q=q, y=y)
        elif kind == 1:
            xp = _to_segments(xl)
            h, u, e_re, e_im = _s5_fwd_ends(xp, pv, full["ssm_w_in"], bd_re, bd_im, ar_vec, ai_vec, i, tm, after=tok)
            s0_re, s0_im = _s5_segment_states(e_re, e_im, ar_vec, ai_vec, L // NSEG, adjoint=False)
            x1p, s_re, s_im, y1, zg, y3, y = _s5_fwd_out(xp, u, pv, bd_re, bd_im, s0_re, s0_im, ar_vec, ai_vec, cd_re, cd_im,
                                                         ssm_d, full["ssm_glu_w"], ssm_glu_b, full["ssm_w_out"], i, tm)
            x1 = _from_segments(x1p)
            mix = dict(xp=xp, h=h, u=u, s_re=s_re, s_im=s_im, y1=y1, zg=zg, y3=y3, y=y)
        else:
            x1, h, uv, vm, q, y = _sg_fwd(xl, pv, full["sg_w_in"], sg_w_s[0], sg_b_s[0].T, vg_full, full["sg_w_out"], i, tm2,
                                          after=tok)
            mix = dict(h=h, uv=uv, vm=vm, q=q, y=y)
        ffn_weights, tok = weights_of(2 * i + 1, x1)
        full.update(ffn_weights)
        x2, h2, a, f = _ffn_fwd(x1, pv, full["ff_w1"], full["ff_w2"], i, tm2, after=tok)
        saved.append(dict(x=xl, x1=x1, h2=h2, a=a, f=f, **mix))
        xl = x2

    dxl, vs_fin = _final(xl, target, final_g[None, :], tm)

    gfull = {n: [None] * w[n].shape[0] for n in BIG}
    vs_mix, vs_ffn = [None] * DEPTH, [None] * DEPTH
    small_g = {}
    scatters = {}
    token = None

    def start_scatter(key, entries, after):
        garrs = [gfull[n][li][None] for n, li in entries]
        gaxes = [BIG_AXIS[n] for n, _ in entries]
        s_sems, r_sems, garrs, lands, tok = _scatter_start(garrs, gaxes, f"scatter_start{key}", after)
        scatters[key] = (s_sems, r_sems, garrs, lands, gaxes, entries)
        return tok

    for i in reversed(range(DEPTH)):
        kind = MIXER_OF_LAYER[i]
        j = i // 3
        sv = saved[i]
        full = fulls[i]
        dx1, p_b, da_b, df_b, vs_ffn[i] = _ffn_bwd(dxl, sv["x1"], sv["a"], sv["f"], pv, full["ff_w1"], full["ff_w2"], i, tm,
                                                   after=token)
        gfull["ff_w1"][i] = _mm_tn(sv["h2"], da_b, f"wgrad_ff_w1_{i}")
        gfull["ff_w2"][i] = _mm_tn(p_b, df_b, f"wgrad_ff_w2_{i}")
        if i == 0:
            token = start_scatter("0f", LAYER_WEIGHTS[0][2:], dx1)
        if kind == 0:
            dxl, dbcx_b, dy_b, vsm = _conv_bwd(dx1, sv["x"], sv["y"], sv["bcx"], sv["conv"], pv, full["conv_w_in"],
                                               full["conv_w_out"], cw_rows, i, j, tm2, after=token if i == 0 else None)
            gfull["conv_w_in"][j] = _mm_tn(sv["h"], dbcx_b, f"wgrad_conv_w_in_{j}")
            gfull["conv_w_out"][j] = _mm_tn(sv["q"], dy_b, f"wgrad_conv_w_out_{j}")
            small_g.setdefault("conv_w", [None, None])[j] = vsm[3:6]
            small_g.setdefault("conv_b", [None, None])[j] = vsm[6]
        elif kind == 1:
            dx1p = _to_segments(dx1)
            dy_b, y2_b, dzg_b, dy1_b, du_skip, eb_re, eb_im, vsm = _s5_bwd_ends(
                dx1p, sv["y"], sv["y1"], sv["zg"], sv["u"], pv, cd_re, cd_im, ar_vec, ai_vec, ssm_d, full["ssm_glu_w"],
                full["ssm_w_out"], i, tm)
            l0_re, l0_im = _s5_segment_states(eb_re, eb_im, ar_vec, ai_vec, L // NSEG, adjoint=True)
            dxp, du_b, lam_re, lam_im, dabar, vs_in = _s5_bwd_in(
                dx1p, dy1_b, du_skip, sv["xp"], sv["s_re"], sv["s_im"], pv, bd_re, bd_im, cd_re, cd_im, l0_re, l0_im,
                ar_vec, ai_vec, full["ssm_w_in"], i, tm)
            dxl = _from_segments(dxp)
            gfull["ssm_w_out"][0] = _mm_tn(sv["y3"], dy_b, "wgrad_ssm_w_out")
            gfull["ssm_glu_w"][0] = _mm_tn(y2_b, dzg_b, "wgrad_ssm_glu_w")
            gfull["ssm_w_in"][0] = _mm_tn(sv["h"], du_b, "wgrad_ssm_w_in")
            s5_late = dict(s_re=sv["s_re"], s_im=sv["s_im"], u=sv["u"], dy1_b=dy1_b, lam_re=lam_re, lam_im=lam_im, dabar=dabar)
            small_g.update(ssm_d=vsm[2], ssm_glu_b=vsm[1])
            vsm = jnp.concatenate([vsm[0:1], vs_in[1:3], jnp.zeros((5, D), F32)], axis=0)
        else:
            dxl, duv_b, dy_b, vsm, d_ws, d_bt = _sg_bwd(dx1, sv["x"], sv["y"], sv["uv"], sv["vm"], pv, full["sg_w_in"],
                                                        sg_w_s[0], vg_full, full["sg_w_out"], i, tm2)
            gfull["sg_w_in"][0] = _mm_tn(sv["h"], duv_b, "wgrad_sg_w_in")
            gfull["sg_w_out"][0] = _mm_tn(sv["q"], dy_b, "wgrad_sg_w_out")
            small_g.update(sg_v_g=vsm[3], sg_w_s=d_ws, sg_b_s=d_bt.T)
        vs_mix[i] = vsm
        token = start_scatter(str(i), LAYER_WEIGHTS[i], dxl) if i > 0 else start_scatter("0c", LAYER_WEIGHTS[0][:2], dxl)
    grad_x = dxl[None]

    sums = {n: [None] * w[n].shape[0] for n in BIG}

    def collect(key, after):
        s_sems, r_sems, garrs, lands, gaxes, entries = scatters[key]
        garrs, recv = _scatter_wait(s_sems, r_sems, garrs, lands, gaxes, f"scatter_wait{key}", after)
        for (n, li), t in zip(entries, _sum_parts(recv, garrs, gaxes, chip1, f"sum_group{key}")):
            sums[n][li] = t
        return sums[entries[-1][0]][entries[-1][1]]

    after = token
    for key in ("3", "2", "1"):
        after = collect(key, after)
    early = [(n, li) for i in (3, 2, 1) for n, li in LAYER_WEIGHTS[i]]
    late = list(LAYER_WEIGHTS[0][2:]) + list(LAYER_WEIGHTS[0][:2])
    s_sems, r_sems, mine_thru, lands, tok = _swap_start([sums[n][li] for n, li in early], "swap_start_early", after)

    blocks = dict(
        c_re=_mm_tn_blocks(s5_late["s_re"], s5_late["dy1_b"], S5_BP, S5_BH, "wgrad_s5_c_re", after=tok),
        c_im=_mm_tn_blocks(s5_late["s_im"], s5_late["dy1_b"], S5_BP, S5_BH, "wgrad_s5_c_im", after=tok),
        b_re=_mm_tn_blocks(s5_late["u"], s5_late["lam_re"], S5_BH, S5_BP, "wgrad_s5_b_re", after=tok),
        b_im=_mm_tn_blocks(s5_late["u"], s5_late["lam_im"], S5_BH, S5_BP, "wgrad_s5_b_im", after=tok))
    d_are, d_aim, d_ldt, d_btre, d_btim = _s5_params_bwd(
        a_re, a_im, log_dt, bt_re, bt_im, s5_late["dabar"][0].reshape(S5_G, S5_P), s5_late["dabar"][1].reshape(S5_G, S5_P),
        _unblock_b(blocks["b_re"]), _unblock_b(blocks["b_im"]))
    small_g.update(ssm_a_re=d_are, ssm_a_im=d_aim, ssm_log_dt=d_ldt, ssm_b_re=d_btre.transpose(1, 2, 0),
                   ssm_b_im=d_btim.transpose(1, 2, 0), ssm_c_re=_unblock_c(blocks["c_re"]), ssm_c_im=-_unblock_c(blocks["c_im"]))

    mine_thru, got = _swap_wait(s_sems, r_sems, mine_thru, lands, "swap_wait_early", blocks["b_im"])
    sib = dict(zip(early, got))
    for (n, li), t in zip(early, mine_thru):
        sums[n][li] = t
    after = got[-1]
    for key in ("0f", "0c"):
        after = collect(key, after)
    sib.update(zip(late, _swap_with_sibling([sums[n][li] for n, li in late], "swap_grad_sums_late")))

    dmod = _mod_bwd(jnp.stack(vs_mix), jnp.stack(vs_ffn), pv)
    small_g.update(ada_b=dmod[:, :6, :], norm1_g=dmod[:, 6, :], norm2_g=dmod[:, 7, :], final_g=vs_fin[1],
                   conv_w=jnp.stack(small_g["conv_w"]), conv_b=jnp.stack(small_g["conv_b"]))

    loss_part = (0.5 / D) * vs_fin[0, 0:1]
    part_shapes = [(1,)] + [tuple(small_g[n].shape) for n in SMALL]
    slots = _reduce_pair(_pack([loss_part] + [small_g[n] for n in SMALL], 16), "reduce_small_pair", sib[late[-1]])
    s_sems, r_sems, slots, tok = _reduce_cross_start(slots, "reduce_small_cross_start")

    res = {}
    for n in BIG:
        res[n] = _adamw_layers(w[n], sums[n], [sib[(n, li)] for li in range(w[n].shape[0])], m[n], v[n], f"adamw_{n}", after=tok)
        tok = res[n][0]

    slots = _reduce_cross_wait(s_sems, r_sems, slots, "reduce_small_cross_wait", tok)
    parts_sum = _reduce_finish(slots, "reduce_small_finish")
    summed = _unpack(parts_sum, part_shapes)
    loss = summed[0][0]
    gsum = dict(zip(SMALL, summed[1:]))
    dmod_all = _allgather_small(_pack([small_g["ada_b"]]), "gather_dmod", parts_sum)
    dmod_all = dmod_all.reshape(N_DEV, DEPTH, 6 * D)
    dmod_cols = lax.dynamic_slice_in_dim(dmod_all, chip * cols, cols, axis=2).transpose(1, 0, 2)
    g_ada_w = _ada_bwd(c16, jnp.pad(dmod_cols, ((0, 0), (0, 16 - N_DEV), (0, 0))))

    shp = ada_w.shape
    two = lambda t: t.reshape(shp[0] * shp[1], shp[2])
    res["ada_w"] = [t.reshape(shp) for t in _adamw(two(ada_w), [two(g_ada_w)], two(m_ada_w), two(v_ada_w), "adamw_ada_w")]

    def mine(n):
        g = gsum[n]
        if n in SMALL_SHARDED:
            g = lax.dynamic_slice_in_dim(g, chip * w[n].shape[-1], w[n].shape[-1], axis=g.ndim - 1)
        return g.reshape(w[n].shape)

    for k, names in enumerate(([n for n in SMALL if n not in SMALL_WIDE_PADDED], list(SMALL_WIDE_PADDED))):
        outs = _adamw_many([w[n] for n in names], [mine(n) for n in names], [m[n] for n in names], [v[n] for n in names],
                           f"adamw_small{k}")
        for idx, n in enumerate(names):
            res[n] = [outs[part][idx] for part in range(4)]

    outs = [loss, grad_x]
    for part in range(4):
        outs += [res[n][part] for n in WEIGHTS]
    return tuple(outs)
```

```python
import functools

import jax
import jax.numpy as jnp
from jax import lax
from jax.experimental import pallas as pl
from jax.experimental.pallas import tpu as pltpu

F32 = jnp.float32
BF16 = jnp.bfloat16
D = 1024
EPS = 1e-6
DEPTH = 4
MIXER_OF_LAYER = (0, 1, 2, 0)
S5_G, S5_H, S5_P = 64, 16, 64
S5_NB = 4
S5_BH = S5_H * 16
S5_BP = S5_P * 16
NSTATE = S5_G * S5_P
SG_HEADS, SG_CHUNK = 8, 128
ADAM_LR, ADAM_B1, ADAM_B2, ADAM_EPS, ADAM_WD, ADAM_STEP = 0.001, 0.9, 0.999, 1e-08, 0.01, 10
N_DEV = 8
MESH = pl.DeviceIdType.MESH
LANES = 1024
R_SH1, R_SC1, R_G1, R_SH2, R_SC2, R_G2, R_N1, R_N2 = range(8)


def _dot(a, b):
    return jnp.dot(a, b, preferred_element_type=F32)


def _dot_nt(a, b):
    return lax.dot_general(a, b, (((1,), (1,)), ((), ())), preferred_element_type=F32)


def _dot_tn(a, b):
    return lax.dot_general(a, b, (((0,), (0,)), ((), ())), preferred_element_type=F32)


def _bf(x):
    return x.astype(BF16)


def _sum0(x):
    return jnp.sum(x, axis=0, keepdims=True)


def _params(n_axes, vmem_mb=48):
    return pltpu.CompilerParams(dimension_semantics=("arbitrary",) * n_axes, vmem_limit_bytes=vmem_mb << 20)


def _rows(tm, cols, nt=None):
    if nt is None:
        return pl.BlockSpec((tm, cols), lambda i: (i, 0))
    return pl.BlockSpec((tm, cols), lambda i: (nt - 1 - i, 0))


def _whole(shape):
    nd = len(shape)
    return pl.BlockSpec(shape, lambda *_: (0,) * nd)


def _layer_w(r, c, layer):
    return pl.BlockSpec((None, r, c), lambda *_: (layer, 0, 0), pipeline_mode=pl.Buffered(1))


def _const_w(shape):
    nd = len(shape)
    return pl.BlockSpec(shape, lambda *_: (0,) * nd, pipeline_mode=pl.Buffered(1))


def _call_after(after, body, n_in, *, in_specs, **kw):
    if after is None:
        return pl.pallas_call(body, in_specs=in_specs, **kw), ()

    def body_after(*refs):
        return body(*refs[:n_in], *refs[n_in + 1:])

    return pl.pallas_call(body_after, in_specs=list(in_specs) + [pl.BlockSpec(memory_space=pl.ANY)], **kw), (after,)


def _norm_mod(x, ng, sc, sh):
    r = lax.rsqrt(jnp.mean(x * x, axis=-1, keepdims=True) + EPS)
    xn = x * r
    return (xn * ng) * (1.0 + sc) + sh, xn, r


def _norm_mod_bwd(dh, xn, r, ng, sc):
    dxn = dh * (ng * (1.0 + sc))
    return r * (dxn - xn * jnp.mean(dxn * xn, axis=-1, keepdims=True))


def _shift_down(z, prev8, k):
    row = lax.broadcasted_iota(jnp.int32, z.shape, 0)
    if k == 1:
        return jnp.where(row >= 1, pltpu.roll(z, 1, 0), prev8[7:8])
    return jnp.where(row >= 2, pltpu.roll(z, 2, 0), jnp.where(row == 0, prev8[6:7], prev8[7:8]))


def _shift_up(z, next8, k):
    n = z.shape[0]
    row = lax.broadcasted_iota(jnp.int32, z.shape, 0)
    if k == 1:
        return jnp.where(row <= n - 2, pltpu.roll(z, n - 1, 0), next8[0:1])
    return jnp.where(row <= n - 3, pltpu.roll(z, n - 2, 0), jnp.where(row == n - 2, next8[0:1], next8[1:2]))


def _place():
    x, y, c = lax.axis_index("x"), lax.axis_index("y"), lax.axis_index("c")
    chips = [(1 - x, y), (x, 1 - y), (1 - x, 1 - y)]
    return x, y, c, chips


def _allgather_small(x_shard, name, after=None):
    m_per, n = x_shard.shape

    def body(x_ref, out_ref, send_sems, recv_sems, local_sem):
        x, y, c, chips = _place()
        me, sibling = (x, y, c), (x, y, 1 - c)

        def rows(px, py, pc):
            return out_ref.at[pl.ds((4 * px + 2 * py + pc) * m_per, m_per), :]

        def copy(k, block, to, src=None):
            return pltpu.make_async_remote_copy(
                src_ref=rows(*block) if src is None else src, dst_ref=rows(*block),
                send_sem=send_sems.at[k], recv_sem=recv_sems.at[k], device_id=to, device_id_type=MESH)

        mine = pltpu.make_async_copy(x_ref, rows(*me), local_sem)
        mine.start()
        first = [copy(0, me, sibling, src=x_ref)]
        first += [copy(1 + j, me, (*chip, c), src=x_ref) for j, chip in enumerate(chips)]
        for cp in first:
            cp.start()
        passed = [copy(4 + j, (*chip, c), sibling) for j, chip in enumerate(chips)]
        for j, chip in enumerate(chips):
            copy(1 + j, (*chip, c), me).wait_recv()
            passed[j].start()
        copy(0, sibling, me).wait_recv()
        for j, chip in enumerate(chips):
            copy(4 + j, (*chip, 1 - c), me).wait_recv()
        for cp in first + passed:
            cp.wait_send()
        mine.wait()

    call, tail = _call_after(
        after, body, 1, name=name, out_shape=jax.ShapeDtypeStruct((N_DEV * m_per, n), F32),
        in_specs=[pl.BlockSpec(memory_space=pltpu.VMEM)], out_specs=pl.BlockSpec(memory_space=pltpu.VMEM),
        scratch_shapes=[pltpu.SemaphoreType.DMA((7,)), pltpu.SemaphoreType.DMA((7,)), pltpu.SemaphoreType.DMA],
        compiler_params=pltpu.CompilerParams(vmem_limit_bytes=48 << 20),
    )
    return call(x_shard, *tail)


def _reduce_pair(x_part, name, after=None):
    m, n = x_part.shape
    h = m // 2

    def body(x_ref, slots_ref, sib_buf, send_sem, recv_sem):
        x, y, c, _ = _place()
        swap = pltpu.make_async_remote_copy(src_ref=x_ref, dst_ref=sib_buf, send_sem=send_sem, recv_sem=recv_sem,
                                            device_id=(x, y, 1 - c), device_id_type=MESH)
        swap.start()
        swap.wait()
        mine = pl.ds(pl.multiple_of(c * h, 8), h)
        slots_ref[pl.ds(2 * x + y, 1)] = (x_ref[mine, :] + sib_buf[mine, :])[None]

    call, tail = _call_after(
        after, body, 1, name=name, out_shape=jax.ShapeDtypeStruct((4, h, n), F32),
        in_specs=[pl.BlockSpec(memory_space=pltpu.VMEM)], out_specs=pl.BlockSpec(memory_space=pltpu.VMEM),
        scratch_shapes=[pltpu.VMEM((m, n), F32), pltpu.SemaphoreType.DMA, pltpu.SemaphoreType.DMA],
        compiler_params=pltpu.CompilerParams(vmem_limit_bytes=48 << 20),
    )
    return call(x_part, *tail)


def _reduce_cross_start(slots, name):
    def body(slots_ref, send_sems, recv_sems, thru, token):
        x, y, c, chips = _place()
        mine = slots_ref.at[pl.ds(2 * x + y, 1)]
        for j, chip in enumerate(chips):
            pltpu.make_async_remote_copy(src_ref=mine, dst_ref=mine, send_sem=send_sems.at[j], recv_sem=recv_sems.at[j],
                                         device_id=(*chip, c), device_id_type=MESH).start()
        token[...] = jnp.zeros_like(token)

    res = pl.pallas_call(
        body, name=name,
        out_shape=(pltpu.SemaphoreType.DMA((3,)), pltpu.SemaphoreType.DMA((3,)), pltpu.HBM(slots.shape, F32),
                   jax.ShapeDtypeStruct((8, 128), F32)),
        in_specs=[HBM_SPEC], out_specs=(SEM_SPEC, SEM_SPEC, HBM_SPEC, pl.BlockSpec(memory_space=pltpu.VMEM)),
        input_output_aliases={0: 2}, compiler_params=SPLIT_COPY_PARAMS,
    )(*_in_hbm([slots]))
    return res


def _reduce_cross_wait(send_sems, recv_sems, slots, name, after):
    def body(slots_ref, s_sems, r_sems, after_ref, thru):
        x, y, c, chips = _place()
        for j, chip in enumerate(chips):
            theirs = slots_ref.at[pl.ds(2 * chip[0] + chip[1], 1)]
            cp = pltpu.make_async_remote_copy(src_ref=theirs, dst_ref=theirs, send_sem=s_sems.at[j], recv_sem=r_sems.at[j],
                                              device_id=(x, y, c), device_id_type=MESH)
            cp.wait_send()
            cp.wait_recv()

    return pl.pallas_call(
        body, name=name, out_shape=pltpu.HBM(slots.shape, F32),
        in_specs=[HBM_SPEC, SEM_SPEC, SEM_SPEC, ANY_SPEC], out_specs=HBM_SPEC,
        input_output_aliases={0: 0}, compiler_params=SPLIT_COPY_PARAMS,
    )(slots, send_sems, recv_sems, after)


def _reduce_finish(slots, name):
    _, h, n = slots.shape

    def body(slots_ref, out_ref, send_sem, recv_sem):
        x, y, c, _ = _place()
        mine = pl.ds(pl.multiple_of(c * h, 8), h)
        theirs = pl.ds(pl.multiple_of((1 - c) * h, 8), h)
        out_ref[mine, :] = ((slots_ref[0] + slots_ref[1]) + slots_ref[2]) + slots_ref[3]
        give = pltpu.make_async_remote_copy(src_ref=out_ref.at[mine, :], dst_ref=out_ref.at[mine, :], send_sem=send_sem,
                                            recv_sem=recv_sem, device_id=(x, y, 1 - c), device_id_type=MESH)
        give.start()
        give.wait_send()
        pltpu.make_async_remote_copy(src_ref=out_ref.at[theirs, :], dst_ref=out_ref.at[theirs, :], send_sem=send_sem,
                                     recv_sem=recv_sem, device_id=(x, y, c), device_id_type=MESH).wait_recv()

    return pl.pallas_call(
        body, name=name, out_shape=jax.ShapeDtypeStruct((2 * h, n), F32),
        in_specs=[pl.BlockSpec(memory_space=pltpu.VMEM)], out_specs=pl.BlockSpec(memory_space=pltpu.VMEM),
        scratch_shapes=[pltpu.SemaphoreType.DMA, pltpu.SemaphoreType.DMA],
        compiler_params=pltpu.CompilerParams(vmem_limit_bytes=48 << 20),
    )(slots)


def _shard_region(ref, full_shape, axis, chip_k, half=None):
    _, r, c = full_shape
    if axis == 1:
        rs = r // 4
        if half is None:
            return ref.at[:, pl.ds(pl.multiple_of(chip_k * rs, 128), rs), :]
        return ref.at[:, pl.ds(pl.multiple_of(chip_k * rs + half * (rs // 2), 128), rs // 2), :]
    cs = c // 4
    if half is None:
        return ref.at[:, :, pl.ds(pl.multiple_of(chip_k * cs, 128), cs)]
    return ref.at[:, pl.ds(pl.multiple_of(half * (r // 2), 128), r // 2), pl.ds(pl.multiple_of(chip_k * cs, 128), cs)]


HBM_SPEC = pl.BlockSpec(memory_space=pltpu.HBM)
SEM_SPEC = pl.BlockSpec(memory_space=pltpu.SEMAPHORE)
ANY_SPEC = pl.BlockSpec(memory_space=pl.ANY)
SPLIT_COPY_PARAMS = pltpu.CompilerParams(has_side_effects=pltpu.SideEffectType.DATAFLOW_SIDE_EFFECTING)


def _in_hbm(arrs):
    return [pltpu.with_memory_space_constraint(a, pltpu.HBM) for a in arrs]


def _cast_place(ws, layers, axes, place, name):
    n_arr = len(ws)
    in_specs, out_specs, fulls = [], [], []
    for w_stack, li, axis in zip(ws, layers, axes):
        _, r, c = w_stack.shape
        tr = r // 4
        fulls.append((1, 4 * r, c) if axis == 1 else (1, r, 4 * c))
        in_specs.append(pl.BlockSpec((None, tr, c), lambda i, p, li=li: (li, 2 * p[1] + i, 0)))
        if axis == 1:
            out_specs.append(pl.BlockSpec((None, tr, c), lambda i, p: (0, 4 * p[0] + 2 * p[1] + i, 0)))
        else:
            out_specs.append(pl.BlockSpec((None, tr, c), lambda i, p: (0, 2 * p[1] + i, p[0])))

    def body(p_ref, *refs):
        for a in range(n_arr):
            refs[n_arr + a][...] = _bf(refs[a][...])

    return pl.pallas_call(
        body, name=name,
        grid_spec=pltpu.PrefetchScalarGridSpec(num_scalar_prefetch=1, grid=(2,), in_specs=in_specs, out_specs=out_specs),
        out_shape=[jax.ShapeDtypeStruct(f, BF16) for f in fulls],
        compiler_params=_params(1),
    )(place, *ws)


def _gather_start(lands, axes, name, after):
    n_arr = len(lands)
    fulls = [tuple(l.shape) for l in lands]

    def body(*refs):
        land = refs[:n_arr]
        send_sems, recv_sems = refs[n_arr + 1:n_arr + 3]
        token = refs[-1]
        x, y, c, chips = _place()
        k_me = 2 * x + y
        for a in range(n_arr):
            mine = _shard_region(land[a], fulls[a], axes[a], k_me, c)
            for j, chip in enumerate(chips):
                pltpu.make_async_remote_copy(
                    src_ref=mine, dst_ref=mine, send_sem=send_sems.at[a * 3 + j], recv_sem=recv_sems.at[a * 3 + j],
                    device_id=(*chip, c), device_id_type=MESH).start()
        token[...] = jnp.zeros_like(token)

    res = pl.pallas_call(
        body, name=name,
        out_shape=(pltpu.SemaphoreType.DMA((3 * n_arr,)), pltpu.SemaphoreType.DMA((3 * n_arr,)),
                   *[pltpu.HBM(f, BF16) for f in fulls], jax.ShapeDtypeStruct((8, 128), F32)),
        in_specs=[HBM_SPEC] * n_arr + [ANY_SPEC],
        out_specs=(SEM_SPEC, SEM_SPEC, *[HBM_SPEC] * n_arr, pl.BlockSpec(memory_space=pltpu.VMEM)),
        input_output_aliases={a: 2 + a for a in range(n_arr)},
        compiler_params=SPLIT_COPY_PARAMS,
    )(*_in_hbm(lands), after)
    return res[0], res[1], list(res[2:2 + n_arr]), res[-1]


def _gather_wait(send_sems, recv_sems, lands, axes, name, after):
    n_arr = len(lands)
    fulls = [tuple(l.shape) for l in lands]

    def body(*refs):
        land = refs[:n_arr]
        s_sems, r_sems = refs[n_arr:n_arr + 2]
        x, y, c, chips = _place()
        for a in range(n_arr):
            for j, chip in enumerate(chips):
                k_j = 2 * chip[0] + chip[1]
                got = _shard_region(land[a], fulls[a], axes[a], k_j, c)
                cp = pltpu.make_async_remote_copy(
                    src_ref=got, dst_ref=got, send_sem=s_sems.at[a * 3 + j], recv_sem=r_sems.at[a * 3 + j],
                    device_id=(x, y, c), device_id_type=MESH)
                cp.wait_send()
                cp.wait_recv()

    res = pl.pallas_call(
        body, name=name,
        out_shape=tuple(pltpu.HBM(f, BF16) for f in fulls),
        in_specs=[HBM_SPEC] * n_arr + [SEM_SPEC, SEM_SPEC] + [ANY_SPEC] * len(after),
        out_specs=tuple([HBM_SPEC] * n_arr),
        input_output_aliases={a: a for a in range(n_arr)},
        compiler_params=SPLIT_COPY_PARAMS,
    )(*lands, send_sems, recv_sems, *after)
    return list(res)


def _gather_share(lands, axes, name):
    n_arr = len(lands)
    fulls = [tuple(l.shape) for l in lands]

    def body(*refs):
        land_in, land = refs[:n_arr], refs[n_arr:2 * n_arr]
        send_sems, recv_sems = refs[2 * n_arr:]
        x, y, c, chips = _place()
        copies = []
        for a in range(n_arr):
            for j, k_j in enumerate([2 * chip[0] + chip[1] for chip in chips] + [2 * x + y]):
                cp = pltpu.make_async_remote_copy(
                    src_ref=_shard_region(land_in[a], fulls[a], axes[a], k_j, c),
                    dst_ref=_shard_region(land[a], fulls[a], axes[a], k_j, c),
                    send_sem=send_sems.at[a * 4 + j], recv_sem=recv_sems.at[a * 4 + j],
                    device_id=(x, y, 1 - c), device_id_type=MESH)
                cp.start()
                copies.append(cp)
        for cp in copies:
            cp.wait()

    return pl.pallas_call(
        body, name=name, out_shape=[jax.ShapeDtypeStruct(f, BF16) for f in fulls],
        in_specs=[ANY_SPEC] * n_arr, out_specs=[ANY_SPEC] * n_arr,
        input_output_aliases={a: a for a in range(n_arr)},
        scratch_shapes=[pltpu.SemaphoreType.DMA((4 * n_arr,)), pltpu.SemaphoreType.DMA((4 * n_arr,))],
    )(*lands)


def _scatter_shapes(grads, axes):
    out = []
    for g, ax in zip(grads, axes):
        shp = list(g.shape)
        shp[ax] //= 4
        out.append((3,) + tuple(shp[1:]))
    return out


def _scatter_start(grads, axes, name, after):
    n_arr = len(grads)
    shapes = _scatter_shapes(grads, axes)
    lands = [lax.empty(s, BF16) for s in shapes]

    def body(*refs):
        ins, land = refs[:n_arr], refs[n_arr:2 * n_arr]
        send_sems, recv_sems = refs[2 * n_arr + 1:2 * n_arr + 3]
        token = refs[-1]
        x, y, c, chips = _place()
        for a in range(n_arr):
            for j, chip in enumerate(chips):
                k_j = 2 * chip[0] + chip[1]
                pltpu.make_async_remote_copy(
                    src_ref=_shard_region(ins[a], grads[a].shape, axes[a], k_j), dst_ref=land[a].at[pl.ds(j, 1)],
                    send_sem=send_sems.at[a * 3 + j], recv_sem=recv_sems.at[a * 3 + j],
                    device_id=(*chip, c), device_id_type=MESH).start()
        token[...] = jnp.zeros_like(token)

    res = pl.pallas_call(
        body, name=name,
        out_shape=(pltpu.SemaphoreType.DMA((3 * n_arr,)), pltpu.SemaphoreType.DMA((3 * n_arr,)),
                   *[pltpu.HBM(g.shape, BF16) for g in grads], *[pltpu.HBM(s, BF16) for s in shapes],
                   jax.ShapeDtypeStruct((8, 128), F32)),
        in_specs=[HBM_SPEC] * (2 * n_arr) + [ANY_SPEC],
        out_specs=(SEM_SPEC, SEM_SPEC, *[HBM_SPEC] * (2 * n_arr), pl.BlockSpec(memory_space=pltpu.VMEM)),
        input_output_aliases={a: 2 + a for a in range(2 * n_arr)},
        compiler_params=SPLIT_COPY_PARAMS,
    )(*_in_hbm(grads), *_in_hbm(lands), after)
    return res[0], res[1], list(res[2:2 + n_arr]), list(res[2 + n_arr:2 + 2 * n_arr]), res[-1]


def _scatter_wait(send_sems, recv_sems, grads, lands, axes, name, after):
    n_arr = len(grads)

    def body(*refs):
        ins, land = refs[:n_arr], refs[n_arr:2 * n_arr]
        s_sems, r_sems = refs[2 * n_arr:2 * n_arr + 2]
        x, y, c, chips = _place()
        for a in range(n_arr):
            for j, chip in enumerate(chips):
                k_j = 2 * chip[0] + chip[1]
                cp = pltpu.make_async_remote_copy(
                    src_ref=_shard_region(ins[a], grads[a].shape, axes[a], k_j), dst_ref=land[a].at[pl.ds(j, 1)],
                    send_sem=s_sems.at[a * 3 + j], recv_sem=r_sems.at[a * 3 + j],
                    device_id=(x, y, c), device_id_type=MESH)
                cp.wait_send()
                cp.wait_recv()

    res = pl.pallas_call(
        body, name=name,
        out_shape=(*[pltpu.HBM(g.shape, BF16) for g in grads], *[pltpu.HBM(l.shape, BF16) for l in lands]),
        in_specs=[HBM_SPEC] * (2 * n_arr) + [SEM_SPEC, SEM_SPEC, ANY_SPEC],
        out_specs=tuple([HBM_SPEC] * (2 * n_arr)),
        input_output_aliases={a: a for a in range(2 * n_arr)},
        compiler_params=SPLIT_COPY_PARAMS,
    )(*grads, *lands, send_sems, recv_sems, after)
    return list(res[:n_arr]), list(res[n_arr:])


def _swap_start(arrs, name, after):
    n_arr = len(arrs)
    lands = [lax.empty(a.shape, a.dtype) for a in arrs]

    def body(*refs):
        ins, land = refs[:n_arr], refs[n_arr:2 * n_arr]
        send_sems, recv_sems = refs[2 * n_arr + 1:2 * n_arr + 3]
        token = refs[-1]
        x, y, c, _ = _place()
        for a in range(n_arr):
            pltpu.make_async_remote_copy(
                src_ref=ins[a], dst_ref=land[a], send_sem=send_sems.at[a], recv_sem=recv_sems.at[a],
                device_id=(x, y, 1 - c), device_id_type=MESH).start()
        token[...] = jnp.zeros_like(token)

    res = pl.pallas_call(
        body, name=name,
        out_shape=(pltpu.SemaphoreType.DMA((n_arr,)), pltpu.SemaphoreType.DMA((n_arr,)),
                   *[pltpu.HBM(a.shape, a.dtype) for a in arrs], *[pltpu.HBM(a.shape, a.dtype) for a in arrs],
                   jax.ShapeDtypeStruct((8, 128), F32)),
        in_specs=[HBM_SPEC] * (2 * n_arr) + [ANY_SPEC],
        out_specs=(SEM_SPEC, SEM_SPEC, *[HBM_SPEC] * (2 * n_arr), pl.BlockSpec(memory_space=pltpu.VMEM)),
        input_output_aliases={a: 2 + a for a in range(2 * n_arr)},
        compiler_params=SPLIT_COPY_PARAMS,
    )(*_in_hbm(arrs), *_in_hbm(lands), after)
    return res[0], res[1], list(res[2:2 + n_arr]), list(res[2 + n_arr:2 + 2 * n_arr]), res[-1]


def _swap_wait(send_sems, recv_sems, arrs, lands, name, after):
    n_arr = len(arrs)

    def body(*refs):
        ins, land = refs[:n_arr], refs[n_arr:2 * n_arr]
        s_sems, r_sems = refs[2 * n_arr:2 * n_arr + 2]
        x, y, c, _ = _place()
        for a in range(n_arr):
            cp = pltpu.make_async_remote_copy(
                src_ref=ins[a], dst_ref=land[a], send_sem=s_sems.at[a], recv_sem=r_sems.at[a],
                device_id=(x, y, c), device_id_type=MESH)
            cp.wait_send()
            cp.wait_recv()

    res = pl.pallas_call(
        body, name=name,
        out_shape=(*[pltpu.HBM(a.shape, a.dtype) for a in arrs], *[pltpu.HBM(a.shape, a.dtype) for a in arrs]),
        in_specs=[HBM_SPEC] * (2 * n_arr) + [SEM_SPEC, SEM_SPEC, ANY_SPEC],
        out_specs=tuple([HBM_SPEC] * (2 * n_arr)),
        input_output_aliases={a: a for a in range(2 * n_arr)},
        compiler_params=SPLIT_COPY_PARAMS,
    )(*arrs, *lands, send_sems, recv_sems, after)
    return list(res[:n_arr]), list(res[n_arr:])


def _swap_with_sibling(arrs, name):
    n_arr = len(arrs)

    def body(*refs):
        ins, outs = refs[:n_arr], refs[n_arr:2 * n_arr]
        send_sems, recv_sems = refs[2 * n_arr:]
        x, y, c, _ = _place()
        copies = []
        for a in range(n_arr):
            cp = pltpu.make_async_remote_copy(
                src_ref=ins[a], dst_ref=outs[a], send_sem=send_sems.at[a], recv_sem=recv_sems.at[a],
                device_id=(x, y, 1 - c), device_id_type=MESH)
            cp.start()
            copies.append(cp)
        for cp in copies:
            cp.wait()

    any_spec = pl.BlockSpec(memory_space=pl.ANY)
    return pl.pallas_call(
        body, name=name, out_shape=[jax.ShapeDtypeStruct(a.shape, a.dtype) for a in arrs],
        in_specs=[any_spec] * n_arr, out_specs=[any_spec] * n_arr,
        scratch_shapes=[pltpu.SemaphoreType.DMA((n_arr,)), pltpu.SemaphoreType.DMA((n_arr,))],
    )(*arrs)


def _mm_tn(a, b, name, out_dtype=BF16):
    L, m = a.shape
    n = b.shape[1]
    bm, bn, bk = min(m, 1024), min(n, 1024), min(L, 2048)
    nk = L // bk

    def body(a_ref, b_ref, o_ref, acc):
        k = pl.program_id(2)

        @pl.when(k == 0)
        def _():
            acc[...] = jnp.zeros_like(acc)

        acc[...] += _dot_tn(_bf(a_ref[...]), _bf(b_ref[...]))

        @pl.when(k == nk - 1)
        def _():
            o_ref[...] = acc[...].astype(out_dtype)

    return pl.pallas_call(
        body, name=name, grid=(m // bm, n // bn, nk),
        in_specs=[pl.BlockSpec((bk, bm), lambda i, j, k: (k, i)), pl.BlockSpec((bk, bn), lambda i, j, k: (k, j))],
        out_specs=pl.BlockSpec((bm, bn), lambda i, j, k: (i, j)),
        out_shape=jax.ShapeDtypeStruct((m, n), out_dtype),
        scratch_shapes=[pltpu.VMEM((bm, bn), F32)],
        compiler_params=_params(3),
    )(a, b)


def _mm_tn_blocks(a, b, wa, wb, name, after=None):
    L = a.shape[0]
    nb = a.shape[1] // wa
    bk = min(L, 1024)
    nk = L // bk

    def body(a_ref, b_ref, o_ref):
        @pl.when(pl.program_id(1) == 0)
        def _():
            o_ref[...] = jnp.zeros_like(o_ref)

        o_ref[...] += _dot_tn(_bf(a_ref[...]), _bf(b_ref[...]))

    call, tail = _call_after(
        after, body, 2, name=name, grid=(nb, nk),
        in_specs=[pl.BlockSpec((bk, wa), lambda j, k: (k, j)), pl.BlockSpec((bk, wb), lambda j, k: (k, j))],
        out_specs=pl.BlockSpec((None, wa, wb), lambda j, k: (j, 0, 0)),
        out_shape=jax.ShapeDtypeStruct((nb, wa, wb), F32),
        compiler_params=_params(2),
    )
    return call(a, b, *tail)


def _sum_parts(parts, owns, axes, chip, name):
    n_arr = len(parts)
    steps = 4
    in_specs, out_specs, shapes = [], [], []
    for part, axis in zip(parts, axes):
        _, r, c = part.shape
        tr = r // steps
        shapes.append((r, c))
        in_specs.append(pl.BlockSpec((3, tr, c), lambda i, k: (0, i, 0)))
        out_specs.append(pl.BlockSpec((tr, c), lambda i, k: (i, 0)))
    for part, axis in zip(parts, axes):
        _, r, c = part.shape
        tr = r // steps
        if axis == 1:
            in_specs.append(pl.BlockSpec((None, tr, c), lambda i, k: (0, steps * k[0] + i, 0)))
        else:
            in_specs.append(pl.BlockSpec((None, tr, c), lambda i, k: (0, i, k[0])))

    def body(k_ref, *refs):
        for a in range(n_arr):
            p = refs[a][...].astype(F32)
            refs[2 * n_arr + a][...] = ((p[0] + p[1]) + p[2]) + refs[n_arr + a][...].astype(F32)

    return pl.pallas_call(
        body, name=name,
        grid_spec=pltpu.PrefetchScalarGridSpec(num_scalar_prefetch=1, grid=(steps,), in_specs=in_specs, out_specs=out_specs),
        out_shape=[jax.ShapeDtypeStruct(sh, F32) for sh in shapes],
        compiler_params=_params(1),
    )(chip, *parts, *owns)


def _adamw(w, g_parts, m, v, name):
    n_g = len(g_parts)
    if w.ndim == 2:
        r, c = w.shape
        tr = r
        for cand in (512, 256, 128, 64, 32, 16, 8):
            if r % cand == 0 and cand * c * 4 <= (2 << 20):
                tr = cand
                break
        spec = pl.BlockSpec((tr, c), lambda i: (i, 0))
        tiling = dict(grid=(r // tr,), in_specs=[spec] * (3 + n_g), out_specs=[spec] * 4, compiler_params=_params(1))
    else:
        tiling = dict(compiler_params=pltpu.CompilerParams(vmem_limit_bytes=48 << 20))

    def body(*refs):
        w_ref, g_refs, m_ref, v_ref = refs[0], refs[1:1 + n_g], refs[1 + n_g], refs[2 + n_g]
        g = g_refs[0][...]
        for gr in g_refs[1:]:
            g = g + gr[...]
        _adamw_update(g, w_ref, m_ref, v_ref, *refs[3 + n_g:])

    return pl.pallas_call(body, name=name, out_shape=[jax.ShapeDtypeStruct(w.shape, F32)] * 4, **tiling)(w, *g_parts, m, v)


def _adamw_update(g, w_ref, m_ref, v_ref, g_out, d_out, m_out, v_out):
    m_new = ADAM_B1 * m_ref[...] + (1.0 - ADAM_B1) * g
    v_new = ADAM_B2 * v_ref[...] + (1.0 - ADAM_B2) * (g * g)
    m_hat = m_new * (1.0 / (1.0 - ADAM_B1 ** ADAM_STEP))
    v_hat = v_new * (1.0 / (1.0 - ADAM_B2 ** ADAM_STEP))
    g_out[...] = g
    d_out[...] = -ADAM_LR * (m_hat / (jnp.sqrt(v_hat) + ADAM_EPS) + ADAM_WD * w_ref[...])
    m_out[...] = m_new
    v_out[...] = v_new


def _adamw_many(ws, gs, ms, vs, name):
    n = len(ws)

    def body(*refs):
        for k in range(n):
            _adamw_update(refs[n + k][...], refs[k], refs[2 * n + k], refs[3 * n + k],
                          refs[4 * n + k], refs[5 * n + k], refs[6 * n + k], refs[7 * n + k])

    outs = pl.pallas_call(body, name=name, out_shape=[jax.ShapeDtypeStruct(t.shape, F32) for t in ws] * 4,
                          compiler_params=pltpu.CompilerParams(vmem_limit_bytes=56 << 20))(*ws, *gs, *ms, *vs)
    return [outs[part * n:(part + 1) * n] for part in range(4)]


def _adamw_layers(w, q_mine, q_sib, m, v, name, after=None):
    n, r, c = w.shape
    tr = r
    for cand in (512, 256, 128, 64, 32, 16, 8):
        if r % cand == 0 and cand * c * 4 <= (1 << 20):
            tr = cand
            break

    def body(*refs):
        w_ref, qm, qs, m_ref, v_ref = refs[0], refs[1:1 + n], refs[1 + n:1 + 2 * n], refs[1 + 2 * n], refs[2 + 2 * n]
        layer = pl.program_id(0)
        g = qm[0][...] + qs[0][...]
        for k in range(1, n):
            g = jnp.where(layer == k, qm[k][...] + qs[k][...], g)
        _adamw_update(g, w_ref, m_ref, v_ref, *refs[3 + 2 * n:])

    stacked = pl.BlockSpec((None, tr, c), lambda l, i: (l, i, 0))
    per_layer = [pl.BlockSpec((tr, c), lambda l, i, k=k: (jnp.where(l == k, i, 0), 0)) for k in range(n)]
    call, tail = _call_after(
        after, body, 3 + 2 * n, name=name, grid=(n, r // tr),
        in_specs=[stacked] + per_layer + per_layer + [stacked, stacked], out_specs=[stacked] * 4,
        out_shape=[jax.ShapeDtypeStruct(w.shape, F32)] * 4,
        compiler_params=_params(2),
    )
    return call(w, *q_mine, *q_sib, m, v, *tail)


def _ada_fwd(c16, ada_w, ada_b_cols):
    cols = ada_w.shape[2]

    def body(c_ref, w_ref, b_ref, o_ref):
        cv = c_ref[...]
        ca = _bf(cv * jax.nn.sigmoid(cv))
        o_ref[...] = _dot(ca, _bf(w_ref[...])) + b_ref[...]

    return pl.pallas_call(
        body, name="ada_fwd", grid=(DEPTH,),
        in_specs=[_whole((16, D)), pl.BlockSpec((None, D, cols), lambda i: (i, 0, 0)),
                  pl.BlockSpec((None, 1, cols), lambda i: (i, 0, 0))],
        out_specs=pl.BlockSpec((None, 16, cols), lambda i: (i, 0, 0)),
        out_shape=jax.ShapeDtypeStruct((DEPTH, 16, cols), F32),
        compiler_params=_params(1),
    )(c16, ada_w, ada_b_cols)


def _ada_bwd(c16, dmod16):
    cols = dmod16.shape[2]

    def body(c_ref, d_ref, o_ref):
        cv = c_ref[...]
        ca = _bf(cv * jax.nn.sigmoid(cv))
        o_ref[...] = _dot_tn(ca, _bf(d_ref[...]))

    return pl.pallas_call(
        body, name="ada_bwd", grid=(DEPTH,),
        in_specs=[_whole((16, D)), pl.BlockSpec((None, 16, cols), lambda i: (i, 0, 0))],
        out_specs=pl.BlockSpec((None, D, cols), lambda i: (i, 0, 0)),
        out_shape=jax.ShapeDtypeStruct((DEPTH, D, cols), F32),
        compiler_params=_params(1),
    )(c16, dmod16)


def _mod_bwd(vs_mix, vs_ffn, pv):
    def body(m_ref, f_ref, pv_ref, o_ref):
        for i in range(DEPTH):
            vm, vf, p = m_ref[i], f_ref[i], pv_ref[i]
            o_ref[i] = jnp.concatenate([
                vm[2:3], vm[1:2] * p[R_N1:R_N1 + 1], vm[0:1],
                vf[2:3], vf[1:2] * p[R_N2:R_N2 + 1], vf[0:1],
                vm[1:2] * (1.0 + p[R_SC1:R_SC1 + 1]), vf[1:2] * (1.0 + p[R_SC2:R_SC2 + 1])], axis=0)

    return pl.pallas_call(body, name="mod_bwd", out_shape=jax.ShapeDtypeStruct((DEPTH, 8, D), F32))(vs_mix, vs_ffn, pv)


def _ffn_fwd(x1, pv, w1, w2, layer, tm, after=None):
    L = x1.shape[0]
    dff = w1.shape[2]

    def body(x1_ref, pv_ref, w1_ref, w2_ref, x2_ref, h2_ref, a_ref, f_ref):
        x1v, p = x1_ref[...], pv_ref[...]
        h2, _, _ = _norm_mod(x1v, p[R_N2:R_N2 + 1], p[R_SC2:R_SC2 + 1], p[R_SH2:R_SH2 + 1])
        hb = _bf(h2)
        h2_ref[...] = hb
        a = _dot(hb, w1_ref[...])
        a_ref[...] = a
        ra = jnp.maximum(a, 0.0)
        f = _dot(_bf(ra * ra), w2_ref[...])
        f_ref[...] = f
        x2_ref[...] = x1v + p[R_G2:R_G2 + 1] * f

    call, tail = _call_after(
        after, body, 4, name=f"ffn_fwd{layer}", grid=(L // tm,),
        in_specs=[_rows(tm, D), pl.BlockSpec((None, 8, D), lambda i: (layer, 0, 0)), _layer_w(D, dff, 0), _layer_w(dff, D, 0)],
        out_specs=[_rows(tm, D), _rows(tm, D), _rows(tm, dff), _rows(tm, D)],
        out_shape=[jax.ShapeDtypeStruct((L, D), F32), jax.ShapeDtypeStruct((L, D), BF16),
                   jax.ShapeDtypeStruct((L, dff), F32), jax.ShapeDtypeStruct((L, D), F32)],
        compiler_params=_params(1, 56),
    )
    return call(x1, pv, w1, w2, *tail)


def _ffn_bwd(dx2, x1, a, f, pv, w1, w2, layer, tm, after=None):
    L = x1.shape[0]
    dff = w1.shape[2]
    extra = [] if after is None else [pl.BlockSpec(memory_space=pl.ANY)]
    extra_args = [] if after is None else [after]

    def body(dx2_ref, x1_ref, a_ref, f_ref, pv_ref, w1_ref, w2_ref, *rest):
        dx1_ref, p_ref, da_ref, df_ref, vs_ref = rest[len(extra):]

        @pl.when(pl.program_id(0) == 0)
        def _():
            vs_ref[...] = jnp.zeros_like(vs_ref)

        dx2v, p = dx2_ref[...], pv_ref[...]
        dfb = _bf(dx2v * p[R_G2:R_G2 + 1])
        df_ref[...] = dfb
        vs_ref[0:1, :] += _sum0(dx2v * f_ref[...])
        dp = _dot_nt(dfb, w2_ref[...])
        ra = jnp.maximum(a_ref[...], 0.0)
        p_ref[...] = _bf(ra * ra)
        dab = _bf(dp * (2.0 * ra))
        da_ref[...] = dab
        dh2 = _dot_nt(dab, w1_ref[...])
        _, xn, r = _norm_mod(x1_ref[...], p[R_N2:R_N2 + 1], p[R_SC2:R_SC2 + 1], p[R_SH2:R_SH2 + 1])
        dx1_ref[...] = dx2v + _norm_mod_bwd(dh2, xn, r, p[R_N2:R_N2 + 1], p[R_SC2:R_SC2 + 1])
        vs_ref[1:2, :] += _sum0(dh2 * xn)
        vs_ref[2:3, :] += _sum0(dh2)

    return pl.pallas_call(
        body, name=f"ffn_bwd{layer}", grid=(L // tm,),
        in_specs=[_rows(tm, D), _rows(tm, D), _rows(tm, dff), _rows(tm, D),
                  pl.BlockSpec((None, 8, D), lambda i: (layer, 0, 0)), _layer_w(D, dff, 0), _layer_w(dff, D, 0)] + extra,
        out_specs=[_rows(tm, D), _rows(tm, dff), _rows(tm, dff), _rows(tm, D), _whole((8, D))],
        out_shape=[jax.ShapeDtypeStruct((L, D), F32), jax.ShapeDtypeStruct((L, dff), BF16),
                   jax.ShapeDtypeStruct((L, dff), BF16), jax.ShapeDtypeStruct((L, D), BF16),
                   jax.ShapeDtypeStruct((8, D), F32)],
        compiler_params=_params(1, 56),
    )(dx2, x1, a, f, pv, w1, w2, *extra_args)


def _conv_fwd(x, pv, w_in, w_out, cw, layer, j, tm, after=None):
    L = x.shape[0]

    def body(x_ref, pv_ref, win_ref, wout_ref, cw_ref, x1_ref, h_ref, bcx_ref, conv_ref, q_ref, y_ref, carry):
        @pl.when(pl.program_id(0) == 0)
        def _():
            carry[...] = jnp.zeros_like(carry)

        xv, p, cwv = x_ref[...], pv_ref[...], cw_ref[...]
        h, _, _ = _norm_mod(xv, p[R_N1:R_N1 + 1], p[R_SC1:R_SC1 + 1], p[R_SH1:R_SH1 + 1])
        hb = _bf(h)
        h_ref[...] = hb
        bcx = _dot(hb, win_ref[...])
        bcx_ref[...] = bcx
        z = bcx[:, D:2 * D] * bcx[:, 2 * D:]
        prev8 = carry[...]
        conv = cwv[0:1] * _shift_down(z, prev8, 2) + cwv[1:2] * _shift_down(z, prev8, 1) + cwv[2:3] * z + cwv[3:4]
        conv_ref[...] = conv
        qb = _bf(bcx[:, :D] * conv)
        q_ref[...] = qb
        y = _dot(qb, wout_ref[...])
        y_ref[...] = y
        x1_ref[...] = xv + p[R_G1:R_G1 + 1] * y
        carry[...] = z[tm - 8:tm]

    call, tail = _call_after(
        after, body, 5, name=f"conv_fwd{layer}", grid=(L // tm,),
        in_specs=[_rows(tm, D), pl.BlockSpec((None, 8, D), lambda i: (layer, 0, 0)), _layer_w(D, 3 * D, 0), _layer_w(D, D, 0),
                  pl.BlockSpec((None, 8, D), lambda i: (j, 0, 0))],
        out_specs=[_rows(tm, D), _rows(tm, D), _rows(tm, 3 * D), _rows(tm, D), _rows(tm, D), _rows(tm, D)],
        out_shape=[jax.ShapeDtypeStruct((L, D), F32), jax.ShapeDtypeStruct((L, D), BF16), jax.ShapeDtypeStruct((L, 3 * D), F32),
                   jax.ShapeDtypeStruct((L, D), F32), jax.ShapeDtypeStruct((L, D), BF16), jax.ShapeDtypeStruct((L, D), F32)],
        scratch_shapes=[pltpu.VMEM((8, D), F32)],
        compiler_params=_params(1, 56),
    )
    return call(x, pv, w_in, w_out, cw, *tail)


def _conv_bwd(dx1, x, y, bcx, conv, pv, w_in, w_out, cw, layer, j, tm, after=None):
    L = x.shape[0]
    nt = L // tm

    def body(dx1_ref, x_ref, y_ref, bcx_ref, conv_ref, halo_ref, pv_ref, win_ref, wout_ref, cw_ref,
             dx_ref, dbcx_ref, dy_ref, vs_ref, carry):
        gi = pl.program_id(0)
        tile = nt - 1 - gi

        @pl.when(gi == 0)
        def _():
            vs_ref[...] = jnp.zeros_like(vs_ref)
            carry[...] = jnp.zeros_like(carry)

        dx1v, p, cwv = dx1_ref[...], pv_ref[...], cw_ref[...]
        dyb = _bf(dx1v * p[R_G1:R_G1 + 1])
        dy_ref[...] = dyb
        vs_ref[0:1, :] += _sum0(dx1v * y_ref[...])
        dq = _dot_nt(dyb, wout_ref[...])
        bcx = bcx_ref[...]
        b, cg, xh = bcx[:, :D], bcx[:, D:2 * D], bcx[:, 2 * D:]
        db = dq * conv_ref[...]
        dc = dq * b
        z = cg * xh
        halo = halo_ref[...]
        zprev = jnp.where(tile > 0, halo[:, D:2 * D] * halo[:, 2 * D:], 0.0)
        vs_ref[3:4, :] += _sum0(dc * _shift_down(z, zprev, 2))
        vs_ref[4:5, :] += _sum0(dc * _shift_down(z, zprev, 1))
        vs_ref[5:6, :] += _sum0(dc * z)
        vs_ref[6:7, :] += _sum0(dc)
        next8 = carry[...]
        dz = cwv[2:3] * dc + cwv[1:2] * _shift_up(dc, next8, 1) + cwv[0:1] * _shift_up(dc, next8, 2)
        dbb, dcgb, dxhb = _bf(db), _bf(dz * xh), _bf(dz * cg)
        dbcx_ref[:, 0:D] = dbb
        dbcx_ref[:, D:2 * D] = dcgb
        dbcx_ref[:, 2 * D:3 * D] = dxhb
        dh = (_dot_nt(dbb, win_ref[:, 0:D]) + _dot_nt(dcgb, win_ref[:, D:2 * D])) + _dot_nt(dxhb, win_ref[:, 2 * D:3 * D])
        _, xn, r = _norm_mod(x_ref[...], p[R_N1:R_N1 + 1], p[R_SC1:R_SC1 + 1], p[R_SH1:R_SH1 + 1])
        dx_ref[...] = dx1v + _norm_mod_bwd(dh, xn, r, p[R_N1:R_N1 + 1], p[R_SC1:R_SC1 + 1])
        vs_ref[1:2, :] += _sum0(dh * xn)
        vs_ref[2:3, :] += _sum0(dh)
        carry[...] = dc[0:8]

    halo_spec = pl.BlockSpec((8, 3 * D), lambda i: (jnp.maximum((nt - 1 - i) * (tm // 8) - 1, 0), 0))
    call, tail = _call_after(
        after, body, 10, name=f"conv_bwd{layer}", grid=(nt,),
        in_specs=[_rows(tm, D, nt), _rows(tm, D, nt), _rows(tm, D, nt), _rows(tm, 3 * D, nt), _rows(tm, D, nt), halo_spec,
                  pl.BlockSpec((None, 8, D), lambda i: (layer, 0, 0)), _layer_w(D, 3 * D, 0), _layer_w(D, D, 0),
                  pl.BlockSpec((None, 8, D), lambda i: (j, 0, 0))],
        out_specs=[_rows(tm, D, nt), _rows(tm, 3 * D, nt), _rows(tm, D, nt), _whole((8, D))],
        out_shape=[jax.ShapeDtypeStruct((L, D), F32), jax.ShapeDtypeStruct((L, 3 * D), BF16),
                   jax.ShapeDtypeStruct((L, D), BF16), jax.ShapeDtypeStruct((8, D), F32)],
        scratch_shapes=[pltpu.VMEM((8, D), F32)],
        compiler_params=_params(1, 56),
    )
    return call(dx1, x, y, bcx, conv, bcx, pv, w_in, w_out, cw, *tail)


def _s5_discretize(a_re, a_im, log_dt, bt_re, bt_im):
    dt = jnp.exp(log_dt)
    mag = jnp.exp(a_re * dt)
    abar_re = mag * jnp.cos(a_im * dt)
    abar_im = mag * jnp.sin(a_im * dt)
    den = a_re * a_re + a_im * a_im
    nr = abar_re - 1.0
    ni = abar_im
    f_re = (nr * a_re + ni * a_im) / den
    f_im = (ni * a_re - nr * a_im) / den
    bbar_re = f_re * bt_re - f_im * bt_im
    bbar_im = f_re * bt_im + f_im * bt_re
    return abar_re, abar_im, bbar_re, bbar_im


def _s5_params_fwd(a_re, a_im, log_dt, bt_re, bt_im):
    def body(ar, ai, ld, br, bi, o_ar, o_ai, o_br, o_bi):
        r = _s5_discretize(ar[...], ai[...], ld[...], br[...], bi[...])
        o_ar[...], o_ai[...], o_br[...], o_bi[...] = r

    gp = jax.ShapeDtypeStruct((S5_G, S5_P), F32)
    hgp = jax.ShapeDtypeStruct((S5_H, S5_G, S5_P), F32)
    return pl.pallas_call(body, name="s5_params_fwd", out_shape=[gp, gp, hgp, hgp])(a_re, a_im, log_dt, bt_re, bt_im)


def _s5_params_bwd(a_re, a_im, log_dt, bt_re, bt_im, d_ar, d_ai, d_br, d_bi):
    def body(ar, ai, ld, br, bi, gar, gai, gbr, gbi, o_ar, o_ai, o_ld, o_br, o_bi):
        _, vjp = jax.vjp(_s5_discretize, ar[...], ai[...], ld[...], br[...], bi[...])
        r = vjp((gar[...], gai[...], gbr[...], gbi[...]))
        o_ar[...], o_ai[...], o_ld[...], o_br[...], o_bi[...] = r

    gp = jax.ShapeDtypeStruct((S5_G, S5_P), F32)
    hgp = jax.ShapeDtypeStruct((S5_H, S5_G, S5_P), F32)
    return pl.pallas_call(body, name="s5_params_bwd", out_shape=[gp, gp, jax.ShapeDtypeStruct((S5_G, 1), F32), hgp, hgp])(
        a_re, a_im, log_dt, bt_re, bt_im, d_ar, d_ai, d_br, d_bi)


NSEG = 8
SCAN_LANES = 1024


def _to_segments(x):
    n, c = x.shape
    return x.reshape(NSEG, n // NSEG, c).transpose(1, 0, 2).reshape(n, c)


def _from_segments(x):
    n, c = x.shape
    return x.reshape(n // NSEG, NSEG, c).transpose(1, 0, 2).reshape(n, c)


def _segment_scan(re_ref, im_ref, st_re, st_im, a_re, a_im, n_slabs, adjoint, write):
    for q in range(NSTATE // SCAN_LANES):
        ls = slice(q * SCAN_LANES, (q + 1) * SCAN_LANES)
        ar = jnp.broadcast_to(a_re[:, ls], (8, SCAN_LANES))
        ai = jnp.broadcast_to(a_im[:, ls], (8, SCAN_LANES))

        def step(k, carry, ls=ls, ar=ar, ai=ai):
            s_r, s_i = carry
            slab = (n_slabs - 1 - k) if adjoint else k
            rows = pl.ds(pl.multiple_of(slab * 8, 8), 8)
            b_r, b_i = re_ref[rows, ls], im_ref[rows, ls]
            if adjoint:
                n_r = b_r + ar * s_r + ai * s_i
                n_i = b_i - ai * s_r + ar * s_i
            else:
                n_r = ar * s_r - ai * s_i + b_r
                n_i = ar * s_i + ai * s_r + b_i
            if write:
                re_ref[rows, ls] = n_r
                im_ref[rows, ls] = n_i
            return n_r, n_i

        s_r, s_i = lax.fori_loop(0, n_slabs, step, (st_re[:, ls], st_im[:, ls]), unroll=4)
        st_re[:, ls] = s_r
        st_im[:, ls] = s_i


def _s5_segment_states(e_re, e_im, ar, ai, seg_len, adjoint):
    def body(ere_ref, eim_ref, ar_ref, ai_ref, ore_ref, oim_ref):
        p_r, p_i = ar_ref[...], ai_ref[...]
        if adjoint:
            p_i = -p_i
        acc_r, acc_i = jnp.ones_like(p_r), jnp.zeros_like(p_r)
        n = seg_len
        while n:
            if n & 1:
                acc_r, acc_i = acc_r * p_r - acc_i * p_i, acc_r * p_i + acc_i * p_r
            n >>= 1
            if n:
                p_r, p_i = p_r * p_r - p_i * p_i, 2.0 * p_r * p_i
        e_r, e_i = ere_ref[...], eim_ref[...]
        s_r, s_i = jnp.zeros_like(acc_r), jnp.zeros_like(acc_r)
        rows_r, rows_i = [None] * NSEG, [None] * NSEG
        order = range(NSEG - 1, -1, -1) if adjoint else range(NSEG)
        for j in order:
            rows_r[j], rows_i[j] = s_r, s_i
            s_r, s_i = (acc_r * s_r - acc_i * s_i + e_r[j:j + 1], acc_r * s_i + acc_i * s_r + e_i[j:j + 1])
        ore_ref[...] = jnp.concatenate(rows_r, axis=0)
        oim_ref[...] = jnp.concatenate(rows_i, axis=0)

    st = jax.ShapeDtypeStruct((NSEG, NSTATE), F32)
    return pl.pallas_call(body, name="s5_segment_states_bwd" if adjoint else "s5_segment_states_fwd", out_shape=[st, st])(
        e_re, e_im, ar, ai)


def _s5_fwd_ends(x, pv, w_in, b_re, b_im, ar, ai, layer, tm, after=None):
    L = x.shape[0]

    def body(x_ref, pv_ref, win_ref, bre_ref, bim_ref, ar_ref, ai_ref, h_ref, u_ref, ere_ref, eim_ref, bu_re, bu_im):
        @pl.when(pl.program_id(0) == 0)
        def _():
            ere_ref[...] = jnp.zeros_like(ere_ref)
            eim_ref[...] = jnp.zeros_like(eim_ref)

        p = pv_ref[...]
        h, _, _ = _norm_mod(x_ref[...], p[R_N1:R_N1 + 1], p[R_SC1:R_SC1 + 1], p[R_SH1:R_SH1 + 1])
        hb = _bf(h)
        h_ref[...] = hb
        u = _dot(hb, win_ref[...])
        u_ref[...] = u
        ub = _bf(u)
        for k in range(S5_NB):
            uk = ub[:, k * S5_BH:(k + 1) * S5_BH]
            bu_re[:, k * S5_BP:(k + 1) * S5_BP] = _dot(uk, bre_ref[k])
            bu_im[:, k * S5_BP:(k + 1) * S5_BP] = _dot(uk, bim_ref[k])
        _segment_scan(bu_re, bu_im, ere_ref, eim_ref, ar_ref[...], ai_ref[...], tm // 8, adjoint=False, write=False)

    call, tail = _call_after(
        after, body, 7, name="s5_fwd_ends", grid=(L // tm,),
        in_specs=[_rows(tm, D), pl.BlockSpec((None, 8, D), lambda i: (layer, 0, 0)), _layer_w(D, D, 0),
                  _const_w((S5_NB, S5_BH, S5_BP)), _const_w((S5_NB, S5_BH, S5_BP)), _whole((1, NSTATE)), _whole((1, NSTATE))],
        out_specs=[_rows(tm, D), _rows(tm, D), _whole((NSEG, NSTATE)), _whole((NSEG, NSTATE))],
        out_shape=[jax.ShapeDtypeStruct((L, D), BF16), jax.ShapeDtypeStruct((L, D), F32),
                   jax.ShapeDtypeStruct((NSEG, NSTATE), F32), jax.ShapeDtypeStruct((NSEG, NSTATE), F32)],
        scratch_shapes=[pltpu.VMEM((tm, NSTATE), F32), pltpu.VMEM((tm, NSTATE), F32)],
        compiler_params=_params(1, 56),
    )
    return call(x, pv, w_in, b_re, b_im, ar, ai, *tail)


def _s5_fwd_out(x, u, pv, b_re, b_im, s0_re, s0_im, ar, ai, c_re, c_im, dvec, glu_w, glu_b, w_out, layer, tm):
    L = x.shape[0]

    def body(x_ref, u_ref, pv_ref, bre_ref, bim_ref, s0re_ref, s0im_ref, ar_ref, ai_ref, cre_ref, cim_ref, d_ref, gw_ref,
             gb_ref, wout_ref, x1_ref, sre_ref, sim_ref, y1_ref, zg_ref, y3_ref, y_ref, st_re, st_im):
        @pl.when(pl.program_id(0) == 0)
        def _():
            st_re[...] = s0re_ref[...]
            st_im[...] = s0im_ref[...]

        p = pv_ref[...]
        uv = u_ref[...]
        ub = _bf(uv)
        for k in range(S5_NB):
            uk = ub[:, k * S5_BH:(k + 1) * S5_BH]
            sre_ref[:, k * S5_BP:(k + 1) * S5_BP] = _dot(uk, bre_ref[k])
            sim_ref[:, k * S5_BP:(k + 1) * S5_BP] = _dot(uk, bim_ref[k])
        _segment_scan(sre_ref, sim_ref, st_re, st_im, ar_ref[...], ai_ref[...], tm // 8, adjoint=False, write=True)
        parts = []
        for k in range(S5_NB):
            sl = slice(k * S5_BP, (k + 1) * S5_BP)
            parts.append(_dot(_bf(sre_ref[:, sl]), cre_ref[k]) - _dot(_bf(sim_ref[:, sl]), cim_ref[k]))
        y1 = jnp.concatenate(parts, axis=1) + d_ref[...] * uv
        y1_ref[...] = y1
        y2 = jax.nn.gelu(y1)
        zg = _dot(_bf(y2), gw_ref[...]) + gb_ref[...]
        zg_ref[...] = zg
        y3b = _bf(y2 * jax.nn.sigmoid(zg))
        y3_ref[...] = y3b
        y = _dot(y3b, wout_ref[...])
        y_ref[...] = y
        x1_ref[...] = x_ref[...] + p[R_G1:R_G1 + 1] * y

    return pl.pallas_call(
        body, name="s5_fwd_out", grid=(L // tm,),
        in_specs=[_rows(tm, D), _rows(tm, D), pl.BlockSpec((None, 8, D), lambda i: (layer, 0, 0)),
                  _const_w((S5_NB, S5_BH, S5_BP)), _const_w((S5_NB, S5_BH, S5_BP)),
                  _whole((NSEG, NSTATE)), _whole((NSEG, NSTATE)), _whole((1, NSTATE)), _whole((1, NSTATE)),
                  _const_w((S5_NB, S5_BP, S5_BH)), _const_w((S5_NB, S5_BP, S5_BH)), _whole((1, D)),
                  _layer_w(D, D, 0), _whole((1, D)), _layer_w(D, D, 0)],
        out_specs=[_rows(tm, D), _rows(tm, NSTATE), _rows(tm, NSTATE), _rows(tm, D), _rows(tm, D), _rows(tm, D), _rows(tm, D)],
        out_shape=[jax.ShapeDtypeStruct((L, D), F32), jax.ShapeDtypeStruct((L, NSTATE), F32), jax.ShapeDtypeStruct((L, NSTATE), F32),
                   jax.ShapeDtypeStruct((L, D), F32), jax.ShapeDtypeStruct((L, D), F32),
                   jax.ShapeDtypeStruct((L, D), BF16), jax.ShapeDtypeStruct((L, D), F32)],
        scratch_shapes=[pltpu.VMEM((NSEG, NSTATE), F32), pltpu.VMEM((NSEG, NSTATE), F32)],
        compiler_params=_params(1, 56),
    )(x, u, pv, b_re, b_im, s0_re, s0_im, ar, ai, c_re, c_im, dvec, glu_w, glu_b, w_out)


def _s5_bwd_ends(dx1, y, y1, zg, u, pv, c_re, c_im, ar, ai, dvec, glu_w, w_out, layer, tm, after=None):
    L = dx1.shape[0]
    nt = L // tm

    def body(dx1_ref, y_ref, y1_ref, zg_ref, u_ref, pv_ref, cre_ref, cim_ref, ar_ref, ai_ref, d_ref, gw_ref, wout_ref,
             dy_ref, y2_ref, dzg_ref, dy1_ref, dus_ref, ere_ref, eim_ref, vs_ref, g_re, g_im):
        @pl.when(pl.program_id(0) == 0)
        def _():
            vs_ref[...] = jnp.zeros_like(vs_ref)
            ere_ref[...] = jnp.zeros_like(ere_ref)
            eim_ref[...] = jnp.zeros_like(eim_ref)

        dx1v, p = dx1_ref[...], pv_ref[...]
        dyb = _bf(dx1v * p[R_G1:R_G1 + 1])
        dy_ref[...] = dyb
        vs_ref[0:1, :] += _sum0(dx1v * y_ref[...])
        dy3 = _dot_nt(dyb, wout_ref[...])
        y2, gelu_vjp = jax.vjp(jax.nn.gelu, y1_ref[...])
        y2_ref[...] = _bf(y2)
        gate = jax.nn.sigmoid(zg_ref[...])
        dzg = dy3 * y2 * gate * (1.0 - gate)
        dzgb = _bf(dzg)
        dzg_ref[...] = dzgb
        vs_ref[1:2, :] += _sum0(dzg)
        dy2 = dy3 * gate + _dot_nt(dzgb, gw_ref[...])
        dy1 = gelu_vjp(dy2)[0]
        vs_ref[2:3, :] += _sum0(dy1 * u_ref[...])
        dus_ref[...] = dy1 * d_ref[...]
        dy1b = _bf(dy1)
        dy1_ref[...] = dy1b
        for k in range(S5_NB):
            dk = dy1b[:, k * S5_BH:(k + 1) * S5_BH]
            g_re[:, k * S5_BP:(k + 1) * S5_BP] = _dot_nt(dk, cre_ref[k])
            g_im[:, k * S5_BP:(k + 1) * S5_BP] = -_dot_nt(dk, cim_ref[k])
        _segment_scan(g_re, g_im, ere_ref, eim_ref, ar_ref[...], ai_ref[...], tm // 8, adjoint=True, write=False)

    call, tail = _call_after(
        after, body, 13, name="s5_bwd_ends", grid=(nt,),
        in_specs=[_rows(tm, D, nt)] * 5 + [pl.BlockSpec((None, 8, D), lambda i: (layer, 0, 0)),
                  _const_w((S5_NB, S5_BP, S5_BH)), _const_w((S5_NB, S5_BP, S5_BH)), _whole((1, NSTATE)), _whole((1, NSTATE)),
                  _whole((1, D)), _layer_w(D, D, 0), _layer_w(D, D, 0)],
        out_specs=[_rows(tm, D, nt)] * 5 + [_whole((NSEG, NSTATE)), _whole((NSEG, NSTATE)), _whole((8, D))],
        out_shape=[jax.ShapeDtypeStruct((L, D), BF16)] * 4 + [jax.ShapeDtypeStruct((L, D), F32),
                   jax.ShapeDtypeStruct((NSEG, NSTATE), F32), jax.ShapeDtypeStruct((NSEG, NSTATE), F32),
                   jax.ShapeDtypeStruct((8, D), F32)],
        scratch_shapes=[pltpu.VMEM((tm, NSTATE), F32), pltpu.VMEM((tm, NSTATE), F32)],
        compiler_params=_params(1, 56),
    )
    return call(dx1, y, y1, zg, u, pv, c_re, c_im, ar, ai, dvec, glu_w, w_out, *tail)


def _s5_bwd_in(dx1, dy1_b, du_skip, x, s_re, s_im, pv, b_re, b_im, c_re, c_im, l0_re, l0_im, ar, ai, w_in, layer, tm):
    L = x.shape[0]
    nt = L // tm

    def body(dx1_ref, dy1_ref, dus_ref, x_ref, sre_ref, sim_ref, hre_ref, him_ref, lre_ref, lim_ref, pv_ref, bre_ref, bim_ref,
             cre_ref, cim_ref, l0re_ref, l0im_ref, ar_ref, ai_ref, win_ref,
             dx_ref, du_ref, lamre_ref, lamim_ref, da_ref, vs_ref, g_re, g_im, st_re, st_im):
        gi = pl.program_id(0)
        tile = nt - 1 - gi

        @pl.when(gi == 0)
        def _():
            vs_ref[...] = jnp.zeros_like(vs_ref)
            da_ref[...] = jnp.zeros_like(da_ref)
            st_re[...] = l0re_ref[...]
            st_im[...] = l0im_ref[...]

        p = pv_ref[...]
        dy1b = dy1_ref[...]
        for k in range(S5_NB):
            dk = dy1b[:, k * S5_BH:(k + 1) * S5_BH]
            g_re[:, k * S5_BP:(k + 1) * S5_BP] = _dot_nt(dk, cre_ref[k])
            g_im[:, k * S5_BP:(k + 1) * S5_BP] = -_dot_nt(dk, cim_ref[k])
        _segment_scan(g_re, g_im, st_re, st_im, ar_ref[...], ai_ref[...], tm // 8, adjoint=True, write=True)
        lam_r, lam_i = g_re[...], g_im[...]
        lrb, lib = _bf(lam_r), _bf(lam_i)
        lamre_ref[...] = lrb
        lamim_ref[...] = lib

        def wrapped(last_ref):
            z = last_ref[...]
            row = lax.broadcasted_iota(jnp.int32, z.shape, 0)
            return jnp.where(row >= 1, pltpu.roll(z, 1, 0), 0.0)

        first_r = jnp.where(tile > 0, hre_ref[...], wrapped(lre_ref))
        first_i = jnp.where(tile > 0, him_ref[...], wrapped(lim_ref))
        sp_r = jnp.concatenate([first_r, sre_ref[0:tm - 8, :]], axis=0)
        sp_i = jnp.concatenate([first_i, sim_ref[0:tm - 8, :]], axis=0)
        da_ref[0:1, :] += _sum0(lam_r * sp_r + lam_i * sp_i)
        da_ref[1:2, :] += _sum0(lam_i * sp_r - lam_r * sp_i)

        parts = []
        for k in range(S5_NB):
            sl = slice(k * S5_BP, (k + 1) * S5_BP)
            parts.append(_dot_nt(lrb[:, sl], bre_ref[k]) + _dot_nt(lib[:, sl], bim_ref[k]))
        dub = _bf(jnp.concatenate(parts, axis=1) + dus_ref[...])
        du_ref[...] = dub
        dh = _dot_nt(dub, win_ref[...])
        _, xn, r = _norm_mod(x_ref[...], p[R_N1:R_N1 + 1], p[R_SC1:R_SC1 + 1], p[R_SH1:R_SH1 + 1])
        dx_ref[...] = dx1_ref[...] + _norm_mod_bwd(dh, xn, r, p[R_N1:R_N1 + 1], p[R_SC1:R_SC1 + 1])
        vs_ref[1:2, :] += _sum0(dh * xn)
        vs_ref[2:3, :] += _sum0(dh)

    halo = pl.BlockSpec((8, NSTATE), lambda i: (jnp.maximum((nt - 1 - i) * (tm // 8) - 1, 0), 0))
    last = pl.BlockSpec((8, NSTATE), lambda i: (L // 8 - 1, 0))
    return pl.pallas_call(
        body, name="s5_bwd_in", grid=(nt,),
        in_specs=[_rows(tm, D, nt), _rows(tm, D, nt), _rows(tm, D, nt), _rows(tm, D, nt), _rows(tm, NSTATE, nt), _rows(tm, NSTATE, nt),
                  halo, halo, last, last, pl.BlockSpec((None, 8, D), lambda i: (layer, 0, 0)),
                  _const_w((S5_NB, S5_BH, S5_BP)), _const_w((S5_NB, S5_BH, S5_BP)),
                  _const_w((S5_NB, S5_BP, S5_BH)), _const_w((S5_NB, S5_BP, S5_BH)),
                  _whole((NSEG, NSTATE)), _whole((NSEG, NSTATE)), _whole((1, NSTATE)), _whole((1, NSTATE)), _layer_w(D, D, 0)],
        out_specs=[_rows(tm, D, nt), _rows(tm, D, nt), _rows(tm, NSTATE, nt), _rows(tm, NSTATE, nt), _whole((8, NSTATE)), _whole((8, D))],
        out_shape=[jax.ShapeDtypeStruct((L, D), F32), jax.ShapeDtypeStruct((L, D), BF16),
                   jax.ShapeDtypeStruct((L, NSTATE), BF16), jax.ShapeDtypeStruct((L, NSTATE), BF16),
                   jax.ShapeDtypeStruct((8, NSTATE), F32), jax.ShapeDtypeStruct((8, D), F32)],
        scratch_shapes=[pltpu.VMEM((tm, NSTATE), F32), pltpu.VMEM((tm, NSTATE), F32),
                        pltpu.VMEM((NSEG, NSTATE), F32), pltpu.VMEM((NSEG, NSTATE), F32)],
        compiler_params=_params(1, 60),
    )(dx1, dy1_b, du_skip, x, s_re, s_im, s_re, s_im, s_re, s_im, pv, b_re, b_im, c_re, c_im, l0_re, l0_im, ar, ai, w_in)


def _blockdiag_b(bt):
    b = bt.reshape(S5_H, S5_NB, 16, S5_P).transpose(1, 2, 0, 3)
    eye = jnp.eye(16, dtype=bt.dtype)
    return (b[:, :, :, None, :] * eye[None, :, None, :, None]).reshape(S5_NB, S5_BH, S5_BP)


def _unblock_b(d):
    d = jnp.einsum("bghgp->bghp", d.reshape(S5_NB, 16, S5_H, 16, S5_P))
    return d.transpose(2, 0, 1, 3).reshape(S5_H, S5_G, S5_P)


def _blockdiag_c(cm):
    c4 = cm.reshape(S5_NB, 16, S5_H, S5_P)
    eye = jnp.eye(16, dtype=cm.dtype)
    out = c4.transpose(0, 1, 3, 2)[:, :, :, None, :] * eye[None, :, None, :, None]
    return out.reshape(S5_NB, S5_BP, S5_BH)


def _unblock_c(d):
    d = jnp.einsum("bgpgh->bghp", d.reshape(S5_NB, 16, S5_P, 16, S5_H))
    return d.reshape(S5_G, S5_H, S5_P)


def _tril_mask():
    return lax.broadcasted_iota(jnp.int32, (SG_CHUNK, SG_CHUNK), 0) >= lax.broadcasted_iota(jnp.int32, (SG_CHUNK, SG_CHUNK), 1)


def _sg_fwd(x, pv, w_in, w_s, b_t, vg, w_out, layer, tm, after=None):
    L = x.shape[0]
    nc = tm // SG_CHUNK

    def body(x_ref, pv_ref, win_ref, ws_ref, bt_ref, vg_ref, wout_ref, x1_ref, h_ref, uv_ref, vm_ref, q_ref, y_ref):
        xv, p = x_ref[...], pv_ref[...]
        h, _, _ = _norm_mod(xv, p[R_N1:R_N1 + 1], p[R_SC1:R_SC1 + 1], p[R_SH1:R_SH1 + 1])
        hb = _bf(h)
        h_ref[...] = hb
        uv = _dot(hb, win_ref[...])
        uv_ref[...] = uv
        v = uv[:, D:]
        rv = lax.rsqrt(jnp.mean(v * v, axis=-1, keepdims=True) + EPS)
        vnb = _bf((v * rv) * vg_ref[...])
        mask = _tril_mask()
        bt = bt_ref[...]
        for hd in range(SG_HEADS):
            wm = _bf(jnp.where(mask, ws_ref[hd], 0.0))
            cs = slice(hd * SG_CHUNK, (hd + 1) * SG_CHUNK)
            for ck in range(nc):
                rs = slice(ck * SG_CHUNK, (ck + 1) * SG_CHUNK)
                vm_ref[rs, cs] = _dot(wm, vnb[rs, cs]) + bt[:, hd:hd + 1]
        qb = _bf(uv[:, :D] * vm_ref[...])
        q_ref[...] = qb
        y = _dot(qb, wout_ref[...])
        y_ref[...] = y
        x1_ref[...] = xv + p[R_G1:R_G1 + 1] * y

    call, tail = _call_after(
        after, body, 7, name="sg_fwd", grid=(L // tm,),
        in_specs=[_rows(tm, D), pl.BlockSpec((None, 8, D), lambda i: (layer, 0, 0)), _layer_w(D, 2 * D, 0),
                  _whole((SG_HEADS, SG_CHUNK, SG_CHUNK)), _whole((SG_CHUNK, SG_HEADS)), _whole((1, D)), _layer_w(D, D, 0)],
        out_specs=[_rows(tm, D), _rows(tm, D), _rows(tm, 2 * D), _rows(tm, D), _rows(tm, D), _rows(tm, D)],
        out_shape=[jax.ShapeDtypeStruct((L, D), F32), jax.ShapeDtypeStruct((L, D), BF16), jax.ShapeDtypeStruct((L, 2 * D), F32),
                   jax.ShapeDtypeStruct((L, D), F32), jax.ShapeDtypeStruct((L, D), BF16), jax.ShapeDtypeStruct((L, D), F32)],
        compiler_params=_params(1, 56),
    )
    return call(x, pv, w_in, w_s, b_t, vg, w_out, *tail)


def _sg_bwd(dx1, x, y, uv, vm, pv, w_in, w_s, vg, w_out, layer, tm, after=None):
    L = x.shape[0]
    nc = tm // SG_CHUNK

    def body(dx1_ref, x_ref, y_ref, uv_ref, vm_ref, pv_ref, win_ref, ws_ref, vg_ref, wout_ref,
             dx_ref, duv_ref, dy_ref, vs_ref, dws_ref, dbt_ref, dvn_scr):
        @pl.when(pl.program_id(0) == 0)
        def _():
            vs_ref[...] = jnp.zeros_like(vs_ref)
            dws_ref[...] = jnp.zeros_like(dws_ref)
            dbt_ref[...] = jnp.zeros_like(dbt_ref)

        dx1v, p = dx1_ref[...], pv_ref[...]
        dyb = _bf(dx1v * p[R_G1:R_G1 + 1])
        dy_ref[...] = dyb
        vs_ref[0:1, :] += _sum0(dx1v * y_ref[...])
        dq = _dot_nt(dyb, wout_ref[...])
        uv = uv_ref[...]
        u, v = uv[:, :D], uv[:, D:]
        dub = _bf(dq * vm_ref[...])
        dvm = dq * u
        dvmb = _bf(dvm)
        rv = lax.rsqrt(jnp.mean(v * v, axis=-1, keepdims=True) + EPS)
        vh = v * rv
        vgv = vg_ref[...]
        vnb = _bf(vh * vgv)
        mask = _tril_mask()
        for hd in range(SG_HEADS):
            wm = _bf(jnp.where(mask, ws_ref[hd], 0.0))
            cs = slice(hd * SG_CHUNK, (hd + 1) * SG_CHUNK)
            dws = jnp.zeros((SG_CHUNK, SG_CHUNK), F32)
            dbs = jnp.zeros((SG_CHUNK, 1), F32)
            for ck in range(nc):
                rs = slice(ck * SG_CHUNK, (ck + 1) * SG_CHUNK)
                dvn_scr[rs, cs] = _dot_tn(wm, dvmb[rs, cs])
                dws = dws + _dot_nt(dvmb[rs, cs], vnb[rs, cs])
                dbs = dbs + jnp.sum(dvm[rs, cs], axis=1, keepdims=True)
            dws_ref[hd] += jnp.where(mask, dws, 0.0)
            dbt_ref[:, hd:hd + 1] += dbs
        dvn = dvn_scr[...]
        vs_ref[3:4, :] += _sum0(dvn * vh)
        dvnn = dvn * vgv
        dvb = _bf(rv * (dvnn - vh * jnp.mean(dvnn * vh, axis=-1, keepdims=True)))
        duv_ref[:, 0:D] = dub
        duv_ref[:, D:2 * D] = dvb
        dh = _dot_nt(dub, win_ref[:, 0:D]) + _dot_nt(dvb, win_ref[:, D:2 * D])
        _, xn, r = _norm_mod(x_ref[...], p[R_N1:R_N1 + 1], p[R_SC1:R_SC1 + 1], p[R_SH1:R_SH1 + 1])
        dx_ref[...] = dx1v + _norm_mod_bwd(dh, xn, r, p[R_N1:R_N1 + 1], p[R_SC1:R_SC1 + 1])
        vs_ref[1:2, :] += _sum0(dh * xn)
        vs_ref[2:3, :] += _sum0(dh)

    call, tail = _call_after(
        after, body, 10, name="sg_bwd", grid=(L // tm,),
        in_specs=[_rows(tm, D), _rows(tm, D), _rows(tm, D), _rows(tm, 2 * D), _rows(tm, D),
                  pl.BlockSpec((None, 8, D), lambda i: (layer, 0, 0)), _layer_w(D, 2 * D, 0),
                  _whole((SG_HEADS, SG_CHUNK, SG_CHUNK)), _whole((1, D)), _layer_w(D, D, 0)],
        out_specs=[_rows(tm, D), _rows(tm, 2 * D), _rows(tm, D), _whole((8, D)),
                   _whole((SG_HEADS, SG_CHUNK, SG_CHUNK)), _whole((SG_CHUNK, SG_HEADS))],
        out_shape=[jax.ShapeDtypeStruct((L, D), F32), jax.ShapeDtypeStruct((L, 2 * D), BF16), jax.ShapeDtypeStruct((L, D), BF16),
                   jax.ShapeDtypeStruct((8, D), F32), jax.ShapeDtypeStruct((SG_HEADS, SG_CHUNK, SG_CHUNK), F32),
                   jax.ShapeDtypeStruct((SG_CHUNK, SG_HEADS), F32)],
        scratch_shapes=[pltpu.VMEM((tm, D), F32)],
        compiler_params=_params(1, 56),
    )
    return call(dx1, x, y, uv, vm, pv, w_in, w_s, vg, w_out, *tail)


def _final(x, target, fg, tm):
    L = x.shape[0]

    def body(x_ref, t_ref, g_ref, dx_ref, vs_ref):
        @pl.when(pl.program_id(0) == 0)
        def _():
            vs_ref[...] = jnp.zeros_like(vs_ref)

        xv, g = x_ref[...], g_ref[...]
        r = lax.rsqrt(jnp.mean(xv * xv, axis=-1, keepdims=True) + EPS)
        xn = xv * r
        e = xn * g - t_ref[...]
        vs_ref[0:1, :] += jnp.sum(e * e)
        dout = e * (1.0 / D)
        vs_ref[1:2, :] += _sum0(dout * xn)
        dxn = dout * g
        dx_ref[...] = r * (dxn - xn * jnp.mean(dxn * xn, axis=-1, keepdims=True))

    return pl.pallas_call(
        body, name="final_loss", grid=(L // tm,),
        in_specs=[_rows(tm, D), _rows(tm, D), _whole((1, D))],
        out_specs=[_rows(tm, D), _whole((8, D))],
        out_shape=[jax.ShapeDtypeStruct((L, D), F32), jax.ShapeDtypeStruct((8, D), F32)],
        compiler_params=_params(1),
    )(x, target, fg)


def _pack_flat(arrs, multiple=LANES):
    flat = jnp.concatenate([a.reshape(-1).astype(F32) for a in arrs])
    return jnp.pad(flat, (0, -flat.shape[0] % multiple))


def _pack(arrs, row_multiple=8):
    return _pack_flat(arrs, row_multiple * LANES).reshape(-1, LANES)


def _unpack(buf, shapes, lead=()):
    flat = buf.reshape(lead + (-1,))
    out, off = [], 0
    for s in shapes:
        n = 1
        for d in s:
            n *= d
        out.append(flat[..., off:off + n].reshape(lead + tuple(s)))
        off += n
    return out


BIG = ("ff_w1", "ff_w2", "conv_w_in", "conv_w_out", "ssm_w_in", "ssm_glu_w", "ssm_w_out", "sg_w_in", "sg_w_out")
BIG_AXIS = {"ff_w1": 2, "ff_w2": 1, "conv_w_in": 2, "conv_w_out": 1, "ssm_w_in": 1, "ssm_glu_w": 1, "ssm_w_out": 1,
            "sg_w_in": 2, "sg_w_out": 1}
LAYER_WEIGHTS = (
    (("conv_w_in", 0), ("conv_w_out", 0), ("ff_w1", 0), ("ff_w2", 0)),
    (("ssm_w_in", 0), ("ssm_glu_w", 0), ("ssm_w_out", 0), ("ff_w1", 1), ("ff_w2", 1)),
    (("sg_w_in", 0), ("sg_w_out", 0), ("ff_w1", 2), ("ff_w2", 2)),
    (("conv_w_in", 1), ("conv_w_out", 1), ("ff_w1", 3), ("ff_w2", 3)),
)
GATHER_GROUPS = tuple(grp for lw in LAYER_WEIGHTS for grp in (lw[:-2], lw[-2:]))
SMALL_SHARDED = ("conv_w", "conv_b", "sg_v_g")
SMALL_WIDE_PADDED = ("ssm_b_re", "ssm_b_im")
SMALL = ("ada_b", "norm1_g", "norm2_g", "final_g", "ssm_a_re", "ssm_a_im", "ssm_log_dt", "ssm_b_re", "ssm_b_im", "ssm_c_re",
         "ssm_c_im", "ssm_d", "ssm_glu_b", "sg_w_s", "sg_b_s") + SMALL_SHARDED
WEIGHTS = ("ada_w", "ada_b", "norm1_g", "norm2_g", "ff_w1", "ff_w2", "final_g", "conv_w_in", "conv_w", "conv_b", "conv_w_out",
           "ssm_w_in", "ssm_a_re", "ssm_a_im", "ssm_log_dt", "ssm_b_re", "ssm_b_im", "ssm_c_re", "ssm_c_im", "ssm_d",
           "ssm_glu_w", "ssm_glu_b", "ssm_w_out", "sg_w_in", "sg_v_g", "sg_w_s", "sg_b_s", "sg_w_out")


def kernel(x, c, ada_w, ada_b, norm1_g, norm2_g, ff_w1, ff_w2, final_g, conv_w_in, conv_w, conv_b, conv_w_out, ssm_w_in, ssm_a_re, ssm_a_im, ssm_log_dt, ssm_b_re, ssm_b_im, ssm_c_re, ssm_c_im, ssm_d, ssm_glu_w, ssm_glu_b, ssm_w_out, sg_w_in, sg_v_g, sg_w_s, sg_b_s, sg_w_out, loss_target, m_ada_w, m_ada_b, m_norm1_g, m_norm2_g, m_ff_w1, m_ff_w2, m_final_g, m_conv_w_in, m_conv_w, m_conv_b, m_conv_w_out, m_ssm_w_in, m_ssm_a_re, m_ssm_a_im, m_ssm_log_dt, m_ssm_b_re, m_ssm_b_im, m_ssm_c_re, m_ssm_c_im, m_ssm_d, m_ssm_glu_w, m_ssm_glu_b, m_ssm_w_out, m_sg_w_in, m_sg_v_g, m_sg_w_s, m_sg_b_s, m_sg_w_out, v_ada_w, v_ada_b, v_norm1_g, v_norm2_g, v_ff_w1, v_ff_w2, v_final_g, v_conv_w_in, v_conv_w, v_conv_b, v_conv_w_out, v_ssm_w_in, v_ssm_a_re, v_ssm_a_im, v_ssm_log_dt, v_ssm_b_re, v_ssm_b_im, v_ssm_c_re, v_ssm_c_im, v_ssm_d, v_ssm_glu_w, v_ssm_glu_b, v_ssm_w_out, v_sg_w_in, v_sg_v_g, v_sg_w_s, v_sg_b_s, v_sg_w_out):
    args = dict(locals())
    w = {n: args[n] for n in WEIGHTS}
    m = {n: args["m_" + n] for n in WEIGHTS}
    v = {n: args["v_" + n] for n in WEIGHTS}
    L = x.shape[1]
    tm = min(L, 256)
    tm2 = min(L, 512)
    chip = 2 * lax.axis_index("x") + lax.axis_index("y")
    me = 2 * chip + lax.axis_index("c")
    xin = x[0]
    target = loss_target[0]
    chip1 = chip.reshape(1).astype(jnp.int32)
    place = jnp.stack([chip, lax.axis_index("c")]).astype(jnp.int32)

    gathers = []

    casts = {}

    def cast_group(g):
        entries = GATHER_GROUPS[g]
        axes = [BIG_AXIS[n] for n, _ in entries]
        casts[g] = _cast_place([w[n] for n, _ in entries], [li for _, li in entries], axes, place, f"cast_group{g}")

    def start_gather(g, after):
        if g not in casts:
            cast_group(g)
        axes = [BIG_AXIS[n] for n, _ in GATHER_GROUPS[g]]
        lands = casts[g]
        s_sems, r_sems, lands, token = _gather_start(lands, axes, f"gather_start{g}", after)
        gathers.append((s_sems, r_sems, lands, axes))
        return token

    def weights_of(g, after):
        s_sems, r_sems, lands, axes = gathers[g]
        lands = _gather_wait(s_sems, r_sems, lands, axes, f"gather_wait{g}", after)
        lands = _gather_share(lands, axes, f"gather_share{g}")
        token = start_gather(g + 2, lands[0]) if g + 2 < len(GATHER_GROUPS) else None
        return dict(zip([n for n, _ in GATHER_GROUPS[g]], lands)), token

    small_in = _pack([c, conv_w, conv_b, sg_v_g])
    got = _allgather_small(small_in, "gather_small_inputs").reshape(N_DEV, -1)
    c_all, cw_sh, cb_sh, vg_sh = _unpack(got, [(D,), conv_w.shape, conv_b.shape, sg_v_g.shape], lead=(N_DEV,))
    conv_w_full = jnp.concatenate([cw_sh[2 * k] for k in range(4)], axis=-1)
    conv_b_full = jnp.concatenate([cb_sh[2 * k] for k in range(4)], axis=-1)
    vg_full = jnp.concatenate([vg_sh[2 * k] for k in range(4)], axis=-1)
    c16 = jnp.pad(c_all, ((0, 16 - N_DEV), (0, 0)))

    cols = ada_w.shape[2]
    ada_b_cols = lax.dynamic_slice_in_dim(ada_b, chip * cols, cols, axis=1)[:, None, :]
    mod_sh = _ada_fwd(c16, ada_w, ada_b_cols)[:, :N_DEV, :]
    mod_all = _allgather_small(_pack([mod_sh]), "gather_mod").reshape(N_DEV, -1)
    mod_all = _unpack(mod_all, [mod_sh.shape], lead=(N_DEV,))[0]
    mod_mine = lax.dynamic_index_in_dim(mod_all[0::2], me, axis=2, keepdims=False)
    mod_mine = mod_mine.transpose(1, 0, 2).reshape(DEPTH, 6, D)
    pv = jnp.concatenate([mod_mine, norm1_g[:, None, :], norm2_g[:, None, :]], axis=1)

    start_gather(1, start_gather(0, pv))
    for g in range(2, len(GATHER_GROUPS)):
        cast_group(g)

    cw_rows = jnp.concatenate([conv_w_full, conv_b_full[:, None, :], jnp.zeros((conv_w_full.shape[0], 4, D), F32)], axis=1)

    a_re, a_im = ssm_a_re[0], ssm_a_im[0]
    log_dt = ssm_log_dt[0][:, None]
    bt_re, bt_im = ssm_b_re[0].transpose(2, 0, 1), ssm_b_im[0].transpose(2, 0, 1)
    abar_re, abar_im, bbar_re, bbar_im = _s5_params_fwd(a_re, a_im, log_dt, bt_re, bt_im)
    ar_vec, ai_vec = abar_re.reshape(1, NSTATE), abar_im.reshape(1, NSTATE)
    bd_re, bd_im = _bf(_blockdiag_b(bbar_re)), _bf(_blockdiag_b(bbar_im))
    cd_re, cd_im = _bf(_blockdiag_c(ssm_c_re[0])), _bf(_blockdiag_c(ssm_c_im[0]))

    saved = []
    fulls = []
    xl = xin
    for i in range(DEPTH):
        kind = MIXER_OF_LAYER[i]
        j = i // 3
        first_after = [bd_re, bd_im, cd_re, cd_im, cw_rows] + [casts[g][0] for g in range(2, len(GATHER_GROUPS))]
        full, tok = weights_of(2 * i, first_after if i == 0 else [xl])
        fulls.append(full)
        if kind == 0:
            x1, h, bcx, conv, q, y = _conv_fwd(xl, pv, full["conv_w_in"], full["conv_w_out"], cw_rows, i, j, tm2, after=tok)
            mix = dict(h=h, bcx=bcx, conv=conv, q=q, y=y)
        elif kind == 1:
            xp = _to_segments(xl)
            h, u, e_re, e_im = _s5_fwd_ends(xp, pv, full["ssm_w_in"], bd_re, bd_im, ar_vec, ai_vec, i, tm, after=tok)
            s0_re, s0_im = _s5_segment_states(e_re, e_im, ar_vec, ai_vec, L // NSEG, adjoint=False)
            x1p, s_re, s_im, y1, zg, y3, y = _s5_fwd_out(xp, u, pv, bd_re, bd_im, s0_re, s0_im, ar_vec, ai_vec, cd_re, cd_im,
                                                         ssm_d, full["ssm_glu_w"], ssm_glu_b, full["ssm_w_out"], i, tm)
            x1 = _from_segments(x1p)
            mix = dict(xp=xp, h=h, u=u, s_re=s_re, s_im=s_im, y1=y1, zg=zg, y3=y3, y=y)
        else:
            x1, h, uv, vm, q, y = _sg_fwd(xl, pv, full["sg_w_in"], sg_w_s[0], sg_b_s[0].T, vg_full, full["sg_w_out"], i, tm2,
                                          after=tok)
            mix = dict(h=h, uv=uv, vm=vm, q=q, y=y)
        ffn_weights, tok = weights_of(2 * i + 1, [x1])
        full.update(ffn_weights)
        x2, h2, a, f = _ffn_fwd(x1, pv, full["ff_w1"], full["ff_w2"], i, tm2, after=tok)
        saved.append(dict(x=xl, x1=x1, h2=h2, a=a, f=f, **mix))
        xl = x2

    dxl, vs_fin = _final(xl, target, final_g[None, :], tm2)

    gfull = {n: [None] * w[n].shape[0] for n in BIG}
    vs_mix, vs_ffn = [None] * DEPTH, [None] * DEPTH
    small_g = {}
    scatters = {}
    token = None

    def start_scatter(key, entries, after):
        garrs = [gfull[n][li][None] for n, li in entries]
        gaxes = [BIG_AXIS[n] for n, _ in entries]
        s_sems, r_sems, garrs, lands, tok = _scatter_start(garrs, gaxes, f"scatter_start{key}", after)
        scatters[key] = (s_sems, r_sems, garrs, lands, gaxes, entries)
        return tok

    for i in reversed(range(DEPTH)):
        kind = MIXER_OF_LAYER[i]
        j = i // 3
        sv = saved[i]
        full = fulls[i]
        dx1, p_b, da_b, df_b, vs_ffn[i] = _ffn_bwd(dxl, sv["x1"], sv["a"], sv["f"], pv, full["ff_w1"], full["ff_w2"], i, tm,
                                                   after=token)
        gfull["ff_w1"][i] = _mm_tn(sv["h2"], da_b, f"wgrad_ff_w1_{i}")
        gfull["ff_w2"][i] = _mm_tn(p_b, df_b, f"wgrad_ff_w2_{i}")
        if i == 0:
            token = start_scatter("0f", LAYER_WEIGHTS[0][2:], dx1)
        if kind == 0:
            dxl, dbcx_b, dy_b, vsm = _conv_bwd(dx1, sv["x"], sv["y"], sv["bcx"], sv["conv"], pv, full["conv_w_in"],
                                               full["conv_w_out"], cw_rows, i, j, tm2, after=token if i == 0 else None)
            gfull["conv_w_in"][j] = _mm_tn(sv["h"], dbcx_b, f"wgrad_conv_w_in_{j}")
            gfull["conv_w_out"][j] = _mm_tn(sv["q"], dy_b, f"wgrad_conv_w_out_{j}")
            small_g.setdefault("conv_w", [None, None])[j] = vsm[3:6]
            small_g.setdefault("conv_b", [None, None])[j] = vsm[6]
        elif kind == 1:
            dx1p = _to_segments(dx1)
            dy_b, y2_b, dzg_b, dy1_b, du_skip, eb_re, eb_im, vsm = _s5_bwd_ends(
                dx1p, sv["y"], sv["y1"], sv["zg"], sv["u"], pv, cd_re, cd_im, ar_vec, ai_vec, ssm_d, full["ssm_glu_w"],
                full["ssm_w_out"], i, tm)
            l0_re, l0_im = _s5_segment_states(eb_re, eb_im, ar_vec, ai_vec, L // NSEG, adjoint=True)
            dxp, du_b, lam_re, lam_im, dabar, vs_in = _s5_bwd_in(
                dx1p, dy1_b, du_skip, sv["xp"], sv["s_re"], sv["s_im"], pv, bd_re, bd_im, cd_re, cd_im, l0_re, l0_im,
                ar_vec, ai_vec, full["ssm_w_in"], i, tm)
            dxl = _from_segments(dxp)
            gfull["ssm_w_out"][0] = _mm_tn(sv["y3"], dy_b, "wgrad_ssm_w_out")
            gfull["ssm_glu_w"][0] = _mm_tn(y2_b, dzg_b, "wgrad_ssm_glu_w")
            gfull["ssm_w_in"][0] = _mm_tn(sv["h"], du_b, "wgrad_ssm_w_in")
            s5_late = dict(s_re=sv["s_re"], s_im=sv["s_im"], u=sv["u"], dy1_b=dy1_b, lam_re=lam_re, lam_im=lam_im, dabar=dabar)
            small_g.update(ssm_d=vsm[2], ssm_glu_b=vsm[1])
            vsm = jnp.concatenate([vsm[0:1], vs_in[1:3], jnp.zeros((5, D), F32)], axis=0)
        else:
            dxl, duv_b, dy_b, vsm, d_ws, d_bt = _sg_bwd(dx1, sv["x"], sv["y"], sv["uv"], sv["vm"], pv, full["sg_w_in"],
                                                        sg_w_s[0], vg_full, full["sg_w_out"], i, tm2)
            gfull["sg_w_in"][0] = _mm_tn(sv["h"], duv_b, "wgrad_sg_w_in")
            gfull["sg_w_out"][0] = _mm_tn(sv["q"], dy_b, "wgrad_sg_w_out")
            small_g.update(sg_v_g=vsm[3], sg_w_s=d_ws, sg_b_s=d_bt.T)
        vs_mix[i] = vsm
        token = start_scatter(str(i), LAYER_WEIGHTS[i], dxl) if i > 0 else start_scatter("0c", LAYER_WEIGHTS[0][:2], dxl)
    grad_x = dxl[None]

    sums = {n: [None] * w[n].shape[0] for n in BIG}

    def collect(key, after):
        s_sems, r_sems, garrs, lands, gaxes, entries = scatters[key]
        garrs, recv = _scatter_wait(s_sems, r_sems, garrs, lands, gaxes, f"scatter_wait{key}", after)
        for (n, li), t in zip(entries, _sum_parts(recv, garrs, gaxes, chip1, f"sum_group{key}")):
            sums[n][li] = t
        return sums[entries[-1][0]][entries[-1][1]]

    after = token
    for key in ("3", "2", "1"):
        after = collect(key, after)
    early = [(n, li) for i in (3, 2, 1) for n, li in LAYER_WEIGHTS[i]]
    late = list(LAYER_WEIGHTS[0][2:]) + list(LAYER_WEIGHTS[0][:2])
    s_sems, r_sems, mine_thru, lands, tok = _swap_start([sums[n][li] for n, li in early], "swap_start_early", after)

    blocks = dict(
        c_re=_mm_tn_blocks(s5_late["s_re"], s5_late["dy1_b"], S5_BP, S5_BH, "wgrad_s5_c_re", after=tok),
        c_im=_mm_tn_blocks(s5_late["s_im"], s5_late["dy1_b"], S5_BP, S5_BH, "wgrad_s5_c_im", after=tok),
        b_re=_mm_tn_blocks(s5_late["u"], s5_late["lam_re"], S5_BH, S5_BP, "wgrad_s5_b_re", after=tok),
        b_im=_mm_tn_blocks(s5_late["u"], s5_late["lam_im"], S5_BH, S5_BP, "wgrad_s5_b_im", after=tok))
    d_are, d_aim, d_ldt, d_btre, d_btim = _s5_params_bwd(
        a_re, a_im, log_dt, bt_re, bt_im, s5_late["dabar"][0].reshape(S5_G, S5_P), s5_late["dabar"][1].reshape(S5_G, S5_P),
        _unblock_b(blocks["b_re"]), _unblock_b(blocks["b_im"]))
    small_g.update(ssm_a_re=d_are, ssm_a_im=d_aim, ssm_log_dt=d_ldt, ssm_b_re=d_btre.transpose(1, 2, 0),
                   ssm_b_im=d_btim.transpose(1, 2, 0), ssm_c_re=_unblock_c(blocks["c_re"]), ssm_c_im=-_unblock_c(blocks["c_im"]))

    mine_thru, got = _swap_wait(s_sems, r_sems, mine_thru, lands, "swap_wait_early", blocks["b_im"])
    sib = dict(zip(early, got))
    for (n, li), t in zip(early, mine_thru):
        sums[n][li] = t
    after = got[-1]
    for key in ("0f", "0c"):
        after = collect(key, after)
    sib.update(zip(late, _swap_with_sibling([sums[n][li] for n, li in late], "swap_grad_sums_late")))

    dmod = _mod_bwd(jnp.stack(vs_mix), jnp.stack(vs_ffn), pv)
    small_g.update(ada_b=dmod[:, :6, :], norm1_g=dmod[:, 6, :], norm2_g=dmod[:, 7, :], final_g=vs_fin[1],
                   conv_w=jnp.stack(small_g["conv_w"]), conv_b=jnp.stack(small_g["conv_b"]))

    loss_part = (0.5 / D) * vs_fin[0, 0:1]
    part_shapes = [(1,)] + [tuple(small_g[n].shape) for n in SMALL]
    slots = _reduce_pair(_pack([loss_part] + [small_g[n] for n in SMALL], 16), "reduce_small_pair", sib[late[-1]])
    s_sems, r_sems, slots, tok = _reduce_cross_start(slots, "reduce_small_cross_start")

    res = {}
    for n in BIG:
        res[n] = _adamw_layers(w[n], sums[n], [sib[(n, li)] for li in range(w[n].shape[0])], m[n], v[n], f"adamw_{n}", after=tok)
        tok = res[n][0]

    slots = _reduce_cross_wait(s_sems, r_sems, slots, "reduce_small_cross_wait", tok)
    parts_sum = _reduce_finish(slots, "reduce_small_finish")
    summed = _unpack(parts_sum, part_shapes)
    loss = summed[0][0]
    gsum = dict(zip(SMALL, summed[1:]))
    dmod_all = _allgather_small(_pack([small_g["ada_b"]]), "gather_dmod", parts_sum)
    dmod_all = dmod_all.reshape(N_DEV, DEPTH, 6 * D)
    dmod_cols = lax.dynamic_slice_in_dim(dmod_all, chip * cols, cols, axis=2).transpose(1, 0, 2)
    g_ada_w = _ada_bwd(c16, jnp.pad(dmod_cols, ((0, 0), (0, 16 - N_DEV), (0, 0))))

    shp = ada_w.shape
    two = lambda t: t.reshape(shp[0] * shp[1], shp[2])
    res["ada_w"] = [t.reshape(shp) for t in _adamw(two(ada_w), [two(g_ada_w)], two(m_ada_w), two(v_ada_w), "adamw_ada_w")]

    def mine(n):
        g = gsum[n]
        if n in SMALL_SHARDED:
            g = lax.dynamic_slice_in_dim(g, chip * w[n].shape[-1], w[n].shape[-1], axis=g.ndim - 1)
        return g.reshape(w[n].shape)

    for k, names in enumerate(([n for n in SMALL if n not in SMALL_WIDE_PADDED], list(SMALL_WIDE_PADDED))):
        outs = _adamw_many([w[n] for n in names], [mine(n) for n in names], [m[n] for n in names], [v[n] for n in names],
                           f"adamw_small{k}")
        for idx, n in enumerate(names):
            res[n] = [outs[part][idx] for part in range(4)]

    outs = [loss, grad_x]
    for part in range(4):
        outs += [res[n][part] for n in WEIGHTS]
    return tuple(outs)
```

```python
import functools

import jax
import jax.numpy as jnp
from jax import lax
from jax.experimental import pallas as pl
from jax.experimental.pallas import tpu as pltpu

F32 = jnp.float32
BF16 = jnp.bfloat16
D = 1024
EPS = 1e-6
DEPTH = 4
MIXER_OF_LAYER = (0, 1, 2, 0)
S5_G, S5_H, S5_P = 64, 16, 64
S5_NB = 4
S5_BH = S5_H * 16
S5_BP = S5_P * 16
NSTATE = S5_G * S5_P
SG_HEADS, SG_CHUNK = 8, 128
ADAM_LR, ADAM_B1, ADAM_B2, ADAM_EPS, ADAM_WD, ADAM_STEP = 0.001, 0.9, 0.999, 1e-08, 0.01, 10
N_DEV = 8
MESH = pl.DeviceIdType.MESH
LANES = 1024
R_SH1, R_SC1, R_G1, R_SH2, R_SC2, R_G2, R_N1, R_N2 = range(8)


def _dot(a, b):
    return jnp.dot(a, b, preferred_element_type=F32)


def _dot_nt(a, b):
    return lax.dot_general(a, b, (((1,), (1,)), ((), ())), preferred_element_type=F32)


def _dot_tn(a, b):
    return lax.dot_general(a, b, (((0,), (0,)), ((), ())), preferred_element_type=F32)


def _bf(x):
    return x.astype(BF16)


def _sum0(x):
    return jnp.sum(x, axis=0, keepdims=True)


def _params(n_axes, vmem_mb=48):
    return pltpu.CompilerParams(dimension_semantics=("arbitrary",) * n_axes, vmem_limit_bytes=vmem_mb << 20)


def _rows(tm, cols, nt=None):
    if nt is None:
        return pl.BlockSpec((tm, cols), lambda i: (i, 0))
    return pl.BlockSpec((tm, cols), lambda i: (nt - 1 - i, 0))


def _whole(shape):
    nd = len(shape)
    return pl.BlockSpec(shape, lambda *_: (0,) * nd)


def _layer_w(r, c, layer):
    return pl.BlockSpec((None, r, c), lambda *_: (layer, 0, 0), pipeline_mode=pl.Buffered(1))


def _const_w(shape):
    nd = len(shape)
    return pl.BlockSpec(shape, lambda *_: (0,) * nd, pipeline_mode=pl.Buffered(1))


def _call_after(after, body, n_in, *, in_specs, **kw):
    if after is None:
        return pl.pallas_call(body, in_specs=in_specs, **kw), ()

    def body_after(*refs):
        return body(*refs[:n_in], *refs[n_in + 1:])

    return pl.pallas_call(body_after, in_specs=list(in_specs) + [pl.BlockSpec(memory_space=pl.ANY)], **kw), (after,)


def _norm_mod(x, ng, sc, sh):
    r = lax.rsqrt(jnp.mean(x * x, axis=-1, keepdims=True) + EPS)
    xn = x * r
    return (xn * ng) * (1.0 + sc) + sh, xn, r


def _norm_mod_bwd(dh, xn, r, ng, sc):
    dxn = dh * (ng * (1.0 + sc))
    return r * (dxn - xn * jnp.mean(dxn * xn, axis=-1, keepdims=True))


def _shift_down(z, prev8, k):
    row = lax.broadcasted_iota(jnp.int32, z.shape, 0)
    if k == 1:
        return jnp.where(row >= 1, pltpu.roll(z, 1, 0), prev8[7:8])
    return jnp.where(row >= 2, pltpu.roll(z, 2, 0), jnp.where(row == 0, prev8[6:7], prev8[7:8]))


def _shift_up(z, next8, k):
    n = z.shape[0]
    row = lax.broadcasted_iota(jnp.int32, z.shape, 0)
    if k == 1:
        return jnp.where(row <= n - 2, pltpu.roll(z, n - 1, 0), next8[0:1])
    return jnp.where(row <= n - 3, pltpu.roll(z, n - 2, 0), jnp.where(row == n - 2, next8[0:1], next8[1:2]))


def _place():
    x, y, c = lax.axis_index("x"), lax.axis_index("y"), lax.axis_index("c")
    chips = [(1 - x, y), (x, 1 - y), (1 - x, 1 - y)]
    return x, y, c, chips


def _allgather_small(x_shard, name, after=None):
    m_per, n = x_shard.shape

    def body(x_ref, out_ref, send_sems, recv_sems, local_sem):
        x, y, c, chips = _place()
        me, sibling = (x, y, c), (x, y, 1 - c)

        def rows(px, py, pc):
            return out_ref.at[pl.ds((4 * px + 2 * py + pc) * m_per, m_per), :]

        def copy(k, block, to, src=None):
            return pltpu.make_async_remote_copy(
                src_ref=rows(*block) if src is None else src, dst_ref=rows(*block),
                send_sem=send_sems.at[k], recv_sem=recv_sems.at[k], device_id=to, device_id_type=MESH)

        mine = pltpu.make_async_copy(x_ref, rows(*me), local_sem)
        mine.start()
        first = [copy(0, me, sibling, src=x_ref)]
        first += [copy(1 + j, me, (*chip, c), src=x_ref) for j, chip in enumerate(chips)]
        for cp in first:
            cp.start()
        passed = [copy(4 + j, (*chip, c), sibling) for j, chip in enumerate(chips)]
        for j, chip in enumerate(chips):
            copy(1 + j, (*chip, c), me).wait_recv()
            passed[j].start()
        copy(0, sibling, me).wait_recv()
        for j, chip in enumerate(chips):
            copy(4 + j, (*chip, 1 - c), me).wait_recv()
        for cp in first + passed:
            cp.wait_send()
        mine.wait()

    call, tail = _call_after(
        after, body, 1, name=name, out_shape=jax.ShapeDtypeStruct((N_DEV * m_per, n), F32),
        in_specs=[pl.BlockSpec(memory_space=pltpu.VMEM)], out_specs=pl.BlockSpec(memory_space=pltpu.VMEM),
        scratch_shapes=[pltpu.SemaphoreType.DMA((7,)), pltpu.SemaphoreType.DMA((7,)), pltpu.SemaphoreType.DMA],
        compiler_params=pltpu.CompilerParams(vmem_limit_bytes=48 << 20),
    )
    return call(x_shard, *tail)


def _reduce_pair(x_part, name, after=None):
    m, n = x_part.shape
    h = m // 2

    def body(x_ref, slots_ref, sib_buf, send_sem, recv_sem):
        x, y, c, _ = _place()
        swap = pltpu.make_async_remote_copy(src_ref=x_ref, dst_ref=sib_buf, send_sem=send_sem, recv_sem=recv_sem,
                                            device_id=(x, y, 1 - c), device_id_type=MESH)
        swap.start()
        swap.wait()
        mine = pl.ds(pl.multiple_of(c * h, 8), h)
        slots_ref[pl.ds(2 * x + y, 1)] = (x_ref[mine, :] + sib_buf[mine, :])[None]

    call, tail = _call_after(
        after, body, 1, name=name, out_shape=jax.ShapeDtypeStruct((4, h, n), F32),
        in_specs=[pl.BlockSpec(memory_space=pltpu.VMEM)], out_specs=pl.BlockSpec(memory_space=pltpu.VMEM),
        scratch_shapes=[pltpu.VMEM((m, n), F32), pltpu.SemaphoreType.DMA, pltpu.SemaphoreType.DMA],
        compiler_params=pltpu.CompilerParams(vmem_limit_bytes=48 << 20),
    )
    return call(x_part, *tail)


def _reduce_cross_start(slots, name):
    def body(slots_ref, send_sems, recv_sems, thru, token):
        x, y, c, chips = _place()
        mine = slots_ref.at[pl.ds(2 * x + y, 1)]
        for j, chip in enumerate(chips):
            pltpu.make_async_remote_copy(src_ref=mine, dst_ref=mine, send_sem=send_sems.at[j], recv_sem=recv_sems.at[j],
                                         device_id=(*chip, c), device_id_type=MESH).start()
        token[...] = jnp.zeros_like(token)

    res = pl.pallas_call(
        body, name=name,
        out_shape=(pltpu.SemaphoreType.DMA((3,)), pltpu.SemaphoreType.DMA((3,)), pltpu.HBM(slots.shape, F32),
                   jax.ShapeDtypeStruct((8, 128), F32)),
        in_specs=[HBM_SPEC], out_specs=(SEM_SPEC, SEM_SPEC, HBM_SPEC, pl.BlockSpec(memory_space=pltpu.VMEM)),
        input_output_aliases={0: 2}, compiler_params=SPLIT_COPY_PARAMS,
    )(*_in_hbm([slots]))
    return res


def _reduce_cross_wait(send_sems, recv_sems, slots, name, after):
    def body(slots_ref, s_sems, r_sems, after_ref, thru):
        x, y, c, chips = _place()
        for j, chip in enumerate(chips):
            theirs = slots_ref.at[pl.ds(2 * chip[0] + chip[1], 1)]
            cp = pltpu.make_async_remote_copy(src_ref=theirs, dst_ref=theirs, send_sem=s_sems.at[j], recv_sem=r_sems.at[j],
                                              device_id=(x, y, c), device_id_type=MESH)
            cp.wait_send()
            cp.wait_recv()

    return pl.pallas_call(
        body, name=name, out_shape=pltpu.HBM(slots.shape, F32),
        in_specs=[HBM_SPEC, SEM_SPEC, SEM_SPEC, ANY_SPEC], out_specs=HBM_SPEC,
        input_output_aliases={0: 0}, compiler_params=SPLIT_COPY_PARAMS,
    )(slots, send_sems, recv_sems, after)


def _reduce_finish(slots, name):
    _, h, n = slots.shape

    def body(slots_ref, out_ref, send_sem, recv_sem):
        x, y, c, _ = _place()
        mine = pl.ds(pl.multiple_of(c * h, 8), h)
        theirs = pl.ds(pl.multiple_of((1 - c) * h, 8), h)
        out_ref[mine, :] = ((slots_ref[0] + slots_ref[1]) + slots_ref[2]) + slots_ref[3]
        give = pltpu.make_async_remote_copy(src_ref=out_ref.at[mine, :], dst_ref=out_ref.at[mine, :], send_sem=send_sem,
                                            recv_sem=recv_sem, device_id=(x, y, 1 - c), device_id_type=MESH)
        give.start()
        give.wait_send()
        pltpu.make_async_remote_copy(src_ref=out_ref.at[theirs, :], dst_ref=out_ref.at[theirs, :], send_sem=send_sem,
                                     recv_sem=recv_sem, device_id=(x, y, c), device_id_type=MESH).wait_recv()

    return pl.pallas_call(
        body, name=name, out_shape=jax.ShapeDtypeStruct((2 * h, n), F32),
        in_specs=[pl.BlockSpec(memory_space=pltpu.VMEM)], out_specs=pl.BlockSpec(memory_space=pltpu.VMEM),
        scratch_shapes=[pltpu.SemaphoreType.DMA, pltpu.SemaphoreType.DMA],
        compiler_params=pltpu.CompilerParams(vmem_limit_bytes=48 << 20),
    )(slots)


def _shard_region(ref, full_shape, axis, chip_k, half=None):
    _, r, c = full_shape
    if axis == 1:
        rs = r // 4
        if half is None:
            return ref.at[:, pl.ds(pl.multiple_of(chip_k * rs, 128), rs), :]
        return ref.at[:, pl.ds(pl.multiple_of(chip_k * rs + half * (rs // 2), 128), rs // 2), :]
    cs = c // 4
    if half is None:
        return ref.at[:, :, pl.ds(pl.multiple_of(chip_k * cs, 128), cs)]
    return ref.at[:, pl.ds(pl.multiple_of(half * (r // 2), 128), r // 2), pl.ds(pl.multiple_of(chip_k * cs, 128), cs)]


HBM_SPEC = pl.BlockSpec(memory_space=pltpu.HBM)
SEM_SPEC = pl.BlockSpec(memory_space=pltpu.SEMAPHORE)
ANY_SPEC = pl.BlockSpec(memory_space=pl.ANY)
SPLIT_COPY_PARAMS = pltpu.CompilerParams(has_side_effects=pltpu.SideEffectType.DATAFLOW_SIDE_EFFECTING)


def _in_hbm(arrs):
    return [pltpu.with_memory_space_constraint(a, pltpu.HBM) for a in arrs]


def _cast_place(ws, layers, axes, place, name, after=None):
    n_arr = len(ws)
    in_specs, out_specs, fulls = [], [], []
    for w_stack, li, axis in zip(ws, layers, axes):
        _, r, c = w_stack.shape
        tr = r // 4
        fulls.append((1, 4 * r, c) if axis == 1 else (1, r, 4 * c))
        in_specs.append(pl.BlockSpec((None, tr, c), lambda i, p, li=li: (li, 2 * p[1] + i, 0)))
        if axis == 1:
            out_specs.append(pl.BlockSpec((None, tr, c), lambda i, p: (0, 4 * p[0] + 2 * p[1] + i, 0)))
        else:
            out_specs.append(pl.BlockSpec((None, tr, c), lambda i, p: (0, 2 * p[1] + i, p[0])))

    extra = [] if after is None else [after]

    def body(p_ref, *refs):
        for a in range(n_arr):
            refs[n_arr + len(extra) + a][...] = _bf(refs[a][...])

    return pl.pallas_call(
        body, name=name,
        grid_spec=pltpu.PrefetchScalarGridSpec(num_scalar_prefetch=1, grid=(2,), in_specs=in_specs + [ANY_SPEC] * len(extra),
                                               out_specs=out_specs),
        out_shape=[jax.ShapeDtypeStruct(f, BF16) for f in fulls],
        compiler_params=_params(1),
    )(place, *ws, *extra)


def _gather_start(lands, axes, name, after):
    n_arr = len(lands)
    fulls = [tuple(l.shape) for l in lands]

    def body(*refs):
        land = refs[:n_arr]
        send_sems, recv_sems = refs[n_arr + 1:n_arr + 3]
        token = refs[-1]
        x, y, c, chips = _place()
        k_me = 2 * x + y
        for a in range(n_arr):
            mine = _shard_region(land[a], fulls[a], axes[a], k_me, c)
            for j, chip in enumerate(chips):
                pltpu.make_async_remote_copy(
                    src_ref=mine, dst_ref=mine, send_sem=send_sems.at[a * 3 + j], recv_sem=recv_sems.at[a * 3 + j],
                    device_id=(*chip, c), device_id_type=MESH).start()
        token[...] = jnp.zeros_like(token)

    res = pl.pallas_call(
        body, name=name,
        out_shape=(pltpu.SemaphoreType.DMA((3 * n_arr,)), pltpu.SemaphoreType.DMA((3 * n_arr,)),
                   *[pltpu.HBM(f, BF16) for f in fulls], jax.ShapeDtypeStruct((8, 128), F32)),
        in_specs=[HBM_SPEC] * n_arr + [ANY_SPEC],
        out_specs=(SEM_SPEC, SEM_SPEC, *[HBM_SPEC] * n_arr, pl.BlockSpec(memory_space=pltpu.VMEM)),
        input_output_aliases={a: 2 + a for a in range(n_arr)},
        compiler_params=SPLIT_COPY_PARAMS,
    )(*_in_hbm(lands), after)
    return res[0], res[1], list(res[2:2 + n_arr]), res[-1]


def _gather_wait(send_sems, recv_sems, lands, axes, name, after):
    n_arr = len(lands)
    fulls = [tuple(l.shape) for l in lands]

    def body(*refs):
        land = refs[:n_arr]
        s_sems, r_sems = refs[n_arr:n_arr + 2]
        x, y, c, chips = _place()
        for a in range(n_arr):
            for j, chip in enumerate(chips):
                k_j = 2 * chip[0] + chip[1]
                got = _shard_region(land[a], fulls[a], axes[a], k_j, c)
                cp = pltpu.make_async_remote_copy(
                    src_ref=got, dst_ref=got, send_sem=s_sems.at[a * 3 + j], recv_sem=r_sems.at[a * 3 + j],
                    device_id=(x, y, c), device_id_type=MESH)
                cp.wait_send()
                cp.wait_recv()

    res = pl.pallas_call(
        body, name=name,
        out_shape=tuple(pltpu.HBM(f, BF16) for f in fulls),
        in_specs=[HBM_SPEC] * n_arr + [SEM_SPEC, SEM_SPEC] + [ANY_SPEC] * len(after),
        out_specs=tuple([HBM_SPEC] * n_arr),
        input_output_aliases={a: a for a in range(n_arr)},
        compiler_params=SPLIT_COPY_PARAMS,
    )(*lands, send_sems, recv_sems, *after)
    return list(res)


def _gather_share(lands, axes, name):
    n_arr = len(lands)
    fulls = [tuple(l.shape) for l in lands]

    def body(*refs):
        land_in, land = refs[:n_arr], refs[n_arr:2 * n_arr]
        send_sems, recv_sems = refs[2 * n_arr:]
        x, y, c, chips = _place()
        copies = []
        for a in range(n_arr):
            for j, k_j in enumerate([2 * chip[0] + chip[1] for chip in chips] + [2 * x + y]):
                cp = pltpu.make_async_remote_copy(
                    src_ref=_shard_region(land_in[a], fulls[a], axes[a], k_j, c),
                    dst_ref=_shard_region(land[a], fulls[a], axes[a], k_j, c),
                    send_sem=send_sems.at[a * 4 + j], recv_sem=recv_sems.at[a * 4 + j],
                    device_id=(x, y, 1 - c), device_id_type=MESH)
                cp.start()
                copies.append(cp)
        for cp in copies:
            cp.wait()

    return pl.pallas_call(
        body, name=name, out_shape=[jax.ShapeDtypeStruct(f, BF16) for f in fulls],
        in_specs=[ANY_SPEC] * n_arr, out_specs=[ANY_SPEC] * n_arr,
        input_output_aliases={a: a for a in range(n_arr)},
        scratch_shapes=[pltpu.SemaphoreType.DMA((4 * n_arr,)), pltpu.SemaphoreType.DMA((4 * n_arr,))],
    )(*lands)


def _scatter_shapes(grads, axes):
    out = []
    for g, ax in zip(grads, axes):
        shp = list(g.shape)
        shp[ax] //= 4
        out.append((3,) + tuple(shp[1:]))
    return out


def _scatter_start(grads, axes, name, after):
    n_arr = len(grads)
    shapes = _scatter_shapes(grads, axes)
    lands = [lax.empty(s, BF16) for s in shapes]

    def body(*refs):
        ins, land = refs[:n_arr], refs[n_arr:2 * n_arr]
        send_sems, recv_sems = refs[2 * n_arr + 1:2 * n_arr + 3]
        token = refs[-1]
        x, y, c, chips = _place()
        for a in range(n_arr):
            for j, chip in enumerate(chips):
                k_j = 2 * chip[0] + chip[1]
                pltpu.make_async_remote_copy(
                    src_ref=_shard_region(ins[a], grads[a].shape, axes[a], k_j), dst_ref=land[a].at[pl.ds(j, 1)],
                    send_sem=send_sems.at[a * 3 + j], recv_sem=recv_sems.at[a * 3 + j],
                    device_id=(*chip, c), device_id_type=MESH).start()
        token[...] = jnp.zeros_like(token)

    res = pl.pallas_call(
        body, name=name,
        out_shape=(pltpu.SemaphoreType.DMA((3 * n_arr,)), pltpu.SemaphoreType.DMA((3 * n_arr,)),
                   *[pltpu.HBM(g.shape, BF16) for g in grads], *[pltpu.HBM(s, BF16) for s in shapes],
                   jax.ShapeDtypeStruct((8, 128), F32)),
        in_specs=[HBM_SPEC] * (2 * n_arr) + [ANY_SPEC],
        out_specs=(SEM_SPEC, SEM_SPEC, *[HBM_SPEC] * (2 * n_arr), pl.BlockSpec(memory_space=pltpu.VMEM)),
        input_output_aliases={a: 2 + a for a in range(2 * n_arr)},
        compiler_params=SPLIT_COPY_PARAMS,
    )(*_in_hbm(grads), *_in_hbm(lands), after)
    return res[0], res[1], list(res[2:2 + n_arr]), list(res[2 + n_arr:2 + 2 * n_arr]), res[-1]


def _scatter_wait(send_sems, recv_sems, grads, lands, axes, name, after):
    n_arr = len(grads)

    def body(*refs):
        ins, land = refs[:n_arr], refs[n_arr:2 * n_arr]
        s_sems, r_sems = refs[2 * n_arr:2 * n_arr + 2]
        x, y, c, chips = _place()
        for a in range(n_arr):
            for j, chip in enumerate(chips):
                k_j = 2 * chip[0] + chip[1]
                cp = pltpu.make_async_remote_copy(
                    src_ref=_shard_region(ins[a], grads[a].shape, axes[a], k_j), dst_ref=land[a].at[pl.ds(j, 1)],
                    send_sem=s_sems.at[a * 3 + j], recv_sem=r_sems.at[a * 3 + j],
                    device_id=(x, y, c), device_id_type=MESH)
                cp.wait_send()
                cp.wait_recv()

    res = pl.pallas_call(
        body, name=name,
        out_shape=(*[pltpu.HBM(g.shape, BF16) for g in grads], *[pltpu.HBM(l.shape, BF16) for l in lands]),
        in_specs=[HBM_SPEC] * (2 * n_arr) + [SEM_SPEC, SEM_SPEC, ANY_SPEC],
        out_specs=tuple([HBM_SPEC] * (2 * n_arr)),
        input_output_aliases={a: a for a in range(2 * n_arr)},
        compiler_params=SPLIT_COPY_PARAMS,
    )(*grads, *lands, send_sems, recv_sems, after)
    return list(res[:n_arr]), list(res[n_arr:])


def _swap_start(arrs, name, after):
    n_arr = len(arrs)
    lands = [lax.empty(a.shape, a.dtype) for a in arrs]

    def body(*refs):
        ins, land = refs[:n_arr], refs[n_arr:2 * n_arr]
        send_sems, recv_sems = refs[2 * n_arr + 1:2 * n_arr + 3]
        token = refs[-1]
        x, y, c, _ = _place()
        for a in range(n_arr):
            pltpu.make_async_remote_copy(
                src_ref=ins[a], dst_ref=land[a], send_sem=send_sems.at[a], recv_sem=recv_sems.at[a],
                device_id=(x, y, 1 - c), device_id_type=MESH).start()
        token[...] = jnp.zeros_like(token)

    res = pl.pallas_call(
        body, name=name,
        out_shape=(pltpu.SemaphoreType.DMA((n_arr,)), pltpu.SemaphoreType.DMA((n_arr,)),
                   *[pltpu.HBM(a.shape, a.dtype) for a in arrs], *[pltpu.HBM(a.shape, a.dtype) for a in arrs],
                   jax.ShapeDtypeStruct((8, 128), F32)),
        in_specs=[HBM_SPEC] * (2 * n_arr) + [ANY_SPEC],
        out_specs=(SEM_SPEC, SEM_SPEC, *[HBM_SPEC] * (2 * n_arr), pl.BlockSpec(memory_space=pltpu.VMEM)),
        input_output_aliases={a: 2 + a for a in range(2 * n_arr)},
        compiler_params=SPLIT_COPY_PARAMS,
    )(*_in_hbm(arrs), *_in_hbm(lands), after)
    return res[0], res[1], list(res[2:2 + n_arr]), list(res[2 + n_arr:2 + 2 * n_arr]), res[-1]


def _swap_wait(send_sems, recv_sems, arrs, lands, name, after):
    n_arr = len(arrs)

    def body(*refs):
        ins, land = refs[:n_arr], refs[n_arr:2 * n_arr]
        s_sems, r_sems = refs[2 * n_arr:2 * n_arr + 2]
        x, y, c, _ = _place()
        for a in range(n_arr):
            cp = pltpu.make_async_remote_copy(
                src_ref=ins[a], dst_ref=land[a], send_sem=s_sems.at[a], recv_sem=r_sems.at[a],
                device_id=(x, y, c), device_id_type=MESH)
            cp.wait_send()
            cp.wait_recv()

    res = pl.pallas_call(
        body, name=name,
        out_shape=(*[pltpu.HBM(a.shape, a.dtype) for a in arrs], *[pltpu.HBM(a.shape, a.dtype) for a in arrs]),
        in_specs=[HBM_SPEC] * (2 * n_arr) + [SEM_SPEC, SEM_SPEC, ANY_SPEC],
        out_specs=tuple([HBM_SPEC] * (2 * n_arr)),
        input_output_aliases={a: a for a in range(2 * n_arr)},
        compiler_params=SPLIT_COPY_PARAMS,
    )(*arrs, *lands, send_sems, recv_sems, after)
    return list(res[:n_arr]), list(res[n_arr:])


def _swap_with_sibling(arrs, name):
    n_arr = len(arrs)

    def body(*refs):
        ins, outs = refs[:n_arr], refs[n_arr:2 * n_arr]
        send_sems, recv_sems = refs[2 * n_arr:]
        x, y, c, _ = _place()
        copies = []
        for a in range(n_arr):
            cp = pltpu.make_async_remote_copy(
                src_ref=ins[a], dst_ref=outs[a], send_sem=send_sems.at[a], recv_sem=recv_sems.at[a],
                device_id=(x, y, 1 - c), device_id_type=MESH)
            cp.start()
            copies.append(cp)
        for cp in copies:
            cp.wait()

    any_spec = pl.BlockSpec(memory_space=pl.ANY)
    return pl.pallas_call(
        body, name=name, out_shape=[jax.ShapeDtypeStruct(a.shape, a.dtype) for a in arrs],
        in_specs=[any_spec] * n_arr, out_specs=[any_spec] * n_arr,
        scratch_shapes=[pltpu.SemaphoreType.DMA((n_arr,)), pltpu.SemaphoreType.DMA((n_arr,))],
    )(*arrs)


def _mm_tn(a, b, name, out_dtype=BF16):
    L, m = a.shape
    n = b.shape[1]
    bm, bn, bk = min(m, 1024), min(n, 1024), min(L, 2048)
    nk = L // bk

    def body(a_ref, b_ref, o_ref, acc):
        k = pl.program_id(2)

        @pl.when(k == 0)
        def _():
            acc[...] = jnp.zeros_like(acc)

        acc[...] += _dot_tn(_bf(a_ref[...]), _bf(b_ref[...]))

        @pl.when(k == nk - 1)
        def _():
            o_ref[...] = acc[...].astype(out_dtype)

    return pl.pallas_call(
        body, name=name, grid=(m // bm, n // bn, nk),
        in_specs=[pl.BlockSpec((bk, bm), lambda i, j, k: (k, i)), pl.BlockSpec((bk, bn), lambda i, j, k: (k, j))],
        out_specs=pl.BlockSpec((bm, bn), lambda i, j, k: (i, j)),
        out_shape=jax.ShapeDtypeStruct((m, n), out_dtype),
        scratch_shapes=[pltpu.VMEM((bm, bn), F32)],
        compiler_params=_params(3),
    )(a, b)


def _mm_tn_blocks(a, b, wa, wb, name, after=None):
    L = a.shape[0]
    nb = a.shape[1] // wa
    bk = min(L, 1024)
    nk = L // bk

    def body(a_ref, b_ref, o_ref):
        @pl.when(pl.program_id(1) == 0)
        def _():
            o_ref[...] = jnp.zeros_like(o_ref)

        o_ref[...] += _dot_tn(_bf(a_ref[...]), _bf(b_ref[...]))

    call, tail = _call_after(
        after, body, 2, name=name, grid=(nb, nk),
        in_specs=[pl.BlockSpec((bk, wa), lambda j, k: (k, j)), pl.BlockSpec((bk, wb), lambda j, k: (k, j))],
        out_specs=pl.BlockSpec((None, wa, wb), lambda j, k: (j, 0, 0)),
        out_shape=jax.ShapeDtypeStruct((nb, wa, wb), F32),
        compiler_params=_params(2),
    )
    return call(a, b, *tail)


def _sum_parts(parts, owns, axes, chip, name):
    n_arr = len(parts)
    steps = 4
    in_specs, out_specs, shapes = [], [], []
    for part, axis in zip(parts, axes):
        _, r, c = part.shape
        tr = r // steps
        shapes.append((r, c))
        in_specs.append(pl.BlockSpec((3, tr, c), lambda i, k: (0, i, 0)))
        out_specs.append(pl.BlockSpec((tr, c), lambda i, k: (i, 0)))
    for part, axis in zip(parts, axes):
        _, r, c = part.shape
        tr = r // steps
        if axis == 1:
            in_specs.append(pl.BlockSpec((None, tr, c), lambda i, k: (0, steps * k[0] + i, 0)))
        else:
            in_specs.append(pl.BlockSpec((None, tr, c), lambda i, k: (0, i, k[0])))

    def body(k_ref, *refs):
        for a in range(n_arr):
            p = refs[a][...].astype(F32)
            refs[2 * n_arr + a][...] = ((p[0] + p[1]) + p[2]) + refs[n_arr + a][...].astype(F32)

    return pl.pallas_call(
        body, name=name,
        grid_spec=pltpu.PrefetchScalarGridSpec(num_scalar_prefetch=1, grid=(steps,), in_specs=in_specs, out_specs=out_specs),
        out_shape=[jax.ShapeDtypeStruct(sh, F32) for sh in shapes],
        compiler_params=_params(1),
    )(chip, *parts, *owns)


def _adamw(w, g_parts, m, v, name):
    n_g = len(g_parts)
    if w.ndim == 2:
        r, c = w.shape
        tr = r
        for cand in (512, 256, 128, 64, 32, 16, 8):
            if r % cand == 0 and cand * c * 4 <= (2 << 20):
                tr = cand
                break
        spec = pl.BlockSpec((tr, c), lambda i: (i, 0))
        tiling = dict(grid=(r // tr,), in_specs=[spec] * (3 + n_g), out_specs=[spec] * 4, compiler_params=_params(1))
    else:
        tiling = dict(compiler_params=pltpu.CompilerParams(vmem_limit_bytes=48 << 20))

    def body(*refs):
        w_ref, g_refs, m_ref, v_ref = refs[0], refs[1:1 + n_g], refs[1 + n_g], refs[2 + n_g]
        g = g_refs[0][...]
        for gr in g_refs[1:]:
            g = g + gr[...]
        _adamw_update(g, w_ref, m_ref, v_ref, *refs[3 + n_g:])

    return pl.pallas_call(body, name=name, out_shape=[jax.ShapeDtypeStruct(w.shape, F32)] * 4, **tiling)(w, *g_parts, m, v)


def _adamw_update(g, w_ref, m_ref, v_ref, g_out, d_out, m_out, v_out):
    m_new = ADAM_B1 * m_ref[...] + (1.0 - ADAM_B1) * g
    v_new = ADAM_B2 * v_ref[...] + (1.0 - ADAM_B2) * (g * g)
    m_hat = m_new * (1.0 / (1.0 - ADAM_B1 ** ADAM_STEP))
    v_hat = v_new * (1.0 / (1.0 - ADAM_B2 ** ADAM_STEP))
    g_out[...] = g
    d_out[...] = -ADAM_LR * (m_hat / (jnp.sqrt(v_hat) + ADAM_EPS) + ADAM_WD * w_ref[...])
    m_out[...] = m_new
    v_out[...] = v_new


def _adamw_many(ws, gs, ms, vs, name):
    n = len(ws)

    def body(*refs):
        for k in range(n):
            _adamw_update(refs[n + k][...], refs[k], refs[2 * n + k], refs[3 * n + k],
                          refs[4 * n + k], refs[5 * n + k], refs[6 * n + k], refs[7 * n + k])

    outs = pl.pallas_call(body, name=name, out_shape=[jax.ShapeDtypeStruct(t.shape, F32) for t in ws] * 4,
                          compiler_params=pltpu.CompilerParams(vmem_limit_bytes=56 << 20))(*ws, *gs, *ms, *vs)
    return [outs[part * n:(part + 1) * n] for part in range(4)]


def _adamw_layers(w, q_mine, q_sib, m, v, name, after=None):
    n, r, c = w.shape
    tr = r
    for cand in (512, 256, 128, 64, 32, 16, 8):
        if r % cand == 0 and cand * c * 4 <= (1 << 20):
            tr = cand
            break

    def body(*refs):
        w_ref, qm, qs, m_ref, v_ref = refs[0], refs[1:1 + n], refs[1 + n:1 + 2 * n], refs[1 + 2 * n], refs[2 + 2 * n]
        layer = pl.program_id(0)
        g = qm[0][...] + qs[0][...]
        for k in range(1, n):
            g = jnp.where(layer == k, qm[k][...] + qs[k][...], g)
        _adamw_update(g, w_ref, m_ref, v_ref, *refs[3 + 2 * n:])

    stacked = pl.BlockSpec((None, tr, c), lambda l, i: (l, i, 0))
    per_layer = [pl.BlockSpec((tr, c), lambda l, i, k=k: (jnp.where(l == k, i, 0), 0)) for k in range(n)]
    call, tail = _call_after(
        after, body, 3 + 2 * n, name=name, grid=(n, r // tr),
        in_specs=[stacked] + per_layer + per_layer + [stacked, stacked], out_specs=[stacked] * 4,
        out_shape=[jax.ShapeDtypeStruct(w.shape, F32)] * 4,
        compiler_params=_params(2),
    )
    return call(w, *q_mine, *q_sib, m, v, *tail)


def _ada_fwd(c16, ada_w, ada_b_cols):
    cols = ada_w.shape[2]

    def body(c_ref, w_ref, b_ref, o_ref):
        cv = c_ref[...]
        ca = _bf(cv * jax.nn.sigmoid(cv))
        o_ref[...] = _dot(ca, _bf(w_ref[...])) + b_ref[...]

    return pl.pallas_call(
        body, name="ada_fwd", grid=(DEPTH,),
        in_specs=[_whole((16, D)), pl.BlockSpec((None, D, cols), lambda i: (i, 0, 0)),
                  pl.BlockSpec((None, 1, cols), lambda i: (i, 0, 0))],
        out_specs=pl.BlockSpec((None, 16, cols), lambda i: (i, 0, 0)),
        out_shape=jax.ShapeDtypeStruct((DEPTH, 16, cols), F32),
        compiler_params=_params(1),
    )(c16, ada_w, ada_b_cols)


def _ada_bwd(c16, dmod16):
    cols = dmod16.shape[2]

    def body(c_ref, d_ref, o_ref):
        cv = c_ref[...]
        ca = _bf(cv * jax.nn.sigmoid(cv))
        o_ref[...] = _dot_tn(ca, _bf(d_ref[...]))

    return pl.pallas_call(
        body, name="ada_bwd", grid=(DEPTH,),
        in_specs=[_whole((16, D)), pl.BlockSpec((None, 16, cols), lambda i: (i, 0, 0))],
        out_specs=pl.BlockSpec((None, D, cols), lambda i: (i, 0, 0)),
        out_shape=jax.ShapeDtypeStruct((DEPTH, D, cols), F32),
        compiler_params=_params(1),
    )(c16, dmod16)


def _mod_bwd(vs_mix, vs_ffn, pv):
    def body(m_ref, f_ref, pv_ref, o_ref):
        for i in range(DEPTH):
            vm, vf, p = m_ref[i], f_ref[i], pv_ref[i]
            o_ref[i] = jnp.concatenate([
                vm[2:3], vm[1:2] * p[R_N1:R_N1 + 1], vm[0:1],
                vf[2:3], vf[1:2] * p[R_N2:R_N2 + 1], vf[0:1],
                vm[1:2] * (1.0 + p[R_SC1:R_SC1 + 1]), vf[1:2] * (1.0 + p[R_SC2:R_SC2 + 1])], axis=0)

    return pl.pallas_call(body, name="mod_bwd", out_shape=jax.ShapeDtypeStruct((DEPTH, 8, D), F32))(vs_mix, vs_ffn, pv)


def _ffn_fwd(x1, pv, w1, w2, layer, tm, after=None):
    L = x1.shape[0]
    dff = w1.shape[2]

    def body(x1_ref, pv_ref, w1_ref, w2_ref, x2_ref, h2_ref, a_ref, f_ref):
        x1v, p = x1_ref[...], pv_ref[...]
        h2, _, _ = _norm_mod(x1v, p[R_N2:R_N2 + 1], p[R_SC2:R_SC2 + 1], p[R_SH2:R_SH2 + 1])
        hb = _bf(h2)
        h2_ref[...] = hb
        a = _dot(hb, w1_ref[...])
        a_ref[...] = a
        ra = jnp.maximum(a, 0.0)
        f = _dot(_bf(ra * ra), w2_ref[...])
        f_ref[...] = f
        x2_ref[...] = x1v + p[R_G2:R_G2 + 1] * f

    call, tail = _call_after(
        after, body, 4, name=f"ffn_fwd{layer}", grid=(L // tm,),
        in_specs=[_rows(tm, D), pl.BlockSpec((None, 8, D), lambda i: (layer, 0, 0)), _layer_w(D, dff, 0), _layer_w(dff, D, 0)],
        out_specs=[_rows(tm, D), _rows(tm, D), _rows(tm, dff), _rows(tm, D)],
        out_shape=[jax.ShapeDtypeStruct((L, D), F32), jax.ShapeDtypeStruct((L, D), BF16),
                   jax.ShapeDtypeStruct((L, dff), F32), jax.ShapeDtypeStruct((L, D), F32)],
        compiler_params=_params(1, 56),
    )
    return call(x1, pv, w1, w2, *tail)


def _ffn_bwd(dx2, x1, a, f, pv, w1, w2, layer, tm, after=None):
    L = x1.shape[0]
    dff = w1.shape[2]
    extra = [] if after is None else [pl.BlockSpec(memory_space=pl.ANY)]
    extra_args = [] if after is None else [after]

    def body(dx2_ref, x1_ref, a_ref, f_ref, pv_ref, w1_ref, w2_ref, *rest):
        dx1_ref, p_ref, da_ref, df_ref, vs_ref = rest[len(extra):]

        @pl.when(pl.program_id(0) == 0)
        def _():
            vs_ref[...] = jnp.zeros_like(vs_ref)

        dx2v, p = dx2_ref[...], pv_ref[...]
        dfb = _bf(dx2v * p[R_G2:R_G2 + 1])
        df_ref[...] = dfb
        vs_ref[0:1, :] += _sum0(dx2v * f_ref[...])
        dp = _dot_nt(dfb, w2_ref[...])
        ra = jnp.maximum(a_ref[...], 0.0)
        p_ref[...] = _bf(ra * ra)
        dab = _bf(dp * (2.0 * ra))
        da_ref[...] = dab
        dh2 = _dot_nt(dab, w1_ref[...])
        _, xn, r = _norm_mod(x1_ref[...], p[R_N2:R_N2 + 1], p[R_SC2:R_SC2 + 1], p[R_SH2:R_SH2 + 1])
        dx1_ref[...] = dx2v + _norm_mod_bwd(dh2, xn, r, p[R_N2:R_N2 + 1], p[R_SC2:R_SC2 + 1])
        vs_ref[1:2, :] += _sum0(dh2 * xn)
        vs_ref[2:3, :] += _sum0(dh2)

    return pl.pallas_call(
        body, name=f"ffn_bwd{layer}", grid=(L // tm,),
        in_specs=[_rows(tm, D), _rows(tm, D), _rows(tm, dff), _rows(tm, D),
                  pl.BlockSpec((None, 8, D), lambda i: (layer, 0, 0)), _layer_w(D, dff, 0), _layer_w(dff, D, 0)] + extra,
        out_specs=[_rows(tm, D), _rows(tm, dff), _rows(tm, dff), _rows(tm, D), _whole((8, D))],
        out_shape=[jax.ShapeDtypeStruct((L, D), F32), jax.ShapeDtypeStruct((L, dff), BF16),
                   jax.ShapeDtypeStruct((L, dff), BF16), jax.ShapeDtypeStruct((L, D), BF16),
                   jax.ShapeDtypeStruct((8, D), F32)],
        compiler_params=_params(1, 56),
    )(dx2, x1, a, f, pv, w1, w2, *extra_args)


def _conv_fwd(x, pv, w_in, w_out, cw, layer, j, tm, after=None):
    L = x.shape[0]

    def body(x_ref, pv_ref, win_ref, wout_ref, cw_ref, x1_ref, h_ref, bcx_ref, conv_ref, q_ref, y_ref, carry):
        @pl.when(pl.program_id(0) == 0)
        def _():
            carry[...] = jnp.zeros_like(carry)

        xv, p, cwv = x_ref[...], pv_ref[...], cw_ref[...]
        h, _, _ = _norm_mod(xv, p[R_N1:R_N1 + 1], p[R_SC1:R_SC1 + 1], p[R_SH1:R_SH1 + 1])
        hb = _bf(h)
        h_ref[...] = hb
        bcx = _dot(hb, win_ref[...])
        bcx_ref[...] = bcx
        z = bcx[:, D:2 * D] * bcx[:, 2 * D:]
        prev8 = carry[...]
        conv = cwv[0:1] * _shift_down(z, prev8, 2) + cwv[1:2] * _shift_down(z, prev8, 1) + cwv[2:3] * z + cwv[3:4]
        conv_ref[...] = conv
        qb = _bf(bcx[:, :D] * conv)
        q_ref[...] = qb
        y = _dot(qb, wout_ref[...])
        y_ref[...] = y
        x1_ref[...] = xv + p[R_G1:R_G1 + 1] * y
        carry[...] = z[tm - 8:tm]

    call, tail = _call_after(
        after, body, 5, name=f"conv_fwd{layer}", grid=(L // tm,),
        in_specs=[_rows(tm, D), pl.BlockSpec((None, 8, D), lambda i: (layer, 0, 0)), _layer_w(D, 3 * D, 0), _layer_w(D, D, 0),
                  pl.BlockSpec((None, 8, D), lambda i: (j, 0, 0))],
        out_specs=[_rows(tm, D), _rows(tm, D), _rows(tm, 3 * D), _rows(tm, D), _rows(tm, D), _rows(tm, D)],
        out_shape=[jax.ShapeDtypeStruct((L, D), F32), jax.ShapeDtypeStruct((L, D), BF16), jax.ShapeDtypeStruct((L, 3 * D), F32),
                   jax.ShapeDtypeStruct((L, D), F32), jax.ShapeDtypeStruct((L, D), BF16), jax.ShapeDtypeStruct((L, D), F32)],
        scratch_shapes=[pltpu.VMEM((8, D), F32)],
        compiler_params=_params(1, 56),
    )
    return call(x, pv, w_in, w_out, cw, *tail)


def _conv_bwd(dx1, x, y, bcx, conv, pv, w_in, w_out, cw, layer, j, tm, after=None):
    L = x.shape[0]
    nt = L // tm

    def body(dx1_ref, x_ref, y_ref, bcx_ref, conv_ref, halo_ref, pv_ref, win_ref, wout_ref, cw_ref,
             dx_ref, dbcx_ref, dy_ref, vs_ref, carry):
        gi = pl.program_id(0)
        tile = nt - 1 - gi

        @pl.when(gi == 0)
        def _():
            vs_ref[...] = jnp.zeros_like(vs_ref)
            carry[...] = jnp.zeros_like(carry)

        dx1v, p, cwv = dx1_ref[...], pv_ref[...], cw_ref[...]
        dyb = _bf(dx1v * p[R_G1:R_G1 + 1])
        dy_ref[...] = dyb
        vs_ref[0:1, :] += _sum0(dx1v * y_ref[...])
        dq = _dot_nt(dyb, wout_ref[...])
        bcx = bcx_ref[...]
        b, cg, xh = bcx[:, :D], bcx[:, D:2 * D], bcx[:, 2 * D:]
        db = dq * conv_ref[...]
        dc = dq * b
        z = cg * xh
        halo = halo_ref[...]
        zprev = jnp.where(tile > 0, halo[:, D:2 * D] * halo[:, 2 * D:], 0.0)
        vs_ref[3:4, :] += _sum0(dc * _shift_down(z, zprev, 2))
        vs_ref[4:5, :] += _sum0(dc * _shift_down(z, zprev, 1))
        vs_ref[5:6, :] += _sum0(dc * z)
        vs_ref[6:7, :] += _sum0(dc)
        next8 = carry[...]
        dz = cwv[2:3] * dc + cwv[1:2] * _shift_up(dc, next8, 1) + cwv[0:1] * _shift_up(dc, next8, 2)
        dbb, dcgb, dxhb = _bf(db), _bf(dz * xh), _bf(dz * cg)
        dbcx_ref[:, 0:D] = dbb
        dbcx_ref[:, D:2 * D] = dcgb
        dbcx_ref[:, 2 * D:3 * D] = dxhb
        dh = (_dot_nt(dbb, win_ref[:, 0:D]) + _dot_nt(dcgb, win_ref[:, D:2 * D])) + _dot_nt(dxhb, win_ref[:, 2 * D:3 * D])
        _, xn, r = _norm_mod(x_ref[...], p[R_N1:R_N1 + 1], p[R_SC1:R_SC1 + 1], p[R_SH1:R_SH1 + 1])
        dx_ref[...] = dx1v + _norm_mod_bwd(dh, xn, r, p[R_N1:R_N1 + 1], p[R_SC1:R_SC1 + 1])
        vs_ref[1:2, :] += _sum0(dh * xn)
        vs_ref[2:3, :] += _sum0(dh)
        carry[...] = dc[0:8]

    halo_spec = pl.BlockSpec((8, 3 * D), lambda i: (jnp.maximum((nt - 1 - i) * (tm // 8) - 1, 0), 0))
    call, tail = _call_after(
        after, body, 10, name=f"conv_bwd{layer}", grid=(nt,),
        in_specs=[_rows(tm, D, nt), _rows(tm, D, nt), _rows(tm, D, nt), _rows(tm, 3 * D, nt), _rows(tm, D, nt), halo_spec,
                  pl.BlockSpec((None, 8, D), lambda i: (layer, 0, 0)), _layer_w(D, 3 * D, 0), _layer_w(D, D, 0),
                  pl.BlockSpec((None, 8, D), lambda i: (j, 0, 0))],
        out_specs=[_rows(tm, D, nt), _rows(tm, 3 * D, nt), _rows(tm, D, nt), _whole((8, D))],
        out_shape=[jax.ShapeDtypeStruct((L, D), F32), jax.ShapeDtypeStruct((L, 3 * D), BF16),
                   jax.ShapeDtypeStruct((L, D), BF16), jax.ShapeDtypeStruct((8, D), F32)],
        scratch_shapes=[pltpu.VMEM((8, D), F32)],
        compiler_params=_params(1, 56),
    )
    return call(dx1, x, y, bcx, conv, bcx, pv, w_in, w_out, cw, *tail)


def _s5_discretize(a_re, a_im, log_dt, bt_re, bt_im):
    dt = jnp.exp(log_dt)
    mag = jnp.exp(a_re * dt)
    abar_re = mag * jnp.cos(a_im * dt)
    abar_im = mag * jnp.sin(a_im * dt)
    den = a_re * a_re + a_im * a_im
    nr = abar_re - 1.0
    ni = abar_im
    f_re = (nr * a_re + ni * a_im) / den
    f_im = (ni * a_re - nr * a_im) / den
    bbar_re = f_re * bt_re - f_im * bt_im
    bbar_im = f_re * bt_im + f_im * bt_re
    return abar_re, abar_im, bbar_re, bbar_im


def _s5_params_fwd(a_re, a_im, log_dt, bt_re, bt_im, after=None):
    def body(ar, ai, ld, br, bi, o_ar, o_ai, o_br, o_bi):
        r = _s5_discretize(ar[...], ai[...], ld[...], br[...], bi[...])
        o_ar[...], o_ai[...], o_br[...], o_bi[...] = r

    gp = jax.ShapeDtypeStruct((S5_G, S5_P), F32)
    hgp = jax.ShapeDtypeStruct((S5_H, S5_G, S5_P), F32)
    call, tail = _call_after(after, body, 5, name="s5_params_fwd", out_shape=[gp, gp, hgp, hgp],
                             in_specs=[pl.BlockSpec(memory_space=pltpu.VMEM)] * 5)
    return call(a_re, a_im, log_dt, bt_re, bt_im, *tail)


def _s5_params_bwd(a_re, a_im, log_dt, bt_re, bt_im, d_ar, d_ai, d_br, d_bi):
    def body(ar, ai, ld, br, bi, gar, gai, gbr, gbi, o_ar, o_ai, o_ld, o_br, o_bi):
        _, vjp = jax.vjp(_s5_discretize, ar[...], ai[...], ld[...], br[...], bi[...])
        r = vjp((gar[...], gai[...], gbr[...], gbi[...]))
        o_ar[...], o_ai[...], o_ld[...], o_br[...], o_bi[...] = r

    gp = jax.ShapeDtypeStruct((S5_G, S5_P), F32)
    hgp = jax.ShapeDtypeStruct((S5_H, S5_G, S5_P), F32)
    return pl.pallas_call(body, name="s5_params_bwd", out_shape=[gp, gp, jax.ShapeDtypeStruct((S5_G, 1), F32), hgp, hgp])(
        a_re, a_im, log_dt, bt_re, bt_im, d_ar, d_ai, d_br, d_bi)


NSEG = 8
SCAN_LANES = 1024


def _to_segments(x):
    n, c = x.shape
    return x.reshape(NSEG, n // NSEG, c).transpose(1, 0, 2).reshape(n, c)


def _from_segments(x):
    n, c = x.shape
    return x.reshape(n // NSEG, NSEG, c).transpose(1, 0, 2).reshape(n, c)


def _segment_scan(re_ref, im_ref, st_re, st_im, a_re, a_im, n_slabs, adjoint, write):
    for q in range(NSTATE // SCAN_LANES):
        ls = slice(q * SCAN_LANES, (q + 1) * SCAN_LANES)
        ar = jnp.broadcast_to(a_re[:, ls], (8, SCAN_LANES))
        ai = jnp.broadcast_to(a_im[:, ls], (8, SCAN_LANES))

        def step(k, carry, ls=ls, ar=ar, ai=ai):
            s_r, s_i = carry
            slab = (n_slabs - 1 - k) if adjoint else k
            rows = pl.ds(pl.multiple_of(slab * 8, 8), 8)
            b_r, b_i = re_ref[rows, ls], im_ref[rows, ls]
            if adjoint:
                n_r = b_r + ar * s_r + ai * s_i
                n_i = b_i - ai * s_r + ar * s_i
            else:
                n_r = ar * s_r - ai * s_i + b_r
                n_i = ar * s_i + ai * s_r + b_i
            if write:
                re_ref[rows, ls] = n_r
                im_ref[rows, ls] = n_i
            return n_r, n_i

        s_r, s_i = lax.fori_loop(0, n_slabs, step, (st_re[:, ls], st_im[:, ls]), unroll=4)
        st_re[:, ls] = s_r
        st_im[:, ls] = s_i


def _s5_segment_states(e_re, e_im, ar, ai, seg_len, adjoint):
    def body(ere_ref, eim_ref, ar_ref, ai_ref, ore_ref, oim_ref):
        p_r, p_i = ar_ref[...], ai_ref[...]
        if adjoint:
            p_i = -p_i
        acc_r, acc_i = jnp.ones_like(p_r), jnp.zeros_like(p_r)
        n = seg_len
        while n:
            if n & 1:
                acc_r, acc_i = acc_r * p_r - acc_i * p_i, acc_r * p_i + acc_i * p_r
            n >>= 1
            if n:
                p_r, p_i = p_r * p_r - p_i * p_i, 2.0 * p_r * p_i
        e_r, e_i = ere_ref[...], eim_ref[...]
        s_r, s_i = jnp.zeros_like(acc_r), jnp.zeros_like(acc_r)
        rows_r, rows_i = [None] * NSEG, [None] * NSEG
        order = range(NSEG - 1, -1, -1) if adjoint else range(NSEG)
        for j in order:
            rows_r[j], rows_i[j] = s_r, s_i
            s_r, s_i = (acc_r * s_r - acc_i * s_i + e_r[j:j + 1], acc_r * s_i + acc_i * s_r + e_i[j:j + 1])
        ore_ref[...] = jnp.concatenate(rows_r, axis=0)
        oim_ref[...] = jnp.concatenate(rows_i, axis=0)

    st = jax.ShapeDtypeStruct((NSEG, NSTATE), F32)
    return pl.pallas_call(body, name="s5_segment_states_bwd" if adjoint else "s5_segment_states_fwd", out_shape=[st, st])(
        e_re, e_im, ar, ai)


def _s5_fwd_ends(x, pv, w_in, b_re, b_im, ar, ai, layer, tm, after=None):
    L = x.shape[0]

    def body(x_ref, pv_ref, win_ref, bre_ref, bim_ref, ar_ref, ai_ref, h_ref, u_ref, ere_ref, eim_ref, bu_re, bu_im):
        @pl.when(pl.program_id(0) == 0)
        def _():
            ere_ref[...] = jnp.zeros_like(ere_ref)
            eim_ref[...] = jnp.zeros_like(eim_ref)

        p = pv_ref[...]
        h, _, _ = _norm_mod(x_ref[...], p[R_N1:R_N1 + 1], p[R_SC1:R_SC1 + 1], p[R_SH1:R_SH1 + 1])
        hb = _bf(h)
        h_ref[...] = hb
        u = _dot(hb, win_ref[...])
        u_ref[...] = u
        ub = _bf(u)
        for k in range(S5_NB):
            uk = ub[:, k * S5_BH:(k + 1) * S5_BH]
            bu_re[:, k * S5_BP:(k + 1) * S5_BP] = _dot(uk, bre_ref[k])
            bu_im[:, k * S5_BP:(k + 1) * S5_BP] = _dot(uk, bim_ref[k])
        _segment_scan(bu_re, bu_im, ere_ref, eim_ref, ar_ref[...], ai_ref[...], tm // 8, adjoint=False, write=False)

    call, tail = _call_after(
        after, body, 7, name="s5_fwd_ends", grid=(L // tm,),
        in_specs=[_rows(tm, D), pl.BlockSpec((None, 8, D), lambda i: (layer, 0, 0)), _layer_w(D, D, 0),
                  _const_w((S5_NB, S5_BH, S5_BP)), _const_w((S5_NB, S5_BH, S5_BP)), _whole((1, NSTATE)), _whole((1, NSTATE))],
        out_specs=[_rows(tm, D), _rows(tm, D), _whole((NSEG, NSTATE)), _whole((NSEG, NSTATE))],
        out_shape=[jax.ShapeDtypeStruct((L, D), BF16), jax.ShapeDtypeStruct((L, D), F32),
                   jax.ShapeDtypeStruct((NSEG, NSTATE), F32), jax.ShapeDtypeStruct((NSEG, NSTATE), F32)],
        scratch_shapes=[pltpu.VMEM((tm, NSTATE), F32), pltpu.VMEM((tm, NSTATE), F32)],
        compiler_params=_params(1, 56),
    )
    return call(x, pv, w_in, b_re, b_im, ar, ai, *tail)


def _s5_fwd_out(x, u, pv, b_re, b_im, s0_re, s0_im, ar, ai, c_re, c_im, dvec, glu_w, glu_b, w_out, layer, tm):
    L = x.shape[0]

    def body(x_ref, u_ref, pv_ref, bre_ref, bim_ref, s0re_ref, s0im_ref, ar_ref, ai_ref, cre_ref, cim_ref, d_ref, gw_ref,
             gb_ref, wout_ref, x1_ref, sre_ref, sim_ref, y1_ref, zg_ref, y3_ref, y_ref, st_re, st_im):
        @pl.when(pl.program_id(0) == 0)
        def _():
            st_re[...] = s0re_ref[...]
            st_im[...] = s0im_ref[...]

        p = pv_ref[...]
        uv = u_ref[...]
        ub = _bf(uv)
        for k in range(S5_NB):
            uk = ub[:, k * S5_BH:(k + 1) * S5_BH]
            sre_ref[:, k * S5_BP:(k + 1) * S5_BP] = _dot(uk, bre_ref[k])
            sim_ref[:, k * S5_BP:(k + 1) * S5_BP] = _dot(uk, bim_ref[k])
        _segment_scan(sre_ref, sim_ref, st_re, st_im, ar_ref[...], ai_ref[...], tm // 8, adjoint=False, write=True)
        parts = []
        for k in range(S5_NB):
            sl = slice(k * S5_BP, (k + 1) * S5_BP)
            parts.append(_dot(_bf(sre_ref[:, sl]), cre_ref[k]) - _dot(_bf(sim_ref[:, sl]), cim_ref[k]))
        y1 = jnp.concatenate(parts, axis=1) + d_ref[...] * uv
        y1_ref[...] = y1
        y2 = jax.nn.gelu(y1)
        zg = _dot(_bf(y2), gw_ref[...]) + gb_ref[...]
        zg_ref[...] = zg
        y3b = _bf(y2 * jax.nn.sigmoid(zg))
        y3_ref[...] = y3b
        y = _dot(y3b, wout_ref[...])
        y_ref[...] = y
        x1_ref[...] = x_ref[...] + p[R_G1:R_G1 + 1] * y

    return pl.pallas_call(
        body, name="s5_fwd_out", grid=(L // tm,),
        in_specs=[_rows(tm, D), _rows(tm, D), pl.BlockSpec((None, 8, D), lambda i: (layer, 0, 0)),
                  _const_w((S5_NB, S5_BH, S5_BP)), _const_w((S5_NB, S5_BH, S5_BP)),
                  _whole((NSEG, NSTATE)), _whole((NSEG, NSTATE)), _whole((1, NSTATE)), _whole((1, NSTATE)),
                  _const_w((S5_NB, S5_BP, S5_BH)), _const_w((S5_NB, S5_BP, S5_BH)), _whole((1, D)),
                  _layer_w(D, D, 0), _whole((1, D)), _layer_w(D, D, 0)],
        out_specs=[_rows(tm, D), _rows(tm, NSTATE), _rows(tm, NSTATE), _rows(tm, D), _rows(tm, D), _rows(tm, D), _rows(tm, D)],
        out_shape=[jax.ShapeDtypeStruct((L, D), F32), jax.ShapeDtypeStruct((L, NSTATE), F32), jax.ShapeDtypeStruct((L, NSTATE), F32),
                   jax.ShapeDtypeStruct((L, D), F32), jax.ShapeDtypeStruct((L, D), F32),
                   jax.ShapeDtypeStruct((L, D), BF16), jax.ShapeDtypeStruct((L, D), F32)],
        scratch_shapes=[pltpu.VMEM((NSEG, NSTATE), F32), pltpu.VMEM((NSEG, NSTATE), F32)],
        compiler_params=_params(1, 56),
    )(x, u, pv, b_re, b_im, s0_re, s0_im, ar, ai, c_re, c_im, dvec, glu_w, glu_b, w_out)


def _s5_bwd_ends(dx1, y, y1, zg, u, pv, c_re, c_im, ar, ai, dvec, glu_w, w_out, layer, tm, after=None):
    L = dx1.shape[0]
    nt = L // tm

    def body(dx1_ref, y_ref, y1_ref, zg_ref, u_ref, pv_ref, cre_ref, cim_ref, ar_ref, ai_ref, d_ref, gw_ref, wout_ref,
             dy_ref, y2_ref, dzg_ref, dy1_ref, dus_ref, ere_ref, eim_ref, vs_ref, g_re, g_im):
        @pl.when(pl.program_id(0) == 0)
        def _():
            vs_ref[...] = jnp.zeros_like(vs_ref)
            ere_ref[...] = jnp.zeros_like(ere_ref)
            eim_ref[...] = jnp.zeros_like(eim_ref)

        dx1v, p = dx1_ref[...], pv_ref[...]
        dyb = _bf(dx1v * p[R_G1:R_G1 + 1])
        dy_ref[...] = dyb
        vs_ref[0:1, :] += _sum0(dx1v * y_ref[...])
        dy3 = _dot_nt(dyb, wout_ref[...])
        y2, gelu_vjp = jax.vjp(jax.nn.gelu, y1_ref[...])
        y2_ref[...] = _bf(y2)
        gate = jax.nn.sigmoid(zg_ref[...])
        dzg = dy3 * y2 * gate * (1.0 - gate)
        dzgb = _bf(dzg)
        dzg_ref[...] = dzgb
        vs_ref[1:2, :] += _sum0(dzg)
        dy2 = dy3 * gate + _dot_nt(dzgb, gw_ref[...])
        dy1 = gelu_vjp(dy2)[0]
        vs_ref[2:3, :] += _sum0(dy1 * u_ref[...])
        dus_ref[...] = dy1 * d_ref[...]
        dy1b = _bf(dy1)
        dy1_ref[...] = dy1b
        for k in range(S5_NB):
            dk = dy1b[:, k * S5_BH:(k + 1) * S5_BH]
            g_re[:, k * S5_BP:(k + 1) * S5_BP] = _dot_nt(dk, cre_ref[k])
            g_im[:, k * S5_BP:(k + 1) * S5_BP] = -_dot_nt(dk, cim_ref[k])
        _segment_scan(g_re, g_im, ere_ref, eim_ref, ar_ref[...], ai_ref[...], tm // 8, adjoint=True, write=False)

    call, tail = _call_after(
        after, body, 13, name="s5_bwd_ends", grid=(nt,),
        in_specs=[_rows(tm, D, nt)] * 5 + [pl.BlockSpec((None, 8, D), lambda i: (layer, 0, 0)),
                  _const_w((S5_NB, S5_BP, S5_BH)), _const_w((S5_NB, S5_BP, S5_BH)), _whole((1, NSTATE)), _whole((1, NSTATE)),
                  _whole((1, D)), _layer_w(D, D, 0), _layer_w(D, D, 0)],
        out_specs=[_rows(tm, D, nt)] * 5 + [_whole((NSEG, NSTATE)), _whole((NSEG, NSTATE)), _whole((8, D))],
        out_shape=[jax.ShapeDtypeStruct((L, D), BF16)] * 4 + [jax.ShapeDtypeStruct((L, D), F32),
                   jax.ShapeDtypeStruct((NSEG, NSTATE), F32), jax.ShapeDtypeStruct((NSEG, NSTATE), F32),
                   jax.ShapeDtypeStruct((8, D), F32)],
        scratch_shapes=[pltpu.VMEM((tm, NSTATE), F32), pltpu.VMEM((tm, NSTATE), F32)],
        compiler_params=_params(1, 56),
    )
    return call(dx1, y, y1, zg, u, pv, c_re, c_im, ar, ai, dvec, glu_w, w_out, *tail)


def _s5_bwd_in(dx1, dy1_b, du_skip, x, s_re, s_im, pv, b_re, b_im, c_re, c_im, l0_re, l0_im, ar, ai, w_in, layer, tm):
    L = x.shape[0]
    nt = L // tm

    def body(dx1_ref, dy1_ref, dus_ref, x_ref, sre_ref, sim_ref, hre_ref, him_ref, lre_ref, lim_ref, pv_ref, bre_ref, bim_ref,
             cre_ref, cim_ref, l0re_ref, l0im_ref, ar_ref, ai_ref, win_ref,
             dx_ref, du_ref, lamre_ref, lamim_ref, da_ref, vs_ref, g_re, g_im, st_re, st_im):
        gi = pl.program_id(0)
        tile = nt - 1 - gi

        @pl.when(gi == 0)
        def _():
            vs_ref[...] = jnp.zeros_like(vs_ref)
            da_ref[...] = jnp.zeros_like(da_ref)
            st_re[...] = l0re_ref[...]
            st_im[...] = l0im_ref[...]

        p = pv_ref[...]
        dy1b = dy1_ref[...]
        for k in range(S5_NB):
            dk = dy1b[:, k * S5_BH:(k + 1) * S5_BH]
            g_re[:, k * S5_BP:(k + 1) * S5_BP] = _dot_nt(dk, cre_ref[k])
            g_im[:, k * S5_BP:(k + 1) * S5_BP] = -_dot_nt(dk, cim_ref[k])
        _segment_scan(g_re, g_im, st_re, st_im, ar_ref[...], ai_ref[...], tm // 8, adjoint=True, write=True)
        lam_r, lam_i = g_re[...], g_im[...]
        lrb, lib = _bf(lam_r), _bf(lam_i)
        lamre_ref[...] = lrb
        lamim_ref[...] = lib

        def wrapped(last_ref):
            z = last_ref[...]
            row = lax.broadcasted_iota(jnp.int32, z.shape, 0)
            return jnp.where(row >= 1, pltpu.roll(z, 1, 0), 0.0)

        first_r = jnp.where(tile > 0, hre_ref[...], wrapped(lre_ref))
        first_i = jnp.where(tile > 0, him_ref[...], wrapped(lim_ref))
        sp_r = jnp.concatenate([first_r, sre_ref[0:tm - 8, :]], axis=0)
        sp_i = jnp.concatenate([first_i, sim_ref[0:tm - 8, :]], axis=0)
        da_ref[0:1, :] += _sum0(lam_r * sp_r + lam_i * sp_i)
        da_ref[1:2, :] += _sum0(lam_i * sp_r - lam_r * sp_i)

        parts = []
        for k in range(S5_NB):
            sl = slice(k * S5_BP, (k + 1) * S5_BP)
            parts.append(_dot_nt(lrb[:, sl], bre_ref[k]) + _dot_nt(lib[:, sl], bim_ref[k]))
        dub = _bf(jnp.concatenate(parts, axis=1) + dus_ref[...])
        du_ref[...] = dub
        dh = _dot_nt(dub, win_ref[...])
        _, xn, r = _norm_mod(x_ref[...], p[R_N1:R_N1 + 1], p[R_SC1:R_SC1 + 1], p[R_SH1:R_SH1 + 1])
        dx_ref[...] = dx1_ref[...] + _norm_mod_bwd(dh, xn, r, p[R_N1:R_N1 + 1], p[R_SC1:R_SC1 + 1])
        vs_ref[1:2, :] += _sum0(dh * xn)
        vs_ref[2:3, :] += _sum0(dh)

    halo = pl.BlockSpec((8, NSTATE), lambda i: (jnp.maximum((nt - 1 - i) * (tm // 8) - 1, 0), 0))
    last = pl.BlockSpec((8, NSTATE), lambda i: (L // 8 - 1, 0))
    return pl.pallas_call(
        body, name="s5_bwd_in", grid=(nt,),
        in_specs=[_rows(tm, D, nt), _rows(tm, D, nt), _rows(tm, D, nt), _rows(tm, D, nt), _rows(tm, NSTATE, nt), _rows(tm, NSTATE, nt),
                  halo, halo, last, last, pl.BlockSpec((None, 8, D), lambda i: (layer, 0, 0)),
                  _const_w((S5_NB, S5_BH, S5_BP)), _const_w((S5_NB, S5_BH, S5_BP)),
                  _const_w((S5_NB, S5_BP, S5_BH)), _const_w((S5_NB, S5_BP, S5_BH)),
                  _whole((NSEG, NSTATE)), _whole((NSEG, NSTATE)), _whole((1, NSTATE)), _whole((1, NSTATE)), _layer_w(D, D, 0)],
        out_specs=[_rows(tm, D, nt), _rows(tm, D, nt), _rows(tm, NSTATE, nt), _rows(tm, NSTATE, nt), _whole((8, NSTATE)), _whole((8, D))],
        out_shape=[jax.ShapeDtypeStruct((L, D), F32), jax.ShapeDtypeStruct((L, D), BF16),
                   jax.ShapeDtypeStruct((L, NSTATE), BF16), jax.ShapeDtypeStruct((L, NSTATE), BF16),
                   jax.ShapeDtypeStruct((8, NSTATE), F32), jax.ShapeDtypeStruct((8, D), F32)],
        scratch_shapes=[pltpu.VMEM((tm, NSTATE), F32), pltpu.VMEM((tm, NSTATE), F32),
                        pltpu.VMEM((NSEG, NSTATE), F32), pltpu.VMEM((NSEG, NSTATE), F32)],
        compiler_params=_params(1, 60),
    )(dx1, dy1_b, du_skip, x, s_re, s_im, s_re, s_im, s_re, s_im, pv, b_re, b_im, c_re, c_im, l0_re, l0_im, ar, ai, w_in)


def _blockdiag_b(bt):
    b = bt.reshape(S5_H, S5_NB, 16, S5_P).transpose(1, 2, 0, 3)
    eye = jnp.eye(16, dtype=bt.dtype)
    return (b[:, :, :, None, :] * eye[None, :, None, :, None]).reshape(S5_NB, S5_BH, S5_BP)


def _unblock_b(d):
    d = jnp.einsum("bghgp->bghp", d.reshape(S5_NB, 16, S5_H, 16, S5_P))
    return d.transpose(2, 0, 1, 3).reshape(S5_H, S5_G, S5_P)


def _blockdiag_c(cm):
    c4 = cm.reshape(S5_NB, 16, S5_H, S5_P)
    eye = jnp.eye(16, dtype=cm.dtype)
    out = c4.transpose(0, 1, 3, 2)[:, :, :, None, :] * eye[None, :, None, :, None]
    return out.reshape(S5_NB, S5_BP, S5_BH)


def _unblock_c(d):
    d = jnp.einsum("bgpgh->bghp", d.reshape(S5_NB, 16, S5_P, 16, S5_H))
    return d.reshape(S5_G, S5_H, S5_P)


def _tril_mask():
    return lax.broadcasted_iota(jnp.int32, (SG_CHUNK, SG_CHUNK), 0) >= lax.broadcasted_iota(jnp.int32, (SG_CHUNK, SG_CHUNK), 1)


def _sg_fwd(x, pv, w_in, w_s, b_t, vg, w_out, layer, tm, after=None):
    L = x.shape[0]
    nc = tm // SG_CHUNK

    def body(x_ref, pv_ref, win_ref, ws_ref, bt_ref, vg_ref, wout_ref, x1_ref, h_ref, uv_ref, vm_ref, q_ref, y_ref):
        xv, p = x_ref[...], pv_ref[...]
        h, _, _ = _norm_mod(xv, p[R_N1:R_N1 + 1], p[R_SC1:R_SC1 + 1], p[R_SH1:R_SH1 + 1])
        hb = _bf(h)
        h_ref[...] = hb
        uv = _dot(hb, win_ref[...])
        uv_ref[...] = uv
        v = uv[:, D:]
        rv = lax.rsqrt(jnp.mean(v * v, axis=-1, keepdims=True) + EPS)
        vnb = _bf((v * rv) * vg_ref[...])
        mask = _tril_mask()
        bt = bt_ref[...]
        for hd in range(SG_HEADS):
            wm = _bf(jnp.where(mask, ws_ref[hd], 0.0))
            cs = slice(hd * SG_CHUNK, (hd + 1) * SG_CHUNK)
            for ck in range(nc):
                rs = slice(ck * SG_CHUNK, (ck + 1) * SG_CHUNK)
                vm_ref[rs, cs] = _dot(wm, vnb[rs, cs]) + bt[:, hd:hd + 1]
        qb = _bf(uv[:, :D] * vm_ref[...])
        q_ref[...] = qb
        y = _dot(qb, wout_ref[...])
        y_ref[...] = y
        x1_ref[...] = xv + p[R_G1:R_G1 + 1] * y

    call, tail = _call_after(
        after, body, 7, name="sg_fwd", grid=(L // tm,),
        in_specs=[_rows(tm, D), pl.BlockSpec((None, 8, D), lambda i: (layer, 0, 0)), _layer_w(D, 2 * D, 0),
                  _whole((SG_HEADS, SG_CHUNK, SG_CHUNK)), _whole((SG_CHUNK, SG_HEADS)), _whole((1, D)), _layer_w(D, D, 0)],
        out_specs=[_rows(tm, D), _rows(tm, D), _rows(tm, 2 * D), _rows(tm, D), _rows(tm, D), _rows(tm, D)],
        out_shape=[jax.ShapeDtypeStruct((L, D), F32), jax.ShapeDtypeStruct((L, D), BF16), jax.ShapeDtypeStruct((L, 2 * D), F32),
                   jax.ShapeDtypeStruct((L, D), F32), jax.ShapeDtypeStruct((L, D), BF16), jax.ShapeDtypeStruct((L, D), F32)],
        compiler_params=_params(1, 56),
    )
    return call(x, pv, w_in, w_s, b_t, vg, w_out, *tail)


def _sg_bwd(dx1, x, y, uv, vm, pv, w_in, w_s, vg, w_out, layer, tm, after=None):
    L = x.shape[0]
    nc = tm // SG_CHUNK

    def body(dx1_ref, x_ref, y_ref, uv_ref, vm_ref, pv_ref, win_ref, ws_ref, vg_ref, wout_ref,
             dx_ref, duv_ref, dy_ref, vs_ref, dws_ref, dbt_ref, dvn_scr):
        @pl.when(pl.program_id(0) == 0)
        def _():
            vs_ref[...] = jnp.zeros_like(vs_ref)
            dws_ref[...] = jnp.zeros_like(dws_ref)
            dbt_ref[...] = jnp.zeros_like(dbt_ref)

        dx1v, p = dx1_ref[...], pv_ref[...]
        dyb = _bf(dx1v * p[R_G1:R_G1 + 1])
        dy_ref[...] = dyb
        vs_ref[0:1, :] += _sum0(dx1v * y_ref[...])
        dq = _dot_nt(dyb, wout_ref[...])
        uv = uv_ref[...]
        u, v = uv[:, :D], uv[:, D:]
        dub = _bf(dq * vm_ref[...])
        dvm = dq * u
        dvmb = _bf(dvm)
        rv = lax.rsqrt(jnp.mean(v * v, axis=-1, keepdims=True) + EPS)
        vh = v * rv
        vgv = vg_ref[...]
        vnb = _bf(vh * vgv)
        mask = _tril_mask()
        for hd in range(SG_HEADS):
            wm = _bf(jnp.where(mask, ws_ref[hd], 0.0))
            cs = slice(hd * SG_CHUNK, (hd + 1) * SG_CHUNK)
            dws = jnp.zeros((SG_CHUNK, SG_CHUNK), F32)
            dbs = jnp.zeros((SG_CHUNK, 1), F32)
            for ck in range(nc):
                rs = slice(ck * SG_CHUNK, (ck + 1) * SG_CHUNK)
                dvn_scr[rs, cs] = _dot_tn(wm, dvmb[rs, cs])
                dws = dws + _dot_nt(dvmb[rs, cs], vnb[rs, cs])
                dbs = dbs + jnp.sum(dvm[rs, cs], axis=1, keepdims=True)
            dws_ref[hd] += jnp.where(mask, dws, 0.0)
            dbt_ref[:, hd:hd + 1] += dbs
        dvn = dvn_scr[...]
        vs_ref[3:4, :] += _sum0(dvn * vh)
        dvnn = dvn * vgv
        dvb = _bf(rv * (dvnn - vh * jnp.mean(dvnn * vh, axis=-1, keepdims=True)))
        duv_ref[:, 0:D] = dub
        duv_ref[:, D:2 * D] = dvb
        dh = _dot_nt(dub, win_ref[:, 0:D]) + _dot_nt(dvb, win_ref[:, D:2 * D])
        _, xn, r = _norm_mod(x_ref[...], p[R_N1:R_N1 + 1], p[R_SC1:R_SC1 + 1], p[R_SH1:R_SH1 + 1])
        dx_ref[...] = dx1v + _norm_mod_bwd(dh, xn, r, p[R_N1:R_N1 + 1], p[R_SC1:R_SC1 + 1])
        vs_ref[1:2, :] += _sum0(dh * xn)
        vs_ref[2:3, :] += _sum0(dh)

    call, tail = _call_after(
        after, body, 10, name="sg_bwd", grid=(L // tm,),
        in_specs=[_rows(tm, D), _rows(tm, D), _rows(tm, D), _rows(tm, 2 * D), _rows(tm, D),
                  pl.BlockSpec((None, 8, D), lambda i: (layer, 0, 0)), _layer_w(D, 2 * D, 0),
                  _whole((SG_HEADS, SG_CHUNK, SG_CHUNK)), _whole((1, D)), _layer_w(D, D, 0)],
        out_specs=[_rows(tm, D), _rows(tm, 2 * D), _rows(tm, D), _whole((8, D)),
                   _whole((SG_HEADS, SG_CHUNK, SG_CHUNK)), _whole((SG_CHUNK, SG_HEADS))],
        out_shape=[jax.ShapeDtypeStruct((L, D), F32), jax.ShapeDtypeStruct((L, 2 * D), BF16), jax.ShapeDtypeStruct((L, D), BF16),
                   jax.ShapeDtypeStruct((8, D), F32), jax.ShapeDtypeStruct((SG_HEADS, SG_CHUNK, SG_CHUNK), F32),
                   jax.ShapeDtypeStruct((SG_CHUNK, SG_HEADS), F32)],
        scratch_shapes=[pltpu.VMEM((tm, D), F32)],
        compiler_params=_params(1, 56),
    )
    return call(dx1, x, y, uv, vm, pv, w_in, w_s, vg, w_out, *tail)


def _final(x, target, fg, tm):
    L = x.shape[0]

    def body(x_ref, t_ref, g_ref, dx_ref, vs_ref):
        @pl.when(pl.program_id(0) == 0)
        def _():
            vs_ref[...] = jnp.zeros_like(vs_ref)

        xv, g = x_ref[...], g_ref[...]
        r = lax.rsqrt(jnp.mean(xv * xv, axis=-1, keepdims=True) + EPS)
        xn = xv * r
        e = xn * g - t_ref[...]
        vs_ref[0:1, :] += jnp.sum(e * e)
        dout = e * (1.0 / D)
        vs_ref[1:2, :] += _sum0(dout * xn)
        dxn = dout * g
        dx_ref[...] = r * (dxn - xn * jnp.mean(dxn * xn, axis=-1, keepdims=True))

    return pl.pallas_call(
        body, name="final_loss", grid=(L // tm,),
        in_specs=[_rows(tm, D), _rows(tm, D), _whole((1, D))],
        out_specs=[_rows(tm, D), _whole((8, D))],
        out_shape=[jax.ShapeDtypeStruct((L, D), F32), jax.ShapeDtypeStruct((8, D), F32)],
        compiler_params=_params(1),
    )(x, target, fg)


def _pack_flat(arrs, multiple=LANES):
    flat = jnp.concatenate([a.reshape(-1).astype(F32) for a in arrs])
    return jnp.pad(flat, (0, -flat.shape[0] % multiple))


def _pack(arrs, row_multiple=8):
    return _pack_flat(arrs, row_multiple * LANES).reshape(-1, LANES)


def _unpack(buf, shapes, lead=()):
    flat = buf.reshape(lead + (-1,))
    out, off = [], 0
    for s in shapes:
        n = 1
        for d in s:
            n *= d
        out.append(flat[..., off:off + n].reshape(lead + tuple(s)))
        off += n
    return out


BIG = ("ff_w1", "ff_w2", "conv_w_in", "conv_w_out", "ssm_w_in", "ssm_glu_w", "ssm_w_out", "sg_w_in", "sg_w_out")
BIG_AXIS = {"ff_w1": 2, "ff_w2": 1, "conv_w_in": 2, "conv_w_out": 1, "ssm_w_in": 1, "ssm_glu_w": 1, "ssm_w_out": 1,
            "sg_w_in": 2, "sg_w_out": 1}
LAYER_WEIGHTS = (
    (("conv_w_in", 0), ("conv_w_out", 0), ("ff_w1", 0), ("ff_w2", 0)),
    (("ssm_w_in", 0), ("ssm_glu_w", 0), ("ssm_w_out", 0), ("ff_w1", 1), ("ff_w2", 1)),
    (("sg_w_in", 0), ("sg_w_out", 0), ("ff_w1", 2), ("ff_w2", 2)),
    (("conv_w_in", 1), ("conv_w_out", 1), ("ff_w1", 3), ("ff_w2", 3)),
)
GATHER_GROUPS = tuple(grp for lw in LAYER_WEIGHTS for grp in (lw[:-2], lw[-2:]))
SMALL_SHARDED = ("conv_w", "conv_b", "sg_v_g")
SMALL_WIDE_PADDED = ("ssm_b_re", "ssm_b_im")
SMALL = ("ada_b", "norm1_g", "norm2_g", "final_g", "ssm_a_re", "ssm_a_im", "ssm_log_dt", "ssm_b_re", "ssm_b_im", "ssm_c_re",
         "ssm_c_im", "ssm_d", "ssm_glu_b", "sg_w_s", "sg_b_s") + SMALL_SHARDED
WEIGHTS = ("ada_w", "ada_b", "norm1_g", "norm2_g", "ff_w1", "ff_w2", "final_g", "conv_w_in", "conv_w", "conv_b", "conv_w_out",
           "ssm_w_in", "ssm_a_re", "ssm_a_im", "ssm_log_dt", "ssm_b_re", "ssm_b_im", "ssm_c_re", "ssm_c_im", "ssm_d",
           "ssm_glu_w", "ssm_glu_b", "ssm_w_out", "sg_w_in", "sg_v_g", "sg_w_s", "sg_b_s", "sg_w_out")


def kernel(x, c, ada_w, ada_b, norm1_g, norm2_g, ff_w1, ff_w2, final_g, conv_w_in, conv_w, conv_b, conv_w_out, ssm_w_in, ssm_a_re, ssm_a_im, ssm_log_dt, ssm_b_re, ssm_b_im, ssm_c_re, ssm_c_im, ssm_d, ssm_glu_w, ssm_glu_b, ssm_w_out, sg_w_in, sg_v_g, sg_w_s, sg_b_s, sg_w_out, loss_target, m_ada_w, m_ada_b, m_norm1_g, m_norm2_g, m_ff_w1, m_ff_w2, m_final_g, m_conv_w_in, m_conv_w, m_conv_b, m_conv_w_out, m_ssm_w_in, m_ssm_a_re, m_ssm_a_im, m_ssm_log_dt, m_ssm_b_re, m_ssm_b_im, m_ssm_c_re, m_ssm_c_im, m_ssm_d, m_ssm_glu_w, m_ssm_glu_b, m_ssm_w_out, m_sg_w_in, m_sg_v_g, m_sg_w_s, m_sg_b_s, m_sg_w_out, v_ada_w, v_ada_b, v_norm1_g, v_norm2_g, v_ff_w1, v_ff_w2, v_final_g, v_conv_w_in, v_conv_w, v_conv_b, v_conv_w_out, v_ssm_w_in, v_ssm_a_re, v_ssm_a_im, v_ssm_log_dt, v_ssm_b_re, v_ssm_b_im, v_ssm_c_re, v_ssm_c_im, v_ssm_d, v_ssm_glu_w, v_ssm_glu_b, v_ssm_w_out, v_sg_w_in, v_sg_v_g, v_sg_w_s, v_sg_b_s, v_sg_w_out):
    args = dict(locals())
    w = {n: args[n] for n in WEIGHTS}
    m = {n: args["m_" + n] for n in WEIGHTS}
    v = {n: args["v_" + n] for n in WEIGHTS}
    L = x.shape[1]
    tm = min(L, 256)
    tm2 = min(L, 512)
    chip = 2 * lax.axis_index("x") + lax.axis_index("y")
    me = 2 * chip + lax.axis_index("c")
    xin = x[0]
    target = loss_target[0]
    chip1 = chip.reshape(1).astype(jnp.int32)
    place = jnp.stack([chip, lax.axis_index("c")]).astype(jnp.int32)

    gathers = []

    casts = {}

    def cast_group(g, after=None):
        entries = GATHER_GROUPS[g]
        axes = [BIG_AXIS[n] for n, _ in entries]
        casts[g] = _cast_place([w[n] for n, _ in entries], [li for _, li in entries], axes, place, f"cast_group{g}", after)

    def start_gather(g, after):
        if g not in casts:
            cast_group(g)
        axes = [BIG_AXIS[n] for n, _ in GATHER_GROUPS[g]]
        lands = casts[g]
        s_sems, r_sems, lands, token = _gather_start(lands, axes, f"gather_start{g}", after)
        gathers.append((s_sems, r_sems, lands, axes))
        return token

    def weights_of(g, after):
        s_sems, r_sems, lands, axes = gathers[g]
        lands = _gather_wait(s_sems, r_sems, lands, axes, f"gather_wait{g}", after)
        lands = _gather_share(lands, axes, f"gather_share{g}")
        token = start_gather(g + 2, lands[0]) if g + 2 < len(GATHER_GROUPS) else None
        return dict(zip([n for n, _ in GATHER_GROUPS[g]], lands)), token

    small_in = _pack([c, conv_w, conv_b, sg_v_g])
    got = _allgather_small(small_in, "gather_small_inputs").reshape(N_DEV, -1)
    c_all, cw_sh, cb_sh, vg_sh = _unpack(got, [(D,), conv_w.shape, conv_b.shape, sg_v_g.shape], lead=(N_DEV,))
    conv_w_full = jnp.concatenate([cw_sh[2 * k] for k in range(4)], axis=-1)
    conv_b_full = jnp.concatenate([cb_sh[2 * k] for k in range(4)], axis=-1)
    vg_full = jnp.concatenate([vg_sh[2 * k] for k in range(4)], axis=-1)
    c16 = jnp.pad(c_all, ((0, 16 - N_DEV), (0, 0)))

    cols = ada_w.shape[2]
    ada_b_cols = lax.dynamic_slice_in_dim(ada_b, chip * cols, cols, axis=1)[:, None, :]
    mod_sh = _ada_fwd(c16, ada_w, ada_b_cols)[:, :N_DEV, :]
    mod_all = _allgather_small(_pack([mod_sh]), "gather_mod").reshape(N_DEV, -1)
    mod_all = _unpack(mod_all, [mod_sh.shape], lead=(N_DEV,))[0]
    mod_mine = lax.dynamic_index_in_dim(mod_all[0::2], me, axis=2, keepdims=False)
    mod_mine = mod_mine.transpose(1, 0, 2).reshape(DEPTH, 6, D)
    pv = jnp.concatenate([mod_mine, norm1_g[:, None, :], norm2_g[:, None, :]], axis=1)

    second_started = start_gather(1, start_gather(0, pv))
    for g in range(2, len(GATHER_GROUPS)):
        cast_group(g, second_started)

    cw_rows = jnp.concatenate([conv_w_full, conv_b_full[:, None, :], jnp.zeros((conv_w_full.shape[0], 4, D), F32)], axis=1)

    a_re, a_im = ssm_a_re[0], ssm_a_im[0]
    log_dt = ssm_log_dt[0][:, None]
    bt_re, bt_im = ssm_b_re[0].transpose(2, 0, 1), ssm_b_im[0].transpose(2, 0, 1)
    abar_re, abar_im, bbar_re, bbar_im = _s5_params_fwd(a_re, a_im, log_dt, bt_re, bt_im, after=second_started)
    ar_vec, ai_vec = abar_re.reshape(1, NSTATE), abar_im.reshape(1, NSTATE)
    bd_re, bd_im = _bf(_blockdiag_b(bbar_re)), _bf(_blockdiag_b(bbar_im))
    cd_re, cd_im = _bf(_blockdiag_c(ssm_c_re[0])), _bf(_blockdiag_c(ssm_c_im[0]))

    saved = []
    fulls = []
    xl = xin
    for i in range(DEPTH):
        kind = MIXER_OF_LAYER[i]
        j = i // 3
        first_after = [second_started, bd_re, bd_im, cd_re, cd_im, cw_rows] + [casts[g][0] for g in range(2, len(GATHER_GROUPS))]
        full, tok = weights_of(2 * i, first_after if i == 0 else [xl])
        fulls.append(full)
        if kind == 0:
            x1, h, bcx, conv, q, y = _conv_fwd(xl, pv, full["conv_w_in"], full["conv_w_out"], cw_rows, i, j, tm2, after=tok)
            mix = dict(h=h, bcx=bcx, conv=conv, q=q, y=y)
        elif kind == 1:
            xp = _to_segments(xl)
            h, u, e_re, e_im = _s5_fwd_ends(xp, pv, full["ssm_w_in"], bd_re, bd_im, ar_vec, ai_vec, i, tm, after=tok)
            s0_re, s0_im = _s5_segment_states(e_re, e_im, ar_vec, ai_vec, L // NSEG, adjoint=False)
            x1p, s_re, s_im, y1, zg, y3, y = _s5_fwd_out(xp, u, pv, bd_re, bd_im, s0_re, s0_im, ar_vec, ai_vec, cd_re, cd_im,
                                                         ssm_d, full["ssm_glu_w"], ssm_glu_b, full["ssm_w_out"], i, tm)
            x1 = _from_segments(x1p)
            mix = dict(xp=xp, h=h, u=u, s_re=s_re, s_im=s_im, y1=y1, zg=zg, y3=y3, y=y)
        else:
            x1, h, uv, vm, q, y = _sg_fwd(xl, pv, full["sg_w_in"], sg_w_s[0], sg_b_s[0].T, vg_full, full["sg_w_out"], i, tm2,
                                          after=tok)
            mix = dict(h=h, uv=uv, vm=vm, q=q, y=y)
        ffn_weights, tok = weights_of(2 * i + 1, [x1])
        full.update(ffn_weights)
        x2, h2, a, f = _ffn_fwd(x1, pv, full["ff_w1"], full["ff_w2"], i, tm2, after=tok)
        saved.append(dict(x=xl, x1=x1, h2=h2, a=a, f=f, **mix))
        xl = x2

    dxl, vs_fin = _final(xl, target, final_g[None, :], tm2)

    gfull = {n: [None] * w[n].shape[0] for n in BIG}
    vs_mix, vs_ffn = [None] * DEPTH, [None] * DEPTH
    small_g = {}
    scatters = {}
    token = None

    def start_scatter(key, entries, after):
        garrs = [gfull[n][li][None] for n, li in entries]
        gaxes = [BIG_AXIS[n] for n, _ in entries]
        s_sems, r_sems, garrs, lands, tok = _scatter_start(garrs, gaxes, f"scatter_start{key}", after)
        scatters[key] = (s_sems, r_sems, garrs, lands, gaxes, entries)
        return tok

    for i in reversed(range(DEPTH)):
        kind = MIXER_OF_LAYER[i]
        j = i // 3
        sv = saved[i]
        full = fulls[i]
        dx1, p_b, da_b, df_b, vs_ffn[i] = _ffn_bwd(dxl, sv["x1"], sv["a"], sv["f"], pv, full["ff_w1"], full["ff_w2"], i, tm,
                                                   after=token)
        gfull["ff_w1"][i] = _mm_tn(sv["h2"], da_b, f"wgrad_ff_w1_{i}")
        gfull["ff_w2"][i] = _mm_tn(p_b, df_b, f"wgrad_ff_w2_{i}")
        if i == 0:
            token = start_scatter("0f", LAYER_WEIGHTS[0][2:], dx1)
        if kind == 0:
            dxl, dbcx_b, dy_b, vsm = _conv_bwd(dx1, sv["x"], sv["y"], sv["bcx"], sv["conv"], pv, full["conv_w_in"],
                                               full["conv_w_out"], cw_rows, i, j, tm2, after=token if i == 0 else None)
            gfull["conv_w_in"][j] = _mm_tn(sv["h"], dbcx_b, f"wgrad_conv_w_in_{j}")
            gfull["conv_w_out"][j] = _mm_tn(sv["q"], dy_b, f"wgrad_conv_w_out_{j}")
            small_g.setdefault("conv_w", [None, None])[j] = vsm[3:6]
            small_g.setdefault("conv_b", [None, None])[j] = vsm[6]
        elif kind == 1:
            dx1p = _to_segments(dx1)
            dy_b, y2_b, dzg_b, dy1_b, du_skip, eb_re, eb_im, vsm = _s5_bwd_ends(
                dx1p, sv["y"], sv["y1"], sv["zg"], sv["u"], pv, cd_re, cd_im, ar_vec, ai_vec, ssm_d, full["ssm_glu_w"],
                full["ssm_w_out"], i, tm)
            l0_re, l0_im = _s5_segment_states(eb_re, eb_im, ar_vec, ai_vec, L // NSEG, adjoint=True)
            dxp, du_b, lam_re, lam_im, dabar, vs_in = _s5_bwd_in(
                dx1p, dy1_b, du_skip, sv["xp"], sv["s_re"], sv["s_im"], pv, bd_re, bd_im, cd_re, cd_im, l0_re, l0_im,
                ar_vec, ai_vec, full["ssm_w_in"], i, tm)
            dxl = _from_segments(dxp)
            gfull["ssm_w_out"][0] = _mm_tn(sv["y3"], dy_b, "wgrad_ssm_w_out")
            gfull["ssm_glu_w"][0] = _mm_tn(y2_b, dzg_b, "wgrad_ssm_glu_w")
            gfull["ssm_w_in"][0] = _mm_tn(sv["h"], du_b, "wgrad_ssm_w_in")
            s5_late = dict(s_re=sv["s_re"], s_im=sv["s_im"], u=sv["u"], dy1_b=dy1_b, lam_re=lam_re, lam_im=lam_im, dabar=dabar)
            small_g.update(ssm_d=vsm[2], ssm_glu_b=vsm[1])
            vsm = jnp.concatenate([vsm[0:1], vs_in[1:3], jnp.zeros((5, D), F32)], axis=0)
        else:
            dxl, duv_b, dy_b, vsm, d_ws, d_bt = _sg_bwd(dx1, sv["x"], sv["y"], sv["uv"], sv["vm"], pv, full["sg_w_in"],
                                                        sg_w_s[0], vg_full, full["sg_w_out"], i, tm2)
            gfull["sg_w_in"][0] = _mm_tn(sv["h"], duv_b, "wgrad_sg_w_in")
            gfull["sg_w_out"][0] = _mm_tn(sv["q"], dy_b, "wgrad_sg_w_out")
            small_g.update(sg_v_g=vsm[3], sg_w_s=d_ws, sg_b_s=d_bt.T)
        vs_mix[i] = vsm
        token = start_scatter(str(i), LAYER_WEIGHTS[i], dxl) if i > 0 else start_scatter("0c", LAYER_WEIGHTS[0][:2], dxl)
    grad_x = dxl[None]

    sums = {n: [None] * w[n].shape[0] for n in BIG}

    def collect(key, after):
        s_sems, r_sems, garrs, lands, gaxes, entries = scatters[key]
        garrs, recv = _scatter_wait(s_sems, r_sems, garrs, lands, gaxes, f"scatter_wait{key}", after)
        for (n, li), t in zip(entries, _sum_parts(recv, garrs, gaxes, chip1, f"sum_group{key}")):
            sums[n][li] = t
        return sums[entries[-1][0]][entries[-1][1]]

    after = token
    for key in ("3", "2", "1"):
        after = collect(key, after)
    early = [(n, li) for i in (3, 2, 1) for n, li in LAYER_WEIGHTS[i]]
    late = list(LAYER_WEIGHTS[0][2:]) + list(LAYER_WEIGHTS[0][:2])
    s_sems, r_sems, mine_thru, lands, tok = _swap_start([sums[n][li] for n, li in early], "swap_start_early", after)

    blocks = dict(
        c_re=_mm_tn_blocks(s5_late["s_re"], s5_late["dy1_b"], S5_BP, S5_BH, "wgrad_s5_c_re", after=tok),
        c_im=_mm_tn_blocks(s5_late["s_im"], s5_late["dy1_b"], S5_BP, S5_BH, "wgrad_s5_c_im", after=tok),
        b_re=_mm_tn_blocks(s5_late["u"], s5_late["lam_re"], S5_BH, S5_BP, "wgrad_s5_b_re", after=tok),
        b_im=_mm_tn_blocks(s5_late["u"], s5_late["lam_im"], S5_BH, S5_BP, "wgrad_s5_b_im", after=tok))
    d_are, d_aim, d_ldt, d_btre, d_btim = _s5_params_bwd(
        a_re, a_im, log_dt, bt_re, bt_im, s5_late["dabar"][0].reshape(S5_G, S5_P), s5_late["dabar"][1].reshape(S5_G, S5_P),
        _unblock_b(blocks["b_re"]), _unblock_b(blocks["b_im"]))
    small_g.update(ssm_a_re=d_are, ssm_a_im=d_aim, ssm_log_dt=d_ldt, ssm_b_re=d_btre.transpose(1, 2, 0),
                   ssm_b_im=d_btim.transpose(1, 2, 0), ssm_c_re=_unblock_c(blocks["c_re"]), ssm_c_im=-_unblock_c(blocks["c_im"]))

    mine_thru, got = _swap_wait(s_sems, r_sems, mine_thru, lands, "swap_wait_early", blocks["b_im"])
    sib = dict(zip(early, got))
    for (n, li), t in zip(early, mine_thru):
        sums[n][li] = t
    after = got[-1]
    for key in ("0f", "0c"):
        after = collect(key, after)
    sib.update(zip(late, _swap_with_sibling([sums[n][li] for n, li in late], "swap_grad_sums_late")))

    dmod = _mod_bwd(jnp.stack(vs_mix), jnp.stack(vs_ffn), pv)
    small_g.update(ada_b=dmod[:, :6, :], norm1_g=dmod[:, 6, :], norm2_g=dmod[:, 7, :], final_g=vs_fin[1],
                   conv_w=jnp.stack(small_g["conv_w"]), conv_b=jnp.stack(small_g["conv_b"]))

    loss_part = (0.5 / D) * vs_fin[0, 0:1]
    part_shapes = [(1,)] + [tuple(small_g[n].shape) for n in SMALL]
    slots = _reduce_pair(_pack([loss_part] + [small_g[n] for n in SMALL], 16), "reduce_small_pair", sib[late[-1]])
    s_sems, r_sems, slots, tok = _reduce_cross_start(slots, "reduce_small_cross_start")

    res = {}
    for n in BIG:
        res[n] = _adamw_layers(w[n], sums[n], [sib[(n, li)] for li in range(w[n].shape[0])], m[n], v[n], f"adamw_{n}", after=tok)
        tok = res[n][0]

    slots = _reduce_cross_wait(s_sems, r_sems, slots, "reduce_small_cross_wait", tok)
    parts_sum = _reduce_finish(slots, "reduce_small_finish")
    summed = _unpack(parts_sum, part_shapes)
    loss = summed[0][0]
    gsum = dict(zip(SMALL, summed[1:]))
    dmod_all = _allgather_small(_pack([small_g["ada_b"]]), "gather_dmod", parts_sum)
    dmod_all = dmod_all.reshape(N_DEV, DEPTH, 6 * D)
    dmod_cols = lax.dynamic_slice_in_dim(dmod_all, chip * cols, cols, axis=2).transpose(1, 0, 2)
    g_ada_w = _ada_bwd(c16, jnp.pad(dmod_cols, ((0, 0), (0, 16 - N_DEV), (0, 0))))

    shp = ada_w.shape
    two = lambda t: t.reshape(shp[0] * shp[1], shp[2])
    res["ada_w"] = [t.reshape(shp) for t in _adamw(two(ada_w), [two(g_ada_w)], two(m_ada_w), two(v_ada_w), "adamw_ada_w")]

    def mine(n):
        g = gsum[n]
        if n in SMALL_SHARDED:
            g = lax.dynamic_slice_in_dim(g, chip * w[n].shape[-1], w[n].shape[-1], axis=g.ndim - 1)
        return g.reshape(w[n].shape)

    for k, names in enumerate(([n for n in SMALL if n not in SMALL_WIDE_PADDED], list(SMALL_WIDE_PADDED))):
        outs = _adamw_many([w[n] for n in names], [mine(n) for n in names], [m[n] for n in names], [v[n] for n in names],
                           f"adamw_small{k}")
        for idx, n in enumerate(names):
            res[n] = [outs[part][idx] for part in range(4)]

    outs = [loss, grad_x]
    for part in range(4):
        outs += [res[n][part] for n in WEIGHTS]
    return tuple(outs)
```

```python
import functools

import jax
import jax.numpy as jnp
from jax import lax
from jax.experimental import pallas as pl
from jax.experimental.pallas import tpu as pltpu

F32 = jnp.float32
BF16 = jnp.bfloat16
D = 1024
EPS = 1e-6
DEPTH = 4
MIXER_OF_LAYER = (0, 1, 2, 0)
S5_G, S5_H, S5_P = 64, 16, 64
S5_NB = 4
S5_BH = S5_H * 16
S5_BP = S5_P * 16
NSTATE = S5_G * S5_P
SG_HEADS, SG_CHUNK = 8, 128
ADAM_LR, ADAM_B1, ADAM_B2, ADAM_EPS, ADAM_WD, ADAM_STEP = 0.001, 0.9, 0.999, 1e-08, 0.01, 10
N_DEV = 8
MESH = pl.DeviceIdType.MESH
LANES = 1024
R_SH1, R_SC1, R_G1, R_SH2, R_SC2, R_G2, R_N1, R_N2 = range(8)


def _dot(a, b):
    return jnp.dot(a, b, preferred_element_type=F32)


def _dot_nt(a, b):
    return lax.dot_general(a, b, (((1,), (1,)), ((), ())), preferred_element_type=F32)


def _dot_tn(a, b):
    return lax.dot_general(a, b, (((0,), (0,)), ((), ())), preferred_element_type=F32)


def _bf(x):
    return x.astype(BF16)


def _sum0(x):
    return jnp.sum(x, axis=0, keepdims=True)


def _params(n_axes, vmem_mb=48):
    return pltpu.CompilerParams(dimension_semantics=("arbitrary",) * n_axes, vmem_limit_bytes=vmem_mb << 20)


def _rows(tm, cols, nt=None):
    if nt is None:
        return pl.BlockSpec((tm, cols), lambda i: (i, 0))
    return pl.BlockSpec((tm, cols), lambda i: (nt - 1 - i, 0))


def _whole(shape):
    nd = len(shape)
    return pl.BlockSpec(shape, lambda *_: (0,) * nd)


def _layer_w(r, c, layer):
    return pl.BlockSpec((None, r, c), lambda *_: (layer, 0, 0), pipeline_mode=pl.Buffered(1))


def _const_w(shape):
    nd = len(shape)
    return pl.BlockSpec(shape, lambda *_: (0,) * nd, pipeline_mode=pl.Buffered(1))


def _call_after(after, body, n_in, *, in_specs, **kw):
    if after is None:
        return pl.pallas_call(body, in_specs=in_specs, **kw), ()

    def body_after(*refs):
        return body(*refs[:n_in], *refs[n_in + 1:])

    return pl.pallas_call(body_after, in_specs=list(in_specs) + [pl.BlockSpec(memory_space=pl.ANY)], **kw), (after,)


def _norm_mod(x, ng, sc, sh):
    r = lax.rsqrt(jnp.mean(x * x, axis=-1, keepdims=True) + EPS)
    xn = x * r
    return (xn * ng) * (1.0 + sc) + sh, xn, r


def _norm_mod_bwd(dh, xn, r, ng, sc):
    dxn = dh * (ng * (1.0 + sc))
    return r * (dxn - xn * jnp.mean(dxn * xn, axis=-1, keepdims=True))


def _shift_down(z, prev8, k):
    row = lax.broadcasted_iota(jnp.int32, z.shape, 0)
    if k == 1:
        return jnp.where(row >= 1, pltpu.roll(z, 1, 0), prev8[7:8])
    return jnp.where(row >= 2, pltpu.roll(z, 2, 0), jnp.where(row == 0, prev8[6:7], prev8[7:8]))


def _shift_up(z, next8, k):
    n = z.shape[0]
    row = lax.broadcasted_iota(jnp.int32, z.shape, 0)
    if k == 1:
        return jnp.where(row <= n - 2, pltpu.roll(z, n - 1, 0), next8[0:1])
    return jnp.where(row <= n - 3, pltpu.roll(z, n - 2, 0), jnp.where(row == n - 2, next8[0:1], next8[1:2]))


def _place():
    x, y, c = lax.axis_index("x"), lax.axis_index("y"), lax.axis_index("c")
    chips = [(1 - x, y), (x, 1 - y), (1 - x, 1 - y)]
    return x, y, c, chips


def _allgather_small(x_shard, name, after=None):
    m_per, n = x_shard.shape

    def body(x_ref, out_ref, send_sems, recv_sems, local_sem):
        x, y, c, chips = _place()
        me, sibling = (x, y, c), (x, y, 1 - c)

        def rows(px, py, pc):
            return out_ref.at[pl.ds((4 * px + 2 * py + pc) * m_per, m_per), :]

        def copy(k, block, to, src=None):
            return pltpu.make_async_remote_copy(
                src_ref=rows(*block) if src is None else src, dst_ref=rows(*block),
                send_sem=send_sems.at[k], recv_sem=recv_sems.at[k], device_id=to, device_id_type=MESH)

        mine = pltpu.make_async_copy(x_ref, rows(*me), local_sem)
        mine.start()
        first = [copy(0, me, sibling, src=x_ref)]
        first += [copy(1 + j, me, (*chip, c), src=x_ref) for j, chip in enumerate(chips)]
        for cp in first:
            cp.start()
        passed = [copy(4 + j, (*chip, c), sibling) for j, chip in enumerate(chips)]
        for j, chip in enumerate(chips):
            copy(1 + j, (*chip, c), me).wait_recv()
            passed[j].start()
        copy(0, sibling, me).wait_recv()
        for j, chip in enumerate(chips):
            copy(4 + j, (*chip, 1 - c), me).wait_recv()
        for cp in first + passed:
            cp.wait_send()
        mine.wait()

    call, tail = _call_after(
        after, body, 1, name=name, out_shape=jax.ShapeDtypeStruct((N_DEV * m_per, n), F32),
        in_specs=[pl.BlockSpec(memory_space=pltpu.VMEM)], out_specs=pl.BlockSpec(memory_space=pltpu.VMEM),
        scratch_shapes=[pltpu.SemaphoreType.DMA((7,)), pltpu.SemaphoreType.DMA((7,)), pltpu.SemaphoreType.DMA],
        compiler_params=pltpu.CompilerParams(vmem_limit_bytes=48 << 20),
    )
    return call(x_shard, *tail)


def _reduce_pair(x_part, name, after=None):
    m, n = x_part.shape
    h = m // 2

    def body(x_ref, slots_ref, sib_buf, send_sem, recv_sem):
        x, y, c, _ = _place()
        swap = pltpu.make_async_remote_copy(src_ref=x_ref, dst_ref=sib_buf, send_sem=send_sem, recv_sem=recv_sem,
                                            device_id=(x, y, 1 - c), device_id_type=MESH)
        swap.start()
        swap.wait()
        mine = pl.ds(pl.multiple_of(c * h, 8), h)
        slots_ref[pl.ds(2 * x + y, 1)] = (x_ref[mine, :] + sib_buf[mine, :])[None]

    call, tail = _call_after(
        after, body, 1, name=name, out_shape=jax.ShapeDtypeStruct((4, h, n), F32),
        in_specs=[pl.BlockSpec(memory_space=pltpu.VMEM)], out_specs=pl.BlockSpec(memory_space=pltpu.VMEM),
        scratch_shapes=[pltpu.VMEM((m, n), F32), pltpu.SemaphoreType.DMA, pltpu.SemaphoreType.DMA],
        compiler_params=pltpu.CompilerParams(vmem_limit_bytes=48 << 20),
    )
    return call(x_part, *tail)


def _reduce_cross_start(slots, name):
    def body(slots_ref, send_sems, recv_sems, thru, token):
        x, y, c, chips = _place()
        mine = slots_ref.at[pl.ds(2 * x + y, 1)]
        for j, chip in enumerate(chips):
            pltpu.make_async_remote_copy(src_ref=mine, dst_ref=mine, send_sem=send_sems.at[j], recv_sem=recv_sems.at[j],
                                         device_id=(*chip, c), device_id_type=MESH).start()
        token[...] = jnp.zeros_like(token)

    res = pl.pallas_call(
        body, name=name,
        out_shape=(pltpu.SemaphoreType.DMA((3,)), pltpu.SemaphoreType.DMA((3,)), pltpu.HBM(slots.shape, F32),
                   jax.ShapeDtypeStruct((8, 128), F32)),
        in_specs=[HBM_SPEC], out_specs=(SEM_SPEC, SEM_SPEC, HBM_SPEC, pl.BlockSpec(memory_space=pltpu.VMEM)),
        input_output_aliases={0: 2}, compiler_params=SPLIT_COPY_PARAMS,
    )(*_in_hbm([slots]))
    return res


def _reduce_cross_wait(send_sems, recv_sems, slots, name, after):
    def body(slots_ref, s_sems, r_sems, after_ref, thru):
        x, y, c, chips = _place()
        for j, chip in enumerate(chips):
            theirs = slots_ref.at[pl.ds(2 * chip[0] + chip[1], 1)]
            cp = pltpu.make_async_remote_copy(src_ref=theirs, dst_ref=theirs, send_sem=s_sems.at[j], recv_sem=r_sems.at[j],
                                              device_id=(x, y, c), device_id_type=MESH)
            cp.wait_send()
            cp.wait_recv()

    return pl.pallas_call(
        body, name=name, out_shape=pltpu.HBM(slots.shape, F32),
        in_specs=[HBM_SPEC, SEM_SPEC, SEM_SPEC, ANY_SPEC], out_specs=HBM_SPEC,
        input_output_aliases={0: 0}, compiler_params=SPLIT_COPY_PARAMS,
    )(slots, send_sems, recv_sems, after)


def _reduce_finish(slots, name):
    _, h, n = slots.shape

    def body(slots_ref, out_ref, send_sem, recv_sem):
        x, y, c, _ = _place()
        mine = pl.ds(pl.multiple_of(c * h, 8), h)
        theirs = pl.ds(pl.multiple_of((1 - c) * h, 8), h)
        out_ref[mine, :] = ((slots_ref[0] + slots_ref[1]) + slots_ref[2]) + slots_ref[3]
        give = pltpu.make_async_remote_copy(src_ref=out_ref.at[mine, :], dst_ref=out_ref.at[mine, :], send_sem=send_sem,
                                            recv_sem=recv_sem, device_id=(x, y, 1 - c), device_id_type=MESH)
        give.start()
        give.wait_send()
        pltpu.make_async_remote_copy(src_ref=out_ref.at[theirs, :], dst_ref=out_ref.at[theirs, :], send_sem=send_sem,
                                     recv_sem=recv_sem, device_id=(x, y, c), device_id_type=MESH).wait_recv()

    return pl.pallas_call(
        body, name=name, out_shape=jax.ShapeDtypeStruct((2 * h, n), F32),
        in_specs=[pl.BlockSpec(memory_space=pltpu.VMEM)], out_specs=pl.BlockSpec(memory_space=pltpu.VMEM),
        scratch_shapes=[pltpu.SemaphoreType.DMA, pltpu.SemaphoreType.DMA],
        compiler_params=pltpu.CompilerParams(vmem_limit_bytes=48 << 20),
    )(slots)


def _shard_region(ref, full_shape, axis, chip_k, half=None):
    _, r, c = full_shape
    if axis == 1:
        rs = r // 4
        if half is None:
            return ref.at[:, pl.ds(pl.multiple_of(chip_k * rs, 128), rs), :]
        return ref.at[:, pl.ds(pl.multiple_of(chip_k * rs + half * (rs // 2), 128), rs // 2), :]
    cs = c // 4
    if half is None:
        return ref.at[:, :, pl.ds(pl.multiple_of(chip_k * cs, 128), cs)]
    return ref.at[:, pl.ds(pl.multiple_of(half * (r // 2), 128), r // 2), pl.ds(pl.multiple_of(chip_k * cs, 128), cs)]


HBM_SPEC = pl.BlockSpec(memory_space=pltpu.HBM)
SEM_SPEC = pl.BlockSpec(memory_space=pltpu.SEMAPHORE)
ANY_SPEC = pl.BlockSpec(memory_space=pl.ANY)
SPLIT_COPY_PARAMS = pltpu.CompilerParams(has_side_effects=pltpu.SideEffectType.DATAFLOW_SIDE_EFFECTING)


def _in_hbm(arrs):
    return [pltpu.with_memory_space_constraint(a, pltpu.HBM) for a in arrs]


def _cast_place(ws, layers, axes, place, name, after=None):
    n_arr = len(ws)
    in_specs, out_specs, fulls = [], [], []
    for w_stack, li, axis in zip(ws, layers, axes):
        _, r, c = w_stack.shape
        tr = r // 4
        fulls.append((1, 4 * r, c) if axis == 1 else (1, r, 4 * c))
        in_specs.append(pl.BlockSpec((None, tr, c), lambda i, p, li=li: (li, 2 * p[1] + i, 0)))
        if axis == 1:
            out_specs.append(pl.BlockSpec((None, tr, c), lambda i, p: (0, 4 * p[0] + 2 * p[1] + i, 0)))
        else:
            out_specs.append(pl.BlockSpec((None, tr, c), lambda i, p: (0, 2 * p[1] + i, p[0])))

    extra = [] if after is None else [after]

    def body(p_ref, *refs):
        for a in range(n_arr):
            refs[n_arr + len(extra) + a][...] = _bf(refs[a][...])

    return pl.pallas_call(
        body, name=name,
        grid_spec=pltpu.PrefetchScalarGridSpec(num_scalar_prefetch=1, grid=(2,), in_specs=in_specs + [ANY_SPEC] * len(extra),
                                               out_specs=out_specs),
        out_shape=[jax.ShapeDtypeStruct(f, BF16) for f in fulls],
        compiler_params=_params(1),
    )(place, *ws, *extra)


def _gather_start(lands, axes, name, after):
    n_arr = len(lands)
    fulls = [tuple(l.shape) for l in lands]

    def body(*refs):
        land = refs[:n_arr]
        send_sems, recv_sems = refs[n_arr + 1:n_arr + 3]
        token = refs[-1]
        x, y, c, chips = _place()
        k_me = 2 * x + y
        for a in range(n_arr):
            mine = _shard_region(land[a], fulls[a], axes[a], k_me, c)
            for j, chip in enumerate(chips):
                pltpu.make_async_remote_copy(
                    src_ref=mine, dst_ref=mine, send_sem=send_sems.at[a * 3 + j], recv_sem=recv_sems.at[a * 3 + j],
                    device_id=(*chip, c), device_id_type=MESH).start()
        token[...] = jnp.zeros_like(token)

    res = pl.pallas_call(
        body, name=name,
        out_shape=(pltpu.SemaphoreType.DMA((3 * n_arr,)), pltpu.SemaphoreType.DMA((3 * n_arr,)),
                   *[pltpu.HBM(f, BF16) for f in fulls], jax.ShapeDtypeStruct((8, 128), F32)),
        in_specs=[HBM_SPEC] * n_arr + [ANY_SPEC],
        out_specs=(SEM_SPEC, SEM_SPEC, *[HBM_SPEC] * n_arr, pl.BlockSpec(memory_space=pltpu.VMEM)),
        input_output_aliases={a: 2 + a for a in range(n_arr)},
        compiler_params=SPLIT_COPY_PARAMS,
    )(*_in_hbm(lands), after)
    return res[0], res[1], list(res[2:2 + n_arr]), res[-1]


def _gather_wait(send_sems, recv_sems, lands, axes, name, after):
    n_arr = len(lands)
    fulls = [tuple(l.shape) for l in lands]

    def body(*refs):
        land = refs[:n_arr]
        s_sems, r_sems = refs[n_arr:n_arr + 2]
        x, y, c, chips = _place()
        for a in range(n_arr):
            for j, chip in enumerate(chips):
                k_j = 2 * chip[0] + chip[1]
                got = _shard_region(land[a], fulls[a], axes[a], k_j, c)
                cp = pltpu.make_async_remote_copy(
                    src_ref=got, dst_ref=got, send_sem=s_sems.at[a * 3 + j], recv_sem=r_sems.at[a * 3 + j],
                    device_id=(x, y, c), device_id_type=MESH)
                cp.wait_send()
                cp.wait_recv()

    res = pl.pallas_call(
        body, name=name,
        out_shape=tuple(pltpu.HBM(f, BF16) for f in fulls),
        in_specs=[HBM_SPEC] * n_arr + [SEM_SPEC, SEM_SPEC] + [ANY_SPEC] * len(after),
        out_specs=tuple([HBM_SPEC] * n_arr),
        input_output_aliases={a: a for a in range(n_arr)},
        compiler_params=SPLIT_COPY_PARAMS,
    )(*lands, send_sems, recv_sems, *after)
    return list(res)


def _gather_share(lands, axes, name):
    n_arr = len(lands)
    fulls = [tuple(l.shape) for l in lands]

    def body(*refs):
        land_in, land = refs[:n_arr], refs[n_arr:2 * n_arr]
        send_sems, recv_sems = refs[2 * n_arr:]
        x, y, c, chips = _place()
        copies = []
        for a in range(n_arr):
            for j, k_j in enumerate([2 * chip[0] + chip[1] for chip in chips] + [2 * x + y]):
                cp = pltpu.make_async_remote_copy(
                    src_ref=_shard_region(land_in[a], fulls[a], axes[a], k_j, c),
                    dst_ref=_shard_region(land[a], fulls[a], axes[a], k_j, c),
                    send_sem=send_sems.at[a * 4 + j], recv_sem=recv_sems.at[a * 4 + j],
                    device_id=(x, y, 1 - c), device_id_type=MESH)
                cp.start()
                copies.append(cp)
        for cp in copies:
            cp.wait()

    return pl.pallas_call(
        body, name=name, out_shape=[jax.ShapeDtypeStruct(f, BF16) for f in fulls],
        in_specs=[ANY_SPEC] * n_arr, out_specs=[ANY_SPEC] * n_arr,
        input_output_aliases={a: a for a in range(n_arr)},
        scratch_shapes=[pltpu.SemaphoreType.DMA((4 * n_arr,)), pltpu.SemaphoreType.DMA((4 * n_arr,))],
    )(*lands)


def _scatter_shapes(grads, axes):
    out = []
    for g, ax in zip(grads, axes):
        shp = list(g.shape)
        shp[ax] //= 4
        out.append((3,) + tuple(shp[1:]))
    return out


def _scatter_start(grads, axes, name, after):
    n_arr = len(grads)
    shapes = _scatter_shapes(grads, axes)
    lands = [lax.empty(s, BF16) for s in shapes]

    def body(*refs):
        ins, land = refs[:n_arr], refs[n_arr:2 * n_arr]
        send_sems, recv_sems = refs[2 * n_arr + 1:2 * n_arr + 3]
        token = refs[-1]
        x, y, c, chips = _place()
        for a in range(n_arr):
            for j, chip in enumerate(chips):
                k_j = 2 * chip[0] + chip[1]
                pltpu.make_async_remote_copy(
                    src_ref=_shard_region(ins[a], grads[a].shape, axes[a], k_j), dst_ref=land[a].at[pl.ds(j, 1)],
                    send_sem=send_sems.at[a * 3 + j], recv_sem=recv_sems.at[a * 3 + j],
                    device_id=(*chip, c), device_id_type=MESH).start()
        token[...] = jnp.zeros_like(token)

    res = pl.pallas_call(
        body, name=name,
        out_shape=(pltpu.SemaphoreType.DMA((3 * n_arr,)), pltpu.SemaphoreType.DMA((3 * n_arr,)),
                   *[pltpu.HBM(g.shape, BF16) for g in grads], *[pltpu.HBM(s, BF16) for s in shapes],
                   jax.ShapeDtypeStruct((8, 128), F32)),
        in_specs=[HBM_SPEC] * (2 * n_arr) + [ANY_SPEC],
        out_specs=(SEM_SPEC, SEM_SPEC, *[HBM_SPEC] * (2 * n_arr), pl.BlockSpec(memory_space=pltpu.VMEM)),
        input_output_aliases={a: 2 + a for a in range(2 * n_arr)},
        compiler_params=SPLIT_COPY_PARAMS,
    )(*_in_hbm(grads), *_in_hbm(lands), after)
    return res[0], res[1], list(res[2:2 + n_arr]), list(res[2 + n_arr:2 + 2 * n_arr]), res[-1]


def _scatter_wait(send_sems, recv_sems, grads, lands, axes, name, after):
    n_arr = len(grads)

    def body(*refs):
        ins, land = refs[:n_arr], refs[n_arr:2 * n_arr]
        s_sems, r_sems = refs[2 * n_arr:2 * n_arr + 2]
        x, y, c, chips = _place()
        for a in range(n_arr):
            for j, chip in enumerate(chips):
                k_j = 2 * chip[0] + chip[1]
                cp = pltpu.make_async_remote_copy(
                    src_ref=_shard_region(ins[a], grads[a].shape, axes[a], k_j), dst_ref=land[a].at[pl.ds(j, 1)],
                    send_sem=s_sems.at[a * 3 + j], recv_sem=r_sems.at[a * 3 + j],
                    device_id=(x, y, c), device_id_type=MESH)
                cp.wait_send()
                cp.wait_recv()

    res = pl.pallas_call(
        body, name=name,
        out_shape=(*[pltpu.HBM(g.shape, BF16) for g in grads], *[pltpu.HBM(l.shape, BF16) for l in lands]),
        in_specs=[HBM_SPEC] * (2 * n_arr) + [SEM_SPEC, SEM_SPEC, ANY_SPEC],
        out_specs=tuple([HBM_SPEC] * (2 * n_arr)),
        input_output_aliases={a: a for a in range(2 * n_arr)},
        compiler_params=SPLIT_COPY_PARAMS,
    )(*grads, *lands, send_sems, recv_sems, after)
    return list(res[:n_arr]), list(res[n_arr:])


def _swap_start(arrs, name, after):
    n_arr = len(arrs)
    lands = [lax.empty(a.shape, a.dtype) for a in arrs]

    def body(*refs):
        ins, land = refs[:n_arr], refs[n_arr:2 * n_arr]
        send_sems, recv_sems = refs[2 * n_arr + 1:2 * n_arr + 3]
        token = refs[-1]
        x, y, c, _ = _place()
        for a in range(n_arr):
            pltpu.make_async_remote_copy(
                src_ref=ins[a], dst_ref=land[a], send_sem=send_sems.at[a], recv_sem=recv_sems.at[a],
                device_id=(x, y, 1 - c), device_id_type=MESH).start()
        token[...] = jnp.zeros_like(token)

    res = pl.pallas_call(
        body, name=name,
        out_shape=(pltpu.SemaphoreType.DMA((n_arr,)), pltpu.SemaphoreType.DMA((n_arr,)),
                   *[pltpu.HBM(a.shape, a.dtype) for a in arrs], *[pltpu.HBM(a.shape, a.dtype) for a in arrs],
                   jax.ShapeDtypeStruct((8, 128), F32)),
        in_specs=[HBM_SPEC] * (2 * n_arr) + [ANY_SPEC],
        out_specs=(SEM_SPEC, SEM_SPEC, *[HBM_SPEC] * (2 * n_arr), pl.BlockSpec(memory_space=pltpu.VMEM)),
        input_output_aliases={a: 2 + a for a in range(2 * n_arr)},
        compiler_params=SPLIT_COPY_PARAMS,
    )(*_in_hbm(arrs), *_in_hbm(lands), after)
    return res[0], res[1], list(res[2:2 + n_arr]), list(res[2 + n_arr:2 + 2 * n_arr]), res[-1]


def _swap_wait(send_sems, recv_sems, arrs, lands, name, after):
    n_arr = len(arrs)

    def body(*refs):
        ins, land = refs[:n_arr], refs[n_arr:2 * n_arr]
        s_sems, r_sems = refs[2 * n_arr:2 * n_arr + 2]
        x, y, c, _ = _place()
        for a in range(n_arr):
            cp = pltpu.make_async_remote_copy(
                src_ref=ins[a], dst_ref=land[a], send_sem=s_sems.at[a], recv_sem=r_sems.at[a],
                device_id=(x, y, c), device_id_type=MESH)
            cp.wait_send()
            cp.wait_recv()

    res = pl.pallas_call(
        body, name=name,
        out_shape=(*[pltpu.HBM(a.shape, a.dtype) for a in arrs], *[pltpu.HBM(a.shape, a.dtype) for a in arrs]),
        in_specs=[HBM_SPEC] * (2 * n_arr) + [SEM_SPEC, SEM_SPEC, ANY_SPEC],
        out_specs=tuple([HBM_SPEC] * (2 * n_arr)),
        input_output_aliases={a: a for a in range(2 * n_arr)},
        compiler_params=SPLIT_COPY_PARAMS,
    )(*arrs, *lands, send_sems, recv_sems, after)
    return list(res[:n_arr]), list(res[n_arr:])


def _mm_tn(a, b, name, out_dtype=BF16):
    L, m = a.shape
    n = b.shape[1]
    bm, bn, bk = min(m, 1024), min(n, 1024), min(L, 2048)
    nk = L // bk

    def body(a_ref, b_ref, o_ref, acc):
        k = pl.program_id(2)

        @pl.when(k == 0)
        def _():
            acc[...] = jnp.zeros_like(acc)

        acc[...] += _dot_tn(_bf(a_ref[...]), _bf(b_ref[...]))

        @pl.when(k == nk - 1)
        def _():
            o_ref[...] = acc[...].astype(out_dtype)

    return pl.pallas_call(
        body, name=name, grid=(m // bm, n // bn, nk),
        in_specs=[pl.BlockSpec((bk, bm), lambda i, j, k: (k, i)), pl.BlockSpec((bk, bn), lambda i, j, k: (k, j))],
        out_specs=pl.BlockSpec((bm, bn), lambda i, j, k: (i, j)),
        out_shape=jax.ShapeDtypeStruct((m, n), out_dtype),
        scratch_shapes=[pltpu.VMEM((bm, bn), F32)],
        compiler_params=_params(3),
    )(a, b)


def _mm_tn_blocks(a, b, wa, wb, name, after=None):
    L = a.shape[0]
    nb = a.shape[1] // wa
    bk = min(L, 1024)
    nk = L // bk

    def body(a_ref, b_ref, o_ref):
        @pl.when(pl.program_id(1) == 0)
        def _():
            o_ref[...] = jnp.zeros_like(o_ref)

        o_ref[...] += _dot_tn(_bf(a_ref[...]), _bf(b_ref[...]))

    call, tail = _call_after(
        after, body, 2, name=name, grid=(nb, nk),
        in_specs=[pl.BlockSpec((bk, wa), lambda j, k: (k, j)), pl.BlockSpec((bk, wb), lambda j, k: (k, j))],
        out_specs=pl.BlockSpec((None, wa, wb), lambda j, k: (j, 0, 0)),
        out_shape=jax.ShapeDtypeStruct((nb, wa, wb), F32),
        compiler_params=_params(2),
    )
    return call(a, b, *tail)


def _sum_parts(parts, owns, axes, chip, name):
    n_arr = len(parts)
    steps = 4
    in_specs, out_specs, shapes = [], [], []
    for part, axis in zip(parts, axes):
        _, r, c = part.shape
        tr = r // steps
        shapes.append((r, c))
        in_specs.append(pl.BlockSpec((3, tr, c), lambda i, k: (0, i, 0)))
        out_specs.append(pl.BlockSpec((tr, c), lambda i, k: (i, 0)))
    for part, axis in zip(parts, axes):
        _, r, c = part.shape
        tr = r // steps
        if axis == 1:
            in_specs.append(pl.BlockSpec((None, tr, c), lambda i, k: (0, steps * k[0] + i, 0)))
        else:
            in_specs.append(pl.BlockSpec((None, tr, c), lambda i, k: (0, i, k[0])))

    def body(k_ref, *refs):
        for a in range(n_arr):
            p = refs[a][...].astype(F32)
            refs[2 * n_arr + a][...] = ((p[0] + p[1]) + p[2]) + refs[n_arr + a][...].astype(F32)

    return pl.pallas_call(
        body, name=name,
        grid_spec=pltpu.PrefetchScalarGridSpec(num_scalar_prefetch=1, grid=(steps,), in_specs=in_specs, out_specs=out_specs),
        out_shape=[jax.ShapeDtypeStruct(sh, F32) for sh in shapes],
        compiler_params=_params(1),
    )(chip, *parts, *owns)


def _adamw(w, g_parts, m, v, name):
    n_g = len(g_parts)
    if w.ndim == 2:
        r, c = w.shape
        tr = r
        for cand in (512, 256, 128, 64, 32, 16, 8):
            if r % cand == 0 and cand * c * 4 <= (2 << 20):
                tr = cand
                break
        spec = pl.BlockSpec((tr, c), lambda i: (i, 0))
        tiling = dict(grid=(r // tr,), in_specs=[spec] * (3 + n_g), out_specs=[spec] * 4, compiler_params=_params(1))
    else:
        tiling = dict(compiler_params=pltpu.CompilerParams(vmem_limit_bytes=48 << 20))

    def body(*refs):
        w_ref, g_refs, m_ref, v_ref = refs[0], refs[1:1 + n_g], refs[1 + n_g], refs[2 + n_g]
        g = g_refs[0][...]
        for gr in g_refs[1:]:
            g = g + gr[...]
        _adamw_update(g, w_ref, m_ref, v_ref, *refs[3 + n_g:])

    return pl.pallas_call(body, name=name, out_shape=[jax.ShapeDtypeStruct(w.shape, F32)] * 4, **tiling)(w, *g_parts, m, v)


def _adamw_update(g, w_ref, m_ref, v_ref, g_out, d_out, m_out, v_out):
    m_new = ADAM_B1 * m_ref[...] + (1.0 - ADAM_B1) * g
    v_new = ADAM_B2 * v_ref[...] + (1.0 - ADAM_B2) * (g * g)
    m_hat = m_new * (1.0 / (1.0 - ADAM_B1 ** ADAM_STEP))
    v_hat = v_new * (1.0 / (1.0 - ADAM_B2 ** ADAM_STEP))
    g_out[...] = g
    d_out[...] = -ADAM_LR * (m_hat / (jnp.sqrt(v_hat) + ADAM_EPS) + ADAM_WD * w_ref[...])
    m_out[...] = m_new
    v_out[...] = v_new


def _adamw_many(ws, gs, ms, vs, name):
    n = len(ws)

    def body(*refs):
        for k in range(n):
            _adamw_update(refs[n + k][...], refs[k], refs[2 * n + k], refs[3 * n + k],
                          refs[4 * n + k], refs[5 * n + k], refs[6 * n + k], refs[7 * n + k])

    outs = pl.pallas_call(body, name=name, out_shape=[jax.ShapeDtypeStruct(t.shape, F32) for t in ws] * 4,
                          compiler_params=pltpu.CompilerParams(vmem_limit_bytes=56 << 20))(*ws, *gs, *ms, *vs)
    return [outs[part * n:(part + 1) * n] for part in range(4)]


def _adamw_layers(w, q_mine, q_sib, m, v, name, after=None):
    n, r, c = w.shape
    tr = r
    for cand in (512, 256, 128, 64, 32, 16, 8):
        if r % cand == 0 and cand * c * 4 <= (1 << 20):
            tr = cand
            break

    def body(*refs):
        w_ref, qm, qs, m_ref, v_ref = refs[0], refs[1:1 + n], refs[1 + n:1 + 2 * n], refs[1 + 2 * n], refs[2 + 2 * n]
        layer = pl.program_id(0)
        g = qm[0][...] + qs[0][...]
        for k in range(1, n):
            g = jnp.where(layer == k, qm[k][...] + qs[k][...], g)
        _adamw_update(g, w_ref, m_ref, v_ref, *refs[3 + 2 * n:])

    stacked = pl.BlockSpec((None, tr, c), lambda l, i: (l, i, 0))
    per_layer = [pl.BlockSpec((tr, c), lambda l, i, k=k: (jnp.where(l == k, i, 0), 0)) for k in range(n)]
    call, tail = _call_after(
        after, body, 3 + 2 * n, name=name, grid=(n, r // tr),
        in_specs=[stacked] + per_layer + per_layer + [stacked, stacked], out_specs=[stacked] * 4,
        out_shape=[jax.ShapeDtypeStruct(w.shape, F32)] * 4,
        compiler_params=_params(2),
    )
    return call(w, *q_mine, *q_sib, m, v, *tail)


def _ada_fwd(c16, ada_w, ada_b_cols):
    cols = ada_w.shape[2]

    def body(c_ref, w_ref, b_ref, o_ref):
        cv = c_ref[...]
        ca = _bf(cv * jax.nn.sigmoid(cv))
        o_ref[...] = _dot(ca, _bf(w_ref[...])) + b_ref[...]

    return pl.pallas_call(
        body, name="ada_fwd", grid=(DEPTH,),
        in_specs=[_whole((16, D)), pl.BlockSpec((None, D, cols), lambda i: (i, 0, 0)),
                  pl.BlockSpec((None, 1, cols), lambda i: (i, 0, 0))],
        out_specs=pl.BlockSpec((None, 16, cols), lambda i: (i, 0, 0)),
        out_shape=jax.ShapeDtypeStruct((DEPTH, 16, cols), F32),
        compiler_params=_params(1),
    )(c16, ada_w, ada_b_cols)


def _ada_bwd(c16, dmod16):
    cols = dmod16.shape[2]

    def body(c_ref, d_ref, o_ref):
        cv = c_ref[...]
        ca = _bf(cv * jax.nn.sigmoid(cv))
        o_ref[...] = _dot_tn(ca, _bf(d_ref[...]))

    return pl.pallas_call(
        body, name="ada_bwd", grid=(DEPTH,),
        in_specs=[_whole((16, D)), pl.BlockSpec((None, 16, cols), lambda i: (i, 0, 0))],
        out_specs=pl.BlockSpec((None, D, cols), lambda i: (i, 0, 0)),
        out_shape=jax.ShapeDtypeStruct((DEPTH, D, cols), F32),
        compiler_params=_params(1),
    )(c16, dmod16)


def _mod_bwd(vs_mix, vs_ffn, pv):
    def body(m_ref, f_ref, pv_ref, o_ref):
        for i in range(DEPTH):
            vm, vf, p = m_ref[i], f_ref[i], pv_ref[i]
            o_ref[i] = jnp.concatenate([
                vm[2:3], vm[1:2] * p[R_N1:R_N1 + 1], vm[0:1],
                vf[2:3], vf[1:2] * p[R_N2:R_N2 + 1], vf[0:1],
                vm[1:2] * (1.0 + p[R_SC1:R_SC1 + 1]), vf[1:2] * (1.0 + p[R_SC2:R_SC2 + 1])], axis=0)

    return pl.pallas_call(body, name="mod_bwd", out_shape=jax.ShapeDtypeStruct((DEPTH, 8, D), F32))(vs_mix, vs_ffn, pv)


def _ffn_fwd(x1, pv, w1, w2, layer, tm, after=None):
    L = x1.shape[0]
    dff = w1.shape[2]

    def body(x1_ref, pv_ref, w1_ref, w2_ref, x2_ref, h2_ref, a_ref, f_ref):
        x1v, p = x1_ref[...], pv_ref[...]
        h2, _, _ = _norm_mod(x1v, p[R_N2:R_N2 + 1], p[R_SC2:R_SC2 + 1], p[R_SH2:R_SH2 + 1])
        hb = _bf(h2)
        h2_ref[...] = hb
        a = _dot(hb, w1_ref[...])
        a_ref[...] = a
        ra = jnp.maximum(a, 0.0)
        f = _dot(_bf(ra * ra), w2_ref[...])
        f_ref[...] = f
        x2_ref[...] = x1v + p[R_G2:R_G2 + 1] * f

    call, tail = _call_after(
        after, body, 4, name=f"ffn_fwd{layer}", grid=(L // tm,),
        in_specs=[_rows(tm, D), pl.BlockSpec((None, 8, D), lambda i: (layer, 0, 0)), _layer_w(D, dff, 0), _layer_w(dff, D, 0)],
        out_specs=[_rows(tm, D), _rows(tm, D), _rows(tm, dff), _rows(tm, D)],
        out_shape=[jax.ShapeDtypeStruct((L, D), F32), jax.ShapeDtypeStruct((L, D), BF16),
                   jax.ShapeDtypeStruct((L, dff), F32), jax.ShapeDtypeStruct((L, D), F32)],
        compiler_params=_params(1, 56),
    )
    return call(x1, pv, w1, w2, *tail)


def _ffn_bwd(dx2, x1, a, f, pv, w1, w2, layer, tm, after=None):
    L = x1.shape[0]
    dff = w1.shape[2]
    extra = [] if after is None else [pl.BlockSpec(memory_space=pl.ANY)]
    extra_args = [] if after is None else [after]

    def body(dx2_ref, x1_ref, a_ref, f_ref, pv_ref, w1_ref, w2_ref, *rest):
        dx1_ref, p_ref, da_ref, df_ref, vs_ref = rest[len(extra):]

        @pl.when(pl.program_id(0) == 0)
        def _():
            vs_ref[...] = jnp.zeros_like(vs_ref)

        dx2v, p = dx2_ref[...], pv_ref[...]
        dfb = _bf(dx2v * p[R_G2:R_G2 + 1])
        df_ref[...] = dfb
        vs_ref[0:1, :] += _sum0(dx2v * f_ref[...])
        dp = _dot_nt(dfb, w2_ref[...])
        ra = jnp.maximum(a_ref[...], 0.0)
        p_ref[...] = _bf(ra * ra)
        dab = _bf(dp * (2.0 * ra))
        da_ref[...] = dab
        dh2 = _dot_nt(dab, w1_ref[...])
        _, xn, r = _norm_mod(x1_ref[...], p[R_N2:R_N2 + 1], p[R_SC2:R_SC2 + 1], p[R_SH2:R_SH2 + 1])
        dx1_ref[...] = dx2v + _norm_mod_bwd(dh2, xn, r, p[R_N2:R_N2 + 1], p[R_SC2:R_SC2 + 1])
        vs_ref[1:2, :] += _sum0(dh2 * xn)
        vs_ref[2:3, :] += _sum0(dh2)

    return pl.pallas_call(
        body, name=f"ffn_bwd{layer}", grid=(L // tm,),
        in_specs=[_rows(tm, D), _rows(tm, D), _rows(tm, dff), _rows(tm, D),
                  pl.BlockSpec((None, 8, D), lambda i: (layer, 0, 0)), _layer_w(D, dff, 0), _layer_w(dff, D, 0)] + extra,
        out_specs=[_rows(tm, D), _rows(tm, dff), _rows(tm, dff), _rows(tm, D), _whole((8, D))],
        out_shape=[jax.ShapeDtypeStruct((L, D), F32), jax.ShapeDtypeStruct((L, dff), BF16),
                   jax.ShapeDtypeStruct((L, dff), BF16), jax.ShapeDtypeStruct((L, D), BF16),
                   jax.ShapeDtypeStruct((8, D), F32)],
        compiler_params=_params(1, 56),
    )(dx2, x1, a, f, pv, w1, w2, *extra_args)


def _conv_fwd(x, pv, w_in, w_out, cw, layer, j, tm, after=None):
    L = x.shape[0]

    def body(x_ref, pv_ref, win_ref, wout_ref, cw_ref, x1_ref, h_ref, bcx_ref, conv_ref, q_ref, y_ref, carry):
        @pl.when(pl.program_id(0) == 0)
        def _():
            carry[...] = jnp.zeros_like(carry)

        xv, p, cwv = x_ref[...], pv_ref[...], cw_ref[...]
        h, _, _ = _norm_mod(xv, p[R_N1:R_N1 + 1], p[R_SC1:R_SC1 + 1], p[R_SH1:R_SH1 + 1])
        hb = _bf(h)
        h_ref[...] = hb
        bcx = _dot(hb, win_ref[...])
        bcx_ref[...] = bcx
        z = bcx[:, D:2 * D] * bcx[:, 2 * D:]
        prev8 = carry[...]
        conv = cwv[0:1] * _shift_down(z, prev8, 2) + cwv[1:2] * _shift_down(z, prev8, 1) + cwv[2:3] * z + cwv[3:4]
        conv_ref[...] = conv
        qb = _bf(bcx[:, :D] * conv)
        q_ref[...] = qb
        y = _dot(qb, wout_ref[...])
        y_ref[...] = y
        x1_ref[...] = xv + p[R_G1:R_G1 + 1] * y
        carry[...] = z[tm - 8:tm]

    call, tail = _call_after(
        after, body, 5, name=f"conv_fwd{layer}", grid=(L // tm,),
        in_specs=[_rows(tm, D), pl.BlockSpec((None, 8, D), lambda i: (layer, 0, 0)), _layer_w(D, 3 * D, 0), _layer_w(D, D, 0),
                  pl.BlockSpec((None, 8, D), lambda i: (j, 0, 0))],
        out_specs=[_rows(tm, D), _rows(tm, D), _rows(tm, 3 * D), _rows(tm, D), _rows(tm, D), _rows(tm, D)],
        out_shape=[jax.ShapeDtypeStruct((L, D), F32), jax.ShapeDtypeStruct((L, D), BF16), jax.ShapeDtypeStruct((L, 3 * D), F32),
                   jax.ShapeDtypeStruct((L, D), F32), jax.ShapeDtypeStruct((L, D), BF16), jax.ShapeDtypeStruct((L, D), F32)],
        scratch_shapes=[pltpu.VMEM((8, D), F32)],
        compiler_params=_params(1, 56),
    )
    return call(x, pv, w_in, w_out, cw, *tail)


def _conv_bwd(dx1, x, y, bcx, conv, pv, w_in, w_out, cw, layer, j, tm, after=None):
    L = x.shape[0]
    nt = L // tm

    def body(dx1_ref, x_ref, y_ref, bcx_ref, conv_ref, halo_ref, pv_ref, win_ref, wout_ref, cw_ref,
             dx_ref, dbcx_ref, dy_ref, vs_ref, carry):
        gi = pl.program_id(0)
        tile = nt - 1 - gi

        @pl.when(gi == 0)
        def _():
            vs_ref[...] = jnp.zeros_like(vs_ref)
            carry[...] = jnp.zeros_like(carry)

        dx1v, p, cwv = dx1_ref[...], pv_ref[...], cw_ref[...]
        dyb = _bf(dx1v * p[R_G1:R_G1 + 1])
        dy_ref[...] = dyb
        vs_ref[0:1, :] += _sum0(dx1v * y_ref[...])
        dq = _dot_nt(dyb, wout_ref[...])
        bcx = bcx_ref[...]
        b, cg, xh = bcx[:, :D], bcx[:, D:2 * D], bcx[:, 2 * D:]
        db = dq * conv_ref[...]
        dc = dq * b
        z = cg * xh
        halo = halo_ref[...]
        zprev = jnp.where(tile > 0, halo[:, D:2 * D] * halo[:, 2 * D:], 0.0)
        vs_ref[3:4, :] += _sum0(dc * _shift_down(z, zprev, 2))
        vs_ref[4:5, :] += _sum0(dc * _shift_down(z, zprev, 1))
        vs_ref[5:6, :] += _sum0(dc * z)
        vs_ref[6:7, :] += _sum0(dc)
        next8 = carry[...]
        dz = cwv[2:3] * dc + cwv[1:2] * _shift_up(dc, next8, 1) + cwv[0:1] * _shift_up(dc, next8, 2)
        dbb, dcgb, dxhb = _bf(db), _bf(dz * xh), _bf(dz * cg)
        dbcx_ref[:, 0:D] = dbb
        dbcx_ref[:, D:2 * D] = dcgb
        dbcx_ref[:, 2 * D:3 * D] = dxhb
        dh = (_dot_nt(dbb, win_ref[:, 0:D]) + _dot_nt(dcgb, win_ref[:, D:2 * D])) + _dot_nt(dxhb, win_ref[:, 2 * D:3 * D])
        _, xn, r = _norm_mod(x_ref[...], p[R_N1:R_N1 + 1], p[R_SC1:R_SC1 + 1], p[R_SH1:R_SH1 + 1])
        dx_ref[...] = dx1v + _norm_mod_bwd(dh, xn, r, p[R_N1:R_N1 + 1], p[R_SC1:R_SC1 + 1])
        vs_ref[1:2, :] += _sum0(dh * xn)
        vs_ref[2:3, :] += _sum0(dh)
        carry[...] = dc[0:8]

    halo_spec = pl.BlockSpec((8, 3 * D), lambda i: (jnp.maximum((nt - 1 - i) * (tm // 8) - 1, 0), 0))
    call, tail = _call_after(
        after, body, 10, name=f"conv_bwd{layer}", grid=(nt,),
        in_specs=[_rows(tm, D, nt), _rows(tm, D, nt), _rows(tm, D, nt), _rows(tm, 3 * D, nt), _rows(tm, D, nt), halo_spec,
                  pl.BlockSpec((None, 8, D), lambda i: (layer, 0, 0)), _layer_w(D, 3 * D, 0), _layer_w(D, D, 0),
                  pl.BlockSpec((None, 8, D), lambda i: (j, 0, 0))],
        out_specs=[_rows(tm, D, nt), _rows(tm, 3 * D, nt), _rows(tm, D, nt), _whole((8, D))],
        out_shape=[jax.ShapeDtypeStruct((L, D), F32), jax.ShapeDtypeStruct((L, 3 * D), BF16),
                   jax.ShapeDtypeStruct((L, D), BF16), jax.ShapeDtypeStruct((8, D), F32)],
        scratch_shapes=[pltpu.VMEM((8, D), F32)],
        compiler_params=_params(1, 56),
    )
    return call(dx1, x, y, bcx, conv, bcx, pv, w_in, w_out, cw, *tail)


def _s5_discretize(a_re, a_im, log_dt, bt_re, bt_im):
    dt = jnp.exp(log_dt)
    mag = jnp.exp(a_re * dt)
    abar_re = mag * jnp.cos(a_im * dt)
    abar_im = mag * jnp.sin(a_im * dt)
    den = a_re * a_re + a_im * a_im
    nr = abar_re - 1.0
    ni = abar_im
    f_re = (nr * a_re + ni * a_im) / den
    f_im = (ni * a_re - nr * a_im) / den
    bbar_re = f_re * bt_re - f_im * bt_im
    bbar_im = f_re * bt_im + f_im * bt_re
    return abar_re, abar_im, bbar_re, bbar_im


def _s5_params_fwd(a_re, a_im, log_dt, bt_re, bt_im, after=None):
    def body(ar, ai, ld, br, bi, o_ar, o_ai, o_br, o_bi):
        r = _s5_discretize(ar[...], ai[...], ld[...], br[...], bi[...])
        o_ar[...], o_ai[...], o_br[...], o_bi[...] = r

    gp = jax.ShapeDtypeStruct((S5_G, S5_P), F32)
    hgp = jax.ShapeDtypeStruct((S5_H, S5_G, S5_P), F32)
    call, tail = _call_after(after, body, 5, name="s5_params_fwd", out_shape=[gp, gp, hgp, hgp],
                             in_specs=[pl.BlockSpec(memory_space=pltpu.VMEM)] * 5)
    return call(a_re, a_im, log_dt, bt_re, bt_im, *tail)


def _s5_params_bwd(a_re, a_im, log_dt, bt_re, bt_im, d_ar, d_ai, d_br, d_bi):
    def body(ar, ai, ld, br, bi, gar, gai, gbr, gbi, o_ar, o_ai, o_ld, o_br, o_bi):
        _, vjp = jax.vjp(_s5_discretize, ar[...], ai[...], ld[...], br[...], bi[...])
        r = vjp((gar[...], gai[...], gbr[...], gbi[...]))
        o_ar[...], o_ai[...], o_ld[...], o_br[...], o_bi[...] = r

    gp = jax.ShapeDtypeStruct((S5_G, S5_P), F32)
    hgp = jax.ShapeDtypeStruct((S5_H, S5_G, S5_P), F32)
    return pl.pallas_call(body, name="s5_params_bwd", out_shape=[gp, gp, jax.ShapeDtypeStruct((S5_G, 1), F32), hgp, hgp])(
        a_re, a_im, log_dt, bt_re, bt_im, d_ar, d_ai, d_br, d_bi)


NSEG = 8
SCAN_LANES = 1024


def _to_segments(x):
    n, c = x.shape
    return x.reshape(NSEG, n // NSEG, c).transpose(1, 0, 2).reshape(n, c)


def _from_segments(x):
    n, c = x.shape
    return x.reshape(n // NSEG, NSEG, c).transpose(1, 0, 2).reshape(n, c)


def _segment_scan(re_ref, im_ref, st_re, st_im, a_re, a_im, n_slabs, adjoint, write):
    for q in range(NSTATE // SCAN_LANES):
        ls = slice(q * SCAN_LANES, (q + 1) * SCAN_LANES)
        ar = jnp.broadcast_to(a_re[:, ls], (8, SCAN_LANES))
        ai = jnp.broadcast_to(a_im[:, ls], (8, SCAN_LANES))

        def step(k, carry, ls=ls, ar=ar, ai=ai):
            s_r, s_i = carry
            slab = (n_slabs - 1 - k) if adjoint else k
            rows = pl.ds(pl.multiple_of(slab * 8, 8), 8)
            b_r, b_i = re_ref[rows, ls], im_ref[rows, ls]
            if adjoint:
                n_r = b_r + ar * s_r + ai * s_i
                n_i = b_i - ai * s_r + ar * s_i
            else:
                n_r = ar * s_r - ai * s_i + b_r
                n_i = ar * s_i + ai * s_r + b_i
            if write:
                re_ref[rows, ls] = n_r
                im_ref[rows, ls] = n_i
            return n_r, n_i

        s_r, s_i = lax.fori_loop(0, n_slabs, step, (st_re[:, ls], st_im[:, ls]), unroll=4)
        st_re[:, ls] = s_r
        st_im[:, ls] = s_i


def _s5_segment_states(e_re, e_im, ar, ai, seg_len, adjoint):
    def body(ere_ref, eim_ref, ar_ref, ai_ref, ore_ref, oim_ref):
        p_r, p_i = ar_ref[...], ai_ref[...]
        if adjoint:
            p_i = -p_i
        acc_r, acc_i = jnp.ones_like(p_r), jnp.zeros_like(p_r)
        n = seg_len
        while n:
            if n & 1:
                acc_r, acc_i = acc_r * p_r - acc_i * p_i, acc_r * p_i + acc_i * p_r
            n >>= 1
            if n:
                p_r, p_i = p_r * p_r - p_i * p_i, 2.0 * p_r * p_i
        e_r, e_i = ere_ref[...], eim_ref[...]
        s_r, s_i = jnp.zeros_like(acc_r), jnp.zeros_like(acc_r)
        rows_r, rows_i = [None] * NSEG, [None] * NSEG
        order = range(NSEG - 1, -1, -1) if adjoint else range(NSEG)
        for j in order:
            rows_r[j], rows_i[j] = s_r, s_i
            s_r, s_i = (acc_r * s_r - acc_i * s_i + e_r[j:j + 1], acc_r * s_i + acc_i * s_r + e_i[j:j + 1])
        ore_ref[...] = jnp.concatenate(rows_r, axis=0)
        oim_ref[...] = jnp.concatenate(rows_i, axis=0)

    st = jax.ShapeDtypeStruct((NSEG, NSTATE), F32)
    return pl.pallas_call(body, name="s5_segment_states_bwd" if adjoint else "s5_segment_states_fwd", out_shape=[st, st])(
        e_re, e_im, ar, ai)


def _s5_fwd_ends(x, pv, w_in, b_re, b_im, ar, ai, layer, tm, after=None):
    L = x.shape[0]

    def body(x_ref, pv_ref, win_ref, bre_ref, bim_ref, ar_ref, ai_ref, h_ref, u_ref, ere_ref, eim_ref, bu_re, bu_im):
        @pl.when(pl.program_id(0) == 0)
        def _():
            ere_ref[...] = jnp.zeros_like(ere_ref)
            eim_ref[...] = jnp.zeros_like(eim_ref)

        p = pv_ref[...]
        h, _, _ = _norm_mod(x_ref[...], p[R_N1:R_N1 + 1], p[R_SC1:R_SC1 + 1], p[R_SH1:R_SH1 + 1])
        hb = _bf(h)
        h_ref[...] = hb
        u = _dot(hb, win_ref[...])
        u_ref[...] = u
        ub = _bf(u)
        for k in range(S5_NB):
            uk = ub[:, k * S5_BH:(k + 1) * S5_BH]
            bu_re[:, k * S5_BP:(k + 1) * S5_BP] = _dot(uk, bre_ref[k])
            bu_im[:, k * S5_BP:(k + 1) * S5_BP] = _dot(uk, bim_ref[k])
        _segment_scan(bu_re, bu_im, ere_ref, eim_ref, ar_ref[...], ai_ref[...], tm // 8, adjoint=False, write=False)

    call, tail = _call_after(
        after, body, 7, name="s5_fwd_ends", grid=(L // tm,),
        in_specs=[_rows(tm, D), pl.BlockSpec((None, 8, D), lambda i: (layer, 0, 0)), _layer_w(D, D, 0),
                  _const_w((S5_NB, S5_BH, S5_BP)), _const_w((S5_NB, S5_BH, S5_BP)), _whole((1, NSTATE)), _whole((1, NSTATE))],
        out_specs=[_rows(tm, D), _rows(tm, D), _whole((NSEG, NSTATE)), _whole((NSEG, NSTATE))],
        out_shape=[jax.ShapeDtypeStruct((L, D), BF16), jax.ShapeDtypeStruct((L, D), F32),
                   jax.ShapeDtypeStruct((NSEG, NSTATE), F32), jax.ShapeDtypeStruct((NSEG, NSTATE), F32)],
        scratch_shapes=[pltpu.VMEM((tm, NSTATE), F32), pltpu.VMEM((tm, NSTATE), F32)],
        compiler_params=_params(1, 56),
    )
    return call(x, pv, w_in, b_re, b_im, ar, ai, *tail)


def _s5_fwd_out(x, u, pv, b_re, b_im, s0_re, s0_im, ar, ai, c_re, c_im, dvec, glu_w, glu_b, w_out, layer, tm):
    L = x.shape[0]

    def body(x_ref, u_ref, pv_ref, bre_ref, bim_ref, s0re_ref, s0im_ref, ar_ref, ai_ref, cre_ref, cim_ref, d_ref, gw_ref,
             gb_ref, wout_ref, x1_ref, sre_ref, sim_ref, y1_ref, zg_ref, y3_ref, y_ref, st_re, st_im):
        @pl.when(pl.program_id(0) == 0)
        def _():
            st_re[...] = s0re_ref[...]
            st_im[...] = s0im_ref[...]

        p = pv_ref[...]
        uv = u_ref[...]
        ub = _bf(uv)
        for k in range(S5_NB):
            uk = ub[:, k * S5_BH:(k + 1) * S5_BH]
            sre_ref[:, k * S5_BP:(k + 1) * S5_BP] = _dot(uk, bre_ref[k])
            sim_ref[:, k * S5_BP:(k + 1) * S5_BP] = _dot(uk, bim_ref[k])
        _segment_scan(sre_ref, sim_ref, st_re, st_im, ar_ref[...], ai_ref[...], tm // 8, adjoint=False, write=True)
        parts = []
        for k in range(S5_NB):
            sl = slice(k * S5_BP, (k + 1) * S5_BP)
            parts.append(_dot(_bf(sre_ref[:, sl]), cre_ref[k]) - _dot(_bf(sim_ref[:, sl]), cim_ref[k]))
        y1 = jnp.concatenate(parts, axis=1) + d_ref[...] * uv
        y1_ref[...] = y1
        y2 = jax.nn.gelu(y1)
        zg = _dot(_bf(y2), gw_ref[...]) + gb_ref[...]
        zg_ref[...] = zg
        y3b = _bf(y2 * jax.nn.sigmoid(zg))
        y3_ref[...] = y3b
        y = _dot(y3b, wout_ref[...])
        y_ref[...] = y
        x1_ref[...] = x_ref[...] + p[R_G1:R_G1 + 1] * y

    return pl.pallas_call(
        body, name="s5_fwd_out", grid=(L // tm,),
        in_specs=[_rows(tm, D), _rows(tm, D), pl.BlockSpec((None, 8, D), lambda i: (layer, 0, 0)),
                  _const_w((S5_NB, S5_BH, S5_BP)), _const_w((S5_NB, S5_BH, S5_BP)),
                  _whole((NSEG, NSTATE)), _whole((NSEG, NSTATE)), _whole((1, NSTATE)), _whole((1, NSTATE)),
                  _const_w((S5_NB, S5_BP, S5_BH)), _const_w((S5_NB, S5_BP, S5_BH)), _whole((1, D)),
                  _layer_w(D, D, 0), _whole((1, D)), _layer_w(D, D, 0)],
        out_specs=[_rows(tm, D), _rows(tm, NSTATE), _rows(tm, NSTATE), _rows(tm, D), _rows(tm, D), _rows(tm, D), _rows(tm, D)],
        out_shape=[jax.ShapeDtypeStruct((L, D), F32), jax.ShapeDtypeStruct((L, NSTATE), F32), jax.ShapeDtypeStruct((L, NSTATE), F32),
                   jax.ShapeDtypeStruct((L, D), F32), jax.ShapeDtypeStruct((L, D), F32),
                   jax.ShapeDtypeStruct((L, D), BF16), jax.ShapeDtypeStruct((L, D), F32)],
        scratch_shapes=[pltpu.VMEM((NSEG, NSTATE), F32), pltpu.VMEM((NSEG, NSTATE), F32)],
        compiler_params=_params(1, 56),
    )(x, u, pv, b_re, b_im, s0_re, s0_im, ar, ai, c_re, c_im, dvec, glu_w, glu_b, w_out)


def _s5_bwd_ends(dx1, y, y1, zg, u, pv, c_re, c_im, ar, ai, dvec, glu_w, w_out, layer, tm, after=None):
    L = dx1.shape[0]
    nt = L // tm

    def body(dx1_ref, y_ref, y1_ref, zg_ref, u_ref, pv_ref, cre_ref, cim_ref, ar_ref, ai_ref, d_ref, gw_ref, wout_ref,
             dy_ref, y2_ref, dzg_ref, dy1_ref, dus_ref, ere_ref, eim_ref, vs_ref, g_re, g_im):
        @pl.when(pl.program_id(0) == 0)
        def _():
            vs_ref[...] = jnp.zeros_like(vs_ref)
            ere_ref[...] = jnp.zeros_like(ere_ref)
            eim_ref[...] = jnp.zeros_like(eim_ref)

        dx1v, p = dx1_ref[...], pv_ref[...]
        dyb = _bf(dx1v * p[R_G1:R_G1 + 1])
        dy_ref[...] = dyb
        vs_ref[0:1, :] += _sum0(dx1v * y_ref[...])
        dy3 = _dot_nt(dyb, wout_ref[...])
        y2, gelu_vjp = jax.vjp(jax.nn.gelu, y1_ref[...])
        y2_ref[...] = _bf(y2)
        gate = jax.nn.sigmoid(zg_ref[...])
        dzg = dy3 * y2 * gate * (1.0 - gate)
        dzgb = _bf(dzg)
        dzg_ref[...] = dzgb
        vs_ref[1:2, :] += _sum0(dzg)
        dy2 = dy3 * gate + _dot_nt(dzgb, gw_ref[...])
        dy1 = gelu_vjp(dy2)[0]
        vs_ref[2:3, :] += _sum0(dy1 * u_ref[...])
        dus_ref[...] = dy1 * d_ref[...]
        dy1b = _bf(dy1)
        dy1_ref[...] = dy1b
        for k in range(S5_NB):
            dk = dy1b[:, k * S5_BH:(k + 1) * S5_BH]
            g_re[:, k * S5_BP:(k + 1) * S5_BP] = _dot_nt(dk, cre_ref[k])
            g_im[:, k * S5_BP:(k + 1) * S5_BP] = -_dot_nt(dk, cim_ref[k])
        _segment_scan(g_re, g_im, ere_ref, eim_ref, ar_ref[...], ai_ref[...], tm // 8, adjoint=True, write=False)

    call, tail = _call_after(
        after, body, 13, name="s5_bwd_ends", grid=(nt,),
        in_specs=[_rows(tm, D, nt)] * 5 + [pl.BlockSpec((None, 8, D), lambda i: (layer, 0, 0)),
                  _const_w((S5_NB, S5_BP, S5_BH)), _const_w((S5_NB, S5_BP, S5_BH)), _whole((1, NSTATE)), _whole((1, NSTATE)),
                  _whole((1, D)), _layer_w(D, D, 0), _layer_w(D, D, 0)],
        out_specs=[_rows(tm, D, nt)] * 5 + [_whole((NSEG, NSTATE)), _whole((NSEG, NSTATE)), _whole((8, D))],
        out_shape=[jax.ShapeDtypeStruct((L, D), BF16)] * 4 + [jax.ShapeDtypeStruct((L, D), F32),
                   jax.ShapeDtypeStruct((NSEG, NSTATE), F32), jax.ShapeDtypeStruct((NSEG, NSTATE), F32),
                   jax.ShapeDtypeStruct((8, D), F32)],
        scratch_shapes=[pltpu.VMEM((tm, NSTATE), F32), pltpu.VMEM((tm, NSTATE), F32)],
        compiler_params=_params(1, 56),
    )
    return call(dx1, y, y1, zg, u, pv, c_re, c_im, ar, ai, dvec, glu_w, w_out, *tail)


def _s5_bwd_in(dx1, dy1_b, du_skip, x, s_re, s_im, pv, b_re, b_im, c_re, c_im, l0_re, l0_im, ar, ai, w_in, layer, tm):
    L = x.shape[0]
    nt = L // tm

    def body(dx1_ref, dy1_ref, dus_ref, x_ref, sre_ref, sim_ref, hre_ref, him_ref, lre_ref, lim_ref, pv_ref, bre_ref, bim_ref,
             cre_ref, cim_ref, l0re_ref, l0im_ref, ar_ref, ai_ref, win_ref,
             dx_ref, du_ref, lamre_ref, lamim_ref, da_ref, vs_ref, g_re, g_im, st_re, st_im):
        gi = pl.program_id(0)
        tile = nt - 1 - gi

        @pl.when(gi == 0)
        def _():
            vs_ref[...] = jnp.zeros_like(vs_ref)
            da_ref[...] = jnp.zeros_like(da_ref)
            st_re[...] = l0re_ref[...]
            st_im[...] = l0im_ref[...]

        p = pv_ref[...]
        dy1b = dy1_ref[...]
        for k in range(S5_NB):
            dk = dy1b[:, k * S5_BH:(k + 1) * S5_BH]
            g_re[:, k * S5_BP:(k + 1) * S5_BP] = _dot_nt(dk, cre_ref[k])
            g_im[:, k * S5_BP:(k + 1) * S5_BP] = -_dot_nt(dk, cim_ref[k])
        _segment_scan(g_re, g_im, st_re, st_im, ar_ref[...], ai_ref[...], tm // 8, adjoint=True, write=True)
        lam_r, lam_i = g_re[...], g_im[...]
        lrb, lib = _bf(lam_r), _bf(lam_i)
        lamre_ref[...] = lrb
        lamim_ref[...] = lib

        def wrapped(last_ref):
            z = last_ref[...]
            row = lax.broadcasted_iota(jnp.int32, z.shape, 0)
            return jnp.where(row >= 1, pltpu.roll(z, 1, 0), 0.0)

        first_r = jnp.where(tile > 0, hre_ref[...], wrapped(lre_ref))
        first_i = jnp.where(tile > 0, him_ref[...], wrapped(lim_ref))
        sp_r = jnp.concatenate([first_r, sre_ref[0:tm - 8, :]], axis=0)
        sp_i = jnp.concatenate([first_i, sim_ref[0:tm - 8, :]], axis=0)
        da_ref[0:1, :] += _sum0(lam_r * sp_r + lam_i * sp_i)
        da_ref[1:2, :] += _sum0(lam_i * sp_r - lam_r * sp_i)

        parts = []
        for k in range(S5_NB):
            sl = slice(k * S5_BP, (k + 1) * S5_BP)
            parts.append(_dot_nt(lrb[:, sl], bre_ref[k]) + _dot_nt(lib[:, sl], bim_ref[k]))
        dub = _bf(jnp.concatenate(parts, axis=1) + dus_ref[...])
        du_ref[...] = dub
        dh = _dot_nt(dub, win_ref[...])
        _, xn, r = _norm_mod(x_ref[...], p[R_N1:R_N1 + 1], p[R_SC1:R_SC1 + 1], p[R_SH1:R_SH1 + 1])
        dx_ref[...] = dx1_ref[...] + _norm_mod_bwd(dh, xn, r, p[R_N1:R_N1 + 1], p[R_SC1:R_SC1 + 1])
        vs_ref[1:2, :] += _sum0(dh * xn)
        vs_ref[2:3, :] += _sum0(dh)

    halo = pl.BlockSpec((8, NSTATE), lambda i: (jnp.maximum((nt - 1 - i) * (tm // 8) - 1, 0), 0))
    last = pl.BlockSpec((8, NSTATE), lambda i: (L // 8 - 1, 0))
    return pl.pallas_call(
        body, name="s5_bwd_in", grid=(nt,),
        in_specs=[_rows(tm, D, nt), _rows(tm, D, nt), _rows(tm, D, nt), _rows(tm, D, nt), _rows(tm, NSTATE, nt), _rows(tm, NSTATE, nt),
                  halo, halo, last, last, pl.BlockSpec((None, 8, D), lambda i: (layer, 0, 0)),
                  _const_w((S5_NB, S5_BH, S5_BP)), _const_w((S5_NB, S5_BH, S5_BP)),
                  _const_w((S5_NB, S5_BP, S5_BH)), _const_w((S5_NB, S5_BP, S5_BH)),
                  _whole((NSEG, NSTATE)), _whole((NSEG, NSTATE)), _whole((1, NSTATE)), _whole((1, NSTATE)), _layer_w(D, D, 0)],
        out_specs=[_rows(tm, D, nt), _rows(tm, D, nt), _rows(tm, NSTATE, nt), _rows(tm, NSTATE, nt), _whole((8, NSTATE)), _whole((8, D))],
        out_shape=[jax.ShapeDtypeStruct((L, D), F32), jax.ShapeDtypeStruct((L, D), BF16),
                   jax.ShapeDtypeStruct((L, NSTATE), BF16), jax.ShapeDtypeStruct((L, NSTATE), BF16),
                   jax.ShapeDtypeStruct((8, NSTATE), F32), jax.ShapeDtypeStruct((8, D), F32)],
        scratch_shapes=[pltpu.VMEM((tm, NSTATE), F32), pltpu.VMEM((tm, NSTATE), F32),
                        pltpu.VMEM((NSEG, NSTATE), F32), pltpu.VMEM((NSEG, NSTATE), F32)],
        compiler_params=_params(1, 60),
    )(dx1, dy1_b, du_skip, x, s_re, s_im, s_re, s_im, s_re, s_im, pv, b_re, b_im, c_re, c_im, l0_re, l0_im, ar, ai, w_in)


def _blockdiag_b(bt):
    b = bt.reshape(S5_H, S5_NB, 16, S5_P).transpose(1, 2, 0, 3)
    eye = jnp.eye(16, dtype=bt.dtype)
    return (b[:, :, :, None, :] * eye[None, :, None, :, None]).reshape(S5_NB, S5_BH, S5_BP)


def _unblock_b(d):
    d = jnp.einsum("bghgp->bghp", d.reshape(S5_NB, 16, S5_H, 16, S5_P))
    return d.transpose(2, 0, 1, 3).reshape(S5_H, S5_G, S5_P)


def _blockdiag_c(cm):
    c4 = cm.reshape(S5_NB, 16, S5_H, S5_P)
    eye = jnp.eye(16, dtype=cm.dtype)
    out = c4.transpose(0, 1, 3, 2)[:, :, :, None, :] * eye[None, :, None, :, None]
    return out.reshape(S5_NB, S5_BP, S5_BH)


def _unblock_c(d):
    d = jnp.einsum("bgpgh->bghp", d.reshape(S5_NB, 16, S5_P, 16, S5_H))
    return d.reshape(S5_G, S5_H, S5_P)


def _tril_mask():
    return lax.broadcasted_iota(jnp.int32, (SG_CHUNK, SG_CHUNK), 0) >= lax.broadcasted_iota(jnp.int32, (SG_CHUNK, SG_CHUNK), 1)


def _sg_fwd(x, pv, w_in, w_s, b_t, vg, w_out, layer, tm, after=None):
    L = x.shape[0]
    nc = tm // SG_CHUNK

    def body(x_ref, pv_ref, win_ref, ws_ref, bt_ref, vg_ref, wout_ref, x1_ref, h_ref, uv_ref, vm_ref, q_ref, y_ref):
        xv, p = x_ref[...], pv_ref[...]
        h, _, _ = _norm_mod(xv, p[R_N1:R_N1 + 1], p[R_SC1:R_SC1 + 1], p[R_SH1:R_SH1 + 1])
        hb = _bf(h)
        h_ref[...] = hb
        uv = _dot(hb, win_ref[...])
        uv_ref[...] = uv
        v = uv[:, D:]
        rv = lax.rsqrt(jnp.mean(v * v, axis=-1, keepdims=True) + EPS)
        vnb = _bf((v * rv) * vg_ref[...])
        mask = _tril_mask()
        bt = bt_ref[...]
        for hd in range(SG_HEADS):
            wm = _bf(jnp.where(mask, ws_ref[hd], 0.0))
            cs = slice(hd * SG_CHUNK, (hd + 1) * SG_CHUNK)
            for ck in range(nc):
                rs = slice(ck * SG_CHUNK, (ck + 1) * SG_CHUNK)
                vm_ref[rs, cs] = _dot(wm, vnb[rs, cs]) + bt[:, hd:hd + 1]
        qb = _bf(uv[:, :D] * vm_ref[...])
        q_ref[...] = qb
        y = _dot(qb, wout_ref[...])
        y_ref[...] = y
        x1_ref[...] = xv + p[R_G1:R_G1 + 1] * y

    call, tail = _call_after(
        after, body, 7, name="sg_fwd", grid=(L // tm,),
        in_specs=[_rows(tm, D), pl.BlockSpec((None, 8, D), lambda i: (layer, 0, 0)), _layer_w(D, 2 * D, 0),
                  _whole((SG_HEADS, SG_CHUNK, SG_CHUNK)), _whole((SG_CHUNK, SG_HEADS)), _whole((1, D)), _layer_w(D, D, 0)],
        out_specs=[_rows(tm, D), _rows(tm, D), _rows(tm, 2 * D), _rows(tm, D), _rows(tm, D), _rows(tm, D)],
        out_shape=[jax.ShapeDtypeStruct((L, D), F32), jax.ShapeDtypeStruct((L, D), BF16), jax.ShapeDtypeStruct((L, 2 * D), F32),
                   jax.ShapeDtypeStruct((L, D), F32), jax.ShapeDtypeStruct((L, D), BF16), jax.ShapeDtypeStruct((L, D), F32)],
        compiler_params=_params(1, 56),
    )
    return call(x, pv, w_in, w_s, b_t, vg, w_out, *tail)


def _sg_bwd(dx1, x, y, uv, vm, pv, w_in, w_s, vg, w_out, layer, tm, after=None):
    L = x.shape[0]
    nc = tm // SG_CHUNK

    def body(dx1_ref, x_ref, y_ref, uv_ref, vm_ref, pv_ref, win_ref, ws_ref, vg_ref, wout_ref,
             dx_ref, duv_ref, dy_ref, vs_ref, dws_ref, dbt_ref, dvn_scr):
        @pl.when(pl.program_id(0) == 0)
        def _():
            vs_ref[...] = jnp.zeros_like(vs_ref)
            dws_ref[...] = jnp.zeros_like(dws_ref)
            dbt_ref[...] = jnp.zeros_like(dbt_ref)

        dx1v, p = dx1_ref[...], pv_ref[...]
        dyb = _bf(dx1v * p[R_G1:R_G1 + 1])
        dy_ref[...] = dyb
        vs_ref[0:1, :] += _sum0(dx1v * y_ref[...])
        dq = _dot_nt(dyb, wout_ref[...])
        uv = uv_ref[...]
        u, v = uv[:, :D], uv[:, D:]
        dub = _bf(dq * vm_ref[...])
        dvm = dq * u
        dvmb = _bf(dvm)
        rv = lax.rsqrt(jnp.mean(v * v, axis=-1, keepdims=True) + EPS)
        vh = v * rv
        vgv = vg_ref[...]
        vnb = _bf(vh * vgv)
        mask = _tril_mask()
        for hd in range(SG_HEADS):
            wm = _bf(jnp.where(mask, ws_ref[hd], 0.0))
            cs = slice(hd * SG_CHUNK, (hd + 1) * SG_CHUNK)
            dws = jnp.zeros((SG_CHUNK, SG_CHUNK), F32)
            dbs = jnp.zeros((SG_CHUNK, 1), F32)
            for ck in range(nc):
                rs = slice(ck * SG_CHUNK, (ck + 1) * SG_CHUNK)
                dvn_scr[rs, cs] = _dot_tn(wm, dvmb[rs, cs])
                dws = dws + _dot_nt(dvmb[rs, cs], vnb[rs, cs])
                dbs = dbs + jnp.sum(dvm[rs, cs], axis=1, keepdims=True)
            dws_ref[hd] += jnp.where(mask, dws, 0.0)
            dbt_ref[:, hd:hd + 1] += dbs
        dvn = dvn_scr[...]
        vs_ref[3:4, :] += _sum0(dvn * vh)
        dvnn = dvn * vgv
        dvb = _bf(rv * (dvnn - vh * jnp.mean(dvnn * vh, axis=-1, keepdims=True)))
        duv_ref[:, 0:D] = dub
        duv_ref[:, D:2 * D] = dvb
        dh = _dot_nt(dub, win_ref[:, 0:D]) + _dot_nt(dvb, win_ref[:, D:2 * D])
        _, xn, r = _norm_mod(x_ref[...], p[R_N1:R_N1 + 1], p[R_SC1:R_SC1 + 1], p[R_SH1:R_SH1 + 1])
        dx_ref[...] = dx1v + _norm_mod_bwd(dh, xn, r, p[R_N1:R_N1 + 1], p[R_SC1:R_SC1 + 1])
        vs_ref[1:2, :] += _sum0(dh * xn)
        vs_ref[2:3, :] += _sum0(dh)

    call, tail = _call_after(
        after, body, 10, name="sg_bwd", grid=(L // tm,),
        in_specs=[_rows(tm, D), _rows(tm, D), _rows(tm, D), _rows(tm, 2 * D), _rows(tm, D),
                  pl.BlockSpec((None, 8, D), lambda i: (layer, 0, 0)), _layer_w(D, 2 * D, 0),
                  _whole((SG_HEADS, SG_CHUNK, SG_CHUNK)), _whole((1, D)), _layer_w(D, D, 0)],
        out_specs=[_rows(tm, D), _rows(tm, 2 * D), _rows(tm, D), _whole((8, D)),
                   _whole((SG_HEADS, SG_CHUNK, SG_CHUNK)), _whole((SG_CHUNK, SG_HEADS))],
        out_shape=[jax.ShapeDtypeStruct((L, D), F32), jax.ShapeDtypeStruct((L, 2 * D), BF16), jax.ShapeDtypeStruct((L, D), BF16),
                   jax.ShapeDtypeStruct((8, D), F32), jax.ShapeDtypeStruct((SG_HEADS, SG_CHUNK, SG_CHUNK), F32),
                   jax.ShapeDtypeStruct((SG_CHUNK, SG_HEADS), F32)],
        scratch_shapes=[pltpu.VMEM((tm, D), F32)],
        compiler_params=_params(1, 56),
    )
    return call(dx1, x, y, uv, vm, pv, w_in, w_s, vg, w_out, *tail)


def _final(x, target, fg, tm):
    L = x.shape[0]

    def body(x_ref, t_ref, g_ref, dx_ref, vs_ref):
        @pl.when(pl.program_id(0) == 0)
        def _():
            vs_ref[...] = jnp.zeros_like(vs_ref)

        xv, g = x_ref[...], g_ref[...]
        r = lax.rsqrt(jnp.mean(xv * xv, axis=-1, keepdims=True) + EPS)
        xn = xv * r
        e = xn * g - t_ref[...]
        vs_ref[0:1, :] += jnp.sum(e * e)
        dout = e * (1.0 / D)
        vs_ref[1:2, :] += _sum0(dout * xn)
        dxn = dout * g
        dx_ref[...] = r * (dxn - xn * jnp.mean(dxn * xn, axis=-1, keepdims=True))

    return pl.pallas_call(
        body, name="final_loss", grid=(L // tm,),
        in_specs=[_rows(tm, D), _rows(tm, D), _whole((1, D))],
        out_specs=[_rows(tm, D), _whole((8, D))],
        out_shape=[jax.ShapeDtypeStruct((L, D), F32), jax.ShapeDtypeStruct((8, D), F32)],
        compiler_params=_params(1),
    )(x, target, fg)


def _pack_flat(arrs, multiple=LANES):
    flat = jnp.concatenate([a.reshape(-1).astype(F32) for a in arrs])
    return jnp.pad(flat, (0, -flat.shape[0] % multiple))


def _pack(arrs, row_multiple=8):
    return _pack_flat(arrs, row_multiple * LANES).reshape(-1, LANES)


def _unpack(buf, shapes, lead=()):
    flat = buf.reshape(lead + (-1,))
    out, off = [], 0
    for s in shapes:
        n = 1
        for d in s:
            n *= d
        out.append(flat[..., off:off + n].reshape(lead + tuple(s)))
        off += n
    return out


BIG = ("ff_w1", "ff_w2", "conv_w_in", "conv_w_out", "ssm_w_in", "ssm_glu_w", "ssm_w_out", "sg_w_in", "sg_w_out")
BIG_AXIS = {"ff_w1": 2, "ff_w2": 1, "conv_w_in": 2, "conv_w_out": 1, "ssm_w_in": 1, "ssm_glu_w": 1, "ssm_w_out": 1,
            "sg_w_in": 2, "sg_w_out": 1}
LAYER_WEIGHTS = (
    (("conv_w_in", 0), ("conv_w_out", 0), ("ff_w1", 0), ("ff_w2", 0)),
    (("ssm_w_in", 0), ("ssm_glu_w", 0), ("ssm_w_out", 0), ("ff_w1", 1), ("ff_w2", 1)),
    (("sg_w_in", 0), ("sg_w_out", 0), ("ff_w1", 2), ("ff_w2", 2)),
    (("conv_w_in", 1), ("conv_w_out", 1), ("ff_w1", 3), ("ff_w2", 3)),
)
GATHER_GROUPS = tuple(grp for lw in LAYER_WEIGHTS for grp in (lw[:-2], lw[-2:]))
SMALL_SHARDED = ("conv_w", "conv_b", "sg_v_g")
SMALL_WIDE_PADDED = ("ssm_b_re", "ssm_b_im")
SMALL = ("ada_b", "norm1_g", "norm2_g", "final_g", "ssm_a_re", "ssm_a_im", "ssm_log_dt", "ssm_b_re", "ssm_b_im", "ssm_c_re",
         "ssm_c_im", "ssm_d", "ssm_glu_b", "sg_w_s", "sg_b_s") + SMALL_SHARDED
WEIGHTS = ("ada_w", "ada_b", "norm1_g", "norm2_g", "ff_w1", "ff_w2", "final_g", "conv_w_in", "conv_w", "conv_b", "conv_w_out",
           "ssm_w_in", "ssm_a_re", "ssm_a_im", "ssm_log_dt", "ssm_b_re", "ssm_b_im", "ssm_c_re", "ssm_c_im", "ssm_d",
           "ssm_glu_w", "ssm_glu_b", "ssm_w_out", "sg_w_in", "sg_v_g", "sg_w_s", "sg_b_s", "sg_w_out")


def kernel(x, c, ada_w, ada_b, norm1_g, norm2_g, ff_w1, ff_w2, final_g, conv_w_in, conv_w, conv_b, conv_w_out, ssm_w_in, ssm_a_re, ssm_a_im, ssm_log_dt, ssm_b_re, ssm_b_im, ssm_c_re, ssm_c_im, ssm_d, ssm_glu_w, ssm_glu_b, ssm_w_out, sg_w_in, sg_v_g, sg_w_s, sg_b_s, sg_w_out, loss_target, m_ada_w, m_ada_b, m_norm1_g, m_norm2_g, m_ff_w1, m_ff_w2, m_final_g, m_conv_w_in, m_conv_w, m_conv_b, m_conv_w_out, m_ssm_w_in, m_ssm_a_re, m_ssm_a_im, m_ssm_log_dt, m_ssm_b_re, m_ssm_b_im, m_ssm_c_re, m_ssm_c_im, m_ssm_d, m_ssm_glu_w, m_ssm_glu_b, m_ssm_w_out, m_sg_w_in, m_sg_v_g, m_sg_w_s, m_sg_b_s, m_sg_w_out, v_ada_w, v_ada_b, v_norm1_g, v_norm2_g, v_ff_w1, v_ff_w2, v_final_g, v_conv_w_in, v_conv_w, v_conv_b, v_conv_w_out, v_ssm_w_in, v_ssm_a_re, v_ssm_a_im, v_ssm_log_dt, v_ssm_b_re, v_ssm_b_im, v_ssm_c_re, v_ssm_c_im, v_ssm_d, v_ssm_glu_w, v_ssm_glu_b, v_ssm_w_out, v_sg_w_in, v_sg_v_g, v_sg_w_s, v_sg_b_s, v_sg_w_out):
    args = dict(locals())
    w = {n: args[n] for n in WEIGHTS}
    m = {n: args["m_" + n] for n in WEIGHTS}
    v = {n: args["v_" + n] for n in WEIGHTS}
    L = x.shape[1]
    tm = min(L, 256)
    tm2 = min(L, 512)
    chip = 2 * lax.axis_index("x") + lax.axis_index("y")
    me = 2 * chip + lax.axis_index("c")
    xin = x[0]
    target = loss_target[0]
    chip1 = chip.reshape(1).astype(jnp.int32)
    place = jnp.stack([chip, lax.axis_index("c")]).astype(jnp.int32)

    gathers = []

    casts = {}

    def cast_group(g, after=None):
        entries = GATHER_GROUPS[g]
        axes = [BIG_AXIS[n] for n, _ in entries]
        casts[g] = _cast_place([w[n] for n, _ in entries], [li for _, li in entries], axes, place, f"cast_group{g}", after)

    def start_gather(g, after):
        if g not in casts:
            cast_group(g)
        axes = [BIG_AXIS[n] for n, _ in GATHER_GROUPS[g]]
        lands = casts[g]
        s_sems, r_sems, lands, token = _gather_start(lands, axes, f"gather_start{g}", after)
        gathers.append((s_sems, r_sems, lands, axes))
        return token

    def weights_of(g, after):
        s_sems, r_sems, lands, axes = gathers[g]
        lands = _gather_wait(s_sems, r_sems, lands, axes, f"gather_wait{g}", after)
        lands = _gather_share(lands, axes, f"gather_share{g}")
        token = start_gather(g + 2, lands[0]) if g + 2 < len(GATHER_GROUPS) else None
        return dict(zip([n for n, _ in GATHER_GROUPS[g]], lands)), token

    small_in = _pack([c, conv_w, conv_b, sg_v_g])
    got = _allgather_small(small_in, "gather_small_inputs").reshape(N_DEV, -1)
    c_all, cw_sh, cb_sh, vg_sh = _unpack(got, [(D,), conv_w.shape, conv_b.shape, sg_v_g.shape], lead=(N_DEV,))
    conv_w_full = jnp.concatenate([cw_sh[2 * k] for k in range(4)], axis=-1)
    conv_b_full = jnp.concatenate([cb_sh[2 * k] for k in range(4)], axis=-1)
    vg_full = jnp.concatenate([vg_sh[2 * k] for k in range(4)], axis=-1)
    c16 = jnp.pad(c_all, ((0, 16 - N_DEV), (0, 0)))

    cols = ada_w.shape[2]
    ada_b_cols = lax.dynamic_slice_in_dim(ada_b, chip * cols, cols, axis=1)[:, None, :]
    mod_sh = _ada_fwd(c16, ada_w, ada_b_cols)[:, :N_DEV, :]
    mod_all = _allgather_small(_pack([mod_sh]), "gather_mod").reshape(N_DEV, -1)
    mod_all = _unpack(mod_all, [mod_sh.shape], lead=(N_DEV,))[0]
    mod_mine = lax.dynamic_index_in_dim(mod_all[0::2], me, axis=2, keepdims=False)
    mod_mine = mod_mine.transpose(1, 0, 2).reshape(DEPTH, 6, D)
    pv = jnp.concatenate([mod_mine, norm1_g[:, None, :], norm2_g[:, None, :]], axis=1)

    second_started = start_gather(1, start_gather(0, pv))
    for g in range(2, len(GATHER_GROUPS)):
        cast_group(g, second_started)

    cw_rows = jnp.concatenate([conv_w_full, conv_b_full[:, None, :], jnp.zeros((conv_w_full.shape[0], 4, D), F32)], axis=1)

    a_re, a_im = ssm_a_re[0], ssm_a_im[0]
    log_dt = ssm_log_dt[0][:, None]
    bt_re, bt_im = ssm_b_re[0].transpose(2, 0, 1), ssm_b_im[0].transpose(2, 0, 1)
    abar_re, abar_im, bbar_re, bbar_im = _s5_params_fwd(a_re, a_im, log_dt, bt_re, bt_im, after=second_started)
    ar_vec, ai_vec = abar_re.reshape(1, NSTATE), abar_im.reshape(1, NSTATE)
    bd_re, bd_im = _bf(_blockdiag_b(bbar_re)), _bf(_blockdiag_b(bbar_im))
    cd_re, cd_im = _bf(_blockdiag_c(ssm_c_re[0])), _bf(_blockdiag_c(ssm_c_im[0]))

    saved = []
    fulls = []
    xl = xin
    for i in range(DEPTH):
        kind = MIXER_OF_LAYER[i]
        j = i // 3
        first_after = [second_started, bd_re, bd_im, cd_re, cd_im, cw_rows] + [casts[g][0] for g in range(2, len(GATHER_GROUPS))]
        full, tok = weights_of(2 * i, first_after if i == 0 else [xl])
        fulls.append(full)
        if kind == 0:
            x1, h, bcx, conv, q, y = _conv_fwd(xl, pv, full["conv_w_in"], full["conv_w_out"], cw_rows, i, j, tm2, after=tok)
            mix = dict(h=h, bcx=bcx, conv=conv, q=q, y=y)
        elif kind == 1:
            xp = _to_segments(xl)
            h, u, e_re, e_im = _s5_fwd_ends(xp, pv, full["ssm_w_in"], bd_re, bd_im, ar_vec, ai_vec, i, tm, after=tok)
            s0_re, s0_im = _s5_segment_states(e_re, e_im, ar_vec, ai_vec, L // NSEG, adjoint=False)
            x1p, s_re, s_im, y1, zg, y3, y = _s5_fwd_out(xp, u, pv, bd_re, bd_im, s0_re, s0_im, ar_vec, ai_vec, cd_re, cd_im,
                                                         ssm_d, full["ssm_glu_w"], ssm_glu_b, full["ssm_w_out"], i, tm)
            x1 = _from_segments(x1p)
            mix = dict(xp=xp, h=h, u=u, s_re=s_re, s_im=s_im, y1=y1, zg=zg, y3=y3, y=y)
        else:
            x1, h, uv, vm, q, y = _sg_fwd(xl, pv, full["sg_w_in"], sg_w_s[0], sg_b_s[0].T, vg_full, full["sg_w_out"], i, tm2,
                                          after=tok)
            mix = dict(h=h, uv=uv, vm=vm, q=q, y=y)
        ffn_weights, tok = weights_of(2 * i + 1, [x1])
        full.update(ffn_weights)
        x2, h2, a, f = _ffn_fwd(x1, pv, full["ff_w1"], full["ff_w2"], i, tm2, after=tok)
        saved.append(dict(x=xl, x1=x1, h2=h2, a=a, f=f, **mix))
        xl = x2

    dxl, vs_fin = _final(xl, target, final_g[None, :], tm2)

    gfull = {n: [None] * w[n].shape[0] for n in BIG}
    vs_mix, vs_ffn = [None] * DEPTH, [None] * DEPTH
    small_g = {}
    scatters = {}
    token = None

    def start_scatter(key, entries, after):
        garrs = [gfull[n][li][None] for n, li in entries]
        gaxes = [BIG_AXIS[n] for n, _ in entries]
        s_sems, r_sems, garrs, lands, tok = _scatter_start(garrs, gaxes, f"scatter_start{key}", after)
        scatters[key] = (s_sems, r_sems, garrs, lands, gaxes, entries)
        return tok

    for i in reversed(range(DEPTH)):
        kind = MIXER_OF_LAYER[i]
        j = i // 3
        sv = saved[i]
        full = fulls[i]
        dx1, p_b, da_b, df_b, vs_ffn[i] = _ffn_bwd(dxl, sv["x1"], sv["a"], sv["f"], pv, full["ff_w1"], full["ff_w2"], i, tm,
                                                   after=token)
        gfull["ff_w1"][i] = _mm_tn(sv["h2"], da_b, f"wgrad_ff_w1_{i}")
        gfull["ff_w2"][i] = _mm_tn(p_b, df_b, f"wgrad_ff_w2_{i}")
        if i == 0:
            token = start_scatter("0f", LAYER_WEIGHTS[0][2:], dx1)
        if kind == 0:
            dxl, dbcx_b, dy_b, vsm = _conv_bwd(dx1, sv["x"], sv["y"], sv["bcx"], sv["conv"], pv, full["conv_w_in"],
                                               full["conv_w_out"], cw_rows, i, j, tm2, after=token if i == 0 else None)
            gfull["conv_w_in"][j] = _mm_tn(sv["h"], dbcx_b, f"wgrad_conv_w_in_{j}")
            gfull["conv_w_out"][j] = _mm_tn(sv["q"], dy_b, f"wgrad_conv_w_out_{j}")
            small_g.setdefault("conv_w", [None, None])[j] = vsm[3:6]
            small_g.setdefault("conv_b", [None, None])[j] = vsm[6]
        elif kind == 1:
            dx1p = _to_segments(dx1)
            dy_b, y2_b, dzg_b, dy1_b, du_skip, eb_re, eb_im, vsm = _s5_bwd_ends(
                dx1p, sv["y"], sv["y1"], sv["zg"], sv["u"], pv, cd_re, cd_im, ar_vec, ai_vec, ssm_d, full["ssm_glu_w"],
                full["ssm_w_out"], i, tm)
            l0_re, l0_im = _s5_segment_states(eb_re, eb_im, ar_vec, ai_vec, L // NSEG, adjoint=True)
            dxp, du_b, lam_re, lam_im, dabar, vs_in = _s5_bwd_in(
                dx1p, dy1_b, du_skip, sv["xp"], sv["s_re"], sv["s_im"], pv, bd_re, bd_im, cd_re, cd_im, l0_re, l0_im,
                ar_vec, ai_vec, full["ssm_w_in"], i, tm)
            dxl = _from_segments(dxp)
            gfull["ssm_w_out"][0] = _mm_tn(sv["y3"], dy_b, "wgrad_ssm_w_out")
            gfull["ssm_glu_w"][0] = _mm_tn(y2_b, dzg_b, "wgrad_ssm_glu_w")
            gfull["ssm_w_in"][0] = _mm_tn(sv["h"], du_b, "wgrad_ssm_w_in")
            s5_late = dict(s_re=sv["s_re"], s_im=sv["s_im"], u=sv["u"], dy1_b=dy1_b, lam_re=lam_re, lam_im=lam_im, dabar=dabar)
            small_g.update(ssm_d=vsm[2], ssm_glu_b=vsm[1])
            vsm = jnp.concatenate([vsm[0:1], vs_in[1:3], jnp.zeros((5, D), F32)], axis=0)
        else:
            dxl, duv_b, dy_b, vsm, d_ws, d_bt = _sg_bwd(dx1, sv["x"], sv["y"], sv["uv"], sv["vm"], pv, full["sg_w_in"],
                                                        sg_w_s[0], vg_full, full["sg_w_out"], i, tm2)
            gfull["sg_w_in"][0] = _mm_tn(sv["h"], duv_b, "wgrad_sg_w_in")
            gfull["sg_w_out"][0] = _mm_tn(sv["q"], dy_b, "wgrad_sg_w_out")
            small_g.update(sg_v_g=vsm[3], sg_w_s=d_ws, sg_b_s=d_bt.T)
        vs_mix[i] = vsm
        token = start_scatter(str(i), LAYER_WEIGHTS[i], dxl) if i > 0 else start_scatter("0c", LAYER_WEIGHTS[0][:2], dxl)
    grad_x = dxl[None]

    sums = {n: [None] * w[n].shape[0] for n in BIG}

    def collect(key, after):
        s_sems, r_sems, garrs, lands, gaxes, entries = scatters[key]
        garrs, recv = _scatter_wait(s_sems, r_sems, garrs, lands, gaxes, f"scatter_wait{key}", after)
        for (n, li), t in zip(entries, _sum_parts(recv, garrs, gaxes, chip1, f"sum_group{key}")):
            sums[n][li] = t
        return sums[entries[-1][0]][entries[-1][1]]

    after = token
    for key in ("3", "2", "1"):
        after = collect(key, after)
    early = [(n, li) for i in (3, 2, 1) for n, li in LAYER_WEIGHTS[i]]
    late = list(LAYER_WEIGHTS[0][2:]) + list(LAYER_WEIGHTS[0][:2])
    s_sems, r_sems, mine_thru, lands, tok = _swap_start([sums[n][li] for n, li in early], "swap_start_early", after)

    blocks = dict(
        c_re=_mm_tn_blocks(s5_late["s_re"], s5_late["dy1_b"], S5_BP, S5_BH, "wgrad_s5_c_re", after=tok),
        c_im=_mm_tn_blocks(s5_late["s_im"], s5_late["dy1_b"], S5_BP, S5_BH, "wgrad_s5_c_im", after=tok),
        b_re=_mm_tn_blocks(s5_late["u"], s5_late["lam_re"], S5_BH, S5_BP, "wgrad_s5_b_re", after=tok),
        b_im=_mm_tn_blocks(s5_late["u"], s5_late["lam_im"], S5_BH, S5_BP, "wgrad_s5_b_im", after=tok))
    d_are, d_aim, d_ldt, d_btre, d_btim = _s5_params_bwd(
        a_re, a_im, log_dt, bt_re, bt_im, s5_late["dabar"][0].reshape(S5_G, S5_P), s5_late["dabar"][1].reshape(S5_G, S5_P),
        _unblock_b(blocks["b_re"]), _unblock_b(blocks["b_im"]))
    small_g.update(ssm_a_re=d_are, ssm_a_im=d_aim, ssm_log_dt=d_ldt, ssm_b_re=d_btre.transpose(1, 2, 0),
                   ssm_b_im=d_btim.transpose(1, 2, 0), ssm_c_re=_unblock_c(blocks["c_re"]), ssm_c_im=-_unblock_c(blocks["c_im"]))

    mine_thru, got = _swap_wait(s_sems, r_sems, mine_thru, lands, "swap_wait_early", blocks["b_im"])
    sib = dict(zip(early, got))
    for (n, li), t in zip(early, mine_thru):
        sums[n][li] = t
    after = got[-1]
    for key in ("0f", "0c"):
        after = collect(key, after)
    late_s, late_r, late_mine, late_lands, late_tok = _swap_start([sums[n][li] for n, li in late], "swap_start_late", after)

    dmod = _mod_bwd(jnp.stack(vs_mix), jnp.stack(vs_ffn), pv)
    small_g.update(ada_b=dmod[:, :6, :], norm1_g=dmod[:, 6, :], norm2_g=dmod[:, 7, :], final_g=vs_fin[1],
                   conv_w=jnp.stack(small_g["conv_w"]), conv_b=jnp.stack(small_g["conv_b"]))

    loss_part = (0.5 / D) * vs_fin[0, 0:1]
    part_shapes = [(1,)] + [tuple(small_g[n].shape) for n in SMALL]
    slots = _reduce_pair(_pack([loss_part] + [small_g[n] for n in SMALL], 16), "reduce_small_pair", late_tok)
    s_sems, r_sems, slots, tok = _reduce_cross_start(slots, "reduce_small_cross_start")

    res = {}
    in_layer0 = [n for n, _ in LAYER_WEIGHTS[0]]
    for n in [n for n in BIG if n not in in_layer0] + in_layer0:
        if n == in_layer0[0]:
            late_mine, got = _swap_wait(late_s, late_r, late_mine, late_lands, "swap_wait_late", tok)
            sib.update(zip(late, got))
            for (name, li), t in zip(late, late_mine):
                sums[name][li] = t
            tok = got[-1]
        res[n] = _adamw_layers(w[n], sums[n], [sib[(n, li)] for li in range(w[n].shape[0])], m[n], v[n], f"adamw_{n}", after=tok)
        tok = res[n][0]

    slots = _reduce_cross_wait(s_sems, r_sems, slots, "reduce_small_cross_wait", tok)
    parts_sum = _reduce_finish(slots, "reduce_small_finish")
    summed = _unpack(parts_sum, part_shapes)
    loss = summed[0][0]
    gsum = dict(zip(SMALL, summed[1:]))
    dmod_all = _allgather_small(_pack([small_g["ada_b"]]), "gather_dmod", parts_sum)
    dmod_all = dmod_all.reshape(N_DEV, DEPTH, 6 * D)
    dmod_cols = lax.dynamic_slice_in_dim(dmod_all, chip * cols, cols, axis=2).transpose(1, 0, 2)
    g_ada_w = _ada_bwd(c16, jnp.pad(dmod_cols, ((0, 0), (0, 16 - N_DEV), (0, 0))))

    shp = ada_w.shape
    two = lambda t: t.reshape(shp[0] * shp[1], shp[2])
    res["ada_w"] = [t.reshape(shp) for t in _adamw(two(ada_w), [two(g_ada_w)], two(m_ada_w), two(v_ada_w), "adamw_ada_w")]

    def mine(n):
        g = gsum[n]
        if n in SMALL_SHARDED:
            g = lax.dynamic_slice_in_dim(g, chip * w[n].shape[-1], w[n].shape[-1], axis=g.ndim - 1)
        return g.reshape(w[n].shape)

    for k, names in enumerate(([n for n in SMALL if n not in SMALL_WIDE_PADDED], list(SMALL_WIDE_PADDED))):
        outs = _adamw_many([w[n] for n in names], [mine(n) for n in names], [m[n] for n in names], [v[n] for n in names],
                           f"adamw_small{k}")
        for idx, n in enumerate(names):
            res[n] = [outs[part][idx] for part in range(4)]

    outs = [loss, grad_x]
    for part in range(4):
        outs += [res[n][part] for n in WEIGHTS]
    return tuple(outs)
```

```python
import functools

import jax
import jax.numpy as jnp
from jax import lax
from jax.experimental import pallas as pl
from jax.experimental.pallas import tpu as pltpu

F32 = jnp.float32
BF16 = jnp.bfloat16
D = 1024
EPS = 1e-6
DEPTH = 4
MIXER_OF_LAYER = (0, 1, 2, 0)
S5_G, S5_H, S5_P = 64, 16, 64
S5_NB = 4
S5_BH = S5_H * 16
S5_BP = S5_P * 16
NSTATE = S5_G * S5_P
SG_HEADS, SG_CHUNK = 8, 128
ADAM_LR, ADAM_B1, ADAM_B2, ADAM_EPS, ADAM_WD, ADAM_STEP = 0.001, 0.9, 0.999, 1e-08, 0.01, 10
N_DEV = 8
MESH = pl.DeviceIdType.MESH
LANES = 1024
R_SH1, R_SC1, R_G1, R_SH2, R_SC2, R_G2, R_N1, R_N2 = range(8)


def _dot(a, b):
    return jnp.dot(a, b, preferred_element_type=F32)


def _dot_nt(a, b):
    return lax.dot_general(a, b, (((1,), (1,)), ((), ())), preferred_element_type=F32)


def _dot_tn(a, b):
    return lax.dot_general(a, b, (((0,), (0,)), ((), ())), preferred_element_type=F32)


def _bf(x):
    return x.astype(BF16)


def _sum0(x):
    return jnp.sum(x, axis=0, keepdims=True)


def _params(n_axes, vmem_mb=48):
    return pltpu.CompilerParams(dimension_semantics=("arbitrary",) * n_axes, vmem_limit_bytes=vmem_mb << 20)


def _rows(tm, cols, nt=None):
    if nt is None:
        return pl.BlockSpec((tm, cols), lambda i: (i, 0))
    return pl.BlockSpec((tm, cols), lambda i: (nt - 1 - i, 0))


def _whole(shape):
    nd = len(shape)
    return pl.BlockSpec(shape, lambda *_: (0,) * nd)


def _layer_w(r, c, layer):
    return pl.BlockSpec((None, r, c), lambda *_: (layer, 0, 0), pipeline_mode=pl.Buffered(1))


def _const_w(shape):
    nd = len(shape)
    return pl.BlockSpec(shape, lambda *_: (0,) * nd, pipeline_mode=pl.Buffered(1))


def _call_after(after, body, n_in, *, in_specs, **kw):
    if after is None:
        return pl.pallas_call(body, in_specs=in_specs, **kw), ()

    def body_after(*refs):
        return body(*refs[:n_in], *refs[n_in + 1:])

    return pl.pallas_call(body_after, in_specs=list(in_specs) + [pl.BlockSpec(memory_space=pl.ANY)], **kw), (after,)


def _norm_mod(x, ng, sc, sh):
    r = lax.rsqrt(jnp.mean(x * x, axis=-1, keepdims=True) + EPS)
    xn = x * r
    return (xn * ng) * (1.0 + sc) + sh, xn, r


def _norm_mod_bwd(dh, xn, r, ng, sc):
    dxn = dh * (ng * (1.0 + sc))
    return r * (dxn - xn * jnp.mean(dxn * xn, axis=-1, keepdims=True))


def _shift_down(z, prev8, k):
    row = lax.broadcasted_iota(jnp.int32, z.shape, 0)
    if k == 1:
        return jnp.where(row >= 1, pltpu.roll(z, 1, 0), prev8[7:8])
    return jnp.where(row >= 2, pltpu.roll(z, 2, 0), jnp.where(row == 0, prev8[6:7], prev8[7:8]))


def _shift_up(z, next8, k):
    n = z.shape[0]
    row = lax.broadcasted_iota(jnp.int32, z.shape, 0)
    if k == 1:
        return jnp.where(row <= n - 2, pltpu.roll(z, n - 1, 0), next8[0:1])
    return jnp.where(row <= n - 3, pltpu.roll(z, n - 2, 0), jnp.where(row == n - 2, next8[0:1], next8[1:2]))


def _place():
    x, y, c = lax.axis_index("x"), lax.axis_index("y"), lax.axis_index("c")
    chips = [(1 - x, y), (x, 1 - y), (1 - x, 1 - y)]
    return x, y, c, chips


def _allgather_small(x_shard, name, after=None):
    m_per, n = x_shard.shape

    def body(x_ref, out_ref, send_sems, recv_sems, local_sem):
        x, y, c, chips = _place()
        me, sibling = (x, y, c), (x, y, 1 - c)

        def rows(px, py, pc):
            return out_ref.at[pl.ds((4 * px + 2 * py + pc) * m_per, m_per), :]

        def copy(k, block, to, src=None):
            return pltpu.make_async_remote_copy(
                src_ref=rows(*block) if src is None else src, dst_ref=rows(*block),
                send_sem=send_sems.at[k], recv_sem=recv_sems.at[k], device_id=to, device_id_type=MESH)

        mine = pltpu.make_async_copy(x_ref, rows(*me), local_sem)
        mine.start()
        first = [copy(0, me, sibling, src=x_ref)]
        first += [copy(1 + j, me, (*chip, c), src=x_ref) for j, chip in enumerate(chips)]
        for cp in first:
            cp.start()
        passed = [copy(4 + j, (*chip, c), sibling) for j, chip in enumerate(chips)]
        for j, chip in enumerate(chips):
            copy(1 + j, (*chip, c), me).wait_recv()
            passed[j].start()
        copy(0, sibling, me).wait_recv()
        for j, chip in enumerate(chips):
            copy(4 + j, (*chip, 1 - c), me).wait_recv()
        for cp in first + passed:
            cp.wait_send()
        mine.wait()

    call, tail = _call_after(
        after, body, 1, name=name, out_shape=jax.ShapeDtypeStruct((N_DEV * m_per, n), F32),
        in_specs=[pl.BlockSpec(memory_space=pltpu.VMEM)], out_specs=pl.BlockSpec(memory_space=pltpu.VMEM),
        scratch_shapes=[pltpu.SemaphoreType.DMA((7,)), pltpu.SemaphoreType.DMA((7,)), pltpu.SemaphoreType.DMA],
        compiler_params=pltpu.CompilerParams(vmem_limit_bytes=48 << 20),
    )
    return call(x_shard, *tail)


def _reduce_pair(x_part, name, after=None):
    m, n = x_part.shape
    h = m // 2

    def body(x_ref, slots_ref, sib_buf, send_sem, recv_sem):
        x, y, c, _ = _place()
        swap = pltpu.make_async_remote_copy(src_ref=x_ref, dst_ref=sib_buf, send_sem=send_sem, recv_sem=recv_sem,
                                            device_id=(x, y, 1 - c), device_id_type=MESH)
        swap.start()
        swap.wait()
        mine = pl.ds(pl.multiple_of(c * h, 8), h)
        slots_ref[pl.ds(2 * x + y, 1)] = (x_ref[mine, :] + sib_buf[mine, :])[None]

    call, tail = _call_after(
        after, body, 1, name=name, out_shape=jax.ShapeDtypeStruct((4, h, n), F32),
        in_specs=[pl.BlockSpec(memory_space=pltpu.VMEM)], out_specs=pl.BlockSpec(memory_space=pltpu.VMEM),
        scratch_shapes=[pltpu.VMEM((m, n), F32), pltpu.SemaphoreType.DMA, pltpu.SemaphoreType.DMA],
        compiler_params=pltpu.CompilerParams(vmem_limit_bytes=48 << 20),
    )
    return call(x_part, *tail)


def _reduce_cross_start(slots, name):
    def body(slots_ref, send_sems, recv_sems, thru, token):
        x, y, c, chips = _place()
        mine = slots_ref.at[pl.ds(2 * x + y, 1)]
        for j, chip in enumerate(chips):
            pltpu.make_async_remote_copy(src_ref=mine, dst_ref=mine, send_sem=send_sems.at[j], recv_sem=recv_sems.at[j],
                                         device_id=(*chip, c), device_id_type=MESH).start()
        token[...] = jnp.zeros_like(token)

    res = pl.pallas_call(
        body, name=name,
        out_shape=(pltpu.SemaphoreType.DMA((3,)), pltpu.SemaphoreType.DMA((3,)), pltpu.HBM(slots.shape, F32),
                   jax.ShapeDtypeStruct((8, 128), F32)),
        in_specs=[HBM_SPEC], out_specs=(SEM_SPEC, SEM_SPEC, HBM_SPEC, pl.BlockSpec(memory_space=pltpu.VMEM)),
        input_output_aliases={0: 2}, compiler_params=SPLIT_COPY_PARAMS,
    )(*_in_hbm([slots]))
    return res


def _reduce_cross_wait(send_sems, recv_sems, slots, name, after):
    def body(slots_ref, s_sems, r_sems, after_ref, thru):
        x, y, c, chips = _place()
        for j, chip in enumerate(chips):
            theirs = slots_ref.at[pl.ds(2 * chip[0] + chip[1], 1)]
            cp = pltpu.make_async_remote_copy(src_ref=theirs, dst_ref=theirs, send_sem=s_sems.at[j], recv_sem=r_sems.at[j],
                                              device_id=(x, y, c), device_id_type=MESH)
            cp.wait_send()
            cp.wait_recv()

    return pl.pallas_call(
        body, name=name, out_shape=pltpu.HBM(slots.shape, F32),
        in_specs=[HBM_SPEC, SEM_SPEC, SEM_SPEC, ANY_SPEC], out_specs=HBM_SPEC,
        input_output_aliases={0: 0}, compiler_params=SPLIT_COPY_PARAMS,
    )(slots, send_sems, recv_sems, after)


def _reduce_finish(slots, name):
    _, h, n = slots.shape

    def body(slots_ref, out_ref, send_sem, recv_sem):
        x, y, c, _ = _place()
        mine = pl.ds(pl.multiple_of(c * h, 8), h)
        theirs = pl.ds(pl.multiple_of((1 - c) * h, 8), h)
        out_ref[mine, :] = ((slots_ref[0] + slots_ref[1]) + slots_ref[2]) + slots_ref[3]
        give = pltpu.make_async_remote_copy(src_ref=out_ref.at[mine, :], dst_ref=out_ref.at[mine, :], send_sem=send_sem,
                                            recv_sem=recv_sem, device_id=(x, y, 1 - c), device_id_type=MESH)
        give.start()
        give.wait_send()
        pltpu.make_async_remote_copy(src_ref=out_ref.at[theirs, :], dst_ref=out_ref.at[theirs, :], send_sem=send_sem,
                                     recv_sem=recv_sem, device_id=(x, y, c), device_id_type=MESH).wait_recv()

    return pl.pallas_call(
        body, name=name, out_shape=jax.ShapeDtypeStruct((2 * h, n), F32),
        in_specs=[pl.BlockSpec(memory_space=pltpu.VMEM)], out_specs=pl.BlockSpec(memory_space=pltpu.VMEM),
        scratch_shapes=[pltpu.SemaphoreType.DMA, pltpu.SemaphoreType.DMA],
        compiler_params=pltpu.CompilerParams(vmem_limit_bytes=48 << 20),
    )(slots)


def _shard_region(ref, full_shape, axis, chip_k, half=None):
    _, r, c = full_shape
    if axis == 1:
        rs = r // 4
        if half is None:
            return ref.at[:, pl.ds(pl.multiple_of(chip_k * rs, 128), rs), :]
        return ref.at[:, pl.ds(pl.multiple_of(chip_k * rs + half * (rs // 2), 128), rs // 2), :]
    cs = c // 4
    if half is None:
        return ref.at[:, :, pl.ds(pl.multiple_of(chip_k * cs, 128), cs)]
    return ref.at[:, pl.ds(pl.multiple_of(half * (r // 2), 128), r // 2), pl.ds(pl.multiple_of(chip_k * cs, 128), cs)]


HBM_SPEC = pl.BlockSpec(memory_space=pltpu.HBM)
SEM_SPEC = pl.BlockSpec(memory_space=pltpu.SEMAPHORE)
ANY_SPEC = pl.BlockSpec(memory_space=pl.ANY)
SPLIT_COPY_PARAMS = pltpu.CompilerParams(has_side_effects=pltpu.SideEffectType.DATAFLOW_SIDE_EFFECTING)


def _in_hbm(arrs):
    return [pltpu.with_memory_space_constraint(a, pltpu.HBM) for a in arrs]


def _cast_place(ws, layers, axes, place, name, after=None):
    n_arr = len(ws)
    in_specs, out_specs, fulls = [], [], []
    for w_stack, li, axis in zip(ws, layers, axes):
        _, r, c = w_stack.shape
        tr = r // 4
        fulls.append((1, 4 * r, c) if axis == 1 else (1, r, 4 * c))
        in_specs.append(pl.BlockSpec((None, tr, c), lambda i, p, li=li: (li, 2 * p[1] + i, 0)))
        if axis == 1:
            out_specs.append(pl.BlockSpec((None, tr, c), lambda i, p: (0, 4 * p[0] + 2 * p[1] + i, 0)))
        else:
            out_specs.append(pl.BlockSpec((None, tr, c), lambda i, p: (0, 2 * p[1] + i, p[0])))

    extra = [] if after is None else [after]

    def body(p_ref, *refs):
        for a in range(n_arr):
            refs[n_arr + len(extra) + a][...] = _bf(refs[a][...])

    return pl.pallas_call(
        body, name=name,
        grid_spec=pltpu.PrefetchScalarGridSpec(num_scalar_prefetch=1, grid=(2,), in_specs=in_specs + [ANY_SPEC] * len(extra),
                                               out_specs=out_specs),
        out_shape=[jax.ShapeDtypeStruct(f, BF16) for f in fulls],
        compiler_params=_params(1),
    )(place, *ws, *extra)


def _gather_start(lands, axes, name, after):
    n_arr = len(lands)
    fulls = [tuple(l.shape) for l in lands]

    def body(*refs):
        land = refs[:n_arr]
        send_sems, recv_sems = refs[n_arr + 1:n_arr + 3]
        token = refs[-1]
        x, y, c, chips = _place()
        k_me = 2 * x + y
        for a in range(n_arr):
            mine = _shard_region(land[a], fulls[a], axes[a], k_me, c)
            for j, chip in enumerate(chips):
                pltpu.make_async_remote_copy(
                    src_ref=mine, dst_ref=mine, send_sem=send_sems.at[a * 3 + j], recv_sem=recv_sems.at[a * 3 + j],
                    device_id=(*chip, c), device_id_type=MESH).start()
        token[...] = jnp.zeros_like(token)

    res = pl.pallas_call(
        body, name=name,
        out_shape=(pltpu.SemaphoreType.DMA((3 * n_arr,)), pltpu.SemaphoreType.DMA((3 * n_arr,)),
                   *[pltpu.HBM(f, BF16) for f in fulls], jax.ShapeDtypeStruct((8, 128), F32)),
        in_specs=[HBM_SPEC] * n_arr + [ANY_SPEC],
        out_specs=(SEM_SPEC, SEM_SPEC, *[HBM_SPEC] * n_arr, pl.BlockSpec(memory_space=pltpu.VMEM)),
        input_output_aliases={a: 2 + a for a in range(n_arr)},
        compiler_params=SPLIT_COPY_PARAMS,
    )(*_in_hbm(lands), after)
    return res[0], res[1], list(res[2:2 + n_arr]), res[-1]


def _gather_wait(send_sems, recv_sems, lands, axes, name, after):
    n_arr = len(lands)
    fulls = [tuple(l.shape) for l in lands]

    def body(*refs):
        land = refs[:n_arr]
        s_sems, r_sems = refs[n_arr:n_arr + 2]
        x, y, c, chips = _place()
        for a in range(n_arr):
            for j, chip in enumerate(chips):
                k_j = 2 * chip[0] + chip[1]
                got = _shard_region(land[a], fulls[a], axes[a], k_j, c)
                cp = pltpu.make_async_remote_copy(
                    src_ref=got, dst_ref=got, send_sem=s_sems.at[a * 3 + j], recv_sem=r_sems.at[a * 3 + j],
                    device_id=(x, y, c), device_id_type=MESH)
                cp.wait_send()
                cp.wait_recv()

    res = pl.pallas_call(
        body, name=name,
        out_shape=tuple(pltpu.HBM(f, BF16) for f in fulls),
        in_specs=[HBM_SPEC] * n_arr + [SEM_SPEC, SEM_SPEC] + [ANY_SPEC] * len(after),
        out_specs=tuple([HBM_SPEC] * n_arr),
        input_output_aliases={a: a for a in range(n_arr)},
        compiler_params=SPLIT_COPY_PARAMS,
    )(*lands, send_sems, recv_sems, *after)
    return list(res)


def _gather_share(lands, axes, name):
    n_arr = len(lands)
    fulls = [tuple(l.shape) for l in lands]

    def body(*refs):
        land_in, land = refs[:n_arr], refs[n_arr:2 * n_arr]
        send_sems, recv_sems = refs[2 * n_arr:]
        x, y, c, chips = _place()
        copies = []
        for a in range(n_arr):
            for j, k_j in enumerate([2 * chip[0] + chip[1] for chip in chips] + [2 * x + y]):
                cp = pltpu.make_async_remote_copy(
                    src_ref=_shard_region(land_in[a], fulls[a], axes[a], k_j, c),
                    dst_ref=_shard_region(land[a], fulls[a], axes[a], k_j, c),
                    send_sem=send_sems.at[a * 4 + j], recv_sem=recv_sems.at[a * 4 + j],
                    device_id=(x, y, 1 - c), device_id_type=MESH)
                cp.start()
                copies.append(cp)
        for cp in copies:
            cp.wait()

    return pl.pallas_call(
        body, name=name, out_shape=[jax.ShapeDtypeStruct(f, BF16) for f in fulls],
        in_specs=[ANY_SPEC] * n_arr, out_specs=[ANY_SPEC] * n_arr,
        input_output_aliases={a: a for a in range(n_arr)},
        scratch_shapes=[pltpu.SemaphoreType.DMA((4 * n_arr,)), pltpu.SemaphoreType.DMA((4 * n_arr,))],
    )(*lands)


def _scatter_shapes(grads, axes):
    out = []
    for g, ax in zip(grads, axes):
        shp = list(g.shape)
        shp[ax] //= 4
        out.append((3,) + tuple(shp[1:]))
    return out


def _scatter_start(grads, axes, name, after):
    n_arr = len(grads)
    shapes = _scatter_shapes(grads, axes)
    lands = [lax.empty(s, BF16) for s in shapes]

    def body(*refs):
        ins, land = refs[:n_arr], refs[n_arr:2 * n_arr]
        send_sems, recv_sems = refs[2 * n_arr + 1:2 * n_arr + 3]
        token = refs[-1]
        x, y, c, chips = _place()
        for a in range(n_arr):
            for j, chip in enumerate(chips):
                k_j = 2 * chip[0] + chip[1]
                pltpu.make_async_remote_copy(
                    src_ref=_shard_region(ins[a], grads[a].shape, axes[a], k_j), dst_ref=land[a].at[pl.ds(j, 1)],
                    send_sem=send_sems.at[a * 3 + j], recv_sem=recv_sems.at[a * 3 + j],
                    device_id=(*chip, c), device_id_type=MESH).start()
        token[...] = jnp.zeros_like(token)

    res = pl.pallas_call(
        body, name=name,
        out_shape=(pltpu.SemaphoreType.DMA((3 * n_arr,)), pltpu.SemaphoreType.DMA((3 * n_arr,)),
                   *[pltpu.HBM(g.shape, BF16) for g in grads], *[pltpu.HBM(s, BF16) for s in shapes],
                   jax.ShapeDtypeStruct((8, 128), F32)),
        in_specs=[HBM_SPEC] * (2 * n_arr) + [ANY_SPEC],
        out_specs=(SEM_SPEC, SEM_SPEC, *[HBM_SPEC] * (2 * n_arr), pl.BlockSpec(memory_space=pltpu.VMEM)),
        input_output_aliases={a: 2 + a for a in range(2 * n_arr)},
        compiler_params=SPLIT_COPY_PARAMS,
    )(*_in_hbm(grads), *_in_hbm(lands), after)
    return res[0], res[1], list(res[2:2 + n_arr]), list(res[2 + n_arr:2 + 2 * n_arr]), res[-1]


def _scatter_wait(send_sems, recv_sems, grads, lands, axes, name, after):
    n_arr = len(grads)

    def body(*refs):
        ins, land = refs[:n_arr], refs[n_arr:2 * n_arr]
        s_sems, r_sems = refs[2 * n_arr:2 * n_arr + 2]
        x, y, c, chips = _place()
        for a in range(n_arr):
            for j, chip in enumerate(chips):
                k_j = 2 * chip[0] + chip[1]
                cp = pltpu.make_async_remote_copy(
                    src_ref=_shard_region(ins[a], grads[a].shape, axes[a], k_j), dst_ref=land[a].at[pl.ds(j, 1)],
                    send_sem=s_sems.at[a * 3 + j], recv_sem=r_sems.at[a * 3 + j],
                    device_id=(x, y, c), device_id_type=MESH)
                cp.wait_send()
                cp.wait_recv()

    res = pl.pallas_call(
        body, name=name,
        out_shape=(*[pltpu.HBM(g.shape, BF16) for g in grads], *[pltpu.HBM(l.shape, BF16) for l in lands]),
        in_specs=[HBM_SPEC] * (2 * n_arr) + [SEM_SPEC, SEM_SPEC, ANY_SPEC],
        out_specs=tuple([HBM_SPEC] * (2 * n_arr)),
        input_output_aliases={a: a for a in range(2 * n_arr)},
        compiler_params=SPLIT_COPY_PARAMS,
    )(*grads, *lands, send_sems, recv_sems, after)
    return list(res[:n_arr]), list(res[n_arr:])


def _swap_start(arrs, name, after):
    n_arr = len(arrs)
    lands = [lax.empty(a.shape, a.dtype) for a in arrs]

    def body(*refs):
        ins, land = refs[:n_arr], refs[n_arr:2 * n_arr]
        send_sems, recv_sems = refs[2 * n_arr + 1:2 * n_arr + 3]
        token = refs[-1]
        x, y, c, _ = _place()
        for a in range(n_arr):
            pltpu.make_async_remote_copy(
                src_ref=ins[a], dst_ref=land[a], send_sem=send_sems.at[a], recv_sem=recv_sems.at[a],
                device_id=(x, y, 1 - c), device_id_type=MESH).start()
        token[...] = jnp.zeros_like(token)

    res = pl.pallas_call(
        body, name=name,
        out_shape=(pltpu.SemaphoreType.DMA((n_arr,)), pltpu.SemaphoreType.DMA((n_arr,)),
                   *[pltpu.HBM(a.shape, a.dtype) for a in arrs], *[pltpu.HBM(a.shape, a.dtype) for a in arrs],
                   jax.ShapeDtypeStruct((8, 128), F32)),
        in_specs=[HBM_SPEC] * (2 * n_arr) + [ANY_SPEC],
        out_specs=(SEM_SPEC, SEM_SPEC, *[HBM_SPEC] * (2 * n_arr), pl.BlockSpec(memory_space=pltpu.VMEM)),
        input_output_aliases={a: 2 + a for a in range(2 * n_arr)},
        compiler_params=SPLIT_COPY_PARAMS,
    )(*_in_hbm(arrs), *_in_hbm(lands), after)
    return res[0], res[1], list(res[2:2 + n_arr]), list(res[2 + n_arr:2 + 2 * n_arr]), res[-1]


def _swap_wait(send_sems, recv_sems, arrs, lands, name, after):
    n_arr = len(arrs)

    def body(*refs):
        ins, land = refs[:n_arr], refs[n_arr:2 * n_arr]
        s_sems, r_sems = refs[2 * n_arr:2 * n_arr + 2]
        x, y, c, _ = _place()
        for a in range(n_arr):
            cp = pltpu.make_async_remote_copy(
                src_ref=ins[a], dst_ref=land[a], send_sem=s_sems.at[a], recv_sem=r_sems.at[a],
                device_id=(x, y, c), device_id_type=MESH)
            cp.wait_send()
            cp.wait_recv()

    res = pl.pallas_call(
        body, name=name,
        out_shape=(*[pltpu.HBM(a.shape, a.dtype) for a in arrs], *[pltpu.HBM(a.shape, a.dtype) for a in arrs]),
        in_specs=[HBM_SPEC] * (2 * n_arr) + [SEM_SPEC, SEM_SPEC, ANY_SPEC],
        out_specs=tuple([HBM_SPEC] * (2 * n_arr)),
        input_output_aliases={a: a for a in range(2 * n_arr)},
        compiler_params=SPLIT_COPY_PARAMS,
    )(*arrs, *lands, send_sems, recv_sems, after)
    return list(res[:n_arr]), list(res[n_arr:])


def _swap_with_sibling(arrs, name):
    n_arr = len(arrs)

    def body(*refs):
        ins, outs = refs[:n_arr], refs[n_arr:2 * n_arr]
        send_sems, recv_sems = refs[2 * n_arr:]
        x, y, c, _ = _place()
        copies = []
        for a in range(n_arr):
            cp = pltpu.make_async_remote_copy(
                src_ref=ins[a], dst_ref=outs[a], send_sem=send_sems.at[a], recv_sem=recv_sems.at[a],
                device_id=(x, y, 1 - c), device_id_type=MESH)
            cp.start()
            copies.append(cp)
        for cp in copies:
            cp.wait()

    any_spec = pl.BlockSpec(memory_space=pl.ANY)
    return pl.pallas_call(
        body, name=name, out_shape=[jax.ShapeDtypeStruct(a.shape, a.dtype) for a in arrs],
        in_specs=[any_spec] * n_arr, out_specs=[any_spec] * n_arr,
        scratch_shapes=[pltpu.SemaphoreType.DMA((n_arr,)), pltpu.SemaphoreType.DMA((n_arr,))],
    )(*arrs)


def _mm_tn(a, b, name, out_dtype=BF16):
    L, m = a.shape
    n = b.shape[1]
    bm, bn = min(m, 1024), min(n, 1024)

    def body(a_ref, b_ref, o_ref):
        o_ref[...] = _dot_tn(_bf(a_ref[...]), _bf(b_ref[...])).astype(out_dtype)

    return pl.pallas_call(
        body, name=name, grid=(m // bm, n // bn),
        in_specs=[pl.BlockSpec((L, bm), lambda i, j: (0, i)), pl.BlockSpec((L, bn), lambda i, j: (0, j))],
        out_specs=pl.BlockSpec((bm, bn), lambda i, j: (i, j)),
        out_shape=jax.ShapeDtypeStruct((m, n), out_dtype),
        compiler_params=_params(2),
    )(a, b)


def _mm_tn_blocks(a, b, wa, wb, name, after=None):
    L = a.shape[0]
    nb = a.shape[1] // wa
    bk = min(L, 2048)
    nk = L // bk

    def body(a_ref, b_ref, o_ref):
        @pl.when(pl.program_id(1) == 0)
        def _():
            o_ref[...] = jnp.zeros_like(o_ref)

        o_ref[...] += _dot_tn(_bf(a_ref[...]), _bf(b_ref[...]))

    call, tail = _call_after(
        after, body, 2, name=name, grid=(nb, nk),
        in_specs=[pl.BlockSpec((bk, wa), lambda j, k: (k, j)), pl.BlockSpec((bk, wb), lambda j, k: (k, j))],
        out_specs=pl.BlockSpec((None, wa, wb), lambda j, k: (j, 0, 0)),
        out_shape=jax.ShapeDtypeStruct((nb, wa, wb), F32),
        compiler_params=_params(2),
    )
    return call(a, b, *tail)


def _sum_parts(parts, owns, axes, chip, name):
    n_arr = len(parts)
    steps = 4
    in_specs, out_specs, shapes = [], [], []
    for part, axis in zip(parts, axes):
        _, r, c = part.shape
        tr = r // steps
        shapes.append((r, c))
        in_specs.append(pl.BlockSpec((3, tr, c), lambda i, k: (0, i, 0)))
        out_specs.append(pl.BlockSpec((tr, c), lambda i, k: (i, 0)))
    for part, axis in zip(parts, axes):
        _, r, c = part.shape
        tr = r // steps
        if axis == 1:
            in_specs.append(pl.BlockSpec((None, tr, c), lambda i, k: (0, steps * k[0] + i, 0)))
        else:
            in_specs.append(pl.BlockSpec((None, tr, c), lambda i, k: (0, i, k[0])))

    def body(k_ref, *refs):
        for a in range(n_arr):
            p = refs[a][...].astype(F32)
            refs[2 * n_arr + a][...] = ((p[0] + p[1]) + p[2]) + refs[n_arr + a][...].astype(F32)

    return pl.pallas_call(
        body, name=name,
        grid_spec=pltpu.PrefetchScalarGridSpec(num_scalar_prefetch=1, grid=(steps,), in_specs=in_specs, out_specs=out_specs),
        out_shape=[jax.ShapeDtypeStruct(sh, F32) for sh in shapes],
        compiler_params=_params(1),
    )(chip, *parts, *owns)


def _adamw(w, g_parts, m, v, name):
    n_g = len(g_parts)
    if w.ndim == 2:
        r, c = w.shape
        tr = r
        for cand in (512, 256, 128, 64, 32, 16, 8):
            if r % cand == 0 and cand * c * 4 <= (2 << 20):
                tr = cand
                break
        spec = pl.BlockSpec((tr, c), lambda i: (i, 0))
        tiling = dict(grid=(r // tr,), in_specs=[spec] * (3 + n_g), out_specs=[spec] * 4, compiler_params=_params(1))
    else:
        tiling = dict(compiler_params=pltpu.CompilerParams(vmem_limit_bytes=48 << 20))

    def body(*refs):
        w_ref, g_refs, m_ref, v_ref = refs[0], refs[1:1 + n_g], refs[1 + n_g], refs[2 + n_g]
        g = g_refs[0][...]
        for gr in g_refs[1:]:
            g = g + gr[...]
        _adamw_update(g, w_ref, m_ref, v_ref, *refs[3 + n_g:])

    return pl.pallas_call(body, name=name, out_shape=[jax.ShapeDtypeStruct(w.shape, F32)] * 4, **tiling)(w, *g_parts, m, v)


def _adamw_update(g, w_ref, m_ref, v_ref, g_out, d_out, m_out, v_out):
    m_new = ADAM_B1 * m_ref[...] + (1.0 - ADAM_B1) * g
    v_new = ADAM_B2 * v_ref[...] + (1.0 - ADAM_B2) * (g * g)
    m_hat = m_new * (1.0 / (1.0 - ADAM_B1 ** ADAM_STEP))
    v_hat = v_new * (1.0 / (1.0 - ADAM_B2 ** ADAM_STEP))
    g_out[...] = g
    d_out[...] = -ADAM_LR * (m_hat / (jnp.sqrt(v_hat) + ADAM_EPS) + ADAM_WD * w_ref[...])
    m_out[...] = m_new
    v_out[...] = v_new


def _adamw_many(ws, gs, ms, vs, name):
    n = len(ws)

    def body(*refs):
        for k in range(n):
            _adamw_update(refs[n + k][...], refs[k], refs[2 * n + k], refs[3 * n + k],
                          refs[4 * n + k], refs[5 * n + k], refs[6 * n + k], refs[7 * n + k])

    outs = pl.pallas_call(body, name=name, out_shape=[jax.ShapeDtypeStruct(t.shape, F32) for t in ws] * 4,
                          compiler_params=pltpu.CompilerParams(vmem_limit_bytes=56 << 20))(*ws, *gs, *ms, *vs)
    return [outs[part * n:(part + 1) * n] for part in range(4)]


def _adamw_layers(w, q_mine, q_sib, m, v, name, after=None):
    n, r, c = w.shape
    tr = r
    for cand in (512, 256, 128, 64, 32, 16, 8):
        if r % cand == 0 and cand * c * 4 <= (1 << 20):
            tr = cand
            break

    def body(*refs):
        w_ref, qm, qs, m_ref, v_ref = refs[0], refs[1:1 + n], refs[1 + n:1 + 2 * n], refs[1 + 2 * n], refs[2 + 2 * n]
        layer = pl.program_id(0)
        g = qm[0][...] + qs[0][...]
        for k in range(1, n):
            g = jnp.where(layer == k, qm[k][...] + qs[k][...], g)
        _adamw_update(g, w_ref, m_ref, v_ref, *refs[3 + 2 * n:])

    stacked = pl.BlockSpec((None, tr, c), lambda l, i: (l, i, 0))
    per_layer = [pl.BlockSpec((tr, c), lambda l, i, k=k: (jnp.where(l == k, i, 0), 0)) for k in range(n)]
    call, tail = _call_after(
        after, body, 3 + 2 * n, name=name, grid=(n, r // tr),
        in_specs=[stacked] + per_layer + per_layer + [stacked, stacked], out_specs=[stacked] * 4,
        out_shape=[jax.ShapeDtypeStruct(w.shape, F32)] * 4,
        compiler_params=_params(2),
    )
    return call(w, *q_mine, *q_sib, m, v, *tail)


def _ada_fwd(c16, ada_w, ada_b_cols):
    cols = ada_w.shape[2]

    def body(c_ref, w_ref, b_ref, o_ref):
        cv = c_ref[...]
        ca = _bf(cv * jax.nn.sigmoid(cv))
        o_ref[...] = _dot(ca, _bf(w_ref[...])) + b_ref[...]

    return pl.pallas_call(
        body, name="ada_fwd", grid=(DEPTH,),
        in_specs=[_whole((16, D)), pl.BlockSpec((None, D, cols), lambda i: (i, 0, 0)),
                  pl.BlockSpec((None, 1, cols), lambda i: (i, 0, 0))],
        out_specs=pl.BlockSpec((None, 16, cols), lambda i: (i, 0, 0)),
        out_shape=jax.ShapeDtypeStruct((DEPTH, 16, cols), F32),
        compiler_params=_params(1),
    )(c16, ada_w, ada_b_cols)


def _ada_bwd(c16, dmod16):
    cols = dmod16.shape[2]

    def body(c_ref, d_ref, o_ref):
        cv = c_ref[...]
        ca = _bf(cv * jax.nn.sigmoid(cv))
        o_ref[...] = _dot_tn(ca, _bf(d_ref[...]))

    return pl.pallas_call(
        body, name="ada_bwd", grid=(DEPTH,),
        in_specs=[_whole((16, D)), pl.BlockSpec((None, 16, cols), lambda i: (i, 0, 0))],
        out_specs=pl.BlockSpec((None, D, cols), lambda i: (i, 0, 0)),
        out_shape=jax.ShapeDtypeStruct((DEPTH, D, cols), F32),
        compiler_params=_params(1),
    )(c16, dmod16)


def _mod_bwd(vs_mix, vs_ffn, pv):
    def body(m_ref, f_ref, pv_ref, o_ref):
        for i in range(DEPTH):
            vm, vf, p = m_ref[i], f_ref[i], pv_ref[i]
            o_ref[i] = jnp.concatenate([
                vm[2:3], vm[1:2] * p[R_N1:R_N1 + 1], vm[0:1],
                vf[2:3], vf[1:2] * p[R_N2:R_N2 + 1], vf[0:1],
                vm[1:2] * (1.0 + p[R_SC1:R_SC1 + 1]), vf[1:2] * (1.0 + p[R_SC2:R_SC2 + 1])], axis=0)

    return pl.pallas_call(body, name="mod_bwd", out_shape=jax.ShapeDtypeStruct((DEPTH, 8, D), F32))(vs_mix, vs_ffn, pv)


def _ffn_fwd(x1, pv, w1, w2, layer, tm, after=None):
    L = x1.shape[0]
    dff = w1.shape[2]

    def body(x1_ref, pv_ref, w1_ref, w2_ref, x2_ref, h2_ref, a_ref, f_ref):
        x1v, p = x1_ref[...], pv_ref[...]
        h2, _, _ = _norm_mod(x1v, p[R_N2:R_N2 + 1], p[R_SC2:R_SC2 + 1], p[R_SH2:R_SH2 + 1])
        hb = _bf(h2)
        h2_ref[...] = hb
        a = _dot(hb, w1_ref[...])
        a_ref[...] = a
        ra = jnp.maximum(a, 0.0)
        f = _dot(_bf(ra * ra), w2_ref[...])
        f_ref[...] = f
        x2_ref[...] = x1v + p[R_G2:R_G2 + 1] * f

    call, tail = _call_after(
        after, body, 4, name=f"ffn_fwd{layer}", grid=(L // tm,),
        in_specs=[_rows(tm, D), pl.BlockSpec((None, 8, D), lambda i: (layer, 0, 0)), _layer_w(D, dff, 0), _layer_w(dff, D, 0)],
        out_specs=[_rows(tm, D), _rows(tm, D), _rows(tm, dff), _rows(tm, D)],
        out_shape=[jax.ShapeDtypeStruct((L, D), F32), jax.ShapeDtypeStruct((L, D), BF16),
                   jax.ShapeDtypeStruct((L, dff), F32), jax.ShapeDtypeStruct((L, D), F32)],
        compiler_params=_params(1, 56),
    )
    return call(x1, pv, w1, w2, *tail)


def _ffn_bwd(dx2, x1, a, f, pv, w1, w2, layer, tm, after=None):
    L = x1.shape[0]
    dff = w1.shape[2]
    extra = [] if after is None else [pl.BlockSpec(memory_space=pl.ANY)]
    extra_args = [] if after is None else [after]

    def body(dx2_ref, x1_ref, a_ref, f_ref, pv_ref, w1_ref, w2_ref, *rest):
        dx1_ref, p_ref, da_ref, df_ref, vs_ref = rest[len(extra):]

        @pl.when(pl.program_id(0) == 0)
        def _():
            vs_ref[...] = jnp.zeros_like(vs_ref)

        dx2v, p = dx2_ref[...], pv_ref[...]
        dfb = _bf(dx2v * p[R_G2:R_G2 + 1])
        df_ref[...] = dfb
        vs_ref[0:1, :] += _sum0(dx2v * f_ref[...])
        dp = _dot_nt(dfb, w2_ref[...])
        ra = jnp.maximum(a_ref[...], 0.0)
        p_ref[...] = _bf(ra * ra)
        dab = _bf(dp * (2.0 * ra))
        da_ref[...] = dab
        dh2 = _dot_nt(dab, w1_ref[...])
        _, xn, r = _norm_mod(x1_ref[...], p[R_N2:R_N2 + 1], p[R_SC2:R_SC2 + 1], p[R_SH2:R_SH2 + 1])
        dx1_ref[...] = dx2v + _norm_mod_bwd(dh2, xn, r, p[R_N2:R_N2 + 1], p[R_SC2:R_SC2 + 1])
        vs_ref[1:2, :] += _sum0(dh2 * xn)
        vs_ref[2:3, :] += _sum0(dh2)

    return pl.pallas_call(
        body, name=f"ffn_bwd{layer}", grid=(L // tm,),
        in_specs=[_rows(tm, D), _rows(tm, D), _rows(tm, dff), _rows(tm, D),
                  pl.BlockSpec((None, 8, D), lambda i: (layer, 0, 0)), _layer_w(D, dff, 0), _layer_w(dff, D, 0)] + extra,
        out_specs=[_rows(tm, D), _rows(tm, dff), _rows(tm, dff), _rows(tm, D), _whole((8, D))],
        out_shape=[jax.ShapeDtypeStruct((L, D), F32), jax.ShapeDtypeStruct((L, dff), BF16),
                   jax.ShapeDtypeStruct((L, dff), BF16), jax.ShapeDtypeStruct((L, D), BF16),
                   jax.ShapeDtypeStruct((8, D), F32)],
        compiler_params=_params(1, 56),
    )(dx2, x1, a, f, pv, w1, w2, *extra_args)


def _conv_fwd(x, pv, w_in, w_out, cw, layer, j, tm, after=None):
    L = x.shape[0]

    def body(x_ref, pv_ref, win_ref, wout_ref, cw_ref, x1_ref, h_ref, bcx_ref, conv_ref, q_ref, y_ref, carry):
        @pl.when(pl.program_id(0) == 0)
        def _():
            carry[...] = jnp.zeros_like(carry)

        xv, p, cwv = x_ref[...], pv_ref[...], cw_ref[...]
        h, _, _ = _norm_mod(xv, p[R_N1:R_N1 + 1], p[R_SC1:R_SC1 + 1], p[R_SH1:R_SH1 + 1])
        hb = _bf(h)
        h_ref[...] = hb
        bcx = _dot(hb, win_ref[...])
        bcx_ref[...] = bcx
        z = bcx[:, D:2 * D] * bcx[:, 2 * D:]
        prev8 = carry[...]
        conv = cwv[0:1] * _shift_down(z, prev8, 2) + cwv[1:2] * _shift_down(z, prev8, 1) + cwv[2:3] * z + cwv[3:4]
        conv_ref[...] = conv
        qb = _bf(bcx[:, :D] * conv)
        q_ref[...] = qb
        y = _dot(qb, wout_ref[...])
        y_ref[...] = y
        x1_ref[...] = xv + p[R_G1:R_G1 + 1] * y
        carry[...] = z[tm - 8:tm]

    call, tail = _call_after(
        after, body, 5, name=f"conv_fwd{layer}", grid=(L // tm,),
        in_specs=[_rows(tm, D), pl.BlockSpec((None, 8, D), lambda i: (layer, 0, 0)), _layer_w(D, 3 * D, 0), _layer_w(D, D, 0),
                  pl.BlockSpec((None, 8, D), lambda i: (j, 0, 0))],
        out_specs=[_rows(tm, D), _rows(tm, D), _rows(tm, 3 * D), _rows(tm, D), _rows(tm, D), _rows(tm, D)],
        out_shape=[jax.ShapeDtypeStruct((L, D), F32), jax.ShapeDtypeStruct((L, D), BF16), jax.ShapeDtypeStruct((L, 3 * D), F32),
                   jax.ShapeDtypeStruct((L, D), F32), jax.ShapeDtypeStruct((L, D), BF16), jax.ShapeDtypeStruct((L, D), F32)],
        scratch_shapes=[pltpu.VMEM((8, D), F32)],
        compiler_params=_params(1, 56),
    )
    return call(x, pv, w_in, w_out, cw, *tail)


def _conv_bwd(dx1, x, y, bcx, conv, pv, w_in, w_out, cw, layer, j, tm, after=None):
    L = x.shape[0]
    nt = L // tm

    def body(dx1_ref, x_ref, y_ref, bcx_ref, conv_ref, halo_ref, pv_ref, win_ref, wout_ref, cw_ref,
             dx_ref, dbcx_ref, dy_ref, vs_ref, carry):
        gi = pl.program_id(0)
        tile = nt - 1 - gi

        @pl.when(gi == 0)
        def _():
            vs_ref[...] = jnp.zeros_like(vs_ref)
            carry[...] = jnp.zeros_like(carry)

        dx1v, p, cwv = dx1_ref[...], pv_ref[...], cw_ref[...]
        dyb = _bf(dx1v * p[R_G1:R_G1 + 1])
        dy_ref[...] = dyb
        vs_ref[0:1, :] += _sum0(dx1v * y_ref[...])
        dq = _dot_nt(dyb, wout_ref[...])
        bcx = bcx_ref[...]
        b, cg, xh = bcx[:, :D], bcx[:, D:2 * D], bcx[:, 2 * D:]
        db = dq * conv_ref[...]
        dc = dq * b
        z = cg * xh
        halo = halo_ref[...]
        zprev = jnp.where(tile > 0, halo[:, D:2 * D] * halo[:, 2 * D:], 0.0)
        vs_ref[3:4, :] += _sum0(dc * _shift_down(z, zprev, 2))
        vs_ref[4:5, :] += _sum0(dc * _shift_down(z, zprev, 1))
        vs_ref[5:6, :] += _sum0(dc * z)
        vs_ref[6:7, :] += _sum0(dc)
        next8 = carry[...]
        dz = cwv[2:3] * dc + cwv[1:2] * _shift_up(dc, next8, 1) + cwv[0:1] * _shift_up(dc, next8, 2)
        dbb, dcgb, dxhb = _bf(db), _bf(dz * xh), _bf(dz * cg)
        dbcx_ref[:, 0:D] = dbb
        dbcx_ref[:, D:2 * D] = dcgb
        dbcx_ref[:, 2 * D:3 * D] = dxhb
        dh = (_dot_nt(dbb, win_ref[:, 0:D]) + _dot_nt(dcgb, win_ref[:, D:2 * D])) + _dot_nt(dxhb, win_ref[:, 2 * D:3 * D])
        _, xn, r = _norm_mod(x_ref[...], p[R_N1:R_N1 + 1], p[R_SC1:R_SC1 + 1], p[R_SH1:R_SH1 + 1])
        dx_ref[...] = dx1v + _norm_mod_bwd(dh, xn, r, p[R_N1:R_N1 + 1], p[R_SC1:R_SC1 + 1])
        vs_ref[1:2, :] += _sum0(dh * xn)
        vs_ref[2:3, :] += _sum0(dh)
        carry[...] = dc[0:8]

    halo_spec = pl.BlockSpec((8, 3 * D), lambda i: (jnp.maximum((nt - 1 - i) * (tm // 8) - 1, 0), 0))
    call, tail = _call_after(
        after, body, 10, name=f"conv_bwd{layer}", grid=(nt,),
        in_specs=[_rows(tm, D, nt), _rows(tm, D, nt), _rows(tm, D, nt), _rows(tm, 3 * D, nt), _rows(tm, D, nt), halo_spec,
                  pl.BlockSpec((None, 8, D), lambda i: (layer, 0, 0)), _layer_w(D, 3 * D, 0), _layer_w(D, D, 0),
                  pl.BlockSpec((None, 8, D), lambda i: (j, 0, 0))],
        out_specs=[_rows(tm, D, nt), _rows(tm, 3 * D, nt), _rows(tm, D, nt), _whole((8, D))],
        out_shape=[jax.ShapeDtypeStruct((L, D), F32), jax.ShapeDtypeStruct((L, 3 * D), BF16),
                   jax.ShapeDtypeStruct((L, D), BF16), jax.ShapeDtypeStruct((8, D), F32)],
        scratch_shapes=[pltpu.VMEM((8, D), F32)],
        compiler_params=_params(1, 56),
    )
    return call(dx1, x, y, bcx, conv, bcx, pv, w_in, w_out, cw, *tail)


def _s5_discretize(a_re, a_im, log_dt, bt_re, bt_im):
    dt = jnp.exp(log_dt)
    mag = jnp.exp(a_re * dt)
    abar_re = mag * jnp.cos(a_im * dt)
    abar_im = mag * jnp.sin(a_im * dt)
    den = a_re * a_re + a_im * a_im
    nr = abar_re - 1.0
    ni = abar_im
    f_re = (nr * a_re + ni * a_im) / den
    f_im = (ni * a_re - nr * a_im) / den
    bbar_re = f_re * bt_re - f_im * bt_im
    bbar_im = f_re * bt_im + f_im * bt_re
    return abar_re, abar_im, bbar_re, bbar_im


def _s5_params_fwd(a_re, a_im, log_dt, bt_re, bt_im, after=None):
    def body(ar, ai, ld, br, bi, o_ar, o_ai, o_br, o_bi):
        r = _s5_discretize(ar[...], ai[...], ld[...], br[...], bi[...])
        o_ar[...], o_ai[...], o_br[...], o_bi[...] = r

    gp = jax.ShapeDtypeStruct((S5_G, S5_P), F32)
    hgp = jax.ShapeDtypeStruct((S5_H, S5_G, S5_P), F32)
    call, tail = _call_after(after, body, 5, name="s5_params_fwd", out_shape=[gp, gp, hgp, hgp],
                             in_specs=[pl.BlockSpec(memory_space=pltpu.VMEM)] * 5)
    return call(a_re, a_im, log_dt, bt_re, bt_im, *tail)


def _s5_params_bwd(a_re, a_im, log_dt, bt_re, bt_im, d_ar, d_ai, d_br, d_bi):
    def body(ar, ai, ld, br, bi, gar, gai, gbr, gbi, o_ar, o_ai, o_ld, o_br, o_bi):
        _, vjp = jax.vjp(_s5_discretize, ar[...], ai[...], ld[...], br[...], bi[...])
        r = vjp((gar[...], gai[...], gbr[...], gbi[...]))
        o_ar[...], o_ai[...], o_ld[...], o_br[...], o_bi[...] = r

    gp = jax.ShapeDtypeStruct((S5_G, S5_P), F32)
    hgp = jax.ShapeDtypeStruct((S5_H, S5_G, S5_P), F32)
    return pl.pallas_call(body, name="s5_params_bwd", out_shape=[gp, gp, jax.ShapeDtypeStruct((S5_G, 1), F32), hgp, hgp])(
        a_re, a_im, log_dt, bt_re, bt_im, d_ar, d_ai, d_br, d_bi)


NSEG = 8
SCAN_LANES = 1024


def _to_segments(x):
    n, c = x.shape
    return x.reshape(NSEG, n // NSEG, c).transpose(1, 0, 2).reshape(n, c)


def _from_segments(x):
    n, c = x.shape
    return x.reshape(n // NSEG, NSEG, c).transpose(1, 0, 2).reshape(n, c)


def _segment_scan(re_ref, im_ref, st_re, st_im, a_re, a_im, n_slabs, adjoint, write):
    for q in range(NSTATE // SCAN_LANES):
        ls = slice(q * SCAN_LANES, (q + 1) * SCAN_LANES)
        ar = jnp.broadcast_to(a_re[:, ls], (8, SCAN_LANES))
        ai = jnp.broadcast_to(a_im[:, ls], (8, SCAN_LANES))

        def step(k, carry, ls=ls, ar=ar, ai=ai):
            s_r, s_i = carry
            slab = (n_slabs - 1 - k) if adjoint else k
            rows = pl.ds(pl.multiple_of(slab * 8, 8), 8)
            b_r, b_i = re_ref[rows, ls], im_ref[rows, ls]
            if adjoint:
                n_r = b_r + ar * s_r + ai * s_i
                n_i = b_i - ai * s_r + ar * s_i
            else:
                n_r = ar * s_r - ai * s_i + b_r
                n_i = ar * s_i + ai * s_r + b_i
            if write:
                re_ref[rows, ls] = n_r
                im_ref[rows, ls] = n_i
            return n_r, n_i

        s_r, s_i = lax.fori_loop(0, n_slabs, step, (st_re[:, ls], st_im[:, ls]), unroll=4)
        st_re[:, ls] = s_r
        st_im[:, ls] = s_i


def _s5_segment_states(e_re, e_im, ar, ai, seg_len, adjoint):
    def body(ere_ref, eim_ref, ar_ref, ai_ref, ore_ref, oim_ref):
        p_r, p_i = ar_ref[...], ai_ref[...]
        if adjoint:
            p_i = -p_i
        acc_r, acc_i = jnp.ones_like(p_r), jnp.zeros_like(p_r)
        n = seg_len
        while n:
            if n & 1:
                acc_r, acc_i = acc_r * p_r - acc_i * p_i, acc_r * p_i + acc_i * p_r
            n >>= 1
            if n:
                p_r, p_i = p_r * p_r - p_i * p_i, 2.0 * p_r * p_i
        e_r, e_i = ere_ref[...], eim_ref[...]
        s_r, s_i = jnp.zeros_like(acc_r), jnp.zeros_like(acc_r)
        rows_r, rows_i = [None] * NSEG, [None] * NSEG
        order = range(NSEG - 1, -1, -1) if adjoint else range(NSEG)
        for j in order:
            rows_r[j], rows_i[j] = s_r, s_i
            s_r, s_i = (acc_r * s_r - acc_i * s_i + e_r[j:j + 1], acc_r * s_i + acc_i * s_r + e_i[j:j + 1])
        ore_ref[...] = jnp.concatenate(rows_r, axis=0)
        oim_ref[...] = jnp.concatenate(rows_i, axis=0)

    st = jax.ShapeDtypeStruct((NSEG, NSTATE), F32)
    return pl.pallas_call(body, name="s5_segment_states_bwd" if adjoint else "s5_segment_states_fwd", out_shape=[st, st])(
        e_re, e_im, ar, ai)


def _s5_fwd_ends(x, pv, w_in, b_re, b_im, ar, ai, layer, tm, after=None):
    L = x.shape[0]

    def body(x_ref, pv_ref, win_ref, bre_ref, bim_ref, ar_ref, ai_ref, h_ref, u_ref, ere_ref, eim_ref, bu_re, bu_im):
        @pl.when(pl.program_id(0) == 0)
        def _():
            ere_ref[...] = jnp.zeros_like(ere_ref)
            eim_ref[...] = jnp.zeros_like(eim_ref)

        p = pv_ref[...]
        h, _, _ = _norm_mod(x_ref[...], p[R_N1:R_N1 + 1], p[R_SC1:R_SC1 + 1], p[R_SH1:R_SH1 + 1])
        hb = _bf(h)
        h_ref[...] = hb
        u = _dot(hb, win_ref[...])
        u_ref[...] = u
        ub = _bf(u)
        for k in range(S5_NB):
            uk = ub[:, k * S5_BH:(k + 1) * S5_BH]
            bu_re[:, k * S5_BP:(k + 1) * S5_BP] = _dot(uk, bre_ref[k])
            bu_im[:, k * S5_BP:(k + 1) * S5_BP] = _dot(uk, bim_ref[k])
        _segment_scan(bu_re, bu_im, ere_ref, eim_ref, ar_ref[...], ai_ref[...], tm // 8, adjoint=False, write=False)

    call, tail = _call_after(
        after, body, 7, name="s5_fwd_ends", grid=(L // tm,),
        in_specs=[_rows(tm, D), pl.BlockSpec((None, 8, D), lambda i: (layer, 0, 0)), _layer_w(D, D, 0),
                  _const_w((S5_NB, S5_BH, S5_BP)), _const_w((S5_NB, S5_BH, S5_BP)), _whole((1, NSTATE)), _whole((1, NSTATE))],
        out_specs=[_rows(tm, D), _rows(tm, D), _whole((NSEG, NSTATE)), _whole((NSEG, NSTATE))],
        out_shape=[jax.ShapeDtypeStruct((L, D), BF16), jax.ShapeDtypeStruct((L, D), F32),
                   jax.ShapeDtypeStruct((NSEG, NSTATE), F32), jax.ShapeDtypeStruct((NSEG, NSTATE), F32)],
        scratch_shapes=[pltpu.VMEM((tm, NSTATE), F32), pltpu.VMEM((tm, NSTATE), F32)],
        compiler_params=_params(1, 56),
    )
    return call(x, pv, w_in, b_re, b_im, ar, ai, *tail)


def _s5_fwd_out(x, u, pv, b_re, b_im, s0_re, s0_im, ar, ai, c_re, c_im, dvec, glu_w, glu_b, w_out, layer, tm):
    L = x.shape[0]

    def body(x_ref, u_ref, pv_ref, bre_ref, bim_ref, s0re_ref, s0im_ref, ar_ref, ai_ref, cre_ref, cim_ref, d_ref, gw_ref,
             gb_ref, wout_ref, x1_ref, sre_ref, sim_ref, y1_ref, zg_ref, y3_ref, y_ref, st_re, st_im):
        @pl.when(pl.program_id(0) == 0)
        def _():
            st_re[...] = s0re_ref[...]
            st_im[...] = s0im_ref[...]

        p = pv_ref[...]
        uv = u_ref[...]
        ub = _bf(uv)
        for k in range(S5_NB):
            uk = ub[:, k * S5_BH:(k + 1) * S5_BH]
            sre_ref[:, k * S5_BP:(k + 1) * S5_BP] = _dot(uk, bre_ref[k])
            sim_ref[:, k * S5_BP:(k + 1) * S5_BP] = _dot(uk, bim_ref[k])
        _segment_scan(sre_ref, sim_ref, st_re, st_im, ar_ref[...], ai_ref[...], tm // 8, adjoint=False, write=True)
        parts = []
        for k in range(S5_NB):
            sl = slice(k * S5_BP, (k + 1) * S5_BP)
            parts.append(_dot(_bf(sre_ref[:, sl]), cre_ref[k]) - _dot(_bf(sim_ref[:, sl]), cim_ref[k]))
        y1 = jnp.concatenate(parts, axis=1) + d_ref[...] * uv
        y1_ref[...] = y1
        y2 = jax.nn.gelu(y1)
        zg = _dot(_bf(y2), gw_ref[...]) + gb_ref[...]
        zg_ref[...] = zg
        y3b = _bf(y2 * jax.nn.sigmoid(zg))
        y3_ref[...] = y3b
        y = _dot(y3b, wout_ref[...])
        y_ref[...] = y
        x1_ref[...] = x_ref[...] + p[R_G1:R_G1 + 1] * y

    return pl.pallas_call(
        body, name="s5_fwd_out", grid=(L // tm,),
        in_specs=[_rows(tm, D), _rows(tm, D), pl.BlockSpec((None, 8, D), lambda i: (layer, 0, 0)),
                  _const_w((S5_NB, S5_BH, S5_BP)), _const_w((S5_NB, S5_BH, S5_BP)),
                  _whole((NSEG, NSTATE)), _whole((NSEG, NSTATE)), _whole((1, NSTATE)), _whole((1, NSTATE)),
                  _const_w((S5_NB, S5_BP, S5_BH)), _const_w((S5_NB, S5_BP, S5_BH)), _whole((1, D)),
                  _layer_w(D, D, 0), _whole((1, D)), _layer_w(D, D, 0)],
        out_specs=[_rows(tm, D), _rows(tm, NSTATE), _rows(tm, NSTATE), _rows(tm, D), _rows(tm, D), _rows(tm, D), _rows(tm, D)],
        out_shape=[jax.ShapeDtypeStruct((L, D), F32), jax.ShapeDtypeStruct((L, NSTATE), F32), jax.ShapeDtypeStruct((L, NSTATE), F32),
                   jax.ShapeDtypeStruct((L, D), F32), jax.ShapeDtypeStruct((L, D), F32),
                   jax.ShapeDtypeStruct((L, D), BF16), jax.ShapeDtypeStruct((L, D), F32)],
        scratch_shapes=[pltpu.VMEM((NSEG, NSTATE), F32), pltpu.VMEM((NSEG, NSTATE), F32)],
        compiler_params=_params(1, 56),
    )(x, u, pv, b_re, b_im, s0_re, s0_im, ar, ai, c_re, c_im, dvec, glu_w, glu_b, w_out)


def _s5_bwd_ends(dx1, y, y1, zg, u, pv, c_re, c_im, ar, ai, dvec, glu_w, w_out, layer, tm, after=None):
    L = dx1.shape[0]
    nt = L // tm

    def body(dx1_ref, y_ref, y1_ref, zg_ref, u_ref, pv_ref, cre_ref, cim_ref, ar_ref, ai_ref, d_ref, gw_ref, wout_ref,
             dy_ref, y2_ref, dzg_ref, dy1_ref, dus_ref, ere_ref, eim_ref, vs_ref, g_re, g_im):
        @pl.when(pl.program_id(0) == 0)
        def _():
            vs_ref[...] = jnp.zeros_like(vs_ref)
            ere_ref[...] = jnp.zeros_like(ere_ref)
            eim_ref[...] = jnp.zeros_like(eim_ref)

        dx1v, p = dx1_ref[...], pv_ref[...]
        dyb = _bf(dx1v * p[R_G1:R_G1 + 1])
        dy_ref[...] = dyb
        vs_ref[0:1, :] += _sum0(dx1v * y_ref[...])
        dy3 = _dot_nt(dyb, wout_ref[...])
        y2, gelu_vjp = jax.vjp(jax.nn.gelu, y1_ref[...])
        y2_ref[...] = _bf(y2)
        gate = jax.nn.sigmoid(zg_ref[...])
        dzg = dy3 * y2 * gate * (1.0 - gate)
        dzgb = _bf(dzg)
        dzg_ref[...] = dzgb
        vs_ref[1:2, :] += _sum0(dzg)
        dy2 = dy3 * gate + _dot_nt(dzgb, gw_ref[...])
        dy1 = gelu_vjp(dy2)[0]
        vs_ref[2:3, :] += _sum0(dy1 * u_ref[...])
        dus_ref[...] = dy1 * d_ref[...]
        dy1b = _bf(dy1)
        dy1_ref[...] = dy1b
        for k in range(S5_NB):
            dk = dy1b[:, k * S5_BH:(k + 1) * S5_BH]
            g_re[:, k * S5_BP:(k + 1) * S5_BP] = _dot_nt(dk, cre_ref[k])
            g_im[:, k * S5_BP:(k + 1) * S5_BP] = -_dot_nt(dk, cim_ref[k])
        _segment_scan(g_re, g_im, ere_ref, eim_ref, ar_ref[...], ai_ref[...], tm // 8, adjoint=True, write=False)

    call, tail = _call_after(
        after, body, 13, name="s5_bwd_ends", grid=(nt,),
        in_specs=[_rows(tm, D, nt)] * 5 + [pl.BlockSpec((None, 8, D), lambda i: (layer, 0, 0)),
                  _const_w((S5_NB, S5_BP, S5_BH)), _const_w((S5_NB, S5_BP, S5_BH)), _whole((1, NSTATE)), _whole((1, NSTATE)),
                  _whole((1, D)), _layer_w(D, D, 0), _layer_w(D, D, 0)],
        out_specs=[_rows(tm, D, nt)] * 5 + [_whole((NSEG, NSTATE)), _whole((NSEG, NSTATE)), _whole((8, D))],
        out_shape=[jax.ShapeDtypeStruct((L, D), BF16)] * 4 + [jax.ShapeDtypeStruct((L, D), F32),
                   jax.ShapeDtypeStruct((NSEG, NSTATE), F32), jax.ShapeDtypeStruct((NSEG, NSTATE), F32),
                   jax.ShapeDtypeStruct((8, D), F32)],
        scratch_shapes=[pltpu.VMEM((tm, NSTATE), F32), pltpu.VMEM((tm, NSTATE), F32)],
        compiler_params=_params(1, 56),
    )
    return call(dx1, y, y1, zg, u, pv, c_re, c_im, ar, ai, dvec, glu_w, w_out, *tail)


def _s5_bwd_in(dx1, dy1_b, du_skip, x, s_re, s_im, pv, b_re, b_im, c_re, c_im, l0_re, l0_im, ar, ai, w_in, layer, tm):
    L = x.shape[0]
    nt = L // tm

    def body(dx1_ref, dy1_ref, dus_ref, x_ref, sre_ref, sim_ref, hre_ref, him_ref, lre_ref, lim_ref, pv_ref, bre_ref, bim_ref,
             cre_ref, cim_ref, l0re_ref, l0im_ref, ar_ref, ai_ref, win_ref,
             dx_ref, du_ref, lamre_ref, lamim_ref, da_ref, vs_ref, g_re, g_im, st_re, st_im):
        gi = pl.program_id(0)
        tile = nt - 1 - gi

        @pl.when(gi == 0)
        def _():
            vs_ref[...] = jnp.zeros_like(vs_ref)
            da_ref[...] = jnp.zeros_like(da_ref)
            st_re[...] = l0re_ref[...]
            st_im[...] = l0im_ref[...]

        p = pv_ref[...]
        dy1b = dy1_ref[...]
        for k in range(S5_NB):
            dk = dy1b[:, k * S5_BH:(k + 1) * S5_BH]
            g_re[:, k * S5_BP:(k + 1) * S5_BP] = _dot_nt(dk, cre_ref[k])
            g_im[:, k * S5_BP:(k + 1) * S5_BP] = -_dot_nt(dk, cim_ref[k])
        _segment_scan(g_re, g_im, st_re, st_im, ar_ref[...], ai_ref[...], tm // 8, adjoint=True, write=True)
        lam_r, lam_i = g_re[...], g_im[...]
        lrb, lib = _bf(lam_r), _bf(lam_i)
        lamre_ref[...] = lrb
        lamim_ref[...] = lib

        def wrapped(last_ref):
            z = last_ref[...]
            row = lax.broadcasted_iota(jnp.int32, z.shape, 0)
            return jnp.where(row >= 1, pltpu.roll(z, 1, 0), 0.0)

        first_r = jnp.where(tile > 0, hre_ref[...], wrapped(lre_ref))
        first_i = jnp.where(tile > 0, him_ref[...], wrapped(lim_ref))
        sp_r = jnp.concatenate([first_r, sre_ref[0:tm - 8, :]], axis=0)
        sp_i = jnp.concatenate([first_i, sim_ref[0:tm - 8, :]], axis=0)
        da_ref[0:1, :] += _sum0(lam_r * sp_r + lam_i * sp_i)
        da_ref[1:2, :] += _sum0(lam_i * sp_r - lam_r * sp_i)

        parts = []
        for k in range(S5_NB):
            sl = slice(k * S5_BP, (k + 1) * S5_BP)
            parts.append(_dot_nt(lrb[:, sl], bre_ref[k]) + _dot_nt(lib[:, sl], bim_ref[k]))
        dub = _bf(jnp.concatenate(parts, axis=1) + dus_ref[...])
        du_ref[...] = dub
        dh = _dot_nt(dub, win_ref[...])
        _, xn, r = _norm_mod(x_ref[...], p[R_N1:R_N1 + 1], p[R_SC1:R_SC1 + 1], p[R_SH1:R_SH1 + 1])
        dx_ref[...] = dx1_ref[...] + _norm_mod_bwd(dh, xn, r, p[R_N1:R_N1 + 1], p[R_SC1:R_SC1 + 1])
        vs_ref[1:2, :] += _sum0(dh * xn)
        vs_ref[2:3, :] += _sum0(dh)

    halo = pl.BlockSpec((8, NSTATE), lambda i: (jnp.maximum((nt - 1 - i) * (tm // 8) - 1, 0), 0))
    last = pl.BlockSpec((8, NSTATE), lambda i: (L // 8 - 1, 0))
    return pl.pallas_call(
        body, name="s5_bwd_in", grid=(nt,),
        in_specs=[_rows(tm, D, nt), _rows(tm, D, nt), _rows(tm, D, nt), _rows(tm, D, nt), _rows(tm, NSTATE, nt), _rows(tm, NSTATE, nt),
                  halo, halo, last, last, pl.BlockSpec((None, 8, D), lambda i: (layer, 0, 0)),
                  _const_w((S5_NB, S5_BH, S5_BP)), _const_w((S5_NB, S5_BH, S5_BP)),
                  _const_w((S5_NB, S5_BP, S5_BH)), _const_w((S5_NB, S5_BP, S5_BH)),
                  _whole((NSEG, NSTATE)), _whole((NSEG, NSTATE)), _whole((1, NSTATE)), _whole((1, NSTATE)), _layer_w(D, D, 0)],
        out_specs=[_rows(tm, D, nt), _rows(tm, D, nt), _rows(tm, NSTATE, nt), _rows(tm, NSTATE, nt), _whole((8, NSTATE)), _whole((8, D))],
        out_shape=[jax.ShapeDtypeStruct((L, D), F32), jax.ShapeDtypeStruct((L, D), BF16),
                   jax.ShapeDtypeStruct((L, NSTATE), BF16), jax.ShapeDtypeStruct((L, NSTATE), BF16),
                   jax.ShapeDtypeStruct((8, NSTATE), F32), jax.ShapeDtypeStruct((8, D), F32)],
        scratch_shapes=[pltpu.VMEM((tm, NSTATE), F32), pltpu.VMEM((tm, NSTATE), F32),
                        pltpu.VMEM((NSEG, NSTATE), F32), pltpu.VMEM((NSEG, NSTATE), F32)],
        compiler_params=_params(1, 60),
    )(dx1, dy1_b, du_skip, x, s_re, s_im, s_re, s_im, s_re, s_im, pv, b_re, b_im, c_re, c_im, l0_re, l0_im, ar, ai, w_in)


def _blockdiag_b(bt):
    b = bt.reshape(S5_H, S5_NB, 16, S5_P).transpose(1, 2, 0, 3)
    eye = jnp.eye(16, dtype=bt.dtype)
    return (b[:, :, :, None, :] * eye[None, :, None, :, None]).reshape(S5_NB, S5_BH, S5_BP)


def _unblock_b(d):
    d = jnp.einsum("bghgp->bghp", d.reshape(S5_NB, 16, S5_H, 16, S5_P))
    return d.transpose(2, 0, 1, 3).reshape(S5_H, S5_G, S5_P)


def _blockdiag_c(cm):
    c4 = cm.reshape(S5_NB, 16, S5_H, S5_P)
    eye = jnp.eye(16, dtype=cm.dtype)
    out = c4.transpose(0, 1, 3, 2)[:, :, :, None, :] * eye[None, :, None, :, None]
    return out.reshape(S5_NB, S5_BP, S5_BH)


def _unblock_c(d):
    d = jnp.einsum("bgpgh->bghp", d.reshape(S5_NB, 16, S5_P, 16, S5_H))
    return d.reshape(S5_G, S5_H, S5_P)


def _tril_mask():
    return lax.broadcasted_iota(jnp.int32, (SG_CHUNK, SG_CHUNK), 0) >= lax.broadcasted_iota(jnp.int32, (SG_CHUNK, SG_CHUNK), 1)


def _sg_fwd(x, pv, w_in, w_s, b_t, vg, w_out, layer, tm, after=None):
    L = x.shape[0]
    nc = tm // SG_CHUNK

    def body(x_ref, pv_ref, win_ref, ws_ref, bt_ref, vg_ref, wout_ref, x1_ref, h_ref, uv_ref, vm_ref, q_ref, y_ref):
        xv, p = x_ref[...], pv_ref[...]
        h, _, _ = _norm_mod(xv, p[R_N1:R_N1 + 1], p[R_SC1:R_SC1 + 1], p[R_SH1:R_SH1 + 1])
        hb = _bf(h)
        h_ref[...] = hb
        uv = _dot(hb, win_ref[...])
        uv_ref[...] = uv
        v = uv[:, D:]
        rv = lax.rsqrt(jnp.mean(v * v, axis=-1, keepdims=True) + EPS)
        vnb = _bf((v * rv) * vg_ref[...])
        mask = _tril_mask()
        bt = bt_ref[...]
        for hd in range(SG_HEADS):
            wm = _bf(jnp.where(mask, ws_ref[hd], 0.0))
            cs = slice(hd * SG_CHUNK, (hd + 1) * SG_CHUNK)
            for ck in range(nc):
                rs = slice(ck * SG_CHUNK, (ck + 1) * SG_CHUNK)
                vm_ref[rs, cs] = _dot(wm, vnb[rs, cs]) + bt[:, hd:hd + 1]
        qb = _bf(uv[:, :D] * vm_ref[...])
        q_ref[...] = qb
        y = _dot(qb, wout_ref[...])
        y_ref[...] = y
        x1_ref[...] = xv + p[R_G1:R_G1 + 1] * y

    call, tail = _call_after(
        after, body, 7, name="sg_fwd", grid=(L // tm,),
        in_specs=[_rows(tm, D), pl.BlockSpec((None, 8, D), lambda i: (layer, 0, 0)), _layer_w(D, 2 * D, 0),
                  _whole((SG_HEADS, SG_CHUNK, SG_CHUNK)), _whole((SG_CHUNK, SG_HEADS)), _whole((1, D)), _layer_w(D, D, 0)],
        out_specs=[_rows(tm, D), _rows(tm, D), _rows(tm, 2 * D), _rows(tm, D), _rows(tm, D), _rows(tm, D)],
        out_shape=[jax.ShapeDtypeStruct((L, D), F32), jax.ShapeDtypeStruct((L, D), BF16), jax.ShapeDtypeStruct((L, 2 * D), F32),
                   jax.ShapeDtypeStruct((L, D), F32), jax.ShapeDtypeStruct((L, D), BF16), jax.ShapeDtypeStruct((L, D), F32)],
        compiler_params=_params(1, 56),
    )
    return call(x, pv, w_in, w_s, b_t, vg, w_out, *tail)


def _sg_bwd(dx1, x, y, uv, vm, pv, w_in, w_s, vg, w_out, layer, tm, after=None):
    L = x.shape[0]
    nc = tm // SG_CHUNK

    def body(dx1_ref, x_ref, y_ref, uv_ref, vm_ref, pv_ref, win_ref, ws_ref, vg_ref, wout_ref,
             dx_ref, duv_ref, dy_ref, vs_ref, dws_ref, dbt_ref, dvn_scr):
        @pl.when(pl.program_id(0) == 0)
        def _():
            vs_ref[...] = jnp.zeros_like(vs_ref)
            dws_ref[...] = jnp.zeros_like(dws_ref)
            dbt_ref[...] = jnp.zeros_like(dbt_ref)

        dx1v, p = dx1_ref[...], pv_ref[...]
        dyb = _bf(dx1v * p[R_G1:R_G1 + 1])
        dy_ref[...] = dyb
        vs_ref[0:1, :] += _sum0(dx1v * y_ref[...])
        dq = _dot_nt(dyb, wout_ref[...])
        uv = uv_ref[...]
        u, v = uv[:, :D], uv[:, D:]
        dub = _bf(dq * vm_ref[...])
        dvm = dq * u
        dvmb = _bf(dvm)
        rv = lax.rsqrt(jnp.mean(v * v, axis=-1, keepdims=True) + EPS)
        vh = v * rv
        vgv = vg_ref[...]
        vnb = _bf(vh * vgv)
        mask = _tril_mask()
        for hd in range(SG_HEADS):
            wm = _bf(jnp.where(mask, ws_ref[hd], 0.0))
            cs = slice(hd * SG_CHUNK, (hd + 1) * SG_CHUNK)
            dws = jnp.zeros((SG_CHUNK, SG_CHUNK), F32)
            dbs = jnp.zeros((SG_CHUNK, 1), F32)
            for ck in range(nc):
                rs = slice(ck * SG_CHUNK, (ck + 1) * SG_CHUNK)
                dvn_scr[rs, cs] = _dot_tn(wm, dvmb[rs, cs])
                dws = dws + _dot_nt(dvmb[rs, cs], vnb[rs, cs])
                dbs = dbs + jnp.sum(dvm[rs, cs], axis=1, keepdims=True)
            dws_ref[hd] += jnp.where(mask, dws, 0.0)
            dbt_ref[:, hd:hd + 1] += dbs
        dvn = dvn_scr[...]
        vs_ref[3:4, :] += _sum0(dvn * vh)
        dvnn = dvn * vgv
        dvb = _bf(rv * (dvnn - vh * jnp.mean(dvnn * vh, axis=-1, keepdims=True)))
        duv_ref[:, 0:D] = dub
        duv_ref[:, D:2 * D] = dvb
        dh = _dot_nt(dub, win_ref[:, 0:D]) + _dot_nt(dvb, win_ref[:, D:2 * D])
        _, xn, r = _norm_mod(x_ref[...], p[R_N1:R_N1 + 1], p[R_SC1:R_SC1 + 1], p[R_SH1:R_SH1 + 1])
        dx_ref[...] = dx1v + _norm_mod_bwd(dh, xn, r, p[R_N1:R_N1 + 1], p[R_SC1:R_SC1 + 1])
        vs_ref[1:2, :] += _sum0(dh * xn)
        vs_ref[2:3, :] += _sum0(dh)

    call, tail = _call_after(
        after, body, 10, name="sg_bwd", grid=(L // tm,),
        in_specs=[_rows(tm, D), _rows(tm, D), _rows(tm, D), _rows(tm, 2 * D), _rows(tm, D),
                  pl.BlockSpec((None, 8, D), lambda i: (layer, 0, 0)), _layer_w(D, 2 * D, 0),
                  _whole((SG_HEADS, SG_CHUNK, SG_CHUNK)), _whole((1, D)), _layer_w(D, D, 0)],
        out_specs=[_rows(tm, D), _rows(tm, 2 * D), _rows(tm, D), _whole((8, D)),
                   _whole((SG_HEADS, SG_CHUNK, SG_CHUNK)), _whole((SG_CHUNK, SG_HEADS))],
        out_shape=[jax.ShapeDtypeStruct((L, D), F32), jax.ShapeDtypeStruct((L, 2 * D), BF16), jax.ShapeDtypeStruct((L, D), BF16),
                   jax.ShapeDtypeStruct((8, D), F32), jax.ShapeDtypeStruct((SG_HEADS, SG_CHUNK, SG_CHUNK), F32),
                   jax.ShapeDtypeStruct((SG_CHUNK, SG_HEADS), F32)],
        scratch_shapes=[pltpu.VMEM((tm, D), F32)],
        compiler_params=_params(1, 56),
    )
    return call(dx1, x, y, uv, vm, pv, w_in, w_s, vg, w_out, *tail)


def _final(x, target, fg, tm):
    L = x.shape[0]

    def body(x_ref, t_ref, g_ref, dx_ref, vs_ref):
        @pl.when(pl.program_id(0) == 0)
        def _():
            vs_ref[...] = jnp.zeros_like(vs_ref)

        xv, g = x_ref[...], g_ref[...]
        r = lax.rsqrt(jnp.mean(xv * xv, axis=-1, keepdims=True) + EPS)
        xn = xv * r
        e = xn * g - t_ref[...]
        vs_ref[0:1, :] += jnp.sum(e * e)
        dout = e * (1.0 / D)
        vs_ref[1:2, :] += _sum0(dout * xn)
        dxn = dout * g
        dx_ref[...] = r * (dxn - xn * jnp.mean(dxn * xn, axis=-1, keepdims=True))

    return pl.pallas_call(
        body, name="final_loss", grid=(L // tm,),
        in_specs=[_rows(tm, D), _rows(tm, D), _whole((1, D))],
        out_specs=[_rows(tm, D), _whole((8, D))],
        out_shape=[jax.ShapeDtypeStruct((L, D), F32), jax.ShapeDtypeStruct((8, D), F32)],
        compiler_params=_params(1),
    )(x, target, fg)


def _pack_flat(arrs, multiple=LANES):
    flat = jnp.concatenate([a.reshape(-1).astype(F32) for a in arrs])
    return jnp.pad(flat, (0, -flat.shape[0] % multiple))


def _pack(arrs, row_multiple=8):
    return _pack_flat(arrs, row_multiple * LANES).reshape(-1, LANES)


def _unpack(buf, shapes, lead=()):
    flat = buf.reshape(lead + (-1,))
    out, off = [], 0
    for s in shapes:
        n = 1
        for d in s:
            n *= d
        out.append(flat[..., off:off + n].reshape(lead + tuple(s)))
        off += n
    return out


BIG = ("ff_w1", "ff_w2", "conv_w_in", "conv_w_out", "ssm_w_in", "ssm_glu_w", "ssm_w_out", "sg_w_in", "sg_w_out")
BIG_AXIS = {"ff_w1": 2, "ff_w2": 1, "conv_w_in": 2, "conv_w_out": 1, "ssm_w_in": 1, "ssm_glu_w": 1, "ssm_w_out": 1,
            "sg_w_in": 2, "sg_w_out": 1}
LAYER_WEIGHTS = (
    (("conv_w_in", 0), ("conv_w_out", 0), ("ff_w1", 0), ("ff_w2", 0)),
    (("ssm_w_in", 0), ("ssm_glu_w", 0), ("ssm_w_out", 0), ("ff_w1", 1), ("ff_w2", 1)),
    (("sg_w_in", 0), ("sg_w_out", 0), ("ff_w1", 2), ("ff_w2", 2)),
    (("conv_w_in", 1), ("conv_w_out", 1), ("ff_w1", 3), ("ff_w2", 3)),
)
GATHER_GROUPS = tuple(grp for lw in LAYER_WEIGHTS for grp in (lw[:-2], lw[-2:]))
SMALL_SHARDED = ("conv_w", "conv_b", "sg_v_g")
SMALL_WIDE_PADDED = ("ssm_b_re", "ssm_b_im")
SMALL = ("ada_b", "norm1_g", "norm2_g", "final_g", "ssm_a_re", "ssm_a_im", "ssm_log_dt", "ssm_b_re", "ssm_b_im", "ssm_c_re",
         "ssm_c_im", "ssm_d", "ssm_glu_b", "sg_w_s", "sg_b_s") + SMALL_SHARDED
WEIGHTS = ("ada_w", "ada_b", "norm1_g", "norm2_g", "ff_w1", "ff_w2", "final_g", "conv_w_in", "conv_w", "conv_b", "conv_w_out",
           "ssm_w_in", "ssm_a_re", "ssm_a_im", "ssm_log_dt", "ssm_b_re", "ssm_b_im", "ssm_c_re", "ssm_c_im", "ssm_d",
           "ssm_glu_w", "ssm_glu_b", "ssm_w_out", "sg_w_in", "sg_v_g", "sg_w_s", "sg_b_s", "sg_w_out")


def kernel(x, c, ada_w, ada_b, norm1_g, norm2_g, ff_w1, ff_w2, final_g, conv_w_in, conv_w, conv_b, conv_w_out, ssm_w_in, ssm_a_re, ssm_a_im, ssm_log_dt, ssm_b_re, ssm_b_im, ssm_c_re, ssm_c_im, ssm_d, ssm_glu_w, ssm_glu_b, ssm_w_out, sg_w_in, sg_v_g, sg_w_s, sg_b_s, sg_w_out, loss_target, m_ada_w, m_ada_b, m_norm1_g, m_norm2_g, m_ff_w1, m_ff_w2, m_final_g, m_conv_w_in, m_conv_w, m_conv_b, m_conv_w_out, m_ssm_w_in, m_ssm_a_re, m_ssm_a_im, m_ssm_log_dt, m_ssm_b_re, m_ssm_b_im, m_ssm_c_re, m_ssm_c_im, m_ssm_d, m_ssm_glu_w, m_ssm_glu_b, m_ssm_w_out, m_sg_w_in, m_sg_v_g, m_sg_w_s, m_sg_b_s, m_sg_w_out, v_ada_w, v_ada_b, v_norm1_g, v_norm2_g, v_ff_w1, v_ff_w2, v_final_g, v_conv_w_in, v_conv_w, v_conv_b, v_conv_w_out, v_ssm_w_in, v_ssm_a_re, v_ssm_a_im, v_ssm_log_dt, v_ssm_b_re, v_ssm_b_im, v_ssm_c_re, v_ssm_c_im, v_ssm_d, v_ssm_glu_w, v_ssm_glu_b, v_ssm_w_out, v_sg_w_in, v_sg_v_g, v_sg_w_s, v_sg_b_s, v_sg_w_out):
    args = dict(locals())
    w = {n: args[n] for n in WEIGHTS}
    m = {n: args["m_" + n] for n in WEIGHTS}
    v = {n: args["v_" + n] for n in WEIGHTS}
    L = x.shape[1]
    tm = min(L, 256)
    tm2 = min(L, 512)
    chip = 2 * lax.axis_index("x") + lax.axis_index("y")
    me = 2 * chip + lax.axis_index("c")
    xin = x[0]
    target = loss_target[0]
    chip1 = chip.reshape(1).astype(jnp.int32)
    place = jnp.stack([chip, lax.axis_index("c")]).astype(jnp.int32)

    gathers = []

    casts = {}

    def cast_group(g, after=None):
        entries = GATHER_GROUPS[g]
        axes = [BIG_AXIS[n] for n, _ in entries]
        casts[g] = _cast_place([w[n] for n, _ in entries], [li for _, li in entries], axes, place, f"cast_group{g}", after)

    def start_gather(g, after):
        if g not in casts:
            cast_group(g)
        axes = [BIG_AXIS[n] for n, _ in GATHER_GROUPS[g]]
        lands = casts[g]
        s_sems, r_sems, lands, token = _gather_start(lands, axes, f"gather_start{g}", after)
        gathers.append((s_sems, r_sems, lands, axes))
        return token

    def weights_of(g, after):
        s_sems, r_sems, lands, axes = gathers[g]
        lands = _gather_wait(s_sems, r_sems, lands, axes, f"gather_wait{g}", after)
        lands = _gather_share(lands, axes, f"gather_share{g}")
        token = start_gather(g + 2, lands[0]) if g + 2 < len(GATHER_GROUPS) else None
        return dict(zip([n for n, _ in GATHER_GROUPS[g]], lands)), token

    small_in = _pack([c, conv_w, conv_b, sg_v_g])
    got = _allgather_small(small_in, "gather_small_inputs").reshape(N_DEV, -1)
    c_all, cw_sh, cb_sh, vg_sh = _unpack(got, [(D,), conv_w.shape, conv_b.shape, sg_v_g.shape], lead=(N_DEV,))
    conv_w_full = jnp.concatenate([cw_sh[2 * k] for k in range(4)], axis=-1)
    conv_b_full = jnp.concatenate([cb_sh[2 * k] for k in range(4)], axis=-1)
    vg_full = jnp.concatenate([vg_sh[2 * k] for k in range(4)], axis=-1)
    c16 = jnp.pad(c_all, ((0, 16 - N_DEV), (0, 0)))

    cols = ada_w.shape[2]
    ada_b_cols = lax.dynamic_slice_in_dim(ada_b, chip * cols, cols, axis=1)[:, None, :]
    mod_sh = _ada_fwd(c16, ada_w, ada_b_cols)[:, :N_DEV, :]
    mod_all = _allgather_small(_pack([mod_sh]), "gather_mod").reshape(N_DEV, -1)
    mod_all = _unpack(mod_all, [mod_sh.shape], lead=(N_DEV,))[0]
    mod_mine = lax.dynamic_index_in_dim(mod_all[0::2], me, axis=2, keepdims=False)
    mod_mine = mod_mine.transpose(1, 0, 2).reshape(DEPTH, 6, D)
    pv = jnp.concatenate([mod_mine, norm1_g[:, None, :], norm2_g[:, None, :]], axis=1)

    second_started = start_gather(1, start_gather(0, pv))
    for g in range(2, len(GATHER_GROUPS)):
        cast_group(g, second_started)

    cw_rows = jnp.concatenate([conv_w_full, conv_b_full[:, None, :], jnp.zeros((conv_w_full.shape[0], 4, D), F32)], axis=1)

    a_re, a_im = ssm_a_re[0], ssm_a_im[0]
    log_dt = ssm_log_dt[0][:, None]
    bt_re, bt_im = ssm_b_re[0].transpose(2, 0, 1), ssm_b_im[0].transpose(2, 0, 1)
    abar_re, abar_im, bbar_re, bbar_im = _s5_params_fwd(a_re, a_im, log_dt, bt_re, bt_im, after=second_started)
    ar_vec, ai_vec = abar_re.reshape(1, NSTATE), abar_im.reshape(1, NSTATE)
    bd_re, bd_im = _bf(_blockdiag_b(bbar_re)), _bf(_blockdiag_b(bbar_im))
    cd_re, cd_im = _bf(_blockdiag_c(ssm_c_re[0])), _bf(_blockdiag_c(ssm_c_im[0]))

    saved = []
    fulls = []
    xl = xin
    for i in range(DEPTH):
        kind = MIXER_OF_LAYER[i]
        j = i // 3
        first_after = [second_started, bd_re, bd_im, cd_re, cd_im, cw_rows] + [casts[g][0] for g in range(2, len(GATHER_GROUPS))]
        full, tok = weights_of(2 * i, first_after if i == 0 else [xl])
        fulls.append(full)
        if kind == 0:
            x1, h, bcx, conv, q, y = _conv_fwd(xl, pv, full["conv_w_in"], full["conv_w_out"], cw_rows, i, j, tm2, after=tok)
            mix = dict(h=h, bcx=bcx, conv=conv, q=q, y=y)
        elif kind == 1:
            xp = _to_segments(xl)
            h, u, e_re, e_im = _s5_fwd_ends(xp, pv, full["ssm_w_in"], bd_re, bd_im, ar_vec, ai_vec, i, tm, after=tok)
            s0_re, s0_im = _s5_segment_states(e_re, e_im, ar_vec, ai_vec, L // NSEG, adjoint=False)
            x1p, s_re, s_im, y1, zg, y3, y = _s5_fwd_out(xp, u, pv, bd_re, bd_im, s0_re, s0_im, ar_vec, ai_vec, cd_re, cd_im,
                                                         ssm_d, full["ssm_glu_w"], ssm_glu_b, full["ssm_w_out"], i, tm)
            x1 = _from_segments(x1p)
            mix = dict(xp=xp, h=h, u=u, s_re=s_re, s_im=s_im, y1=y1, zg=zg, y3=y3, y=y)
        else:
            x1, h, uv, vm, q, y = _sg_fwd(xl, pv, full["sg_w_in"], sg_w_s[0], sg_b_s[0].T, vg_full, full["sg_w_out"], i, tm2,
                                          after=tok)
            mix = dict(h=h, uv=uv, vm=vm, q=q, y=y)
        ffn_weights, tok = weights_of(2 * i + 1, [x1])
        full.update(ffn_weights)
        x2, h2, a, f = _ffn_fwd(x1, pv, full["ff_w1"], full["ff_w2"], i, tm2, after=tok)
        saved.append(dict(x=xl, x1=x1, h2=h2, a=a, f=f, **mix))
        xl = x2

    dxl, vs_fin = _final(xl, target, final_g[None, :], tm2)

    gfull = {n: [None] * w[n].shape[0] for n in BIG}
    vs_mix, vs_ffn = [None] * DEPTH, [None] * DEPTH
    small_g = {}
    scatters = {}
    token = None

    def start_scatter(key, entries, after):
        garrs = [gfull[n][li][None] for n, li in entries]
        gaxes = [BIG_AXIS[n] for n, _ in entries]
        s_sems, r_sems, garrs, lands, tok = _scatter_start(garrs, gaxes, f"scatter_start{key}", after)
        scatters[key] = (s_sems, r_sems, garrs, lands, gaxes, entries)
        return tok

    for i in reversed(range(DEPTH)):
        kind = MIXER_OF_LAYER[i]
        j = i // 3
        sv = saved[i]
        full = fulls[i]
        dx1, p_b, da_b, df_b, vs_ffn[i] = _ffn_bwd(dxl, sv["x1"], sv["a"], sv["f"], pv, full["ff_w1"], full["ff_w2"], i, tm,
                                                   after=token)
        gfull["ff_w1"][i] = _mm_tn(sv["h2"], da_b, f"wgrad_ff_w1_{i}")
        gfull["ff_w2"][i] = _mm_tn(p_b, df_b, f"wgrad_ff_w2_{i}")
        if i == 0:
            token = start_scatter("0f", LAYER_WEIGHTS[0][2:], dx1)
        if kind == 0:
            dxl, dbcx_b, dy_b, vsm = _conv_bwd(dx1, sv["x"], sv["y"], sv["bcx"], sv["conv"], pv, full["conv_w_in"],
                                               full["conv_w_out"], cw_rows, i, j, tm2, after=token if i == 0 else None)
            gfull["conv_w_in"][j] = _mm_tn(sv["h"], dbcx_b, f"wgrad_conv_w_in_{j}")
            gfull["conv_w_out"][j] = _mm_tn(sv["q"], dy_b, f"wgrad_conv_w_out_{j}")
            small_g.setdefault("conv_w", [None, None])[j] = vsm[3:6]
            small_g.setdefault("conv_b", [None, None])[j] = vsm[6]
        elif kind == 1:
            dx1p = _to_segments(dx1)
            dy_b, y2_b, dzg_b, dy1_b, du_skip, eb_re, eb_im, vsm = _s5_bwd_ends(
                dx1p, sv["y"], sv["y1"], sv["zg"], sv["u"], pv, cd_re, cd_im, ar_vec, ai_vec, ssm_d, full["ssm_glu_w"],
                full["ssm_w_out"], i, tm)
            l0_re, l0_im = _s5_segment_states(eb_re, eb_im, ar_vec, ai_vec, L // NSEG, adjoint=True)
            dxp, du_b, lam_re, lam_im, dabar, vs_in = _s5_bwd_in(
                dx1p, dy1_b, du_skip, sv["xp"], sv["s_re"], sv["s_im"], pv, bd_re, bd_im, cd_re, cd_im, l0_re, l0_im,
                ar_vec, ai_vec, full["ssm_w_in"], i, tm)
            dxl = _from_segments(dxp)
            gfull["ssm_w_out"][0] = _mm_tn(sv["y3"], dy_b, "wgrad_ssm_w_out")
            gfull["ssm_glu_w"][0] = _mm_tn(y2_b, dzg_b, "wgrad_ssm_glu_w")
            gfull["ssm_w_in"][0] = _mm_tn(sv["h"], du_b, "wgrad_ssm_w_in")
            s5_late = dict(s_re=sv["s_re"], s_im=sv["s_im"], u=sv["u"], dy1_b=dy1_b, lam_re=lam_re, lam_im=lam_im, dabar=dabar)
            small_g.update(ssm_d=vsm[2], ssm_glu_b=vsm[1])
            vsm = jnp.concatenate([vsm[0:1], vs_in[1:3], jnp.zeros((5, D), F32)], axis=0)
        else:
            dxl, duv_b, dy_b, vsm, d_ws, d_bt = _sg_bwd(dx1, sv["x"], sv["y"], sv["uv"], sv["vm"], pv, full["sg_w_in"],
                                                        sg_w_s[0], vg_full, full["sg_w_out"], i, tm2)
            gfull["sg_w_in"][0] = _mm_tn(sv["h"], duv_b, "wgrad_sg_w_in")
            gfull["sg_w_out"][0] = _mm_tn(sv["q"], dy_b, "wgrad_sg_w_out")
            small_g.update(sg_v_g=vsm[3], sg_w_s=d_ws, sg_b_s=d_bt.T)
        vs_mix[i] = vsm
        token = start_scatter(str(i), LAYER_WEIGHTS[i], dxl) if i > 0 else start_scatter("0c", LAYER_WEIGHTS[0][:2], dxl)
    grad_x = dxl[None]

    sums = {n: [None] * w[n].shape[0] for n in BIG}

    def collect(key, after):
        s_sems, r_sems, garrs, lands, gaxes, entries = scatters[key]
        garrs, recv = _scatter_wait(s_sems, r_sems, garrs, lands, gaxes, f"scatter_wait{key}", after)
        for (n, li), t in zip(entries, _sum_parts(recv, garrs, gaxes, chip1, f"sum_group{key}")):
            sums[n][li] = t
        return sums[entries[-1][0]][entries[-1][1]]

    after = token
    for key in ("3", "2", "1"):
        after = collect(key, after)
    early = [(n, li) for i in (3, 2, 1) for n, li in LAYER_WEIGHTS[i]]
    late = list(LAYER_WEIGHTS[0][2:]) + list(LAYER_WEIGHTS[0][:2])
    s_sems, r_sems, mine_thru, lands, tok = _swap_start([sums[n][li] for n, li in early], "swap_start_early", after)

    blocks = dict(
        c_re=_mm_tn_blocks(s5_late["s_re"], s5_late["dy1_b"], S5_BP, S5_BH, "wgrad_s5_c_re", after=tok),
        c_im=_mm_tn_blocks(s5_late["s_im"], s5_late["dy1_b"], S5_BP, S5_BH, "wgrad_s5_c_im", after=tok),
        b_re=_mm_tn_blocks(s5_late["u"], s5_late["lam_re"], S5_BH, S5_BP, "wgrad_s5_b_re", after=tok),
        b_im=_mm_tn_blocks(s5_late["u"], s5_late["lam_im"], S5_BH, S5_BP, "wgrad_s5_b_im", after=tok))
    d_are, d_aim, d_ldt, d_btre, d_btim = _s5_params_bwd(
        a_re, a_im, log_dt, bt_re, bt_im, s5_late["dabar"][0].reshape(S5_G, S5_P), s5_late["dabar"][1].reshape(S5_G, S5_P),
        _unblock_b(blocks["b_re"]), _unblock_b(blocks["b_im"]))
    small_g.update(ssm_a_re=d_are, ssm_a_im=d_aim, ssm_log_dt=d_ldt, ssm_b_re=d_btre.transpose(1, 2, 0),
                   ssm_b_im=d_btim.transpose(1, 2, 0), ssm_c_re=_unblock_c(blocks["c_re"]), ssm_c_im=-_unblock_c(blocks["c_im"]))

    mine_thru, got = _swap_wait(s_sems, r_sems, mine_thru, lands, "swap_wait_early", blocks["b_im"])
    sib = dict(zip(early, got))
    for (n, li), t in zip(early, mine_thru):
        sums[n][li] = t
    after = got[-1]
    for key in ("0f", "0c"):
        after = collect(key, after)
    sib.update(zip(late, _swap_with_sibling([sums[n][li] for n, li in late], "swap_grad_sums_late")))

    dmod = _mod_bwd(jnp.stack(vs_mix), jnp.stack(vs_ffn), pv)
    small_g.update(ada_b=dmod[:, :6, :], norm1_g=dmod[:, 6, :], norm2_g=dmod[:, 7, :], final_g=vs_fin[1],
                   conv_w=jnp.stack(small_g["conv_w"]), conv_b=jnp.stack(small_g["conv_b"]))

    loss_part = (0.5 / D) * vs_fin[0, 0:1]
    part_shapes = [(1,)] + [tuple(small_g[n].shape) for n in SMALL]
    slots = _reduce_pair(_pack([loss_part] + [small_g[n] for n in SMALL], 16), "reduce_small_pair", sib[late[-1]])
    s_sems, r_sems, slots, tok = _reduce_cross_start(slots, "reduce_small_cross_start")

    res = {}
    for n in BIG:
        res[n] = _adamw_layers(w[n], sums[n], [sib[(n, li)] for li in range(w[n].shape[0])], m[n], v[n], f"adamw_{n}", after=tok)
        tok = res[n][0]

    slots = _reduce_cross_wait(s_sems, r_sems, slots, "reduce_small_cross_wait", tok)
    parts_sum = _reduce_finish(slots, "reduce_small_finish")
    summed = _unpack(parts_sum, part_shapes)
    loss = summed[0][0]
    gsum = dict(zip(SMALL, summed[1:]))
    dmod_all = _allgather_small(_pack([small_g["ada_b"]]), "gather_dmod", parts_sum)
    dmod_all = dmod_all.reshape(N_DEV, DEPTH, 6 * D)
    dmod_cols = lax.dynamic_slice_in_dim(dmod_all, chip * cols, cols, axis=2).transpose(1, 0, 2)
    g_ada_w = _ada_bwd(c16, jnp.pad(dmod_cols, ((0, 0), (0, 16 - N_DEV), (0, 0))))

    shp = ada_w.shape
    two = lambda t: t.reshape(shp[0] * shp[1], shp[2])
    res["ada_w"] = [t.reshape(shp) for t in _adamw(two(ada_w), [two(g_ada_w)], two(m_ada_w), two(v_ada_w), "adamw_ada_w")]

    def mine(n):
        g = gsum[n]
        if n in SMALL_SHARDED:
            g = lax.dynamic_slice_in_dim(g, chip * w[n].shape[-1], w[n].shape[-1], axis=g.ndim - 1)
        return g.reshape(w[n].shape)

    for k, names in enumerate(([n for n in SMALL if n not in SMALL_WIDE_PADDED], list(SMALL_WIDE_PADDED))):
        outs = _adamw_many([w[n] for n in names], [mine(n) for n in names], [m[n] for n in names], [v[n] for n in names],
                           f"adamw_small{k}")
        for idx, n in enumerate(names):
            res[n] = [outs[part][idx] for part in range(4)]

    outs = [loss, grad_x]
    for part in range(4):
        outs += [res[n][part] for n in WEIGHTS]
    return tuple(outs)
```

```python
import functools

import jax
import jax.numpy as jnp
from jax import lax
from jax.experimental import pallas as pl
from jax.experimental.pallas import tpu as pltpu

F32 = jnp.float32
BF16 = jnp.bfloat16
D = 1024
EPS = 1e-6
DEPTH = 4
MIXER_OF_LAYER = (0, 1, 2, 0)
S5_G, S5_H, S5_P = 64, 16, 64
S5_NB = 4
S5_BH = S5_H * 16
S5_BP = S5_P * 16
NSTATE = S5_G * S5_P
SG_HEADS, SG_CHUNK = 8, 128
ADAM_LR, ADAM_B1, ADAM_B2, ADAM_EPS, ADAM_WD, ADAM_STEP = 0.001, 0.9, 0.999, 1e-08, 0.01, 10
N_DEV = 8
MESH = pl.DeviceIdType.MESH
LANES = 1024
R_SH1, R_SC1, R_G1, R_SH2, R_SC2, R_G2, R_N1, R_N2 = range(8)


def _dot(a, b):
    return jnp.dot(a, b, preferred_element_type=F32)


def _dot_nt(a, b):
    return lax.dot_general(a, b, (((1,), (1,)), ((), ())), preferred_element_type=F32)


def _dot_tn(a, b):
    return lax.dot_general(a, b, (((0,), (0,)), ((), ())), preferred_element_type=F32)


def _bf(x):
    return x.astype(BF16)


def _sum0(x):
    return jnp.sum(x, axis=0, keepdims=True)


def _params(n_axes, vmem_mb=48):
    return pltpu.CompilerParams(dimension_semantics=("arbitrary",) * n_axes, vmem_limit_bytes=vmem_mb << 20)


def _rows(tm, cols, nt=None):
    if nt is None:
        return pl.BlockSpec((tm, cols), lambda i: (i, 0))
    return pl.BlockSpec((tm, cols), lambda i: (nt - 1 - i, 0))


def _whole(shape):
    nd = len(shape)
    return pl.BlockSpec(shape, lambda *_: (0,) * nd)


def _layer_w(r, c, layer):
    return pl.BlockSpec((None, r, c), lambda *_: (layer, 0, 0), pipeline_mode=pl.Buffered(1))


def _const_w(shape):
    nd = len(shape)
    return pl.BlockSpec(shape, lambda *_: (0,) * nd, pipeline_mode=pl.Buffered(1))


def _call_after(after, body, n_in, *, in_specs, **kw):
    if after is None:
        return pl.pallas_call(body, in_specs=in_specs, **kw), ()

    def body_after(*refs):
        return body(*refs[:n_in], *refs[n_in + 1:])

    return pl.pallas_call(body_after, in_specs=list(in_specs) + [pl.BlockSpec(memory_space=pl.ANY)], **kw), (after,)


def _norm_mod(x, ng, sc, sh):
    r = lax.rsqrt(jnp.mean(x * x, axis=-1, keepdims=True) + EPS)
    xn = x * r
    return (xn * ng) * (1.0 + sc) + sh, xn, r


def _norm_mod_bwd(dh, xn, r, ng, sc):
    dxn = dh * (ng * (1.0 + sc))
    return r * (dxn - xn * jnp.mean(dxn * xn, axis=-1, keepdims=True))


def _shift_down(z, prev8, k):
    row = lax.broadcasted_iota(jnp.int32, z.shape, 0)
    if k == 1:
        return jnp.where(row >= 1, pltpu.roll(z, 1, 0), prev8[7:8])
    return jnp.where(row >= 2, pltpu.roll(z, 2, 0), jnp.where(row == 0, prev8[6:7], prev8[7:8]))


def _shift_up(z, next8, k):
    n = z.shape[0]
    row = lax.broadcasted_iota(jnp.int32, z.shape, 0)
    if k == 1:
        return jnp.where(row <= n - 2, pltpu.roll(z, n - 1, 0), next8[0:1])
    return jnp.where(row <= n - 3, pltpu.roll(z, n - 2, 0), jnp.where(row == n - 2, next8[0:1], next8[1:2]))


def _place():
    x, y, c = lax.axis_index("x"), lax.axis_index("y"), lax.axis_index("c")
    chips = [(1 - x, y), (x, 1 - y), (1 - x, 1 - y)]
    return x, y, c, chips


def _allgather_small(x_shard, name, after=None):
    m_per, n = x_shard.shape

    def body(x_ref, out_ref, send_sems, recv_sems, local_sem):
        x, y, c, chips = _place()
        me, sibling = (x, y, c), (x, y, 1 - c)

        def rows(px, py, pc):
            return out_ref.at[pl.ds((4 * px + 2 * py + pc) * m_per, m_per), :]

        def copy(k, block, to, src=None):
            return pltpu.make_async_remote_copy(
                src_ref=rows(*block) if src is None else src, dst_ref=rows(*block),
                send_sem=send_sems.at[k], recv_sem=recv_sems.at[k], device_id=to, device_id_type=MESH)

        mine = pltpu.make_async_copy(x_ref, rows(*me), local_sem)
        mine.start()
        first = [copy(0, me, sibling, src=x_ref)]
        first += [copy(1 + j, me, (*chip, c), src=x_ref) for j, chip in enumerate(chips)]
        for cp in first:
            cp.start()
        passed = [copy(4 + j, (*chip, c), sibling) for j, chip in enumerate(chips)]
        for j, chip in enumerate(chips):
            copy(1 + j, (*chip, c), me).wait_recv()
            passed[j].start()
        copy(0, sibling, me).wait_recv()
        for j, chip in enumerate(chips):
            copy(4 + j, (*chip, 1 - c), me).wait_recv()
        for cp in first + passed:
            cp.wait_send()
        mine.wait()

    call, tail = _call_after(
        after, body, 1, name=name, out_shape=jax.ShapeDtypeStruct((N_DEV * m_per, n), F32),
        in_specs=[pl.BlockSpec(memory_space=pltpu.VMEM)], out_specs=pl.BlockSpec(memory_space=pltpu.VMEM),
        scratch_shapes=[pltpu.SemaphoreType.DMA((7,)), pltpu.SemaphoreType.DMA((7,)), pltpu.SemaphoreType.DMA],
        compiler_params=pltpu.CompilerParams(vmem_limit_bytes=48 << 20),
    )
    return call(x_shard, *tail)


def _reduce_pair(x_part, name, after=None):
    m, n = x_part.shape
    h = m // 2

    def body(x_ref, slots_ref, sib_buf, send_sem, recv_sem):
        x, y, c, _ = _place()
        swap = pltpu.make_async_remote_copy(src_ref=x_ref, dst_ref=sib_buf, send_sem=send_sem, recv_sem=recv_sem,
                                            device_id=(x, y, 1 - c), device_id_type=MESH)
        swap.start()
        swap.wait()
        mine = pl.ds(pl.multiple_of(c * h, 8), h)
        slots_ref[pl.ds(2 * x + y, 1)] = (x_ref[mine, :] + sib_buf[mine, :])[None]

    call, tail = _call_after(
        after, body, 1, name=name, out_shape=jax.ShapeDtypeStruct((4, h, n), F32),
        in_specs=[pl.BlockSpec(memory_space=pltpu.VMEM)], out_specs=pl.BlockSpec(memory_space=pltpu.VMEM),
        scratch_shapes=[pltpu.VMEM((m, n), F32), pltpu.SemaphoreType.DMA, pltpu.SemaphoreType.DMA],
        compiler_params=pltpu.CompilerParams(vmem_limit_bytes=48 << 20),
    )
    return call(x_part, *tail)


def _reduce_cross_start(slots, name):
    def body(slots_ref, send_sems, recv_sems, thru, token):
        x, y, c, chips = _place()
        mine = slots_ref.at[pl.ds(2 * x + y, 1)]
        for j, chip in enumerate(chips):
            pltpu.make_async_remote_copy(src_ref=mine, dst_ref=mine, send_sem=send_sems.at[j], recv_sem=recv_sems.at[j],
                                         device_id=(*chip, c), device_id_type=MESH).start()
        token[...] = jnp.zeros_like(token)

    res = pl.pallas_call(
        body, name=name,
        out_shape=(pltpu.SemaphoreType.DMA((3,)), pltpu.SemaphoreType.DMA((3,)), pltpu.HBM(slots.shape, F32),
                   jax.ShapeDtypeStruct((8, 128), F32)),
        in_specs=[HBM_SPEC], out_specs=(SEM_SPEC, SEM_SPEC, HBM_SPEC, pl.BlockSpec(memory_space=pltpu.VMEM)),
        input_output_aliases={0: 2}, compiler_params=SPLIT_COPY_PARAMS,
    )(*_in_hbm([slots]))
    return res


def _reduce_cross_wait(send_sems, recv_sems, slots, name, after):
    def body(slots_ref, s_sems, r_sems, after_ref, thru):
        x, y, c, chips = _place()
        for j, chip in enumerate(chips):
            theirs = slots_ref.at[pl.ds(2 * chip[0] + chip[1], 1)]
            cp = pltpu.make_async_remote_copy(src_ref=theirs, dst_ref=theirs, send_sem=s_sems.at[j], recv_sem=r_sems.at[j],
                                              device_id=(x, y, c), device_id_type=MESH)
            cp.wait_send()
            cp.wait_recv()

    return pl.pallas_call(
        body, name=name, out_shape=pltpu.HBM(slots.shape, F32),
        in_specs=[HBM_SPEC, SEM_SPEC, SEM_SPEC, ANY_SPEC], out_specs=HBM_SPEC,
        input_output_aliases={0: 0}, compiler_params=SPLIT_COPY_PARAMS,
    )(slots, send_sems, recv_sems, after)


def _reduce_finish(slots, name):
    _, h, n = slots.shape

    def body(slots_ref, out_ref, send_sem, recv_sem):
        x, y, c, _ = _place()
        mine = pl.ds(pl.multiple_of(c * h, 8), h)
        theirs = pl.ds(pl.multiple_of((1 - c) * h, 8), h)
        out_ref[mine, :] = ((slots_ref[0] + slots_ref[1]) + slots_ref[2]) + slots_ref[3]
        give = pltpu.make_async_remote_copy(src_ref=out_ref.at[mine, :], dst_ref=out_ref.at[mine, :], send_sem=send_sem,
                                            recv_sem=recv_sem, device_id=(x, y, 1 - c), device_id_type=MESH)
        give.start()
        give.wait_send()
        pltpu.make_async_remote_copy(src_ref=out_ref.at[theirs, :], dst_ref=out_ref.at[theirs, :], send_sem=send_sem,
                                     recv_sem=recv_sem, device_id=(x, y, c), device_id_type=MESH).wait_recv()

    return pl.pallas_call(
        body, name=name, out_shape=jax.ShapeDtypeStruct((2 * h, n), F32),
        in_specs=[pl.BlockSpec(memory_space=pltpu.VMEM)], out_specs=pl.BlockSpec(memory_space=pltpu.VMEM),
        scratch_shapes=[pltpu.SemaphoreType.DMA, pltpu.SemaphoreType.DMA],
        compiler_params=pltpu.CompilerParams(vmem_limit_bytes=48 << 20),
    )(slots)


def _shard_region(ref, full_shape, axis, chip_k, half=None):
    _, r, c = full_shape
    if axis == 1:
        rs = r // 4
        if half is None:
            return ref.at[:, pl.ds(pl.multiple_of(chip_k * rs, 128), rs), :]
        return ref.at[:, pl.ds(pl.multiple_of(chip_k * rs + half * (rs // 2), 128), rs // 2), :]
    cs = c // 4
    if half is None:
        return ref.at[:, :, pl.ds(pl.multiple_of(chip_k * cs, 128), cs)]
    return ref.at[:, pl.ds(pl.multiple_of(half * (r // 2), 128), r // 2), pl.ds(pl.multiple_of(chip_k * cs, 128), cs)]


HBM_SPEC = pl.BlockSpec(memory_space=pltpu.HBM)
SEM_SPEC = pl.BlockSpec(memory_space=pltpu.SEMAPHORE)
ANY_SPEC = pl.BlockSpec(memory_space=pl.ANY)
SPLIT_COPY_PARAMS = pltpu.CompilerParams(has_side_effects=pltpu.SideEffectType.DATAFLOW_SIDE_EFFECTING)


def _in_hbm(arrs):
    return [pltpu.with_memory_space_constraint(a, pltpu.HBM) for a in arrs]


def _cast_place(ws, layers, axes, place, name, after=None):
    n_arr = len(ws)
    in_specs, out_specs, fulls = [], [], []
    for w_stack, li, axis in zip(ws, layers, axes):
        _, r, c = w_stack.shape
        tr = r // 4
        fulls.append((1, 4 * r, c) if axis == 1 else (1, r, 4 * c))
        in_specs.append(pl.BlockSpec((None, tr, c), lambda i, p, li=li: (li, 2 * p[1] + i, 0)))
        if axis == 1:
            out_specs.append(pl.BlockSpec((None, tr, c), lambda i, p: (0, 4 * p[0] + 2 * p[1] + i, 0)))
        else:
            out_specs.append(pl.BlockSpec((None, tr, c), lambda i, p: (0, 2 * p[1] + i, p[0])))

    extra = [] if after is None else [after]

    def body(p_ref, *refs):
        for a in range(n_arr):
            refs[n_arr + len(extra) + a][...] = _bf(refs[a][...])

    return pl.pallas_call(
        body, name=name,
        grid_spec=pltpu.PrefetchScalarGridSpec(num_scalar_prefetch=1, grid=(2,), in_specs=in_specs + [ANY_SPEC] * len(extra),
                                               out_specs=out_specs),
        out_shape=[jax.ShapeDtypeStruct(f, BF16) for f in fulls],
        compiler_params=_params(1),
    )(place, *ws, *extra)


def _gather_start(lands, axes, name, after):
    n_arr = len(lands)
    fulls = [tuple(l.shape) for l in lands]

    def body(*refs):
        land = refs[:n_arr]
        send_sems, recv_sems = refs[n_arr + 1:n_arr + 3]
        token = refs[-1]
        x, y, c, chips = _place()
        k_me = 2 * x + y
        for a in range(n_arr):
            mine = _shard_region(land[a], fulls[a], axes[a], k_me, c)
            for j, chip in enumerate(chips):
                pltpu.make_async_remote_copy(
                    src_ref=mine, dst_ref=mine, send_sem=send_sems.at[a * 3 + j], recv_sem=recv_sems.at[a * 3 + j],
                    device_id=(*chip, c), device_id_type=MESH).start()
        token[...] = jnp.zeros_like(token)

    res = pl.pallas_call(
        body, name=name,
        out_shape=(pltpu.SemaphoreType.DMA((3 * n_arr,)), pltpu.SemaphoreType.DMA((3 * n_arr,)),
                   *[pltpu.HBM(f, BF16) for f in fulls], jax.ShapeDtypeStruct((8, 128), F32)),
        in_specs=[HBM_SPEC] * n_arr + [ANY_SPEC],
        out_specs=(SEM_SPEC, SEM_SPEC, *[HBM_SPEC] * n_arr, pl.BlockSpec(memory_space=pltpu.VMEM)),
        input_output_aliases={a: 2 + a for a in range(n_arr)},
        compiler_params=SPLIT_COPY_PARAMS,
    )(*_in_hbm(lands), after)
    return res[0], res[1], list(res[2:2 + n_arr]), res[-1]


def _gather_wait(send_sems, recv_sems, lands, axes, name, after):
    n_arr = len(lands)
    fulls = [tuple(l.shape) for l in lands]

    def body(*refs):
        land = refs[:n_arr]
        s_sems, r_sems = refs[n_arr:n_arr + 2]
        x, y, c, chips = _place()
        for a in range(n_arr):
            for j, chip in enumerate(chips):
                k_j = 2 * chip[0] + chip[1]
                got = _shard_region(land[a], fulls[a], axes[a], k_j, c)
                cp = pltpu.make_async_remote_copy(
                    src_ref=got, dst_ref=got, send_sem=s_sems.at[a * 3 + j], recv_sem=r_sems.at[a * 3 + j],
                    device_id=(x, y, c), device_id_type=MESH)
                cp.wait_send()
                cp.wait_recv()

    res = pl.pallas_call(
        body, name=name,
        out_shape=tuple(pltpu.HBM(f, BF16) for f in fulls),
        in_specs=[HBM_SPEC] * n_arr + [SEM_SPEC, SEM_SPEC] + [ANY_SPEC] * len(after),
        out_specs=tuple([HBM_SPEC] * n_arr),
        input_output_aliases={a: a for a in range(n_arr)},
        compiler_params=SPLIT_COPY_PARAMS,
    )(*lands, send_sems, recv_sems, *after)
    return list(res)


def _gather_share(lands, axes, name):
    n_arr = len(lands)
    fulls = [tuple(l.shape) for l in lands]

    def body(*refs):
        land_in, land = refs[:n_arr], refs[n_arr:2 * n_arr]
        send_sems, recv_sems = refs[2 * n_arr:]
        x, y, c, chips = _place()
        copies = []
        for a in range(n_arr):
            for j, k_j in enumerate([2 * chip[0] + chip[1] for chip in chips] + [2 * x + y]):
                cp = pltpu.make_async_remote_copy(
                    src_ref=_shard_region(land_in[a], fulls[a], axes[a], k_j, c),
                    dst_ref=_shard_region(land[a], fulls[a], axes[a], k_j, c),
                    send_sem=send_sems.at[a * 4 + j], recv_sem=recv_sems.at[a * 4 + j],
                    device_id=(x, y, 1 - c), device_id_type=MESH)
                cp.start()
                copies.append(cp)
        for cp in copies:
            cp.wait()

    return pl.pallas_call(
        body, name=name, out_shape=[jax.ShapeDtypeStruct(f, BF16) for f in fulls],
        in_specs=[ANY_SPEC] * n_arr, out_specs=[ANY_SPEC] * n_arr,
        input_output_aliases={a: a for a in range(n_arr)},
        scratch_shapes=[pltpu.SemaphoreType.DMA((4 * n_arr,)), pltpu.SemaphoreType.DMA((4 * n_arr,))],
    )(*lands)


def _scatter_shapes(grads, axes):
    out = []
    for g, ax in zip(grads, axes):
        shp = list(g.shape)
        shp[ax] //= 4
        out.append((3,) + tuple(shp[1:]))
    return out


def _scatter_start(grads, axes, name, after):
    n_arr = len(grads)
    shapes = _scatter_shapes(grads, axes)
    lands = [lax.empty(s, BF16) for s in shapes]

    def body(*refs):
        ins, land = refs[:n_arr], refs[n_arr:2 * n_arr]
        send_sems, recv_sems = refs[2 * n_arr + 1:2 * n_arr + 3]
        token = refs[-1]
        x, y, c, chips = _place()
        for a in range(n_arr):
            for j, chip in enumerate(chips):
                k_j = 2 * chip[0] + chip[1]
                pltpu.make_async_remote_copy(
                    src_ref=_shard_region(ins[a], grads[a].shape, axes[a], k_j), dst_ref=land[a].at[pl.ds(j, 1)],
                    send_sem=send_sems.at[a * 3 + j], recv_sem=recv_sems.at[a * 3 + j],
                    device_id=(*chip, c), device_id_type=MESH).start()
        token[...] = jnp.zeros_like(token)

    res = pl.pallas_call(
        body, name=name,
        out_shape=(pltpu.SemaphoreType.DMA((3 * n_arr,)), pltpu.SemaphoreType.DMA((3 * n_arr,)),
                   *[pltpu.HBM(g.shape, BF16) for g in grads], *[pltpu.HBM(s, BF16) for s in shapes],
                   jax.ShapeDtypeStruct((8, 128), F32)),
        in_specs=[HBM_SPEC] * (2 * n_arr) + [ANY_SPEC],
        out_specs=(SEM_SPEC, SEM_SPEC, *[HBM_SPEC] * (2 * n_arr), pl.BlockSpec(memory_space=pltpu.VMEM)),
        input_output_aliases={a: 2 + a for a in range(2 * n_arr)},
        compiler_params=SPLIT_COPY_PARAMS,
    )(*_in_hbm(grads), *_in_hbm(lands), after)
    return res[0], res[1], list(res[2:2 + n_arr]), list(res[2 + n_arr:2 + 2 * n_arr]), res[-1]


def _scatter_wait(send_sems, recv_sems, grads, lands, axes, name, after):
    n_arr = len(grads)

    def body(*refs):
        ins, land = refs[:n_arr], refs[n_arr:2 * n_arr]
        s_sems, r_sems = refs[2 * n_arr:2 * n_arr + 2]
        x, y, c, chips = _place()
        for a in range(n_arr):
            for j, chip in enumerate(chips):
                k_j = 2 * chip[0] + chip[1]
                cp = pltpu.make_async_remote_copy(
                    src_ref=_shard_region(ins[a], grads[a].shape, axes[a], k_j), dst_ref=land[a].at[pl.ds(j, 1)],
                    send_sem=s_sems.at[a * 3 + j], recv_sem=r_sems.at[a * 3 + j],
                    device_id=(x, y, c), device_id_type=MESH)
                cp.wait_send()
                cp.wait_recv()

    res = pl.pallas_call(
        body, name=name,
        out_shape=(*[pltpu.HBM(g.shape, BF16) for g in grads], *[pltpu.HBM(l.shape, BF16) for l in lands]),
        in_specs=[HBM_SPEC] * (2 * n_arr) + [SEM_SPEC, SEM_SPEC, ANY_SPEC],
        out_specs=tuple([HBM_SPEC] * (2 * n_arr)),
        input_output_aliases={a: a for a in range(2 * n_arr)},
        compiler_params=SPLIT_COPY_PARAMS,
    )(*grads, *lands, send_sems, recv_sems, after)
    return list(res[:n_arr]), list(res[n_arr:])


def _swap_start(arrs, name, after):
    n_arr = len(arrs)
    lands = [lax.empty(a.shape, a.dtype) for a in arrs]

    def body(*refs):
        ins, land = refs[:n_arr], refs[n_arr:2 * n_arr]
        send_sems, recv_sems = refs[2 * n_arr + 1:2 * n_arr + 3]
        token = refs[-1]
        x, y, c, _ = _place()
        for a in range(n_arr):
            pltpu.make_async_remote_copy(
                src_ref=ins[a], dst_ref=land[a], send_sem=send_sems.at[a], recv_sem=recv_sems.at[a],
                device_id=(x, y, 1 - c), device_id_type=MESH).start()
        token[...] = jnp.zeros_like(token)

    res = pl.pallas_call(
        body, name=name,
        out_shape=(pltpu.SemaphoreType.DMA((n_arr,)), pltpu.SemaphoreType.DMA((n_arr,)),
                   *[pltpu.HBM(a.shape, a.dtype) for a in arrs], *[pltpu.HBM(a.shape, a.dtype) for a in arrs],
                   jax.ShapeDtypeStruct((8, 128), F32)),
        in_specs=[HBM_SPEC] * (2 * n_arr) + [ANY_SPEC],
        out_specs=(SEM_SPEC, SEM_SPEC, *[HBM_SPEC] * (2 * n_arr), pl.BlockSpec(memory_space=pltpu.VMEM)),
        input_output_aliases={a: 2 + a for a in range(2 * n_arr)},
        compiler_params=SPLIT_COPY_PARAMS,
    )(*_in_hbm(arrs), *_in_hbm(lands), after)
    return res[0], res[1], list(res[2:2 + n_arr]), list(res[2 + n_arr:2 + 2 * n_arr]), res[-1]


def _swap_wait(send_sems, recv_sems, arrs, lands, name, after):
    n_arr = len(arrs)

    def body(*refs):
        ins, land = refs[:n_arr], refs[n_arr:2 * n_arr]
        s_sems, r_sems = refs[2 * n_arr:2 * n_arr + 2]
        x, y, c, _ = _place()
        for a in range(n_arr):
            cp = pltpu.make_async_remote_copy(
                src_ref=ins[a], dst_ref=land[a], send_sem=s_sems.at[a], recv_sem=r_sems.at[a],
                device_id=(x, y, c), device_id_type=MESH)
            cp.wait_send()
            cp.wait_recv()

    res = pl.pallas_call(
        body, name=name,
        out_shape=(*[pltpu.HBM(a.shape, a.dtype) for a in arrs], *[pltpu.HBM(a.shape, a.dtype) for a in arrs]),
        in_specs=[HBM_SPEC] * (2 * n_arr) + [SEM_SPEC, SEM_SPEC, ANY_SPEC],
        out_specs=tuple([HBM_SPEC] * (2 * n_arr)),
        input_output_aliases={a: a for a in range(2 * n_arr)},
        compiler_params=SPLIT_COPY_PARAMS,
    )(*arrs, *lands, send_sems, recv_sems, after)
    return list(res[:n_arr]), list(res[n_arr:])


def _swap_with_sibling(arrs, name):
    n_arr = len(arrs)

    def body(*refs):
        ins, outs = refs[:n_arr], refs[n_arr:2 * n_arr]
        send_sems, recv_sems = refs[2 * n_arr:]
        x, y, c, _ = _place()
        copies = []
        for a in range(n_arr):
            cp = pltpu.make_async_remote_copy(
                src_ref=ins[a], dst_ref=outs[a], send_sem=send_sems.at[a], recv_sem=recv_sems.at[a],
                device_id=(x, y, 1 - c), device_id_type=MESH)
            cp.start()
            copies.append(cp)
        for cp in copies:
            cp.wait()

    any_spec = pl.BlockSpec(memory_space=pl.ANY)
    return pl.pallas_call(
        body, name=name, out_shape=[jax.ShapeDtypeStruct(a.shape, a.dtype) for a in arrs],
        in_specs=[any_spec] * n_arr, out_specs=[any_spec] * n_arr,
        scratch_shapes=[pltpu.SemaphoreType.DMA((n_arr,)), pltpu.SemaphoreType.DMA((n_arr,))],
    )(*arrs)


def _mm_tn(a, b, name, out_dtype=BF16):
    L, m = a.shape
    n = b.shape[1]
    bm, bn = min(m, 1024), min(n, 1024)
    nk = 2 if (m // bm) * (n // bn) == 1 and L % 32 == 0 else 1
    bk = L // nk

    def body(a_ref, b_ref, o_ref, *acc):
        part = _dot_tn(_bf(a_ref[...]), _bf(b_ref[...]))
        if nk == 1:
            o_ref[...] = part.astype(out_dtype)
        else:
            @pl.when(pl.program_id(2) == 0)
            def _():
                acc[0][...] = part

            @pl.when(pl.program_id(2) == 1)
            def _():
                o_ref[...] = (acc[0][...] + part).astype(out_dtype)

    return pl.pallas_call(
        body, name=name, grid=(m // bm, n // bn, nk),
        in_specs=[pl.BlockSpec((bk, bm), lambda i, j, k: (k, i)), pl.BlockSpec((bk, bn), lambda i, j, k: (k, j))],
        out_specs=pl.BlockSpec((bm, bn), lambda i, j, k: (i, j)),
        out_shape=jax.ShapeDtypeStruct((m, n), out_dtype),
        scratch_shapes=[pltpu.VMEM((bm, bn), F32)] if nk > 1 else [],
        compiler_params=_params(3),
    )(a, b)


def _mm_tn_blocks(a, b, wa, wb, name, after=None):
    L = a.shape[0]
    nb = a.shape[1] // wa
    bk = min(L, 2048)
    nk = L // bk

    def body(a_ref, b_ref, o_ref):
        @pl.when(pl.program_id(1) == 0)
        def _():
            o_ref[...] = jnp.zeros_like(o_ref)

        o_ref[...] += _dot_tn(_bf(a_ref[...]), _bf(b_ref[...]))

    call, tail = _call_after(
        after, body, 2, name=name, grid=(nb, nk),
        in_specs=[pl.BlockSpec((bk, wa), lambda j, k: (k, j)), pl.BlockSpec((bk, wb), lambda j, k: (k, j))],
        out_specs=pl.BlockSpec((None, wa, wb), lambda j, k: (j, 0, 0)),
        out_shape=jax.ShapeDtypeStruct((nb, wa, wb), F32),
        compiler_params=_params(2),
    )
    return call(a, b, *tail)


def _sum_parts(parts, owns, axes, chip, name):
    n_arr = len(parts)
    steps = 4
    in_specs, out_specs, shapes = [], [], []
    for part, axis in zip(parts, axes):
        _, r, c = part.shape
        tr = r // steps
        shapes.append((r, c))
        in_specs.append(pl.BlockSpec((3, tr, c), lambda i, k: (0, i, 0)))
        out_specs.append(pl.BlockSpec((tr, c), lambda i, k: (i, 0)))
    for part, axis in zip(parts, axes):
        _, r, c = part.shape
        tr = r // steps
        if axis == 1:
            in_specs.append(pl.BlockSpec((None, tr, c), lambda i, k: (0, steps * k[0] + i, 0)))
        else:
            in_specs.append(pl.BlockSpec((None, tr, c), lambda i, k: (0, i, k[0])))

    def body(k_ref, *refs):
        for a in range(n_arr):
            p = refs[a][...].astype(F32)
            refs[2 * n_arr + a][...] = ((p[0] + p[1]) + p[2]) + refs[n_arr + a][...].astype(F32)

    return pl.pallas_call(
        body, name=name,
        grid_spec=pltpu.PrefetchScalarGridSpec(num_scalar_prefetch=1, grid=(steps,), in_specs=in_specs, out_specs=out_specs),
        out_shape=[jax.ShapeDtypeStruct(sh, F32) for sh in shapes],
        compiler_params=_params(1),
    )(chip, *parts, *owns)


def _adamw(w, g_parts, m, v, name):
    n_g = len(g_parts)
    if w.ndim == 2:
        r, c = w.shape
        tr = r
        for cand in (512, 256, 128, 64, 32, 16, 8):
            if r % cand == 0 and cand * c * 4 <= (2 << 20):
                tr = cand
                break
        spec = pl.BlockSpec((tr, c), lambda i: (i, 0))
        tiling = dict(grid=(r // tr,), in_specs=[spec] * (3 + n_g), out_specs=[spec] * 4, compiler_params=_params(1))
    else:
        tiling = dict(compiler_params=pltpu.CompilerParams(vmem_limit_bytes=48 << 20))

    def body(*refs):
        w_ref, g_refs, m_ref, v_ref = refs[0], refs[1:1 + n_g], refs[1 + n_g], refs[2 + n_g]
        g = g_refs[0][...]
        for gr in g_refs[1:]:
            g = g + gr[...]
        _adamw_update(g, w_ref, m_ref, v_ref, *refs[3 + n_g:])

    return pl.pallas_call(body, name=name, out_shape=[jax.ShapeDtypeStruct(w.shape, F32)] * 4, **tiling)(w, *g_parts, m, v)


def _adamw_update(g, w_ref, m_ref, v_ref, g_out, d_out, m_out, v_out):
    m_new = ADAM_B1 * m_ref[...] + (1.0 - ADAM_B1) * g
    v_new = ADAM_B2 * v_ref[...] + (1.0 - ADAM_B2) * (g * g)
    m_hat = m_new * (1.0 / (1.0 - ADAM_B1 ** ADAM_STEP))
    v_hat = v_new * (1.0 / (1.0 - ADAM_B2 ** ADAM_STEP))
    g_out[...] = g
    d_out[...] = -ADAM_LR * (m_hat / (jnp.sqrt(v_hat) + ADAM_EPS) + ADAM_WD * w_ref[...])
    m_out[...] = m_new
    v_out[...] = v_new


def _adamw_many(ws, gs, ms, vs, name):
    n = len(ws)

    def body(*refs):
        for k in range(n):
            _adamw_update(refs[n + k][...], refs[k], refs[2 * n + k], refs[3 * n + k],
                          refs[4 * n + k], refs[5 * n + k], refs[6 * n + k], refs[7 * n + k])

    outs = pl.pallas_call(body, name=name, out_shape=[jax.ShapeDtypeStruct(t.shape, F32) for t in ws] * 4,
                          compiler_params=pltpu.CompilerParams(vmem_limit_bytes=56 << 20))(*ws, *gs, *ms, *vs)
    return [outs[part * n:(part + 1) * n] for part in range(4)]


def _adamw_layers(w, q_mine, q_sib, m, v, name, after=None):
    n, r, c = w.shape
    tr = r
    for cand in (512, 256, 128, 64, 32, 16, 8):
        if r % cand == 0 and cand * c * 4 <= (1 << 20):
            tr = cand
            break

    def body(*refs):
        w_ref, qm, qs, m_ref, v_ref = refs[0], refs[1:1 + n], refs[1 + n:1 + 2 * n], refs[1 + 2 * n], refs[2 + 2 * n]
        layer = pl.program_id(0)
        g = qm[0][...] + qs[0][...]
        for k in range(1, n):
            g = jnp.where(layer == k, qm[k][...] + qs[k][...], g)
        _adamw_update(g, w_ref, m_ref, v_ref, *refs[3 + 2 * n:])

    stacked = pl.BlockSpec((None, tr, c), lambda l, i: (l, i, 0))
    per_layer = [pl.BlockSpec((tr, c), lambda l, i, k=k: (jnp.where(l == k, i, 0), 0)) for k in range(n)]
    call, tail = _call_after(
        after, body, 3 + 2 * n, name=name, grid=(n, r // tr),
        in_specs=[stacked] + per_layer + per_layer + [stacked, stacked], out_specs=[stacked] * 4,
        out_shape=[jax.ShapeDtypeStruct(w.shape, F32)] * 4,
        compiler_params=_params(2),
    )
    return call(w, *q_mine, *q_sib, m, v, *tail)


def _ada_fwd(c16, ada_w, ada_b_cols):
    cols = ada_w.shape[2]

    def body(c_ref, w_ref, b_ref, o_ref):
        cv = c_ref[...]
        ca = _bf(cv * jax.nn.sigmoid(cv))
        o_ref[...] = _dot(ca, _bf(w_ref[...])) + b_ref[...]

    return pl.pallas_call(
        body, name="ada_fwd", grid=(DEPTH,),
        in_specs=[_whole((16, D)), pl.BlockSpec((None, D, cols), lambda i: (i, 0, 0)),
                  pl.BlockSpec((None, 1, cols), lambda i: (i, 0, 0))],
        out_specs=pl.BlockSpec((None, 16, cols), lambda i: (i, 0, 0)),
        out_shape=jax.ShapeDtypeStruct((DEPTH, 16, cols), F32),
        compiler_params=_params(1),
    )(c16, ada_w, ada_b_cols)


def _ada_bwd(c16, dmod16):
    cols = dmod16.shape[2]

    def body(c_ref, d_ref, o_ref):
        cv = c_ref[...]
        ca = _bf(cv * jax.nn.sigmoid(cv))
        o_ref[...] = _dot_tn(ca, _bf(d_ref[...]))

    return pl.pallas_call(
        body, name="ada_bwd", grid=(DEPTH,),
        in_specs=[_whole((16, D)), pl.BlockSpec((None, 16, cols), lambda i: (i, 0, 0))],
        out_specs=pl.BlockSpec((None, D, cols), lambda i: (i, 0, 0)),
        out_shape=jax.ShapeDtypeStruct((DEPTH, D, cols), F32),
        compiler_params=_params(1),
    )(c16, dmod16)


def _mod_bwd(vs_mix, vs_ffn, pv):
    def body(m_ref, f_ref, pv_ref, o_ref):
        for i in range(DEPTH):
            vm, vf, p = m_ref[i], f_ref[i], pv_ref[i]
            o_ref[i] = jnp.concatenate([
                vm[2:3], vm[1:2] * p[R_N1:R_N1 + 1], vm[0:1],
                vf[2:3], vf[1:2] * p[R_N2:R_N2 + 1], vf[0:1],
                vm[1:2] * (1.0 + p[R_SC1:R_SC1 + 1]), vf[1:2] * (1.0 + p[R_SC2:R_SC2 + 1])], axis=0)

    return pl.pallas_call(body, name="mod_bwd", out_shape=jax.ShapeDtypeStruct((DEPTH, 8, D), F32))(vs_mix, vs_ffn, pv)


def _ffn_fwd(x1, pv, w1, w2, layer, tm, after=None):
    L = x1.shape[0]
    dff = w1.shape[2]

    def body(x1_ref, pv_ref, w1_ref, w2_ref, x2_ref, h2_ref, a_ref, f_ref):
        x1v, p = x1_ref[...], pv_ref[...]
        h2, _, _ = _norm_mod(x1v, p[R_N2:R_N2 + 1], p[R_SC2:R_SC2 + 1], p[R_SH2:R_SH2 + 1])
        hb = _bf(h2)
        h2_ref[...] = hb
        a = _dot(hb, w1_ref[...])
        a_ref[...] = a
        ra = jnp.maximum(a, 0.0)
        f = _dot(_bf(ra * ra), w2_ref[...])
        f_ref[...] = f
        x2_ref[...] = x1v + p[R_G2:R_G2 + 1] * f

    call, tail = _call_after(
        after, body, 4, name=f"ffn_fwd{layer}", grid=(L // tm,),
        in_specs=[_rows(tm, D), pl.BlockSpec((None, 8, D), lambda i: (layer, 0, 0)), _layer_w(D, dff, 0), _layer_w(dff, D, 0)],
        out_specs=[_rows(tm, D), _rows(tm, D), _rows(tm, dff), _rows(tm, D)],
        out_shape=[jax.ShapeDtypeStruct((L, D), F32), jax.ShapeDtypeStruct((L, D), BF16),
                   jax.ShapeDtypeStruct((L, dff), F32), jax.ShapeDtypeStruct((L, D), F32)],
        compiler_params=_params(1, 56),
    )
    return call(x1, pv, w1, w2, *tail)


def _ffn_bwd(dx2, x1, a, f, pv, w1, w2, layer, tm, after=None):
    L = x1.shape[0]
    dff = w1.shape[2]
    extra = [] if after is None else [pl.BlockSpec(memory_space=pl.ANY)]
    extra_args = [] if after is None else [after]

    def body(dx2_ref, x1_ref, a_ref, f_ref, pv_ref, w1_ref, w2_ref, *rest):
        dx1_ref, p_ref, da_ref, df_ref, vs_ref = rest[len(extra):]

        @pl.when(pl.program_id(0) == 0)
        def _():
            vs_ref[...] = jnp.zeros_like(vs_ref)

        dx2v, p = dx2_ref[...], pv_ref[...]
        dfb = _bf(dx2v * p[R_G2:R_G2 + 1])
        df_ref[...] = dfb
        vs_ref[0:1, :] += _sum0(dx2v * f_ref[...])
        dp = _dot_nt(dfb, w2_ref[...])
        ra = jnp.maximum(a_ref[...], 0.0)
        p_ref[...] = _bf(ra * ra)
        dab = _bf(dp * (2.0 * ra))
        da_ref[...] = dab
        dh2 = _dot_nt(dab, w1_ref[...])
        _, xn, r = _norm_mod(x1_ref[...], p[R_N2:R_N2 + 1], p[R_SC2:R_SC2 + 1], p[R_SH2:R_SH2 + 1])
        dx1_ref[...] = dx2v + _norm_mod_bwd(dh2, xn, r, p[R_N2:R_N2 + 1], p[R_SC2:R_SC2 + 1])
        vs_ref[1:2, :] += _sum0(dh2 * xn)
        vs_ref[2:3, :] += _sum0(dh2)

    return pl.pallas_call(
        body, name=f"ffn_bwd{layer}", grid=(L // tm,),
        in_specs=[_rows(tm, D), _rows(tm, D), _rows(tm, dff), _rows(tm, D),
                  pl.BlockSpec((None, 8, D), lambda i: (layer, 0, 0)), _layer_w(D, dff, 0), _layer_w(dff, D, 0)] + extra,
        out_specs=[_rows(tm, D), _rows(tm, dff), _rows(tm, dff), _rows(tm, D), _whole((8, D))],
        out_shape=[jax.ShapeDtypeStruct((L, D), F32), jax.ShapeDtypeStruct((L, dff), BF16),
                   jax.ShapeDtypeStruct((L, dff), BF16), jax.ShapeDtypeStruct((L, D), BF16),
                   jax.ShapeDtypeStruct((8, D), F32)],
        compiler_params=_params(1, 56),
    )(dx2, x1, a, f, pv, w1, w2, *extra_args)


def _conv_fwd(x, pv, w_in, w_out, cw, layer, j, tm, after=None):
    L = x.shape[0]

    def body(x_ref, pv_ref, win_ref, wout_ref, cw_ref, x1_ref, h_ref, bcx_ref, conv_ref, q_ref, y_ref, carry):
        @pl.when(pl.program_id(0) == 0)
        def _():
            carry[...] = jnp.zeros_like(carry)

        xv, p, cwv = x_ref[...], pv_ref[...], cw_ref[...]
        h, _, _ = _norm_mod(xv, p[R_N1:R_N1 + 1], p[R_SC1:R_SC1 + 1], p[R_SH1:R_SH1 + 1])
        hb = _bf(h)
        h_ref[...] = hb
        bcx = _dot(hb, win_ref[...])
        bcx_ref[...] = bcx
        z = bcx[:, D:2 * D] * bcx[:, 2 * D:]
        prev8 = carry[...]
        conv = cwv[0:1] * _shift_down(z, prev8, 2) + cwv[1:2] * _shift_down(z, prev8, 1) + cwv[2:3] * z + cwv[3:4]
        conv_ref[...] = conv
        qb = _bf(bcx[:, :D] * conv)
        q_ref[...] = qb
        y = _dot(qb, wout_ref[...])
        y_ref[...] = y
        x1_ref[...] = xv + p[R_G1:R_G1 + 1] * y
        carry[...] = z[tm - 8:tm]

    call, tail = _call_after(
        after, body, 5, name=f"conv_fwd{layer}", grid=(L // tm,),
        in_specs=[_rows(tm, D), pl.BlockSpec((None, 8, D), lambda i: (layer, 0, 0)), _layer_w(D, 3 * D, 0), _layer_w(D, D, 0),
                  pl.BlockSpec((None, 8, D), lambda i: (j, 0, 0))],
        out_specs=[_rows(tm, D), _rows(tm, D), _rows(tm, 3 * D), _rows(tm, D), _rows(tm, D), _rows(tm, D)],
        out_shape=[jax.ShapeDtypeStruct((L, D), F32), jax.ShapeDtypeStruct((L, D), BF16), jax.ShapeDtypeStruct((L, 3 * D), F32),
                   jax.ShapeDtypeStruct((L, D), F32), jax.ShapeDtypeStruct((L, D), BF16), jax.ShapeDtypeStruct((L, D), F32)],
        scratch_shapes=[pltpu.VMEM((8, D), F32)],
        compiler_params=_params(1, 56),
    )
    return call(x, pv, w_in, w_out, cw, *tail)


def _conv_bwd(dx1, x, y, bcx, conv, pv, w_in, w_out, cw, layer, j, tm, after=None):
    L = x.shape[0]
    nt = L // tm

    def body(dx1_ref, x_ref, y_ref, bcx_ref, conv_ref, halo_ref, pv_ref, win_ref, wout_ref, cw_ref,
             dx_ref, dbcx_ref, dy_ref, vs_ref, carry):
        gi = pl.program_id(0)
        tile = nt - 1 - gi

        @pl.when(gi == 0)
        def _():
            vs_ref[...] = jnp.zeros_like(vs_ref)
            carry[...] = jnp.zeros_like(carry)

        dx1v, p, cwv = dx1_ref[...], pv_ref[...], cw_ref[...]
        dyb = _bf(dx1v * p[R_G1:R_G1 + 1])
        dy_ref[...] = dyb
        vs_ref[0:1, :] += _sum0(dx1v * y_ref[...])
        dq = _dot_nt(dyb, wout_ref[...])
        bcx = bcx_ref[...]
        b, cg, xh = bcx[:, :D], bcx[:, D:2 * D], bcx[:, 2 * D:]
        db = dq * conv_ref[...]
        dc = dq * b
        z = cg * xh
        halo = halo_ref[...]
        zprev = jnp.where(tile > 0, halo[:, D:2 * D] * halo[:, 2 * D:], 0.0)
        vs_ref[3:4, :] += _sum0(dc * _shift_down(z, zprev, 2))
        vs_ref[4:5, :] += _sum0(dc * _shift_down(z, zprev, 1))
        vs_ref[5:6, :] += _sum0(dc * z)
        vs_ref[6:7, :] += _sum0(dc)
        next8 = carry[...]
        dz = cwv[2:3] * dc + cwv[1:2] * _shift_up(dc, next8, 1) + cwv[0:1] * _shift_up(dc, next8, 2)
        dbb, dcgb, dxhb = _bf(db), _bf(dz * xh), _bf(dz * cg)
        dbcx_ref[:, 0:D] = dbb
        dbcx_ref[:, D:2 * D] = dcgb
        dbcx_ref[:, 2 * D:3 * D] = dxhb
        dh = (_dot_nt(dbb, win_ref[:, 0:D]) + _dot_nt(dcgb, win_ref[:, D:2 * D])) + _dot_nt(dxhb, win_ref[:, 2 * D:3 * D])
        _, xn, r = _norm_mod(x_ref[...], p[R_N1:R_N1 + 1], p[R_SC1:R_SC1 + 1], p[R_SH1:R_SH1 + 1])
        dx_ref[...] = dx1v + _norm_mod_bwd(dh, xn, r, p[R_N1:R_N1 + 1], p[R_SC1:R_SC1 + 1])
        vs_ref[1:2, :] += _sum0(dh * xn)
        vs_ref[2:3, :] += _sum0(dh)
        carry[...] = dc[0:8]

    halo_spec = pl.BlockSpec((8, 3 * D), lambda i: (jnp.maximum((nt - 1 - i) * (tm // 8) - 1, 0), 0))
    call, tail = _call_after(
        after, body, 10, name=f"conv_bwd{layer}", grid=(nt,),
        in_specs=[_rows(tm, D, nt), _rows(tm, D, nt), _rows(tm, D, nt), _rows(tm, 3 * D, nt), _rows(tm, D, nt), halo_spec,
                  pl.BlockSpec((None, 8, D), lambda i: (layer, 0, 0)), _layer_w(D, 3 * D, 0), _layer_w(D, D, 0),
                  pl.BlockSpec((None, 8, D), lambda i: (j, 0, 0))],
        out_specs=[_rows(tm, D, nt), _rows(tm, 3 * D, nt), _rows(tm, D, nt), _whole((8, D))],
        out_shape=[jax.ShapeDtypeStruct((L, D), F32), jax.ShapeDtypeStruct((L, 3 * D), BF16),
                   jax.ShapeDtypeStruct((L, D), BF16), jax.ShapeDtypeStruct((8, D), F32)],
        scratch_shapes=[pltpu.VMEM((8, D), F32)],
        compiler_params=_params(1, 56),
    )
    return call(dx1, x, y, bcx, conv, bcx, pv, w_in, w_out, cw, *tail)


def _s5_discretize(a_re, a_im, log_dt, bt_re, bt_im):
    dt = jnp.exp(log_dt)
    mag = jnp.exp(a_re * dt)
    abar_re = mag * jnp.cos(a_im * dt)
    abar_im = mag * jnp.sin(a_im * dt)
    den = a_re * a_re + a_im * a_im
    nr = abar_re - 1.0
    ni = abar_im
    f_re = (nr * a_re + ni * a_im) / den
    f_im = (ni * a_re - nr * a_im) / den
    bbar_re = f_re * bt_re - f_im * bt_im
    bbar_im = f_re * bt_im + f_im * bt_re
    return abar_re, abar_im, bbar_re, bbar_im


def _s5_params_fwd(a_re, a_im, log_dt, bt_re, bt_im, after=None):
    def body(ar, ai, ld, br, bi, o_ar, o_ai, o_br, o_bi):
        r = _s5_discretize(ar[...], ai[...], ld[...], br[...], bi[...])
        o_ar[...], o_ai[...], o_br[...], o_bi[...] = r

    gp = jax.ShapeDtypeStruct((S5_G, S5_P), F32)
    hgp = jax.ShapeDtypeStruct((S5_H, S5_G, S5_P), F32)
    call, tail = _call_after(after, body, 5, name="s5_params_fwd", out_shape=[gp, gp, hgp, hgp],
                             in_specs=[pl.BlockSpec(memory_space=pltpu.VMEM)] * 5)
    return call(a_re, a_im, log_dt, bt_re, bt_im, *tail)


def _s5_params_bwd(a_re, a_im, log_dt, bt_re, bt_im, d_ar, d_ai, d_br, d_bi):
    def body(ar, ai, ld, br, bi, gar, gai, gbr, gbi, o_ar, o_ai, o_ld, o_br, o_bi):
        _, vjp = jax.vjp(_s5_discretize, ar[...], ai[...], ld[...], br[...], bi[...])
        r = vjp((gar[...], gai[...], gbr[...], gbi[...]))
        o_ar[...], o_ai[...], o_ld[...], o_br[...], o_bi[...] = r

    gp = jax.ShapeDtypeStruct((S5_G, S5_P), F32)
    hgp = jax.ShapeDtypeStruct((S5_H, S5_G, S5_P), F32)
    return pl.pallas_call(body, name="s5_params_bwd", out_shape=[gp, gp, jax.ShapeDtypeStruct((S5_G, 1), F32), hgp, hgp])(
        a_re, a_im, log_dt, bt_re, bt_im, d_ar, d_ai, d_br, d_bi)


NSEG = 8
SCAN_LANES = 1024


def _to_segments(x):
    n, c = x.shape
    return x.reshape(NSEG, n // NSEG, c).transpose(1, 0, 2).reshape(n, c)


def _from_segments(x):
    n, c = x.shape
    return x.reshape(n // NSEG, NSEG, c).transpose(1, 0, 2).reshape(n, c)


def _segment_scan(re_ref, im_ref, st_re, st_im, a_re, a_im, n_slabs, adjoint, write):
    for q in range(NSTATE // SCAN_LANES):
        ls = slice(q * SCAN_LANES, (q + 1) * SCAN_LANES)
        ar = jnp.broadcast_to(a_re[:, ls], (8, SCAN_LANES))
        ai = jnp.broadcast_to(a_im[:, ls], (8, SCAN_LANES))

        def step(k, carry, ls=ls, ar=ar, ai=ai):
            s_r, s_i = carry
            slab = (n_slabs - 1 - k) if adjoint else k
            rows = pl.ds(pl.multiple_of(slab * 8, 8), 8)
            b_r, b_i = re_ref[rows, ls], im_ref[rows, ls]
            if adjoint:
                n_r = b_r + ar * s_r + ai * s_i
                n_i = b_i - ai * s_r + ar * s_i
            else:
                n_r = ar * s_r - ai * s_i + b_r
                n_i = ar * s_i + ai * s_r + b_i
            if write:
                re_ref[rows, ls] = n_r
                im_ref[rows, ls] = n_i
            return n_r, n_i

        s_r, s_i = lax.fori_loop(0, n_slabs, step, (st_re[:, ls], st_im[:, ls]), unroll=4)
        st_re[:, ls] = s_r
        st_im[:, ls] = s_i


def _s5_segment_states(e_re, e_im, ar, ai, seg_len, adjoint):
    def body(ere_ref, eim_ref, ar_ref, ai_ref, ore_ref, oim_ref):
        p_r, p_i = ar_ref[...], ai_ref[...]
        if adjoint:
            p_i = -p_i
        acc_r, acc_i = jnp.ones_like(p_r), jnp.zeros_like(p_r)
        n = seg_len
        while n:
            if n & 1:
                acc_r, acc_i = acc_r * p_r - acc_i * p_i, acc_r * p_i + acc_i * p_r
            n >>= 1
            if n:
                p_r, p_i = p_r * p_r - p_i * p_i, 2.0 * p_r * p_i
        e_r, e_i = ere_ref[...], eim_ref[...]
        s_r, s_i = jnp.zeros_like(acc_r), jnp.zeros_like(acc_r)
        rows_r, rows_i = [None] * NSEG, [None] * NSEG
        order = range(NSEG - 1, -1, -1) if adjoint else range(NSEG)
        for j in order:
            rows_r[j], rows_i[j] = s_r, s_i
            s_r, s_i = (acc_r * s_r - acc_i * s_i + e_r[j:j + 1], acc_r * s_i + acc_i * s_r + e_i[j:j + 1])
        ore_ref[...] = jnp.concatenate(rows_r, axis=0)
        oim_ref[...] = jnp.concatenate(rows_i, axis=0)

    st = jax.ShapeDtypeStruct((NSEG, NSTATE), F32)
    return pl.pallas_call(body, name="s5_segment_states_bwd" if adjoint else "s5_segment_states_fwd", out_shape=[st, st])(
        e_re, e_im, ar, ai)


def _s5_fwd_ends(x, pv, w_in, b_re, b_im, ar, ai, layer, tm, after=None):
    L = x.shape[0]

    def body(x_ref, pv_ref, win_ref, bre_ref, bim_ref, ar_ref, ai_ref, h_ref, u_ref, ere_ref, eim_ref, bu_re, bu_im):
        @pl.when(pl.program_id(0) == 0)
        def _():
            ere_ref[...] = jnp.zeros_like(ere_ref)
            eim_ref[...] = jnp.zeros_like(eim_ref)

        p = pv_ref[...]
        h, _, _ = _norm_mod(x_ref[...], p[R_N1:R_N1 + 1], p[R_SC1:R_SC1 + 1], p[R_SH1:R_SH1 + 1])
        hb = _bf(h)
        h_ref[...] = hb
        u = _dot(hb, win_ref[...])
        u_ref[...] = u
        ub = _bf(u)
        for k in range(S5_NB):
            uk = ub[:, k * S5_BH:(k + 1) * S5_BH]
            bu_re[:, k * S5_BP:(k + 1) * S5_BP] = _dot(uk, bre_ref[k])
            bu_im[:, k * S5_BP:(k + 1) * S5_BP] = _dot(uk, bim_ref[k])
        _segment_scan(bu_re, bu_im, ere_ref, eim_ref, ar_ref[...], ai_ref[...], tm // 8, adjoint=False, write=False)

    call, tail = _call_after(
        after, body, 7, name="s5_fwd_ends", grid=(L // tm,),
        in_specs=[_rows(tm, D), pl.BlockSpec((None, 8, D), lambda i: (layer, 0, 0)), _layer_w(D, D, 0),
                  _const_w((S5_NB, S5_BH, S5_BP)), _const_w((S5_NB, S5_BH, S5_BP)), _whole((1, NSTATE)), _whole((1, NSTATE))],
        out_specs=[_rows(tm, D), _rows(tm, D), _whole((NSEG, NSTATE)), _whole((NSEG, NSTATE))],
        out_shape=[jax.ShapeDtypeStruct((L, D), BF16), jax.ShapeDtypeStruct((L, D), F32),
                   jax.ShapeDtypeStruct((NSEG, NSTATE), F32), jax.ShapeDtypeStruct((NSEG, NSTATE), F32)],
        scratch_shapes=[pltpu.VMEM((tm, NSTATE), F32), pltpu.VMEM((tm, NSTATE), F32)],
        compiler_params=_params(1, 56),
    )
    return call(x, pv, w_in, b_re, b_im, ar, ai, *tail)


def _s5_fwd_out(x, u, pv, b_re, b_im, s0_re, s0_im, ar, ai, c_re, c_im, dvec, glu_w, glu_b, w_out, layer, tm):
    L = x.shape[0]

    def body(x_ref, u_ref, pv_ref, bre_ref, bim_ref, s0re_ref, s0im_ref, ar_ref, ai_ref, cre_ref, cim_ref, d_ref, gw_ref,
             gb_ref, wout_ref, x1_ref, sre_ref, sim_ref, y1_ref, zg_ref, y3_ref, y_ref, st_re, st_im):
        @pl.when(pl.program_id(0) == 0)
        def _():
            st_re[...] = s0re_ref[...]
            st_im[...] = s0im_ref[...]

        p = pv_ref[...]
        uv = u_ref[...]
        ub = _bf(uv)
        for k in range(S5_NB):
            uk = ub[:, k * S5_BH:(k + 1) * S5_BH]
            sre_ref[:, k * S5_BP:(k + 1) * S5_BP] = _dot(uk, bre_ref[k])
            sim_ref[:, k * S5_BP:(k + 1) * S5_BP] = _dot(uk, bim_ref[k])
        _segment_scan(sre_ref, sim_ref, st_re, st_im, ar_ref[...], ai_ref[...], tm // 8, adjoint=False, write=True)
        parts = []
        for k in range(S5_NB):
            sl = slice(k * S5_BP, (k + 1) * S5_BP)
            parts.append(_dot(_bf(sre_ref[:, sl]), cre_ref[k]) - _dot(_bf(sim_ref[:, sl]), cim_ref[k]))
        y1 = jnp.concatenate(parts, axis=1) + d_ref[...] * uv
        y1_ref[...] = y1
        y2 = jax.nn.gelu(y1)
        zg = _dot(_bf(y2), gw_ref[...]) + gb_ref[...]
        zg_ref[...] = zg
        y3b = _bf(y2 * jax.nn.sigmoid(zg))
        y3_ref[...] = y3b
        y = _dot(y3b, wout_ref[...])
        y_ref[...] = y
        x1_ref[...] = x_ref[...] + p[R_G1:R_G1 + 1] * y

    return pl.pallas_call(
        body, name="s5_fwd_out", grid=(L // tm,),
        in_specs=[_rows(tm, D), _rows(tm, D), pl.BlockSpec((None, 8, D), lambda i: (layer, 0, 0)),
                  _const_w((S5_NB, S5_BH, S5_BP)), _const_w((S5_NB, S5_BH, S5_BP)),
                  _whole((NSEG, NSTATE)), _whole((NSEG, NSTATE)), _whole((1, NSTATE)), _whole((1, NSTATE)),
                  _const_w((S5_NB, S5_BP, S5_BH)), _const_w((S5_NB, S5_BP, S5_BH)), _whole((1, D)),
                  _layer_w(D, D, 0), _whole((1, D)), _layer_w(D, D, 0)],
        out_specs=[_rows(tm, D), _rows(tm, NSTATE), _rows(tm, NSTATE), _rows(tm, D), _rows(tm, D), _rows(tm, D), _rows(tm, D)],
        out_shape=[jax.ShapeDtypeStruct((L, D), F32), jax.ShapeDtypeStruct((L, NSTATE), F32), jax.ShapeDtypeStruct((L, NSTATE), F32),
                   jax.ShapeDtypeStruct((L, D), F32), jax.ShapeDtypeStruct((L, D), F32),
                   jax.ShapeDtypeStruct((L, D), BF16), jax.ShapeDtypeStruct((L, D), F32)],
        scratch_shapes=[pltpu.VMEM((NSEG, NSTATE), F32), pltpu.VMEM((NSEG, NSTATE), F32)],
        compiler_params=_params(1, 56),
    )(x, u, pv, b_re, b_im, s0_re, s0_im, ar, ai, c_re, c_im, dvec, glu_w, glu_b, w_out)


def _s5_bwd_ends(dx1, y, y1, zg, u, pv, c_re, c_im, ar, ai, dvec, glu_w, w_out, layer, tm, after=None):
    L = dx1.shape[0]
    nt = L // tm

    def body(dx1_ref, y_ref, y1_ref, zg_ref, u_ref, pv_ref, cre_ref, cim_ref, ar_ref, ai_ref, d_ref, gw_ref, wout_ref,
             dy_ref, y2_ref, dzg_ref, dy1_ref, dus_ref, ere_ref, eim_ref, vs_ref, g_re, g_im):
        @pl.when(pl.program_id(0) == 0)
        def _():
            vs_ref[...] = jnp.zeros_like(vs_ref)
            ere_ref[...] = jnp.zeros_like(ere_ref)
            eim_ref[...] = jnp.zeros_like(eim_ref)

        dx1v, p = dx1_ref[...], pv_ref[...]
        dyb = _bf(dx1v * p[R_G1:R_G1 + 1])
        dy_ref[...] = dyb
        vs_ref[0:1, :] += _sum0(dx1v * y_ref[...])
        dy3 = _dot_nt(dyb, wout_ref[...])
        y2, gelu_vjp = jax.vjp(jax.nn.gelu, y1_ref[...])
        y2_ref[...] = _bf(y2)
        gate = jax.nn.sigmoid(zg_ref[...])
        dzg = dy3 * y2 * gate * (1.0 - gate)
        dzgb = _bf(dzg)
        dzg_ref[...] = dzgb
        vs_ref[1:2, :] += _sum0(dzg)
        dy2 = dy3 * gate + _dot_nt(dzgb, gw_ref[...])
        dy1 = gelu_vjp(dy2)[0]
        vs_ref[2:3, :] += _sum0(dy1 * u_ref[...])
        dus_ref[...] = dy1 * d_ref[...]
        dy1b = _bf(dy1)
        dy1_ref[...] = dy1b
        for k in range(S5_NB):
            dk = dy1b[:, k * S5_BH:(k + 1) * S5_BH]
            g_re[:, k * S5_BP:(k + 1) * S5_BP] = _dot_nt(dk, cre_ref[k])
            g_im[:, k * S5_BP:(k + 1) * S5_BP] = -_dot_nt(dk, cim_ref[k])
        _segment_scan(g_re, g_im, ere_ref, eim_ref, ar_ref[...], ai_ref[...], tm // 8, adjoint=True, write=False)

    call, tail = _call_after(
        after, body, 13, name="s5_bwd_ends", grid=(nt,),
        in_specs=[_rows(tm, D, nt)] * 5 + [pl.BlockSpec((None, 8, D), lambda i: (layer, 0, 0)),
                  _const_w((S5_NB, S5_BP, S5_BH)), _const_w((S5_NB, S5_BP, S5_BH)), _whole((1, NSTATE)), _whole((1, NSTATE)),
                  _whole((1, D)), _layer_w(D, D, 0), _layer_w(D, D, 0)],
        out_specs=[_rows(tm, D, nt)] * 5 + [_whole((NSEG, NSTATE)), _whole((NSEG, NSTATE)), _whole((8, D))],
        out_shape=[jax.ShapeDtypeStruct((L, D), BF16)] * 4 + [jax.ShapeDtypeStruct((L, D), F32),
                   jax.ShapeDtypeStruct((NSEG, NSTATE), F32), jax.ShapeDtypeStruct((NSEG, NSTATE), F32),
                   jax.ShapeDtypeStruct((8, D), F32)],
        scratch_shapes=[pltpu.VMEM((tm, NSTATE), F32), pltpu.VMEM((tm, NSTATE), F32)],
        compiler_params=_params(1, 56),
    )
    return call(dx1, y, y1, zg, u, pv, c_re, c_im, ar, ai, dvec, glu_w, w_out, *tail)


def _s5_bwd_in(dx1, dy1_b, du_skip, x, s_re, s_im, pv, b_re, b_im, c_re, c_im, l0_re, l0_im, ar, ai, w_in, layer, tm):
    L = x.shape[0]
    nt = L // tm

    def body(dx1_ref, dy1_ref, dus_ref, x_ref, sre_ref, sim_ref, hre_ref, him_ref, lre_ref, lim_ref, pv_ref, bre_ref, bim_ref,
             cre_ref, cim_ref, l0re_ref, l0im_ref, ar_ref, ai_ref, win_ref,
             dx_ref, du_ref, lamre_ref, lamim_ref, da_ref, vs_ref, g_re, g_im, st_re, st_im):
        gi = pl.program_id(0)
        tile = nt - 1 - gi

        @pl.when(gi == 0)
        def _():
            vs_ref[...] = jnp.zeros_like(vs_ref)
            da_ref[...] = jnp.zeros_like(da_ref)
            st_re[...] = l0re_ref[...]
            st_im[...] = l0im_ref[...]

        p = pv_ref[...]
        dy1b = dy1_ref[...]
        for k in range(S5_NB):
            dk = dy1b[:, k * S5_BH:(k + 1) * S5_BH]
            g_re[:, k * S5_BP:(k + 1) * S5_BP] = _dot_nt(dk, cre_ref[k])
            g_im[:, k * S5_BP:(k + 1) * S5_BP] = -_dot_nt(dk, cim_ref[k])
        _segment_scan(g_re, g_im, st_re, st_im, ar_ref[...], ai_ref[...], tm // 8, adjoint=True, write=True)
        lam_r, lam_i = g_re[...], g_im[...]
        lrb, lib = _bf(lam_r), _bf(lam_i)
        lamre_ref[...] = lrb
        lamim_ref[...] = lib

        def wrapped(last_ref):
            z = last_ref[...]
            row = lax.broadcasted_iota(jnp.int32, z.shape, 0)
            return jnp.where(row >= 1, pltpu.roll(z, 1, 0), 0.0)

        first_r = jnp.where(tile > 0, hre_ref[...], wrapped(lre_ref))
        first_i = jnp.where(tile > 0, him_ref[...], wrapped(lim_ref))
        sp_r = jnp.concatenate([first_r, sre_ref[0:tm - 8, :]], axis=0)
        sp_i = jnp.concatenate([first_i, sim_ref[0:tm - 8, :]], axis=0)
        da_ref[0:1, :] += _sum0(lam_r * sp_r + lam_i * sp_i)
        da_ref[1:2, :] += _sum0(lam_i * sp_r - lam_r * sp_i)

        parts = []
        for k in range(S5_NB):
            sl = slice(k * S5_BP, (k + 1) * S5_BP)
            parts.append(_dot_nt(lrb[:, sl], bre_ref[k]) + _dot_nt(lib[:, sl], bim_ref[k]))
        dub = _bf(jnp.concatenate(parts, axis=1) + dus_ref[...])
        du_ref[...] = dub
        dh = _dot_nt(dub, win_ref[...])
        _, xn, r = _norm_mod(x_ref[...], p[R_N1:R_N1 + 1], p[R_SC1:R_SC1 + 1], p[R_SH1:R_SH1 + 1])
        dx_ref[...] = dx1_ref[...] + _norm_mod_bwd(dh, xn, r, p[R_N1:R_N1 + 1], p[R_SC1:R_SC1 + 1])
        vs_ref[1:2, :] += _sum0(dh * xn)
        vs_ref[2:3, :] += _sum0(dh)

    halo = pl.BlockSpec((8, NSTATE), lambda i: (jnp.maximum((nt - 1 - i) * (tm // 8) - 1, 0), 0))
    last = pl.BlockSpec((8, NSTATE), lambda i: (L // 8 - 1, 0))
    return pl.pallas_call(
        body, name="s5_bwd_in", grid=(nt,),
        in_specs=[_rows(tm, D, nt), _rows(tm, D, nt), _rows(tm, D, nt), _rows(tm, D, nt), _rows(tm, NSTATE, nt), _rows(tm, NSTATE, nt),
                  halo, halo, last, last, pl.BlockSpec((None, 8, D), lambda i: (layer, 0, 0)),
                  _const_w((S5_NB, S5_BH, S5_BP)), _const_w((S5_NB, S5_BH, S5_BP)),
                  _const_w((S5_NB, S5_BP, S5_BH)), _const_w((S5_NB, S5_BP, S5_BH)),
                  _whole((NSEG, NSTATE)), _whole((NSEG, NSTATE)), _whole((1, NSTATE)), _whole((1, NSTATE)), _layer_w(D, D, 0)],
        out_specs=[_rows(tm, D, nt), _rows(tm, D, nt), _rows(tm, NSTATE, nt), _rows(tm, NSTATE, nt), _whole((8, NSTATE)), _whole((8, D))],
        out_shape=[jax.ShapeDtypeStruct((L, D), F32), jax.ShapeDtypeStruct((L, D), BF16),
                   jax.ShapeDtypeStruct((L, NSTATE), BF16), jax.ShapeDtypeStruct((L, NSTATE), BF16),
                   jax.ShapeDtypeStruct((8, NSTATE), F32), jax.ShapeDtypeStruct((8, D), F32)],
        scratch_shapes=[pltpu.VMEM((tm, NSTATE), F32), pltpu.VMEM((tm, NSTATE), F32),
                        pltpu.VMEM((NSEG, NSTATE), F32), pltpu.VMEM((NSEG, NSTATE), F32)],
        compiler_params=_params(1, 60),
    )(dx1, dy1_b, du_skip, x, s_re, s_im, s_re, s_im, s_re, s_im, pv, b_re, b_im, c_re, c_im, l0_re, l0_im, ar, ai, w_in)


def _blockdiag_b(bt):
    b = bt.reshape(S5_H, S5_NB, 16, S5_P).transpose(1, 2, 0, 3)
    eye = jnp.eye(16, dtype=bt.dtype)
    return (b[:, :, :, None, :] * eye[None, :, None, :, None]).reshape(S5_NB, S5_BH, S5_BP)


def _unblock_b(d):
    d = jnp.einsum("bghgp->bghp", d.reshape(S5_NB, 16, S5_H, 16, S5_P))
    return d.transpose(2, 0, 1, 3).reshape(S5_H, S5_G, S5_P)


def _blockdiag_c(cm):
    c4 = cm.reshape(S5_NB, 16, S5_H, S5_P)
    eye = jnp.eye(16, dtype=cm.dtype)
    out = c4.transpose(0, 1, 3, 2)[:, :, :, None, :] * eye[None, :, None, :, None]
    return out.reshape(S5_NB, S5_BP, S5_BH)


def _unblock_c(d):
    d = jnp.einsum("bgpgh->bghp", d.reshape(S5_NB, 16, S5_P, 16, S5_H))
    return d.reshape(S5_G, S5_H, S5_P)


def _tril_mask():
    return lax.broadcasted_iota(jnp.int32, (SG_CHUNK, SG_CHUNK), 0) >= lax.broadcasted_iota(jnp.int32, (SG_CHUNK, SG_CHUNK), 1)


def _sg_fwd(x, pv, w_in, w_s, b_t, vg, w_out, layer, tm, after=None):
    L = x.shape[0]
    nc = tm // SG_CHUNK

    def body(x_ref, pv_ref, win_ref, ws_ref, bt_ref, vg_ref, wout_ref, x1_ref, h_ref, uv_ref, vm_ref, q_ref, y_ref):
        xv, p = x_ref[...], pv_ref[...]
        h, _, _ = _norm_mod(xv, p[R_N1:R_N1 + 1], p[R_SC1:R_SC1 + 1], p[R_SH1:R_SH1 + 1])
        hb = _bf(h)
        h_ref[...] = hb
        uv = _dot(hb, win_ref[...])
        uv_ref[...] = uv
        v = uv[:, D:]
        rv = lax.rsqrt(jnp.mean(v * v, axis=-1, keepdims=True) + EPS)
        vnb = _bf((v * rv) * vg_ref[...])
        mask = _tril_mask()
        bt = bt_ref[...]
        for hd in range(SG_HEADS):
            wm = _bf(jnp.where(mask, ws_ref[hd], 0.0))
            cs = slice(hd * SG_CHUNK, (hd + 1) * SG_CHUNK)
            for ck in range(nc):
                rs = slice(ck * SG_CHUNK, (ck + 1) * SG_CHUNK)
                vm_ref[rs, cs] = _dot(wm, vnb[rs, cs]) + bt[:, hd:hd + 1]
        qb = _bf(uv[:, :D] * vm_ref[...])
        q_ref[...] = qb
        y = _dot(qb, wout_ref[...])
        y_ref[...] = y
        x1_ref[...] = xv + p[R_G1:R_G1 + 1] * y

    call, tail = _call_after(
        after, body, 7, name="sg_fwd", grid=(L // tm,),
        in_specs=[_rows(tm, D), pl.BlockSpec((None, 8, D), lambda i: (layer, 0, 0)), _layer_w(D, 2 * D, 0),
                  _whole((SG_HEADS, SG_CHUNK, SG_CHUNK)), _whole((SG_CHUNK, SG_HEADS)), _whole((1, D)), _layer_w(D, D, 0)],
        out_specs=[_rows(tm, D), _rows(tm, D), _rows(tm, 2 * D), _rows(tm, D), _rows(tm, D), _rows(tm, D)],
        out_shape=[jax.ShapeDtypeStruct((L, D), F32), jax.ShapeDtypeStruct((L, D), BF16), jax.ShapeDtypeStruct((L, 2 * D), F32),
                   jax.ShapeDtypeStruct((L, D), F32), jax.ShapeDtypeStruct((L, D), BF16), jax.ShapeDtypeStruct((L, D), F32)],
        compiler_params=_params(1, 56),
    )
    return call(x, pv, w_in, w_s, b_t, vg, w_out, *tail)


def _sg_bwd(dx1, x, y, uv, vm, pv, w_in, w_s, vg, w_out, layer, tm, after=None):
    L = x.shape[0]
    nc = tm // SG_CHUNK

    def body(dx1_ref, x_ref, y_ref, uv_ref, vm_ref, pv_ref, win_ref, ws_ref, vg_ref, wout_ref,
             dx_ref, duv_ref, dy_ref, vs_ref, dws_ref, dbt_ref, dvn_scr):
        @pl.when(pl.program_id(0) == 0)
        def _():
            vs_ref[...] = jnp.zeros_like(vs_ref)
            dws_ref[...] = jnp.zeros_like(dws_ref)
            dbt_ref[...] = jnp.zeros_like(dbt_ref)

        dx1v, p = dx1_ref[...], pv_ref[...]
        dyb = _bf(dx1v * p[R_G1:R_G1 + 1])
        dy_ref[...] = dyb
        vs_ref[0:1, :] += _sum0(dx1v * y_ref[...])
        dq = _dot_nt(dyb, wout_ref[...])
        uv = uv_ref[...]
        u, v = uv[:, :D], uv[:, D:]
        dub = _bf(dq * vm_ref[...])
        dvm = dq * u
        dvmb = _bf(dvm)
        rv = lax.rsqrt(jnp.mean(v * v, axis=-1, keepdims=True) + EPS)
        vh = v * rv
        vgv = vg_ref[...]
        vnb = _bf(vh * vgv)
        mask = _tril_mask()
        for hd in range(SG_HEADS):
            wm = _bf(jnp.where(mask, ws_ref[hd], 0.0))
            cs = slice(hd * SG_CHUNK, (hd + 1) * SG_CHUNK)
            dws = jnp.zeros((SG_CHUNK, SG_CHUNK), F32)
            dbs = jnp.zeros((SG_CHUNK, 1), F32)
            for ck in range(nc):
                rs = slice(ck * SG_CHUNK, (ck + 1) * SG_CHUNK)
                dvn_scr[rs, cs] = _dot_tn(wm, dvmb[rs, cs])
                dws = dws + _dot_nt(dvmb[rs, cs], vnb[rs, cs])
                dbs = dbs + jnp.sum(dvm[rs, cs], axis=1, keepdims=True)
            dws_ref[hd] += jnp.where(mask, dws, 0.0)
            dbt_ref[:, hd:hd + 1] += dbs
        dvn = dvn_scr[...]
        vs_ref[3:4, :] += _sum0(dvn * vh)
        dvnn = dvn * vgv
        dvb = _bf(rv * (dvnn - vh * jnp.mean(dvnn * vh, axis=-1, keepdims=True)))
        duv_ref[:, 0:D] = dub
        duv_ref[:, D:2 * D] = dvb
        dh = _dot_nt(dub, win_ref[:, 0:D]) + _dot_nt(dvb, win_ref[:, D:2 * D])
        _, xn, r = _norm_mod(x_ref[...], p[R_N1:R_N1 + 1], p[R_SC1:R_SC1 + 1], p[R_SH1:R_SH1 + 1])
        dx_ref[...] = dx1v + _norm_mod_bwd(dh, xn, r, p[R_N1:R_N1 + 1], p[R_SC1:R_SC1 + 1])
        vs_ref[1:2, :] += _sum0(dh * xn)
        vs_ref[2:3, :] += _sum0(dh)

    call, tail = _call_after(
        after, body, 10, name="sg_bwd", grid=(L // tm,),
        in_specs=[_rows(tm, D), _rows(tm, D), _rows(tm, D), _rows(tm, 2 * D), _rows(tm, D),
                  pl.BlockSpec((None, 8, D), lambda i: (layer, 0, 0)), _layer_w(D, 2 * D, 0),
                  _whole((SG_HEADS, SG_CHUNK, SG_CHUNK)), _whole((1, D)), _layer_w(D, D, 0)],
        out_specs=[_rows(tm, D), _rows(tm, 2 * D), _rows(tm, D), _whole((8, D)),
                   _whole((SG_HEADS, SG_CHUNK, SG_CHUNK)), _whole((SG_CHUNK, SG_HEADS))],
        out_shape=[jax.ShapeDtypeStruct((L, D), F32), jax.ShapeDtypeStruct((L, 2 * D), BF16), jax.ShapeDtypeStruct((L, D), BF16),
                   jax.ShapeDtypeStruct((8, D), F32), jax.ShapeDtypeStruct((SG_HEADS, SG_CHUNK, SG_CHUNK), F32),
                   jax.ShapeDtypeStruct((SG_CHUNK, SG_HEADS), F32)],
        scratch_shapes=[pltpu.VMEM((tm, D), F32)],
        compiler_params=_params(1, 56),
    )
    return call(dx1, x, y, uv, vm, pv, w_in, w_s, vg, w_out, *tail)


def _final(x, target, fg, tm):
    L = x.shape[0]

    def body(x_ref, t_ref, g_ref, dx_ref, vs_ref):
        @pl.when(pl.program_id(0) == 0)
        def _():
            vs_ref[...] = jnp.zeros_like(vs_ref)

        xv, g = x_ref[...], g_ref[...]
        r = lax.rsqrt(jnp.mean(xv * xv, axis=-1, keepdims=True) + EPS)
        xn = xv * r
        e = xn * g - t_ref[...]
        vs_ref[0:1, :] += jnp.sum(e * e)
        dout = e * (1.0 / D)
        vs_ref[1:2, :] += _sum0(dout * xn)
        dxn = dout * g
        dx_ref[...] = r * (dxn - xn * jnp.mean(dxn * xn, axis=-1, keepdims=True))

    return pl.pallas_call(
        body, name="final_loss", grid=(L // tm,),
        in_specs=[_rows(tm, D), _rows(tm, D), _whole((1, D))],
        out_specs=[_rows(tm, D), _whole((8, D))],
        out_shape=[jax.ShapeDtypeStruct((L, D), F32), jax.ShapeDtypeStruct((8, D), F32)],
        compiler_params=_params(1),
    )(x, target, fg)


def _pack_flat(arrs, multiple=LANES):
    flat = jnp.concatenate([a.reshape(-1).astype(F32) for a in arrs])
    return jnp.pad(flat, (0, -flat.shape[0] % multiple))


def _pack(arrs, row_multiple=8):
    return _pack_flat(arrs, row_multiple * LANES).reshape(-1, LANES)


def _unpack(buf, shapes, lead=()):
    flat = buf.reshape(lead + (-1,))
    out, off = [], 0
    for s in shapes:
        n = 1
        for d in s:
            n *= d
        out.append(flat[..., off:off + n].reshape(lead + tuple(s)))
        off += n
    return out


BIG = ("ff_w1", "ff_w2", "conv_w_in", "conv_w_out", "ssm_w_in", "ssm_glu_w", "ssm_w_out", "sg_w_in", "sg_w_out")
BIG_AXIS = {"ff_w1": 2, "ff_w2": 1, "conv_w_in": 2, "conv_w_out": 1, "ssm_w_in": 1, "ssm_glu_w": 1, "ssm_w_out": 1,
            "sg_w_in": 2, "sg_w_out": 1}
LAYER_WEIGHTS = (
    (("conv_w_in", 0), ("conv_w_out", 0), ("ff_w1", 0), ("ff_w2", 0)),
    (("ssm_w_in", 0), ("ssm_glu_w", 0), ("ssm_w_out", 0), ("ff_w1", 1), ("ff_w2", 1)),
    (("sg_w_in", 0), ("sg_w_out", 0), ("ff_w1", 2), ("ff_w2", 2)),
    (("conv_w_in", 1), ("conv_w_out", 1), ("ff_w1", 3), ("ff_w2", 3)),
)
GATHER_GROUPS = tuple(grp for lw in LAYER_WEIGHTS for grp in (lw[:-2], lw[-2:]))
SMALL_SHARDED = ("conv_w", "conv_b", "sg_v_g")
SMALL_WIDE_PADDED = ("ssm_b_re", "ssm_b_im")
SMALL = ("ada_b", "norm1_g", "norm2_g", "final_g", "ssm_a_re", "ssm_a_im", "ssm_log_dt", "ssm_b_re", "ssm_b_im", "ssm_c_re",
         "ssm_c_im", "ssm_d", "ssm_glu_b", "sg_w_s", "sg_b_s") + SMALL_SHARDED
WEIGHTS = ("ada_w", "ada_b", "norm1_g", "norm2_g", "ff_w1", "ff_w2", "final_g", "conv_w_in", "conv_w", "conv_b", "conv_w_out",
           "ssm_w_in", "ssm_a_re", "ssm_a_im", "ssm_log_dt", "ssm_b_re", "ssm_b_im", "ssm_c_re", "ssm_c_im", "ssm_d",
           "ssm_glu_w", "ssm_glu_b", "ssm_w_out", "sg_w_in", "sg_v_g", "sg_w_s", "sg_b_s", "sg_w_out")


def kernel(x, c, ada_w, ada_b, norm1_g, norm2_g, ff_w1, ff_w2, final_g, conv_w_in, conv_w, conv_b, conv_w_out, ssm_w_in, ssm_a_re, ssm_a_im, ssm_log_dt, ssm_b_re, ssm_b_im, ssm_c_re, ssm_c_im, ssm_d, ssm_glu_w, ssm_glu_b, ssm_w_out, sg_w_in, sg_v_g, sg_w_s, sg_b_s, sg_w_out, loss_target, m_ada_w, m_ada_b, m_norm1_g, m_norm2_g, m_ff_w1, m_ff_w2, m_final_g, m_conv_w_in, m_conv_w, m_conv_b, m_conv_w_out, m_ssm_w_in, m_ssm_a_re, m_ssm_a_im, m_ssm_log_dt, m_ssm_b_re, m_ssm_b_im, m_ssm_c_re, m_ssm_c_im, m_ssm_d, m_ssm_glu_w, m_ssm_glu_b, m_ssm_w_out, m_sg_w_in, m_sg_v_g, m_sg_w_s, m_sg_b_s, m_sg_w_out, v_ada_w, v_ada_b, v_norm1_g, v_norm2_g, v_ff_w1, v_ff_w2, v_final_g, v_conv_w_in, v_conv_w, v_conv_b, v_conv_w_out, v_ssm_w_in, v_ssm_a_re, v_ssm_a_im, v_ssm_log_dt, v_ssm_b_re, v_ssm_b_im, v_ssm_c_re, v_ssm_c_im, v_ssm_d, v_ssm_glu_w, v_ssm_glu_b, v_ssm_w_out, v_sg_w_in, v_sg_v_g, v_sg_w_s, v_sg_b_s, v_sg_w_out):
    args = dict(locals())
    w = {n: args[n] for n in WEIGHTS}
    m = {n: args["m_" + n] for n in WEIGHTS}
    v = {n: args["v_" + n] for n in WEIGHTS}
    L = x.shape[1]
    tm = min(L, 256)
    tm2 = min(L, 512)
    chip = 2 * lax.axis_index("x") + lax.axis_index("y")
    me = 2 * chip + lax.axis_index("c")
    xin = x[0]
    target = loss_target[0]
    chip1 = chip.reshape(1).astype(jnp.int32)
    place = jnp.stack([chip, lax.axis_index("c")]).astype(jnp.int32)

    gathers = []

    casts = {}

    def cast_group(g, after=None):
        entries = GATHER_GROUPS[g]
        axes = [BIG_AXIS[n] for n, _ in entries]
        casts[g] = _cast_place([w[n] for n, _ in entries], [li for _, li in entries], axes, place, f"cast_group{g}", after)

    def start_gather(g, after):
        if g not in casts:
            cast_group(g)
        axes = [BIG_AXIS[n] for n, _ in GATHER_GROUPS[g]]
        lands = casts[g]
        s_sems, r_sems, lands, token = _gather_start(lands, axes, f"gather_start{g}", after)
        gathers.append((s_sems, r_sems, lands, axes))
        return token

    def weights_of(g, after):
        s_sems, r_sems, lands, axes = gathers[g]
        lands = _gather_wait(s_sems, r_sems, lands, axes, f"gather_wait{g}", after)
        lands = _gather_share(lands, axes, f"gather_share{g}")
        token = start_gather(g + 2, lands[0]) if g + 2 < len(GATHER_GROUPS) else None
        return dict(zip([n for n, _ in GATHER_GROUPS[g]], lands)), token

    small_in = _pack([c, conv_w, conv_b, sg_v_g])
    got = _allgather_small(small_in, "gather_small_inputs").reshape(N_DEV, -1)
    c_all, cw_sh, cb_sh, vg_sh = _unpack(got, [(D,), conv_w.shape, conv_b.shape, sg_v_g.shape], lead=(N_DEV,))
    conv_w_full = jnp.concatenate([cw_sh[2 * k] for k in range(4)], axis=-1)
    conv_b_full = jnp.concatenate([cb_sh[2 * k] for k in range(4)], axis=-1)
    vg_full = jnp.concatenate([vg_sh[2 * k] for k in range(4)], axis=-1)
    c16 = jnp.pad(c_all, ((0, 16 - N_DEV), (0, 0)))

    cols = ada_w.shape[2]
    ada_b_cols = lax.dynamic_slice_in_dim(ada_b, chip * cols, cols, axis=1)[:, None, :]
    mod_sh = _ada_fwd(c16, ada_w, ada_b_cols)[:, :N_DEV, :]
    mod_all = _allgather_small(_pack([mod_sh]), "gather_mod").reshape(N_DEV, -1)
    mod_all = _unpack(mod_all, [mod_sh.shape], lead=(N_DEV,))[0]
    mod_mine = lax.dynamic_index_in_dim(mod_all[0::2], me, axis=2, keepdims=False)
    mod_mine = mod_mine.transpose(1, 0, 2).reshape(DEPTH, 6, D)
    pv = jnp.concatenate([mod_mine, norm1_g[:, None, :], norm2_g[:, None, :]], axis=1)

    second_started = start_gather(1, start_gather(0, pv))
    for g in range(2, len(GATHER_GROUPS)):
        cast_group(g, second_started)

    cw_rows = jnp.concatenate([conv_w_full, conv_b_full[:, None, :], jnp.zeros((conv_w_full.shape[0], 4, D), F32)], axis=1)

    a_re, a_im = ssm_a_re[0], ssm_a_im[0]
    log_dt = ssm_log_dt[0][:, None]
    bt_re, bt_im = ssm_b_re[0].transpose(2, 0, 1), ssm_b_im[0].transpose(2, 0, 1)
    abar_re, abar_im, bbar_re, bbar_im = _s5_params_fwd(a_re, a_im, log_dt, bt_re, bt_im, after=second_started)
    ar_vec, ai_vec = abar_re.reshape(1, NSTATE), abar_im.reshape(1, NSTATE)
    bd_re, bd_im = _bf(_blockdiag_b(bbar_re)), _bf(_blockdiag_b(bbar_im))
    cd_re, cd_im = _bf(_blockdiag_c(ssm_c_re[0])), _bf(_blockdiag_c(ssm_c_im[0]))

    saved = []
    fulls = []
    xl = xin
    for i in range(DEPTH):
        kind = MIXER_OF_LAYER[i]
        j = i // 3
        first_after = [second_started, bd_re, bd_im, cd_re, cd_im, cw_rows] + [casts[g][0] for g in range(2, len(GATHER_GROUPS))]
        full, tok = weights_of(2 * i, first_after if i == 0 else [xl])
        fulls.append(full)
        if kind == 0:
            x1, h, bcx, conv, q, y = _conv_fwd(xl, pv, full["conv_w_in"], full["conv_w_out"], cw_rows, i, j, tm2, after=tok)
            mix = dict(h=h, bcx=bcx, conv=conv, q=q, y=y)
        elif kind == 1:
            xp = _to_segments(xl)
            h, u, e_re, e_im = _s5_fwd_ends(xp, pv, full["ssm_w_in"], bd_re, bd_im, ar_vec, ai_vec, i, tm, after=tok)
            s0_re, s0_im = _s5_segment_states(e_re, e_im, ar_vec, ai_vec, L // NSEG, adjoint=False)
            x1p, s_re, s_im, y1, zg, y3, y = _s5_fwd_out(xp, u, pv, bd_re, bd_im, s0_re, s0_im, ar_vec, ai_vec, cd_re, cd_im,
                                                         ssm_d, full["ssm_glu_w"], ssm_glu_b, full["ssm_w_out"], i, tm)
            x1 = _from_segments(x1p)
            mix = dict(xp=xp, h=h, u=u, s_re=s_re, s_im=s_im, y1=y1, zg=zg, y3=y3, y=y)
        else:
            x1, h, uv, vm, q, y = _sg_fwd(xl, pv, full["sg_w_in"], sg_w_s[0], sg_b_s[0].T, vg_full, full["sg_w_out"], i, tm2,
                                          after=tok)
            mix = dict(h=h, uv=uv, vm=vm, q=q, y=y)
        ffn_weights, tok = weights_of(2 * i + 1, [x1])
        full.update(ffn_weights)
        x2, h2, a, f = _ffn_fwd(x1, pv, full["ff_w1"], full["ff_w2"], i, tm2, after=tok)
        saved.append(dict(x=xl, x1=x1, h2=h2, a=a, f=f, **mix))
        xl = x2

    dxl, vs_fin = _final(xl, target, final_g[None, :], tm2)

    gfull = {n: [None] * w[n].shape[0] for n in BIG}
    vs_mix, vs_ffn = [None] * DEPTH, [None] * DEPTH
    small_g = {}
    scatters = {}
    token = None

    def start_scatter(key, entries, after):
        garrs = [gfull[n][li][None] for n, li in entries]
        gaxes = [BIG_AXIS[n] for n, _ in entries]
        s_sems, r_sems, garrs, lands, tok = _scatter_start(garrs, gaxes, f"scatter_start{key}", after)
        scatters[key] = (s_sems, r_sems, garrs, lands, gaxes, entries)
        return tok

    for i in reversed(range(DEPTH)):
        kind = MIXER_OF_LAYER[i]
        j = i // 3
        sv = saved[i]
        full = fulls[i]
        dx1, p_b, da_b, df_b, vs_ffn[i] = _ffn_bwd(dxl, sv["x1"], sv["a"], sv["f"], pv, full["ff_w1"], full["ff_w2"], i, tm,
                                                   after=token)
        gfull["ff_w1"][i] = _mm_tn(sv["h2"], da_b, f"wgrad_ff_w1_{i}")
        gfull["ff_w2"][i] = _mm_tn(p_b, df_b, f"wgrad_ff_w2_{i}")
        if i == 0:
            token = start_scatter("0f", LAYER_WEIGHTS[0][2:], dx1)
        if kind == 0:
            dxl, dbcx_b, dy_b, vsm = _conv_bwd(dx1, sv["x"], sv["y"], sv["bcx"], sv["conv"], pv, full["conv_w_in"],
                                               full["conv_w_out"], cw_rows, i, j, tm2, after=token if i == 0 else None)
            gfull["conv_w_in"][j] = _mm_tn(sv["h"], dbcx_b, f"wgrad_conv_w_in_{j}")
            gfull["conv_w_out"][j] = _mm_tn(sv["q"], dy_b, f"wgrad_conv_w_out_{j}")
            small_g.setdefault("conv_w", [None, None])[j] = vsm[3:6]
            small_g.setdefault("conv_b", [None, None])[j] = vsm[6]
        elif kind == 1:
            dx1p = _to_segments(dx1)
            dy_b, y2_b, dzg_b, dy1_b, du_skip, eb_re, eb_im, vsm = _s5_bwd_ends(
                dx1p, sv["y"], sv["y1"], sv["zg"], sv["u"], pv, cd_re, cd_im, ar_vec, ai_vec, ssm_d, full["ssm_glu_w"],
                full["ssm_w_out"], i, tm)
            l0_re, l0_im = _s5_segment_states(eb_re, eb_im, ar_vec, ai_vec, L // NSEG, adjoint=True)
            dxp, du_b, lam_re, lam_im, dabar, vs_in = _s5_bwd_in(
                dx1p, dy1_b, du_skip, sv["xp"], sv["s_re"], sv["s_im"], pv, bd_re, bd_im, cd_re, cd_im, l0_re, l0_im,
                ar_vec, ai_vec, full["ssm_w_in"], i, tm)
            dxl = _from_segments(dxp)
            gfull["ssm_w_out"][0] = _mm_tn(sv["y3"], dy_b, "wgrad_ssm_w_out")
            gfull["ssm_glu_w"][0] = _mm_tn(y2_b, dzg_b, "wgrad_ssm_glu_w")
            gfull["ssm_w_in"][0] = _mm_tn(sv["h"], du_b, "wgrad_ssm_w_in")
            s5_late = dict(s_re=sv["s_re"], s_im=sv["s_im"], u=sv["u"], dy1_b=dy1_b, lam_re=lam_re, lam_im=lam_im, dabar=dabar)
            small_g.update(ssm_d=vsm[2], ssm_glu_b=vsm[1])
            vsm = jnp.concatenate([vsm[0:1], vs_in[1:3], jnp.zeros((5, D), F32)], axis=0)
        else:
            dxl, duv_b, dy_b, vsm, d_ws, d_bt = _sg_bwd(dx1, sv["x"], sv["y"], sv["uv"], sv["vm"], pv, full["sg_w_in"],
                                                        sg_w_s[0], vg_full, full["sg_w_out"], i, tm2)
            gfull["sg_w_in"][0] = _mm_tn(sv["h"], duv_b, "wgrad_sg_w_in")
            gfull["sg_w_out"][0] = _mm_tn(sv["q"], dy_b, "wgrad_sg_w_out")
            small_g.update(sg_v_g=vsm[3], sg_w_s=d_ws, sg_b_s=d_bt.T)
        vs_mix[i] = vsm
        token = start_scatter(str(i), LAYER_WEIGHTS[i], dxl) if i > 0 else start_scatter("0c", LAYER_WEIGHTS[0][:2], dxl)
    grad_x = dxl[None]

    sums = {n: [None] * w[n].shape[0] for n in BIG}

    def collect(key, after):
        s_sems, r_sems, garrs, lands, gaxes, entries = scatters[key]
        garrs, recv = _scatter_wait(s_sems, r_sems, garrs, lands, gaxes, f"scatter_wait{key}", after)
        for (n, li), t in zip(entries, _sum_parts(recv, garrs, gaxes, chip1, f"sum_group{key}")):
            sums[n][li] = t
        return sums[entries[-1][0]][entries[-1][1]]

    after = token
    for key in ("3", "2", "1"):
        after = collect(key, after)
    early = [(n, li) for i in (3, 2, 1) for n, li in LAYER_WEIGHTS[i]]
    late = list(LAYER_WEIGHTS[0][2:]) + list(LAYER_WEIGHTS[0][:2])
    s_sems, r_sems, mine_thru, lands, tok = _swap_start([sums[n][li] for n, li in early], "swap_start_early", after)

    blocks = dict(
        c_re=_mm_tn_blocks(s5_late["s_re"], s5_late["dy1_b"], S5_BP, S5_BH, "wgrad_s5_c_re", after=tok),
        c_im=_mm_tn_blocks(s5_late["s_im"], s5_late["dy1_b"], S5_BP, S5_BH, "wgrad_s5_c_im", after=tok),
        b_re=_mm_tn_blocks(s5_late["u"], s5_late["lam_re"], S5_BH, S5_BP, "wgrad_s5_b_re", after=tok),
        b_im=_mm_tn_blocks(s5_late["u"], s5_late["lam_im"], S5_BH, S5_BP, "wgrad_s5_b_im", after=tok))
    d_are, d_aim, d_ldt, d_btre, d_btim = _s5_params_bwd(
        a_re, a_im, log_dt, bt_re, bt_im, s5_late["dabar"][0].reshape(S5_G, S5_P), s5_late["dabar"][1].reshape(S5_G, S5_P),
        _unblock_b(blocks["b_re"]), _unblock_b(blocks["b_im"]))
    small_g.update(ssm_a_re=d_are, ssm_a_im=d_aim, ssm_log_dt=d_ldt, ssm_b_re=d_btre.transpose(1, 2, 0),
                   ssm_b_im=d_btim.transpose(1, 2, 0), ssm_c_re=_unblock_c(blocks["c_re"]), ssm_c_im=-_unblock_c(blocks["c_im"]))

    mine_thru, got = _swap_wait(s_sems, r_sems, mine_thru, lands, "swap_wait_early", blocks["b_im"])
    sib = dict(zip(early, got))
    for (n, li), t in zip(early, mine_thru):
        sums[n][li] = t
    after = got[-1]
    for key in ("0f", "0c"):
        after = collect(key, after)
    sib.update(zip(late, _swap_with_sibling([sums[n][li] for n, li in late], "swap_grad_sums_late")))

    dmod = _mod_bwd(jnp.stack(vs_mix), jnp.stack(vs_ffn), pv)
    small_g.update(ada_b=dmod[:, :6, :], norm1_g=dmod[:, 6, :], norm2_g=dmod[:, 7, :], final_g=vs_fin[1],
                   conv_w=jnp.stack(small_g["conv_w"]), conv_b=jnp.stack(small_g["conv_b"]))

    loss_part = (0.5 / D) * vs_fin[0, 0:1]
    part_shapes = [(1,)] + [tuple(small_g[n].shape) for n in SMALL]
    slots = _reduce_pair(_pack([loss_part] + [small_g[n] for n in SMALL], 16), "reduce_small_pair", sib[late[-1]])
    s_sems, r_sems, slots, tok = _reduce_cross_start(slots, "reduce_small_cross_start")

    res = {}
    for n in BIG:
        res[n] = _adamw_layers(w[n], sums[n], [sib[(n, li)] for li in range(w[n].shape[0])], m[n], v[n], f"adamw_{n}", after=tok)
        tok = res[n][0]

    slots = _reduce_cross_wait(s_sems, r_sems, slots, "reduce_small_cross_wait", tok)
    parts_sum = _reduce_finish(slots, "reduce_small_finish")
    summed = _unpack(parts_sum, part_shapes)
    loss = summed[0][0]
    gsum = dict(zip(SMALL, summed[1:]))
    dmod_all = _allgather_small(_pack([small_g["ada_b"]]), "gather_dmod", parts_sum)
    dmod_all = dmod_all.reshape(N_DEV, DEPTH, 6 * D)
    dmod_cols = lax.dynamic_slice_in_dim(dmod_all, chip * cols, cols, axis=2).transpose(1, 0, 2)
    g_ada_w = _ada_bwd(c16, jnp.pad(dmod_cols, ((0, 0), (0, 16 - N_DEV), (0, 0))))

    shp = ada_w.shape
    two = lambda t: t.reshape(shp[0] * shp[1], shp[2])
    res["ada_w"] = [t.reshape(shp) for t in _adamw(two(ada_w), [two(g_ada_w)], two(m_ada_w), two(v_ada_w), "adamw_ada_w")]

    def mine(n):
        g = gsum[n]
        if n in SMALL_SHARDED:
            g = lax.dynamic_slice_in_dim(g, chip * w[n].shape[-1], w[n].shape[-1], axis=g.ndim - 1)
        return g.reshape(w[n].shape)

    for k, names in enumerate(([n for n in SMALL if n not in SMALL_WIDE_PADDED], list(SMALL_WIDE_PADDED))):
        outs = _adamw_many([w[n] for n in names], [mine(n) for n in names], [m[n] for n in names], [v[n] for n in names],
                           f"adamw_small{k}")
        for idx, n in enumerate(names):
            res[n] = [outs[part][idx] for part in range(4)]

    outs = [loss, grad_x]
    for part in range(4):
        outs += [res[n][part] for n in WEIGHTS]
    return tuple(outs)
```

```python
import functools

import jax
import jax.numpy as jnp
from jax import lax
from jax.experimental import pallas as pl
from jax.experimental.pallas import tpu as pltpu

F32 = jnp.float32
BF16 = jnp.bfloat16
D = 1024
EPS = 1e-6
DEPTH = 4
MIXER_OF_LAYER = (0, 1, 2, 0)
S5_G, S5_H, S5_P = 64, 16, 64
S5_NB = 4
S5_BH = S5_H * 16
S5_BP = S5_P * 16
NSTATE = S5_G * S5_P
SG_HEADS, SG_CHUNK = 8, 128
ADAM_LR, ADAM_B1, ADAM_B2, ADAM_EPS, ADAM_WD, ADAM_STEP = 0.001, 0.9, 0.999, 1e-08, 0.01, 10
N_DEV = 8
MESH = pl.DeviceIdType.MESH
LANES = 1024
R_SH1, R_SC1, R_G1, R_SH2, R_SC2, R_G2, R_N1, R_N2 = range(8)


def _dot(a, b):
    return jnp.dot(a, b, preferred_element_type=F32)


def _dot_nt(a, b):
    return lax.dot_general(a, b, (((1,), (1,)), ((), ())), preferred_element_type=F32)


def _dot_tn(a, b):
    return lax.dot_general(a, b, (((0,), (0,)), ((), ())), preferred_element_type=F32)


def _bf(x):
    return x.astype(BF16)


def _sum0(x):
    return jnp.sum(x, axis=0, keepdims=True)


def _params(n_axes, vmem_mb=48):
    return pltpu.CompilerParams(dimension_semantics=("arbitrary",) * n_axes, vmem_limit_bytes=vmem_mb << 20)


def _rows(tm, cols, nt=None):
    if nt is None:
        return pl.BlockSpec((tm, cols), lambda i: (i, 0))
    return pl.BlockSpec((tm, cols), lambda i: (nt - 1 - i, 0))


def _whole(shape):
    nd = len(shape)
    return pl.BlockSpec(shape, lambda *_: (0,) * nd)


def _layer_w(r, c, layer):
    return pl.BlockSpec((None, r, c), lambda *_: (layer, 0, 0), pipeline_mode=pl.Buffered(1))


def _const_w(shape):
    nd = len(shape)
    return pl.BlockSpec(shape, lambda *_: (0,) * nd, pipeline_mode=pl.Buffered(1))


def _call_after(after, body, n_in, *, in_specs, **kw):
    if after is None:
        return pl.pallas_call(body, in_specs=in_specs, **kw), ()

    def body_after(*refs):
        return body(*refs[:n_in], *refs[n_in + 1:])

    return pl.pallas_call(body_after, in_specs=list(in_specs) + [pl.BlockSpec(memory_space=pl.ANY)], **kw), (after,)


def _norm_mod(x, ng, sc, sh):
    r = lax.rsqrt(jnp.mean(x * x, axis=-1, keepdims=True) + EPS)
    xn = x * r
    return (xn * ng) * (1.0 + sc) + sh, xn, r


def _norm_mod_bwd(dh, xn, r, ng, sc):
    dxn = dh * (ng * (1.0 + sc))
    return r * (dxn - xn * jnp.mean(dxn * xn, axis=-1, keepdims=True))


def _shift_down(z, prev8, k):
    row = lax.broadcasted_iota(jnp.int32, z.shape, 0)
    if k == 1:
        return jnp.where(row >= 1, pltpu.roll(z, 1, 0), prev8[7:8])
    return jnp.where(row >= 2, pltpu.roll(z, 2, 0), jnp.where(row == 0, prev8[6:7], prev8[7:8]))


def _shift_up(z, next8, k):
    n = z.shape[0]
    row = lax.broadcasted_iota(jnp.int32, z.shape, 0)
    if k == 1:
        return jnp.where(row <= n - 2, pltpu.roll(z, n - 1, 0), next8[0:1])
    return jnp.where(row <= n - 3, pltpu.roll(z, n - 2, 0), jnp.where(row == n - 2, next8[0:1], next8[1:2]))


def _place():
    x, y, c = lax.axis_index("x"), lax.axis_index("y"), lax.axis_index("c")
    chips = [(1 - x, y), (x, 1 - y), (1 - x, 1 - y)]
    return x, y, c, chips


def _allgather_small(x_shard, name, after=None):
    m_per, n = x_shard.shape

    def body(x_ref, out_ref, send_sems, recv_sems, local_sem):
        x, y, c, chips = _place()
        me, sibling = (x, y, c), (x, y, 1 - c)

        def rows(px, py, pc):
            return out_ref.at[pl.ds((4 * px + 2 * py + pc) * m_per, m_per), :]

        def copy(k, block, to, src=None):
            return pltpu.make_async_remote_copy(
                src_ref=rows(*block) if src is None else src, dst_ref=rows(*block),
                send_sem=send_sems.at[k], recv_sem=recv_sems.at[k], device_id=to, device_id_type=MESH)

        mine = pltpu.make_async_copy(x_ref, rows(*me), local_sem)
        mine.start()
        first = [copy(0, me, sibling, src=x_ref)]
        first += [copy(1 + j, me, (*chip, c), src=x_ref) for j, chip in enumerate(chips)]
        for cp in first:
            cp.start()
        passed = [copy(4 + j, (*chip, c), sibling) for j, chip in enumerate(chips)]
        for j, chip in enumerate(chips):
            copy(1 + j, (*chip, c), me).wait_recv()
            passed[j].start()
        copy(0, sibling, me).wait_recv()
        for j, chip in enumerate(chips):
            copy(4 + j, (*chip, 1 - c), me).wait_recv()
        for cp in first + passed:
            cp.wait_send()
        mine.wait()

    call, tail = _call_after(
        after, body, 1, name=name, out_shape=jax.ShapeDtypeStruct((N_DEV * m_per, n), F32),
        in_specs=[pl.BlockSpec(memory_space=pltpu.VMEM)], out_specs=pl.BlockSpec(memory_space=pltpu.VMEM),
        scratch_shapes=[pltpu.SemaphoreType.DMA((7,)), pltpu.SemaphoreType.DMA((7,)), pltpu.SemaphoreType.DMA],
        compiler_params=pltpu.CompilerParams(vmem_limit_bytes=48 << 20),
    )
    return call(x_shard, *tail)


def _reduce_pair(x_part, name, after=None):
    m, n = x_part.shape
    h = m // 2

    def body(x_ref, slots_ref, sib_buf, send_sem, recv_sem):
        x, y, c, _ = _place()
        swap = pltpu.make_async_remote_copy(src_ref=x_ref, dst_ref=sib_buf, send_sem=send_sem, recv_sem=recv_sem,
                                            device_id=(x, y, 1 - c), device_id_type=MESH)
        swap.start()
        swap.wait()
        mine = pl.ds(pl.multiple_of(c * h, 8), h)
        slots_ref[pl.ds(2 * x + y, 1)] = (x_ref[mine, :] + sib_buf[mine, :])[None]

    call, tail = _call_after(
        after, body, 1, name=name, out_shape=jax.ShapeDtypeStruct((4, h, n), F32),
        in_specs=[pl.BlockSpec(memory_space=pltpu.VMEM)], out_specs=pl.BlockSpec(memory_space=pltpu.VMEM),
        scratch_shapes=[pltpu.VMEM((m, n), F32), pltpu.SemaphoreType.DMA, pltpu.SemaphoreType.DMA],
        compiler_params=pltpu.CompilerParams(vmem_limit_bytes=48 << 20),
    )
    return call(x_part, *tail)


def _reduce_cross_start(slots, name):
    def body(slots_ref, send_sems, recv_sems, thru, token):
        x, y, c, chips = _place()
        mine = slots_ref.at[pl.ds(2 * x + y, 1)]
        for j, chip in enumerate(chips):
            pltpu.make_async_remote_copy(src_ref=mine, dst_ref=mine, send_sem=send_sems.at[j], recv_sem=recv_sems.at[j],
                                         device_id=(*chip, c), device_id_type=MESH).start()
        token[...] = jnp.zeros_like(token)

    res = pl.pallas_call(
        body, name=name,
        out_shape=(pltpu.SemaphoreType.DMA((3,)), pltpu.SemaphoreType.DMA((3,)), pltpu.HBM(slots.shape, F32),
                   jax.ShapeDtypeStruct((8, 128), F32)),
        in_specs=[HBM_SPEC], out_specs=(SEM_SPEC, SEM_SPEC, HBM_SPEC, pl.BlockSpec(memory_space=pltpu.VMEM)),
        input_output_aliases={0: 2}, compiler_params=SPLIT_COPY_PARAMS,
    )(*_in_hbm([slots]))
    return res


def _reduce_cross_wait(send_sems, recv_sems, slots, name, after):
    def body(slots_ref, s_sems, r_sems, after_ref, thru):
        x, y, c, chips = _place()
        for j, chip in enumerate(chips):
            theirs = slots_ref.at[pl.ds(2 * chip[0] + chip[1], 1)]
            cp = pltpu.make_async_remote_copy(src_ref=theirs, dst_ref=theirs, send_sem=s_sems.at[j], recv_sem=r_sems.at[j],
                                              device_id=(x, y, c), device_id_type=MESH)
            cp.wait_send()
            cp.wait_recv()

    return pl.pallas_call(
        body, name=name, out_shape=pltpu.HBM(slots.shape, F32),
        in_specs=[HBM_SPEC, SEM_SPEC, SEM_SPEC, ANY_SPEC], out_specs=HBM_SPEC,
        input_output_aliases={0: 0}, compiler_params=SPLIT_COPY_PARAMS,
    )(slots, send_sems, recv_sems, after)


def _reduce_finish(slots, name):
    _, h, n = slots.shape

    def body(slots_ref, out_ref, send_sem, recv_sem):
        x, y, c, _ = _place()
        mine = pl.ds(pl.multiple_of(c * h, 8), h)
        theirs = pl.ds(pl.multiple_of((1 - c) * h, 8), h)
        out_ref[mine, :] = ((slots_ref[0] + slots_ref[1]) + slots_ref[2]) + slots_ref[3]
        give = pltpu.make_async_remote_copy(src_ref=out_ref.at[mine, :], dst_ref=out_ref.at[mine, :], send_sem=send_sem,
                                            recv_sem=recv_sem, device_id=(x, y, 1 - c), device_id_type=MESH)
        give.start()
        give.wait_send()
        pltpu.make_async_remote_copy(src_ref=out_ref.at[theirs, :], dst_ref=out_ref.at[theirs, :], send_sem=send_sem,
                                     recv_sem=recv_sem, device_id=(x, y, c), device_id_type=MESH).wait_recv()

    return pl.pallas_call(
        body, name=name, out_shape=jax.ShapeDtypeStruct((2 * h, n), F32),
        in_specs=[pl.BlockSpec(memory_space=pltpu.VMEM)], out_specs=pl.BlockSpec(memory_space=pltpu.VMEM),
        scratch_shapes=[pltpu.SemaphoreType.DMA, pltpu.SemaphoreType.DMA],
        compiler_params=pltpu.CompilerParams(vmem_limit_bytes=48 << 20),
    )(slots)


def _shard_region(ref, full_shape, axis, chip_k, half=None):
    _, r, c = full_shape
    if axis == 1:
        rs = r // 4
        if half is None:
            return ref.at[:, pl.ds(pl.multiple_of(chip_k * rs, 128), rs), :]
        return ref.at[:, pl.ds(pl.multiple_of(chip_k * rs + half * (rs // 2), 128), rs // 2), :]
    cs = c // 4
    if half is None:
        return ref.at[:, :, pl.ds(pl.multiple_of(chip_k * cs, 128), cs)]
    return ref.at[:, pl.ds(pl.multiple_of(half * (r // 2), 128), r // 2), pl.ds(pl.multiple_of(chip_k * cs, 128), cs)]


HBM_SPEC = pl.BlockSpec(memory_space=pltpu.HBM)
SEM_SPEC = pl.BlockSpec(memory_space=pltpu.SEMAPHORE)
ANY_SPEC = pl.BlockSpec(memory_space=pl.ANY)
SPLIT_COPY_PARAMS = pltpu.CompilerParams(has_side_effects=pltpu.SideEffectType.DATAFLOW_SIDE_EFFECTING)


def _in_hbm(arrs):
    return [pltpu.with_memory_space_constraint(a, pltpu.HBM) for a in arrs]


def _cast_place(ws, layers, axes, place, name, after=None):
    n_arr = len(ws)
    in_specs, out_specs, fulls = [], [], []
    for w_stack, li, axis in zip(ws, layers, axes):
        _, r, c = w_stack.shape
        tr = r // 4
        fulls.append((1, 4 * r, c) if axis == 1 else (1, r, 4 * c))
        in_specs.append(pl.BlockSpec((None, tr, c), lambda i, p, li=li: (li, 2 * p[1] + i, 0)))
        if axis == 1:
            out_specs.append(pl.BlockSpec((None, tr, c), lambda i, p: (0, 4 * p[0] + 2 * p[1] + i, 0)))
        else:
            out_specs.append(pl.BlockSpec((None, tr, c), lambda i, p: (0, 2 * p[1] + i, p[0])))

    extra = [] if after is None else [after]

    def body(p_ref, *refs):
        for a in range(n_arr):
            refs[n_arr + len(extra) + a][...] = _bf(refs[a][...])

    return pl.pallas_call(
        body, name=name,
        grid_spec=pltpu.PrefetchScalarGridSpec(num_scalar_prefetch=1, grid=(2,), in_specs=in_specs + [ANY_SPEC] * len(extra),
                                               out_specs=out_specs),
        out_shape=[jax.ShapeDtypeStruct(f, BF16) for f in fulls],
        compiler_params=_params(1),
    )(place, *ws, *extra)


def _gather_start(lands, axes, name, after):
    n_arr = len(lands)
    fulls = [tuple(l.shape) for l in lands]

    def body(*refs):
        land = refs[:n_arr]
        send_sems, recv_sems = refs[n_arr + 1:n_arr + 3]
        token = refs[-1]
        x, y, c, chips = _place()
        k_me = 2 * x + y
        for a in range(n_arr):
            mine = _shard_region(land[a], fulls[a], axes[a], k_me, c)
            for j, chip in enumerate(chips):
                pltpu.make_async_remote_copy(
                    src_ref=mine, dst_ref=mine, send_sem=send_sems.at[a * 3 + j], recv_sem=recv_sems.at[a * 3 + j],
                    device_id=(*chip, c), device_id_type=MESH).start()
        token[...] = jnp.zeros_like(token)

    res = pl.pallas_call(
        body, name=name,
        out_shape=(pltpu.SemaphoreType.DMA((3 * n_arr,)), pltpu.SemaphoreType.DMA((3 * n_arr,)),
                   *[pltpu.HBM(f, BF16) for f in fulls], jax.ShapeDtypeStruct((8, 128), F32)),
        in_specs=[HBM_SPEC] * n_arr + [ANY_SPEC],
        out_specs=(SEM_SPEC, SEM_SPEC, *[HBM_SPEC] * n_arr, pl.BlockSpec(memory_space=pltpu.VMEM)),
        input_output_aliases={a: 2 + a for a in range(n_arr)},
        compiler_params=SPLIT_COPY_PARAMS,
    )(*_in_hbm(lands), after)
    return res[0], res[1], list(res[2:2 + n_arr]), res[-1]


def _gather_wait(send_sems, recv_sems, lands, axes, name, after):
    n_arr = len(lands)
    fulls = [tuple(l.shape) for l in lands]

    def body(*refs):
        land = refs[:n_arr]
        s_sems, r_sems = refs[n_arr:n_arr + 2]
        x, y, c, chips = _place()
        for a in range(n_arr):
            for j, chip in enumerate(chips):
                k_j = 2 * chip[0] + chip[1]
                got = _shard_region(land[a], fulls[a], axes[a], k_j, c)
                cp = pltpu.make_async_remote_copy(
                    src_ref=got, dst_ref=got, send_sem=s_sems.at[a * 3 + j], recv_sem=r_sems.at[a * 3 + j],
                    device_id=(x, y, c), device_id_type=MESH)
                cp.wait_send()
                cp.wait_recv()

    res = pl.pallas_call(
        body, name=name,
        out_shape=tuple(pltpu.HBM(f, BF16) for f in fulls),
        in_specs=[HBM_SPEC] * n_arr + [SEM_SPEC, SEM_SPEC] + [ANY_SPEC] * len(after),
        out_specs=tuple([HBM_SPEC] * n_arr),
        input_output_aliases={a: a for a in range(n_arr)},
        compiler_params=SPLIT_COPY_PARAMS,
    )(*lands, send_sems, recv_sems, *after)
    return list(res)


def _gather_share(lands, axes, name):
    n_arr = len(lands)
    fulls = [tuple(l.shape) for l in lands]

    def body(*refs):
        land_in, land = refs[:n_arr], refs[n_arr:2 * n_arr]
        send_sems, recv_sems = refs[2 * n_arr:]
        x, y, c, chips = _place()
        copies = []
        for a in range(n_arr):
            for j, k_j in enumerate([2 * chip[0] + chip[1] for chip in chips] + [2 * x + y]):
                cp = pltpu.make_async_remote_copy(
                    src_ref=_shard_region(land_in[a], fulls[a], axes[a], k_j, c),
                    dst_ref=_shard_region(land[a], fulls[a], axes[a], k_j, c),
                    send_sem=send_sems.at[a * 4 + j], recv_sem=recv_sems.at[a * 4 + j],
                    device_id=(x, y, 1 - c), device_id_type=MESH)
                cp.start()
                copies.append(cp)
        for cp in copies:
            cp.wait()

    return pl.pallas_call(
        body, name=name, out_shape=[jax.ShapeDtypeStruct(f, BF16) for f in fulls],
        in_specs=[ANY_SPEC] * n_arr, out_specs=[ANY_SPEC] * n_arr,
        input_output_aliases={a: a for a in range(n_arr)},
        scratch_shapes=[pltpu.SemaphoreType.DMA((4 * n_arr,)), pltpu.SemaphoreType.DMA((4 * n_arr,))],
    )(*lands)


def _scatter_shapes(grads, axes):
    out = []
    for g, ax in zip(grads, axes):
        shp = list(g.shape)
        shp[ax] //= 4
        out.append((3,) + tuple(shp[1:]))
    return out


def _scatter_start(grads, axes, name, after):
    n_arr = len(grads)
    shapes = _scatter_shapes(grads, axes)
    lands = [lax.empty(s, BF16) for s in shapes]

    def body(*refs):
        ins, land = refs[:n_arr], refs[n_arr:2 * n_arr]
        send_sems, recv_sems = refs[2 * n_arr + 1:2 * n_arr + 3]
        token = refs[-1]
        x, y, c, chips = _place()
        for a in range(n_arr):
            for j, chip in enumerate(chips):
                k_j = 2 * chip[0] + chip[1]
                pltpu.make_async_remote_copy(
                    src_ref=_shard_region(ins[a], grads[a].shape, axes[a], k_j), dst_ref=land[a].at[pl.ds(j, 1)],
                    send_sem=send_sems.at[a * 3 + j], recv_sem=recv_sems.at[a * 3 + j],
                    device_id=(*chip, c), device_id_type=MESH).start()
        token[...] = jnp.zeros_like(token)

    res = pl.pallas_call(
        body, name=name,
        out_shape=(pltpu.SemaphoreType.DMA((3 * n_arr,)), pltpu.SemaphoreType.DMA((3 * n_arr,)),
                   *[pltpu.HBM(g.shape, BF16) for g in grads], *[pltpu.HBM(s, BF16) for s in shapes],
                   jax.ShapeDtypeStruct((8, 128), F32)),
        in_specs=[HBM_SPEC] * (2 * n_arr) + [ANY_SPEC],
        out_specs=(SEM_SPEC, SEM_SPEC, *[HBM_SPEC] * (2 * n_arr), pl.BlockSpec(memory_space=pltpu.VMEM)),
        input_output_aliases={a: 2 + a for a in range(2 * n_arr)},
        compiler_params=SPLIT_COPY_PARAMS,
    )(*_in_hbm(grads), *_in_hbm(lands), after)
    return res[0], res[1], list(res[2:2 + n_arr]), list(res[2 + n_arr:2 + 2 * n_arr]), res[-1]


def _scatter_wait(send_sems, recv_sems, grads, lands, axes, name, after):
    n_arr = len(grads)

    def body(*refs):
        ins, land = refs[:n_arr], refs[n_arr:2 * n_arr]
        s_sems, r_sems = refs[2 * n_arr:2 * n_arr + 2]
        x, y, c, chips = _place()
        for a in range(n_arr):
            for j, chip in enumerate(chips):
                k_j = 2 * chip[0] + chip[1]
                cp = pltpu.make_async_remote_copy(
                    src_ref=_shard_region(ins[a], grads[a].shape, axes[a], k_j), dst_ref=land[a].at[pl.ds(j, 1)],
                    send_sem=s_sems.at[a * 3 + j], recv_sem=r_sems.at[a * 3 + j],
                    device_id=(x, y, c), device_id_type=MESH)
                cp.wait_send()
                cp.wait_recv()

    res = pl.pallas_call(
        body, name=name,
        out_shape=(*[pltpu.HBM(g.shape, BF16) for g in grads], *[pltpu.HBM(l.shape, BF16) for l in lands]),
        in_specs=[HBM_SPEC] * (2 * n_arr) + [SEM_SPEC, SEM_SPEC, ANY_SPEC],
        out_specs=tuple([HBM_SPEC] * (2 * n_arr)),
        input_output_aliases={a: a for a in range(2 * n_arr)},
        compiler_params=SPLIT_COPY_PARAMS,
    )(*grads, *lands, send_sems, recv_sems, after)
    return list(res[:n_arr]), list(res[n_arr:])


def _swap_start(arrs, name, after):
    n_arr = len(arrs)
    lands = [lax.empty(a.shape, a.dtype) for a in arrs]

    def body(*refs):
        ins, land = refs[:n_arr], refs[n_arr:2 * n_arr]
        send_sems, recv_sems = refs[2 * n_arr + 1:2 * n_arr + 3]
        token = refs[-1]
        x, y, c, _ = _place()
        for a in range(n_arr):
            pltpu.make_async_remote_copy(
                src_ref=ins[a], dst_ref=land[a], send_sem=send_sems.at[a], recv_sem=recv_sems.at[a],
                device_id=(x, y, 1 - c), device_id_type=MESH).start()
        token[...] = jnp.zeros_like(token)

    res = pl.pallas_call(
        body, name=name,
        out_shape=(pltpu.SemaphoreType.DMA((n_arr,)), pltpu.SemaphoreType.DMA((n_arr,)),
                   *[pltpu.HBM(a.shape, a.dtype) for a in arrs], *[pltpu.HBM(a.shape, a.dtype) for a in arrs],
                   jax.ShapeDtypeStruct((8, 128), F32)),
        in_specs=[HBM_SPEC] * (2 * n_arr) + [ANY_SPEC],
        out_specs=(SEM_SPEC, SEM_SPEC, *[HBM_SPEC] * (2 * n_arr), pl.BlockSpec(memory_space=pltpu.VMEM)),
        input_output_aliases={a: 2 + a for a in range(2 * n_arr)},
        compiler_params=SPLIT_COPY_PARAMS,
    )(*_in_hbm(arrs), *_in_hbm(lands), after)
    return res[0], res[1], list(res[2:2 + n_arr]), list(res[2 + n_arr:2 + 2 * n_arr]), res[-1]


def _swap_wait(send_sems, recv_sems, arrs, lands, name, after):
    n_arr = len(arrs)

    def body(*refs):
        ins, land = refs[:n_arr], refs[n_arr:2 * n_arr]
        s_sems, r_sems = refs[2 * n_arr:2 * n_arr + 2]
        x, y, c, _ = _place()
        for a in range(n_arr):
            cp = pltpu.make_async_remote_copy(
                src_ref=ins[a], dst_ref=land[a], send_sem=s_sems.at[a], recv_sem=r_sems.at[a],
                device_id=(x, y, c), device_id_type=MESH)
            cp.wait_send()
            cp.wait_recv()

    res = pl.pallas_call(
        body, name=name,
        out_shape=(*[pltpu.HBM(a.shape, a.dtype) for a in arrs], *[pltpu.HBM(a.shape, a.dtype) for a in arrs]),
        in_specs=[HBM_SPEC] * (2 * n_arr) + [SEM_SPEC, SEM_SPEC, ANY_SPEC],
        out_specs=tuple([HBM_SPEC] * (2 * n_arr)),
        input_output_aliases={a: a for a in range(2 * n_arr)},
        compiler_params=SPLIT_COPY_PARAMS,
    )(*arrs, *lands, send_sems, recv_sems, after)
    return list(res[:n_arr]), list(res[n_arr:])


def _swap_with_sibling(arrs, name):
    n_arr = len(arrs)

    def body(*refs):
        ins, outs = refs[:n_arr], refs[n_arr:2 * n_arr]
        send_sems, recv_sems = refs[2 * n_arr:]
        x, y, c, _ = _place()
        copies = []
        for a in range(n_arr):
            cp = pltpu.make_async_remote_copy(
                src_ref=ins[a], dst_ref=outs[a], send_sem=send_sems.at[a], recv_sem=recv_sems.at[a],
                device_id=(x, y, 1 - c), device_id_type=MESH)
            cp.start()
            copies.append(cp)
        for cp in copies:
            cp.wait()

    any_spec = pl.BlockSpec(memory_space=pl.ANY)
    return pl.pallas_call(
        body, name=name, out_shape=[jax.ShapeDtypeStruct(a.shape, a.dtype) for a in arrs],
        in_specs=[any_spec] * n_arr, out_specs=[any_spec] * n_arr,
        scratch_shapes=[pltpu.SemaphoreType.DMA((n_arr,)), pltpu.SemaphoreType.DMA((n_arr,))],
    )(*arrs)


def _mm_tn(a, b, name, out_dtype=BF16):
    L, m = a.shape
    n = b.shape[1]
    bm, bn = min(m, 1024), min(n, 1024)
    nk = 2 if (m // bm) * (n // bn) == 1 and L % 32 == 0 else 1
    bk = L // nk

    def body(a_ref, b_ref, o_ref, *acc):
        part = _dot_tn(_bf(a_ref[...]), _bf(b_ref[...]))
        if nk == 1:
            o_ref[...] = part.astype(out_dtype)
        else:
            @pl.when(pl.program_id(2) == 0)
            def _():
                acc[0][...] = part

            @pl.when(pl.program_id(2) == 1)
            def _():
                o_ref[...] = (acc[0][...] + part).astype(out_dtype)

    return pl.pallas_call(
        body, name=name, grid=(m // bm, n // bn, nk),
        in_specs=[pl.BlockSpec((bk, bm), lambda i, j, k: (k, i)), pl.BlockSpec((bk, bn), lambda i, j, k: (k, j))],
        out_specs=pl.BlockSpec((bm, bn), lambda i, j, k: (i, j)),
        out_shape=jax.ShapeDtypeStruct((m, n), out_dtype),
        scratch_shapes=[pltpu.VMEM((bm, bn), F32)] if nk > 1 else [],
        compiler_params=_params(3),
    )(a, b)


def _mm_tn_blocks(a, b, wa, wb, name, after=None):
    L = a.shape[0]
    nb = a.shape[1] // wa
    bk = min(L, 2048)
    nk = L // bk

    def body(a_ref, b_ref, o_ref):
        @pl.when(pl.program_id(1) == 0)
        def _():
            o_ref[...] = jnp.zeros_like(o_ref)

        o_ref[...] += _dot_tn(_bf(a_ref[...]), _bf(b_ref[...]))

    call, tail = _call_after(
        after, body, 2, name=name, grid=(nb, nk),
        in_specs=[pl.BlockSpec((bk, wa), lambda j, k: (k, j)), pl.BlockSpec((bk, wb), lambda j, k: (k, j))],
        out_specs=pl.BlockSpec((None, wa, wb), lambda j, k: (j, 0, 0)),
        out_shape=jax.ShapeDtypeStruct((nb, wa, wb), F32),
        compiler_params=_params(2),
    )
    return call(a, b, *tail)


def _sum_parts(parts, owns, axes, chip, name):
    n_arr = len(parts)
    steps = 4
    in_specs, out_specs, shapes = [], [], []
    for part, axis in zip(parts, axes):
        _, r, c = part.shape
        tr = r // steps
        shapes.append((r, c))
        in_specs.append(pl.BlockSpec((3, tr, c), lambda i, k: (0, i, 0)))
        out_specs.append(pl.BlockSpec((tr, c), lambda i, k: (i, 0)))
    for part, axis in zip(parts, axes):
        _, r, c = part.shape
        tr = r // steps
        if axis == 1:
            in_specs.append(pl.BlockSpec((None, tr, c), lambda i, k: (0, steps * k[0] + i, 0)))
        else:
            in_specs.append(pl.BlockSpec((None, tr, c), lambda i, k: (0, i, k[0])))

    def body(k_ref, *refs):
        for a in range(n_arr):
            p = refs[a][...].astype(F32)
            refs[2 * n_arr + a][...] = ((p[0] + p[1]) + p[2]) + refs[n_arr + a][...].astype(F32)

    return pl.pallas_call(
        body, name=name,
        grid_spec=pltpu.PrefetchScalarGridSpec(num_scalar_prefetch=1, grid=(steps,), in_specs=in_specs, out_specs=out_specs),
        out_shape=[jax.ShapeDtypeStruct(sh, F32) for sh in shapes],
        compiler_params=_params(1),
    )(chip, *parts, *owns)


def _adamw(w, g_parts, m, v, name):
    n_g = len(g_parts)
    if w.ndim == 2:
        r, c = w.shape
        tr = r
        for cand in (512, 256, 128, 64, 32, 16, 8):
            if r % cand == 0 and cand * c * 4 <= (2 << 20):
                tr = cand
                break
        spec = pl.BlockSpec((tr, c), lambda i: (i, 0))
        tiling = dict(grid=(r // tr,), in_specs=[spec] * (3 + n_g), out_specs=[spec] * 4, compiler_params=_params(1))
    else:
        tiling = dict(compiler_params=pltpu.CompilerParams(vmem_limit_bytes=48 << 20))

    def body(*refs):
        w_ref, g_refs, m_ref, v_ref = refs[0], refs[1:1 + n_g], refs[1 + n_g], refs[2 + n_g]
        g = g_refs[0][...]
        for gr in g_refs[1:]:
            g = g + gr[...]
        _adamw_update(g, w_ref, m_ref, v_ref, *refs[3 + n_g:])

    return pl.pallas_call(body, name=name, out_shape=[jax.ShapeDtypeStruct(w.shape, F32)] * 4, **tiling)(w, *g_parts, m, v)


def _adamw_update(g, w_ref, m_ref, v_ref, g_out, d_out, m_out, v_out):
    m_new = ADAM_B1 * m_ref[...] + (1.0 - ADAM_B1) * g
    v_new = ADAM_B2 * v_ref[...] + (1.0 - ADAM_B2) * (g * g)
    m_hat = m_new * (1.0 / (1.0 - ADAM_B1 ** ADAM_STEP))
    v_hat = v_new * (1.0 / (1.0 - ADAM_B2 ** ADAM_STEP))
    g_out[...] = g
    d_out[...] = -ADAM_LR * (m_hat / (jnp.sqrt(v_hat) + ADAM_EPS) + ADAM_WD * w_ref[...])
    m_out[...] = m_new
    v_out[...] = v_new


def _adamw_many(ws, gs, ms, vs, name):
    n = len(ws)

    def body(*refs):
        for k in range(n):
            _adamw_update(refs[n + k][...], refs[k], refs[2 * n + k], refs[3 * n + k],
                          refs[4 * n + k], refs[5 * n + k], refs[6 * n + k], refs[7 * n + k])

    outs = pl.pallas_call(body, name=name, out_shape=[jax.ShapeDtypeStruct(t.shape, F32) for t in ws] * 4,
                          compiler_params=pltpu.CompilerParams(vmem_limit_bytes=56 << 20))(*ws, *gs, *ms, *vs)
    return [outs[part * n:(part + 1) * n] for part in range(4)]


def _adamw_layers(w, q_mine, q_sib, m, v, name, after=None):
    n, r, c = w.shape
    tr = r
    for cand in (512, 256, 128, 64, 32, 16, 8):
        if r % cand == 0 and cand * c * 4 <= (1 << 20):
            tr = cand
            break

    def body(*refs):
        w_ref, qm, qs, m_ref, v_ref = refs[0], refs[1:1 + n], refs[1 + n:1 + 2 * n], refs[1 + 2 * n], refs[2 + 2 * n]
        layer = pl.program_id(0)
        g = qm[0][...] + qs[0][...]
        for k in range(1, n):
            g = jnp.where(layer == k, qm[k][...] + qs[k][...], g)
        _adamw_update(g, w_ref, m_ref, v_ref, *refs[3 + 2 * n:])

    stacked = pl.BlockSpec((None, tr, c), lambda l, i: (l, i, 0))
    per_layer = [pl.BlockSpec((tr, c), lambda l, i, k=k: (jnp.where(l == k, i, 0), 0)) for k in range(n)]
    call, tail = _call_after(
        after, body, 3 + 2 * n, name=name, grid=(n, r // tr),
        in_specs=[stacked] + per_layer + per_layer + [stacked, stacked], out_specs=[stacked] * 4,
        out_shape=[jax.ShapeDtypeStruct(w.shape, F32)] * 4,
        compiler_params=_params(2),
    )
    return call(w, *q_mine, *q_sib, m, v, *tail)


def _ada_fwd(c16, ada_w, ada_b_cols):
    cols = ada_w.shape[2]

    def body(c_ref, w_ref, b_ref, o_ref):
        cv = c_ref[...]
        ca = _bf(cv * jax.nn.sigmoid(cv))
        o_ref[...] = _dot(ca, _bf(w_ref[...])) + b_ref[...]

    return pl.pallas_call(
        body, name="ada_fwd", grid=(DEPTH,),
        in_specs=[_whole((16, D)), pl.BlockSpec((None, D, cols), lambda i: (i, 0, 0)),
                  pl.BlockSpec((None, 1, cols), lambda i: (i, 0, 0))],
        out_specs=pl.BlockSpec((None, 16, cols), lambda i: (i, 0, 0)),
        out_shape=jax.ShapeDtypeStruct((DEPTH, 16, cols), F32),
        compiler_params=_params(1),
    )(c16, ada_w, ada_b_cols)


def _ada_bwd(c16, dmod16):
    cols = dmod16.shape[2]

    def body(c_ref, d_ref, o_ref):
        cv = c_ref[...]
        ca = _bf(cv * jax.nn.sigmoid(cv))
        o_ref[...] = _dot_tn(ca, _bf(d_ref[...]))

    return pl.pallas_call(
        body, name="ada_bwd", grid=(DEPTH,),
        in_specs=[_whole((16, D)), pl.BlockSpec((None, 16, cols), lambda i: (i, 0, 0))],
        out_specs=pl.BlockSpec((None, D, cols), lambda i: (i, 0, 0)),
        out_shape=jax.ShapeDtypeStruct((DEPTH, D, cols), F32),
        compiler_params=_params(1),
    )(c16, dmod16)


def _mod_bwd(vs_mix, vs_ffn, pv):
    def body(m_ref, f_ref, pv_ref, o_ref):
        for i in range(DEPTH):
            vm, vf, p = m_ref[i], f_ref[i], pv_ref[i]
            o_ref[i] = jnp.concatenate([
                vm[2:3], vm[1:2] * p[R_N1:R_N1 + 1], vm[0:1],
                vf[2:3], vf[1:2] * p[R_N2:R_N2 + 1], vf[0:1],
                vm[1:2] * (1.0 + p[R_SC1:R_SC1 + 1]), vf[1:2] * (1.0 + p[R_SC2:R_SC2 + 1])], axis=0)

    return pl.pallas_call(body, name="mod_bwd", out_shape=jax.ShapeDtypeStruct((DEPTH, 8, D), F32))(vs_mix, vs_ffn, pv)


def _ffn_fwd(x1, pv, w1, w2, layer, tm, after=None):
    L = x1.shape[0]
    dff = w1.shape[2]

    def body(x1_ref, pv_ref, w1_ref, w2_ref, x2_ref, h2_ref, a_ref, f_ref):
        x1v, p = x1_ref[...], pv_ref[...]
        h2, _, _ = _norm_mod(x1v, p[R_N2:R_N2 + 1], p[R_SC2:R_SC2 + 1], p[R_SH2:R_SH2 + 1])
        hb = _bf(h2)
        h2_ref[...] = hb
        a = _dot(hb, w1_ref[...])
        a_ref[...] = a
        ra = jnp.maximum(a, 0.0)
        f = _dot(_bf(ra * ra), w2_ref[...])
        f_ref[...] = f
        x2_ref[...] = x1v + p[R_G2:R_G2 + 1] * f

    call, tail = _call_after(
        after, body, 4, name=f"ffn_fwd{layer}", grid=(L // tm,),
        in_specs=[_rows(tm, D), pl.BlockSpec((None, 8, D), lambda i: (layer, 0, 0)), _layer_w(D, dff, 0), _layer_w(dff, D, 0)],
        out_specs=[_rows(tm, D), _rows(tm, D), _rows(tm, dff), _rows(tm, D)],
        out_shape=[jax.ShapeDtypeStruct((L, D), F32), jax.ShapeDtypeStruct((L, D), BF16),
                   jax.ShapeDtypeStruct((L, dff), F32), jax.ShapeDtypeStruct((L, D), F32)],
        compiler_params=_params(1, 56),
    )
    return call(x1, pv, w1, w2, *tail)


def _ffn_bwd(dx2, x1, a, f, pv, w1, w2, layer, tm, after=None):
    L = x1.shape[0]
    dff = w1.shape[2]
    extra = [] if after is None else [pl.BlockSpec(memory_space=pl.ANY)]
    extra_args = [] if after is None else [after]

    def body(dx2_ref, x1_ref, a_ref, f_ref, pv_ref, w1_ref, w2_ref, *rest):
        dx1_ref, p_ref, da_ref, df_ref, vs_ref = rest[len(extra):]

        @pl.when(pl.program_id(0) == 0)
        def _():
            vs_ref[...] = jnp.zeros_like(vs_ref)

        dx2v, p = dx2_ref[...], pv_ref[...]
        dfb = _bf(dx2v * p[R_G2:R_G2 + 1])
        df_ref[...] = dfb
        vs_ref[0:1, :] += _sum0(dx2v * f_ref[...])
        dp = _dot_nt(dfb, w2_ref[...])
        ra = jnp.maximum(a_ref[...], 0.0)
        p_ref[...] = _bf(ra * ra)
        dab = _bf(dp * (2.0 * ra))
        da_ref[...] = dab
        dh2 = _dot_nt(dab, w1_ref[...])
        _, xn, r = _norm_mod(x1_ref[...], p[R_N2:R_N2 + 1], p[R_SC2:R_SC2 + 1], p[R_SH2:R_SH2 + 1])
        dx1_ref[...] = dx2v + _norm_mod_bwd(dh2, xn, r, p[R_N2:R_N2 + 1], p[R_SC2:R_SC2 + 1])
        vs_ref[1:2, :] += _sum0(dh2 * xn)
        vs_ref[2:3, :] += _sum0(dh2)

    return pl.pallas_call(
        body, name=f"ffn_bwd{layer}", grid=(L // tm,),
        in_specs=[_rows(tm, D), _rows(tm, D), _rows(tm, dff), _rows(tm, D),
                  pl.BlockSpec((None, 8, D), lambda i: (layer, 0, 0)), _layer_w(D, dff, 0), _layer_w(dff, D, 0)] + extra,
        out_specs=[_rows(tm, D), _rows(tm, dff), _rows(tm, dff), _rows(tm, D), _whole((8, D))],
        out_shape=[jax.ShapeDtypeStruct((L, D), F32), jax.ShapeDtypeStruct((L, dff), BF16),
                   jax.ShapeDtypeStruct((L, dff), BF16), jax.ShapeDtypeStruct((L, D), BF16),
                   jax.ShapeDtypeStruct((8, D), F32)],
        compiler_params=_params(1, 56),
    )(dx2, x1, a, f, pv, w1, w2, *extra_args)


def _conv_fwd(x, pv, w_in, w_out, cw, layer, j, tm, after=None):
    L = x.shape[0]

    def body(x_ref, pv_ref, win_ref, wout_ref, cw_ref, x1_ref, h_ref, bcx_ref, conv_ref, q_ref, y_ref, carry):
        @pl.when(pl.program_id(0) == 0)
        def _():
            carry[...] = jnp.zeros_like(carry)

        xv, p, cwv = x_ref[...], pv_ref[...], cw_ref[...]
        h, _, _ = _norm_mod(xv, p[R_N1:R_N1 + 1], p[R_SC1:R_SC1 + 1], p[R_SH1:R_SH1 + 1])
        hb = _bf(h)
        h_ref[...] = hb
        bcx = _dot(hb, win_ref[...])
        bcx_ref[...] = bcx
        z = bcx[:, D:2 * D] * bcx[:, 2 * D:]
        prev8 = carry[...]
        conv = cwv[0:1] * _shift_down(z, prev8, 2) + cwv[1:2] * _shift_down(z, prev8, 1) + cwv[2:3] * z + cwv[3:4]
        conv_ref[...] = conv
        qb = _bf(bcx[:, :D] * conv)
        q_ref[...] = qb
        y = _dot(qb, wout_ref[...])
        y_ref[...] = y
        x1_ref[...] = xv + p[R_G1:R_G1 + 1] * y
        carry[...] = z[tm - 8:tm]

    call, tail = _call_after(
        after, body, 5, name=f"conv_fwd{layer}", grid=(L // tm,),
        in_specs=[_rows(tm, D), pl.BlockSpec((None, 8, D), lambda i: (layer, 0, 0)), _layer_w(D, 3 * D, 0), _layer_w(D, D, 0),
                  pl.BlockSpec((None, 8, D), lambda i: (j, 0, 0))],
        out_specs=[_rows(tm, D), _rows(tm, D), _rows(tm, 3 * D), _rows(tm, D), _rows(tm, D), _rows(tm, D)],
        out_shape=[jax.ShapeDtypeStruct((L, D), F32), jax.ShapeDtypeStruct((L, D), BF16), jax.ShapeDtypeStruct((L, 3 * D), F32),
                   jax.ShapeDtypeStruct((L, D), F32), jax.ShapeDtypeStruct((L, D), BF16), jax.ShapeDtypeStruct((L, D), F32)],
        scratch_shapes=[pltpu.VMEM((8, D), F32)],
        compiler_params=_params(1, 56),
    )
    return call(x, pv, w_in, w_out, cw, *tail)


def _conv_bwd(dx1, x, y, bcx, conv, pv, w_in, w_out, cw, layer, j, tm, after=None):
    L = x.shape[0]
    nt = L // tm

    def body(dx1_ref, x_ref, y_ref, bcx_ref, conv_ref, halo_ref, pv_ref, win_ref, wout_ref, cw_ref,
             dx_ref, dbcx_ref, dy_ref, vs_ref, carry):
        gi = pl.program_id(0)
        tile = nt - 1 - gi

        @pl.when(gi == 0)
        def _():
            vs_ref[...] = jnp.zeros_like(vs_ref)
            carry[...] = jnp.zeros_like(carry)

        dx1v, p, cwv = dx1_ref[...], pv_ref[...], cw_ref[...]
        dyb = _bf(dx1v * p[R_G1:R_G1 + 1])
        dy_ref[...] = dyb
        vs_ref[0:1, :] += _sum0(dx1v * y_ref[...])
        dq = _dot_nt(dyb, wout_ref[...])
        bcx = bcx_ref[...]
        b, cg, xh = bcx[:, :D], bcx[:, D:2 * D], bcx[:, 2 * D:]
        db = dq * conv_ref[...]
        dc = dq * b
        z = cg * xh
        halo = halo_ref[...]
        zprev = jnp.where(tile > 0, halo[:, D:2 * D] * halo[:, 2 * D:], 0.0)
        vs_ref[3:4, :] += _sum0(dc * _shift_down(z, zprev, 2))
        vs_ref[4:5, :] += _sum0(dc * _shift_down(z, zprev, 1))
        vs_ref[5:6, :] += _sum0(dc * z)
        vs_ref[6:7, :] += _sum0(dc)
        next8 = carry[...]
        dz = cwv[2:3] * dc + cwv[1:2] * _shift_up(dc, next8, 1) + cwv[0:1] * _shift_up(dc, next8, 2)
        dbb, dcgb, dxhb = _bf(db), _bf(dz * xh), _bf(dz * cg)
        dbcx_ref[:, 0:D] = dbb
        dbcx_ref[:, D:2 * D] = dcgb
        dbcx_ref[:, 2 * D:3 * D] = dxhb
        dh = (_dot_nt(dbb, win_ref[:, 0:D]) + _dot_nt(dcgb, win_ref[:, D:2 * D])) + _dot_nt(dxhb, win_ref[:, 2 * D:3 * D])
        _, xn, r = _norm_mod(x_ref[...], p[R_N1:R_N1 + 1], p[R_SC1:R_SC1 + 1], p[R_SH1:R_SH1 + 1])
        dx_ref[...] = dx1v + _norm_mod_bwd(dh, xn, r, p[R_N1:R_N1 + 1], p[R_SC1:R_SC1 + 1])
        vs_ref[1:2, :] += _sum0(dh * xn)
        vs_ref[2:3, :] += _sum0(dh)
        carry[...] = dc[0:8]

    halo_spec = pl.BlockSpec((8, 3 * D), lambda i: (jnp.maximum((nt - 1 - i) * (tm // 8) - 1, 0), 0))
    call, tail = _call_after(
        after, body, 10, name=f"conv_bwd{layer}", grid=(nt,),
        in_specs=[_rows(tm, D, nt), _rows(tm, D, nt), _rows(tm, D, nt), _rows(tm, 3 * D, nt), _rows(tm, D, nt), halo_spec,
                  pl.BlockSpec((None, 8, D), lambda i: (layer, 0, 0)), _layer_w(D, 3 * D, 0), _layer_w(D, D, 0),
                  pl.BlockSpec((None, 8, D), lambda i: (j, 0, 0))],
        out_specs=[_rows(tm, D, nt), _rows(tm, 3 * D, nt), _rows(tm, D, nt), _whole((8, D))],
        out_shape=[jax.ShapeDtypeStruct((L, D), F32), jax.ShapeDtypeStruct((L, 3 * D), BF16),
                   jax.ShapeDtypeStruct((L, D), BF16), jax.ShapeDtypeStruct((8, D), F32)],
        scratch_shapes=[pltpu.VMEM((8, D), F32)],
        compiler_params=_params(1, 56),
    )
    return call(dx1, x, y, bcx, conv, bcx, pv, w_in, w_out, cw, *tail)


def _s5_discretize(a_re, a_im, log_dt, bt_re, bt_im):
    dt = jnp.exp(log_dt)
    mag = jnp.exp(a_re * dt)
    abar_re = mag * jnp.cos(a_im * dt)
    abar_im = mag * jnp.sin(a_im * dt)
    den = a_re * a_re + a_im * a_im
    nr = abar_re - 1.0
    ni = abar_im
    f_re = (nr * a_re + ni * a_im) / den
    f_im = (ni * a_re - nr * a_im) / den
    bbar_re = f_re * bt_re - f_im * bt_im
    bbar_im = f_re * bt_im + f_im * bt_re
    return abar_re, abar_im, bbar_re, bbar_im


def _s5_params_fwd(a_re, a_im, log_dt, bt_re, bt_im, after=None):
    def body(ar, ai, ld, br, bi, o_ar, o_ai, o_br, o_bi):
        r = _s5_discretize(ar[...], ai[...], ld[...], br[...], bi[...])
        o_ar[...], o_ai[...], o_br[...], o_bi[...] = r

    gp = jax.ShapeDtypeStruct((S5_G, S5_P), F32)
    hgp = jax.ShapeDtypeStruct((S5_H, S5_G, S5_P), F32)
    call, tail = _call_after(after, body, 5, name="s5_params_fwd", out_shape=[gp, gp, hgp, hgp],
                             in_specs=[pl.BlockSpec(memory_space=pltpu.VMEM)] * 5)
    return call(a_re, a_im, log_dt, bt_re, bt_im, *tail)


def _s5_params_bwd(a_re, a_im, log_dt, bt_re, bt_im, d_ar, d_ai, d_br, d_bi):
    def body(ar, ai, ld, br, bi, gar, gai, gbr, gbi, o_ar, o_ai, o_ld, o_br, o_bi):
        _, vjp = jax.vjp(_s5_discretize, ar[...], ai[...], ld[...], br[...], bi[...])
        r = vjp((gar[...], gai[...], gbr[...], gbi[...]))
        o_ar[...], o_ai[...], o_ld[...], o_br[...], o_bi[...] = r

    gp = jax.ShapeDtypeStruct((S5_G, S5_P), F32)
    hgp = jax.ShapeDtypeStruct((S5_H, S5_G, S5_P), F32)
    return pl.pallas_call(body, name="s5_params_bwd", out_shape=[gp, gp, jax.ShapeDtypeStruct((S5_G, 1), F32), hgp, hgp])(
        a_re, a_im, log_dt, bt_re, bt_im, d_ar, d_ai, d_br, d_bi)


NSEG = 8
SCAN_LANES = 1024


def _to_segments(x):
    n, c = x.shape
    return x.reshape(NSEG, n // NSEG, c).transpose(1, 0, 2).reshape(n, c)


def _from_segments(x):
    n, c = x.shape
    return x.reshape(n // NSEG, NSEG, c).transpose(1, 0, 2).reshape(n, c)


def _segment_scan(re_ref, im_ref, st_re, st_im, a_re, a_im, n_slabs, adjoint, write):
    for q in range(NSTATE // SCAN_LANES):
        ls = slice(q * SCAN_LANES, (q + 1) * SCAN_LANES)
        ar = jnp.broadcast_to(a_re[:, ls], (8, SCAN_LANES))
        ai = jnp.broadcast_to(a_im[:, ls], (8, SCAN_LANES))

        def step(k, carry, ls=ls, ar=ar, ai=ai):
            s_r, s_i = carry
            slab = (n_slabs - 1 - k) if adjoint else k
            rows = pl.ds(pl.multiple_of(slab * 8, 8), 8)
            b_r, b_i = re_ref[rows, ls], im_ref[rows, ls]
            if adjoint:
                n_r = b_r + ar * s_r + ai * s_i
                n_i = b_i - ai * s_r + ar * s_i
            else:
                n_r = ar * s_r - ai * s_i + b_r
                n_i = ar * s_i + ai * s_r + b_i
            if write:
                re_ref[rows, ls] = n_r
                im_ref[rows, ls] = n_i
            return n_r, n_i

        s_r, s_i = lax.fori_loop(0, n_slabs, step, (st_re[:, ls], st_im[:, ls]), unroll=4)
        st_re[:, ls] = s_r
        st_im[:, ls] = s_i


def _s5_segment_states(e_re, e_im, ar, ai, seg_len, adjoint):
    def body(ere_ref, eim_ref, ar_ref, ai_ref, ore_ref, oim_ref):
        p_r, p_i = ar_ref[...], ai_ref[...]
        if adjoint:
            p_i = -p_i
        acc_r, acc_i = jnp.ones_like(p_r), jnp.zeros_like(p_r)
        n = seg_len
        while n:
            if n & 1:
                acc_r, acc_i = acc_r * p_r - acc_i * p_i, acc_r * p_i + acc_i * p_r
            n >>= 1
            if n:
                p_r, p_i = p_r * p_r - p_i * p_i, 2.0 * p_r * p_i
        e_r, e_i = ere_ref[...], eim_ref[...]
        s_r, s_i = jnp.zeros_like(acc_r), jnp.zeros_like(acc_r)
        rows_r, rows_i = [None] * NSEG, [None] * NSEG
        order = range(NSEG - 1, -1, -1) if adjoint else range(NSEG)
        for j in order:
            rows_r[j], rows_i[j] = s_r, s_i
            s_r, s_i = (acc_r * s_r - acc_i * s_i + e_r[j:j + 1], acc_r * s_i + acc_i * s_r + e_i[j:j + 1])
        ore_ref[...] = jnp.concatenate(rows_r, axis=0)
        oim_ref[...] = jnp.concatenate(rows_i, axis=0)

    st = jax.ShapeDtypeStruct((NSEG, NSTATE), F32)
    return pl.pallas_call(body, name="s5_segment_states_bwd" if adjoint else "s5_segment_states_fwd", out_shape=[st, st])(
        e_re, e_im, ar, ai)


def _s5_fwd_ends(x, pv, w_in, b_re, b_im, ar, ai, layer, tm, after=None):
    L = x.shape[0]

    def body(x_ref, pv_ref, win_ref, bre_ref, bim_ref, ar_ref, ai_ref, h_ref, u_ref, ere_ref, eim_ref, bu_re, bu_im):
        @pl.when(pl.program_id(0) == 0)
        def _():
            ere_ref[...] = jnp.zeros_like(ere_ref)
            eim_ref[...] = jnp.zeros_like(eim_ref)

        p = pv_ref[...]
        h, _, _ = _norm_mod(x_ref[...], p[R_N1:R_N1 + 1], p[R_SC1:R_SC1 + 1], p[R_SH1:R_SH1 + 1])
        hb = _bf(h)
        h_ref[...] = hb
        u = _dot(hb, win_ref[...])
        u_ref[...] = u
        ub = _bf(u)
        for k in range(S5_NB):
            uk = ub[:, k * S5_BH:(k + 1) * S5_BH]
            bu_re[:, k * S5_BP:(k + 1) * S5_BP] = _dot(uk, bre_ref[k])
            bu_im[:, k * S5_BP:(k + 1) * S5_BP] = _dot(uk, bim_ref[k])
        _segment_scan(bu_re, bu_im, ere_ref, eim_ref, ar_ref[...], ai_ref[...], tm // 8, adjoint=False, write=False)

    call, tail = _call_after(
        after, body, 7, name="s5_fwd_ends", grid=(L // tm,),
        in_specs=[_rows(tm, D), pl.BlockSpec((None, 8, D), lambda i: (layer, 0, 0)), _layer_w(D, D, 0),
                  _const_w((S5_NB, S5_BH, S5_BP)), _const_w((S5_NB, S5_BH, S5_BP)), _whole((1, NSTATE)), _whole((1, NSTATE))],
        out_specs=[_rows(tm, D), _rows(tm, D), _whole((NSEG, NSTATE)), _whole((NSEG, NSTATE))],
        out_shape=[jax.ShapeDtypeStruct((L, D), BF16), jax.ShapeDtypeStruct((L, D), F32),
                   jax.ShapeDtypeStruct((NSEG, NSTATE), F32), jax.ShapeDtypeStruct((NSEG, NSTATE), F32)],
        scratch_shapes=[pltpu.VMEM((tm, NSTATE), F32), pltpu.VMEM((tm, NSTATE), F32)],
        compiler_params=_params(1, 56),
    )
    return call(x, pv, w_in, b_re, b_im, ar, ai, *tail)


def _s5_fwd_out(x, u, pv, b_re, b_im, s0_re, s0_im, ar, ai, c_re, c_im, dvec, glu_w, glu_b, w_out, layer, tm):
    L = x.shape[0]

    def body(x_ref, u_ref, pv_ref, bre_ref, bim_ref, s0re_ref, s0im_ref, ar_ref, ai_ref, cre_ref, cim_ref, d_ref, gw_ref,
             gb_ref, wout_ref, x1_ref, sre_ref, sim_ref, y1_ref, zg_ref, y3_ref, y_ref, srb_ref, sib_ref, st_re, st_im):
        @pl.when(pl.program_id(0) == 0)
        def _():
            st_re[...] = s0re_ref[...]
            st_im[...] = s0im_ref[...]

        p = pv_ref[...]
        uv = u_ref[...]
        ub = _bf(uv)
        for k in range(S5_NB):
            uk = ub[:, k * S5_BH:(k + 1) * S5_BH]
            sre_ref[:, k * S5_BP:(k + 1) * S5_BP] = _dot(uk, bre_ref[k])
            sim_ref[:, k * S5_BP:(k + 1) * S5_BP] = _dot(uk, bim_ref[k])
        _segment_scan(sre_ref, sim_ref, st_re, st_im, ar_ref[...], ai_ref[...], tm // 8, adjoint=False, write=True)
        parts = []
        for k in range(S5_NB):
            sl = slice(k * S5_BP, (k + 1) * S5_BP)
            srb, sib = _bf(sre_ref[:, sl]), _bf(sim_ref[:, sl])
            srb_ref[:, sl] = srb
            sib_ref[:, sl] = sib
            parts.append(_dot(srb, cre_ref[k]) - _dot(sib, cim_ref[k]))
        y1 = jnp.concatenate(parts, axis=1) + d_ref[...] * uv
        y1_ref[...] = y1
        y2 = jax.nn.gelu(y1)
        zg = _dot(_bf(y2), gw_ref[...]) + gb_ref[...]
        zg_ref[...] = zg
        y3b = _bf(y2 * jax.nn.sigmoid(zg))
        y3_ref[...] = y3b
        y = _dot(y3b, wout_ref[...])
        y_ref[...] = y
        x1_ref[...] = x_ref[...] + p[R_G1:R_G1 + 1] * y

    return pl.pallas_call(
        body, name="s5_fwd_out", grid=(L // tm,),
        in_specs=[_rows(tm, D), _rows(tm, D), pl.BlockSpec((None, 8, D), lambda i: (layer, 0, 0)),
                  _const_w((S5_NB, S5_BH, S5_BP)), _const_w((S5_NB, S5_BH, S5_BP)),
                  _whole((NSEG, NSTATE)), _whole((NSEG, NSTATE)), _whole((1, NSTATE)), _whole((1, NSTATE)),
                  _const_w((S5_NB, S5_BP, S5_BH)), _const_w((S5_NB, S5_BP, S5_BH)), _whole((1, D)),
                  _layer_w(D, D, 0), _whole((1, D)), _layer_w(D, D, 0)],
        out_specs=[_rows(tm, D), _rows(tm, NSTATE), _rows(tm, NSTATE), _rows(tm, D), _rows(tm, D), _rows(tm, D), _rows(tm, D),
                   _rows(tm, NSTATE), _rows(tm, NSTATE)],
        out_shape=[jax.ShapeDtypeStruct((L, D), F32), jax.ShapeDtypeStruct((L, NSTATE), F32), jax.ShapeDtypeStruct((L, NSTATE), F32),
                   jax.ShapeDtypeStruct((L, D), F32), jax.ShapeDtypeStruct((L, D), F32),
                   jax.ShapeDtypeStruct((L, D), BF16), jax.ShapeDtypeStruct((L, D), F32),
                   jax.ShapeDtypeStruct((L, NSTATE), BF16), jax.ShapeDtypeStruct((L, NSTATE), BF16)],
        scratch_shapes=[pltpu.VMEM((NSEG, NSTATE), F32), pltpu.VMEM((NSEG, NSTATE), F32)],
        compiler_params=_params(1, 60),
    )(x, u, pv, b_re, b_im, s0_re, s0_im, ar, ai, c_re, c_im, dvec, glu_w, glu_b, w_out)


def _s5_bwd_ends(dx1, y, y1, zg, u, pv, c_re, c_im, ar, ai, dvec, glu_w, w_out, layer, tm, after=None):
    L = dx1.shape[0]
    nt = L // tm

    def body(dx1_ref, y_ref, y1_ref, zg_ref, u_ref, pv_ref, cre_ref, cim_ref, ar_ref, ai_ref, d_ref, gw_ref, wout_ref,
             dy_ref, y2_ref, dzg_ref, dy1_ref, dus_ref, ere_ref, eim_ref, vs_ref, g_re, g_im):
        @pl.when(pl.program_id(0) == 0)
        def _():
            vs_ref[...] = jnp.zeros_like(vs_ref)
            ere_ref[...] = jnp.zeros_like(ere_ref)
            eim_ref[...] = jnp.zeros_like(eim_ref)

        dx1v, p = dx1_ref[...], pv_ref[...]
        dyb = _bf(dx1v * p[R_G1:R_G1 + 1])
        dy_ref[...] = dyb
        vs_ref[0:1, :] += _sum0(dx1v * y_ref[...])
        dy3 = _dot_nt(dyb, wout_ref[...])
        y2, gelu_vjp = jax.vjp(jax.nn.gelu, y1_ref[...])
        y2_ref[...] = _bf(y2)
        gate = jax.nn.sigmoid(zg_ref[...])
        dzg = dy3 * y2 * gate * (1.0 - gate)
        dzgb = _bf(dzg)
        dzg_ref[...] = dzgb
        vs_ref[1:2, :] += _sum0(dzg)
        dy2 = dy3 * gate + _dot_nt(dzgb, gw_ref[...])
        dy1 = gelu_vjp(dy2)[0]
        vs_ref[2:3, :] += _sum0(dy1 * u_ref[...])
        dus_ref[...] = dy1 * d_ref[...]
        dy1b = _bf(dy1)
        dy1_ref[...] = dy1b
        for k in range(S5_NB):
            dk = dy1b[:, k * S5_BH:(k + 1) * S5_BH]
            g_re[:, k * S5_BP:(k + 1) * S5_BP] = _dot_nt(dk, cre_ref[k])
            g_im[:, k * S5_BP:(k + 1) * S5_BP] = -_dot_nt(dk, cim_ref[k])
        _segment_scan(g_re, g_im, ere_ref, eim_ref, ar_ref[...], ai_ref[...], tm // 8, adjoint=True, write=False)

    call, tail = _call_after(
        after, body, 13, name="s5_bwd_ends", grid=(nt,),
        in_specs=[_rows(tm, D, nt)] * 5 + [pl.BlockSpec((None, 8, D), lambda i: (layer, 0, 0)),
                  _const_w((S5_NB, S5_BP, S5_BH)), _const_w((S5_NB, S5_BP, S5_BH)), _whole((1, NSTATE)), _whole((1, NSTATE)),
                  _whole((1, D)), _layer_w(D, D, 0), _layer_w(D, D, 0)],
        out_specs=[_rows(tm, D, nt)] * 5 + [_whole((NSEG, NSTATE)), _whole((NSEG, NSTATE)), _whole((8, D))],
        out_shape=[jax.ShapeDtypeStruct((L, D), BF16)] * 4 + [jax.ShapeDtypeStruct((L, D), F32),
                   jax.ShapeDtypeStruct((NSEG, NSTATE), F32), jax.ShapeDtypeStruct((NSEG, NSTATE), F32),
                   jax.ShapeDtypeStruct((8, D), F32)],
        scratch_shapes=[pltpu.VMEM((tm, NSTATE), F32), pltpu.VMEM((tm, NSTATE), F32)],
        compiler_params=_params(1, 56),
    )
    return call(dx1, y, y1, zg, u, pv, c_re, c_im, ar, ai, dvec, glu_w, w_out, *tail)


def _s5_bwd_in(dx1, dy1_b, du_skip, x, s_re, s_im, pv, b_re, b_im, c_re, c_im, l0_re, l0_im, ar, ai, w_in, layer, tm):
    L = x.shape[0]
    nt = L // tm

    def body(dx1_ref, dy1_ref, dus_ref, x_ref, sre_ref, sim_ref, hre_ref, him_ref, lre_ref, lim_ref, pv_ref, bre_ref, bim_ref,
             cre_ref, cim_ref, l0re_ref, l0im_ref, ar_ref, ai_ref, win_ref,
             dx_ref, du_ref, lamre_ref, lamim_ref, da_ref, vs_ref, g_re, g_im, st_re, st_im):
        gi = pl.program_id(0)
        tile = nt - 1 - gi

        @pl.when(gi == 0)
        def _():
            vs_ref[...] = jnp.zeros_like(vs_ref)
            da_ref[...] = jnp.zeros_like(da_ref)
            st_re[...] = l0re_ref[...]
            st_im[...] = l0im_ref[...]

        p = pv_ref[...]
        dy1b = dy1_ref[...]
        for k in range(S5_NB):
            dk = dy1b[:, k * S5_BH:(k + 1) * S5_BH]
            g_re[:, k * S5_BP:(k + 1) * S5_BP] = _dot_nt(dk, cre_ref[k])
            g_im[:, k * S5_BP:(k + 1) * S5_BP] = -_dot_nt(dk, cim_ref[k])
        _segment_scan(g_re, g_im, st_re, st_im, ar_ref[...], ai_ref[...], tm // 8, adjoint=True, write=True)
        lam_r, lam_i = g_re[...], g_im[...]
        lrb, lib = _bf(lam_r), _bf(lam_i)
        lamre_ref[...] = lrb
        lamim_ref[...] = lib

        def wrapped(last_ref):
            z = last_ref[...]
            row = lax.broadcasted_iota(jnp.int32, z.shape, 0)
            return jnp.where(row >= 1, pltpu.roll(z, 1, 0), 0.0)

        first_r = jnp.where(tile > 0, hre_ref[...], wrapped(lre_ref))
        first_i = jnp.where(tile > 0, him_ref[...], wrapped(lim_ref))
        sp_r = jnp.concatenate([first_r, sre_ref[0:tm - 8, :]], axis=0)
        sp_i = jnp.concatenate([first_i, sim_ref[0:tm - 8, :]], axis=0)
        da_ref[0:1, :] += _sum0(lam_r * sp_r + lam_i * sp_i)
        da_ref[1:2, :] += _sum0(lam_i * sp_r - lam_r * sp_i)

        parts = []
        for k in range(S5_NB):
            sl = slice(k * S5_BP, (k + 1) * S5_BP)
            parts.append(_dot_nt(lrb[:, sl], bre_ref[k]) + _dot_nt(lib[:, sl], bim_ref[k]))
        dub = _bf(jnp.concatenate(parts, axis=1) + dus_ref[...])
        du_ref[...] = dub
        dh = _dot_nt(dub, win_ref[...])
        _, xn, r = _norm_mod(x_ref[...], p[R_N1:R_N1 + 1], p[R_SC1:R_SC1 + 1], p[R_SH1:R_SH1 + 1])
        dx_ref[...] = dx1_ref[...] + _norm_mod_bwd(dh, xn, r, p[R_N1:R_N1 + 1], p[R_SC1:R_SC1 + 1])
        vs_ref[1:2, :] += _sum0(dh * xn)
        vs_ref[2:3, :] += _sum0(dh)

    halo = pl.BlockSpec((8, NSTATE), lambda i: (jnp.maximum((nt - 1 - i) * (tm // 8) - 1, 0), 0))
    last = pl.BlockSpec((8, NSTATE), lambda i: (L // 8 - 1, 0))
    return pl.pallas_call(
        body, name="s5_bwd_in", grid=(nt,),
        in_specs=[_rows(tm, D, nt), _rows(tm, D, nt), _rows(tm, D, nt), _rows(tm, D, nt), _rows(tm, NSTATE, nt), _rows(tm, NSTATE, nt),
                  halo, halo, last, last, pl.BlockSpec((None, 8, D), lambda i: (layer, 0, 0)),
                  _const_w((S5_NB, S5_BH, S5_BP)), _const_w((S5_NB, S5_BH, S5_BP)),
                  _const_w((S5_NB, S5_BP, S5_BH)), _const_w((S5_NB, S5_BP, S5_BH)),
                  _whole((NSEG, NSTATE)), _whole((NSEG, NSTATE)), _whole((1, NSTATE)), _whole((1, NSTATE)), _layer_w(D, D, 0)],
        out_specs=[_rows(tm, D, nt), _rows(tm, D, nt), _rows(tm, NSTATE, nt), _rows(tm, NSTATE, nt), _whole((8, NSTATE)), _whole((8, D))],
        out_shape=[jax.ShapeDtypeStruct((L, D), F32), jax.ShapeDtypeStruct((L, D), BF16),
                   jax.ShapeDtypeStruct((L, NSTATE), BF16), jax.ShapeDtypeStruct((L, NSTATE), BF16),
                   jax.ShapeDtypeStruct((8, NSTATE), F32), jax.ShapeDtypeStruct((8, D), F32)],
        scratch_shapes=[pltpu.VMEM((tm, NSTATE), F32), pltpu.VMEM((tm, NSTATE), F32),
                        pltpu.VMEM((NSEG, NSTATE), F32), pltpu.VMEM((NSEG, NSTATE), F32)],
        compiler_params=_params(1, 60),
    )(dx1, dy1_b, du_skip, x, s_re, s_im, s_re, s_im, s_re, s_im, pv, b_re, b_im, c_re, c_im, l0_re, l0_im, ar, ai, w_in)


def _blockdiag_b(bt):
    b = bt.reshape(S5_H, S5_NB, 16, S5_P).transpose(1, 2, 0, 3)
    eye = jnp.eye(16, dtype=bt.dtype)
    return (b[:, :, :, None, :] * eye[None, :, None, :, None]).reshape(S5_NB, S5_BH, S5_BP)


def _unblock_b(d):
    d = jnp.einsum("bghgp->bghp", d.reshape(S5_NB, 16, S5_H, 16, S5_P))
    return d.transpose(2, 0, 1, 3).reshape(S5_H, S5_G, S5_P)


def _blockdiag_c(cm):
    c4 = cm.reshape(S5_NB, 16, S5_H, S5_P)
    eye = jnp.eye(16, dtype=cm.dtype)
    out = c4.transpose(0, 1, 3, 2)[:, :, :, None, :] * eye[None, :, None, :, None]
    return out.reshape(S5_NB, S5_BP, S5_BH)


def _unblock_c(d):
    d = jnp.einsum("bgpgh->bghp", d.reshape(S5_NB, 16, S5_P, 16, S5_H))
    return d.reshape(S5_G, S5_H, S5_P)


def _tril_mask():
    return lax.broadcasted_iota(jnp.int32, (SG_CHUNK, SG_CHUNK), 0) >= lax.broadcasted_iota(jnp.int32, (SG_CHUNK, SG_CHUNK), 1)


def _sg_fwd(x, pv, w_in, w_s, b_t, vg, w_out, layer, tm, after=None):
    L = x.shape[0]
    nc = tm // SG_CHUNK

    def body(x_ref, pv_ref, win_ref, ws_ref, bt_ref, vg_ref, wout_ref, x1_ref, h_ref, uv_ref, vm_ref, q_ref, y_ref):
        xv, p = x_ref[...], pv_ref[...]
        h, _, _ = _norm_mod(xv, p[R_N1:R_N1 + 1], p[R_SC1:R_SC1 + 1], p[R_SH1:R_SH1 + 1])
        hb = _bf(h)
        h_ref[...] = hb
        uv = _dot(hb, win_ref[...])
        uv_ref[...] = uv
        v = uv[:, D:]
        rv = lax.rsqrt(jnp.mean(v * v, axis=-1, keepdims=True) + EPS)
        vnb = _bf((v * rv) * vg_ref[...])
        mask = _tril_mask()
        bt = bt_ref[...]
        for hd in range(SG_HEADS):
            wm = _bf(jnp.where(mask, ws_ref[hd], 0.0))
            cs = slice(hd * SG_CHUNK, (hd + 1) * SG_CHUNK)
            for ck in range(nc):
                rs = slice(ck * SG_CHUNK, (ck + 1) * SG_CHUNK)
                vm_ref[rs, cs] = _dot(wm, vnb[rs, cs]) + bt[:, hd:hd + 1]
        qb = _bf(uv[:, :D] * vm_ref[...])
        q_ref[...] = qb
        y = _dot(qb, wout_ref[...])
        y_ref[...] = y
        x1_ref[...] = xv + p[R_G1:R_G1 + 1] * y

    call, tail = _call_after(
        after, body, 7, name="sg_fwd", grid=(L // tm,),
        in_specs=[_rows(tm, D), pl.BlockSpec((None, 8, D), lambda i: (layer, 0, 0)), _layer_w(D, 2 * D, 0),
                  _whole((SG_HEADS, SG_CHUNK, SG_CHUNK)), _whole((SG_CHUNK, SG_HEADS)), _whole((1, D)), _layer_w(D, D, 0)],
        out_specs=[_rows(tm, D), _rows(tm, D), _rows(tm, 2 * D), _rows(tm, D), _rows(tm, D), _rows(tm, D)],
        out_shape=[jax.ShapeDtypeStruct((L, D), F32), jax.ShapeDtypeStruct((L, D), BF16), jax.ShapeDtypeStruct((L, 2 * D), F32),
                   jax.ShapeDtypeStruct((L, D), F32), jax.ShapeDtypeStruct((L, D), BF16), jax.ShapeDtypeStruct((L, D), F32)],
        compiler_params=_params(1, 56),
    )
    return call(x, pv, w_in, w_s, b_t, vg, w_out, *tail)


def _sg_bwd(dx1, x, y, uv, vm, pv, w_in, w_s, vg, w_out, layer, tm, after=None):
    L = x.shape[0]
    nc = tm // SG_CHUNK

    def body(dx1_ref, x_ref, y_ref, uv_ref, vm_ref, pv_ref, win_ref, ws_ref, vg_ref, wout_ref,
             dx_ref, duv_ref, dy_ref, vs_ref, dws_ref, dbt_ref, dvn_scr):
        @pl.when(pl.program_id(0) == 0)
        def _():
            vs_ref[...] = jnp.zeros_like(vs_ref)
            dws_ref[...] = jnp.zeros_like(dws_ref)
            dbt_ref[...] = jnp.zeros_like(dbt_ref)

        dx1v, p = dx1_ref[...], pv_ref[...]
        dyb = _bf(dx1v * p[R_G1:R_G1 + 1])
        dy_ref[...] = dyb
        vs_ref[0:1, :] += _sum0(dx1v * y_ref[...])
        dq = _dot_nt(dyb, wout_ref[...])
        uv = uv_ref[...]
        u, v = uv[:, :D], uv[:, D:]
        dub = _bf(dq * vm_ref[...])
        dvm = dq * u
        dvmb = _bf(dvm)
        rv = lax.rsqrt(jnp.mean(v * v, axis=-1, keepdims=True) + EPS)
        vh = v * rv
        vgv = vg_ref[...]
        vnb = _bf(vh * vgv)
        mask = _tril_mask()
        for hd in range(SG_HEADS):
            wm = _bf(jnp.where(mask, ws_ref[hd], 0.0))
            cs = slice(hd * SG_CHUNK, (hd + 1) * SG_CHUNK)
            dws = jnp.zeros((SG_CHUNK, SG_CHUNK), F32)
            dbs = jnp.zeros((SG_CHUNK, 1), F32)
            for ck in range(nc):
                rs = slice(ck * SG_CHUNK, (ck + 1) * SG_CHUNK)
                dvn_scr[rs, cs] = _dot_tn(wm, dvmb[rs, cs])
                dws = dws + _dot_nt(dvmb[rs, cs], vnb[rs, cs])
                dbs = dbs + jnp.sum(dvm[rs, cs], axis=1, keepdims=True)
            dws_ref[hd] += jnp.where(mask, dws, 0.0)
            dbt_ref[:, hd:hd + 1] += dbs
        dvn = dvn_scr[...]
        vs_ref[3:4, :] += _sum0(dvn * vh)
        dvnn = dvn * vgv
        dvb = _bf(rv * (dvnn - vh * jnp.mean(dvnn * vh, axis=-1, keepdims=True)))
        duv_ref[:, 0:D] = dub
        duv_ref[:, D:2 * D] = dvb
        dh = _dot_nt(dub, win_ref[:, 0:D]) + _dot_nt(dvb, win_ref[:, D:2 * D])
        _, xn, r = _norm_mod(x_ref[...], p[R_N1:R_N1 + 1], p[R_SC1:R_SC1 + 1], p[R_SH1:R_SH1 + 1])
        dx_ref[...] = dx1v + _norm_mod_bwd(dh, xn, r, p[R_N1:R_N1 + 1], p[R_SC1:R_SC1 + 1])
        vs_ref[1:2, :] += _sum0(dh * xn)
        vs_ref[2:3, :] += _sum0(dh)

    call, tail = _call_after(
        after, body, 10, name="sg_bwd", grid=(L // tm,),
        in_specs=[_rows(tm, D), _rows(tm, D), _rows(tm, D), _rows(tm, 2 * D), _rows(tm, D),
                  pl.BlockSpec((None, 8, D), lambda i: (layer, 0, 0)), _layer_w(D, 2 * D, 0),
                  _whole((SG_HEADS, SG_CHUNK, SG_CHUNK)), _whole((1, D)), _layer_w(D, D, 0)],
        out_specs=[_rows(tm, D), _rows(tm, 2 * D), _rows(tm, D), _whole((8, D)),
                   _whole((SG_HEADS, SG_CHUNK, SG_CHUNK)), _whole((SG_CHUNK, SG_HEADS))],
        out_shape=[jax.ShapeDtypeStruct((L, D), F32), jax.ShapeDtypeStruct((L, 2 * D), BF16), jax.ShapeDtypeStruct((L, D), BF16),
                   jax.ShapeDtypeStruct((8, D), F32), jax.ShapeDtypeStruct((SG_HEADS, SG_CHUNK, SG_CHUNK), F32),
                   jax.ShapeDtypeStruct((SG_CHUNK, SG_HEADS), F32)],
        scratch_shapes=[pltpu.VMEM((tm, D), F32)],
        compiler_params=_params(1, 56),
    )
    return call(dx1, x, y, uv, vm, pv, w_in, w_s, vg, w_out, *tail)


def _final(x, target, fg, tm):
    L = x.shape[0]

    def body(x_ref, t_ref, g_ref, dx_ref, vs_ref):
        @pl.when(pl.program_id(0) == 0)
        def _():
            vs_ref[...] = jnp.zeros_like(vs_ref)

        xv, g = x_ref[...], g_ref[...]
        r = lax.rsqrt(jnp.mean(xv * xv, axis=-1, keepdims=True) + EPS)
        xn = xv * r
        e = xn * g - t_ref[...]
        vs_ref[0:1, :] += jnp.sum(e * e)
        dout = e * (1.0 / D)
        vs_ref[1:2, :] += _sum0(dout * xn)
        dxn = dout * g
        dx_ref[...] = r * (dxn - xn * jnp.mean(dxn * xn, axis=-1, keepdims=True))

    return pl.pallas_call(
        body, name="final_loss", grid=(L // tm,),
        in_specs=[_rows(tm, D), _rows(tm, D), _whole((1, D))],
        out_specs=[_rows(tm, D), _whole((8, D))],
        out_shape=[jax.ShapeDtypeStruct((L, D), F32), jax.ShapeDtypeStruct((8, D), F32)],
        compiler_params=_params(1),
    )(x, target, fg)


def _pack_flat(arrs, multiple=LANES):
    flat = jnp.concatenate([a.reshape(-1).astype(F32) for a in arrs])
    return jnp.pad(flat, (0, -flat.shape[0] % multiple))


def _pack(arrs, row_multiple=8):
    return _pack_flat(arrs, row_multiple * LANES).reshape(-1, LANES)


def _unpack(buf, shapes, lead=()):
    flat = buf.reshape(lead + (-1,))
    out, off = [], 0
    for s in shapes:
        n = 1
        for d in s:
            n *= d
        out.append(flat[..., off:off + n].reshape(lead + tuple(s)))
        off += n
    return out


BIG = ("ff_w1", "ff_w2", "conv_w_in", "conv_w_out", "ssm_w_in", "ssm_glu_w", "ssm_w_out", "sg_w_in", "sg_w_out")
BIG_AXIS = {"ff_w1": 2, "ff_w2": 1, "conv_w_in": 2, "conv_w_out": 1, "ssm_w_in": 1, "ssm_glu_w": 1, "ssm_w_out": 1,
            "sg_w_in": 2, "sg_w_out": 1}
LAYER_WEIGHTS = (
    (("conv_w_in", 0), ("conv_w_out", 0), ("ff_w1", 0), ("ff_w2", 0)),
    (("ssm_w_in", 0), ("ssm_glu_w", 0), ("ssm_w_out", 0), ("ff_w1", 1), ("ff_w2", 1)),
    (("sg_w_in", 0), ("sg_w_out", 0), ("ff_w1", 2), ("ff_w2", 2)),
    (("conv_w_in", 1), ("conv_w_out", 1), ("ff_w1", 3), ("ff_w2", 3)),
)
GATHER_GROUPS = tuple(grp for lw in LAYER_WEIGHTS for grp in (lw[:-2], lw[-2:]))
SMALL_SHARDED = ("conv_w", "conv_b", "sg_v_g")
SMALL_WIDE_PADDED = ("ssm_b_re", "ssm_b_im")
SMALL = ("ada_b", "norm1_g", "norm2_g", "final_g", "ssm_a_re", "ssm_a_im", "ssm_log_dt", "ssm_b_re", "ssm_b_im", "ssm_c_re",
         "ssm_c_im", "ssm_d", "ssm_glu_b", "sg_w_s", "sg_b_s") + SMALL_SHARDED
WEIGHTS = ("ada_w", "ada_b", "norm1_g", "norm2_g", "ff_w1", "ff_w2", "final_g", "conv_w_in", "conv_w", "conv_b", "conv_w_out",
           "ssm_w_in", "ssm_a_re", "ssm_a_im", "ssm_log_dt", "ssm_b_re", "ssm_b_im", "ssm_c_re", "ssm_c_im", "ssm_d",
           "ssm_glu_w", "ssm_glu_b", "ssm_w_out", "sg_w_in", "sg_v_g", "sg_w_s", "sg_b_s", "sg_w_out")


def kernel(x, c, ada_w, ada_b, norm1_g, norm2_g, ff_w1, ff_w2, final_g, conv_w_in, conv_w, conv_b, conv_w_out, ssm_w_in, ssm_a_re, ssm_a_im, ssm_log_dt, ssm_b_re, ssm_b_im, ssm_c_re, ssm_c_im, ssm_d, ssm_glu_w, ssm_glu_b, ssm_w_out, sg_w_in, sg_v_g, sg_w_s, sg_b_s, sg_w_out, loss_target, m_ada_w, m_ada_b, m_norm1_g, m_norm2_g, m_ff_w1, m_ff_w2, m_final_g, m_conv_w_in, m_conv_w, m_conv_b, m_conv_w_out, m_ssm_w_in, m_ssm_a_re, m_ssm_a_im, m_ssm_log_dt, m_ssm_b_re, m_ssm_b_im, m_ssm_c_re, m_ssm_c_im, m_ssm_d, m_ssm_glu_w, m_ssm_glu_b, m_ssm_w_out, m_sg_w_in, m_sg_v_g, m_sg_w_s, m_sg_b_s, m_sg_w_out, v_ada_w, v_ada_b, v_norm1_g, v_norm2_g, v_ff_w1, v_ff_w2, v_final_g, v_conv_w_in, v_conv_w, v_conv_b, v_conv_w_out, v_ssm_w_in, v_ssm_a_re, v_ssm_a_im, v_ssm_log_dt, v_ssm_b_re, v_ssm_b_im, v_ssm_c_re, v_ssm_c_im, v_ssm_d, v_ssm_glu_w, v_ssm_glu_b, v_ssm_w_out, v_sg_w_in, v_sg_v_g, v_sg_w_s, v_sg_b_s, v_sg_w_out):
    args = dict(locals())
    w = {n: args[n] for n in WEIGHTS}
    m = {n: args["m_" + n] for n in WEIGHTS}
    v = {n: args["v_" + n] for n in WEIGHTS}
    L = x.shape[1]
    tm = min(L, 256)
    tm2 = min(L, 512)
    chip = 2 * lax.axis_index("x") + lax.axis_index("y")
    me = 2 * chip + lax.axis_index("c")
    xin = x[0]
    target = loss_target[0]
    chip1 = chip.reshape(1).astype(jnp.int32)
    place = jnp.stack([chip, lax.axis_index("c")]).astype(jnp.int32)

    gathers = []

    casts = {}

    def cast_group(g, after=None):
        entries = GATHER_GROUPS[g]
        axes = [BIG_AXIS[n] for n, _ in entries]
        casts[g] = _cast_place([w[n] for n, _ in entries], [li for _, li in entries], axes, place, f"cast_group{g}", after)

    def start_gather(g, after):
        if g not in casts:
            cast_group(g)
        axes = [BIG_AXIS[n] for n, _ in GATHER_GROUPS[g]]
        lands = casts[g]
        s_sems, r_sems, lands, token = _gather_start(lands, axes, f"gather_start{g}", after)
        gathers.append((s_sems, r_sems, lands, axes))
        return token

    def weights_of(g, after):
        s_sems, r_sems, lands, axes = gathers[g]
        lands = _gather_wait(s_sems, r_sems, lands, axes, f"gather_wait{g}", after)
        lands = _gather_share(lands, axes, f"gather_share{g}")
        token = start_gather(g + 2, lands[0]) if g + 2 < len(GATHER_GROUPS) else None
        return dict(zip([n for n, _ in GATHER_GROUPS[g]], lands)), token

    small_in = _pack([c, conv_w, conv_b, sg_v_g])
    got = _allgather_small(small_in, "gather_small_inputs").reshape(N_DEV, -1)
    c_all, cw_sh, cb_sh, vg_sh = _unpack(got, [(D,), conv_w.shape, conv_b.shape, sg_v_g.shape], lead=(N_DEV,))
    conv_w_full = jnp.concatenate([cw_sh[2 * k] for k in range(4)], axis=-1)
    conv_b_full = jnp.concatenate([cb_sh[2 * k] for k in range(4)], axis=-1)
    vg_full = jnp.concatenate([vg_sh[2 * k] for k in range(4)], axis=-1)
    c16 = jnp.pad(c_all, ((0, 16 - N_DEV), (0, 0)))

    cols = ada_w.shape[2]
    ada_b_cols = lax.dynamic_slice_in_dim(ada_b, chip * cols, cols, axis=1)[:, None, :]
    mod_sh = _ada_fwd(c16, ada_w, ada_b_cols)[:, :N_DEV, :]
    mod_all = _allgather_small(_pack([mod_sh]), "gather_mod").reshape(N_DEV, -1)
    mod_all = _unpack(mod_all, [mod_sh.shape], lead=(N_DEV,))[0]
    mod_mine = lax.dynamic_index_in_dim(mod_all[0::2], me, axis=2, keepdims=False)
    mod_mine = mod_mine.transpose(1, 0, 2).reshape(DEPTH, 6, D)
    pv = jnp.concatenate([mod_mine, norm1_g[:, None, :], norm2_g[:, None, :]], axis=1)

    second_started = start_gather(1, start_gather(0, pv))
    for g in range(2, len(GATHER_GROUPS)):
        cast_group(g, second_started)

    cw_rows = jnp.concatenate([conv_w_full, conv_b_full[:, None, :], jnp.zeros((conv_w_full.shape[0], 4, D), F32)], axis=1)

    a_re, a_im = ssm_a_re[0], ssm_a_im[0]
    log_dt = ssm_log_dt[0][:, None]
    bt_re, bt_im = ssm_b_re[0].transpose(2, 0, 1), ssm_b_im[0].transpose(2, 0, 1)
    abar_re, abar_im, bbar_re, bbar_im = _s5_params_fwd(a_re, a_im, log_dt, bt_re, bt_im, after=second_started)
    ar_vec, ai_vec = abar_re.reshape(1, NSTATE), abar_im.reshape(1, NSTATE)
    bd_re, bd_im = _bf(_blockdiag_b(bbar_re)), _bf(_blockdiag_b(bbar_im))
    cd_re, cd_im = _bf(_blockdiag_c(ssm_c_re[0])), _bf(_blockdiag_c(ssm_c_im[0]))

    saved = []
    fulls = []
    xl = xin
    for i in range(DEPTH):
        kind = MIXER_OF_LAYER[i]
        j = i // 3
        first_after = [second_started, bd_re, bd_im, cd_re, cd_im, cw_rows] + [casts[g][0] for g in range(2, len(GATHER_GROUPS))]
        full, tok = weights_of(2 * i, first_after if i == 0 else [xl])
        fulls.append(full)
        if kind == 0:
            x1, h, bcx, conv, q, y = _conv_fwd(xl, pv, full["conv_w_in"], full["conv_w_out"], cw_rows, i, j, tm2, after=tok)
            mix = dict(h=h, bcx=bcx, conv=conv, q=q, y=y)
        elif kind == 1:
            xp = _to_segments(xl)
            h, u, e_re, e_im = _s5_fwd_ends(xp, pv, full["ssm_w_in"], bd_re, bd_im, ar_vec, ai_vec, i, tm, after=tok)
            s0_re, s0_im = _s5_segment_states(e_re, e_im, ar_vec, ai_vec, L // NSEG, adjoint=False)
            x1p, s_re, s_im, y1, zg, y3, y, s_re_b, s_im_b = _s5_fwd_out(
                xp, u, pv, bd_re, bd_im, s0_re, s0_im, ar_vec, ai_vec, cd_re, cd_im, ssm_d, full["ssm_glu_w"], ssm_glu_b,
                full["ssm_w_out"], i, tm)
            x1 = _from_segments(x1p)
            mix = dict(xp=xp, h=h, u=u, s_re=s_re, s_im=s_im, s_re_b=s_re_b, s_im_b=s_im_b, y1=y1, zg=zg, y3=y3, y=y)
        else:
            x1, h, uv, vm, q, y = _sg_fwd(xl, pv, full["sg_w_in"], sg_w_s[0], sg_b_s[0].T, vg_full, full["sg_w_out"], i, tm2,
                                          after=tok)
            mix = dict(h=h, uv=uv, vm=vm, q=q, y=y)
        ffn_weights, tok = weights_of(2 * i + 1, [x1])
        full.update(ffn_weights)
        x2, h2, a, f = _ffn_fwd(x1, pv, full["ff_w1"], full["ff_w2"], i, tm2, after=tok)
        saved.append(dict(x=xl, x1=x1, h2=h2, a=a, f=f, **mix))
        xl = x2

    dxl, vs_fin = _final(xl, target, final_g[None, :], tm2)

    gfull = {n: [None] * w[n].shape[0] for n in BIG}
    vs_mix, vs_ffn = [None] * DEPTH, [None] * DEPTH
    small_g = {}
    scatters = {}
    token = None

    def start_scatter(key, entries, after):
        garrs = [gfull[n][li][None] for n, li in entries]
        gaxes = [BIG_AXIS[n] for n, _ in entries]
        s_sems, r_sems, garrs, lands, tok = _scatter_start(garrs, gaxes, f"scatter_start{key}", after)
        scatters[key] = (s_sems, r_sems, garrs, lands, gaxes, entries)
        return tok

    for i in reversed(range(DEPTH)):
        kind = MIXER_OF_LAYER[i]
        j = i // 3
        sv = saved[i]
        full = fulls[i]
        dx1, p_b, da_b, df_b, vs_ffn[i] = _ffn_bwd(dxl, sv["x1"], sv["a"], sv["f"], pv, full["ff_w1"], full["ff_w2"], i, tm,
                                                   after=token)
        gfull["ff_w1"][i] = _mm_tn(sv["h2"], da_b, f"wgrad_ff_w1_{i}")
        gfull["ff_w2"][i] = _mm_tn(p_b, df_b, f"wgrad_ff_w2_{i}")
        if i == 0:
            token = start_scatter("0f", LAYER_WEIGHTS[0][2:], dx1)
        if kind == 0:
            dxl, dbcx_b, dy_b, vsm = _conv_bwd(dx1, sv["x"], sv["y"], sv["bcx"], sv["conv"], pv, full["conv_w_in"],
                                               full["conv_w_out"], cw_rows, i, j, tm2, after=token if i == 0 else None)
            gfull["conv_w_in"][j] = _mm_tn(sv["h"], dbcx_b, f"wgrad_conv_w_in_{j}")
            gfull["conv_w_out"][j] = _mm_tn(sv["q"], dy_b, f"wgrad_conv_w_out_{j}")
            small_g.setdefault("conv_w", [None, None])[j] = vsm[3:6]
            small_g.setdefault("conv_b", [None, None])[j] = vsm[6]
        elif kind == 1:
            dx1p = _to_segments(dx1)
            dy_b, y2_b, dzg_b, dy1_b, du_skip, eb_re, eb_im, vsm = _s5_bwd_ends(
                dx1p, sv["y"], sv["y1"], sv["zg"], sv["u"], pv, cd_re, cd_im, ar_vec, ai_vec, ssm_d, full["ssm_glu_w"],
                full["ssm_w_out"], i, tm)
            l0_re, l0_im = _s5_segment_states(eb_re, eb_im, ar_vec, ai_vec, L // NSEG, adjoint=True)
            dxp, du_b, lam_re, lam_im, dabar, vs_in = _s5_bwd_in(
                dx1p, dy1_b, du_skip, sv["xp"], sv["s_re"], sv["s_im"], pv, bd_re, bd_im, cd_re, cd_im, l0_re, l0_im,
                ar_vec, ai_vec, full["ssm_w_in"], i, tm)
            dxl = _from_segments(dxp)
            gfull["ssm_w_out"][0] = _mm_tn(sv["y3"], dy_b, "wgrad_ssm_w_out")
            gfull["ssm_glu_w"][0] = _mm_tn(y2_b, dzg_b, "wgrad_ssm_glu_w")
            gfull["ssm_w_in"][0] = _mm_tn(sv["h"], du_b, "wgrad_ssm_w_in")
            s5_late = dict(s_re=sv["s_re_b"], s_im=sv["s_im_b"], u=sv["u"], dy1_b=dy1_b, lam_re=lam_re, lam_im=lam_im, dabar=dabar)
            small_g.update(ssm_d=vsm[2], ssm_glu_b=vsm[1])
            vsm = jnp.concatenate([vsm[0:1], vs_in[1:3], jnp.zeros((5, D), F32)], axis=0)
        else:
            dxl, duv_b, dy_b, vsm, d_ws, d_bt = _sg_bwd(dx1, sv["x"], sv["y"], sv["uv"], sv["vm"], pv, full["sg_w_in"],
                                                        sg_w_s[0], vg_full, full["sg_w_out"], i, tm2)
            gfull["sg_w_in"][0] = _mm_tn(sv["h"], duv_b, "wgrad_sg_w_in")
            gfull["sg_w_out"][0] = _mm_tn(sv["q"], dy_b, "wgrad_sg_w_out")
            small_g.update(sg_v_g=vsm[3], sg_w_s=d_ws, sg_b_s=d_bt.T)
        vs_mix[i] = vsm
        token = start_scatter(str(i), LAYER_WEIGHTS[i], dxl) if i > 0 else start_scatter("0c", LAYER_WEIGHTS[0][:2], dxl)
    grad_x = dxl[None]

    sums = {n: [None] * w[n].shape[0] for n in BIG}

    def collect(key, after):
        s_sems, r_sems, garrs, lands, gaxes, entries = scatters[key]
        garrs, recv = _scatter_wait(s_sems, r_sems, garrs, lands, gaxes, f"scatter_wait{key}", after)
        for (n, li), t in zip(entries, _sum_parts(recv, garrs, gaxes, chip1, f"sum_group{key}")):
            sums[n][li] = t
        return sums[entries[-1][0]][entries[-1][1]]

    after = token
    for key in ("3", "2", "1"):
        after = collect(key, after)
    early = [(n, li) for i in (3, 2, 1) for n, li in LAYER_WEIGHTS[i]]
    late = list(LAYER_WEIGHTS[0][2:]) + list(LAYER_WEIGHTS[0][:2])
    s_sems, r_sems, mine_thru, lands, tok = _swap_start([sums[n][li] for n, li in early], "swap_start_early", after)

    blocks = dict(
        c_re=_mm_tn_blocks(s5_late["s_re"], s5_late["dy1_b"], S5_BP, S5_BH, "wgrad_s5_c_re", after=tok),
        c_im=_mm_tn_blocks(s5_late["s_im"], s5_late["dy1_b"], S5_BP, S5_BH, "wgrad_s5_c_im", after=tok),
        b_re=_mm_tn_blocks(s5_late["u"], s5_late["lam_re"], S5_BH, S5_BP, "wgrad_s5_b_re", after=tok),
        b_im=_mm_tn_blocks(s5_late["u"], s5_late["lam_im"], S5_BH, S5_BP, "wgrad_s5_b_im", after=tok))
    d_are, d_aim, d_ldt, d_btre, d_btim = _s5_params_bwd(
        a_re, a_im, log_dt, bt_re, bt_im, s5_late["dabar"][0].reshape(S5_G, S5_P), s5_late["dabar"][1].reshape(S5_G, S5_P),
        _unblock_b(blocks["b_re"]), _unblock_b(blocks["b_im"]))
    small_g.update(ssm_a_re=d_are, ssm_a_im=d_aim, ssm_log_dt=d_ldt, ssm_b_re=d_btre.transpose(1, 2, 0),
                   ssm_b_im=d_btim.transpose(1, 2, 0), ssm_c_re=_unblock_c(blocks["c_re"]), ssm_c_im=-_unblock_c(blocks["c_im"]))

    mine_thru, got = _swap_wait(s_sems, r_sems, mine_thru, lands, "swap_wait_early", blocks["b_im"])
    sib = dict(zip(early, got))
    for (n, li), t in zip(early, mine_thru):
        sums[n][li] = t
    after = got[-1]
    for key in ("0f", "0c"):
        after = collect(key, after)
    sib.update(zip(late, _swap_with_sibling([sums[n][li] for n, li in late], "swap_grad_sums_late")))

    dmod = _mod_bwd(jnp.stack(vs_mix), jnp.stack(vs_ffn), pv)
    small_g.update(ada_b=dmod[:, :6, :], norm1_g=dmod[:, 6, :], norm2_g=dmod[:, 7, :], final_g=vs_fin[1],
                   conv_w=jnp.stack(small_g["conv_w"]), conv_b=jnp.stack(small_g["conv_b"]))

    loss_part = (0.5 / D) * vs_fin[0, 0:1]
    part_shapes = [(1,)] + [tuple(small_g[n].shape) for n in SMALL]
    slots = _reduce_pair(_pack([loss_part] + [small_g[n] for n in SMALL], 16), "reduce_small_pair", sib[late[-1]])
    s_sems, r_sems, slots, tok = _reduce_cross_start(slots, "reduce_small_cross_start")

    res = {}
    for n in BIG:
        res[n] = _adamw_layers(w[n], sums[n], [sib[(n, li)] for li in range(w[n].shape[0])], m[n], v[n], f"adamw_{n}", after=tok)
        tok = res[n][0]

    slots = _reduce_cross_wait(s_sems, r_sems, slots, "reduce_small_cross_wait", tok)
    parts_sum = _reduce_finish(slots, "reduce_small_finish")
    summed = _unpack(parts_sum, part_shapes)
    loss = summed[0][0]
    gsum = dict(zip(SMALL, summed[1:]))
    dmod_all = _allgather_small(_pack([small_g["ada_b"]]), "gather_dmod", parts_sum)
    dmod_all = dmod_all.reshape(N_DEV, DEPTH, 6 * D)
    dmod_cols = lax.dynamic_slice_in_dim(dmod_all, chip * cols, cols, axis=2).transpose(1, 0, 2)
    g_ada_w = _ada_bwd(c16, jnp.pad(dmod_cols, ((0, 0), (0, 16 - N_DEV), (0, 0))))

    shp = ada_w.shape
    two = lambda t: t.reshape(shp[0] * shp[1], shp[2])
    res["ada_w"] = [t.reshape(shp) for t in _adamw(two(ada_w), [two(g_ada_w)], two(m_ada_w), two(v_ada_w), "adamw_ada_w")]

    def mine(n):
        g = gsum[n]
        if n in SMALL_SHARDED:
            g = lax.dynamic_slice_in_dim(g, chip * w[n].shape[-1], w[n].shape[-1], axis=g.ndim - 1)
        return g.reshape(w[n].shape)

    for k, names in enumerate(([n for n in SMALL if n not in SMALL_WIDE_PADDED], list(SMALL_WIDE_PADDED))):
        outs = _adamw_many([w[n] for n in names], [mine(n) for n in names], [m[n] for n in names], [v[n] for n in names],
                           f"adamw_small{k}")
        for idx, n in enumerate(names):
            res[n] = [outs[part][idx] for part in range(4)]

    outs = [loss, grad_x]
    for part in range(4):
        outs += [res[n][part] for n in WEIGHTS]
    return tuple(outs)
```

```python
import functools

import jax
import jax.numpy as jnp
from jax import lax
from jax.experimental import pallas as pl
from jax.experimental.pallas import tpu as pltpu

F32 = jnp.float32
BF16 = jnp.bfloat16
D = 1024
EPS = 1e-6
DEPTH = 4
MIXER_OF_LAYER = (0, 1, 2, 0)
S5_G, S5_H, S5_P = 64, 16, 64
S5_NB = 4
S5_BH = S5_H * 16
S5_BP = S5_P * 16
NSTATE = S5_G * S5_P
SG_HEADS, SG_CHUNK = 8, 128
ADAM_LR, ADAM_B1, ADAM_B2, ADAM_EPS, ADAM_WD, ADAM_STEP = 0.001, 0.9, 0.999, 1e-08, 0.01, 10
N_DEV = 8
MESH = pl.DeviceIdType.MESH
LANES = 1024
R_SH1, R_SC1, R_G1, R_SH2, R_SC2, R_G2, R_N1, R_N2 = range(8)


def _dot(a, b):
    return jnp.dot(a, b, preferred_element_type=F32)


def _dot_nt(a, b):
    return lax.dot_general(a, b, (((1,), (1,)), ((), ())), preferred_element_type=F32)


def _dot_tn(a, b):
    return lax.dot_general(a, b, (((0,), (0,)), ((), ())), preferred_element_type=F32)


def _bf(x):
    return x.astype(BF16)


def _sum0(x):
    return jnp.sum(x, axis=0, keepdims=True)


def _params(n_axes, vmem_mb=48):
    return pltpu.CompilerParams(dimension_semantics=("arbitrary",) * n_axes, vmem_limit_bytes=vmem_mb << 20)


def _rows(tm, cols, nt=None):
    if nt is None:
        return pl.BlockSpec((tm, cols), lambda i: (i, 0))
    return pl.BlockSpec((tm, cols), lambda i: (nt - 1 - i, 0))


def _whole(shape):
    nd = len(shape)
    return pl.BlockSpec(shape, lambda *_: (0,) * nd)


def _layer_w(r, c, layer):
    return pl.BlockSpec((None, r, c), lambda *_: (layer, 0, 0), pipeline_mode=pl.Buffered(1))


def _const_w(shape):
    nd = len(shape)
    return pl.BlockSpec(shape, lambda *_: (0,) * nd, pipeline_mode=pl.Buffered(1))


def _call_after(after, body, n_in, *, in_specs, **kw):
    if after is None:
        return pl.pallas_call(body, in_specs=in_specs, **kw), ()

    def body_after(*refs):
        return body(*refs[:n_in], *refs[n_in + 1:])

    return pl.pallas_call(body_after, in_specs=list(in_specs) + [pl.BlockSpec(memory_space=pl.ANY)], **kw), (after,)


def _norm_mod(x, ng, sc, sh):
    r = lax.rsqrt(jnp.mean(x * x, axis=-1, keepdims=True) + EPS)
    xn = x * r
    return (xn * ng) * (1.0 + sc) + sh, xn, r


def _norm_mod_bwd(dh, xn, r, ng, sc):
    dxn = dh * (ng * (1.0 + sc))
    return r * (dxn - xn * jnp.mean(dxn * xn, axis=-1, keepdims=True))


def _shift_down(z, prev8, k):
    row = lax.broadcasted_iota(jnp.int32, z.shape, 0)
    if k == 1:
        return jnp.where(row >= 1, pltpu.roll(z, 1, 0), prev8[7:8])
    return jnp.where(row >= 2, pltpu.roll(z, 2, 0), jnp.where(row == 0, prev8[6:7], prev8[7:8]))


def _shift_up(z, next8, k):
    n = z.shape[0]
    row = lax.broadcasted_iota(jnp.int32, z.shape, 0)
    if k == 1:
        return jnp.where(row <= n - 2, pltpu.roll(z, n - 1, 0), next8[0:1])
    return jnp.where(row <= n - 3, pltpu.roll(z, n - 2, 0), jnp.where(row == n - 2, next8[0:1], next8[1:2]))


def _place():
    x, y, c = lax.axis_index("x"), lax.axis_index("y"), lax.axis_index("c")
    chips = [(1 - x, y), (x, 1 - y), (1 - x, 1 - y)]
    return x, y, c, chips


def _allgather_small(x_shard, name, after=None):
    m_per, n = x_shard.shape

    def body(x_ref, out_ref, send_sems, recv_sems, local_sem):
        x, y, c, chips = _place()
        me, sibling = (x, y, c), (x, y, 1 - c)

        def rows(px, py, pc):
            return out_ref.at[pl.ds((4 * px + 2 * py + pc) * m_per, m_per), :]

        def copy(k, block, to, src=None):
            return pltpu.make_async_remote_copy(
                src_ref=rows(*block) if src is None else src, dst_ref=rows(*block),
                send_sem=send_sems.at[k], recv_sem=recv_sems.at[k], device_id=to, device_id_type=MESH)

        mine = pltpu.make_async_copy(x_ref, rows(*me), local_sem)
        mine.start()
        first = [copy(0, me, sibling, src=x_ref)]
        first += [copy(1 + j, me, (*chip, c), src=x_ref) for j, chip in enumerate(chips)]
        for cp in first:
            cp.start()
        passed = [copy(4 + j, (*chip, c), sibling) for j, chip in enumerate(chips)]
        for j, chip in enumerate(chips):
            copy(1 + j, (*chip, c), me).wait_recv()
            passed[j].start()
        copy(0, sibling, me).wait_recv()
        for j, chip in enumerate(chips):
            copy(4 + j, (*chip, 1 - c), me).wait_recv()
        for cp in first + passed:
            cp.wait_send()
        mine.wait()

    call, tail = _call_after(
        after, body, 1, name=name, out_shape=jax.ShapeDtypeStruct((N_DEV * m_per, n), F32),
        in_specs=[pl.BlockSpec(memory_space=pltpu.VMEM)], out_specs=pl.BlockSpec(memory_space=pltpu.VMEM),
        scratch_shapes=[pltpu.SemaphoreType.DMA((7,)), pltpu.SemaphoreType.DMA((7,)), pltpu.SemaphoreType.DMA],
        compiler_params=pltpu.CompilerParams(vmem_limit_bytes=48 << 20),
    )
    return call(x_shard, *tail)


def _reduce_pair(x_part, name, after=None):
    m, n = x_part.shape
    h = m // 2

    def body(x_ref, slots_ref, sib_buf, send_sem, recv_sem):
        x, y, c, _ = _place()
        swap = pltpu.make_async_remote_copy(src_ref=x_ref, dst_ref=sib_buf, send_sem=send_sem, recv_sem=recv_sem,
                                            device_id=(x, y, 1 - c), device_id_type=MESH)
        swap.start()
        swap.wait()
        mine = pl.ds(pl.multiple_of(c * h, 8), h)
        slots_ref[pl.ds(2 * x + y, 1)] = (x_ref[mine, :] + sib_buf[mine, :])[None]

    call, tail = _call_after(
        after, body, 1, name=name, out_shape=jax.ShapeDtypeStruct((4, h, n), F32),
        in_specs=[pl.BlockSpec(memory_space=pltpu.VMEM)], out_specs=pl.BlockSpec(memory_space=pltpu.VMEM),
        scratch_shapes=[pltpu.VMEM((m, n), F32), pltpu.SemaphoreType.DMA, pltpu.SemaphoreType.DMA],
        compiler_params=pltpu.CompilerParams(vmem_limit_bytes=48 << 20),
    )
    return call(x_part, *tail)


def _reduce_cross_start(slots, name):
    def body(slots_ref, send_sems, recv_sems, thru, token):
        x, y, c, chips = _place()
        mine = slots_ref.at[pl.ds(2 * x + y, 1)]
        for j, chip in enumerate(chips):
            pltpu.make_async_remote_copy(src_ref=mine, dst_ref=mine, send_sem=send_sems.at[j], recv_sem=recv_sems.at[j],
                                         device_id=(*chip, c), device_id_type=MESH).start()
        token[...] = jnp.zeros_like(token)

    res = pl.pallas_call(
        body, name=name,
        out_shape=(pltpu.SemaphoreType.DMA((3,)), pltpu.SemaphoreType.DMA((3,)), pltpu.HBM(slots.shape, F32),
                   jax.ShapeDtypeStruct((8, 128), F32)),
        in_specs=[HBM_SPEC], out_specs=(SEM_SPEC, SEM_SPEC, HBM_SPEC, pl.BlockSpec(memory_space=pltpu.VMEM)),
        input_output_aliases={0: 2}, compiler_params=SPLIT_COPY_PARAMS,
    )(*_in_hbm([slots]))
    return res


def _reduce_cross_wait(send_sems, recv_sems, slots, name, after):
    def body(slots_ref, s_sems, r_sems, after_ref, thru):
        x, y, c, chips = _place()
        for j, chip in enumerate(chips):
            theirs = slots_ref.at[pl.ds(2 * chip[0] + chip[1], 1)]
            cp = pltpu.make_async_remote_copy(src_ref=theirs, dst_ref=theirs, send_sem=s_sems.at[j], recv_sem=r_sems.at[j],
                                              device_id=(x, y, c), device_id_type=MESH)
            cp.wait_send()
            cp.wait_recv()

    return pl.pallas_call(
        body, name=name, out_shape=pltpu.HBM(slots.shape, F32),
        in_specs=[HBM_SPEC, SEM_SPEC, SEM_SPEC, ANY_SPEC], out_specs=HBM_SPEC,
        input_output_aliases={0: 0}, compiler_params=SPLIT_COPY_PARAMS,
    )(slots, send_sems, recv_sems, after)


def _reduce_finish(slots, name):
    _, h, n = slots.shape

    def body(slots_ref, out_ref, send_sem, recv_sem):
        x, y, c, _ = _place()
        mine = pl.ds(pl.multiple_of(c * h, 8), h)
        theirs = pl.ds(pl.multiple_of((1 - c) * h, 8), h)
        out_ref[mine, :] = ((slots_ref[0] + slots_ref[1]) + slots_ref[2]) + slots_ref[3]
        give = pltpu.make_async_remote_copy(src_ref=out_ref.at[mine, :], dst_ref=out_ref.at[mine, :], send_sem=send_sem,
                                            recv_sem=recv_sem, device_id=(x, y, 1 - c), device_id_type=MESH)
        give.start()
        give.wait_send()
        pltpu.make_async_remote_copy(src_ref=out_ref.at[theirs, :], dst_ref=out_ref.at[theirs, :], send_sem=send_sem,
                                     recv_sem=recv_sem, device_id=(x, y, c), device_id_type=MESH).wait_recv()

    return pl.pallas_call(
        body, name=name, out_shape=jax.ShapeDtypeStruct((2 * h, n), F32),
        in_specs=[pl.BlockSpec(memory_space=pltpu.VMEM)], out_specs=pl.BlockSpec(memory_space=pltpu.VMEM),
        scratch_shapes=[pltpu.SemaphoreType.DMA, pltpu.SemaphoreType.DMA],
        compiler_params=pltpu.CompilerParams(vmem_limit_bytes=48 << 20),
    )(slots)


def _shard_region(ref, full_shape, axis, chip_k, half=None):
    _, r, c = full_shape
    if axis == 1:
        rs = r // 4
        if half is None:
            return ref.at[:, pl.ds(pl.multiple_of(chip_k * rs, 128), rs), :]
        return ref.at[:, pl.ds(pl.multiple_of(chip_k * rs + half * (rs // 2), 128), rs // 2), :]
    cs = c // 4
    if half is None:
        return ref.at[:, :, pl.ds(pl.multiple_of(chip_k * cs, 128), cs)]
    return ref.at[:, pl.ds(pl.multiple_of(half * (r // 2), 128), r // 2), pl.ds(pl.multiple_of(chip_k * cs, 128), cs)]


HBM_SPEC = pl.BlockSpec(memory_space=pltpu.HBM)
SEM_SPEC = pl.BlockSpec(memory_space=pltpu.SEMAPHORE)
ANY_SPEC = pl.BlockSpec(memory_space=pl.ANY)
SPLIT_COPY_PARAMS = pltpu.CompilerParams(has_side_effects=pltpu.SideEffectType.DATAFLOW_SIDE_EFFECTING)


def _in_hbm(arrs):
    return [pltpu.with_memory_space_constraint(a, pltpu.HBM) for a in arrs]


def _cast_place(ws, layers, axes, place, name, after=None):
    n_arr = len(ws)
    in_specs, out_specs, fulls = [], [], []
    for w_stack, li, axis in zip(ws, layers, axes):
        _, r, c = w_stack.shape
        tr = r // 4
        fulls.append((1, 4 * r, c) if axis == 1 else (1, r, 4 * c))
        in_specs.append(pl.BlockSpec((None, tr, c), lambda i, p, li=li: (li, 2 * p[1] + i, 0)))
        if axis == 1:
            out_specs.append(pl.BlockSpec((None, tr, c), lambda i, p: (0, 4 * p[0] + 2 * p[1] + i, 0)))
        else:
            out_specs.append(pl.BlockSpec((None, tr, c), lambda i, p: (0, 2 * p[1] + i, p[0])))

    extra = [] if after is None else [after]

    def body(p_ref, *refs):
        for a in range(n_arr):
            refs[n_arr + len(extra) + a][...] = _bf(refs[a][...])

    return pl.pallas_call(
        body, name=name,
        grid_spec=pltpu.PrefetchScalarGridSpec(num_scalar_prefetch=1, grid=(2,), in_specs=in_specs + [ANY_SPEC] * len(extra),
                                               out_specs=out_specs),
        out_shape=[jax.ShapeDtypeStruct(f, BF16) for f in fulls],
        compiler_params=_params(1),
    )(place, *ws, *extra)


def _gather_start(lands, axes, name, after):
    n_arr = len(lands)
    fulls = [tuple(l.shape) for l in lands]

    def body(*refs):
        land = refs[:n_arr]
        send_sems, recv_sems = refs[n_arr + 1:n_arr + 3]
        token = refs[-1]
        x, y, c, chips = _place()
        k_me = 2 * x + y
        for a in range(n_arr):
            mine = _shard_region(land[a], fulls[a], axes[a], k_me, c)
            for j, chip in enumerate(chips):
                pltpu.make_async_remote_copy(
                    src_ref=mine, dst_ref=mine, send_sem=send_sems.at[a * 3 + j], recv_sem=recv_sems.at[a * 3 + j],
                    device_id=(*chip, c), device_id_type=MESH).start()
        token[...] = jnp.zeros_like(token)

    res = pl.pallas_call(
        body, name=name,
        out_shape=(pltpu.SemaphoreType.DMA((3 * n_arr,)), pltpu.SemaphoreType.DMA((3 * n_arr,)),
                   *[pltpu.HBM(f, BF16) for f in fulls], jax.ShapeDtypeStruct((8, 128), F32)),
        in_specs=[HBM_SPEC] * n_arr + [ANY_SPEC],
        out_specs=(SEM_SPEC, SEM_SPEC, *[HBM_SPEC] * n_arr, pl.BlockSpec(memory_space=pltpu.VMEM)),
        input_output_aliases={a: 2 + a for a in range(n_arr)},
        compiler_params=SPLIT_COPY_PARAMS,
    )(*_in_hbm(lands), after)
    return res[0], res[1], list(res[2:2 + n_arr]), res[-1]


def _gather_wait(send_sems, recv_sems, lands, axes, name, after):
    n_arr = len(lands)
    fulls = [tuple(l.shape) for l in lands]

    def body(*refs):
        land = refs[:n_arr]
        s_sems, r_sems = refs[n_arr:n_arr + 2]
        x, y, c, chips = _place()
        for a in range(n_arr):
            for j, chip in enumerate(chips):
                k_j = 2 * chip[0] + chip[1]
                got = _shard_region(land[a], fulls[a], axes[a], k_j, c)
                cp = pltpu.make_async_remote_copy(
                    src_ref=got, dst_ref=got, send_sem=s_sems.at[a * 3 + j], recv_sem=r_sems.at[a * 3 + j],
                    device_id=(x, y, c), device_id_type=MESH)
                cp.wait_send()
                cp.wait_recv()

    res = pl.pallas_call(
        body, name=name,
        out_shape=tuple(pltpu.HBM(f, BF16) for f in fulls),
        in_specs=[HBM_SPEC] * n_arr + [SEM_SPEC, SEM_SPEC] + [ANY_SPEC] * len(after),
        out_specs=tuple([HBM_SPEC] * n_arr),
        input_output_aliases={a: a for a in range(n_arr)},
        compiler_params=SPLIT_COPY_PARAMS,
    )(*lands, send_sems, recv_sems, *after)
    return list(res)


def _gather_share(lands, axes, name):
    n_arr = len(lands)
    fulls = [tuple(l.shape) for l in lands]

    def body(*refs):
        land_in, land = refs[:n_arr], refs[n_arr:2 * n_arr]
        send_sems, recv_sems = refs[2 * n_arr:]
        x, y, c, chips = _place()
        copies = []
        for a in range(n_arr):
            for j, k_j in enumerate([2 * chip[0] + chip[1] for chip in chips] + [2 * x + y]):
                cp = pltpu.make_async_remote_copy(
                    src_ref=_shard_region(land_in[a], fulls[a], axes[a], k_j, c),
                    dst_ref=_shard_region(land[a], fulls[a], axes[a], k_j, c),
                    send_sem=send_sems.at[a * 4 + j], recv_sem=recv_sems.at[a * 4 + j],
                    device_id=(x, y, 1 - c), device_id_type=MESH)
                cp.start()
                copies.append(cp)
        for cp in copies:
            cp.wait()

    return pl.pallas_call(
        body, name=name, out_shape=[jax.ShapeDtypeStruct(f, BF16) for f in fulls],
        in_specs=[ANY_SPEC] * n_arr, out_specs=[ANY_SPEC] * n_arr,
        input_output_aliases={a: a for a in range(n_arr)},
        scratch_shapes=[pltpu.SemaphoreType.DMA((4 * n_arr,)), pltpu.SemaphoreType.DMA((4 * n_arr,))],
    )(*lands)


def _scatter_shapes(grads, axes):
    out = []
    for g, ax in zip(grads, axes):
        shp = list(g.shape)
        shp[ax] //= 4
        out.append((3,) + tuple(shp[1:]))
    return out


def _scatter_start(grads, axes, name, after):
    n_arr = len(grads)
    shapes = _scatter_shapes(grads, axes)
    lands = [lax.empty(s, BF16) for s in shapes]

    def body(*refs):
        ins, land = refs[:n_arr], refs[n_arr:2 * n_arr]
        send_sems, recv_sems = refs[2 * n_arr + 1:2 * n_arr + 3]
        token = refs[-1]
        x, y, c, chips = _place()
        for a in range(n_arr):
            for j, chip in enumerate(chips):
                k_j = 2 * chip[0] + chip[1]
                pltpu.make_async_remote_copy(
                    src_ref=_shard_region(ins[a], grads[a].shape, axes[a], k_j), dst_ref=land[a].at[pl.ds(j, 1)],
                    send_sem=send_sems.at[a * 3 + j], recv_sem=recv_sems.at[a * 3 + j],
                    device_id=(*chip, c), device_id_type=MESH).start()
        token[...] = jnp.zeros_like(token)

    res = pl.pallas_call(
        body, name=name,
        out_shape=(pltpu.SemaphoreType.DMA((3 * n_arr,)), pltpu.SemaphoreType.DMA((3 * n_arr,)),
                   *[pltpu.HBM(g.shape, BF16) for g in grads], *[pltpu.HBM(s, BF16) for s in shapes],
                   jax.ShapeDtypeStruct((8, 128), F32)),
        in_specs=[HBM_SPEC] * (2 * n_arr) + [ANY_SPEC],
        out_specs=(SEM_SPEC, SEM_SPEC, *[HBM_SPEC] * (2 * n_arr), pl.BlockSpec(memory_space=pltpu.VMEM)),
        input_output_aliases={a: 2 + a for a in range(2 * n_arr)},
        compiler_params=SPLIT_COPY_PARAMS,
    )(*_in_hbm(grads), *_in_hbm(lands), after)
    return res[0], res[1], list(res[2:2 + n_arr]), list(res[2 + n_arr:2 + 2 * n_arr]), res[-1]


def _scatter_wait(send_sems, recv_sems, grads, lands, axes, name, after):
    n_arr = len(grads)

    def body(*refs):
        ins, land = refs[:n_arr], refs[n_arr:2 * n_arr]
        s_sems, r_sems = refs[2 * n_arr:2 * n_arr + 2]
        x, y, c, chips = _place()
        for a in range(n_arr):
            for j, chip in enumerate(chips):
                k_j = 2 * chip[0] + chip[1]
                cp = pltpu.make_async_remote_copy(
                    src_ref=_shard_region(ins[a], grads[a].shape, axes[a], k_j), dst_ref=land[a].at[pl.ds(j, 1)],
                    send_sem=s_sems.at[a * 3 + j], recv_sem=r_sems.at[a * 3 + j],
                    device_id=(x, y, c), device_id_type=MESH)
                cp.wait_send()
                cp.wait_recv()

    res = pl.pallas_call(
        body, name=name,
        out_shape=(*[pltpu.HBM(g.shape, BF16) for g in grads], *[pltpu.HBM(l.shape, BF16) for l in lands]),
        in_specs=[HBM_SPEC] * (2 * n_arr) + [SEM_SPEC, SEM_SPEC, ANY_SPEC],
        out_specs=tuple([HBM_SPEC] * (2 * n_arr)),
        input_output_aliases={a: a for a in range(2 * n_arr)},
        compiler_params=SPLIT_COPY_PARAMS,
    )(*grads, *lands, send_sems, recv_sems, after)
    return list(res[:n_arr]), list(res[n_arr:])


def _swap_start(arrs, name, after):
    n_arr = len(arrs)
    lands = [lax.empty(a.shape, a.dtype) for a in arrs]

    def body(*refs):
        ins, land = refs[:n_arr], refs[n_arr:2 * n_arr]
        send_sems, recv_sems = refs[2 * n_arr + 1:2 * n_arr + 3]
        token = refs[-1]
        x, y, c, _ = _place()
        for a in range(n_arr):
            pltpu.make_async_remote_copy(
                src_ref=ins[a], dst_ref=land[a], send_sem=send_sems.at[a], recv_sem=recv_sems.at[a],
                device_id=(x, y, 1 - c), device_id_type=MESH).start()
        token[...] = jnp.zeros_like(token)

    res = pl.pallas_call(
        body, name=name,
        out_shape=(pltpu.SemaphoreType.DMA((n_arr,)), pltpu.SemaphoreType.DMA((n_arr,)),
                   *[pltpu.HBM(a.shape, a.dtype) for a in arrs], *[pltpu.HBM(a.shape, a.dtype) for a in arrs],
                   jax.ShapeDtypeStruct((8, 128), F32)),
        in_specs=[HBM_SPEC] * (2 * n_arr) + [ANY_SPEC],
        out_specs=(SEM_SPEC, SEM_SPEC, *[HBM_SPEC] * (2 * n_arr), pl.BlockSpec(memory_space=pltpu.VMEM)),
        input_output_aliases={a: 2 + a for a in range(2 * n_arr)},
        compiler_params=SPLIT_COPY_PARAMS,
    )(*_in_hbm(arrs), *_in_hbm(lands), after)
    return res[0], res[1], list(res[2:2 + n_arr]), list(res[2 + n_arr:2 + 2 * n_arr]), res[-1]


def _swap_wait(send_sems, recv_sems, arrs, lands, name, after):
    n_arr = len(arrs)

    def body(*refs):
        ins, land = refs[:n_arr], refs[n_arr:2 * n_arr]
        s_sems, r_sems = refs[2 * n_arr:2 * n_arr + 2]
        x, y, c, _ = _place()
        for a in range(n_arr):
            cp = pltpu.make_async_remote_copy(
                src_ref=ins[a], dst_ref=land[a], send_sem=s_sems.at[a], recv_sem=r_sems.at[a],
                device_id=(x, y, c), device_id_type=MESH)
            cp.wait_send()
            cp.wait_recv()

    res = pl.pallas_call(
        body, name=name,
        out_shape=(*[pltpu.HBM(a.shape, a.dtype) for a in arrs], *[pltpu.HBM(a.shape, a.dtype) for a in arrs]),
        in_specs=[HBM_SPEC] * (2 * n_arr) + [SEM_SPEC, SEM_SPEC, ANY_SPEC],
        out_specs=tuple([HBM_SPEC] * (2 * n_arr)),
        input_output_aliases={a: a for a in range(2 * n_arr)},
        compiler_params=SPLIT_COPY_PARAMS,
    )(*arrs, *lands, send_sems, recv_sems, after)
    return list(res[:n_arr]), list(res[n_arr:])


def _swap_with_sibling(arrs, name):
    n_arr = len(arrs)

    def body(*refs):
        ins, outs = refs[:n_arr], refs[n_arr:2 * n_arr]
        send_sems, recv_sems = refs[2 * n_arr:]
        x, y, c, _ = _place()
        copies = []
        for a in range(n_arr):
            cp = pltpu.make_async_remote_copy(
                src_ref=ins[a], dst_ref=outs[a], send_sem=send_sems.at[a], recv_sem=recv_sems.at[a],
                device_id=(x, y, 1 - c), device_id_type=MESH)
            cp.start()
            copies.append(cp)
        for cp in copies:
            cp.wait()

    any_spec = pl.BlockSpec(memory_space=pl.ANY)
    return pl.pallas_call(
        body, name=name, out_shape=[jax.ShapeDtypeStruct(a.shape, a.dtype) for a in arrs],
        in_specs=[any_spec] * n_arr, out_specs=[any_spec] * n_arr,
        scratch_shapes=[pltpu.SemaphoreType.DMA((n_arr,)), pltpu.SemaphoreType.DMA((n_arr,))],
    )(*arrs)


def _mm_tn(a, b, name, out_dtype=BF16):
    L, m = a.shape
    n = b.shape[1]
    bm, bn = min(m, 1024), min(n, 1024)
    nk = 2 if (m // bm) * (n // bn) == 1 and L % 32 == 0 else 1
    bk = L // nk

    def body(a_ref, b_ref, o_ref, *acc):
        part = _dot_tn(_bf(a_ref[...]), _bf(b_ref[...]))
        if nk == 1:
            o_ref[...] = part.astype(out_dtype)
        else:
            @pl.when(pl.program_id(2) == 0)
            def _():
                acc[0][...] = part

            @pl.when(pl.program_id(2) == 1)
            def _():
                o_ref[...] = (acc[0][...] + part).astype(out_dtype)

    return pl.pallas_call(
        body, name=name, grid=(m // bm, n // bn, nk),
        in_specs=[pl.BlockSpec((bk, bm), lambda i, j, k: (k, i)), pl.BlockSpec((bk, bn), lambda i, j, k: (k, j))],
        out_specs=pl.BlockSpec((bm, bn), lambda i, j, k: (i, j)),
        out_shape=jax.ShapeDtypeStruct((m, n), out_dtype),
        scratch_shapes=[pltpu.VMEM((bm, bn), F32)] if nk > 1 else [],
        compiler_params=_params(3),
    )(a, b)


def _mm_tn_blocks(a, b, wa, wb, name, after=None):
    L = a.shape[0]
    nb = a.shape[1] // wa
    bk = min(L, 4096)
    nk = L // bk

    def body(a_ref, b_ref, o_ref):
        @pl.when(pl.program_id(1) == 0)
        def _():
            o_ref[...] = jnp.zeros_like(o_ref)

        o_ref[...] += _dot_tn(_bf(a_ref[...]), _bf(b_ref[...]))

    call, tail = _call_after(
        after, body, 2, name=name, grid=(nb, nk),
        in_specs=[pl.BlockSpec((bk, wa), lambda j, k: (k, j)), pl.BlockSpec((bk, wb), lambda j, k: (k, j))],
        out_specs=pl.BlockSpec((None, wa, wb), lambda j, k: (j, 0, 0)),
        out_shape=jax.ShapeDtypeStruct((nb, wa, wb), F32),
        compiler_params=_params(2),
    )
    return call(a, b, *tail)


def _sum_parts(parts, owns, axes, chip, name):
    n_arr = len(parts)
    steps = 4
    in_specs, out_specs, shapes = [], [], []
    for part, axis in zip(parts, axes):
        _, r, c = part.shape
        tr = r // steps
        shapes.append((r, c))
        in_specs.append(pl.BlockSpec((3, tr, c), lambda i, k: (0, i, 0)))
        out_specs.append(pl.BlockSpec((tr, c), lambda i, k: (i, 0)))
    for part, axis in zip(parts, axes):
        _, r, c = part.shape
        tr = r // steps
        if axis == 1:
            in_specs.append(pl.BlockSpec((None, tr, c), lambda i, k: (0, steps * k[0] + i, 0)))
        else:
            in_specs.append(pl.BlockSpec((None, tr, c), lambda i, k: (0, i, k[0])))

    def body(k_ref, *refs):
        for a in range(n_arr):
            p = refs[a][...].astype(F32)
            refs[2 * n_arr + a][...] = ((p[0] + p[1]) + p[2]) + refs[n_arr + a][...].astype(F32)

    return pl.pallas_call(
        body, name=name,
        grid_spec=pltpu.PrefetchScalarGridSpec(num_scalar_prefetch=1, grid=(steps,), in_specs=in_specs, out_specs=out_specs),
        out_shape=[jax.ShapeDtypeStruct(sh, F32) for sh in shapes],
        compiler_params=_params(1),
    )(chip, *parts, *owns)


def _adamw(w, g_parts, m, v, name):
    n_g = len(g_parts)
    if w.ndim == 2:
        r, c = w.shape
        tr = r
        for cand in (512, 256, 128, 64, 32, 16, 8):
            if r % cand == 0 and cand * c * 4 <= (2 << 20):
                tr = cand
                break
        spec = pl.BlockSpec((tr, c), lambda i: (i, 0))
        tiling = dict(grid=(r // tr,), in_specs=[spec] * (3 + n_g), out_specs=[spec] * 4, compiler_params=_params(1))
    else:
        tiling = dict(compiler_params=pltpu.CompilerParams(vmem_limit_bytes=48 << 20))

    def body(*refs):
        w_ref, g_refs, m_ref, v_ref = refs[0], refs[1:1 + n_g], refs[1 + n_g], refs[2 + n_g]
        g = g_refs[0][...]
        for gr in g_refs[1:]:
            g = g + gr[...]
        _adamw_update(g, w_ref, m_ref, v_ref, *refs[3 + n_g:])

    return pl.pallas_call(body, name=name, out_shape=[jax.ShapeDtypeStruct(w.shape, F32)] * 4, **tiling)(w, *g_parts, m, v)


def _adamw_update(g, w_ref, m_ref, v_ref, g_out, d_out, m_out, v_out):
    m_new = ADAM_B1 * m_ref[...] + (1.0 - ADAM_B1) * g
    v_new = ADAM_B2 * v_ref[...] + (1.0 - ADAM_B2) * (g * g)
    m_hat = m_new * (1.0 / (1.0 - ADAM_B1 ** ADAM_STEP))
    v_hat = v_new * (1.0 / (1.0 - ADAM_B2 ** ADAM_STEP))
    g_out[...] = g
    d_out[...] = -ADAM_LR * (m_hat / (jnp.sqrt(v_hat) + ADAM_EPS) + ADAM_WD * w_ref[...])
    m_out[...] = m_new
    v_out[...] = v_new


def _adamw_many(ws, gs, ms, vs, name):
    n = len(ws)

    def body(*refs):
        for k in range(n):
            _adamw_update(refs[n + k][...], refs[k], refs[2 * n + k], refs[3 * n + k],
                          refs[4 * n + k], refs[5 * n + k], refs[6 * n + k], refs[7 * n + k])

    outs = pl.pallas_call(body, name=name, out_shape=[jax.ShapeDtypeStruct(t.shape, F32) for t in ws] * 4,
                          compiler_params=pltpu.CompilerParams(vmem_limit_bytes=56 << 20))(*ws, *gs, *ms, *vs)
    return [outs[part * n:(part + 1) * n] for part in range(4)]


def _adamw_layers(w, q_mine, q_sib, m, v, name, after=None):
    n, r, c = w.shape
    tr = r
    for cand in (512, 256, 128, 64, 32, 16, 8):
        if r % cand == 0 and cand * c * 4 <= (1 << 20):
            tr = cand
            break

    def body(*refs):
        w_ref, qm, qs, m_ref, v_ref = refs[0], refs[1:1 + n], refs[1 + n:1 + 2 * n], refs[1 + 2 * n], refs[2 + 2 * n]
        layer = pl.program_id(0)
        g = qm[0][...] + qs[0][...]
        for k in range(1, n):
            g = jnp.where(layer == k, qm[k][...] + qs[k][...], g)
        _adamw_update(g, w_ref, m_ref, v_ref, *refs[3 + 2 * n:])

    stacked = pl.BlockSpec((None, tr, c), lambda l, i: (l, i, 0))
    per_layer = [pl.BlockSpec((tr, c), lambda l, i, k=k: (jnp.where(l == k, i, 0), 0)) for k in range(n)]
    call, tail = _call_after(
        after, body, 3 + 2 * n, name=name, grid=(n, r // tr),
        in_specs=[stacked] + per_layer + per_layer + [stacked, stacked], out_specs=[stacked] * 4,
        out_shape=[jax.ShapeDtypeStruct(w.shape, F32)] * 4,
        compiler_params=_params(2),
    )
    return call(w, *q_mine, *q_sib, m, v, *tail)


def _ada_fwd(c16, ada_w, ada_b_cols):
    cols = ada_w.shape[2]

    def body(c_ref, w_ref, b_ref, o_ref):
        cv = c_ref[...]
        ca = _bf(cv * jax.nn.sigmoid(cv))
        o_ref[...] = _dot(ca, _bf(w_ref[...])) + b_ref[...]

    return pl.pallas_call(
        body, name="ada_fwd", grid=(DEPTH,),
        in_specs=[_whole((16, D)), pl.BlockSpec((None, D, cols), lambda i: (i, 0, 0)),
                  pl.BlockSpec((None, 1, cols), lambda i: (i, 0, 0))],
        out_specs=pl.BlockSpec((None, 16, cols), lambda i: (i, 0, 0)),
        out_shape=jax.ShapeDtypeStruct((DEPTH, 16, cols), F32),
        compiler_params=_params(1),
    )(c16, ada_w, ada_b_cols)


def _ada_bwd(c16, dmod16):
    cols = dmod16.shape[2]

    def body(c_ref, d_ref, o_ref):
        cv = c_ref[...]
        ca = _bf(cv * jax.nn.sigmoid(cv))
        o_ref[...] = _dot_tn(ca, _bf(d_ref[...]))

    return pl.pallas_call(
        body, name="ada_bwd", grid=(DEPTH,),
        in_specs=[_whole((16, D)), pl.BlockSpec((None, 16, cols), lambda i: (i, 0, 0))],
        out_specs=pl.BlockSpec((None, D, cols), lambda i: (i, 0, 0)),
        out_shape=jax.ShapeDtypeStruct((DEPTH, D, cols), F32),
        compiler_params=_params(1),
    )(c16, dmod16)


def _mod_bwd(vs_mix, vs_ffn, pv):
    def body(m_ref, f_ref, pv_ref, o_ref):
        for i in range(DEPTH):
            vm, vf, p = m_ref[i], f_ref[i], pv_ref[i]
            o_ref[i] = jnp.concatenate([
                vm[2:3], vm[1:2] * p[R_N1:R_N1 + 1], vm[0:1],
                vf[2:3], vf[1:2] * p[R_N2:R_N2 + 1], vf[0:1],
                vm[1:2] * (1.0 + p[R_SC1:R_SC1 + 1]), vf[1:2] * (1.0 + p[R_SC2:R_SC2 + 1])], axis=0)

    return pl.pallas_call(body, name="mod_bwd", out_shape=jax.ShapeDtypeStruct((DEPTH, 8, D), F32))(vs_mix, vs_ffn, pv)


def _ffn_fwd(x1, pv, w1, w2, layer, tm, after=None):
    L = x1.shape[0]
    dff = w1.shape[2]

    def body(x1_ref, pv_ref, w1_ref, w2_ref, x2_ref, h2_ref, a_ref, f_ref):
        x1v, p = x1_ref[...], pv_ref[...]
        h2, _, _ = _norm_mod(x1v, p[R_N2:R_N2 + 1], p[R_SC2:R_SC2 + 1], p[R_SH2:R_SH2 + 1])
        hb = _bf(h2)
        h2_ref[...] = hb
        a = _dot(hb, w1_ref[...])
        a_ref[...] = a
        ra = jnp.maximum(a, 0.0)
        f = _dot(_bf(ra * ra), w2_ref[...])
        f_ref[...] = f
        x2_ref[...] = x1v + p[R_G2:R_G2 + 1] * f

    call, tail = _call_after(
        after, body, 4, name=f"ffn_fwd{layer}", grid=(L // tm,),
        in_specs=[_rows(tm, D), pl.BlockSpec((None, 8, D), lambda i: (layer, 0, 0)), _layer_w(D, dff, 0), _layer_w(dff, D, 0)],
        out_specs=[_rows(tm, D), _rows(tm, D), _rows(tm, dff), _rows(tm, D)],
        out_shape=[jax.ShapeDtypeStruct((L, D), F32), jax.ShapeDtypeStruct((L, D), BF16),
                   jax.ShapeDtypeStruct((L, dff), F32), jax.ShapeDtypeStruct((L, D), F32)],
        compiler_params=_params(1, 56),
    )
    return call(x1, pv, w1, w2, *tail)


def _ffn_bwd(dx2, x1, a, f, pv, w1, w2, layer, tm, after=None):
    L = x1.shape[0]
    dff = w1.shape[2]
    extra = [] if after is None else [pl.BlockSpec(memory_space=pl.ANY)]
    extra_args = [] if after is None else [after]

    def body(dx2_ref, x1_ref, a_ref, f_ref, pv_ref, w1_ref, w2_ref, *rest):
        dx1_ref, p_ref, da_ref, df_ref, vs_ref = rest[len(extra):]

        @pl.when(pl.program_id(0) == 0)
        def _():
            vs_ref[...] = jnp.zeros_like(vs_ref)

        dx2v, p = dx2_ref[...], pv_ref[...]
        dfb = _bf(dx2v * p[R_G2:R_G2 + 1])
        df_ref[...] = dfb
        vs_ref[0:1, :] += _sum0(dx2v * f_ref[...])
        dp = _dot_nt(dfb, w2_ref[...])
        ra = jnp.maximum(a_ref[...], 0.0)
        p_ref[...] = _bf(ra * ra)
        dab = _bf(dp * (2.0 * ra))
        da_ref[...] = dab
        dh2 = _dot_nt(dab, w1_ref[...])
        _, xn, r = _norm_mod(x1_ref[...], p[R_N2:R_N2 + 1], p[R_SC2:R_SC2 + 1], p[R_SH2:R_SH2 + 1])
        dx1_ref[...] = dx2v + _norm_mod_bwd(dh2, xn, r, p[R_N2:R_N2 + 1], p[R_SC2:R_SC2 + 1])
        vs_ref[1:2, :] += _sum0(dh2 * xn)
        vs_ref[2:3, :] += _sum0(dh2)

    return pl.pallas_call(
        body, name=f"ffn_bwd{layer}", grid=(L // tm,),
        in_specs=[_rows(tm, D), _rows(tm, D), _rows(tm, dff), _rows(tm, D),
                  pl.BlockSpec((None, 8, D), lambda i: (layer, 0, 0)), _layer_w(D, dff, 0), _layer_w(dff, D, 0)] + extra,
        out_specs=[_rows(tm, D), _rows(tm, dff), _rows(tm, dff), _rows(tm, D), _whole((8, D))],
        out_shape=[jax.ShapeDtypeStruct((L, D), F32), jax.ShapeDtypeStruct((L, dff), BF16),
                   jax.ShapeDtypeStruct((L, dff), BF16), jax.ShapeDtypeStruct((L, D), BF16),
                   jax.ShapeDtypeStruct((8, D), F32)],
        compiler_params=_params(1, 56),
    )(dx2, x1, a, f, pv, w1, w2, *extra_args)


def _conv_fwd(x, pv, w_in, w_out, cw, layer, j, tm, after=None):
    L = x.shape[0]

    def body(x_ref, pv_ref, win_ref, wout_ref, cw_ref, x1_ref, h_ref, bcx_ref, conv_ref, q_ref, y_ref, carry):
        @pl.when(pl.program_id(0) == 0)
        def _():
            carry[...] = jnp.zeros_like(carry)

        xv, p, cwv = x_ref[...], pv_ref[...], cw_ref[...]
        h, _, _ = _norm_mod(xv, p[R_N1:R_N1 + 1], p[R_SC1:R_SC1 + 1], p[R_SH1:R_SH1 + 1])
        hb = _bf(h)
        h_ref[...] = hb
        bcx = _dot(hb, win_ref[...])
        bcx_ref[...] = bcx
        z = bcx[:, D:2 * D] * bcx[:, 2 * D:]
        prev8 = carry[...]
        conv = cwv[0:1] * _shift_down(z, prev8, 2) + cwv[1:2] * _shift_down(z, prev8, 1) + cwv[2:3] * z + cwv[3:4]
        conv_ref[...] = conv
        qb = _bf(bcx[:, :D] * conv)
        q_ref[...] = qb
        y = _dot(qb, wout_ref[...])
        y_ref[...] = y
        x1_ref[...] = xv + p[R_G1:R_G1 + 1] * y
        carry[...] = z[tm - 8:tm]

    call, tail = _call_after(
        after, body, 5, name=f"conv_fwd{layer}", grid=(L // tm,),
        in_specs=[_rows(tm, D), pl.BlockSpec((None, 8, D), lambda i: (layer, 0, 0)), _layer_w(D, 3 * D, 0), _layer_w(D, D, 0),
                  pl.BlockSpec((None, 8, D), lambda i: (j, 0, 0))],
        out_specs=[_rows(tm, D), _rows(tm, D), _rows(tm, 3 * D), _rows(tm, D), _rows(tm, D), _rows(tm, D)],
        out_shape=[jax.ShapeDtypeStruct((L, D), F32), jax.ShapeDtypeStruct((L, D), BF16), jax.ShapeDtypeStruct((L, 3 * D), F32),
                   jax.ShapeDtypeStruct((L, D), F32), jax.ShapeDtypeStruct((L, D), BF16), jax.ShapeDtypeStruct((L, D), F32)],
        scratch_shapes=[pltpu.VMEM((8, D), F32)],
        compiler_params=_params(1, 56),
    )
    return call(x, pv, w_in, w_out, cw, *tail)


def _conv_bwd(dx1, x, y, bcx, conv, pv, w_in, w_out, cw, layer, j, tm, after=None):
    L = x.shape[0]
    nt = L // tm

    def body(dx1_ref, x_ref, y_ref, bcx_ref, conv_ref, halo_ref, pv_ref, win_ref, wout_ref, cw_ref,
             dx_ref, dbcx_ref, dy_ref, vs_ref, carry):
        gi = pl.program_id(0)
        tile = nt - 1 - gi

        @pl.when(gi == 0)
        def _():
            vs_ref[...] = jnp.zeros_like(vs_ref)
            carry[...] = jnp.zeros_like(carry)

        dx1v, p, cwv = dx1_ref[...], pv_ref[...], cw_ref[...]
        dyb = _bf(dx1v * p[R_G1:R_G1 + 1])
        dy_ref[...] = dyb
        vs_ref[0:1, :] += _sum0(dx1v * y_ref[...])
        dq = _dot_nt(dyb, wout_ref[...])
        bcx = bcx_ref[...]
        b, cg, xh = bcx[:, :D], bcx[:, D:2 * D], bcx[:, 2 * D:]
        db = dq * conv_ref[...]
        dc = dq * b
        z = cg * xh
        halo = halo_ref[...]
        zprev = jnp.where(tile > 0, halo[:, D:2 * D] * halo[:, 2 * D:], 0.0)
        vs_ref[3:4, :] += _sum0(dc * _shift_down(z, zprev, 2))
        vs_ref[4:5, :] += _sum0(dc * _shift_down(z, zprev, 1))
        vs_ref[5:6, :] += _sum0(dc * z)
        vs_ref[6:7, :] += _sum0(dc)
        next8 = carry[...]
        dz = cwv[2:3] * dc + cwv[1:2] * _shift_up(dc, next8, 1) + cwv[0:1] * _shift_up(dc, next8, 2)
        dbb, dcgb, dxhb = _bf(db), _bf(dz * xh), _bf(dz * cg)
        dbcx_ref[:, 0:D] = dbb
        dbcx_ref[:, D:2 * D] = dcgb
        dbcx_ref[:, 2 * D:3 * D] = dxhb
        dh = (_dot_nt(dbb, win_ref[:, 0:D]) + _dot_nt(dcgb, win_ref[:, D:2 * D])) + _dot_nt(dxhb, win_ref[:, 2 * D:3 * D])
        _, xn, r = _norm_mod(x_ref[...], p[R_N1:R_N1 + 1], p[R_SC1:R_SC1 + 1], p[R_SH1:R_SH1 + 1])
        dx_ref[...] = dx1v + _norm_mod_bwd(dh, xn, r, p[R_N1:R_N1 + 1], p[R_SC1:R_SC1 + 1])
        vs_ref[1:2, :] += _sum0(dh * xn)
        vs_ref[2:3, :] += _sum0(dh)
        carry[...] = dc[0:8]

    halo_spec = pl.BlockSpec((8, 3 * D), lambda i: (jnp.maximum((nt - 1 - i) * (tm // 8) - 1, 0), 0))
    call, tail = _call_after(
        after, body, 10, name=f"conv_bwd{layer}", grid=(nt,),
        in_specs=[_rows(tm, D, nt), _rows(tm, D, nt), _rows(tm, D, nt), _rows(tm, 3 * D, nt), _rows(tm, D, nt), halo_spec,
                  pl.BlockSpec((None, 8, D), lambda i: (layer, 0, 0)), _layer_w(D, 3 * D, 0), _layer_w(D, D, 0),
                  pl.BlockSpec((None, 8, D), lambda i: (j, 0, 0))],
        out_specs=[_rows(tm, D, nt), _rows(tm, 3 * D, nt), _rows(tm, D, nt), _whole((8, D))],
        out_shape=[jax.ShapeDtypeStruct((L, D), F32), jax.ShapeDtypeStruct((L, 3 * D), BF16),
                   jax.ShapeDtypeStruct((L, D), BF16), jax.ShapeDtypeStruct((8, D), F32)],
        scratch_shapes=[pltpu.VMEM((8, D), F32)],
        compiler_params=_params(1, 56),
    )
    return call(dx1, x, y, bcx, conv, bcx, pv, w_in, w_out, cw, *tail)


def _s5_discretize(a_re, a_im, log_dt, bt_re, bt_im):
    dt = jnp.exp(log_dt)
    mag = jnp.exp(a_re * dt)
    abar_re = mag * jnp.cos(a_im * dt)
    abar_im = mag * jnp.sin(a_im * dt)
    den = a_re * a_re + a_im * a_im
    nr = abar_re - 1.0
    ni = abar_im
    f_re = (nr * a_re + ni * a_im) / den
    f_im = (ni * a_re - nr * a_im) / den
    bbar_re = f_re * bt_re - f_im * bt_im
    bbar_im = f_re * bt_im + f_im * bt_re
    return abar_re, abar_im, bbar_re, bbar_im


def _s5_params_fwd(a_re, a_im, log_dt, bt_re, bt_im, after=None):
    def body(ar, ai, ld, br, bi, o_ar, o_ai, o_br, o_bi):
        r = _s5_discretize(ar[...], ai[...], ld[...], br[...], bi[...])
        o_ar[...], o_ai[...], o_br[...], o_bi[...] = r

    gp = jax.ShapeDtypeStruct((S5_G, S5_P), F32)
    hgp = jax.ShapeDtypeStruct((S5_H, S5_G, S5_P), F32)
    call, tail = _call_after(after, body, 5, name="s5_params_fwd", out_shape=[gp, gp, hgp, hgp],
                             in_specs=[pl.BlockSpec(memory_space=pltpu.VMEM)] * 5)
    return call(a_re, a_im, log_dt, bt_re, bt_im, *tail)


def _s5_params_bwd(a_re, a_im, log_dt, bt_re, bt_im, d_ar, d_ai, d_br, d_bi):
    def body(ar, ai, ld, br, bi, gar, gai, gbr, gbi, o_ar, o_ai, o_ld, o_br, o_bi):
        _, vjp = jax.vjp(_s5_discretize, ar[...], ai[...], ld[...], br[...], bi[...])
        r = vjp((gar[...], gai[...], gbr[...], gbi[...]))
        o_ar[...], o_ai[...], o_ld[...], o_br[...], o_bi[...] = r

    gp = jax.ShapeDtypeStruct((S5_G, S5_P), F32)
    hgp = jax.ShapeDtypeStruct((S5_H, S5_G, S5_P), F32)
    return pl.pallas_call(body, name="s5_params_bwd", out_shape=[gp, gp, jax.ShapeDtypeStruct((S5_G, 1), F32), hgp, hgp])(
        a_re, a_im, log_dt, bt_re, bt_im, d_ar, d_ai, d_br, d_bi)


NSEG = 8
SCAN_LANES = 1024


def _to_segments(x):
    n, c = x.shape
    return x.reshape(NSEG, n // NSEG, c).transpose(1, 0, 2).reshape(n, c)


def _from_segments(x):
    n, c = x.shape
    return x.reshape(n // NSEG, NSEG, c).transpose(1, 0, 2).reshape(n, c)


def _segment_scan(re_ref, im_ref, st_re, st_im, a_re, a_im, n_slabs, adjoint, write):
    for q in range(NSTATE // SCAN_LANES):
        ls = slice(q * SCAN_LANES, (q + 1) * SCAN_LANES)
        ar = jnp.broadcast_to(a_re[:, ls], (8, SCAN_LANES))
        ai = jnp.broadcast_to(a_im[:, ls], (8, SCAN_LANES))

        def step(k, carry, ls=ls, ar=ar, ai=ai):
            s_r, s_i = carry
            slab = (n_slabs - 1 - k) if adjoint else k
            rows = pl.ds(pl.multiple_of(slab * 8, 8), 8)
            b_r, b_i = re_ref[rows, ls], im_ref[rows, ls]
            if adjoint:
                n_r = b_r + ar * s_r + ai * s_i
                n_i = b_i - ai * s_r + ar * s_i
            else:
                n_r = ar * s_r - ai * s_i + b_r
                n_i = ar * s_i + ai * s_r + b_i
            if write:
                re_ref[rows, ls] = n_r
                im_ref[rows, ls] = n_i
            return n_r, n_i

        s_r, s_i = lax.fori_loop(0, n_slabs, step, (st_re[:, ls], st_im[:, ls]), unroll=4)
        st_re[:, ls] = s_r
        st_im[:, ls] = s_i


def _s5_segment_states(e_re, e_im, ar, ai, seg_len, adjoint):
    def body(ere_ref, eim_ref, ar_ref, ai_ref, ore_ref, oim_ref):
        p_r, p_i = ar_ref[...], ai_ref[...]
        if adjoint:
            p_i = -p_i
        acc_r, acc_i = jnp.ones_like(p_r), jnp.zeros_like(p_r)
        n = seg_len
        while n:
            if n & 1:
                acc_r, acc_i = acc_r * p_r - acc_i * p_i, acc_r * p_i + acc_i * p_r
            n >>= 1
            if n:
                p_r, p_i = p_r * p_r - p_i * p_i, 2.0 * p_r * p_i
        e_r, e_i = ere_ref[...], eim_ref[...]
        s_r, s_i = jnp.zeros_like(acc_r), jnp.zeros_like(acc_r)
        rows_r, rows_i = [None] * NSEG, [None] * NSEG
        order = range(NSEG - 1, -1, -1) if adjoint else range(NSEG)
        for j in order:
            rows_r[j], rows_i[j] = s_r, s_i
            s_r, s_i = (acc_r * s_r - acc_i * s_i + e_r[j:j + 1], acc_r * s_i + acc_i * s_r + e_i[j:j + 1])
        ore_ref[...] = jnp.concatenate(rows_r, axis=0)
        oim_ref[...] = jnp.concatenate(rows_i, axis=0)

    st = jax.ShapeDtypeStruct((NSEG, NSTATE), F32)
    return pl.pallas_call(body, name="s5_segment_states_bwd" if adjoint else "s5_segment_states_fwd", out_shape=[st, st])(
        e_re, e_im, ar, ai)


def _s5_fwd_ends(x, pv, w_in, b_re, b_im, ar, ai, layer, tm, after=None):
    L = x.shape[0]

    def body(x_ref, pv_ref, win_ref, bre_ref, bim_ref, ar_ref, ai_ref, h_ref, u_ref, ere_ref, eim_ref, ub_ref, bu_re, bu_im):
        @pl.when(pl.program_id(0) == 0)
        def _():
            ere_ref[...] = jnp.zeros_like(ere_ref)
            eim_ref[...] = jnp.zeros_like(eim_ref)

        p = pv_ref[...]
        h, _, _ = _norm_mod(x_ref[...], p[R_N1:R_N1 + 1], p[R_SC1:R_SC1 + 1], p[R_SH1:R_SH1 + 1])
        hb = _bf(h)
        h_ref[...] = hb
        u = _dot(hb, win_ref[...])
        u_ref[...] = u
        ub = _bf(u)
        ub_ref[...] = ub
        for k in range(S5_NB):
            uk = ub[:, k * S5_BH:(k + 1) * S5_BH]
            bu_re[:, k * S5_BP:(k + 1) * S5_BP] = _dot(uk, bre_ref[k])
            bu_im[:, k * S5_BP:(k + 1) * S5_BP] = _dot(uk, bim_ref[k])
        _segment_scan(bu_re, bu_im, ere_ref, eim_ref, ar_ref[...], ai_ref[...], tm // 8, adjoint=False, write=False)

    call, tail = _call_after(
        after, body, 7, name="s5_fwd_ends", grid=(L // tm,),
        in_specs=[_rows(tm, D), pl.BlockSpec((None, 8, D), lambda i: (layer, 0, 0)), _layer_w(D, D, 0),
                  _const_w((S5_NB, S5_BH, S5_BP)), _const_w((S5_NB, S5_BH, S5_BP)), _whole((1, NSTATE)), _whole((1, NSTATE))],
        out_specs=[_rows(tm, D), _rows(tm, D), _whole((NSEG, NSTATE)), _whole((NSEG, NSTATE)), _rows(tm, D)],
        out_shape=[jax.ShapeDtypeStruct((L, D), BF16), jax.ShapeDtypeStruct((L, D), F32),
                   jax.ShapeDtypeStruct((NSEG, NSTATE), F32), jax.ShapeDtypeStruct((NSEG, NSTATE), F32),
                   jax.ShapeDtypeStruct((L, D), BF16)],
        scratch_shapes=[pltpu.VMEM((tm, NSTATE), F32), pltpu.VMEM((tm, NSTATE), F32)],
        compiler_params=_params(1, 56),
    )
    return call(x, pv, w_in, b_re, b_im, ar, ai, *tail)


def _s5_fwd_out(x, u, pv, b_re, b_im, s0_re, s0_im, ar, ai, c_re, c_im, dvec, glu_w, glu_b, w_out, layer, tm):
    L = x.shape[0]

    def body(x_ref, u_ref, pv_ref, bre_ref, bim_ref, s0re_ref, s0im_ref, ar_ref, ai_ref, cre_ref, cim_ref, d_ref, gw_ref,
             gb_ref, wout_ref, x1_ref, sre_ref, sim_ref, y1_ref, zg_ref, y3_ref, y_ref, srb_ref, sib_ref, st_re, st_im):
        @pl.when(pl.program_id(0) == 0)
        def _():
            st_re[...] = s0re_ref[...]
            st_im[...] = s0im_ref[...]

        p = pv_ref[...]
        uv = u_ref[...]
        ub = _bf(uv)
        for k in range(S5_NB):
            uk = ub[:, k * S5_BH:(k + 1) * S5_BH]
            sre_ref[:, k * S5_BP:(k + 1) * S5_BP] = _dot(uk, bre_ref[k])
            sim_ref[:, k * S5_BP:(k + 1) * S5_BP] = _dot(uk, bim_ref[k])
        _segment_scan(sre_ref, sim_ref, st_re, st_im, ar_ref[...], ai_ref[...], tm // 8, adjoint=False, write=True)
        parts = []
        for k in range(S5_NB):
            sl = slice(k * S5_BP, (k + 1) * S5_BP)
            srb, sib = _bf(sre_ref[:, sl]), _bf(sim_ref[:, sl])
            srb_ref[:, sl] = srb
            sib_ref[:, sl] = sib
            parts.append(_dot(srb, cre_ref[k]) - _dot(sib, cim_ref[k]))
        y1 = jnp.concatenate(parts, axis=1) + d_ref[...] * uv
        y1_ref[...] = y1
        y2 = jax.nn.gelu(y1)
        zg = _dot(_bf(y2), gw_ref[...]) + gb_ref[...]
        zg_ref[...] = zg
        y3b = _bf(y2 * jax.nn.sigmoid(zg))
        y3_ref[...] = y3b
        y = _dot(y3b, wout_ref[...])
        y_ref[...] = y
        x1_ref[...] = x_ref[...] + p[R_G1:R_G1 + 1] * y

    return pl.pallas_call(
        body, name="s5_fwd_out", grid=(L // tm,),
        in_specs=[_rows(tm, D), _rows(tm, D), pl.BlockSpec((None, 8, D), lambda i: (layer, 0, 0)),
                  _const_w((S5_NB, S5_BH, S5_BP)), _const_w((S5_NB, S5_BH, S5_BP)),
                  _whole((NSEG, NSTATE)), _whole((NSEG, NSTATE)), _whole((1, NSTATE)), _whole((1, NSTATE)),
                  _const_w((S5_NB, S5_BP, S5_BH)), _const_w((S5_NB, S5_BP, S5_BH)), _whole((1, D)),
                  _layer_w(D, D, 0), _whole((1, D)), _layer_w(D, D, 0)],
        out_specs=[_rows(tm, D), _rows(tm, NSTATE), _rows(tm, NSTATE), _rows(tm, D), _rows(tm, D), _rows(tm, D), _rows(tm, D),
                   _rows(tm, NSTATE), _rows(tm, NSTATE)],
        out_shape=[jax.ShapeDtypeStruct((L, D), F32), jax.ShapeDtypeStruct((L, NSTATE), F32), jax.ShapeDtypeStruct((L, NSTATE), F32),
                   jax.ShapeDtypeStruct((L, D), F32), jax.ShapeDtypeStruct((L, D), F32),
                   jax.ShapeDtypeStruct((L, D), BF16), jax.ShapeDtypeStruct((L, D), F32),
                   jax.ShapeDtypeStruct((L, NSTATE), BF16), jax.ShapeDtypeStruct((L, NSTATE), BF16)],
        scratch_shapes=[pltpu.VMEM((NSEG, NSTATE), F32), pltpu.VMEM((NSEG, NSTATE), F32)],
        compiler_params=_params(1, 60),
    )(x, u, pv, b_re, b_im, s0_re, s0_im, ar, ai, c_re, c_im, dvec, glu_w, glu_b, w_out)


def _s5_bwd_ends(dx1, y, y1, zg, u, pv, c_re, c_im, ar, ai, dvec, glu_w, w_out, layer, tm, after=None):
    L = dx1.shape[0]
    nt = L // tm

    def body(dx1_ref, y_ref, y1_ref, zg_ref, u_ref, pv_ref, cre_ref, cim_ref, ar_ref, ai_ref, d_ref, gw_ref, wout_ref,
             dy_ref, y2_ref, dzg_ref, dy1_ref, dus_ref, ere_ref, eim_ref, vs_ref, g_re, g_im):
        @pl.when(pl.program_id(0) == 0)
        def _():
            vs_ref[...] = jnp.zeros_like(vs_ref)
            ere_ref[...] = jnp.zeros_like(ere_ref)
            eim_ref[...] = jnp.zeros_like(eim_ref)

        dx1v, p = dx1_ref[...], pv_ref[...]
        dyb = _bf(dx1v * p[R_G1:R_G1 + 1])
        dy_ref[...] = dyb
        vs_ref[0:1, :] += _sum0(dx1v * y_ref[...])
        dy3 = _dot_nt(dyb, wout_ref[...])
        y2, gelu_vjp = jax.vjp(jax.nn.gelu, y1_ref[...])
        y2_ref[...] = _bf(y2)
        gate = jax.nn.sigmoid(zg_ref[...])
        dzg = dy3 * y2 * gate * (1.0 - gate)
        dzgb = _bf(dzg)
        dzg_ref[...] = dzgb
        vs_ref[1:2, :] += _sum0(dzg)
        dy2 = dy3 * gate + _dot_nt(dzgb, gw_ref[...])
        dy1 = gelu_vjp(dy2)[0]
        vs_ref[2:3, :] += _sum0(dy1 * u_ref[...])
        dus_ref[...] = dy1 * d_ref[...]
        dy1b = _bf(dy1)
        dy1_ref[...] = dy1b
        for k in range(S5_NB):
            dk = dy1b[:, k * S5_BH:(k + 1) * S5_BH]
            g_re[:, k * S5_BP:(k + 1) * S5_BP] = _dot_nt(dk, cre_ref[k])
            g_im[:, k * S5_BP:(k + 1) * S5_BP] = -_dot_nt(dk, cim_ref[k])
        _segment_scan(g_re, g_im, ere_ref, eim_ref, ar_ref[...], ai_ref[...], tm // 8, adjoint=True, write=False)

    call, tail = _call_after(
        after, body, 13, name="s5_bwd_ends", grid=(nt,),
        in_specs=[_rows(tm, D, nt)] * 5 + [pl.BlockSpec((None, 8, D), lambda i: (layer, 0, 0)),
                  _const_w((S5_NB, S5_BP, S5_BH)), _const_w((S5_NB, S5_BP, S5_BH)), _whole((1, NSTATE)), _whole((1, NSTATE)),
                  _whole((1, D)), _layer_w(D, D, 0), _layer_w(D, D, 0)],
        out_specs=[_rows(tm, D, nt)] * 5 + [_whole((NSEG, NSTATE)), _whole((NSEG, NSTATE)), _whole((8, D))],
        out_shape=[jax.ShapeDtypeStruct((L, D), BF16)] * 4 + [jax.ShapeDtypeStruct((L, D), F32),
                   jax.ShapeDtypeStruct((NSEG, NSTATE), F32), jax.ShapeDtypeStruct((NSEG, NSTATE), F32),
                   jax.ShapeDtypeStruct((8, D), F32)],
        scratch_shapes=[pltpu.VMEM((tm, NSTATE), F32), pltpu.VMEM((tm, NSTATE), F32)],
        compiler_params=_params(1, 56),
    )
    return call(dx1, y, y1, zg, u, pv, c_re, c_im, ar, ai, dvec, glu_w, w_out, *tail)


def _s5_bwd_in(dx1, dy1_b, du_skip, x, s_re, s_im, pv, b_re, b_im, c_re, c_im, l0_re, l0_im, ar, ai, w_in, layer, tm):
    L = x.shape[0]
    nt = L // tm

    def body(dx1_ref, dy1_ref, dus_ref, x_ref, sre_ref, sim_ref, hre_ref, him_ref, lre_ref, lim_ref, pv_ref, bre_ref, bim_ref,
             cre_ref, cim_ref, l0re_ref, l0im_ref, ar_ref, ai_ref, win_ref,
             dx_ref, du_ref, lamre_ref, lamim_ref, da_ref, vs_ref, g_re, g_im, st_re, st_im):
        gi = pl.program_id(0)
        tile = nt - 1 - gi

        @pl.when(gi == 0)
        def _():
            vs_ref[...] = jnp.zeros_like(vs_ref)
            da_ref[...] = jnp.zeros_like(da_ref)
            st_re[...] = l0re_ref[...]
            st_im[...] = l0im_ref[...]

        p = pv_ref[...]
        dy1b = dy1_ref[...]
        for k in range(S5_NB):
            dk = dy1b[:, k * S5_BH:(k + 1) * S5_BH]
            g_re[:, k * S5_BP:(k + 1) * S5_BP] = _dot_nt(dk, cre_ref[k])
            g_im[:, k * S5_BP:(k + 1) * S5_BP] = -_dot_nt(dk, cim_ref[k])
        _segment_scan(g_re, g_im, st_re, st_im, ar_ref[...], ai_ref[...], tm // 8, adjoint=True, write=True)
        lam_r, lam_i = g_re[...], g_im[...]
        lrb, lib = _bf(lam_r), _bf(lam_i)
        lamre_ref[...] = lrb
        lamim_ref[...] = lib

        def wrapped(last_ref):
            z = last_ref[...]
            row = lax.broadcasted_iota(jnp.int32, z.shape, 0)
            return jnp.where(row >= 1, pltpu.roll(z, 1, 0), 0.0)

        first_r = jnp.where(tile > 0, hre_ref[...], wrapped(lre_ref))
        first_i = jnp.where(tile > 0, him_ref[...], wrapped(lim_ref))
        sp_r = jnp.concatenate([first_r, sre_ref[0:tm - 8, :]], axis=0)
        sp_i = jnp.concatenate([first_i, sim_ref[0:tm - 8, :]], axis=0)
        da_ref[0:1, :] += _sum0(lam_r * sp_r + lam_i * sp_i)
        da_ref[1:2, :] += _sum0(lam_i * sp_r - lam_r * sp_i)

        parts = []
        for k in range(S5_NB):
            sl = slice(k * S5_BP, (k + 1) * S5_BP)
            parts.append(_dot_nt(lrb[:, sl], bre_ref[k]) + _dot_nt(lib[:, sl], bim_ref[k]))
        dub = _bf(jnp.concatenate(parts, axis=1) + dus_ref[...])
        du_ref[...] = dub
        dh = _dot_nt(dub, win_ref[...])
        _, xn, r = _norm_mod(x_ref[...], p[R_N1:R_N1 + 1], p[R_SC1:R_SC1 + 1], p[R_SH1:R_SH1 + 1])
        dx_ref[...] = dx1_ref[...] + _norm_mod_bwd(dh, xn, r, p[R_N1:R_N1 + 1], p[R_SC1:R_SC1 + 1])
        vs_ref[1:2, :] += _sum0(dh * xn)
        vs_ref[2:3, :] += _sum0(dh)

    halo = pl.BlockSpec((8, NSTATE), lambda i: (jnp.maximum((nt - 1 - i) * (tm // 8) - 1, 0), 0))
    last = pl.BlockSpec((8, NSTATE), lambda i: (L // 8 - 1, 0))
    return pl.pallas_call(
        body, name="s5_bwd_in", grid=(nt,),
        in_specs=[_rows(tm, D, nt), _rows(tm, D, nt), _rows(tm, D, nt), _rows(tm, D, nt), _rows(tm, NSTATE, nt), _rows(tm, NSTATE, nt),
                  halo, halo, last, last, pl.BlockSpec((None, 8, D), lambda i: (layer, 0, 0)),
                  _const_w((S5_NB, S5_BH, S5_BP)), _const_w((S5_NB, S5_BH, S5_BP)),
                  _const_w((S5_NB, S5_BP, S5_BH)), _const_w((S5_NB, S5_BP, S5_BH)),
                  _whole((NSEG, NSTATE)), _whole((NSEG, NSTATE)), _whole((1, NSTATE)), _whole((1, NSTATE)), _layer_w(D, D, 0)],
        out_specs=[_rows(tm, D, nt), _rows(tm, D, nt), _rows(tm, NSTATE, nt), _rows(tm, NSTATE, nt), _whole((8, NSTATE)), _whole((8, D))],
        out_shape=[jax.ShapeDtypeStruct((L, D), F32), jax.ShapeDtypeStruct((L, D), BF16),
                   jax.ShapeDtypeStruct((L, NSTATE), BF16), jax.ShapeDtypeStruct((L, NSTATE), BF16),
                   jax.ShapeDtypeStruct((8, NSTATE), F32), jax.ShapeDtypeStruct((8, D), F32)],
        scratch_shapes=[pltpu.VMEM((tm, NSTATE), F32), pltpu.VMEM((tm, NSTATE), F32),
                        pltpu.VMEM((NSEG, NSTATE), F32), pltpu.VMEM((NSEG, NSTATE), F32)],
        compiler_params=_params(1, 60),
    )(dx1, dy1_b, du_skip, x, s_re, s_im, s_re, s_im, s_re, s_im, pv, b_re, b_im, c_re, c_im, l0_re, l0_im, ar, ai, w_in)


def _blockdiag_b(bt):
    b = bt.reshape(S5_H, S5_NB, 16, S5_P).transpose(1, 2, 0, 3)
    eye = jnp.eye(16, dtype=bt.dtype)
    return (b[:, :, :, None, :] * eye[None, :, None, :, None]).reshape(S5_NB, S5_BH, S5_BP)


def _unblock_b(d):
    d = jnp.einsum("bghgp->bghp", d.reshape(S5_NB, 16, S5_H, 16, S5_P))
    return d.transpose(2, 0, 1, 3).reshape(S5_H, S5_G, S5_P)


def _blockdiag_c(cm):
    c4 = cm.reshape(S5_NB, 16, S5_H, S5_P)
    eye = jnp.eye(16, dtype=cm.dtype)
    out = c4.transpose(0, 1, 3, 2)[:, :, :, None, :] * eye[None, :, None, :, None]
    return out.reshape(S5_NB, S5_BP, S5_BH)


def _unblock_c(d):
    d = jnp.einsum("bgpgh->bghp", d.reshape(S5_NB, 16, S5_P, 16, S5_H))
    return d.reshape(S5_G, S5_H, S5_P)


def _tril_mask():
    return lax.broadcasted_iota(jnp.int32, (SG_CHUNK, SG_CHUNK), 0) >= lax.broadcasted_iota(jnp.int32, (SG_CHUNK, SG_CHUNK), 1)


def _sg_fwd(x, pv, w_in, w_s, b_t, vg, w_out, layer, tm, after=None):
    L = x.shape[0]
    nc = tm // SG_CHUNK

    def body(x_ref, pv_ref, win_ref, ws_ref, bt_ref, vg_ref, wout_ref, x1_ref, h_ref, uv_ref, vm_ref, q_ref, y_ref):
        xv, p = x_ref[...], pv_ref[...]
        h, _, _ = _norm_mod(xv, p[R_N1:R_N1 + 1], p[R_SC1:R_SC1 + 1], p[R_SH1:R_SH1 + 1])
        hb = _bf(h)
        h_ref[...] = hb
        uv = _dot(hb, win_ref[...])
        uv_ref[...] = uv
        v = uv[:, D:]
        rv = lax.rsqrt(jnp.mean(v * v, axis=-1, keepdims=True) + EPS)
        vnb = _bf((v * rv) * vg_ref[...])
        mask = _tril_mask()
        bt = bt_ref[...]
        for hd in range(SG_HEADS):
            wm = _bf(jnp.where(mask, ws_ref[hd], 0.0))
            cs = slice(hd * SG_CHUNK, (hd + 1) * SG_CHUNK)
            for ck in range(nc):
                rs = slice(ck * SG_CHUNK, (ck + 1) * SG_CHUNK)
                vm_ref[rs, cs] = _dot(wm, vnb[rs, cs]) + bt[:, hd:hd + 1]
        qb = _bf(uv[:, :D] * vm_ref[...])
        q_ref[...] = qb
        y = _dot(qb, wout_ref[...])
        y_ref[...] = y
        x1_ref[...] = xv + p[R_G1:R_G1 + 1] * y

    call, tail = _call_after(
        after, body, 7, name="sg_fwd", grid=(L // tm,),
        in_specs=[_rows(tm, D), pl.BlockSpec((None, 8, D), lambda i: (layer, 0, 0)), _layer_w(D, 2 * D, 0),
                  _whole((SG_HEADS, SG_CHUNK, SG_CHUNK)), _whole((SG_CHUNK, SG_HEADS)), _whole((1, D)), _layer_w(D, D, 0)],
        out_specs=[_rows(tm, D), _rows(tm, D), _rows(tm, 2 * D), _rows(tm, D), _rows(tm, D), _rows(tm, D)],
        out_shape=[jax.ShapeDtypeStruct((L, D), F32), jax.ShapeDtypeStruct((L, D), BF16), jax.ShapeDtypeStruct((L, 2 * D), F32),
                   jax.ShapeDtypeStruct((L, D), F32), jax.ShapeDtypeStruct((L, D), BF16), jax.ShapeDtypeStruct((L, D), F32)],
        compiler_params=_params(1, 56),
    )
    return call(x, pv, w_in, w_s, b_t, vg, w_out, *tail)


def _sg_bwd(dx1, x, y, uv, vm, pv, w_in, w_s, vg, w_out, layer, tm, after=None):
    L = x.shape[0]
    nc = tm // SG_CHUNK

    def body(dx1_ref, x_ref, y_ref, uv_ref, vm_ref, pv_ref, win_ref, ws_ref, vg_ref, wout_ref,
             dx_ref, duv_ref, dy_ref, vs_ref, dws_ref, dbt_ref, dvn_scr):
        @pl.when(pl.program_id(0) == 0)
        def _():
            vs_ref[...] = jnp.zeros_like(vs_ref)
            dws_ref[...] = jnp.zeros_like(dws_ref)
            dbt_ref[...] = jnp.zeros_like(dbt_ref)

        dx1v, p = dx1_ref[...], pv_ref[...]
        dyb = _bf(dx1v * p[R_G1:R_G1 + 1])
        dy_ref[...] = dyb
        vs_ref[0:1, :] += _sum0(dx1v * y_ref[...])
        dq = _dot_nt(dyb, wout_ref[...])
        uv = uv_ref[...]
        u, v = uv[:, :D], uv[:, D:]
        dub = _bf(dq * vm_ref[...])
        dvm = dq * u
        dvmb = _bf(dvm)
        rv = lax.rsqrt(jnp.mean(v * v, axis=-1, keepdims=True) + EPS)
        vh = v * rv
        vgv = vg_ref[...]
        vnb = _bf(vh * vgv)
        mask = _tril_mask()
        for hd in range(SG_HEADS):
            wm = _bf(jnp.where(mask, ws_ref[hd], 0.0))
            cs = slice(hd * SG_CHUNK, (hd + 1) * SG_CHUNK)
            dws = jnp.zeros((SG_CHUNK, SG_CHUNK), F32)
            dbs = jnp.zeros((SG_CHUNK, 1), F32)
            for ck in range(nc):
                rs = slice(ck * SG_CHUNK, (ck + 1) * SG_CHUNK)
                dvn_scr[rs, cs] = _dot_tn(wm, dvmb[rs, cs])
                dws = dws + _dot_nt(dvmb[rs, cs], vnb[rs, cs])
                dbs = dbs + jnp.sum(dvm[rs, cs], axis=1, keepdims=True)
            dws_ref[hd] += jnp.where(mask, dws, 0.0)
            dbt_ref[:, hd:hd + 1] += dbs
        dvn = dvn_scr[...]
        vs_ref[3:4, :] += _sum0(dvn * vh)
        dvnn = dvn * vgv
        dvb = _bf(rv * (dvnn - vh * jnp.mean(dvnn * vh, axis=-1, keepdims=True)))
        duv_ref[:, 0:D] = dub
        duv_ref[:, D:2 * D] = dvb
        dh = _dot_nt(dub, win_ref[:, 0:D]) + _dot_nt(dvb, win_ref[:, D:2 * D])
        _, xn, r = _norm_mod(x_ref[...], p[R_N1:R_N1 + 1], p[R_SC1:R_SC1 + 1], p[R_SH1:R_SH1 + 1])
        dx_ref[...] = dx1v + _norm_mod_bwd(dh, xn, r, p[R_N1:R_N1 + 1], p[R_SC1:R_SC1 + 1])
        vs_ref[1:2, :] += _sum0(dh * xn)
        vs_ref[2:3, :] += _sum0(dh)

    call, tail = _call_after(
        after, body, 10, name="sg_bwd", grid=(L // tm,),
        in_specs=[_rows(tm, D), _rows(tm, D), _rows(tm, D), _rows(tm, 2 * D), _rows(tm, D),
                  pl.BlockSpec((None, 8, D), lambda i: (layer, 0, 0)), _layer_w(D, 2 * D, 0),
                  _whole((SG_HEADS, SG_CHUNK, SG_CHUNK)), _whole((1, D)), _layer_w(D, D, 0)],
        out_specs=[_rows(tm, D), _rows(tm, 2 * D), _rows(tm, D), _whole((8, D)),
                   _whole((SG_HEADS, SG_CHUNK, SG_CHUNK)), _whole((SG_CHUNK, SG_HEADS))],
        out_shape=[jax.ShapeDtypeStruct((L, D), F32), jax.ShapeDtypeStruct((L, 2 * D), BF16), jax.ShapeDtypeStruct((L, D), BF16),
                   jax.ShapeDtypeStruct((8, D), F32), jax.ShapeDtypeStruct((SG_HEADS, SG_CHUNK, SG_CHUNK), F32),
                   jax.ShapeDtypeStruct((SG_CHUNK, SG_HEADS), F32)],
        scratch_shapes=[pltpu.VMEM((tm, D), F32)],
        compiler_params=_params(1, 56),
    )
    return call(dx1, x, y, uv, vm, pv, w_in, w_s, vg, w_out, *tail)


def _final(x, target, fg, tm):
    L = x.shape[0]

    def body(x_ref, t_ref, g_ref, dx_ref, vs_ref):
        @pl.when(pl.program_id(0) == 0)
        def _():
            vs_ref[...] = jnp.zeros_like(vs_ref)

        xv, g = x_ref[...], g_ref[...]
        r = lax.rsqrt(jnp.mean(xv * xv, axis=-1, keepdims=True) + EPS)
        xn = xv * r
        e = xn * g - t_ref[...]
        vs_ref[0:1, :] += jnp.sum(e * e)
        dout = e * (1.0 / D)
        vs_ref[1:2, :] += _sum0(dout * xn)
        dxn = dout * g
        dx_ref[...] = r * (dxn - xn * jnp.mean(dxn * xn, axis=-1, keepdims=True))

    return pl.pallas_call(
        body, name="final_loss", grid=(L // tm,),
        in_specs=[_rows(tm, D), _rows(tm, D), _whole((1, D))],
        out_specs=[_rows(tm, D), _whole((8, D))],
        out_shape=[jax.ShapeDtypeStruct((L, D), F32), jax.ShapeDtypeStruct((8, D), F32)],
        compiler_params=_params(1),
    )(x, target, fg)


def _pack_flat(arrs, multiple=LANES):
    flat = jnp.concatenate([a.reshape(-1).astype(F32) for a in arrs])
    return jnp.pad(flat, (0, -flat.shape[0] % multiple))


def _pack(arrs, row_multiple=8):
    return _pack_flat(arrs, row_multiple * LANES).reshape(-1, LANES)


def _unpack(buf, shapes, lead=()):
    flat = buf.reshape(lead + (-1,))
    out, off = [], 0
    for s in shapes:
        n = 1
        for d in s:
            n *= d
        out.append(flat[..., off:off + n].reshape(lead + tuple(s)))
        off += n
    return out


BIG = ("ff_w1", "ff_w2", "conv_w_in", "conv_w_out", "ssm_w_in", "ssm_glu_w", "ssm_w_out", "sg_w_in", "sg_w_out")
BIG_AXIS = {"ff_w1": 2, "ff_w2": 1, "conv_w_in": 2, "conv_w_out": 1, "ssm_w_in": 1, "ssm_glu_w": 1, "ssm_w_out": 1,
            "sg_w_in": 2, "sg_w_out": 1}
LAYER_WEIGHTS = (
    (("conv_w_in", 0), ("conv_w_out", 0), ("ff_w1", 0), ("ff_w2", 0)),
    (("ssm_w_in", 0), ("ssm_glu_w", 0), ("ssm_w_out", 0), ("ff_w1", 1), ("ff_w2", 1)),
    (("sg_w_in", 0), ("sg_w_out", 0), ("ff_w1", 2), ("ff_w2", 2)),
    (("conv_w_in", 1), ("conv_w_out", 1), ("ff_w1", 3), ("ff_w2", 3)),
)
GATHER_GROUPS = tuple(grp for lw in LAYER_WEIGHTS for grp in (lw[:-2], lw[-2:]))
SMALL_SHARDED = ("conv_w", "conv_b", "sg_v_g")
SMALL_WIDE_PADDED = ("ssm_b_re", "ssm_b_im")
SMALL = ("ada_b", "norm1_g", "norm2_g", "final_g", "ssm_a_re", "ssm_a_im", "ssm_log_dt", "ssm_b_re", "ssm_b_im", "ssm_c_re",
         "ssm_c_im", "ssm_d", "ssm_glu_b", "sg_w_s", "sg_b_s") + SMALL_SHARDED
WEIGHTS = ("ada_w", "ada_b", "norm1_g", "norm2_g", "ff_w1", "ff_w2", "final_g", "conv_w_in", "conv_w", "conv_b", "conv_w_out",
           "ssm_w_in", "ssm_a_re", "ssm_a_im", "ssm_log_dt", "ssm_b_re", "ssm_b_im", "ssm_c_re", "ssm_c_im", "ssm_d",
           "ssm_glu_w", "ssm_glu_b", "ssm_w_out", "sg_w_in", "sg_v_g", "sg_w_s", "sg_b_s", "sg_w_out")


def kernel(x, c, ada_w, ada_b, norm1_g, norm2_g, ff_w1, ff_w2, final_g, conv_w_in, conv_w, conv_b, conv_w_out, ssm_w_in, ssm_a_re, ssm_a_im, ssm_log_dt, ssm_b_re, ssm_b_im, ssm_c_re, ssm_c_im, ssm_d, ssm_glu_w, ssm_glu_b, ssm_w_out, sg_w_in, sg_v_g, sg_w_s, sg_b_s, sg_w_out, loss_target, m_ada_w, m_ada_b, m_norm1_g, m_norm2_g, m_ff_w1, m_ff_w2, m_final_g, m_conv_w_in, m_conv_w, m_conv_b, m_conv_w_out, m_ssm_w_in, m_ssm_a_re, m_ssm_a_im, m_ssm_log_dt, m_ssm_b_re, m_ssm_b_im, m_ssm_c_re, m_ssm_c_im, m_ssm_d, m_ssm_glu_w, m_ssm_glu_b, m_ssm_w_out, m_sg_w_in, m_sg_v_g, m_sg_w_s, m_sg_b_s, m_sg_w_out, v_ada_w, v_ada_b, v_norm1_g, v_norm2_g, v_ff_w1, v_ff_w2, v_final_g, v_conv_w_in, v_conv_w, v_conv_b, v_conv_w_out, v_ssm_w_in, v_ssm_a_re, v_ssm_a_im, v_ssm_log_dt, v_ssm_b_re, v_ssm_b_im, v_ssm_c_re, v_ssm_c_im, v_ssm_d, v_ssm_glu_w, v_ssm_glu_b, v_ssm_w_out, v_sg_w_in, v_sg_v_g, v_sg_w_s, v_sg_b_s, v_sg_w_out):
    args = dict(locals())
    w = {n: args[n] for n in WEIGHTS}
    m = {n: args["m_" + n] for n in WEIGHTS}
    v = {n: args["v_" + n] for n in WEIGHTS}
    L = x.shape[1]
    tm = min(L, 256)
    tm2 = min(L, 512)
    chip = 2 * lax.axis_index("x") + lax.axis_index("y")
    me = 2 * chip + lax.axis_index("c")
    xin = x[0]
    target = loss_target[0]
    chip1 = chip.reshape(1).astype(jnp.int32)
    place = jnp.stack([chip, lax.axis_index("c")]).astype(jnp.int32)

    gathers = []

    casts = {}

    def cast_group(g, after=None):
        entries = GATHER_GROUPS[g]
        axes = [BIG_AXIS[n] for n, _ in entries]
        casts[g] = _cast_place([w[n] for n, _ in entries], [li for _, li in entries], axes, place, f"cast_group{g}", after)

    def start_gather(g, after):
        if g not in casts:
            cast_group(g)
        axes = [BIG_AXIS[n] for n, _ in GATHER_GROUPS[g]]
        lands = casts[g]
        s_sems, r_sems, lands, token = _gather_start(lands, axes, f"gather_start{g}", after)
        gathers.append((s_sems, r_sems, lands, axes))
        return token

    def weights_of(g, after):
        s_sems, r_sems, lands, axes = gathers[g]
        lands = _gather_wait(s_sems, r_sems, lands, axes, f"gather_wait{g}", after)
        lands = _gather_share(lands, axes, f"gather_share{g}")
        token = start_gather(g + 2, lands[0]) if g + 2 < len(GATHER_GROUPS) else None
        return dict(zip([n for n, _ in GATHER_GROUPS[g]], lands)), token

    small_in = _pack([c, conv_w, conv_b, sg_v_g])
    got = _allgather_small(small_in, "gather_small_inputs").reshape(N_DEV, -1)
    c_all, cw_sh, cb_sh, vg_sh = _unpack(got, [(D,), conv_w.shape, conv_b.shape, sg_v_g.shape], lead=(N_DEV,))
    conv_w_full = jnp.concatenate([cw_sh[2 * k] for k in range(4)], axis=-1)
    conv_b_full = jnp.concatenate([cb_sh[2 * k] for k in range(4)], axis=-1)
    vg_full = jnp.concatenate([vg_sh[2 * k] for k in range(4)], axis=-1)
    c16 = jnp.pad(c_all, ((0, 16 - N_DEV), (0, 0)))

    cols = ada_w.shape[2]
    ada_b_cols = lax.dynamic_slice_in_dim(ada_b, chip * cols, cols, axis=1)[:, None, :]
    mod_sh = _ada_fwd(c16, ada_w, ada_b_cols)[:, :N_DEV, :]
    mod_all = _allgather_small(_pack([mod_sh]), "gather_mod").reshape(N_DEV, -1)
    mod_all = _unpack(mod_all, [mod_sh.shape], lead=(N_DEV,))[0]
    mod_mine = lax.dynamic_index_in_dim(mod_all[0::2], me, axis=2, keepdims=False)
    mod_mine = mod_mine.transpose(1, 0, 2).reshape(DEPTH, 6, D)
    pv = jnp.concatenate([mod_mine, norm1_g[:, None, :], norm2_g[:, None, :]], axis=1)

    second_started = start_gather(1, start_gather(0, pv))
    for g in range(2, len(GATHER_GROUPS)):
        cast_group(g, second_started)

    cw_rows = jnp.concatenate([conv_w_full, conv_b_full[:, None, :], jnp.zeros((conv_w_full.shape[0], 4, D), F32)], axis=1)

    a_re, a_im = ssm_a_re[0], ssm_a_im[0]
    log_dt = ssm_log_dt[0][:, None]
    bt_re, bt_im = ssm_b_re[0].transpose(2, 0, 1), ssm_b_im[0].transpose(2, 0, 1)
    abar_re, abar_im, bbar_re, bbar_im = _s5_params_fwd(a_re, a_im, log_dt, bt_re, bt_im, after=second_started)
    ar_vec, ai_vec = abar_re.reshape(1, NSTATE), abar_im.reshape(1, NSTATE)
    bd_re, bd_im = _bf(_blockdiag_b(bbar_re)), _bf(_blockdiag_b(bbar_im))
    cd_re, cd_im = _bf(_blockdiag_c(ssm_c_re[0])), _bf(_blockdiag_c(ssm_c_im[0]))

    saved = []
    fulls = []
    xl = xin
    for i in range(DEPTH):
        kind = MIXER_OF_LAYER[i]
        j = i // 3
        first_after = [second_started, bd_re, bd_im, cd_re, cd_im, cw_rows] + [casts[g][0] for g in range(2, len(GATHER_GROUPS))]
        full, tok = weights_of(2 * i, first_after if i == 0 else [xl])
        fulls.append(full)
        if kind == 0:
            x1, h, bcx, conv, q, y = _conv_fwd(xl, pv, full["conv_w_in"], full["conv_w_out"], cw_rows, i, j, tm2, after=tok)
            mix = dict(h=h, bcx=bcx, conv=conv, q=q, y=y)
        elif kind == 1:
            xp = _to_segments(xl)
            h, u, e_re, e_im, u_b = _s5_fwd_ends(xp, pv, full["ssm_w_in"], bd_re, bd_im, ar_vec, ai_vec, i, tm2, after=tok)
            s0_re, s0_im = _s5_segment_states(e_re, e_im, ar_vec, ai_vec, L // NSEG, adjoint=False)
            x1p, s_re, s_im, y1, zg, y3, y, s_re_b, s_im_b = _s5_fwd_out(
                xp, u, pv, bd_re, bd_im, s0_re, s0_im, ar_vec, ai_vec, cd_re, cd_im, ssm_d, full["ssm_glu_w"], ssm_glu_b,
                full["ssm_w_out"], i, tm)
            x1 = _from_segments(x1p)
            mix = dict(xp=xp, h=h, u=u, u_b=u_b, s_re=s_re, s_im=s_im, s_re_b=s_re_b, s_im_b=s_im_b, y1=y1, zg=zg, y3=y3, y=y)
        else:
            x1, h, uv, vm, q, y = _sg_fwd(xl, pv, full["sg_w_in"], sg_w_s[0], sg_b_s[0].T, vg_full, full["sg_w_out"], i, tm2,
                                          after=tok)
            mix = dict(h=h, uv=uv, vm=vm, q=q, y=y)
        ffn_weights, tok = weights_of(2 * i + 1, [x1])
        full.update(ffn_weights)
        x2, h2, a, f = _ffn_fwd(x1, pv, full["ff_w1"], full["ff_w2"], i, tm2, after=tok)
        saved.append(dict(x=xl, x1=x1, h2=h2, a=a, f=f, **mix))
        xl = x2

    dxl, vs_fin = _final(xl, target, final_g[None, :], tm2)

    gfull = {n: [None] * w[n].shape[0] for n in BIG}
    vs_mix, vs_ffn = [None] * DEPTH, [None] * DEPTH
    small_g = {}
    scatters = {}
    token = None

    def start_scatter(key, entries, after):
        garrs = [gfull[n][li][None] for n, li in entries]
        gaxes = [BIG_AXIS[n] for n, _ in entries]
        s_sems, r_sems, garrs, lands, tok = _scatter_start(garrs, gaxes, f"scatter_start{key}", after)
        scatters[key] = (s_sems, r_sems, garrs, lands, gaxes, entries)
        return tok

    for i in reversed(range(DEPTH)):
        kind = MIXER_OF_LAYER[i]
        j = i // 3
        sv = saved[i]
        full = fulls[i]
        dx1, p_b, da_b, df_b, vs_ffn[i] = _ffn_bwd(dxl, sv["x1"], sv["a"], sv["f"], pv, full["ff_w1"], full["ff_w2"], i, tm,
                                                   after=token)
        gfull["ff_w1"][i] = _mm_tn(sv["h2"], da_b, f"wgrad_ff_w1_{i}")
        gfull["ff_w2"][i] = _mm_tn(p_b, df_b, f"wgrad_ff_w2_{i}")
        if i == 0:
            token = start_scatter("0f", LAYER_WEIGHTS[0][2:], dx1)
        if kind == 0:
            dxl, dbcx_b, dy_b, vsm = _conv_bwd(dx1, sv["x"], sv["y"], sv["bcx"], sv["conv"], pv, full["conv_w_in"],
                                               full["conv_w_out"], cw_rows, i, j, tm2, after=token if i == 0 else None)
            gfull["conv_w_in"][j] = _mm_tn(sv["h"], dbcx_b, f"wgrad_conv_w_in_{j}")
            gfull["conv_w_out"][j] = _mm_tn(sv["q"], dy_b, f"wgrad_conv_w_out_{j}")
            small_g.setdefault("conv_w", [None, None])[j] = vsm[3:6]
            small_g.setdefault("conv_b", [None, None])[j] = vsm[6]
        elif kind == 1:
            dx1p = _to_segments(dx1)
            dy_b, y2_b, dzg_b, dy1_b, du_skip, eb_re, eb_im, vsm = _s5_bwd_ends(
                dx1p, sv["y"], sv["y1"], sv["zg"], sv["u"], pv, cd_re, cd_im, ar_vec, ai_vec, ssm_d, full["ssm_glu_w"],
                full["ssm_w_out"], i, tm)
            l0_re, l0_im = _s5_segment_states(eb_re, eb_im, ar_vec, ai_vec, L // NSEG, adjoint=True)
            dxp, du_b, lam_re, lam_im, dabar, vs_in = _s5_bwd_in(
                dx1p, dy1_b, du_skip, sv["xp"], sv["s_re"], sv["s_im"], pv, bd_re, bd_im, cd_re, cd_im, l0_re, l0_im,
                ar_vec, ai_vec, full["ssm_w_in"], i, tm)
            dxl = _from_segments(dxp)
            gfull["ssm_w_out"][0] = _mm_tn(sv["y3"], dy_b, "wgrad_ssm_w_out")
            gfull["ssm_glu_w"][0] = _mm_tn(y2_b, dzg_b, "wgrad_ssm_glu_w")
            gfull["ssm_w_in"][0] = _mm_tn(sv["h"], du_b, "wgrad_ssm_w_in")
            s5_late = dict(s_re=sv["s_re_b"], s_im=sv["s_im_b"], u=sv["u_b"], dy1_b=dy1_b, lam_re=lam_re, lam_im=lam_im, dabar=dabar)
            small_g.update(ssm_d=vsm[2], ssm_glu_b=vsm[1])
            vsm = jnp.concatenate([vsm[0:1], vs_in[1:3], jnp.zeros((5, D), F32)], axis=0)
        else:
            dxl, duv_b, dy_b, vsm, d_ws, d_bt = _sg_bwd(dx1, sv["x"], sv["y"], sv["uv"], sv["vm"], pv, full["sg_w_in"],
                                                        sg_w_s[0], vg_full, full["sg_w_out"], i, tm2)
            gfull["sg_w_in"][0] = _mm_tn(sv["h"], duv_b, "wgrad_sg_w_in")
            gfull["sg_w_out"][0] = _mm_tn(sv["q"], dy_b, "wgrad_sg_w_out")
            small_g.update(sg_v_g=vsm[3], sg_w_s=d_ws, sg_b_s=d_bt.T)
        vs_mix[i] = vsm
        token = start_scatter(str(i), LAYER_WEIGHTS[i], dxl) if i > 0 else start_scatter("0c", LAYER_WEIGHTS[0][:2], dxl)
    grad_x = dxl[None]

    sums = {n: [None] * w[n].shape[0] for n in BIG}

    def collect(key, after):
        s_sems, r_sems, garrs, lands, gaxes, entries = scatters[key]
        garrs, recv = _scatter_wait(s_sems, r_sems, garrs, lands, gaxes, f"scatter_wait{key}", after)
        for (n, li), t in zip(entries, _sum_parts(recv, garrs, gaxes, chip1, f"sum_group{key}")):
            sums[n][li] = t
        return sums[entries[-1][0]][entries[-1][1]]

    after = token
    for key in ("3", "2", "1"):
        after = collect(key, after)
    early = [(n, li) for i in (3, 2, 1) for n, li in LAYER_WEIGHTS[i]]
    late = list(LAYER_WEIGHTS[0][2:]) + list(LAYER_WEIGHTS[0][:2])
    s_sems, r_sems, mine_thru, lands, tok = _swap_start([sums[n][li] for n, li in early], "swap_start_early", after)

    blocks = dict(
        c_re=_mm_tn_blocks(s5_late["s_re"], s5_late["dy1_b"], S5_BP, S5_BH, "wgrad_s5_c_re", after=tok),
        c_im=_mm_tn_blocks(s5_late["s_im"], s5_late["dy1_b"], S5_BP, S5_BH, "wgrad_s5_c_im", after=tok),
        b_re=_mm_tn_blocks(s5_late["u"], s5_late["lam_re"], S5_BH, S5_BP, "wgrad_s5_b_re", after=tok),
        b_im=_mm_tn_blocks(s5_late["u"], s5_late["lam_im"], S5_BH, S5_BP, "wgrad_s5_b_im", after=tok))
    d_are, d_aim, d_ldt, d_btre, d_btim = _s5_params_bwd(
        a_re, a_im, log_dt, bt_re, bt_im, s5_late["dabar"][0].reshape(S5_G, S5_P), s5_late["dabar"][1].reshape(S5_G, S5_P),
        _unblock_b(blocks["b_re"]), _unblock_b(blocks["b_im"]))
    small_g.update(ssm_a_re=d_are, ssm_a_im=d_aim, ssm_log_dt=d_ldt, ssm_b_re=d_btre.transpose(1, 2, 0),
                   ssm_b_im=d_btim.transpose(1, 2, 0), ssm_c_re=_unblock_c(blocks["c_re"]), ssm_c_im=-_unblock_c(blocks["c_im"]))

    mine_thru, got = _swap_wait(s_sems, r_sems, mine_thru, lands, "swap_wait_early", blocks["b_im"])
    sib = dict(zip(early, got))
    for (n, li), t in zip(early, mine_thru):
        sums[n][li] = t
    after = got[-1]
    for key in ("0f", "0c"):
        after = collect(key, after)
    sib.update(zip(late, _swap_with_sibling([sums[n][li] for n, li in late], "swap_grad_sums_late")))

    dmod = _mod_bwd(jnp.stack(vs_mix), jnp.stack(vs_ffn), pv)
    small_g.update(ada_b=dmod[:, :6, :], norm1_g=dmod[:, 6, :], norm2_g=dmod[:, 7, :], final_g=vs_fin[1],
                   conv_w=jnp.stack(small_g["conv_w"]), conv_b=jnp.stack(small_g["conv_b"]))

    loss_part = (0.5 / D) * vs_fin[0, 0:1]
    part_shapes = [(1,)] + [tuple(small_g[n].shape) for n in SMALL]
    slots = _reduce_pair(_pack([loss_part] + [small_g[n] for n in SMALL], 16), "reduce_small_pair", sib[late[-1]])
    s_sems, r_sems, slots, tok = _reduce_cross_start(slots, "reduce_small_cross_start")

    res = {}
    for n in BIG:
        res[n] = _adamw_layers(w[n], sums[n], [sib[(n, li)] for li in range(w[n].shape[0])], m[n], v[n], f"adamw_{n}", after=tok)
        tok = res[n][0]

    slots = _reduce_cross_wait(s_sems, r_sems, slots, "reduce_small_cross_wait", tok)
    parts_sum = _reduce_finish(slots, "reduce_small_finish")
    summed = _unpack(parts_sum, part_shapes)
    loss = summed[0][0]
    gsum = dict(zip(SMALL, summed[1:]))
    dmod_all = _allgather_small(_pack([small_g["ada_b"]]), "gather_dmod", parts_sum)
    dmod_all = dmod_all.reshape(N_DEV, DEPTH, 6 * D)
    dmod_cols = lax.dynamic_slice_in_dim(dmod_all, chip * cols, cols, axis=2).transpose(1, 0, 2)
    g_ada_w = _ada_bwd(c16, jnp.pad(dmod_cols, ((0, 0), (0, 16 - N_DEV), (0, 0))))

    shp = ada_w.shape
    two = lambda t: t.reshape(shp[0] * shp[1], shp[2])
    res["ada_w"] = [t.reshape(shp) for t in _adamw(two(ada_w), [two(g_ada_w)], two(m_ada_w), two(v_ada_w), "adamw_ada_w")]

    def mine(n):
        g = gsum[n]
        if n in SMALL_SHARDED:
            g = lax.dynamic_slice_in_dim(g, chip * w[n].shape[-1], w[n].shape[-1], axis=g.ndim - 1)
        return g.reshape(w[n].shape)

    for k, names in enumerate(([n for n in SMALL if n not in SMALL_WIDE_PADDED], list(SMALL_WIDE_PADDED))):
        outs = _adamw_many([w[n] for n in names], [mine(n) for n in names], [m[n] for n in names], [v[n] for n in names],
                           f"adamw_small{k}")
        for idx, n in enumerate(names):
            res[n] = [outs[part][idx] for part in range(4)]

    outs = [loss, grad_x]
    for part in range(4):
        outs += [res[n][part] for n in WEIGHTS]
    return tuple(outs)
```

```python
import functools

import jax
import jax.numpy as jnp
from jax import lax
from jax.experimental import pallas as pl
from jax.experimental.pallas import tpu as pltpu

F32 = jnp.float32
BF16 = jnp.bfloat16
D = 1024
EPS = 1e-6
DEPTH = 4
MIXER_OF_LAYER = (0, 1, 2, 0)
S5_G, S5_H, S5_P = 64, 16, 64
S5_NB = 4
S5_BH = S5_H * 16
S5_BP = S5_P * 16
NSTATE = S5_G * S5_P
SG_HEADS, SG_CHUNK = 8, 128
ADAM_LR, ADAM_B1, ADAM_B2, ADAM_EPS, ADAM_WD, ADAM_STEP = 0.001, 0.9, 0.999, 1e-08, 0.01, 10
N_DEV = 8
MESH = pl.DeviceIdType.MESH
LANES = 1024
R_SH1, R_SC1, R_G1, R_SH2, R_SC2, R_G2, R_N1, R_N2 = range(8)


def _dot(a, b):
    return jnp.dot(a, b, preferred_element_type=F32)


def _dot_nt(a, b):
    return lax.dot_general(a, b, (((1,), (1,)), ((), ())), preferred_element_type=F32)


def _dot_tn(a, b):
    return lax.dot_general(a, b, (((0,), (0,)), ((), ())), preferred_element_type=F32)


def _bf(x):
    return x.astype(BF16)


def _sum0(x):
    return jnp.sum(x, axis=0, keepdims=True)


def _params(n_axes, vmem_mb=48):
    return pltpu.CompilerParams(dimension_semantics=("arbitrary",) * n_axes, vmem_limit_bytes=vmem_mb << 20)


def _rows(tm, cols, nt=None):
    if nt is None:
        return pl.BlockSpec((tm, cols), lambda i: (i, 0))
    return pl.BlockSpec((tm, cols), lambda i: (nt - 1 - i, 0))


def _whole(shape):
    nd = len(shape)
    return pl.BlockSpec(shape, lambda *_: (0,) * nd)


def _layer_w(r, c, layer):
    return pl.BlockSpec((None, r, c), lambda *_: (layer, 0, 0), pipeline_mode=pl.Buffered(1))


def _const_w(shape):
    nd = len(shape)
    return pl.BlockSpec(shape, lambda *_: (0,) * nd, pipeline_mode=pl.Buffered(1))


def _call_after(after, body, n_in, *, in_specs, **kw):
    if after is None:
        return pl.pallas_call(body, in_specs=in_specs, **kw), ()

    def body_after(*refs):
        return body(*refs[:n_in], *refs[n_in + 1:])

    return pl.pallas_call(body_after, in_specs=list(in_specs) + [pl.BlockSpec(memory_space=pl.ANY)], **kw), (after,)


def _norm_mod(x, ng, sc, sh):
    r = lax.rsqrt(jnp.mean(x * x, axis=-1, keepdims=True) + EPS)
    xn = x * r
    return (xn * ng) * (1.0 + sc) + sh, xn, r


def _norm_mod_bwd(dh, xn, r, ng, sc):
    dxn = dh * (ng * (1.0 + sc))
    return r * (dxn - xn * jnp.mean(dxn * xn, axis=-1, keepdims=True))


def _shift_down(z, prev8, k):
    row = lax.broadcasted_iota(jnp.int32, z.shape, 0)
    if k == 1:
        return jnp.where(row >= 1, pltpu.roll(z, 1, 0), prev8[7:8])
    return jnp.where(row >= 2, pltpu.roll(z, 2, 0), jnp.where(row == 0, prev8[6:7], prev8[7:8]))


def _shift_up(z, next8, k):
    n = z.shape[0]
    row = lax.broadcasted_iota(jnp.int32, z.shape, 0)
    if k == 1:
        return jnp.where(row <= n - 2, pltpu.roll(z, n - 1, 0), next8[0:1])
    return jnp.where(row <= n - 3, pltpu.roll(z, n - 2, 0), jnp.where(row == n - 2, next8[0:1], next8[1:2]))


def _place():
    x, y, c = lax.axis_index("x"), lax.axis_index("y"), lax.axis_index("c")
    chips = [(1 - x, y), (x, 1 - y), (1 - x, 1 - y)]
    return x, y, c, chips


def _allgather_small(x_shard, name, after=None):
    m_per, n = x_shard.shape

    def body(x_ref, out_ref, send_sems, recv_sems, local_sem):
        x, y, c, chips = _place()
        me, sibling = (x, y, c), (x, y, 1 - c)

        def rows(px, py, pc):
            return out_ref.at[pl.ds((4 * px + 2 * py + pc) * m_per, m_per), :]

        def copy(k, block, to, src=None):
            return pltpu.make_async_remote_copy(
                src_ref=rows(*block) if src is None else src, dst_ref=rows(*block),
                send_sem=send_sems.at[k], recv_sem=recv_sems.at[k], device_id=to, device_id_type=MESH)

        mine = pltpu.make_async_copy(x_ref, rows(*me), local_sem)
        mine.start()
        first = [copy(0, me, sibling, src=x_ref)]
        first += [copy(1 + j, me, (*chip, c), src=x_ref) for j, chip in enumerate(chips)]
        for cp in first:
            cp.start()
        passed = [copy(4 + j, (*chip, c), sibling) for j, chip in enumerate(chips)]
        for j, chip in enumerate(chips):
            copy(1 + j, (*chip, c), me).wait_recv()
            passed[j].start()
        copy(0, sibling, me).wait_recv()
        for j, chip in enumerate(chips):
            copy(4 + j, (*chip, 1 - c), me).wait_recv()
        for cp in first + passed:
            cp.wait_send()
        mine.wait()

    call, tail = _call_after(
        after, body, 1, name=name, out_shape=jax.ShapeDtypeStruct((N_DEV * m_per, n), F32),
        in_specs=[pl.BlockSpec(memory_space=pltpu.VMEM)], out_specs=pl.BlockSpec(memory_space=pltpu.VMEM),
        scratch_shapes=[pltpu.SemaphoreType.DMA((7,)), pltpu.SemaphoreType.DMA((7,)), pltpu.SemaphoreType.DMA],
        compiler_params=pltpu.CompilerParams(vmem_limit_bytes=48 << 20),
    )
    return call(x_shard, *tail)


def _reduce_pair(x_part, name, after=None):
    m, n = x_part.shape
    h = m // 2

    def body(x_ref, slots_ref, sib_buf, send_sem, recv_sem):
        x, y, c, _ = _place()
        swap = pltpu.make_async_remote_copy(src_ref=x_ref, dst_ref=sib_buf, send_sem=send_sem, recv_sem=recv_sem,
                                            device_id=(x, y, 1 - c), device_id_type=MESH)
        swap.start()
        swap.wait()
        mine = pl.ds(pl.multiple_of(c * h, 8), h)
        slots_ref[pl.ds(2 * x + y, 1)] = (x_ref[mine, :] + sib_buf[mine, :])[None]

    call, tail = _call_after(
        after, body, 1, name=name, out_shape=jax.ShapeDtypeStruct((4, h, n), F32),
        in_specs=[pl.BlockSpec(memory_space=pltpu.VMEM)], out_specs=pl.BlockSpec(memory_space=pltpu.VMEM),
        scratch_shapes=[pltpu.VMEM((m, n), F32), pltpu.SemaphoreType.DMA, pltpu.SemaphoreType.DMA],
        compiler_params=pltpu.CompilerParams(vmem_limit_bytes=48 << 20),
    )
    return call(x_part, *tail)


def _reduce_cross_start(slots, name):
    def body(slots_ref, send_sems, recv_sems, thru, token):
        x, y, c, chips = _place()
        mine = slots_ref.at[pl.ds(2 * x + y, 1)]
        for j, chip in enumerate(chips):
            pltpu.make_async_remote_copy(src_ref=mine, dst_ref=mine, send_sem=send_sems.at[j], recv_sem=recv_sems.at[j],
                                         device_id=(*chip, c), device_id_type=MESH).start()
        token[...] = jnp.zeros_like(token)

    res = pl.pallas_call(
        body, name=name,
        out_shape=(pltpu.SemaphoreType.DMA((3,)), pltpu.SemaphoreType.DMA((3,)), pltpu.HBM(slots.shape, F32),
                   jax.ShapeDtypeStruct((8, 128), F32)),
        in_specs=[HBM_SPEC], out_specs=(SEM_SPEC, SEM_SPEC, HBM_SPEC, pl.BlockSpec(memory_space=pltpu.VMEM)),
        input_output_aliases={0: 2}, compiler_params=SPLIT_COPY_PARAMS,
    )(*_in_hbm([slots]))
    return res


def _reduce_cross_wait(send_sems, recv_sems, slots, name, after):
    def body(slots_ref, s_sems, r_sems, after_ref, thru):
        x, y, c, chips = _place()
        for j, chip in enumerate(chips):
            theirs = slots_ref.at[pl.ds(2 * chip[0] + chip[1], 1)]
            cp = pltpu.make_async_remote_copy(src_ref=theirs, dst_ref=theirs, send_sem=s_sems.at[j], recv_sem=r_sems.at[j],
                                              device_id=(x, y, c), device_id_type=MESH)
            cp.wait_send()
            cp.wait_recv()

    return pl.pallas_call(
        body, name=name, out_shape=pltpu.HBM(slots.shape, F32),
        in_specs=[HBM_SPEC, SEM_SPEC, SEM_SPEC, ANY_SPEC], out_specs=HBM_SPEC,
        input_output_aliases={0: 0}, compiler_params=SPLIT_COPY_PARAMS,
    )(slots, send_sems, recv_sems, after)


def _reduce_finish(slots, name):
    _, h, n = slots.shape

    def body(slots_ref, out_ref, send_sem, recv_sem):
        x, y, c, _ = _place()
        mine = pl.ds(pl.multiple_of(c * h, 8), h)
        theirs = pl.ds(pl.multiple_of((1 - c) * h, 8), h)
        out_ref[mine, :] = ((slots_ref[0] + slots_ref[1]) + slots_ref[2]) + slots_ref[3]
        give = pltpu.make_async_remote_copy(src_ref=out_ref.at[mine, :], dst_ref=out_ref.at[mine, :], send_sem=send_sem,
                                            recv_sem=recv_sem, device_id=(x, y, 1 - c), device_id_type=MESH)
        give.start()
        give.wait_send()
        pltpu.make_async_remote_copy(src_ref=out_ref.at[theirs, :], dst_ref=out_ref.at[theirs, :], send_sem=send_sem,
                                     recv_sem=recv_sem, device_id=(x, y, c), device_id_type=MESH).wait_recv()

    return pl.pallas_call(
        body, name=name, out_shape=jax.ShapeDtypeStruct((2 * h, n), F32),
        in_specs=[pl.BlockSpec(memory_space=pltpu.VMEM)], out_specs=pl.BlockSpec(memory_space=pltpu.VMEM),
        scratch_shapes=[pltpu.SemaphoreType.DMA, pltpu.SemaphoreType.DMA],
        compiler_params=pltpu.CompilerParams(vmem_limit_bytes=48 << 20),
    )(slots)


def _shard_region(ref, full_shape, axis, chip_k, half=None):
    _, r, c = full_shape
    if axis == 1:
        rs = r // 4
        if half is None:
            return ref.at[:, pl.ds(pl.multiple_of(chip_k * rs, 128), rs), :]
        return ref.at[:, pl.ds(pl.multiple_of(chip_k * rs + half * (rs // 2), 128), rs // 2), :]
    cs = c // 4
    if half is None:
        return ref.at[:, :, pl.ds(pl.multiple_of(chip_k * cs, 128), cs)]
    return ref.at[:, pl.ds(pl.multiple_of(half * (r // 2), 128), r // 2), pl.ds(pl.multiple_of(chip_k * cs, 128), cs)]


HBM_SPEC = pl.BlockSpec(memory_space=pltpu.HBM)
SEM_SPEC = pl.BlockSpec(memory_space=pltpu.SEMAPHORE)
ANY_SPEC = pl.BlockSpec(memory_space=pl.ANY)
SPLIT_COPY_PARAMS = pltpu.CompilerParams(has_side_effects=pltpu.SideEffectType.DATAFLOW_SIDE_EFFECTING)


def _in_hbm(arrs):
    return [pltpu.with_memory_space_constraint(a, pltpu.HBM) for a in arrs]


def _cast_place(ws, layers, axes, place, name, after=None):
    n_arr = len(ws)
    in_specs, out_specs, fulls = [], [], []
    for w_stack, li, axis in zip(ws, layers, axes):
        _, r, c = w_stack.shape
        tr = r // 4
        fulls.append((1, 4 * r, c) if axis == 1 else (1, r, 4 * c))
        in_specs.append(pl.BlockSpec((None, tr, c), lambda i, p, li=li: (li, 2 * p[1] + i, 0)))
        if axis == 1:
            out_specs.append(pl.BlockSpec((None, tr, c), lambda i, p: (0, 4 * p[0] + 2 * p[1] + i, 0)))
        else:
            out_specs.append(pl.BlockSpec((None, tr, c), lambda i, p: (0, 2 * p[1] + i, p[0])))

    extra = [] if after is None else [after]

    def body(p_ref, *refs):
        for a in range(n_arr):
            refs[n_arr + len(extra) + a][...] = _bf(refs[a][...])

    return pl.pallas_call(
        body, name=name,
        grid_spec=pltpu.PrefetchScalarGridSpec(num_scalar_prefetch=1, grid=(2,), in_specs=in_specs + [ANY_SPEC] * len(extra),
                                               out_specs=out_specs),
        out_shape=[jax.ShapeDtypeStruct(f, BF16) for f in fulls],
        compiler_params=_params(1),
    )(place, *ws, *extra)


def _gather_start(lands, axes, name, after):
    n_arr = len(lands)
    fulls = [tuple(l.shape) for l in lands]

    def body(*refs):
        land = refs[:n_arr]
        send_sems, recv_sems = refs[n_arr + 1:n_arr + 3]
        token = refs[-1]
        x, y, c, chips = _place()
        k_me = 2 * x + y
        for a in range(n_arr):
            mine = _shard_region(land[a], fulls[a], axes[a], k_me, c)
            for j, chip in enumerate(chips):
                pltpu.make_async_remote_copy(
                    src_ref=mine, dst_ref=mine, send_sem=send_sems.at[a * 3 + j], recv_sem=recv_sems.at[a * 3 + j],
                    device_id=(*chip, c), device_id_type=MESH).start()
        token[...] = jnp.zeros_like(token)

    res = pl.pallas_call(
        body, name=name,
        out_shape=(pltpu.SemaphoreType.DMA((3 * n_arr,)), pltpu.SemaphoreType.DMA((3 * n_arr,)),
                   *[pltpu.HBM(f, BF16) for f in fulls], jax.ShapeDtypeStruct((8, 128), F32)),
        in_specs=[HBM_SPEC] * n_arr + [ANY_SPEC],
        out_specs=(SEM_SPEC, SEM_SPEC, *[HBM_SPEC] * n_arr, pl.BlockSpec(memory_space=pltpu.VMEM)),
        input_output_aliases={a: 2 + a for a in range(n_arr)},
        compiler_params=SPLIT_COPY_PARAMS,
    )(*_in_hbm(lands), after)
    return res[0], res[1], list(res[2:2 + n_arr]), res[-1]


def _gather_wait(send_sems, recv_sems, lands, axes, name, after):
    n_arr = len(lands)
    fulls = [tuple(l.shape) for l in lands]

    def body(*refs):
        land = refs[:n_arr]
        s_sems, r_sems = refs[n_arr:n_arr + 2]
        x, y, c, chips = _place()
        for a in range(n_arr):
            for j, chip in enumerate(chips):
                k_j = 2 * chip[0] + chip[1]
                got = _shard_region(land[a], fulls[a], axes[a], k_j, c)
                cp = pltpu.make_async_remote_copy(
                    src_ref=got, dst_ref=got, send_sem=s_sems.at[a * 3 + j], recv_sem=r_sems.at[a * 3 + j],
                    device_id=(x, y, c), device_id_type=MESH)
                cp.wait_send()
                cp.wait_recv()

    res = pl.pallas_call(
        body, name=name,
        out_shape=tuple(pltpu.HBM(f, BF16) for f in fulls),
        in_specs=[HBM_SPEC] * n_arr + [SEM_SPEC, SEM_SPEC] + [ANY_SPEC] * len(after),
        out_specs=tuple([HBM_SPEC] * n_arr),
        input_output_aliases={a: a for a in range(n_arr)},
        compiler_params=SPLIT_COPY_PARAMS,
    )(*lands, send_sems, recv_sems, *after)
    return list(res)


def _gather_share(lands, axes, name):
    n_arr = len(lands)
    fulls = [tuple(l.shape) for l in lands]

    def body(*refs):
        land_in, land = refs[:n_arr], refs[n_arr:2 * n_arr]
        send_sems, recv_sems = refs[2 * n_arr:]
        x, y, c, chips = _place()
        copies = []
        for a in range(n_arr):
            for j, k_j in enumerate([2 * chip[0] + chip[1] for chip in chips] + [2 * x + y]):
                cp = pltpu.make_async_remote_copy(
                    src_ref=_shard_region(land_in[a], fulls[a], axes[a], k_j, c),
                    dst_ref=_shard_region(land[a], fulls[a], axes[a], k_j, c),
                    send_sem=send_sems.at[a * 4 + j], recv_sem=recv_sems.at[a * 4 + j],
                    device_id=(x, y, 1 - c), device_id_type=MESH)
                cp.start()
                copies.append(cp)
        for cp in copies:
            cp.wait()

    return pl.pallas_call(
        body, name=name, out_shape=[jax.ShapeDtypeStruct(f, BF16) for f in fulls],
        in_specs=[ANY_SPEC] * n_arr, out_specs=[ANY_SPEC] * n_arr,
        input_output_aliases={a: a for a in range(n_arr)},
        scratch_shapes=[pltpu.SemaphoreType.DMA((4 * n_arr,)), pltpu.SemaphoreType.DMA((4 * n_arr,))],
    )(*lands)


def _scatter_shapes(grads, axes):
    out = []
    for g, ax in zip(grads, axes):
        shp = list(g.shape)
        shp[ax] //= 4
        out.append((3,) + tuple(shp[1:]))
    return out


def _scatter_start(grads, axes, name, after):
    n_arr = len(grads)
    shapes = _scatter_shapes(grads, axes)
    lands = [lax.empty(s, BF16) for s in shapes]

    def body(*refs):
        ins, land = refs[:n_arr], refs[n_arr:2 * n_arr]
        send_sems, recv_sems = refs[2 * n_arr + 1:2 * n_arr + 3]
        token = refs[-1]
        x, y, c, chips = _place()
        for a in range(n_arr):
            for j, chip in enumerate(chips):
                k_j = 2 * chip[0] + chip[1]
                pltpu.make_async_remote_copy(
                    src_ref=_shard_region(ins[a], grads[a].shape, axes[a], k_j), dst_ref=land[a].at[pl.ds(j, 1)],
                    send_sem=send_sems.at[a * 3 + j], recv_sem=recv_sems.at[a * 3 + j],
                    device_id=(*chip, c), device_id_type=MESH).start()
        token[...] = jnp.zeros_like(token)

    res = pl.pallas_call(
        body, name=name,
        out_shape=(pltpu.SemaphoreType.DMA((3 * n_arr,)), pltpu.SemaphoreType.DMA((3 * n_arr,)),
                   *[pltpu.HBM(g.shape, BF16) for g in grads], *[pltpu.HBM(s, BF16) for s in shapes],
                   jax.ShapeDtypeStruct((8, 128), F32)),
        in_specs=[HBM_SPEC] * (2 * n_arr) + [ANY_SPEC],
        out_specs=(SEM_SPEC, SEM_SPEC, *[HBM_SPEC] * (2 * n_arr), pl.BlockSpec(memory_space=pltpu.VMEM)),
        input_output_aliases={a: 2 + a for a in range(2 * n_arr)},
        compiler_params=SPLIT_COPY_PARAMS,
    )(*_in_hbm(grads), *_in_hbm(lands), after)
    return res[0], res[1], list(res[2:2 + n_arr]), list(res[2 + n_arr:2 + 2 * n_arr]), res[-1]


def _scatter_wait(send_sems, recv_sems, grads, lands, axes, name, after):
    n_arr = len(grads)

    def body(*refs):
        ins, land = refs[:n_arr], refs[n_arr:2 * n_arr]
        s_sems, r_sems = refs[2 * n_arr:2 * n_arr + 2]
        x, y, c, chips = _place()
        for a in range(n_arr):
            for j, chip in enumerate(chips):
                k_j = 2 * chip[0] + chip[1]
                cp = pltpu.make_async_remote_copy(
                    src_ref=_shard_region(ins[a], grads[a].shape, axes[a], k_j), dst_ref=land[a].at[pl.ds(j, 1)],
                    send_sem=s_sems.at[a * 3 + j], recv_sem=r_sems.at[a * 3 + j],
                    device_id=(x, y, c), device_id_type=MESH)
                cp.wait_send()
                cp.wait_recv()

    res = pl.pallas_call(
        body, name=name,
        out_shape=(*[pltpu.HBM(g.shape, BF16) for g in grads], *[pltpu.HBM(l.shape, BF16) for l in lands]),
        in_specs=[HBM_SPEC] * (2 * n_arr) + [SEM_SPEC, SEM_SPEC, ANY_SPEC],
        out_specs=tuple([HBM_SPEC] * (2 * n_arr)),
        input_output_aliases={a: a for a in range(2 * n_arr)},
        compiler_params=SPLIT_COPY_PARAMS,
    )(*grads, *lands, send_sems, recv_sems, after)
    return list(res[:n_arr]), list(res[n_arr:])


def _swap_start(arrs, name, after):
    n_arr = len(arrs)
    lands = [lax.empty(a.shape, a.dtype) for a in arrs]

    def body(*refs):
        ins, land = refs[:n_arr], refs[n_arr:2 * n_arr]
        send_sems, recv_sems = refs[2 * n_arr + 1:2 * n_arr + 3]
        token = refs[-1]
        x, y, c, _ = _place()
        for a in range(n_arr):
            pltpu.make_async_remote_copy(
                src_ref=ins[a], dst_ref=land[a], send_sem=send_sems.at[a], recv_sem=recv_sems.at[a],
                device_id=(x, y, 1 - c), device_id_type=MESH).start()
        token[...] = jnp.zeros_like(token)

    res = pl.pallas_call(
        body, name=name,
        out_shape=(pltpu.SemaphoreType.DMA((n_arr,)), pltpu.SemaphoreType.DMA((n_arr,)),
                   *[pltpu.HBM(a.shape, a.dtype) for a in arrs], *[pltpu.HBM(a.shape, a.dtype) for a in arrs],
                   jax.ShapeDtypeStruct((8, 128), F32)),
        in_specs=[HBM_SPEC] * (2 * n_arr) + [ANY_SPEC],
        out_specs=(SEM_SPEC, SEM_SPEC, *[HBM_SPEC] * (2 * n_arr), pl.BlockSpec(memory_space=pltpu.VMEM)),
        input_output_aliases={a: 2 + a for a in range(2 * n_arr)},
        compiler_params=SPLIT_COPY_PARAMS,
    )(*_in_hbm(arrs), *_in_hbm(lands), after)
    return res[0], res[1], list(res[2:2 + n_arr]), list(res[2 + n_arr:2 + 2 * n_arr]), res[-1]


def _swap_wait(send_sems, recv_sems, arrs, lands, name, after):
    n_arr = len(arrs)

    def body(*refs):
        ins, land = refs[:n_arr], refs[n_arr:2 * n_arr]
        s_sems, r_sems = refs[2 * n_arr:2 * n_arr + 2]
        x, y, c, _ = _place()
        for a in range(n_arr):
            cp = pltpu.make_async_remote_copy(
                src_ref=ins[a], dst_ref=land[a], send_sem=s_sems.at[a], recv_sem=r_sems.at[a],
                device_id=(x, y, c), device_id_type=MESH)
            cp.wait_send()
            cp.wait_recv()

    res = pl.pallas_call(
        body, name=name,
        out_shape=(*[pltpu.HBM(a.shape, a.dtype) for a in arrs], *[pltpu.HBM(a.shape, a.dtype) for a in arrs]),
        in_specs=[HBM_SPEC] * (2 * n_arr) + [SEM_SPEC, SEM_SPEC, ANY_SPEC],
        out_specs=tuple([HBM_SPEC] * (2 * n_arr)),
        input_output_aliases={a: a for a in range(2 * n_arr)},
        compiler_params=SPLIT_COPY_PARAMS,
    )(*arrs, *lands, send_sems, recv_sems, after)
    return list(res[:n_arr]), list(res[n_arr:])


def _swap_with_sibling(arrs, name):
    n_arr = len(arrs)

    def body(*refs):
        ins, outs = refs[:n_arr], refs[n_arr:2 * n_arr]
        send_sems, recv_sems = refs[2 * n_arr:]
        x, y, c, _ = _place()
        copies = []
        for a in range(n_arr):
            cp = pltpu.make_async_remote_copy(
                src_ref=ins[a], dst_ref=outs[a], send_sem=send_sems.at[a], recv_sem=recv_sems.at[a],
                device_id=(x, y, 1 - c), device_id_type=MESH)
            cp.start()
            copies.append(cp)
        for cp in copies:
            cp.wait()

    any_spec = pl.BlockSpec(memory_space=pl.ANY)
    return pl.pallas_call(
        body, name=name, out_shape=[jax.ShapeDtypeStruct(a.shape, a.dtype) for a in arrs],
        in_specs=[any_spec] * n_arr, out_specs=[any_spec] * n_arr,
        scratch_shapes=[pltpu.SemaphoreType.DMA((n_arr,)), pltpu.SemaphoreType.DMA((n_arr,))],
    )(*arrs)


def _mm_tn(a, b, name, out_dtype=BF16):
    L, m = a.shape
    n = b.shape[1]
    bm, bn = min(m, 1024), min(n, 1024)
    nk = 2 if (m // bm) * (n // bn) == 1 and L % 32 == 0 else 1
    bk = L // nk

    def body(a_ref, b_ref, o_ref, *acc):
        part = _dot_tn(_bf(a_ref[...]), _bf(b_ref[...]))
        if nk == 1:
            o_ref[...] = part.astype(out_dtype)
        else:
            @pl.when(pl.program_id(2) == 0)
            def _():
                acc[0][...] = part

            @pl.when(pl.program_id(2) == 1)
            def _():
                o_ref[...] = (acc[0][...] + part).astype(out_dtype)

    return pl.pallas_call(
        body, name=name, grid=(m // bm, n // bn, nk),
        in_specs=[pl.BlockSpec((bk, bm), lambda i, j, k: (k, i)), pl.BlockSpec((bk, bn), lambda i, j, k: (k, j))],
        out_specs=pl.BlockSpec((bm, bn), lambda i, j, k: (i, j)),
        out_shape=jax.ShapeDtypeStruct((m, n), out_dtype),
        scratch_shapes=[pltpu.VMEM((bm, bn), F32)] if nk > 1 else [],
        compiler_params=_params(3),
    )(a, b)


def _mm_tn_blocks(a, b, wa, wb, name, after=None):
    L = a.shape[0]
    nb = a.shape[1] // wa
    bk = min(L, 4096)
    nk = L // bk

    def body(a_ref, b_ref, o_ref):
        @pl.when(pl.program_id(1) == 0)
        def _():
            o_ref[...] = jnp.zeros_like(o_ref)

        o_ref[...] += _dot_tn(_bf(a_ref[...]), _bf(b_ref[...]))

    call, tail = _call_after(
        after, body, 2, name=name, grid=(nb, nk),
        in_specs=[pl.BlockSpec((bk, wa), lambda j, k: (k, j)), pl.BlockSpec((bk, wb), lambda j, k: (k, j))],
        out_specs=pl.BlockSpec((None, wa, wb), lambda j, k: (j, 0, 0)),
        out_shape=jax.ShapeDtypeStruct((nb, wa, wb), F32),
        compiler_params=_params(2),
    )
    return call(a, b, *tail)


def _sum_parts(parts, owns, axes, chip, name):
    n_arr = len(parts)
    steps = 4
    in_specs, out_specs, shapes = [], [], []
    for part, axis in zip(parts, axes):
        _, r, c = part.shape
        tr = r // steps
        shapes.append((r, c))
        in_specs.append(pl.BlockSpec((3, tr, c), lambda i, k: (0, i, 0)))
        out_specs.append(pl.BlockSpec((tr, c), lambda i, k: (i, 0)))
    for part, axis in zip(parts, axes):
        _, r, c = part.shape
        tr = r // steps
        if axis == 1:
            in_specs.append(pl.BlockSpec((None, tr, c), lambda i, k: (0, steps * k[0] + i, 0)))
        else:
            in_specs.append(pl.BlockSpec((None, tr, c), lambda i, k: (0, i, k[0])))

    def body(k_ref, *refs):
        for a in range(n_arr):
            p = refs[a][...].astype(F32)
            refs[2 * n_arr + a][...] = ((p[0] + p[1]) + p[2]) + refs[n_arr + a][...].astype(F32)

    return pl.pallas_call(
        body, name=name,
        grid_spec=pltpu.PrefetchScalarGridSpec(num_scalar_prefetch=1, grid=(steps,), in_specs=in_specs, out_specs=out_specs),
        out_shape=[jax.ShapeDtypeStruct(sh, F32) for sh in shapes],
        compiler_params=_params(1),
    )(chip, *parts, *owns)


def _adamw(w, g_parts, m, v, name):
    n_g = len(g_parts)
    if w.ndim == 2:
        r, c = w.shape
        tr = r
        for cand in (512, 256, 128, 64, 32, 16, 8):
            if r % cand == 0 and cand * c * 4 <= (2 << 20):
                tr = cand
                break
        spec = pl.BlockSpec((tr, c), lambda i: (i, 0))
        tiling = dict(grid=(r // tr,), in_specs=[spec] * (3 + n_g), out_specs=[spec] * 4, compiler_params=_params(1))
    else:
        tiling = dict(compiler_params=pltpu.CompilerParams(vmem_limit_bytes=48 << 20))

    def body(*refs):
        w_ref, g_refs, m_ref, v_ref = refs[0], refs[1:1 + n_g], refs[1 + n_g], refs[2 + n_g]
        g = g_refs[0][...]
        for gr in g_refs[1:]:
            g = g + gr[...]
        _adamw_update(g, w_ref, m_ref, v_ref, *refs[3 + n_g:])

    return pl.pallas_call(body, name=name, out_shape=[jax.ShapeDtypeStruct(w.shape, F32)] * 4, **tiling)(w, *g_parts, m, v)


def _adamw_update(g, w_ref, m_ref, v_ref, g_out, d_out, m_out, v_out):
    m_new = ADAM_B1 * m_ref[...] + (1.0 - ADAM_B1) * g
    v_new = ADAM_B2 * v_ref[...] + (1.0 - ADAM_B2) * (g * g)
    m_hat = m_new * (1.0 / (1.0 - ADAM_B1 ** ADAM_STEP))
    v_hat = v_new * (1.0 / (1.0 - ADAM_B2 ** ADAM_STEP))
    g_out[...] = g
    d_out[...] = -ADAM_LR * (m_hat / (jnp.sqrt(v_hat) + ADAM_EPS) + ADAM_WD * w_ref[...])
    m_out[...] = m_new
    v_out[...] = v_new


def _adamw_many(ws, gs, ms, vs, name):
    n = len(ws)

    def body(*refs):
        for k in range(n):
            _adamw_update(refs[n + k][...], refs[k], refs[2 * n + k], refs[3 * n + k],
                          refs[4 * n + k], refs[5 * n + k], refs[6 * n + k], refs[7 * n + k])

    outs = pl.pallas_call(body, name=name, out_shape=[jax.ShapeDtypeStruct(t.shape, F32) for t in ws] * 4,
                          compiler_params=pltpu.CompilerParams(vmem_limit_bytes=56 << 20))(*ws, *gs, *ms, *vs)
    return [outs[part * n:(part + 1) * n] for part in range(4)]


def _adamw_layers(w, q_mine, q_sib, m, v, name, after=None):
    n, r, c = w.shape
    tr = r
    for cand in (512, 256, 128, 64, 32, 16, 8):
        if r % cand == 0 and cand * c * 4 <= (1 << 20):
            tr = cand
            break

    def body(*refs):
        w_ref, qm, qs, m_ref, v_ref = refs[0], refs[1:1 + n], refs[1 + n:1 + 2 * n], refs[1 + 2 * n], refs[2 + 2 * n]
        layer = pl.program_id(0)
        g = qm[0][...] + qs[0][...]
        for k in range(1, n):
            g = jnp.where(layer == k, qm[k][...] + qs[k][...], g)
        _adamw_update(g, w_ref, m_ref, v_ref, *refs[3 + 2 * n:])

    stacked = pl.BlockSpec((None, tr, c), lambda l, i: (l, i, 0))
    per_layer = [pl.BlockSpec((tr, c), lambda l, i, k=k: (jnp.where(l == k, i, 0), 0)) for k in range(n)]
    call, tail = _call_after(
        after, body, 3 + 2 * n, name=name, grid=(n, r // tr),
        in_specs=[stacked] + per_layer + per_layer + [stacked, stacked], out_specs=[stacked] * 4,
        out_shape=[jax.ShapeDtypeStruct(w.shape, F32)] * 4,
        compiler_params=_params(2),
    )
    return call(w, *q_mine, *q_sib, m, v, *tail)


def _ada_fwd(c16, ada_w, ada_b_cols):
    cols = ada_w.shape[2]

    def body(c_ref, w_ref, b_ref, o_ref):
        cv = c_ref[...]
        ca = _bf(cv * jax.nn.sigmoid(cv))
        o_ref[...] = _dot(ca, _bf(w_ref[...])) + b_ref[...]

    return pl.pallas_call(
        body, name="ada_fwd", grid=(DEPTH,),
        in_specs=[_whole((16, D)), pl.BlockSpec((None, D, cols), lambda i: (i, 0, 0)),
                  pl.BlockSpec((None, 1, cols), lambda i: (i, 0, 0))],
        out_specs=pl.BlockSpec((None, 16, cols), lambda i: (i, 0, 0)),
        out_shape=jax.ShapeDtypeStruct((DEPTH, 16, cols), F32),
        compiler_params=_params(1),
    )(c16, ada_w, ada_b_cols)


def _ada_bwd(c16, dmod16):
    cols = dmod16.shape[2]

    def body(c_ref, d_ref, o_ref):
        cv = c_ref[...]
        ca = _bf(cv * jax.nn.sigmoid(cv))
        o_ref[...] = _dot_tn(ca, _bf(d_ref[...]))

    return pl.pallas_call(
        body, name="ada_bwd", grid=(DEPTH,),
        in_specs=[_whole((16, D)), pl.BlockSpec((None, 16, cols), lambda i: (i, 0, 0))],
        out_specs=pl.BlockSpec((None, D, cols), lambda i: (i, 0, 0)),
        out_shape=jax.ShapeDtypeStruct((DEPTH, D, cols), F32),
        compiler_params=_params(1),
    )(c16, dmod16)


def _mod_bwd(vs_mix, vs_ffn, pv):
    def body(m_ref, f_ref, pv_ref, o_ref):
        for i in range(DEPTH):
            vm, vf, p = m_ref[i], f_ref[i], pv_ref[i]
            o_ref[i] = jnp.concatenate([
                vm[2:3], vm[1:2] * p[R_N1:R_N1 + 1], vm[0:1],
                vf[2:3], vf[1:2] * p[R_N2:R_N2 + 1], vf[0:1],
                vm[1:2] * (1.0 + p[R_SC1:R_SC1 + 1]), vf[1:2] * (1.0 + p[R_SC2:R_SC2 + 1])], axis=0)

    return pl.pallas_call(body, name="mod_bwd", out_shape=jax.ShapeDtypeStruct((DEPTH, 8, D), F32))(vs_mix, vs_ffn, pv)


def _ffn_fwd(x1, pv, w1, w2, layer, tm, after=None):
    L = x1.shape[0]
    dff = w1.shape[2]

    def body(x1_ref, pv_ref, w1_ref, w2_ref, x2_ref, h2_ref, a_ref, f_ref):
        x1v, p = x1_ref[...], pv_ref[...]
        h2, _, _ = _norm_mod(x1v, p[R_N2:R_N2 + 1], p[R_SC2:R_SC2 + 1], p[R_SH2:R_SH2 + 1])
        hb = _bf(h2)
        h2_ref[...] = hb
        a = _dot(hb, w1_ref[...])
        a_ref[...] = a
        ra = jnp.maximum(a, 0.0)
        f = _dot(_bf(ra * ra), w2_ref[...])
        f_ref[...] = f
        x2_ref[...] = x1v + p[R_G2:R_G2 + 1] * f

    call, tail = _call_after(
        after, body, 4, name=f"ffn_fwd{layer}", grid=(L // tm,),
        in_specs=[_rows(tm, D), pl.BlockSpec((None, 8, D), lambda i: (layer, 0, 0)), _layer_w(D, dff, 0), _layer_w(dff, D, 0)],
        out_specs=[_rows(tm, D), _rows(tm, D), _rows(tm, dff), _rows(tm, D)],
        out_shape=[jax.ShapeDtypeStruct((L, D), F32), jax.ShapeDtypeStruct((L, D), BF16),
                   jax.ShapeDtypeStruct((L, dff), F32), jax.ShapeDtypeStruct((L, D), F32)],
        compiler_params=_params(1, 56),
    )
    return call(x1, pv, w1, w2, *tail)


def _ffn_bwd(dx2, x1, a, f, pv, w1, w2, layer, tm, after=None):
    L = x1.shape[0]
    dff = w1.shape[2]
    extra = [] if after is None else [pl.BlockSpec(memory_space=pl.ANY)]
    extra_args = [] if after is None else [after]

    def body(dx2_ref, x1_ref, a_ref, f_ref, pv_ref, w1_ref, w2_ref, *rest):
        dx1_ref, p_ref, da_ref, df_ref, vs_ref = rest[len(extra):]

        @pl.when(pl.program_id(0) == 0)
        def _():
            vs_ref[...] = jnp.zeros_like(vs_ref)

        dx2v, p = dx2_ref[...], pv_ref[...]
        dfb = _bf(dx2v * p[R_G2:R_G2 + 1])
        df_ref[...] = dfb
        vs_ref[0:1, :] += _sum0(dx2v * f_ref[...])
        dp = _dot_nt(dfb, w2_ref[...])
        ra = jnp.maximum(a_ref[...], 0.0)
        p_ref[...] = _bf(ra * ra)
        dab = _bf(dp * (2.0 * ra))
        da_ref[...] = dab
        dh2 = _dot_nt(dab, w1_ref[...])
        _, xn, r = _norm_mod(x1_ref[...], p[R_N2:R_N2 + 1], p[R_SC2:R_SC2 + 1], p[R_SH2:R_SH2 + 1])
        dx1_ref[...] = dx2v + _norm_mod_bwd(dh2, xn, r, p[R_N2:R_N2 + 1], p[R_SC2:R_SC2 + 1])
        vs_ref[1:2, :] += _sum0(dh2 * xn)
        vs_ref[2:3, :] += _sum0(dh2)

    return pl.pallas_call(
        body, name=f"ffn_bwd{layer}", grid=(L // tm,),
        in_specs=[_rows(tm, D), _rows(tm, D), _rows(tm, dff), _rows(tm, D),
                  pl.BlockSpec((None, 8, D), lambda i: (layer, 0, 0)), _layer_w(D, dff, 0), _layer_w(dff, D, 0)] + extra,
        out_specs=[_rows(tm, D), _rows(tm, dff), _rows(tm, dff), _rows(tm, D), _whole((8, D))],
        out_shape=[jax.ShapeDtypeStruct((L, D), F32), jax.ShapeDtypeStruct((L, dff), BF16),
                   jax.ShapeDtypeStruct((L, dff), BF16), jax.ShapeDtypeStruct((L, D), BF16),
                   jax.ShapeDtypeStruct((8, D), F32)],
        compiler_params=_params(1, 56),
    )(dx2, x1, a, f, pv, w1, w2, *extra_args)


def _conv_fwd(x, pv, w_in, w_out, cw, layer, j, tm, after=None):
    L = x.shape[0]

    def body(x_ref, pv_ref, win_ref, wout_ref, cw_ref, x1_ref, h_ref, bcx_ref, conv_ref, q_ref, y_ref, carry):
        @pl.when(pl.program_id(0) == 0)
        def _():
            carry[...] = jnp.zeros_like(carry)

        xv, p, cwv = x_ref[...], pv_ref[...], cw_ref[...]
        h, _, _ = _norm_mod(xv, p[R_N1:R_N1 + 1], p[R_SC1:R_SC1 + 1], p[R_SH1:R_SH1 + 1])
        hb = _bf(h)
        h_ref[...] = hb
        bcx = _dot(hb, win_ref[...])
        bcx_ref[...] = bcx
        z = bcx[:, D:2 * D] * bcx[:, 2 * D:]
        prev8 = carry[...]
        conv = cwv[0:1] * _shift_down(z, prev8, 2) + cwv[1:2] * _shift_down(z, prev8, 1) + cwv[2:3] * z + cwv[3:4]
        conv_ref[...] = conv
        qb = _bf(bcx[:, :D] * conv)
        q_ref[...] = qb
        y = _dot(qb, wout_ref[...])
        y_ref[...] = y
        x1_ref[...] = xv + p[R_G1:R_G1 + 1] * y
        carry[...] = z[tm - 8:tm]

    call, tail = _call_after(
        after, body, 5, name=f"conv_fwd{layer}", grid=(L // tm,),
        in_specs=[_rows(tm, D), pl.BlockSpec((None, 8, D), lambda i: (layer, 0, 0)), _layer_w(D, 3 * D, 0), _layer_w(D, D, 0),
                  pl.BlockSpec((None, 8, D), lambda i: (j, 0, 0))],
        out_specs=[_rows(tm, D), _rows(tm, D), _rows(tm, 3 * D), _rows(tm, D), _rows(tm, D), _rows(tm, D)],
        out_shape=[jax.ShapeDtypeStruct((L, D), F32), jax.ShapeDtypeStruct((L, D), BF16), jax.ShapeDtypeStruct((L, 3 * D), F32),
                   jax.ShapeDtypeStruct((L, D), F32), jax.ShapeDtypeStruct((L, D), BF16), jax.ShapeDtypeStruct((L, D), F32)],
        scratch_shapes=[pltpu.VMEM((8, D), F32)],
        compiler_params=_params(1, 56),
    )
    return call(x, pv, w_in, w_out, cw, *tail)


def _conv_bwd(dx1, x, y, bcx, conv, pv, w_in, w_out, cw, layer, j, tm, after=None):
    L = x.shape[0]
    nt = L // tm

    def body(dx1_ref, x_ref, y_ref, bcx_ref, conv_ref, halo_ref, pv_ref, win_ref, wout_ref, cw_ref,
             dx_ref, dbcx_ref, dy_ref, vs_ref, carry):
        gi = pl.program_id(0)
        tile = nt - 1 - gi

        @pl.when(gi == 0)
        def _():
            vs_ref[...] = jnp.zeros_like(vs_ref)
            carry[...] = jnp.zeros_like(carry)

        dx1v, p, cwv = dx1_ref[...], pv_ref[...], cw_ref[...]
        dyb = _bf(dx1v * p[R_G1:R_G1 + 1])
        dy_ref[...] = dyb
        vs_ref[0:1, :] += _sum0(dx1v * y_ref[...])
        dq = _dot_nt(dyb, wout_ref[...])
        bcx = bcx_ref[...]
        b, cg, xh = bcx[:, :D], bcx[:, D:2 * D], bcx[:, 2 * D:]
        db = dq * conv_ref[...]
        dc = dq * b
        z = cg * xh
        halo = halo_ref[...]
        zprev = jnp.where(tile > 0, halo[:, D:2 * D] * halo[:, 2 * D:], 0.0)
        vs_ref[3:4, :] += _sum0(dc * _shift_down(z, zprev, 2))
        vs_ref[4:5, :] += _sum0(dc * _shift_down(z, zprev, 1))
        vs_ref[5:6, :] += _sum0(dc * z)
        vs_ref[6:7, :] += _sum0(dc)
        next8 = carry[...]
        dz = cwv[2:3] * dc + cwv[1:2] * _shift_up(dc, next8, 1) + cwv[0:1] * _shift_up(dc, next8, 2)
        dbb, dcgb, dxhb = _bf(db), _bf(dz * xh), _bf(dz * cg)
        dbcx_ref[:, 0:D] = dbb
        dbcx_ref[:, D:2 * D] = dcgb
        dbcx_ref[:, 2 * D:3 * D] = dxhb
        dh = (_dot_nt(dbb, win_ref[:, 0:D]) + _dot_nt(dcgb, win_ref[:, D:2 * D])) + _dot_nt(dxhb, win_ref[:, 2 * D:3 * D])
        _, xn, r = _norm_mod(x_ref[...], p[R_N1:R_N1 + 1], p[R_SC1:R_SC1 + 1], p[R_SH1:R_SH1 + 1])
        dx_ref[...] = dx1v + _norm_mod_bwd(dh, xn, r, p[R_N1:R_N1 + 1], p[R_SC1:R_SC1 + 1])
        vs_ref[1:2, :] += _sum0(dh * xn)
        vs_ref[2:3, :] += _sum0(dh)
        carry[...] = dc[0:8]

    halo_spec = pl.BlockSpec((8, 3 * D), lambda i: (jnp.maximum((nt - 1 - i) * (tm // 8) - 1, 0), 0))
    call, tail = _call_after(
        after, body, 10, name=f"conv_bwd{layer}", grid=(nt,),
        in_specs=[_rows(tm, D, nt), _rows(tm, D, nt), _rows(tm, D, nt), _rows(tm, 3 * D, nt), _rows(tm, D, nt), halo_spec,
                  pl.BlockSpec((None, 8, D), lambda i: (layer, 0, 0)), _layer_w(D, 3 * D, 0), _layer_w(D, D, 0),
                  pl.BlockSpec((None, 8, D), lambda i: (j, 0, 0))],
        out_specs=[_rows(tm, D, nt), _rows(tm, 3 * D, nt), _rows(tm, D, nt), _whole((8, D))],
        out_shape=[jax.ShapeDtypeStruct((L, D), F32), jax.ShapeDtypeStruct((L, 3 * D), BF16),
                   jax.ShapeDtypeStruct((L, D), BF16), jax.ShapeDtypeStruct((8, D), F32)],
        scratch_shapes=[pltpu.VMEM((8, D), F32)],
        compiler_params=_params(1, 56),
    )
    return call(dx1, x, y, bcx, conv, bcx, pv, w_in, w_out, cw, *tail)


def _s5_discretize(a_re, a_im, log_dt, bt_re, bt_im):
    dt = jnp.exp(log_dt)
    mag = jnp.exp(a_re * dt)
    abar_re = mag * jnp.cos(a_im * dt)
    abar_im = mag * jnp.sin(a_im * dt)
    den = a_re * a_re + a_im * a_im
    nr = abar_re - 1.0
    ni = abar_im
    f_re = (nr * a_re + ni * a_im) / den
    f_im = (ni * a_re - nr * a_im) / den
    bbar_re = f_re * bt_re - f_im * bt_im
    bbar_im = f_re * bt_im + f_im * bt_re
    return abar_re, abar_im, bbar_re, bbar_im


def _s5_params_fwd(a_re, a_im, log_dt, bt_re, bt_im, after=None):
    def body(ar, ai, ld, br, bi, o_ar, o_ai, o_br, o_bi):
        r = _s5_discretize(ar[...], ai[...], ld[...], br[...], bi[...])
        o_ar[...], o_ai[...], o_br[...], o_bi[...] = r

    gp = jax.ShapeDtypeStruct((S5_G, S5_P), F32)
    hgp = jax.ShapeDtypeStruct((S5_H, S5_G, S5_P), F32)
    call, tail = _call_after(after, body, 5, name="s5_params_fwd", out_shape=[gp, gp, hgp, hgp],
                             in_specs=[pl.BlockSpec(memory_space=pltpu.VMEM)] * 5)
    return call(a_re, a_im, log_dt, bt_re, bt_im, *tail)


def _s5_params_bwd(a_re, a_im, log_dt, bt_re, bt_im, d_ar, d_ai, d_br, d_bi):
    def body(ar, ai, ld, br, bi, gar, gai, gbr, gbi, o_ar, o_ai, o_ld, o_br, o_bi):
        _, vjp = jax.vjp(_s5_discretize, ar[...], ai[...], ld[...], br[...], bi[...])
        r = vjp((gar[...], gai[...], gbr[...], gbi[...]))
        o_ar[...], o_ai[...], o_ld[...], o_br[...], o_bi[...] = r

    gp = jax.ShapeDtypeStruct((S5_G, S5_P), F32)
    hgp = jax.ShapeDtypeStruct((S5_H, S5_G, S5_P), F32)
    return pl.pallas_call(body, name="s5_params_bwd", out_shape=[gp, gp, jax.ShapeDtypeStruct((S5_G, 1), F32), hgp, hgp])(
        a_re, a_im, log_dt, bt_re, bt_im, d_ar, d_ai, d_br, d_bi)


NSEG = 8
SCAN_LANES = 1024


def _to_segments(x):
    n, c = x.shape
    return x.reshape(NSEG, n // NSEG, c).transpose(1, 0, 2).reshape(n, c)


def _from_segments(x):
    n, c = x.shape
    return x.reshape(n // NSEG, NSEG, c).transpose(1, 0, 2).reshape(n, c)


def _segment_scan(re_ref, im_ref, st_re, st_im, a_re, a_im, n_slabs, adjoint, write):
    for q in range(NSTATE // SCAN_LANES):
        ls = slice(q * SCAN_LANES, (q + 1) * SCAN_LANES)
        ar = jnp.broadcast_to(a_re[:, ls], (8, SCAN_LANES))
        ai = jnp.broadcast_to(a_im[:, ls], (8, SCAN_LANES))

        def step(k, carry, ls=ls, ar=ar, ai=ai):
            s_r, s_i = carry
            slab = (n_slabs - 1 - k) if adjoint else k
            rows = pl.ds(pl.multiple_of(slab * 8, 8), 8)
            b_r, b_i = re_ref[rows, ls], im_ref[rows, ls]
            if adjoint:
                n_r = b_r + ar * s_r + ai * s_i
                n_i = b_i - ai * s_r + ar * s_i
            else:
                n_r = ar * s_r - ai * s_i + b_r
                n_i = ar * s_i + ai * s_r + b_i
            if write:
                re_ref[rows, ls] = n_r
                im_ref[rows, ls] = n_i
            return n_r, n_i

        s_r, s_i = lax.fori_loop(0, n_slabs, step, (st_re[:, ls], st_im[:, ls]), unroll=4)
        st_re[:, ls] = s_r
        st_im[:, ls] = s_i


def _s5_segment_states(e_re, e_im, ar, ai, seg_len, adjoint):
    def body(ere_ref, eim_ref, ar_ref, ai_ref, ore_ref, oim_ref):
        p_r, p_i = ar_ref[...], ai_ref[...]
        if adjoint:
            p_i = -p_i
        acc_r, acc_i = jnp.ones_like(p_r), jnp.zeros_like(p_r)
        n = seg_len
        while n:
            if n & 1:
                acc_r, acc_i = acc_r * p_r - acc_i * p_i, acc_r * p_i + acc_i * p_r
            n >>= 1
            if n:
                p_r, p_i = p_r * p_r - p_i * p_i, 2.0 * p_r * p_i
        e_r, e_i = ere_ref[...], eim_ref[...]
        s_r, s_i = jnp.zeros_like(acc_r), jnp.zeros_like(acc_r)
        rows_r, rows_i = [None] * NSEG, [None] * NSEG
        order = range(NSEG - 1, -1, -1) if adjoint else range(NSEG)
        for j in order:
            rows_r[j], rows_i[j] = s_r, s_i
            s_r, s_i = (acc_r * s_r - acc_i * s_i + e_r[j:j + 1], acc_r * s_i + acc_i * s_r + e_i[j:j + 1])
        ore_ref[...] = jnp.concatenate(rows_r, axis=0)
        oim_ref[...] = jnp.concatenate(rows_i, axis=0)

    st = jax.ShapeDtypeStruct((NSEG, NSTATE), F32)
    return pl.pallas_call(body, name="s5_segment_states_bwd" if adjoint else "s5_segment_states_fwd", out_shape=[st, st])(
        e_re, e_im, ar, ai)


def _s5_fwd_ends(x, pv, w_in, b_re, b_im, ar, ai, layer, tm, after=None):
    L = x.shape[0]

    def body(x_ref, pv_ref, win_ref, bre_ref, bim_ref, ar_ref, ai_ref, h_ref, u_ref, ere_ref, eim_ref, ub_ref, bu_re, bu_im):
        @pl.when(pl.program_id(0) == 0)
        def _():
            ere_ref[...] = jnp.zeros_like(ere_ref)
            eim_ref[...] = jnp.zeros_like(eim_ref)

        p = pv_ref[...]
        h, _, _ = _norm_mod(x_ref[...], p[R_N1:R_N1 + 1], p[R_SC1:R_SC1 + 1], p[R_SH1:R_SH1 + 1])
        hb = _bf(h)
        h_ref[...] = hb
        u = _dot(hb, win_ref[...])
        u_ref[...] = u
        ub = _bf(u)
        ub_ref[...] = ub
        for k in range(S5_NB):
            uk = ub[:, k * S5_BH:(k + 1) * S5_BH]
            bu_re[:, k * S5_BP:(k + 1) * S5_BP] = _dot(uk, bre_ref[k])
            bu_im[:, k * S5_BP:(k + 1) * S5_BP] = _dot(uk, bim_ref[k])
        _segment_scan(bu_re, bu_im, ere_ref, eim_ref, ar_ref[...], ai_ref[...], tm // 8, adjoint=False, write=False)

    call, tail = _call_after(
        after, body, 7, name="s5_fwd_ends", grid=(L // tm,),
        in_specs=[_rows(tm, D), pl.BlockSpec((None, 8, D), lambda i: (layer, 0, 0)), _layer_w(D, D, 0),
                  _const_w((S5_NB, S5_BH, S5_BP)), _const_w((S5_NB, S5_BH, S5_BP)), _whole((1, NSTATE)), _whole((1, NSTATE))],
        out_specs=[_rows(tm, D), _rows(tm, D), _whole((NSEG, NSTATE)), _whole((NSEG, NSTATE)), _rows(tm, D)],
        out_shape=[jax.ShapeDtypeStruct((L, D), BF16), jax.ShapeDtypeStruct((L, D), F32),
                   jax.ShapeDtypeStruct((NSEG, NSTATE), F32), jax.ShapeDtypeStruct((NSEG, NSTATE), F32),
                   jax.ShapeDtypeStruct((L, D), BF16)],
        scratch_shapes=[pltpu.VMEM((tm, NSTATE), F32), pltpu.VMEM((tm, NSTATE), F32)],
        compiler_params=_params(1, 56),
    )
    return call(x, pv, w_in, b_re, b_im, ar, ai, *tail)


def _s5_fwd_out(x, u, pv, b_re, b_im, s0_re, s0_im, ar, ai, c_re, c_im, dvec, glu_w, glu_b, w_out, layer, tm):
    L = x.shape[0]

    def body(x_ref, u_ref, pv_ref, bre_ref, bim_ref, s0re_ref, s0im_ref, ar_ref, ai_ref, cre_ref, cim_ref, d_ref, gw_ref,
             gb_ref, wout_ref, x1_ref, sre_ref, sim_ref, y1_ref, zg_ref, y3_ref, y_ref, srb_ref, sib_ref, st_re, st_im):
        @pl.when(pl.program_id(0) == 0)
        def _():
            st_re[...] = s0re_ref[...]
            st_im[...] = s0im_ref[...]

        p = pv_ref[...]
        uv = u_ref[...]
        ub = _bf(uv)
        for k in range(S5_NB):
            uk = ub[:, k * S5_BH:(k + 1) * S5_BH]
            sre_ref[:, k * S5_BP:(k + 1) * S5_BP] = _dot(uk, bre_ref[k])
            sim_ref[:, k * S5_BP:(k + 1) * S5_BP] = _dot(uk, bim_ref[k])
        _segment_scan(sre_ref, sim_ref, st_re, st_im, ar_ref[...], ai_ref[...], tm // 8, adjoint=False, write=True)
        parts = []
        for k in range(S5_NB):
            sl = slice(k * S5_BP, (k + 1) * S5_BP)
            srb, sib = _bf(sre_ref[:, sl]), _bf(sim_ref[:, sl])
            srb_ref[:, sl] = srb
            sib_ref[:, sl] = sib
            parts.append(_dot(srb, cre_ref[k]) - _dot(sib, cim_ref[k]))
        y1 = jnp.concatenate(parts, axis=1) + d_ref[...] * uv
        y1_ref[...] = y1
        y2 = jax.nn.gelu(y1)
        zg = _dot(_bf(y2), gw_ref[...]) + gb_ref[...]
        zg_ref[...] = zg
        y3b = _bf(y2 * jax.nn.sigmoid(zg))
        y3_ref[...] = y3b
        y = _dot(y3b, wout_ref[...])
        y_ref[...] = y
        x1_ref[...] = x_ref[...] + p[R_G1:R_G1 + 1] * y

    return pl.pallas_call(
        body, name="s5_fwd_out", grid=(L // tm,),
        in_specs=[_rows(tm, D), _rows(tm, D), pl.BlockSpec((None, 8, D), lambda i: (layer, 0, 0)),
                  _const_w((S5_NB, S5_BH, S5_BP)), _const_w((S5_NB, S5_BH, S5_BP)),
                  _whole((NSEG, NSTATE)), _whole((NSEG, NSTATE)), _whole((1, NSTATE)), _whole((1, NSTATE)),
                  _const_w((S5_NB, S5_BP, S5_BH)), _const_w((S5_NB, S5_BP, S5_BH)), _whole((1, D)),
                  _layer_w(D, D, 0), _whole((1, D)), _layer_w(D, D, 0)],
        out_specs=[_rows(tm, D), _rows(tm, NSTATE), _rows(tm, NSTATE), _rows(tm, D), _rows(tm, D), _rows(tm, D), _rows(tm, D),
                   _rows(tm, NSTATE), _rows(tm, NSTATE)],
        out_shape=[jax.ShapeDtypeStruct((L, D), F32), jax.ShapeDtypeStruct((L, NSTATE), F32), jax.ShapeDtypeStruct((L, NSTATE), F32),
                   jax.ShapeDtypeStruct((L, D), F32), jax.ShapeDtypeStruct((L, D), F32),
                   jax.ShapeDtypeStruct((L, D), BF16), jax.ShapeDtypeStruct((L, D), F32),
                   jax.ShapeDtypeStruct((L, NSTATE), BF16), jax.ShapeDtypeStruct((L, NSTATE), BF16)],
        scratch_shapes=[pltpu.VMEM((NSEG, NSTATE), F32), pltpu.VMEM((NSEG, NSTATE), F32)],
        compiler_params=_params(1, 60),
    )(x, u, pv, b_re, b_im, s0_re, s0_im, ar, ai, c_re, c_im, dvec, glu_w, glu_b, w_out)


def _s5_bwd_ends(dx1, y, y1, zg, u, pv, c_re, c_im, ar, ai, dvec, glu_w, w_out, layer, tm, after=None):
    L = dx1.shape[0]
    nt = L // tm

    def body(dx1_ref, y_ref, y1_ref, zg_ref, u_ref, pv_ref, cre_ref, cim_ref, ar_ref, ai_ref, d_ref, gw_ref, wout_ref,
             dy_ref, y2_ref, dzg_ref, dy1_ref, dus_ref, ere_ref, eim_ref, vs_ref, g_re, g_im):
        @pl.when(pl.program_id(0) == 0)
        def _():
            vs_ref[...] = jnp.zeros_like(vs_ref)
            ere_ref[...] = jnp.zeros_like(ere_ref)
            eim_ref[...] = jnp.zeros_like(eim_ref)

        dx1v, p = dx1_ref[...], pv_ref[...]
        dyb = _bf(dx1v * p[R_G1:R_G1 + 1])
        dy_ref[...] = dyb
        vs_ref[0:1, :] += _sum0(dx1v * y_ref[...])
        dy3 = _dot_nt(dyb, wout_ref[...])
        y2, gelu_vjp = jax.vjp(jax.nn.gelu, y1_ref[...])
        y2_ref[...] = _bf(y2)
        gate = jax.nn.sigmoid(zg_ref[...])
        dzg = dy3 * y2 * gate * (1.0 - gate)
        dzgb = _bf(dzg)
        dzg_ref[...] = dzgb
        vs_ref[1:2, :] += _sum0(dzg)
        dy2 = dy3 * gate + _dot_nt(dzgb, gw_ref[...])
        dy1 = gelu_vjp(dy2)[0]
        vs_ref[2:3, :] += _sum0(dy1 * u_ref[...])
        dus_ref[...] = dy1 * d_ref[...]
        dy1b = _bf(dy1)
        dy1_ref[...] = dy1b
        for k in range(S5_NB):
            dk = dy1b[:, k * S5_BH:(k + 1) * S5_BH]
            g_re[:, k * S5_BP:(k + 1) * S5_BP] = _dot_nt(dk, cre_ref[k])
            g_im[:, k * S5_BP:(k + 1) * S5_BP] = -_dot_nt(dk, cim_ref[k])
        _segment_scan(g_re, g_im, ere_ref, eim_ref, ar_ref[...], ai_ref[...], tm // 8, adjoint=True, write=False)

    call, tail = _call_after(
        after, body, 13, name="s5_bwd_ends", grid=(nt,),
        in_specs=[_rows(tm, D, nt)] * 5 + [pl.BlockSpec((None, 8, D), lambda i: (layer, 0, 0)),
                  _const_w((S5_NB, S5_BP, S5_BH)), _const_w((S5_NB, S5_BP, S5_BH)), _whole((1, NSTATE)), _whole((1, NSTATE)),
                  _whole((1, D)), _layer_w(D, D, 0), _layer_w(D, D, 0)],
        out_specs=[_rows(tm, D, nt)] * 5 + [_whole((NSEG, NSTATE)), _whole((NSEG, NSTATE)), _whole((8, D))],
        out_shape=[jax.ShapeDtypeStruct((L, D), BF16)] * 4 + [jax.ShapeDtypeStruct((L, D), F32),
                   jax.ShapeDtypeStruct((NSEG, NSTATE), F32), jax.ShapeDtypeStruct((NSEG, NSTATE), F32),
                   jax.ShapeDtypeStruct((8, D), F32)],
        scratch_shapes=[pltpu.VMEM((tm, NSTATE), F32), pltpu.VMEM((tm, NSTATE), F32)],
        compiler_params=_params(1, 56),
    )
    return call(dx1, y, y1, zg, u, pv, c_re, c_im, ar, ai, dvec, glu_w, w_out, *tail)


def _s5_bwd_in(dx1, dy1_b, du_skip, x, s_re, s_im, pv, b_re, b_im, c_re, c_im, l0_re, l0_im, ar, ai, w_in, layer, tm):
    L = x.shape[0]
    nt = L // tm

    def body(dx1_ref, dy1_ref, dus_ref, x_ref, sre_ref, sim_ref, hre_ref, him_ref, lre_ref, lim_ref, pv_ref, bre_ref, bim_ref,
             cre_ref, cim_ref, l0re_ref, l0im_ref, ar_ref, ai_ref, win_ref,
             dx_ref, du_ref, lamre_ref, lamim_ref, da_ref, vs_ref, g_re, g_im, st_re, st_im):
        gi = pl.program_id(0)
        tile = nt - 1 - gi

        @pl.when(gi == 0)
        def _():
            vs_ref[...] = jnp.zeros_like(vs_ref)
            da_ref[...] = jnp.zeros_like(da_ref)
            st_re[...] = l0re_ref[...]
            st_im[...] = l0im_ref[...]

        p = pv_ref[...]
        dy1b = dy1_ref[...]
        for k in range(S5_NB):
            dk = dy1b[:, k * S5_BH:(k + 1) * S5_BH]
            g_re[:, k * S5_BP:(k + 1) * S5_BP] = _dot_nt(dk, cre_ref[k])
            g_im[:, k * S5_BP:(k + 1) * S5_BP] = -_dot_nt(dk, cim_ref[k])
        _segment_scan(g_re, g_im, st_re, st_im, ar_ref[...], ai_ref[...], tm // 8, adjoint=True, write=True)
        lam_r, lam_i = g_re[...], g_im[...]
        lrb, lib = _bf(lam_r), _bf(lam_i)
        lamre_ref[...] = lrb
        lamim_ref[...] = lib

        def wrapped(last_ref):
            z = last_ref[...]
            row = lax.broadcasted_iota(jnp.int32, z.shape, 0)
            return jnp.where(row >= 1, pltpu.roll(z, 1, 0), 0.0)

        first_r = jnp.where(tile > 0, hre_ref[...], wrapped(lre_ref))
        first_i = jnp.where(tile > 0, him_ref[...], wrapped(lim_ref))
        sp_r = jnp.concatenate([first_r, sre_ref[0:tm - 8, :]], axis=0)
        sp_i = jnp.concatenate([first_i, sim_ref[0:tm - 8, :]], axis=0)
        da_ref[0:1, :] += _sum0(lam_r * sp_r + lam_i * sp_i)
        da_ref[1:2, :] += _sum0(lam_i * sp_r - lam_r * sp_i)

        parts = []
        for k in range(S5_NB):
            sl = slice(k * S5_BP, (k + 1) * S5_BP)
            parts.append(_dot_nt(lrb[:, sl], bre_ref[k]) + _dot_nt(lib[:, sl], bim_ref[k]))
        dub = _bf(jnp.concatenate(parts, axis=1) + dus_ref[...])
        du_ref[...] = dub
        dh = _dot_nt(dub, win_ref[...])
        _, xn, r = _norm_mod(x_ref[...], p[R_N1:R_N1 + 1], p[R_SC1:R_SC1 + 1], p[R_SH1:R_SH1 + 1])
        dx_ref[...] = dx1_ref[...] + _norm_mod_bwd(dh, xn, r, p[R_N1:R_N1 + 1], p[R_SC1:R_SC1 + 1])
        vs_ref[1:2, :] += _sum0(dh * xn)
        vs_ref[2:3, :] += _sum0(dh)

    halo = pl.BlockSpec((8, NSTATE), lambda i: (jnp.maximum((nt - 1 - i) * (tm // 8) - 1, 0), 0))
    last = pl.BlockSpec((8, NSTATE), lambda i: (L // 8 - 1, 0))
    return pl.pallas_call(
        body, name="s5_bwd_in", grid=(nt,),
        in_specs=[_rows(tm, D, nt), _rows(tm, D, nt), _rows(tm, D, nt), _rows(tm, D, nt), _rows(tm, NSTATE, nt), _rows(tm, NSTATE, nt),
                  halo, halo, last, last, pl.BlockSpec((None, 8, D), lambda i: (layer, 0, 0)),
                  _const_w((S5_NB, S5_BH, S5_BP)), _const_w((S5_NB, S5_BH, S5_BP)),
                  _const_w((S5_NB, S5_BP, S5_BH)), _const_w((S5_NB, S5_BP, S5_BH)),
                  _whole((NSEG, NSTATE)), _whole((NSEG, NSTATE)), _whole((1, NSTATE)), _whole((1, NSTATE)), _layer_w(D, D, 0)],
        out_specs=[_rows(tm, D, nt), _rows(tm, D, nt), _rows(tm, NSTATE, nt), _rows(tm, NSTATE, nt), _whole((8, NSTATE)), _whole((8, D))],
        out_shape=[jax.ShapeDtypeStruct((L, D), F32), jax.ShapeDtypeStruct((L, D), BF16),
                   jax.ShapeDtypeStruct((L, NSTATE), BF16), jax.ShapeDtypeStruct((L, NSTATE), BF16),
                   jax.ShapeDtypeStruct((8, NSTATE), F32), jax.ShapeDtypeStruct((8, D), F32)],
        scratch_shapes=[pltpu.VMEM((tm, NSTATE), F32), pltpu.VMEM((tm, NSTATE), F32),
                        pltpu.VMEM((NSEG, NSTATE), F32), pltpu.VMEM((NSEG, NSTATE), F32)],
        compiler_params=_params(1, 60),
    )(dx1, dy1_b, du_skip, x, s_re, s_im, s_re, s_im, s_re, s_im, pv, b_re, b_im, c_re, c_im, l0_re, l0_im, ar, ai, w_in)


def _blockdiag_b(bt):
    b = bt.reshape(S5_H, S5_NB, 16, S5_P).transpose(1, 2, 0, 3)
    eye = jnp.eye(16, dtype=bt.dtype)
    return (b[:, :, :, None, :] * eye[None, :, None, :, None]).reshape(S5_NB, S5_BH, S5_BP)


def _unblock_b(d):
    d = jnp.einsum("bghgp->bghp", d.reshape(S5_NB, 16, S5_H, 16, S5_P))
    return d.transpose(2, 0, 1, 3).reshape(S5_H, S5_G, S5_P)


def _blockdiag_c(cm):
    c4 = cm.reshape(S5_NB, 16, S5_H, S5_P)
    eye = jnp.eye(16, dtype=cm.dtype)
    out = c4.transpose(0, 1, 3, 2)[:, :, :, None, :] * eye[None, :, None, :, None]
    return out.reshape(S5_NB, S5_BP, S5_BH)


def _unblock_c(d):
    d = jnp.einsum("bgpgh->bghp", d.reshape(S5_NB, 16, S5_P, 16, S5_H))
    return d.reshape(S5_G, S5_H, S5_P)


def _tril_mask():
    return lax.broadcasted_iota(jnp.int32, (SG_CHUNK, SG_CHUNK), 0) >= lax.broadcasted_iota(jnp.int32, (SG_CHUNK, SG_CHUNK), 1)


def _sg_fwd(x, pv, w_in, w_s, b_t, vg, w_out, layer, tm, after=None):
    L = x.shape[0]
    nc = tm // SG_CHUNK

    def body(x_ref, pv_ref, win_ref, ws_ref, bt_ref, vg_ref, wout_ref, x1_ref, h_ref, uv_ref, vm_ref, q_ref, y_ref):
        xv, p = x_ref[...], pv_ref[...]
        h, _, _ = _norm_mod(xv, p[R_N1:R_N1 + 1], p[R_SC1:R_SC1 + 1], p[R_SH1:R_SH1 + 1])
        hb = _bf(h)
        h_ref[...] = hb
        uv = _dot(hb, win_ref[...])
        uv_ref[...] = uv
        v = uv[:, D:]
        rv = lax.rsqrt(jnp.mean(v * v, axis=-1, keepdims=True) + EPS)
        vnb = _bf((v * rv) * vg_ref[...])
        mask = _tril_mask()
        bt = bt_ref[...]
        for hd in range(SG_HEADS):
            wm = _bf(jnp.where(mask, ws_ref[hd], 0.0))
            cs = slice(hd * SG_CHUNK, (hd + 1) * SG_CHUNK)
            for ck in range(nc):
                rs = slice(ck * SG_CHUNK, (ck + 1) * SG_CHUNK)
                vm_ref[rs, cs] = _dot(wm, vnb[rs, cs]) + bt[:, hd:hd + 1]
        qb = _bf(uv[:, :D] * vm_ref[...])
        q_ref[...] = qb
        y = _dot(qb, wout_ref[...])
        y_ref[...] = y
        x1_ref[...] = xv + p[R_G1:R_G1 + 1] * y

    call, tail = _call_after(
        after, body, 7, name="sg_fwd", grid=(L // tm,),
        in_specs=[_rows(tm, D), pl.BlockSpec((None, 8, D), lambda i: (layer, 0, 0)), _layer_w(D, 2 * D, 0),
                  _whole((SG_HEADS, SG_CHUNK, SG_CHUNK)), _whole((SG_CHUNK, SG_HEADS)), _whole((1, D)), _layer_w(D, D, 0)],
        out_specs=[_rows(tm, D), _rows(tm, D), _rows(tm, 2 * D), _rows(tm, D), _rows(tm, D), _rows(tm, D)],
        out_shape=[jax.ShapeDtypeStruct((L, D), F32), jax.ShapeDtypeStruct((L, D), BF16), jax.ShapeDtypeStruct((L, 2 * D), F32),
                   jax.ShapeDtypeStruct((L, D), F32), jax.ShapeDtypeStruct((L, D), BF16), jax.ShapeDtypeStruct((L, D), F32)],
        compiler_params=_params(1, 56),
    )
    return call(x, pv, w_in, w_s, b_t, vg, w_out, *tail)


def _sg_bwd(dx1, x, y, uv, vm, pv, w_in, w_s, vg, w_out, layer, tm, after=None):
    L = x.shape[0]
    nc = tm // SG_CHUNK

    def body(dx1_ref, x_ref, y_ref, uv_ref, vm_ref, pv_ref, win_ref, ws_ref, vg_ref, wout_ref,
             dx_ref, duv_ref, dy_ref, vs_ref, dws_ref, dbt_ref, dvn_scr):
        @pl.when(pl.program_id(0) == 0)
        def _():
            vs_ref[...] = jnp.zeros_like(vs_ref)
            dws_ref[...] = jnp.zeros_like(dws_ref)
            dbt_ref[...] = jnp.zeros_like(dbt_ref)

        dx1v, p = dx1_ref[...], pv_ref[...]
        dyb = _bf(dx1v * p[R_G1:R_G1 + 1])
        dy_ref[...] = dyb
        vs_ref[0:1, :] += _sum0(dx1v * y_ref[...])
        dq = _dot_nt(dyb, wout_ref[...])
        uv = uv_ref[...]
        u, v = uv[:, :D], uv[:, D:]
        dub = _bf(dq * vm_ref[...])
        dvm = dq * u
        dvmb = _bf(dvm)
        rv = lax.rsqrt(jnp.mean(v * v, axis=-1, keepdims=True) + EPS)
        vh = v * rv
        vgv = vg_ref[...]
        vnb = _bf(vh * vgv)
        mask = _tril_mask()
        for hd in range(SG_HEADS):
            wm = _bf(jnp.where(mask, ws_ref[hd], 0.0))
            cs = slice(hd * SG_CHUNK, (hd + 1) * SG_CHUNK)
            dws = jnp.zeros((SG_CHUNK, SG_CHUNK), F32)
            dbs = jnp.zeros((SG_CHUNK, 1), F32)
            for ck in range(nc):
                rs = slice(ck * SG_CHUNK, (ck + 1) * SG_CHUNK)
                dvn_scr[rs, cs] = _dot_tn(wm, dvmb[rs, cs])
                dws = dws + _dot_nt(dvmb[rs, cs], vnb[rs, cs])
                dbs = dbs + jnp.sum(dvm[rs, cs], axis=1, keepdims=True)
            dws_ref[hd] += jnp.where(mask, dws, 0.0)
            dbt_ref[:, hd:hd + 1] += dbs
        dvn = dvn_scr[...]
        vs_ref[3:4, :] += _sum0(dvn * vh)
        dvnn = dvn * vgv
        dvb = _bf(rv * (dvnn - vh * jnp.mean(dvnn * vh, axis=-1, keepdims=True)))
        duv_ref[:, 0:D] = dub
        duv_ref[:, D:2 * D] = dvb
        dh = _dot_nt(dub, win_ref[:, 0:D]) + _dot_nt(dvb, win_ref[:, D:2 * D])
        _, xn, r = _norm_mod(x_ref[...], p[R_N1:R_N1 + 1], p[R_SC1:R_SC1 + 1], p[R_SH1:R_SH1 + 1])
        dx_ref[...] = dx1v + _norm_mod_bwd(dh, xn, r, p[R_N1:R_N1 + 1], p[R_SC1:R_SC1 + 1])
        vs_ref[1:2, :] += _sum0(dh * xn)
        vs_ref[2:3, :] += _sum0(dh)

    call, tail = _call_after(
        after, body, 10, name="sg_bwd", grid=(L // tm,),
        in_specs=[_rows(tm, D), _rows(tm, D), _rows(tm, D), _rows(tm, 2 * D), _rows(tm, D),
                  pl.BlockSpec((None, 8, D), lambda i: (layer, 0, 0)), _layer_w(D, 2 * D, 0),
                  _whole((SG_HEADS, SG_CHUNK, SG_CHUNK)), _whole((1, D)), _layer_w(D, D, 0)],
        out_specs=[_rows(tm, D), _rows(tm, 2 * D), _rows(tm, D), _whole((8, D)),
                   _whole((SG_HEADS, SG_CHUNK, SG_CHUNK)), _whole((SG_CHUNK, SG_HEADS))],
        out_shape=[jax.ShapeDtypeStruct((L, D), F32), jax.ShapeDtypeStruct((L, 2 * D), BF16), jax.ShapeDtypeStruct((L, D), BF16),
                   jax.ShapeDtypeStruct((8, D), F32), jax.ShapeDtypeStruct((SG_HEADS, SG_CHUNK, SG_CHUNK), F32),
                   jax.ShapeDtypeStruct((SG_CHUNK, SG_HEADS), F32)],
        scratch_shapes=[pltpu.VMEM((tm, D), F32)],
        compiler_params=_params(1, 56),
    )
    return call(dx1, x, y, uv, vm, pv, w_in, w_s, vg, w_out, *tail)


def _final(x, target, fg, tm):
    L = x.shape[0]

    def body(x_ref, t_ref, g_ref, dx_ref, vs_ref):
        @pl.when(pl.program_id(0) == 0)
        def _():
            vs_ref[...] = jnp.zeros_like(vs_ref)

        xv, g = x_ref[...], g_ref[...]
        r = lax.rsqrt(jnp.mean(xv * xv, axis=-1, keepdims=True) + EPS)
        xn = xv * r
        e = xn * g - t_ref[...]
        vs_ref[0:1, :] += jnp.sum(e * e)
        dout = e * (1.0 / D)
        vs_ref[1:2, :] += _sum0(dout * xn)
        dxn = dout * g
        dx_ref[...] = r * (dxn - xn * jnp.mean(dxn * xn, axis=-1, keepdims=True))

    return pl.pallas_call(
        body, name="final_loss", grid=(L // tm,),
        in_specs=[_rows(tm, D), _rows(tm, D), _whole((1, D))],
        out_specs=[_rows(tm, D), _whole((8, D))],
        out_shape=[jax.ShapeDtypeStruct((L, D), F32), jax.ShapeDtypeStruct((8, D), F32)],
        compiler_params=_params(1),
    )(x, target, fg)


def _pack_flat(arrs, multiple=LANES):
    flat = jnp.concatenate([a.reshape(-1).astype(F32) for a in arrs])
    return jnp.pad(flat, (0, -flat.shape[0] % multiple))


def _pack(arrs, row_multiple=8):
    return _pack_flat(arrs, row_multiple * LANES).reshape(-1, LANES)


def _unpack(buf, shapes, lead=()):
    flat = buf.reshape(lead + (-1,))
    out, off = [], 0
    for s in shapes:
        n = 1
        for d in s:
            n *= d
        out.append(flat[..., off:off + n].reshape(lead + tuple(s)))
        off += n
    return out


BIG = ("ff_w1", "ff_w2", "conv_w_in", "conv_w_out", "ssm_w_in", "ssm_glu_w", "ssm_w_out", "sg_w_in", "sg_w_out")
BIG_AXIS = {"ff_w1": 2, "ff_w2": 1, "conv_w_in": 2, "conv_w_out": 1, "ssm_w_in": 1, "ssm_glu_w": 1, "ssm_w_out": 1,
            "sg_w_in": 2, "sg_w_out": 1}
LAYER_WEIGHTS = (
    (("conv_w_in", 0), ("conv_w_out", 0), ("ff_w1", 0), ("ff_w2", 0)),
    (("ssm_w_in", 0), ("ssm_glu_w", 0), ("ssm_w_out", 0), ("ff_w1", 1), ("ff_w2", 1)),
    (("sg_w_in", 0), ("sg_w_out", 0), ("ff_w1", 2), ("ff_w2", 2)),
    (("conv_w_in", 1), ("conv_w_out", 1), ("ff_w1", 3), ("ff_w2", 3)),
)
GATHER_GROUPS = tuple(grp for lw in LAYER_WEIGHTS for grp in (lw[:-2], lw[-2:]))
SMALL_SHARDED = ("conv_w", "conv_b", "sg_v_g")
SMALL_WIDE_PADDED = ("ssm_b_re", "ssm_b_im")
SMALL = ("ada_b", "norm1_g", "norm2_g", "final_g", "ssm_a_re", "ssm_a_im", "ssm_log_dt", "ssm_b_re", "ssm_b_im", "ssm_c_re",
         "ssm_c_im", "ssm_d", "ssm_glu_b", "sg_w_s", "sg_b_s") + SMALL_SHARDED
WEIGHTS = ("ada_w", "ada_b", "norm1_g", "norm2_g", "ff_w1", "ff_w2", "final_g", "conv_w_in", "conv_w", "conv_b", "conv_w_out",
           "ssm_w_in", "ssm_a_re", "ssm_a_im", "ssm_log_dt", "ssm_b_re", "ssm_b_im", "ssm_c_re", "ssm_c_im", "ssm_d",
           "ssm_glu_w", "ssm_glu_b", "ssm_w_out", "sg_w_in", "sg_v_g", "sg_w_s", "sg_b_s", "sg_w_out")


def kernel(x, c, ada_w, ada_b, norm1_g, norm2_g, ff_w1, ff_w2, final_g, conv_w_in, conv_w, conv_b, conv_w_out, ssm_w_in, ssm_a_re, ssm_a_im, ssm_log_dt, ssm_b_re, ssm_b_im, ssm_c_re, ssm_c_im, ssm_d, ssm_glu_w, ssm_glu_b, ssm_w_out, sg_w_in, sg_v_g, sg_w_s, sg_b_s, sg_w_out, loss_target, m_ada_w, m_ada_b, m_norm1_g, m_norm2_g, m_ff_w1, m_ff_w2, m_final_g, m_conv_w_in, m_conv_w, m_conv_b, m_conv_w_out, m_ssm_w_in, m_ssm_a_re, m_ssm_a_im, m_ssm_log_dt, m_ssm_b_re, m_ssm_b_im, m_ssm_c_re, m_ssm_c_im, m_ssm_d, m_ssm_glu_w, m_ssm_glu_b, m_ssm_w_out, m_sg_w_in, m_sg_v_g, m_sg_w_s, m_sg_b_s, m_sg_w_out, v_ada_w, v_ada_b, v_norm1_g, v_norm2_g, v_ff_w1, v_ff_w2, v_final_g, v_conv_w_in, v_conv_w, v_conv_b, v_conv_w_out, v_ssm_w_in, v_ssm_a_re, v_ssm_a_im, v_ssm_log_dt, v_ssm_b_re, v_ssm_b_im, v_ssm_c_re, v_ssm_c_im, v_ssm_d, v_ssm_glu_w, v_ssm_glu_b, v_ssm_w_out, v_sg_w_in, v_sg_v_g, v_sg_w_s, v_sg_b_s, v_sg_w_out):
    args = dict(locals())
    w = {n: args[n] for n in WEIGHTS}
    m = {n: args["m_" + n] for n in WEIGHTS}
    v = {n: args["v_" + n] for n in WEIGHTS}
    L = x.shape[1]
    tm = min(L, 256)
    tm2 = min(L, 512)
    chip = 2 * lax.axis_index("x") + lax.axis_index("y")
    me = 2 * chip + lax.axis_index("c")
    xin = x[0]
    target = loss_target[0]
    chip1 = chip.reshape(1).astype(jnp.int32)
    place = jnp.stack([chip, lax.axis_index("c")]).astype(jnp.int32)

    gathers = []

    casts = {}

    def cast_group(g, after=None):
        entries = GATHER_GROUPS[g]
        axes = [BIG_AXIS[n] for n, _ in entries]
        casts[g] = _cast_place([w[n] for n, _ in entries], [li for _, li in entries], axes, place, f"cast_group{g}", after)

    def start_gather(g, after):
        if g not in casts:
            cast_group(g)
        axes = [BIG_AXIS[n] for n, _ in GATHER_GROUPS[g]]
        lands = casts[g]
        s_sems, r_sems, lands, token = _gather_start(lands, axes, f"gather_start{g}", after)
        gathers.append((s_sems, r_sems, lands, axes))
        return token

    def weights_of(g, after):
        s_sems, r_sems, lands, axes = gathers[g]
        lands = _gather_wait(s_sems, r_sems, lands, axes, f"gather_wait{g}", after)
        lands = _gather_share(lands, axes, f"gather_share{g}")
        token = start_gather(g + 2, lands[0]) if g + 2 < len(GATHER_GROUPS) else None
        return dict(zip([n for n, _ in GATHER_GROUPS[g]], lands)), token

    small_in = _pack([c, conv_w, conv_b, sg_v_g])
    got = _allgather_small(small_in, "gather_small_inputs").reshape(N_DEV, -1)
    c_all, cw_sh, cb_sh, vg_sh = _unpack(got, [(D,), conv_w.shape, conv_b.shape, sg_v_g.shape], lead=(N_DEV,))
    conv_w_full = jnp.concatenate([cw_sh[2 * k] for k in range(4)], axis=-1)
    conv_b_full = jnp.concatenate([cb_sh[2 * k] for k in range(4)], axis=-1)
    vg_full = jnp.concatenate([vg_sh[2 * k] for k in range(4)], axis=-1)
    c16 = jnp.pad(c_all, ((0, 16 - N_DEV), (0, 0)))

    cols = ada_w.shape[2]
    ada_b_cols = lax.dynamic_slice_in_dim(ada_b, chip * cols, cols, axis=1)[:, None, :]
    mod_sh = _ada_fwd(c16, ada_w, ada_b_cols)[:, :N_DEV, :]
    mod_all = _allgather_small(_pack([mod_sh]), "gather_mod").reshape(N_DEV, -1)
    mod_all = _unpack(mod_all, [mod_sh.shape], lead=(N_DEV,))[0]
    mod_mine = lax.dynamic_index_in_dim(mod_all[0::2], me, axis=2, keepdims=False)
    mod_mine = mod_mine.transpose(1, 0, 2).reshape(DEPTH, 6, D)
    pv = jnp.concatenate([mod_mine, norm1_g[:, None, :], norm2_g[:, None, :]], axis=1)

    second_started = start_gather(1, start_gather(0, pv))
    for g in range(2, len(GATHER_GROUPS)):
        cast_group(g, second_started)

    cw_rows = jnp.concatenate([conv_w_full, conv_b_full[:, None, :], jnp.zeros((conv_w_full.shape[0], 4, D), F32)], axis=1)

    a_re, a_im = ssm_a_re[0], ssm_a_im[0]
    log_dt = ssm_log_dt[0][:, None]
    bt_re, bt_im = ssm_b_re[0].transpose(2, 0, 1), ssm_b_im[0].transpose(2, 0, 1)
    abar_re, abar_im, bbar_re, bbar_im = _s5_params_fwd(a_re, a_im, log_dt, bt_re, bt_im, after=second_started)
    ar_vec, ai_vec = abar_re.reshape(1, NSTATE), abar_im.reshape(1, NSTATE)
    bd_re, bd_im = _bf(_blockdiag_b(bbar_re)), _bf(_blockdiag_b(bbar_im))
    cd_re, cd_im = _bf(_blockdiag_c(ssm_c_re[0])), _bf(_blockdiag_c(ssm_c_im[0]))

    saved = []
    fulls = []
    xl = xin
    for i in range(DEPTH):
        kind = MIXER_OF_LAYER[i]
        j = i // 3
        first_after = [second_started, bd_re, bd_im, cd_re, cd_im, cw_rows] + [casts[g][0] for g in range(2, len(GATHER_GROUPS))]
        full, tok = weights_of(2 * i, first_after if i == 0 else [xl])
        fulls.append(full)
        if kind == 0:
            x1, h, bcx, conv, q, y = _conv_fwd(xl, pv, full["conv_w_in"], full["conv_w_out"], cw_rows, i, j, tm2, after=tok)
            mix = dict(h=h, bcx=bcx, conv=conv, q=q, y=y)
        elif kind == 1:
            xp = _to_segments(xl)
            h, u, e_re, e_im, u_b = _s5_fwd_ends(xp, pv, full["ssm_w_in"], bd_re, bd_im, ar_vec, ai_vec, i, tm2, after=tok)
            s0_re, s0_im = _s5_segment_states(e_re, e_im, ar_vec, ai_vec, L // NSEG, adjoint=False)
            x1p, s_re, s_im, y1, zg, y3, y, s_re_b, s_im_b = _s5_fwd_out(
                xp, u, pv, bd_re, bd_im, s0_re, s0_im, ar_vec, ai_vec, cd_re, cd_im, ssm_d, full["ssm_glu_w"], ssm_glu_b,
                full["ssm_w_out"], i, tm)
            x1 = _from_segments(x1p)
            mix = dict(xp=xp, h=h, u=u, u_b=u_b, s_re=s_re, s_im=s_im, s_re_b=s_re_b, s_im_b=s_im_b, y1=y1, zg=zg, y3=y3, y=y)
        else:
            x1, h, uv, vm, q, y = _sg_fwd(xl, pv, full["sg_w_in"], sg_w_s[0], sg_b_s[0].T, vg_full, full["sg_w_out"], i, tm2,
                                          after=tok)
            mix = dict(h=h, uv=uv, vm=vm, q=q, y=y)
        ffn_weights, tok = weights_of(2 * i + 1, [x1])
        full.update(ffn_weights)
        x2, h2, a, f = _ffn_fwd(x1, pv, full["ff_w1"], full["ff_w2"], i, tm2, after=tok)
        saved.append(dict(x=xl, x1=x1, h2=h2, a=a, f=f, **mix))
        xl = x2

    dxl, vs_fin = _final(xl, target, final_g[None, :], tm2)

    gfull = {n: [None] * w[n].shape[0] for n in BIG}
    vs_mix, vs_ffn = [None] * DEPTH, [None] * DEPTH
    small_g = {}
    scatters = {}
    token = None

    def start_scatter(key, entries, after):
        garrs = [gfull[n][li][None] for n, li in entries]
        gaxes = [BIG_AXIS[n] for n, _ in entries]
        s_sems, r_sems, garrs, lands, tok = _scatter_start(garrs, gaxes, f"scatter_start{key}", after)
        scatters[key] = (s_sems, r_sems, garrs, lands, gaxes, entries)
        return tok

    for i in reversed(range(DEPTH)):
        kind = MIXER_OF_LAYER[i]
        j = i // 3
        sv = saved[i]
        full = fulls[i]
        dx1, p_b, da_b, df_b, vs_ffn[i] = _ffn_bwd(dxl, sv["x1"], sv["a"], sv["f"], pv, full["ff_w1"], full["ff_w2"], i, tm,
                                                   after=token)
        gfull["ff_w1"][i] = _mm_tn(sv["h2"], da_b, f"wgrad_ff_w1_{i}")
        gfull["ff_w2"][i] = _mm_tn(p_b, df_b, f"wgrad_ff_w2_{i}")
        if i == 0:
            token = start_scatter("0f", LAYER_WEIGHTS[0][2:], dx1)
        if kind == 0:
            dxl, dbcx_b, dy_b, vsm = _conv_bwd(dx1, sv["x"], sv["y"], sv["bcx"], sv["conv"], pv, full["conv_w_in"],
                                               full["conv_w_out"], cw_rows, i, j, tm2, after=token if i == 0 else None)
            gfull["conv_w_in"][j] = _mm_tn(sv["h"], dbcx_b, f"wgrad_conv_w_in_{j}")
            gfull["conv_w_out"][j] = _mm_tn(sv["q"], dy_b, f"wgrad_conv_w_out_{j}")
            small_g.setdefault("conv_w", [None, None])[j] = vsm[3:6]
            small_g.setdefault("conv_b", [None, None])[j] = vsm[6]
        elif kind == 1:
            dx1p = _to_segments(dx1)
            dy_b, y2_b, dzg_b, dy1_b, du_skip, eb_re, eb_im, vsm = _s5_bwd_ends(
                dx1p, sv["y"], sv["y1"], sv["zg"], sv["u"], pv, cd_re, cd_im, ar_vec, ai_vec, ssm_d, full["ssm_glu_w"],
                full["ssm_w_out"], i, tm)
            l0_re, l0_im = _s5_segment_states(eb_re, eb_im, ar_vec, ai_vec, L // NSEG, adjoint=True)
            dxp, du_b, lam_re, lam_im, dabar, vs_in = _s5_bwd_in(
                dx1p, dy1_b, du_skip, sv["xp"], sv["s_re"], sv["s_im"], pv, bd_re, bd_im, cd_re, cd_im, l0_re, l0_im,
                ar_vec, ai_vec, full["ssm_w_in"], i, tm)
            dxl = _from_segments(dxp)
            gfull["ssm_w_out"][0] = _mm_tn(sv["y3"], dy_b, "wgrad_ssm_w_out")
            gfull["ssm_glu_w"][0] = _mm_tn(y2_b, dzg_b, "wgrad_ssm_glu_w")
            gfull["ssm_w_in"][0] = _mm_tn(sv["h"], du_b, "wgrad_ssm_w_in")
            s5_late = dict(s_re=sv["s_re_b"], s_im=sv["s_im_b"], u=sv["u_b"], dy1_b=dy1_b, lam_re=lam_re, lam_im=lam_im, dabar=dabar)
            small_g.update(ssm_d=vsm[2], ssm_glu_b=vsm[1])
            vsm = jnp.concatenate([vsm[0:1], vs_in[1:3], jnp.zeros((5, D), F32)], axis=0)
        else:
            dxl, duv_b, dy_b, vsm, d_ws, d_bt = _sg_bwd(dx1, sv["x"], sv["y"], sv["uv"], sv["vm"], pv, full["sg_w_in"],
                                                        sg_w_s[0], vg_full, full["sg_w_out"], i, tm2)
            gfull["sg_w_in"][0] = _mm_tn(sv["h"], duv_b, "wgrad_sg_w_in")
            gfull["sg_w_out"][0] = _mm_tn(sv["q"], dy_b, "wgrad_sg_w_out")
            small_g.update(sg_v_g=vsm[3], sg_w_s=d_ws, sg_b_s=d_bt.T)
        vs_mix[i] = vsm
        token = start_scatter(str(i), LAYER_WEIGHTS[i], dxl) if i > 0 else start_scatter("0c", LAYER_WEIGHTS[0][:2], dxl)
    grad_x = dxl[None]

    sums = {n: [None] * w[n].shape[0] for n in BIG}

    def collect(key, after):
        s_sems, r_sems, garrs, lands, gaxes, entries = scatters[key]
        garrs, recv = _scatter_wait(s_sems, r_sems, garrs, lands, gaxes, f"scatter_wait{key}", after)
        for (n, li), t in zip(entries, _sum_parts(recv, garrs, gaxes, chip1, f"sum_group{key}")):
            sums[n][li] = t
        return sums[entries[-1][0]][entries[-1][1]]

    after = token
    for key in ("3", "2", "1"):
        after = collect(key, after)
    early = [(n, li) for i in (3, 2, 1) for n, li in LAYER_WEIGHTS[i]]
    late = list(LAYER_WEIGHTS[0][2:]) + list(LAYER_WEIGHTS[0][:2])
    s_sems, r_sems, mine_thru, lands, tok = _swap_start([sums[n][li] for n, li in early], "swap_start_early", after)

    blocks = dict(
        c_re=_mm_tn_blocks(s5_late["s_re"], s5_late["dy1_b"], S5_BP, S5_BH, "wgrad_s5_c_re", after=tok),
        c_im=_mm_tn_blocks(s5_late["s_im"], s5_late["dy1_b"], S5_BP, S5_BH, "wgrad_s5_c_im", after=tok),
        b_re=_mm_tn_blocks(s5_late["u"], s5_late["lam_re"], S5_BH, S5_BP, "wgrad_s5_b_re", after=tok),
        b_im=_mm_tn_blocks(s5_late["u"], s5_late["lam_im"], S5_BH, S5_BP, "wgrad_s5_b_im", after=tok))
    d_are, d_aim, d_ldt, d_btre, d_btim = _s5_params_bwd(
        a_re, a_im, log_dt, bt_re, bt_im, s5_late["dabar"][0].reshape(S5_G, S5_P), s5_late["dabar"][1].reshape(S5_G, S5_P),
        _unblock_b(blocks["b_re"]), _unblock_b(blocks["b_im"]))
    small_g.update(ssm_a_re=d_are, ssm_a_im=d_aim, ssm_log_dt=d_ldt, ssm_b_re=d_btre.transpose(1, 2, 0),
                   ssm_b_im=d_btim.transpose(1, 2, 0), ssm_c_re=_unblock_c(blocks["c_re"]), ssm_c_im=-_unblock_c(blocks["c_im"]))

    mine_thru, got = _swap_wait(s_sems, r_sems, mine_thru, lands, "swap_wait_early", blocks["b_im"])
    sib = dict(zip(early, got))
    for (n, li), t in zip(early, mine_thru):
        sums[n][li] = t
    after = got[-1]
    for key in ("0f", "0c"):
        after = collect(key, after)
    sib.update(zip(late, _swap_with_sibling([sums[n][li] for n, li in late], "swap_grad_sums_late")))

    dmod = _mod_bwd(jnp.stack(vs_mix), jnp.stack(vs_ffn), pv)
    small_g.update(ada_b=dmod[:, :6, :], norm1_g=dmod[:, 6, :], norm2_g=dmod[:, 7, :], final_g=vs_fin[1],
                   conv_w=jnp.stack(small_g["conv_w"]), conv_b=jnp.stack(small_g["conv_b"]))

    loss_part = (0.5 / D) * vs_fin[0, 0:1]
    dmod_rows = lax.dynamic_update_slice(jnp.zeros((N_DEV, DEPTH * 6 * D), F32), small_g["ada_b"].reshape(1, -1), (me, 0))
    part_shapes = [(1,)] + [tuple(small_g[n].shape) for n in SMALL] + [tuple(dmod_rows.shape)]
    slots = _reduce_pair(_pack([loss_part] + [small_g[n] for n in SMALL] + [dmod_rows], 16), "reduce_small_pair", sib[late[-1]])
    s_sems, r_sems, slots, tok = _reduce_cross_start(slots, "reduce_small_cross_start")

    res = {}
    for n in BIG:
        res[n] = _adamw_layers(w[n], sums[n], [sib[(n, li)] for li in range(w[n].shape[0])], m[n], v[n], f"adamw_{n}", after=tok)
        tok = res[n][0]

    slots = _reduce_cross_wait(s_sems, r_sems, slots, "reduce_small_cross_wait", tok)
    parts_sum = _reduce_finish(slots, "reduce_small_finish")
    summed = _unpack(parts_sum, part_shapes)
    loss = summed[0][0]
    gsum = dict(zip(SMALL, summed[1:1 + len(SMALL)]))
    dmod_all = summed[-1].reshape(N_DEV, DEPTH, 6 * D)
    dmod_cols = lax.dynamic_slice_in_dim(dmod_all, chip * cols, cols, axis=2).transpose(1, 0, 2)
    g_ada_w = _ada_bwd(c16, jnp.pad(dmod_cols, ((0, 0), (0, 16 - N_DEV), (0, 0))))

    shp = ada_w.shape
    two = lambda t: t.reshape(shp[0] * shp[1], shp[2])
    res["ada_w"] = [t.reshape(shp) for t in _adamw(two(ada_w), [two(g_ada_w)], two(m_ada_w), two(v_ada_w), "adamw_ada_w")]

    def mine(n):
        g = gsum[n]
        if n in SMALL_SHARDED:
            g = lax.dynamic_slice_in_dim(g, chip * w[n].shape[-1], w[n].shape[-1], axis=g.ndim - 1)
        return g.reshape(w[n].shape)

    for k, names in enumerate(([n for n in SMALL if n not in SMALL_WIDE_PADDED], list(SMALL_WIDE_PADDED))):
        outs = _adamw_many([w[n] for n in names], [mine(n) for n in names], [m[n] for n in names], [v[n] for n in names],
                           f"adamw_small{k}")
        for idx, n in enumerate(names):
            res[n] = [outs[part][idx] for part in range(4)]

    outs = [loss, grad_x]
    for part in range(4):
        outs += [res[n][part] for n in WEIGHTS]
    return tuple(outs)
```

```python
import functools

import jax
import jax.numpy as jnp
from jax import lax
from jax.experimental import pallas as pl
from jax.experimental.pallas import tpu as pltpu

F32 = jnp.float32
BF16 = jnp.bfloat16
D = 1024
EPS = 1e-6
DEPTH = 4
MIXER_OF_LAYER = (0, 1, 2, 0)
S5_G, S5_H, S5_P = 64, 16, 64
S5_NB = 4
S5_BH = S5_H * 16
S5_BP = S5_P * 16
NSTATE = S5_G * S5_P
SG_HEADS, SG_CHUNK = 8, 128
ADAM_LR, ADAM_B1, ADAM_B2, ADAM_EPS, ADAM_WD, ADAM_STEP = 0.001, 0.9, 0.999, 1e-08, 0.01, 10
N_DEV = 8
MESH = pl.DeviceIdType.MESH
LANES = 1024
R_SH1, R_SC1, R_G1, R_SH2, R_SC2, R_G2, R_N1, R_N2 = range(8)


def _dot(a, b):
    return jnp.dot(a, b, preferred_element_type=F32)


def _dot_nt(a, b):
    return lax.dot_general(a, b, (((1,), (1,)), ((), ())), preferred_element_type=F32)


def _dot_tn(a, b):
    return lax.dot_general(a, b, (((0,), (0,)), ((), ())), preferred_element_type=F32)


def _bf(x):
    return x.astype(BF16)


def _sum0(x):
    return jnp.sum(x, axis=0, keepdims=True)


def _params(n_axes, vmem_mb=48):
    return pltpu.CompilerParams(dimension_semantics=("arbitrary",) * n_axes, vmem_limit_bytes=vmem_mb << 20)


def _rows(tm, cols, nt=None):
    if nt is None:
        return pl.BlockSpec((tm, cols), lambda i: (i, 0))
    return pl.BlockSpec((tm, cols), lambda i: (nt - 1 - i, 0))


def _whole(shape):
    nd = len(shape)
    return pl.BlockSpec(shape, lambda *_: (0,) * nd)


def _layer_w(r, c, layer):
    return pl.BlockSpec((None, r, c), lambda *_: (layer, 0, 0), pipeline_mode=pl.Buffered(1))


def _const_w(shape):
    nd = len(shape)
    return pl.BlockSpec(shape, lambda *_: (0,) * nd, pipeline_mode=pl.Buffered(1))


def _call_after(after, body, n_in, *, in_specs, **kw):
    if after is None:
        return pl.pallas_call(body, in_specs=in_specs, **kw), ()

    def body_after(*refs):
        return body(*refs[:n_in], *refs[n_in + 1:])

    return pl.pallas_call(body_after, in_specs=list(in_specs) + [pl.BlockSpec(memory_space=pl.ANY)], **kw), (after,)


def _norm_mod(x, ng, sc, sh):
    r = lax.rsqrt(jnp.mean(x * x, axis=-1, keepdims=True) + EPS)
    xn = x * r
    return (xn * ng) * (1.0 + sc) + sh, xn, r


def _norm_mod_bwd(dh, xn, r, ng, sc):
    dxn = dh * (ng * (1.0 + sc))
    return r * (dxn - xn * jnp.mean(dxn * xn, axis=-1, keepdims=True))


def _shift_down(z, prev8, k):
    row = lax.broadcasted_iota(jnp.int32, z.shape, 0)
    if k == 1:
        return jnp.where(row >= 1, pltpu.roll(z, 1, 0), prev8[7:8])
    return jnp.where(row >= 2, pltpu.roll(z, 2, 0), jnp.where(row == 0, prev8[6:7], prev8[7:8]))


def _shift_up(z, next8, k):
    n = z.shape[0]
    row = lax.broadcasted_iota(jnp.int32, z.shape, 0)
    if k == 1:
        return jnp.where(row <= n - 2, pltpu.roll(z, n - 1, 0), next8[0:1])
    return jnp.where(row <= n - 3, pltpu.roll(z, n - 2, 0), jnp.where(row == n - 2, next8[0:1], next8[1:2]))


def _place():
    x, y, c = lax.axis_index("x"), lax.axis_index("y"), lax.axis_index("c")
    chips = [(1 - x, y), (x, 1 - y), (1 - x, 1 - y)]
    return x, y, c, chips


def _allgather_small(x_shard, name, after=None):
    m_per, n = x_shard.shape

    def body(x_ref, out_ref, send_sems, recv_sems, local_sem):
        x, y, c, chips = _place()
        me, sibling = (x, y, c), (x, y, 1 - c)

        def rows(px, py, pc):
            return out_ref.at[pl.ds((4 * px + 2 * py + pc) * m_per, m_per), :]

        def copy(k, block, to, src=None):
            return pltpu.make_async_remote_copy(
                src_ref=rows(*block) if src is None else src, dst_ref=rows(*block),
                send_sem=send_sems.at[k], recv_sem=recv_sems.at[k], device_id=to, device_id_type=MESH)

        mine = pltpu.make_async_copy(x_ref, rows(*me), local_sem)
        mine.start()
        first = [copy(0, me, sibling, src=x_ref)]
        first += [copy(1 + j, me, (*chip, c), src=x_ref) for j, chip in enumerate(chips)]
        for cp in first:
            cp.start()
        passed = [copy(4 + j, (*chip, c), sibling) for j, chip in enumerate(chips)]
        for j, chip in enumerate(chips):
            copy(1 + j, (*chip, c), me).wait_recv()
            passed[j].start()
        copy(0, sibling, me).wait_recv()
        for j, chip in enumerate(chips):
            copy(4 + j, (*chip, 1 - c), me).wait_recv()
        for cp in first + passed:
            cp.wait_send()
        mine.wait()

    call, tail = _call_after(
        after, body, 1, name=name, out_shape=jax.ShapeDtypeStruct((N_DEV * m_per, n), F32),
        in_specs=[pl.BlockSpec(memory_space=pltpu.VMEM)], out_specs=pl.BlockSpec(memory_space=pltpu.VMEM),
        scratch_shapes=[pltpu.SemaphoreType.DMA((7,)), pltpu.SemaphoreType.DMA((7,)), pltpu.SemaphoreType.DMA],
        compiler_params=pltpu.CompilerParams(vmem_limit_bytes=48 << 20),
    )
    return call(x_shard, *tail)


def _reduce_pair(x_part, name, after=None):
    m, n = x_part.shape
    h = m // 2

    def body(x_ref, slots_ref, sib_buf, send_sem, recv_sem):
        x, y, c, _ = _place()
        swap = pltpu.make_async_remote_copy(src_ref=x_ref, dst_ref=sib_buf, send_sem=send_sem, recv_sem=recv_sem,
                                            device_id=(x, y, 1 - c), device_id_type=MESH)
        swap.start()
        swap.wait()
        mine = pl.ds(pl.multiple_of(c * h, 8), h)
        slots_ref[pl.ds(2 * x + y, 1)] = (x_ref[mine, :] + sib_buf[mine, :])[None]

    call, tail = _call_after(
        after, body, 1, name=name, out_shape=jax.ShapeDtypeStruct((4, h, n), F32),
        in_specs=[pl.BlockSpec(memory_space=pltpu.VMEM)], out_specs=pl.BlockSpec(memory_space=pltpu.VMEM),
        scratch_shapes=[pltpu.VMEM((m, n), F32), pltpu.SemaphoreType.DMA, pltpu.SemaphoreType.DMA],
        compiler_params=pltpu.CompilerParams(vmem_limit_bytes=48 << 20),
    )
    return call(x_part, *tail)


def _reduce_cross_start(slots, name):
    def body(slots_ref, send_sems, recv_sems, thru, token):
        x, y, c, chips = _place()
        mine = slots_ref.at[pl.ds(2 * x + y, 1)]
        for j, chip in enumerate(chips):
            pltpu.make_async_remote_copy(src_ref=mine, dst_ref=mine, send_sem=send_sems.at[j], recv_sem=recv_sems.at[j],
                                         device_id=(*chip, c), device_id_type=MESH).start()
        token[...] = jnp.zeros_like(token)

    res = pl.pallas_call(
        body, name=name,
        out_shape=(pltpu.SemaphoreType.DMA((3,)), pltpu.SemaphoreType.DMA((3,)), pltpu.HBM(slots.shape, F32),
                   jax.ShapeDtypeStruct((8, 128), F32)),
        in_specs=[HBM_SPEC], out_specs=(SEM_SPEC, SEM_SPEC, HBM_SPEC, pl.BlockSpec(memory_space=pltpu.VMEM)),
        input_output_aliases={0: 2}, compiler_params=SPLIT_COPY_PARAMS,
    )(*_in_hbm([slots]))
    return res


def _reduce_cross_wait(send_sems, recv_sems, slots, name, after):
    def body(slots_ref, s_sems, r_sems, after_ref, thru):
        x, y, c, chips = _place()
        for j, chip in enumerate(chips):
            theirs = slots_ref.at[pl.ds(2 * chip[0] + chip[1], 1)]
            cp = pltpu.make_async_remote_copy(src_ref=theirs, dst_ref=theirs, send_sem=s_sems.at[j], recv_sem=r_sems.at[j],
                                              device_id=(x, y, c), device_id_type=MESH)
            cp.wait_send()
            cp.wait_recv()

    return pl.pallas_call(
        body, name=name, out_shape=pltpu.HBM(slots.shape, F32),
        in_specs=[HBM_SPEC, SEM_SPEC, SEM_SPEC, ANY_SPEC], out_specs=HBM_SPEC,
        input_output_aliases={0: 0}, compiler_params=SPLIT_COPY_PARAMS,
    )(slots, send_sems, recv_sems, after)


def _reduce_finish(slots, name):
    _, h, n = slots.shape

    def body(slots_ref, out_ref, send_sem, recv_sem):
        x, y, c, _ = _place()
        mine = pl.ds(pl.multiple_of(c * h, 8), h)
        theirs = pl.ds(pl.multiple_of((1 - c) * h, 8), h)
        out_ref[mine, :] = ((slots_ref[0] + slots_ref[1]) + slots_ref[2]) + slots_ref[3]
        give = pltpu.make_async_remote_copy(src_ref=out_ref.at[mine, :], dst_ref=out_ref.at[mine, :], send_sem=send_sem,
                                            recv_sem=recv_sem, device_id=(x, y, 1 - c), device_id_type=MESH)
        give.start()
        give.wait_send()
        pltpu.make_async_remote_copy(src_ref=out_ref.at[theirs, :], dst_ref=out_ref.at[theirs, :], send_sem=send_sem,
                                     recv_sem=recv_sem, device_id=(x, y, c), device_id_type=MESH).wait_recv()

    return pl.pallas_call(
        body, name=name, out_shape=jax.ShapeDtypeStruct((2 * h, n), F32),
        in_specs=[pl.BlockSpec(memory_space=pltpu.VMEM)], out_specs=pl.BlockSpec(memory_space=pltpu.VMEM),
        scratch_shapes=[pltpu.SemaphoreType.DMA, pltpu.SemaphoreType.DMA],
        compiler_params=pltpu.CompilerParams(vmem_limit_bytes=48 << 20),
    )(slots)


def _shard_region(ref, full_shape, axis, chip_k, half=None):
    _, r, c = full_shape
    if axis == 1:
        rs = r // 4
        if half is None:
            return ref.at[:, pl.ds(pl.multiple_of(chip_k * rs, 128), rs), :]
        return ref.at[:, pl.ds(pl.multiple_of(chip_k * rs + half * (rs // 2), 128), rs // 2), :]
    cs = c // 4
    if half is None:
        return ref.at[:, :, pl.ds(pl.multiple_of(chip_k * cs, 128), cs)]
    return ref.at[:, pl.ds(pl.multiple_of(half * (r // 2), 128), r // 2), pl.ds(pl.multiple_of(chip_k * cs, 128), cs)]


HBM_SPEC = pl.BlockSpec(memory_space=pltpu.HBM)
SEM_SPEC = pl.BlockSpec(memory_space=pltpu.SEMAPHORE)
ANY_SPEC = pl.BlockSpec(memory_space=pl.ANY)
SPLIT_COPY_PARAMS = pltpu.CompilerParams(has_side_effects=pltpu.SideEffectType.DATAFLOW_SIDE_EFFECTING)


def _in_hbm(arrs):
    return [pltpu.with_memory_space_constraint(a, pltpu.HBM) for a in arrs]


def _cast_place(ws, layers, axes, place, name, after=None):
    n_arr = len(ws)
    in_specs, out_specs, fulls = [], [], []
    for w_stack, li, axis in zip(ws, layers, axes):
        _, r, c = w_stack.shape
        tr = r // 4
        fulls.append((1, 4 * r, c) if axis == 1 else (1, r, 4 * c))
        in_specs.append(pl.BlockSpec((None, tr, c), lambda i, p, li=li: (li, 2 * p[1] + i, 0)))
        if axis == 1:
            out_specs.append(pl.BlockSpec((None, tr, c), lambda i, p: (0, 4 * p[0] + 2 * p[1] + i, 0)))
        else:
            out_specs.append(pl.BlockSpec((None, tr, c), lambda i, p: (0, 2 * p[1] + i, p[0])))

    extra = [] if after is None else [after]

    def body(p_ref, *refs):
        for a in range(n_arr):
            refs[n_arr + len(extra) + a][...] = _bf(refs[a][...])

    return pl.pallas_call(
        body, name=name,
        grid_spec=pltpu.PrefetchScalarGridSpec(num_scalar_prefetch=1, grid=(2,), in_specs=in_specs + [ANY_SPEC] * len(extra),
                                               out_specs=out_specs),
        out_shape=[jax.ShapeDtypeStruct(f, BF16) for f in fulls],
        compiler_params=_params(1),
    )(place, *ws, *extra)


def _gather_start(lands, axes, name, after):
    n_arr = len(lands)
    fulls = [tuple(l.shape) for l in lands]

    def body(*refs):
        land = refs[:n_arr]
        send_sems, recv_sems = refs[n_arr + 1:n_arr + 3]
        token = refs[-1]
        x, y, c, chips = _place()
        k_me = 2 * x + y
        for a in range(n_arr):
            mine = _shard_region(land[a], fulls[a], axes[a], k_me, c)
            for j, chip in enumerate(chips):
                pltpu.make_async_remote_copy(
                    src_ref=mine, dst_ref=mine, send_sem=send_sems.at[a * 3 + j], recv_sem=recv_sems.at[a * 3 + j],
                    device_id=(*chip, c), device_id_type=MESH).start()
        token[...] = jnp.zeros_like(token)

    res = pl.pallas_call(
        body, name=name,
        out_shape=(pltpu.SemaphoreType.DMA((3 * n_arr,)), pltpu.SemaphoreType.DMA((3 * n_arr,)),
                   *[pltpu.HBM(f, BF16) for f in fulls], jax.ShapeDtypeStruct((8, 128), F32)),
        in_specs=[HBM_SPEC] * n_arr + [ANY_SPEC],
        out_specs=(SEM_SPEC, SEM_SPEC, *[HBM_SPEC] * n_arr, pl.BlockSpec(memory_space=pltpu.VMEM)),
        input_output_aliases={a: 2 + a for a in range(n_arr)},
        compiler_params=SPLIT_COPY_PARAMS,
    )(*_in_hbm(lands), after)
    return res[0], res[1], list(res[2:2 + n_arr]), res[-1]


def _gather_wait(send_sems, recv_sems, lands, axes, name, after):
    n_arr = len(lands)
    fulls = [tuple(l.shape) for l in lands]

    def body(*refs):
        land = refs[:n_arr]
        s_sems, r_sems = refs[n_arr:n_arr + 2]
        x, y, c, chips = _place()
        for a in range(n_arr):
            for j, chip in enumerate(chips):
                k_j = 2 * chip[0] + chip[1]
                got = _shard_region(land[a], fulls[a], axes[a], k_j, c)
                cp = pltpu.make_async_remote_copy(
                    src_ref=got, dst_ref=got, send_sem=s_sems.at[a * 3 + j], recv_sem=r_sems.at[a * 3 + j],
                    device_id=(x, y, c), device_id_type=MESH)
                cp.wait_send()
                cp.wait_recv()

    res = pl.pallas_call(
        body, name=name,
        out_shape=tuple(pltpu.HBM(f, BF16) for f in fulls),
        in_specs=[HBM_SPEC] * n_arr + [SEM_SPEC, SEM_SPEC] + [ANY_SPEC] * len(after),
        out_specs=tuple([HBM_SPEC] * n_arr),
        input_output_aliases={a: a for a in range(n_arr)},
        compiler_params=SPLIT_COPY_PARAMS,
    )(*lands, send_sems, recv_sems, *after)
    return list(res)


def _gather_share(lands, axes, name):
    n_arr = len(lands)
    fulls = [tuple(l.shape) for l in lands]

    def body(*refs):
        land_in, land = refs[:n_arr], refs[n_arr:2 * n_arr]
        send_sems, recv_sems = refs[2 * n_arr:]
        x, y, c, chips = _place()
        copies = []
        for a in range(n_arr):
            for j, k_j in enumerate([2 * chip[0] + chip[1] for chip in chips] + [2 * x + y]):
                cp = pltpu.make_async_remote_copy(
                    src_ref=_shard_region(land_in[a], fulls[a], axes[a], k_j, c),
                    dst_ref=_shard_region(land[a], fulls[a], axes[a], k_j, c),
                    send_sem=send_sems.at[a * 4 + j], recv_sem=recv_sems.at[a * 4 + j],
                    device_id=(x, y, 1 - c), device_id_type=MESH)
                cp.start()
                copies.append(cp)
        for cp in copies:
            cp.wait()

    return pl.pallas_call(
        body, name=name, out_shape=[jax.ShapeDtypeStruct(f, BF16) for f in fulls],
        in_specs=[ANY_SPEC] * n_arr, out_specs=[ANY_SPEC] * n_arr,
        input_output_aliases={a: a for a in range(n_arr)},
        scratch_shapes=[pltpu.SemaphoreType.DMA((4 * n_arr,)), pltpu.SemaphoreType.DMA((4 * n_arr,))],
    )(*lands)


def _scatter_shapes(grads, axes):
    out = []
    for g, ax in zip(grads, axes):
        shp = list(g.shape)
        shp[ax] //= 4
        out.append((3,) + tuple(shp[1:]))
    return out


def _scatter_start(grads, axes, name, after):
    n_arr = len(grads)
    shapes = _scatter_shapes(grads, axes)
    lands = [lax.empty(s, BF16) for s in shapes]

    def body(*refs):
        ins, land = refs[:n_arr], refs[n_arr:2 * n_arr]
        send_sems, recv_sems = refs[2 * n_arr + 1:2 * n_arr + 3]
        token = refs[-1]
        x, y, c, chips = _place()
        for a in range(n_arr):
            for j, chip in enumerate(chips):
                k_j = 2 * chip[0] + chip[1]
                pltpu.make_async_remote_copy(
                    src_ref=_shard_region(ins[a], grads[a].shape, axes[a], k_j), dst_ref=land[a].at[pl.ds(j, 1)],
                    send_sem=send_sems.at[a * 3 + j], recv_sem=recv_sems.at[a * 3 + j],
                    device_id=(*chip, c), device_id_type=MESH).start()
        token[...] = jnp.zeros_like(token)

    res = pl.pallas_call(
        body, name=name,
        out_shape=(pltpu.SemaphoreType.DMA((3 * n_arr,)), pltpu.SemaphoreType.DMA((3 * n_arr,)),
                   *[pltpu.HBM(g.shape, BF16) for g in grads], *[pltpu.HBM(s, BF16) for s in shapes],
                   jax.ShapeDtypeStruct((8, 128), F32)),
        in_specs=[HBM_SPEC] * (2 * n_arr) + [ANY_SPEC],
        out_specs=(SEM_SPEC, SEM_SPEC, *[HBM_SPEC] * (2 * n_arr), pl.BlockSpec(memory_space=pltpu.VMEM)),
        input_output_aliases={a: 2 + a for a in range(2 * n_arr)},
        compiler_params=SPLIT_COPY_PARAMS,
    )(*_in_hbm(grads), *_in_hbm(lands), after)
    return res[0], res[1], list(res[2:2 + n_arr]), list(res[2 + n_arr:2 + 2 * n_arr]), res[-1]


def _scatter_wait(send_sems, recv_sems, grads, lands, axes, name, after):
    n_arr = len(grads)

    def body(*refs):
        ins, land = refs[:n_arr], refs[n_arr:2 * n_arr]
        s_sems, r_sems = refs[2 * n_arr:2 * n_arr + 2]
        x, y, c, chips = _place()
        for a in range(n_arr):
            for j, chip in enumerate(chips):
                k_j = 2 * chip[0] + chip[1]
                cp = pltpu.make_async_remote_copy(
                    src_ref=_shard_region(ins[a], grads[a].shape, axes[a], k_j), dst_ref=land[a].at[pl.ds(j, 1)],
                    send_sem=s_sems.at[a * 3 + j], recv_sem=r_sems.at[a * 3 + j],
                    device_id=(x, y, c), device_id_type=MESH)
                cp.wait_send()
                cp.wait_recv()

    res = pl.pallas_call(
        body, name=name,
        out_shape=(*[pltpu.HBM(g.shape, BF16) for g in grads], *[pltpu.HBM(l.shape, BF16) for l in lands]),
        in_specs=[HBM_SPEC] * (2 * n_arr) + [SEM_SPEC, SEM_SPEC, ANY_SPEC],
        out_specs=tuple([HBM_SPEC] * (2 * n_arr)),
        input_output_aliases={a: a for a in range(2 * n_arr)},
        compiler_params=SPLIT_COPY_PARAMS,
    )(*grads, *lands, send_sems, recv_sems, after)
    return list(res[:n_arr]), list(res[n_arr:])


def _swap_start(arrs, name, after):
    n_arr = len(arrs)
    lands = [lax.empty(a.shape, a.dtype) for a in arrs]

    def body(*refs):
        ins, land = refs[:n_arr], refs[n_arr:2 * n_arr]
        send_sems, recv_sems = refs[2 * n_arr + 1:2 * n_arr + 3]
        token = refs[-1]
        x, y, c, _ = _place()
        for a in range(n_arr):
            pltpu.make_async_remote_copy(
                src_ref=ins[a], dst_ref=land[a], send_sem=send_sems.at[a], recv_sem=recv_sems.at[a],
                device_id=(x, y, 1 - c), device_id_type=MESH).start()
        token[...] = jnp.zeros_like(token)

    res = pl.pallas_call(
        body, name=name,
        out_shape=(pltpu.SemaphoreType.DMA((n_arr,)), pltpu.SemaphoreType.DMA((n_arr,)),
                   *[pltpu.HBM(a.shape, a.dtype) for a in arrs], *[pltpu.HBM(a.shape, a.dtype) for a in arrs],
                   jax.ShapeDtypeStruct((8, 128), F32)),
        in_specs=[HBM_SPEC] * (2 * n_arr) + [ANY_SPEC],
        out_specs=(SEM_SPEC, SEM_SPEC, *[HBM_SPEC] * (2 * n_arr), pl.BlockSpec(memory_space=pltpu.VMEM)),
        input_output_aliases={a: 2 + a for a in range(2 * n_arr)},
        compiler_params=SPLIT_COPY_PARAMS,
    )(*_in_hbm(arrs), *_in_hbm(lands), after)
    return res[0], res[1], list(res[2:2 + n_arr]), list(res[2 + n_arr:2 + 2 * n_arr]), res[-1]


def _swap_wait(send_sems, recv_sems, arrs, lands, name, after):
    n_arr = len(arrs)

    def body(*refs):
        ins, land = refs[:n_arr], refs[n_arr:2 * n_arr]
        s_sems, r_sems = refs[2 * n_arr:2 * n_arr + 2]
        x, y, c, _ = _place()
        for a in range(n_arr):
            cp = pltpu.make_async_remote_copy(
                src_ref=ins[a], dst_ref=land[a], send_sem=s_sems.at[a], recv_sem=r_sems.at[a],
                device_id=(x, y, c), device_id_type=MESH)
            cp.wait_send()
            cp.wait_recv()

    res = pl.pallas_call(
        body, name=name,
        out_shape=(*[pltpu.HBM(a.shape, a.dtype) for a in arrs], *[pltpu.HBM(a.shape, a.dtype) for a in arrs]),
        in_specs=[HBM_SPEC] * (2 * n_arr) + [SEM_SPEC, SEM_SPEC, ANY_SPEC],
        out_specs=tuple([HBM_SPEC] * (2 * n_arr)),
        input_output_aliases={a: a for a in range(2 * n_arr)},
        compiler_params=SPLIT_COPY_PARAMS,
    )(*arrs, *lands, send_sems, recv_sems, after)
    return list(res[:n_arr]), list(res[n_arr:])


def _swap_with_sibling(arrs, name):
    n_arr = len(arrs)

    def body(*refs):
        ins, outs = refs[:n_arr], refs[n_arr:2 * n_arr]
        send_sems, recv_sems = refs[2 * n_arr:]
        x, y, c, _ = _place()
        copies = []
        for a in range(n_arr):
            cp = pltpu.make_async_remote_copy(
                src_ref=ins[a], dst_ref=outs[a], send_sem=send_sems.at[a], recv_sem=recv_sems.at[a],
                device_id=(x, y, 1 - c), device_id_type=MESH)
            cp.start()
            copies.append(cp)
        for cp in copies:
            cp.wait()

    any_spec = pl.BlockSpec(memory_space=pl.ANY)
    return pl.pallas_call(
        body, name=name, out_shape=[jax.ShapeDtypeStruct(a.shape, a.dtype) for a in arrs],
        in_specs=[any_spec] * n_arr, out_specs=[any_spec] * n_arr,
        scratch_shapes=[pltpu.SemaphoreType.DMA((n_arr,)), pltpu.SemaphoreType.DMA((n_arr,))],
    )(*arrs)


def _mm_tn(a, b, name, out_dtype=BF16):
    L, m = a.shape
    n = b.shape[1]
    bm, bn = min(m, 1024), min(n, 1024)
    nk = 2 if (m // bm) * (n // bn) == 1 and L % 32 == 0 else 1
    bk = L // nk

    def body(a_ref, b_ref, o_ref, *acc):
        part = _dot_tn(_bf(a_ref[...]), _bf(b_ref[...]))
        if nk == 1:
            o_ref[...] = part.astype(out_dtype)
        else:
            @pl.when(pl.program_id(2) == 0)
            def _():
                acc[0][...] = part

            @pl.when(pl.program_id(2) == 1)
            def _():
                o_ref[...] = (acc[0][...] + part).astype(out_dtype)

    return pl.pallas_call(
        body, name=name, grid=(m // bm, n // bn, nk),
        in_specs=[pl.BlockSpec((bk, bm), lambda i, j, k: (k, i)), pl.BlockSpec((bk, bn), lambda i, j, k: (k, j))],
        out_specs=pl.BlockSpec((bm, bn), lambda i, j, k: (i, j)),
        out_shape=jax.ShapeDtypeStruct((m, n), out_dtype),
        scratch_shapes=[pltpu.VMEM((bm, bn), F32)] if nk > 1 else [],
        compiler_params=_params(3),
    )(a, b)


def _mm_tn_blocks(a, b, wa, wb, name, after=None):
    L = a.shape[0]
    nb = a.shape[1] // wa
    bk = min(L, 4096)
    nk = L // bk

    def body(a_ref, b_ref, o_ref):
        @pl.when(pl.program_id(1) == 0)
        def _():
            o_ref[...] = jnp.zeros_like(o_ref)

        o_ref[...] += _dot_tn(_bf(a_ref[...]), _bf(b_ref[...]))

    call, tail = _call_after(
        after, body, 2, name=name, grid=(nb, nk),
        in_specs=[pl.BlockSpec((bk, wa), lambda j, k: (k, j)), pl.BlockSpec((bk, wb), lambda j, k: (k, j))],
        out_specs=pl.BlockSpec((None, wa, wb), lambda j, k: (j, 0, 0)),
        out_shape=jax.ShapeDtypeStruct((nb, wa, wb), F32),
        compiler_params=_params(2),
    )
    return call(a, b, *tail)


def _sum_parts(parts, owns, axes, chip, name):
    n_arr = len(parts)
    steps = 4
    in_specs, out_specs, shapes = [], [], []
    for part, axis in zip(parts, axes):
        _, r, c = part.shape
        tr = r // steps
        shapes.append((r, c))
        in_specs.append(pl.BlockSpec((3, tr, c), lambda i, k: (0, i, 0)))
        out_specs.append(pl.BlockSpec((tr, c), lambda i, k: (i, 0)))
    for part, axis in zip(parts, axes):
        _, r, c = part.shape
        tr = r // steps
        if axis == 1:
            in_specs.append(pl.BlockSpec((None, tr, c), lambda i, k: (0, steps * k[0] + i, 0)))
        else:
            in_specs.append(pl.BlockSpec((None, tr, c), lambda i, k: (0, i, k[0])))

    def body(k_ref, *refs):
        for a in range(n_arr):
            p = refs[a][...].astype(F32)
            refs[2 * n_arr + a][...] = ((p[0] + p[1]) + p[2]) + refs[n_arr + a][...].astype(F32)

    return pl.pallas_call(
        body, name=name,
        grid_spec=pltpu.PrefetchScalarGridSpec(num_scalar_prefetch=1, grid=(steps,), in_specs=in_specs, out_specs=out_specs),
        out_shape=[jax.ShapeDtypeStruct(sh, F32) for sh in shapes],
        compiler_params=_params(1),
    )(chip, *parts, *owns)


def _adamw(w, g_parts, m, v, name):
    n_g = len(g_parts)
    if w.ndim == 2:
        r, c = w.shape
        tr = r
        for cand in (512, 256, 128, 64, 32, 16, 8):
            if r % cand == 0 and cand * c * 4 <= (2 << 20):
                tr = cand
                break
        spec = pl.BlockSpec((tr, c), lambda i: (i, 0))
        tiling = dict(grid=(r // tr,), in_specs=[spec] * (3 + n_g), out_specs=[spec] * 4, compiler_params=_params(1))
    else:
        tiling = dict(compiler_params=pltpu.CompilerParams(vmem_limit_bytes=48 << 20))

    def body(*refs):
        w_ref, g_refs, m_ref, v_ref = refs[0], refs[1:1 + n_g], refs[1 + n_g], refs[2 + n_g]
        g = g_refs[0][...]
        for gr in g_refs[1:]:
            g = g + gr[...]
        _adamw_update(g, w_ref, m_ref, v_ref, *refs[3 + n_g:])

    return pl.pallas_call(body, name=name, out_shape=[jax.ShapeDtypeStruct(w.shape, F32)] * 4, **tiling)(w, *g_parts, m, v)


def _adamw_update(g, w_ref, m_ref, v_ref, g_out, d_out, m_out, v_out):
    m_new = ADAM_B1 * m_ref[...] + (1.0 - ADAM_B1) * g
    v_new = ADAM_B2 * v_ref[...] + (1.0 - ADAM_B2) * (g * g)
    m_hat = m_new * (1.0 / (1.0 - ADAM_B1 ** ADAM_STEP))
    v_hat = v_new * (1.0 / (1.0 - ADAM_B2 ** ADAM_STEP))
    g_out[...] = g
    d_out[...] = -ADAM_LR * (m_hat / (jnp.sqrt(v_hat) + ADAM_EPS) + ADAM_WD * w_ref[...])
    m_out[...] = m_new
    v_out[...] = v_new


def _adamw_many(ws, gs, ms, vs, name):
    n = len(ws)

    def body(*refs):
        for k in range(n):
            _adamw_update(refs[n + k][...], refs[k], refs[2 * n + k], refs[3 * n + k],
                          refs[4 * n + k], refs[5 * n + k], refs[6 * n + k], refs[7 * n + k])

    outs = pl.pallas_call(body, name=name, out_shape=[jax.ShapeDtypeStruct(t.shape, F32) for t in ws] * 4,
                          compiler_params=pltpu.CompilerParams(vmem_limit_bytes=56 << 20))(*ws, *gs, *ms, *vs)
    return [outs[part * n:(part + 1) * n] for part in range(4)]


def _adamw_layers(w, q_mine, q_sib, m, v, name, after=None):
    n, r, c = w.shape
    tr = r
    for cand in (512, 256, 128, 64, 32, 16, 8):
        if r % cand == 0 and cand * c * 4 <= (1 << 20):
            tr = cand
            break

    def body(*refs):
        w_ref, qm, qs, m_ref, v_ref = refs[0], refs[1:1 + n], refs[1 + n:1 + 2 * n], refs[1 + 2 * n], refs[2 + 2 * n]
        layer = pl.program_id(0)
        g = qm[0][...] + qs[0][...]
        for k in range(1, n):
            g = jnp.where(layer == k, qm[k][...] + qs[k][...], g)
        _adamw_update(g, w_ref, m_ref, v_ref, *refs[3 + 2 * n:])

    stacked = pl.BlockSpec((None, tr, c), lambda l, i: (l, i, 0))
    per_layer = [pl.BlockSpec((tr, c), lambda l, i, k=k: (jnp.where(l == k, i, 0), 0)) for k in range(n)]
    call, tail = _call_after(
        after, body, 3 + 2 * n, name=name, grid=(n, r // tr),
        in_specs=[stacked] + per_layer + per_layer + [stacked, stacked], out_specs=[stacked] * 4,
        out_shape=[jax.ShapeDtypeStruct(w.shape, F32)] * 4,
        compiler_params=_params(2),
    )
    return call(w, *q_mine, *q_sib, m, v, *tail)


def _ada_fwd(c16, ada_w, ada_b_cols):
    cols = ada_w.shape[2]

    def body(c_ref, w_ref, b_ref, o_ref):
        cv = c_ref[...]
        ca = _bf(cv * jax.nn.sigmoid(cv))
        o_ref[...] = _dot(ca, _bf(w_ref[...])) + b_ref[...]

    return pl.pallas_call(
        body, name="ada_fwd", grid=(DEPTH,),
        in_specs=[_whole((16, D)), pl.BlockSpec((None, D, cols), lambda i: (i, 0, 0)),
                  pl.BlockSpec((None, 1, cols), lambda i: (i, 0, 0))],
        out_specs=pl.BlockSpec((None, 16, cols), lambda i: (i, 0, 0)),
        out_shape=jax.ShapeDtypeStruct((DEPTH, 16, cols), F32),
        compiler_params=_params(1),
    )(c16, ada_w, ada_b_cols)


def _ada_bwd(c16, dmod16):
    cols = dmod16.shape[2]

    def body(c_ref, d_ref, o_ref):
        cv = c_ref[...]
        ca = _bf(cv * jax.nn.sigmoid(cv))
        o_ref[...] = _dot_tn(ca, _bf(d_ref[...]))

    return pl.pallas_call(
        body, name="ada_bwd", grid=(DEPTH,),
        in_specs=[_whole((16, D)), pl.BlockSpec((None, 16, cols), lambda i: (i, 0, 0))],
        out_specs=pl.BlockSpec((None, D, cols), lambda i: (i, 0, 0)),
        out_shape=jax.ShapeDtypeStruct((DEPTH, D, cols), F32),
        compiler_params=_params(1),
    )(c16, dmod16)


def _mod_bwd(vs_mix, vs_ffn, pv):
    def body(m_ref, f_ref, pv_ref, o_ref):
        for i in range(DEPTH):
            vm, vf, p = m_ref[i], f_ref[i], pv_ref[i]
            o_ref[i] = jnp.concatenate([
                vm[2:3], vm[1:2] * p[R_N1:R_N1 + 1], vm[0:1],
                vf[2:3], vf[1:2] * p[R_N2:R_N2 + 1], vf[0:1],
                vm[1:2] * (1.0 + p[R_SC1:R_SC1 + 1]), vf[1:2] * (1.0 + p[R_SC2:R_SC2 + 1])], axis=0)

    return pl.pallas_call(body, name="mod_bwd", out_shape=jax.ShapeDtypeStruct((DEPTH, 8, D), F32))(vs_mix, vs_ffn, pv)


def _ffn_fwd(x1, pv, w1, w2, layer, tm, after=None):
    L = x1.shape[0]
    dff = w1.shape[2]

    def body(x1_ref, pv_ref, w1_ref, w2_ref, x2_ref, h2_ref, a_ref, f_ref):
        x1v, p = x1_ref[...], pv_ref[...]
        h2, _, _ = _norm_mod(x1v, p[R_N2:R_N2 + 1], p[R_SC2:R_SC2 + 1], p[R_SH2:R_SH2 + 1])
        hb = _bf(h2)
        h2_ref[...] = hb
        a = _dot(hb, w1_ref[...])
        a_ref[...] = a
        ra = jnp.maximum(a, 0.0)
        f = _dot(_bf(ra * ra), w2_ref[...])
        f_ref[...] = f
        x2_ref[...] = x1v + p[R_G2:R_G2 + 1] * f

    call, tail = _call_after(
        after, body, 4, name=f"ffn_fwd{layer}", grid=(L // tm,),
        in_specs=[_rows(tm, D), pl.BlockSpec((None, 8, D), lambda i: (layer, 0, 0)), _layer_w(D, dff, 0), _layer_w(dff, D, 0)],
        out_specs=[_rows(tm, D), _rows(tm, D), _rows(tm, dff), _rows(tm, D)],
        out_shape=[jax.ShapeDtypeStruct((L, D), F32), jax.ShapeDtypeStruct((L, D), BF16),
                   jax.ShapeDtypeStruct((L, dff), F32), jax.ShapeDtypeStruct((L, D), F32)],
        compiler_params=_params(1, 56),
    )
    return call(x1, pv, w1, w2, *tail)


def _ffn_bwd(dx2, x1, a, f, pv, w1, w2, layer, tm, after=None):
    L = x1.shape[0]
    dff = w1.shape[2]
    extra = [] if after is None else [pl.BlockSpec(memory_space=pl.ANY)]
    extra_args = [] if after is None else [after]

    def body(dx2_ref, x1_ref, a_ref, f_ref, pv_ref, w1_ref, w2_ref, *rest):
        dx1_ref, p_ref, da_ref, df_ref, vs_ref = rest[len(extra):]

        @pl.when(pl.program_id(0) == 0)
        def _():
            vs_ref[...] = jnp.zeros_like(vs_ref)

        dx2v, p = dx2_ref[...], pv_ref[...]
        dfb = _bf(dx2v * p[R_G2:R_G2 + 1])
        df_ref[...] = dfb
        vs_ref[0:1, :] += _sum0(dx2v * f_ref[...])
        dp = _dot_nt(dfb, w2_ref[...])
        ra = jnp.maximum(a_ref[...], 0.0)
        p_ref[...] = _bf(ra * ra)
        dab = _bf(dp * (2.0 * ra))
        da_ref[...] = dab
        dh2 = _dot_nt(dab, w1_ref[...])
        _, xn, r = _norm_mod(x1_ref[...], p[R_N2:R_N2 + 1], p[R_SC2:R_SC2 + 1], p[R_SH2:R_SH2 + 1])
        dx1_ref[...] = dx2v + _norm_mod_bwd(dh2, xn, r, p[R_N2:R_N2 + 1], p[R_SC2:R_SC2 + 1])
        vs_ref[1:2, :] += _sum0(dh2 * xn)
        vs_ref[2:3, :] += _sum0(dh2)

    return pl.pallas_call(
        body, name=f"ffn_bwd{layer}", grid=(L // tm,),
        in_specs=[_rows(tm, D), _rows(tm, D), _rows(tm, dff), _rows(tm, D),
                  pl.BlockSpec((None, 8, D), lambda i: (layer, 0, 0)), _layer_w(D, dff, 0), _layer_w(dff, D, 0)] + extra,
        out_specs=[_rows(tm, D), _rows(tm, dff), _rows(tm, dff), _rows(tm, D), _whole((8, D))],
        out_shape=[jax.ShapeDtypeStruct((L, D), F32), jax.ShapeDtypeStruct((L, dff), BF16),
                   jax.ShapeDtypeStruct((L, dff), BF16), jax.ShapeDtypeStruct((L, D), BF16),
                   jax.ShapeDtypeStruct((8, D), F32)],
        compiler_params=_params(1, 56),
    )(dx2, x1, a, f, pv, w1, w2, *extra_args)


def _conv_fwd(x, pv, w_in, w_out, cw, layer, j, tm, after=None):
    L = x.shape[0]

    def body(x_ref, pv_ref, win_ref, wout_ref, cw_ref, x1_ref, h_ref, bcx_ref, conv_ref, q_ref, y_ref, carry):
        @pl.when(pl.program_id(0) == 0)
        def _():
            carry[...] = jnp.zeros_like(carry)

        xv, p, cwv = x_ref[...], pv_ref[...], cw_ref[...]
        h, _, _ = _norm_mod(xv, p[R_N1:R_N1 + 1], p[R_SC1:R_SC1 + 1], p[R_SH1:R_SH1 + 1])
        hb = _bf(h)
        h_ref[...] = hb
        bcx = _dot(hb, win_ref[...])
        bcx_ref[...] = bcx
        z = bcx[:, D:2 * D] * bcx[:, 2 * D:]
        prev8 = carry[...]
        conv = cwv[0:1] * _shift_down(z, prev8, 2) + cwv[1:2] * _shift_down(z, prev8, 1) + cwv[2:3] * z + cwv[3:4]
        conv_ref[...] = conv
        qb = _bf(bcx[:, :D] * conv)
        q_ref[...] = qb
        y = _dot(qb, wout_ref[...])
        y_ref[...] = y
        x1_ref[...] = xv + p[R_G1:R_G1 + 1] * y
        carry[...] = z[tm - 8:tm]

    call, tail = _call_after(
        after, body, 5, name=f"conv_fwd{layer}", grid=(L // tm,),
        in_specs=[_rows(tm, D), pl.BlockSpec((None, 8, D), lambda i: (layer, 0, 0)), _layer_w(D, 3 * D, 0), _layer_w(D, D, 0),
                  pl.BlockSpec((None, 8, D), lambda i: (j, 0, 0))],
        out_specs=[_rows(tm, D), _rows(tm, D), _rows(tm, 3 * D), _rows(tm, D), _rows(tm, D), _rows(tm, D)],
        out_shape=[jax.ShapeDtypeStruct((L, D), F32), jax.ShapeDtypeStruct((L, D), BF16), jax.ShapeDtypeStruct((L, 3 * D), F32),
                   jax.ShapeDtypeStruct((L, D), F32), jax.ShapeDtypeStruct((L, D), BF16), jax.ShapeDtypeStruct((L, D), F32)],
        scratch_shapes=[pltpu.VMEM((8, D), F32)],
        compiler_params=_params(1, 56),
    )
    return call(x, pv, w_in, w_out, cw, *tail)


def _conv_bwd(dx1, x, y, bcx, conv, pv, w_in, w_out, cw, layer, j, tm, after=None):
    L = x.shape[0]
    nt = L // tm

    def body(dx1_ref, x_ref, y_ref, bcx_ref, conv_ref, halo_ref, pv_ref, win_ref, wout_ref, cw_ref,
             dx_ref, dbcx_ref, dy_ref, vs_ref, carry):
        gi = pl.program_id(0)
        tile = nt - 1 - gi

        @pl.when(gi == 0)
        def _():
            vs_ref[...] = jnp.zeros_like(vs_ref)
            carry[...] = jnp.zeros_like(carry)

        dx1v, p, cwv = dx1_ref[...], pv_ref[...], cw_ref[...]
        dyb = _bf(dx1v * p[R_G1:R_G1 + 1])
        dy_ref[...] = dyb
        vs_ref[0:1, :] += _sum0(dx1v * y_ref[...])
        dq = _dot_nt(dyb, wout_ref[...])
        bcx = bcx_ref[...]
        b, cg, xh = bcx[:, :D], bcx[:, D:2 * D], bcx[:, 2 * D:]
        db = dq * conv_ref[...]
        dc = dq * b
        z = cg * xh
        halo = halo_ref[...]
        zprev = jnp.where(tile > 0, halo[:, D:2 * D] * halo[:, 2 * D:], 0.0)
        vs_ref[3:4, :] += _sum0(dc * _shift_down(z, zprev, 2))
        vs_ref[4:5, :] += _sum0(dc * _shift_down(z, zprev, 1))
        vs_ref[5:6, :] += _sum0(dc * z)
        vs_ref[6:7, :] += _sum0(dc)
        next8 = carry[...]
        dz = cwv[2:3] * dc + cwv[1:2] * _shift_up(dc, next8, 1) + cwv[0:1] * _shift_up(dc, next8, 2)
        dbb, dcgb, dxhb = _bf(db), _bf(dz * xh), _bf(dz * cg)
        dbcx_ref[:, 0:D] = dbb
        dbcx_ref[:, D:2 * D] = dcgb
        dbcx_ref[:, 2 * D:3 * D] = dxhb
        dh = (_dot_nt(dbb, win_ref[:, 0:D]) + _dot_nt(dcgb, win_ref[:, D:2 * D])) + _dot_nt(dxhb, win_ref[:, 2 * D:3 * D])
        _, xn, r = _norm_mod(x_ref[...], p[R_N1:R_N1 + 1], p[R_SC1:R_SC1 + 1], p[R_SH1:R_SH1 + 1])
        dx_ref[...] = dx1v + _norm_mod_bwd(dh, xn, r, p[R_N1:R_N1 + 1], p[R_SC1:R_SC1 + 1])
        vs_ref[1:2, :] += _sum0(dh * xn)
        vs_ref[2:3, :] += _sum0(dh)
        carry[...] = dc[0:8]

    halo_spec = pl.BlockSpec((8, 3 * D), lambda i: (jnp.maximum((nt - 1 - i) * (tm // 8) - 1, 0), 0))
    call, tail = _call_after(
        after, body, 10, name=f"conv_bwd{layer}", grid=(nt,),
        in_specs=[_rows(tm, D, nt), _rows(tm, D, nt), _rows(tm, D, nt), _rows(tm, 3 * D, nt), _rows(tm, D, nt), halo_spec,
                  pl.BlockSpec((None, 8, D), lambda i: (layer, 0, 0)), _layer_w(D, 3 * D, 0), _layer_w(D, D, 0),
                  pl.BlockSpec((None, 8, D), lambda i: (j, 0, 0))],
        out_specs=[_rows(tm, D, nt), _rows(tm, 3 * D, nt), _rows(tm, D, nt), _whole((8, D))],
        out_shape=[jax.ShapeDtypeStruct((L, D), F32), jax.ShapeDtypeStruct((L, 3 * D), BF16),
                   jax.ShapeDtypeStruct((L, D), BF16), jax.ShapeDtypeStruct((8, D), F32)],
        scratch_shapes=[pltpu.VMEM((8, D), F32)],
        compiler_params=_params(1, 56),
    )
    return call(dx1, x, y, bcx, conv, bcx, pv, w_in, w_out, cw, *tail)


def _s5_discretize(a_re, a_im, log_dt, bt_re, bt_im):
    dt = jnp.exp(log_dt)
    mag = jnp.exp(a_re * dt)
    abar_re = mag * jnp.cos(a_im * dt)
    abar_im = mag * jnp.sin(a_im * dt)
    den = a_re * a_re + a_im * a_im
    nr = abar_re - 1.0
    ni = abar_im
    f_re = (nr * a_re + ni * a_im) / den
    f_im = (ni * a_re - nr * a_im) / den
    bbar_re = f_re * bt_re - f_im * bt_im
    bbar_im = f_re * bt_im + f_im * bt_re
    return abar_re, abar_im, bbar_re, bbar_im


def _s5_params_fwd(a_re, a_im, log_dt, bt_re, bt_im, after=None):
    def body(ar, ai, ld, br, bi, o_ar, o_ai, o_br, o_bi):
        r = _s5_discretize(ar[...], ai[...], ld[...], br[...], bi[...])
        o_ar[...], o_ai[...], o_br[...], o_bi[...] = r

    gp = jax.ShapeDtypeStruct((S5_G, S5_P), F32)
    hgp = jax.ShapeDtypeStruct((S5_H, S5_G, S5_P), F32)
    call, tail = _call_after(after, body, 5, name="s5_params_fwd", out_shape=[gp, gp, hgp, hgp],
                             in_specs=[pl.BlockSpec(memory_space=pltpu.VMEM)] * 5)
    return call(a_re, a_im, log_dt, bt_re, bt_im, *tail)


def _s5_params_bwd(a_re, a_im, log_dt, bt_re, bt_im, d_ar, d_ai, d_br, d_bi):
    def body(ar, ai, ld, br, bi, gar, gai, gbr, gbi, o_ar, o_ai, o_ld, o_br, o_bi):
        _, vjp = jax.vjp(_s5_discretize, ar[...], ai[...], ld[...], br[...], bi[...])
        r = vjp((gar[...], gai[...], gbr[...], gbi[...]))
        o_ar[...], o_ai[...], o_ld[...], o_br[...], o_bi[...] = r

    gp = jax.ShapeDtypeStruct((S5_G, S5_P), F32)
    hgp = jax.ShapeDtypeStruct((S5_H, S5_G, S5_P), F32)
    return pl.pallas_call(body, name="s5_params_bwd", out_shape=[gp, gp, jax.ShapeDtypeStruct((S5_G, 1), F32), hgp, hgp])(
        a_re, a_im, log_dt, bt_re, bt_im, d_ar, d_ai, d_br, d_bi)


NSEG = 8
SCAN_LANES = 1024


def _to_segments(x):
    n, c = x.shape
    return x.reshape(NSEG, n // NSEG, c).transpose(1, 0, 2).reshape(n, c)


def _from_segments(x):
    n, c = x.shape
    return x.reshape(n // NSEG, NSEG, c).transpose(1, 0, 2).reshape(n, c)


def _segment_scan(re_ref, im_ref, st_re, st_im, a_re, a_im, n_slabs, adjoint, write):
    for q in range(NSTATE // SCAN_LANES):
        ls = slice(q * SCAN_LANES, (q + 1) * SCAN_LANES)
        ar = jnp.broadcast_to(a_re[:, ls], (8, SCAN_LANES))
        ai = jnp.broadcast_to(a_im[:, ls], (8, SCAN_LANES))

        def step(k, carry, ls=ls, ar=ar, ai=ai):
            s_r, s_i = carry
            slab = (n_slabs - 1 - k) if adjoint else k
            rows = pl.ds(pl.multiple_of(slab * 8, 8), 8)
            b_r, b_i = re_ref[rows, ls], im_ref[rows, ls]
            if adjoint:
                n_r = b_r + ar * s_r + ai * s_i
                n_i = b_i - ai * s_r + ar * s_i
            else:
                n_r = ar * s_r - ai * s_i + b_r
                n_i = ar * s_i + ai * s_r + b_i
            if write:
                re_ref[rows, ls] = n_r
                im_ref[rows, ls] = n_i
            return n_r, n_i

        s_r, s_i = lax.fori_loop(0, n_slabs, step, (st_re[:, ls], st_im[:, ls]), unroll=4)
        st_re[:, ls] = s_r
        st_im[:, ls] = s_i


def _s5_segment_states(e_re, e_im, ar, ai, seg_len, adjoint):
    def body(ere_ref, eim_ref, ar_ref, ai_ref, ore_ref, oim_ref):
        p_r, p_i = ar_ref[...], ai_ref[...]
        if adjoint:
            p_i = -p_i
        acc_r, acc_i = jnp.ones_like(p_r), jnp.zeros_like(p_r)
        n = seg_len
        while n:
            if n & 1:
                acc_r, acc_i = acc_r * p_r - acc_i * p_i, acc_r * p_i + acc_i * p_r
            n >>= 1
            if n:
                p_r, p_i = p_r * p_r - p_i * p_i, 2.0 * p_r * p_i
        e_r, e_i = ere_ref[...], eim_ref[...]
        s_r, s_i = jnp.zeros_like(acc_r), jnp.zeros_like(acc_r)
        rows_r, rows_i = [None] * NSEG, [None] * NSEG
        order = range(NSEG - 1, -1, -1) if adjoint else range(NSEG)
        for j in order:
            rows_r[j], rows_i[j] = s_r, s_i
            s_r, s_i = (acc_r * s_r - acc_i * s_i + e_r[j:j + 1], acc_r * s_i + acc_i * s_r + e_i[j:j + 1])
        ore_ref[...] = jnp.concatenate(rows_r, axis=0)
        oim_ref[...] = jnp.concatenate(rows_i, axis=0)

    st = jax.ShapeDtypeStruct((NSEG, NSTATE), F32)
    return pl.pallas_call(body, name="s5_segment_states_bwd" if adjoint else "s5_segment_states_fwd", out_shape=[st, st])(
        e_re, e_im, ar, ai)


def _s5_fwd_ends(x, pv, w_in, b_re, b_im, ar, ai, layer, tm, after=None):
    L = x.shape[0]

    def body(x_ref, pv_ref, win_ref, bre_ref, bim_ref, ar_ref, ai_ref, h_ref, u_ref, ere_ref, eim_ref, ub_ref, bu_re, bu_im):
        @pl.when(pl.program_id(0) == 0)
        def _():
            ere_ref[...] = jnp.zeros_like(ere_ref)
            eim_ref[...] = jnp.zeros_like(eim_ref)

        p = pv_ref[...]
        h, _, _ = _norm_mod(x_ref[...], p[R_N1:R_N1 + 1], p[R_SC1:R_SC1 + 1], p[R_SH1:R_SH1 + 1])
        hb = _bf(h)
        h_ref[...] = hb
        u = _dot(hb, win_ref[...])
        u_ref[...] = u
        ub = _bf(u)
        ub_ref[...] = ub
        for k in range(S5_NB):
            uk = ub[:, k * S5_BH:(k + 1) * S5_BH]
            bu_re[:, k * S5_BP:(k + 1) * S5_BP] = _dot(uk, bre_ref[k])
            bu_im[:, k * S5_BP:(k + 1) * S5_BP] = _dot(uk, bim_ref[k])
        _segment_scan(bu_re, bu_im, ere_ref, eim_ref, ar_ref[...], ai_ref[...], tm // 8, adjoint=False, write=False)

    call, tail = _call_after(
        after, body, 7, name="s5_fwd_ends", grid=(L // tm,),
        in_specs=[_rows(tm, D), pl.BlockSpec((None, 8, D), lambda i: (layer, 0, 0)), _layer_w(D, D, 0),
                  _const_w((S5_NB, S5_BH, S5_BP)), _const_w((S5_NB, S5_BH, S5_BP)), _whole((1, NSTATE)), _whole((1, NSTATE))],
        out_specs=[_rows(tm, D), _rows(tm, D), _whole((NSEG, NSTATE)), _whole((NSEG, NSTATE)), _rows(tm, D)],
        out_shape=[jax.ShapeDtypeStruct((L, D), BF16), jax.ShapeDtypeStruct((L, D), F32),
                   jax.ShapeDtypeStruct((NSEG, NSTATE), F32), jax.ShapeDtypeStruct((NSEG, NSTATE), F32),
                   jax.ShapeDtypeStruct((L, D), BF16)],
        scratch_shapes=[pltpu.VMEM((tm, NSTATE), F32), pltpu.VMEM((tm, NSTATE), F32)],
        compiler_params=_params(1, 56),
    )
    return call(x, pv, w_in, b_re, b_im, ar, ai, *tail)


def _s5_fwd_out(x, u, pv, b_re, b_im, s0_re, s0_im, ar, ai, c_re, c_im, dvec, glu_w, glu_b, w_out, layer, tm):
    L = x.shape[0]

    def body(x_ref, u_ref, pv_ref, bre_ref, bim_ref, s0re_ref, s0im_ref, ar_ref, ai_ref, cre_ref, cim_ref, d_ref, gw_ref,
             gb_ref, wout_ref, x1_ref, sre_ref, sim_ref, y1_ref, zg_ref, y3_ref, y_ref, srb_ref, sib_ref, st_re, st_im):
        @pl.when(pl.program_id(0) == 0)
        def _():
            st_re[...] = s0re_ref[...]
            st_im[...] = s0im_ref[...]

        p = pv_ref[...]
        uv = u_ref[...]
        ub = _bf(uv)
        for k in range(S5_NB):
            uk = ub[:, k * S5_BH:(k + 1) * S5_BH]
            sre_ref[:, k * S5_BP:(k + 1) * S5_BP] = _dot(uk, bre_ref[k])
            sim_ref[:, k * S5_BP:(k + 1) * S5_BP] = _dot(uk, bim_ref[k])
        _segment_scan(sre_ref, sim_ref, st_re, st_im, ar_ref[...], ai_ref[...], tm // 8, adjoint=False, write=True)
        parts = []
        for k in range(S5_NB):
            sl = slice(k * S5_BP, (k + 1) * S5_BP)
            srb, sib = _bf(sre_ref[:, sl]), _bf(sim_ref[:, sl])
            srb_ref[:, sl] = srb
            sib_ref[:, sl] = sib
            parts.append(_dot(srb, cre_ref[k]) - _dot(sib, cim_ref[k]))
        y1 = jnp.concatenate(parts, axis=1) + d_ref[...] * uv
        y1_ref[...] = y1
        y2 = jax.nn.gelu(y1)
        zg = _dot(_bf(y2), gw_ref[...]) + gb_ref[...]
        zg_ref[...] = zg
        y3b = _bf(y2 * jax.nn.sigmoid(zg))
        y3_ref[...] = y3b
        y = _dot(y3b, wout_ref[...])
        y_ref[...] = y
        x1_ref[...] = x_ref[...] + p[R_G1:R_G1 + 1] * y

    return pl.pallas_call(
        body, name="s5_fwd_out", grid=(L // tm,),
        in_specs=[_rows(tm, D), _rows(tm, D), pl.BlockSpec((None, 8, D), lambda i: (layer, 0, 0)),
                  _const_w((S5_NB, S5_BH, S5_BP)), _const_w((S5_NB, S5_BH, S5_BP)),
                  _whole((NSEG, NSTATE)), _whole((NSEG, NSTATE)), _whole((1, NSTATE)), _whole((1, NSTATE)),
                  _const_w((S5_NB, S5_BP, S5_BH)), _const_w((S5_NB, S5_BP, S5_BH)), _whole((1, D)),
                  _layer_w(D, D, 0), _whole((1, D)), _layer_w(D, D, 0)],
        out_specs=[_rows(tm, D), _rows(tm, NSTATE), _rows(tm, NSTATE), _rows(tm, D), _rows(tm, D), _rows(tm, D), _rows(tm, D),
                   _rows(tm, NSTATE), _rows(tm, NSTATE)],
        out_shape=[jax.ShapeDtypeStruct((L, D), F32), jax.ShapeDtypeStruct((L, NSTATE), F32), jax.ShapeDtypeStruct((L, NSTATE), F32),
                   jax.ShapeDtypeStruct((L, D), F32), jax.ShapeDtypeStruct((L, D), F32),
                   jax.ShapeDtypeStruct((L, D), BF16), jax.ShapeDtypeStruct((L, D), F32),
                   jax.ShapeDtypeStruct((L, NSTATE), BF16), jax.ShapeDtypeStruct((L, NSTATE), BF16)],
        scratch_shapes=[pltpu.VMEM((NSEG, NSTATE), F32), pltpu.VMEM((NSEG, NSTATE), F32)],
        compiler_params=_params(1, 60),
    )(x, u, pv, b_re, b_im, s0_re, s0_im, ar, ai, c_re, c_im, dvec, glu_w, glu_b, w_out)


def _s5_bwd_ends(dx1, y, y1, zg, u, pv, c_re, c_im, ar, ai, dvec, glu_w, w_out, layer, tm, after=None):
    L = dx1.shape[0]
    nt = L // tm

    def body(dx1_ref, y_ref, y1_ref, zg_ref, u_ref, pv_ref, cre_ref, cim_ref, ar_ref, ai_ref, d_ref, gw_ref, wout_ref,
             dy_ref, y2_ref, dzg_ref, dy1_ref, dus_ref, ere_ref, eim_ref, vs_ref, g_re, g_im):
        @pl.when(pl.program_id(0) == 0)
        def _():
            vs_ref[...] = jnp.zeros_like(vs_ref)
            ere_ref[...] = jnp.zeros_like(ere_ref)
            eim_ref[...] = jnp.zeros_like(eim_ref)

        dx1v, p = dx1_ref[...], pv_ref[...]
        dyb = _bf(dx1v * p[R_G1:R_G1 + 1])
        dy_ref[...] = dyb
        vs_ref[0:1, :] += _sum0(dx1v * y_ref[...])
        dy3 = _dot_nt(dyb, wout_ref[...])
        y2, gelu_vjp = jax.vjp(jax.nn.gelu, y1_ref[...])
        y2_ref[...] = _bf(y2)
        gate = jax.nn.sigmoid(zg_ref[...])
        dzg = dy3 * y2 * gate * (1.0 - gate)
        dzgb = _bf(dzg)
        dzg_ref[...] = dzgb
        vs_ref[1:2, :] += _sum0(dzg)
        dy2 = dy3 * gate + _dot_nt(dzgb, gw_ref[...])
        dy1 = gelu_vjp(dy2)[0]
        vs_ref[2:3, :] += _sum0(dy1 * u_ref[...])
        dus_ref[...] = dy1 * d_ref[...]
        dy1b = _bf(dy1)
        dy1_ref[...] = dy1b
        for k in range(S5_NB):
            dk = dy1b[:, k * S5_BH:(k + 1) * S5_BH]
            g_re[:, k * S5_BP:(k + 1) * S5_BP] = _dot_nt(dk, cre_ref[k])
            g_im[:, k * S5_BP:(k + 1) * S5_BP] = -_dot_nt(dk, cim_ref[k])
        _segment_scan(g_re, g_im, ere_ref, eim_ref, ar_ref[...], ai_ref[...], tm // 8, adjoint=True, write=False)

    call, tail = _call_after(
        after, body, 13, name="s5_bwd_ends", grid=(nt,),
        in_specs=[_rows(tm, D, nt)] * 5 + [pl.BlockSpec((None, 8, D), lambda i: (layer, 0, 0)),
                  _const_w((S5_NB, S5_BP, S5_BH)), _const_w((S5_NB, S5_BP, S5_BH)), _whole((1, NSTATE)), _whole((1, NSTATE)),
                  _whole((1, D)), _layer_w(D, D, 0), _layer_w(D, D, 0)],
        out_specs=[_rows(tm, D, nt)] * 5 + [_whole((NSEG, NSTATE)), _whole((NSEG, NSTATE)), _whole((8, D))],
        out_shape=[jax.ShapeDtypeStruct((L, D), BF16)] * 4 + [jax.ShapeDtypeStruct((L, D), F32),
                   jax.ShapeDtypeStruct((NSEG, NSTATE), F32), jax.ShapeDtypeStruct((NSEG, NSTATE), F32),
                   jax.ShapeDtypeStruct((8, D), F32)],
        scratch_shapes=[pltpu.VMEM((tm, NSTATE), F32), pltpu.VMEM((tm, NSTATE), F32)],
        compiler_params=_params(1, 56),
    )
    return call(dx1, y, y1, zg, u, pv, c_re, c_im, ar, ai, dvec, glu_w, w_out, *tail)


def _s5_bwd_in(dx1, dy1_b, du_skip, x, s_re, s_im, pv, b_re, b_im, c_re, c_im, l0_re, l0_im, ar, ai, w_in, layer, tm):
    L = x.shape[0]
    nt = L // tm

    def body(dx1_ref, dy1_ref, dus_ref, x_ref, sre_ref, sim_ref, hre_ref, him_ref, lre_ref, lim_ref, pv_ref, bre_ref, bim_ref,
             cre_ref, cim_ref, l0re_ref, l0im_ref, ar_ref, ai_ref, win_ref,
             dx_ref, du_ref, lamre_ref, lamim_ref, da_ref, vs_ref, g_re, g_im, st_re, st_im):
        gi = pl.program_id(0)
        tile = nt - 1 - gi

        @pl.when(gi == 0)
        def _():
            vs_ref[...] = jnp.zeros_like(vs_ref)
            da_ref[...] = jnp.zeros_like(da_ref)
            st_re[...] = l0re_ref[...]
            st_im[...] = l0im_ref[...]

        p = pv_ref[...]
        dy1b = dy1_ref[...]
        for k in range(S5_NB):
            dk = dy1b[:, k * S5_BH:(k + 1) * S5_BH]
            g_re[:, k * S5_BP:(k + 1) * S5_BP] = _dot_nt(dk, cre_ref[k])
            g_im[:, k * S5_BP:(k + 1) * S5_BP] = -_dot_nt(dk, cim_ref[k])
        _segment_scan(g_re, g_im, st_re, st_im, ar_ref[...], ai_ref[...], tm // 8, adjoint=True, write=True)
        lam_r, lam_i = g_re[...], g_im[...]
        lrb, lib = _bf(lam_r), _bf(lam_i)
        lamre_ref[...] = lrb
        lamim_ref[...] = lib

        def wrapped(last_ref):
            z = last_ref[...]
            row = lax.broadcasted_iota(jnp.int32, z.shape, 0)
            return jnp.where(row >= 1, pltpu.roll(z, 1, 0), 0.0)

        first_r = jnp.where(tile > 0, hre_ref[...], wrapped(lre_ref))
        first_i = jnp.where(tile > 0, him_ref[...], wrapped(lim_ref))
        sp_r = jnp.concatenate([first_r, sre_ref[0:tm - 8, :]], axis=0)
        sp_i = jnp.concatenate([first_i, sim_ref[0:tm - 8, :]], axis=0)
        da_ref[0:1, :] += _sum0(lam_r * sp_r + lam_i * sp_i)
        da_ref[1:2, :] += _sum0(lam_i * sp_r - lam_r * sp_i)

        parts = []
        for k in range(S5_NB):
            sl = slice(k * S5_BP, (k + 1) * S5_BP)
            parts.append(_dot_nt(lrb[:, sl], bre_ref[k]) + _dot_nt(lib[:, sl], bim_ref[k]))
        dub = _bf(jnp.concatenate(parts, axis=1) + dus_ref[...])
        du_ref[...] = dub
        dh = _dot_nt(dub, win_ref[...])
        _, xn, r = _norm_mod(x_ref[...], p[R_N1:R_N1 + 1], p[R_SC1:R_SC1 + 1], p[R_SH1:R_SH1 + 1])
        dx_ref[...] = dx1_ref[...] + _norm_mod_bwd(dh, xn, r, p[R_N1:R_N1 + 1], p[R_SC1:R_SC1 + 1])
        vs_ref[1:2, :] += _sum0(dh * xn)
        vs_ref[2:3, :] += _sum0(dh)

    halo = pl.BlockSpec((8, NSTATE), lambda i: (jnp.maximum((nt - 1 - i) * (tm // 8) - 1, 0), 0))
    last = pl.BlockSpec((8, NSTATE), lambda i: (L // 8 - 1, 0))
    return pl.pallas_call(
        body, name="s5_bwd_in", grid=(nt,),
        in_specs=[_rows(tm, D, nt), _rows(tm, D, nt), _rows(tm, D, nt), _rows(tm, D, nt), _rows(tm, NSTATE, nt), _rows(tm, NSTATE, nt),
                  halo, halo, last, last, pl.BlockSpec((None, 8, D), lambda i: (layer, 0, 0)),
                  _const_w((S5_NB, S5_BH, S5_BP)), _const_w((S5_NB, S5_BH, S5_BP)),
                  _const_w((S5_NB, S5_BP, S5_BH)), _const_w((S5_NB, S5_BP, S5_BH)),
                  _whole((NSEG, NSTATE)), _whole((NSEG, NSTATE)), _whole((1, NSTATE)), _whole((1, NSTATE)), _layer_w(D, D, 0)],
        out_specs=[_rows(tm, D, nt), _rows(tm, D, nt), _rows(tm, NSTATE, nt), _rows(tm, NSTATE, nt), _whole((8, NSTATE)), _whole((8, D))],
        out_shape=[jax.ShapeDtypeStruct((L, D), F32), jax.ShapeDtypeStruct((L, D), BF16),
                   jax.ShapeDtypeStruct((L, NSTATE), BF16), jax.ShapeDtypeStruct((L, NSTATE), BF16),
                   jax.ShapeDtypeStruct((8, NSTATE), F32), jax.ShapeDtypeStruct((8, D), F32)],
        scratch_shapes=[pltpu.VMEM((tm, NSTATE), F32), pltpu.VMEM((tm, NSTATE), F32),
                        pltpu.VMEM((NSEG, NSTATE), F32), pltpu.VMEM((NSEG, NSTATE), F32)],
        compiler_params=_params(1, 60),
    )(dx1, dy1_b, du_skip, x, s_re, s_im, s_re, s_im, s_re, s_im, pv, b_re, b_im, c_re, c_im, l0_re, l0_im, ar, ai, w_in)


def _blockdiag_b(bt):
    b = bt.reshape(S5_H, S5_NB, 16, S5_P).transpose(1, 2, 0, 3)
    eye = jnp.eye(16, dtype=bt.dtype)
    return (b[:, :, :, None, :] * eye[None, :, None, :, None]).reshape(S5_NB, S5_BH, S5_BP)


def _unblock_b(d):
    d = jnp.einsum("bghgp->bghp", d.reshape(S5_NB, 16, S5_H, 16, S5_P))
    return d.transpose(2, 0, 1, 3).reshape(S5_H, S5_G, S5_P)


def _blockdiag_c(cm):
    c4 = cm.reshape(S5_NB, 16, S5_H, S5_P)
    eye = jnp.eye(16, dtype=cm.dtype)
    out = c4.transpose(0, 1, 3, 2)[:, :, :, None, :] * eye[None, :, None, :, None]
    return out.reshape(S5_NB, S5_BP, S5_BH)


def _unblock_c(d):
    d = jnp.einsum("bgpgh->bghp", d.reshape(S5_NB, 16, S5_P, 16, S5_H))
    return d.reshape(S5_G, S5_H, S5_P)


def _tril_mask():
    return lax.broadcasted_iota(jnp.int32, (SG_CHUNK, SG_CHUNK), 0) >= lax.broadcasted_iota(jnp.int32, (SG_CHUNK, SG_CHUNK), 1)


def _sg_fwd(x, pv, w_in, w_s, b_t, vg, w_out, layer, tm, after=None):
    L = x.shape[0]
    nc = tm // SG_CHUNK

    def body(x_ref, pv_ref, win_ref, ws_ref, bt_ref, vg_ref, wout_ref, x1_ref, h_ref, uv_ref, vm_ref, q_ref, y_ref):
        xv, p = x_ref[...], pv_ref[...]
        h, _, _ = _norm_mod(xv, p[R_N1:R_N1 + 1], p[R_SC1:R_SC1 + 1], p[R_SH1:R_SH1 + 1])
        hb = _bf(h)
        h_ref[...] = hb
        uv = _dot(hb, win_ref[...])
        uv_ref[...] = uv
        v = uv[:, D:]
        rv = lax.rsqrt(jnp.mean(v * v, axis=-1, keepdims=True) + EPS)
        vnb = _bf((v * rv) * vg_ref[...])
        mask = _tril_mask()
        bt = bt_ref[...]
        for hd in range(SG_HEADS):
            wm = _bf(jnp.where(mask, ws_ref[hd], 0.0))
            cs = slice(hd * SG_CHUNK, (hd + 1) * SG_CHUNK)
            for ck in range(nc):
                rs = slice(ck * SG_CHUNK, (ck + 1) * SG_CHUNK)
                vm_ref[rs, cs] = _dot(wm, vnb[rs, cs]) + bt[:, hd:hd + 1]
        qb = _bf(uv[:, :D] * vm_ref[...])
        q_ref[...] = qb
        y = _dot(qb, wout_ref[...])
        y_ref[...] = y
        x1_ref[...] = xv + p[R_G1:R_G1 + 1] * y

    call, tail = _call_after(
        after, body, 7, name="sg_fwd", grid=(L // tm,),
        in_specs=[_rows(tm, D), pl.BlockSpec((None, 8, D), lambda i: (layer, 0, 0)), _layer_w(D, 2 * D, 0),
                  _whole((SG_HEADS, SG_CHUNK, SG_CHUNK)), _whole((SG_CHUNK, SG_HEADS)), _whole((1, D)), _layer_w(D, D, 0)],
        out_specs=[_rows(tm, D), _rows(tm, D), _rows(tm, 2 * D), _rows(tm, D), _rows(tm, D), _rows(tm, D)],
        out_shape=[jax.ShapeDtypeStruct((L, D), F32), jax.ShapeDtypeStruct((L, D), BF16), jax.ShapeDtypeStruct((L, 2 * D), F32),
                   jax.ShapeDtypeStruct((L, D), F32), jax.ShapeDtypeStruct((L, D), BF16), jax.ShapeDtypeStruct((L, D), F32)],
        compiler_params=_params(1, 56),
    )
    return call(x, pv, w_in, w_s, b_t, vg, w_out, *tail)


def _sg_bwd(dx1, x, y, uv, vm, pv, w_in, w_s, vg, w_out, layer, tm, after=None):
    L = x.shape[0]
    nc = tm // SG_CHUNK

    def body(dx1_ref, x_ref, y_ref, uv_ref, vm_ref, pv_ref, win_ref, ws_ref, vg_ref, wout_ref,
             dx_ref, duv_ref, dy_ref, vs_ref, dws_ref, dbt_ref, dvn_scr):
        @pl.when(pl.program_id(0) == 0)
        def _():
            vs_ref[...] = jnp.zeros_like(vs_ref)
            dws_ref[...] = jnp.zeros_like(dws_ref)
            dbt_ref[...] = jnp.zeros_like(dbt_ref)

        dx1v, p = dx1_ref[...], pv_ref[...]
        dyb = _bf(dx1v * p[R_G1:R_G1 + 1])
        dy_ref[...] = dyb
        vs_ref[0:1, :] += _sum0(dx1v * y_ref[...])
        dq = _dot_nt(dyb, wout_ref[...])
        uv = uv_ref[...]
        u, v = uv[:, :D], uv[:, D:]
        dub = _bf(dq * vm_ref[...])
        dvm = dq * u
        dvmb = _bf(dvm)
        rv = lax.rsqrt(jnp.mean(v * v, axis=-1, keepdims=True) + EPS)
        vh = v * rv
        vgv = vg_ref[...]
        vnb = _bf(vh * vgv)
        mask = _tril_mask()
        for hd in range(SG_HEADS):
            wm = _bf(jnp.where(mask, ws_ref[hd], 0.0))
            cs = slice(hd * SG_CHUNK, (hd + 1) * SG_CHUNK)
            dws = jnp.zeros((SG_CHUNK, SG_CHUNK), F32)
            dbs = jnp.zeros((SG_CHUNK, 1), F32)
            for ck in range(nc):
                rs = slice(ck * SG_CHUNK, (ck + 1) * SG_CHUNK)
                dvn_scr[rs, cs] = _dot_tn(wm, dvmb[rs, cs])
                dws = dws + _dot_nt(dvmb[rs, cs], vnb[rs, cs])
                dbs = dbs + jnp.sum(dvm[rs, cs], axis=1, keepdims=True)
            dws_ref[hd] += jnp.where(mask, dws, 0.0)
            dbt_ref[:, hd:hd + 1] += dbs
        dvn = dvn_scr[...]
        vs_ref[3:4, :] += _sum0(dvn * vh)
        dvnn = dvn * vgv
        dvb = _bf(rv * (dvnn - vh * jnp.mean(dvnn * vh, axis=-1, keepdims=True)))
        duv_ref[:, 0:D] = dub
        duv_ref[:, D:2 * D] = dvb
        dh = _dot_nt(dub, win_ref[:, 0:D]) + _dot_nt(dvb, win_ref[:, D:2 * D])
        _, xn, r = _norm_mod(x_ref[...], p[R_N1:R_N1 + 1], p[R_SC1:R_SC1 + 1], p[R_SH1:R_SH1 + 1])
        dx_ref[...] = dx1v + _norm_mod_bwd(dh, xn, r, p[R_N1:R_N1 + 1], p[R_SC1:R_SC1 + 1])
        vs_ref[1:2, :] += _sum0(dh * xn)
        vs_ref[2:3, :] += _sum0(dh)

    call, tail = _call_after(
        after, body, 10, name="sg_bwd", grid=(L // tm,),
        in_specs=[_rows(tm, D), _rows(tm, D), _rows(tm, D), _rows(tm, 2 * D), _rows(tm, D),
                  pl.BlockSpec((None, 8, D), lambda i: (layer, 0, 0)), _layer_w(D, 2 * D, 0),
                  _whole((SG_HEADS, SG_CHUNK, SG_CHUNK)), _whole((1, D)), _layer_w(D, D, 0)],
        out_specs=[_rows(tm, D), _rows(tm, 2 * D), _rows(tm, D), _whole((8, D)),
                   _whole((SG_HEADS, SG_CHUNK, SG_CHUNK)), _whole((SG_CHUNK, SG_HEADS))],
        out_shape=[jax.ShapeDtypeStruct((L, D), F32), jax.ShapeDtypeStruct((L, 2 * D), BF16), jax.ShapeDtypeStruct((L, D), BF16),
                   jax.ShapeDtypeStruct((8, D), F32), jax.ShapeDtypeStruct((SG_HEADS, SG_CHUNK, SG_CHUNK), F32),
                   jax.ShapeDtypeStruct((SG_CHUNK, SG_HEADS), F32)],
        scratch_shapes=[pltpu.VMEM((tm, D), F32)],
        compiler_params=_params(1, 56),
    )
    return call(dx1, x, y, uv, vm, pv, w_in, w_s, vg, w_out, *tail)


def _final(x, target, fg, tm):
    L = x.shape[0]

    def body(x_ref, t_ref, g_ref, dx_ref, vs_ref):
        @pl.when(pl.program_id(0) == 0)
        def _():
            vs_ref[...] = jnp.zeros_like(vs_ref)

        xv, g = x_ref[...], g_ref[...]
        r = lax.rsqrt(jnp.mean(xv * xv, axis=-1, keepdims=True) + EPS)
        xn = xv * r
        e = xn * g - t_ref[...]
        vs_ref[0:1, :] += jnp.sum(e * e)
        dout = e * (1.0 / D)
        vs_ref[1:2, :] += _sum0(dout * xn)
        dxn = dout * g
        dx_ref[...] = r * (dxn - xn * jnp.mean(dxn * xn, axis=-1, keepdims=True))

    return pl.pallas_call(
        body, name="final_loss", grid=(L // tm,),
        in_specs=[_rows(tm, D), _rows(tm, D), _whole((1, D))],
        out_specs=[_rows(tm, D), _whole((8, D))],
        out_shape=[jax.ShapeDtypeStruct((L, D), F32), jax.ShapeDtypeStruct((8, D), F32)],
        compiler_params=_params(1),
    )(x, target, fg)


def _pack_flat(arrs, multiple=LANES):
    flat = jnp.concatenate([a.reshape(-1).astype(F32) for a in arrs])
    return jnp.pad(flat, (0, -flat.shape[0] % multiple))


def _pack(arrs, row_multiple=8):
    return _pack_flat(arrs, row_multiple * LANES).reshape(-1, LANES)


def _unpack(buf, shapes, lead=()):
    flat = buf.reshape(lead + (-1,))
    out, off = [], 0
    for s in shapes:
        n = 1
        for d in s:
            n *= d
        out.append(flat[..., off:off + n].reshape(lead + tuple(s)))
        off += n
    return out


BIG = ("ff_w1", "ff_w2", "conv_w_in", "conv_w_out", "ssm_w_in", "ssm_glu_w", "ssm_w_out", "sg_w_in", "sg_w_out")
BIG_AXIS = {"ff_w1": 2, "ff_w2": 1, "conv_w_in": 2, "conv_w_out": 1, "ssm_w_in": 1, "ssm_glu_w": 1, "ssm_w_out": 1,
            "sg_w_in": 2, "sg_w_out": 1}
LAYER_WEIGHTS = (
    (("conv_w_in", 0), ("conv_w_out", 0), ("ff_w1", 0), ("ff_w2", 0)),
    (("ssm_w_in", 0), ("ssm_glu_w", 0), ("ssm_w_out", 0), ("ff_w1", 1), ("ff_w2", 1)),
    (("sg_w_in", 0), ("sg_w_out", 0), ("ff_w1", 2), ("ff_w2", 2)),
    (("conv_w_in", 1), ("conv_w_out", 1), ("ff_w1", 3), ("ff_w2", 3)),
)
GATHER_GROUPS = tuple(grp for lw in LAYER_WEIGHTS for grp in (lw[:-2], lw[-2:]))
SMALL_SHARDED = ("conv_w", "conv_b", "sg_v_g")
SMALL_WIDE_PADDED = ("ssm_b_re", "ssm_b_im")
SMALL = ("ada_b", "norm1_g", "norm2_g", "final_g", "ssm_a_re", "ssm_a_im", "ssm_log_dt", "ssm_b_re", "ssm_b_im", "ssm_c_re",
         "ssm_c_im", "ssm_d", "ssm_glu_b", "sg_w_s", "sg_b_s") + SMALL_SHARDED
WEIGHTS = ("ada_w", "ada_b", "norm1_g", "norm2_g", "ff_w1", "ff_w2", "final_g", "conv_w_in", "conv_w", "conv_b", "conv_w_out",
           "ssm_w_in", "ssm_a_re", "ssm_a_im", "ssm_log_dt", "ssm_b_re", "ssm_b_im", "ssm_c_re", "ssm_c_im", "ssm_d",
           "ssm_glu_w", "ssm_glu_b", "ssm_w_out", "sg_w_in", "sg_v_g", "sg_w_s", "sg_b_s", "sg_w_out")


def kernel(x, c, ada_w, ada_b, norm1_g, norm2_g, ff_w1, ff_w2, final_g, conv_w_in, conv_w, conv_b, conv_w_out, ssm_w_in, ssm_a_re, ssm_a_im, ssm_log_dt, ssm_b_re, ssm_b_im, ssm_c_re, ssm_c_im, ssm_d, ssm_glu_w, ssm_glu_b, ssm_w_out, sg_w_in, sg_v_g, sg_w_s, sg_b_s, sg_w_out, loss_target, m_ada_w, m_ada_b, m_norm1_g, m_norm2_g, m_ff_w1, m_ff_w2, m_final_g, m_conv_w_in, m_conv_w, m_conv_b, m_conv_w_out, m_ssm_w_in, m_ssm_a_re, m_ssm_a_im, m_ssm_log_dt, m_ssm_b_re, m_ssm_b_im, m_ssm_c_re, m_ssm_c_im, m_ssm_d, m_ssm_glu_w, m_ssm_glu_b, m_ssm_w_out, m_sg_w_in, m_sg_v_g, m_sg_w_s, m_sg_b_s, m_sg_w_out, v_ada_w, v_ada_b, v_norm1_g, v_norm2_g, v_ff_w1, v_ff_w2, v_final_g, v_conv_w_in, v_conv_w, v_conv_b, v_conv_w_out, v_ssm_w_in, v_ssm_a_re, v_ssm_a_im, v_ssm_log_dt, v_ssm_b_re, v_ssm_b_im, v_ssm_c_re, v_ssm_c_im, v_ssm_d, v_ssm_glu_w, v_ssm_glu_b, v_ssm_w_out, v_sg_w_in, v_sg_v_g, v_sg_w_s, v_sg_b_s, v_sg_w_out):
    args = dict(locals())
    w = {n: args[n] for n in WEIGHTS}
    m = {n: args["m_" + n] for n in WEIGHTS}
    v = {n: args["v_" + n] for n in WEIGHTS}
    L = x.shape[1]
    tm = min(L, 256)
    tm2 = min(L, 512)
    chip = 2 * lax.axis_index("x") + lax.axis_index("y")
    me = 2 * chip + lax.axis_index("c")
    xin = x[0]
    target = loss_target[0]
    chip1 = chip.reshape(1).astype(jnp.int32)
    place = jnp.stack([chip, lax.axis_index("c")]).astype(jnp.int32)

    gathers = []

    casts = {}

    def cast_group(g, after=None):
        entries = GATHER_GROUPS[g]
        axes = [BIG_AXIS[n] for n, _ in entries]
        casts[g] = _cast_place([w[n] for n, _ in entries], [li for _, li in entries], axes, place, f"cast_group{g}", after)

    def start_gather(g, after):
        if g not in casts:
            cast_group(g)
        axes = [BIG_AXIS[n] for n, _ in GATHER_GROUPS[g]]
        lands = casts[g]
        s_sems, r_sems, lands, token = _gather_start(lands, axes, f"gather_start{g}", after)
        gathers.append((s_sems, r_sems, lands, axes))
        return token

    def weights_of(g, after):
        s_sems, r_sems, lands, axes = gathers[g]
        lands = _gather_wait(s_sems, r_sems, lands, axes, f"gather_wait{g}", after)
        lands = _gather_share(lands, axes, f"gather_share{g}")
        token = start_gather(g + 2, lands[0]) if g + 2 < len(GATHER_GROUPS) else None
        return dict(zip([n for n, _ in GATHER_GROUPS[g]], lands)), token

    small_in = _pack([c, conv_w, conv_b, sg_v_g])
    got = _allgather_small(small_in, "gather_small_inputs").reshape(N_DEV, -1)
    c_all, cw_sh, cb_sh, vg_sh = _unpack(got, [(D,), conv_w.shape, conv_b.shape, sg_v_g.shape], lead=(N_DEV,))
    conv_w_full = jnp.concatenate([cw_sh[2 * k] for k in range(4)], axis=-1)
    conv_b_full = jnp.concatenate([cb_sh[2 * k] for k in range(4)], axis=-1)
    vg_full = jnp.concatenate([vg_sh[2 * k] for k in range(4)], axis=-1)
    c16 = jnp.pad(c_all, ((0, 16 - N_DEV), (0, 0)))

    cols = ada_w.shape[2]
    ada_b_cols = lax.dynamic_slice_in_dim(ada_b, chip * cols, cols, axis=1)[:, None, :]
    mod_sh = _ada_fwd(c16, ada_w, ada_b_cols)[:, :N_DEV, :]
    mod_all = _allgather_small(_pack([mod_sh]), "gather_mod").reshape(N_DEV, -1)
    mod_all = _unpack(mod_all, [mod_sh.shape], lead=(N_DEV,))[0]
    mod_mine = lax.dynamic_index_in_dim(mod_all[0::2], me, axis=2, keepdims=False)
    mod_mine = mod_mine.transpose(1, 0, 2).reshape(DEPTH, 6, D)
    pv = jnp.concatenate([mod_mine, norm1_g[:, None, :], norm2_g[:, None, :]], axis=1)

    second_started = start_gather(1, start_gather(0, pv))
    for g in range(2, len(GATHER_GROUPS)):
        cast_group(g, second_started)

    cw_rows = jnp.concatenate([conv_w_full, conv_b_full[:, None, :], jnp.zeros((conv_w_full.shape[0], 4, D), F32)], axis=1)

    a_re, a_im = ssm_a_re[0], ssm_a_im[0]
    log_dt = ssm_log_dt[0][:, None]
    bt_re, bt_im = ssm_b_re[0].transpose(2, 0, 1), ssm_b_im[0].transpose(2, 0, 1)
    abar_re, abar_im, bbar_re, bbar_im = _s5_params_fwd(a_re, a_im, log_dt, bt_re, bt_im, after=second_started)
    ar_vec, ai_vec = abar_re.reshape(1, NSTATE), abar_im.reshape(1, NSTATE)
    bd_re, bd_im = _bf(_blockdiag_b(bbar_re)), _bf(_blockdiag_b(bbar_im))
    cd_re, cd_im = _bf(_blockdiag_c(ssm_c_re[0])), _bf(_blockdiag_c(ssm_c_im[0]))

    saved = []
    fulls = []
    xl = xin
    for i in range(DEPTH):
        kind = MIXER_OF_LAYER[i]
        j = i // 3
        first_after = [second_started, bd_re, bd_im, cd_re, cd_im, cw_rows] + [casts[g][0] for g in range(2, len(GATHER_GROUPS))]
        full, tok = weights_of(2 * i, first_after if i == 0 else [xl])
        fulls.append(full)
        if kind == 0:
            x1, h, bcx, conv, q, y = _conv_fwd(xl, pv, full["conv_w_in"], full["conv_w_out"], cw_rows, i, j, tm2, after=tok)
            mix = dict(h=h, bcx=bcx, conv=conv, q=q, y=y)
        elif kind == 1:
            xp = _to_segments(xl)
            h, u, e_re, e_im, u_b = _s5_fwd_ends(xp, pv, full["ssm_w_in"], bd_re, bd_im, ar_vec, ai_vec, i, tm2, after=tok)
            s0_re, s0_im = _s5_segment_states(e_re, e_im, ar_vec, ai_vec, L // NSEG, adjoint=False)
            x1p, s_re, s_im, y1, zg, y3, y, s_re_b, s_im_b = _s5_fwd_out(
                xp, u, pv, bd_re, bd_im, s0_re, s0_im, ar_vec, ai_vec, cd_re, cd_im, ssm_d, full["ssm_glu_w"], ssm_glu_b,
                full["ssm_w_out"], i, tm)
            x1 = _from_segments(x1p)
            mix = dict(xp=xp, h=h, u=u, u_b=u_b, s_re=s_re, s_im=s_im, s_re_b=s_re_b, s_im_b=s_im_b, y1=y1, zg=zg, y3=y3, y=y)
        else:
            x1, h, uv, vm, q, y = _sg_fwd(xl, pv, full["sg_w_in"], sg_w_s[0], sg_b_s[0].T, vg_full, full["sg_w_out"], i, tm2,
                                          after=tok)
            mix = dict(h=h, uv=uv, vm=vm, q=q, y=y)
        ffn_weights, tok = weights_of(2 * i + 1, [x1])
        full.update(ffn_weights)
        x2, h2, a, f = _ffn_fwd(x1, pv, full["ff_w1"], full["ff_w2"], i, tm2, after=tok)
        saved.append(dict(x=xl, x1=x1, h2=h2, a=a, f=f, **mix))
        xl = x2

    dxl, vs_fin = _final(xl, target, final_g[None, :], tm2)

    gfull = {n: [None] * w[n].shape[0] for n in BIG}
    vs_mix, vs_ffn = [None] * DEPTH, [None] * DEPTH
    small_g = {}
    scatters = {}
    token = None

    def start_scatter(key, entries, after):
        garrs = [gfull[n][li][None] for n, li in entries]
        gaxes = [BIG_AXIS[n] for n, _ in entries]
        s_sems, r_sems, garrs, lands, tok = _scatter_start(garrs, gaxes, f"scatter_start{key}", after)
        scatters[key] = (s_sems, r_sems, garrs, lands, gaxes, entries)
        return tok

    for i in reversed(range(DEPTH)):
        kind = MIXER_OF_LAYER[i]
        j = i // 3
        sv = saved[i]
        full = fulls[i]
        dx1, p_b, da_b, df_b, vs_ffn[i] = _ffn_bwd(dxl, sv["x1"], sv["a"], sv["f"], pv, full["ff_w1"], full["ff_w2"], i, tm,
                                                   after=token)
        gfull["ff_w1"][i] = _mm_tn(sv["h2"], da_b, f"wgrad_ff_w1_{i}")
        gfull["ff_w2"][i] = _mm_tn(p_b, df_b, f"wgrad_ff_w2_{i}")
        if i == 0:
            token = start_scatter("0f", LAYER_WEIGHTS[0][2:], dx1)
        if kind == 0:
            dxl, dbcx_b, dy_b, vsm = _conv_bwd(dx1, sv["x"], sv["y"], sv["bcx"], sv["conv"], pv, full["conv_w_in"],
                                               full["conv_w_out"], cw_rows, i, j, tm2, after=token if i == 0 else None)
            gfull["conv_w_in"][j] = _mm_tn(sv["h"], dbcx_b, f"wgrad_conv_w_in_{j}")
            gfull["conv_w_out"][j] = _mm_tn(sv["q"], dy_b, f"wgrad_conv_w_out_{j}")
            small_g.setdefault("conv_w", [None, None])[j] = vsm[3:6]
            small_g.setdefault("conv_b", [None, None])[j] = vsm[6]
        elif kind == 1:
            dx1p = _to_segments(dx1)
            dy_b, y2_b, dzg_b, dy1_b, du_skip, eb_re, eb_im, vsm = _s5_bwd_ends(
                dx1p, sv["y"], sv["y1"], sv["zg"], sv["u"], pv, cd_re, cd_im, ar_vec, ai_vec, ssm_d, full["ssm_glu_w"],
                full["ssm_w_out"], i, tm)
            l0_re, l0_im = _s5_segment_states(eb_re, eb_im, ar_vec, ai_vec, L // NSEG, adjoint=True)
            dxp, du_b, lam_re, lam_im, dabar, vs_in = _s5_bwd_in(
                dx1p, dy1_b, du_skip, sv["xp"], sv["s_re"], sv["s_im"], pv, bd_re, bd_im, cd_re, cd_im, l0_re, l0_im,
                ar_vec, ai_vec, full["ssm_w_in"], i, tm)
            dxl = _from_segments(dxp)
            gfull["ssm_w_out"][0] = _mm_tn(sv["y3"], dy_b, "wgrad_ssm_w_out")
            gfull["ssm_glu_w"][0] = _mm_tn(y2_b, dzg_b, "wgrad_ssm_glu_w")
            gfull["ssm_w_in"][0] = _mm_tn(sv["h"], du_b, "wgrad_ssm_w_in")
            s5_late = dict(s_re=sv["s_re_b"], s_im=sv["s_im_b"], u=sv["u_b"], dy1_b=dy1_b, lam_re=lam_re, lam_im=lam_im, dabar=dabar)
            small_g.update(ssm_d=vsm[2], ssm_glu_b=vsm[1])
            vsm = jnp.concatenate([vsm[0:1], vs_in[1:3], jnp.zeros((5, D), F32)], axis=0)
        else:
            dxl, duv_b, dy_b, vsm, d_ws, d_bt = _sg_bwd(dx1, sv["x"], sv["y"], sv["uv"], sv["vm"], pv, full["sg_w_in"],
                                                        sg_w_s[0], vg_full, full["sg_w_out"], i, tm2)
            gfull["sg_w_in"][0] = _mm_tn(sv["h"], duv_b, "wgrad_sg_w_in")
            gfull["sg_w_out"][0] = _mm_tn(sv["q"], dy_b, "wgrad_sg_w_out")
            small_g.update(sg_v_g=vsm[3], sg_w_s=d_ws, sg_b_s=d_bt.T)
        vs_mix[i] = vsm
        token = start_scatter(str(i), LAYER_WEIGHTS[i], dxl) if i > 0 else start_scatter("0c", LAYER_WEIGHTS[0][:2], dxl)
    grad_x = dxl[None]

    sums = {n: [None] * w[n].shape[0] for n in BIG}

    def collect(key, after):
        s_sems, r_sems, garrs, lands, gaxes, entries = scatters[key]
        garrs, recv = _scatter_wait(s_sems, r_sems, garrs, lands, gaxes, f"scatter_wait{key}", after)
        for (n, li), t in zip(entries, _sum_parts(recv, garrs, gaxes, chip1, f"sum_group{key}")):
            sums[n][li] = t
        return sums[entries[-1][0]][entries[-1][1]]

    after = token
    for key in ("3", "2", "1"):
        after = collect(key, after)
    early = [(n, li) for i in (3, 2, 1) for n, li in LAYER_WEIGHTS[i]]
    late = list(LAYER_WEIGHTS[0][2:]) + list(LAYER_WEIGHTS[0][:2])
    s_sems, r_sems, mine_thru, lands, tok = _swap_start([sums[n][li] for n, li in early], "swap_start_early", after)

    blocks = dict(
        c_re=_mm_tn_blocks(s5_late["s_re"], s5_late["dy1_b"], S5_BP, S5_BH, "wgrad_s5_c_re", after=tok),
        c_im=_mm_tn_blocks(s5_late["s_im"], s5_late["dy1_b"], S5_BP, S5_BH, "wgrad_s5_c_im", after=tok),
        b_re=_mm_tn_blocks(s5_late["u"], s5_late["lam_re"], S5_BH, S5_BP, "wgrad_s5_b_re", after=tok),
        b_im=_mm_tn_blocks(s5_late["u"], s5_late["lam_im"], S5_BH, S5_BP, "wgrad_s5_b_im", after=tok))
    d_are, d_aim, d_ldt, d_btre, d_btim = _s5_params_bwd(
        a_re, a_im, log_dt, bt_re, bt_im, s5_late["dabar"][0].reshape(S5_G, S5_P), s5_late["dabar"][1].reshape(S5_G, S5_P),
        _unblock_b(blocks["b_re"]), _unblock_b(blocks["b_im"]))
    small_g.update(ssm_a_re=d_are, ssm_a_im=d_aim, ssm_log_dt=d_ldt, ssm_b_re=d_btre.transpose(1, 2, 0),
                   ssm_b_im=d_btim.transpose(1, 2, 0), ssm_c_re=_unblock_c(blocks["c_re"]), ssm_c_im=-_unblock_c(blocks["c_im"]))

    mine_thru, got = _swap_wait(s_sems, r_sems, mine_thru, lands, "swap_wait_early", blocks["b_im"])
    sib = dict(zip(early, got))
    for (n, li), t in zip(early, mine_thru):
        sums[n][li] = t
    after = got[-1]
    for key in ("0f", "0c"):
        after = collect(key, after)
    sib.update(zip(late, _swap_with_sibling([sums[n][li] for n, li in late], "swap_grad_sums_late")))

    dmod = _mod_bwd(jnp.stack(vs_mix), jnp.stack(vs_ffn), pv)
    small_g.update(ada_b=dmod[:, :6, :], norm1_g=dmod[:, 6, :], norm2_g=dmod[:, 7, :], final_g=vs_fin[1],
                   conv_w=jnp.stack(small_g["conv_w"]), conv_b=jnp.stack(small_g["conv_b"]))

    loss_part = (0.5 / D) * vs_fin[0, 0:1]
    part_shapes = [(1,)] + [tuple(small_g[n].shape) for n in SMALL]
    dmod_all = _allgather_small(_pack([small_g["ada_b"]]), "gather_dmod", sib[late[-1]])
    dmod_all = dmod_all.reshape(N_DEV, DEPTH, 6 * D)
    dmod_cols = lax.dynamic_slice_in_dim(dmod_all, chip * cols, cols, axis=2).transpose(1, 0, 2)
    g_ada_w = _ada_bwd(c16, jnp.pad(dmod_cols, ((0, 0), (0, 16 - N_DEV), (0, 0))))
    slots = _reduce_pair(_pack([loss_part] + [small_g[n] for n in SMALL], 16), "reduce_small_pair", g_ada_w)
    s_sems, r_sems, slots, tok = _reduce_cross_start(slots, "reduce_small_cross_start")

    res = {}
    shp = ada_w.shape
    two = lambda t: t.reshape(shp[0] * shp[1], shp[2])
    res["ada_w"] = [t.reshape(shp) for t in _adamw(two(ada_w), [two(g_ada_w)], two(m_ada_w), two(v_ada_w), "adamw_ada_w")]
    for n in BIG:
        res[n] = _adamw_layers(w[n], sums[n], [sib[(n, li)] for li in range(w[n].shape[0])], m[n], v[n], f"adamw_{n}", after=tok)
        tok = res[n][0]

    slots = _reduce_cross_wait(s_sems, r_sems, slots, "reduce_small_cross_wait", tok)
    parts_sum = _reduce_finish(slots, "reduce_small_finish")
    summed = _unpack(parts_sum, part_shapes)
    loss = summed[0][0]
    gsum = dict(zip(SMALL, summed[1:]))


    def mine(n):
        g = gsum[n]
        if n in SMALL_SHARDED:
            g = lax.dynamic_slice_in_dim(g, chip * w[n].shape[-1], w[n].shape[-1], axis=g.ndim - 1)
        return g.reshape(w[n].shape)

    for k, names in enumerate(([n for n in SMALL if n not in SMALL_WIDE_PADDED], list(SMALL_WIDE_PADDED))):
        outs = _adamw_many([w[n] for n in names], [mine(n) for n in names], [m[n] for n in names], [v[n] for n in names],
                           f"adamw_small{k}")
        for idx, n in enumerate(names):
            res[n] = [outs[part][idx] for part in range(4)]

    outs = [loss, grad_x]
    for part in range(4):
        outs += [res[n][part] for n in WEIGHTS]
    return tuple(outs)
```
